```python
import jax, jax.numpy as jnp
from jax import lax
import numpy as np


D_MODEL = 1024
BATCH = 16
SEQ = 2048
DEPTH = 2

A_HEADS = 4
A_KEY_DIM = 128
A_VAL_DIM = 128
A_KEY_WIDTH = A_HEADS * A_KEY_DIM
A_WIDTH = A_HEADS * A_VAL_DIM
HGRN_CHUNK = 64
B_WIDTH = D_MODEL - A_WIDTH
B_BLOCKS = 8
B_BLOCK_DIM = B_WIDTH // B_BLOCKS
B_CONV = 4
RG_C = 8.0
C_HEADS = 16
C_HEAD_DIM = D_MODEL // C_HEADS
Q_BLOCK = 128
D_FF = 2816
FFN_CONV = 3
EPS = 1e-6

N_EVEN = (DEPTH + 1) // 2
N_ODD = DEPTH // 2
EVEN_SIZES = (A_KEY_WIDTH, A_KEY_WIDTH, A_WIDTH, A_WIDTH, B_WIDTH, B_WIDTH)
EVEN_IN = sum(EVEN_SIZES)
ODD_SIZES = (D_MODEL, D_MODEL, D_MODEL, C_HEADS)
ODD_IN = sum(ODD_SIZES)

kernel_name = 'hybrid_hgrn2_rglru_fox_convffn'


def rms_norm(x, gain):
    xf = x.astype(jnp.float32)
    y = xf * lax.rsqrt(jnp.mean(xf * xf, axis=-1, keepdims=True) + EPS)
    return (y * gain.astype(jnp.float32)).astype(x.dtype)


def split_cols(t, sizes):
    return jnp.split(t, list(np.cumsum(sizes)[:-1]), axis=-1)


def causal_dwconv(x, w, b):
    K, C = w.shape
    y = lax.conv_general_dilated(x, w[:, None, :].astype(x.dtype), window_strides=(1,),
                                 padding=[(K - 1, 0)], dimension_numbers=('NWC', 'WIO', 'NWC'),
                                 feature_group_count=C)
    return y + b.astype(x.dtype)


def hgrn2_mix(q, f_logit, v, g, lb, norm_gain):
    Bsz, T, _ = q.shape
    N = T // HGRN_CHUNK

    def heads(t, d):
        return t.reshape(Bsz, N, HGRN_CHUNK, A_HEADS, d).transpose(0, 3, 1, 2, 4)

    forget = lb + (1.0 - lb) * jax.nn.sigmoid(f_logit.astype(jnp.float32))
    qh = heads(jax.nn.silu(q.astype(jnp.float32)), A_KEY_DIM)
    kh = heads(1.0 - forget, A_KEY_DIM)
    bcum = jnp.cumsum(heads(jnp.log(forget), A_KEY_DIM), axis=3)
    vh = heads(v.astype(jnp.float32), A_VAL_DIM)
    b_last = bcum[:, :, :, -1:]
    q_dec = qh * jnp.exp(bcum)
    scores = jnp.einsum('bhnck,bhnsk->bhncs', q_dec, kh * jnp.exp(-bcum))
    causal = jnp.tril(jnp.ones((HGRN_CHUNK, HGRN_CHUNK), dtype=bool))
    scores = jnp.where(causal, scores, 0.0)
    o_intra = jnp.einsum('bhncs,bhnsv->bhncv', scores, vh)
    chunk_upd = jnp.einsum('bhnck,bhncv->bhnkv', kh * jnp.exp(b_last - bcum), vh)
    chunk_dec = jnp.exp(b_last[:, :, :, 0])

    def step(S, inp):
        dec, upd = inp
        return dec[..., None] * S + upd, S

    S0 = jnp.zeros((Bsz, A_HEADS, A_KEY_DIM, A_VAL_DIM), jnp.float32)
    _, S_prev = lax.scan(step, S0, (jnp.moveaxis(chunk_dec, 2, 0), jnp.moveaxis(chunk_upd, 2, 0)))
    S_prev = jnp.moveaxis(S_prev, 0, 2)
    o = o_intra + jnp.einsum('bhnck,bhnkv->bhncv', q_dec, S_prev)
    o = o.transpose(0, 2, 3, 1, 4).reshape(Bsz, T, A_HEADS, A_VAL_DIM)
    o = o * lax.rsqrt(jnp.mean(o * o, axis=-1, keepdims=True) + EPS)
    o = o * norm_gain.astype(jnp.float32).reshape(A_HEADS, A_VAL_DIM)
    return o.reshape(Bsz, T, A_WIDTH) * jax.nn.sigmoid(g.astype(jnp.float32))


def block_diag(x, w, b):
    xb = x.reshape(x.shape[0], x.shape[1], B_BLOCKS, B_BLOCK_DIM)
    return jnp.einsum('btni,nij->btnj', xb, w.astype(jnp.float32)).reshape(x.shape) + b.astype(jnp.float32)


def rglru_mix(x_br, y_br, conv_w, conv_b, wa, ba, wx, bx, lam):
    xf = causal_dwconv(x_br, conv_w, conv_b).astype(jnp.float32)
    r = jax.nn.sigmoid(block_diag(xf, wa, ba))
    i = jax.nn.sigmoid(block_diag(xf, wx, bx))
    log_a = -RG_C * r * jax.nn.softplus(-lam.astype(jnp.float32))
    a = jnp.exp(log_a)
    u = jnp.sqrt(-jnp.expm1(2.0 * log_a)) * (i * xf)

    def combine(left, right):
        a1, b1 = left
        a2, b2 = right
        return a1 * a2, a2 * b1 + b2

    _, h = lax.associative_scan(combine, (a, u), axis=1)
    return h * jax.nn.gelu(y_br.astype(jnp.float32))


def fox_attention(q, k, v, f_logit):
    Bsz, T, _ = q.shape

    def heads(t):
        return t.astype(jnp.float32).reshape(Bsz, T, C_HEADS, C_HEAD_DIM).transpose(0, 2, 1, 3)

    qh = heads(q) * (C_HEAD_DIM ** -0.5)
    kh, vh = heads(k), heads(v)
    c = jnp.cumsum(jax.nn.log_sigmoid(f_logit.astype(jnp.float32)), axis=1).transpose(0, 2, 1)
    outs = []
    for blk in range(T // Q_BLOCK):
        s0 = blk * Q_BLOCK
        end = s0 + Q_BLOCK
        logits = jnp.einsum('bhqd,bhkd->bhqk', qh[:, :, s0:end], kh[:, :, :end])
        logits = logits + c[:, :, s0:end, None] - c[:, :, None, :end]
        mask = (s0 + jnp.arange(Q_BLOCK))[:, None] >= jnp.arange(end)[None, :]
        p = jax.nn.softmax(jnp.where(mask, logits, -jnp.inf), axis=-1)
        outs.append(jnp.einsum('bhqk,bhkd->bhqd', p, vh[:, :, :end]))
    o = jnp.concatenate(outs, axis=2)
    return o.transpose(0, 2, 1, 3).reshape(Bsz, T, D_MODEL)


def conv_ffn(x, w_up, conv_w, conv_b, w_down):
    hid = causal_dwconv(x @ w_up, conv_w, conv_b)
    gate, val = jnp.split(hid, 2, axis=-1)
    return (jax.nn.gelu(gate) * val) @ w_down


def _fwd_setup_inputs(seed: int = 0) -> dict:
    key = jax.random.key(seed)
    ks = jax.random.split(key, 20)
    f32 = jnp.float32
    nrm = jax.random.normal
    a0 = jax.random.uniform(ks[11], (N_EVEN, B_WIDTH), f32, 0.9 ** (1.0 / RG_C), 0.999 ** (1.0 / RG_C))
    return {
        'x': nrm(ks[0], (BATCH, SEQ, D_MODEL), f32),
        'norm_gains': 1.0 + 0.1 * nrm(ks[1], (DEPTH, 4, D_MODEL), f32),
        'even_w_in': nrm(ks[2], (N_EVEN, D_MODEL, EVEN_IN), f32) * D_MODEL ** -0.5,
        'hgrn_lb_logits': 0.1 * nrm(ks[3], (DEPTH + 1, A_KEY_WIDTH), f32),
        'hgrn_norm': 1.0 + 0.1 * nrm(ks[4], (N_EVEN, A_WIDTH), f32),
        'rg_conv_w': nrm(ks[5], (N_EVEN, B_CONV, B_WIDTH), f32) * B_CONV ** -0.5,
        'rg_conv_b': 0.01 * nrm(ks[6], (N_EVEN, B_WIDTH), f32),
        'rg_wa': nrm(ks[7], (N_EVEN, B_BLOCKS, B_BLOCK_DIM, B_BLOCK_DIM), f32) * B_BLOCK_DIM ** -0.5,
        'rg_ba': 0.01 * nrm(ks[8], (N_EVEN, B_WIDTH), f32),
        'rg_wx': nrm(ks[9], (N_EVEN, B_BLOCKS, B_BLOCK_DIM, B_BLOCK_DIM), f32) * B_BLOCK_DIM ** -0.5,
        'rg_bx': 0.01 * nrm(ks[10], (N_EVEN, B_WIDTH), f32),
        'rg_lambda': jnp.log(a0) - jnp.log1p(-a0),
        'even_w_out': nrm(ks[12], (N_EVEN, A_WIDTH + B_WIDTH, D_MODEL), f32) * (A_WIDTH + B_WIDTH) ** -0.5,
        'odd_w_in': nrm(ks[13], (N_ODD, D_MODEL, ODD_IN), f32) * D_MODEL ** -0.5,
        'fox_f_bias': jax.random.uniform(ks[14], (N_ODD, C_HEADS), f32, 1.0, 5.0),
        'odd_w_out': nrm(ks[15], (N_ODD, D_MODEL, D_MODEL), f32) * D_MODEL ** -0.5,
        'ffn_w_up': nrm(ks[16], (DEPTH, D_MODEL, 2 * D_FF), f32) * D_MODEL ** -0.5,
        'ffn_conv_w': nrm(ks[17], (DEPTH, FFN_CONV, 2 * D_FF), f32) * FFN_CONV ** -0.5,
        'ffn_conv_b': 0.01 * nrm(ks[18], (DEPTH, 2 * D_FF), f32),
        'ffn_w_down': nrm(ks[19], (DEPTH, D_FF, D_MODEL), f32) * D_FF ** -0.5,
    }


def _fwd_reference(x, norm_gains, even_w_in, hgrn_lb_logits, hgrn_norm, rg_conv_w, rg_conv_b, rg_wa, rg_ba,
              rg_wx, rg_bx, rg_lambda, even_w_out, odd_w_in, fox_f_bias, odd_w_out,
              ffn_w_up, ffn_conv_w, ffn_conv_b, ffn_w_down):
    lb_all = jnp.cumsum(jax.nn.softmax(hgrn_lb_logits.astype(jnp.float32), axis=0), axis=0)
    for l in range(DEPTH):
        g = norm_gains[l]
        h = rms_norm(x, g[0])
        if l % 2 == 0:
            e = l // 2
            qa, fa, ia, ga, xb, yb = split_cols(h @ even_w_in[e], EVEN_SIZES)
            oa = hgrn2_mix(qa, fa, ia, ga, lb_all[l], hgrn_norm[e])
            ob = rglru_mix(xb, yb, rg_conv_w[e], rg_conv_b[e], rg_wa[e], rg_ba[e], rg_wx[e], rg_bx[e], rg_lambda[e])
            mix = jnp.concatenate([oa, ob], axis=-1).astype(h.dtype) @ even_w_out[e]
        else:
            o = l // 2
            qc, kc, vc, fc = split_cols(h @ odd_w_in[o], ODD_SIZES)
            mix = fox_attention(qc, kc, vc, fc + fox_f_bias[o]).astype(h.dtype) @ odd_w_out[o]
        x = x + rms_norm(mix, g[1])
        h = rms_norm(x, g[2])
        x = x + rms_norm(conv_ffn(h, ffn_w_up[l], ffn_conv_w[l], ffn_conv_b[l], ffn_w_down[l]), g[3])
    return x


import jax as _jax
import jax.numpy as _jnp

TWIN_FORMAT = 'train_step'
FWD_PARAMS = ['x', 'norm_gains', 'even_w_in', 'hgrn_lb_logits', 'hgrn_norm', 'rg_conv_w', 'rg_conv_b', 'rg_wa', 'rg_ba', 'rg_wx', 'rg_bx', 'rg_lambda', 'even_w_out', 'odd_w_in', 'fox_f_bias', 'odd_w_out', 'ffn_w_up', 'ffn_conv_w', 'ffn_conv_b', 'ffn_w_down']
TWIN_WEIGHTS = ['norm_gains', 'even_w_in', 'hgrn_lb_logits', 'hgrn_norm', 'rg_conv_w', 'rg_conv_b', 'rg_wa', 'rg_ba', 'rg_wx', 'rg_bx', 'rg_lambda', 'even_w_out', 'odd_w_in', 'fox_f_bias', 'odd_w_out', 'ffn_w_up', 'ffn_conv_w', 'ffn_conv_b', 'ffn_w_down']
TWIN_DIFF_INPUT = 'x'
TWIN_INPUTS = ['x', 'norm_gains', 'even_w_in', 'hgrn_lb_logits', 'hgrn_norm', 'rg_conv_w', 'rg_conv_b', 'rg_wa', 'rg_ba', 'rg_wx', 'rg_bx', 'rg_lambda', 'even_w_out', 'odd_w_in', 'fox_f_bias', 'odd_w_out', 'ffn_w_up', 'ffn_conv_w', 'ffn_conv_b', 'ffn_w_down', 'loss_target', 'm_norm_gains', 'm_even_w_in', 'm_hgrn_lb_logits', 'm_hgrn_norm', 'm_rg_conv_w', 'm_rg_conv_b', 'm_rg_wa', 'm_rg_ba', 'm_rg_wx', 'm_rg_bx', 'm_rg_lambda', 'm_even_w_out', 'm_odd_w_in', 'm_fox_f_bias', 'm_odd_w_out', 'm_ffn_w_up', 'm_ffn_conv_w', 'm_ffn_conv_b', 'm_ffn_w_down', 'v_norm_gains', 'v_even_w_in', 'v_hgrn_lb_logits', 'v_hgrn_norm', 'v_rg_conv_w', 'v_rg_conv_b', 'v_rg_wa', 'v_rg_ba', 'v_rg_wx', 'v_rg_bx', 'v_rg_lambda', 'v_even_w_out', 'v_odd_w_in', 'v_fox_f_bias', 'v_odd_w_out', 'v_ffn_w_up', 'v_ffn_conv_w', 'v_ffn_conv_b', 'v_ffn_w_down']
TWIN_OUTPUTS = ['loss', 'grad_x', 'grad_norm_gains', 'grad_even_w_in', 'grad_hgrn_lb_logits', 'grad_hgrn_norm', 'grad_rg_conv_w', 'grad_rg_conv_b', 'grad_rg_wa', 'grad_rg_ba', 'grad_rg_wx', 'grad_rg_bx', 'grad_rg_lambda', 'grad_even_w_out', 'grad_odd_w_in', 'grad_fox_f_bias', 'grad_odd_w_out', 'grad_ffn_w_up', 'grad_ffn_conv_w', 'grad_ffn_conv_b', 'grad_ffn_w_down', 'delta_norm_gains', 'delta_even_w_in', 'delta_hgrn_lb_logits', 'delta_hgrn_norm', 'delta_rg_conv_w', 'delta_rg_conv_b', 'delta_rg_wa', 'delta_rg_ba', 'delta_rg_wx', 'delta_rg_bx', 'delta_rg_lambda', 'delta_even_w_out', 'delta_odd_w_in', 'delta_fox_f_bias', 'delta_odd_w_out', 'delta_ffn_w_up', 'delta_ffn_conv_w', 'delta_ffn_conv_b', 'delta_ffn_w_down', 'new_m_norm_gains', 'new_m_even_w_in', 'new_m_hgrn_lb_logits', 'new_m_hgrn_norm', 'new_m_rg_conv_w', 'new_m_rg_conv_b', 'new_m_rg_wa', 'new_m_rg_ba', 'new_m_rg_wx', 'new_m_rg_bx', 'new_m_rg_lambda', 'new_m_even_w_out', 'new_m_odd_w_in', 'new_m_fox_f_bias', 'new_m_odd_w_out', 'new_m_ffn_w_up', 'new_m_ffn_conv_w', 'new_m_ffn_conv_b', 'new_m_ffn_w_down', 'new_v_norm_gains', 'new_v_even_w_in', 'new_v_hgrn_lb_logits', 'new_v_hgrn_norm', 'new_v_rg_conv_w', 'new_v_rg_conv_b', 'new_v_rg_wa', 'new_v_rg_ba', 'new_v_rg_wx', 'new_v_rg_bx', 'new_v_rg_lambda', 'new_v_even_w_out', 'new_v_odd_w_in', 'new_v_fox_f_bias', 'new_v_odd_w_out', 'new_v_ffn_w_up', 'new_v_ffn_conv_w', 'new_v_ffn_conv_b', 'new_v_ffn_w_down']
TWIN_LEAF_KINDS = {'loss': 'loss', 'grad_x': 'grad_x', 'grad_norm_gains': 'grad_w', 'grad_even_w_in': 'grad_w', 'grad_hgrn_lb_logits': 'grad_w', 'grad_hgrn_norm': 'grad_w', 'grad_rg_conv_w': 'grad_w', 'grad_rg_conv_b': 'grad_w', 'grad_rg_wa': 'grad_w', 'grad_rg_ba': 'grad_w', 'grad_rg_wx': 'grad_w', 'grad_rg_bx': 'grad_w', 'grad_rg_lambda': 'grad_w', 'grad_even_w_out': 'grad_w', 'grad_odd_w_in': 'grad_w', 'grad_fox_f_bias': 'grad_w', 'grad_odd_w_out': 'grad_w', 'grad_ffn_w_up': 'grad_w', 'grad_ffn_conv_w': 'grad_w', 'grad_ffn_conv_b': 'grad_w', 'grad_ffn_w_down': 'grad_w', 'delta_norm_gains': 'delta_w', 'delta_even_w_in': 'delta_w', 'delta_hgrn_lb_logits': 'delta_w', 'delta_hgrn_norm': 'delta_w', 'delta_rg_conv_w': 'delta_w', 'delta_rg_conv_b': 'delta_w', 'delta_rg_wa': 'delta_w', 'delta_rg_ba': 'delta_w', 'delta_rg_wx': 'delta_w', 'delta_rg_bx': 'delta_w', 'delta_rg_lambda': 'delta_w', 'delta_even_w_out': 'delta_w', 'delta_odd_w_in': 'delta_w', 'delta_fox_f_bias': 'delta_w', 'delta_odd_w_out': 'delta_w', 'delta_ffn_w_up': 'delta_w', 'delta_ffn_conv_w': 'delta_w', 'delta_ffn_conv_b': 'delta_w', 'delta_ffn_w_down': 'delta_w', 'new_m_norm_gains': 'new_m', 'new_m_even_w_in': 'new_m', 'new_m_hgrn_lb_logits': 'new_m', 'new_m_hgrn_norm': 'new_m', 'new_m_rg_conv_w': 'new_m', 'new_m_rg_conv_b': 'new_m', 'new_m_rg_wa': 'new_m', 'new_m_rg_ba': 'new_m', 'new_m_rg_wx': 'new_m', 'new_m_rg_bx': 'new_m', 'new_m_rg_lambda': 'new_m', 'new_m_even_w_out': 'new_m', 'new_m_odd_w_in': 'new_m', 'new_m_fox_f_bias': 'new_m', 'new_m_odd_w_out': 'new_m', 'new_m_ffn_w_up': 'new_m', 'new_m_ffn_conv_w': 'new_m', 'new_m_ffn_conv_b': 'new_m', 'new_m_ffn_w_down': 'new_m', 'new_v_norm_gains': 'new_v', 'new_v_even_w_in': 'new_v', 'new_v_hgrn_lb_logits': 'new_v', 'new_v_hgrn_norm': 'new_v', 'new_v_rg_conv_w': 'new_v', 'new_v_rg_conv_b': 'new_v', 'new_v_rg_wa': 'new_v', 'new_v_rg_ba': 'new_v', 'new_v_rg_wx': 'new_v', 'new_v_rg_bx': 'new_v', 'new_v_rg_lambda': 'new_v', 'new_v_even_w_out': 'new_v', 'new_v_odd_w_in': 'new_v', 'new_v_fox_f_bias': 'new_v', 'new_v_odd_w_out': 'new_v', 'new_v_ffn_w_up': 'new_v', 'new_v_ffn_conv_w': 'new_v', 'new_v_ffn_conv_b': 'new_v', 'new_v_ffn_w_down': 'new_v'}


def _forward(args):
    return _fwd_reference(*[args[k] for k in FWD_PARAMS])


def _output_shape():
    out = _jax.eval_shape(lambda: _forward(_fwd_setup_inputs(0)))
    return out.shape, out.dtype

N_MICROBATCH = 1
ADAM_LR = 0.001
ADAM_B1 = 0.9
ADAM_B2 = 0.999
ADAM_EPS = 1e-08
ADAM_WD = 0.01
ADAM_STEP = 10
PER_EXAMPLE_BATCH_AXIS = {'x': 0, 'loss_target': 0}
SHARED_INPUTS = []
_WEIGHT_DTYPES = {'norm_gains': _jnp.float32, 'even_w_in': _jnp.float32, 'hgrn_lb_logits': _jnp.float32, 'hgrn_norm': _jnp.float32, 'rg_conv_w': _jnp.float32, 'rg_conv_b': _jnp.float32, 'rg_wa': _jnp.float32, 'rg_ba': _jnp.float32, 'rg_wx': _jnp.float32, 'rg_bx': _jnp.float32, 'rg_lambda': _jnp.float32, 'even_w_out': _jnp.float32, 'odd_w_in': _jnp.float32, 'fox_f_bias': _jnp.float32, 'odd_w_out': _jnp.float32, 'ffn_w_up': _jnp.float32, 'ffn_conv_w': _jnp.float32, 'ffn_conv_b': _jnp.float32, 'ffn_w_down': _jnp.float32}
MOMENT_SCALE = {'norm_gains': 2.263098e+01, 'even_w_in': 7.701717e-01, 'hgrn_lb_logits': 4.883803e-02, 'hgrn_norm': 1.313843e+00, 'rg_conv_w': 2.223570e+00, 'rg_conv_b': 3.837656e+01, 'rg_wa': 1.632114e+00, 'rg_ba': 8.141206e-01, 'rg_wx': 2.936345e+00, 'rg_bx': 5.354582e-01, 'rg_lambda': 1.114008e+00, 'even_w_out': 1.972509e+00, 'odd_w_in': 9.174092e-01, 'fox_f_bias': 3.498563e+00, 'odd_w_out': 1.657352e+00, 'ffn_w_up': 4.351828e-01, 'ffn_conv_w': 4.831897e-01, 'ffn_conv_b': 2.410478e+00, 'ffn_w_down': 8.812068e-01}


def _to_microbatches(a, axis):
    t = _jnp.moveaxis(a, axis, 0)
    t = t.reshape((N_MICROBATCH, t.shape[0] // N_MICROBATCH) + t.shape[1:])
    return _jnp.moveaxis(t, 1, axis + 1)


def setup_inputs(seed: int = 0) -> dict:
    inp = _fwd_setup_inputs(seed)
    key = _jax.random.fold_in(_jax.random.key(seed), 7919)
    shape, _ = _output_shape()
    out = dict(inp)
    out["loss_target"] = _jax.random.normal(_jax.random.fold_in(key, 0), shape, _jnp.float32)
    for i, name in enumerate(TWIN_WEIGHTS):
        w = inp[name].astype(_jnp.float32)
        if MOMENT_SCALE is None:
            s = _jnp.sqrt(_jnp.mean(_jnp.square(w)) + 1e-30)
        else:
            s = MOMENT_SCALE[name]
        km, kv = _jax.random.split(_jax.random.fold_in(key, i + 1))
        out[name] = w
        out["m_" + name] = s * _jax.random.normal(km, w.shape, _jnp.float32)
        out["v_" + name] = (s * s) * _jax.random.uniform(kv, w.shape, _jnp.float32, 0.5, 1.5)
    if N_MICROBATCH > 1:
        for name, axis in PER_EXAMPLE_BATCH_AXIS.items():
            out[name] = _to_microbatches(out[name], axis)
    return {'x': out['x'], 'norm_gains': out['norm_gains'], 'even_w_in': out['even_w_in'], 'hgrn_lb_logits': out['hgrn_lb_logits'], 'hgrn_norm': out['hgrn_norm'], 'rg_conv_w': out['rg_conv_w'], 'rg_conv_b': out['rg_conv_b'], 'rg_wa': out['rg_wa'], 'rg_ba': out['rg_ba'], 'rg_wx': out['rg_wx'], 'rg_bx': out['rg_bx'], 'rg_lambda': out['rg_lambda'], 'even_w_out': out['even_w_out'], 'odd_w_in': out['odd_w_in'], 'fox_f_bias': out['fox_f_bias'], 'odd_w_out': out['odd_w_out'], 'ffn_w_up': out['ffn_w_up'], 'ffn_conv_w': out['ffn_conv_w'], 'ffn_conv_b': out['ffn_conv_b'], 'ffn_w_down': out['ffn_w_down'], 'loss_target': out['loss_target'], 'm_norm_gains': out['m_norm_gains'], 'm_even_w_in': out['m_even_w_in'], 'm_hgrn_lb_logits': out['m_hgrn_lb_logits'], 'm_hgrn_norm': out['m_hgrn_norm'], 'm_rg_conv_w': out['m_rg_conv_w'], 'm_rg_conv_b': out['m_rg_conv_b'], 'm_rg_wa': out['m_rg_wa'], 'm_rg_ba': out['m_rg_ba'], 'm_rg_wx': out['m_rg_wx'], 'm_rg_bx': out['m_rg_bx'], 'm_rg_lambda': out['m_rg_lambda'], 'm_even_w_out': out['m_even_w_out'], 'm_odd_w_in': out['m_odd_w_in'], 'm_fox_f_bias': out['m_fox_f_bias'], 'm_odd_w_out': out['m_odd_w_out'], 'm_ffn_w_up': out['m_ffn_w_up'], 'm_ffn_conv_w': out['m_ffn_conv_w'], 'm_ffn_conv_b': out['m_ffn_conv_b'], 'm_ffn_w_down': out['m_ffn_w_down'], 'v_norm_gains': out['v_norm_gains'], 'v_even_w_in': out['v_even_w_in'], 'v_hgrn_lb_logits': out['v_hgrn_lb_logits'], 'v_hgrn_norm': out['v_hgrn_norm'], 'v_rg_conv_w': out['v_rg_conv_w'], 'v_rg_conv_b': out['v_rg_conv_b'], 'v_rg_wa': out['v_rg_wa'], 'v_rg_ba': out['v_rg_ba'], 'v_rg_wx': out['v_rg_wx'], 'v_rg_bx': out['v_rg_bx'], 'v_rg_lambda': out['v_rg_lambda'], 'v_even_w_out': out['v_even_w_out'], 'v_odd_w_in': out['v_odd_w_in'], 'v_fox_f_bias': out['v_fox_f_bias'], 'v_odd_w_out': out['v_odd_w_out'], 'v_ffn_w_up': out['v_ffn_w_up'], 'v_ffn_conv_w': out['v_ffn_conv_w'], 'v_ffn_conv_b': out['v_ffn_conv_b'], 'v_ffn_w_down': out['v_ffn_w_down']}


def _loss(weights, diff, rest, loss_target):
    with _jax.named_scope("forward"):
        args = {**rest, TWIN_DIFF_INPUT: diff, **{k: w.astype(_WEIGHT_DTYPES[k]) for k, w in weights.items()}}
        y = _forward(args)
    with _jax.named_scope("loss_head"):
        err = _jnp.square(y.astype(_jnp.float32) - loss_target)
        return 0.5 * _jnp.sum(_jnp.mean(err, axis=-1)) if err.ndim else 0.5 * err


def _adamw(w, g, m, v):
    m = ADAM_B1 * m + (1.0 - ADAM_B1) * g
    v = ADAM_B2 * v + (1.0 - ADAM_B2) * _jnp.square(g)
    m_hat = m / (1.0 - ADAM_B1 ** ADAM_STEP)
    v_hat = v / (1.0 - ADAM_B2 ** ADAM_STEP)
    delta = -ADAM_LR * (m_hat / (_jnp.sqrt(v_hat) + ADAM_EPS) + ADAM_WD * w)
    return delta, m, v


def reference(x, norm_gains, even_w_in, hgrn_lb_logits, hgrn_norm, rg_conv_w, rg_conv_b, rg_wa, rg_ba, rg_wx, rg_bx, rg_lambda, even_w_out, odd_w_in, fox_f_bias, odd_w_out, ffn_w_up, ffn_conv_w, ffn_conv_b, ffn_w_down, loss_target, m_norm_gains, m_even_w_in, m_hgrn_lb_logits, m_hgrn_norm, m_rg_conv_w, m_rg_conv_b, m_rg_wa, m_rg_ba, m_rg_wx, m_rg_bx, m_rg_lambda, m_even_w_out, m_odd_w_in, m_fox_f_bias, m_odd_w_out, m_ffn_w_up, m_ffn_conv_w, m_ffn_conv_b, m_ffn_w_down, v_norm_gains, v_even_w_in, v_hgrn_lb_logits, v_hgrn_norm, v_rg_conv_w, v_rg_conv_b, v_rg_wa, v_rg_ba, v_rg_wx, v_rg_bx, v_rg_lambda, v_even_w_out, v_odd_w_in, v_fox_f_bias, v_odd_w_out, v_ffn_w_up, v_ffn_conv_w, v_ffn_conv_b, v_ffn_w_down):
    given = dict(x=x, norm_gains=norm_gains, even_w_in=even_w_in, hgrn_lb_logits=hgrn_lb_logits, hgrn_norm=hgrn_norm, rg_conv_w=rg_conv_w, rg_conv_b=rg_conv_b, rg_wa=rg_wa, rg_ba=rg_ba, rg_wx=rg_wx, rg_bx=rg_bx, rg_lambda=rg_lambda, even_w_out=even_w_out, odd_w_in=odd_w_in, fox_f_bias=fox_f_bias, odd_w_out=odd_w_out, ffn_w_up=ffn_w_up, ffn_conv_w=ffn_conv_w, ffn_conv_b=ffn_conv_b, ffn_w_down=ffn_w_down, loss_target=loss_target, m_norm_gains=m_norm_gains, m_even_w_in=m_even_w_in, m_hgrn_lb_logits=m_hgrn_lb_logits, m_hgrn_norm=m_hgrn_norm, m_rg_conv_w=m_rg_conv_w, m_rg_conv_b=m_rg_conv_b, m_rg_wa=m_rg_wa, m_rg_ba=m_rg_ba, m_rg_wx=m_rg_wx, m_rg_bx=m_rg_bx, m_rg_lambda=m_rg_lambda, m_even_w_out=m_even_w_out, m_odd_w_in=m_odd_w_in, m_fox_f_bias=m_fox_f_bias, m_odd_w_out=m_odd_w_out, m_ffn_w_up=m_ffn_w_up, m_ffn_conv_w=m_ffn_conv_w, m_ffn_conv_b=m_ffn_conv_b, m_ffn_w_down=m_ffn_w_down, v_norm_gains=v_norm_gains, v_even_w_in=v_even_w_in, v_hgrn_lb_logits=v_hgrn_lb_logits, v_hgrn_norm=v_hgrn_norm, v_rg_conv_w=v_rg_conv_w, v_rg_conv_b=v_rg_conv_b, v_rg_wa=v_rg_wa, v_rg_ba=v_rg_ba, v_rg_wx=v_rg_wx, v_rg_bx=v_rg_bx, v_rg_lambda=v_rg_lambda, v_even_w_out=v_even_w_out, v_odd_w_in=v_odd_w_in, v_fox_f_bias=v_fox_f_bias, v_odd_w_out=v_odd_w_out, v_ffn_w_up=v_ffn_w_up, v_ffn_conv_w=v_ffn_conv_w, v_ffn_conv_b=v_ffn_conv_b, v_ffn_w_down=v_ffn_w_down)
    weights = {n: given[n] for n in TWIN_WEIGHTS}
    shared = {n: given[n] for n in SHARED_INPUTS}
    per_example = {n: given[n] for n in ['x']}
    grad_fn = _jax.value_and_grad(_loss, argnums=(0, 1))

    def one_microbatch(ex, loss_target):
        ex = dict(ex)
        diff = ex.pop(TWIN_DIFF_INPUT)
        return grad_fn(weights, diff, {**shared, **ex}, loss_target)

    if N_MICROBATCH == 1:
        loss, (grad_w, grad_x) = one_microbatch(per_example, given["loss_target"])
    else:
        def body(carry, xs):
            loss_sum, grad_sum = carry
            l_k, (gw_k, gx_k) = one_microbatch(xs[0], xs[1])
            with _jax.named_scope("update"):
                return (loss_sum + l_k, _jax.tree.map(_jnp.add, grad_sum, gw_k)), gx_k

        init = (_jnp.zeros((), _jnp.float32), _jax.tree.map(_jnp.zeros_like, weights))
        (loss, grad_w), grad_x = _jax.lax.scan(body, init, (per_example, given["loss_target"]))
    with _jax.named_scope("update"):
        delta_w, new_m, new_v = {}, {}, {}
        for n in TWIN_WEIGHTS:
            delta_w[n], new_m[n], new_v[n] = _adamw(weights[n], grad_w[n], given["m_" + n], given["v_" + n])
    return (loss, grad_x, *[grad_w[n] for n in TWIN_WEIGHTS], *[delta_w[n] for n in TWIN_WEIGHTS],
            *[new_m[n] for n in TWIN_WEIGHTS], *[new_v[n] for n in TWIN_WEIGHTS])
```

```python
import functools

import jax
import jax.numpy as jnp
import numpy as np
from jax import lax
from jax.experimental import pallas as pl
from jax.experimental.pallas import tpu as pltpu

F32 = jnp.float32
BF16 = jnp.bfloat16

D_MODEL = 1024
A_HEADS = 4
A_DIM = 128
A_WIDTH = A_HEADS * A_DIM
HGRN_CHUNK = 64
HGRN_SEG = 512
B_WIDTH = 512
B_BLOCKS = 8
B_BLOCK_DIM = 64
B_CONV = 4
RG_C = 8.0
C_HEADS = 16
C_HEAD_DIM = 64
D_FF = 2816
FFN_CONV = 3
EPS = 1e-6
LANES = 128
SEG_ALIGN = 2048
N_DEV = 8
MESH = pl.DeviceIdType.MESH
NEG = -1e30

ADAM_LR = 0.001
ADAM_B1 = 0.9
ADAM_B2 = 0.999
ADAM_EPS = 1e-08
ADAM_WD = 0.01
ADAM_STEP = 10


def _dg(a, b, pat):
    nb = a.ndim - 2
    batch = (tuple(range(nb)), tuple(range(nb)))
    ca = a.ndim - 1 if pat[0] == "n" else a.ndim - 2
    cb = b.ndim - 2 if pat[1] == "n" else b.ndim - 1
    return lax.dot_general(a.astype(BF16), b.astype(BF16), (((ca,), (cb,)), batch), preferred_element_type=F32)


@functools.partial(jax.custom_vjp, nondiff_argnums=(2,))
def bdot(a, b, pat):
    return _dg(a, b, pat)


def _bdot_fwd(a, b, pat):
    return _dg(a, b, pat), (a, b)


def _bdot_bwd(pat, res, g):
    a, b = res
    if pat == "nn":
        return _dg(g, b, "nt"), _dg(a, g, "tn")
    if pat == "nt":
        return _dg(g, b, "nn"), _dg(g, a, "tn")
    return _dg(b, g, "nt"), _dg(a, g, "nn")


bdot.defvjp(_bdot_fwd, _bdot_bwd)


def _shift_raw(x, s, up, fill):
    if s == 0:
        return x
    n = x.shape[0]
    r = pltpu.roll(x, (n - s) if up else s, 0)
    idx = lax.broadcasted_iota(jnp.int32, x.shape, 0)
    mask = (idx >= n - s) if up else (idx < s)
    return jnp.where(mask, jnp.asarray(fill, x.dtype), r)


@functools.partial(jax.custom_vjp, nondiff_argnums=(1,))
def shift_down(x, s):
    return _shift_raw(x, s, False, 0.0)


def _shift_down_fwd(x, s):
    return _shift_raw(x, s, False, 0.0), None


def _shift_down_bwd(s, _, g):
    return (_shift_raw(g, s, True, 0.0),)


shift_down.defvjp(_shift_down_fwd, _shift_down_bwd)


def _scan_impl(a, u, up):
    n = a.shape[0]
    s = 1
    while s < n:
        u = a * _shift_raw(u, s, up, 0.0) + u
        if 2 * s < n:
            a = a * _shift_raw(a, s, up, 1.0)
        s *= 2
    return u


@jax.custom_vjp
def lin_scan(a, u):
    return _scan_impl(a, u, False)


def _lin_scan_fwd(a, u):
    h = _scan_impl(a, u, False)
    return h, (a, h)


def _lin_scan_bwd(res, g):
    a, h = res
    gh = _scan_impl(_shift_raw(a, 1, True, 0.0), g, True)
    return gh * _shift_raw(h, 1, False, 0.0), gh


lin_scan.defvjp(_lin_scan_fwd, _lin_scan_bwd)


def _cumsum_impl(x, up, period):
    n = x.shape[0]
    span = n if period is None else period
    idx = lax.broadcasted_iota(jnp.int32, x.shape, 0)
    pos = idx if period is None else idx % period
    s = 1
    while s < span:
        sh = _shift_raw(x, s, up, 0.0)
        if period is not None:
            keep = (pos < period - s) if up else (pos >= s)
            sh = jnp.where(keep, sh, 0.0)
        x = x + sh
        s *= 2
    return x


@functools.partial(jax.custom_vjp, nondiff_argnums=(1,))
def cumsum_rows(x, period):
    return _cumsum_impl(x, False, period)


def _cumsum_fwd(x, period):
    return _cumsum_impl(x, False, period), None


def _cumsum_bwd(period, _, g):
    return (_cumsum_impl(g, True, period),)


cumsum_rows.defvjp(_cumsum_fwd, _cumsum_bwd)


def _sigmoid(x):
    return jax.nn.sigmoid(x)


def _expm1(x):
    return jnp.tanh(0.5 * x) * (jnp.exp(x) + 1.0)


def _softplus(x):
    return jnp.maximum(x, 0.0) + jnp.log(1.0 + jnp.exp(-jnp.abs(x)))


def _rms(x, g):
    return x * lax.rsqrt(jnp.mean(x * x, axis=-1, keepdims=True) + EPS) * g


def fn_prenorm(x, g):
    return (_rms(x, g).astype(BF16),)


def fn_addnorm2(x, y, g_post, g_pre):
    x1 = x + _rms(y, g_post)
    return x1, _rms(x1, g_pre).astype(BF16)


def fn_input_norm(x, g):
    return x, _rms(x, g).astype(BF16)


def fn_final(x, y, tgt, g_post):
    out = x + _rms(y, g_post)
    err = out - tgt
    dy = err * (1.0 / D_MODEL)
    loss = 0.5 * jnp.sum(jnp.mean(err * err, axis=-1, keepdims=True), axis=0, keepdims=True)
    return dy, jnp.broadcast_to(loss, (1, LANES))


def fn_rms_only(y, g):
    return (_rms(y, g),)


def fn_ffn_mid(hid, cw, cb):
    c = cb
    for k in range(FFN_CONV):
        c = c + cw[k:k + 1, :] * shift_down(hid, FFN_CONV - 1 - k)
    return ((jax.nn.gelu(c[:, :LANES]) * c[:, LANES:]).astype(BF16),)


def fn_rglru(z, cw, cb, wa, ba, wx, bx, lam):
    xb, yb = z[:, :LANES], z[:, LANES:]
    xf = cb
    for k in range(B_CONV):
        xf = xf + cw[k:k + 1, :] * shift_down(xb, B_CONV - 1 - k)
    r = _sigmoid(bdot(xf, wa, "nn") + ba)
    i = _sigmoid(bdot(xf, wx, "nn") + bx)
    log_a = -RG_C * r * _softplus(-lam)
    a = jnp.exp(log_a)
    u = jnp.sqrt(-_expm1(2.0 * log_a)) * (i * xf)
    h = lin_scan(a, u)
    return ((h * jax.nn.gelu(yb)).astype(BF16),)


def fn_fox_gate(zf, bias):
    return (cumsum_rows(jax.nn.log_sigmoid(zf + bias), None),)


def fn_hgrn_seg(z, st, logits, hn):
    rows = z.shape[0]
    nc = rows // HGRN_CHUNK
    q, fl, v, g = z[:, :128], z[:, 128:256], z[:, 256:384], z[:, 384:512]
    l0, l1, l2 = logits[0:1, :], logits[1:2, :], logits[2:3, :]
    mx = jnp.maximum(jnp.maximum(l0, l1), l2)
    e0, e1, e2 = jnp.exp(l0 - mx), jnp.exp(l1 - mx), jnp.exp(l2 - mx)
    lb = e0 / (e0 + e1 + e2)
    forget = lb + (1.0 - lb) * _sigmoid(fl)
    qs = q * _sigmoid(q)
    kk = 1.0 - forget
    logf = jnp.log(forget)
    bcum = cumsum_rows(logf, HGRN_CHUNK)
    c3 = lambda t: t.reshape(nc, HGRN_CHUNK, 128)
    b_last = jnp.sum(c3(logf), axis=1, keepdims=True)
    bcum3 = c3(bcum)
    q_dec = c3(qs) * jnp.exp(bcum3)
    k_dec = c3(kk) * jnp.exp(-bcum3)
    k_upd = c3(kk) * jnp.exp(b_last - bcum3)
    v3 = c3(v)
    scores = bdot(q_dec, k_dec, "nt")
    ri = lax.broadcasted_iota(jnp.int32, scores.shape, 1)
    ci = lax.broadcasted_iota(jnp.int32, scores.shape, 2)
    scores = jnp.where(ri >= ci, scores, 0.0)
    o = bdot(scores, v3, "nn")
    upd_t = bdot(v3, k_upd, "tn")
    dec = jnp.exp(b_last)
    prev = []
    for n in range(nc):
        prev.append(st)
        st = st * dec[n] + upd_t[n]
    o = o + bdot(q_dec, jnp.stack(prev), "nt")
    o = o.reshape(rows, 128)
    o = o * lax.rsqrt(jnp.mean(o * o, axis=-1, keepdims=True) + EPS) * hn
    return (o * _sigmoid(g)).astype(BF16), st


class Row:
    def __init__(self, arr, cb=None, off=0):
        self.arr, self.cb, self.off = arr, cb, off

    def spec(self, tm):
        if self.cb is None:
            return pl.BlockSpec((tm, self.arr.shape[1]), lambda j, i: (i, 0))
        off = self.off
        return pl.BlockSpec((tm, self.cb), lambda j, i: (i, j + off))


class Par:
    def __init__(self, arr, kind="full", bs=None):
        self.arr, self.kind, self.bs = arr, kind, bs

    def block(self):
        if self.kind == "full":
            return self.arr.shape
        if self.kind == "col":
            return (self.arr.shape[0], self.bs)
        return (self.bs, self.arr.shape[1])

    def spec(self):
        if self.kind == "full":
            return pl.BlockSpec(self.block(), lambda j, i: (0, 0))
        if self.kind == "col":
            return pl.BlockSpec(self.block(), lambda j, i: (0, j))
        return pl.BlockSpec(self.block(), lambda j, i: (j, 0))


class Out:
    def __init__(self, width, dtype, cb=None, off=0):
        self.width, self.dtype, self.cb, self.off = width, dtype, cb, off

    def spec(self, tm):
        if self.cb is None:
            return pl.BlockSpec((tm, self.width), lambda j, i: (i, 0))
        off = self.off
        return pl.BlockSpec((tm, self.cb), lambda j, i: (i, j + off))


def _params(sem):
    return pltpu.CompilerParams(dimension_semantics=sem, vmem_limit_bytes=56 * 1024 * 1024)


def tile_fwd(name, fn, *, m, tm, nj, rows, pars, outs, n_acc=0):
    n_r, n_p, n_o = len(rows), len(pars), len(outs)

    def body(*refs):
        ins = [r[...] for r in refs[:n_r + n_p]]
        res = fn(*ins)
        o_refs = refs[n_r + n_p:]
        for k in range(n_o):
            o_refs[k][...] = res[k].astype(o_refs[k].dtype)
        first = jnp.logical_and(pl.program_id(0) == 0, pl.program_id(1) == 0)
        for k in range(n_acc):
            ref = o_refs[n_o + k]

            @pl.when(first)
            def _():
                ref[...] = jnp.zeros_like(ref)

            ref[...] += res[n_o + k]

    out_shape = [jax.ShapeDtypeStruct((m, o.width), o.dtype) for o in outs]
    out_specs = [o.spec(tm) for o in outs]
    for _ in range(n_acc):
        out_shape.append(jax.ShapeDtypeStruct((1, LANES), F32))
        out_specs.append(pl.BlockSpec((1, LANES), lambda j, i: (0, 0)))
    sem = ("arbitrary", "arbitrary") if n_acc else ("parallel", "parallel")
    return pl.pallas_call(
        body, grid=(nj, m // tm), name=name,
        in_specs=[r.spec(tm) for r in rows] + [p.spec() for p in pars],
        out_specs=out_specs, out_shape=out_shape, compiler_params=_params(sem),
    )(*[r.arr for r in rows], *[p.arr for p in pars])


def tile_bwd(name, fn, *, m, tm, nj, rows, pars, cts, drows):
    n_r, n_p, n_c = len(rows), len(pars), len(cts)
    want = [k for k in range(n_r) if drows[k] is not None]

    def body(*refs):
        ins = [r[...] for r in refs[:n_r + n_p]]
        ct = [r[...] for r in refs[n_r + n_p:n_r + n_p + n_c]]
        o_refs = refs[n_r + n_p + n_c:]
        res, vjp = jax.vjp(fn, *ins)
        grads = vjp(tuple(c.astype(r.dtype) for c, r in zip(ct, res)))
        for pos, k in enumerate(want):
            o_refs[pos][...] = grads[k].astype(o_refs[pos].dtype)
        for k in range(n_p):
            ref = o_refs[len(want) + k]
            first = pl.program_id(1) == 0
            if pars[k].kind == "full":
                first = jnp.logical_and(first, pl.program_id(0) == 0)

            @pl.when(first)
            def _():
                ref[...] = jnp.zeros_like(ref)

            ref[...] += grads[n_r + k].astype(F32)

    out_shape = [jax.ShapeDtypeStruct((m, drows[k].width), drows[k].dtype) for k in want]
    out_specs = [drows[k].spec(tm) for k in want]
    for p in pars:
        out_shape.append(jax.ShapeDtypeStruct(p.arr.shape, F32))
        out_specs.append(p.spec())
    return pl.pallas_call(
        body, grid=(nj, m // tm), name=name,
        in_specs=[r.spec(tm) for r in rows] + [p.spec() for p in pars] + [c.spec(tm) for c in cts],
        out_specs=out_specs, out_shape=out_shape, compiler_params=_params(("arbitrary", "arbitrary")),
    )(*[r.arr for r in rows], *[p.arr for p in pars], *[c.arr for c in cts])


def _pick(n, cands):
    for c in cands:
        if n % c == 0:
            return c
    return n


def matmul(name, a, b, *, ta=False, tb=False, out_dtype=F32):
    m, k = (a.shape[1], a.shape[0]) if ta else a.shape
    n = b.shape[0] if tb else b.shape[1]
    assert k == (b.shape[1] if tb else b.shape[0])
    tm = _pick(m, (512, 256, 128))
    tn = _pick(n, (512, 640, 384, 256, 128))
    tk = _pick(k, (512, 640, 256, 128))
    nk = k // tk
    pat = ("t" if ta else "n") + ("t" if tb else "n")

    def body(a_ref, b_ref, o_ref, acc_ref):
        kk = pl.program_id(2)

        @pl.when(kk == 0)
        def _():
            acc_ref[...] = jnp.zeros_like(acc_ref)

        av, bv = a_ref[...].astype(BF16), b_ref[...].astype(BF16)
        if pat == "tt":
            av = av.T
            acc_ref[...] += _dg(av, bv, "nt")
        else:
            acc_ref[...] += _dg(av, bv, pat)

        @pl.when(kk == nk - 1)
        def _():
            o_ref[...] = acc_ref[...].astype(out_dtype)

    a_spec = pl.BlockSpec((tk, tm), lambda i, j, kk: (kk, i)) if ta else pl.BlockSpec((tm, tk), lambda i, j, kk: (i, kk))
    b_spec = pl.BlockSpec((tn, tk), lambda i, j, kk: (j, kk)) if tb else pl.BlockSpec((tk, tn), lambda i, j, kk: (kk, j))
    return pl.pallas_call(
        body, grid=(m // tm, n // tn, nk), name=name,
        in_specs=[a_spec, b_spec], out_specs=pl.BlockSpec((tm, tn), lambda i, j, kk: (i, j)),
        out_shape=jax.ShapeDtypeStruct((m, n), out_dtype), scratch_shapes=[pltpu.VMEM((tm, tn), F32)],
        compiler_params=_params(("parallel", "parallel", "arbitrary")),
    )(a, b)


def hgrn_fwd(name, z, logits, hnorm, *, n_batch, seq, col_off=0):
    m = n_batch * seq
    ts = min(HGRN_SEG, seq)
    n_seg = seq // ts

    def body(z_ref, lg_ref, hn_ref, o_ref, sp_ref, st_ref):
        s = pl.program_id(2)

        @pl.when(s == 0)
        def _():
            st_ref[...] = jnp.zeros_like(st_ref)

        st = st_ref[...]
        sp_ref[...] = st
        o, st_new = fn_hgrn_seg(z_ref[...], st, lg_ref[...], hn_ref[...])
        o_ref[...] = o
        st_ref[...] = st_new

    return pl.pallas_call(
        body, grid=(A_HEADS, n_batch, n_seg), name=name,
        in_specs=[pl.BlockSpec((ts, 512), lambda h, b, s: (b * n_seg + s, h + col_off)),
                  pl.BlockSpec((3, 128), lambda h, b, s: (0, h)),
                  pl.BlockSpec((1, 128), lambda h, b, s: (0, h))],
        out_specs=[pl.BlockSpec((ts, 128), lambda h, b, s: (b * n_seg + s, h)),
                   pl.BlockSpec((128, 128), lambda h, b, s: ((b * n_seg + s) * A_HEADS + h, 0))],
        out_shape=[jax.ShapeDtypeStruct((m, A_WIDTH), BF16),
                   jax.ShapeDtypeStruct((n_batch * n_seg * A_HEADS * 128, 128), F32)],
        scratch_shapes=[pltpu.VMEM((128, 128), F32)],
        compiler_params=_params(("arbitrary", "arbitrary", "arbitrary")),
    )(z, logits, hnorm)


def hgrn_bwd(name, z, sprev, logits, hnorm, do, *, n_batch, seq, col_off=0, do_off=0):
    m = n_batch * seq
    ts = min(HGRN_SEG, seq)
    n_seg = seq // ts

    def body(z_ref, sp_ref, lg_ref, hn_ref, do_ref, dz_ref, dlg_ref, dhn_ref, dst_ref):
        s = pl.program_id(2)

        @pl.when(s == 0)
        def _():
            dst_ref[...] = jnp.zeros_like(dst_ref)

        res, vjp = jax.vjp(fn_hgrn_seg, z_ref[...], sp_ref[...], lg_ref[...], hn_ref[...])
        dz, dst, dlg, dhn = vjp((do_ref[...].astype(res[0].dtype), dst_ref[...]))
        dz_ref[...] = dz.astype(dz_ref.dtype)
        dst_ref[...] = dst
        first = jnp.logical_and(pl.program_id(1) == 0, s == 0)

        @pl.when(first)
        def _():
            dlg_ref[...] = jnp.zeros_like(dlg_ref)
            dhn_ref[...] = jnp.zeros_like(dhn_ref)

        dlg_ref[...] += dlg
        dhn_ref[...] += dhn

    rev = lambda b, s: b * n_seg + (n_seg - 1 - s)
    return pl.pallas_call(
        body, grid=(A_HEADS, n_batch, n_seg), name=name,
        in_specs=[pl.BlockSpec((ts, 512), lambda h, b, s: (rev(b, s), h + col_off)),
                  pl.BlockSpec((128, 128), lambda h, b, s: (rev(b, s) * A_HEADS + h, 0)),
                  pl.BlockSpec((3, 128), lambda h, b, s: (0, h)),
                  pl.BlockSpec((1, 128), lambda h, b, s: (0, h)),
                  pl.BlockSpec((ts, 128), lambda h, b, s: (rev(b, s), h + do_off))],
        out_specs=[pl.BlockSpec((ts, 512), lambda h, b, s: (rev(b, s), h)),
                   pl.BlockSpec((3, 128), lambda h, b, s: (0, h)),
                   pl.BlockSpec((1, 128), lambda h, b, s: (0, h))],
        out_shape=[jax.ShapeDtypeStruct((m, 4 * A_WIDTH), BF16),
                   jax.ShapeDtypeStruct(logits.shape, F32),
                   jax.ShapeDtypeStruct(hnorm.shape, F32)],
        scratch_shapes=[pltpu.VMEM((128, 128), F32)],
        compiler_params=_params(("arbitrary", "arbitrary", "arbitrary")),
    )(z, sprev, logits, hnorm, do)


ATT_BLK = 256


def fox_fwd(name, q, k, v, ccol, crow):
    bh, seq, dh = q.shape
    blk = min(ATT_BLK, seq)
    nq = seq // blk
    scale = dh ** -0.5

    def body(q_ref, k_ref, v_ref, cc_ref, cr_ref, o_ref, lse_ref):
        qi = pl.program_id(1)
        qv = q_ref[0]
        cq = cc_ref[0]
        row = qi * blk + lax.broadcasted_iota(jnp.int32, (blk, blk), 0)

        def step(j, carry):
            mx, l, acc = carry
            st = pl.multiple_of(j * blk, blk)
            kj = k_ref[0, pl.ds(st, blk), :]
            vj = v_ref[0, pl.ds(st, blk), :]
            s = _dg(qv, kj, "nt") * scale + cq - cr_ref[0, j]
            col = j * blk + lax.broadcasted_iota(jnp.int32, (blk, blk), 1)
            s = jnp.where(row >= col, s, NEG)
            mx_new = jnp.maximum(mx, jnp.max(s, axis=-1, keepdims=True))
            p = jnp.exp(s - mx_new)
            alpha = jnp.exp(mx - mx_new)
            l = alpha * l + jnp.sum(p, axis=-1, keepdims=True)
            acc = alpha * acc + _dg(p, vj, "nn")
            return mx_new, l, acc

        init = (jnp.full((blk, 1), NEG, F32), jnp.zeros((blk, 1), F32), jnp.zeros((blk, dh), F32))
        mx, l, acc = lax.fori_loop(0, qi + 1, step, init)
        o_ref[0] = (acc / l).astype(o_ref.dtype)
        lse_ref[0] = mx + jnp.log(l)

    return pl.pallas_call(
        body, grid=(bh, nq), name=name,
        in_specs=[pl.BlockSpec((1, blk, dh), lambda b, i: (b, i, 0)),
                  pl.BlockSpec((1, seq, dh), lambda b, i: (b, 0, 0)),
                  pl.BlockSpec((1, seq, dh), lambda b, i: (b, 0, 0)),
                  pl.BlockSpec((1, blk, 1), lambda b, i: (b, i, 0)),
                  pl.BlockSpec((1, nq, 1, blk), lambda b, i: (b, 0, 0, 0))],
        out_specs=[pl.BlockSpec((1, blk, dh), lambda b, i: (b, i, 0)),
                   pl.BlockSpec((1, blk, 1), lambda b, i: (b, i, 0))],
        out_shape=[jax.ShapeDtypeStruct((bh, seq, dh), BF16), jax.ShapeDtypeStruct((bh, seq, 1), F32)],
        compiler_params=_params(("parallel", "arbitrary")),
    )(q, k, v, ccol, crow)


def fox_bwd(name, q, k, v, o, do, lse, ccol, crow):
    bh, seq, dh = q.shape
    blk = min(ATT_BLK, seq)
    nq = seq // blk
    scale = dh ** -0.5

    def body(q_ref, k_ref, v_ref, o_ref, do_ref, lse_ref, cc_ref, cr_ref, dq_ref, dk_ref, dv_ref, drow_ref, dcol_ref, del_ref):
        j = pl.program_id(1)

        @pl.when(j == 0)
        def _():
            dq_ref[...] = jnp.zeros_like(dq_ref)
            drow_ref[...] = jnp.zeros_like(drow_ref)
            del_ref[...] = jnp.sum(do_ref[0].astype(F32) * o_ref[0].astype(F32), axis=-1, keepdims=True)

        kj, vj = k_ref[0], v_ref[0]
        ck = cr_ref[0, 0]
        col = j * blk + lax.broadcasted_iota(jnp.int32, (blk, blk), 1)

        def step(i, carry):
            dk, dv, dcol = carry
            st = pl.multiple_of(i * blk, blk)
            rows = pl.ds(st, blk)
            qv, dov = q_ref[0, rows, :], do_ref[0, rows, :]
            s = _dg(qv, kj, "nt") * scale + cc_ref[0, rows, :] - ck
            row = i * blk + lax.broadcasted_iota(jnp.int32, (blk, blk), 0)
            p = jnp.where(row >= col, jnp.exp(s - lse_ref[0, rows, :]), 0.0)
            dv = dv + _dg(p, dov, "tn")
            ds = p * (_dg(dov, vj, "nt") - del_ref[rows, :])
            dk = dk + _dg(ds, qv, "tn") * scale
            dq_ref[0, rows, :] += _dg(ds, kj, "nn") * scale
            drow_ref[0, rows, :] += jnp.sum(ds, axis=-1, keepdims=True)
            return dk, dv, dcol + jnp.sum(ds, axis=0, keepdims=True)

        init = (jnp.zeros((blk, dh), F32), jnp.zeros((blk, dh), F32), jnp.zeros((1, blk), F32))
        dk, dv, dcol = lax.fori_loop(j, nq, step, init)
        dk_ref[0] = dk
        dv_ref[0] = dv
        dcol_ref[0, 0] = dcol

    full = pl.BlockSpec((1, seq, dh), lambda b, j: (b, 0, 0))
    blkd = pl.BlockSpec((1, blk, dh), lambda b, j: (b, j, 0))
    col1 = pl.BlockSpec((1, seq, 1), lambda b, j: (b, 0, 0))
    rowb = pl.BlockSpec((1, 1, 1, blk), lambda b, j: (b, j, 0, 0))
    return pl.pallas_call(
        body, grid=(bh, nq), name=name,
        in_specs=[full, blkd, blkd, full, full, col1, col1, rowb],
        out_specs=[full, blkd, blkd, col1, rowb],
        out_shape=[jax.ShapeDtypeStruct((bh, seq, dh), F32), jax.ShapeDtypeStruct((bh, seq, dh), F32),
                   jax.ShapeDtypeStruct((bh, seq, dh), F32), jax.ShapeDtypeStruct((bh, seq, 1), F32),
                   jax.ShapeDtypeStruct((bh, nq, 1, blk), F32)],
        scratch_shapes=[pltpu.VMEM((seq, 1), F32)],
        compiler_params=_params(("parallel", "arbitrary")),
    )(q, k, v, o, do, lse, ccol, crow)


def _mesh_pos():
    return lax.axis_index("x"), lax.axis_index("y"), lax.axis_index("c")


def _flip(v, bit):
    return 1 - v if bit else v


def all_gather(name, block):
    r, c_ = block.shape

    def body(x_ref, out_ref, send_sems, recv_sems, local_sem):
        x, y, c = _mesh_pos()
        me, sibling = (x, y, c), (x, y, 1 - c)
        chips = [(1 - x, y), (x, 1 - y), (1 - x, 1 - y)]

        def slot(px, py, pc):
            return out_ref.at[4 * px + 2 * py + pc]

        def copy(k, blk, to, src=None):
            return pltpu.make_async_remote_copy(
                src_ref=slot(*blk) if src is None else src, dst_ref=slot(*blk),
                send_sem=send_sems.at[k], recv_sem=recv_sems.at[k], device_id=to, device_id_type=MESH)

        mine = pltpu.make_async_copy(x_ref, slot(*me), local_sem)
        mine.start()
        first = [copy(0, me, sibling, src=x_ref)]
        first += [copy(1 + j, me, (*chip, c), src=x_ref) for j, chip in enumerate(chips)]
        for cp in first:
            cp.start()
        passed = [copy(4 + j, (*chip, c), sibling) for j, chip in enumerate(chips)]
        for j, chip in enumerate(chips):
            copy(1 + j, (*chip, c), me).wait_recv()
            passed[j].start()
        copy(0, sibling, me).wait_recv()
        for j, chip in enumerate(chips):
            copy(4 + j, (*chip, 1 - c), me).wait_recv()
        for cp in first + passed:
            cp.wait_send()
        mine.wait()

    return pl.pallas_call(
        body, name=name, out_shape=jax.ShapeDtypeStruct((N_DEV, r, c_), block.dtype),
        in_specs=[pl.BlockSpec(memory_space=pl.ANY)], out_specs=pl.BlockSpec(memory_space=pl.ANY),
        scratch_shapes=[pltpu.SemaphoreType.DMA((7,)), pltpu.SemaphoreType.DMA((7,)), pltpu.SemaphoreType.DMA],
    )(block)


def all_to_all(name, send):
    def body(s_ref, r_ref, send_sems, recv_sems, local_sem):
        x, y, c = _mesh_pos()
        me = 4 * x + 2 * y + c
        mine = pltpu.make_async_copy(s_ref.at[me], r_ref.at[me], local_sem)
        mine.start()
        copies = []
        for k in range(1, N_DEV):
            px, py, pc = _flip(x, k & 4), _flip(y, k & 2), _flip(c, k & 1)
            copies.append(pltpu.make_async_remote_copy(
                src_ref=s_ref.at[4 * px + 2 * py + pc], dst_ref=r_ref.at[me],
                send_sem=send_sems.at[k - 1], recv_sem=recv_sems.at[k - 1], device_id=(px, py, pc), device_id_type=MESH))
        for cp in copies:
            cp.start()
        for cp in copies:
            cp.wait_recv()
        for cp in copies:
            cp.wait_send()
        mine.wait()

    return pl.pallas_call(
        body, name=name, out_shape=jax.ShapeDtypeStruct(send.shape, send.dtype),
        in_specs=[pl.BlockSpec(memory_space=pl.ANY)], out_specs=pl.BlockSpec(memory_space=pl.ANY),
        scratch_shapes=[pltpu.SemaphoreType.DMA((7,)), pltpu.SemaphoreType.DMA((7,)), pltpu.SemaphoreType.DMA],
    )(send)


def adam_update(name, partials, w, m_, v_):
    r = w.shape[0]
    tr = _pick(r, (1024, 512, 256, 128, 64, 32, 16, 8))

    def body(p_ref, w_ref, m_ref, v_ref, g_ref, d_ref, nm_ref, nv_ref):
        g = ((p_ref[0] + p_ref[1]) + (p_ref[2] + p_ref[3])) + ((p_ref[4] + p_ref[5]) + (p_ref[6] + p_ref[7]))
        m = ADAM_B1 * m_ref[...] + (1.0 - ADAM_B1) * g
        v = ADAM_B2 * v_ref[...] + (1.0 - ADAM_B2) * (g * g)
        m_hat = m / (1.0 - ADAM_B1 ** ADAM_STEP)
        v_hat = v / (1.0 - ADAM_B2 ** ADAM_STEP)
        g_ref[...] = g
        d_ref[...] = -ADAM_LR * (m_hat / (jnp.sqrt(v_hat) + ADAM_EPS) + ADAM_WD * w_ref[...])
        nm_ref[...] = m
        nv_ref[...] = v

    spec = pl.BlockSpec((tr, LANES), lambda i: (i, 0))
    return pl.pallas_call(
        body, grid=(r // tr,), name=name,
        in_specs=[pl.BlockSpec((N_DEV, tr, LANES), lambda i: (0, i, 0)), spec, spec, spec],
        out_specs=[spec] * 4, out_shape=[jax.ShapeDtypeStruct((r, LANES), F32)] * 4,
        compiler_params=_params(("parallel",)),
    )(partials, w, m_, v_)


def _seg_len(n):
    return -(-n // SEG_ALIGN) * SEG_ALIGN


def pack_last(arrs, lead=()):
    parts = []
    for a in arrs:
        flat = a.reshape(lead + (-1,))
        n = flat.shape[-1]
        parts.append(jnp.pad(flat, [(0, 0)] * len(lead) + [(0, _seg_len(n) - n)]))
    cat = jnp.concatenate(parts, axis=-1)
    return cat.reshape(lead + (cat.shape[-1] // LANES, LANES))


def unpack_last(buf, shapes, lead=()):
    flat = buf.reshape(lead + (-1,))
    out, off = [], 0
    for shp in shapes:
        n = int(np.prod(shp))
        out.append(flat[..., off:off + n].reshape(lead + tuple(shp)))
        off += _seg_len(n)
    return out


def even_in_to_internal(w):
    lead = w.shape[:-1]
    a = w[..., :2048].reshape(lead + (4, 4, 128))
    a = jnp.swapaxes(a, -3, -2).reshape(lead + (2048,))
    b = w[..., 2048:].reshape(lead + (2, 4, 128))
    b = jnp.swapaxes(b, -3, -2).reshape(lead + (1024,))
    return jnp.concatenate([a, b], axis=-1)


def even_in_to_external(w):
    lead = w.shape[:-1]
    a = w[..., :2048].reshape(lead + (4, 4, 128))
    a = jnp.swapaxes(a, -3, -2).reshape(lead + (2048,))
    b = w[..., 2048:].reshape(lead + (4, 2, 128))
    b = jnp.swapaxes(b, -3, -2).reshape(lead + (1024,))
    return jnp.concatenate([a, b], axis=-1)


def ffn_to_internal(w):
    lead = w.shape[:-1]
    return jnp.swapaxes(w.reshape(lead + (2, D_FF // LANES, LANES)), -3, -2).reshape(lead + (2 * D_FF,))


def ffn_to_external(w):
    lead = w.shape[:-1]
    return jnp.swapaxes(w.reshape(lead + (D_FF // LANES, 2, LANES)), -3, -2).reshape(lead + (2 * D_FF,))


def _cols_from_gather(g):
    g = jnp.moveaxis(g, 0, -2)
    return g.reshape(g.shape[:-2] + (g.shape[-2] * g.shape[-1],))


def _cols_to_blocks(w):
    w = w.reshape(w.shape[:-1] + (N_DEV, w.shape[-1] // N_DEV))
    return jnp.moveaxis(w, -2, 0)


def _rows_from_gather(g):
    g = jnp.moveaxis(g, 0, 1)
    return g.reshape(g.shape[0], g.shape[1] * g.shape[2], g.shape[3])


def _rows_to_blocks(w):
    w = w.reshape(w.shape[0], N_DEV, w.shape[1] // N_DEV, w.shape[2])
    return jnp.moveaxis(w, 1, 0)


def _block_diag(w):
    z = jnp.zeros((B_BLOCK_DIM, B_BLOCK_DIM), w.dtype)
    rows = []
    for j in range(B_BLOCKS // 2):
        top = jnp.concatenate([w[2 * j], z], axis=1)
        bot = jnp.concatenate([z, w[2 * j + 1]], axis=1)
        rows.append(jnp.concatenate([top, bot], axis=0))
    return jnp.concatenate(rows, axis=0)


def _block_diag_grad(d):
    out = []
    for j in range(B_BLOCKS // 2):
        blk = d[128 * j:128 * (j + 1)]
        out.append(blk[:64, :64])
        out.append(blk[64:, 64:])
    return jnp.stack(out)


SHARDED = ("norm_gains", "even_w_in", "rg_conv_w", "even_w_out", "odd_w_in", "odd_w_out", "ffn_w_up", "ffn_conv_w", "ffn_w_down")
REPLICATED = ("hgrn_lb_logits", "hgrn_norm", "rg_conv_b", "rg_wa", "rg_ba", "rg_wx", "rg_bx", "rg_lambda", "fox_f_bias", "ffn_conv_b")
NAMES = ("norm_gains", "even_w_in", "hgrn_lb_logits", "hgrn_norm", "rg_conv_w", "rg_conv_b", "rg_wa", "rg_ba", "rg_wx", "rg_bx",
         "rg_lambda", "even_w_out", "odd_w_in", "fox_f_bias", "odd_w_out", "ffn_w_up", "ffn_conv_w", "ffn_conv_b", "ffn_w_down")
ROW_SHARDED = ("even_w_out", "odd_w_out", "ffn_w_down")


def _row(v):
    return v.reshape(1, -1)


def _ffn_forward(tag, h, w_up, cw, cb, w_down, m, seq):
    hid = matmul(f"{tag}_up", h, w_up)
    (act,) = tile_fwd(f"{tag}_mid", fn_ffn_mid, m=m, tm=seq, nj=D_FF // LANES, rows=[Row(hid, 2 * LANES)],
                      pars=[Par(cw, "col", 2 * LANES), Par(cb, "col", 2 * LANES)], outs=[Out(D_FF, BF16, LANES)])
    return hid, act, matmul(f"{tag}_down", act, w_down)


def _ffn_backward(tag, df, h, hid, act, w_up, cw, cb, w_down, m, seq):
    dact = matmul(f"{tag}_dact", df, w_down, tb=True, out_dtype=BF16)
    d_wdown = matmul(f"{tag}_dwdown", act, df, ta=True)
    dhid, d_cw, d_cb = tile_bwd(f"{tag}_dmid", fn_ffn_mid, m=m, tm=seq, nj=D_FF // LANES, rows=[Row(hid, 2 * LANES)],
                                pars=[Par(cw, "col", 2 * LANES), Par(cb, "col", 2 * LANES)],
                                cts=[Row(dact, LANES)], drows=[Out(2 * D_FF, BF16, 2 * LANES)])
    dh = matmul(f"{tag}_dh", dhid, w_up, tb=True, out_dtype=BF16)
    d_wup = matmul(f"{tag}_dwup", h, dhid, ta=True)
    return dh, d_wup, d_cw, d_cb, d_wdown


def kernel(x, norm_gains, even_w_in, hgrn_lb_logits, hgrn_norm, rg_conv_w, rg_conv_b, rg_wa, rg_ba, rg_wx, rg_bx, rg_lambda, even_w_out, odd_w_in, fox_f_bias, odd_w_out, ffn_w_up, ffn_conv_w, ffn_conv_b, ffn_w_down, loss_target, m_norm_gains, m_even_w_in, m_hgrn_lb_logits, m_hgrn_norm, m_rg_conv_w, m_rg_conv_b, m_rg_wa, m_rg_ba, m_rg_wx, m_rg_bx, m_rg_lambda, m_even_w_out, m_odd_w_in, m_fox_f_bias, m_odd_w_out, m_ffn_w_up, m_ffn_conv_w, m_ffn_conv_b, m_ffn_w_down, v_norm_gains, v_even_w_in, v_hgrn_lb_logits, v_hgrn_norm, v_rg_conv_w, v_rg_conv_b, v_rg_wa, v_rg_ba, v_rg_wx, v_rg_bx, v_rg_lambda, v_even_w_out, v_odd_w_in, v_fox_f_bias, v_odd_w_out, v_ffn_w_up, v_ffn_conv_w, v_ffn_conv_b, v_ffn_w_down):
    local = dict(locals())
    w = {n: local[n] for n in NAMES}
    mom = {n: local["m_" + n] for n in NAMES}
    var = {n: local["v_" + n] for n in NAMES}
    n_batch, seq, _ = x.shape
    m = n_batch * seq
    tm = _pick(m, (512, 256, 128))

    big = ("even_w_in", "even_w_out", "odd_w_in", "odd_w_out", "ffn_w_up", "ffn_w_down")
    small = ("norm_gains", "rg_conv_w", "ffn_conv_w")
    gb = all_gather("gather_weights", pack_last([w[n].astype(BF16) for n in big]))
    gs = all_gather("gather_small", pack_last([w[n] for n in small]))
    gb = dict(zip(big, unpack_last(gb, [w[n].shape for n in big], lead=(N_DEV,))))
    gs = dict(zip(small, unpack_last(gs, [w[n].shape for n in small], lead=(N_DEV,))))

    w_in_e = even_in_to_internal(_cols_from_gather(gb["even_w_in"])[0])
    w_out_e = _rows_from_gather(gb["even_w_out"])[0]
    w_in_o = jnp.pad(_cols_from_gather(gb["odd_w_in"])[0], ((0, 0), (0, 3200 - 3088)))
    w_out_o = _rows_from_gather(gb["odd_w_out"])[0]
    w_up = ffn_to_internal(_cols_from_gather(gb["ffn_w_up"]))
    w_down = _rows_from_gather(gb["ffn_w_down"])
    gains = _cols_from_gather(gs["norm_gains"])
    rg_cw = _cols_from_gather(gs["rg_conv_w"])[0]
    ffn_cw = ffn_to_internal(_cols_from_gather(gs["ffn_conv_w"]))
    ffn_cb = ffn_to_internal(ffn_conv_b)
    gain = lambda l, k: gains[l, k:k + 1, :]
    wa_bd, wx_bd = _block_diag(rg_wa[0]), _block_diag(rg_wx[0])
    fbias = jnp.pad(fox_f_bias, ((0, 0), (0, LANES - C_HEADS)))

    x0 = x.reshape(m, D_MODEL)
    tgt = loss_target.reshape(m, D_MODEL)

    (h0,) = tile_fwd("l0_prenorm", fn_prenorm, m=m, tm=tm, nj=1, rows=[Row(x0)], pars=[Par(gain(0, 0))], outs=[Out(D_MODEL, BF16)])
    z0 = matmul("l0_in", h0, w_in_e)
    oa, sprev = hgrn_fwd("l0_hgrn", z0, hgrn_lb_logits, hgrn_norm, n_batch=n_batch, seq=seq)
    rg_pars = lambda: [Par(rg_cw, "col", LANES), Par(rg_conv_b, "col", LANES), Par(wa_bd, "row", LANES), Par(rg_ba, "col", LANES),
                       Par(wx_bd, "row", LANES), Par(rg_bx, "col", LANES), Par(rg_lambda, "col", LANES)]
    (ob,) = tile_fwd("l0_rglru", fn_rglru, m=m, tm=seq, nj=B_WIDTH // LANES, rows=[Row(z0, 2 * LANES, 2048 // (2 * LANES))],
                     pars=rg_pars(), outs=[Out(B_WIDTH, BF16, LANES)])
    mixcat0 = jnp.concatenate([oa, ob], axis=-1)
    mix0 = matmul("l0_out", mixcat0, w_out_e)
    x1, h1 = tile_fwd("l0_postnorm", fn_addnorm2, m=m, tm=tm, nj=1, rows=[Row(x0), Row(mix0)], pars=[Par(gain(0, 1)), Par(gain(0, 2))],
                      outs=[Out(D_MODEL, F32), Out(D_MODEL, BF16)])
    hid0, act0, f0 = _ffn_forward("l0_ffn", h1, w_up[0], ffn_cw[0], ffn_cb[0:1], w_down[0], m, seq)
    x2, h2 = tile_fwd("l0_ffnnorm", fn_addnorm2, m=m, tm=tm, nj=1, rows=[Row(x1), Row(f0)], pars=[Par(gain(0, 3)), Par(gain(1, 0))],
                      outs=[Out(D_MODEL, F32), Out(D_MODEL, BF16)])

    z1 = matmul("l1_in", h2, w_in_o)
    (cgate,) = tile_fwd("l1_gate", fn_fox_gate, m=m, tm=seq, nj=1, rows=[Row(z1, LANES, 3072 // LANES)], pars=[Par(fbias)],
                        outs=[Out(LANES, F32)])
    bh = n_batch * C_HEADS
    nqb = seq // min(ATT_BLK, seq)
    heads = lambda t: t.reshape(n_batch, seq, C_HEADS, C_HEAD_DIM).transpose(0, 2, 1, 3).reshape(bh, seq, C_HEAD_DIM)
    unheads = lambda t: t.reshape(n_batch, C_HEADS, seq, C_HEAD_DIM).transpose(0, 2, 1, 3).reshape(m, D_MODEL)
    qh, kh, vh = (heads(z1[:, i * D_MODEL:(i + 1) * D_MODEL].astype(BF16)) for i in range(3))
    c_bht = cgate[:, :C_HEADS].reshape(n_batch, seq, C_HEADS).transpose(0, 2, 1).reshape(bh, seq)
    ccol, crow = c_bht.reshape(bh, seq, 1), c_bht.reshape(bh, nqb, 1, seq // nqb)
    oh, lse = fox_fwd("l1_attn", qh, kh, vh, ccol, crow)
    oc = unheads(oh)
    mix1 = matmul("l1_out", oc, w_out_o)
    x3, h3 = tile_fwd("l1_postnorm", fn_addnorm2, m=m, tm=tm, nj=1, rows=[Row(x2), Row(mix1)], pars=[Par(gain(1, 1)), Par(gain(1, 2))],
                      outs=[Out(D_MODEL, F32), Out(D_MODEL, BF16)])
    hid1, act1, f1 = _ffn_forward("l1_ffn", h3, w_up[1], ffn_cw[1], ffn_cb[1:2], w_down[1], m, seq)
    dy, loss_part = tile_fwd("loss", fn_final, m=m, tm=tm, nj=1, rows=[Row(x3), Row(f1), Row(tgt)], pars=[Par(gain(1, 3))],
                             outs=[Out(D_MODEL, F32)], n_acc=1)

    df1, d_g13 = tile_bwd("l1_dffnnorm", fn_rms_only, m=m, tm=tm, nj=1, rows=[Row(f1)], pars=[Par(gain(1, 3))], cts=[Row(dy)],
                          drows=[Out(D_MODEL, BF16)])
    dh3, d_wup1, d_cw1, d_cb1, d_wdown1 = _ffn_backward("l1_ffn", df1, h3, hid1, act1, w_up[1], ffn_cw[1], ffn_cb[1:2], w_down[1], m, seq)
    dx2, dmix1, d_g11, d_g12 = tile_bwd("l1_dpostnorm", fn_addnorm2, m=m, tm=tm, nj=1, rows=[Row(x2), Row(mix1)],
                                        pars=[Par(gain(1, 1)), Par(gain(1, 2))], cts=[Row(dy), Row(dh3)],
                                        drows=[Out(D_MODEL, F32), Out(D_MODEL, BF16)])
    doc = matmul("l1_doc", dmix1, w_out_o, tb=True, out_dtype=BF16)
    d_wout_o = matmul("l1_dwout", oc, dmix1, ta=True)
    dq, dk, dv, drow, dcol = fox_bwd("l1_dattn", qh, kh, vh, oh, heads(doc), lse, ccol, crow)
    dc = (drow.reshape(bh, seq) - dcol.reshape(bh, seq)).reshape(n_batch, C_HEADS, seq).transpose(0, 2, 1).reshape(m, C_HEADS)
    dc = jnp.pad(dc, ((0, 0), (0, LANES - C_HEADS)))
    dzf, d_fbias = tile_bwd("l1_dgate", fn_fox_gate, m=m, tm=seq, nj=1, rows=[Row(z1, LANES, 3072 // LANES)], pars=[Par(fbias)],
                            cts=[Row(dc)], drows=[Out(LANES, BF16)])
    dz1 = jnp.concatenate([unheads(dq).astype(BF16), unheads(dk).astype(BF16), unheads(dv).astype(BF16), dzf], axis=-1)
    dh2 = matmul("l1_dh", dz1, w_in_o, tb=True, out_dtype=BF16)
    d_win_o = matmul("l1_dwin", h2, dz1, ta=True)

    dx1, df0, d_g03, d_g10 = tile_bwd("l0_dffnnorm", fn_addnorm2, m=m, tm=tm, nj=1, rows=[Row(x1), Row(f0)],
                                      pars=[Par(gain(0, 3)), Par(gain(1, 0))], cts=[Row(dx2), Row(dh2)],
                                      drows=[Out(D_MODEL, F32), Out(D_MODEL, BF16)])
    dh1, d_wup0, d_cw0, d_cb0, d_wdown0 = _ffn_backward("l0_ffn", df0, h1, hid0, act0, w_up[0], ffn_cw[0], ffn_cb[0:1], w_down[0], m, seq)
    dx0a, dmix0, d_g01, d_g02 = tile_bwd("l0_dpostnorm", fn_addnorm2, m=m, tm=tm, nj=1, rows=[Row(x0), Row(mix0)],
                                         pars=[Par(gain(0, 1)), Par(gain(0, 2))], cts=[Row(dx1), Row(dh1)],
                                         drows=[Out(D_MODEL, F32), Out(D_MODEL, BF16)])
    dmixcat0 = matmul("l0_dmixcat", dmix0, w_out_e, tb=True, out_dtype=BF16)
    d_wout_e = matmul("l0_dwout", mixcat0, dmix0, ta=True)
    dz_h, d_lb, d_hnorm = hgrn_bwd("l0_dhgrn", z0, sprev, hgrn_lb_logits, hgrn_norm, dmixcat0, n_batch=n_batch, seq=seq)
    dz_r, d_rcw, d_rcb, d_wa, d_ba, d_wx, d_bx, d_lam = tile_bwd(
        "l0_drglru", fn_rglru, m=m, tm=seq, nj=B_WIDTH // LANES, rows=[Row(z0, 2 * LANES, 2048 // (2 * LANES))], pars=rg_pars(),
        cts=[Row(dmixcat0, LANES, A_WIDTH // LANES)], drows=[Out(2 * B_WIDTH, BF16, 2 * LANES)])
    dz0 = jnp.concatenate([dz_h, dz_r], axis=-1)
    dh0 = matmul("l0_dh", dz0, w_in_e, tb=True, out_dtype=BF16)
    d_win_e = matmul("l0_dwin", h0, dz0, ta=True)
    dx0, d_g00 = tile_bwd("l0_dprenorm", fn_input_norm, m=m, tm=tm, nj=1, rows=[Row(x0)], pars=[Par(gain(0, 0))],
                          cts=[Row(dx0a), Row(dh0)], drows=[Out(D_MODEL, F32)])

    grads = {
        "norm_gains": jnp.stack([jnp.concatenate([d_g00, d_g01, d_g02, d_g03], axis=0), jnp.concatenate([d_g10, d_g11, d_g12, d_g13], axis=0)]),
        "even_w_in": even_in_to_external(d_win_e)[None],
        "hgrn_lb_logits": d_lb,
        "hgrn_norm": d_hnorm,
        "rg_conv_w": d_rcw[None],
        "rg_conv_b": d_rcb,
        "rg_wa": _block_diag_grad(d_wa)[None],
        "rg_ba": d_ba,
        "rg_wx": _block_diag_grad(d_wx)[None],
        "rg_bx": d_bx,
        "rg_lambda": d_lam,
        "even_w_out": d_wout_e[None],
        "odd_w_in": d_win_o[None, :, :3088],
        "fox_f_bias": d_fbias[:, :C_HEADS],
        "odd_w_out": d_wout_o[None],
        "ffn_w_up": ffn_to_external(jnp.stack([d_wup0, d_wup1])),
        "ffn_conv_w": ffn_to_external(jnp.stack([d_cw0, d_cw1])),
        "ffn_conv_b": ffn_to_external(jnp.concatenate([d_cb0, d_cb1], axis=0)),
        "ffn_w_down": jnp.stack([d_wdown0, d_wdown1]),
    }

    blocks = [(_rows_to_blocks(grads[n]) if n in ROW_SHARDED else _cols_to_blocks(grads[n])) for n in SHARDED]
    recv = all_to_all("exchange_grads", pack_last(blocks, lead=(N_DEV,)))
    res_s = adam_update("adam_sharded", recv, pack_last([w[n] for n in SHARDED]), pack_last([mom[n] for n in SHARDED]),
                        pack_last([var[n] for n in SHARDED]))
    res_s = [dict(zip(SHARDED, unpack_last(r, [w[n].shape for n in SHARDED]))) for r in res_s]

    rep_arrs = [grads[n] for n in REPLICATED] + [loss_part]
    gathered = all_gather("gather_partials", pack_last(rep_arrs))
    zero = jnp.zeros_like(loss_part)
    res_r = adam_update("adam_replicated", gathered, pack_last([w[n] for n in REPLICATED] + [zero]),
                        pack_last([mom[n] for n in REPLICATED] + [zero]), pack_last([var[n] for n in REPLICATED] + [zero]))
    shapes_r = [w[n].shape for n in REPLICATED] + [loss_part.shape]
    res_r = [dict(zip(REPLICATED + ("loss",), unpack_last(r, shapes_r))) for r in res_r]

    loss = res_r[0]["loss"][0, 0]
    out = [loss, dx0.reshape(x.shape)]
    for k in range(4):
        for n in NAMES:
            out.append(res_s[k][n] if n in SHARDED else res_r[k][n])
    return tuple(out)
```

```python
import functools

import jax
import jax.numpy as jnp
from jax import lax
from jax.experimental import pallas as pl
from jax.experimental.pallas import tpu as pltpu

F32 = jnp.float32
BF16 = jnp.bfloat16

D_MODEL = 1024
A_HEADS = 4
A_WIDTH = 512
HGRN_CHUNK = 64
HGRN_SEG = 512
B_WIDTH = 512
B_BLOCKS = 8
B_BLOCK_DIM = 64
B_CONV = 4
RG_C = 8.0
C_HEADS = 16
C_HEAD_DIM = 64
D_FF = 2816
FFN_CONV = 3
EPS = 1e-6
LANES = 128
HALO = 16
N_DEV = 8
FF_BLK = 2 * D_FF // N_DEV
MESH = pl.DeviceIdType.MESH
NEG = -1e30
VMEM_LIMIT = 56 * 1024 * 1024

ADAM_LR = 0.001
ADAM_B1 = 0.9
ADAM_B2 = 0.999
ADAM_EPS = 1e-08
ADAM_WD = 0.01
ADAM_STEP = 10


def _dg(a, b, pat):
    nb = a.ndim - 2
    batch = (tuple(range(nb)), tuple(range(nb)))
    ca = a.ndim - 1 if pat[0] == "n" else a.ndim - 2
    cb = b.ndim - 2 if pat[1] == "n" else b.ndim - 1
    return lax.dot_general(a.astype(BF16), b.astype(BF16), (((ca,), (cb,)), batch), preferred_element_type=F32)


@functools.partial(jax.custom_vjp, nondiff_argnums=(2,))
def bdot(a, b, pat):
    return _dg(a, b, pat)


def _bdot_fwd(a, b, pat):
    return _dg(a, b, pat), (a, b)


def _bdot_bwd(pat, res, g):
    a, b = res
    if pat == "nn":
        return _dg(g, b, "nt"), _dg(a, g, "tn")
    if pat == "nt":
        return _dg(g, b, "nn"), _dg(g, a, "tn")
    return _dg(b, g, "nt"), _dg(a, g, "nn")


bdot.defvjp(_bdot_fwd, _bdot_bwd)


def _shift_raw(x, s, up, fill):
    if s == 0:
        return x
    n = x.shape[0]
    r = pltpu.roll(x, (n - s) if up else s, 0)
    idx = lax.broadcasted_iota(jnp.int32, x.shape, 0)
    mask = (idx >= n - s) if up else (idx < s)
    return jnp.where(mask, jnp.asarray(fill, x.dtype), r)


@functools.partial(jax.custom_vjp, nondiff_argnums=(1,))
def shift_down(x, s):
    return _shift_raw(x, s, False, 0.0)


def _shift_down_fwd(x, s):
    return _shift_raw(x, s, False, 0.0), None


def _shift_down_bwd(s, _, g):
    return (_shift_raw(g, s, True, 0.0),)


shift_down.defvjp(_shift_down_fwd, _shift_down_bwd)


def _scan_impl(a, u, up):
    n = a.shape[0]
    s = 1
    while s < n:
        u = a * _shift_raw(u, s, up, 0.0) + u
        if 2 * s < n:
            a = a * _shift_raw(a, s, up, 1.0)
        s *= 2
    return u


@jax.custom_vjp
def lin_scan(a, u):
    return _scan_impl(a, u, False)


def _lin_scan_fwd(a, u):
    h = _scan_impl(a, u, False)
    return h, (a, h)


def _lin_scan_bwd(res, g):
    a, h = res
    gh = _scan_impl(_shift_raw(a, 1, True, 0.0), g, True)
    return gh * _shift_raw(h, 1, False, 0.0), gh


lin_scan.defvjp(_lin_scan_fwd, _lin_scan_bwd)


def _cumsum_impl(x, up, period):
    n = x.shape[0]
    span = n if period is None else period
    idx = lax.broadcasted_iota(jnp.int32, x.shape, 0)
    pos = idx if period is None else idx % period
    s = 1
    while s < span:
        sh = _shift_raw(x, s, up, 0.0)
        if period is not None:
            keep = (pos < period - s) if up else (pos >= s)
            sh = jnp.where(keep, sh, 0.0)
        x = x + sh
        s *= 2
    return x


@functools.partial(jax.custom_vjp, nondiff_argnums=(1,))
def cumsum_rows(x, period):
    return _cumsum_impl(x, False, period)


def _cumsum_fwd(x, period):
    return _cumsum_impl(x, False, period), None


def _cumsum_bwd(period, _, g):
    return (_cumsum_impl(g, True, period),)


cumsum_rows.defvjp(_cumsum_fwd, _cumsum_bwd)


def _sigmoid(x):
    return jax.nn.sigmoid(x)


def _expm1(x):
    return jnp.tanh(0.5 * x) * (jnp.exp(x) + 1.0)


def _softplus(x):
    return jnp.maximum(x, 0.0) + jnp.log(1.0 + jnp.exp(-jnp.abs(x)))


def _rms(x, g):
    return x * lax.rsqrt(jnp.mean(x * x, axis=-1, keepdims=True) + EPS) * g


def fn_prenorm(x, g):
    return (_rms(x, g).astype(BF16),)


def fn_addnorm2(x, y, g_post, g_pre):
    x1 = x + _rms(y, g_post)
    return x1, _rms(x1, g_pre).astype(BF16)


def fn_input_norm(x, g):
    return x, _rms(x, g).astype(BF16)


def fn_final(x, y, tgt, g_post):
    out = x + _rms(y, g_post)
    err = out - tgt
    dy = err * (1.0 / D_MODEL)
    loss = 0.5 * jnp.sum(jnp.mean(err * err, axis=-1, keepdims=True), axis=0, keepdims=True)
    return dy, jnp.broadcast_to(loss, (1, LANES))


def fn_rms_only(y, g):
    return (_rms(y, g),)


def _causal_conv(x, w, b, taps):
    c = b
    for k in range(taps):
        c = c + w[k:k + 1, :] * shift_down(x, taps - 1 - k)
    return c


def fn_rglru(xb, yb, cw, cb, wa, ba, wx, bx, lam):
    xf = _causal_conv(xb, cw, cb, B_CONV)
    r = _sigmoid(bdot(xf, wa, "nn") + ba)
    i = _sigmoid(bdot(xf, wx, "nn") + bx)
    log_a = -RG_C * r * _softplus(-lam)
    a = jnp.exp(log_a)
    u = jnp.sqrt(-_expm1(2.0 * log_a)) * (i * xf)
    h = lin_scan(a, u)
    return ((h * jax.nn.gelu(yb)).astype(BF16),)


def fn_fox_gate(zf, bias):
    return (cumsum_rows(jax.nn.log_sigmoid(zf + bias), None),)


def fn_hgrn_seg(q, fl, v, g, st, logits, hn):
    rows = q.shape[0]
    nc = rows // HGRN_CHUNK
    l0, l1, l2 = logits[0:1, :], logits[1:2, :], logits[2:3, :]
    mx = jnp.maximum(jnp.maximum(l0, l1), l2)
    e0, e1, e2 = jnp.exp(l0 - mx), jnp.exp(l1 - mx), jnp.exp(l2 - mx)
    lb = e0 / (e0 + e1 + e2)
    forget = lb + (1.0 - lb) * _sigmoid(fl)
    qs = q * _sigmoid(q)
    kk = 1.0 - forget
    logf = jnp.log(forget)
    bcum = cumsum_rows(logf, HGRN_CHUNK)
    c3 = lambda t: t.reshape(nc, HGRN_CHUNK, 128)
    b_last = jnp.sum(c3(logf), axis=1, keepdims=True)
    bcum3 = c3(bcum)
    q_dec = c3(qs) * jnp.exp(bcum3)
    k_dec = c3(kk) * jnp.exp(-bcum3)
    k_upd = c3(kk) * jnp.exp(b_last - bcum3)
    v3 = c3(v)
    scores = bdot(q_dec, k_dec, "nt")
    ri = lax.broadcasted_iota(jnp.int32, scores.shape, 1)
    ci = lax.broadcasted_iota(jnp.int32, scores.shape, 2)
    scores = jnp.where(ri >= ci, scores, 0.0)
    o = bdot(scores, v3, "nn")
    upd_t = bdot(v3, k_upd, "tn")
    dec = jnp.exp(b_last)
    prev = []
    for n in range(nc):
        prev.append(st)
        st = st * dec[n] + upd_t[n]
    o = o + bdot(q_dec, jnp.stack(prev), "nt")
    o = o.reshape(rows, 128)
    o = o * lax.rsqrt(jnp.mean(o * o, axis=-1, keepdims=True) + EPS) * hn
    return (o * _sigmoid(g)).astype(BF16), st


def _ffn_conv(xg, xv, cw, cb):
    cg = _causal_conv(xg, cw[0], cb[0], FFN_CONV)[HALO:]
    cv = _causal_conv(xv, cw[1], cb[1], FFN_CONV)[HALO:]
    return cg, cv


def _ffn_gate(cg, cv):
    return jax.nn.gelu(cg) * cv


class Row:
    def __init__(self, arr, cb=None, off=0):
        self.arr, self.cb, self.off = arr, cb, off

    def spec(self, tm):
        if self.cb is None:
            return pl.BlockSpec((tm, self.arr.shape[1]), lambda j, i: (i, 0))
        off = self.off
        return pl.BlockSpec((tm, self.cb), lambda j, i: (i, j + off))


class Par:
    def __init__(self, arr, kind="full", bs=None):
        self.arr, self.kind, self.bs = arr, kind, bs

    def block(self):
        if self.kind == "full":
            return self.arr.shape
        if self.kind == "col":
            return (self.arr.shape[0], self.bs)
        return (self.bs, self.arr.shape[1])

    def spec(self):
        if self.kind == "full":
            return pl.BlockSpec(self.block(), lambda j, i: (0, 0))
        if self.kind == "col":
            return pl.BlockSpec(self.block(), lambda j, i: (0, j))
        return pl.BlockSpec(self.block(), lambda j, i: (j, 0))


class Out:
    def __init__(self, width, dtype, cb=None, off=0):
        self.width, self.dtype, self.cb, self.off = width, dtype, cb, off

    def spec(self, tm):
        if self.cb is None:
            return pl.BlockSpec((tm, self.width), lambda j, i: (i, 0))
        off = self.off
        return pl.BlockSpec((tm, self.cb), lambda j, i: (i, j + off))


def _params(sem):
    return pltpu.CompilerParams(dimension_semantics=sem, vmem_limit_bytes=VMEM_LIMIT)


def tile_fwd(name, fn, *, m, tm, nj, rows, pars, outs, n_acc=0):
    n_r, n_p, n_o = len(rows), len(pars), len(outs)

    def body(*refs):
        ins = [r[...] for r in refs[:n_r + n_p]]
        res = fn(*ins)
        o_refs = refs[n_r + n_p:]
        for k in range(n_o):
            o_refs[k][...] = res[k].astype(o_refs[k].dtype)
        first = jnp.logical_and(pl.program_id(0) == 0, pl.program_id(1) == 0)
        for k in range(n_acc):
            ref = o_refs[n_o + k]

            @pl.when(first)
            def _():
                ref[...] = jnp.zeros_like(ref)

            ref[...] += res[n_o + k]

    out_shape = [jax.ShapeDtypeStruct((m, o.width), o.dtype) for o in outs]
    out_specs = [o.spec(tm) for o in outs]
    for _ in range(n_acc):
        out_shape.append(jax.ShapeDtypeStruct((1, LANES), F32))
        out_specs.append(pl.BlockSpec((1, LANES), lambda j, i: (0, 0)))
    sem = ("arbitrary", "arbitrary") if n_acc else ("parallel", "parallel")
    return pl.pallas_call(
        body, grid=(nj, m // tm), name=name,
        in_specs=[r.spec(tm) for r in rows] + [p.spec() for p in pars],
        out_specs=out_specs, out_shape=out_shape, compiler_params=_params(sem),
    )(*[r.arr for r in rows], *[p.arr for p in pars])


def tile_bwd(name, fn, *, m, tm, nj, rows, pars, cts, drows):
    n_r, n_p, n_c = len(rows), len(pars), len(cts)
    want = [k for k in range(n_r) if drows[k] is not None]

    def body(*refs):
        ins = [r[...] for r in refs[:n_r + n_p]]
        ct = [r[...] for r in refs[n_r + n_p:n_r + n_p + n_c]]
        o_refs = refs[n_r + n_p + n_c:]
        res, vjp = jax.vjp(fn, *ins)
        grads = vjp(tuple(c.astype(r.dtype) for c, r in zip(ct, res)))
        for pos, k in enumerate(want):
            o_refs[pos][...] = grads[k].astype(o_refs[pos].dtype)
        for k in range(n_p):
            ref = o_refs[len(want) + k]
            first = pl.program_id(1) == 0
            if pars[k].kind == "full":
                first = jnp.logical_and(first, pl.program_id(0) == 0)

            @pl.when(first)
            def _():
                ref[...] = jnp.zeros_like(ref)

            ref[...] += grads[n_r + k].astype(F32)

    out_shape = [jax.ShapeDtypeStruct((m, drows[k].width), drows[k].dtype) for k in want]
    out_specs = [drows[k].spec(tm) for k in want]
    for p in pars:
        out_shape.append(jax.ShapeDtypeStruct(p.arr.shape, F32))
        out_specs.append(p.spec())
    return pl.pallas_call(
        body, grid=(nj, m // tm), name=name,
        in_specs=[r.spec(tm) for r in rows] + [p.spec() for p in pars] + [c.spec(tm) for c in cts],
        out_specs=out_specs, out_shape=out_shape, compiler_params=_params(("arbitrary", "arbitrary")),
    )(*[r.arr for r in rows], *[p.arr for p in pars], *[c.arr for c in cts])


class Blk:
    def __init__(self, arr, block, index):
        self.arr, self.block, self.index = arr, block, index

    def spec(self):
        return pl.BlockSpec(self.block, self.index)


def _flat2(v):
    return v if v.ndim == 2 else v.reshape(-1, v.shape[-1])


def mm(name, pat, a, b, o, out_dtype, grid, into=None):
    nk = grid[2]
    o_shape = o.arr

    def body(*refs):
        a_ref, b_ref = refs[0], refs[1]
        o_ref = refs[3] if into is not None else refs[2]
        r = _dg(_flat2(a_ref[...]), _flat2(b_ref[...]), pat)
        if nk == 1:
            o_ref[...] = r.astype(out_dtype).reshape(o_ref.shape)
            return
        acc_ref = refs[-1]
        kk = pl.program_id(2)

        @pl.when(kk == 0)
        def _():
            acc_ref[...] = r

        @pl.when(kk > 0)
        def _():
            acc_ref[...] += r

        @pl.when(kk == nk - 1)
        def _():
            o_ref[...] = acc_ref[...].astype(out_dtype).reshape(o_ref.shape)

    ob = [d for d in o.block if d is not None]
    acc_shape = (ob[0], ob[1]) if len(ob) == 2 else (ob[0] * ob[1], ob[2])
    in_specs = [a.spec(), b.spec()]
    args = [a.arr, b.arr]
    aliases = {}
    if into is not None:
        in_specs.append(pl.BlockSpec(memory_space=pl.ANY))
        args.append(into)
        aliases = {2: 0}
    return pl.pallas_call(
        body, grid=grid, name=name, in_specs=in_specs, out_specs=o.spec(),
        out_shape=jax.ShapeDtypeStruct(o_shape, out_dtype),
        scratch_shapes=[pltpu.VMEM(acc_shape, F32)] if nk > 1 else [],
        input_output_aliases=aliases,
        compiler_params=_params(("parallel", "parallel", "arbitrary")),
    )(*args)


def _div_tile(n, cap):
    if n <= cap:
        return n
    best = 128
    for t in range(128, cap + 1, 128):
        if n % t == 0:
            best = t
    return best


def mm2d(name, pat, a, b, out_dtype=F32):
    if pat == "tn":
        k, m = a.shape
    else:
        m, k = a.shape
    n = b.shape[0] if pat == "nt" else b.shape[1]
    tm, tn, tk = _div_tile(m, 1024), _div_tile(n, 1024), _div_tile(k, 1024)
    a_blk = Blk(a, (tk, tm), lambda i, j, kk: (kk, i)) if pat == "tn" else Blk(a, (tm, tk), lambda i, j, kk: (i, kk))
    b_blk = Blk(b, (tn, tk), lambda i, j, kk: (j, kk)) if pat == "nt" else Blk(b, (tk, tn), lambda i, j, kk: (kk, j))
    o_blk = Blk((m, n), (tm, tn), lambda i, j, kk: (i, j))
    return mm(name, pat, a_blk, b_blk, o_blk, out_dtype, (m // tm, n // tn, k // tk))


def hgrn_fwd(name, z, logits, hnorm, *, n_batch, seq):
    m = n_batch * seq
    ts = min(HGRN_SEG, seq)
    n_seg = seq // ts

    def body(q_ref, f_ref, v_ref, g_ref, lg_ref, hn_ref, o_ref, sp_ref, st_ref):
        s = pl.program_id(2)

        @pl.when(s == 0)
        def _():
            st_ref[...] = jnp.zeros_like(st_ref)

        st = st_ref[...]
        sp_ref[...] = st
        o, st_new = fn_hgrn_seg(q_ref[...], f_ref[...], v_ref[...], g_ref[...], st, lg_ref[...], hn_ref[...])
        o_ref[...] = o
        st_ref[...] = st_new

    part = lambda p: pl.BlockSpec((ts, 128), lambda h, b, s: (b * n_seg + s, 4 * p + h))
    return pl.pallas_call(
        body, grid=(A_HEADS, n_batch, n_seg), name=name,
        in_specs=[part(0), part(1), part(2), part(3),
                  pl.BlockSpec((3, 128), lambda h, b, s: (0, h)),
                  pl.BlockSpec((1, 128), lambda h, b, s: (0, h))],
        out_specs=[pl.BlockSpec((ts, 128), lambda h, b, s: (b * n_seg + s, h)),
                   pl.BlockSpec((128, 128), lambda h, b, s: ((b * n_seg + s) * A_HEADS + h, 0))],
        out_shape=[jax.ShapeDtypeStruct((m, A_WIDTH), BF16),
                   jax.ShapeDtypeStruct((n_batch * n_seg * A_HEADS * 128, 128), F32)],
        scratch_shapes=[pltpu.VMEM((128, 128), F32)],
        compiler_params=_params(("arbitrary", "arbitrary", "arbitrary")),
    )(z, z, z, z, logits, hnorm)


def hgrn_bwd(name, z, sprev, logits, hnorm, do, *, n_batch, seq):
    m = n_batch * seq
    ts = min(HGRN_SEG, seq)
    n_seg = seq // ts

    def body(q_ref, f_ref, v_ref, g_ref, sp_ref, lg_ref, hn_ref, do_ref, dq_ref, df_ref, dv_ref, dg_ref, dlg_ref, dhn_ref, dst_ref):
        s = pl.program_id(2)

        @pl.when(s == 0)
        def _():
            dst_ref[...] = jnp.zeros_like(dst_ref)

        res, vjp = jax.vjp(fn_hgrn_seg, q_ref[...], f_ref[...], v_ref[...], g_ref[...], sp_ref[...], lg_ref[...], hn_ref[...])
        dq, df, dv, dg, dst, dlg, dhn = vjp((do_ref[...].astype(res[0].dtype), dst_ref[...]))
        dq_ref[...] = dq.astype(dq_ref.dtype)
        df_ref[...] = df.astype(df_ref.dtype)
        dv_ref[...] = dv.astype(dv_ref.dtype)
        dg_ref[...] = dg.astype(dg_ref.dtype)
        dst_ref[...] = dst
        first = jnp.logical_and(pl.program_id(1) == 0, s == 0)

        @pl.when(first)
        def _():
            dlg_ref[...] = jnp.zeros_like(dlg_ref)
            dhn_ref[...] = jnp.zeros_like(dhn_ref)

        dlg_ref[...] += dlg
        dhn_ref[...] += dhn

    rev = lambda b, s: b * n_seg + (n_seg - 1 - s)
    part = lambda p: pl.BlockSpec((ts, 128), lambda h, b, s: (rev(b, s), 4 * p + h))
    head = pl.BlockSpec((ts, 128), lambda h, b, s: (rev(b, s), h))
    dpart = jax.ShapeDtypeStruct((m, A_WIDTH), BF16)
    return pl.pallas_call(
        body, grid=(A_HEADS, n_batch, n_seg), name=name,
        in_specs=[part(0), part(1), part(2), part(3),
                  pl.BlockSpec((128, 128), lambda h, b, s: (rev(b, s) * A_HEADS + h, 0)),
                  pl.BlockSpec((3, 128), lambda h, b, s: (0, h)),
                  pl.BlockSpec((1, 128), lambda h, b, s: (0, h)),
                  head],
        out_specs=[head, head, head, head,
                   pl.BlockSpec((3, 128), lambda h, b, s: (0, h)),
                   pl.BlockSpec((1, 128), lambda h, b, s: (0, h))],
        out_shape=[dpart, dpart, dpart, dpart,
                   jax.ShapeDtypeStruct(logits.shape, F32),
                   jax.ShapeDtypeStruct(hnorm.shape, F32)],
        scratch_shapes=[pltpu.VMEM((128, 128), F32)],
        compiler_params=_params(("arbitrary", "arbitrary", "arbitrary")),
    )(z, z, z, z, sprev, logits, hnorm, do)


def _ffn_tiles(m, seq):
    tm = min(512, seq)
    return tm, seq // tm, m // tm


def ffn_mid_fwd(name, hid, cw, cb, layer, *, m, seq):
    tm, n_t, n_i = _ffn_tiles(m, seq)
    hb = tm // HALO

    def body(x_ref, xb_ref, cw_ref, cb_ref, o_ref):
        first = pl.program_id(1) % n_t == 0
        before = jnp.where(first, 0.0, xb_ref[...])
        ext = jnp.concatenate([before, x_ref[...]], axis=1)
        cg, cv = _ffn_conv(ext[0], ext[1], cw_ref[...], cb_ref[...])
        o_ref[...] = _ffn_gate(cg, cv).astype(o_ref.dtype)

    return pl.pallas_call(
        body, grid=(N_DEV // 2, n_i), name=name,
        in_specs=[pl.BlockSpec((2, None, tm, FF_BLK), lambda d, i: (0, d, i, 0)),
                  pl.BlockSpec((2, None, HALO, FF_BLK), lambda d, i: (0, d, jnp.maximum(i * hb - 1, 0), 0)),
                  pl.BlockSpec((2, None, None, FFN_CONV, FF_BLK), lambda d, i: (0, d, layer, 0, 0)),
                  pl.BlockSpec((None, 2, None, 1, FF_BLK), lambda d, i: (layer, 0, d, 0, 0))],
        out_specs=pl.BlockSpec((None, tm, FF_BLK), lambda d, i: (d, i, 0)),
        out_shape=jax.ShapeDtypeStruct((N_DEV // 2, m, FF_BLK), BF16),
        compiler_params=_params(("parallel", "parallel")),
    )(hid, hid, cw, cb)


def ffn_mid_bwd(name, hid, cw, cb, dact, layer, *, m, seq):
    tm, n_t, n_i = _ffn_tiles(m, seq)
    hb = tm // HALO
    last_blk = m // HALO - 1

    def body(x_ref, xb_ref, xa_ref, cw_ref, cb_ref, da_ref, daa_ref, dx_ref, dcw_ref, dcb_ref):
        i = pl.program_id(1)
        first = i % n_t == 0
        last = i % n_t == n_t - 1
        before = jnp.where(first, 0.0, xb_ref[...])
        ext = jnp.concatenate([before, x_ref[...], xa_ref[...]], axis=1)
        dact_ext = jnp.concatenate([da_ref[...].astype(F32), jnp.where(last, 0.0, daa_ref[...].astype(F32))], axis=0)
        (cg, cv), vjp_conv = jax.vjp(_ffn_conv, ext[0], ext[1], cw_ref[...], cb_ref[...])
        _, vjp_gate = jax.vjp(_ffn_gate, cg, cv)
        dcg, dcv = vjp_gate(dact_ext)
        dxg, dxv, _, _ = vjp_conv((dcg, dcv))
        dx_ref[0] = dxg[HALO:HALO + tm].astype(dx_ref.dtype)
        dx_ref[1] = dxv[HALO:HALO + tm].astype(dx_ref.dtype)
        own = lax.broadcasted_iota(jnp.int32, dcg.shape, 0) < tm
        _, _, dcw, dcb = vjp_conv((jnp.where(own, dcg, 0.0), jnp.where(own, dcv, 0.0)))

        @pl.when(i == 0)
        def _():
            dcw_ref[...] = jnp.zeros_like(dcw_ref)
            dcb_ref[...] = jnp.zeros_like(dcb_ref)

        dcw_ref[...] += dcw
        dcb_ref[...] += dcb

    return pl.pallas_call(
        body, grid=(N_DEV // 2, n_i), name=name,
        in_specs=[pl.BlockSpec((2, None, tm, FF_BLK), lambda d, i: (0, d, i, 0)),
                  pl.BlockSpec((2, None, HALO, FF_BLK), lambda d, i: (0, d, jnp.maximum(i * hb - 1, 0), 0)),
                  pl.BlockSpec((2, None, HALO, FF_BLK), lambda d, i: (0, d, jnp.minimum((i + 1) * hb, last_blk), 0)),
                  pl.BlockSpec((2, None, None, FFN_CONV, FF_BLK), lambda d, i: (0, d, layer, 0, 0)),
                  pl.BlockSpec((None, 2, None, 1, FF_BLK), lambda d, i: (layer, 0, d, 0, 0)),
                  pl.BlockSpec((None, tm, FF_BLK), lambda d, i: (d, i, 0)),
                  pl.BlockSpec((None, HALO, FF_BLK), lambda d, i: (d, jnp.minimum((i + 1) * hb, last_blk), 0))],
        out_specs=[pl.BlockSpec((2, None, tm, FF_BLK), lambda d, i: (0, d, i, 0)),
                   pl.BlockSpec((2, None, FFN_CONV, FF_BLK), lambda d, i: (0, d, 0, 0)),
                   pl.BlockSpec((2, None, 1, FF_BLK), lambda d, i: (0, d, 0, 0))],
        out_shape=[jax.ShapeDtypeStruct((2, N_DEV // 2, m, FF_BLK), BF16),
                   jax.ShapeDtypeStruct((2, N_DEV // 2, FFN_CONV, FF_BLK), F32),
                   jax.ShapeDtypeStruct((2, N_DEV // 2, 1, FF_BLK), F32)],
        compiler_params=_params(("arbitrary", "arbitrary")),
    )(hid, hid, hid, cw, cb, dact, dact)


ATT_BLK = 256


def fox_fwd(name, q, k, v, ccol, crow):
    bh, seq, dh = q.shape
    blk = min(ATT_BLK, seq)
    nq = seq // blk
    scale = dh ** -0.5

    def body(q_ref, k_ref, v_ref, cc_ref, cr_ref, o_ref, lse_ref):
        qi = pl.program_id(1)
        qv = q_ref[0]
        cq = cc_ref[0]
        row = qi * blk + lax.broadcasted_iota(jnp.int32, (blk, blk), 0)

        def step(j, carry):
            mx, l, acc = carry
            st = pl.multiple_of(j * blk, blk)
            kj = k_ref[0, pl.ds(st, blk), :]
            vj = v_ref[0, pl.ds(st, blk), :]
            s = _dg(qv, kj, "nt") * scale + cq - cr_ref[0, j]
            col = j * blk + lax.broadcasted_iota(jnp.int32, (blk, blk), 1)
            s = jnp.where(row >= col, s, NEG)
            mx_new = jnp.maximum(mx, jnp.max(s, axis=-1, keepdims=True))
            p = jnp.exp(s - mx_new)
            alpha = jnp.exp(mx - mx_new)
            l = alpha * l + jnp.sum(p, axis=-1, keepdims=True)
            acc = alpha * acc + _dg(p, vj, "nn")
            return mx_new, l, acc

        init = (jnp.full((blk, 1), NEG, F32), jnp.zeros((blk, 1), F32), jnp.zeros((blk, dh), F32))
        mx, l, acc = lax.fori_loop(0, qi + 1, step, init)
        o_ref[0] = (acc / l).astype(o_ref.dtype)
        lse_ref[0] = mx + jnp.log(l)

    return pl.pallas_call(
        body, grid=(bh, nq), name=name,
        in_specs=[pl.BlockSpec((1, blk, dh), lambda b, i: (b, i, 0)),
                  pl.BlockSpec((1, seq, dh), lambda b, i: (b, 0, 0)),
                  pl.BlockSpec((1, seq, dh), lambda b, i: (b, 0, 0)),
                  pl.BlockSpec((1, blk, 1), lambda b, i: (b, i, 0)),
                  pl.BlockSpec((1, nq, 1, blk), lambda b, i: (b, 0, 0, 0))],
        out_specs=[pl.BlockSpec((1, blk, dh), lambda b, i: (b, i, 0)),
                   pl.BlockSpec((1, blk, 1), lambda b, i: (b, i, 0))],
        out_shape=[jax.ShapeDtypeStruct((bh, seq, dh), BF16), jax.ShapeDtypeStruct((bh, seq, 1), F32)],
        compiler_params=_params(("parallel", "arbitrary")),
    )(q, k, v, ccol, crow)


def fox_bwd(name, q, k, v, o, do, lse, ccol, crow):
    bh, seq, dh = q.shape
    blk = min(ATT_BLK, seq)
    nq = seq // blk
    scale = dh ** -0.5

    def body(q_ref, k_ref, v_ref, o_ref, do_ref, lse_ref, cc_ref, cr_ref, dq_ref, dk_ref, dv_ref, drow_ref, dcol_ref, del_ref):
        j = pl.program_id(1)

        @pl.when(j == 0)
        def _():
            dq_ref[...] = jnp.zeros_like(dq_ref)
            drow_ref[...] = jnp.zeros_like(drow_ref)
            del_ref[...] = jnp.sum(do_ref[0].astype(F32) * o_ref[0].astype(F32), axis=-1, keepdims=True)

        kj, vj = k_ref[0], v_ref[0]
        ck = cr_ref[0, 0]
        col = j * blk + lax.broadcasted_iota(jnp.int32, (blk, blk), 1)

        def step(i, carry):
            dk, dv, dcol = carry
            st = pl.multiple_of(i * blk, blk)
            rows = pl.ds(st, blk)
            qv, dov = q_ref[0, rows, :], do_ref[0, rows, :]
            s = _dg(qv, kj, "nt") * scale + cc_ref[0, rows, :] - ck
            row = i * blk + lax.broadcasted_iota(jnp.int32, (blk, blk), 0)
            p = jnp.where(row >= col, jnp.exp(s - lse_ref[0, rows, :]), 0.0)
            dv = dv + _dg(p, dov, "tn")
            ds = p * (_dg(dov, vj, "nt") - del_ref[rows, :])
            dk = dk + _dg(ds, qv, "tn") * scale
            dq_ref[0, rows, :] += _dg(ds, kj, "nn") * scale
            drow_ref[0, rows, :] += jnp.sum(ds, axis=-1, keepdims=True)
            return dk, dv, dcol + jnp.sum(ds, axis=0, keepdims=True)

        init = (jnp.zeros((blk, dh), F32), jnp.zeros((blk, dh), F32), jnp.zeros((1, blk), F32))
        dk, dv, dcol = lax.fori_loop(j, nq, step, init)
        dk_ref[0] = dk
        dv_ref[0] = dv
        dcol_ref[0, 0] = dcol

    full = pl.BlockSpec((1, seq, dh), lambda b, j: (b, 0, 0))
    blkd = pl.BlockSpec((1, blk, dh), lambda b, j: (b, j, 0))
    col1 = pl.BlockSpec((1, seq, 1), lambda b, j: (b, 0, 0))
    rowb = pl.BlockSpec((1, 1, 1, blk), lambda b, j: (b, j, 0, 0))
    return pl.pallas_call(
        body, grid=(bh, nq), name=name,
        in_specs=[full, blkd, blkd, full, full, col1, col1, rowb],
        out_specs=[full, blkd, blkd, col1, rowb],
        out_shape=[jax.ShapeDtypeStruct((bh, seq, dh), F32), jax.ShapeDtypeStruct((bh, seq, dh), F32),
                   jax.ShapeDtypeStruct((bh, seq, dh), F32), jax.ShapeDtypeStruct((bh, seq, 1), F32),
                   jax.ShapeDtypeStruct((bh, nq, 1, blk), F32)],
        scratch_shapes=[pltpu.VMEM((seq, 1), F32)],
        compiler_params=_params(("parallel", "arbitrary")),
    )(q, k, v, o, do, lse, ccol, crow)


def _mesh_pos():
    return lax.axis_index("x"), lax.axis_index("y"), lax.axis_index("c")


def _flip(v, bit):
    return 1 - v if bit else v


def all_gather(name, blocks):
    n = len(blocks)

    def body(*refs):
        x_refs, out_refs = refs[:n], refs[n:2 * n]
        send_sems, recv_sems, local_sems = refs[2 * n:]
        x, y, c = _mesh_pos()
        me, sibling = (x, y, c), (x, y, 1 - c)
        chips = [(1 - x, y), (x, 1 - y), (1 - x, 1 - y)]

        def slot(a, px, py, pc):
            return out_refs[a].at[4 * px + 2 * py + pc]

        def copy(a, k, blk, to, src=None):
            return pltpu.make_async_remote_copy(
                src_ref=slot(a, *blk) if src is None else src, dst_ref=slot(a, *blk),
                send_sem=send_sems.at[a, k], recv_sem=recv_sems.at[a, k], device_id=to, device_id_type=MESH)

        mine = [pltpu.make_async_copy(x_refs[a], slot(a, *me), local_sems.at[a]) for a in range(n)]
        for cp in mine:
            cp.start()
        sends = []
        for a in range(n):
            sends.append(copy(a, 0, me, sibling, src=x_refs[a]))
            sends += [copy(a, 1 + j, me, (*chip, c), src=x_refs[a]) for j, chip in enumerate(chips)]
        for cp in sends:
            cp.start()
        for j, chip in enumerate(chips):
            for a in range(n):
                copy(a, 1 + j, (*chip, c), me).wait_recv()
                passed = copy(a, 4 + j, (*chip, c), sibling)
                passed.start()
                sends.append(passed)
        for a in range(n):
            copy(a, 0, sibling, me).wait_recv()
            for j, chip in enumerate(chips):
                copy(a, 4 + j, (*chip, 1 - c), me).wait_recv()
        for cp in sends:
            cp.wait_send()
        for cp in mine:
            cp.wait()

    hbm = pl.BlockSpec(memory_space=pl.ANY)
    return pl.pallas_call(
        body, name=name, out_shape=[jax.ShapeDtypeStruct((N_DEV,) + b.shape, b.dtype) for b in blocks],
        in_specs=[hbm] * n, out_specs=[hbm] * n,
        scratch_shapes=[pltpu.SemaphoreType.DMA((n, 7)), pltpu.SemaphoreType.DMA((n, 7)), pltpu.SemaphoreType.DMA((n,))],
    )(*blocks)


def all_to_all(name, sends):
    n = len(sends)

    def body(*refs):
        s_refs, r_refs = refs[:n], refs[n:2 * n]
        send_sems, recv_sems, local_sems = refs[2 * n:]
        x, y, c = _mesh_pos()
        me = 4 * x + 2 * y + c
        mine = [pltpu.make_async_copy(s_refs[a].at[me], r_refs[a].at[me], local_sems.at[a]) for a in range(n)]
        for cp in mine:
            cp.start()
        copies = []
        for k in range(1, N_DEV):
            px, py, pc = _flip(x, k & 4), _flip(y, k & 2), _flip(c, k & 1)
            for a in range(n):
                copies.append(pltpu.make_async_remote_copy(
                    src_ref=s_refs[a].at[4 * px + 2 * py + pc], dst_ref=r_refs[a].at[me],
                    send_sem=send_sems.at[a, k - 1], recv_sem=recv_sems.at[a, k - 1],
                    device_id=(px, py, pc), device_id_type=MESH))
        for cp in copies:
            cp.start()
        for cp in copies:
            cp.wait_recv()
        for cp in copies:
            cp.wait_send()
        for cp in mine:
            cp.wait()

    hbm = pl.BlockSpec(memory_space=pl.ANY)
    return pl.pallas_call(
        body, name=name, out_shape=[jax.ShapeDtypeStruct(s.shape, s.dtype) for s in sends],
        in_specs=[hbm] * n, out_specs=[hbm] * n,
        scratch_shapes=[pltpu.SemaphoreType.DMA((n, 7)), pltpu.SemaphoreType.DMA((n, 7)), pltpu.SemaphoreType.DMA((n,))],
    )(*sends)


def _sum8(p):
    return ((p[0] + p[1]) + (p[2] + p[3])) + ((p[4] + p[5]) + (p[6] + p[7]))


def _adam(g, w, m, v):
    m = ADAM_B1 * m + (1.0 - ADAM_B1) * g
    v = ADAM_B2 * v + (1.0 - ADAM_B2) * (g * g)
    m_hat = m / (1.0 - ADAM_B1 ** ADAM_STEP)
    v_hat = v / (1.0 - ADAM_B2 ** ADAM_STEP)
    return -ADAM_LR * (m_hat / (jnp.sqrt(v_hat) + ADAM_EPS) + ADAM_WD * w), m, v


def adam_tiled(name, partials, w, m_, v_):
    shape = w.shape
    c = shape[-1]
    r = 1
    for d in shape[:-1]:
        r *= d
    tr = next((t for t in range(256, 7, -8) if r % t == 0), r)

    def body(p_ref, w_ref, m_ref, v_ref, g_ref, d_ref, nm_ref, nv_ref):
        g = _sum8(p_ref)
        g_ref[...] = g
        d_ref[...], nm_ref[...], nv_ref[...] = _adam(g, w_ref[...], m_ref[...], v_ref[...])

    spec = pl.BlockSpec((tr, c), lambda i: (i, 0))
    res = pl.pallas_call(
        body, grid=(r // tr,), name=name,
        in_specs=[pl.BlockSpec((N_DEV, tr, c), lambda i: (0, i, 0)), spec, spec, spec],
        out_specs=[spec] * 4, out_shape=[jax.ShapeDtypeStruct((r, c), F32)] * 4,
        compiler_params=_params(("parallel",)),
    )(partials.reshape(N_DEV, r, c), w.reshape(r, c), m_.reshape(r, c), v_.reshape(r, c))
    return [t.reshape(shape) for t in res]


def adam_small(name, items, extra):
    n, ne = len(items), len(extra)

    def body(*refs):
        ins, outs = refs[:4 * n + ne], refs[4 * n + ne:]
        for a in range(n):
            p_ref, w_ref, m_ref, v_ref = ins[4 * a:4 * a + 4]
            g = _sum8(p_ref)
            outs[4 * a][...] = g
            outs[4 * a + 1][...], outs[4 * a + 2][...], outs[4 * a + 3][...] = _adam(g, w_ref[...], m_ref[...], v_ref[...])
        for e in range(ne):
            outs[4 * n + e][...] = _sum8(ins[4 * n + e])

    args, out_shape = [], []
    for p, w, m_, v_ in items:
        args += [p, w, m_, v_]
        out_shape += [jax.ShapeDtypeStruct(w.shape, F32)] * 4
    for e in extra:
        args.append(e)
        out_shape.append(jax.ShapeDtypeStruct(e.shape[1:], F32))
    vmem = pl.BlockSpec(memory_space=pltpu.VMEM)
    res = pl.pallas_call(body, name=name, in_specs=[vmem] * len(args), out_specs=[vmem] * len(out_shape), out_shape=out_shape)(*args)
    return [res[4 * a:4 * a + 4] for a in range(n)], res[4 * n:]


def _cols_from_gather(g):
    g = jnp.moveaxis(g, 0, -2)
    return g.reshape(g.shape[:-2] + (g.shape[-2] * g.shape[-1],))


def _cols_to_blocks(w):
    w = w.reshape(w.shape[:-1] + (N_DEV, w.shape[-1] // N_DEV))
    return jnp.moveaxis(w, -2, 0)


def _block_diag(w):
    z = jnp.zeros((B_BLOCK_DIM, B_BLOCK_DIM), w.dtype)
    rows = []
    for j in range(B_BLOCKS // 2):
        top = jnp.concatenate([w[2 * j], z], axis=1)
        bot = jnp.concatenate([z, w[2 * j + 1]], axis=1)
        rows.append(jnp.concatenate([top, bot], axis=0))
    return jnp.concatenate(rows, axis=0)


def _block_diag_grad(d):
    out = []
    for j in range(B_BLOCKS // 2):
        blk = d[128 * j:128 * (j + 1)]
        out.append(blk[:64, :64])
        out.append(blk[64:, 64:])
    return jnp.stack(out)


NAMES = ("norm_gains", "even_w_in", "hgrn_lb_logits", "hgrn_norm", "rg_conv_w", "rg_conv_b", "rg_wa", "rg_ba", "rg_wx", "rg_bx",
         "rg_lambda", "even_w_out", "odd_w_in", "fox_f_bias", "odd_w_out", "ffn_w_up", "ffn_conv_w", "ffn_conv_b", "ffn_w_down")
BIG = ("even_w_in", "even_w_out", "odd_w_in", "odd_w_out", "ffn_w_up", "ffn_w_down")
SMALL_SHARDED = ("norm_gains", "rg_conv_w", "ffn_conv_w")
REPLICATED = ("hgrn_lb_logits", "hgrn_norm", "rg_conv_b", "rg_wa", "rg_ba", "rg_wx", "rg_bx", "rg_lambda", "fox_f_bias", "ffn_conv_b")


def _ffn_forward(tag, layer, h, w_up_g, cw5, cb5, w_down_g, m, seq):
    tm = _div_tile(m, 1024)
    nm = m // tm
    hid = mm(f"{tag}_up", "nn",
             Blk(h, (tm, D_MODEL), lambda i, j, k: (i, 0)),
             Blk(w_up_g, (None, None, D_MODEL, FF_BLK), lambda i, j, k: (j, layer, 0, 0)),
             Blk((N_DEV, m, FF_BLK), (None, tm, FF_BLK), lambda i, j, k: (j, i, 0)), F32, (nm, N_DEV, 1))
    hid = hid.reshape(2, N_DEV // 2, m, FF_BLK)
    act = ffn_mid_fwd(f"{tag}_mid", hid, cw5, cb5, layer, m=m, seq=seq)
    f = mm(f"{tag}_down", "nn",
           Blk(act, (None, tm, FF_BLK), lambda i, j, k: (k, i, 0)),
           Blk(w_down_g, (2, None, FF_BLK // 2, D_MODEL), lambda i, j, k: (k, layer, 0, 0)),
           Blk((m, D_MODEL), (tm, D_MODEL), lambda i, j, k: (i, 0)), F32, (nm, 1, N_DEV // 2))
    return hid, act, f


def _ffn_backward(tag, layer, df, h, hid, act, w_up_g, cw5, cb5, w_down_g, d_wup, d_wdown, m, seq):
    tm = _div_tile(m, 1024)
    nm = m // tm
    dact = mm(f"{tag}_dact", "nt",
              Blk(df, (tm, D_MODEL), lambda i, j, k: (i, 0)),
              Blk(w_down_g, (2, None, FF_BLK // 2, D_MODEL), lambda i, j, k: (j, layer, 0, 0)),
              Blk((N_DEV // 2, m, FF_BLK), (None, tm, FF_BLK), lambda i, j, k: (j, i, 0)), BF16, (nm, N_DEV // 2, 1))
    d_wdown = mm(f"{tag}_dwdown", "tn",
                 Blk(act, (None, tm, FF_BLK), lambda i, j, k: (i, k, 0)),
                 Blk(df, (tm, D_MODEL), lambda i, j, k: (k, 0)),
                 Blk(w_down_g.shape, (2, None, FF_BLK // 2, D_MODEL), lambda i, j, k: (i, layer, 0, 0)), F32,
                 (N_DEV // 2, 1, nm), into=d_wdown)
    dhid, d_cw, d_cb = ffn_mid_bwd(f"{tag}_dmid", hid, cw5, cb5, dact, layer, m=m, seq=seq)
    dhid = dhid.reshape(N_DEV, m, FF_BLK)
    dh = mm(f"{tag}_dh", "nt",
            Blk(dhid, (None, tm, FF_BLK), lambda i, j, k: (k, i, 0)),
            Blk(w_up_g, (None, None, D_MODEL, FF_BLK), lambda i, j, k: (k, layer, 0, 0)),
            Blk((m, D_MODEL), (tm, D_MODEL), lambda i, j, k: (i, 0)), BF16, (nm, 1, N_DEV))
    d_wup = mm(f"{tag}_dwup", "tn",
               Blk(h, (tm, D_MODEL), lambda i, j, k: (k, 0)),
               Blk(dhid, (None, tm, FF_BLK), lambda i, j, k: (j, k, 0)),
               Blk(w_up_g.shape, (None, None, D_MODEL, FF_BLK), lambda i, j, k: (j, layer, 0, 0)), F32,
               (1, N_DEV, nm), into=d_wup)
    return dh, d_wup, d_cw, d_cb, d_wdown


def kernel(x, norm_gains, even_w_in, hgrn_lb_logits, hgrn_norm, rg_conv_w, rg_conv_b, rg_wa, rg_ba, rg_wx, rg_bx, rg_lambda, even_w_out, odd_w_in, fox_f_bias, odd_w_out, ffn_w_up, ffn_conv_w, ffn_conv_b, ffn_w_down, loss_target, m_norm_gains, m_even_w_in, m_hgrn_lb_logits, m_hgrn_norm, m_rg_conv_w, m_rg_conv_b, m_rg_wa, m_rg_ba, m_rg_wx, m_rg_bx, m_rg_lambda, m_even_w_out, m_odd_w_in, m_fox_f_bias, m_odd_w_out, m_ffn_w_up, m_ffn_conv_w, m_ffn_conv_b, m_ffn_w_down, v_norm_gains, v_even_w_in, v_hgrn_lb_logits, v_hgrn_norm, v_rg_conv_w, v_rg_conv_b, v_rg_wa, v_rg_ba, v_rg_wx, v_rg_bx, v_rg_lambda, v_even_w_out, v_odd_w_in, v_fox_f_bias, v_odd_w_out, v_ffn_w_up, v_ffn_conv_w, v_ffn_conv_b, v_ffn_w_down):
    local = dict(locals())
    w = {n: local[n] for n in NAMES}
    mom = {n: local["m_" + n] for n in NAMES}
    var = {n: local["v_" + n] for n in NAMES}
    n_batch, seq, _ = x.shape
    m = n_batch * seq
    tm = _div_tile(m, 512)
    tmm = _div_tile(m, 1024)
    nm = m // tmm

    gathered = all_gather("gather_weights", [w[n].astype(BF16) for n in BIG] + [w[n] for n in SMALL_SHARDED])
    g = dict(zip(BIG + SMALL_SHARDED, gathered))
    w_in_e = g["even_w_in"]
    w_out_e = g["even_w_out"].reshape(D_MODEL, D_MODEL)
    w_in_o = jnp.pad(_cols_from_gather(g["odd_w_in"])[0], ((0, 0), (0, 3200 - 3088)))
    w_out_o = g["odd_w_out"].reshape(D_MODEL, D_MODEL)
    w_up_g, w_down_g = g["ffn_w_up"], g["ffn_w_down"]
    gains = _cols_from_gather(g["norm_gains"])
    rg_cw = _cols_from_gather(g["rg_conv_w"])[0]
    n_layer = ffn_conv_w.shape[0]
    cw5 = g["ffn_conv_w"].reshape(2, N_DEV // 2, n_layer, FFN_CONV, FF_BLK)
    cb5 = ffn_conv_b.reshape(n_layer, 2, N_DEV // 2, 1, FF_BLK)
    gain = lambda l, k: gains[l, k:k + 1, :]
    wa_bd, wx_bd = _block_diag(rg_wa[0]), _block_diag(rg_wx[0])
    fbias = jnp.pad(fox_f_bias, ((0, 0), (0, LANES - C_HEADS)))

    x0 = x.reshape(m, D_MODEL)
    tgt = loss_target.reshape(m, D_MODEL)

    (h0,) = tile_fwd("l0_prenorm", fn_prenorm, m=m, tm=tm, nj=1, rows=[Row(x0)], pars=[Par(gain(0, 0))], outs=[Out(D_MODEL, BF16)])
    z0 = mm("l0_in", "nn",
            Blk(h0, (tmm, D_MODEL), lambda i, j, k: (i, 0)),
            Blk(w_in_e, (None, None, D_MODEL, 384), lambda i, j, k: (j, 0, 0, 0)),
            Blk((m, 3072), (tmm, 384), lambda i, j, k: (i, j)), F32, (nm, N_DEV, 1))
    oa, sprev = hgrn_fwd("l0_hgrn", z0, hgrn_lb_logits, hgrn_norm, n_batch=n_batch, seq=seq)
    rg_rows = lambda: [Row(z0, LANES, 16), Row(z0, LANES, 20)]
    rg_pars = lambda: [Par(rg_cw, "col", LANES), Par(rg_conv_b, "col", LANES), Par(wa_bd, "row", LANES), Par(rg_ba, "col", LANES),
                       Par(wx_bd, "row", LANES), Par(rg_bx, "col", LANES), Par(rg_lambda, "col", LANES)]
    (ob,) = tile_fwd("l0_rglru", fn_rglru, m=m, tm=seq, nj=B_WIDTH // LANES, rows=rg_rows(), pars=rg_pars(),
                     outs=[Out(B_WIDTH, BF16, LANES)])
    mixcat0 = jnp.concatenate([oa, ob], axis=-1)
    mix0 = mm2d("l0_out", "nn", mixcat0, w_out_e)
    x1, h1 = tile_fwd("l0_postnorm", fn_addnorm2, m=m, tm=tm, nj=1, rows=[Row(x0), Row(mix0)], pars=[Par(gain(0, 1)), Par(gain(0, 2))],
                      outs=[Out(D_MODEL, F32), Out(D_MODEL, BF16)])
    hid0, act0, f0 = _ffn_forward("l0_ffn", 0, h1, w_up_g, cw5, cb5, w_down_g, m, seq)
    x2, h2 = tile_fwd("l0_ffnnorm", fn_addnorm2, m=m, tm=tm, nj=1, rows=[Row(x1), Row(f0)], pars=[Par(gain(0, 3)), Par(gain(1, 0))],
                      outs=[Out(D_MODEL, F32), Out(D_MODEL, BF16)])

    z1 = mm2d("l1_in", "nn", h2, w_in_o)
    (cgate,) = tile_fwd("l1_gate", fn_fox_gate, m=m, tm=seq, nj=1, rows=[Row(z1, LANES, 3072 // LANES)], pars=[Par(fbias)],
                        outs=[Out(LANES, F32)])
    bh = n_batch * C_HEADS
    nqb = seq // min(ATT_BLK, seq)
    heads = lambda t: t.reshape(n_batch, seq, C_HEADS, C_HEAD_DIM).transpose(0, 2, 1, 3).reshape(bh, seq, C_HEAD_DIM)
    unheads = lambda t: t.reshape(n_batch, C_HEADS, seq, C_HEAD_DIM).transpose(0, 2, 1, 3).reshape(m, D_MODEL)
    qh, kh, vh = (heads(z1[:, i * D_MODEL:(i + 1) * D_MODEL].astype(BF16)) for i in range(3))
    c_bht = cgate[:, :C_HEADS].reshape(n_batch, seq, C_HEADS).transpose(0, 2, 1).reshape(bh, seq)
    ccol, crow = c_bht.reshape(bh, seq, 1), c_bht.reshape(bh, nqb, 1, seq // nqb)
    oh, lse = fox_fwd("l1_attn", qh, kh, vh, ccol, crow)
    oc = unheads(oh)
    mix1 = mm2d("l1_out", "nn", oc, w_out_o)
    x3, h3 = tile_fwd("l1_postnorm", fn_addnorm2, m=m, tm=tm, nj=1, rows=[Row(x2), Row(mix1)], pars=[Par(gain(1, 1)), Par(gain(1, 2))],
                      outs=[Out(D_MODEL, F32), Out(D_MODEL, BF16)])
    hid1, act1, f1 = _ffn_forward("l1_ffn", 1, h3, w_up_g, cw5, cb5, w_down_g, m, seq)
    dy, loss_part = tile_fwd("loss", fn_final, m=m, tm=tm, nj=1, rows=[Row(x3), Row(f1), Row(tgt)], pars=[Par(gain(1, 3))],
                             outs=[Out(D_MODEL, F32)], n_acc=1)

    df1, d_g13 = tile_bwd("l1_dffnnorm", fn_rms_only, m=m, tm=tm, nj=1, rows=[Row(f1)], pars=[Par(gain(1, 3))], cts=[Row(dy)],
                          drows=[Out(D_MODEL, BF16)])
    dh3, d_wup, d_cw1, d_cb1, d_wdown = _ffn_backward("l1_ffn", 1, df1, h3, hid1, act1, w_up_g, cw5, cb5, w_down_g, None, None, m, seq)
    dx2, dmix1, d_g11, d_g12 = tile_bwd("l1_dpostnorm", fn_addnorm2, m=m, tm=tm, nj=1, rows=[Row(x2), Row(mix1)],
                                        pars=[Par(gain(1, 1)), Par(gain(1, 2))], cts=[Row(dy), Row(dh3)],
                                        drows=[Out(D_MODEL, F32), Out(D_MODEL, BF16)])
    doc = mm2d("l1_doc", "nt", dmix1, w_out_o, BF16)
    d_wout_o = mm2d("l1_dwout", "tn", oc, dmix1)
    dq, dk, dv, drow, dcol = fox_bwd("l1_dattn", qh, kh, vh, oh, heads(doc), lse, ccol, crow)
    dc = (drow.reshape(bh, seq) - dcol.reshape(bh, seq)).reshape(n_batch, C_HEADS, seq).transpose(0, 2, 1).reshape(m, C_HEADS)
    dc = jnp.pad(dc, ((0, 0), (0, LANES - C_HEADS)))
    dzf, d_fbias = tile_bwd("l1_dgate", fn_fox_gate, m=m, tm=seq, nj=1, rows=[Row(z1, LANES, 3072 // LANES)], pars=[Par(fbias)],
                            cts=[Row(dc)], drows=[Out(LANES, BF16)])
    dz1 = jnp.concatenate([unheads(dq).astype(BF16), unheads(dk).astype(BF16), unheads(dv).astype(BF16), dzf], axis=-1)
    dh2 = mm2d("l1_dh", "nt", dz1, w_in_o, BF16)
    d_win_o = mm2d("l1_dwin", "tn", h2, dz1)

    dx1, df0, d_g03, d_g10 = tile_bwd("l0_dffnnorm", fn_addnorm2, m=m, tm=tm, nj=1, rows=[Row(x1), Row(f0)],
                                      pars=[Par(gain(0, 3)), Par(gain(1, 0))], cts=[Row(dx2), Row(dh2)],
                                      drows=[Out(D_MODEL, F32), Out(D_MODEL, BF16)])
    dh1, d_wup, d_cw0, d_cb0, d_wdown = _ffn_backward("l0_ffn", 0, df0, h1, hid0, act0, w_up_g, cw5, cb5, w_down_g, d_wup, d_wdown, m, seq)
    dx0a, dmix0, d_g01, d_g02 = tile_bwd("l0_dpostnorm", fn_addnorm2, m=m, tm=tm, nj=1, rows=[Row(x0), Row(mix0)],
                                         pars=[Par(gain(0, 1)), Par(gain(0, 2))], cts=[Row(dx1), Row(dh1)],
                                         drows=[Out(D_MODEL, F32), Out(D_MODEL, BF16)])
    dmixcat0 = mm2d("l0_dmixcat", "nt", dmix0, w_out_e, BF16)
    d_wout_e = mm2d("l0_dwout", "tn", mixcat0, dmix0)
    dzq, dzf0, dzv, dzg, d_lb, d_hnorm = hgrn_bwd("l0_dhgrn", z0, sprev, hgrn_lb_logits, hgrn_norm, dmixcat0, n_batch=n_batch, seq=seq)
    dzx, dzy, d_rcw, d_rcb, d_wa, d_ba, d_wx, d_bx, d_lam = tile_bwd(
        "l0_drglru", fn_rglru, m=m, tm=seq, nj=B_WIDTH // LANES, rows=rg_rows(), pars=rg_pars(),
        cts=[Row(dmixcat0, LANES, A_WIDTH // LANES)], drows=[Out(B_WIDTH, BF16, LANES), Out(B_WIDTH, BF16, LANES)])
    dz0 = jnp.concatenate([dzq, dzf0, dzv, dzg, dzx, dzy], axis=-1)
    dh0 = mm("l0_dh", "nt",
             Blk(dz0, (tmm, 384), lambda i, j, k: (i, k)),
             Blk(w_in_e, (None, None, D_MODEL, 384), lambda i, j, k: (k, 0, 0, 0)),
             Blk((m, D_MODEL), (tmm, D_MODEL), lambda i, j, k: (i, 0)), BF16, (nm, 1, N_DEV))
    d_win_e = mm("l0_dwin", "tn",
                 Blk(h0, (tmm, D_MODEL), lambda i, j, k: (k, 0)),
                 Blk(dz0, (tmm, 384), lambda i, j, k: (k, j)),
                 Blk(w_in_e.shape, (None, None, D_MODEL, 384), lambda i, j, k: (j, 0, 0, 0)), F32, (1, N_DEV, nm))
    dx0, d_g00 = tile_bwd("l0_dprenorm", fn_input_norm, m=m, tm=tm, nj=1, rows=[Row(x0)], pars=[Par(gain(0, 0))],
                          cts=[Row(dx0a), Row(dh0)], drows=[Out(D_MODEL, F32)])

    d_gains = jnp.stack([jnp.concatenate([d_g00, d_g01, d_g02, d_g03], axis=0), jnp.concatenate([d_g10, d_g11, d_g12, d_g13], axis=0)])
    d_ffn_cw = jnp.stack([d_cw0, d_cw1], axis=2).reshape(N_DEV, n_layer, FFN_CONV, FF_BLK)
    send = {
        "even_w_in": d_win_e,
        "even_w_out": d_wout_e.reshape(N_DEV, 1, D_MODEL // N_DEV, D_MODEL),
        "odd_w_in": _cols_to_blocks(d_win_o[None, :, :3088]),
        "odd_w_out": d_wout_o.reshape(N_DEV, 1, D_MODEL // N_DEV, D_MODEL),
        "ffn_w_up": d_wup,
        "ffn_w_down": d_wdown,
        "norm_gains": _cols_to_blocks(d_gains),
        "rg_conv_w": _cols_to_blocks(d_rcw[None]),
        "ffn_conv_w": d_ffn_cw,
    }
    order = BIG + SMALL_SHARDED
    recv = dict(zip(order, all_to_all("exchange_grads", [send[n] for n in order])))
    res = {n: adam_tiled("adam_" + n, recv[n], w[n], mom[n], var[n]) for n in BIG}

    d_ffn_cb = jnp.stack([d_cb0, d_cb1]).reshape(n_layer, 2 * D_FF)
    rep = {"hgrn_lb_logits": d_lb, "hgrn_norm": d_hnorm, "rg_conv_b": d_rcb, "rg_wa": _block_diag_grad(d_wa)[None], "rg_ba": d_ba,
           "rg_wx": _block_diag_grad(d_wx)[None], "rg_bx": d_bx, "rg_lambda": d_lam, "fox_f_bias": d_fbias[:, :C_HEADS],
           "ffn_conv_b": d_ffn_cb}
    parts = all_gather("gather_partials", [rep[n] for n in REPLICATED] + [loss_part])
    for n, p in zip(REPLICATED, parts):
        recv[n] = p
    small = SMALL_SHARDED + REPLICATED
    small_res, (loss_sum,) = adam_small("adam_small", [(recv[n], w[n], mom[n], var[n]) for n in small], [parts[-1]])
    res.update(dict(zip(small, small_res)))

    out = [loss_sum[0, 0], dx0.reshape(x.shape)]
    for k in range(4):
        out += [res[n][k] for n in NAMES]
    return tuple(out)
```

```python
import functools

import jax
import jax.numpy as jnp
from jax import lax
from jax.experimental import pallas as pl
from jax.experimental.pallas import tpu as pltpu

F32 = jnp.float32
BF16 = jnp.bfloat16

D_MODEL = 1024
A_HEADS = 4
A_WIDTH = 512
HGRN_CHUNK = 64
HGRN_SEG = 512
B_WIDTH = 512
B_BLOCKS = 8
B_BLOCK_DIM = 64
B_CONV = 4
RG_C = 8.0
C_HEADS = 16
C_HEAD_DIM = 64
D_FF = 2816
FFN_CONV = 3
EPS = 1e-6
LANES = 128
HALO = 16
N_DEV = 8
FF_BLK = 2 * D_FF // N_DEV
MESH = pl.DeviceIdType.MESH
NEG = -1e30
VMEM_LIMIT = 56 * 1024 * 1024

ADAM_LR = 0.001
ADAM_B1 = 0.9
ADAM_B2 = 0.999
ADAM_EPS = 1e-08
ADAM_WD = 0.01
ADAM_STEP = 10


def _dg(a, b, pat):
    nb = a.ndim - 2
    batch = (tuple(range(nb)), tuple(range(nb)))
    ca = a.ndim - 1 if pat[0] == "n" else a.ndim - 2
    cb = b.ndim - 2 if pat[1] == "n" else b.ndim - 1
    return lax.dot_general(a.astype(BF16), b.astype(BF16), (((ca,), (cb,)), batch), preferred_element_type=F32)


@functools.partial(jax.custom_vjp, nondiff_argnums=(2,))
def bdot(a, b, pat):
    return _dg(a, b, pat)


def _bdot_fwd(a, b, pat):
    return _dg(a, b, pat), (a, b)


def _bdot_bwd(pat, res, g):
    a, b = res
    if pat == "nn":
        return _dg(g, b, "nt"), _dg(a, g, "tn")
    if pat == "nt":
        return _dg(g, b, "nn"), _dg(g, a, "tn")
    return _dg(b, g, "nt"), _dg(a, g, "nn")


bdot.defvjp(_bdot_fwd, _bdot_bwd)


def _shift_raw(x, s, up, fill):
    if s == 0:
        return x
    n = x.shape[0]
    r = pltpu.roll(x, (n - s) if up else s, 0)
    idx = lax.broadcasted_iota(jnp.int32, x.shape, 0)
    mask = (idx >= n - s) if up else (idx < s)
    return jnp.where(mask, jnp.asarray(fill, x.dtype), r)


@functools.partial(jax.custom_vjp, nondiff_argnums=(1,))
def shift_down(x, s):
    return _shift_raw(x, s, False, 0.0)


def _shift_down_fwd(x, s):
    return _shift_raw(x, s, False, 0.0), None


def _shift_down_bwd(s, _, g):
    return (_shift_raw(g, s, True, 0.0),)


shift_down.defvjp(_shift_down_fwd, _shift_down_bwd)


def _scan_impl(a, u, up):
    n = a.shape[0]
    s = 1
    while s < n:
        u = a * _shift_raw(u, s, up, 0.0) + u
        if 2 * s < n:
            a = a * _shift_raw(a, s, up, 1.0)
        s *= 2
    return u


@jax.custom_vjp
def lin_scan(a, u):
    return _scan_impl(a, u, False)


def _lin_scan_fwd(a, u):
    h = _scan_impl(a, u, False)
    return h, (a, h)


def _lin_scan_bwd(res, g):
    a, h = res
    gh = _scan_impl(_shift_raw(a, 1, True, 0.0), g, True)
    return gh * _shift_raw(h, 1, False, 0.0), gh


lin_scan.defvjp(_lin_scan_fwd, _lin_scan_bwd)


def _cumsum_impl(x, up, period):
    n = x.shape[0]
    span = n if period is None else period
    idx = lax.broadcasted_iota(jnp.int32, x.shape, 0)
    pos = idx if period is None else idx % period
    s = 1
    while s < span:
        sh = _shift_raw(x, s, up, 0.0)
        if period is not None:
            keep = (pos < period - s) if up else (pos >= s)
            sh = jnp.where(keep, sh, 0.0)
        x = x + sh
        s *= 2
    return x


@functools.partial(jax.custom_vjp, nondiff_argnums=(1,))
def cumsum_rows(x, period):
    return _cumsum_impl(x, False, period)


def _cumsum_fwd(x, period):
    return _cumsum_impl(x, False, period), None


def _cumsum_bwd(period, _, g):
    return (_cumsum_impl(g, True, period),)


cumsum_rows.defvjp(_cumsum_fwd, _cumsum_bwd)


def _sigmoid(x):
    return jax.nn.sigmoid(x)


def _expm1(x):
    return jnp.tanh(0.5 * x) * (jnp.exp(x) + 1.0)


def _softplus(x):
    return jnp.maximum(x, 0.0) + jnp.log(1.0 + jnp.exp(-jnp.abs(x)))


def _rms(x, g):
    return x * lax.rsqrt(jnp.mean(x * x, axis=-1, keepdims=True) + EPS) * g


def fn_prenorm(x, g):
    return (_rms(x, g).astype(BF16),)


def fn_addnorm2(x, y, g_post, g_pre):
    x1 = x + _rms(y, g_post)
    return x1, _rms(x1, g_pre).astype(BF16)


def fn_input_norm(x, g):
    return x, _rms(x, g).astype(BF16)


def fn_final(x, y, tgt, g_post):
    out = x + _rms(y, g_post)
    err = out - tgt
    dy = err * (1.0 / D_MODEL)
    loss = 0.5 * jnp.sum(jnp.mean(err * err, axis=-1, keepdims=True), axis=0, keepdims=True)
    return dy, jnp.broadcast_to(loss, (1, LANES))


def fn_rms_only(y, g):
    return (_rms(y, g),)


def _causal_conv(x, w, b, taps):
    c = b
    for k in range(taps):
        c = c + w[k:k + 1, :] * shift_down(x, taps - 1 - k)
    return c


def fn_rglru(xb, yb, cw, cb, wa, ba, wx, bx, lam):
    xf = _causal_conv(xb, cw, cb, B_CONV)
    r = _sigmoid(bdot(xf, wa, "nn") + ba)
    i = _sigmoid(bdot(xf, wx, "nn") + bx)
    log_a = -RG_C * r * _softplus(-lam)
    a = jnp.exp(log_a)
    u = jnp.sqrt(-_expm1(2.0 * log_a)) * (i * xf)
    h = lin_scan(a, u)
    return ((h * jax.nn.gelu(yb)).astype(BF16),)


def fn_fox_gate(zf, bias):
    return (cumsum_rows(jax.nn.log_sigmoid(zf + bias), None),)


def fn_hgrn_seg(q, fl, v, g, st, logits, hn):
    rows = q.shape[0]
    nc = rows // HGRN_CHUNK
    l0, l1, l2 = logits[0:1, :], logits[1:2, :], logits[2:3, :]
    mx = jnp.maximum(jnp.maximum(l0, l1), l2)
    e0, e1, e2 = jnp.exp(l0 - mx), jnp.exp(l1 - mx), jnp.exp(l2 - mx)
    lb = e0 / (e0 + e1 + e2)
    forget = lb + (1.0 - lb) * _sigmoid(fl)
    qs = q * _sigmoid(q)
    kk = 1.0 - forget
    logf = jnp.log(forget)
    bcum = cumsum_rows(logf, HGRN_CHUNK)
    c3 = lambda t: t.reshape(nc, HGRN_CHUNK, 128)
    b_last = jnp.sum(c3(logf), axis=1, keepdims=True)
    bcum3 = c3(bcum)
    q_dec = c3(qs) * jnp.exp(bcum3)
    k_dec = c3(kk) * jnp.exp(-bcum3)
    k_upd = c3(kk) * jnp.exp(b_last - bcum3)
    v3 = c3(v)
    scores = bdot(q_dec, k_dec, "nt")
    ri = lax.broadcasted_iota(jnp.int32, scores.shape, 1)
    ci = lax.broadcasted_iota(jnp.int32, scores.shape, 2)
    scores = jnp.where(ri >= ci, scores, 0.0)
    o = bdot(scores, v3, "nn")
    upd_t = bdot(v3, k_upd, "tn")
    dec = jnp.exp(b_last)
    prev = []
    for n in range(nc):
        prev.append(st)
        st = st * dec[n] + upd_t[n]
    o = o + bdot(q_dec, jnp.stack(prev), "nt")
    o = o.reshape(rows, 128)
    o = o * lax.rsqrt(jnp.mean(o * o, axis=-1, keepdims=True) + EPS) * hn
    return (o * _sigmoid(g)).astype(BF16), st


def _ffn_conv(xg, xv, cw, cb):
    cg = _causal_conv(xg, cw[0], cb[0], FFN_CONV)[HALO:]
    cv = _causal_conv(xv, cw[1], cb[1], FFN_CONV)[HALO:]
    return cg, cv


def _ffn_gate(cg, cv):
    return jax.nn.gelu(cg) * cv


class Row:
    def __init__(self, arr, cb=None, off=0):
        self.arr, self.cb, self.off = arr, cb, off

    def spec(self, tm):
        if self.cb is None:
            return pl.BlockSpec((tm, self.arr.shape[1]), lambda j, i: (i, 0))
        off = self.off
        return pl.BlockSpec((tm, self.cb), lambda j, i: (i, j + off))


class Par:
    def __init__(self, arr, kind="full", bs=None):
        self.arr, self.kind, self.bs = arr, kind, bs

    def block(self):
        if self.kind == "full":
            return self.arr.shape
        if self.kind == "col":
            return (self.arr.shape[0], self.bs)
        return (self.bs, self.arr.shape[1])

    def spec(self):
        if self.kind == "full":
            return pl.BlockSpec(self.block(), lambda j, i: (0, 0))
        if self.kind == "col":
            return pl.BlockSpec(self.block(), lambda j, i: (0, j))
        return pl.BlockSpec(self.block(), lambda j, i: (j, 0))


class Out:
    def __init__(self, width, dtype, cb=None, off=0):
        self.width, self.dtype, self.cb, self.off = width, dtype, cb, off

    def spec(self, tm):
        if self.cb is None:
            return pl.BlockSpec((tm, self.width), lambda j, i: (i, 0))
        off = self.off
        return pl.BlockSpec((tm, self.cb), lambda j, i: (i, j + off))


def _params(sem):
    return pltpu.CompilerParams(dimension_semantics=sem, vmem_limit_bytes=VMEM_LIMIT)


def tile_fwd(name, fn, *, m, tm, nj, rows, pars, outs, n_acc=0):
    n_r, n_p, n_o = len(rows), len(pars), len(outs)

    def body(*refs):
        ins = [r[...] for r in refs[:n_r + n_p]]
        res = fn(*ins)
        o_refs = refs[n_r + n_p:]
        for k in range(n_o):
            o_refs[k][...] = res[k].astype(o_refs[k].dtype)
        first = jnp.logical_and(pl.program_id(0) == 0, pl.program_id(1) == 0)
        for k in range(n_acc):
            ref = o_refs[n_o + k]

            @pl.when(first)
            def _():
                ref[...] = jnp.zeros_like(ref)

            ref[...] += res[n_o + k]

    out_shape = [jax.ShapeDtypeStruct((m, o.width), o.dtype) for o in outs]
    out_specs = [o.spec(tm) for o in outs]
    for _ in range(n_acc):
        out_shape.append(jax.ShapeDtypeStruct((1, LANES), F32))
        out_specs.append(pl.BlockSpec((1, LANES), lambda j, i: (0, 0)))
    sem = ("arbitrary", "arbitrary") if n_acc else ("parallel", "parallel")
    return pl.pallas_call(
        body, grid=(nj, m // tm), name=name,
        in_specs=[r.spec(tm) for r in rows] + [p.spec() for p in pars],
        out_specs=out_specs, out_shape=out_shape, compiler_params=_params(sem),
    )(*[r.arr for r in rows], *[p.arr for p in pars])


def tile_bwd(name, fn, *, m, tm, nj, rows, pars, cts, drows):
    n_r, n_p, n_c = len(rows), len(pars), len(cts)
    want = [k for k in range(n_r) if drows[k] is not None]

    def body(*refs):
        ins = [r[...] for r in refs[:n_r + n_p]]
        ct = [r[...] for r in refs[n_r + n_p:n_r + n_p + n_c]]
        o_refs = refs[n_r + n_p + n_c:]
        res, vjp = jax.vjp(fn, *ins)
        grads = vjp(tuple(c.astype(r.dtype) for c, r in zip(ct, res)))
        for pos, k in enumerate(want):
            o_refs[pos][...] = grads[k].astype(o_refs[pos].dtype)
        for k in range(n_p):
            ref = o_refs[len(want) + k]
            first = pl.program_id(1) == 0
            if pars[k].kind == "full":
                first = jnp.logical_and(first, pl.program_id(0) == 0)

            @pl.when(first)
            def _():
                ref[...] = jnp.zeros_like(ref)

            ref[...] += grads[n_r + k].astype(F32)

    out_shape = [jax.ShapeDtypeStruct((m, drows[k].width), drows[k].dtype) for k in want]
    out_specs = [drows[k].spec(tm) for k in want]
    for p in pars:
        out_shape.append(jax.ShapeDtypeStruct(p.arr.shape, F32))
        out_specs.append(p.spec())
    return pl.pallas_call(
        body, grid=(nj, m // tm), name=name,
        in_specs=[r.spec(tm) for r in rows] + [p.spec() for p in pars] + [c.spec(tm) for c in cts],
        out_specs=out_specs, out_shape=out_shape, compiler_params=_params(("arbitrary", "arbitrary")),
    )(*[r.arr for r in rows], *[p.arr for p in pars], *[c.arr for c in cts])


class Blk:
    def __init__(self, arr, block, index):
        self.arr, self.block, self.index = arr, block, index

    def spec(self):
        return pl.BlockSpec(self.block, self.index)


def _flat2(v):
    return v if v.ndim == 2 else v.reshape(-1, v.shape[-1])


def mm(name, pat, a, b, o, out_dtype, grid, into=None):
    nk = grid[2]
    o_shape = o.arr

    def body(*refs):
        a_ref, b_ref = refs[0], refs[1]
        o_ref = refs[3] if into is not None else refs[2]
        r = _dg(_flat2(a_ref[...]), _flat2(b_ref[...]), pat)
        if nk == 1:
            o_ref[...] = r.astype(out_dtype).reshape(o_ref.shape)
            return
        acc_ref = refs[-1]
        kk = pl.program_id(2)

        @pl.when(kk == 0)
        def _():
            acc_ref[...] = r

        @pl.when(kk > 0)
        def _():
            acc_ref[...] += r

        @pl.when(kk == nk - 1)
        def _():
            o_ref[...] = acc_ref[...].astype(out_dtype).reshape(o_ref.shape)

    ob = [d for d in o.block if d is not None]
    acc_shape = (ob[0], ob[1]) if len(ob) == 2 else (ob[0] * ob[1], ob[2])
    in_specs = [a.spec(), b.spec()]
    args = [a.arr, b.arr]
    aliases = {}
    if into is not None:
        in_specs.append(pl.BlockSpec(memory_space=pl.ANY))
        args.append(into)
        aliases = {2: 0}
    return pl.pallas_call(
        body, grid=grid, name=name, in_specs=in_specs, out_specs=o.spec(),
        out_shape=jax.ShapeDtypeStruct(o_shape, out_dtype),
        scratch_shapes=[pltpu.VMEM(acc_shape, F32)] if nk > 1 else [],
        input_output_aliases=aliases,
        compiler_params=_params(("parallel", "parallel", "arbitrary")),
    )(*args)


def _div_tile(n, cap):
    if n <= cap:
        return n
    best = 128
    for t in range(128, cap + 1, 128):
        if n % t == 0:
            best = t
    return best


def mm2d(name, pat, a, b, out_dtype=F32):
    if pat == "tn":
        k, m = a.shape
    else:
        m, k = a.shape
    n = b.shape[0] if pat == "nt" else b.shape[1]
    tm, tn, tk = _div_tile(m, 1024), _div_tile(n, 1024), _div_tile(k, 1024)
    a_blk = Blk(a, (tk, tm), lambda i, j, kk: (kk, i)) if pat == "tn" else Blk(a, (tm, tk), lambda i, j, kk: (i, kk))
    b_blk = Blk(b, (tn, tk), lambda i, j, kk: (j, kk)) if pat == "nt" else Blk(b, (tk, tn), lambda i, j, kk: (kk, j))
    o_blk = Blk((m, n), (tm, tn), lambda i, j, kk: (i, j))
    return mm(name, pat, a_blk, b_blk, o_blk, out_dtype, (m // tm, n // tn, k // tk))


def hgrn_fwd(name, z, logits, hnorm, *, n_batch, seq):
    m = n_batch * seq
    ts = min(HGRN_SEG, seq)
    n_seg = seq // ts

    def body(q_ref, f_ref, v_ref, g_ref, lg_ref, hn_ref, o_ref, sp_ref, st_ref):
        s = pl.program_id(2)

        @pl.when(s == 0)
        def _():
            st_ref[...] = jnp.zeros_like(st_ref)

        st = st_ref[...]
        sp_ref[...] = st
        o, st_new = fn_hgrn_seg(q_ref[...], f_ref[...], v_ref[...], g_ref[...], st, lg_ref[...], hn_ref[...])
        o_ref[...] = o
        st_ref[...] = st_new

    part = lambda p: pl.BlockSpec((ts, 128), lambda h, b, s: (b * n_seg + s, 4 * p + h))
    return pl.pallas_call(
        body, grid=(A_HEADS, n_batch, n_seg), name=name,
        in_specs=[part(0), part(1), part(2), part(3),
                  pl.BlockSpec((3, 128), lambda h, b, s: (0, h)),
                  pl.BlockSpec((1, 128), lambda h, b, s: (0, h))],
        out_specs=[pl.BlockSpec((ts, 128), lambda h, b, s: (b * n_seg + s, h)),
                   pl.BlockSpec((128, 128), lambda h, b, s: ((b * n_seg + s) * A_HEADS + h, 0))],
        out_shape=[jax.ShapeDtypeStruct((m, A_WIDTH), BF16),
                   jax.ShapeDtypeStruct((n_batch * n_seg * A_HEADS * 128, 128), F32)],
        scratch_shapes=[pltpu.VMEM((128, 128), F32)],
        compiler_params=_params(("arbitrary", "arbitrary", "arbitrary")),
    )(z, z, z, z, logits, hnorm)


def hgrn_bwd(name, z, sprev, logits, hnorm, do, *, n_batch, seq):
    m = n_batch * seq
    ts = min(HGRN_SEG, seq)
    n_seg = seq // ts

    def body(q_ref, f_ref, v_ref, g_ref, sp_ref, lg_ref, hn_ref, do_ref, dq_ref, df_ref, dv_ref, dg_ref, dlg_ref, dhn_ref, dst_ref):
        s = pl.program_id(2)

        @pl.when(s == 0)
        def _():
            dst_ref[...] = jnp.zeros_like(dst_ref)

        res, vjp = jax.vjp(fn_hgrn_seg, q_ref[...], f_ref[...], v_ref[...], g_ref[...], sp_ref[...], lg_ref[...], hn_ref[...])
        dq, df, dv, dg, dst, dlg, dhn = vjp((do_ref[...].astype(res[0].dtype), dst_ref[...]))
        dq_ref[...] = dq.astype(dq_ref.dtype)
        df_ref[...] = df.astype(df_ref.dtype)
        dv_ref[...] = dv.astype(dv_ref.dtype)
        dg_ref[...] = dg.astype(dg_ref.dtype)
        dst_ref[...] = dst
        first = jnp.logical_and(pl.program_id(1) == 0, s == 0)

        @pl.when(first)
        def _():
            dlg_ref[...] = jnp.zeros_like(dlg_ref)
            dhn_ref[...] = jnp.zeros_like(dhn_ref)

        dlg_ref[...] += dlg
        dhn_ref[...] += dhn

    rev = lambda b, s: b * n_seg + (n_seg - 1 - s)
    part = lambda p: pl.BlockSpec((ts, 128), lambda h, b, s: (rev(b, s), 4 * p + h))
    head = pl.BlockSpec((ts, 128), lambda h, b, s: (rev(b, s), h))
    dpart = jax.ShapeDtypeStruct((m, A_WIDTH), BF16)
    return pl.pallas_call(
        body, grid=(A_HEADS, n_batch, n_seg), name=name,
        in_specs=[part(0), part(1), part(2), part(3),
                  pl.BlockSpec((128, 128), lambda h, b, s: (rev(b, s) * A_HEADS + h, 0)),
                  pl.BlockSpec((3, 128), lambda h, b, s: (0, h)),
                  pl.BlockSpec((1, 128), lambda h, b, s: (0, h)),
                  head],
        out_specs=[head, head, head, head,
                   pl.BlockSpec((3, 128), lambda h, b, s: (0, h)),
                   pl.BlockSpec((1, 128), lambda h, b, s: (0, h))],
        out_shape=[dpart, dpart, dpart, dpart,
                   jax.ShapeDtypeStruct(logits.shape, F32),
                   jax.ShapeDtypeStruct(hnorm.shape, F32)],
        scratch_shapes=[pltpu.VMEM((128, 128), F32)],
        compiler_params=_params(("arbitrary", "arbitrary", "arbitrary")),
    )(z, z, z, z, sprev, logits, hnorm, do)


def _ffn_tiles(m, seq):
    tm = min(512, seq)
    return tm, seq // tm, m // tm


def ffn_mid_fwd(name, hid, cw, cb, layer, *, m, seq):
    tm, n_t, n_i = _ffn_tiles(m, seq)
    hb = tm // HALO

    def body(x_ref, xb_ref, cw_ref, cb_ref, o_ref):
        first = pl.program_id(1) % n_t == 0
        before = jnp.where(first, 0.0, xb_ref[...])
        ext = jnp.concatenate([before, x_ref[...]], axis=1)
        cg, cv = _ffn_conv(ext[0], ext[1], cw_ref[...], cb_ref[...])
        o_ref[...] = _ffn_gate(cg, cv).astype(o_ref.dtype)

    return pl.pallas_call(
        body, grid=(N_DEV // 2, n_i), name=name,
        in_specs=[pl.BlockSpec((2, None, tm, FF_BLK), lambda d, i: (0, d, i, 0)),
                  pl.BlockSpec((2, None, HALO, FF_BLK), lambda d, i: (0, d, jnp.maximum(i * hb - 1, 0), 0)),
                  pl.BlockSpec((2, None, None, FFN_CONV, FF_BLK), lambda d, i: (0, d, layer, 0, 0)),
                  pl.BlockSpec((None, 2, None, 1, FF_BLK), lambda d, i: (layer, 0, d, 0, 0))],
        out_specs=pl.BlockSpec((None, tm, FF_BLK), lambda d, i: (d, i, 0)),
        out_shape=jax.ShapeDtypeStruct((N_DEV // 2, m, FF_BLK), BF16),
        compiler_params=_params(("parallel", "parallel")),
    )(hid, hid, cw, cb)


def ffn_mid_bwd(name, hid, cw, cb, dact, layer, *, m, seq):
    tm, n_t, n_i = _ffn_tiles(m, seq)
    hb = tm // HALO
    last_blk = m // HALO - 1

    def body(x_ref, xb_ref, xa_ref, cw_ref, cb_ref, da_ref, daa_ref, dx_ref, dcw_ref, dcb_ref):
        i = pl.program_id(1)
        first = i % n_t == 0
        last = i % n_t == n_t - 1
        before = jnp.where(first, 0.0, xb_ref[...])
        ext = jnp.concatenate([before, x_ref[...], xa_ref[...]], axis=1)
        dact_ext = jnp.concatenate([da_ref[...].astype(F32), jnp.where(last, 0.0, daa_ref[...].astype(F32))], axis=0)
        (cg, cv), vjp_conv = jax.vjp(_ffn_conv, ext[0], ext[1], cw_ref[...], cb_ref[...])
        _, vjp_gate = jax.vjp(_ffn_gate, cg, cv)
        dcg, dcv = vjp_gate(dact_ext)
        dxg, dxv, _, _ = vjp_conv((dcg, dcv))
        dx_ref[0] = dxg[HALO:HALO + tm].astype(dx_ref.dtype)
        dx_ref[1] = dxv[HALO:HALO + tm].astype(dx_ref.dtype)
        own = lax.broadcasted_iota(jnp.int32, dcg.shape, 0) < tm
        _, _, dcw, dcb = vjp_conv((jnp.where(own, dcg, 0.0), jnp.where(own, dcv, 0.0)))

        @pl.when(i == 0)
        def _():
            dcw_ref[...] = jnp.zeros_like(dcw_ref)
            dcb_ref[...] = jnp.zeros_like(dcb_ref)

        dcw_ref[...] += dcw
        dcb_ref[...] += dcb

    return pl.pallas_call(
        body, grid=(N_DEV // 2, n_i), name=name,
        in_specs=[pl.BlockSpec((2, None, tm, FF_BLK), lambda d, i: (0, d, i, 0)),
                  pl.BlockSpec((2, None, HALO, FF_BLK), lambda d, i: (0, d, jnp.maximum(i * hb - 1, 0), 0)),
                  pl.BlockSpec((2, None, HALO, FF_BLK), lambda d, i: (0, d, jnp.minimum((i + 1) * hb, last_blk), 0)),
                  pl.BlockSpec((2, None, None, FFN_CONV, FF_BLK), lambda d, i: (0, d, layer, 0, 0)),
                  pl.BlockSpec((None, 2, None, 1, FF_BLK), lambda d, i: (layer, 0, d, 0, 0)),
                  pl.BlockSpec((None, tm, FF_BLK), lambda d, i: (d, i, 0)),
                  pl.BlockSpec((None, HALO, FF_BLK), lambda d, i: (d, jnp.minimum((i + 1) * hb, last_blk), 0))],
        out_specs=[pl.BlockSpec((2, None, tm, FF_BLK), lambda d, i: (0, d, i, 0)),
                   pl.BlockSpec((2, None, FFN_CONV, FF_BLK), lambda d, i: (0, d, 0, 0)),
                   pl.BlockSpec((2, None, 1, FF_BLK), lambda d, i: (0, d, 0, 0))],
        out_shape=[jax.ShapeDtypeStruct((2, N_DEV // 2, m, FF_BLK), BF16),
                   jax.ShapeDtypeStruct((2, N_DEV // 2, FFN_CONV, FF_BLK), F32),
                   jax.ShapeDtypeStruct((2, N_DEV // 2, 1, FF_BLK), F32)],
        compiler_params=_params(("arbitrary", "arbitrary")),
    )(hid, hid, hid, cw, cb, dact, dact)


ATT_BLK = 256


def fox_fwd(name, q, k, v, ccol, crow):
    bh, seq, dh = q.shape
    blk = min(ATT_BLK, seq)
    nq = seq // blk
    scale = dh ** -0.5

    def body(q_ref, k_ref, v_ref, cc_ref, cr_ref, o_ref, lse_ref):
        qi = pl.program_id(1)
        qv = q_ref[0]
        cq = cc_ref[0]
        row = qi * blk + lax.broadcasted_iota(jnp.int32, (blk, blk), 0)

        def step(j, carry):
            mx, l, acc = carry
            st = pl.multiple_of(j * blk, blk)
            kj = k_ref[0, pl.ds(st, blk), :]
            vj = v_ref[0, pl.ds(st, blk), :]
            s = _dg(qv, kj, "nt") * scale + cq - cr_ref[0, j]
            col = j * blk + lax.broadcasted_iota(jnp.int32, (blk, blk), 1)
            s = jnp.where(row >= col, s, NEG)
            mx_new = jnp.maximum(mx, jnp.max(s, axis=-1, keepdims=True))
            p = jnp.exp(s - mx_new)
            alpha = jnp.exp(mx - mx_new)
            l = alpha * l + jnp.sum(p, axis=-1, keepdims=True)
            acc = alpha * acc + _dg(p, vj, "nn")
            return mx_new, l, acc

        init = (jnp.full((blk, 1), NEG, F32), jnp.zeros((blk, 1), F32), jnp.zeros((blk, dh), F32))
        mx, l, acc = lax.fori_loop(0, qi + 1, step, init)
        o_ref[0] = (acc / l).astype(o_ref.dtype)
        lse_ref[0] = mx + jnp.log(l)

    return pl.pallas_call(
        body, grid=(bh, nq), name=name,
        in_specs=[pl.BlockSpec((1, blk, dh), lambda b, i: (b, i, 0)),
                  pl.BlockSpec((1, seq, dh), lambda b, i: (b, 0, 0)),
                  pl.BlockSpec((1, seq, dh), lambda b, i: (b, 0, 0)),
                  pl.BlockSpec((1, blk, 1), lambda b, i: (b, i, 0)),
                  pl.BlockSpec((1, nq, 1, blk), lambda b, i: (b, 0, 0, 0))],
        out_specs=[pl.BlockSpec((1, blk, dh), lambda b, i: (b, i, 0)),
                   pl.BlockSpec((1, blk, 1), lambda b, i: (b, i, 0))],
        out_shape=[jax.ShapeDtypeStruct((bh, seq, dh), BF16), jax.ShapeDtypeStruct((bh, seq, 1), F32)],
        compiler_params=_params(("parallel", "arbitrary")),
    )(q, k, v, ccol, crow)


def fox_bwd(name, q, k, v, o, do, lse, ccol, crow):
    bh, seq, dh = q.shape
    blk = min(ATT_BLK, seq)
    nq = seq // blk
    scale = dh ** -0.5

    def body(q_ref, k_ref, v_ref, o_ref, do_ref, lse_ref, cc_ref, cr_ref, dq_ref, dk_ref, dv_ref, drow_ref, dcol_ref, del_ref):
        j = pl.program_id(1)

        @pl.when(j == 0)
        def _():
            dq_ref[...] = jnp.zeros_like(dq_ref)
            drow_ref[...] = jnp.zeros_like(drow_ref)
            del_ref[...] = jnp.sum(do_ref[0].astype(F32) * o_ref[0].astype(F32), axis=-1, keepdims=True)

        kj, vj = k_ref[0], v_ref[0]
        ck = cr_ref[0, 0]
        col = j * blk + lax.broadcasted_iota(jnp.int32, (blk, blk), 1)

        def step(i, carry):
            dk, dv, dcol = carry
            st = pl.multiple_of(i * blk, blk)
            rows = pl.ds(st, blk)
            qv, dov = q_ref[0, rows, :], do_ref[0, rows, :]
            s = _dg(qv, kj, "nt") * scale + cc_ref[0, rows, :] - ck
            row = i * blk + lax.broadcasted_iota(jnp.int32, (blk, blk), 0)
            p = jnp.where(row >= col, jnp.exp(s - lse_ref[0, rows, :]), 0.0)
            dv = dv + _dg(p, dov, "tn")
            ds = p * (_dg(dov, vj, "nt") - del_ref[rows, :])
            dk = dk + _dg(ds, qv, "tn") * scale
            dq_ref[0, rows, :] += _dg(ds, kj, "nn") * scale
            drow_ref[0, rows, :] += jnp.sum(ds, axis=-1, keepdims=True)
            return dk, dv, dcol + jnp.sum(ds, axis=0, keepdims=True)

        init = (jnp.zeros((blk, dh), F32), jnp.zeros((blk, dh), F32), jnp.zeros((1, blk), F32))
        dk, dv, dcol = lax.fori_loop(j, nq, step, init)
        dk_ref[0] = dk
        dv_ref[0] = dv
        dcol_ref[0, 0] = dcol

    full = pl.BlockSpec((1, seq, dh), lambda b, j: (b, 0, 0))
    blkd = pl.BlockSpec((1, blk, dh), lambda b, j: (b, j, 0))
    col1 = pl.BlockSpec((1, seq, 1), lambda b, j: (b, 0, 0))
    rowb = pl.BlockSpec((1, 1, 1, blk), lambda b, j: (b, j, 0, 0))
    return pl.pallas_call(
        body, grid=(bh, nq), name=name,
        in_specs=[full, blkd, blkd, full, full, col1, col1, rowb],
        out_specs=[full, blkd, blkd, col1, rowb],
        out_shape=[jax.ShapeDtypeStruct((bh, seq, dh), F32), jax.ShapeDtypeStruct((bh, seq, dh), F32),
                   jax.ShapeDtypeStruct((bh, seq, dh), F32), jax.ShapeDtypeStruct((bh, seq, 1), F32),
                   jax.ShapeDtypeStruct((bh, nq, 1, blk), F32)],
        scratch_shapes=[pltpu.VMEM((seq, 1), F32)],
        compiler_params=_params(("parallel", "arbitrary")),
    )(q, k, v, o, do, lse, ccol, crow)


def _mesh_pos():
    return lax.axis_index("x"), lax.axis_index("y"), lax.axis_index("c")


def _flip(v, bit):
    return 1 - v if bit else v


def all_gather(name, blocks):
    n = len(blocks)

    def body(*refs):
        x_refs, out_refs = refs[:n], refs[n:2 * n]
        send_sems, recv_sems, local_sems = refs[2 * n:]
        x, y, c = _mesh_pos()
        me, sibling = (x, y, c), (x, y, 1 - c)
        chips = [(1 - x, y), (x, 1 - y), (1 - x, 1 - y)]

        def slot(a, px, py, pc):
            return out_refs[a].at[4 * px + 2 * py + pc]

        def copy(a, k, blk, to, src=None):
            return pltpu.make_async_remote_copy(
                src_ref=slot(a, *blk) if src is None else src, dst_ref=slot(a, *blk),
                send_sem=send_sems.at[a, k], recv_sem=recv_sems.at[a, k], device_id=to, device_id_type=MESH)

        mine = [pltpu.make_async_copy(x_refs[a], slot(a, *me), local_sems.at[a]) for a in range(n)]
        for cp in mine:
            cp.start()
        sends = []
        for a in range(n):
            sends.append(copy(a, 0, me, sibling, src=x_refs[a]))
            sends += [copy(a, 1 + j, me, (*chip, c), src=x_refs[a]) for j, chip in enumerate(chips)]
        for cp in sends:
            cp.start()
        for j, chip in enumerate(chips):
            for a in range(n):
                copy(a, 1 + j, (*chip, c), me).wait_recv()
                passed = copy(a, 4 + j, (*chip, c), sibling)
                passed.start()
                sends.append(passed)
        for a in range(n):
            copy(a, 0, sibling, me).wait_recv()
            for j, chip in enumerate(chips):
                copy(a, 4 + j, (*chip, 1 - c), me).wait_recv()
        for cp in sends:
            cp.wait_send()
        for cp in mine:
            cp.wait()

    hbm = pl.BlockSpec(memory_space=pl.ANY)
    return pl.pallas_call(
        body, name=name, out_shape=[jax.ShapeDtypeStruct((N_DEV,) + b.shape, b.dtype) for b in blocks],
        in_specs=[hbm] * n, out_specs=[hbm] * n,
        scratch_shapes=[pltpu.SemaphoreType.DMA((n, 7)), pltpu.SemaphoreType.DMA((n, 7)), pltpu.SemaphoreType.DMA((n,))],
    )(*blocks)


def all_to_all(name, sends):
    n = len(sends)

    def body(*refs):
        s_refs, r_refs = refs[:n], refs[n:2 * n]
        send_sems, recv_sems, local_sems = refs[2 * n:]
        x, y, c = _mesh_pos()
        me = 4 * x + 2 * y + c
        mine = [pltpu.make_async_copy(s_refs[a].at[me], r_refs[a].at[me], local_sems.at[a]) for a in range(n)]
        for cp in mine:
            cp.start()
        copies = []
        for k in range(1, N_DEV):
            px, py, pc = _flip(x, k & 4), _flip(y, k & 2), _flip(c, k & 1)
            for a in range(n):
                copies.append(pltpu.make_async_remote_copy(
                    src_ref=s_refs[a].at[4 * px + 2 * py + pc], dst_ref=r_refs[a].at[me],
                    send_sem=send_sems.at[a, k - 1], recv_sem=recv_sems.at[a, k - 1],
                    device_id=(px, py, pc), device_id_type=MESH))
        for cp in copies:
            cp.start()
        for cp in copies:
            cp.wait_recv()
        for cp in copies:
            cp.wait_send()
        for cp in mine:
            cp.wait()

    hbm = pl.BlockSpec(memory_space=pl.ANY)
    return pl.pallas_call(
        body, name=name, out_shape=[jax.ShapeDtypeStruct(s.shape, s.dtype) for s in sends],
        in_specs=[hbm] * n, out_specs=[hbm] * n,
        scratch_shapes=[pltpu.SemaphoreType.DMA((n, 7)), pltpu.SemaphoreType.DMA((n, 7)), pltpu.SemaphoreType.DMA((n,))],
    )(*sends)


def pair_exchange(name, hs):
    n = len(hs)

    def body(*refs):
        h_refs, r_refs = refs[:n], refs[n:2 * n]
        send_sems, recv_sems = refs[2 * n:]
        x, y, c = _mesh_pos()
        copies = [pltpu.make_async_remote_copy(
            src_ref=h_refs[a].at[1 - c], dst_ref=r_refs[a], send_sem=send_sems.at[a], recv_sem=recv_sems.at[a],
            device_id=(x, y, 1 - c), device_id_type=MESH) for a in range(n)]
        for cp in copies:
            cp.start()
        for cp in copies:
            cp.wait_recv()
        for cp in copies:
            cp.wait_send()

    hbm = pl.BlockSpec(memory_space=pl.ANY)
    return pl.pallas_call(
        body, name=name, out_shape=[jax.ShapeDtypeStruct(h.shape[1:], h.dtype) for h in hs],
        in_specs=[hbm] * n, out_specs=[hbm] * n,
        scratch_shapes=[pltpu.SemaphoreType.DMA((n,)), pltpu.SemaphoreType.DMA((n,))],
    )(*hs)


def quad_exchange(name, ss):
    n = len(ss)

    def body(*refs):
        s_refs, r_refs = refs[:n], refs[n:2 * n]
        send_sems, recv_sems, local_sems = refs[2 * n:]
        x, y, c = _mesh_pos()
        me = 2 * x + y
        mine = [pltpu.make_async_copy(s_refs[a].at[me], r_refs[a].at[me], local_sems.at[a]) for a in range(n)]
        for cp in mine:
            cp.start()
        copies = []
        for k in range(1, 4):
            px, py = _flip(x, k & 2), _flip(y, k & 1)
            for a in range(n):
                copies.append(pltpu.make_async_remote_copy(
                    src_ref=s_refs[a].at[2 * px + py], dst_ref=r_refs[a].at[me],
                    send_sem=send_sems.at[a, k - 1], recv_sem=recv_sems.at[a, k - 1],
                    device_id=(px, py, c), device_id_type=MESH))
        for cp in copies:
            cp.start()
        for cp in copies:
            cp.wait_recv()
        for cp in copies:
            cp.wait_send()
        for cp in mine:
            cp.wait()

    hbm = pl.BlockSpec(memory_space=pl.ANY)
    return pl.pallas_call(
        body, name=name, out_shape=[jax.ShapeDtypeStruct(s.shape, s.dtype) for s in ss],
        in_specs=[hbm] * n, out_specs=[hbm] * n,
        scratch_shapes=[pltpu.SemaphoreType.DMA((n, 3)), pltpu.SemaphoreType.DMA((n, 3)), pltpu.SemaphoreType.DMA((n,))],
    )(*ss)


def _rows_cols(shape):
    r = 1
    for d in shape[:-1]:
        r *= d
    return r, shape[-1]


def _row_tile(r, cap, step):
    return next((t for t in range(cap, step - 1, -step) if r % t == 0), r)


def pair_add(name, h, recv, core):
    shape = recv.shape
    r, c = _rows_cols(shape[1:])
    tr = _row_tile(r, 256, 16)

    def body(core_ref, h_ref, r_ref, o_ref):
        o_ref[...] = (h_ref[...].astype(F32) + r_ref[...].astype(F32)).astype(o_ref.dtype)

    spec = pl.BlockSpec((None, tr, c), lambda q, i, core_ref: (q, i, 0))
    res = pl.pallas_call(
        body, name=name, out_shape=jax.ShapeDtypeStruct((4, r, c), h.dtype),
        grid_spec=pltpu.PrefetchScalarGridSpec(
            num_scalar_prefetch=1, grid=(4, r // tr),
            in_specs=[pl.BlockSpec((None, None, tr, c), lambda q, i, core_ref: (core_ref[0], q, i, 0)), spec],
            out_specs=spec),
        compiler_params=_params(("parallel", "parallel")),
    )(core, h.reshape(2, 4, r, c), recv.reshape(4, r, c))
    return res.reshape(shape)


def _sum_parts(p, n):
    t = [p[k].astype(F32) for k in range(n)]
    while len(t) > 1:
        t = [t[k] + t[k + 1] for k in range(0, len(t), 2)]
    return t[0]


def _adam(g, w, m, v):
    m = ADAM_B1 * m + (1.0 - ADAM_B1) * g
    v = ADAM_B2 * v + (1.0 - ADAM_B2) * (g * g)
    m_hat = m / (1.0 - ADAM_B1 ** ADAM_STEP)
    v_hat = v / (1.0 - ADAM_B2 ** ADAM_STEP)
    return -ADAM_LR * (m_hat / (jnp.sqrt(v_hat) + ADAM_EPS) + ADAM_WD * w), m, v


def adam_tiled(name, partials, w, m_, v_):
    shape = w.shape
    n_part = partials.shape[0]
    r, c = _rows_cols(shape)
    tr = _row_tile(r, 256, 16)

    def body(p_ref, w_ref, m_ref, v_ref, g_ref, d_ref, nm_ref, nv_ref):
        g = _sum_parts(p_ref, n_part)
        g_ref[...] = g
        d_ref[...], nm_ref[...], nv_ref[...] = _adam(g, w_ref[...], m_ref[...], v_ref[...])

    spec = pl.BlockSpec((tr, c), lambda i: (i, 0))
    res = pl.pallas_call(
        body, grid=(r // tr,), name=name,
        in_specs=[pl.BlockSpec((n_part, tr, c), lambda i: (0, i, 0)), spec, spec, spec],
        out_specs=[spec] * 4, out_shape=[jax.ShapeDtypeStruct((r, c), F32)] * 4,
        compiler_params=_params(("parallel",)),
    )(partials.reshape(n_part, r, c), w.reshape(r, c), m_.reshape(r, c), v_.reshape(r, c))
    return [t.reshape(shape) for t in res]


def adam_small(name, items, extra):
    n, ne = len(items), len(extra)

    def body(*refs):
        ins, outs = refs[:4 * n + ne], refs[4 * n + ne:]
        for a in range(n):
            p_ref, w_ref, m_ref, v_ref = ins[4 * a:4 * a + 4]
            g = _sum_parts(p_ref, N_DEV)
            outs[4 * a][...] = g
            outs[4 * a + 1][...], outs[4 * a + 2][...], outs[4 * a + 3][...] = _adam(g, w_ref[...], m_ref[...], v_ref[...])
        for e in range(ne):
            outs[4 * n + e][...] = _sum_parts(ins[4 * n + e], N_DEV)

    args, out_shape = [], []
    for p, w, m_, v_ in items:
        args += [p, w, m_, v_]
        out_shape += [jax.ShapeDtypeStruct(w.shape, F32)] * 4
    for e in extra:
        args.append(e)
        out_shape.append(jax.ShapeDtypeStruct(e.shape[1:], F32))
    vmem = pl.BlockSpec(memory_space=pltpu.VMEM)
    res = pl.pallas_call(body, name=name, in_specs=[vmem] * len(args), out_specs=[vmem] * len(out_shape), out_shape=out_shape)(*args)
    return [res[4 * a:4 * a + 4] for a in range(n)], res[4 * n:]


def _cols_from_gather(g):
    g = jnp.moveaxis(g, 0, -2)
    return g.reshape(g.shape[:-2] + (g.shape[-2] * g.shape[-1],))


def _cols_to_blocks(w):
    w = w.reshape(w.shape[:-1] + (N_DEV, w.shape[-1] // N_DEV))
    return jnp.moveaxis(w, -2, 0)


def _block_diag(w):
    z = jnp.zeros((B_BLOCK_DIM, B_BLOCK_DIM), w.dtype)
    rows = []
    for j in range(B_BLOCKS // 2):
        top = jnp.concatenate([w[2 * j], z], axis=1)
        bot = jnp.concatenate([z, w[2 * j + 1]], axis=1)
        rows.append(jnp.concatenate([top, bot], axis=0))
    return jnp.concatenate(rows, axis=0)


def _block_diag_grad(d):
    out = []
    for j in range(B_BLOCKS // 2):
        blk = d[128 * j:128 * (j + 1)]
        out.append(blk[:64, :64])
        out.append(blk[64:, 64:])
    return jnp.stack(out)


NAMES = ("norm_gains", "even_w_in", "hgrn_lb_logits", "hgrn_norm", "rg_conv_w", "rg_conv_b", "rg_wa", "rg_ba", "rg_wx", "rg_bx",
         "rg_lambda", "even_w_out", "odd_w_in", "fox_f_bias", "odd_w_out", "ffn_w_up", "ffn_conv_w", "ffn_conv_b", "ffn_w_down")
BIG = ("even_w_in", "even_w_out", "odd_w_in", "odd_w_out", "ffn_w_up", "ffn_w_down")
SMALL_SHARDED = ("norm_gains", "rg_conv_w", "ffn_conv_w")
REPLICATED = ("hgrn_lb_logits", "hgrn_norm", "rg_conv_b", "rg_wa", "rg_ba", "rg_wx", "rg_bx", "rg_lambda", "fox_f_bias", "ffn_conv_b")


def _ffn_forward(tag, layer, h, w_up_g, cw5, cb5, w_down_g, m, seq):
    tm = _div_tile(m, 1024)
    nm = m // tm
    hid = mm(f"{tag}_up", "nn",
             Blk(h, (tm, D_MODEL), lambda i, j, k: (i, 0)),
             Blk(w_up_g, (None, None, D_MODEL, FF_BLK), lambda i, j, k: (j, layer, 0, 0)),
             Blk((N_DEV, m, FF_BLK), (None, tm, FF_BLK), lambda i, j, k: (j, i, 0)), F32, (nm, N_DEV, 1))
    hid = hid.reshape(2, N_DEV // 2, m, FF_BLK)
    act = ffn_mid_fwd(f"{tag}_mid", hid, cw5, cb5, layer, m=m, seq=seq)
    f = mm(f"{tag}_down", "nn",
           Blk(act, (None, tm, FF_BLK), lambda i, j, k: (k, i, 0)),
           Blk(w_down_g, (2, None, FF_BLK // 2, D_MODEL), lambda i, j, k: (k, layer, 0, 0)),
           Blk((m, D_MODEL), (tm, D_MODEL), lambda i, j, k: (i, 0)), F32, (nm, 1, N_DEV // 2))
    return hid, act, f


def _ffn_backward(tag, layer, df, h, hid, act, w_up_g, cw5, cb5, w_down_g, d_wup, d_wdown, m, seq):
    tm = _div_tile(m, 1024)
    nm = m // tm
    dact = mm(f"{tag}_dact", "nt",
              Blk(df, (tm, D_MODEL), lambda i, j, k: (i, 0)),
              Blk(w_down_g, (2, None, FF_BLK // 2, D_MODEL), lambda i, j, k: (j, layer, 0, 0)),
              Blk((N_DEV // 2, m, FF_BLK), (None, tm, FF_BLK), lambda i, j, k: (j, i, 0)), BF16, (nm, N_DEV // 2, 1))
    d_wdown = mm(f"{tag}_dwdown", "tn",
                 Blk(act, (None, tm, FF_BLK), lambda i, j, k: (i, k, 0)),
                 Blk(df, (tm, D_MODEL), lambda i, j, k: (k, 0)),
                 Blk((2, 4) + w_down_g.shape[1:], (2, None, None, FF_BLK // 2, D_MODEL), lambda i, j, k: (0, i, layer, 0, 0)), BF16,
                 (N_DEV // 2, 1, nm), into=d_wdown)
    dhid, d_cw, d_cb = ffn_mid_bwd(f"{tag}_dmid", hid, cw5, cb5, dact, layer, m=m, seq=seq)
    dhid = dhid.reshape(N_DEV, m, FF_BLK)
    dh = mm(f"{tag}_dh", "nt",
            Blk(dhid, (None, tm, FF_BLK), lambda i, j, k: (k, i, 0)),
            Blk(w_up_g, (None, None, D_MODEL, FF_BLK), lambda i, j, k: (k, layer, 0, 0)),
            Blk((m, D_MODEL), (tm, D_MODEL), lambda i, j, k: (i, 0)), BF16, (nm, 1, N_DEV))
    d_wup = mm(f"{tag}_dwup", "tn",
               Blk(h, (tm, D_MODEL), lambda i, j, k: (k, 0)),
               Blk(dhid, (None, tm, FF_BLK), lambda i, j, k: (j, k, 0)),
               Blk((2, 4) + w_up_g.shape[1:], (None, None, None, D_MODEL, FF_BLK), lambda i, j, k: (j % 2, j // 2, layer, 0, 0)), BF16,
               (1, N_DEV, nm), into=d_wup)
    return dh, d_wup, d_cw, d_cb, d_wdown


def kernel(x, norm_gains, even_w_in, hgrn_lb_logits, hgrn_norm, rg_conv_w, rg_conv_b, rg_wa, rg_ba, rg_wx, rg_bx, rg_lambda, even_w_out, odd_w_in, fox_f_bias, odd_w_out, ffn_w_up, ffn_conv_w, ffn_conv_b, ffn_w_down, loss_target, m_norm_gains, m_even_w_in, m_hgrn_lb_logits, m_hgrn_norm, m_rg_conv_w, m_rg_conv_b, m_rg_wa, m_rg_ba, m_rg_wx, m_rg_bx, m_rg_lambda, m_even_w_out, m_odd_w_in, m_fox_f_bias, m_odd_w_out, m_ffn_w_up, m_ffn_conv_w, m_ffn_conv_b, m_ffn_w_down, v_norm_gains, v_even_w_in, v_hgrn_lb_logits, v_hgrn_norm, v_rg_conv_w, v_rg_conv_b, v_rg_wa, v_rg_ba, v_rg_wx, v_rg_bx, v_rg_lambda, v_even_w_out, v_odd_w_in, v_fox_f_bias, v_odd_w_out, v_ffn_w_up, v_ffn_conv_w, v_ffn_conv_b, v_ffn_w_down):
    local = dict(locals())
    w = {n: local[n] for n in NAMES}
    mom = {n: local["m_" + n] for n in NAMES}
    var = {n: local["v_" + n] for n in NAMES}
    n_batch, seq, _ = x.shape
    m = n_batch * seq
    tm = _div_tile(m, 512)
    tmm = _div_tile(m, 1024)
    nm = m // tmm

    gathered = all_gather("gather_weights", [w[n].astype(BF16) for n in BIG] + [w[n] for n in SMALL_SHARDED])
    g = dict(zip(BIG + SMALL_SHARDED, gathered))
    w_in_e = g["even_w_in"]
    w_out_e = g["even_w_out"].reshape(D_MODEL, D_MODEL)
    w_in_o = jnp.pad(_cols_from_gather(g["odd_w_in"])[0], ((0, 0), (0, 3200 - 3088)))
    w_out_o = g["odd_w_out"].reshape(D_MODEL, D_MODEL)
    w_up_g, w_down_g = g["ffn_w_up"], g["ffn_w_down"]
    gains = _cols_from_gather(g["norm_gains"])
    rg_cw = _cols_from_gather(g["rg_conv_w"])[0]
    n_layer = ffn_conv_w.shape[0]
    cw5 = g["ffn_conv_w"].reshape(2, N_DEV // 2, n_layer, FFN_CONV, FF_BLK)
    cb5 = ffn_conv_b.reshape(n_layer, 2, N_DEV // 2, 1, FF_BLK)
    gain = lambda l, k: gains[l, k:k + 1, :]
    wa_bd, wx_bd = _block_diag(rg_wa[0]), _block_diag(rg_wx[0])
    fbias = jnp.pad(fox_f_bias, ((0, 0), (0, LANES - C_HEADS)))

    x0 = x.reshape(m, D_MODEL)
    tgt = loss_target.reshape(m, D_MODEL)

    (h0,) = tile_fwd("l0_prenorm", fn_prenorm, m=m, tm=tm, nj=1, rows=[Row(x0)], pars=[Par(gain(0, 0))], outs=[Out(D_MODEL, BF16)])
    z0 = mm("l0_in", "nn",
            Blk(h0, (tmm, D_MODEL), lambda i, j, k: (i, 0)),
            Blk(w_in_e, (None, None, D_MODEL, 384), lambda i, j, k: (j, 0, 0, 0)),
            Blk((m, 3072), (tmm, 384), lambda i, j, k: (i, j)), F32, (nm, N_DEV, 1))
    oa, sprev = hgrn_fwd("l0_hgrn", z0, hgrn_lb_logits, hgrn_norm, n_batch=n_batch, seq=seq)
    rg_rows = lambda: [Row(z0, LANES, 16), Row(z0, LANES, 20)]
    rg_pars = lambda: [Par(rg_cw, "col", LANES), Par(rg_conv_b, "col", LANES), Par(wa_bd, "row", LANES), Par(rg_ba, "col", LANES),
                       Par(wx_bd, "row", LANES), Par(rg_bx, "col", LANES), Par(rg_lambda, "col", LANES)]
    (ob,) = tile_fwd("l0_rglru", fn_rglru, m=m, tm=seq, nj=B_WIDTH // LANES, rows=rg_rows(), pars=rg_pars(),
                     outs=[Out(B_WIDTH, BF16, LANES)])
    mixcat0 = jnp.concatenate([oa, ob], axis=-1)
    mix0 = mm2d("l0_out", "nn", mixcat0, w_out_e)
    x1, h1 = tile_fwd("l0_postnorm", fn_addnorm2, m=m, tm=tm, nj=1, rows=[Row(x0), Row(mix0)], pars=[Par(gain(0, 1)), Par(gain(0, 2))],
                      outs=[Out(D_MODEL, F32), Out(D_MODEL, BF16)])
    hid0, act0, f0 = _ffn_forward("l0_ffn", 0, h1, w_up_g, cw5, cb5, w_down_g, m, seq)
    x2, h2 = tile_fwd("l0_ffnnorm", fn_addnorm2, m=m, tm=tm, nj=1, rows=[Row(x1), Row(f0)], pars=[Par(gain(0, 3)), Par(gain(1, 0))],
                      outs=[Out(D_MODEL, F32), Out(D_MODEL, BF16)])

    z1 = mm2d("l1_in", "nn", h2, w_in_o)
    (cgate,) = tile_fwd("l1_gate", fn_fox_gate, m=m, tm=seq, nj=1, rows=[Row(z1, LANES, 3072 // LANES)], pars=[Par(fbias)],
                        outs=[Out(LANES, F32)])
    bh = n_batch * C_HEADS
    nqb = seq // min(ATT_BLK, seq)
    heads = lambda t: t.reshape(n_batch, seq, C_HEADS, C_HEAD_DIM).transpose(0, 2, 1, 3).reshape(bh, seq, C_HEAD_DIM)
    unheads = lambda t: t.reshape(n_batch, C_HEADS, seq, C_HEAD_DIM).transpose(0, 2, 1, 3).reshape(m, D_MODEL)
    qh, kh, vh = (heads(z1[:, i * D_MODEL:(i + 1) * D_MODEL].astype(BF16)) for i in range(3))
    c_bht = cgate[:, :C_HEADS].reshape(n_batch, seq, C_HEADS).transpose(0, 2, 1).reshape(bh, seq)
    ccol, crow = c_bht.reshape(bh, seq, 1), c_bht.reshape(bh, nqb, 1, seq // nqb)
    oh, lse = fox_fwd("l1_attn", qh, kh, vh, ccol, crow)
    oc = unheads(oh)
    mix1 = mm2d("l1_out", "nn", oc, w_out_o)
    x3, h3 = tile_fwd("l1_postnorm", fn_addnorm2, m=m, tm=tm, nj=1, rows=[Row(x2), Row(mix1)], pars=[Par(gain(1, 1)), Par(gain(1, 2))],
                      outs=[Out(D_MODEL, F32), Out(D_MODEL, BF16)])
    hid1, act1, f1 = _ffn_forward("l1_ffn", 1, h3, w_up_g, cw5, cb5, w_down_g, m, seq)
    dy, loss_part = tile_fwd("loss", fn_final, m=m, tm=tm, nj=1, rows=[Row(x3), Row(f1), Row(tgt)], pars=[Par(gain(1, 3))],
                             outs=[Out(D_MODEL, F32)], n_acc=1)

    df1, d_g13 = tile_bwd("l1_dffnnorm", fn_rms_only, m=m, tm=tm, nj=1, rows=[Row(f1)], pars=[Par(gain(1, 3))], cts=[Row(dy)],
                          drows=[Out(D_MODEL, BF16)])
    dh3, d_wup, d_cw1, d_cb1, d_wdown = _ffn_backward("l1_ffn", 1, df1, h3, hid1, act1, w_up_g, cw5, cb5, w_down_g, None, None, m, seq)
    dx2, dmix1, d_g11, d_g12 = tile_bwd("l1_dpostnorm", fn_addnorm2, m=m, tm=tm, nj=1, rows=[Row(x2), Row(mix1)],
                                        pars=[Par(gain(1, 1)), Par(gain(1, 2))], cts=[Row(dy), Row(dh3)],
                                        drows=[Out(D_MODEL, F32), Out(D_MODEL, BF16)])
    doc = mm2d("l1_doc", "nt", dmix1, w_out_o, BF16)
    d_wout_o = mm2d("l1_dwout", "tn", oc, dmix1)
    dq, dk, dv, drow, dcol = fox_bwd("l1_dattn", qh, kh, vh, oh, heads(doc), lse, ccol, crow)
    dc = (drow.reshape(bh, seq) - dcol.reshape(bh, seq)).reshape(n_batch, C_HEADS, seq).transpose(0, 2, 1).reshape(m, C_HEADS)
    dc = jnp.pad(dc, ((0, 0), (0, LANES - C_HEADS)))
    dzf, d_fbias = tile_bwd("l1_dgate", fn_fox_gate, m=m, tm=seq, nj=1, rows=[Row(z1, LANES, 3072 // LANES)], pars=[Par(fbias)],
                            cts=[Row(dc)], drows=[Out(LANES, BF16)])
    dz1 = jnp.concatenate([unheads(dq).astype(BF16), unheads(dk).astype(BF16), unheads(dv).astype(BF16), dzf], axis=-1)
    dh2 = mm2d("l1_dh", "nt", dz1, w_in_o, BF16)
    d_win_o = mm2d("l1_dwin", "tn", h2, dz1)

    dx1, df0, d_g03, d_g10 = tile_bwd("l0_dffnnorm", fn_addnorm2, m=m, tm=tm, nj=1, rows=[Row(x1), Row(f0)],
                                      pars=[Par(gain(0, 3)), Par(gain(1, 0))], cts=[Row(dx2), Row(dh2)],
                                      drows=[Out(D_MODEL, F32), Out(D_MODEL, BF16)])
    dh1, d_wup, d_cw0, d_cb0, d_wdown = _ffn_backward("l0_ffn", 0, df0, h1, hid0, act0, w_up_g, cw5, cb5, w_down_g, d_wup, d_wdown, m, seq)
    dx0a, dmix0, d_g01, d_g02 = tile_bwd("l0_dpostnorm", fn_addnorm2, m=m, tm=tm, nj=1, rows=[Row(x0), Row(mix0)],
                                         pars=[Par(gain(0, 1)), Par(gain(0, 2))], cts=[Row(dx1), Row(dh1)],
                                         drows=[Out(D_MODEL, F32), Out(D_MODEL, BF16)])
    dmixcat0 = mm2d("l0_dmixcat", "nt", dmix0, w_out_e, BF16)
    d_wout_e = mm2d("l0_dwout", "tn", mixcat0, dmix0)
    dzq, dzf0, dzv, dzg, d_lb, d_hnorm = hgrn_bwd("l0_dhgrn", z0, sprev, hgrn_lb_logits, hgrn_norm, dmixcat0, n_batch=n_batch, seq=seq)
    dzx, dzy, d_rcw, d_rcb, d_wa, d_ba, d_wx, d_bx, d_lam = tile_bwd(
        "l0_drglru", fn_rglru, m=m, tm=seq, nj=B_WIDTH // LANES, rows=rg_rows(), pars=rg_pars(),
        cts=[Row(dmixcat0, LANES, A_WIDTH // LANES)], drows=[Out(B_WIDTH, BF16, LANES), Out(B_WIDTH, BF16, LANES)])
    dz0 = jnp.concatenate([dzq, dzf0, dzv, dzg, dzx, dzy], axis=-1)
    dh0 = mm("l0_dh", "nt",
             Blk(dz0, (tmm, 384), lambda i, j, k: (i, k)),
             Blk(w_in_e, (None, None, D_MODEL, 384), lambda i, j, k: (k, 0, 0, 0)),
             Blk((m, D_MODEL), (tmm, D_MODEL), lambda i, j, k: (i, 0)), BF16, (nm, 1, N_DEV))
    d_win_e = mm("l0_dwin", "tn",
                 Blk(h0, (tmm, D_MODEL), lambda i, j, k: (k, 0)),
                 Blk(dz0, (tmm, 384), lambda i, j, k: (k, j)),
                 Blk((2, 4) + w_in_e.shape[1:], (None, None, None, D_MODEL, 384), lambda i, j, k: (j % 2, j // 2, 0, 0, 0)), BF16,
                 (1, N_DEV, nm))
    dx0, d_g00 = tile_bwd("l0_dprenorm", fn_input_norm, m=m, tm=tm, nj=1, rows=[Row(x0)], pars=[Par(gain(0, 0))],
                          cts=[Row(dx0a), Row(dh0)], drows=[Out(D_MODEL, F32)])

    d_gains = jnp.stack([jnp.concatenate([d_g00, d_g01, d_g02, d_g03], axis=0), jnp.concatenate([d_g10, d_g11, d_g12, d_g13], axis=0)])
    d_ffn_cw = jnp.stack([d_cw0, d_cw1], axis=2).reshape(N_DEV, n_layer, FFN_CONV, FF_BLK)
    by_core = lambda t: jnp.swapaxes(t.reshape((4, 2) + t.shape[1:]), 0, 1).astype(BF16)
    half = {
        "even_w_in": d_win_e,
        "even_w_out": by_core(d_wout_e.reshape(N_DEV, 1, D_MODEL // N_DEV, D_MODEL)),
        "odd_w_in": by_core(_cols_to_blocks(d_win_o[None, :, :3088])),
        "odd_w_out": by_core(d_wout_o.reshape(N_DEV, 1, D_MODEL // N_DEV, D_MODEL)),
        "ffn_w_up": d_wup,
        "ffn_w_down": d_wdown,
    }
    core = lax.axis_index("c").astype(jnp.int32).reshape(1)
    from_sibling = pair_exchange("exchange_core", [half[n] for n in BIG])
    chip_sums = [pair_add("add_" + n, half[n], r, core) for n, r in zip(BIG, from_sibling)]
    recv = dict(zip(BIG, quad_exchange("exchange_chips", chip_sums)))
    res = {n: adam_tiled("adam_" + n, recv[n], w[n], mom[n], var[n]) for n in BIG}
    small_send = [_cols_to_blocks(d_gains), _cols_to_blocks(d_rcw[None]), d_ffn_cw]
    recv.update(zip(SMALL_SHARDED, all_to_all("exchange_small", small_send)))

    d_ffn_cb = jnp.stack([d_cb0, d_cb1]).reshape(n_layer, 2 * D_FF)
    rep = {"hgrn_lb_logits": d_lb, "hgrn_norm": d_hnorm, "rg_conv_b": d_rcb, "rg_wa": _block_diag_grad(d_wa)[None], "rg_ba": d_ba,
           "rg_wx": _block_diag_grad(d_wx)[None], "rg_bx": d_bx, "rg_lambda": d_lam, "fox_f_bias": d_fbias[:, :C_HEADS],
           "ffn_conv_b": d_ffn_cb}
    parts = all_gather("gather_partials", [rep[n] for n in REPLICATED] + [loss_part])
    for n, p in zip(REPLICATED, parts):
        recv[n] = p
    small = SMALL_SHARDED + REPLICATED
    small_res, (loss_sum,) = adam_small("adam_small", [(recv[n], w[n], mom[n], var[n]) for n in small], [parts[-1]])
    res.update(dict(zip(small, small_res)))

    out = [loss_sum[0, 0], dx0.reshape(x.shape)]
    for k in range(4):
        out += [res[n][k] for n in NAMES]
    return tuple(out)
```

```python
import functools

import jax
import jax.numpy as jnp
from jax import lax
from jax.experimental import pallas as pl
from jax.experimental.pallas import tpu as pltpu

F32 = jnp.float32
BF16 = jnp.bfloat16

D_MODEL = 1024
A_HEADS = 4
A_WIDTH = 512
HGRN_CHUNK = 64
HGRN_SEG = 512
B_WIDTH = 512
B_BLOCKS = 8
B_BLOCK_DIM = 64
B_CONV = 4
RG_C = 8.0
C_HEADS = 16
C_HEAD_DIM = 64
D_FF = 2816
FFN_CONV = 3
EPS = 1e-6
LANES = 128
HALO = 16
N_DEV = 8
FF_BLK = 2 * D_FF // N_DEV
MESH = pl.DeviceIdType.MESH
NEG = -1e30
VMEM_LIMIT = 56 * 1024 * 1024

ADAM_LR = 0.001
ADAM_B1 = 0.9
ADAM_B2 = 0.999
ADAM_EPS = 1e-08
ADAM_WD = 0.01
ADAM_STEP = 10


def _dg(a, b, pat):
    nb = a.ndim - 2
    batch = (tuple(range(nb)), tuple(range(nb)))
    ca = a.ndim - 1 if pat[0] == "n" else a.ndim - 2
    cb = b.ndim - 2 if pat[1] == "n" else b.ndim - 1
    return lax.dot_general(a.astype(BF16), b.astype(BF16), (((ca,), (cb,)), batch), preferred_element_type=F32)


@functools.partial(jax.custom_vjp, nondiff_argnums=(2,))
def bdot(a, b, pat):
    return _dg(a, b, pat)


def _bdot_fwd(a, b, pat):
    return _dg(a, b, pat), (a, b)


def _bdot_bwd(pat, res, g):
    a, b = res
    if pat == "nn":
        return _dg(g, b, "nt"), _dg(a, g, "tn")
    if pat == "nt":
        return _dg(g, b, "nn"), _dg(g, a, "tn")
    return _dg(b, g, "nt"), _dg(a, g, "nn")


bdot.defvjp(_bdot_fwd, _bdot_bwd)


def _shift_raw(x, s, up, fill):
    if s == 0:
        return x
    n = x.shape[0]
    r = pltpu.roll(x, (n - s) if up else s, 0)
    idx = lax.broadcasted_iota(jnp.int32, x.shape, 0)
    mask = (idx >= n - s) if up else (idx < s)
    return jnp.where(mask, jnp.asarray(fill, x.dtype), r)


@functools.partial(jax.custom_vjp, nondiff_argnums=(1,))
def shift_down(x, s):
    return _shift_raw(x, s, False, 0.0)


def _shift_down_fwd(x, s):
    return _shift_raw(x, s, False, 0.0), None


def _shift_down_bwd(s, _, g):
    return (_shift_raw(g, s, True, 0.0),)


shift_down.defvjp(_shift_down_fwd, _shift_down_bwd)


def _scan_impl(a, u, up):
    n = a.shape[0]
    s = 1
    while s < n:
        u = a * _shift_raw(u, s, up, 0.0) + u
        if 2 * s < n:
            a = a * _shift_raw(a, s, up, 1.0)
        s *= 2
    return u


@jax.custom_vjp
def lin_scan(a, u):
    return _scan_impl(a, u, False)


def _lin_scan_fwd(a, u):
    h = _scan_impl(a, u, False)
    return h, (a, h)


def _lin_scan_bwd(res, g):
    a, h = res
    gh = _scan_impl(_shift_raw(a, 1, True, 0.0), g, True)
    return gh * _shift_raw(h, 1, False, 0.0), gh


lin_scan.defvjp(_lin_scan_fwd, _lin_scan_bwd)


def _cumsum_impl(x, up, period):
    n = x.shape[0]
    span = n if period is None else period
    idx = lax.broadcasted_iota(jnp.int32, x.shape, 0)
    pos = idx if period is None else idx % period
    s = 1
    while s < span:
        sh = _shift_raw(x, s, up, 0.0)
        if period is not None:
            keep = (pos < period - s) if up else (pos >= s)
            sh = jnp.where(keep, sh, 0.0)
        x = x + sh
        s *= 2
    return x


@functools.partial(jax.custom_vjp, nondiff_argnums=(1,))
def cumsum_rows(x, period):
    return _cumsum_impl(x, False, period)


def _cumsum_fwd(x, period):
    return _cumsum_impl(x, False, period), None


def _cumsum_bwd(period, _, g):
    return (_cumsum_impl(g, True, period),)


cumsum_rows.defvjp(_cumsum_fwd, _cumsum_bwd)


def _sigmoid(x):
    return jax.nn.sigmoid(x)


def _expm1(x):
    return jnp.tanh(0.5 * x) * (jnp.exp(x) + 1.0)


def _softplus(x):
    return jnp.maximum(x, 0.0) + jnp.log(1.0 + jnp.exp(-jnp.abs(x)))


def _rms(x, g):
    return x * lax.rsqrt(jnp.mean(x * x, axis=-1, keepdims=True) + EPS) * g


def fn_prenorm(x, g):
    return (_rms(x, g).astype(BF16),)


def fn_addnorm2(x, y, g_post, g_pre):
    x1 = x + _rms(y, g_post)
    return x1, _rms(x1, g_pre).astype(BF16)


def fn_input_norm(x, g):
    return x, _rms(x, g).astype(BF16)


def fn_final(x, y, tgt, g_post):
    out = x + _rms(y, g_post)
    err = out - tgt
    dy = err * (1.0 / D_MODEL)
    loss = 0.5 * jnp.sum(jnp.mean(err * err, axis=-1, keepdims=True), axis=0, keepdims=True)
    return dy, jnp.broadcast_to(loss, (1, LANES))


def fn_rms_only(y, g):
    return (_rms(y, g),)


def _causal_conv(x, w, b, taps):
    c = b
    for k in range(taps):
        c = c + w[k:k + 1, :] * shift_down(x, taps - 1 - k)
    return c


def fn_rglru(xb, yb, cw, cb, wa, ba, wx, bx, lam):
    xf = _causal_conv(xb, cw, cb, B_CONV)
    r = _sigmoid(bdot(xf, wa, "nn") + ba)
    i = _sigmoid(bdot(xf, wx, "nn") + bx)
    log_a = -RG_C * r * _softplus(-lam)
    a = jnp.exp(log_a)
    u = jnp.sqrt(-_expm1(2.0 * log_a)) * (i * xf)
    h = lin_scan(a, u)
    return ((h * jax.nn.gelu(yb)).astype(BF16),)


def fn_fox_gate(zf, bias):
    return (cumsum_rows(jax.nn.log_sigmoid(zf + bias), None),)


def fn_hgrn_seg(q, fl, v, g, st, logits, hn):
    rows = q.shape[0]
    nc = rows // HGRN_CHUNK
    l0, l1, l2 = logits[0:1, :], logits[1:2, :], logits[2:3, :]
    mx = jnp.maximum(jnp.maximum(l0, l1), l2)
    e0, e1, e2 = jnp.exp(l0 - mx), jnp.exp(l1 - mx), jnp.exp(l2 - mx)
    lb = e0 / (e0 + e1 + e2)
    forget = lb + (1.0 - lb) * _sigmoid(fl)
    qs = q * _sigmoid(q)
    kk = 1.0 - forget
    logf = jnp.log(forget)
    bcum = cumsum_rows(logf, HGRN_CHUNK)
    c3 = lambda t: t.reshape(nc, HGRN_CHUNK, 128)
    b_last = jnp.sum(c3(logf), axis=1, keepdims=True)
    bcum3 = c3(bcum)
    q_dec = c3(qs) * jnp.exp(bcum3)
    k_dec = c3(kk) * jnp.exp(-bcum3)
    k_upd = c3(kk) * jnp.exp(b_last - bcum3)
    v3 = c3(v)
    scores = bdot(q_dec, k_dec, "nt")
    ri = lax.broadcasted_iota(jnp.int32, scores.shape, 1)
    ci = lax.broadcasted_iota(jnp.int32, scores.shape, 2)
    scores = jnp.where(ri >= ci, scores, 0.0)
    o = bdot(scores, v3, "nn")
    upd_t = bdot(v3, k_upd, "tn")
    dec = jnp.exp(b_last)
    prev = []
    for n in range(nc):
        prev.append(st)
        st = st * dec[n] + upd_t[n]
    o = o + bdot(q_dec, jnp.stack(prev), "nt")
    o = o.reshape(rows, 128)
    o = o * lax.rsqrt(jnp.mean(o * o, axis=-1, keepdims=True) + EPS) * hn
    return (o * _sigmoid(g)).astype(BF16), st


def _ffn_conv(xg, xv, cw, cb):
    cg = _causal_conv(xg, cw[0], cb[0], FFN_CONV)[HALO:]
    cv = _causal_conv(xv, cw[1], cb[1], FFN_CONV)[HALO:]
    return cg, cv


def _ffn_gate(cg, cv):
    return jax.nn.gelu(cg) * cv


class Row:
    def __init__(self, arr, cb=None, off=0):
        self.arr, self.cb, self.off = arr, cb, off

    def spec(self, tm):
        if self.cb is None:
            return pl.BlockSpec((tm, self.arr.shape[1]), lambda j, i: (i, 0))
        off = self.off
        return pl.BlockSpec((tm, self.cb), lambda j, i: (i, j + off))


class Par:
    def __init__(self, arr, kind="full", bs=None):
        self.arr, self.kind, self.bs = arr, kind, bs

    def block(self):
        if self.kind == "full":
            return self.arr.shape
        if self.kind == "col":
            return (self.arr.shape[0], self.bs)
        return (self.bs, self.arr.shape[1])

    def spec(self):
        if self.kind == "full":
            return pl.BlockSpec(self.block(), lambda j, i: (0, 0))
        if self.kind == "col":
            return pl.BlockSpec(self.block(), lambda j, i: (0, j))
        return pl.BlockSpec(self.block(), lambda j, i: (j, 0))


class Out:
    def __init__(self, width, dtype, cb=None, off=0):
        self.width, self.dtype, self.cb, self.off = width, dtype, cb, off

    def spec(self, tm):
        if self.cb is None:
            return pl.BlockSpec((tm, self.width), lambda j, i: (i, 0))
        off = self.off
        return pl.BlockSpec((tm, self.cb), lambda j, i: (i, j + off))


def _params(sem):
    return pltpu.CompilerParams(dimension_semantics=sem, vmem_limit_bytes=VMEM_LIMIT)


def tile_fwd(name, fn, *, m, tm, nj, rows, pars, outs, n_acc=0):
    n_r, n_p, n_o = len(rows), len(pars), len(outs)

    def body(*refs):
        ins = [r[...] for r in refs[:n_r + n_p]]
        res = fn(*ins)
        o_refs = refs[n_r + n_p:]
        for k in range(n_o):
            o_refs[k][...] = res[k].astype(o_refs[k].dtype)
        first = jnp.logical_and(pl.program_id(0) == 0, pl.program_id(1) == 0)
        for k in range(n_acc):
            ref = o_refs[n_o + k]

            @pl.when(first)
            def _():
                ref[...] = jnp.zeros_like(ref)

            ref[...] += res[n_o + k]

    out_shape = [jax.ShapeDtypeStruct((m, o.width), o.dtype) for o in outs]
    out_specs = [o.spec(tm) for o in outs]
    for _ in range(n_acc):
        out_shape.append(jax.ShapeDtypeStruct((1, LANES), F32))
        out_specs.append(pl.BlockSpec((1, LANES), lambda j, i: (0, 0)))
    sem = ("arbitrary", "arbitrary") if n_acc else ("parallel", "parallel")
    return pl.pallas_call(
        body, grid=(nj, m // tm), name=name,
        in_specs=[r.spec(tm) for r in rows] + [p.spec() for p in pars],
        out_specs=out_specs, out_shape=out_shape, compiler_params=_params(sem),
    )(*[r.arr for r in rows], *[p.arr for p in pars])


def tile_bwd(name, fn, *, m, tm, nj, rows, pars, cts, drows):
    n_r, n_p, n_c = len(rows), len(pars), len(cts)
    want = [k for k in range(n_r) if drows[k] is not None]

    def body(*refs):
        ins = [r[...] for r in refs[:n_r + n_p]]
        ct = [r[...] for r in refs[n_r + n_p:n_r + n_p + n_c]]
        o_refs = refs[n_r + n_p + n_c:]
        res, vjp = jax.vjp(fn, *ins)
        grads = vjp(tuple(c.astype(r.dtype) for c, r in zip(ct, res)))
        for pos, k in enumerate(want):
            o_refs[pos][...] = grads[k].astype(o_refs[pos].dtype)
        for k in range(n_p):
            ref = o_refs[len(want) + k]
            first = pl.program_id(1) == 0
            if pars[k].kind == "full":
                first = jnp.logical_and(first, pl.program_id(0) == 0)

            @pl.when(first)
            def _():
                ref[...] = jnp.zeros_like(ref)

            ref[...] += grads[n_r + k].astype(F32)

    out_shape = [jax.ShapeDtypeStruct((m, drows[k].width), drows[k].dtype) for k in want]
    out_specs = [drows[k].spec(tm) for k in want]
    for p in pars:
        out_shape.append(jax.ShapeDtypeStruct(p.arr.shape, F32))
        out_specs.append(p.spec())
    return pl.pallas_call(
        body, grid=(nj, m // tm), name=name,
        in_specs=[r.spec(tm) for r in rows] + [p.spec() for p in pars] + [c.spec(tm) for c in cts],
        out_specs=out_specs, out_shape=out_shape, compiler_params=_params(("arbitrary", "arbitrary")),
    )(*[r.arr for r in rows], *[p.arr for p in pars], *[c.arr for c in cts])


class Blk:
    def __init__(self, arr, block, index):
        self.arr, self.block, self.index = arr, block, index

    def spec(self):
        return pl.BlockSpec(self.block, self.index)


def _flat2(v):
    return v if v.ndim == 2 else v.reshape(-1, v.shape[-1])


def mm(name, pat, a, b, o, out_dtype, grid, into=None):
    nk = grid[2]
    o_shape = o.arr

    def body(*refs):
        a_ref, b_ref = refs[0], refs[1]
        o_ref = refs[3] if into is not None else refs[2]
        r = _dg(_flat2(a_ref[...]), _flat2(b_ref[...]), pat)
        if nk == 1:
            o_ref[...] = r.astype(out_dtype).reshape(o_ref.shape)
            return
        acc_ref = refs[-1]
        kk = pl.program_id(2)

        @pl.when(kk == 0)
        def _():
            acc_ref[...] = r

        @pl.when(kk > 0)
        def _():
            acc_ref[...] += r

        @pl.when(kk == nk - 1)
        def _():
            o_ref[...] = acc_ref[...].astype(out_dtype).reshape(o_ref.shape)

    ob = [d for d in o.block if d is not None]
    acc_shape = (ob[0], ob[1]) if len(ob) == 2 else (ob[0] * ob[1], ob[2])
    in_specs = [a.spec(), b.spec()]
    args = [a.arr, b.arr]
    aliases = {}
    if into is not None:
        in_specs.append(pl.BlockSpec(memory_space=pl.ANY))
        args.append(into)
        aliases = {2: 0}
    return pl.pallas_call(
        body, grid=grid, name=name, in_specs=in_specs, out_specs=o.spec(),
        out_shape=jax.ShapeDtypeStruct(o_shape, out_dtype),
        scratch_shapes=[pltpu.VMEM(acc_shape, F32)] if nk > 1 else [],
        input_output_aliases=aliases,
        compiler_params=_params(("parallel", "parallel", "arbitrary")),
    )(*args)


def _div_tile(n, cap):
    if n <= cap:
        return n
    best = 128
    for t in range(128, cap + 1, 128):
        if n % t == 0:
            best = t
    return best


def mm2d(name, pat, a, b, out_dtype=F32):
    if pat == "tn":
        k, m = a.shape
    else:
        m, k = a.shape
    n = b.shape[0] if pat == "nt" else b.shape[1]
    tm, tn, tk = _div_tile(m, 1024), _div_tile(n, 1024), _div_tile(k, 1024)
    a_blk = Blk(a, (tk, tm), lambda i, j, kk: (kk, i)) if pat == "tn" else Blk(a, (tm, tk), lambda i, j, kk: (i, kk))
    b_blk = Blk(b, (tn, tk), lambda i, j, kk: (j, kk)) if pat == "nt" else Blk(b, (tk, tn), lambda i, j, kk: (kk, j))
    o_blk = Blk((m, n), (tm, tn), lambda i, j, kk: (i, j))
    return mm(name, pat, a_blk, b_blk, o_blk, out_dtype, (m // tm, n // tn, k // tk))


def hgrn_fwd(name, z, logits, hnorm, *, n_batch, seq):
    m = n_batch * seq
    ts = min(HGRN_SEG, seq)
    n_seg = seq // ts

    def body(q_ref, f_ref, v_ref, g_ref, lg_ref, hn_ref, o_ref, sp_ref, st_ref):
        s = pl.program_id(2)

        @pl.when(s == 0)
        def _():
            st_ref[...] = jnp.zeros_like(st_ref)

        st = st_ref[...]
        sp_ref[...] = st
        o, st_new = fn_hgrn_seg(q_ref[...], f_ref[...], v_ref[...], g_ref[...], st, lg_ref[...], hn_ref[...])
        o_ref[...] = o
        st_ref[...] = st_new

    part = lambda p: pl.BlockSpec((ts, 128), lambda h, b, s: (b * n_seg + s, 4 * p + h))
    return pl.pallas_call(
        body, grid=(A_HEADS, n_batch, n_seg), name=name,
        in_specs=[part(0), part(1), part(2), part(3),
                  pl.BlockSpec((3, 128), lambda h, b, s: (0, h)),
                  pl.BlockSpec((1, 128), lambda h, b, s: (0, h))],
        out_specs=[pl.BlockSpec((ts, 128), lambda h, b, s: (b * n_seg + s, h)),
                   pl.BlockSpec((128, 128), lambda h, b, s: ((b * n_seg + s) * A_HEADS + h, 0))],
        out_shape=[jax.ShapeDtypeStruct((m, A_WIDTH), BF16),
                   jax.ShapeDtypeStruct((n_batch * n_seg * A_HEADS * 128, 128), F32)],
        scratch_shapes=[pltpu.VMEM((128, 128), F32)],
        compiler_params=_params(("arbitrary", "arbitrary", "arbitrary")),
    )(z, z, z, z, logits, hnorm)


def hgrn_bwd(name, z, sprev, logits, hnorm, do, *, n_batch, seq):
    m = n_batch * seq
    ts = min(HGRN_SEG, seq)
    n_seg = seq // ts

    def body(q_ref, f_ref, v_ref, g_ref, sp_ref, lg_ref, hn_ref, do_ref, dq_ref, df_ref, dv_ref, dg_ref, dlg_ref, dhn_ref, dst_ref):
        s = pl.program_id(2)

        @pl.when(s == 0)
        def _():
            dst_ref[...] = jnp.zeros_like(dst_ref)

        res, vjp = jax.vjp(fn_hgrn_seg, q_ref[...], f_ref[...], v_ref[...], g_ref[...], sp_ref[...], lg_ref[...], hn_ref[...])
        dq, df, dv, dg, dst, dlg, dhn = vjp((do_ref[...].astype(res[0].dtype), dst_ref[...]))
        dq_ref[...] = dq.astype(dq_ref.dtype)
        df_ref[...] = df.astype(df_ref.dtype)
        dv_ref[...] = dv.astype(dv_ref.dtype)
        dg_ref[...] = dg.astype(dg_ref.dtype)
        dst_ref[...] = dst
        first = jnp.logical_and(pl.program_id(1) == 0, s == 0)

        @pl.when(first)
        def _():
            dlg_ref[...] = jnp.zeros_like(dlg_ref)
            dhn_ref[...] = jnp.zeros_like(dhn_ref)

        dlg_ref[...] += dlg
        dhn_ref[...] += dhn

    rev = lambda b, s: b * n_seg + (n_seg - 1 - s)
    part = lambda p: pl.BlockSpec((ts, 128), lambda h, b, s: (rev(b, s), 4 * p + h))
    head = pl.BlockSpec((ts, 128), lambda h, b, s: (rev(b, s), h))
    dpart = jax.ShapeDtypeStruct((m, A_WIDTH), BF16)
    return pl.pallas_call(
        body, grid=(A_HEADS, n_batch, n_seg), name=name,
        in_specs=[part(0), part(1), part(2), part(3),
                  pl.BlockSpec((128, 128), lambda h, b, s: (rev(b, s) * A_HEADS + h, 0)),
                  pl.BlockSpec((3, 128), lambda h, b, s: (0, h)),
                  pl.BlockSpec((1, 128), lambda h, b, s: (0, h)),
                  head],
        out_specs=[head, head, head, head,
                   pl.BlockSpec((3, 128), lambda h, b, s: (0, h)),
                   pl.BlockSpec((1, 128), lambda h, b, s: (0, h))],
        out_shape=[dpart, dpart, dpart, dpart,
                   jax.ShapeDtypeStruct(logits.shape, F32),
                   jax.ShapeDtypeStruct(hnorm.shape, F32)],
        scratch_shapes=[pltpu.VMEM((128, 128), F32)],
        compiler_params=_params(("arbitrary", "arbitrary", "arbitrary")),
    )(z, z, z, z, sprev, logits, hnorm, do)


def _ffn_tiles(m, seq):
    tm = min(512, seq)
    return tm, seq // tm, m // tm


def ffn_mid_fwd(name, hid, cw, cb, layer, *, m, seq):
    tm, n_t, n_i = _ffn_tiles(m, seq)
    hb = tm // HALO

    def body(x_ref, xb_ref, cw_ref, cb_ref, o_ref):
        first = pl.program_id(1) % n_t == 0
        before = jnp.where(first, 0.0, xb_ref[...])
        ext = jnp.concatenate([before, x_ref[...]], axis=1)
        cg, cv = _ffn_conv(ext[0], ext[1], cw_ref[...], cb_ref[...])
        o_ref[...] = _ffn_gate(cg, cv).astype(o_ref.dtype)

    return pl.pallas_call(
        body, grid=(N_DEV // 2, n_i), name=name,
        in_specs=[pl.BlockSpec((2, None, tm, FF_BLK), lambda d, i: (0, d, i, 0)),
                  pl.BlockSpec((2, None, HALO, FF_BLK), lambda d, i: (0, d, jnp.maximum(i * hb - 1, 0), 0)),
                  pl.BlockSpec((2, None, None, FFN_CONV, FF_BLK), lambda d, i: (0, d, layer, 0, 0)),
                  pl.BlockSpec((None, 2, None, 1, FF_BLK), lambda d, i: (layer, 0, d, 0, 0))],
        out_specs=pl.BlockSpec((None, tm, FF_BLK), lambda d, i: (d, i, 0)),
        out_shape=jax.ShapeDtypeStruct((N_DEV // 2, m, FF_BLK), BF16),
        compiler_params=_params(("parallel", "parallel")),
    )(hid, hid, cw, cb)


def ffn_mid_bwd(name, hid, cw, cb, dact, layer, *, m, seq):
    tm, n_t, n_i = _ffn_tiles(m, seq)
    hb = tm // HALO
    last_blk = m // HALO - 1

    def body(x_ref, xb_ref, xa_ref, cw_ref, cb_ref, da_ref, daa_ref, dx_ref, dcw_ref, dcb_ref):
        i = pl.program_id(1)
        first = i % n_t == 0
        last = i % n_t == n_t - 1
        before = jnp.where(first, 0.0, xb_ref[...])
        ext = jnp.concatenate([before, x_ref[...], xa_ref[...]], axis=1)
        dact_ext = jnp.concatenate([da_ref[...].astype(F32), jnp.where(last, 0.0, daa_ref[...].astype(F32))], axis=0)
        (cg, cv), vjp_conv = jax.vjp(_ffn_conv, ext[0], ext[1], cw_ref[...], cb_ref[...])
        _, vjp_gate = jax.vjp(_ffn_gate, cg, cv)
        dcg, dcv = vjp_gate(dact_ext)
        dxg, dxv, _, _ = vjp_conv((dcg, dcv))
        dx_ref[0] = dxg[HALO:HALO + tm].astype(dx_ref.dtype)
        dx_ref[1] = dxv[HALO:HALO + tm].astype(dx_ref.dtype)
        own = lax.broadcasted_iota(jnp.int32, dcg.shape, 0) < tm
        _, _, dcw, dcb = vjp_conv((jnp.where(own, dcg, 0.0), jnp.where(own, dcv, 0.0)))

        @pl.when(i == 0)
        def _():
            dcw_ref[...] = jnp.zeros_like(dcw_ref)
            dcb_ref[...] = jnp.zeros_like(dcb_ref)

        dcw_ref[...] += dcw
        dcb_ref[...] += dcb

    return pl.pallas_call(
        body, grid=(N_DEV // 2, n_i), name=name,
        in_specs=[pl.BlockSpec((2, None, tm, FF_BLK), lambda d, i: (0, d, i, 0)),
                  pl.BlockSpec((2, None, HALO, FF_BLK), lambda d, i: (0, d, jnp.maximum(i * hb - 1, 0), 0)),
                  pl.BlockSpec((2, None, HALO, FF_BLK), lambda d, i: (0, d, jnp.minimum((i + 1) * hb, last_blk), 0)),
                  pl.BlockSpec((2, None, None, FFN_CONV, FF_BLK), lambda d, i: (0, d, layer, 0, 0)),
                  pl.BlockSpec((None, 2, None, 1, FF_BLK), lambda d, i: (layer, 0, d, 0, 0)),
                  pl.BlockSpec((None, tm, FF_BLK), lambda d, i: (d, i, 0)),
                  pl.BlockSpec((None, HALO, FF_BLK), lambda d, i: (d, jnp.minimum((i + 1) * hb, last_blk), 0))],
        out_specs=[pl.BlockSpec((2, None, tm, FF_BLK), lambda d, i: (0, d, i, 0)),
                   pl.BlockSpec((2, None, FFN_CONV, FF_BLK), lambda d, i: (0, d, 0, 0)),
                   pl.BlockSpec((2, None, 1, FF_BLK), lambda d, i: (0, d, 0, 0))],
        out_shape=[jax.ShapeDtypeStruct((2, N_DEV // 2, m, FF_BLK), BF16),
                   jax.ShapeDtypeStruct((2, N_DEV // 2, FFN_CONV, FF_BLK), F32),
                   jax.ShapeDtypeStruct((2, N_DEV // 2, 1, FF_BLK), F32)],
        compiler_params=_params(("arbitrary", "arbitrary")),
    )(hid, hid, hid, cw, cb, dact, dact)


ATT_BLK = 512


def _split3(c):
    c1 = c.astype(BF16)
    r1 = c - c1.astype(F32)
    c2 = r1.astype(BF16)
    c3 = (r1 - c2.astype(F32)).astype(BF16)
    return c1, c2, c3


def fox_operands(q, k, c):
    bh, seq, dh = q.shape
    c1, c2, c3 = (t[..., None] for t in _split3(c))
    one = jnp.ones((bh, seq, 1), BF16)
    pad = jnp.zeros((bh, seq, LANES - dh - 6), BF16)
    qa = jnp.concatenate([(q * dh ** -0.5).astype(BF16), c1, c2, c3, one, one, one, pad], axis=-1)
    ka = jnp.concatenate([k.astype(BF16), one, one, one, -c1, -c2, -c3, pad], axis=-1)
    return qa, ka


def fox_fwd(name, qa, ka, vt):
    bh, seq, da = qa.shape
    blk = min(ATT_BLK, seq)
    nq = seq // blk
    dh = vt.shape[2]

    def body(q_ref, k_ref, v_ref, o_ref, lse_ref):
        qi = pl.program_id(1)
        qv = q_ref[0]

        def block(j, carry, diagonal):
            mx, l, acc = carry
            kj = k_ref[0, pl.ds(pl.multiple_of(j * blk, blk), blk), :]
            s = _dg(kj, qv, "nt")
            if diagonal:
                key = lax.broadcasted_iota(jnp.int32, (blk, blk), 0)
                qry = lax.broadcasted_iota(jnp.int32, (blk, blk), 1)
                s = jnp.where(qry >= key, s, NEG)
            mx_new = jnp.maximum(mx, jnp.max(s, axis=0, keepdims=True))
            p = jnp.exp(s - mx_new)
            alpha = jnp.exp(mx - mx_new)
            l = alpha * l + jnp.sum(p, axis=0, keepdims=True)
            acc = alpha * acc + _dg(v_ref[0, j], p, "nn")
            return mx_new, l, acc

        init = (jnp.full((1, blk), NEG, F32), jnp.zeros((1, blk), F32), jnp.zeros((dh, blk), F32))
        carry = lax.fori_loop(0, qi, lambda j, cr: block(j, cr, False), init)
        mx, l, acc = block(qi, carry, True)
        o_ref[0] = (acc / l).astype(o_ref.dtype)
        lse_ref[0, 0] = mx + jnp.log(l)

    return pl.pallas_call(
        body, grid=(bh, nq), name=name,
        in_specs=[pl.BlockSpec((1, blk, da), lambda b, i: (b, i, 0)),
                  pl.BlockSpec((1, seq, da), lambda b, i: (b, 0, 0)),
                  pl.BlockSpec((1, nq, dh, blk), lambda b, i: (b, 0, 0, 0))],
        out_specs=[pl.BlockSpec((1, dh, blk), lambda b, i: (b, 0, i)),
                   pl.BlockSpec((1, 1, 1, blk), lambda b, i: (b, i, 0, 0))],
        out_shape=[jax.ShapeDtypeStruct((bh, dh, seq), BF16), jax.ShapeDtypeStruct((bh, nq, 1, blk), F32)],
        compiler_params=_params(("parallel", "arbitrary")),
    )(qa, ka, vt)


def fox_bwd(name, qa, ka, kat, v, do, dot, ot, lse):
    bh, seq, da = qa.shape
    blk = min(ATT_BLK, seq)
    nq = seq // blk
    dh = v.shape[2]

    def body(q_ref, k_ref, kt_ref, v_ref, do_ref, dot_ref, ot_ref, lse_ref, dq_ref, dk_ref, dv_ref, del_ref):
        j = pl.program_id(1)

        @pl.when(j == 0)
        def _():
            dq_ref[...] = jnp.zeros_like(dq_ref)
            for i in range(nq):
                cols = slice(i * blk, (i + 1) * blk)
                del_ref[i] = jnp.sum(dot_ref[0, :, cols].astype(F32) * ot_ref[0, :, cols].astype(F32), axis=0, keepdims=True)

        kj, kjt, vj = k_ref[0], kt_ref[0, 0], v_ref[0]

        def block(i, carry, diagonal):
            dk, dv = carry
            rows = pl.ds(pl.multiple_of(i * blk, blk), blk)
            qv, dov = q_ref[0, rows, :], do_ref[0, rows, :]
            p = jnp.exp(_dg(kj, qv, "nt") - lse_ref[0, i])
            if diagonal:
                key = lax.broadcasted_iota(jnp.int32, (blk, blk), 0)
                qry = lax.broadcasted_iota(jnp.int32, (blk, blk), 1)
                p = jnp.where(qry >= key, p, 0.0)
            dv = dv + _dg(p, dov, "nn")
            ds = p * (_dg(vj, dov, "nt") - del_ref[i])
            dk = dk + _dg(ds, qv, "nn")
            dq_ref[0, i] += _dg(kjt, ds, "nn")
            return dk, dv

        init = (jnp.zeros((blk, da), F32), jnp.zeros((blk, dh), F32))
        carry = block(j, init, True)
        dk, dv = lax.fori_loop(j + 1, nq, lambda i, cr: block(i, cr, False), carry)
        dk_ref[0] = dk
        dv_ref[0] = dv

    return pl.pallas_call(
        body, grid=(bh, nq), name=name,
        in_specs=[pl.BlockSpec((1, seq, da), lambda b, j: (b, 0, 0)),
                  pl.BlockSpec((1, blk, da), lambda b, j: (b, j, 0)),
                  pl.BlockSpec((1, 1, da, blk), lambda b, j: (b, j, 0, 0)),
                  pl.BlockSpec((1, blk, dh), lambda b, j: (b, j, 0)),
                  pl.BlockSpec((1, seq, dh), lambda b, j: (b, 0, 0)),
                  pl.BlockSpec((1, dh, seq), lambda b, j: (b, 0, 0)),
                  pl.BlockSpec((1, dh, seq), lambda b, j: (b, 0, 0)),
                  pl.BlockSpec((1, nq, 1, blk), lambda b, j: (b, 0, 0, 0))],
        out_specs=[pl.BlockSpec((1, nq, da, blk), lambda b, j: (b, 0, 0, 0)),
                   pl.BlockSpec((1, blk, da), lambda b, j: (b, j, 0)),
                   pl.BlockSpec((1, blk, dh), lambda b, j: (b, j, 0))],
        out_shape=[jax.ShapeDtypeStruct((bh, nq, da, blk), F32), jax.ShapeDtypeStruct((bh, seq, da), F32),
                   jax.ShapeDtypeStruct((bh, seq, dh), F32)],
        scratch_shapes=[pltpu.VMEM((nq, 1, blk), F32)],
        compiler_params=_params(("parallel", "arbitrary")),
    )(qa, ka, kat, v, do, dot, ot, lse)


def _mesh_pos():
    return lax.axis_index("x"), lax.axis_index("y"), lax.axis_index("c")


def _flip(v, bit):
    return 1 - v if bit else v


def all_gather(name, blocks):
    n = len(blocks)

    def body(*refs):
        x_refs, out_refs = refs[:n], refs[n:2 * n]
        send_sems, recv_sems, local_sems = refs[2 * n:]
        x, y, c = _mesh_pos()
        me, sibling = (x, y, c), (x, y, 1 - c)
        chips = [(1 - x, y), (x, 1 - y), (1 - x, 1 - y)]

        def slot(a, px, py, pc):
            return out_refs[a].at[4 * px + 2 * py + pc]

        def copy(a, k, blk, to, src=None):
            return pltpu.make_async_remote_copy(
                src_ref=slot(a, *blk) if src is None else src, dst_ref=slot(a, *blk),
                send_sem=send_sems.at[a, k], recv_sem=recv_sems.at[a, k], device_id=to, device_id_type=MESH)

        mine = [pltpu.make_async_copy(x_refs[a], slot(a, *me), local_sems.at[a]) for a in range(n)]
        for cp in mine:
            cp.start()
        sends = []
        for a in range(n):
            sends.append(copy(a, 0, me, sibling, src=x_refs[a]))
            sends += [copy(a, 1 + j, me, (*chip, c), src=x_refs[a]) for j, chip in enumerate(chips)]
        for cp in sends:
            cp.start()
        for j, chip in enumerate(chips):
            for a in range(n):
                copy(a, 1 + j, (*chip, c), me).wait_recv()
                passed = copy(a, 4 + j, (*chip, c), sibling)
                passed.start()
                sends.append(passed)
        for a in range(n):
            copy(a, 0, sibling, me).wait_recv()
            for j, chip in enumerate(chips):
                copy(a, 4 + j, (*chip, 1 - c), me).wait_recv()
        for cp in sends:
            cp.wait_send()
        for cp in mine:
            cp.wait()

    hbm = pl.BlockSpec(memory_space=pl.ANY)
    return pl.pallas_call(
        body, name=name, out_shape=[jax.ShapeDtypeStruct((N_DEV,) + b.shape, b.dtype) for b in blocks],
        in_specs=[hbm] * n, out_specs=[hbm] * n,
        scratch_shapes=[pltpu.SemaphoreType.DMA((n, 7)), pltpu.SemaphoreType.DMA((n, 7)), pltpu.SemaphoreType.DMA((n,))],
    )(*blocks)


def all_to_all(name, sends):
    n = len(sends)

    def body(*refs):
        s_refs, r_refs = refs[:n], refs[n:2 * n]
        send_sems, recv_sems, local_sems = refs[2 * n:]
        x, y, c = _mesh_pos()
        me = 4 * x + 2 * y + c
        mine = [pltpu.make_async_copy(s_refs[a].at[me], r_refs[a].at[me], local_sems.at[a]) for a in range(n)]
        for cp in mine:
            cp.start()
        copies = []
        for k in range(1, N_DEV):
            px, py, pc = _flip(x, k & 4), _flip(y, k & 2), _flip(c, k & 1)
            for a in range(n):
                copies.append(pltpu.make_async_remote_copy(
                    src_ref=s_refs[a].at[4 * px + 2 * py + pc], dst_ref=r_refs[a].at[me],
                    send_sem=send_sems.at[a, k - 1], recv_sem=recv_sems.at[a, k - 1],
                    device_id=(px, py, pc), device_id_type=MESH))
        for cp in copies:
            cp.start()
        for cp in copies:
            cp.wait_recv()
        for cp in copies:
            cp.wait_send()
        for cp in mine:
            cp.wait()

    hbm = pl.BlockSpec(memory_space=pl.ANY)
    return pl.pallas_call(
        body, name=name, out_shape=[jax.ShapeDtypeStruct(s.shape, s.dtype) for s in sends],
        in_specs=[hbm] * n, out_specs=[hbm] * n,
        scratch_shapes=[pltpu.SemaphoreType.DMA((n, 7)), pltpu.SemaphoreType.DMA((n, 7)), pltpu.SemaphoreType.DMA((n,))],
    )(*sends)


def pair_exchange(name, hs):
    n = len(hs)

    def body(*refs):
        h_refs, r_refs = refs[:n], refs[n:2 * n]
        send_sems, recv_sems = refs[2 * n:]
        x, y, c = _mesh_pos()
        copies = [pltpu.make_async_remote_copy(
            src_ref=h_refs[a].at[1 - c], dst_ref=r_refs[a], send_sem=send_sems.at[a], recv_sem=recv_sems.at[a],
            device_id=(x, y, 1 - c), device_id_type=MESH) for a in range(n)]
        for cp in copies:
            cp.start()
        for cp in copies:
            cp.wait_recv()
        for cp in copies:
            cp.wait_send()

    hbm = pl.BlockSpec(memory_space=pl.ANY)
    return pl.pallas_call(
        body, name=name, out_shape=[jax.ShapeDtypeStruct(h.shape[1:], h.dtype) for h in hs],
        in_specs=[hbm] * n, out_specs=[hbm] * n,
        scratch_shapes=[pltpu.SemaphoreType.DMA((n,)), pltpu.SemaphoreType.DMA((n,))],
    )(*hs)


def quad_exchange(name, ss):
    n = len(ss)

    def body(*refs):
        s_refs, r_refs = refs[:n], refs[n:2 * n]
        send_sems, recv_sems, local_sems = refs[2 * n:]
        x, y, c = _mesh_pos()
        me = 2 * x + y
        mine = [pltpu.make_async_copy(s_refs[a].at[me], r_refs[a].at[me], local_sems.at[a]) for a in range(n)]
        for cp in mine:
            cp.start()
        copies = []
        for k in range(1, 4):
            px, py = _flip(x, k & 2), _flip(y, k & 1)
            for a in range(n):
                copies.append(pltpu.make_async_remote_copy(
                    src_ref=s_refs[a].at[2 * px + py], dst_ref=r_refs[a].at[me],
                    send_sem=send_sems.at[a, k - 1], recv_sem=recv_sems.at[a, k - 1],
                    device_id=(px, py, c), device_id_type=MESH))
        for cp in copies:
            cp.start()
        for cp in copies:
            cp.wait_recv()
        for cp in copies:
            cp.wait_send()
        for cp in mine:
            cp.wait()

    hbm = pl.BlockSpec(memory_space=pl.ANY)
    return pl.pallas_call(
        body, name=name, out_shape=[jax.ShapeDtypeStruct(s.shape, s.dtype) for s in ss],
        in_specs=[hbm] * n, out_specs=[hbm] * n,
        scratch_shapes=[pltpu.SemaphoreType.DMA((n, 3)), pltpu.SemaphoreType.DMA((n, 3)), pltpu.SemaphoreType.DMA((n,))],
    )(*ss)


def _rows_cols(shape):
    r = 1
    for d in shape[:-1]:
        r *= d
    return r, shape[-1]


def _row_tile(r, cap, step):
    return next((t for t in range(cap, step - 1, -step) if r % t == 0), r)


def pair_add(name, h, recv, core):
    shape = recv.shape
    r, c = _rows_cols(shape[1:])
    tr = _row_tile(r, 256, 16)

    def body(core_ref, h_ref, r_ref, o_ref):
        o_ref[...] = (h_ref[...].astype(F32) + r_ref[...].astype(F32)).astype(o_ref.dtype)

    spec = pl.BlockSpec((None, tr, c), lambda q, i, core_ref: (q, i, 0))
    res = pl.pallas_call(
        body, name=name, out_shape=jax.ShapeDtypeStruct((4, r, c), h.dtype),
        grid_spec=pltpu.PrefetchScalarGridSpec(
            num_scalar_prefetch=1, grid=(4, r // tr),
            in_specs=[pl.BlockSpec((None, None, tr, c), lambda q, i, core_ref: (core_ref[0], q, i, 0)), spec],
            out_specs=spec),
        compiler_params=_params(("parallel", "parallel")),
    )(core, h.reshape(2, 4, r, c), recv.reshape(4, r, c))
    return res.reshape(shape)


def _sum_parts(p, n):
    t = [p[k].astype(F32) for k in range(n)]
    while len(t) > 1:
        t = [t[k] + t[k + 1] for k in range(0, len(t), 2)]
    return t[0]


def _adam(g, w, m, v):
    m = ADAM_B1 * m + (1.0 - ADAM_B1) * g
    v = ADAM_B2 * v + (1.0 - ADAM_B2) * (g * g)
    m_hat = m / (1.0 - ADAM_B1 ** ADAM_STEP)
    v_hat = v / (1.0 - ADAM_B2 ** ADAM_STEP)
    return -ADAM_LR * (m_hat / (jnp.sqrt(v_hat) + ADAM_EPS) + ADAM_WD * w), m, v


def adam_tiled(name, partials, w, m_, v_):
    shape = w.shape
    n_part = partials.shape[0]
    r, c = _rows_cols(shape)
    tr = _row_tile(r, 256, 16)

    def body(p_ref, w_ref, m_ref, v_ref, g_ref, d_ref, nm_ref, nv_ref):
        g = _sum_parts(p_ref, n_part)
        g_ref[...] = g
        d_ref[...], nm_ref[...], nv_ref[...] = _adam(g, w_ref[...], m_ref[...], v_ref[...])

    spec = pl.BlockSpec((tr, c), lambda i: (i, 0))
    res = pl.pallas_call(
        body, grid=(r // tr,), name=name,
        in_specs=[pl.BlockSpec((n_part, tr, c), lambda i: (0, i, 0)), spec, spec, spec],
        out_specs=[spec] * 4, out_shape=[jax.ShapeDtypeStruct((r, c), F32)] * 4,
        compiler_params=_params(("parallel",)),
    )(partials.reshape(n_part, r, c), w.reshape(r, c), m_.reshape(r, c), v_.reshape(r, c))
    return [t.reshape(shape) for t in res]


def adam_small(name, items, extra):
    n, ne = len(items), len(extra)

    def body(*refs):
        ins, outs = refs[:4 * n + ne], refs[4 * n + ne:]
        for a in range(n):
            p_ref, w_ref, m_ref, v_ref = ins[4 * a:4 * a + 4]
            g = _sum_parts(p_ref, N_DEV)
            outs[4 * a][...] = g
            outs[4 * a + 1][...], outs[4 * a + 2][...], outs[4 * a + 3][...] = _adam(g, w_ref[...], m_ref[...], v_ref[...])
        for e in range(ne):
            outs[4 * n + e][...] = _sum_parts(ins[4 * n + e], N_DEV)

    args, out_shape = [], []
    for p, w, m_, v_ in items:
        args += [p, w, m_, v_]
        out_shape += [jax.ShapeDtypeStruct(w.shape, F32)] * 4
    for e in extra:
        args.append(e)
        out_shape.append(jax.ShapeDtypeStruct(e.shape[1:], F32))
    vmem = pl.BlockSpec(memory_space=pltpu.VMEM)
    res = pl.pallas_call(body, name=name, in_specs=[vmem] * len(args), out_specs=[vmem] * len(out_shape), out_shape=out_shape)(*args)
    return [res[4 * a:4 * a + 4] for a in range(n)], res[4 * n:]


def _cols_from_gather(g):
    g = jnp.moveaxis(g, 0, -2)
    return g.reshape(g.shape[:-2] + (g.shape[-2] * g.shape[-1],))


def _cols_to_blocks(w):
    w = w.reshape(w.shape[:-1] + (N_DEV, w.shape[-1] // N_DEV))
    return jnp.moveaxis(w, -2, 0)


def _block_diag(w):
    z = jnp.zeros((B_BLOCK_DIM, B_BLOCK_DIM), w.dtype)
    rows = []
    for j in range(B_BLOCKS // 2):
        top = jnp.concatenate([w[2 * j], z], axis=1)
        bot = jnp.concatenate([z, w[2 * j + 1]], axis=1)
        rows.append(jnp.concatenate([top, bot], axis=0))
    return jnp.concatenate(rows, axis=0)


def _block_diag_grad(d):
    out = []
    for j in range(B_BLOCKS // 2):
        blk = d[128 * j:128 * (j + 1)]
        out.append(blk[:64, :64])
        out.append(blk[64:, 64:])
    return jnp.stack(out)


NAMES = ("norm_gains", "even_w_in", "hgrn_lb_logits", "hgrn_norm", "rg_conv_w", "rg_conv_b", "rg_wa", "rg_ba", "rg_wx", "rg_bx",
         "rg_lambda", "even_w_out", "odd_w_in", "fox_f_bias", "odd_w_out", "ffn_w_up", "ffn_conv_w", "ffn_conv_b", "ffn_w_down")
BIG = ("even_w_in", "even_w_out", "odd_w_in", "odd_w_out", "ffn_w_up", "ffn_w_down")
SMALL_SHARDED = ("norm_gains", "rg_conv_w", "ffn_conv_w")
REPLICATED = ("hgrn_lb_logits", "hgrn_norm", "rg_conv_b", "rg_wa", "rg_ba", "rg_wx", "rg_bx", "rg_lambda", "fox_f_bias", "ffn_conv_b")


def _ffn_forward(tag, layer, h, w_up_g, cw5, cb5, w_down_g, m, seq):
    tm = _div_tile(m, 1024)
    nm = m // tm
    hid = mm(f"{tag}_up", "nn",
             Blk(h, (tm, D_MODEL), lambda i, j, k: (i, 0)),
             Blk(w_up_g, (None, None, D_MODEL, FF_BLK), lambda i, j, k: (j, layer, 0, 0)),
             Blk((N_DEV, m, FF_BLK), (None, tm, FF_BLK), lambda i, j, k: (j, i, 0)), F32, (nm, N_DEV, 1))
    hid = hid.reshape(2, N_DEV // 2, m, FF_BLK)
    act = ffn_mid_fwd(f"{tag}_mid", hid, cw5, cb5, layer, m=m, seq=seq)
    f = mm(f"{tag}_down", "nn",
           Blk(act, (None, tm, FF_BLK), lambda i, j, k: (k, i, 0)),
           Blk(w_down_g, (2, None, FF_BLK // 2, D_MODEL), lambda i, j, k: (k, layer, 0, 0)),
           Blk((m, D_MODEL), (tm, D_MODEL), lambda i, j, k: (i, 0)), F32, (nm, 1, N_DEV // 2))
    return hid, act, f


def _ffn_backward(tag, layer, df, h, hid, act, w_up_g, cw5, cb5, w_down_g, d_wup, d_wdown, m, seq):
    tm = _div_tile(m, 1024)
    nm = m // tm
    dact = mm(f"{tag}_dact", "nt",
              Blk(df, (tm, D_MODEL), lambda i, j, k: (i, 0)),
              Blk(w_down_g, (2, None, FF_BLK // 2, D_MODEL), lambda i, j, k: (j, layer, 0, 0)),
              Blk((N_DEV // 2, m, FF_BLK), (None, tm, FF_BLK), lambda i, j, k: (j, i, 0)), BF16, (nm, N_DEV // 2, 1))
    d_wdown = mm(f"{tag}_dwdown", "tn",
                 Blk(act, (None, tm, FF_BLK), lambda i, j, k: (i, k, 0)),
                 Blk(df, (tm, D_MODEL), lambda i, j, k: (k, 0)),
                 Blk((2, 4) + w_down_g.shape[1:], (2, None, None, FF_BLK // 2, D_MODEL), lambda i, j, k: (0, i, layer, 0, 0)), BF16,
                 (N_DEV // 2, 1, nm), into=d_wdown)
    dhid, d_cw, d_cb = ffn_mid_bwd(f"{tag}_dmid", hid, cw5, cb5, dact, layer, m=m, seq=seq)
    dhid = dhid.reshape(N_DEV, m, FF_BLK)
    dh = mm(f"{tag}_dh", "nt",
            Blk(dhid, (None, tm, FF_BLK), lambda i, j, k: (k, i, 0)),
            Blk(w_up_g, (None, None, D_MODEL, FF_BLK), lambda i, j, k: (k, layer, 0, 0)),
            Blk((m, D_MODEL), (tm, D_MODEL), lambda i, j, k: (i, 0)), BF16, (nm, 1, N_DEV))
    d_wup = mm(f"{tag}_dwup", "tn",
               Blk(h, (tm, D_MODEL), lambda i, j, k: (k, 0)),
               Blk(dhid, (None, tm, FF_BLK), lambda i, j, k: (j, k, 0)),
               Blk((2, 4) + w_up_g.shape[1:], (None, None, None, D_MODEL, FF_BLK), lambda i, j, k: (j % 2, j // 2, layer, 0, 0)), BF16,
               (1, N_DEV, nm), into=d_wup)
    return dh, d_wup, d_cw, d_cb, d_wdown


def kernel(x, norm_gains, even_w_in, hgrn_lb_logits, hgrn_norm, rg_conv_w, rg_conv_b, rg_wa, rg_ba, rg_wx, rg_bx, rg_lambda, even_w_out, odd_w_in, fox_f_bias, odd_w_out, ffn_w_up, ffn_conv_w, ffn_conv_b, ffn_w_down, loss_target, m_norm_gains, m_even_w_in, m_hgrn_lb_logits, m_hgrn_norm, m_rg_conv_w, m_rg_conv_b, m_rg_wa, m_rg_ba, m_rg_wx, m_rg_bx, m_rg_lambda, m_even_w_out, m_odd_w_in, m_fox_f_bias, m_odd_w_out, m_ffn_w_up, m_ffn_conv_w, m_ffn_conv_b, m_ffn_w_down, v_norm_gains, v_even_w_in, v_hgrn_lb_logits, v_hgrn_norm, v_rg_conv_w, v_rg_conv_b, v_rg_wa, v_rg_ba, v_rg_wx, v_rg_bx, v_rg_lambda, v_even_w_out, v_odd_w_in, v_fox_f_bias, v_odd_w_out, v_ffn_w_up, v_ffn_conv_w, v_ffn_conv_b, v_ffn_w_down):
    local = dict(locals())
    w = {n: local[n] for n in NAMES}
    mom = {n: local["m_" + n] for n in NAMES}
    var = {n: local["v_" + n] for n in NAMES}
    n_batch, seq, _ = x.shape
    m = n_batch * seq
    tm = _div_tile(m, 512)
    tmm = _div_tile(m, 1024)
    nm = m // tmm

    gathered = all_gather("gather_weights", [w[n].astype(BF16) for n in BIG] + [w[n] for n in SMALL_SHARDED])
    g = dict(zip(BIG + SMALL_SHARDED, gathered))
    w_in_e = g["even_w_in"]
    w_out_e = g["even_w_out"].reshape(D_MODEL, D_MODEL)
    w_in_o = jnp.pad(_cols_from_gather(g["odd_w_in"])[0], ((0, 0), (0, 3200 - 3088)))
    w_out_o = g["odd_w_out"].reshape(D_MODEL, D_MODEL)
    w_up_g, w_down_g = g["ffn_w_up"], g["ffn_w_down"]
    gains = _cols_from_gather(g["norm_gains"])
    rg_cw = _cols_from_gather(g["rg_conv_w"])[0]
    n_layer = ffn_conv_w.shape[0]
    cw5 = g["ffn_conv_w"].reshape(2, N_DEV // 2, n_layer, FFN_CONV, FF_BLK)
    cb5 = ffn_conv_b.reshape(n_layer, 2, N_DEV // 2, 1, FF_BLK)
    gain = lambda l, k: gains[l, k:k + 1, :]
    wa_bd, wx_bd = _block_diag(rg_wa[0]), _block_diag(rg_wx[0])
    fbias = jnp.pad(fox_f_bias, ((0, 0), (0, LANES - C_HEADS)))

    x0 = x.reshape(m, D_MODEL)
    tgt = loss_target.reshape(m, D_MODEL)

    (h0,) = tile_fwd("l0_prenorm", fn_prenorm, m=m, tm=tm, nj=1, rows=[Row(x0)], pars=[Par(gain(0, 0))], outs=[Out(D_MODEL, BF16)])
    z0 = mm("l0_in", "nn",
            Blk(h0, (tmm, D_MODEL), lambda i, j, k: (i, 0)),
            Blk(w_in_e, (None, None, D_MODEL, 384), lambda i, j, k: (j, 0, 0, 0)),
            Blk((m, 3072), (tmm, 384), lambda i, j, k: (i, j)), F32, (nm, N_DEV, 1))
    oa, sprev = hgrn_fwd("l0_hgrn", z0, hgrn_lb_logits, hgrn_norm, n_batch=n_batch, seq=seq)
    rg_rows = lambda: [Row(z0, LANES, 16), Row(z0, LANES, 20)]
    rg_pars = lambda: [Par(rg_cw, "col", LANES), Par(rg_conv_b, "col", LANES), Par(wa_bd, "row", LANES), Par(rg_ba, "col", LANES),
                       Par(wx_bd, "row", LANES), Par(rg_bx, "col", LANES), Par(rg_lambda, "col", LANES)]
    (ob,) = tile_fwd("l0_rglru", fn_rglru, m=m, tm=seq, nj=B_WIDTH // LANES, rows=rg_rows(), pars=rg_pars(),
                     outs=[Out(B_WIDTH, BF16, LANES)])
    mixcat0 = jnp.concatenate([oa, ob], axis=-1)
    mix0 = mm2d("l0_out", "nn", mixcat0, w_out_e)
    x1, h1 = tile_fwd("l0_postnorm", fn_addnorm2, m=m, tm=tm, nj=1, rows=[Row(x0), Row(mix0)], pars=[Par(gain(0, 1)), Par(gain(0, 2))],
                      outs=[Out(D_MODEL, F32), Out(D_MODEL, BF16)])
    hid0, act0, f0 = _ffn_forward("l0_ffn", 0, h1, w_up_g, cw5, cb5, w_down_g, m, seq)
    x2, h2 = tile_fwd("l0_ffnnorm", fn_addnorm2, m=m, tm=tm, nj=1, rows=[Row(x1), Row(f0)], pars=[Par(gain(0, 3)), Par(gain(1, 0))],
                      outs=[Out(D_MODEL, F32), Out(D_MODEL, BF16)])

    z1 = mm2d("l1_in", "nn", h2, w_in_o)
    (cgate,) = tile_fwd("l1_gate", fn_fox_gate, m=m, tm=seq, nj=1, rows=[Row(z1, LANES, 3072 // LANES)], pars=[Par(fbias)],
                        outs=[Out(LANES, F32)])
    bh = n_batch * C_HEADS
    nqb = seq // min(ATT_BLK, seq)
    heads = lambda t: t.reshape(n_batch, seq, C_HEADS, C_HEAD_DIM).transpose(0, 2, 1, 3).reshape(bh, seq, C_HEAD_DIM)
    unheads = lambda t: t.reshape(n_batch, C_HEADS, seq, C_HEAD_DIM).transpose(0, 2, 1, 3).reshape(m, D_MODEL)
    per_blk_t = lambda t: t.reshape(bh, nqb, seq // nqb, t.shape[-1]).transpose(0, 1, 3, 2)
    qh, kh, vh = (heads(z1[:, i * D_MODEL:(i + 1) * D_MODEL].astype(BF16)) for i in range(3))
    c_bht = cgate[:, :C_HEADS].reshape(n_batch, seq, C_HEADS).transpose(0, 2, 1).reshape(bh, seq)
    qa, ka = fox_operands(qh, kh, c_bht)
    ot, lse = fox_fwd("l1_attn", qa, ka, per_blk_t(vh))
    oc = ot.reshape(n_batch, C_HEADS, C_HEAD_DIM, seq).transpose(0, 3, 1, 2).reshape(m, D_MODEL)
    mix1 = mm2d("l1_out", "nn", oc, w_out_o)
    x3, h3 = tile_fwd("l1_postnorm", fn_addnorm2, m=m, tm=tm, nj=1, rows=[Row(x2), Row(mix1)], pars=[Par(gain(1, 1)), Par(gain(1, 2))],
                      outs=[Out(D_MODEL, F32), Out(D_MODEL, BF16)])
    hid1, act1, f1 = _ffn_forward("l1_ffn", 1, h3, w_up_g, cw5, cb5, w_down_g, m, seq)
    dy, loss_part = tile_fwd("loss", fn_final, m=m, tm=tm, nj=1, rows=[Row(x3), Row(f1), Row(tgt)], pars=[Par(gain(1, 3))],
                             outs=[Out(D_MODEL, F32)], n_acc=1)

    df1, d_g13 = tile_bwd("l1_dffnnorm", fn_rms_only, m=m, tm=tm, nj=1, rows=[Row(f1)], pars=[Par(gain(1, 3))], cts=[Row(dy)],
                          drows=[Out(D_MODEL, BF16)])
    dh3, d_wup, d_cw1, d_cb1, d_wdown = _ffn_backward("l1_ffn", 1, df1, h3, hid1, act1, w_up_g, cw5, cb5, w_down_g, None, None, m, seq)
    dx2, dmix1, d_g11, d_g12 = tile_bwd("l1_dpostnorm", fn_addnorm2, m=m, tm=tm, nj=1, rows=[Row(x2), Row(mix1)],
                                        pars=[Par(gain(1, 1)), Par(gain(1, 2))], cts=[Row(dy), Row(dh3)],
                                        drows=[Out(D_MODEL, F32), Out(D_MODEL, BF16)])
    doc = mm2d("l1_doc", "nt", dmix1, w_out_o, BF16)
    d_wout_o = mm2d("l1_dwout", "tn", oc, dmix1)
    doh = heads(doc)
    dqa_t, dka, dv = fox_bwd("l1_dattn", qa, ka, per_blk_t(ka), vh, doh, doh.transpose(0, 2, 1), ot, lse)
    dq = dqa_t[:, :, :C_HEAD_DIM, :].transpose(0, 1, 3, 2).reshape(bh, seq, C_HEAD_DIM) * C_HEAD_DIM ** -0.5
    dk = dka[:, :, :C_HEAD_DIM]
    dc = dqa_t[:, :, C_HEAD_DIM, :].reshape(bh, seq) - dka[:, :, C_HEAD_DIM + 3]
    dc = jnp.pad(dc.reshape(n_batch, C_HEADS, seq).transpose(0, 2, 1).reshape(m, C_HEADS), ((0, 0), (0, LANES - C_HEADS)))
    dzf, d_fbias = tile_bwd("l1_dgate", fn_fox_gate, m=m, tm=seq, nj=1, rows=[Row(z1, LANES, 3072 // LANES)], pars=[Par(fbias)],
                            cts=[Row(dc)], drows=[Out(LANES, BF16)])
    dz1 = jnp.concatenate([unheads(dq).astype(BF16), unheads(dk).astype(BF16), unheads(dv).astype(BF16), dzf], axis=-1)
    dh2 = mm2d("l1_dh", "nt", dz1, w_in_o, BF16)
    d_win_o = mm2d("l1_dwin", "tn", h2, dz1)

    dx1, df0, d_g03, d_g10 = tile_bwd("l0_dffnnorm", fn_addnorm2, m=m, tm=tm, nj=1, rows=[Row(x1), Row(f0)],
                                      pars=[Par(gain(0, 3)), Par(gain(1, 0))], cts=[Row(dx2), Row(dh2)],
                                      drows=[Out(D_MODEL, F32), Out(D_MODEL, BF16)])
    dh1, d_wup, d_cw0, d_cb0, d_wdown = _ffn_backward("l0_ffn", 0, df0, h1, hid0, act0, w_up_g, cw5, cb5, w_down_g, d_wup, d_wdown, m, seq)
    dx0a, dmix0, d_g01, d_g02 = tile_bwd("l0_dpostnorm", fn_addnorm2, m=m, tm=tm, nj=1, rows=[Row(x0), Row(mix0)],
                                         pars=[Par(gain(0, 1)), Par(gain(0, 2))], cts=[Row(dx1), Row(dh1)],
                                         drows=[Out(D_MODEL, F32), Out(D_MODEL, BF16)])
    dmixcat0 = mm2d("l0_dmixcat", "nt", dmix0, w_out_e, BF16)
    d_wout_e = mm2d("l0_dwout", "tn", mixcat0, dmix0)
    dzq, dzf0, dzv, dzg, d_lb, d_hnorm = hgrn_bwd("l0_dhgrn", z0, sprev, hgrn_lb_logits, hgrn_norm, dmixcat0, n_batch=n_batch, seq=seq)
    dzx, dzy, d_rcw, d_rcb, d_wa, d_ba, d_wx, d_bx, d_lam = tile_bwd(
        "l0_drglru", fn_rglru, m=m, tm=seq, nj=B_WIDTH // LANES, rows=rg_rows(), pars=rg_pars(),
        cts=[Row(dmixcat0, LANES, A_WIDTH // LANES)], drows=[Out(B_WIDTH, BF16, LANES), Out(B_WIDTH, BF16, LANES)])
    dz0 = jnp.concatenate([dzq, dzf0, dzv, dzg, dzx, dzy], axis=-1)
    dh0 = mm("l0_dh", "nt",
             Blk(dz0, (tmm, 384), lambda i, j, k: (i, k)),
             Blk(w_in_e, (None, None, D_MODEL, 384), lambda i, j, k: (k, 0, 0, 0)),
             Blk((m, D_MODEL), (tmm, D_MODEL), lambda i, j, k: (i, 0)), BF16, (nm, 1, N_DEV))
    d_win_e = mm("l0_dwin", "tn",
                 Blk(h0, (tmm, D_MODEL), lambda i, j, k: (k, 0)),
                 Blk(dz0, (tmm, 384), lambda i, j, k: (k, j)),
                 Blk((2, 4) + w_in_e.shape[1:], (None, None, None, D_MODEL, 384), lambda i, j, k: (j % 2, j // 2, 0, 0, 0)), BF16,
                 (1, N_DEV, nm))
    dx0, d_g00 = tile_bwd("l0_dprenorm", fn_input_norm, m=m, tm=tm, nj=1, rows=[Row(x0)], pars=[Par(gain(0, 0))],
                          cts=[Row(dx0a), Row(dh0)], drows=[Out(D_MODEL, F32)])

    d_gains = jnp.stack([jnp.concatenate([d_g00, d_g01, d_g02, d_g03], axis=0), jnp.concatenate([d_g10, d_g11, d_g12, d_g13], axis=0)])
    d_ffn_cw = jnp.stack([d_cw0, d_cw1], axis=2).reshape(N_DEV, n_layer, FFN_CONV, FF_BLK)
    by_core = lambda t: jnp.swapaxes(t.reshape((4, 2) + t.shape[1:]), 0, 1).astype(BF16)
    half = {
        "even_w_in": d_win_e,
        "even_w_out": by_core(d_wout_e.reshape(N_DEV, 1, D_MODEL // N_DEV, D_MODEL)),
        "odd_w_in": by_core(_cols_to_blocks(d_win_o[None, :, :3088])),
        "odd_w_out": by_core(d_wout_o.reshape(N_DEV, 1, D_MODEL // N_DEV, D_MODEL)),
        "ffn_w_up": d_wup,
        "ffn_w_down": d_wdown,
    }
    core = lax.axis_index("c").astype(jnp.int32).reshape(1)
    from_sibling = pair_exchange("exchange_core", [half[n] for n in BIG])
    chip_sums = [pair_add("add_" + n, half[n], r, core) for n, r in zip(BIG, from_sibling)]
    recv = dict(zip(BIG, quad_exchange("exchange_chips", chip_sums)))
    res = {n: adam_tiled("adam_" + n, recv[n], w[n], mom[n], var[n]) for n in BIG}
    small_send = [_cols_to_blocks(d_gains), _cols_to_blocks(d_rcw[None]), d_ffn_cw]
    recv.update(zip(SMALL_SHARDED, all_to_all("exchange_small", small_send)))

    d_ffn_cb = jnp.stack([d_cb0, d_cb1]).reshape(n_layer, 2 * D_FF)
    rep = {"hgrn_lb_logits": d_lb, "hgrn_norm": d_hnorm, "rg_conv_b": d_rcb, "rg_wa": _block_diag_grad(d_wa)[None], "rg_ba": d_ba,
           "rg_wx": _block_diag_grad(d_wx)[None], "rg_bx": d_bx, "rg_lambda": d_lam, "fox_f_bias": d_fbias[:, :C_HEADS],
           "ffn_conv_b": d_ffn_cb}
    parts = all_gather("gather_partials", [rep[n] for n in REPLICATED] + [loss_part])
    for n, p in zip(REPLICATED, parts):
        recv[n] = p
    small = SMALL_SHARDED + REPLICATED
    small_res, (loss_sum,) = adam_small("adam_small", [(recv[n], w[n], mom[n], var[n]) for n in small], [parts[-1]])
    res.update(dict(zip(small, small_res)))

    out = [loss_sum[0, 0], dx0.reshape(x.shape)]
    for k in range(4):
        out += [res[n][k] for n in NAMES]
    return tuple(out)
```

```python
import functools

import jax
import jax.numpy as jnp
from jax import lax
from jax.experimental import pallas as pl
from jax.experimental.pallas import tpu as pltpu

F32 = jnp.float32
BF16 = jnp.bfloat16

D_MODEL = 1024
A_HEADS = 4
A_WIDTH = 512
HGRN_CHUNK = 64
HGRN_SEG = 512
B_WIDTH = 512
B_BLOCKS = 8
B_BLOCK_DIM = 64
B_CONV = 4
RG_C = 8.0
C_HEADS = 16
C_HEAD_DIM = 64
D_FF = 2816
FFN_CONV = 3
EPS = 1e-6
LANES = 128
HALO = 16
N_DEV = 8
FF_BLK = 2 * D_FF // N_DEV
MESH = pl.DeviceIdType.MESH
NEG = -1e30
VMEM_LIMIT = 56 * 1024 * 1024

ADAM_LR = 0.001
ADAM_B1 = 0.9
ADAM_B2 = 0.999
ADAM_EPS = 1e-08
ADAM_WD = 0.01
ADAM_STEP = 10


def _dg(a, b, pat):
    nb = a.ndim - 2
    batch = (tuple(range(nb)), tuple(range(nb)))
    ca = a.ndim - 1 if pat[0] == "n" else a.ndim - 2
    cb = b.ndim - 2 if pat[1] == "n" else b.ndim - 1
    return lax.dot_general(a.astype(BF16), b.astype(BF16), (((ca,), (cb,)), batch), preferred_element_type=F32)


@functools.partial(jax.custom_vjp, nondiff_argnums=(2,))
def bdot(a, b, pat):
    return _dg(a, b, pat)


def _bdot_fwd(a, b, pat):
    return _dg(a, b, pat), (a, b)


def _bdot_bwd(pat, res, g):
    a, b = res
    if pat == "nn":
        return _dg(g, b, "nt"), _dg(a, g, "tn")
    if pat == "nt":
        return _dg(g, b, "nn"), _dg(g, a, "tn")
    return _dg(b, g, "nt"), _dg(a, g, "nn")


bdot.defvjp(_bdot_fwd, _bdot_bwd)


def _shift_raw(x, s, up, fill):
    if s == 0:
        return x
    n = x.shape[0]
    r = pltpu.roll(x, (n - s) if up else s, 0)
    idx = lax.broadcasted_iota(jnp.int32, x.shape, 0)
    mask = (idx >= n - s) if up else (idx < s)
    return jnp.where(mask, jnp.asarray(fill, x.dtype), r)


@functools.partial(jax.custom_vjp, nondiff_argnums=(1,))
def shift_down(x, s):
    return _shift_raw(x, s, False, 0.0)


def _shift_down_fwd(x, s):
    return _shift_raw(x, s, False, 0.0), None


def _shift_down_bwd(s, _, g):
    return (_shift_raw(g, s, True, 0.0),)


shift_down.defvjp(_shift_down_fwd, _shift_down_bwd)


def _scan_impl(a, u, up):
    n = a.shape[0]
    s = 1
    while s < n:
        u = a * _shift_raw(u, s, up, 0.0) + u
        if 2 * s < n:
            a = a * _shift_raw(a, s, up, 1.0)
        s *= 2
    return u


@jax.custom_vjp
def lin_scan(a, u):
    return _scan_impl(a, u, False)


def _lin_scan_fwd(a, u):
    h = _scan_impl(a, u, False)
    return h, (a, h)


def _lin_scan_bwd(res, g):
    a, h = res
    gh = _scan_impl(_shift_raw(a, 1, True, 0.0), g, True)
    return gh * _shift_raw(h, 1, False, 0.0), gh


lin_scan.defvjp(_lin_scan_fwd, _lin_scan_bwd)


def _cumsum_impl(x, up, period):
    n = x.shape[0]
    span = n if period is None else period
    idx = lax.broadcasted_iota(jnp.int32, x.shape, 0)
    pos = idx if period is None else idx % period
    s = 1
    while s < span:
        sh = _shift_raw(x, s, up, 0.0)
        if period is not None:
            keep = (pos < period - s) if up else (pos >= s)
            sh = jnp.where(keep, sh, 0.0)
        x = x + sh
        s *= 2
    return x


@functools.partial(jax.custom_vjp, nondiff_argnums=(1,))
def cumsum_rows(x, period):
    return _cumsum_impl(x, False, period)


def _cumsum_fwd(x, period):
    return _cumsum_impl(x, False, period), None


def _cumsum_bwd(period, _, g):
    return (_cumsum_impl(g, True, period),)


cumsum_rows.defvjp(_cumsum_fwd, _cumsum_bwd)


def _sigmoid(x):
    return jax.nn.sigmoid(x)


def _expm1(x):
    return jnp.tanh(0.5 * x) * (jnp.exp(x) + 1.0)


def _softplus(x):
    return jnp.maximum(x, 0.0) + jnp.log(1.0 + jnp.exp(-jnp.abs(x)))


def _rms(x, g):
    return x * lax.rsqrt(jnp.mean(x * x, axis=-1, keepdims=True) + EPS) * g


def fn_prenorm(x, g):
    return (_rms(x, g).astype(BF16),)


def fn_addnorm2(x, y, g_post, g_pre):
    x1 = x + _rms(y, g_post)
    return x1, _rms(x1, g_pre).astype(BF16)


def fn_input_norm(x, g):
    return x, _rms(x, g).astype(BF16)


def fn_final(x, y, tgt, g_post):
    out = x + _rms(y, g_post)
    err = out - tgt
    dy = err * (1.0 / D_MODEL)
    loss = 0.5 * jnp.sum(jnp.mean(err * err, axis=-1, keepdims=True), axis=0, keepdims=True)
    return dy, jnp.broadcast_to(loss, (1, LANES))


def fn_rms_only(y, g):
    return (_rms(y, g),)


def _causal_conv(x, w, b, taps):
    c = b
    for k in range(taps):
        c = c + w[k:k + 1, :] * shift_down(x, taps - 1 - k)
    return c


def fn_rglru(xb, yb, cw, cb, wa, ba, wx, bx, lam):
    xf = _causal_conv(xb, cw, cb, B_CONV)
    r = _sigmoid(bdot(xf, wa, "nn") + ba)
    i = _sigmoid(bdot(xf, wx, "nn") + bx)
    log_a = -RG_C * r * _softplus(-lam)
    a = jnp.exp(log_a)
    u = jnp.sqrt(-_expm1(2.0 * log_a)) * (i * xf)
    h = lin_scan(a, u)
    return ((h * jax.nn.gelu(yb)).astype(BF16),)


def fn_fox_gate(zf, bias):
    return (cumsum_rows(jax.nn.log_sigmoid(zf + bias), None),)


def fn_hgrn_seg(q, fl, v, g, st, logits, hn):
    rows = q.shape[0]
    nc = rows // HGRN_CHUNK
    l0, l1, l2 = logits[0:1, :], logits[1:2, :], logits[2:3, :]
    mx = jnp.maximum(jnp.maximum(l0, l1), l2)
    e0, e1, e2 = jnp.exp(l0 - mx), jnp.exp(l1 - mx), jnp.exp(l2 - mx)
    lb = e0 / (e0 + e1 + e2)
    forget = lb + (1.0 - lb) * _sigmoid(fl)
    qs = q * _sigmoid(q)
    kk = 1.0 - forget
    logf = jnp.log(forget)
    bcum = cumsum_rows(logf, HGRN_CHUNK)
    c3 = lambda t: t.reshape(nc, HGRN_CHUNK, 128)
    b_last = jnp.sum(c3(logf), axis=1, keepdims=True)
    bcum3 = c3(bcum)
    q_dec = c3(qs) * jnp.exp(bcum3)
    k_dec = c3(kk) * jnp.exp(-bcum3)
    k_upd = c3(kk) * jnp.exp(b_last - bcum3)
    v3 = c3(v)
    scores = bdot(q_dec, k_dec, "nt")
    ri = lax.broadcasted_iota(jnp.int32, scores.shape, 1)
    ci = lax.broadcasted_iota(jnp.int32, scores.shape, 2)
    scores = jnp.where(ri >= ci, scores, 0.0)
    o = bdot(scores, v3, "nn")
    upd_t = bdot(v3, k_upd, "tn")
    dec = jnp.exp(b_last)
    prev = []
    for n in range(nc):
        prev.append(st)
        st = st * dec[n] + upd_t[n]
    o = o + bdot(q_dec, jnp.stack(prev), "nt")
    o = o.reshape(rows, 128)
    o = o * lax.rsqrt(jnp.mean(o * o, axis=-1, keepdims=True) + EPS) * hn
    return (o * _sigmoid(g)).astype(BF16), st


def _ffn_conv(xg, xv, cw, cb):
    cg = _causal_conv(xg, cw[0], cb[0], FFN_CONV)[HALO:]
    cv = _causal_conv(xv, cw[1], cb[1], FFN_CONV)[HALO:]
    return cg, cv


def _ffn_gate(cg, cv):
    return jax.nn.gelu(cg) * cv


class Row:
    def __init__(self, arr, cb=None, off=0):
        self.arr, self.cb, self.off = arr, cb, off

    def spec(self, tm):
        if self.cb is None:
            return pl.BlockSpec((tm, self.arr.shape[1]), lambda j, i: (i, 0))
        off = self.off
        return pl.BlockSpec((tm, self.cb), lambda j, i: (i, j + off))


class Par:
    def __init__(self, arr, kind="full", bs=None):
        self.arr, self.kind, self.bs = arr, kind, bs

    def block(self):
        if self.kind == "full":
            return self.arr.shape
        if self.kind == "col":
            return (self.arr.shape[0], self.bs)
        return (self.bs, self.arr.shape[1])

    def spec(self):
        if self.kind == "full":
            return pl.BlockSpec(self.block(), lambda j, i: (0, 0))
        if self.kind == "col":
            return pl.BlockSpec(self.block(), lambda j, i: (0, j))
        return pl.BlockSpec(self.block(), lambda j, i: (j, 0))


class Out:
    def __init__(self, width, dtype, cb=None, off=0):
        self.width, self.dtype, self.cb, self.off = width, dtype, cb, off

    def spec(self, tm):
        if self.cb is None:
            return pl.BlockSpec((tm, self.width), lambda j, i: (i, 0))
        off = self.off
        return pl.BlockSpec((tm, self.cb), lambda j, i: (i, j + off))


def _params(sem):
    return pltpu.CompilerParams(dimension_semantics=sem, vmem_limit_bytes=VMEM_LIMIT)


def tile_fwd(name, fn, *, m, tm, nj, rows, pars, outs, n_acc=0):
    n_r, n_p, n_o = len(rows), len(pars), len(outs)

    def body(*refs):
        ins = [r[...] for r in refs[:n_r + n_p]]
        res = fn(*ins)
        o_refs = refs[n_r + n_p:]
        for k in range(n_o):
            o_refs[k][...] = res[k].astype(o_refs[k].dtype)
        first = jnp.logical_and(pl.program_id(0) == 0, pl.program_id(1) == 0)
        for k in range(n_acc):
            ref = o_refs[n_o + k]

            @pl.when(first)
            def _():
                ref[...] = jnp.zeros_like(ref)

            ref[...] += res[n_o + k]

    out_shape = [jax.ShapeDtypeStruct((m, o.width), o.dtype) for o in outs]
    out_specs = [o.spec(tm) for o in outs]
    for _ in range(n_acc):
        out_shape.append(jax.ShapeDtypeStruct((1, LANES), F32))
        out_specs.append(pl.BlockSpec((1, LANES), lambda j, i: (0, 0)))
    sem = ("arbitrary", "arbitrary") if n_acc else ("parallel", "parallel")
    return pl.pallas_call(
        body, grid=(nj, m // tm), name=name,
        in_specs=[r.spec(tm) for r in rows] + [p.spec() for p in pars],
        out_specs=out_specs, out_shape=out_shape, compiler_params=_params(sem),
    )(*[r.arr for r in rows], *[p.arr for p in pars])


def tile_bwd(name, fn, *, m, tm, nj, rows, pars, cts, drows):
    n_r, n_p, n_c = len(rows), len(pars), len(cts)
    want = [k for k in range(n_r) if drows[k] is not None]

    def body(*refs):
        ins = [r[...] for r in refs[:n_r + n_p]]
        ct = [r[...] for r in refs[n_r + n_p:n_r + n_p + n_c]]
        o_refs = refs[n_r + n_p + n_c:]
        res, vjp = jax.vjp(fn, *ins)
        grads = vjp(tuple(c.astype(r.dtype) for c, r in zip(ct, res)))
        for pos, k in enumerate(want):
            o_refs[pos][...] = grads[k].astype(o_refs[pos].dtype)
        for k in range(n_p):
            ref = o_refs[len(want) + k]
            first = pl.program_id(1) == 0
            if pars[k].kind == "full":
                first = jnp.logical_and(first, pl.program_id(0) == 0)

            @pl.when(first)
            def _():
                ref[...] = jnp.zeros_like(ref)

            ref[...] += grads[n_r + k].astype(F32)

    out_shape = [jax.ShapeDtypeStruct((m, drows[k].width), drows[k].dtype) for k in want]
    out_specs = [drows[k].spec(tm) for k in want]
    for p in pars:
        out_shape.append(jax.ShapeDtypeStruct(p.arr.shape, F32))
        out_specs.append(p.spec())
    return pl.pallas_call(
        body, grid=(nj, m // tm), name=name,
        in_specs=[r.spec(tm) for r in rows] + [p.spec() for p in pars] + [c.spec(tm) for c in cts],
        out_specs=out_specs, out_shape=out_shape, compiler_params=_params(("arbitrary", "arbitrary")),
    )(*[r.arr for r in rows], *[p.arr for p in pars], *[c.arr for c in cts])


class Blk:
    def __init__(self, arr, block, index):
        self.arr, self.block, self.index = arr, block, index

    def spec(self):
        return pl.BlockSpec(self.block, self.index)


def _flat2(v):
    return v if v.ndim == 2 else v.reshape(-1, v.shape[-1])


def mm(name, pat, a, b, o, out_dtype, grid, into=None):
    nk = grid[2]
    o_shape = o.arr

    def body(*refs):
        a_ref, b_ref = refs[0], refs[1]
        o_ref = refs[3] if into is not None else refs[2]
        r = _dg(_flat2(a_ref[...]), _flat2(b_ref[...]), pat)
        if nk == 1:
            o_ref[...] = r.astype(out_dtype).reshape(o_ref.shape)
            return
        acc_ref = refs[-1]
        kk = pl.program_id(2)

        @pl.when(kk == 0)
        def _():
            acc_ref[...] = r

        @pl.when(kk > 0)
        def _():
            acc_ref[...] += r

        @pl.when(kk == nk - 1)
        def _():
            o_ref[...] = acc_ref[...].astype(out_dtype).reshape(o_ref.shape)

    ob = [d for d in o.block if d is not None]
    acc_shape = (ob[0], ob[1]) if len(ob) == 2 else (ob[0] * ob[1], ob[2])
    in_specs = [a.spec(), b.spec()]
    args = [a.arr, b.arr]
    aliases = {}
    if into is not None:
        in_specs.append(pl.BlockSpec(memory_space=pl.ANY))
        args.append(into)
        aliases = {2: 0}
    return pl.pallas_call(
        body, grid=grid, name=name, in_specs=in_specs, out_specs=o.spec(),
        out_shape=jax.ShapeDtypeStruct(o_shape, out_dtype),
        scratch_shapes=[pltpu.VMEM(acc_shape, F32)] if nk > 1 else [],
        input_output_aliases=aliases,
        compiler_params=_params(("parallel", "parallel", "arbitrary")),
    )(*args)


def _div_tile(n, cap):
    if n <= cap:
        return n
    best = 128
    for t in range(128, cap + 1, 128):
        if n % t == 0:
            best = t
    return best


def mm2d(name, pat, a, b, out_dtype=F32):
    if pat == "tn":
        k, m = a.shape
    else:
        m, k = a.shape
    n = b.shape[0] if pat == "nt" else b.shape[1]
    tm, tn, tk = _div_tile(m, 1024), _div_tile(n, 1024), _div_tile(k, 1024)
    a_blk = Blk(a, (tk, tm), lambda i, j, kk: (kk, i)) if pat == "tn" else Blk(a, (tm, tk), lambda i, j, kk: (i, kk))
    b_blk = Blk(b, (tn, tk), lambda i, j, kk: (j, kk)) if pat == "nt" else Blk(b, (tk, tn), lambda i, j, kk: (kk, j))
    o_blk = Blk((m, n), (tm, tn), lambda i, j, kk: (i, j))
    return mm(name, pat, a_blk, b_blk, o_blk, out_dtype, (m // tm, n // tn, k // tk))


def hgrn_fwd(name, z, logits, hnorm, *, n_batch, seq):
    m = n_batch * seq
    ts = min(HGRN_SEG, seq)
    n_seg = seq // ts

    def body(q_ref, f_ref, v_ref, g_ref, lg_ref, hn_ref, o_ref, sp_ref, st_ref):
        s = pl.program_id(2)

        @pl.when(s == 0)
        def _():
            st_ref[...] = jnp.zeros_like(st_ref)

        st = st_ref[...]
        sp_ref[...] = st
        o, st_new = fn_hgrn_seg(q_ref[...], f_ref[...], v_ref[...], g_ref[...], st, lg_ref[...], hn_ref[...])
        o_ref[...] = o
        st_ref[...] = st_new

    part = lambda p: pl.BlockSpec((ts, 128), lambda h, b, s: (b * n_seg + s, 4 * p + h))
    return pl.pallas_call(
        body, grid=(A_HEADS, n_batch, n_seg), name=name,
        in_specs=[part(0), part(1), part(2), part(3),
                  pl.BlockSpec((3, 128), lambda h, b, s: (0, h)),
                  pl.BlockSpec((1, 128), lambda h, b, s: (0, h))],
        out_specs=[pl.BlockSpec((ts, 128), lambda h, b, s: (b * n_seg + s, h)),
                   pl.BlockSpec((128, 128), lambda h, b, s: ((b * n_seg + s) * A_HEADS + h, 0))],
        out_shape=[jax.ShapeDtypeStruct((m, A_WIDTH), BF16),
                   jax.ShapeDtypeStruct((n_batch * n_seg * A_HEADS * 128, 128), F32)],
        scratch_shapes=[pltpu.VMEM((128, 128), F32)],
        compiler_params=_params(("arbitrary", "arbitrary", "arbitrary")),
    )(z, z, z, z, logits, hnorm)


def hgrn_bwd(name, z, sprev, logits, hnorm, do, *, n_batch, seq):
    m = n_batch * seq
    ts = min(HGRN_SEG, seq)
    n_seg = seq // ts

    def body(q_ref, f_ref, v_ref, g_ref, sp_ref, lg_ref, hn_ref, do_ref, dq_ref, df_ref, dv_ref, dg_ref, dlg_ref, dhn_ref, dst_ref):
        s = pl.program_id(2)

        @pl.when(s == 0)
        def _():
            dst_ref[...] = jnp.zeros_like(dst_ref)

        res, vjp = jax.vjp(fn_hgrn_seg, q_ref[...], f_ref[...], v_ref[...], g_ref[...], sp_ref[...], lg_ref[...], hn_ref[...])
        dq, df, dv, dg, dst, dlg, dhn = vjp((do_ref[...].astype(res[0].dtype), dst_ref[...]))
        dq_ref[...] = dq.astype(dq_ref.dtype)
        df_ref[...] = df.astype(df_ref.dtype)
        dv_ref[...] = dv.astype(dv_ref.dtype)
        dg_ref[...] = dg.astype(dg_ref.dtype)
        dst_ref[...] = dst
        first = jnp.logical_and(pl.program_id(1) == 0, s == 0)

        @pl.when(first)
        def _():
            dlg_ref[...] = jnp.zeros_like(dlg_ref)
            dhn_ref[...] = jnp.zeros_like(dhn_ref)

        dlg_ref[...] += dlg
        dhn_ref[...] += dhn

    rev = lambda b, s: b * n_seg + (n_seg - 1 - s)
    part = lambda p: pl.BlockSpec((ts, 128), lambda h, b, s: (rev(b, s), 4 * p + h))
    head = pl.BlockSpec((ts, 128), lambda h, b, s: (rev(b, s), h))
    dpart = jax.ShapeDtypeStruct((m, A_WIDTH), BF16)
    return pl.pallas_call(
        body, grid=(A_HEADS, n_batch, n_seg), name=name,
        in_specs=[part(0), part(1), part(2), part(3),
                  pl.BlockSpec((128, 128), lambda h, b, s: (rev(b, s) * A_HEADS + h, 0)),
                  pl.BlockSpec((3, 128), lambda h, b, s: (0, h)),
                  pl.BlockSpec((1, 128), lambda h, b, s: (0, h)),
                  head],
        out_specs=[head, head, head, head,
                   pl.BlockSpec((3, 128), lambda h, b, s: (0, h)),
                   pl.BlockSpec((1, 128), lambda h, b, s: (0, h))],
        out_shape=[dpart, dpart, dpart, dpart,
                   jax.ShapeDtypeStruct(logits.shape, F32),
                   jax.ShapeDtypeStruct(hnorm.shape, F32)],
        scratch_shapes=[pltpu.VMEM((128, 128), F32)],
        compiler_params=_params(("arbitrary", "arbitrary", "arbitrary")),
    )(z, z, z, z, sprev, logits, hnorm, do)


def _ffn_tiles(m, seq):
    tm = min(512, seq)
    return tm, seq // tm, m // tm


def ffn_mid_fwd(name, hid, cw, cb, layer, *, m, seq):
    tm, n_t, n_i = _ffn_tiles(m, seq)
    hb = tm // HALO

    def body(x_ref, xb_ref, cw_ref, cb_ref, o_ref):
        first = pl.program_id(1) % n_t == 0
        before = jnp.where(first, 0.0, xb_ref[...])
        ext = jnp.concatenate([before, x_ref[...]], axis=1)
        cg, cv = _ffn_conv(ext[0], ext[1], cw_ref[...], cb_ref[...])
        o_ref[...] = _ffn_gate(cg, cv).astype(o_ref.dtype)

    return pl.pallas_call(
        body, grid=(N_DEV // 2, n_i), name=name,
        in_specs=[pl.BlockSpec((2, None, tm, FF_BLK), lambda d, i: (0, d, i, 0)),
                  pl.BlockSpec((2, None, HALO, FF_BLK), lambda d, i: (0, d, jnp.maximum(i * hb - 1, 0), 0)),
                  pl.BlockSpec((2, None, None, FFN_CONV, FF_BLK), lambda d, i: (0, d, layer, 0, 0)),
                  pl.BlockSpec((None, 2, None, 1, FF_BLK), lambda d, i: (layer, 0, d, 0, 0))],
        out_specs=pl.BlockSpec((None, tm, FF_BLK), lambda d, i: (d, i, 0)),
        out_shape=jax.ShapeDtypeStruct((N_DEV // 2, m, FF_BLK), BF16),
        compiler_params=_params(("parallel", "parallel")),
    )(hid, hid, cw, cb)


def ffn_mid_bwd(name, hid, cw, cb, dact, layer, *, m, seq):
    tm, n_t, n_i = _ffn_tiles(m, seq)
    hb = tm // HALO
    last_blk = m // HALO - 1

    def body(x_ref, xb_ref, xa_ref, cw_ref, cb_ref, da_ref, daa_ref, dx_ref, dcw_ref, dcb_ref):
        i = pl.program_id(1)
        first = i % n_t == 0
        last = i % n_t == n_t - 1
        before = jnp.where(first, 0.0, xb_ref[...])
        ext = jnp.concatenate([before, x_ref[...], xa_ref[...]], axis=1)
        dact_ext = jnp.concatenate([da_ref[...].astype(F32), jnp.where(last, 0.0, daa_ref[...].astype(F32))], axis=0)
        (cg, cv), vjp_conv = jax.vjp(_ffn_conv, ext[0], ext[1], cw_ref[...], cb_ref[...])
        _, vjp_gate = jax.vjp(_ffn_gate, cg, cv)
        dcg, dcv = vjp_gate(dact_ext)
        dxg, dxv, _, _ = vjp_conv((dcg, dcv))
        dx_ref[0] = dxg[HALO:HALO + tm].astype(dx_ref.dtype)
        dx_ref[1] = dxv[HALO:HALO + tm].astype(dx_ref.dtype)
        own = lax.broadcasted_iota(jnp.int32, dcg.shape, 0) < tm
        _, _, dcw, dcb = vjp_conv((jnp.where(own, dcg, 0.0), jnp.where(own, dcv, 0.0)))

        @pl.when(i == 0)
        def _():
            dcw_ref[...] = jnp.zeros_like(dcw_ref)
            dcb_ref[...] = jnp.zeros_like(dcb_ref)

        dcw_ref[...] += dcw
        dcb_ref[...] += dcb

    return pl.pallas_call(
        body, grid=(N_DEV // 2, n_i), name=name,
        in_specs=[pl.BlockSpec((2, None, tm, FF_BLK), lambda d, i: (0, d, i, 0)),
                  pl.BlockSpec((2, None, HALO, FF_BLK), lambda d, i: (0, d, jnp.maximum(i * hb - 1, 0), 0)),
                  pl.BlockSpec((2, None, HALO, FF_BLK), lambda d, i: (0, d, jnp.minimum((i + 1) * hb, last_blk), 0)),
                  pl.BlockSpec((2, None, None, FFN_CONV, FF_BLK), lambda d, i: (0, d, layer, 0, 0)),
                  pl.BlockSpec((None, 2, None, 1, FF_BLK), lambda d, i: (layer, 0, d, 0, 0)),
                  pl.BlockSpec((None, tm, FF_BLK), lambda d, i: (d, i, 0)),
                  pl.BlockSpec((None, HALO, FF_BLK), lambda d, i: (d, jnp.minimum((i + 1) * hb, last_blk), 0))],
        out_specs=[pl.BlockSpec((2, None, tm, FF_BLK), lambda d, i: (0, d, i, 0)),
                   pl.BlockSpec((2, None, FFN_CONV, FF_BLK), lambda d, i: (0, d, 0, 0)),
                   pl.BlockSpec((2, None, 1, FF_BLK), lambda d, i: (0, d, 0, 0))],
        out_shape=[jax.ShapeDtypeStruct((2, N_DEV // 2, m, FF_BLK), BF16),
                   jax.ShapeDtypeStruct((2, N_DEV // 2, FFN_CONV, FF_BLK), F32),
                   jax.ShapeDtypeStruct((2, N_DEV // 2, 1, FF_BLK), F32)],
        compiler_params=_params(("arbitrary", "arbitrary")),
    )(hid, hid, hid, cw, cb, dact, dact)


ATT_BLK = 512
N_PAIR = C_HEADS // 2
TERM_W = C_HEADS * LANES


def term_placement():
    import numpy as np
    place = np.zeros((6, LANES, TERM_W), np.float32)
    ones_q = np.zeros((1, TERM_W), np.float32)
    ones_k = np.zeros((1, TERM_W), np.float32)
    for h in range(C_HEADS):
        for j in range(3):
            place[j, h, h * LANES + C_HEAD_DIM + j] = 1.0
            place[3 + j, h, h * LANES + C_HEAD_DIM + 3 + j] = 1.0
            ones_q[0, h * LANES + C_HEAD_DIM + 3 + j] = 1.0
            ones_k[0, h * LANES + C_HEAD_DIM + j] = 1.0
    return (jnp.asarray(place.reshape(6 * LANES, TERM_W), BF16), jnp.asarray(ones_q, F32), jnp.asarray(ones_k, F32))


def fn_fox_terms(c, place, ones_q, ones_k):
    parts = _split3(c)
    qt = ones_q
    kt = ones_k
    for j in range(3):
        qt = qt + _dg(parts[j], place[j * LANES:(j + 1) * LANES], "nn")
        kt = kt - _dg(parts[j], place[(3 + j) * LANES:(4 + j) * LANES], "nn")
    return qt.astype(BF16), kt.astype(BF16)


def _head_tile(z, terms, e):
    lane = lax.broadcasted_iota(jnp.int32, z.shape, 1)
    base = z if e == 0 else pltpu.roll(z, C_HEAD_DIM, 1)
    return jnp.where(lane < C_HEAD_DIM, base, terms.astype(z.dtype))


def _head_only(z, e):
    lane = lax.broadcasted_iota(jnp.int32, z.shape, 1)
    mine = (lane < C_HEAD_DIM) if e == 0 else (lane >= C_HEAD_DIM)
    return jnp.where(mine, z, jnp.zeros_like(z)).astype(BF16)


def _pair_tile(a0, a1):
    lane = lax.broadcasted_iota(jnp.int32, a0.shape, 1)
    return jnp.where(lane < C_HEAD_DIM, a0, pltpu.roll(a1, C_HEAD_DIM, 1))


def _lane_col(a, k):
    lane = lax.broadcasted_iota(jnp.int32, a.shape, 1)
    return jnp.sum(jnp.where(lane == k, a, 0.0), axis=1, keepdims=True)


def _causal(s):
    key = lax.broadcasted_iota(jnp.int32, s.shape, 0)
    qry = lax.broadcasted_iota(jnp.int32, s.shape, 1)
    return qry >= key


def fox_pair_fwd(name, z, qterm, kterm, *, n_batch, seq):
    m = n_batch * seq
    blk = min(ATT_BLK, seq)
    nq = seq // blk
    dh = C_HEAD_DIM

    def body(zq_ref, zk_ref, zv_ref, qt_ref, kt_ref, o_ref, lse_ref, ka_ref, vt_ref):
        qi = pl.program_id(2)

        @pl.when(qi == 0)
        def _():
            zk = zk_ref[...]
            for e in range(2):
                ka_ref[e] = _head_tile(zk, kt_ref[:, e * LANES:(e + 1) * LANES], e).astype(BF16)
            for cb in range(nq):
                vt_ref[cb] = zv_ref[cb * blk:(cb + 1) * blk, :].T.astype(BF16)

        zq = zq_ref[...] * dh ** -0.5
        qa = [_head_tile(zq, qt_ref[:, e * LANES:(e + 1) * LANES], e).astype(BF16) for e in range(2)]

        def block(j, carry, diagonal):
            rows = pl.ds(pl.multiple_of(j * blk, blk), blk)
            out = []
            for e in range(2):
                mx, l, acc = carry[e]
                s = _dg(ka_ref[e, rows, :], qa[e], "nt")
                if diagonal:
                    s = jnp.where(_causal(s), s, NEG)
                mx_new = jnp.maximum(mx, jnp.max(s, axis=0, keepdims=True))
                p = jnp.exp(s - mx_new)
                alpha = jnp.exp(mx - mx_new)
                l = alpha * l + jnp.sum(p, axis=0, keepdims=True)
                acc = alpha * acc + _dg(vt_ref[j, e * dh:(e + 1) * dh, :], p, "nn")
                out.append((mx_new, l, acc))
            return tuple(out)

        one = (jnp.full((1, blk), NEG, F32), jnp.zeros((1, blk), F32), jnp.zeros((dh, blk), F32))
        carry = lax.fori_loop(0, qi, lambda j, cr: block(j, cr, False), (one, one))
        res = block(qi, carry, True)
        ot = jnp.concatenate([res[e][2] / res[e][1] for e in range(2)], axis=0)
        o_ref[...] = ot.T.astype(o_ref.dtype)
        for e in range(2):
            lse_ref[e] = res[e][0] + jnp.log(res[e][1])

    col = lambda part: (lambda b, g, i: (b, part * N_PAIR + g))
    return pl.pallas_call(
        body, grid=(n_batch, N_PAIR, nq), name=name,
        in_specs=[pl.BlockSpec((blk, LANES), lambda b, g, i: (b * nq + i, g)),
                  pl.BlockSpec((seq, LANES), col(1)),
                  pl.BlockSpec((seq, LANES), col(2)),
                  pl.BlockSpec((blk, 2 * LANES), lambda b, g, i: (b * nq + i, g)),
                  pl.BlockSpec((seq, 2 * LANES), lambda b, g, i: (b, g))],
        out_specs=[pl.BlockSpec((blk, LANES), lambda b, g, i: (b * nq + i, g)),
                   pl.BlockSpec((None, None, None, 2, 1, blk), lambda b, g, i: (b, g, i, 0, 0, 0))],
        out_shape=[jax.ShapeDtypeStruct((m, D_MODEL), BF16), jax.ShapeDtypeStruct((n_batch, N_PAIR, nq, 2, 1, blk), F32)],
        scratch_shapes=[pltpu.VMEM((2, seq, LANES), BF16), pltpu.VMEM((nq, LANES, blk), BF16)],
        compiler_params=_params(("parallel", "parallel", "arbitrary")),
    )(z, z, z, qterm, kterm)


def fox_pair_bwd(name, z, qterm, kterm, o, do, lse, *, n_batch, seq):
    m = n_batch * seq
    blk = min(ATT_BLK, seq)
    nq = seq // blk
    dh = C_HEAD_DIM

    def body(zq_ref, zk_ref, zv_ref, qt_ref, kt_ref, o_ref, do_ref, lse_ref, dq_ref, dk_ref, dv_ref, dc_ref,
             qa_ref, doh_ref, del_ref, dqt_ref, dk_acc, dv_acc):
        g, j = pl.program_id(1), pl.program_id(2)
        lane = lax.broadcasted_iota(jnp.int32, (blk, LANES), 1)

        @pl.when(jnp.logical_and(g == 0, j == 0))
        def _():
            dc_ref[...] = jnp.zeros_like(dc_ref)

        @pl.when(j == 0)
        def _():
            zq = zq_ref[...] * dh ** -0.5
            dov = do_ref[...]
            for e in range(2):
                qa_ref[e] = _head_tile(zq, qt_ref[:, e * LANES:(e + 1) * LANES], e).astype(BF16)
                doh_ref[e] = _head_only(dov, e)
            for cb in range(nq):
                rows = slice(cb * blk, (cb + 1) * blk)
                prod_t = (do_ref[rows, :].astype(F32) * o_ref[rows, :].astype(F32)).T
                for e in range(2):
                    del_ref[cb, e] = jnp.sum(prod_t[e * dh:(e + 1) * dh], axis=0, keepdims=True)
            dqt_ref[...] = jnp.zeros_like(dqt_ref)

        zk, zv = zk_ref[...], zv_ref[...]
        ka32 = [_head_tile(zk, kt_ref[:, e * LANES:(e + 1) * LANES], e) for e in range(2)]
        ka = [t.astype(BF16) for t in ka32]
        kat = [t.T.astype(BF16) for t in ka32]
        vh = [_head_only(zv, e) for e in range(2)]
        dk_acc[...] = jnp.zeros_like(dk_acc)
        dv_acc[...] = jnp.zeros_like(dv_acc)

        def block(i, diagonal):
            rows = pl.ds(pl.multiple_of(i * blk, blk), blk)
            for e in range(2):
                qv, dov = qa_ref[e, rows, :], doh_ref[e, rows, :]
                p = jnp.exp(_dg(ka[e], qv, "nt") - lse_ref[i, e])
                if diagonal:
                    p = jnp.where(_causal(p), p, 0.0)
                dv_acc[...] += _dg(p, dov, "nn")
                ds = p * (_dg(vh[e], dov, "nt") - del_ref[i, e])
                dk_acc[e] += _dg(ds, qv, "nn")
                dqt_ref[i, e] += _dg(kat[e], ds, "nn")

        block(j, True)

        def rest(i, carry):
            block(i, False)
            return carry

        lax.fori_loop(j + 1, nq, rest, 0)
        dk0, dk1 = dk_acc[0], dk_acc[1]
        dk_ref[...] = _pair_tile(dk0, dk1).astype(dk_ref.dtype)
        dv_ref[...] = dv_acc[...].astype(dv_ref.dtype)
        rows_j = pl.ds(pl.multiple_of(j * blk, blk), blk)
        for e, dke in enumerate((dk0, dk1)):
            dc_ref[rows_j, :] -= jnp.where(lane == 2 * g + e, _lane_col(dke, dh + 3), 0.0)

        @pl.when(j == nq - 1)
        def _():
            for i in range(nq):
                nat = [dqt_ref[i, e].T for e in range(2)]
                rows = slice(i * blk, (i + 1) * blk)
                dq_ref[rows, :] = (_pair_tile(nat[0], nat[1]) * dh ** -0.5).astype(dq_ref.dtype)
                for e in range(2):
                    dc_ref[rows, :] += jnp.where(lane == 2 * g + e, _lane_col(nat[e], dh), 0.0)

    col = lambda part: (lambda b, g, j: (b, part * N_PAIR + g))
    colj = lambda part: (lambda b, g, j: (b * nq + j, part * N_PAIR + g))
    pair = jax.ShapeDtypeStruct((m, D_MODEL), BF16)
    return pl.pallas_call(
        body, grid=(n_batch, N_PAIR, nq), name=name,
        in_specs=[pl.BlockSpec((seq, LANES), col(0)),
                  pl.BlockSpec((blk, LANES), colj(1)),
                  pl.BlockSpec((blk, LANES), colj(2)),
                  pl.BlockSpec((seq, 2 * LANES), lambda b, g, j: (b, g)),
                  pl.BlockSpec((blk, 2 * LANES), lambda b, g, j: (b * nq + j, g)),
                  pl.BlockSpec((seq, LANES), col(0)),
                  pl.BlockSpec((seq, LANES), col(0)),
                  pl.BlockSpec((None, None, nq, 2, 1, blk), lambda b, g, j: (b, g, 0, 0, 0, 0))],
        out_specs=[pl.BlockSpec((seq, LANES), col(0)),
                   pl.BlockSpec((blk, LANES), colj(0)),
                   pl.BlockSpec((blk, LANES), colj(0)),
                   pl.BlockSpec((seq, LANES), lambda b, g, j: (b, 0))],
        out_shape=[pair, pair, pair, jax.ShapeDtypeStruct((m, LANES), F32)],
        scratch_shapes=[pltpu.VMEM((2, seq, LANES), BF16), pltpu.VMEM((2, seq, LANES), BF16),
                        pltpu.VMEM((nq, 2, 1, blk), F32), pltpu.VMEM((nq, 2, LANES, blk), F32),
                        pltpu.VMEM((2, blk, LANES), F32), pltpu.VMEM((blk, LANES), F32)],
        compiler_params=_params(("arbitrary", "arbitrary", "arbitrary")),
    )(z, z, z, qterm, kterm, o, do, lse)


def _split3(c):
    c1 = c.astype(BF16)
    r1 = c - c1.astype(F32)
    c2 = r1.astype(BF16)
    c3 = (r1 - c2.astype(F32)).astype(BF16)
    return c1, c2, c3


def fox_operands(q, k, c):
    bh, seq, dh = q.shape
    c1, c2, c3 = (t[..., None] for t in _split3(c))
    one = jnp.ones((bh, seq, 1), BF16)
    pad = jnp.zeros((bh, seq, LANES - dh - 6), BF16)
    qa = jnp.concatenate([(q * dh ** -0.5).astype(BF16), c1, c2, c3, one, one, one, pad], axis=-1)
    ka = jnp.concatenate([k.astype(BF16), one, one, one, -c1, -c2, -c3, pad], axis=-1)
    return qa, ka


def fox_fwd(name, qa, ka, vt):
    bh, seq, da = qa.shape
    blk = min(ATT_BLK, seq)
    nq = seq // blk
    dh = vt.shape[2]

    def body(q_ref, k_ref, v_ref, o_ref, lse_ref):
        qi = pl.program_id(1)
        qv = q_ref[0]

        def block(j, carry, diagonal):
            mx, l, acc = carry
            kj = k_ref[0, pl.ds(pl.multiple_of(j * blk, blk), blk), :]
            s = _dg(kj, qv, "nt")
            if diagonal:
                key = lax.broadcasted_iota(jnp.int32, (blk, blk), 0)
                qry = lax.broadcasted_iota(jnp.int32, (blk, blk), 1)
                s = jnp.where(qry >= key, s, NEG)
            mx_new = jnp.maximum(mx, jnp.max(s, axis=0, keepdims=True))
            p = jnp.exp(s - mx_new)
            alpha = jnp.exp(mx - mx_new)
            l = alpha * l + jnp.sum(p, axis=0, keepdims=True)
            acc = alpha * acc + _dg(v_ref[0, j], p, "nn")
            return mx_new, l, acc

        init = (jnp.full((1, blk), NEG, F32), jnp.zeros((1, blk), F32), jnp.zeros((dh, blk), F32))
        carry = lax.fori_loop(0, qi, lambda j, cr: block(j, cr, False), init)
        mx, l, acc = block(qi, carry, True)
        o_ref[0] = (acc / l).astype(o_ref.dtype)
        lse_ref[0, 0] = mx + jnp.log(l)

    return pl.pallas_call(
        body, grid=(bh, nq), name=name,
        in_specs=[pl.BlockSpec((1, blk, da), lambda b, i: (b, i, 0)),
                  pl.BlockSpec((1, seq, da), lambda b, i: (b, 0, 0)),
                  pl.BlockSpec((1, nq, dh, blk), lambda b, i: (b, 0, 0, 0))],
        out_specs=[pl.BlockSpec((1, dh, blk), lambda b, i: (b, 0, i)),
                   pl.BlockSpec((1, 1, 1, blk), lambda b, i: (b, i, 0, 0))],
        out_shape=[jax.ShapeDtypeStruct((bh, dh, seq), BF16), jax.ShapeDtypeStruct((bh, nq, 1, blk), F32)],
        compiler_params=_params(("parallel", "arbitrary")),
    )(qa, ka, vt)


def fox_bwd(name, qa, ka, kat, v, do, dot, ot, lse):
    bh, seq, da = qa.shape
    blk = min(ATT_BLK, seq)
    nq = seq // blk
    dh = v.shape[2]

    def body(q_ref, k_ref, kt_ref, v_ref, do_ref, dot_ref, ot_ref, lse_ref, dq_ref, dk_ref, dv_ref, del_ref):
        j = pl.program_id(1)

        @pl.when(j == 0)
        def _():
            dq_ref[...] = jnp.zeros_like(dq_ref)
            for i in range(nq):
                cols = slice(i * blk, (i + 1) * blk)
                del_ref[i] = jnp.sum(dot_ref[0, :, cols].astype(F32) * ot_ref[0, :, cols].astype(F32), axis=0, keepdims=True)

        kj, kjt, vj = k_ref[0], kt_ref[0, 0], v_ref[0]

        def block(i, carry, diagonal):
            dk, dv = carry
            rows = pl.ds(pl.multiple_of(i * blk, blk), blk)
            qv, dov = q_ref[0, rows, :], do_ref[0, rows, :]
            p = jnp.exp(_dg(kj, qv, "nt") - lse_ref[0, i])
            if diagonal:
                key = lax.broadcasted_iota(jnp.int32, (blk, blk), 0)
                qry = lax.broadcasted_iota(jnp.int32, (blk, blk), 1)
                p = jnp.where(qry >= key, p, 0.0)
            dv = dv + _dg(p, dov, "nn")
            ds = p * (_dg(vj, dov, "nt") - del_ref[i])
            dk = dk + _dg(ds, qv, "nn")
            dq_ref[0, i] += _dg(kjt, ds, "nn")
            return dk, dv

        init = (jnp.zeros((blk, da), F32), jnp.zeros((blk, dh), F32))
        carry = block(j, init, True)
        dk, dv = lax.fori_loop(j + 1, nq, lambda i, cr: block(i, cr, False), carry)
        dk_ref[0] = dk
        dv_ref[0] = dv

    return pl.pallas_call(
        body, grid=(bh, nq), name=name,
        in_specs=[pl.BlockSpec((1, seq, da), lambda b, j: (b, 0, 0)),
                  pl.BlockSpec((1, blk, da), lambda b, j: (b, j, 0)),
                  pl.BlockSpec((1, 1, da, blk), lambda b, j: (b, j, 0, 0)),
                  pl.BlockSpec((1, blk, dh), lambda b, j: (b, j, 0)),
                  pl.BlockSpec((1, seq, dh), lambda b, j: (b, 0, 0)),
                  pl.BlockSpec((1, dh, seq), lambda b, j: (b, 0, 0)),
                  pl.BlockSpec((1, dh, seq), lambda b, j: (b, 0, 0)),
                  pl.BlockSpec((1, nq, 1, blk), lambda b, j: (b, 0, 0, 0))],
        out_specs=[pl.BlockSpec((1, nq, da, blk), lambda b, j: (b, 0, 0, 0)),
                   pl.BlockSpec((1, blk, da), lambda b, j: (b, j, 0)),
                   pl.BlockSpec((1, blk, dh), lambda b, j: (b, j, 0))],
        out_shape=[jax.ShapeDtypeStruct((bh, nq, da, blk), F32), jax.ShapeDtypeStruct((bh, seq, da), F32),
                   jax.ShapeDtypeStruct((bh, seq, dh), F32)],
        scratch_shapes=[pltpu.VMEM((nq, 1, blk), F32)],
        compiler_params=_params(("parallel", "arbitrary")),
    )(qa, ka, kat, v, do, dot, ot, lse)


def _mesh_pos():
    return lax.axis_index("x"), lax.axis_index("y"), lax.axis_index("c")


def _flip(v, bit):
    return 1 - v if bit else v


def all_gather(name, blocks):
    n = len(blocks)

    def body(*refs):
        x_refs, out_refs = refs[:n], refs[n:2 * n]
        send_sems, recv_sems, local_sems = refs[2 * n:]
        x, y, c = _mesh_pos()
        me, sibling = (x, y, c), (x, y, 1 - c)
        chips = [(1 - x, y), (x, 1 - y), (1 - x, 1 - y)]

        def slot(a, px, py, pc):
            return out_refs[a].at[4 * px + 2 * py + pc]

        def copy(a, k, blk, to, src=None):
            return pltpu.make_async_remote_copy(
                src_ref=slot(a, *blk) if src is None else src, dst_ref=slot(a, *blk),
                send_sem=send_sems.at[a, k], recv_sem=recv_sems.at[a, k], device_id=to, device_id_type=MESH)

        mine = [pltpu.make_async_copy(x_refs[a], slot(a, *me), local_sems.at[a]) for a in range(n)]
        for cp in mine:
            cp.start()
        sends = []
        for a in range(n):
            sends.append(copy(a, 0, me, sibling, src=x_refs[a]))
            sends += [copy(a, 1 + j, me, (*chip, c), src=x_refs[a]) for j, chip in enumerate(chips)]
        for cp in sends:
            cp.start()
        for j, chip in enumerate(chips):
            for a in range(n):
                copy(a, 1 + j, (*chip, c), me).wait_recv()
                passed = copy(a, 4 + j, (*chip, c), sibling)
                passed.start()
                sends.append(passed)
        for a in range(n):
            copy(a, 0, sibling, me).wait_recv()
            for j, chip in enumerate(chips):
                copy(a, 4 + j, (*chip, 1 - c), me).wait_recv()
        for cp in sends:
            cp.wait_send()
        for cp in mine:
            cp.wait()

    hbm = pl.BlockSpec(memory_space=pl.ANY)
    return pl.pallas_call(
        body, name=name, out_shape=[jax.ShapeDtypeStruct((N_DEV,) + b.shape, b.dtype) for b in blocks],
        in_specs=[hbm] * n, out_specs=[hbm] * n,
        scratch_shapes=[pltpu.SemaphoreType.DMA((n, 7)), pltpu.SemaphoreType.DMA((n, 7)), pltpu.SemaphoreType.DMA((n,))],
    )(*blocks)


def all_to_all(name, sends):
    n = len(sends)

    def body(*refs):
        s_refs, r_refs = refs[:n], refs[n:2 * n]
        send_sems, recv_sems, local_sems = refs[2 * n:]
        x, y, c = _mesh_pos()
        me = 4 * x + 2 * y + c
        mine = [pltpu.make_async_copy(s_refs[a].at[me], r_refs[a].at[me], local_sems.at[a]) for a in range(n)]
        for cp in mine:
            cp.start()
        copies = []
        for k in range(1, N_DEV):
            px, py, pc = _flip(x, k & 4), _flip(y, k & 2), _flip(c, k & 1)
            for a in range(n):
                copies.append(pltpu.make_async_remote_copy(
                    src_ref=s_refs[a].at[4 * px + 2 * py + pc], dst_ref=r_refs[a].at[me],
                    send_sem=send_sems.at[a, k - 1], recv_sem=recv_sems.at[a, k - 1],
                    device_id=(px, py, pc), device_id_type=MESH))
        for cp in copies:
            cp.start()
        for cp in copies:
            cp.wait_recv()
        for cp in copies:
            cp.wait_send()
        for cp in mine:
            cp.wait()

    hbm = pl.BlockSpec(memory_space=pl.ANY)
    return pl.pallas_call(
        body, name=name, out_shape=[jax.ShapeDtypeStruct(s.shape, s.dtype) for s in sends],
        in_specs=[hbm] * n, out_specs=[hbm] * n,
        scratch_shapes=[pltpu.SemaphoreType.DMA((n, 7)), pltpu.SemaphoreType.DMA((n, 7)), pltpu.SemaphoreType.DMA((n,))],
    )(*sends)


def pair_exchange(name, hs):
    n = len(hs)

    def body(*refs):
        h_refs, r_refs = refs[:n], refs[n:2 * n]
        send_sems, recv_sems = refs[2 * n:]
        x, y, c = _mesh_pos()
        copies = [pltpu.make_async_remote_copy(
            src_ref=h_refs[a].at[1 - c], dst_ref=r_refs[a], send_sem=send_sems.at[a], recv_sem=recv_sems.at[a],
            device_id=(x, y, 1 - c), device_id_type=MESH) for a in range(n)]
        for cp in copies:
            cp.start()
        for cp in copies:
            cp.wait_recv()
        for cp in copies:
            cp.wait_send()

    hbm = pl.BlockSpec(memory_space=pl.ANY)
    return pl.pallas_call(
        body, name=name, out_shape=[jax.ShapeDtypeStruct(h.shape[1:], h.dtype) for h in hs],
        in_specs=[hbm] * n, out_specs=[hbm] * n,
        scratch_shapes=[pltpu.SemaphoreType.DMA((n,)), pltpu.SemaphoreType.DMA((n,))],
    )(*hs)


def quad_exchange(name, ss):
    n = len(ss)

    def body(*refs):
        s_refs, r_refs = refs[:n], refs[n:2 * n]
        send_sems, recv_sems, local_sems = refs[2 * n:]
        x, y, c = _mesh_pos()
        me = 2 * x + y
        mine = [pltpu.make_async_copy(s_refs[a].at[me], r_refs[a].at[me], local_sems.at[a]) for a in range(n)]
        for cp in mine:
            cp.start()
        copies = []
        for k in range(1, 4):
            px, py = _flip(x, k & 2), _flip(y, k & 1)
            for a in range(n):
                copies.append(pltpu.make_async_remote_copy(
                    src_ref=s_refs[a].at[2 * px + py], dst_ref=r_refs[a].at[me],
                    send_sem=send_sems.at[a, k - 1], recv_sem=recv_sems.at[a, k - 1],
                    device_id=(px, py, c), device_id_type=MESH))
        for cp in copies:
            cp.start()
        for cp in copies:
            cp.wait_recv()
        for cp in copies:
            cp.wait_send()
        for cp in mine:
            cp.wait()

    hbm = pl.BlockSpec(memory_space=pl.ANY)
    return pl.pallas_call(
        body, name=name, out_shape=[jax.ShapeDtypeStruct(s.shape, s.dtype) for s in ss],
        in_specs=[hbm] * n, out_specs=[hbm] * n,
        scratch_shapes=[pltpu.SemaphoreType.DMA((n, 3)), pltpu.SemaphoreType.DMA((n, 3)), pltpu.SemaphoreType.DMA((n,))],
    )(*ss)


def _rows_cols(shape):
    r = 1
    for d in shape[:-1]:
        r *= d
    return r, shape[-1]


def _row_tile(r, cap, step):
    return next((t for t in range(cap, step - 1, -step) if r % t == 0), r)


def pair_add(name, h, recv, core):
    shape = recv.shape
    r, c = _rows_cols(shape[1:])
    tr = _row_tile(r, 256, 16)

    def body(core_ref, h_ref, r_ref, o_ref):
        o_ref[...] = (h_ref[...].astype(F32) + r_ref[...].astype(F32)).astype(o_ref.dtype)

    spec = pl.BlockSpec((None, tr, c), lambda q, i, core_ref: (q, i, 0))
    res = pl.pallas_call(
        body, name=name, out_shape=jax.ShapeDtypeStruct((4, r, c), h.dtype),
        grid_spec=pltpu.PrefetchScalarGridSpec(
            num_scalar_prefetch=1, grid=(4, r // tr),
            in_specs=[pl.BlockSpec((None, None, tr, c), lambda q, i, core_ref: (core_ref[0], q, i, 0)), spec],
            out_specs=spec),
        compiler_params=_params(("parallel", "parallel")),
    )(core, h.reshape(2, 4, r, c), recv.reshape(4, r, c))
    return res.reshape(shape)


def _sum_parts(p, n):
    t = [p[k].astype(F32) for k in range(n)]
    while len(t) > 1:
        t = [t[k] + t[k + 1] for k in range(0, len(t), 2)]
    return t[0]


def _adam(g, w, m, v):
    m = ADAM_B1 * m + (1.0 - ADAM_B1) * g
    v = ADAM_B2 * v + (1.0 - ADAM_B2) * (g * g)
    m_hat = m / (1.0 - ADAM_B1 ** ADAM_STEP)
    v_hat = v / (1.0 - ADAM_B2 ** ADAM_STEP)
    return -ADAM_LR * (m_hat / (jnp.sqrt(v_hat) + ADAM_EPS) + ADAM_WD * w), m, v


def adam_tiled(name, partials, w, m_, v_):
    shape = w.shape
    n_part = partials.shape[0]
    r, c = _rows_cols(shape)
    tr = _row_tile(r, 256, 16)

    def body(p_ref, w_ref, m_ref, v_ref, g_ref, d_ref, nm_ref, nv_ref):
        g = _sum_parts(p_ref, n_part)
        g_ref[...] = g
        d_ref[...], nm_ref[...], nv_ref[...] = _adam(g, w_ref[...], m_ref[...], v_ref[...])

    spec = pl.BlockSpec((tr, c), lambda i: (i, 0))
    res = pl.pallas_call(
        body, grid=(r // tr,), name=name,
        in_specs=[pl.BlockSpec((n_part, tr, c), lambda i: (0, i, 0)), spec, spec, spec],
        out_specs=[spec] * 4, out_shape=[jax.ShapeDtypeStruct((r, c), F32)] * 4,
        compiler_params=_params(("parallel",)),
    )(partials.reshape(n_part, r, c), w.reshape(r, c), m_.reshape(r, c), v_.reshape(r, c))
    return [t.reshape(shape) for t in res]


def adam_small(name, items, extra):
    n, ne = len(items), len(extra)

    def body(*refs):
        ins, outs = refs[:4 * n + ne], refs[4 * n + ne:]
        for a in range(n):
            p_ref, w_ref, m_ref, v_ref = ins[4 * a:4 * a + 4]
            g = _sum_parts(p_ref, N_DEV)
            outs[4 * a][...] = g
            outs[4 * a + 1][...], outs[4 * a + 2][...], outs[4 * a + 3][...] = _adam(g, w_ref[...], m_ref[...], v_ref[...])
        for e in range(ne):
            outs[4 * n + e][...] = _sum_parts(ins[4 * n + e], N_DEV)

    args, out_shape = [], []
    for p, w, m_, v_ in items:
        args += [p, w, m_, v_]
        out_shape += [jax.ShapeDtypeStruct(w.shape, F32)] * 4
    for e in extra:
        args.append(e)
        out_shape.append(jax.ShapeDtypeStruct(e.shape[1:], F32))
    vmem = pl.BlockSpec(memory_space=pltpu.VMEM)
    res = pl.pallas_call(body, name=name, in_specs=[vmem] * len(args), out_specs=[vmem] * len(out_shape), out_shape=out_shape)(*args)
    return [res[4 * a:4 * a + 4] for a in range(n)], res[4 * n:]


def _cols_from_gather(g):
    g = jnp.moveaxis(g, 0, -2)
    return g.reshape(g.shape[:-2] + (g.shape[-2] * g.shape[-1],))


def _cols_to_blocks(w):
    w = w.reshape(w.shape[:-1] + (N_DEV, w.shape[-1] // N_DEV))
    return jnp.moveaxis(w, -2, 0)


def _block_diag(w):
    z = jnp.zeros((B_BLOCK_DIM, B_BLOCK_DIM), w.dtype)
    rows = []
    for j in range(B_BLOCKS // 2):
        top = jnp.concatenate([w[2 * j], z], axis=1)
        bot = jnp.concatenate([z, w[2 * j + 1]], axis=1)
        rows.append(jnp.concatenate([top, bot], axis=0))
    return jnp.concatenate(rows, axis=0)


def _block_diag_grad(d):
    out = []
    for j in range(B_BLOCKS // 2):
        blk = d[128 * j:128 * (j + 1)]
        out.append(blk[:64, :64])
        out.append(blk[64:, 64:])
    return jnp.stack(out)


NAMES = ("norm_gains", "even_w_in", "hgrn_lb_logits", "hgrn_norm", "rg_conv_w", "rg_conv_b", "rg_wa", "rg_ba", "rg_wx", "rg_bx",
         "rg_lambda", "even_w_out", "odd_w_in", "fox_f_bias", "odd_w_out", "ffn_w_up", "ffn_conv_w", "ffn_conv_b", "ffn_w_down")
BIG = ("even_w_in", "even_w_out", "odd_w_in", "odd_w_out", "ffn_w_up", "ffn_w_down")
SMALL_SHARDED = ("norm_gains", "rg_conv_w", "ffn_conv_w")
REPLICATED = ("hgrn_lb_logits", "hgrn_norm", "rg_conv_b", "rg_wa", "rg_ba", "rg_wx", "rg_bx", "rg_lambda", "fox_f_bias", "ffn_conv_b")


def _ffn_forward(tag, layer, h, w_up_g, cw5, cb5, w_down_g, m, seq):
    tm = _div_tile(m, 1024)
    nm = m // tm
    hid = mm(f"{tag}_up", "nn",
             Blk(h, (tm, D_MODEL), lambda i, j, k: (i, 0)),
             Blk(w_up_g, (None, None, D_MODEL, FF_BLK), lambda i, j, k: (j, layer, 0, 0)),
             Blk((N_DEV, m, FF_BLK), (None, tm, FF_BLK), lambda i, j, k: (j, i, 0)), F32, (nm, N_DEV, 1))
    hid = hid.reshape(2, N_DEV // 2, m, FF_BLK)
    act = ffn_mid_fwd(f"{tag}_mid", hid, cw5, cb5, layer, m=m, seq=seq)
    f = mm(f"{tag}_down", "nn",
           Blk(act, (None, tm, FF_BLK), lambda i, j, k: (k, i, 0)),
           Blk(w_down_g, (2, None, FF_BLK // 2, D_MODEL), lambda i, j, k: (k, layer, 0, 0)),
           Blk((m, D_MODEL), (tm, D_MODEL), lambda i, j, k: (i, 0)), F32, (nm, 1, N_DEV // 2))
    return hid, act, f


def _ffn_backward(tag, layer, df, h, hid, act, w_up_g, cw5, cb5, w_down_g, d_wup, d_wdown, m, seq):
    tm = _div_tile(m, 1024)
    nm = m // tm
    dact = mm(f"{tag}_dact", "nt",
              Blk(df, (tm, D_MODEL), lambda i, j, k: (i, 0)),
              Blk(w_down_g, (2, None, FF_BLK // 2, D_MODEL), lambda i, j, k: (j, layer, 0, 0)),
              Blk((N_DEV // 2, m, FF_BLK), (None, tm, FF_BLK), lambda i, j, k: (j, i, 0)), BF16, (nm, N_DEV // 2, 1))
    d_wdown = mm(f"{tag}_dwdown", "tn",
                 Blk(act, (None, tm, FF_BLK), lambda i, j, k: (i, k, 0)),
                 Blk(df, (tm, D_MODEL), lambda i, j, k: (k, 0)),
                 Blk((2, 4) + w_down_g.shape[1:], (2, None, None, FF_BLK // 2, D_MODEL), lambda i, j, k: (0, i, layer, 0, 0)), BF16,
                 (N_DEV // 2, 1, nm), into=d_wdown)
    dhid, d_cw, d_cb = ffn_mid_bwd(f"{tag}_dmid", hid, cw5, cb5, dact, layer, m=m, seq=seq)
    dhid = dhid.reshape(N_DEV, m, FF_BLK)
    dh = mm(f"{tag}_dh", "nt",
            Blk(dhid, (None, tm, FF_BLK), lambda i, j, k: (k, i, 0)),
            Blk(w_up_g, (None, None, D_MODEL, FF_BLK), lambda i, j, k: (k, layer, 0, 0)),
            Blk((m, D_MODEL), (tm, D_MODEL), lambda i, j, k: (i, 0)), BF16, (nm, 1, N_DEV))
    d_wup = mm(f"{tag}_dwup", "tn",
               Blk(h, (tm, D_MODEL), lambda i, j, k: (k, 0)),
               Blk(dhid, (None, tm, FF_BLK), lambda i, j, k: (j, k, 0)),
               Blk((2, 4) + w_up_g.shape[1:], (None, None, None, D_MODEL, FF_BLK), lambda i, j, k: (j % 2, j // 2, layer, 0, 0)), BF16,
               (1, N_DEV, nm), into=d_wup)
    return dh, d_wup, d_cw, d_cb, d_wdown


def kernel(x, norm_gains, even_w_in, hgrn_lb_logits, hgrn_norm, rg_conv_w, rg_conv_b, rg_wa, rg_ba, rg_wx, rg_bx, rg_lambda, even_w_out, odd_w_in, fox_f_bias, odd_w_out, ffn_w_up, ffn_conv_w, ffn_conv_b, ffn_w_down, loss_target, m_norm_gains, m_even_w_in, m_hgrn_lb_logits, m_hgrn_norm, m_rg_conv_w, m_rg_conv_b, m_rg_wa, m_rg_ba, m_rg_wx, m_rg_bx, m_rg_lambda, m_even_w_out, m_odd_w_in, m_fox_f_bias, m_odd_w_out, m_ffn_w_up, m_ffn_conv_w, m_ffn_conv_b, m_ffn_w_down, v_norm_gains, v_even_w_in, v_hgrn_lb_logits, v_hgrn_norm, v_rg_conv_w, v_rg_conv_b, v_rg_wa, v_rg_ba, v_rg_wx, v_rg_bx, v_rg_lambda, v_even_w_out, v_odd_w_in, v_fox_f_bias, v_odd_w_out, v_ffn_w_up, v_ffn_conv_w, v_ffn_conv_b, v_ffn_w_down):
    local = dict(locals())
    w = {n: local[n] for n in NAMES}
    mom = {n: local["m_" + n] for n in NAMES}
    var = {n: local["v_" + n] for n in NAMES}
    n_batch, seq, _ = x.shape
    m = n_batch * seq
    tm = _div_tile(m, 512)
    tmm = _div_tile(m, 1024)
    nm = m // tmm

    gathered = all_gather("gather_weights", [w[n].astype(BF16) for n in BIG] + [w[n] for n in SMALL_SHARDED])
    g = dict(zip(BIG + SMALL_SHARDED, gathered))
    w_in_e = g["even_w_in"]
    w_out_e = g["even_w_out"].reshape(D_MODEL, D_MODEL)
    w_in_o = jnp.pad(_cols_from_gather(g["odd_w_in"])[0], ((0, 0), (0, 3200 - 3088)))
    w_out_o = g["odd_w_out"].reshape(D_MODEL, D_MODEL)
    w_up_g, w_down_g = g["ffn_w_up"], g["ffn_w_down"]
    gains = _cols_from_gather(g["norm_gains"])
    rg_cw = _cols_from_gather(g["rg_conv_w"])[0]
    n_layer = ffn_conv_w.shape[0]
    cw5 = g["ffn_conv_w"].reshape(2, N_DEV // 2, n_layer, FFN_CONV, FF_BLK)
    cb5 = ffn_conv_b.reshape(n_layer, 2, N_DEV // 2, 1, FF_BLK)
    gain = lambda l, k: gains[l, k:k + 1, :]
    wa_bd, wx_bd = _block_diag(rg_wa[0]), _block_diag(rg_wx[0])
    fbias = jnp.pad(fox_f_bias, ((0, 0), (0, LANES - C_HEADS)))

    x0 = x.reshape(m, D_MODEL)
    tgt = loss_target.reshape(m, D_MODEL)

    (h0,) = tile_fwd("l0_prenorm", fn_prenorm, m=m, tm=tm, nj=1, rows=[Row(x0)], pars=[Par(gain(0, 0))], outs=[Out(D_MODEL, BF16)])
    z0 = mm("l0_in", "nn",
            Blk(h0, (tmm, D_MODEL), lambda i, j, k: (i, 0)),
            Blk(w_in_e, (None, None, D_MODEL, 384), lambda i, j, k: (j, 0, 0, 0)),
            Blk((m, 3072), (tmm, 384), lambda i, j, k: (i, j)), F32, (nm, N_DEV, 1))
    oa, sprev = hgrn_fwd("l0_hgrn", z0, hgrn_lb_logits, hgrn_norm, n_batch=n_batch, seq=seq)
    rg_rows = lambda: [Row(z0, LANES, 16), Row(z0, LANES, 20)]
    rg_pars = lambda: [Par(rg_cw, "col", LANES), Par(rg_conv_b, "col", LANES), Par(wa_bd, "row", LANES), Par(rg_ba, "col", LANES),
                       Par(wx_bd, "row", LANES), Par(rg_bx, "col", LANES), Par(rg_lambda, "col", LANES)]
    (ob,) = tile_fwd("l0_rglru", fn_rglru, m=m, tm=seq, nj=B_WIDTH // LANES, rows=rg_rows(), pars=rg_pars(),
                     outs=[Out(B_WIDTH, BF16, LANES)])
    mixcat0 = jnp.concatenate([oa, ob], axis=-1)
    mix0 = mm2d("l0_out", "nn", mixcat0, w_out_e)
    x1, h1 = tile_fwd("l0_postnorm", fn_addnorm2, m=m, tm=tm, nj=1, rows=[Row(x0), Row(mix0)], pars=[Par(gain(0, 1)), Par(gain(0, 2))],
                      outs=[Out(D_MODEL, F32), Out(D_MODEL, BF16)])
    hid0, act0, f0 = _ffn_forward("l0_ffn", 0, h1, w_up_g, cw5, cb5, w_down_g, m, seq)
    x2, h2 = tile_fwd("l0_ffnnorm", fn_addnorm2, m=m, tm=tm, nj=1, rows=[Row(x1), Row(f0)], pars=[Par(gain(0, 3)), Par(gain(1, 0))],
                      outs=[Out(D_MODEL, F32), Out(D_MODEL, BF16)])

    z1 = mm2d("l1_in", "nn", h2, w_in_o)
    (cgate,) = tile_fwd("l1_gate", fn_fox_gate, m=m, tm=seq, nj=1, rows=[Row(z1, LANES, 3072 // LANES)], pars=[Par(fbias)],
                        outs=[Out(LANES, F32)])
    place, ones_q, ones_k = term_placement()
    qterm, kterm = tile_fwd("l1_terms", fn_fox_terms, m=m, tm=tm, nj=1, rows=[Row(cgate)],
                            pars=[Par(place), Par(ones_q), Par(ones_k)], outs=[Out(TERM_W, BF16), Out(TERM_W, BF16)])
    oc, lse = fox_pair_fwd("l1_attn", z1, qterm, kterm, n_batch=n_batch, seq=seq)
    mix1 = mm2d("l1_out", "nn", oc, w_out_o)
    x3, h3 = tile_fwd("l1_postnorm", fn_addnorm2, m=m, tm=tm, nj=1, rows=[Row(x2), Row(mix1)], pars=[Par(gain(1, 1)), Par(gain(1, 2))],
                      outs=[Out(D_MODEL, F32), Out(D_MODEL, BF16)])
    hid1, act1, f1 = _ffn_forward("l1_ffn", 1, h3, w_up_g, cw5, cb5, w_down_g, m, seq)
    dy, loss_part = tile_fwd("loss", fn_final, m=m, tm=tm, nj=1, rows=[Row(x3), Row(f1), Row(tgt)], pars=[Par(gain(1, 3))],
                             outs=[Out(D_MODEL, F32)], n_acc=1)

    df1, d_g13 = tile_bwd("l1_dffnnorm", fn_rms_only, m=m, tm=tm, nj=1, rows=[Row(f1)], pars=[Par(gain(1, 3))], cts=[Row(dy)],
                          drows=[Out(D_MODEL, BF16)])
    dh3, d_wup, d_cw1, d_cb1, d_wdown = _ffn_backward("l1_ffn", 1, df1, h3, hid1, act1, w_up_g, cw5, cb5, w_down_g, None, None, m, seq)
    dx2, dmix1, d_g11, d_g12 = tile_bwd("l1_dpostnorm", fn_addnorm2, m=m, tm=tm, nj=1, rows=[Row(x2), Row(mix1)],
                                        pars=[Par(gain(1, 1)), Par(gain(1, 2))], cts=[Row(dy), Row(dh3)],
                                        drows=[Out(D_MODEL, F32), Out(D_MODEL, BF16)])
    doc = mm2d("l1_doc", "nt", dmix1, w_out_o, BF16)
    d_wout_o = mm2d("l1_dwout", "tn", oc, dmix1)
    dq, dk, dv, dc = fox_pair_bwd("l1_dattn", z1, qterm, kterm, oc, doc, lse, n_batch=n_batch, seq=seq)
    dzf, d_fbias = tile_bwd("l1_dgate", fn_fox_gate, m=m, tm=seq, nj=1, rows=[Row(z1, LANES, 3072 // LANES)], pars=[Par(fbias)],
                            cts=[Row(dc)], drows=[Out(LANES, BF16)])
    dz1 = jnp.concatenate([dq, dk, dv, dzf], axis=-1)
    dh2 = mm2d("l1_dh", "nt", dz1, w_in_o, BF16)
    d_win_o = mm2d("l1_dwin", "tn", h2, dz1)

    dx1, df0, d_g03, d_g10 = tile_bwd("l0_dffnnorm", fn_addnorm2, m=m, tm=tm, nj=1, rows=[Row(x1), Row(f0)],
                                      pars=[Par(gain(0, 3)), Par(gain(1, 0))], cts=[Row(dx2), Row(dh2)],
                                      drows=[Out(D_MODEL, F32), Out(D_MODEL, BF16)])
    dh1, d_wup, d_cw0, d_cb0, d_wdown = _ffn_backward("l0_ffn", 0, df0, h1, hid0, act0, w_up_g, cw5, cb5, w_down_g, d_wup, d_wdown, m, seq)
    dx0a, dmix0, d_g01, d_g02 = tile_bwd("l0_dpostnorm", fn_addnorm2, m=m, tm=tm, nj=1, rows=[Row(x0), Row(mix0)],
                                         pars=[Par(gain(0, 1)), Par(gain(0, 2))], cts=[Row(dx1), Row(dh1)],
                                         drows=[Out(D_MODEL, F32), Out(D_MODEL, BF16)])
    dmixcat0 = mm2d("l0_dmixcat", "nt", dmix0, w_out_e, BF16)
    d_wout_e = mm2d("l0_dwout", "tn", mixcat0, dmix0)
    dzq, dzf0, dzv, dzg, d_lb, d_hnorm = hgrn_bwd("l0_dhgrn", z0, sprev, hgrn_lb_logits, hgrn_norm, dmixcat0, n_batch=n_batch, seq=seq)
    dzx, dzy, d_rcw, d_rcb, d_wa, d_ba, d_wx, d_bx, d_lam = tile_bwd(
        "l0_drglru", fn_rglru, m=m, tm=seq, nj=B_WIDTH // LANES, rows=rg_rows(), pars=rg_pars(),
        cts=[Row(dmixcat0, LANES, A_WIDTH // LANES)], drows=[Out(B_WIDTH, BF16, LANES), Out(B_WIDTH, BF16, LANES)])
    dz0 = jnp.concatenate([dzq, dzf0, dzv, dzg, dzx, dzy], axis=-1)
    dh0 = mm("l0_dh", "nt",
             Blk(dz0, (tmm, 384), lambda i, j, k: (i, k)),
             Blk(w_in_e, (None, None, D_MODEL, 384), lambda i, j, k: (k, 0, 0, 0)),
             Blk((m, D_MODEL), (tmm, D_MODEL), lambda i, j, k: (i, 0)), BF16, (nm, 1, N_DEV))
    d_win_e = mm("l0_dwin", "tn",
                 Blk(h0, (tmm, D_MODEL), lambda i, j, k: (k, 0)),
                 Blk(dz0, (tmm, 384), lambda i, j, k: (k, j)),
                 Blk((2, 4) + w_in_e.shape[1:], (None, None, None, D_MODEL, 384), lambda i, j, k: (j % 2, j // 2, 0, 0, 0)), BF16,
                 (1, N_DEV, nm))
    dx0, d_g00 = tile_bwd("l0_dprenorm", fn_input_norm, m=m, tm=tm, nj=1, rows=[Row(x0)], pars=[Par(gain(0, 0))],
                          cts=[Row(dx0a), Row(dh0)], drows=[Out(D_MODEL, F32)])

    d_gains = jnp.stack([jnp.concatenate([d_g00, d_g01, d_g02, d_g03], axis=0), jnp.concatenate([d_g10, d_g11, d_g12, d_g13], axis=0)])
    d_ffn_cw = jnp.stack([d_cw0, d_cw1], axis=2).reshape(N_DEV, n_layer, FFN_CONV, FF_BLK)
    by_core = lambda t: jnp.swapaxes(t.reshape((4, 2) + t.shape[1:]), 0, 1).astype(BF16)
    half = {
        "even_w_in": d_win_e,
        "even_w_out": by_core(d_wout_e.reshape(N_DEV, 1, D_MODEL // N_DEV, D_MODEL)),
        "odd_w_in": by_core(_cols_to_blocks(d_win_o[None, :, :3088])),
        "odd_w_out": by_core(d_wout_o.reshape(N_DEV, 1, D_MODEL // N_DEV, D_MODEL)),
        "ffn_w_up": d_wup,
        "ffn_w_down": d_wdown,
    }
    core = lax.axis_index("c").astype(jnp.int32).reshape(1)
    from_sibling = pair_exchange("exchange_core", [half[n] for n in BIG])
    chip_sums = [pair_add("add_" + n, half[n], r, core) for n, r in zip(BIG, from_sibling)]
    recv = dict(zip(BIG, quad_exchange("exchange_chips", chip_sums)))
    res = {n: adam_tiled("adam_" + n, recv[n], w[n], mom[n], var[n]) for n in BIG}
    small_send = [_cols_to_blocks(d_gains), _cols_to_blocks(d_rcw[None]), d_ffn_cw]
    recv.update(zip(SMALL_SHARDED, all_to_all("exchange_small", small_send)))

    d_ffn_cb = jnp.stack([d_cb0, d_cb1]).reshape(n_layer, 2 * D_FF)
    rep = {"hgrn_lb_logits": d_lb, "hgrn_norm": d_hnorm, "rg_conv_b": d_rcb, "rg_wa": _block_diag_grad(d_wa)[None], "rg_ba": d_ba,
           "rg_wx": _block_diag_grad(d_wx)[None], "rg_bx": d_bx, "rg_lambda": d_lam, "fox_f_bias": d_fbias[:, :C_HEADS],
           "ffn_conv_b": d_ffn_cb}
    parts = all_gather("gather_partials", [rep[n] for n in REPLICATED] + [loss_part])
    for n, p in zip(REPLICATED, parts):
        recv[n] = p
    small = SMALL_SHARDED + REPLICATED
    small_res, (loss_sum,) = adam_small("adam_small", [(recv[n], w[n], mom[n], var[n]) for n in small], [parts[-1]])
    res.update(dict(zip(small, small_res)))

    out = [loss_sum[0, 0], dx0.reshape(x.shape)]
    for k in range(4):
        out += [res[n][k] for n in NAMES]
    return tuple(out)
```

```python
import functools

import jax
import jax.numpy as jnp
from jax import lax
from jax.experimental import pallas as pl
from jax.experimental.pallas import tpu as pltpu

F32 = jnp.float32
BF16 = jnp.bfloat16

D_MODEL = 1024
A_HEADS = 4
A_WIDTH = 512
HGRN_CHUNK = 64
HGRN_SEG = 512
B_WIDTH = 512
B_BLOCKS = 8
B_BLOCK_DIM = 64
B_CONV = 4
RG_C = 8.0
C_HEADS = 16
C_HEAD_DIM = 64
D_FF = 2816
FFN_CONV = 3
EPS = 1e-6
LANES = 128
HALO = 16
N_DEV = 8
FF_BLK = 2 * D_FF // N_DEV
MESH = pl.DeviceIdType.MESH
NEG = -1e30
VMEM_LIMIT = 56 * 1024 * 1024

ADAM_LR = 0.001
ADAM_B1 = 0.9
ADAM_B2 = 0.999
ADAM_EPS = 1e-08
ADAM_WD = 0.01
ADAM_STEP = 10


def _dg(a, b, pat):
    nb = a.ndim - 2
    batch = (tuple(range(nb)), tuple(range(nb)))
    ca = a.ndim - 1 if pat[0] == "n" else a.ndim - 2
    cb = b.ndim - 2 if pat[1] == "n" else b.ndim - 1
    return lax.dot_general(a.astype(BF16), b.astype(BF16), (((ca,), (cb,)), batch), preferred_element_type=F32)


@functools.partial(jax.custom_vjp, nondiff_argnums=(2,))
def bdot(a, b, pat):
    return _dg(a, b, pat)


def _bdot_fwd(a, b, pat):
    return _dg(a, b, pat), (a, b)


def _bdot_bwd(pat, res, g):
    a, b = res
    if pat == "nn":
        return _dg(g, b, "nt"), _dg(a, g, "tn")
    if pat == "nt":
        return _dg(g, b, "nn"), _dg(g, a, "tn")
    return _dg(b, g, "nt"), _dg(a, g, "nn")


bdot.defvjp(_bdot_fwd, _bdot_bwd)


def _shift_raw(x, s, up, fill):
    if s == 0:
        return x
    n = x.shape[0]
    r = pltpu.roll(x, (n - s) if up else s, 0)
    idx = lax.broadcasted_iota(jnp.int32, x.shape, 0)
    mask = (idx >= n - s) if up else (idx < s)
    return jnp.where(mask, jnp.asarray(fill, x.dtype), r)


@functools.partial(jax.custom_vjp, nondiff_argnums=(1,))
def shift_down(x, s):
    return _shift_raw(x, s, False, 0.0)


def _shift_down_fwd(x, s):
    return _shift_raw(x, s, False, 0.0), None


def _shift_down_bwd(s, _, g):
    return (_shift_raw(g, s, True, 0.0),)


shift_down.defvjp(_shift_down_fwd, _shift_down_bwd)


def _scan_impl(a, u, up):
    n = a.shape[0]
    s = 1
    while s < n:
        u = a * _shift_raw(u, s, up, 0.0) + u
        if 2 * s < n:
            a = a * _shift_raw(a, s, up, 1.0)
        s *= 2
    return u


@jax.custom_vjp
def lin_scan(a, u):
    return _scan_impl(a, u, False)


def _lin_scan_fwd(a, u):
    h = _scan_impl(a, u, False)
    return h, (a, h)


def _lin_scan_bwd(res, g):
    a, h = res
    gh = _scan_impl(_shift_raw(a, 1, True, 0.0), g, True)
    return gh * _shift_raw(h, 1, False, 0.0), gh


lin_scan.defvjp(_lin_scan_fwd, _lin_scan_bwd)


def _cumsum_impl(x, up, period):
    n = x.shape[0]
    span = n if period is None else period
    idx = lax.broadcasted_iota(jnp.int32, x.shape, 0)
    pos = idx if period is None else idx % period
    s = 1
    while s < span:
        sh = _shift_raw(x, s, up, 0.0)
        if period is not None:
            keep = (pos < period - s) if up else (pos >= s)
            sh = jnp.where(keep, sh, 0.0)
        x = x + sh
        s *= 2
    return x


@functools.partial(jax.custom_vjp, nondiff_argnums=(1,))
def cumsum_rows(x, period):
    return _cumsum_impl(x, False, period)


def _cumsum_fwd(x, period):
    return _cumsum_impl(x, False, period), None


def _cumsum_bwd(period, _, g):
    return (_cumsum_impl(g, True, period),)


cumsum_rows.defvjp(_cumsum_fwd, _cumsum_bwd)


def _sigmoid(x):
    return jax.nn.sigmoid(x)


def _expm1(x):
    return jnp.tanh(0.5 * x) * (jnp.exp(x) + 1.0)


def _softplus(x):
    return jnp.maximum(x, 0.0) + jnp.log(1.0 + jnp.exp(-jnp.abs(x)))


def _rms(x, g):
    return x * lax.rsqrt(jnp.mean(x * x, axis=-1, keepdims=True) + EPS) * g


def fn_prenorm(x, g):
    return (_rms(x, g).astype(BF16),)


def fn_prenorm_after(x, g, _token):
    return fn_prenorm(x, g)


def fn_addnorm2(x, y, g_post, g_pre):
    x1 = x + _rms(y, g_post)
    return x1, _rms(x1, g_pre).astype(BF16)


def fn_input_norm(x, g):
    return x, _rms(x, g).astype(BF16)


def fn_final(x, y, tgt, g_post):
    out = x + _rms(y, g_post)
    err = out - tgt
    dy = err * (1.0 / D_MODEL)
    loss = 0.5 * jnp.sum(jnp.mean(err * err, axis=-1, keepdims=True), axis=0, keepdims=True)
    return dy, jnp.broadcast_to(loss, (1, LANES))


def fn_rms_only(y, g):
    return (_rms(y, g),)


def _causal_conv(x, w, b, taps):
    c = b
    for k in range(taps):
        c = c + w[k:k + 1, :] * shift_down(x, taps - 1 - k)
    return c


def fn_rglru(xb, yb, cw, cb, wa, ba, wx, bx, lam):
    xf = _causal_conv(xb, cw, cb, B_CONV)
    r = _sigmoid(bdot(xf, wa, "nn") + ba)
    i = _sigmoid(bdot(xf, wx, "nn") + bx)
    log_a = -RG_C * r * _softplus(-lam)
    a = jnp.exp(log_a)
    u = jnp.sqrt(-_expm1(2.0 * log_a)) * (i * xf)
    h = lin_scan(a, u)
    return ((h * jax.nn.gelu(yb)).astype(BF16),)


def fn_fox_gate(zf, bias):
    return (cumsum_rows(jax.nn.log_sigmoid(zf + bias), None),)


def fn_hgrn_seg(q, fl, v, g, st, logits, hn):
    rows = q.shape[0]
    nc = rows // HGRN_CHUNK
    l0, l1, l2 = logits[0:1, :], logits[1:2, :], logits[2:3, :]
    mx = jnp.maximum(jnp.maximum(l0, l1), l2)
    e0, e1, e2 = jnp.exp(l0 - mx), jnp.exp(l1 - mx), jnp.exp(l2 - mx)
    lb = e0 / (e0 + e1 + e2)
    forget = lb + (1.0 - lb) * _sigmoid(fl)
    qs = q * _sigmoid(q)
    kk = 1.0 - forget
    logf = jnp.log(forget)
    bcum = cumsum_rows(logf, HGRN_CHUNK)
    c3 = lambda t: t.reshape(nc, HGRN_CHUNK, 128)
    b_last = jnp.sum(c3(logf), axis=1, keepdims=True)
    bcum3 = c3(bcum)
    q_dec = c3(qs) * jnp.exp(bcum3)
    k_dec = c3(kk) * jnp.exp(-bcum3)
    k_upd = c3(kk) * jnp.exp(b_last - bcum3)
    v3 = c3(v)
    scores = bdot(q_dec, k_dec, "nt")
    ri = lax.broadcasted_iota(jnp.int32, scores.shape, 1)
    ci = lax.broadcasted_iota(jnp.int32, scores.shape, 2)
    scores = jnp.where(ri >= ci, scores, 0.0)
    o = bdot(scores, v3, "nn")
    upd_t = bdot(v3, k_upd, "tn")
    dec = jnp.exp(b_last)
    prev = []
    for n in range(nc):
        prev.append(st)
        st = st * dec[n] + upd_t[n]
    o = o + bdot(q_dec, jnp.stack(prev), "nt")
    o = o.reshape(rows, 128)
    o = o * lax.rsqrt(jnp.mean(o * o, axis=-1, keepdims=True) + EPS) * hn
    return (o * _sigmoid(g)).astype(BF16), st


def _ffn_conv(xg, xv, cw, cb):
    cg = _causal_conv(xg, cw[0], cb[0], FFN_CONV)[HALO:]
    cv = _causal_conv(xv, cw[1], cb[1], FFN_CONV)[HALO:]
    return cg, cv


def _ffn_gate(cg, cv):
    return jax.nn.gelu(cg) * cv


class Row:
    def __init__(self, arr, cb=None, off=0):
        self.arr, self.cb, self.off = arr, cb, off

    def spec(self, tm):
        if self.cb is None:
            return pl.BlockSpec((tm, self.arr.shape[1]), lambda j, i: (i, 0))
        off = self.off
        return pl.BlockSpec((tm, self.cb), lambda j, i: (i, j + off))


class Par:
    def __init__(self, arr, kind="full", bs=None):
        self.arr, self.kind, self.bs = arr, kind, bs

    def block(self):
        if self.kind == "full":
            return self.arr.shape
        if self.kind == "col":
            return (self.arr.shape[0], self.bs)
        return (self.bs, self.arr.shape[1])

    def spec(self):
        if self.kind == "full":
            return pl.BlockSpec(self.block(), lambda j, i: (0, 0))
        if self.kind == "col":
            return pl.BlockSpec(self.block(), lambda j, i: (0, j))
        return pl.BlockSpec(self.block(), lambda j, i: (j, 0))


class Out:
    def __init__(self, width, dtype, cb=None, off=0):
        self.width, self.dtype, self.cb, self.off = width, dtype, cb, off

    def spec(self, tm):
        if self.cb is None:
            return pl.BlockSpec((tm, self.width), lambda j, i: (i, 0))
        off = self.off
        return pl.BlockSpec((tm, self.cb), lambda j, i: (i, j + off))


def _params(sem):
    return pltpu.CompilerParams(dimension_semantics=sem, vmem_limit_bytes=VMEM_LIMIT)


def tile_fwd(name, fn, *, m, tm, nj, rows, pars, outs, n_acc=0):
    n_r, n_p, n_o = len(rows), len(pars), len(outs)

    def body(*refs):
        ins = [r[...] for r in refs[:n_r + n_p]]
        res = fn(*ins)
        o_refs = refs[n_r + n_p:]
        for k in range(n_o):
            o_refs[k][...] = res[k].astype(o_refs[k].dtype)
        first = jnp.logical_and(pl.program_id(0) == 0, pl.program_id(1) == 0)
        for k in range(n_acc):
            ref = o_refs[n_o + k]

            @pl.when(first)
            def _():
                ref[...] = jnp.zeros_like(ref)

            ref[...] += res[n_o + k]

    out_shape = [jax.ShapeDtypeStruct((m, o.width), o.dtype) for o in outs]
    out_specs = [o.spec(tm) for o in outs]
    for _ in range(n_acc):
        out_shape.append(jax.ShapeDtypeStruct((1, LANES), F32))
        out_specs.append(pl.BlockSpec((1, LANES), lambda j, i: (0, 0)))
    sem = ("arbitrary", "arbitrary") if n_acc else ("parallel", "parallel")
    return pl.pallas_call(
        body, grid=(nj, m // tm), name=name,
        in_specs=[r.spec(tm) for r in rows] + [p.spec() for p in pars],
        out_specs=out_specs, out_shape=out_shape, compiler_params=_params(sem),
    )(*[r.arr for r in rows], *[p.arr for p in pars])


def tile_bwd(name, fn, *, m, tm, nj, rows, pars, cts, drows):
    n_r, n_p, n_c = len(rows), len(pars), len(cts)
    want = [k for k in range(n_r) if drows[k] is not None]

    def body(*refs):
        ins = [r[...] for r in refs[:n_r + n_p]]
        ct = [r[...] for r in refs[n_r + n_p:n_r + n_p + n_c]]
        o_refs = refs[n_r + n_p + n_c:]
        res, vjp = jax.vjp(fn, *ins)
        grads = vjp(tuple(c.astype(r.dtype) for c, r in zip(ct, res)))
        for pos, k in enumerate(want):
            o_refs[pos][...] = grads[k].astype(o_refs[pos].dtype)
        for k in range(n_p):
            ref = o_refs[len(want) + k]
            first = pl.program_id(1) == 0
            if pars[k].kind == "full":
                first = jnp.logical_and(first, pl.program_id(0) == 0)

            @pl.when(first)
            def _():
                ref[...] = jnp.zeros_like(ref)

            ref[...] += grads[n_r + k].astype(F32)

    out_shape = [jax.ShapeDtypeStruct((m, drows[k].width), drows[k].dtype) for k in want]
    out_specs = [drows[k].spec(tm) for k in want]
    for p in pars:
        out_shape.append(jax.ShapeDtypeStruct(p.arr.shape, F32))
        out_specs.append(p.spec())
    return pl.pallas_call(
        body, grid=(nj, m // tm), name=name,
        in_specs=[r.spec(tm) for r in rows] + [p.spec() for p in pars] + [c.spec(tm) for c in cts],
        out_specs=out_specs, out_shape=out_shape, compiler_params=_params(("arbitrary", "arbitrary")),
    )(*[r.arr for r in rows], *[p.arr for p in pars], *[c.arr for c in cts])


class Blk:
    def __init__(self, arr, block, index):
        self.arr, self.block, self.index = arr, block, index

    def spec(self):
        return pl.BlockSpec(self.block, self.index)


def _flat2(v):
    return v if v.ndim == 2 else v.reshape(-1, v.shape[-1])


def mm(name, pat, a, b, o, out_dtype, grid, into=None):
    nk = grid[2]
    o_shape = o.arr

    def body(*refs):
        a_ref, b_ref = refs[0], refs[1]
        o_ref = refs[3] if into is not None else refs[2]
        r = _dg(_flat2(a_ref[...]), _flat2(b_ref[...]), pat)
        if nk == 1:
            o_ref[...] = r.astype(out_dtype).reshape(o_ref.shape)
            return
        acc_ref = refs[-1]
        kk = pl.program_id(2)

        @pl.when(kk == 0)
        def _():
            acc_ref[...] = r

        @pl.when(kk > 0)
        def _():
            acc_ref[...] += r

        @pl.when(kk == nk - 1)
        def _():
            o_ref[...] = acc_ref[...].astype(out_dtype).reshape(o_ref.shape)

    ob = [d for d in o.block if d is not None]
    acc_shape = (ob[0], ob[1]) if len(ob) == 2 else (ob[0] * ob[1], ob[2])
    in_specs = [a.spec(), b.spec()]
    args = [a.arr, b.arr]
    aliases = {}
    if into is not None:
        in_specs.append(pl.BlockSpec(memory_space=pl.ANY))
        args.append(into)
        aliases = {2: 0}
    return pl.pallas_call(
        body, grid=grid, name=name, in_specs=in_specs, out_specs=o.spec(),
        out_shape=jax.ShapeDtypeStruct(o_shape, out_dtype),
        scratch_shapes=[pltpu.VMEM(acc_shape, F32)] if nk > 1 else [],
        input_output_aliases=aliases,
        compiler_params=_params(("parallel", "parallel", "arbitrary")),
    )(*args)


def _div_tile(n, cap):
    if n <= cap:
        return n
    best = 128
    for t in range(128, cap + 1, 128):
        if n % t == 0:
            best = t
    return best


def mm2d(name, pat, a, b, out_dtype=F32):
    if pat == "tn":
        k, m = a.shape
    else:
        m, k = a.shape
    n = b.shape[0] if pat == "nt" else b.shape[1]
    tm, tn, tk = _div_tile(m, 1024), _div_tile(n, 1024), _div_tile(k, 1024)
    a_blk = Blk(a, (tk, tm), lambda i, j, kk: (kk, i)) if pat == "tn" else Blk(a, (tm, tk), lambda i, j, kk: (i, kk))
    b_blk = Blk(b, (tn, tk), lambda i, j, kk: (j, kk)) if pat == "nt" else Blk(b, (tk, tn), lambda i, j, kk: (kk, j))
    o_blk = Blk((m, n), (tm, tn), lambda i, j, kk: (i, j))
    return mm(name, pat, a_blk, b_blk, o_blk, out_dtype, (m // tm, n // tn, k // tk))


def hgrn_fwd(name, z, logits, hnorm, *, n_batch, seq):
    m = n_batch * seq
    ts = min(HGRN_SEG, seq)
    n_seg = seq // ts

    def body(q_ref, f_ref, v_ref, g_ref, lg_ref, hn_ref, o_ref, sp_ref, st_ref):
        s = pl.program_id(2)

        @pl.when(s == 0)
        def _():
            st_ref[...] = jnp.zeros_like(st_ref)

        st = st_ref[...]
        sp_ref[...] = st
        o, st_new = fn_hgrn_seg(q_ref[...], f_ref[...], v_ref[...], g_ref[...], st, lg_ref[...], hn_ref[...])
        o_ref[...] = o
        st_ref[...] = st_new

    part = lambda p: pl.BlockSpec((ts, 128), lambda h, b, s: (b * n_seg + s, 4 * p + h))
    return pl.pallas_call(
        body, grid=(A_HEADS, n_batch, n_seg), name=name,
        in_specs=[part(0), part(1), part(2), part(3),
                  pl.BlockSpec((3, 128), lambda h, b, s: (0, h)),
                  pl.BlockSpec((1, 128), lambda h, b, s: (0, h))],
        out_specs=[pl.BlockSpec((ts, 128), lambda h, b, s: (b * n_seg + s, h)),
                   pl.BlockSpec((128, 128), lambda h, b, s: ((b * n_seg + s) * A_HEADS + h, 0))],
        out_shape=[jax.ShapeDtypeStruct((m, A_WIDTH), BF16),
                   jax.ShapeDtypeStruct((n_batch * n_seg * A_HEADS * 128, 128), F32)],
        scratch_shapes=[pltpu.VMEM((128, 128), F32)],
        compiler_params=_params(("arbitrary", "arbitrary", "arbitrary")),
    )(z, z, z, z, logits, hnorm)


def hgrn_bwd(name, z, sprev, logits, hnorm, do, *, n_batch, seq):
    m = n_batch * seq
    ts = min(HGRN_SEG, seq)
    n_seg = seq // ts

    def body(q_ref, f_ref, v_ref, g_ref, sp_ref, lg_ref, hn_ref, do_ref, dq_ref, df_ref, dv_ref, dg_ref, dlg_ref, dhn_ref, dst_ref):
        s = pl.program_id(2)

        @pl.when(s == 0)
        def _():
            dst_ref[...] = jnp.zeros_like(dst_ref)

        res, vjp = jax.vjp(fn_hgrn_seg, q_ref[...], f_ref[...], v_ref[...], g_ref[...], sp_ref[...], lg_ref[...], hn_ref[...])
        dq, df, dv, dg, dst, dlg, dhn = vjp((do_ref[...].astype(res[0].dtype), dst_ref[...]))
        dq_ref[...] = dq.astype(dq_ref.dtype)
        df_ref[...] = df.astype(df_ref.dtype)
        dv_ref[...] = dv.astype(dv_ref.dtype)
        dg_ref[...] = dg.astype(dg_ref.dtype)
        dst_ref[...] = dst
        first = jnp.logical_and(pl.program_id(1) == 0, s == 0)

        @pl.when(first)
        def _():
            dlg_ref[...] = jnp.zeros_like(dlg_ref)
            dhn_ref[...] = jnp.zeros_like(dhn_ref)

        dlg_ref[...] += dlg
        dhn_ref[...] += dhn

    rev = lambda b, s: b * n_seg + (n_seg - 1 - s)
    part = lambda p: pl.BlockSpec((ts, 128), lambda h, b, s: (rev(b, s), 4 * p + h))
    head = pl.BlockSpec((ts, 128), lambda h, b, s: (rev(b, s), h))
    dpart = jax.ShapeDtypeStruct((m, A_WIDTH), BF16)
    return pl.pallas_call(
        body, grid=(A_HEADS, n_batch, n_seg), name=name,
        in_specs=[part(0), part(1), part(2), part(3),
                  pl.BlockSpec((128, 128), lambda h, b, s: (rev(b, s) * A_HEADS + h, 0)),
                  pl.BlockSpec((3, 128), lambda h, b, s: (0, h)),
                  pl.BlockSpec((1, 128), lambda h, b, s: (0, h)),
                  head],
        out_specs=[head, head, head, head,
                   pl.BlockSpec((3, 128), lambda h, b, s: (0, h)),
                   pl.BlockSpec((1, 128), lambda h, b, s: (0, h))],
        out_shape=[dpart, dpart, dpart, dpart,
                   jax.ShapeDtypeStruct(logits.shape, F32),
                   jax.ShapeDtypeStruct(hnorm.shape, F32)],
        scratch_shapes=[pltpu.VMEM((128, 128), F32)],
        compiler_params=_params(("arbitrary", "arbitrary", "arbitrary")),
    )(z, z, z, z, sprev, logits, hnorm, do)


def _ffn_tiles(m, seq):
    tm = min(512, seq)
    return tm, seq // tm, m // tm


def ffn_mid_fwd(name, hid, cw, cb, layer, *, m, seq):
    tm, n_t, n_i = _ffn_tiles(m, seq)
    hb = tm // HALO

    def body(x_ref, xb_ref, cw_ref, cb_ref, o_ref):
        first = pl.program_id(1) % n_t == 0
        before = jnp.where(first, 0.0, xb_ref[...])
        ext = jnp.concatenate([before, x_ref[...]], axis=1)
        cg, cv = _ffn_conv(ext[0], ext[1], cw_ref[...], cb_ref[...])
        o_ref[...] = _ffn_gate(cg, cv).astype(o_ref.dtype)

    return pl.pallas_call(
        body, grid=(N_DEV // 2, n_i), name=name,
        in_specs=[pl.BlockSpec((2, None, tm, FF_BLK), lambda d, i: (0, d, i, 0)),
                  pl.BlockSpec((2, None, HALO, FF_BLK), lambda d, i: (0, d, jnp.maximum(i * hb - 1, 0), 0)),
                  pl.BlockSpec((2, None, None, FFN_CONV, FF_BLK), lambda d, i: (0, d, layer, 0, 0)),
                  pl.BlockSpec((None, 2, None, 1, FF_BLK), lambda d, i: (layer, 0, d, 0, 0))],
        out_specs=pl.BlockSpec((None, tm, FF_BLK), lambda d, i: (d, i, 0)),
        out_shape=jax.ShapeDtypeStruct((N_DEV // 2, m, FF_BLK), BF16),
        compiler_params=_params(("parallel", "parallel")),
    )(hid, hid, cw, cb)


def ffn_mid_bwd(name, hid, cw, cb, dact, layer, *, m, seq):
    tm, n_t, n_i = _ffn_tiles(m, seq)
    hb = tm // HALO
    last_blk = m // HALO - 1

    def body(x_ref, xb_ref, xa_ref, cw_ref, cb_ref, da_ref, daa_ref, dx_ref, dcw_ref, dcb_ref):
        i = pl.program_id(1)
        first = i % n_t == 0
        last = i % n_t == n_t - 1
        before = jnp.where(first, 0.0, xb_ref[...])
        ext = jnp.concatenate([before, x_ref[...], xa_ref[...]], axis=1)
        dact_ext = jnp.concatenate([da_ref[...].astype(F32), jnp.where(last, 0.0, daa_ref[...].astype(F32))], axis=0)
        (cg, cv), vjp_conv = jax.vjp(_ffn_conv, ext[0], ext[1], cw_ref[...], cb_ref[...])
        _, vjp_gate = jax.vjp(_ffn_gate, cg, cv)
        dcg, dcv = vjp_gate(dact_ext)
        dxg, dxv, _, _ = vjp_conv((dcg, dcv))
        dx_ref[0] = dxg[HALO:HALO + tm].astype(dx_ref.dtype)
        dx_ref[1] = dxv[HALO:HALO + tm].astype(dx_ref.dtype)
        own = lax.broadcasted_iota(jnp.int32, dcg.shape, 0) < tm
        _, _, dcw, dcb = vjp_conv((jnp.where(own, dcg, 0.0), jnp.where(own, dcv, 0.0)))

        @pl.when(i == 0)
        def _():
            dcw_ref[...] = jnp.zeros_like(dcw_ref)
            dcb_ref[...] = jnp.zeros_like(dcb_ref)

        dcw_ref[...] += dcw
        dcb_ref[...] += dcb

    return pl.pallas_call(
        body, grid=(N_DEV // 2, n_i), name=name,
        in_specs=[pl.BlockSpec((2, None, tm, FF_BLK), lambda d, i: (0, d, i, 0)),
                  pl.BlockSpec((2, None, HALO, FF_BLK), lambda d, i: (0, d, jnp.maximum(i * hb - 1, 0), 0)),
                  pl.BlockSpec((2, None, HALO, FF_BLK), lambda d, i: (0, d, jnp.minimum((i + 1) * hb, last_blk), 0)),
                  pl.BlockSpec((2, None, None, FFN_CONV, FF_BLK), lambda d, i: (0, d, layer, 0, 0)),
                  pl.BlockSpec((None, 2, None, 1, FF_BLK), lambda d, i: (layer, 0, d, 0, 0)),
                  pl.BlockSpec((None, tm, FF_BLK), lambda d, i: (d, i, 0)),
                  pl.BlockSpec((None, HALO, FF_BLK), lambda d, i: (d, jnp.minimum((i + 1) * hb, last_blk), 0))],
        out_specs=[pl.BlockSpec((2, None, tm, FF_BLK), lambda d, i: (0, d, i, 0)),
                   pl.BlockSpec((2, None, FFN_CONV, FF_BLK), lambda d, i: (0, d, 0, 0)),
                   pl.BlockSpec((2, None, 1, FF_BLK), lambda d, i: (0, d, 0, 0))],
        out_shape=[jax.ShapeDtypeStruct((2, N_DEV // 2, m, FF_BLK), BF16),
                   jax.ShapeDtypeStruct((2, N_DEV // 2, FFN_CONV, FF_BLK), F32),
                   jax.ShapeDtypeStruct((2, N_DEV // 2, 1, FF_BLK), F32)],
        compiler_params=_params(("arbitrary", "arbitrary")),
    )(hid, hid, hid, cw, cb, dact, dact)


ATT_BLK = 512
N_PAIR = C_HEADS // 2
TERM_W = C_HEADS * LANES


def term_placement():
    import numpy as np
    place = np.zeros((6, LANES, TERM_W), np.float32)
    ones_q = np.zeros((1, TERM_W), np.float32)
    ones_k = np.zeros((1, TERM_W), np.float32)
    for h in range(C_HEADS):
        for j in range(3):
            place[j, h, h * LANES + C_HEAD_DIM + j] = 1.0
            place[3 + j, h, h * LANES + C_HEAD_DIM + 3 + j] = 1.0
            ones_q[0, h * LANES + C_HEAD_DIM + 3 + j] = 1.0
            ones_k[0, h * LANES + C_HEAD_DIM + j] = 1.0
    return (jnp.asarray(place.reshape(6 * LANES, TERM_W), BF16), jnp.asarray(ones_q, F32), jnp.asarray(ones_k, F32))


def fn_fox_terms(c, place, ones_q, ones_k):
    parts = _split3(c)
    qt = ones_q
    kt = ones_k
    for j in range(3):
        qt = qt + _dg(parts[j], place[j * LANES:(j + 1) * LANES], "nn")
        kt = kt - _dg(parts[j], place[(3 + j) * LANES:(4 + j) * LANES], "nn")
    return qt.astype(BF16), kt.astype(BF16)


def _head_tile(z, terms, e):
    lane = lax.broadcasted_iota(jnp.int32, z.shape, 1)
    base = z if e == 0 else pltpu.roll(z, C_HEAD_DIM, 1)
    return jnp.where(lane < C_HEAD_DIM, base, terms.astype(z.dtype))


def _head_only(z, e):
    lane = lax.broadcasted_iota(jnp.int32, z.shape, 1)
    mine = (lane < C_HEAD_DIM) if e == 0 else (lane >= C_HEAD_DIM)
    return jnp.where(mine, z, jnp.zeros_like(z)).astype(BF16)


def _pair_tile(a0, a1):
    lane = lax.broadcasted_iota(jnp.int32, a0.shape, 1)
    return jnp.where(lane < C_HEAD_DIM, a0, pltpu.roll(a1, C_HEAD_DIM, 1))


def _lane_col(a, k):
    lane = lax.broadcasted_iota(jnp.int32, a.shape, 1)
    return jnp.sum(jnp.where(lane == k, a, 0.0), axis=1, keepdims=True)


def _causal(s):
    key = lax.broadcasted_iota(jnp.int32, s.shape, 0)
    qry = lax.broadcasted_iota(jnp.int32, s.shape, 1)
    return qry >= key


def fox_pair_fwd(name, z, qterm, kterm, *, n_batch, seq):
    m = n_batch * seq
    blk = min(ATT_BLK, seq)
    nq = seq // blk
    dh = C_HEAD_DIM

    def body(zq_ref, zk_ref, zv_ref, qt_ref, kt_ref, o_ref, lse_ref, ka_ref, vt_ref):
        qi = pl.program_id(2)

        @pl.when(qi == 0)
        def _():
            zk = zk_ref[...]
            for e in range(2):
                ka_ref[e] = _head_tile(zk, kt_ref[:, e * LANES:(e + 1) * LANES], e).astype(BF16)
            for cb in range(nq):
                vt_ref[cb] = zv_ref[cb * blk:(cb + 1) * blk, :].T.astype(BF16)

        zq = zq_ref[...] * dh ** -0.5
        qa = [_head_tile(zq, qt_ref[:, e * LANES:(e + 1) * LANES], e).astype(BF16) for e in range(2)]

        def block(j, carry, diagonal):
            rows = pl.ds(pl.multiple_of(j * blk, blk), blk)
            out = []
            for e in range(2):
                mx, l, acc = carry[e]
                s = _dg(ka_ref[e, rows, :], qa[e], "nt")
                if diagonal:
                    s = jnp.where(_causal(s), s, NEG)
                mx_new = jnp.maximum(mx, jnp.max(s, axis=0, keepdims=True))
                p = jnp.exp(s - mx_new)
                alpha = jnp.exp(mx - mx_new)
                l = alpha * l + jnp.sum(p, axis=0, keepdims=True)
                acc = alpha * acc + _dg(vt_ref[j, e * dh:(e + 1) * dh, :], p, "nn")
                out.append((mx_new, l, acc))
            return tuple(out)

        one = (jnp.full((1, blk), NEG, F32), jnp.zeros((1, blk), F32), jnp.zeros((dh, blk), F32))
        carry = lax.fori_loop(0, qi, lambda j, cr: block(j, cr, False), (one, one))
        res = block(qi, carry, True)
        ot = jnp.concatenate([res[e][2] / res[e][1] for e in range(2)], axis=0)
        o_ref[...] = ot.T.astype(o_ref.dtype)
        for e in range(2):
            lse_ref[e] = res[e][0] + jnp.log(res[e][1])

    col = lambda part: (lambda b, g, i: (b, part * N_PAIR + g))
    return pl.pallas_call(
        body, grid=(n_batch, N_PAIR, nq), name=name,
        in_specs=[pl.BlockSpec((blk, LANES), lambda b, g, i: (b * nq + i, g)),
                  pl.BlockSpec((seq, LANES), col(1)),
                  pl.BlockSpec((seq, LANES), col(2)),
                  pl.BlockSpec((blk, 2 * LANES), lambda b, g, i: (b * nq + i, g)),
                  pl.BlockSpec((seq, 2 * LANES), lambda b, g, i: (b, g))],
        out_specs=[pl.BlockSpec((blk, LANES), lambda b, g, i: (b * nq + i, g)),
                   pl.BlockSpec((None, None, None, 2, 1, blk), lambda b, g, i: (b, g, i, 0, 0, 0))],
        out_shape=[jax.ShapeDtypeStruct((m, D_MODEL), BF16), jax.ShapeDtypeStruct((n_batch, N_PAIR, nq, 2, 1, blk), F32)],
        scratch_shapes=[pltpu.VMEM((2, seq, LANES), BF16), pltpu.VMEM((nq, LANES, blk), BF16)],
        compiler_params=_params(("parallel", "parallel", "arbitrary")),
    )(z, z, z, qterm, kterm)


def fox_pair_bwd(name, z, qterm, kterm, o, do, lse, *, n_batch, seq):
    m = n_batch * seq
    blk = min(ATT_BLK, seq)
    nq = seq // blk
    dh = C_HEAD_DIM

    def body(zq_ref, zk_ref, zv_ref, qt_ref, kt_ref, o_ref, do_ref, lse_ref, dq_ref, dk_ref, dv_ref, dc_ref,
             qa_ref, doh_ref, del_ref, dqt_ref, dk_acc, dv_acc):
        g, j = pl.program_id(1), pl.program_id(2)
        lane = lax.broadcasted_iota(jnp.int32, (blk, LANES), 1)

        @pl.when(jnp.logical_and(g == 0, j == 0))
        def _():
            dc_ref[...] = jnp.zeros_like(dc_ref)

        @pl.when(j == 0)
        def _():
            zq = zq_ref[...] * dh ** -0.5
            dov = do_ref[...]
            for e in range(2):
                qa_ref[e] = _head_tile(zq, qt_ref[:, e * LANES:(e + 1) * LANES], e).astype(BF16)
                doh_ref[e] = _head_only(dov, e)
            for cb in range(nq):
                rows = slice(cb * blk, (cb + 1) * blk)
                prod_t = (do_ref[rows, :].astype(F32) * o_ref[rows, :].astype(F32)).T
                for e in range(2):
                    del_ref[cb, e] = jnp.sum(prod_t[e * dh:(e + 1) * dh], axis=0, keepdims=True)
            dqt_ref[...] = jnp.zeros_like(dqt_ref)

        zk, zv = zk_ref[...], zv_ref[...]
        ka32 = [_head_tile(zk, kt_ref[:, e * LANES:(e + 1) * LANES], e) for e in range(2)]
        ka = [t.astype(BF16) for t in ka32]
        kat = [t.T.astype(BF16) for t in ka32]
        vh = [_head_only(zv, e) for e in range(2)]
        dk_acc[...] = jnp.zeros_like(dk_acc)
        dv_acc[...] = jnp.zeros_like(dv_acc)

        def block(i, diagonal):
            rows = pl.ds(pl.multiple_of(i * blk, blk), blk)
            for e in range(2):
                qv, dov = qa_ref[e, rows, :], doh_ref[e, rows, :]
                p = jnp.exp(_dg(ka[e], qv, "nt") - lse_ref[i, e])
                if diagonal:
                    p = jnp.where(_causal(p), p, 0.0)
                dv_acc[...] += _dg(p, dov, "nn")
                ds = p * (_dg(vh[e], dov, "nt") - del_ref[i, e])
                dk_acc[e] += _dg(ds, qv, "nn")
                dqt_ref[i, e] += _dg(kat[e], ds, "nn")

        block(j, True)

        def rest(i, carry):
            block(i, False)
            return carry

        lax.fori_loop(j + 1, nq, rest, 0)
        dk0, dk1 = dk_acc[0], dk_acc[1]
        dk_ref[...] = _pair_tile(dk0, dk1).astype(dk_ref.dtype)
        dv_ref[...] = dv_acc[...].astype(dv_ref.dtype)
        rows_j = pl.ds(pl.multiple_of(j * blk, blk), blk)
        for e, dke in enumerate((dk0, dk1)):
            dc_ref[rows_j, :] -= jnp.where(lane == 2 * g + e, _lane_col(dke, dh + 3), 0.0)

        @pl.when(j == nq - 1)
        def _():
            for i in range(nq):
                nat = [dqt_ref[i, e].T for e in range(2)]
                rows = slice(i * blk, (i + 1) * blk)
                dq_ref[rows, :] = (_pair_tile(nat[0], nat[1]) * dh ** -0.5).astype(dq_ref.dtype)
                for e in range(2):
                    dc_ref[rows, :] += jnp.where(lane == 2 * g + e, _lane_col(nat[e], dh), 0.0)

    col = lambda part: (lambda b, g, j: (b, part * N_PAIR + g))
    colj = lambda part: (lambda b, g, j: (b * nq + j, part * N_PAIR + g))
    pair = jax.ShapeDtypeStruct((m, D_MODEL), BF16)
    return pl.pallas_call(
        body, grid=(n_batch, N_PAIR, nq), name=name,
        in_specs=[pl.BlockSpec((seq, LANES), col(0)),
                  pl.BlockSpec((blk, LANES), colj(1)),
                  pl.BlockSpec((blk, LANES), colj(2)),
                  pl.BlockSpec((seq, 2 * LANES), lambda b, g, j: (b, g)),
                  pl.BlockSpec((blk, 2 * LANES), lambda b, g, j: (b * nq + j, g)),
                  pl.BlockSpec((seq, LANES), col(0)),
                  pl.BlockSpec((seq, LANES), col(0)),
                  pl.BlockSpec((None, None, nq, 2, 1, blk), lambda b, g, j: (b, g, 0, 0, 0, 0))],
        out_specs=[pl.BlockSpec((seq, LANES), col(0)),
                   pl.BlockSpec((blk, LANES), colj(0)),
                   pl.BlockSpec((blk, LANES), colj(0)),
                   pl.BlockSpec((seq, LANES), lambda b, g, j: (b, 0))],
        out_shape=[pair, pair, pair, jax.ShapeDtypeStruct((m, LANES), F32)],
        scratch_shapes=[pltpu.VMEM((2, seq, LANES), BF16), pltpu.VMEM((2, seq, LANES), BF16),
                        pltpu.VMEM((nq, 2, 1, blk), F32), pltpu.VMEM((nq, 2, LANES, blk), F32),
                        pltpu.VMEM((2, blk, LANES), F32), pltpu.VMEM((blk, LANES), F32)],
        compiler_params=_params(("arbitrary", "arbitrary", "arbitrary")),
    )(z, z, z, qterm, kterm, o, do, lse)


def _split3(c):
    c1 = c.astype(BF16)
    r1 = c - c1.astype(F32)
    c2 = r1.astype(BF16)
    c3 = (r1 - c2.astype(F32)).astype(BF16)
    return c1, c2, c3


def fox_operands(q, k, c):
    bh, seq, dh = q.shape
    c1, c2, c3 = (t[..., None] for t in _split3(c))
    one = jnp.ones((bh, seq, 1), BF16)
    pad = jnp.zeros((bh, seq, LANES - dh - 6), BF16)
    qa = jnp.concatenate([(q * dh ** -0.5).astype(BF16), c1, c2, c3, one, one, one, pad], axis=-1)
    ka = jnp.concatenate([k.astype(BF16), one, one, one, -c1, -c2, -c3, pad], axis=-1)
    return qa, ka


def fox_fwd(name, qa, ka, vt):
    bh, seq, da = qa.shape
    blk = min(ATT_BLK, seq)
    nq = seq // blk
    dh = vt.shape[2]

    def body(q_ref, k_ref, v_ref, o_ref, lse_ref):
        qi = pl.program_id(1)
        qv = q_ref[0]

        def block(j, carry, diagonal):
            mx, l, acc = carry
            kj = k_ref[0, pl.ds(pl.multiple_of(j * blk, blk), blk), :]
            s = _dg(kj, qv, "nt")
            if diagonal:
                key = lax.broadcasted_iota(jnp.int32, (blk, blk), 0)
                qry = lax.broadcasted_iota(jnp.int32, (blk, blk), 1)
                s = jnp.where(qry >= key, s, NEG)
            mx_new = jnp.maximum(mx, jnp.max(s, axis=0, keepdims=True))
            p = jnp.exp(s - mx_new)
            alpha = jnp.exp(mx - mx_new)
            l = alpha * l + jnp.sum(p, axis=0, keepdims=True)
            acc = alpha * acc + _dg(v_ref[0, j], p, "nn")
            return mx_new, l, acc

        init = (jnp.full((1, blk), NEG, F32), jnp.zeros((1, blk), F32), jnp.zeros((dh, blk), F32))
        carry = lax.fori_loop(0, qi, lambda j, cr: block(j, cr, False), init)
        mx, l, acc = block(qi, carry, True)
        o_ref[0] = (acc / l).astype(o_ref.dtype)
        lse_ref[0, 0] = mx + jnp.log(l)

    return pl.pallas_call(
        body, grid=(bh, nq), name=name,
        in_specs=[pl.BlockSpec((1, blk, da), lambda b, i: (b, i, 0)),
                  pl.BlockSpec((1, seq, da), lambda b, i: (b, 0, 0)),
                  pl.BlockSpec((1, nq, dh, blk), lambda b, i: (b, 0, 0, 0))],
        out_specs=[pl.BlockSpec((1, dh, blk), lambda b, i: (b, 0, i)),
                   pl.BlockSpec((1, 1, 1, blk), lambda b, i: (b, i, 0, 0))],
        out_shape=[jax.ShapeDtypeStruct((bh, dh, seq), BF16), jax.ShapeDtypeStruct((bh, nq, 1, blk), F32)],
        compiler_params=_params(("parallel", "arbitrary")),
    )(qa, ka, vt)


def fox_bwd(name, qa, ka, kat, v, do, dot, ot, lse):
    bh, seq, da = qa.shape
    blk = min(ATT_BLK, seq)
    nq = seq // blk
    dh = v.shape[2]

    def body(q_ref, k_ref, kt_ref, v_ref, do_ref, dot_ref, ot_ref, lse_ref, dq_ref, dk_ref, dv_ref, del_ref):
        j = pl.program_id(1)

        @pl.when(j == 0)
        def _():
            dq_ref[...] = jnp.zeros_like(dq_ref)
            for i in range(nq):
                cols = slice(i * blk, (i + 1) * blk)
                del_ref[i] = jnp.sum(dot_ref[0, :, cols].astype(F32) * ot_ref[0, :, cols].astype(F32), axis=0, keepdims=True)

        kj, kjt, vj = k_ref[0], kt_ref[0, 0], v_ref[0]

        def block(i, carry, diagonal):
            dk, dv = carry
            rows = pl.ds(pl.multiple_of(i * blk, blk), blk)
            qv, dov = q_ref[0, rows, :], do_ref[0, rows, :]
            p = jnp.exp(_dg(kj, qv, "nt") - lse_ref[0, i])
            if diagonal:
                key = lax.broadcasted_iota(jnp.int32, (blk, blk), 0)
                qry = lax.broadcasted_iota(jnp.int32, (blk, blk), 1)
                p = jnp.where(qry >= key, p, 0.0)
            dv = dv + _dg(p, dov, "nn")
            ds = p * (_dg(vj, dov, "nt") - del_ref[i])
            dk = dk + _dg(ds, qv, "nn")
            dq_ref[0, i] += _dg(kjt, ds, "nn")
            return dk, dv

        init = (jnp.zeros((blk, da), F32), jnp.zeros((blk, dh), F32))
        carry = block(j, init, True)
        dk, dv = lax.fori_loop(j + 1, nq, lambda i, cr: block(i, cr, False), carry)
        dk_ref[0] = dk
        dv_ref[0] = dv

    return pl.pallas_call(
        body, grid=(bh, nq), name=name,
        in_specs=[pl.BlockSpec((1, seq, da), lambda b, j: (b, 0, 0)),
                  pl.BlockSpec((1, blk, da), lambda b, j: (b, j, 0)),
                  pl.BlockSpec((1, 1, da, blk), lambda b, j: (b, j, 0, 0)),
                  pl.BlockSpec((1, blk, dh), lambda b, j: (b, j, 0)),
                  pl.BlockSpec((1, seq, dh), lambda b, j: (b, 0, 0)),
                  pl.BlockSpec((1, dh, seq), lambda b, j: (b, 0, 0)),
                  pl.BlockSpec((1, dh, seq), lambda b, j: (b, 0, 0)),
                  pl.BlockSpec((1, nq, 1, blk), lambda b, j: (b, 0, 0, 0))],
        out_specs=[pl.BlockSpec((1, nq, da, blk), lambda b, j: (b, 0, 0, 0)),
                   pl.BlockSpec((1, blk, da), lambda b, j: (b, j, 0)),
                   pl.BlockSpec((1, blk, dh), lambda b, j: (b, j, 0))],
        out_shape=[jax.ShapeDtypeStruct((bh, nq, da, blk), F32), jax.ShapeDtypeStruct((bh, seq, da), F32),
                   jax.ShapeDtypeStruct((bh, seq, dh), F32)],
        scratch_shapes=[pltpu.VMEM((nq, 1, blk), F32)],
        compiler_params=_params(("parallel", "arbitrary")),
    )(qa, ka, kat, v, do, dot, ot, lse)


def _mesh_pos():
    return lax.axis_index("x"), lax.axis_index("y"), lax.axis_index("c")


def _flip(v, bit):
    return 1 - v if bit else v


def all_gather(name, blocks):
    n = len(blocks)

    def body(*refs):
        x_refs, out_refs = refs[:n], refs[n:2 * n]
        send_sems, recv_sems, local_sems = refs[2 * n:]
        x, y, c = _mesh_pos()
        me, sibling = (x, y, c), (x, y, 1 - c)
        chips = [(1 - x, y), (x, 1 - y), (1 - x, 1 - y)]

        def slot(a, px, py, pc):
            return out_refs[a].at[4 * px + 2 * py + pc]

        def copy(a, k, blk, to, src=None):
            return pltpu.make_async_remote_copy(
                src_ref=slot(a, *blk) if src is None else src, dst_ref=slot(a, *blk),
                send_sem=send_sems.at[a, k], recv_sem=recv_sems.at[a, k], device_id=to, device_id_type=MESH)

        mine = [pltpu.make_async_copy(x_refs[a], slot(a, *me), local_sems.at[a]) for a in range(n)]
        for cp in mine:
            cp.start()
        sends = []
        for a in range(n):
            sends.append(copy(a, 0, me, sibling, src=x_refs[a]))
            sends += [copy(a, 1 + j, me, (*chip, c), src=x_refs[a]) for j, chip in enumerate(chips)]
        for cp in sends:
            cp.start()
        for j, chip in enumerate(chips):
            for a in range(n):
                copy(a, 1 + j, (*chip, c), me).wait_recv()
                passed = copy(a, 4 + j, (*chip, c), sibling)
                passed.start()
                sends.append(passed)
        for a in range(n):
            copy(a, 0, sibling, me).wait_recv()
            for j, chip in enumerate(chips):
                copy(a, 4 + j, (*chip, 1 - c), me).wait_recv()
        for cp in sends:
            cp.wait_send()
        for cp in mine:
            cp.wait()

    hbm = pl.BlockSpec(memory_space=pl.ANY)
    return pl.pallas_call(
        body, name=name, out_shape=[jax.ShapeDtypeStruct((N_DEV,) + b.shape, b.dtype) for b in blocks],
        in_specs=[hbm] * n, out_specs=[hbm] * n,
        scratch_shapes=[pltpu.SemaphoreType.DMA((n, 7)), pltpu.SemaphoreType.DMA((n, 7)), pltpu.SemaphoreType.DMA((n,))],
    )(*blocks)


def _peers(x, y, c):
    return [(_flip(x, k & 4), _flip(y, k & 2), _flip(c, k & 1)) for k in range(1, N_DEV)]


def gather_start(name, blocks, lands):
    n = len(blocks)

    def body(*refs):
        x_refs, land_refs = refs[:n], refs[n:2 * n]
        send_sems, recv_sems = refs[2 * n], refs[2 * n + 1]
        token = refs[-1]
        x, y, c = _mesh_pos()
        me = 4 * x + 2 * y + c
        for k, peer in enumerate(_peers(x, y, c)):
            for a in range(n):
                pltpu.make_async_remote_copy(
                    src_ref=x_refs[a], dst_ref=land_refs[a].at[me], send_sem=send_sems.at[7 * a + k], recv_sem=recv_sems.at[7 * a + k],
                    device_id=peer, device_id_type=MESH).start()
        token[...] = jnp.zeros_like(token)

    hbm = pl.BlockSpec(memory_space=pltpu.HBM)
    sem = pl.BlockSpec(memory_space=pltpu.SEMAPHORE)
    out_shape = ([pltpu.SemaphoreType.DMA((7 * n,)), pltpu.SemaphoreType.DMA((7 * n,))]
                 + [pltpu.HBM(b.shape, b.dtype) for b in blocks] + [pltpu.HBM(l.shape, l.dtype) for l in lands]
                 + [jax.ShapeDtypeStruct((8, LANES), F32)])
    res = pl.pallas_call(
        body, name=name, out_shape=out_shape, in_specs=[hbm] * (2 * n),
        out_specs=[sem, sem] + [hbm] * (2 * n) + [pl.BlockSpec(memory_space=pltpu.VMEM)],
        input_output_aliases={a: 2 + a for a in range(2 * n)},
        compiler_params=pltpu.CompilerParams(has_side_effects=pltpu.SideEffectType.DATAFLOW_SIDE_EFFECTING),
    )(*[pltpu.with_memory_space_constraint(b, pltpu.HBM) for b in blocks],
      *[pltpu.with_memory_space_constraint(l, pltpu.HBM) for l in lands])
    return res[0], res[1], res[2:2 + n], res[2 + n:2 + 2 * n], res[-1]


def gather_wait(name, send_sems, recv_sems, blocks, lands, after):
    n = len(blocks)

    def body(*refs):
        x_refs, land_refs = refs[:n], refs[n:2 * n]
        s_sems, r_sems = refs[2 * n], refs[2 * n + 1]
        x, y, c = _mesh_pos()
        me = 4 * x + 2 * y + c
        for k, peer in enumerate(_peers(x, y, c)):
            for a in range(n):
                cp = pltpu.make_async_remote_copy(
                    src_ref=x_refs[a], dst_ref=land_refs[a].at[me], send_sem=s_sems.at[7 * a + k], recv_sem=r_sems.at[7 * a + k],
                    device_id=peer, device_id_type=MESH)
                cp.wait_send()
                cp.wait_recv()

    hbm = pl.BlockSpec(memory_space=pltpu.HBM)
    sem = pl.BlockSpec(memory_space=pltpu.SEMAPHORE)
    res = pl.pallas_call(
        body, name=name,
        out_shape=[pltpu.HBM(b.shape, b.dtype) for b in blocks] + [pltpu.HBM(l.shape, l.dtype) for l in lands],
        in_specs=[hbm] * (2 * n) + [sem, sem, pl.BlockSpec(memory_space=pl.ANY)], out_specs=[hbm] * (2 * n),
        input_output_aliases={a: a for a in range(2 * n)},
        compiler_params=pltpu.CompilerParams(has_side_effects=pltpu.SideEffectType.DATAFLOW_SIDE_EFFECTING),
    )(*blocks, *lands, send_sems, recv_sems, after)
    return res[n:]


def all_to_all(name, sends):
    n = len(sends)

    def body(*refs):
        s_refs, r_refs = refs[:n], refs[n:2 * n]
        send_sems, recv_sems, local_sems = refs[2 * n:]
        x, y, c = _mesh_pos()
        me = 4 * x + 2 * y + c
        mine = [pltpu.make_async_copy(s_refs[a].at[me], r_refs[a].at[me], local_sems.at[a]) for a in range(n)]
        for cp in mine:
            cp.start()
        copies = []
        for k in range(1, N_DEV):
            px, py, pc = _flip(x, k & 4), _flip(y, k & 2), _flip(c, k & 1)
            for a in range(n):
                copies.append(pltpu.make_async_remote_copy(
                    src_ref=s_refs[a].at[4 * px + 2 * py + pc], dst_ref=r_refs[a].at[me],
                    send_sem=send_sems.at[a, k - 1], recv_sem=recv_sems.at[a, k - 1],
                    device_id=(px, py, pc), device_id_type=MESH))
        for cp in copies:
            cp.start()
        for cp in copies:
            cp.wait_recv()
        for cp in copies:
            cp.wait_send()
        for cp in mine:
            cp.wait()

    hbm = pl.BlockSpec(memory_space=pl.ANY)
    return pl.pallas_call(
        body, name=name, out_shape=[jax.ShapeDtypeStruct(s.shape, s.dtype) for s in sends],
        in_specs=[hbm] * n, out_specs=[hbm] * n,
        scratch_shapes=[pltpu.SemaphoreType.DMA((n, 7)), pltpu.SemaphoreType.DMA((n, 7)), pltpu.SemaphoreType.DMA((n,))],
    )(*sends)


def pair_exchange(name, hs):
    n = len(hs)

    def body(*refs):
        h_refs, r_refs = refs[:n], refs[n:2 * n]
        send_sems, recv_sems = refs[2 * n:]
        x, y, c = _mesh_pos()
        copies = [pltpu.make_async_remote_copy(
            src_ref=h_refs[a].at[1 - c], dst_ref=r_refs[a], send_sem=send_sems.at[a], recv_sem=recv_sems.at[a],
            device_id=(x, y, 1 - c), device_id_type=MESH) for a in range(n)]
        for cp in copies:
            cp.start()
        for cp in copies:
            cp.wait_recv()
        for cp in copies:
            cp.wait_send()

    hbm = pl.BlockSpec(memory_space=pl.ANY)
    return pl.pallas_call(
        body, name=name, out_shape=[jax.ShapeDtypeStruct(h.shape[1:], h.dtype) for h in hs],
        in_specs=[hbm] * n, out_specs=[hbm] * n,
        scratch_shapes=[pltpu.SemaphoreType.DMA((n,)), pltpu.SemaphoreType.DMA((n,))],
    )(*hs)


def quad_exchange(name, ss):
    n = len(ss)

    def body(*refs):
        s_refs, r_refs = refs[:n], refs[n:2 * n]
        send_sems, recv_sems, local_sems = refs[2 * n:]
        x, y, c = _mesh_pos()
        me = 2 * x + y
        mine = [pltpu.make_async_copy(s_refs[a].at[me], r_refs[a].at[me], local_sems.at[a]) for a in range(n)]
        for cp in mine:
            cp.start()
        copies = []
        for k in range(1, 4):
            px, py = _flip(x, k & 2), _flip(y, k & 1)
            for a in range(n):
                copies.append(pltpu.make_async_remote_copy(
                    src_ref=s_refs[a].at[2 * px + py], dst_ref=r_refs[a].at[me],
                    send_sem=send_sems.at[a, k - 1], recv_sem=recv_sems.at[a, k - 1],
                    device_id=(px, py, c), device_id_type=MESH))
        for cp in copies:
            cp.start()
        for cp in copies:
            cp.wait_recv()
        for cp in copies:
            cp.wait_send()
        for cp in mine:
            cp.wait()

    hbm = pl.BlockSpec(memory_space=pl.ANY)
    return pl.pallas_call(
        body, name=name, out_shape=[jax.ShapeDtypeStruct(s.shape, s.dtype) for s in ss],
        in_specs=[hbm] * n, out_specs=[hbm] * n,
        scratch_shapes=[pltpu.SemaphoreType.DMA((n, 3)), pltpu.SemaphoreType.DMA((n, 3)), pltpu.SemaphoreType.DMA((n,))],
    )(*ss)


def _rows_cols(shape):
    r = 1
    for d in shape[:-1]:
        r *= d
    return r, shape[-1]


def _row_tile(r, cap, step):
    return next((t for t in range(cap, step - 1, -step) if r % t == 0), r)


def pair_add(name, h, recv, core):
    shape = recv.shape
    r, c = _rows_cols(shape[1:])
    tr = _row_tile(r, 256, 16)

    def body(core_ref, h_ref, r_ref, o_ref):
        o_ref[...] = (h_ref[...].astype(F32) + r_ref[...].astype(F32)).astype(o_ref.dtype)

    spec = pl.BlockSpec((None, tr, c), lambda q, i, core_ref: (q, i, 0))
    res = pl.pallas_call(
        body, name=name, out_shape=jax.ShapeDtypeStruct((4, r, c), h.dtype),
        grid_spec=pltpu.PrefetchScalarGridSpec(
            num_scalar_prefetch=1, grid=(4, r // tr),
            in_specs=[pl.BlockSpec((None, None, tr, c), lambda q, i, core_ref: (core_ref[0], q, i, 0)), spec],
            out_specs=spec),
        compiler_params=_params(("parallel", "parallel")),
    )(core, h.reshape(2, 4, r, c), recv.reshape(4, r, c))
    return res.reshape(shape)


def _sum_parts(p, n):
    t = [p[k].astype(F32) for k in range(n)]
    while len(t) > 1:
        t = [t[k] + t[k + 1] for k in range(0, len(t), 2)]
    return t[0]


def _adam(g, w, m, v):
    m = ADAM_B1 * m + (1.0 - ADAM_B1) * g
    v = ADAM_B2 * v + (1.0 - ADAM_B2) * (g * g)
    m_hat = m / (1.0 - ADAM_B1 ** ADAM_STEP)
    v_hat = v / (1.0 - ADAM_B2 ** ADAM_STEP)
    return -ADAM_LR * (m_hat / (jnp.sqrt(v_hat) + ADAM_EPS) + ADAM_WD * w), m, v


def adam_tiled(name, partials, w, m_, v_):
    shape = w.shape
    n_part = partials.shape[0]
    r, c = _rows_cols(shape)
    tr = _row_tile(r, 256, 16)

    def body(p_ref, w_ref, m_ref, v_ref, g_ref, d_ref, nm_ref, nv_ref):
        g = _sum_parts(p_ref, n_part)
        g_ref[...] = g
        d_ref[...], nm_ref[...], nv_ref[...] = _adam(g, w_ref[...], m_ref[...], v_ref[...])

    spec = pl.BlockSpec((tr, c), lambda i: (i, 0))
    res = pl.pallas_call(
        body, grid=(r // tr,), name=name,
        in_specs=[pl.BlockSpec((n_part, tr, c), lambda i: (0, i, 0)), spec, spec, spec],
        out_specs=[spec] * 4, out_shape=[jax.ShapeDtypeStruct((r, c), F32)] * 4,
        compiler_params=_params(("parallel",)),
    )(partials.reshape(n_part, r, c), w.reshape(r, c), m_.reshape(r, c), v_.reshape(r, c))
    return [t.reshape(shape) for t in res]


def adam_small(name, items, extra):
    n, ne = len(items), len(extra)

    def body(*refs):
        ins, outs = refs[:4 * n + ne], refs[4 * n + ne:]
        for a in range(n):
            p_ref, w_ref, m_ref, v_ref = ins[4 * a:4 * a + 4]
            g = _sum_parts(p_ref, N_DEV)
            outs[4 * a][...] = g
            outs[4 * a + 1][...], outs[4 * a + 2][...], outs[4 * a + 3][...] = _adam(g, w_ref[...], m_ref[...], v_ref[...])
        for e in range(ne):
            outs[4 * n + e][...] = _sum_parts(ins[4 * n + e], N_DEV)

    args, out_shape = [], []
    for p, w, m_, v_ in items:
        args += [p, w, m_, v_]
        out_shape += [jax.ShapeDtypeStruct(w.shape, F32)] * 4
    for e in extra:
        args.append(e)
        out_shape.append(jax.ShapeDtypeStruct(e.shape[1:], F32))
    vmem = pl.BlockSpec(memory_space=pltpu.VMEM)
    res = pl.pallas_call(body, name=name, in_specs=[vmem] * len(args), out_specs=[vmem] * len(out_shape), out_shape=out_shape)(*args)
    return [res[4 * a:4 * a + 4] for a in range(n)], res[4 * n:]


def _cols_from_gather(g):
    g = jnp.moveaxis(g, 0, -2)
    return g.reshape(g.shape[:-2] + (g.shape[-2] * g.shape[-1],))


def _cols_to_blocks(w):
    w = w.reshape(w.shape[:-1] + (N_DEV, w.shape[-1] // N_DEV))
    return jnp.moveaxis(w, -2, 0)


def _block_diag(w):
    z = jnp.zeros((B_BLOCK_DIM, B_BLOCK_DIM), w.dtype)
    rows = []
    for j in range(B_BLOCKS // 2):
        top = jnp.concatenate([w[2 * j], z], axis=1)
        bot = jnp.concatenate([z, w[2 * j + 1]], axis=1)
        rows.append(jnp.concatenate([top, bot], axis=0))
    return jnp.concatenate(rows, axis=0)


def _block_diag_grad(d):
    out = []
    for j in range(B_BLOCKS // 2):
        blk = d[128 * j:128 * (j + 1)]
        out.append(blk[:64, :64])
        out.append(blk[64:, 64:])
    return jnp.stack(out)


NAMES = ("norm_gains", "even_w_in", "hgrn_lb_logits", "hgrn_norm", "rg_conv_w", "rg_conv_b", "rg_wa", "rg_ba", "rg_wx", "rg_bx",
         "rg_lambda", "even_w_out", "odd_w_in", "fox_f_bias", "odd_w_out", "ffn_w_up", "ffn_conv_w", "ffn_conv_b", "ffn_w_down")
BIG = ("even_w_in", "even_w_out", "odd_w_in", "odd_w_out", "ffn_w_up", "ffn_w_down")
SMALL_SHARDED = ("norm_gains", "rg_conv_w", "ffn_conv_w")
REPLICATED = ("hgrn_lb_logits", "hgrn_norm", "rg_conv_b", "rg_wa", "rg_ba", "rg_wx", "rg_bx", "rg_lambda", "fox_f_bias", "ffn_conv_b")


def _ffn_forward(tag, layer, h, w_up_g, cw5, cb5, w_down_g, m, seq):
    tm = _div_tile(m, 1024)
    nm = m // tm
    hid = mm(f"{tag}_up", "nn",
             Blk(h, (tm, D_MODEL), lambda i, j, k: (i, 0)),
             Blk(w_up_g, (None, None, D_MODEL, FF_BLK), lambda i, j, k: (j, 0, 0, 0)),
             Blk((N_DEV, m, FF_BLK), (None, tm, FF_BLK), lambda i, j, k: (j, i, 0)), F32, (nm, N_DEV, 1))
    hid = hid.reshape(2, N_DEV // 2, m, FF_BLK)
    act = ffn_mid_fwd(f"{tag}_mid", hid, cw5, cb5, layer, m=m, seq=seq)
    f = mm(f"{tag}_down", "nn",
           Blk(act, (None, tm, FF_BLK), lambda i, j, k: (k, i, 0)),
           Blk(w_down_g, (2, None, FF_BLK // 2, D_MODEL), lambda i, j, k: (k, 0, 0, 0)),
           Blk((m, D_MODEL), (tm, D_MODEL), lambda i, j, k: (i, 0)), F32, (nm, 1, N_DEV // 2))
    return hid, act, f


def _ffn_backward(tag, layer, df, h, hid, act, w_up_g, cw5, cb5, w_down_g, d_wup, d_wdown, m, seq):
    tm = _div_tile(m, 1024)
    nm = m // tm
    dact = mm(f"{tag}_dact", "nt",
              Blk(df, (tm, D_MODEL), lambda i, j, k: (i, 0)),
              Blk(w_down_g, (2, None, FF_BLK // 2, D_MODEL), lambda i, j, k: (j, 0, 0, 0)),
              Blk((N_DEV // 2, m, FF_BLK), (None, tm, FF_BLK), lambda i, j, k: (j, i, 0)), BF16, (nm, N_DEV // 2, 1))
    d_wdown = mm(f"{tag}_dwdown", "tn",
                 Blk(act, (None, tm, FF_BLK), lambda i, j, k: (i, k, 0)),
                 Blk(df, (tm, D_MODEL), lambda i, j, k: (k, 0)),
                 Blk((2, 4, cb5.shape[0]) + w_down_g.shape[2:], (2, None, None, FF_BLK // 2, D_MODEL), lambda i, j, k: (0, i, layer, 0, 0)), BF16,
                 (N_DEV // 2, 1, nm), into=d_wdown)
    dhid, d_cw, d_cb = ffn_mid_bwd(f"{tag}_dmid", hid, cw5, cb5, dact, layer, m=m, seq=seq)
    dhid = dhid.reshape(N_DEV, m, FF_BLK)
    dh = mm(f"{tag}_dh", "nt",
            Blk(dhid, (None, tm, FF_BLK), lambda i, j, k: (k, i, 0)),
            Blk(w_up_g, (None, None, D_MODEL, FF_BLK), lambda i, j, k: (k, 0, 0, 0)),
            Blk((m, D_MODEL), (tm, D_MODEL), lambda i, j, k: (i, 0)), BF16, (nm, 1, N_DEV))
    d_wup = mm(f"{tag}_dwup", "tn",
               Blk(h, (tm, D_MODEL), lambda i, j, k: (k, 0)),
               Blk(dhid, (None, tm, FF_BLK), lambda i, j, k: (j, k, 0)),
               Blk((2, 4, cb5.shape[0]) + w_up_g.shape[2:], (None, None, None, D_MODEL, FF_BLK), lambda i, j, k: (j % 2, j // 2, layer, 0, 0)), BF16,
               (1, N_DEV, nm), into=d_wup)
    return dh, d_wup, d_cw, d_cb, d_wdown


def kernel(x, norm_gains, even_w_in, hgrn_lb_logits, hgrn_norm, rg_conv_w, rg_conv_b, rg_wa, rg_ba, rg_wx, rg_bx, rg_lambda, even_w_out, odd_w_in, fox_f_bias, odd_w_out, ffn_w_up, ffn_conv_w, ffn_conv_b, ffn_w_down, loss_target, m_norm_gains, m_even_w_in, m_hgrn_lb_logits, m_hgrn_norm, m_rg_conv_w, m_rg_conv_b, m_rg_wa, m_rg_ba, m_rg_wx, m_rg_bx, m_rg_lambda, m_even_w_out, m_odd_w_in, m_fox_f_bias, m_odd_w_out, m_ffn_w_up, m_ffn_conv_w, m_ffn_conv_b, m_ffn_w_down, v_norm_gains, v_even_w_in, v_hgrn_lb_logits, v_hgrn_norm, v_rg_conv_w, v_rg_conv_b, v_rg_wa, v_rg_ba, v_rg_wx, v_rg_bx, v_rg_lambda, v_even_w_out, v_odd_w_in, v_fox_f_bias, v_odd_w_out, v_ffn_w_up, v_ffn_conv_w, v_ffn_conv_b, v_ffn_w_down):
    local = dict(locals())
    w = {n: local[n] for n in NAMES}
    mom = {n: local["m_" + n] for n in NAMES}
    var = {n: local["v_" + n] for n in NAMES}
    n_batch, seq, _ = x.shape
    m = n_batch * seq
    tm = _div_tile(m, 512)
    tmm = _div_tile(m, 1024)
    nm = m // tmm

    now = [w["even_w_in"], w["even_w_out"], w["ffn_w_up"][0:1], w["ffn_w_down"][0:1]]
    gathered = all_gather("gather_weights", [t.astype(BF16) for t in now] + [w[n] for n in SMALL_SHARDED])
    g = dict(zip(("even_w_in", "even_w_out", "ffn_w_up", "ffn_w_down") + SMALL_SHARDED, gathered))
    w_in_e = g["even_w_in"]
    w_out_e = g["even_w_out"].reshape(D_MODEL, D_MODEL)
    w_up_g0, w_down_g0 = g["ffn_w_up"], g["ffn_w_down"]
    gains = _cols_from_gather(g["norm_gains"])
    after_first = (g["norm_gains"][0, 0, 0, 0] * 0.0).astype(BF16)
    later = [w["odd_w_in"], w["odd_w_out"], w["ffn_w_up"][1:2], w["ffn_w_down"][1:2]]
    later = [t.astype(BF16) + after_first for t in later]
    me = 4 * lax.axis_index("x") + 2 * lax.axis_index("y") + lax.axis_index("c")
    lands = [lax.dynamic_update_index_in_dim(lax.empty((N_DEV,) + t.shape, BF16), t, me, 0) for t in later]
    send_sems, recv_sems, later, lands, started = gather_start("gather_later_start", later, lands)
    rg_cw = _cols_from_gather(g["rg_conv_w"])[0]
    n_layer = ffn_conv_w.shape[0]
    cw5 = g["ffn_conv_w"].reshape(2, N_DEV // 2, n_layer, FFN_CONV, FF_BLK)
    cb5 = ffn_conv_b.reshape(n_layer, 2, N_DEV // 2, 1, FF_BLK)
    gain = lambda l, k: gains[l, k:k + 1, :]
    wa_bd, wx_bd = _block_diag(rg_wa[0]), _block_diag(rg_wx[0])
    fbias = jnp.pad(fox_f_bias, ((0, 0), (0, LANES - C_HEADS)))

    x0 = x.reshape(m, D_MODEL)
    tgt = loss_target.reshape(m, D_MODEL)

    (h0,) = tile_fwd("l0_prenorm", fn_prenorm_after, m=m, tm=tm, nj=1, rows=[Row(x0)], pars=[Par(gain(0, 0)), Par(started)],
                     outs=[Out(D_MODEL, BF16)])
    z0 = mm("l0_in", "nn",
            Blk(h0, (tmm, D_MODEL), lambda i, j, k: (i, 0)),
            Blk(w_in_e, (None, None, D_MODEL, 384), lambda i, j, k: (j, 0, 0, 0)),
            Blk((m, 3072), (tmm, 384), lambda i, j, k: (i, j)), F32, (nm, N_DEV, 1))
    oa, sprev = hgrn_fwd("l0_hgrn", z0, hgrn_lb_logits, hgrn_norm, n_batch=n_batch, seq=seq)
    rg_rows = lambda: [Row(z0, LANES, 16), Row(z0, LANES, 20)]
    rg_pars = lambda: [Par(rg_cw, "col", LANES), Par(rg_conv_b, "col", LANES), Par(wa_bd, "row", LANES), Par(rg_ba, "col", LANES),
                       Par(wx_bd, "row", LANES), Par(rg_bx, "col", LANES), Par(rg_lambda, "col", LANES)]
    (ob,) = tile_fwd("l0_rglru", fn_rglru, m=m, tm=seq, nj=B_WIDTH // LANES, rows=rg_rows(), pars=rg_pars(),
                     outs=[Out(B_WIDTH, BF16, LANES)])
    mixcat0 = jnp.concatenate([oa, ob], axis=-1)
    mix0 = mm2d("l0_out", "nn", mixcat0, w_out_e)
    x1, h1 = tile_fwd("l0_postnorm", fn_addnorm2, m=m, tm=tm, nj=1, rows=[Row(x0), Row(mix0)], pars=[Par(gain(0, 1)), Par(gain(0, 2))],
                      outs=[Out(D_MODEL, F32), Out(D_MODEL, BF16)])
    hid0, act0, f0 = _ffn_forward("l0_ffn", 0, h1, w_up_g0, cw5, cb5, w_down_g0, m, seq)
    x2, h2 = tile_fwd("l0_ffnnorm", fn_addnorm2, m=m, tm=tm, nj=1, rows=[Row(x1), Row(f0)], pars=[Par(gain(0, 3)), Par(gain(1, 0))],
                      outs=[Out(D_MODEL, F32), Out(D_MODEL, BF16)])

    g_in_o, g_out_o, w_up_g1, w_down_g1 = gather_wait("gather_later_wait", send_sems, recv_sems, later, lands, h2)
    w_in_o = jnp.pad(_cols_from_gather(g_in_o)[0], ((0, 0), (0, 3200 - 3088)))
    w_out_o = g_out_o.reshape(D_MODEL, D_MODEL)
    z1 = mm2d("l1_in", "nn", h2, w_in_o)
    (cgate,) = tile_fwd("l1_gate", fn_fox_gate, m=m, tm=seq, nj=1, rows=[Row(z1, LANES, 3072 // LANES)], pars=[Par(fbias)],
                        outs=[Out(LANES, F32)])
    place, ones_q, ones_k = term_placement()
    qterm, kterm = tile_fwd("l1_terms", fn_fox_terms, m=m, tm=tm, nj=1, rows=[Row(cgate)],
                            pars=[Par(place), Par(ones_q), Par(ones_k)], outs=[Out(TERM_W, BF16), Out(TERM_W, BF16)])
    oc, lse = fox_pair_fwd("l1_attn", z1, qterm, kterm, n_batch=n_batch, seq=seq)
    mix1 = mm2d("l1_out", "nn", oc, w_out_o)
    x3, h3 = tile_fwd("l1_postnorm", fn_addnorm2, m=m, tm=tm, nj=1, rows=[Row(x2), Row(mix1)], pars=[Par(gain(1, 1)), Par(gain(1, 2))],
                      outs=[Out(D_MODEL, F32), Out(D_MODEL, BF16)])
    hid1, act1, f1 = _ffn_forward("l1_ffn", 1, h3, w_up_g1, cw5, cb5, w_down_g1, m, seq)
    dy, loss_part = tile_fwd("loss", fn_final, m=m, tm=tm, nj=1, rows=[Row(x3), Row(f1), Row(tgt)], pars=[Par(gain(1, 3))],
                             outs=[Out(D_MODEL, F32)], n_acc=1)

    df1, d_g13 = tile_bwd("l1_dffnnorm", fn_rms_only, m=m, tm=tm, nj=1, rows=[Row(f1)], pars=[Par(gain(1, 3))], cts=[Row(dy)],
                          drows=[Out(D_MODEL, BF16)])
    dh3, d_wup, d_cw1, d_cb1, d_wdown = _ffn_backward("l1_ffn", 1, df1, h3, hid1, act1, w_up_g1, cw5, cb5, w_down_g1, None, None, m, seq)
    dx2, dmix1, d_g11, d_g12 = tile_bwd("l1_dpostnorm", fn_addnorm2, m=m, tm=tm, nj=1, rows=[Row(x2), Row(mix1)],
                                        pars=[Par(gain(1, 1)), Par(gain(1, 2))], cts=[Row(dy), Row(dh3)],
                                        drows=[Out(D_MODEL, F32), Out(D_MODEL, BF16)])
    doc = mm2d("l1_doc", "nt", dmix1, w_out_o, BF16)
    d_wout_o = mm2d("l1_dwout", "tn", oc, dmix1)
    dq, dk, dv, dc = fox_pair_bwd("l1_dattn", z1, qterm, kterm, oc, doc, lse, n_batch=n_batch, seq=seq)
    dzf, d_fbias = tile_bwd("l1_dgate", fn_fox_gate, m=m, tm=seq, nj=1, rows=[Row(z1, LANES, 3072 // LANES)], pars=[Par(fbias)],
                            cts=[Row(dc)], drows=[Out(LANES, BF16)])
    dz1 = jnp.concatenate([dq, dk, dv, dzf], axis=-1)
    dh2 = mm2d("l1_dh", "nt", dz1, w_in_o, BF16)
    d_win_o = mm2d("l1_dwin", "tn", h2, dz1)

    dx1, df0, d_g03, d_g10 = tile_bwd("l0_dffnnorm", fn_addnorm2, m=m, tm=tm, nj=1, rows=[Row(x1), Row(f0)],
                                      pars=[Par(gain(0, 3)), Par(gain(1, 0))], cts=[Row(dx2), Row(dh2)],
                                      drows=[Out(D_MODEL, F32), Out(D_MODEL, BF16)])
    dh1, d_wup, d_cw0, d_cb0, d_wdown = _ffn_backward("l0_ffn", 0, df0, h1, hid0, act0, w_up_g0, cw5, cb5, w_down_g0, d_wup, d_wdown, m, seq)
    dx0a, dmix0, d_g01, d_g02 = tile_bwd("l0_dpostnorm", fn_addnorm2, m=m, tm=tm, nj=1, rows=[Row(x0), Row(mix0)],
                                         pars=[Par(gain(0, 1)), Par(gain(0, 2))], cts=[Row(dx1), Row(dh1)],
                                         drows=[Out(D_MODEL, F32), Out(D_MODEL, BF16)])
    dmixcat0 = mm2d("l0_dmixcat", "nt", dmix0, w_out_e, BF16)
    d_wout_e = mm2d("l0_dwout", "tn", mixcat0, dmix0)
    dzq, dzf0, dzv, dzg, d_lb, d_hnorm = hgrn_bwd("l0_dhgrn", z0, sprev, hgrn_lb_logits, hgrn_norm, dmixcat0, n_batch=n_batch, seq=seq)
    dzx, dzy, d_rcw, d_rcb, d_wa, d_ba, d_wx, d_bx, d_lam = tile_bwd(
        "l0_drglru", fn_rglru, m=m, tm=seq, nj=B_WIDTH // LANES, rows=rg_rows(), pars=rg_pars(),
        cts=[Row(dmixcat0, LANES, A_WIDTH // LANES)], drows=[Out(B_WIDTH, BF16, LANES), Out(B_WIDTH, BF16, LANES)])
    dz0 = jnp.concatenate([dzq, dzf0, dzv, dzg, dzx, dzy], axis=-1)
    dh0 = mm("l0_dh", "nt",
             Blk(dz0, (tmm, 384), lambda i, j, k: (i, k)),
             Blk(w_in_e, (None, None, D_MODEL, 384), lambda i, j, k: (k, 0, 0, 0)),
             Blk((m, D_MODEL), (tmm, D_MODEL), lambda i, j, k: (i, 0)), BF16, (nm, 1, N_DEV))
    d_win_e = mm("l0_dwin", "tn",
                 Blk(h0, (tmm, D_MODEL), lambda i, j, k: (k, 0)),
                 Blk(dz0, (tmm, 384), lambda i, j, k: (k, j)),
                 Blk((2, 4) + w_in_e.shape[1:], (None, None, None, D_MODEL, 384), lambda i, j, k: (j % 2, j // 2, 0, 0, 0)), BF16,
                 (1, N_DEV, nm))
    dx0, d_g00 = tile_bwd("l0_dprenorm", fn_input_norm, m=m, tm=tm, nj=1, rows=[Row(x0)], pars=[Par(gain(0, 0))],
                          cts=[Row(dx0a), Row(dh0)], drows=[Out(D_MODEL, F32)])

    d_gains = jnp.stack([jnp.concatenate([d_g00, d_g01, d_g02, d_g03], axis=0), jnp.concatenate([d_g10, d_g11, d_g12, d_g13], axis=0)])
    d_ffn_cw = jnp.stack([d_cw0, d_cw1], axis=2).reshape(N_DEV, n_layer, FFN_CONV, FF_BLK)
    by_core = lambda t: jnp.swapaxes(t.reshape((4, 2) + t.shape[1:]), 0, 1).astype(BF16)
    half = {
        "even_w_in": d_win_e,
        "even_w_out": by_core(d_wout_e.reshape(N_DEV, 1, D_MODEL // N_DEV, D_MODEL)),
        "odd_w_in": by_core(_cols_to_blocks(d_win_o[None, :, :3088])),
        "odd_w_out": by_core(d_wout_o.reshape(N_DEV, 1, D_MODEL // N_DEV, D_MODEL)),
        "ffn_w_up": d_wup,
        "ffn_w_down": d_wdown,
    }
    core = lax.axis_index("c").astype(jnp.int32).reshape(1)
    from_sibling = pair_exchange("exchange_core", [half[n] for n in BIG])
    chip_sums = [pair_add("add_" + n, half[n], r, core) for n, r in zip(BIG, from_sibling)]
    recv = dict(zip(BIG, quad_exchange("exchange_chips", chip_sums)))
    res = {n: adam_tiled("adam_" + n, recv[n], w[n], mom[n], var[n]) for n in BIG}
    small_send = [_cols_to_blocks(d_gains), _cols_to_blocks(d_rcw[None]), d_ffn_cw]
    recv.update(zip(SMALL_SHARDED, all_to_all("exchange_small", small_send)))

    d_ffn_cb = jnp.stack([d_cb0, d_cb1]).reshape(n_layer, 2 * D_FF)
    rep = {"hgrn_lb_logits": d_lb, "hgrn_norm": d_hnorm, "rg_conv_b": d_rcb, "rg_wa": _block_diag_grad(d_wa)[None], "rg_ba": d_ba,
           "rg_wx": _block_diag_grad(d_wx)[None], "rg_bx": d_bx, "rg_lambda": d_lam, "fox_f_bias": d_fbias[:, :C_HEADS],
           "ffn_conv_b": d_ffn_cb}
    parts = all_gather("gather_partials", [rep[n] for n in REPLICATED] + [loss_part])
    for n, p in zip(REPLICATED, parts):
        recv[n] = p
    small = SMALL_SHARDED + REPLICATED
    small_res, (loss_sum,) = adam_small("adam_small", [(recv[n], w[n], mom[n], var[n]) for n in small], [parts[-1]])
    res.update(dict(zip(small, small_res)))

    out = [loss_sum[0, 0], dx0.reshape(x.shape)]
    for k in range(4):
        out += [res[n][k] for n in NAMES]
    return tuple(out)
```

```python
import functools

import jax
import jax.numpy as jnp
from jax import lax
from jax.experimental import pallas as pl
from jax.experimental.pallas import tpu as pltpu

F32 = jnp.float32
BF16 = jnp.bfloat16

D_MODEL = 1024
A_HEADS = 4
A_WIDTH = 512
HGRN_CHUNK = 64
HGRN_SEG = 512
B_WIDTH = 512
B_BLOCKS = 8
B_BLOCK_DIM = 64
B_CONV = 4
RG_C = 8.0
C_HEADS = 16
C_HEAD_DIM = 64
D_FF = 2816
FFN_CONV = 3
EPS = 1e-6
LANES = 128
HALO = 16
N_DEV = 8
FF_BLK = 2 * D_FF // N_DEV
MESH = pl.DeviceIdType.MESH
NEG = -1e30
VMEM_LIMIT = 56 * 1024 * 1024

ADAM_LR = 0.001
ADAM_B1 = 0.9
ADAM_B2 = 0.999
ADAM_EPS = 1e-08
ADAM_WD = 0.01
ADAM_STEP = 10


def _dg(a, b, pat):
    nb = a.ndim - 2
    batch = (tuple(range(nb)), tuple(range(nb)))
    ca = a.ndim - 1 if pat[0] == "n" else a.ndim - 2
    cb = b.ndim - 2 if pat[1] == "n" else b.ndim - 1
    return lax.dot_general(a.astype(BF16), b.astype(BF16), (((ca,), (cb,)), batch), preferred_element_type=F32)


@functools.partial(jax.custom_vjp, nondiff_argnums=(2,))
def bdot(a, b, pat):
    return _dg(a, b, pat)


def _bdot_fwd(a, b, pat):
    return _dg(a, b, pat), (a, b)


def _bdot_bwd(pat, res, g):
    a, b = res
    if pat == "nn":
        return _dg(g, b, "nt"), _dg(a, g, "tn")
    if pat == "nt":
        return _dg(g, b, "nn"), _dg(g, a, "tn")
    return _dg(b, g, "nt"), _dg(a, g, "nn")


bdot.defvjp(_bdot_fwd, _bdot_bwd)


def _shift_raw(x, s, up, fill):
    if s == 0:
        return x
    n = x.shape[0]
    r = pltpu.roll(x, (n - s) if up else s, 0)
    idx = lax.broadcasted_iota(jnp.int32, x.shape, 0)
    mask = (idx >= n - s) if up else (idx < s)
    return jnp.where(mask, jnp.asarray(fill, x.dtype), r)


@functools.partial(jax.custom_vjp, nondiff_argnums=(1,))
def shift_down(x, s):
    return _shift_raw(x, s, False, 0.0)


def _shift_down_fwd(x, s):
    return _shift_raw(x, s, False, 0.0), None


def _shift_down_bwd(s, _, g):
    return (_shift_raw(g, s, True, 0.0),)


shift_down.defvjp(_shift_down_fwd, _shift_down_bwd)


def _scan_impl(a, u, up):
    n = a.shape[0]
    s = 1
    while s < n:
        u = a * _shift_raw(u, s, up, 0.0) + u
        if 2 * s < n:
            a = a * _shift_raw(a, s, up, 1.0)
        s *= 2
    return u


@jax.custom_vjp
def lin_scan(a, u):
    return _scan_impl(a, u, False)


def _lin_scan_fwd(a, u):
    h = _scan_impl(a, u, False)
    return h, (a, h)


def _lin_scan_bwd(res, g):
    a, h = res
    gh = _scan_impl(_shift_raw(a, 1, True, 0.0), g, True)
    return gh * _shift_raw(h, 1, False, 0.0), gh


lin_scan.defvjp(_lin_scan_fwd, _lin_scan_bwd)


def _cumsum_impl(x, up, period):
    n = x.shape[0]
    span = n if period is None else period
    idx = lax.broadcasted_iota(jnp.int32, x.shape, 0)
    pos = idx if period is None else idx % period
    s = 1
    while s < span:
        sh = _shift_raw(x, s, up, 0.0)
        if period is not None:
            keep = (pos < period - s) if up else (pos >= s)
            sh = jnp.where(keep, sh, 0.0)
        x = x + sh
        s *= 2
    return x


@functools.partial(jax.custom_vjp, nondiff_argnums=(1,))
def cumsum_rows(x, period):
    return _cumsum_impl(x, False, period)


def _cumsum_fwd(x, period):
    return _cumsum_impl(x, False, period), None


def _cumsum_bwd(period, _, g):
    return (_cumsum_impl(g, True, period),)


cumsum_rows.defvjp(_cumsum_fwd, _cumsum_bwd)


def _sigmoid(x):
    return jax.nn.sigmoid(x)


def _expm1(x):
    return jnp.tanh(0.5 * x) * (jnp.exp(x) + 1.0)


def _softplus(x):
    return jnp.maximum(x, 0.0) + jnp.log(1.0 + jnp.exp(-jnp.abs(x)))


def _rms(x, g):
    return x * lax.rsqrt(jnp.mean(x * x, axis=-1, keepdims=True) + EPS) * g


def fn_prenorm(x, g):
    return (_rms(x, g).astype(BF16),)


def fn_prenorm_after(x, g, _token):
    return fn_prenorm(x, g)


def fn_addnorm2(x, y, g_post, g_pre):
    x1 = x + _rms(y, g_post)
    return x1, _rms(x1, g_pre).astype(BF16)


def fn_addnorm2_after(x, y, g_post, g_pre, _token):
    return fn_addnorm2(x, y, g_post, g_pre)


def fn_input_norm(x, g):
    return x, _rms(x, g).astype(BF16)


def fn_final(x, y, tgt, g_post):
    out = x + _rms(y, g_post)
    err = out - tgt
    dy = err * (1.0 / D_MODEL)
    loss = 0.5 * jnp.sum(jnp.mean(err * err, axis=-1, keepdims=True), axis=0, keepdims=True)
    return dy, jnp.broadcast_to(loss, (1, LANES))


def fn_rms_only(y, g):
    return (_rms(y, g),)


def _causal_conv(x, w, b, taps):
    c = b
    for k in range(taps):
        c = c + w[k:k + 1, :] * shift_down(x, taps - 1 - k)
    return c


def fn_rglru(xb, yb, cw, cb, wa, ba, wx, bx, lam):
    xf = _causal_conv(xb, cw, cb, B_CONV)
    r = _sigmoid(bdot(xf, wa, "nn") + ba)
    i = _sigmoid(bdot(xf, wx, "nn") + bx)
    log_a = -RG_C * r * _softplus(-lam)
    a = jnp.exp(log_a)
    u = jnp.sqrt(-_expm1(2.0 * log_a)) * (i * xf)
    h = lin_scan(a, u)
    return ((h * jax.nn.gelu(yb)).astype(BF16),)


def fn_fox_gate(zf, bias):
    return (cumsum_rows(jax.nn.log_sigmoid(zf + bias), None),)


def fn_hgrn_seg(q, fl, v, g, st, logits, hn):
    rows = q.shape[0]
    nc = rows // HGRN_CHUNK
    l0, l1, l2 = logits[0:1, :], logits[1:2, :], logits[2:3, :]
    mx = jnp.maximum(jnp.maximum(l0, l1), l2)
    e0, e1, e2 = jnp.exp(l0 - mx), jnp.exp(l1 - mx), jnp.exp(l2 - mx)
    lb = e0 / (e0 + e1 + e2)
    forget = lb + (1.0 - lb) * _sigmoid(fl)
    qs = q * _sigmoid(q)
    kk = 1.0 - forget
    logf = jnp.log(forget)
    bcum = cumsum_rows(logf, HGRN_CHUNK)
    c3 = lambda t: t.reshape(nc, HGRN_CHUNK, 128)
    b_last = jnp.sum(c3(logf), axis=1, keepdims=True)
    bcum3 = c3(bcum)
    q_dec = c3(qs) * jnp.exp(bcum3)
    k_dec = c3(kk) * jnp.exp(-bcum3)
    k_upd = c3(kk) * jnp.exp(b_last - bcum3)
    v3 = c3(v)
    scores = bdot(q_dec, k_dec, "nt")
    ri = lax.broadcasted_iota(jnp.int32, scores.shape, 1)
    ci = lax.broadcasted_iota(jnp.int32, scores.shape, 2)
    scores = jnp.where(ri >= ci, scores, 0.0)
    o = bdot(scores, v3, "nn")
    upd_t = bdot(v3, k_upd, "tn")
    dec = jnp.exp(b_last)
    prev = []
    for n in range(nc):
        prev.append(st)
        st = st * dec[n] + upd_t[n]
    o = o + bdot(q_dec, jnp.stack(prev), "nt")
    o = o.reshape(rows, 128)
    o = o * lax.rsqrt(jnp.mean(o * o, axis=-1, keepdims=True) + EPS) * hn
    return (o * _sigmoid(g)).astype(BF16), st


def _ffn_conv(xg, xv, cw, cb):
    cg = _causal_conv(xg, cw[0], cb[0], FFN_CONV)[HALO:]
    cv = _causal_conv(xv, cw[1], cb[1], FFN_CONV)[HALO:]
    return cg, cv


def _ffn_gate(cg, cv):
    return jax.nn.gelu(cg) * cv


class Row:
    def __init__(self, arr, cb=None, off=0):
        self.arr, self.cb, self.off = arr, cb, off

    def spec(self, tm):
        if self.cb is None:
            return pl.BlockSpec((tm, self.arr.shape[1]), lambda j, i: (i, 0))
        off = self.off
        return pl.BlockSpec((tm, self.cb), lambda j, i: (i, j + off))


class Par:
    def __init__(self, arr, kind="full", bs=None):
        self.arr, self.kind, self.bs = arr, kind, bs

    def block(self):
        if self.kind == "full":
            return self.arr.shape
        if self.kind == "col":
            return (self.arr.shape[0], self.bs)
        return (self.bs, self.arr.shape[1])

    def spec(self):
        if self.kind == "full":
            return pl.BlockSpec(self.block(), lambda j, i: (0, 0))
        if self.kind == "col":
            return pl.BlockSpec(self.block(), lambda j, i: (0, j))
        return pl.BlockSpec(self.block(), lambda j, i: (j, 0))


class Out:
    def __init__(self, width, dtype, cb=None, off=0):
        self.width, self.dtype, self.cb, self.off = width, dtype, cb, off

    def spec(self, tm):
        if self.cb is None:
            return pl.BlockSpec((tm, self.width), lambda j, i: (i, 0))
        off = self.off
        return pl.BlockSpec((tm, self.cb), lambda j, i: (i, j + off))


def _params(sem):
    return pltpu.CompilerParams(dimension_semantics=sem, vmem_limit_bytes=VMEM_LIMIT)


def tile_fwd(name, fn, *, m, tm, nj, rows, pars, outs, n_acc=0):
    n_r, n_p, n_o = len(rows), len(pars), len(outs)

    def body(*refs):
        ins = [r[...] for r in refs[:n_r + n_p]]
        res = fn(*ins)
        o_refs = refs[n_r + n_p:]
        for k in range(n_o):
            o_refs[k][...] = res[k].astype(o_refs[k].dtype)
        first = jnp.logical_and(pl.program_id(0) == 0, pl.program_id(1) == 0)
        for k in range(n_acc):
            ref = o_refs[n_o + k]

            @pl.when(first)
            def _():
                ref[...] = jnp.zeros_like(ref)

            ref[...] += res[n_o + k]

    out_shape = [jax.ShapeDtypeStruct((m, o.width), o.dtype) for o in outs]
    out_specs = [o.spec(tm) for o in outs]
    for _ in range(n_acc):
        out_shape.append(jax.ShapeDtypeStruct((1, LANES), F32))
        out_specs.append(pl.BlockSpec((1, LANES), lambda j, i: (0, 0)))
    sem = ("arbitrary", "arbitrary") if n_acc else ("parallel", "parallel")
    return pl.pallas_call(
        body, grid=(nj, m // tm), name=name,
        in_specs=[r.spec(tm) for r in rows] + [p.spec() for p in pars],
        out_specs=out_specs, out_shape=out_shape, compiler_params=_params(sem),
    )(*[r.arr for r in rows], *[p.arr for p in pars])


def tile_bwd(name, fn, *, m, tm, nj, rows, pars, cts, drows):
    n_r, n_p, n_c = len(rows), len(pars), len(cts)
    want = [k for k in range(n_r) if drows[k] is not None]

    def body(*refs):
        ins = [r[...] for r in refs[:n_r + n_p]]
        ct = [r[...] for r in refs[n_r + n_p:n_r + n_p + n_c]]
        o_refs = refs[n_r + n_p + n_c:]
        res, vjp = jax.vjp(fn, *ins)
        grads = vjp(tuple(c.astype(r.dtype) for c, r in zip(ct, res)))
        for pos, k in enumerate(want):
            o_refs[pos][...] = grads[k].astype(o_refs[pos].dtype)
        for k in range(n_p):
            ref = o_refs[len(want) + k]
            first = pl.program_id(1) == 0
            if pars[k].kind == "full":
                first = jnp.logical_and(first, pl.program_id(0) == 0)

            @pl.when(first)
            def _():
                ref[...] = jnp.zeros_like(ref)

            ref[...] += grads[n_r + k].astype(F32)

    out_shape = [jax.ShapeDtypeStruct((m, drows[k].width), drows[k].dtype) for k in want]
    out_specs = [drows[k].spec(tm) for k in want]
    for p in pars:
        out_shape.append(jax.ShapeDtypeStruct(p.arr.shape, F32))
        out_specs.append(p.spec())
    return pl.pallas_call(
        body, grid=(nj, m // tm), name=name,
        in_specs=[r.spec(tm) for r in rows] + [p.spec() for p in pars] + [c.spec(tm) for c in cts],
        out_specs=out_specs, out_shape=out_shape, compiler_params=_params(("arbitrary", "arbitrary")),
    )(*[r.arr for r in rows], *[p.arr for p in pars], *[c.arr for c in cts])


class Blk:
    def __init__(self, arr, block, index):
        self.arr, self.block, self.index = arr, block, index

    def spec(self):
        return pl.BlockSpec(self.block, self.index)


def _flat2(v):
    return v if v.ndim == 2 else v.reshape(-1, v.shape[-1])


def mm(name, pat, a, b, o, out_dtype, grid, into=None):
    nk = grid[2]
    o_shape = o.arr

    def body(*refs):
        a_ref, b_ref = refs[0], refs[1]
        o_ref = refs[3] if into is not None else refs[2]
        r = _dg(_flat2(a_ref[...]), _flat2(b_ref[...]), pat)
        if nk == 1:
            o_ref[...] = r.astype(out_dtype).reshape(o_ref.shape)
            return
        acc_ref = refs[-1]
        kk = pl.program_id(2)

        @pl.when(kk == 0)
        def _():
            acc_ref[...] = r

        @pl.when(kk > 0)
        def _():
            acc_ref[...] += r

        @pl.when(kk == nk - 1)
        def _():
            o_ref[...] = acc_ref[...].astype(out_dtype).reshape(o_ref.shape)

    ob = [d for d in o.block if d is not None]
    acc_shape = (ob[0], ob[1]) if len(ob) == 2 else (ob[0] * ob[1], ob[2])
    in_specs = [a.spec(), b.spec()]
    args = [a.arr, b.arr]
    aliases = {}
    if into is not None:
        in_specs.append(pl.BlockSpec(memory_space=pl.ANY))
        args.append(into)
        aliases = {2: 0}
    return pl.pallas_call(
        body, grid=grid, name=name, in_specs=in_specs, out_specs=o.spec(),
        out_shape=jax.ShapeDtypeStruct(o_shape, out_dtype),
        scratch_shapes=[pltpu.VMEM(acc_shape, F32)] if nk > 1 else [],
        input_output_aliases=aliases,
        compiler_params=_params(("parallel", "parallel", "arbitrary")),
    )(*args)


def _div_tile(n, cap):
    if n <= cap:
        return n
    best = 128
    for t in range(128, cap + 1, 128):
        if n % t == 0:
            best = t
    return best


def mm2d(name, pat, a, b, out_dtype=F32):
    if pat == "tn":
        k, m = a.shape
    else:
        m, k = a.shape
    n = b.shape[0] if pat == "nt" else b.shape[1]
    tm, tn, tk = _div_tile(m, 1024), _div_tile(n, 1024), _div_tile(k, 1024)
    a_blk = Blk(a, (tk, tm), lambda i, j, kk: (kk, i)) if pat == "tn" else Blk(a, (tm, tk), lambda i, j, kk: (i, kk))
    b_blk = Blk(b, (tn, tk), lambda i, j, kk: (j, kk)) if pat == "nt" else Blk(b, (tk, tn), lambda i, j, kk: (kk, j))
    o_blk = Blk((m, n), (tm, tn), lambda i, j, kk: (i, j))
    return mm(name, pat, a_blk, b_blk, o_blk, out_dtype, (m // tm, n // tn, k // tk))


def hgrn_fwd(name, z, logits, hnorm, *, n_batch, seq):
    m = n_batch * seq
    ts = min(HGRN_SEG, seq)
    n_seg = seq // ts

    def body(q_ref, f_ref, v_ref, g_ref, lg_ref, hn_ref, o_ref, sp_ref, st_ref):
        s = pl.program_id(2)

        @pl.when(s == 0)
        def _():
            st_ref[...] = jnp.zeros_like(st_ref)

        st = st_ref[...]
        sp_ref[...] = st
        o, st_new = fn_hgrn_seg(q_ref[...], f_ref[...], v_ref[...], g_ref[...], st, lg_ref[...], hn_ref[...])
        o_ref[...] = o
        st_ref[...] = st_new

    part = lambda p: pl.BlockSpec((ts, 128), lambda h, b, s: (b * n_seg + s, 4 * p + h))
    return pl.pallas_call(
        body, grid=(A_HEADS, n_batch, n_seg), name=name,
        in_specs=[part(0), part(1), part(2), part(3),
                  pl.BlockSpec((3, 128), lambda h, b, s: (0, h)),
                  pl.BlockSpec((1, 128), lambda h, b, s: (0, h))],
        out_specs=[pl.BlockSpec((ts, 128), lambda h, b, s: (b * n_seg + s, h)),
                   pl.BlockSpec((128, 128), lambda h, b, s: ((b * n_seg + s) * A_HEADS + h, 0))],
        out_shape=[jax.ShapeDtypeStruct((m, A_WIDTH), BF16),
                   jax.ShapeDtypeStruct((n_batch * n_seg * A_HEADS * 128, 128), F32)],
        scratch_shapes=[pltpu.VMEM((128, 128), F32)],
        compiler_params=_params(("arbitrary", "arbitrary", "arbitrary")),
    )(z, z, z, z, logits, hnorm)


def hgrn_bwd(name, z, sprev, logits, hnorm, do, *, n_batch, seq):
    m = n_batch * seq
    ts = min(HGRN_SEG, seq)
    n_seg = seq // ts

    def body(q_ref, f_ref, v_ref, g_ref, sp_ref, lg_ref, hn_ref, do_ref, dq_ref, df_ref, dv_ref, dg_ref, dlg_ref, dhn_ref, dst_ref):
        s = pl.program_id(2)

        @pl.when(s == 0)
        def _():
            dst_ref[...] = jnp.zeros_like(dst_ref)

        res, vjp = jax.vjp(fn_hgrn_seg, q_ref[...], f_ref[...], v_ref[...], g_ref[...], sp_ref[...], lg_ref[...], hn_ref[...])
        dq, df, dv, dg, dst, dlg, dhn = vjp((do_ref[...].astype(res[0].dtype), dst_ref[...]))
        dq_ref[...] = dq.astype(dq_ref.dtype)
        df_ref[...] = df.astype(df_ref.dtype)
        dv_ref[...] = dv.astype(dv_ref.dtype)
        dg_ref[...] = dg.astype(dg_ref.dtype)
        dst_ref[...] = dst
        first = jnp.logical_and(pl.program_id(1) == 0, s == 0)

        @pl.when(first)
        def _():
            dlg_ref[...] = jnp.zeros_like(dlg_ref)
            dhn_ref[...] = jnp.zeros_like(dhn_ref)

        dlg_ref[...] += dlg
        dhn_ref[...] += dhn

    rev = lambda b, s: b * n_seg + (n_seg - 1 - s)
    part = lambda p: pl.BlockSpec((ts, 128), lambda h, b, s: (rev(b, s), 4 * p + h))
    head = pl.BlockSpec((ts, 128), lambda h, b, s: (rev(b, s), h))
    dpart = jax.ShapeDtypeStruct((m, A_WIDTH), BF16)
    return pl.pallas_call(
        body, grid=(A_HEADS, n_batch, n_seg), name=name,
        in_specs=[part(0), part(1), part(2), part(3),
                  pl.BlockSpec((128, 128), lambda h, b, s: (rev(b, s) * A_HEADS + h, 0)),
                  pl.BlockSpec((3, 128), lambda h, b, s: (0, h)),
                  pl.BlockSpec((1, 128), lambda h, b, s: (0, h)),
                  head],
        out_specs=[head, head, head, head,
                   pl.BlockSpec((3, 128), lambda h, b, s: (0, h)),
                   pl.BlockSpec((1, 128), lambda h, b, s: (0, h))],
        out_shape=[dpart, dpart, dpart, dpart,
                   jax.ShapeDtypeStruct(logits.shape, F32),
                   jax.ShapeDtypeStruct(hnorm.shape, F32)],
        scratch_shapes=[pltpu.VMEM((128, 128), F32)],
        compiler_params=_params(("arbitrary", "arbitrary", "arbitrary")),
    )(z, z, z, z, sprev, logits, hnorm, do)


def _ffn_tiles(m, seq):
    tm = min(512, seq)
    return tm, seq // tm, m // tm


def ffn_mid_fwd(name, hid, cw, cb, layer, *, m, seq):
    tm, n_t, n_i = _ffn_tiles(m, seq)
    hb = tm // HALO

    def body(x_ref, xb_ref, cw_ref, cb_ref, o_ref):
        first = pl.program_id(1) % n_t == 0
        before = jnp.where(first, 0.0, xb_ref[...])
        ext = jnp.concatenate([before, x_ref[...]], axis=1)
        cg, cv = _ffn_conv(ext[0], ext[1], cw_ref[...], cb_ref[...])
        o_ref[...] = _ffn_gate(cg, cv).astype(o_ref.dtype)

    return pl.pallas_call(
        body, grid=(N_DEV // 2, n_i), name=name,
        in_specs=[pl.BlockSpec((2, None, tm, FF_BLK), lambda d, i: (0, d, i, 0)),
                  pl.BlockSpec((2, None, HALO, FF_BLK), lambda d, i: (0, d, jnp.maximum(i * hb - 1, 0), 0)),
                  pl.BlockSpec((2, None, None, FFN_CONV, FF_BLK), lambda d, i: (0, d, layer, 0, 0)),
                  pl.BlockSpec((None, 2, None, 1, FF_BLK), lambda d, i: (layer, 0, d, 0, 0))],
        out_specs=pl.BlockSpec((None, tm, FF_BLK), lambda d, i: (d, i, 0)),
        out_shape=jax.ShapeDtypeStruct((N_DEV // 2, m, FF_BLK), BF16),
        compiler_params=_params(("parallel", "parallel")),
    )(hid, hid, cw, cb)


def ffn_mid_bwd(name, hid, cw, cb, dact, layer, *, m, seq):
    tm, n_t, n_i = _ffn_tiles(m, seq)
    hb = tm // HALO
    last_blk = m // HALO - 1

    def body(x_ref, xb_ref, xa_ref, cw_ref, cb_ref, da_ref, daa_ref, dx_ref, dcw_ref, dcb_ref):
        i = pl.program_id(1)
        first = i % n_t == 0
        last = i % n_t == n_t - 1
        before = jnp.where(first, 0.0, xb_ref[...])
        ext = jnp.concatenate([before, x_ref[...], xa_ref[...]], axis=1)
        dact_ext = jnp.concatenate([da_ref[...].astype(F32), jnp.where(last, 0.0, daa_ref[...].astype(F32))], axis=0)
        (cg, cv), vjp_conv = jax.vjp(_ffn_conv, ext[0], ext[1], cw_ref[...], cb_ref[...])
        _, vjp_gate = jax.vjp(_ffn_gate, cg, cv)
        dcg, dcv = vjp_gate(dact_ext)
        dxg, dxv, _, _ = vjp_conv((dcg, dcv))
        dx_ref[0] = dxg[HALO:HALO + tm].astype(dx_ref.dtype)
        dx_ref[1] = dxv[HALO:HALO + tm].astype(dx_ref.dtype)
        own = lax.broadcasted_iota(jnp.int32, dcg.shape, 0) < tm
        _, _, dcw, dcb = vjp_conv((jnp.where(own, dcg, 0.0), jnp.where(own, dcv, 0.0)))

        @pl.when(i == 0)
        def _():
            dcw_ref[...] = jnp.zeros_like(dcw_ref)
            dcb_ref[...] = jnp.zeros_like(dcb_ref)

        dcw_ref[...] += dcw
        dcb_ref[...] += dcb

    return pl.pallas_call(
        body, grid=(N_DEV // 2, n_i), name=name,
        in_specs=[pl.BlockSpec((2, None, tm, FF_BLK), lambda d, i: (0, d, i, 0)),
                  pl.BlockSpec((2, None, HALO, FF_BLK), lambda d, i: (0, d, jnp.maximum(i * hb - 1, 0), 0)),
                  pl.BlockSpec((2, None, HALO, FF_BLK), lambda d, i: (0, d, jnp.minimum((i + 1) * hb, last_blk), 0)),
                  pl.BlockSpec((2, None, None, FFN_CONV, FF_BLK), lambda d, i: (0, d, layer, 0, 0)),
                  pl.BlockSpec((None, 2, None, 1, FF_BLK), lambda d, i: (layer, 0, d, 0, 0)),
                  pl.BlockSpec((None, tm, FF_BLK), lambda d, i: (d, i, 0)),
                  pl.BlockSpec((None, HALO, FF_BLK), lambda d, i: (d, jnp.minimum((i + 1) * hb, last_blk), 0))],
        out_specs=[pl.BlockSpec((2, None, tm, FF_BLK), lambda d, i: (0, d, i, 0)),
                   pl.BlockSpec((2, None, FFN_CONV, FF_BLK), lambda d, i: (0, d, 0, 0)),
                   pl.BlockSpec((2, None, 1, FF_BLK), lambda d, i: (0, d, 0, 0))],
        out_shape=[jax.ShapeDtypeStruct((2, N_DEV // 2, m, FF_BLK), BF16),
                   jax.ShapeDtypeStruct((2, N_DEV // 2, FFN_CONV, FF_BLK), F32),
                   jax.ShapeDtypeStruct((2, N_DEV // 2, 1, FF_BLK), F32)],
        compiler_params=_params(("arbitrary", "arbitrary")),
    )(hid, hid, hid, cw, cb, dact, dact)


ATT_BLK = 512
N_PAIR = C_HEADS // 2
TERM_W = C_HEADS * LANES


def term_placement():
    import numpy as np
    place = np.zeros((6, LANES, TERM_W), np.float32)
    ones_q = np.zeros((1, TERM_W), np.float32)
    ones_k = np.zeros((1, TERM_W), np.float32)
    for h in range(C_HEADS):
        for j in range(3):
            place[j, h, h * LANES + C_HEAD_DIM + j] = 1.0
            place[3 + j, h, h * LANES + C_HEAD_DIM + 3 + j] = 1.0
            ones_q[0, h * LANES + C_HEAD_DIM + 3 + j] = 1.0
            ones_k[0, h * LANES + C_HEAD_DIM + j] = 1.0
    return (jnp.asarray(place.reshape(6 * LANES, TERM_W), BF16), jnp.asarray(ones_q, F32), jnp.asarray(ones_k, F32))


def fn_fox_terms(c, place, ones_q, ones_k):
    parts = _split3(c)
    qt = ones_q
    kt = ones_k
    for j in range(3):
        qt = qt + _dg(parts[j], place[j * LANES:(j + 1) * LANES], "nn")
        kt = kt - _dg(parts[j], place[(3 + j) * LANES:(4 + j) * LANES], "nn")
    return qt.astype(BF16), kt.astype(BF16)


def _head_tile(z, terms, e):
    lane = lax.broadcasted_iota(jnp.int32, z.shape, 1)
    base = z if e == 0 else pltpu.roll(z, C_HEAD_DIM, 1)
    return jnp.where(lane < C_HEAD_DIM, base, terms.astype(z.dtype))


def _head_only(z, e):
    lane = lax.broadcasted_iota(jnp.int32, z.shape, 1)
    mine = (lane < C_HEAD_DIM) if e == 0 else (lane >= C_HEAD_DIM)
    return jnp.where(mine, z, jnp.zeros_like(z)).astype(BF16)


def _pair_tile(a0, a1):
    lane = lax.broadcasted_iota(jnp.int32, a0.shape, 1)
    return jnp.where(lane < C_HEAD_DIM, a0, pltpu.roll(a1, C_HEAD_DIM, 1))


def _lane_col(a, k):
    lane = lax.broadcasted_iota(jnp.int32, a.shape, 1)
    return jnp.sum(jnp.where(lane == k, a, 0.0), axis=1, keepdims=True)


def _causal(s):
    key = lax.broadcasted_iota(jnp.int32, s.shape, 0)
    qry = lax.broadcasted_iota(jnp.int32, s.shape, 1)
    return qry >= key


def fox_pair_fwd(name, z, qterm, kterm, *, n_batch, seq):
    m = n_batch * seq
    blk = min(ATT_BLK, seq)
    nq = seq // blk
    dh = C_HEAD_DIM

    def body(zq_ref, zk_ref, zv_ref, qt_ref, kt_ref, o_ref, lse_ref, ka_ref, vt_ref):
        qi = pl.program_id(2)

        @pl.when(qi == 0)
        def _():
            zk = zk_ref[...]
            for e in range(2):
                ka_ref[e] = _head_tile(zk, kt_ref[:, e * LANES:(e + 1) * LANES], e).astype(BF16)
            for cb in range(nq):
                vt_ref[cb] = zv_ref[cb * blk:(cb + 1) * blk, :].T.astype(BF16)

        zq = zq_ref[...] * dh ** -0.5
        qa = [_head_tile(zq, qt_ref[:, e * LANES:(e + 1) * LANES], e).astype(BF16) for e in range(2)]

        def block(j, carry, diagonal):
            rows = pl.ds(pl.multiple_of(j * blk, blk), blk)
            out = []
            for e in range(2):
                mx, l, acc = carry[e]
                s = _dg(ka_ref[e, rows, :], qa[e], "nt")
                if diagonal:
                    s = jnp.where(_causal(s), s, NEG)
                mx_new = jnp.maximum(mx, jnp.max(s, axis=0, keepdims=True))
                p = jnp.exp(s - mx_new)
                alpha = jnp.exp(mx - mx_new)
                l = alpha * l + jnp.sum(p, axis=0, keepdims=True)
                acc = alpha * acc + _dg(vt_ref[j, e * dh:(e + 1) * dh, :], p, "nn")
                out.append((mx_new, l, acc))
            return tuple(out)

        one = (jnp.full((1, blk), NEG, F32), jnp.zeros((1, blk), F32), jnp.zeros((dh, blk), F32))
        carry = lax.fori_loop(0, qi, lambda j, cr: block(j, cr, False), (one, one))
        res = block(qi, carry, True)
        ot = jnp.concatenate([res[e][2] / res[e][1] for e in range(2)], axis=0)
        o_ref[...] = ot.T.astype(o_ref.dtype)
        for e in range(2):
            lse_ref[e] = res[e][0] + jnp.log(res[e][1])

    col = lambda part: (lambda b, g, i: (b, part * N_PAIR + g))
    return pl.pallas_call(
        body, grid=(n_batch, N_PAIR, nq), name=name,
        in_specs=[pl.BlockSpec((blk, LANES), lambda b, g, i: (b * nq + i, g)),
                  pl.BlockSpec((seq, LANES), col(1)),
                  pl.BlockSpec((seq, LANES), col(2)),
                  pl.BlockSpec((blk, 2 * LANES), lambda b, g, i: (b * nq + i, g)),
                  pl.BlockSpec((seq, 2 * LANES), lambda b, g, i: (b, g))],
        out_specs=[pl.BlockSpec((blk, LANES), lambda b, g, i: (b * nq + i, g)),
                   pl.BlockSpec((None, None, None, 2, 1, blk), lambda b, g, i: (b, g, i, 0, 0, 0))],
        out_shape=[jax.ShapeDtypeStruct((m, D_MODEL), BF16), jax.ShapeDtypeStruct((n_batch, N_PAIR, nq, 2, 1, blk), F32)],
        scratch_shapes=[pltpu.VMEM((2, seq, LANES), BF16), pltpu.VMEM((nq, LANES, blk), BF16)],
        compiler_params=_params(("parallel", "parallel", "arbitrary")),
    )(z, z, z, qterm, kterm)


def fox_pair_bwd(name, z, qterm, kterm, o, do, lse, *, n_batch, seq):
    m = n_batch * seq
    blk = min(ATT_BLK, seq)
    nq = seq // blk
    dh = C_HEAD_DIM

    def body(zq_ref, zk_ref, zv_ref, qt_ref, kt_ref, o_ref, do_ref, lse_ref, dq_ref, dk_ref, dv_ref, dc_ref,
             qa_ref, doh_ref, del_ref, dqt_ref, dk_acc, dv_acc):
        g, j = pl.program_id(1), pl.program_id(2)
        lane = lax.broadcasted_iota(jnp.int32, (blk, LANES), 1)

        @pl.when(jnp.logical_and(g == 0, j == 0))
        def _():
            dc_ref[...] = jnp.zeros_like(dc_ref)

        @pl.when(j == 0)
        def _():
            zq = zq_ref[...] * dh ** -0.5
            dov = do_ref[...]
            for e in range(2):
                qa_ref[e] = _head_tile(zq, qt_ref[:, e * LANES:(e + 1) * LANES], e).astype(BF16)
                doh_ref[e] = _head_only(dov, e)
            for cb in range(nq):
                rows = slice(cb * blk, (cb + 1) * blk)
                prod_t = (do_ref[rows, :].astype(F32) * o_ref[rows, :].astype(F32)).T
                for e in range(2):
                    del_ref[cb, e] = jnp.sum(prod_t[e * dh:(e + 1) * dh], axis=0, keepdims=True)
            dqt_ref[...] = jnp.zeros_like(dqt_ref)

        zk, zv = zk_ref[...], zv_ref[...]
        ka32 = [_head_tile(zk, kt_ref[:, e * LANES:(e + 1) * LANES], e) for e in range(2)]
        ka = [t.astype(BF16) for t in ka32]
        kat = [t.T.astype(BF16) for t in ka32]
        vh = [_head_only(zv, e) for e in range(2)]
        dk_acc[...] = jnp.zeros_like(dk_acc)
        dv_acc[...] = jnp.zeros_like(dv_acc)

        def block(i, diagonal):
            rows = pl.ds(pl.multiple_of(i * blk, blk), blk)
            for e in range(2):
                qv, dov = qa_ref[e, rows, :], doh_ref[e, rows, :]
                p = jnp.exp(_dg(ka[e], qv, "nt") - lse_ref[i, e])
                if diagonal:
                    p = jnp.where(_causal(p), p, 0.0)
                dv_acc[...] += _dg(p, dov, "nn")
                ds = p * (_dg(vh[e], dov, "nt") - del_ref[i, e])
                dk_acc[e] += _dg(ds, qv, "nn")
                dqt_ref[i, e] += _dg(kat[e], ds, "nn")

        block(j, True)

        def rest(i, carry):
            block(i, False)
            return carry

        lax.fori_loop(j + 1, nq, rest, 0)
        dk0, dk1 = dk_acc[0], dk_acc[1]
        dk_ref[...] = _pair_tile(dk0, dk1).astype(dk_ref.dtype)
        dv_ref[...] = dv_acc[...].astype(dv_ref.dtype)
        rows_j = pl.ds(pl.multiple_of(j * blk, blk), blk)
        for e, dke in enumerate((dk0, dk1)):
            dc_ref[rows_j, :] -= jnp.where(lane == 2 * g + e, _lane_col(dke, dh + 3), 0.0)

        @pl.when(j == nq - 1)
        def _():
            for i in range(nq):
                nat = [dqt_ref[i, e].T for e in range(2)]
                rows = slice(i * blk, (i + 1) * blk)
                dq_ref[rows, :] = (_pair_tile(nat[0], nat[1]) * dh ** -0.5).astype(dq_ref.dtype)
                for e in range(2):
                    dc_ref[rows, :] += jnp.where(lane == 2 * g + e, _lane_col(nat[e], dh), 0.0)

    col = lambda part: (lambda b, g, j: (b, part * N_PAIR + g))
    colj = lambda part: (lambda b, g, j: (b * nq + j, part * N_PAIR + g))
    pair = jax.ShapeDtypeStruct((m, D_MODEL), BF16)
    return pl.pallas_call(
        body, grid=(n_batch, N_PAIR, nq), name=name,
        in_specs=[pl.BlockSpec((seq, LANES), col(0)),
                  pl.BlockSpec((blk, LANES), colj(1)),
                  pl.BlockSpec((blk, LANES), colj(2)),
                  pl.BlockSpec((seq, 2 * LANES), lambda b, g, j: (b, g)),
                  pl.BlockSpec((blk, 2 * LANES), lambda b, g, j: (b * nq + j, g)),
                  pl.BlockSpec((seq, LANES), col(0)),
                  pl.BlockSpec((seq, LANES), col(0)),
                  pl.BlockSpec((None, None, nq, 2, 1, blk), lambda b, g, j: (b, g, 0, 0, 0, 0))],
        out_specs=[pl.BlockSpec((seq, LANES), col(0)),
                   pl.BlockSpec((blk, LANES), colj(0)),
                   pl.BlockSpec((blk, LANES), colj(0)),
                   pl.BlockSpec((seq, LANES), lambda b, g, j: (b, 0))],
        out_shape=[pair, pair, pair, jax.ShapeDtypeStruct((m, LANES), F32)],
        scratch_shapes=[pltpu.VMEM((2, seq, LANES), BF16), pltpu.VMEM((2, seq, LANES), BF16),
                        pltpu.VMEM((nq, 2, 1, blk), F32), pltpu.VMEM((nq, 2, LANES, blk), F32),
                        pltpu.VMEM((2, blk, LANES), F32), pltpu.VMEM((blk, LANES), F32)],
        compiler_params=_params(("arbitrary", "arbitrary", "arbitrary")),
    )(z, z, z, qterm, kterm, o, do, lse)


def _split3(c):
    c1 = c.astype(BF16)
    r1 = c - c1.astype(F32)
    c2 = r1.astype(BF16)
    c3 = (r1 - c2.astype(F32)).astype(BF16)
    return c1, c2, c3


def fox_operands(q, k, c):
    bh, seq, dh = q.shape
    c1, c2, c3 = (t[..., None] for t in _split3(c))
    one = jnp.ones((bh, seq, 1), BF16)
    pad = jnp.zeros((bh, seq, LANES - dh - 6), BF16)
    qa = jnp.concatenate([(q * dh ** -0.5).astype(BF16), c1, c2, c3, one, one, one, pad], axis=-1)
    ka = jnp.concatenate([k.astype(BF16), one, one, one, -c1, -c2, -c3, pad], axis=-1)
    return qa, ka


def fox_fwd(name, qa, ka, vt):
    bh, seq, da = qa.shape
    blk = min(ATT_BLK, seq)
    nq = seq // blk
    dh = vt.shape[2]

    def body(q_ref, k_ref, v_ref, o_ref, lse_ref):
        qi = pl.program_id(1)
        qv = q_ref[0]

        def block(j, carry, diagonal):
            mx, l, acc = carry
            kj = k_ref[0, pl.ds(pl.multiple_of(j * blk, blk), blk), :]
            s = _dg(kj, qv, "nt")
            if diagonal:
                key = lax.broadcasted_iota(jnp.int32, (blk, blk), 0)
                qry = lax.broadcasted_iota(jnp.int32, (blk, blk), 1)
                s = jnp.where(qry >= key, s, NEG)
            mx_new = jnp.maximum(mx, jnp.max(s, axis=0, keepdims=True))
            p = jnp.exp(s - mx_new)
            alpha = jnp.exp(mx - mx_new)
            l = alpha * l + jnp.sum(p, axis=0, keepdims=True)
            acc = alpha * acc + _dg(v_ref[0, j], p, "nn")
            return mx_new, l, acc

        init = (jnp.full((1, blk), NEG, F32), jnp.zeros((1, blk), F32), jnp.zeros((dh, blk), F32))
        carry = lax.fori_loop(0, qi, lambda j, cr: block(j, cr, False), init)
        mx, l, acc = block(qi, carry, True)
        o_ref[0] = (acc / l).astype(o_ref.dtype)
        lse_ref[0, 0] = mx + jnp.log(l)

    return pl.pallas_call(
        body, grid=(bh, nq), name=name,
        in_specs=[pl.BlockSpec((1, blk, da), lambda b, i: (b, i, 0)),
                  pl.BlockSpec((1, seq, da), lambda b, i: (b, 0, 0)),
                  pl.BlockSpec((1, nq, dh, blk), lambda b, i: (b, 0, 0, 0))],
        out_specs=[pl.BlockSpec((1, dh, blk), lambda b, i: (b, 0, i)),
                   pl.BlockSpec((1, 1, 1, blk), lambda b, i: (b, i, 0, 0))],
        out_shape=[jax.ShapeDtypeStruct((bh, dh, seq), BF16), jax.ShapeDtypeStruct((bh, nq, 1, blk), F32)],
        compiler_params=_params(("parallel", "arbitrary")),
    )(qa, ka, vt)


def fox_bwd(name, qa, ka, kat, v, do, dot, ot, lse):
    bh, seq, da = qa.shape
    blk = min(ATT_BLK, seq)
    nq = seq // blk
    dh = v.shape[2]

    def body(q_ref, k_ref, kt_ref, v_ref, do_ref, dot_ref, ot_ref, lse_ref, dq_ref, dk_ref, dv_ref, del_ref):
        j = pl.program_id(1)

        @pl.when(j == 0)
        def _():
            dq_ref[...] = jnp.zeros_like(dq_ref)
            for i in range(nq):
                cols = slice(i * blk, (i + 1) * blk)
                del_ref[i] = jnp.sum(dot_ref[0, :, cols].astype(F32) * ot_ref[0, :, cols].astype(F32), axis=0, keepdims=True)

        kj, kjt, vj = k_ref[0], kt_ref[0, 0], v_ref[0]

        def block(i, carry, diagonal):
            dk, dv = carry
            rows = pl.ds(pl.multiple_of(i * blk, blk), blk)
            qv, dov = q_ref[0, rows, :], do_ref[0, rows, :]
            p = jnp.exp(_dg(kj, qv, "nt") - lse_ref[0, i])
            if diagonal:
                key = lax.broadcasted_iota(jnp.int32, (blk, blk), 0)
                qry = lax.broadcasted_iota(jnp.int32, (blk, blk), 1)
                p = jnp.where(qry >= key, p, 0.0)
            dv = dv + _dg(p, dov, "nn")
            ds = p * (_dg(vj, dov, "nt") - del_ref[i])
            dk = dk + _dg(ds, qv, "nn")
            dq_ref[0, i] += _dg(kjt, ds, "nn")
            return dk, dv

        init = (jnp.zeros((blk, da), F32), jnp.zeros((blk, dh), F32))
        carry = block(j, init, True)
        dk, dv = lax.fori_loop(j + 1, nq, lambda i, cr: block(i, cr, False), carry)
        dk_ref[0] = dk
        dv_ref[0] = dv

    return pl.pallas_call(
        body, grid=(bh, nq), name=name,
        in_specs=[pl.BlockSpec((1, seq, da), lambda b, j: (b, 0, 0)),
                  pl.BlockSpec((1, blk, da), lambda b, j: (b, j, 0)),
                  pl.BlockSpec((1, 1, da, blk), lambda b, j: (b, j, 0, 0)),
                  pl.BlockSpec((1, blk, dh), lambda b, j: (b, j, 0)),
                  pl.BlockSpec((1, seq, dh), lambda b, j: (b, 0, 0)),
                  pl.BlockSpec((1, dh, seq), lambda b, j: (b, 0, 0)),
                  pl.BlockSpec((1, dh, seq), lambda b, j: (b, 0, 0)),
                  pl.BlockSpec((1, nq, 1, blk), lambda b, j: (b, 0, 0, 0))],
        out_specs=[pl.BlockSpec((1, nq, da, blk), lambda b, j: (b, 0, 0, 0)),
                   pl.BlockSpec((1, blk, da), lambda b, j: (b, j, 0)),
                   pl.BlockSpec((1, blk, dh), lambda b, j: (b, j, 0))],
        out_shape=[jax.ShapeDtypeStruct((bh, nq, da, blk), F32), jax.ShapeDtypeStruct((bh, seq, da), F32),
                   jax.ShapeDtypeStruct((bh, seq, dh), F32)],
        scratch_shapes=[pltpu.VMEM((nq, 1, blk), F32)],
        compiler_params=_params(("parallel", "arbitrary")),
    )(qa, ka, kat, v, do, dot, ot, lse)


def _mesh_pos():
    return lax.axis_index("x"), lax.axis_index("y"), lax.axis_index("c")


def _flip(v, bit):
    return 1 - v if bit else v


def all_gather(name, blocks):
    n = len(blocks)

    def body(*refs):
        x_refs, out_refs = refs[:n], refs[n:2 * n]
        send_sems, recv_sems, local_sems = refs[2 * n:]
        x, y, c = _mesh_pos()
        me, sibling = (x, y, c), (x, y, 1 - c)
        chips = [(1 - x, y), (x, 1 - y), (1 - x, 1 - y)]

        def slot(a, px, py, pc):
            return out_refs[a].at[4 * px + 2 * py + pc]

        def copy(a, k, blk, to, src=None):
            return pltpu.make_async_remote_copy(
                src_ref=slot(a, *blk) if src is None else src, dst_ref=slot(a, *blk),
                send_sem=send_sems.at[a, k], recv_sem=recv_sems.at[a, k], device_id=to, device_id_type=MESH)

        mine = [pltpu.make_async_copy(x_refs[a], slot(a, *me), local_sems.at[a]) for a in range(n)]
        for cp in mine:
            cp.start()
        sends = []
        for a in range(n):
            sends.append(copy(a, 0, me, sibling, src=x_refs[a]))
            sends += [copy(a, 1 + j, me, (*chip, c), src=x_refs[a]) for j, chip in enumerate(chips)]
        for cp in sends:
            cp.start()
        for j, chip in enumerate(chips):
            for a in range(n):
                copy(a, 1 + j, (*chip, c), me).wait_recv()
                passed = copy(a, 4 + j, (*chip, c), sibling)
                passed.start()
                sends.append(passed)
        for a in range(n):
            copy(a, 0, sibling, me).wait_recv()
            for j, chip in enumerate(chips):
                copy(a, 4 + j, (*chip, 1 - c), me).wait_recv()
        for cp in sends:
            cp.wait_send()
        for cp in mine:
            cp.wait()

    hbm = pl.BlockSpec(memory_space=pl.ANY)
    return pl.pallas_call(
        body, name=name, out_shape=[jax.ShapeDtypeStruct((N_DEV,) + b.shape, b.dtype) for b in blocks],
        in_specs=[hbm] * n, out_specs=[hbm] * n,
        scratch_shapes=[pltpu.SemaphoreType.DMA((n, 7)), pltpu.SemaphoreType.DMA((n, 7)), pltpu.SemaphoreType.DMA((n,))],
    )(*blocks)


def _peers(x, y, c):
    return [(_flip(x, k & 4), _flip(y, k & 2), _flip(c, k & 1)) for k in range(1, N_DEV)]


def gather_start(name, blocks, lands):
    n = len(blocks)

    def body(*refs):
        x_refs, land_refs = refs[:n], refs[n:2 * n]
        send_sems, recv_sems = refs[2 * n], refs[2 * n + 1]
        token = refs[-1]
        x, y, c = _mesh_pos()
        me = 4 * x + 2 * y + c
        for k, peer in enumerate(_peers(x, y, c)):
            for a in range(n):
                pltpu.make_async_remote_copy(
                    src_ref=x_refs[a], dst_ref=land_refs[a].at[me], send_sem=send_sems.at[7 * a + k], recv_sem=recv_sems.at[7 * a + k],
                    device_id=peer, device_id_type=MESH).start()
        token[...] = jnp.zeros_like(token)

    hbm = pl.BlockSpec(memory_space=pltpu.HBM)
    sem = pl.BlockSpec(memory_space=pltpu.SEMAPHORE)
    out_shape = ([pltpu.SemaphoreType.DMA((7 * n,)), pltpu.SemaphoreType.DMA((7 * n,))]
                 + [pltpu.HBM(b.shape, b.dtype) for b in blocks] + [pltpu.HBM(l.shape, l.dtype) for l in lands]
                 + [jax.ShapeDtypeStruct((8, LANES), F32)])
    res = pl.pallas_call(
        body, name=name, out_shape=out_shape, in_specs=[hbm] * (2 * n),
        out_specs=[sem, sem] + [hbm] * (2 * n) + [pl.BlockSpec(memory_space=pltpu.VMEM)],
        input_output_aliases={a: 2 + a for a in range(2 * n)},
        compiler_params=pltpu.CompilerParams(has_side_effects=pltpu.SideEffectType.DATAFLOW_SIDE_EFFECTING),
    )(*[pltpu.with_memory_space_constraint(b, pltpu.HBM) for b in blocks],
      *[pltpu.with_memory_space_constraint(l, pltpu.HBM) for l in lands])
    return res[0], res[1], res[2:2 + n], res[2 + n:2 + 2 * n], res[-1]


def gather_wait(name, send_sems, recv_sems, blocks, lands, after):
    n = len(blocks)

    def body(*refs):
        x_refs, land_refs = refs[:n], refs[n:2 * n]
        s_sems, r_sems = refs[2 * n], refs[2 * n + 1]
        x, y, c = _mesh_pos()
        me = 4 * x + 2 * y + c
        for k, peer in enumerate(_peers(x, y, c)):
            for a in range(n):
                cp = pltpu.make_async_remote_copy(
                    src_ref=x_refs[a], dst_ref=land_refs[a].at[me], send_sem=s_sems.at[7 * a + k], recv_sem=r_sems.at[7 * a + k],
                    device_id=peer, device_id_type=MESH)
                cp.wait_send()
                cp.wait_recv()

    hbm = pl.BlockSpec(memory_space=pltpu.HBM)
    sem = pl.BlockSpec(memory_space=pltpu.SEMAPHORE)
    res = pl.pallas_call(
        body, name=name,
        out_shape=[pltpu.HBM(b.shape, b.dtype) for b in blocks] + [pltpu.HBM(l.shape, l.dtype) for l in lands],
        in_specs=[hbm] * (2 * n) + [sem, sem, pl.BlockSpec(memory_space=pl.ANY)], out_specs=[hbm] * (2 * n),
        input_output_aliases={a: a for a in range(2 * n)},
        compiler_params=pltpu.CompilerParams(has_side_effects=pltpu.SideEffectType.DATAFLOW_SIDE_EFFECTING),
    )(*blocks, *lands, send_sems, recv_sems, after)
    return res[n:]


def _split_exchange(name, sends, lands, sems, after):
    n = len(sends)
    starting = sems is None

    def body(*refs):
        s_refs, l_refs = refs[:n], refs[n:2 * n]
        send_sems, recv_sems = refs[2 * n], refs[2 * n + 1]
        x, y, c = _mesh_pos()
        me = 4 * x + 2 * y + c
        for k, (px, py, pc) in enumerate(_peers(x, y, c)):
            for a in range(n):
                cp = pltpu.make_async_remote_copy(
                    src_ref=s_refs[a].at[4 * px + 2 * py + pc], dst_ref=l_refs[a].at[me],
                    send_sem=send_sems.at[7 * a + k], recv_sem=recv_sems.at[7 * a + k],
                    device_id=(px, py, pc), device_id_type=MESH)
                if starting:
                    cp.start()
                else:
                    cp.wait_send()
                    cp.wait_recv()
        if starting:
            refs[-1][...] = jnp.zeros_like(refs[-1])

    hbm = pl.BlockSpec(memory_space=pltpu.HBM)
    sem = pl.BlockSpec(memory_space=pltpu.SEMAPHORE)
    thru = [pltpu.HBM(t.shape, t.dtype) for t in list(sends) + list(lands)]
    effect = pltpu.CompilerParams(has_side_effects=pltpu.SideEffectType.DATAFLOW_SIDE_EFFECTING)
    if starting:
        res = pl.pallas_call(
            body, name=name, in_specs=[hbm] * (2 * n),
            out_shape=[pltpu.SemaphoreType.DMA((7 * n,)), pltpu.SemaphoreType.DMA((7 * n,))] + thru + [jax.ShapeDtypeStruct((8, LANES), F32)],
            out_specs=[sem, sem] + [hbm] * (2 * n) + [pl.BlockSpec(memory_space=pltpu.VMEM)],
            input_output_aliases={a: 2 + a for a in range(2 * n)}, compiler_params=effect,
        )(*[pltpu.with_memory_space_constraint(t, pltpu.HBM) for t in list(sends) + list(lands)])
        return res[0], res[1], res[2:2 + n], res[2 + n:2 + 2 * n], res[-1]
    res = pl.pallas_call(
        body, name=name, out_shape=thru, in_specs=[hbm] * (2 * n) + [sem, sem, pl.BlockSpec(memory_space=pl.ANY)],
        out_specs=[hbm] * (2 * n), input_output_aliases={a: a for a in range(2 * n)}, compiler_params=effect,
    )(*sends, *lands, sems[0], sems[1], after)
    return res[n:]


def own_slot_only(send, me):
    mine = lax.dynamic_index_in_dim(send, me, 0, keepdims=False)
    return lax.dynamic_update_index_in_dim(lax.empty(send.shape, send.dtype), mine, me, 0)


def all_to_all(name, sends):
    n = len(sends)

    def body(*refs):
        s_refs, r_refs = refs[:n], refs[n:2 * n]
        send_sems, recv_sems, local_sems = refs[2 * n:]
        x, y, c = _mesh_pos()
        me = 4 * x + 2 * y + c
        mine = [pltpu.make_async_copy(s_refs[a].at[me], r_refs[a].at[me], local_sems.at[a]) for a in range(n)]
        for cp in mine:
            cp.start()
        copies = []
        for k in range(1, N_DEV):
            px, py, pc = _flip(x, k & 4), _flip(y, k & 2), _flip(c, k & 1)
            for a in range(n):
                copies.append(pltpu.make_async_remote_copy(
                    src_ref=s_refs[a].at[4 * px + 2 * py + pc], dst_ref=r_refs[a].at[me],
                    send_sem=send_sems.at[a, k - 1], recv_sem=recv_sems.at[a, k - 1],
                    device_id=(px, py, pc), device_id_type=MESH))
        for cp in copies:
            cp.start()
        for cp in copies:
            cp.wait_recv()
        for cp in copies:
            cp.wait_send()
        for cp in mine:
            cp.wait()

    hbm = pl.BlockSpec(memory_space=pl.ANY)
    return pl.pallas_call(
        body, name=name, out_shape=[jax.ShapeDtypeStruct(s.shape, s.dtype) for s in sends],
        in_specs=[hbm] * n, out_specs=[hbm] * n,
        scratch_shapes=[pltpu.SemaphoreType.DMA((n, 7)), pltpu.SemaphoreType.DMA((n, 7)), pltpu.SemaphoreType.DMA((n,))],
    )(*sends)


def pair_exchange(name, hs):
    n = len(hs)

    def body(*refs):
        h_refs, r_refs = refs[:n], refs[n:2 * n]
        send_sems, recv_sems = refs[2 * n:]
        x, y, c = _mesh_pos()
        copies = [pltpu.make_async_remote_copy(
            src_ref=h_refs[a].at[1 - c], dst_ref=r_refs[a], send_sem=send_sems.at[a], recv_sem=recv_sems.at[a],
            device_id=(x, y, 1 - c), device_id_type=MESH) for a in range(n)]
        for cp in copies:
            cp.start()
        for cp in copies:
            cp.wait_recv()
        for cp in copies:
            cp.wait_send()

    hbm = pl.BlockSpec(memory_space=pl.ANY)
    return pl.pallas_call(
        body, name=name, out_shape=[jax.ShapeDtypeStruct(h.shape[1:], h.dtype) for h in hs],
        in_specs=[hbm] * n, out_specs=[hbm] * n,
        scratch_shapes=[pltpu.SemaphoreType.DMA((n,)), pltpu.SemaphoreType.DMA((n,))],
    )(*hs)


def quad_exchange(name, ss):
    n = len(ss)

    def body(*refs):
        s_refs, r_refs = refs[:n], refs[n:2 * n]
        send_sems, recv_sems, local_sems = refs[2 * n:]
        x, y, c = _mesh_pos()
        me = 2 * x + y
        mine = [pltpu.make_async_copy(s_refs[a].at[me], r_refs[a].at[me], local_sems.at[a]) for a in range(n)]
        for cp in mine:
            cp.start()
        copies = []
        for k in range(1, 4):
            px, py = _flip(x, k & 2), _flip(y, k & 1)
            for a in range(n):
                copies.append(pltpu.make_async_remote_copy(
                    src_ref=s_refs[a].at[2 * px + py], dst_ref=r_refs[a].at[me],
                    send_sem=send_sems.at[a, k - 1], recv_sem=recv_sems.at[a, k - 1],
                    device_id=(px, py, c), device_id_type=MESH))
        for cp in copies:
            cp.start()
        for cp in copies:
            cp.wait_recv()
        for cp in copies:
            cp.wait_send()
        for cp in mine:
            cp.wait()

    hbm = pl.BlockSpec(memory_space=pl.ANY)
    return pl.pallas_call(
        body, name=name, out_shape=[jax.ShapeDtypeStruct(s.shape, s.dtype) for s in ss],
        in_specs=[hbm] * n, out_specs=[hbm] * n,
        scratch_shapes=[pltpu.SemaphoreType.DMA((n, 3)), pltpu.SemaphoreType.DMA((n, 3)), pltpu.SemaphoreType.DMA((n,))],
    )(*ss)


def _rows_cols(shape):
    r = 1
    for d in shape[:-1]:
        r *= d
    return r, shape[-1]


def _row_tile(r, cap, step):
    return next((t for t in range(cap, step - 1, -step) if r % t == 0), r)


def pair_add(name, h, recv, core):
    shape = recv.shape
    r, c = _rows_cols(shape[1:])
    tr = _row_tile(r, 256, 16)

    def body(core_ref, h_ref, r_ref, o_ref):
        o_ref[...] = (h_ref[...].astype(F32) + r_ref[...].astype(F32)).astype(o_ref.dtype)

    spec = pl.BlockSpec((None, tr, c), lambda q, i, core_ref: (q, i, 0))
    res = pl.pallas_call(
        body, name=name, out_shape=jax.ShapeDtypeStruct((4, r, c), h.dtype),
        grid_spec=pltpu.PrefetchScalarGridSpec(
            num_scalar_prefetch=1, grid=(4, r // tr),
            in_specs=[pl.BlockSpec((None, None, tr, c), lambda q, i, core_ref: (core_ref[0], q, i, 0)), spec],
            out_specs=spec),
        compiler_params=_params(("parallel", "parallel")),
    )(core, h.reshape(2, 4, r, c), recv.reshape(4, r, c))
    return res.reshape(shape)


def _sum_parts(p, n):
    t = [p[k].astype(F32) for k in range(n)]
    while len(t) > 1:
        t = [t[k] + t[k + 1] for k in range(0, len(t), 2)]
    return t[0]


def _adam(g, w, m, v):
    m = ADAM_B1 * m + (1.0 - ADAM_B1) * g
    v = ADAM_B2 * v + (1.0 - ADAM_B2) * (g * g)
    m_hat = m / (1.0 - ADAM_B1 ** ADAM_STEP)
    v_hat = v / (1.0 - ADAM_B2 ** ADAM_STEP)
    return -ADAM_LR * (m_hat / (jnp.sqrt(v_hat) + ADAM_EPS) + ADAM_WD * w), m, v


def adam_tiled(name, partials, w, m_, v_):
    shape = w.shape
    n_part = partials.shape[0]
    r, c = _rows_cols(shape)
    tr = _row_tile(r, 256, 16)

    def body(p_ref, w_ref, m_ref, v_ref, g_ref, d_ref, nm_ref, nv_ref):
        g = _sum_parts(p_ref, n_part)
        g_ref[...] = g
        d_ref[...], nm_ref[...], nv_ref[...] = _adam(g, w_ref[...], m_ref[...], v_ref[...])

    spec = pl.BlockSpec((tr, c), lambda i: (i, 0))
    res = pl.pallas_call(
        body, grid=(r // tr,), name=name,
        in_specs=[pl.BlockSpec((n_part, tr, c), lambda i: (0, i, 0)), spec, spec, spec],
        out_specs=[spec] * 4, out_shape=[jax.ShapeDtypeStruct((r, c), F32)] * 4,
        compiler_params=_params(("parallel",)),
    )(partials.reshape(n_part, r, c), w.reshape(r, c), m_.reshape(r, c), v_.reshape(r, c))
    return [t.reshape(shape) for t in res]


def adam_small(name, items, extra):
    n, ne = len(items), len(extra)

    def body(*refs):
        ins, outs = refs[:4 * n + ne], refs[4 * n + ne:]
        for a in range(n):
            p_ref, w_ref, m_ref, v_ref = ins[4 * a:4 * a + 4]
            g = _sum_parts(p_ref, N_DEV)
            outs[4 * a][...] = g
            outs[4 * a + 1][...], outs[4 * a + 2][...], outs[4 * a + 3][...] = _adam(g, w_ref[...], m_ref[...], v_ref[...])
        for e in range(ne):
            outs[4 * n + e][...] = _sum_parts(ins[4 * n + e], N_DEV)

    args, out_shape = [], []
    for p, w, m_, v_ in items:
        args += [p, w, m_, v_]
        out_shape += [jax.ShapeDtypeStruct(w.shape, F32)] * 4
    for e in extra:
        args.append(e)
        out_shape.append(jax.ShapeDtypeStruct(e.shape[1:], F32))
    vmem = pl.BlockSpec(memory_space=pltpu.VMEM)
    res = pl.pallas_call(body, name=name, in_specs=[vmem] * len(args), out_specs=[vmem] * len(out_shape), out_shape=out_shape)(*args)
    return [res[4 * a:4 * a + 4] for a in range(n)], res[4 * n:]


def _cols_from_gather(g):
    g = jnp.moveaxis(g, 0, -2)
    return g.reshape(g.shape[:-2] + (g.shape[-2] * g.shape[-1],))


def _cols_to_blocks(w):
    w = w.reshape(w.shape[:-1] + (N_DEV, w.shape[-1] // N_DEV))
    return jnp.moveaxis(w, -2, 0)


def _block_diag(w):
    z = jnp.zeros((B_BLOCK_DIM, B_BLOCK_DIM), w.dtype)
    rows = []
    for j in range(B_BLOCKS // 2):
        top = jnp.concatenate([w[2 * j], z], axis=1)
        bot = jnp.concatenate([z, w[2 * j + 1]], axis=1)
        rows.append(jnp.concatenate([top, bot], axis=0))
    return jnp.concatenate(rows, axis=0)


def _block_diag_grad(d):
    out = []
    for j in range(B_BLOCKS // 2):
        blk = d[128 * j:128 * (j + 1)]
        out.append(blk[:64, :64])
        out.append(blk[64:, 64:])
    return jnp.stack(out)


NAMES = ("norm_gains", "even_w_in", "hgrn_lb_logits", "hgrn_norm", "rg_conv_w", "rg_conv_b", "rg_wa", "rg_ba", "rg_wx", "rg_bx",
         "rg_lambda", "even_w_out", "odd_w_in", "fox_f_bias", "odd_w_out", "ffn_w_up", "ffn_conv_w", "ffn_conv_b", "ffn_w_down")
BIG = ("even_w_in", "even_w_out", "odd_w_in", "odd_w_out", "ffn_w_up", "ffn_w_down")
SMALL_SHARDED = ("norm_gains", "rg_conv_w", "ffn_conv_w")
REPLICATED = ("hgrn_lb_logits", "hgrn_norm", "rg_conv_b", "rg_wa", "rg_ba", "rg_wx", "rg_bx", "rg_lambda", "fox_f_bias", "ffn_conv_b")


def _ffn_forward(tag, layer, h, w_up_g, cw5, cb5, w_down_g, m, seq):
    tm = _div_tile(m, 1024)
    nm = m // tm
    hid = mm(f"{tag}_up", "nn",
             Blk(h, (tm, D_MODEL), lambda i, j, k: (i, 0)),
             Blk(w_up_g, (None, None, D_MODEL, FF_BLK), lambda i, j, k: (j, 0, 0, 0)),
             Blk((N_DEV, m, FF_BLK), (None, tm, FF_BLK), lambda i, j, k: (j, i, 0)), F32, (nm, N_DEV, 1))
    hid = hid.reshape(2, N_DEV // 2, m, FF_BLK)
    act = ffn_mid_fwd(f"{tag}_mid", hid, cw5, cb5, layer, m=m, seq=seq)
    f = mm(f"{tag}_down", "nn",
           Blk(act, (None, tm, FF_BLK), lambda i, j, k: (k, i, 0)),
           Blk(w_down_g, (2, None, FF_BLK // 2, D_MODEL), lambda i, j, k: (k, 0, 0, 0)),
           Blk((m, D_MODEL), (tm, D_MODEL), lambda i, j, k: (i, 0)), F32, (nm, 1, N_DEV // 2))
    return hid, act, f


def _ffn_backward(tag, layer, df, h, hid, act, w_up_g, cw5, cb5, w_down_g, m, seq):
    tm = _div_tile(m, 1024)
    nm = m // tm
    dact = mm(f"{tag}_dact", "nt",
              Blk(df, (tm, D_MODEL), lambda i, j, k: (i, 0)),
              Blk(w_down_g, (2, None, FF_BLK // 2, D_MODEL), lambda i, j, k: (j, 0, 0, 0)),
              Blk((N_DEV // 2, m, FF_BLK), (None, tm, FF_BLK), lambda i, j, k: (j, i, 0)), BF16, (nm, N_DEV // 2, 1))
    d_wdown = mm(f"{tag}_dwdown", "tn",
                 Blk(act, (None, tm, FF_BLK), lambda i, j, k: (i, k, 0)),
                 Blk(df, (tm, D_MODEL), lambda i, j, k: (k, 0)),
                 Blk(w_down_g.shape, (2, None, FF_BLK // 2, D_MODEL), lambda i, j, k: (i, 0, 0, 0)), BF16,
                 (N_DEV // 2, 1, nm))
    dhid, d_cw, d_cb = ffn_mid_bwd(f"{tag}_dmid", hid, cw5, cb5, dact, layer, m=m, seq=seq)
    dhid = dhid.reshape(N_DEV, m, FF_BLK)
    dh = mm(f"{tag}_dh", "nt",
            Blk(dhid, (None, tm, FF_BLK), lambda i, j, k: (k, i, 0)),
            Blk(w_up_g, (None, None, D_MODEL, FF_BLK), lambda i, j, k: (k, 0, 0, 0)),
            Blk((m, D_MODEL), (tm, D_MODEL), lambda i, j, k: (i, 0)), BF16, (nm, 1, N_DEV))
    d_wup = mm(f"{tag}_dwup", "tn",
               Blk(h, (tm, D_MODEL), lambda i, j, k: (k, 0)),
               Blk(dhid, (None, tm, FF_BLK), lambda i, j, k: (j, k, 0)),
               Blk(w_up_g.shape, (None, None, D_MODEL, FF_BLK), lambda i, j, k: (j, 0, 0, 0)), BF16,
               (1, N_DEV, nm))
    return dh, d_wup, d_cw, d_cb, d_wdown


def kernel(x, norm_gains, even_w_in, hgrn_lb_logits, hgrn_norm, rg_conv_w, rg_conv_b, rg_wa, rg_ba, rg_wx, rg_bx, rg_lambda, even_w_out, odd_w_in, fox_f_bias, odd_w_out, ffn_w_up, ffn_conv_w, ffn_conv_b, ffn_w_down, loss_target, m_norm_gains, m_even_w_in, m_hgrn_lb_logits, m_hgrn_norm, m_rg_conv_w, m_rg_conv_b, m_rg_wa, m_rg_ba, m_rg_wx, m_rg_bx, m_rg_lambda, m_even_w_out, m_odd_w_in, m_fox_f_bias, m_odd_w_out, m_ffn_w_up, m_ffn_conv_w, m_ffn_conv_b, m_ffn_w_down, v_norm_gains, v_even_w_in, v_hgrn_lb_logits, v_hgrn_norm, v_rg_conv_w, v_rg_conv_b, v_rg_wa, v_rg_ba, v_rg_wx, v_rg_bx, v_rg_lambda, v_even_w_out, v_odd_w_in, v_fox_f_bias, v_odd_w_out, v_ffn_w_up, v_ffn_conv_w, v_ffn_conv_b, v_ffn_w_down):
    local = dict(locals())
    w = {n: local[n] for n in NAMES}
    mom = {n: local["m_" + n] for n in NAMES}
    var = {n: local["v_" + n] for n in NAMES}
    n_batch, seq, _ = x.shape
    m = n_batch * seq
    tm = _div_tile(m, 512)
    tmm = _div_tile(m, 1024)
    nm = m // tmm

    now = [w["even_w_in"], w["even_w_out"], w["ffn_w_up"][0:1], w["ffn_w_down"][0:1]]
    gathered = all_gather("gather_weights", [t.astype(BF16) for t in now] + [w[n] for n in SMALL_SHARDED])
    g = dict(zip(("even_w_in", "even_w_out", "ffn_w_up", "ffn_w_down") + SMALL_SHARDED, gathered))
    w_in_e = g["even_w_in"]
    w_out_e = g["even_w_out"].reshape(D_MODEL, D_MODEL)
    w_up_g0, w_down_g0 = g["ffn_w_up"], g["ffn_w_down"]
    gains = _cols_from_gather(g["norm_gains"])
    after_first = (g["norm_gains"][0, 0, 0, 0] * 0.0).astype(BF16)
    later = [w["odd_w_in"], w["odd_w_out"], w["ffn_w_up"][1:2], w["ffn_w_down"][1:2]]
    later = [t.astype(BF16) + after_first for t in later]
    me = 4 * lax.axis_index("x") + 2 * lax.axis_index("y") + lax.axis_index("c")
    lands = [lax.dynamic_update_index_in_dim(lax.empty((N_DEV,) + t.shape, BF16), t, me, 0) for t in later]
    send_sems, recv_sems, later, lands, started = gather_start("gather_later_start", later, lands)
    rg_cw = _cols_from_gather(g["rg_conv_w"])[0]
    n_layer = ffn_conv_w.shape[0]
    cw5 = g["ffn_conv_w"].reshape(2, N_DEV // 2, n_layer, FFN_CONV, FF_BLK)
    cb5 = ffn_conv_b.reshape(n_layer, 2, N_DEV // 2, 1, FF_BLK)
    gain = lambda l, k: gains[l, k:k + 1, :]
    wa_bd, wx_bd = _block_diag(rg_wa[0]), _block_diag(rg_wx[0])
    fbias = jnp.pad(fox_f_bias, ((0, 0), (0, LANES - C_HEADS)))

    x0 = x.reshape(m, D_MODEL)
    tgt = loss_target.reshape(m, D_MODEL)

    (h0,) = tile_fwd("l0_prenorm", fn_prenorm_after, m=m, tm=tm, nj=1, rows=[Row(x0)], pars=[Par(gain(0, 0)), Par(started)],
                     outs=[Out(D_MODEL, BF16)])
    z0 = mm("l0_in", "nn",
            Blk(h0, (tmm, D_MODEL), lambda i, j, k: (i, 0)),
            Blk(w_in_e, (None, None, D_MODEL, 384), lambda i, j, k: (j, 0, 0, 0)),
            Blk((m, 3072), (tmm, 384), lambda i, j, k: (i, j)), F32, (nm, N_DEV, 1))
    oa, sprev = hgrn_fwd("l0_hgrn", z0, hgrn_lb_logits, hgrn_norm, n_batch=n_batch, seq=seq)
    rg_rows = lambda: [Row(z0, LANES, 16), Row(z0, LANES, 20)]
    rg_pars = lambda: [Par(rg_cw, "col", LANES), Par(rg_conv_b, "col", LANES), Par(wa_bd, "row", LANES), Par(rg_ba, "col", LANES),
                       Par(wx_bd, "row", LANES), Par(rg_bx, "col", LANES), Par(rg_lambda, "col", LANES)]
    (ob,) = tile_fwd("l0_rglru", fn_rglru, m=m, tm=seq, nj=B_WIDTH // LANES, rows=rg_rows(), pars=rg_pars(),
                     outs=[Out(B_WIDTH, BF16, LANES)])
    mixcat0 = jnp.concatenate([oa, ob], axis=-1)
    mix0 = mm2d("l0_out", "nn", mixcat0, w_out_e)
    x1, h1 = tile_fwd("l0_postnorm", fn_addnorm2, m=m, tm=tm, nj=1, rows=[Row(x0), Row(mix0)], pars=[Par(gain(0, 1)), Par(gain(0, 2))],
                      outs=[Out(D_MODEL, F32), Out(D_MODEL, BF16)])
    hid0, act0, f0 = _ffn_forward("l0_ffn", 0, h1, w_up_g0, cw5, cb5, w_down_g0, m, seq)
    x2, h2 = tile_fwd("l0_ffnnorm", fn_addnorm2, m=m, tm=tm, nj=1, rows=[Row(x1), Row(f0)], pars=[Par(gain(0, 3)), Par(gain(1, 0))],
                      outs=[Out(D_MODEL, F32), Out(D_MODEL, BF16)])

    g_in_o, g_out_o, w_up_g1, w_down_g1 = gather_wait("gather_later_wait", send_sems, recv_sems, later, lands, h2)
    w_in_o = jnp.pad(_cols_from_gather(g_in_o)[0], ((0, 0), (0, 3200 - 3088)))
    w_out_o = g_out_o.reshape(D_MODEL, D_MODEL)
    z1 = mm2d("l1_in", "nn", h2, w_in_o)
    (cgate,) = tile_fwd("l1_gate", fn_fox_gate, m=m, tm=seq, nj=1, rows=[Row(z1, LANES, 3072 // LANES)], pars=[Par(fbias)],
                        outs=[Out(LANES, F32)])
    place, ones_q, ones_k = term_placement()
    qterm, kterm = tile_fwd("l1_terms", fn_fox_terms, m=m, tm=tm, nj=1, rows=[Row(cgate)],
                            pars=[Par(place), Par(ones_q), Par(ones_k)], outs=[Out(TERM_W, BF16), Out(TERM_W, BF16)])
    oc, lse = fox_pair_fwd("l1_attn", z1, qterm, kterm, n_batch=n_batch, seq=seq)
    mix1 = mm2d("l1_out", "nn", oc, w_out_o)
    x3, h3 = tile_fwd("l1_postnorm", fn_addnorm2, m=m, tm=tm, nj=1, rows=[Row(x2), Row(mix1)], pars=[Par(gain(1, 1)), Par(gain(1, 2))],
                      outs=[Out(D_MODEL, F32), Out(D_MODEL, BF16)])
    hid1, act1, f1 = _ffn_forward("l1_ffn", 1, h3, w_up_g1, cw5, cb5, w_down_g1, m, seq)
    dy, loss_part = tile_fwd("loss", fn_final, m=m, tm=tm, nj=1, rows=[Row(x3), Row(f1), Row(tgt)], pars=[Par(gain(1, 3))],
                             outs=[Out(D_MODEL, F32)], n_acc=1)

    df1, d_g13 = tile_bwd("l1_dffnnorm", fn_rms_only, m=m, tm=tm, nj=1, rows=[Row(f1)], pars=[Par(gain(1, 3))], cts=[Row(dy)],
                          drows=[Out(D_MODEL, BF16)])
    dh3, d_wup1, d_cw1, d_cb1, d_wdown1 = _ffn_backward("l1_ffn", 1, df1, h3, hid1, act1, w_up_g1, cw5, cb5, w_down_g1, m, seq)
    dx2, dmix1, d_g11, d_g12 = tile_bwd("l1_dpostnorm", fn_addnorm2, m=m, tm=tm, nj=1, rows=[Row(x2), Row(mix1)],
                                        pars=[Par(gain(1, 1)), Par(gain(1, 2))], cts=[Row(dy), Row(dh3)],
                                        drows=[Out(D_MODEL, F32), Out(D_MODEL, BF16)])
    doc = mm2d("l1_doc", "nt", dmix1, w_out_o, BF16)
    d_wout_o = mm2d("l1_dwout", "tn", oc, dmix1)
    dq, dk, dv, dc = fox_pair_bwd("l1_dattn", z1, qterm, kterm, oc, doc, lse, n_batch=n_batch, seq=seq)
    dzf, d_fbias = tile_bwd("l1_dgate", fn_fox_gate, m=m, tm=seq, nj=1, rows=[Row(z1, LANES, 3072 // LANES)], pars=[Par(fbias)],
                            cts=[Row(dc)], drows=[Out(LANES, BF16)])
    dz1 = jnp.concatenate([dq, dk, dv, dzf], axis=-1)
    dh2 = mm2d("l1_dh", "nt", dz1, w_in_o, BF16)
    d_win_o = mm2d("l1_dwin", "tn", h2, dz1)

    send1 = [_cols_to_blocks(d_win_o[None, :, :3088]).astype(BF16),
             d_wout_o.reshape(N_DEV, 1, D_MODEL // N_DEV, D_MODEL).astype(BF16), d_wup1, d_wdown1]
    sent1 = _split_exchange("exchange_l1_start", send1, [own_slot_only(t, me) for t in send1], None, None)

    dx1, df0, d_g03, d_g10 = tile_bwd("l0_dffnnorm", fn_addnorm2_after, m=m, tm=tm, nj=1, rows=[Row(x1), Row(f0)],
                                      pars=[Par(gain(0, 3)), Par(gain(1, 0)), Par(sent1[4])], cts=[Row(dx2), Row(dh2)],
                                      drows=[Out(D_MODEL, F32), Out(D_MODEL, BF16)])[:4]
    dh1, d_wup0, d_cw0, d_cb0, d_wdown0 = _ffn_backward("l0_ffn", 0, df0, h1, hid0, act0, w_up_g0, cw5, cb5, w_down_g0, m, seq)
    send0 = [d_wup0, d_wdown0]
    sent0 = _split_exchange("exchange_ffn0_start", send0, [own_slot_only(t, me) for t in send0], None, None)
    dx0a, dmix0, d_g01, d_g02 = tile_bwd("l0_dpostnorm", fn_addnorm2_after, m=m, tm=tm, nj=1, rows=[Row(x0), Row(mix0)],
                                         pars=[Par(gain(0, 1)), Par(gain(0, 2)), Par(sent0[4])], cts=[Row(dx1), Row(dh1)],
                                         drows=[Out(D_MODEL, F32), Out(D_MODEL, BF16)])[:4]
    dmixcat0 = mm2d("l0_dmixcat", "nt", dmix0, w_out_e, BF16)
    d_wout_e = mm2d("l0_dwout", "tn", mixcat0, dmix0)
    dzq, dzf0, dzv, dzg, d_lb, d_hnorm = hgrn_bwd("l0_dhgrn", z0, sprev, hgrn_lb_logits, hgrn_norm, dmixcat0, n_batch=n_batch, seq=seq)
    dzx, dzy, d_rcw, d_rcb, d_wa, d_ba, d_wx, d_bx, d_lam = tile_bwd(
        "l0_drglru", fn_rglru, m=m, tm=seq, nj=B_WIDTH // LANES, rows=rg_rows(), pars=rg_pars(),
        cts=[Row(dmixcat0, LANES, A_WIDTH // LANES)], drows=[Out(B_WIDTH, BF16, LANES), Out(B_WIDTH, BF16, LANES)])
    dz0 = jnp.concatenate([dzq, dzf0, dzv, dzg, dzx, dzy], axis=-1)
    dh0 = mm("l0_dh", "nt",
             Blk(dz0, (tmm, 384), lambda i, j, k: (i, k)),
             Blk(w_in_e, (None, None, D_MODEL, 384), lambda i, j, k: (k, 0, 0, 0)),
             Blk((m, D_MODEL), (tmm, D_MODEL), lambda i, j, k: (i, 0)), BF16, (nm, 1, N_DEV))
    d_win_e = mm("l0_dwin", "tn",
                 Blk(h0, (tmm, D_MODEL), lambda i, j, k: (k, 0)),
                 Blk(dz0, (tmm, 384), lambda i, j, k: (k, j)),
                 Blk((2, 4) + w_in_e.shape[1:], (None, None, None, D_MODEL, 384), lambda i, j, k: (j % 2, j // 2, 0, 0, 0)), BF16,
                 (1, N_DEV, nm))
    dx0, d_g00 = tile_bwd("l0_dprenorm", fn_input_norm, m=m, tm=tm, nj=1, rows=[Row(x0)], pars=[Par(gain(0, 0))],
                          cts=[Row(dx0a), Row(dh0)], drows=[Out(D_MODEL, F32)])

    d_gains = jnp.stack([jnp.concatenate([d_g00, d_g01, d_g02, d_g03], axis=0), jnp.concatenate([d_g10, d_g11, d_g12, d_g13], axis=0)])
    d_ffn_cw = jnp.stack([d_cw0, d_cw1], axis=2).reshape(N_DEV, n_layer, FFN_CONV, FF_BLK)
    by_core = lambda t: jnp.swapaxes(t.reshape((4, 2) + t.shape[1:]), 0, 1).astype(BF16)
    last = ("even_w_in", "even_w_out")
    half = {"even_w_in": d_win_e, "even_w_out": by_core(d_wout_e.reshape(N_DEV, 1, D_MODEL // N_DEV, D_MODEL))}
    core = lax.axis_index("c").astype(jnp.int32).reshape(1)
    from_sibling = pair_exchange("exchange_core", [half[n] for n in last])
    chip_sums = [pair_add("add_" + n, half[n], r, core) for n, r in zip(last, from_sibling)]
    recv = dict(zip(last, quad_exchange("exchange_chips", chip_sums)))
    res = {n: adam_tiled("adam_" + n, recv[n], w[n], mom[n], var[n]) for n in last}
    r_in_o, r_out_o, r_up1, r_down1 = _split_exchange("exchange_l1_wait", sent1[2], sent1[3], sent1[:2], dx0)
    r_up0, r_down0 = _split_exchange("exchange_ffn0_wait", sent0[2], sent0[3], sent0[:2], dx0)
    res["odd_w_in"] = adam_tiled("adam_odd_w_in", r_in_o, w["odd_w_in"], mom["odd_w_in"], var["odd_w_in"])
    res["odd_w_out"] = adam_tiled("adam_odd_w_out", r_out_o, w["odd_w_out"], mom["odd_w_out"], var["odd_w_out"])
    for n, parts_l in (("ffn_w_up", (r_up0, r_up1)), ("ffn_w_down", (r_down0, r_down1))):
        per_layer = [adam_tiled(f"adam_{n}_{l}", p, w[n][l:l + 1], mom[n][l:l + 1], var[n][l:l + 1]) for l, p in enumerate(parts_l)]
        res[n] = [jnp.concatenate([per_layer[0][k], per_layer[1][k]], axis=0) for k in range(4)]
    small_send = [_cols_to_blocks(d_gains), _cols_to_blocks(d_rcw[None]), d_ffn_cw]
    recv.update(zip(SMALL_SHARDED, all_to_all("exchange_small", small_send)))

    d_ffn_cb = jnp.stack([d_cb0, d_cb1]).reshape(n_layer, 2 * D_FF)
    rep = {"hgrn_lb_logits": d_lb, "hgrn_norm": d_hnorm, "rg_conv_b": d_rcb, "rg_wa": _block_diag_grad(d_wa)[None], "rg_ba": d_ba,
           "rg_wx": _block_diag_grad(d_wx)[None], "rg_bx": d_bx, "rg_lambda": d_lam, "fox_f_bias": d_fbias[:, :C_HEADS],
           "ffn_conv_b": d_ffn_cb}
    parts = all_gather("gather_partials", [rep[n] for n in REPLICATED] + [loss_part])
    for n, p in zip(REPLICATED, parts):
        recv[n] = p
    small = SMALL_SHARDED + REPLICATED
    small_res, (loss_sum,) = adam_small("adam_small", [(recv[n], w[n], mom[n], var[n]) for n in small], [parts[-1]])
    res.update(dict(zip(small, small_res)))

    out = [loss_sum[0, 0], dx0.reshape(x.shape)]
    for k in range(4):
        out += [res[n][k] for n in NAMES]
    return tuple(out)
```

```python
import functools

import jax
import jax.numpy as jnp
from jax import lax
from jax.experimental import pallas as pl
from jax.experimental.pallas import tpu as pltpu

F32 = jnp.float32
BF16 = jnp.bfloat16

D_MODEL = 1024
A_HEADS = 4
A_WIDTH = 512
HGRN_CHUNK = 64
HGRN_SEG = 512
B_WIDTH = 512
B_BLOCKS = 8
B_BLOCK_DIM = 64
B_CONV = 4
RG_C = 8.0
C_HEADS = 16
C_HEAD_DIM = 64
D_FF = 2816
FFN_CONV = 3
EPS = 1e-6
LANES = 128
HALO = 16
N_DEV = 8
FF_BLK = 2 * D_FF // N_DEV
MESH = pl.DeviceIdType.MESH
NEG = -1e30
VMEM_LIMIT = 56 * 1024 * 1024

ADAM_LR = 0.001
ADAM_B1 = 0.9
ADAM_B2 = 0.999
ADAM_EPS = 1e-08
ADAM_WD = 0.01
ADAM_STEP = 10


def _dg(a, b, pat):
    nb = a.ndim - 2
    batch = (tuple(range(nb)), tuple(range(nb)))
    ca = a.ndim - 1 if pat[0] == "n" else a.ndim - 2
    cb = b.ndim - 2 if pat[1] == "n" else b.ndim - 1
    return lax.dot_general(a.astype(BF16), b.astype(BF16), (((ca,), (cb,)), batch), preferred_element_type=F32)


@functools.partial(jax.custom_vjp, nondiff_argnums=(2,))
def bdot(a, b, pat):
    return _dg(a, b, pat)


def _bdot_fwd(a, b, pat):
    return _dg(a, b, pat), (a, b)


def _bdot_bwd(pat, res, g):
    a, b = res
    if pat == "nn":
        return _dg(g, b, "nt"), _dg(a, g, "tn")
    if pat == "nt":
        return _dg(g, b, "nn"), _dg(g, a, "tn")
    return _dg(b, g, "nt"), _dg(a, g, "nn")


bdot.defvjp(_bdot_fwd, _bdot_bwd)


def _shift_raw(x, s, up, fill):
    if s == 0:
        return x
    n = x.shape[0]
    r = pltpu.roll(x, (n - s) if up else s, 0)
    idx = lax.broadcasted_iota(jnp.int32, x.shape, 0)
    mask = (idx >= n - s) if up else (idx < s)
    return jnp.where(mask, jnp.asarray(fill, x.dtype), r)


@functools.partial(jax.custom_vjp, nondiff_argnums=(1,))
def shift_down(x, s):
    return _shift_raw(x, s, False, 0.0)


def _shift_down_fwd(x, s):
    return _shift_raw(x, s, False, 0.0), None


def _shift_down_bwd(s, _, g):
    return (_shift_raw(g, s, True, 0.0),)


shift_down.defvjp(_shift_down_fwd, _shift_down_bwd)


def _scan_impl(a, u, up):
    n = a.shape[0]
    s = 1
    while s < n:
        u = a * _shift_raw(u, s, up, 0.0) + u
        if 2 * s < n:
            a = a * _shift_raw(a, s, up, 1.0)
        s *= 2
    return u


@jax.custom_vjp
def lin_scan(a, u):
    return _scan_impl(a, u, False)


def _lin_scan_fwd(a, u):
    h = _scan_impl(a, u, False)
    return h, (a, h)


def _lin_scan_bwd(res, g):
    a, h = res
    gh = _scan_impl(_shift_raw(a, 1, True, 0.0), g, True)
    return gh * _shift_raw(h, 1, False, 0.0), gh


lin_scan.defvjp(_lin_scan_fwd, _lin_scan_bwd)


def _cumsum_impl(x, up, period):
    n = x.shape[0]
    span = n if period is None else period
    idx = lax.broadcasted_iota(jnp.int32, x.shape, 0)
    pos = idx if period is None else idx % period
    s = 1
    while s < span:
        sh = _shift_raw(x, s, up, 0.0)
        if period is not None:
            keep = (pos < period - s) if up else (pos >= s)
            sh = jnp.where(keep, sh, 0.0)
        x = x + sh
        s *= 2
    return x


@functools.partial(jax.custom_vjp, nondiff_argnums=(1,))
def cumsum_rows(x, period):
    return _cumsum_impl(x, False, period)


def _cumsum_fwd(x, period):
    return _cumsum_impl(x, False, period), None


def _cumsum_bwd(period, _, g):
    return (_cumsum_impl(g, True, period),)


cumsum_rows.defvjp(_cumsum_fwd, _cumsum_bwd)


def _sigmoid(x):
    return jax.nn.sigmoid(x)


def _expm1(x):
    return jnp.tanh(0.5 * x) * (jnp.exp(x) + 1.0)


def _softplus(x):
    return jnp.maximum(x, 0.0) + jnp.log(1.0 + jnp.exp(-jnp.abs(x)))


def _rms(x, g):
    return x * lax.rsqrt(jnp.mean(x * x, axis=-1, keepdims=True) + EPS) * g


def fn_prenorm(x, g):
    return (_rms(x, g).astype(BF16),)


def fn_prenorm_after(x, g, _token):
    return fn_prenorm(x, g)


def fn_addnorm2(x, y, g_post, g_pre):
    x1 = x + _rms(y, g_post)
    return x1, _rms(x1, g_pre).astype(BF16)


def fn_addnorm2_after(x, y, g_post, g_pre, _token):
    return fn_addnorm2(x, y, g_post, g_pre)


def fn_input_norm(x, g):
    return x, _rms(x, g).astype(BF16)


def fn_final(x, y, tgt, g_post):
    out = x + _rms(y, g_post)
    err = out - tgt
    dy = err * (1.0 / D_MODEL)
    loss = 0.5 * jnp.sum(jnp.mean(err * err, axis=-1, keepdims=True), axis=0, keepdims=True)
    return dy, jnp.broadcast_to(loss, (1, LANES))


def fn_rms_only(y, g):
    return (_rms(y, g),)


def _causal_conv(x, w, b, taps):
    c = b
    for k in range(taps):
        c = c + w[k:k + 1, :] * shift_down(x, taps - 1 - k)
    return c


def fn_rglru(xb, yb, cw, cb, wa, ba, wx, bx, lam):
    xf = _causal_conv(xb, cw, cb, B_CONV)
    r = _sigmoid(bdot(xf, wa, "nn") + ba)
    i = _sigmoid(bdot(xf, wx, "nn") + bx)
    log_a = -RG_C * r * _softplus(-lam)
    a = jnp.exp(log_a)
    u = jnp.sqrt(-_expm1(2.0 * log_a)) * (i * xf)
    h = lin_scan(a, u)
    return ((h * jax.nn.gelu(yb)).astype(BF16),)


def fn_fox_gate(zf, bias):
    return (cumsum_rows(jax.nn.log_sigmoid(zf + bias), None),)


def fn_hgrn_seg(q, fl, v, g, st, logits, hn):
    rows = q.shape[0]
    nc = rows // HGRN_CHUNK
    l0, l1, l2 = logits[0:1, :], logits[1:2, :], logits[2:3, :]
    mx = jnp.maximum(jnp.maximum(l0, l1), l2)
    e0, e1, e2 = jnp.exp(l0 - mx), jnp.exp(l1 - mx), jnp.exp(l2 - mx)
    lb = e0 / (e0 + e1 + e2)
    forget = lb + (1.0 - lb) * _sigmoid(fl)
    qs = q * _sigmoid(q)
    kk = 1.0 - forget
    logf = jnp.log(forget)
    bcum = cumsum_rows(logf, HGRN_CHUNK)
    c3 = lambda t: t.reshape(nc, HGRN_CHUNK, 128)
    b_last = jnp.sum(c3(logf), axis=1, keepdims=True)
    bcum3 = c3(bcum)
    q_dec = c3(qs) * jnp.exp(bcum3)
    k_dec = c3(kk) * jnp.exp(-bcum3)
    k_upd = c3(kk) * jnp.exp(b_last - bcum3)
    v3 = c3(v)
    scores = bdot(q_dec, k_dec, "nt")
    ri = lax.broadcasted_iota(jnp.int32, scores.shape, 1)
    ci = lax.broadcasted_iota(jnp.int32, scores.shape, 2)
    scores = jnp.where(ri >= ci, scores, 0.0)
    o = bdot(scores, v3, "nn")
    upd_t = bdot(v3, k_upd, "tn")
    dec = jnp.exp(b_last)
    prev = []
    for n in range(nc):
        prev.append(st)
        st = st * dec[n] + upd_t[n]
    o = o + bdot(q_dec, jnp.stack(prev), "nt")
    o = o.reshape(rows, 128)
    o = o * lax.rsqrt(jnp.mean(o * o, axis=-1, keepdims=True) + EPS) * hn
    return (o * _sigmoid(g)).astype(BF16), st


def _ffn_conv(xg, xv, cw, cb):
    cg = _causal_conv(xg, cw[0], cb[0], FFN_CONV)[HALO:]
    cv = _causal_conv(xv, cw[1], cb[1], FFN_CONV)[HALO:]
    return cg, cv


def _ffn_gate(cg, cv):
    return jax.nn.gelu(cg) * cv


class Row:
    def __init__(self, arr, cb=None, off=0):
        self.arr, self.cb, self.off = arr, cb, off

    def spec(self, tm):
        if self.cb is None:
            return pl.BlockSpec((tm, self.arr.shape[1]), lambda j, i: (i, 0))
        off = self.off
        return pl.BlockSpec((tm, self.cb), lambda j, i: (i, j + off))


class Par:
    def __init__(self, arr, kind="full", bs=None):
        self.arr, self.kind, self.bs = arr, kind, bs

    def block(self):
        if self.kind == "full":
            return self.arr.shape
        if self.kind == "col":
            return (self.arr.shape[0], self.bs)
        return (self.bs, self.arr.shape[1])

    def spec(self):
        if self.kind == "full":
            return pl.BlockSpec(self.block(), lambda j, i: (0, 0))
        if self.kind == "col":
            return pl.BlockSpec(self.block(), lambda j, i: (0, j))
        return pl.BlockSpec(self.block(), lambda j, i: (j, 0))


class Out:
    def __init__(self, width, dtype, cb=None, off=0):
        self.width, self.dtype, self.cb, self.off = width, dtype, cb, off

    def spec(self, tm):
        if self.cb is None:
            return pl.BlockSpec((tm, self.width), lambda j, i: (i, 0))
        off = self.off
        return pl.BlockSpec((tm, self.cb), lambda j, i: (i, j + off))


def _params(sem):
    return pltpu.CompilerParams(dimension_semantics=sem, vmem_limit_bytes=VMEM_LIMIT)


def tile_fwd(name, fn, *, m, tm, nj, rows, pars, outs, n_acc=0):
    n_r, n_p, n_o = len(rows), len(pars), len(outs)

    def body(*refs):
        ins = [r[...] for r in refs[:n_r + n_p]]
        res = fn(*ins)
        o_refs = refs[n_r + n_p:]
        for k in range(n_o):
            o_refs[k][...] = res[k].astype(o_refs[k].dtype)
        first = jnp.logical_and(pl.program_id(0) == 0, pl.program_id(1) == 0)
        for k in range(n_acc):
            ref = o_refs[n_o + k]

            @pl.when(first)
            def _():
                ref[...] = jnp.zeros_like(ref)

            ref[...] += res[n_o + k]

    out_shape = [jax.ShapeDtypeStruct((m, o.width), o.dtype) for o in outs]
    out_specs = [o.spec(tm) for o in outs]
    for _ in range(n_acc):
        out_shape.append(jax.ShapeDtypeStruct((1, LANES), F32))
        out_specs.append(pl.BlockSpec((1, LANES), lambda j, i: (0, 0)))
    sem = ("arbitrary", "arbitrary") if n_acc else ("parallel", "parallel")
    return pl.pallas_call(
        body, grid=(nj, m // tm), name=name,
        in_specs=[r.spec(tm) for r in rows] + [p.spec() for p in pars],
        out_specs=out_specs, out_shape=out_shape, compiler_params=_params(sem),
    )(*[r.arr for r in rows], *[p.arr for p in pars])


def tile_bwd(name, fn, *, m, tm, nj, rows, pars, cts, drows):
    n_r, n_p, n_c = len(rows), len(pars), len(cts)
    want = [k for k in range(n_r) if drows[k] is not None]

    def body(*refs):
        ins = [r[...] for r in refs[:n_r + n_p]]
        ct = [r[...] for r in refs[n_r + n_p:n_r + n_p + n_c]]
        o_refs = refs[n_r + n_p + n_c:]
        res, vjp = jax.vjp(fn, *ins)
        grads = vjp(tuple(c.astype(r.dtype) for c, r in zip(ct, res)))
        for pos, k in enumerate(want):
            o_refs[pos][...] = grads[k].astype(o_refs[pos].dtype)
        for k in range(n_p):
            ref = o_refs[len(want) + k]
            first = pl.program_id(1) == 0
            if pars[k].kind == "full":
                first = jnp.logical_and(first, pl.program_id(0) == 0)

            @pl.when(first)
            def _():
                ref[...] = jnp.zeros_like(ref)

            ref[...] += grads[n_r + k].astype(F32)

    out_shape = [jax.ShapeDtypeStruct((m, drows[k].width), drows[k].dtype) for k in want]
    out_specs = [drows[k].spec(tm) for k in want]
    for p in pars:
        out_shape.append(jax.ShapeDtypeStruct(p.arr.shape, F32))
        out_specs.append(p.spec())
    return pl.pallas_call(
        body, grid=(nj, m // tm), name=name,
        in_specs=[r.spec(tm) for r in rows] + [p.spec() for p in pars] + [c.spec(tm) for c in cts],
        out_specs=out_specs, out_shape=out_shape, compiler_params=_params(("arbitrary", "arbitrary")),
    )(*[r.arr for r in rows], *[p.arr for p in pars], *[c.arr for c in cts])


class Blk:
    def __init__(self, arr, block, index):
        self.arr, self.block, self.index = arr, block, index

    def spec(self):
        return pl.BlockSpec(self.block, self.index)


def _flat2(v):
    return v if v.ndim == 2 else v.reshape(-1, v.shape[-1])


def mm(name, pat, a, b, o, out_dtype, grid, after=None):
    nk = grid[2]
    o_shape = o.arr

    def body(*refs):
        a_ref, b_ref = refs[0], refs[1]
        o_ref = refs[3] if after is not None else refs[2]
        r = _dg(_flat2(a_ref[...]), _flat2(b_ref[...]), pat)
        if nk == 1:
            o_ref[...] = r.astype(out_dtype).reshape(o_ref.shape)
            return
        acc_ref = refs[-1]
        kk = pl.program_id(2)

        @pl.when(kk == 0)
        def _():
            acc_ref[...] = r

        @pl.when(kk > 0)
        def _():
            acc_ref[...] += r

        @pl.when(kk == nk - 1)
        def _():
            o_ref[...] = acc_ref[...].astype(out_dtype).reshape(o_ref.shape)

    ob = [d for d in o.block if d is not None]
    acc_shape = (ob[0], ob[1]) if len(ob) == 2 else (ob[0] * ob[1], ob[2])
    in_specs = [a.spec(), b.spec()]
    args = [a.arr, b.arr]
    if after is not None:
        in_specs.append(pl.BlockSpec(memory_space=pl.ANY))
        args.append(after)
    return pl.pallas_call(
        body, grid=grid, name=name, in_specs=in_specs, out_specs=o.spec(),
        out_shape=jax.ShapeDtypeStruct(o_shape, out_dtype),
        scratch_shapes=[pltpu.VMEM(acc_shape, F32)] if nk > 1 else [],
        compiler_params=_params(("parallel", "parallel", "arbitrary")),
    )(*args)


def _div_tile(n, cap):
    if n <= cap:
        return n
    best = 128
    for t in range(128, cap + 1, 128):
        if n % t == 0:
            best = t
    return best


def mm2d(name, pat, a, b, out_dtype=F32):
    if pat == "tn":
        k, m = a.shape
    else:
        m, k = a.shape
    n = b.shape[0] if pat == "nt" else b.shape[1]
    tm, tn, tk = _div_tile(m, 1024), _div_tile(n, 1024), _div_tile(k, 1024)
    a_blk = Blk(a, (tk, tm), lambda i, j, kk: (kk, i)) if pat == "tn" else Blk(a, (tm, tk), lambda i, j, kk: (i, kk))
    b_blk = Blk(b, (tn, tk), lambda i, j, kk: (j, kk)) if pat == "nt" else Blk(b, (tk, tn), lambda i, j, kk: (kk, j))
    o_blk = Blk((m, n), (tm, tn), lambda i, j, kk: (i, j))
    return mm(name, pat, a_blk, b_blk, o_blk, out_dtype, (m // tm, n // tn, k // tk))


def hgrn_fwd(name, z, logits, hnorm, *, n_batch, seq):
    m = n_batch * seq
    ts = min(HGRN_SEG, seq)
    n_seg = seq // ts

    def body(q_ref, f_ref, v_ref, g_ref, lg_ref, hn_ref, o_ref, sp_ref, st_ref):
        s = pl.program_id(2)

        @pl.when(s == 0)
        def _():
            st_ref[...] = jnp.zeros_like(st_ref)

        st = st_ref[...]
        sp_ref[...] = st
        o, st_new = fn_hgrn_seg(q_ref[...], f_ref[...], v_ref[...], g_ref[...], st, lg_ref[...], hn_ref[...])
        o_ref[...] = o
        st_ref[...] = st_new

    part = lambda p: pl.BlockSpec((ts, 128), lambda h, b, s: (b * n_seg + s, 4 * p + h))
    return pl.pallas_call(
        body, grid=(A_HEADS, n_batch, n_seg), name=name,
        in_specs=[part(0), part(1), part(2), part(3),
                  pl.BlockSpec((3, 128), lambda h, b, s: (0, h)),
                  pl.BlockSpec((1, 128), lambda h, b, s: (0, h))],
        out_specs=[pl.BlockSpec((ts, 128), lambda h, b, s: (b * n_seg + s, h)),
                   pl.BlockSpec((128, 128), lambda h, b, s: ((b * n_seg + s) * A_HEADS + h, 0))],
        out_shape=[jax.ShapeDtypeStruct((m, A_WIDTH), BF16),
                   jax.ShapeDtypeStruct((n_batch * n_seg * A_HEADS * 128, 128), F32)],
        scratch_shapes=[pltpu.VMEM((128, 128), F32)],
        compiler_params=_params(("arbitrary", "arbitrary", "arbitrary")),
    )(z, z, z, z, logits, hnorm)


def hgrn_bwd(name, z, sprev, logits, hnorm, do, *, n_batch, seq):
    m = n_batch * seq
    ts = min(HGRN_SEG, seq)
    n_seg = seq // ts

    def body(q_ref, f_ref, v_ref, g_ref, sp_ref, lg_ref, hn_ref, do_ref, dq_ref, df_ref, dv_ref, dg_ref, dlg_ref, dhn_ref, dst_ref):
        s = pl.program_id(2)

        @pl.when(s == 0)
        def _():
            dst_ref[...] = jnp.zeros_like(dst_ref)

        res, vjp = jax.vjp(fn_hgrn_seg, q_ref[...], f_ref[...], v_ref[...], g_ref[...], sp_ref[...], lg_ref[...], hn_ref[...])
        dq, df, dv, dg, dst, dlg, dhn = vjp((do_ref[...].astype(res[0].dtype), dst_ref[...]))
        dq_ref[...] = dq.astype(dq_ref.dtype)
        df_ref[...] = df.astype(df_ref.dtype)
        dv_ref[...] = dv.astype(dv_ref.dtype)
        dg_ref[...] = dg.astype(dg_ref.dtype)
        dst_ref[...] = dst
        first = jnp.logical_and(pl.program_id(1) == 0, s == 0)

        @pl.when(first)
        def _():
            dlg_ref[...] = jnp.zeros_like(dlg_ref)
            dhn_ref[...] = jnp.zeros_like(dhn_ref)

        dlg_ref[...] += dlg
        dhn_ref[...] += dhn

    rev = lambda b, s: b * n_seg + (n_seg - 1 - s)
    part = lambda p: pl.BlockSpec((ts, 128), lambda h, b, s: (rev(b, s), 4 * p + h))
    head = pl.BlockSpec((ts, 128), lambda h, b, s: (rev(b, s), h))
    dpart = jax.ShapeDtypeStruct((m, A_WIDTH), BF16)
    return pl.pallas_call(
        body, grid=(A_HEADS, n_batch, n_seg), name=name,
        in_specs=[part(0), part(1), part(2), part(3),
                  pl.BlockSpec((128, 128), lambda h, b, s: (rev(b, s) * A_HEADS + h, 0)),
                  pl.BlockSpec((3, 128), lambda h, b, s: (0, h)),
                  pl.BlockSpec((1, 128), lambda h, b, s: (0, h)),
                  head],
        out_specs=[head, head, head, head,
                   pl.BlockSpec((3, 128), lambda h, b, s: (0, h)),
                   pl.BlockSpec((1, 128), lambda h, b, s: (0, h))],
        out_shape=[dpart, dpart, dpart, dpart,
                   jax.ShapeDtypeStruct(logits.shape, F32),
                   jax.ShapeDtypeStruct(hnorm.shape, F32)],
        scratch_shapes=[pltpu.VMEM((128, 128), F32)],
        compiler_params=_params(("arbitrary", "arbitrary", "arbitrary")),
    )(z, z, z, z, sprev, logits, hnorm, do)


def _ffn_tiles(m, seq):
    tm = min(512, seq)
    return tm, seq // tm, m // tm


def ffn_mid_fwd(name, hid, cw, cb, layer, *, m, seq):
    tm, n_t, n_i = _ffn_tiles(m, seq)
    hb = tm // HALO

    def body(x_ref, xb_ref, cw_ref, cb_ref, o_ref):
        first = pl.program_id(1) % n_t == 0
        before = jnp.where(first, 0.0, xb_ref[...])
        ext = jnp.concatenate([before, x_ref[...]], axis=1)
        cg, cv = _ffn_conv(ext[0], ext[1], cw_ref[...], cb_ref[...])
        o_ref[...] = _ffn_gate(cg, cv).astype(o_ref.dtype)

    return pl.pallas_call(
        body, grid=(N_DEV // 2, n_i), name=name,
        in_specs=[pl.BlockSpec((2, None, tm, FF_BLK), lambda d, i: (0, d, i, 0)),
                  pl.BlockSpec((2, None, HALO, FF_BLK), lambda d, i: (0, d, jnp.maximum(i * hb - 1, 0), 0)),
                  pl.BlockSpec((2, None, None, FFN_CONV, FF_BLK), lambda d, i: (0, d, layer, 0, 0)),
                  pl.BlockSpec((None, 2, None, 1, FF_BLK), lambda d, i: (layer, 0, d, 0, 0))],
        out_specs=pl.BlockSpec((None, tm, FF_BLK), lambda d, i: (d, i, 0)),
        out_shape=jax.ShapeDtypeStruct((N_DEV // 2, m, FF_BLK), BF16),
        compiler_params=_params(("parallel", "parallel")),
    )(hid, hid, cw, cb)


def ffn_mid_bwd(name, hid, cw, cb, dact, layer, *, m, seq):
    tm, n_t, n_i = _ffn_tiles(m, seq)
    hb = tm // HALO
    last_blk = m // HALO - 1

    def body(x_ref, xb_ref, xa_ref, cw_ref, cb_ref, da_ref, daa_ref, dx_ref, dcw_ref, dcb_ref):
        i = pl.program_id(1)
        first = i % n_t == 0
        last = i % n_t == n_t - 1
        before = jnp.where(first, 0.0, xb_ref[...])
        ext = jnp.concatenate([before, x_ref[...], xa_ref[...]], axis=1)
        dact_ext = jnp.concatenate([da_ref[...].astype(F32), jnp.where(last, 0.0, daa_ref[...].astype(F32))], axis=0)
        (cg, cv), vjp_conv = jax.vjp(_ffn_conv, ext[0], ext[1], cw_ref[...], cb_ref[...])
        _, vjp_gate = jax.vjp(_ffn_gate, cg, cv)
        dcg, dcv = vjp_gate(dact_ext)
        dxg, dxv, _, _ = vjp_conv((dcg, dcv))
        dx_ref[0] = dxg[HALO:HALO + tm].astype(dx_ref.dtype)
        dx_ref[1] = dxv[HALO:HALO + tm].astype(dx_ref.dtype)
        own = lax.broadcasted_iota(jnp.int32, dcg.shape, 0) < tm
        _, _, dcw, dcb = vjp_conv((jnp.where(own, dcg, 0.0), jnp.where(own, dcv, 0.0)))

        @pl.when(i == 0)
        def _():
            dcw_ref[...] = jnp.zeros_like(dcw_ref)
            dcb_ref[...] = jnp.zeros_like(dcb_ref)

        dcw_ref[...] += dcw
        dcb_ref[...] += dcb

    return pl.pallas_call(
        body, grid=(N_DEV // 2, n_i), name=name,
        in_specs=[pl.BlockSpec((2, None, tm, FF_BLK), lambda d, i: (0, d, i, 0)),
                  pl.BlockSpec((2, None, HALO, FF_BLK), lambda d, i: (0, d, jnp.maximum(i * hb - 1, 0), 0)),
                  pl.BlockSpec((2, None, HALO, FF_BLK), lambda d, i: (0, d, jnp.minimum((i + 1) * hb, last_blk), 0)),
                  pl.BlockSpec((2, None, None, FFN_CONV, FF_BLK), lambda d, i: (0, d, layer, 0, 0)),
                  pl.BlockSpec((None, 2, None, 1, FF_BLK), lambda d, i: (layer, 0, d, 0, 0)),
                  pl.BlockSpec((None, tm, FF_BLK), lambda d, i: (d, i, 0)),
                  pl.BlockSpec((None, HALO, FF_BLK), lambda d, i: (d, jnp.minimum((i + 1) * hb, last_blk), 0))],
        out_specs=[pl.BlockSpec((2, None, tm, FF_BLK), lambda d, i: (0, d, i, 0)),
                   pl.BlockSpec((2, None, FFN_CONV, FF_BLK), lambda d, i: (0, d, 0, 0)),
                   pl.BlockSpec((2, None, 1, FF_BLK), lambda d, i: (0, d, 0, 0))],
        out_shape=[jax.ShapeDtypeStruct((2, N_DEV // 2, m, FF_BLK), BF16),
                   jax.ShapeDtypeStruct((2, N_DEV // 2, FFN_CONV, FF_BLK), F32),
                   jax.ShapeDtypeStruct((2, N_DEV // 2, 1, FF_BLK), F32)],
        compiler_params=_params(("arbitrary", "arbitrary")),
    )(hid, hid, hid, cw, cb, dact, dact)


ATT_BLK = 512
N_PAIR = C_HEADS // 2
TERM_W = C_HEADS * LANES


def term_placement():
    import numpy as np
    place = np.zeros((6, LANES, TERM_W), np.float32)
    ones_q = np.zeros((1, TERM_W), np.float32)
    ones_k = np.zeros((1, TERM_W), np.float32)
    for h in range(C_HEADS):
        for j in range(3):
            place[j, h, h * LANES + C_HEAD_DIM + j] = 1.0
            place[3 + j, h, h * LANES + C_HEAD_DIM + 3 + j] = 1.0
            ones_q[0, h * LANES + C_HEAD_DIM + 3 + j] = 1.0
            ones_k[0, h * LANES + C_HEAD_DIM + j] = 1.0
    return (jnp.asarray(place.reshape(6 * LANES, TERM_W), BF16), jnp.asarray(ones_q, F32), jnp.asarray(ones_k, F32))


def fn_fox_terms(c, place, ones_q, ones_k):
    parts = _split3(c)
    qt = ones_q
    kt = ones_k
    for j in range(3):
        qt = qt + _dg(parts[j], place[j * LANES:(j + 1) * LANES], "nn")
        kt = kt - _dg(parts[j], place[(3 + j) * LANES:(4 + j) * LANES], "nn")
    return qt.astype(BF16), kt.astype(BF16)


def _head_tile(z, terms, e):
    lane = lax.broadcasted_iota(jnp.int32, z.shape, 1)
    base = z if e == 0 else pltpu.roll(z, C_HEAD_DIM, 1)
    return jnp.where(lane < C_HEAD_DIM, base, terms.astype(z.dtype))


def _head_only(z, e):
    lane = lax.broadcasted_iota(jnp.int32, z.shape, 1)
    mine = (lane < C_HEAD_DIM) if e == 0 else (lane >= C_HEAD_DIM)
    return jnp.where(mine, z, jnp.zeros_like(z)).astype(BF16)


def _pair_tile(a0, a1):
    lane = lax.broadcasted_iota(jnp.int32, a0.shape, 1)
    return jnp.where(lane < C_HEAD_DIM, a0, pltpu.roll(a1, C_HEAD_DIM, 1))


def _lane_col(a, k):
    lane = lax.broadcasted_iota(jnp.int32, a.shape, 1)
    return jnp.sum(jnp.where(lane == k, a, 0.0), axis=1, keepdims=True)


def _causal(s):
    key = lax.broadcasted_iota(jnp.int32, s.shape, 0)
    qry = lax.broadcasted_iota(jnp.int32, s.shape, 1)
    return qry >= key


def fox_pair_fwd(name, z, qterm, kterm, *, n_batch, seq):
    m = n_batch * seq
    blk = min(ATT_BLK, seq)
    nq = seq // blk
    dh = C_HEAD_DIM

    def body(zq_ref, zk_ref, zv_ref, qt_ref, kt_ref, o_ref, lse_ref, ka_ref, vt_ref):
        qi = pl.program_id(2)

        @pl.when(qi == 0)
        def _():
            zk = zk_ref[...]
            for e in range(2):
                ka_ref[e] = _head_tile(zk, kt_ref[:, e * LANES:(e + 1) * LANES], e).astype(BF16)
            for cb in range(nq):
                vt_ref[cb] = zv_ref[cb * blk:(cb + 1) * blk, :].T.astype(BF16)

        zq = zq_ref[...] * dh ** -0.5
        qa = [_head_tile(zq, qt_ref[:, e * LANES:(e + 1) * LANES], e).astype(BF16) for e in range(2)]

        def block(j, carry, diagonal):
            rows = pl.ds(pl.multiple_of(j * blk, blk), blk)
            out = []
            for e in range(2):
                mx, l, acc = carry[e]
                s = _dg(ka_ref[e, rows, :], qa[e], "nt")
                if diagonal:
                    s = jnp.where(_causal(s), s, NEG)
                mx_new = jnp.maximum(mx, jnp.max(s, axis=0, keepdims=True))
                p = jnp.exp(s - mx_new)
                alpha = jnp.exp(mx - mx_new)
                l = alpha * l + jnp.sum(p, axis=0, keepdims=True)
                acc = alpha * acc + _dg(vt_ref[j, e * dh:(e + 1) * dh, :], p, "nn")
                out.append((mx_new, l, acc))
            return tuple(out)

        one = (jnp.full((1, blk), NEG, F32), jnp.zeros((1, blk), F32), jnp.zeros((dh, blk), F32))
        carry = lax.fori_loop(0, qi, lambda j, cr: block(j, cr, False), (one, one))
        res = block(qi, carry, True)
        ot = jnp.concatenate([res[e][2] / res[e][1] for e in range(2)], axis=0)
        o_ref[...] = ot.T.astype(o_ref.dtype)
        for e in range(2):
            lse_ref[e] = res[e][0] + jnp.log(res[e][1])

    col = lambda part: (lambda b, g, i: (b, part * N_PAIR + g))
    return pl.pallas_call(
        body, grid=(n_batch, N_PAIR, nq), name=name,
        in_specs=[pl.BlockSpec((blk, LANES), lambda b, g, i: (b * nq + i, g)),
                  pl.BlockSpec((seq, LANES), col(1)),
                  pl.BlockSpec((seq, LANES), col(2)),
                  pl.BlockSpec((blk, 2 * LANES), lambda b, g, i: (b * nq + i, g)),
                  pl.BlockSpec((seq, 2 * LANES), lambda b, g, i: (b, g))],
        out_specs=[pl.BlockSpec((blk, LANES), lambda b, g, i: (b * nq + i, g)),
                   pl.BlockSpec((None, None, None, 2, 1, blk), lambda b, g, i: (b, g, i, 0, 0, 0))],
        out_shape=[jax.ShapeDtypeStruct((m, D_MODEL), BF16), jax.ShapeDtypeStruct((n_batch, N_PAIR, nq, 2, 1, blk), F32)],
        scratch_shapes=[pltpu.VMEM((2, seq, LANES), BF16), pltpu.VMEM((nq, LANES, blk), BF16)],
        compiler_params=_params(("parallel", "parallel", "arbitrary")),
    )(z, z, z, qterm, kterm)


def fox_pair_bwd(name, z, qterm, kterm, o, do, lse, *, n_batch, seq):
    m = n_batch * seq
    blk = min(ATT_BLK, seq)
    nq = seq // blk
    dh = C_HEAD_DIM

    def body(zq_ref, zk_ref, zv_ref, qt_ref, kt_ref, o_ref, do_ref, lse_ref, dq_ref, dk_ref, dv_ref, dc_ref,
             qa_ref, doh_ref, del_ref, dqt_ref, dk_acc, dv_acc):
        g, j = pl.program_id(1), pl.program_id(2)
        lane = lax.broadcasted_iota(jnp.int32, (blk, LANES), 1)

        @pl.when(jnp.logical_and(g == 0, j == 0))
        def _():
            dc_ref[...] = jnp.zeros_like(dc_ref)

        @pl.when(j == 0)
        def _():
            zq = zq_ref[...] * dh ** -0.5
            dov = do_ref[...]
            for e in range(2):
                qa_ref[e] = _head_tile(zq, qt_ref[:, e * LANES:(e + 1) * LANES], e).astype(BF16)
                doh_ref[e] = _head_only(dov, e)
            for cb in range(nq):
                rows = slice(cb * blk, (cb + 1) * blk)
                prod_t = (do_ref[rows, :].astype(F32) * o_ref[rows, :].astype(F32)).T
                for e in range(2):
                    del_ref[cb, e] = jnp.sum(prod_t[e * dh:(e + 1) * dh], axis=0, keepdims=True)
            dqt_ref[...] = jnp.zeros_like(dqt_ref)

        zk, zv = zk_ref[...], zv_ref[...]
        ka32 = [_head_tile(zk, kt_ref[:, e * LANES:(e + 1) * LANES], e) for e in range(2)]
        ka = [t.astype(BF16) for t in ka32]
        kat = [t.T.astype(BF16) for t in ka32]
        vh = [_head_only(zv, e) for e in range(2)]
        dk_acc[...] = jnp.zeros_like(dk_acc)
        dv_acc[...] = jnp.zeros_like(dv_acc)

        def block(i, diagonal):
            rows = pl.ds(pl.multiple_of(i * blk, blk), blk)
            for e in range(2):
                qv, dov = qa_ref[e, rows, :], doh_ref[e, rows, :]
                p = jnp.exp(_dg(ka[e], qv, "nt") - lse_ref[i, e])
                if diagonal:
                    p = jnp.where(_causal(p), p, 0.0)
                dv_acc[...] += _dg(p, dov, "nn")
                ds = p * (_dg(vh[e], dov, "nt") - del_ref[i, e])
                dk_acc[e] += _dg(ds, qv, "nn")
                dqt_ref[i, e] += _dg(kat[e], ds, "nn")

        block(j, True)

        def rest(i, carry):
            block(i, False)
            return carry

        lax.fori_loop(j + 1, nq, rest, 0)
        dk0, dk1 = dk_acc[0], dk_acc[1]
        dk_ref[...] = _pair_tile(dk0, dk1).astype(dk_ref.dtype)
        dv_ref[...] = dv_acc[...].astype(dv_ref.dtype)
        rows_j = pl.ds(pl.multiple_of(j * blk, blk), blk)
        for e, dke in enumerate((dk0, dk1)):
            dc_ref[rows_j, :] -= jnp.where(lane == 2 * g + e, _lane_col(dke, dh + 3), 0.0)

        @pl.when(j == nq - 1)
        def _():
            for i in range(nq):
                nat = [dqt_ref[i, e].T for e in range(2)]
                rows = slice(i * blk, (i + 1) * blk)
                dq_ref[rows, :] = (_pair_tile(nat[0], nat[1]) * dh ** -0.5).astype(dq_ref.dtype)
                for e in range(2):
                    dc_ref[rows, :] += jnp.where(lane == 2 * g + e, _lane_col(nat[e], dh), 0.0)

    col = lambda part: (lambda b, g, j: (b, part * N_PAIR + g))
    colj = lambda part: (lambda b, g, j: (b * nq + j, part * N_PAIR + g))
    pair = jax.ShapeDtypeStruct((m, D_MODEL), BF16)
    return pl.pallas_call(
        body, grid=(n_batch, N_PAIR, nq), name=name,
        in_specs=[pl.BlockSpec((seq, LANES), col(0)),
                  pl.BlockSpec((blk, LANES), colj(1)),
                  pl.BlockSpec((blk, LANES), colj(2)),
                  pl.BlockSpec((seq, 2 * LANES), lambda b, g, j: (b, g)),
                  pl.BlockSpec((blk, 2 * LANES), lambda b, g, j: (b * nq + j, g)),
                  pl.BlockSpec((seq, LANES), col(0)),
                  pl.BlockSpec((seq, LANES), col(0)),
                  pl.BlockSpec((None, None, nq, 2, 1, blk), lambda b, g, j: (b, g, 0, 0, 0, 0))],
        out_specs=[pl.BlockSpec((seq, LANES), col(0)),
                   pl.BlockSpec((blk, LANES), colj(0)),
                   pl.BlockSpec((blk, LANES), colj(0)),
                   pl.BlockSpec((seq, LANES), lambda b, g, j: (b, 0))],
        out_shape=[pair, pair, pair, jax.ShapeDtypeStruct((m, LANES), F32)],
        scratch_shapes=[pltpu.VMEM((2, seq, LANES), BF16), pltpu.VMEM((2, seq, LANES), BF16),
                        pltpu.VMEM((nq, 2, 1, blk), F32), pltpu.VMEM((nq, 2, LANES, blk), F32),
                        pltpu.VMEM((2, blk, LANES), F32), pltpu.VMEM((blk, LANES), F32)],
        compiler_params=_params(("arbitrary", "arbitrary", "arbitrary")),
    )(z, z, z, qterm, kterm, o, do, lse)


def _split3(c):
    c1 = c.astype(BF16)
    r1 = c - c1.astype(F32)
    c2 = r1.astype(BF16)
    c3 = (r1 - c2.astype(F32)).astype(BF16)
    return c1, c2, c3


def fox_operands(q, k, c):
    bh, seq, dh = q.shape
    c1, c2, c3 = (t[..., None] for t in _split3(c))
    one = jnp.ones((bh, seq, 1), BF16)
    pad = jnp.zeros((bh, seq, LANES - dh - 6), BF16)
    qa = jnp.concatenate([(q * dh ** -0.5).astype(BF16), c1, c2, c3, one, one, one, pad], axis=-1)
    ka = jnp.concatenate([k.astype(BF16), one, one, one, -c1, -c2, -c3, pad], axis=-1)
    return qa, ka


def fox_fwd(name, qa, ka, vt):
    bh, seq, da = qa.shape
    blk = min(ATT_BLK, seq)
    nq = seq // blk
    dh = vt.shape[2]

    def body(q_ref, k_ref, v_ref, o_ref, lse_ref):
        qi = pl.program_id(1)
        qv = q_ref[0]

        def block(j, carry, diagonal):
            mx, l, acc = carry
            kj = k_ref[0, pl.ds(pl.multiple_of(j * blk, blk), blk), :]
            s = _dg(kj, qv, "nt")
            if diagonal:
                key = lax.broadcasted_iota(jnp.int32, (blk, blk), 0)
                qry = lax.broadcasted_iota(jnp.int32, (blk, blk), 1)
                s = jnp.where(qry >= key, s, NEG)
            mx_new = jnp.maximum(mx, jnp.max(s, axis=0, keepdims=True))
            p = jnp.exp(s - mx_new)
            alpha = jnp.exp(mx - mx_new)
            l = alpha * l + jnp.sum(p, axis=0, keepdims=True)
            acc = alpha * acc + _dg(v_ref[0, j], p, "nn")
            return mx_new, l, acc

        init = (jnp.full((1, blk), NEG, F32), jnp.zeros((1, blk), F32), jnp.zeros((dh, blk), F32))
        carry = lax.fori_loop(0, qi, lambda j, cr: block(j, cr, False), init)
        mx, l, acc = block(qi, carry, True)
        o_ref[0] = (acc / l).astype(o_ref.dtype)
        lse_ref[0, 0] = mx + jnp.log(l)

    return pl.pallas_call(
        body, grid=(bh, nq), name=name,
        in_specs=[pl.BlockSpec((1, blk, da), lambda b, i: (b, i, 0)),
                  pl.BlockSpec((1, seq, da), lambda b, i: (b, 0, 0)),
                  pl.BlockSpec((1, nq, dh, blk), lambda b, i: (b, 0, 0, 0))],
        out_specs=[pl.BlockSpec((1, dh, blk), lambda b, i: (b, 0, i)),
                   pl.BlockSpec((1, 1, 1, blk), lambda b, i: (b, i, 0, 0))],
        out_shape=[jax.ShapeDtypeStruct((bh, dh, seq), BF16), jax.ShapeDtypeStruct((bh, nq, 1, blk), F32)],
        compiler_params=_params(("parallel", "arbitrary")),
    )(qa, ka, vt)


def fox_bwd(name, qa, ka, kat, v, do, dot, ot, lse):
    bh, seq, da = qa.shape
    blk = min(ATT_BLK, seq)
    nq = seq // blk
    dh = v.shape[2]

    def body(q_ref, k_ref, kt_ref, v_ref, do_ref, dot_ref, ot_ref, lse_ref, dq_ref, dk_ref, dv_ref, del_ref):
        j = pl.program_id(1)

        @pl.when(j == 0)
        def _():
            dq_ref[...] = jnp.zeros_like(dq_ref)
            for i in range(nq):
                cols = slice(i * blk, (i + 1) * blk)
                del_ref[i] = jnp.sum(dot_ref[0, :, cols].astype(F32) * ot_ref[0, :, cols].astype(F32), axis=0, keepdims=True)

        kj, kjt, vj = k_ref[0], kt_ref[0, 0], v_ref[0]

        def block(i, carry, diagonal):
            dk, dv = carry
            rows = pl.ds(pl.multiple_of(i * blk, blk), blk)
            qv, dov = q_ref[0, rows, :], do_ref[0, rows, :]
            p = jnp.exp(_dg(kj, qv, "nt") - lse_ref[0, i])
            if diagonal:
                key = lax.broadcasted_iota(jnp.int32, (blk, blk), 0)
                qry = lax.broadcasted_iota(jnp.int32, (blk, blk), 1)
                p = jnp.where(qry >= key, p, 0.0)
            dv = dv + _dg(p, dov, "nn")
            ds = p * (_dg(vj, dov, "nt") - del_ref[i])
            dk = dk + _dg(ds, qv, "nn")
            dq_ref[0, i] += _dg(kjt, ds, "nn")
            return dk, dv

        init = (jnp.zeros((blk, da), F32), jnp.zeros((blk, dh), F32))
        carry = block(j, init, True)
        dk, dv = lax.fori_loop(j + 1, nq, lambda i, cr: block(i, cr, False), carry)
        dk_ref[0] = dk
        dv_ref[0] = dv

    return pl.pallas_call(
        body, grid=(bh, nq), name=name,
        in_specs=[pl.BlockSpec((1, seq, da), lambda b, j: (b, 0, 0)),
                  pl.BlockSpec((1, blk, da), lambda b, j: (b, j, 0)),
                  pl.BlockSpec((1, 1, da, blk), lambda b, j: (b, j, 0, 0)),
                  pl.BlockSpec((1, blk, dh), lambda b, j: (b, j, 0)),
                  pl.BlockSpec((1, seq, dh), lambda b, j: (b, 0, 0)),
                  pl.BlockSpec((1, dh, seq), lambda b, j: (b, 0, 0)),
                  pl.BlockSpec((1, dh, seq), lambda b, j: (b, 0, 0)),
                  pl.BlockSpec((1, nq, 1, blk), lambda b, j: (b, 0, 0, 0))],
        out_specs=[pl.BlockSpec((1, nq, da, blk), lambda b, j: (b, 0, 0, 0)),
                   pl.BlockSpec((1, blk, da), lambda b, j: (b, j, 0)),
                   pl.BlockSpec((1, blk, dh), lambda b, j: (b, j, 0))],
        out_shape=[jax.ShapeDtypeStruct((bh, nq, da, blk), F32), jax.ShapeDtypeStruct((bh, seq, da), F32),
                   jax.ShapeDtypeStruct((bh, seq, dh), F32)],
        scratch_shapes=[pltpu.VMEM((nq, 1, blk), F32)],
        compiler_params=_params(("parallel", "arbitrary")),
    )(qa, ka, kat, v, do, dot, ot, lse)


def _mesh_pos():
    return lax.axis_index("x"), lax.axis_index("y"), lax.axis_index("c")


def _flip(v, bit):
    return 1 - v if bit else v


def all_gather(name, blocks):
    n = len(blocks)

    def body(*refs):
        x_refs, out_refs = refs[:n], refs[n:2 * n]
        send_sems, recv_sems, local_sems = refs[2 * n:]
        x, y, c = _mesh_pos()
        me, sibling = (x, y, c), (x, y, 1 - c)
        chips = [(1 - x, y), (x, 1 - y), (1 - x, 1 - y)]

        def slot(a, px, py, pc):
            return out_refs[a].at[4 * px + 2 * py + pc]

        def copy(a, k, blk, to, src=None):
            return pltpu.make_async_remote_copy(
                src_ref=slot(a, *blk) if src is None else src, dst_ref=slot(a, *blk),
                send_sem=send_sems.at[a, k], recv_sem=recv_sems.at[a, k], device_id=to, device_id_type=MESH)

        mine = [pltpu.make_async_copy(x_refs[a], slot(a, *me), local_sems.at[a]) for a in range(n)]
        for cp in mine:
            cp.start()
        sends = []
        for a in range(n):
            sends.append(copy(a, 0, me, sibling, src=x_refs[a]))
            sends += [copy(a, 1 + j, me, (*chip, c), src=x_refs[a]) for j, chip in enumerate(chips)]
        for cp in sends:
            cp.start()
        for j, chip in enumerate(chips):
            for a in range(n):
                copy(a, 1 + j, (*chip, c), me).wait_recv()
                passed = copy(a, 4 + j, (*chip, c), sibling)
                passed.start()
                sends.append(passed)
        for a in range(n):
            copy(a, 0, sibling, me).wait_recv()
            for j, chip in enumerate(chips):
                copy(a, 4 + j, (*chip, 1 - c), me).wait_recv()
        for cp in sends:
            cp.wait_send()
        for cp in mine:
            cp.wait()

    hbm = pl.BlockSpec(memory_space=pl.ANY)
    return pl.pallas_call(
        body, name=name, out_shape=[jax.ShapeDtypeStruct((N_DEV,) + b.shape, b.dtype) for b in blocks],
        in_specs=[hbm] * n, out_specs=[hbm] * n,
        scratch_shapes=[pltpu.SemaphoreType.DMA((n, 7)), pltpu.SemaphoreType.DMA((n, 7)), pltpu.SemaphoreType.DMA((n,))],
    )(*blocks)


def _peers(x, y, c):
    return [(_flip(x, k & 4), _flip(y, k & 2), _flip(c, k & 1)) for k in range(1, N_DEV)]


def gather_start(name, blocks, lands):
    n = len(blocks)

    def body(*refs):
        x_refs, land_refs = refs[:n], refs[n:2 * n]
        send_sems, recv_sems = refs[2 * n], refs[2 * n + 1]
        token = refs[-1]
        x, y, c = _mesh_pos()
        me = 4 * x + 2 * y + c
        for k, peer in enumerate(_peers(x, y, c)):
            for a in range(n):
                pltpu.make_async_remote_copy(
                    src_ref=x_refs[a], dst_ref=land_refs[a].at[me], send_sem=send_sems.at[7 * a + k], recv_sem=recv_sems.at[7 * a + k],
                    device_id=peer, device_id_type=MESH).start()
        token[...] = jnp.zeros_like(token)

    hbm = pl.BlockSpec(memory_space=pltpu.HBM)
    sem = pl.BlockSpec(memory_space=pltpu.SEMAPHORE)
    out_shape = ([pltpu.SemaphoreType.DMA((7 * n,)), pltpu.SemaphoreType.DMA((7 * n,))]
                 + [pltpu.HBM(b.shape, b.dtype) for b in blocks] + [pltpu.HBM(l.shape, l.dtype) for l in lands]
                 + [jax.ShapeDtypeStruct((8, LANES), F32)])
    res = pl.pallas_call(
        body, name=name, out_shape=out_shape, in_specs=[hbm] * (2 * n),
        out_specs=[sem, sem] + [hbm] * (2 * n) + [pl.BlockSpec(memory_space=pltpu.VMEM)],
        input_output_aliases={a: 2 + a for a in range(2 * n)},
        compiler_params=pltpu.CompilerParams(has_side_effects=pltpu.SideEffectType.DATAFLOW_SIDE_EFFECTING),
    )(*[pltpu.with_memory_space_constraint(b, pltpu.HBM) for b in blocks],
      *[pltpu.with_memory_space_constraint(l, pltpu.HBM) for l in lands])
    return res[0], res[1], res[2:2 + n], res[2 + n:2 + 2 * n], res[-1]


def gather_wait(name, send_sems, recv_sems, blocks, lands, after):
    n = len(blocks)

    def body(*refs):
        x_refs, land_refs = refs[:n], refs[n:2 * n]
        s_sems, r_sems = refs[2 * n], refs[2 * n + 1]
        x, y, c = _mesh_pos()
        me = 4 * x + 2 * y + c
        for k, peer in enumerate(_peers(x, y, c)):
            for a in range(n):
                cp = pltpu.make_async_remote_copy(
                    src_ref=x_refs[a], dst_ref=land_refs[a].at[me], send_sem=s_sems.at[7 * a + k], recv_sem=r_sems.at[7 * a + k],
                    device_id=peer, device_id_type=MESH)
                cp.wait_send()
                cp.wait_recv()

    hbm = pl.BlockSpec(memory_space=pltpu.HBM)
    sem = pl.BlockSpec(memory_space=pltpu.SEMAPHORE)
    res = pl.pallas_call(
        body, name=name,
        out_shape=[pltpu.HBM(b.shape, b.dtype) for b in blocks] + [pltpu.HBM(l.shape, l.dtype) for l in lands],
        in_specs=[hbm] * (2 * n) + [sem, sem, pl.BlockSpec(memory_space=pl.ANY)], out_specs=[hbm] * (2 * n),
        input_output_aliases={a: a for a in range(2 * n)},
        compiler_params=pltpu.CompilerParams(has_side_effects=pltpu.SideEffectType.DATAFLOW_SIDE_EFFECTING),
    )(*blocks, *lands, send_sems, recv_sems, after)
    return res[n:]


def _split_exchange(name, sends, lands, sems, after):
    n = len(sends)
    starting = sems is None

    def body(*refs):
        s_refs, l_refs = refs[:n], refs[n:2 * n]
        send_sems, recv_sems = refs[2 * n], refs[2 * n + 1]
        x, y, c = _mesh_pos()
        me = 4 * x + 2 * y + c
        for k, (px, py, pc) in enumerate(_peers(x, y, c)):
            for a in range(n):
                cp = pltpu.make_async_remote_copy(
                    src_ref=s_refs[a].at[4 * px + 2 * py + pc], dst_ref=l_refs[a].at[me],
                    send_sem=send_sems.at[7 * a + k], recv_sem=recv_sems.at[7 * a + k],
                    device_id=(px, py, pc), device_id_type=MESH)
                if starting:
                    cp.start()
                else:
                    cp.wait_send()
                    cp.wait_recv()
        if starting:
            refs[-1][...] = jnp.zeros_like(refs[-1])

    hbm = pl.BlockSpec(memory_space=pltpu.HBM)
    sem = pl.BlockSpec(memory_space=pltpu.SEMAPHORE)
    thru = [pltpu.HBM(t.shape, t.dtype) for t in list(sends) + list(lands)]
    effect = pltpu.CompilerParams(has_side_effects=pltpu.SideEffectType.DATAFLOW_SIDE_EFFECTING)
    if starting:
        res = pl.pallas_call(
            body, name=name, in_specs=[hbm] * (2 * n),
            out_shape=[pltpu.SemaphoreType.DMA((7 * n,)), pltpu.SemaphoreType.DMA((7 * n,))] + thru + [jax.ShapeDtypeStruct((8, LANES), F32)],
            out_specs=[sem, sem] + [hbm] * (2 * n) + [pl.BlockSpec(memory_space=pltpu.VMEM)],
            input_output_aliases={a: 2 + a for a in range(2 * n)}, compiler_params=effect,
        )(*[pltpu.with_memory_space_constraint(t, pltpu.HBM) for t in list(sends) + list(lands)])
        return res[0], res[1], res[2:2 + n], res[2 + n:2 + 2 * n], res[-1]
    res = pl.pallas_call(
        body, name=name, out_shape=thru, in_specs=[hbm] * (2 * n) + [sem, sem, pl.BlockSpec(memory_space=pl.ANY)],
        out_specs=[hbm] * (2 * n), input_output_aliases={a: a for a in range(2 * n)}, compiler_params=effect,
    )(*sends, *lands, sems[0], sems[1], after)
    return res[n:]


def own_slot_only(send, me):
    mine = lax.dynamic_index_in_dim(send, me, 0, keepdims=False)
    return lax.dynamic_update_index_in_dim(lax.empty(send.shape, send.dtype), mine, me, 0)


def all_to_all(name, sends):
    n = len(sends)

    def body(*refs):
        s_refs, r_refs = refs[:n], refs[n:2 * n]
        send_sems, recv_sems, local_sems = refs[2 * n:]
        x, y, c = _mesh_pos()
        me = 4 * x + 2 * y + c
        mine = [pltpu.make_async_copy(s_refs[a].at[me], r_refs[a].at[me], local_sems.at[a]) for a in range(n)]
        for cp in mine:
            cp.start()
        copies = []
        for k in range(1, N_DEV):
            px, py, pc = _flip(x, k & 4), _flip(y, k & 2), _flip(c, k & 1)
            for a in range(n):
                copies.append(pltpu.make_async_remote_copy(
                    src_ref=s_refs[a].at[4 * px + 2 * py + pc], dst_ref=r_refs[a].at[me],
                    send_sem=send_sems.at[a, k - 1], recv_sem=recv_sems.at[a, k - 1],
                    device_id=(px, py, pc), device_id_type=MESH))
        for cp in copies:
            cp.start()
        for cp in copies:
            cp.wait_recv()
        for cp in copies:
            cp.wait_send()
        for cp in mine:
            cp.wait()

    hbm = pl.BlockSpec(memory_space=pl.ANY)
    return pl.pallas_call(
        body, name=name, out_shape=[jax.ShapeDtypeStruct(s.shape, s.dtype) for s in sends],
        in_specs=[hbm] * n, out_specs=[hbm] * n,
        scratch_shapes=[pltpu.SemaphoreType.DMA((n, 7)), pltpu.SemaphoreType.DMA((n, 7)), pltpu.SemaphoreType.DMA((n,))],
    )(*sends)


def pair_exchange(name, hs):
    n = len(hs)

    def body(*refs):
        h_refs, r_refs = refs[:n], refs[n:2 * n]
        send_sems, recv_sems = refs[2 * n:]
        x, y, c = _mesh_pos()
        copies = [pltpu.make_async_remote_copy(
            src_ref=h_refs[a].at[1 - c], dst_ref=r_refs[a], send_sem=send_sems.at[a], recv_sem=recv_sems.at[a],
            device_id=(x, y, 1 - c), device_id_type=MESH) for a in range(n)]
        for cp in copies:
            cp.start()
        for cp in copies:
            cp.wait_recv()
        for cp in copies:
            cp.wait_send()

    hbm = pl.BlockSpec(memory_space=pl.ANY)
    return pl.pallas_call(
        body, name=name, out_shape=[jax.ShapeDtypeStruct(h.shape[1:], h.dtype) for h in hs],
        in_specs=[hbm] * n, out_specs=[hbm] * n,
        scratch_shapes=[pltpu.SemaphoreType.DMA((n,)), pltpu.SemaphoreType.DMA((n,))],
    )(*hs)


def quad_exchange(name, ss):
    n = len(ss)

    def body(*refs):
        s_refs, r_refs = refs[:n], refs[n:2 * n]
        send_sems, recv_sems, local_sems = refs[2 * n:]
        x, y, c = _mesh_pos()
        me = 2 * x + y
        mine = [pltpu.make_async_copy(s_refs[a].at[me], r_refs[a].at[me], local_sems.at[a]) for a in range(n)]
        for cp in mine:
            cp.start()
        copies = []
        for k in range(1, 4):
            px, py = _flip(x, k & 2), _flip(y, k & 1)
            for a in range(n):
                copies.append(pltpu.make_async_remote_copy(
                    src_ref=s_refs[a].at[2 * px + py], dst_ref=r_refs[a].at[me],
                    send_sem=send_sems.at[a, k - 1], recv_sem=recv_sems.at[a, k - 1],
                    device_id=(px, py, c), device_id_type=MESH))
        for cp in copies:
            cp.start()
        for cp in copies:
            cp.wait_recv()
        for cp in copies:
            cp.wait_send()
        for cp in mine:
            cp.wait()

    hbm = pl.BlockSpec(memory_space=pl.ANY)
    return pl.pallas_call(
        body, name=name, out_shape=[jax.ShapeDtypeStruct(s.shape, s.dtype) for s in ss],
        in_specs=[hbm] * n, out_specs=[hbm] * n,
        scratch_shapes=[pltpu.SemaphoreType.DMA((n, 3)), pltpu.SemaphoreType.DMA((n, 3)), pltpu.SemaphoreType.DMA((n,))],
    )(*ss)


def _rows_cols(shape):
    r = 1
    for d in shape[:-1]:
        r *= d
    return r, shape[-1]


def _row_tile(r, cap, step):
    return next((t for t in range(cap, step - 1, -step) if r % t == 0), r)


def pair_add(name, h, recv, core):
    shape = recv.shape
    r, c = _rows_cols(shape[1:])
    tr = _row_tile(r, 256, 16)

    def body(core_ref, h_ref, r_ref, o_ref):
        o_ref[...] = (h_ref[...].astype(F32) + r_ref[...].astype(F32)).astype(o_ref.dtype)

    spec = pl.BlockSpec((None, tr, c), lambda q, i, core_ref: (q, i, 0))
    res = pl.pallas_call(
        body, name=name, out_shape=jax.ShapeDtypeStruct((4, r, c), h.dtype),
        grid_spec=pltpu.PrefetchScalarGridSpec(
            num_scalar_prefetch=1, grid=(4, r // tr),
            in_specs=[pl.BlockSpec((None, None, tr, c), lambda q, i, core_ref: (core_ref[0], q, i, 0)), spec],
            out_specs=spec),
        compiler_params=_params(("parallel", "parallel")),
    )(core, h.reshape(2, 4, r, c), recv.reshape(4, r, c))
    return res.reshape(shape)


def _sum_parts(p, n):
    t = [p[k].astype(F32) for k in range(n)]
    while len(t) > 1:
        t = [t[k] + t[k + 1] for k in range(0, len(t), 2)]
    return t[0]


def _adam(g, w, m, v):
    m = ADAM_B1 * m + (1.0 - ADAM_B1) * g
    v = ADAM_B2 * v + (1.0 - ADAM_B2) * (g * g)
    m_hat = m / (1.0 - ADAM_B1 ** ADAM_STEP)
    v_hat = v / (1.0 - ADAM_B2 ** ADAM_STEP)
    return -ADAM_LR * (m_hat / (jnp.sqrt(v_hat) + ADAM_EPS) + ADAM_WD * w), m, v


def adam_tiled(name, partials, w, m_, v_):
    shape = w.shape
    n_part = partials.shape[0]
    r, c = _rows_cols(shape)
    tr = _row_tile(r, 256, 16)

    def body(p_ref, w_ref, m_ref, v_ref, g_ref, d_ref, nm_ref, nv_ref):
        g = _sum_parts(p_ref, n_part)
        g_ref[...] = g
        d_ref[...], nm_ref[...], nv_ref[...] = _adam(g, w_ref[...], m_ref[...], v_ref[...])

    spec = pl.BlockSpec((tr, c), lambda i: (i, 0))
    res = pl.pallas_call(
        body, grid=(r // tr,), name=name,
        in_specs=[pl.BlockSpec((n_part, tr, c), lambda i: (0, i, 0)), spec, spec, spec],
        out_specs=[spec] * 4, out_shape=[jax.ShapeDtypeStruct((r, c), F32)] * 4,
        compiler_params=_params(("parallel",)),
    )(partials.reshape(n_part, r, c), w.reshape(r, c), m_.reshape(r, c), v_.reshape(r, c))
    return [t.reshape(shape) for t in res]


def adam_small(name, items, extra):
    n, ne = len(items), len(extra)

    def body(*refs):
        ins, outs = refs[:4 * n + ne], refs[4 * n + ne:]
        for a in range(n):
            p_ref, w_ref, m_ref, v_ref = ins[4 * a:4 * a + 4]
            g = _sum_parts(p_ref, N_DEV)
            outs[4 * a][...] = g
            outs[4 * a + 1][...], outs[4 * a + 2][...], outs[4 * a + 3][...] = _adam(g, w_ref[...], m_ref[...], v_ref[...])
        for e in range(ne):
            outs[4 * n + e][...] = _sum_parts(ins[4 * n + e], N_DEV)

    args, out_shape = [], []
    for p, w, m_, v_ in items:
        args += [p, w, m_, v_]
        out_shape += [jax.ShapeDtypeStruct(w.shape, F32)] * 4
    for e in extra:
        args.append(e)
        out_shape.append(jax.ShapeDtypeStruct(e.shape[1:], F32))
    vmem = pl.BlockSpec(memory_space=pltpu.VMEM)
    res = pl.pallas_call(body, name=name, in_specs=[vmem] * len(args), out_specs=[vmem] * len(out_shape), out_shape=out_shape)(*args)
    return [res[4 * a:4 * a + 4] for a in range(n)], res[4 * n:]


def _cols_from_gather(g):
    g = jnp.moveaxis(g, 0, -2)
    return g.reshape(g.shape[:-2] + (g.shape[-2] * g.shape[-1],))


def _cols_to_blocks(w):
    w = w.reshape(w.shape[:-1] + (N_DEV, w.shape[-1] // N_DEV))
    return jnp.moveaxis(w, -2, 0)


def _block_diag(w):
    z = jnp.zeros((B_BLOCK_DIM, B_BLOCK_DIM), w.dtype)
    rows = []
    for j in range(B_BLOCKS // 2):
        top = jnp.concatenate([w[2 * j], z], axis=1)
        bot = jnp.concatenate([z, w[2 * j + 1]], axis=1)
        rows.append(jnp.concatenate([top, bot], axis=0))
    return jnp.concatenate(rows, axis=0)


def _block_diag_grad(d):
    out = []
    for j in range(B_BLOCKS // 2):
        blk = d[128 * j:128 * (j + 1)]
        out.append(blk[:64, :64])
        out.append(blk[64:, 64:])
    return jnp.stack(out)


NAMES = ("norm_gains", "even_w_in", "hgrn_lb_logits", "hgrn_norm", "rg_conv_w", "rg_conv_b", "rg_wa", "rg_ba", "rg_wx", "rg_bx",
         "rg_lambda", "even_w_out", "odd_w_in", "fox_f_bias", "odd_w_out", "ffn_w_up", "ffn_conv_w", "ffn_conv_b", "ffn_w_down")
BIG = ("even_w_in", "even_w_out", "odd_w_in", "odd_w_out", "ffn_w_up", "ffn_w_down")
SMALL_SHARDED = ("norm_gains", "rg_conv_w", "ffn_conv_w")
REPLICATED = ("hgrn_lb_logits", "hgrn_norm", "rg_conv_b", "rg_wa", "rg_ba", "rg_wx", "rg_bx", "rg_lambda", "fox_f_bias", "ffn_conv_b")


def _ffn_forward(tag, layer, h, w_up_g, cw5, cb5, w_down_g, m, seq):
    tm = _div_tile(m, 1024)
    nm = m // tm
    hid = mm(f"{tag}_up", "nn",
             Blk(h, (tm, D_MODEL), lambda i, j, k: (i, 0)),
             Blk(w_up_g, (None, None, D_MODEL, FF_BLK), lambda i, j, k: (j, 0, 0, 0)),
             Blk((N_DEV, m, FF_BLK), (None, tm, FF_BLK), lambda i, j, k: (j, i, 0)), F32, (nm, N_DEV, 1))
    hid = hid.reshape(2, N_DEV // 2, m, FF_BLK)
    act = ffn_mid_fwd(f"{tag}_mid", hid, cw5, cb5, layer, m=m, seq=seq)
    f = mm(f"{tag}_down", "nn",
           Blk(act, (None, tm, FF_BLK), lambda i, j, k: (k, i, 0)),
           Blk(w_down_g, (2, None, FF_BLK // 2, D_MODEL), lambda i, j, k: (k, 0, 0, 0)),
           Blk((m, D_MODEL), (tm, D_MODEL), lambda i, j, k: (i, 0)), F32, (nm, 1, N_DEV // 2))
    return hid, act, f


def _ffn_backward(tag, layer, df, h, hid, act, w_up_g, cw5, cb5, w_down_g, m, seq):
    tm = _div_tile(m, 1024)
    nm = m // tm
    dact = mm(f"{tag}_dact", "nt",
              Blk(df, (tm, D_MODEL), lambda i, j, k: (i, 0)),
              Blk(w_down_g, (2, None, FF_BLK // 2, D_MODEL), lambda i, j, k: (j, 0, 0, 0)),
              Blk((N_DEV // 2, m, FF_BLK), (None, tm, FF_BLK), lambda i, j, k: (j, i, 0)), BF16, (nm, N_DEV // 2, 1))
    d_wdown = mm(f"{tag}_dwdown", "tn",
                 Blk(act, (None, tm, FF_BLK), lambda i, j, k: (i, k, 0)),
                 Blk(df, (tm, D_MODEL), lambda i, j, k: (k, 0)),
                 Blk(w_down_g.shape, (2, None, FF_BLK // 2, D_MODEL), lambda i, j, k: (i, 0, 0, 0)), BF16,
                 (N_DEV // 2, 1, nm))
    dhid, d_cw, d_cb = ffn_mid_bwd(f"{tag}_dmid", hid, cw5, cb5, dact, layer, m=m, seq=seq)
    dhid = dhid.reshape(N_DEV, m, FF_BLK)
    dh = mm(f"{tag}_dh", "nt",
            Blk(dhid, (None, tm, FF_BLK), lambda i, j, k: (k, i, 0)),
            Blk(w_up_g, (None, None, D_MODEL, FF_BLK), lambda i, j, k: (k, 0, 0, 0)),
            Blk((m, D_MODEL), (tm, D_MODEL), lambda i, j, k: (i, 0)), BF16, (nm, 1, N_DEV))
    d_wup = mm(f"{tag}_dwup", "tn",
               Blk(h, (tm, D_MODEL), lambda i, j, k: (k, 0)),
               Blk(dhid, (None, tm, FF_BLK), lambda i, j, k: (j, k, 0)),
               Blk(w_up_g.shape, (None, None, D_MODEL, FF_BLK), lambda i, j, k: (j, 0, 0, 0)), BF16,
               (1, N_DEV, nm))
    return dh, d_wup, d_cw, d_cb, d_wdown


def kernel(x, norm_gains, even_w_in, hgrn_lb_logits, hgrn_norm, rg_conv_w, rg_conv_b, rg_wa, rg_ba, rg_wx, rg_bx, rg_lambda, even_w_out, odd_w_in, fox_f_bias, odd_w_out, ffn_w_up, ffn_conv_w, ffn_conv_b, ffn_w_down, loss_target, m_norm_gains, m_even_w_in, m_hgrn_lb_logits, m_hgrn_norm, m_rg_conv_w, m_rg_conv_b, m_rg_wa, m_rg_ba, m_rg_wx, m_rg_bx, m_rg_lambda, m_even_w_out, m_odd_w_in, m_fox_f_bias, m_odd_w_out, m_ffn_w_up, m_ffn_conv_w, m_ffn_conv_b, m_ffn_w_down, v_norm_gains, v_even_w_in, v_hgrn_lb_logits, v_hgrn_norm, v_rg_conv_w, v_rg_conv_b, v_rg_wa, v_rg_ba, v_rg_wx, v_rg_bx, v_rg_lambda, v_even_w_out, v_odd_w_in, v_fox_f_bias, v_odd_w_out, v_ffn_w_up, v_ffn_conv_w, v_ffn_conv_b, v_ffn_w_down):
    local = dict(locals())
    w = {n: local[n] for n in NAMES}
    mom = {n: local["m_" + n] for n in NAMES}
    var = {n: local["v_" + n] for n in NAMES}
    n_batch, seq, _ = x.shape
    m = n_batch * seq
    tm = _div_tile(m, 512)
    tmm = _div_tile(m, 1024)
    nm = m // tmm

    now = [w["even_w_in"], w["even_w_out"]]
    gathered = all_gather("gather_weights", [t.astype(BF16) for t in now] + [w[n] for n in SMALL_SHARDED])
    g = dict(zip(("even_w_in", "even_w_out") + SMALL_SHARDED, gathered))
    w_in_e = g["even_w_in"]
    w_out_e = g["even_w_out"].reshape(D_MODEL, D_MODEL)
    gains = _cols_from_gather(g["norm_gains"])
    me = 4 * lax.axis_index("x") + 2 * lax.axis_index("y") + lax.axis_index("c")
    own_block_only = lambda t: lax.dynamic_update_index_in_dim(lax.empty((N_DEV,) + t.shape, t.dtype), t, me, 0)
    behind = (g["norm_gains"][0, 0, 0, 0] * 0.0).astype(BF16)
    ffn0 = [w["ffn_w_up"][0:1].astype(BF16) + behind, w["ffn_w_down"][0:1].astype(BF16) + behind]
    ffn0_sent = gather_start("gather_ffn0_start", ffn0, [own_block_only(t) for t in ffn0])
    behind = (ffn0_sent[4][0, 0] * 0.0).astype(BF16)
    later = [w["odd_w_in"], w["odd_w_out"], w["ffn_w_up"][1:2], w["ffn_w_down"][1:2]]
    later = [t.astype(BF16) + behind for t in later]
    send_sems, recv_sems, later, lands, started = gather_start("gather_later_start", later, [own_block_only(t) for t in later])
    rg_cw = _cols_from_gather(g["rg_conv_w"])[0]
    n_layer = ffn_conv_w.shape[0]
    cw5 = g["ffn_conv_w"].reshape(2, N_DEV // 2, n_layer, FFN_CONV, FF_BLK)
    cb5 = ffn_conv_b.reshape(n_layer, 2, N_DEV // 2, 1, FF_BLK)
    gain = lambda l, k: gains[l, k:k + 1, :]
    wa_bd, wx_bd = _block_diag(rg_wa[0]), _block_diag(rg_wx[0])
    fbias = jnp.pad(fox_f_bias, ((0, 0), (0, LANES - C_HEADS)))

    x0 = x.reshape(m, D_MODEL)
    tgt = loss_target.reshape(m, D_MODEL)

    (h0,) = tile_fwd("l0_prenorm", fn_prenorm_after, m=m, tm=tm, nj=1, rows=[Row(x0)], pars=[Par(gain(0, 0)), Par(started)],
                     outs=[Out(D_MODEL, BF16)])
    z0 = mm("l0_in", "nn",
            Blk(h0, (tmm, D_MODEL), lambda i, j, k: (i, 0)),
            Blk(w_in_e, (None, None, D_MODEL, 384), lambda i, j, k: (j, 0, 0, 0)),
            Blk((m, 3072), (tmm, 384), lambda i, j, k: (i, j)), F32, (nm, N_DEV, 1))
    oa, sprev = hgrn_fwd("l0_hgrn", z0, hgrn_lb_logits, hgrn_norm, n_batch=n_batch, seq=seq)
    rg_rows = lambda: [Row(z0, LANES, 16), Row(z0, LANES, 20)]
    rg_pars = lambda: [Par(rg_cw, "col", LANES), Par(rg_conv_b, "col", LANES), Par(wa_bd, "row", LANES), Par(rg_ba, "col", LANES),
                       Par(wx_bd, "row", LANES), Par(rg_bx, "col", LANES), Par(rg_lambda, "col", LANES)]
    (ob,) = tile_fwd("l0_rglru", fn_rglru, m=m, tm=seq, nj=B_WIDTH // LANES, rows=rg_rows(), pars=rg_pars(),
                     outs=[Out(B_WIDTH, BF16, LANES)])
    mixcat0 = jnp.concatenate([oa, ob], axis=-1)
    mix0 = mm2d("l0_out", "nn", mixcat0, w_out_e)
    x1, h1 = tile_fwd("l0_postnorm", fn_addnorm2, m=m, tm=tm, nj=1, rows=[Row(x0), Row(mix0)], pars=[Par(gain(0, 1)), Par(gain(0, 2))],
                      outs=[Out(D_MODEL, F32), Out(D_MODEL, BF16)])
    w_up_g0, w_down_g0 = gather_wait("gather_ffn0_wait", ffn0_sent[0], ffn0_sent[1], ffn0_sent[2], ffn0_sent[3], h1)
    hid0, act0, f0 = _ffn_forward("l0_ffn", 0, h1, w_up_g0, cw5, cb5, w_down_g0, m, seq)
    x2, h2 = tile_fwd("l0_ffnnorm", fn_addnorm2, m=m, tm=tm, nj=1, rows=[Row(x1), Row(f0)], pars=[Par(gain(0, 3)), Par(gain(1, 0))],
                      outs=[Out(D_MODEL, F32), Out(D_MODEL, BF16)])

    g_in_o, g_out_o, w_up_g1, w_down_g1 = gather_wait("gather_later_wait", send_sems, recv_sems, later, lands, h2)
    w_in_o = jnp.pad(_cols_from_gather(g_in_o)[0], ((0, 0), (0, 3200 - 3088)))
    w_out_o = g_out_o.reshape(D_MODEL, D_MODEL)
    z1 = mm2d("l1_in", "nn", h2, w_in_o)
    (cgate,) = tile_fwd("l1_gate", fn_fox_gate, m=m, tm=seq, nj=1, rows=[Row(z1, LANES, 3072 // LANES)], pars=[Par(fbias)],
                        outs=[Out(LANES, F32)])
    place, ones_q, ones_k = term_placement()
    qterm, kterm = tile_fwd("l1_terms", fn_fox_terms, m=m, tm=tm, nj=1, rows=[Row(cgate)],
                            pars=[Par(place), Par(ones_q), Par(ones_k)], outs=[Out(TERM_W, BF16), Out(TERM_W, BF16)])
    oc, lse = fox_pair_fwd("l1_attn", z1, qterm, kterm, n_batch=n_batch, seq=seq)
    mix1 = mm2d("l1_out", "nn", oc, w_out_o)
    x3, h3 = tile_fwd("l1_postnorm", fn_addnorm2, m=m, tm=tm, nj=1, rows=[Row(x2), Row(mix1)], pars=[Par(gain(1, 1)), Par(gain(1, 2))],
                      outs=[Out(D_MODEL, F32), Out(D_MODEL, BF16)])
    hid1, act1, f1 = _ffn_forward("l1_ffn", 1, h3, w_up_g1, cw5, cb5, w_down_g1, m, seq)
    dy, loss_part = tile_fwd("loss", fn_final, m=m, tm=tm, nj=1, rows=[Row(x3), Row(f1), Row(tgt)], pars=[Par(gain(1, 3))],
                             outs=[Out(D_MODEL, F32)], n_acc=1)

    df1, d_g13 = tile_bwd("l1_dffnnorm", fn_rms_only, m=m, tm=tm, nj=1, rows=[Row(f1)], pars=[Par(gain(1, 3))], cts=[Row(dy)],
                          drows=[Out(D_MODEL, BF16)])
    dh3, d_wup1, d_cw1, d_cb1, d_wdown1 = _ffn_backward("l1_ffn", 1, df1, h3, hid1, act1, w_up_g1, cw5, cb5, w_down_g1, m, seq)
    dx2, dmix1, d_g11, d_g12 = tile_bwd("l1_dpostnorm", fn_addnorm2, m=m, tm=tm, nj=1, rows=[Row(x2), Row(mix1)],
                                        pars=[Par(gain(1, 1)), Par(gain(1, 2))], cts=[Row(dy), Row(dh3)],
                                        drows=[Out(D_MODEL, F32), Out(D_MODEL, BF16)])
    doc = mm2d("l1_doc", "nt", dmix1, w_out_o, BF16)
    d_wout_o = mm2d("l1_dwout", "tn", oc, dmix1)
    dq, dk, dv, dc = fox_pair_bwd("l1_dattn", z1, qterm, kterm, oc, doc, lse, n_batch=n_batch, seq=seq)
    dzf, d_fbias = tile_bwd("l1_dgate", fn_fox_gate, m=m, tm=seq, nj=1, rows=[Row(z1, LANES, 3072 // LANES)], pars=[Par(fbias)],
                            cts=[Row(dc)], drows=[Out(LANES, BF16)])
    dz1 = jnp.concatenate([dq, dk, dv, dzf], axis=-1)
    dh2 = mm2d("l1_dh", "nt", dz1, w_in_o, BF16)
    d_win_o = mm2d("l1_dwin", "tn", h2, dz1)

    send1 = [_cols_to_blocks(d_win_o[None, :, :3088]).astype(BF16),
             d_wout_o.reshape(N_DEV, 1, D_MODEL // N_DEV, D_MODEL).astype(BF16), d_wup1, d_wdown1]
    sent1 = _split_exchange("exchange_l1_start", send1, [own_slot_only(t, me) for t in send1], None, None)

    dx1, df0, d_g03, d_g10 = tile_bwd("l0_dffnnorm", fn_addnorm2_after, m=m, tm=tm, nj=1, rows=[Row(x1), Row(f0)],
                                      pars=[Par(gain(0, 3)), Par(gain(1, 0)), Par(sent1[4])], cts=[Row(dx2), Row(dh2)],
                                      drows=[Out(D_MODEL, F32), Out(D_MODEL, BF16)])[:4]
    dh1, d_wup0, d_cw0, d_cb0, d_wdown0 = _ffn_backward("l0_ffn", 0, df0, h1, hid0, act0, w_up_g0, cw5, cb5, w_down_g0, m, seq)
    send0 = [d_wup0, d_wdown0]
    sent0 = _split_exchange("exchange_ffn0_start", send0, [own_slot_only(t, me) for t in send0], None, None)
    dx0a, dmix0, d_g01, d_g02 = tile_bwd("l0_dpostnorm", fn_addnorm2_after, m=m, tm=tm, nj=1, rows=[Row(x0), Row(mix0)],
                                         pars=[Par(gain(0, 1)), Par(gain(0, 2)), Par(sent0[4])], cts=[Row(dx1), Row(dh1)],
                                         drows=[Out(D_MODEL, F32), Out(D_MODEL, BF16)])[:4]
    dmixcat0 = mm2d("l0_dmixcat", "nt", dmix0, w_out_e, BF16)
    d_wout_e = mm2d("l0_dwout", "tn", mixcat0, dmix0)
    dzq, dzf0, dzv, dzg, d_lb, d_hnorm = hgrn_bwd("l0_dhgrn", z0, sprev, hgrn_lb_logits, hgrn_norm, dmixcat0, n_batch=n_batch, seq=seq)
    dzx, dzy, d_rcw, d_rcb, d_wa, d_ba, d_wx, d_bx, d_lam = tile_bwd(
        "l0_drglru", fn_rglru, m=m, tm=seq, nj=B_WIDTH // LANES, rows=rg_rows(), pars=rg_pars(),
        cts=[Row(dmixcat0, LANES, A_WIDTH // LANES)], drows=[Out(B_WIDTH, BF16, LANES), Out(B_WIDTH, BF16, LANES)])
    dz0 = jnp.concatenate([dzq, dzf0, dzv, dzg, dzx, dzy], axis=-1)
    d_win_e = mm("l0_dwin", "tn",
                 Blk(h0, (tmm, D_MODEL), lambda i, j, k: (k, 0)),
                 Blk(dz0, (tmm, 384), lambda i, j, k: (k, j)),
                 Blk(w_in_e.shape, (None, None, D_MODEL, 384), lambda i, j, k: (j, 0, 0, 0)), BF16, (1, N_DEV, nm))
    send_e = [d_win_e, d_wout_e.reshape(N_DEV, 1, D_MODEL // N_DEV, D_MODEL).astype(BF16)]
    sent_e = _split_exchange("exchange_even_start", send_e, [own_slot_only(t, me) for t in send_e], None, None)
    d_ffn_cb = jnp.stack([d_cb0, d_cb1]).reshape(n_layer, 2 * D_FF)
    rep = {"hgrn_lb_logits": d_lb, "hgrn_norm": d_hnorm, "rg_conv_b": d_rcb, "rg_wa": _block_diag_grad(d_wa)[None], "rg_ba": d_ba,
           "rg_wx": _block_diag_grad(d_wx)[None], "rg_bx": d_bx, "rg_lambda": d_lam, "fox_f_bias": d_fbias[:, :C_HEADS],
           "ffn_conv_b": d_ffn_cb}
    rep_blocks = [rep[n] for n in REPLICATED] + [loss_part]
    rep_sent = gather_start("gather_partials_start", rep_blocks, [own_block_only(t) for t in rep_blocks])
    dh0 = mm("l0_dh", "nt",
             Blk(dz0, (tmm, 384), lambda i, j, k: (i, k)),
             Blk(w_in_e, (None, None, D_MODEL, 384), lambda i, j, k: (k, 0, 0, 0)),
             Blk((m, D_MODEL), (tmm, D_MODEL), lambda i, j, k: (i, 0)), BF16, (nm, 1, N_DEV), after=sent_e[4] + rep_sent[4])
    dx0, d_g00 = tile_bwd("l0_dprenorm", fn_input_norm, m=m, tm=tm, nj=1, rows=[Row(x0)], pars=[Par(gain(0, 0))],
                          cts=[Row(dx0a), Row(dh0)], drows=[Out(D_MODEL, F32)])

    d_gains = jnp.stack([jnp.concatenate([d_g00, d_g01, d_g02, d_g03], axis=0), jnp.concatenate([d_g10, d_g11, d_g12, d_g13], axis=0)])
    d_ffn_cw = jnp.stack([d_cw0, d_cw1], axis=2).reshape(N_DEV, n_layer, FFN_CONV, FF_BLK)
    r_in_o, r_out_o, r_up1, r_down1 = _split_exchange("exchange_l1_wait", sent1[2], sent1[3], sent1[:2], dx0)
    r_up0, r_down0 = _split_exchange("exchange_ffn0_wait", sent0[2], sent0[3], sent0[:2], dx0)
    r_in_e, r_out_e = _split_exchange("exchange_even_wait", sent_e[2], sent_e[3], sent_e[:2], dx0)
    recv, res = {}, {}
    for n, r in (("even_w_in", r_in_e), ("even_w_out", r_out_e), ("odd_w_in", r_in_o), ("odd_w_out", r_out_o)):
        res[n] = adam_tiled("adam_" + n, r, w[n], mom[n], var[n])
    for n, parts_l in (("ffn_w_up", (r_up0, r_up1)), ("ffn_w_down", (r_down0, r_down1))):
        per_layer = [adam_tiled(f"adam_{n}_{l}", p, w[n][l:l + 1], mom[n][l:l + 1], var[n][l:l + 1]) for l, p in enumerate(parts_l)]
        res[n] = [jnp.concatenate([per_layer[0][k], per_layer[1][k]], axis=0) for k in range(4)]
    small_send = [_cols_to_blocks(d_gains), _cols_to_blocks(d_rcw[None]), d_ffn_cw]
    recv.update(zip(SMALL_SHARDED, all_to_all("exchange_small", small_send)))

    parts = gather_wait("gather_partials_wait", rep_sent[0], rep_sent[1], rep_sent[2], rep_sent[3], dx0)
    for n, p in zip(REPLICATED, parts):
        recv[n] = p
    small = SMALL_SHARDED + REPLICATED
    small_res, (loss_sum,) = adam_small("adam_small", [(recv[n], w[n], mom[n], var[n]) for n in small], [parts[-1]])
    res.update(dict(zip(small, small_res)))

    out = [loss_sum[0, 0], dx0.reshape(x.shape)]
    for k in range(4):
        out += [res[n][k] for n in NAMES]
    return tuple(out)
```

```python
import functools

import jax
import jax.numpy as jnp
from jax import lax
from jax.experimental import pallas as pl
from jax.experimental.pallas import tpu as pltpu

F32 = jnp.float32
BF16 = jnp.bfloat16

D_MODEL = 1024
A_HEADS = 4
A_WIDTH = 512
HGRN_CHUNK = 64
HGRN_SEG = 512
B_WIDTH = 512
B_BLOCKS = 8
B_BLOCK_DIM = 64
B_CONV = 4
RG_C = 8.0
C_HEADS = 16
C_HEAD_DIM = 64
D_FF = 2816
FFN_CONV = 3
EPS = 1e-6
LANES = 128
HALO = 16
N_DEV = 8
FF_BLK = 2 * D_FF // N_DEV
MESH = pl.DeviceIdType.MESH
NEG = -1e30
VMEM_LIMIT = 56 * 1024 * 1024

ADAM_LR = 0.001
ADAM_B1 = 0.9
ADAM_B2 = 0.999
ADAM_EPS = 1e-08
ADAM_WD = 0.01
ADAM_STEP = 10


def _dg(a, b, pat):
    nb = a.ndim - 2
    batch = (tuple(range(nb)), tuple(range(nb)))
    ca = a.ndim - 1 if pat[0] == "n" else a.ndim - 2
    cb = b.ndim - 2 if pat[1] == "n" else b.ndim - 1
    return lax.dot_general(a.astype(BF16), b.astype(BF16), (((ca,), (cb,)), batch), preferred_element_type=F32)


@functools.partial(jax.custom_vjp, nondiff_argnums=(2,))
def bdot(a, b, pat):
    return _dg(a, b, pat)


def _bdot_fwd(a, b, pat):
    return _dg(a, b, pat), (a, b)


def _bdot_bwd(pat, res, g):
    a, b = res
    if pat == "nn":
        return _dg(g, b, "nt"), _dg(a, g, "tn")
    if pat == "nt":
        return _dg(g, b, "nn"), _dg(g, a, "tn")
    return _dg(b, g, "nt"), _dg(a, g, "nn")


bdot.defvjp(_bdot_fwd, _bdot_bwd)


def _shift_raw(x, s, up, fill):
    if s == 0:
        return x
    n = x.shape[0]
    r = pltpu.roll(x, (n - s) if up else s, 0)
    idx = lax.broadcasted_iota(jnp.int32, x.shape, 0)
    mask = (idx >= n - s) if up else (idx < s)
    return jnp.where(mask, jnp.asarray(fill, x.dtype), r)


@functools.partial(jax.custom_vjp, nondiff_argnums=(1,))
def shift_down(x, s):
    return _shift_raw(x, s, False, 0.0)


def _shift_down_fwd(x, s):
    return _shift_raw(x, s, False, 0.0), None


def _shift_down_bwd(s, _, g):
    return (_shift_raw(g, s, True, 0.0),)


shift_down.defvjp(_shift_down_fwd, _shift_down_bwd)


def _scan_impl(a, u, up):
    n = a.shape[0]
    s = 1
    while s < n:
        u = a * _shift_raw(u, s, up, 0.0) + u
        if 2 * s < n:
            a = a * _shift_raw(a, s, up, 1.0)
        s *= 2
    return u


@jax.custom_vjp
def lin_scan(a, u):
    return _scan_impl(a, u, False)


def _lin_scan_fwd(a, u):
    h = _scan_impl(a, u, False)
    return h, (a, h)


def _lin_scan_bwd(res, g):
    a, h = res
    gh = _scan_impl(_shift_raw(a, 1, True, 0.0), g, True)
    return gh * _shift_raw(h, 1, False, 0.0), gh


lin_scan.defvjp(_lin_scan_fwd, _lin_scan_bwd)


def _cumsum_impl(x, up, period):
    n = x.shape[0]
    span = n if period is None else period
    idx = lax.broadcasted_iota(jnp.int32, x.shape, 0)
    pos = idx if period is None else idx % period
    s = 1
    while s < span:
        sh = _shift_raw(x, s, up, 0.0)
        if period is not None:
            keep = (pos < period - s) if up else (pos >= s)
            sh = jnp.where(keep, sh, 0.0)
        x = x + sh
        s *= 2
    return x


@functools.partial(jax.custom_vjp, nondiff_argnums=(1,))
def cumsum_rows(x, period):
    return _cumsum_impl(x, False, period)


def _cumsum_fwd(x, period):
    return _cumsum_impl(x, False, period), None


def _cumsum_bwd(period, _, g):
    return (_cumsum_impl(g, True, period),)


cumsum_rows.defvjp(_cumsum_fwd, _cumsum_bwd)


def _sigmoid(x):
    return jax.nn.sigmoid(x)


def _expm1(x):
    return jnp.tanh(0.5 * x) * (jnp.exp(x) + 1.0)


def _softplus(x):
    return jnp.maximum(x, 0.0) + jnp.log(1.0 + jnp.exp(-jnp.abs(x)))


def _rms(x, g):
    return x * lax.rsqrt(jnp.mean(x * x, axis=-1, keepdims=True) + EPS) * g


def fn_prenorm(x, g):
    return (_rms(x, g).astype(BF16),)


def fn_prenorm_after(x, g, _token):
    return fn_prenorm(x, g)


def fn_addnorm2(x, y, g_post, g_pre):
    x1 = x + _rms(y, g_post)
    return x1, _rms(x1, g_pre).astype(BF16)


def fn_addnorm2_after(x, y, g_post, g_pre, _token):
    return fn_addnorm2(x, y, g_post, g_pre)


def fn_input_norm(x, g):
    return x, _rms(x, g).astype(BF16)


def fn_final(x, y, tgt, g_post):
    out = x + _rms(y, g_post)
    err = out - tgt
    dy = err * (1.0 / D_MODEL)
    loss = 0.5 * jnp.sum(jnp.mean(err * err, axis=-1, keepdims=True), axis=0, keepdims=True)
    return dy, jnp.broadcast_to(loss, (1, LANES))


def fn_rms_only(y, g):
    return (_rms(y, g),)


def _causal_conv(x, w, b, taps):
    c = b
    for k in range(taps):
        c = c + w[k:k + 1, :] * shift_down(x, taps - 1 - k)
    return c


def fn_rglru(xb, yb, cw, cb, wa, ba, wx, bx, lam):
    xf = _causal_conv(xb, cw, cb, B_CONV)
    r = _sigmoid(bdot(xf, wa, "nn") + ba)
    i = _sigmoid(bdot(xf, wx, "nn") + bx)
    log_a = -RG_C * r * _softplus(-lam)
    a = jnp.exp(log_a)
    u = jnp.sqrt(-_expm1(2.0 * log_a)) * (i * xf)
    h = lin_scan(a, u)
    return ((h * jax.nn.gelu(yb)).astype(BF16),)


def fn_fox_gate(zf, bias):
    return (cumsum_rows(jax.nn.log_sigmoid(zf + bias), None),)


def fn_hgrn_seg(q, fl, v, g, st, logits, hn):
    rows = q.shape[0]
    nc = rows // HGRN_CHUNK
    l0, l1, l2 = logits[0:1, :], logits[1:2, :], logits[2:3, :]
    mx = jnp.maximum(jnp.maximum(l0, l1), l2)
    e0, e1, e2 = jnp.exp(l0 - mx), jnp.exp(l1 - mx), jnp.exp(l2 - mx)
    lb = e0 / (e0 + e1 + e2)
    forget = lb + (1.0 - lb) * _sigmoid(fl)
    qs = q * _sigmoid(q)
    kk = 1.0 - forget
    logf = jnp.log(forget)
    bcum = cumsum_rows(logf, HGRN_CHUNK)
    c3 = lambda t: t.reshape(nc, HGRN_CHUNK, 128)
    b_last = jnp.sum(c3(logf), axis=1, keepdims=True)
    bcum3 = c3(bcum)
    q_dec = c3(qs) * jnp.exp(bcum3)
    k_dec = c3(kk) * jnp.exp(-bcum3)
    k_upd = c3(kk) * jnp.exp(b_last - bcum3)
    v3 = c3(v)
    scores = bdot(q_dec, k_dec, "nt")
    ri = lax.broadcasted_iota(jnp.int32, scores.shape, 1)
    ci = lax.broadcasted_iota(jnp.int32, scores.shape, 2)
    scores = jnp.where(ri >= ci, scores, 0.0)
    o = bdot(scores, v3, "nn")
    upd_t = bdot(v3, k_upd, "tn")
    dec = jnp.exp(b_last)
    prev = []
    for n in range(nc):
        prev.append(st)
        st = st * dec[n] + upd_t[n]
    o = o + bdot(q_dec, jnp.stack(prev), "nt")
    o = o.reshape(rows, 128)
    o = o * lax.rsqrt(jnp.mean(o * o, axis=-1, keepdims=True) + EPS) * hn
    return (o * _sigmoid(g)).astype(BF16), st


def _ffn_conv(xg, xv, cw, cb):
    cg = _causal_conv(xg, cw[0], cb[0], FFN_CONV)[HALO:]
    cv = _causal_conv(xv, cw[1], cb[1], FFN_CONV)[HALO:]
    return cg, cv


def _ffn_gate(cg, cv):
    return jax.nn.gelu(cg) * cv


class Row:
    def __init__(self, arr, cb=None, off=0):
        self.arr, self.cb, self.off = arr, cb, off

    def spec(self, tm):
        if self.cb is None:
            return pl.BlockSpec((tm, self.arr.shape[1]), lambda j, i: (i, 0))
        off = self.off
        return pl.BlockSpec((tm, self.cb), lambda j, i: (i, j + off))


class Par:
    def __init__(self, arr, kind="full", bs=None):
        self.arr, self.kind, self.bs = arr, kind, bs

    def block(self):
        if self.kind == "full":
            return self.arr.shape
        if self.kind == "col":
            return (self.arr.shape[0], self.bs)
        return (self.bs, self.arr.shape[1])

    def spec(self):
        if self.kind == "full":
            return pl.BlockSpec(self.block(), lambda j, i: (0, 0))
        if self.kind == "col":
            return pl.BlockSpec(self.block(), lambda j, i: (0, j))
        return pl.BlockSpec(self.block(), lambda j, i: (j, 0))


class Out:
    def __init__(self, width, dtype, cb=None, off=0):
        self.width, self.dtype, self.cb, self.off = width, dtype, cb, off

    def spec(self, tm):
        if self.cb is None:
            return pl.BlockSpec((tm, self.width), lambda j, i: (i, 0))
        off = self.off
        return pl.BlockSpec((tm, self.cb), lambda j, i: (i, j + off))


def _params(sem):
    return pltpu.CompilerParams(dimension_semantics=sem, vmem_limit_bytes=VMEM_LIMIT)


def tile_fwd(name, fn, *, m, tm, nj, rows, pars, outs, n_acc=0):
    n_r, n_p, n_o = len(rows), len(pars), len(outs)

    def body(*refs):
        ins = [r[...] for r in refs[:n_r + n_p]]
        res = fn(*ins)
        o_refs = refs[n_r + n_p:]
        for k in range(n_o):
            o_refs[k][...] = res[k].astype(o_refs[k].dtype)
        first = jnp.logical_and(pl.program_id(0) == 0, pl.program_id(1) == 0)
        for k in range(n_acc):
            ref = o_refs[n_o + k]

            @pl.when(first)
            def _():
                ref[...] = jnp.zeros_like(ref)

            ref[...] += res[n_o + k]

    out_shape = [jax.ShapeDtypeStruct((m, o.width), o.dtype) for o in outs]
    out_specs = [o.spec(tm) for o in outs]
    for _ in range(n_acc):
        out_shape.append(jax.ShapeDtypeStruct((1, LANES), F32))
        out_specs.append(pl.BlockSpec((1, LANES), lambda j, i: (0, 0)))
    sem = ("arbitrary", "arbitrary") if n_acc else ("parallel", "parallel")
    return pl.pallas_call(
        body, grid=(nj, m // tm), name=name,
        in_specs=[r.spec(tm) for r in rows] + [p.spec() for p in pars],
        out_specs=out_specs, out_shape=out_shape, compiler_params=_params(sem),
    )(*[r.arr for r in rows], *[p.arr for p in pars])


def tile_bwd(name, fn, *, m, tm, nj, rows, pars, cts, drows):
    n_r, n_p, n_c = len(rows), len(pars), len(cts)
    want = [k for k in range(n_r) if drows[k] is not None]

    def body(*refs):
        ins = [r[...] for r in refs[:n_r + n_p]]
        ct = [r[...] for r in refs[n_r + n_p:n_r + n_p + n_c]]
        o_refs = refs[n_r + n_p + n_c:]
        res, vjp = jax.vjp(fn, *ins)
        grads = vjp(tuple(c.astype(r.dtype) for c, r in zip(ct, res)))
        for pos, k in enumerate(want):
            o_refs[pos][...] = grads[k].astype(o_refs[pos].dtype)
        for k in range(n_p):
            ref = o_refs[len(want) + k]
            first = pl.program_id(1) == 0
            if pars[k].kind == "full":
                first = jnp.logical_and(first, pl.program_id(0) == 0)

            @pl.when(first)
            def _():
                ref[...] = jnp.zeros_like(ref)

            ref[...] += grads[n_r + k].astype(F32)

    out_shape = [jax.ShapeDtypeStruct((m, drows[k].width), drows[k].dtype) for k in want]
    out_specs = [drows[k].spec(tm) for k in want]
    for p in pars:
        out_shape.append(jax.ShapeDtypeStruct(p.arr.shape, F32))
        out_specs.append(p.spec())
    return pl.pallas_call(
        body, grid=(nj, m // tm), name=name,
        in_specs=[r.spec(tm) for r in rows] + [p.spec() for p in pars] + [c.spec(tm) for c in cts],
        out_specs=out_specs, out_shape=out_shape, compiler_params=_params(("arbitrary", "arbitrary")),
    )(*[r.arr for r in rows], *[p.arr for p in pars], *[c.arr for c in cts])


class Blk:
    def __init__(self, arr, block, index):
        self.arr, self.block, self.index = arr, block, index

    def spec(self):
        return pl.BlockSpec(self.block, self.index)


def _flat2(v):
    return v if v.ndim == 2 else v.reshape(-1, v.shape[-1])


def mm(name, pat, a, b, o, out_dtype, grid, after=None):
    nk = grid[2]
    o_shape = o.arr

    def body(*refs):
        a_ref, b_ref = refs[0], refs[1]
        o_ref = refs[3] if after is not None else refs[2]
        r = _dg(_flat2(a_ref[...]), _flat2(b_ref[...]), pat)
        if nk == 1:
            o_ref[...] = r.astype(out_dtype).reshape(o_ref.shape)
            return
        acc_ref = refs[-1]
        kk = pl.program_id(2)

        @pl.when(kk == 0)
        def _():
            acc_ref[...] = r

        @pl.when(kk > 0)
        def _():
            acc_ref[...] += r

        @pl.when(kk == nk - 1)
        def _():
            o_ref[...] = acc_ref[...].astype(out_dtype).reshape(o_ref.shape)

    ob = [d for d in o.block if d is not None]
    acc_shape = (ob[0], ob[1]) if len(ob) == 2 else (ob[0] * ob[1], ob[2])
    in_specs = [a.spec(), b.spec()]
    args = [a.arr, b.arr]
    if after is not None:
        in_specs.append(pl.BlockSpec(memory_space=pl.ANY))
        args.append(after)
    return pl.pallas_call(
        body, grid=grid, name=name, in_specs=in_specs, out_specs=o.spec(),
        out_shape=jax.ShapeDtypeStruct(o_shape, out_dtype),
        scratch_shapes=[pltpu.VMEM(acc_shape, F32)] if nk > 1 else [],
        compiler_params=_params(("parallel", "parallel", "arbitrary")),
    )(*args)


def _div_tile(n, cap):
    if n <= cap:
        return n
    best = 128
    for t in range(128, cap + 1, 128):
        if n % t == 0:
            best = t
    return best


def mm2d(name, pat, a, b, out_dtype=F32):
    if pat == "tn":
        k, m = a.shape
    else:
        m, k = a.shape
    n = b.shape[0] if pat == "nt" else b.shape[1]
    tm, tn, tk = _div_tile(m, 1024), _div_tile(n, 1024), _div_tile(k, 1024)
    a_blk = Blk(a, (tk, tm), lambda i, j, kk: (kk, i)) if pat == "tn" else Blk(a, (tm, tk), lambda i, j, kk: (i, kk))
    b_blk = Blk(b, (tn, tk), lambda i, j, kk: (j, kk)) if pat == "nt" else Blk(b, (tk, tn), lambda i, j, kk: (kk, j))
    o_blk = Blk((m, n), (tm, tn), lambda i, j, kk: (i, j))
    return mm(name, pat, a_blk, b_blk, o_blk, out_dtype, (m // tm, n // tn, k // tk))


def hgrn_fwd(name, z, logits, hnorm, *, n_batch, seq):
    m = n_batch * seq
    ts = min(HGRN_SEG, seq)
    n_seg = seq // ts

    def body(q_ref, f_ref, v_ref, g_ref, lg_ref, hn_ref, o_ref, sp_ref, st_ref):
        s = pl.program_id(2)

        @pl.when(s == 0)
        def _():
            st_ref[...] = jnp.zeros_like(st_ref)

        st = st_ref[...]
        sp_ref[...] = st
        o, st_new = fn_hgrn_seg(q_ref[...], f_ref[...], v_ref[...], g_ref[...], st, lg_ref[...], hn_ref[...])
        o_ref[...] = o
        st_ref[...] = st_new

    part = lambda p: pl.BlockSpec((ts, 128), lambda h, b, s: (b * n_seg + s, 4 * p + h))
    return pl.pallas_call(
        body, grid=(A_HEADS, n_batch, n_seg), name=name,
        in_specs=[part(0), part(1), part(2), part(3),
                  pl.BlockSpec((3, 128), lambda h, b, s: (0, h)),
                  pl.BlockSpec((1, 128), lambda h, b, s: (0, h))],
        out_specs=[pl.BlockSpec((ts, 128), lambda h, b, s: (b * n_seg + s, h)),
                   pl.BlockSpec((128, 128), lambda h, b, s: ((b * n_seg + s) * A_HEADS + h, 0))],
        out_shape=[jax.ShapeDtypeStruct((m, A_WIDTH), BF16),
                   jax.ShapeDtypeStruct((n_batch * n_seg * A_HEADS * 128, 128), F32)],
        scratch_shapes=[pltpu.VMEM((128, 128), F32)],
        compiler_params=_params(("arbitrary", "arbitrary", "arbitrary")),
    )(z, z, z, z, logits, hnorm)


def hgrn_bwd(name, z, sprev, logits, hnorm, do, *, n_batch, seq):
    m = n_batch * seq
    ts = min(HGRN_SEG, seq)
    n_seg = seq // ts

    def body(q_ref, f_ref, v_ref, g_ref, sp_ref, lg_ref, hn_ref, do_ref, dq_ref, df_ref, dv_ref, dg_ref, dlg_ref, dhn_ref, dst_ref):
        s = pl.program_id(2)

        @pl.when(s == 0)
        def _():
            dst_ref[...] = jnp.zeros_like(dst_ref)

        res, vjp = jax.vjp(fn_hgrn_seg, q_ref[...], f_ref[...], v_ref[...], g_ref[...], sp_ref[...], lg_ref[...], hn_ref[...])
        dq, df, dv, dg, dst, dlg, dhn = vjp((do_ref[...].astype(res[0].dtype), dst_ref[...]))
        dq_ref[...] = dq.astype(dq_ref.dtype)
        df_ref[...] = df.astype(df_ref.dtype)
        dv_ref[...] = dv.astype(dv_ref.dtype)
        dg_ref[...] = dg.astype(dg_ref.dtype)
        dst_ref[...] = dst
        first = jnp.logical_and(pl.program_id(1) == 0, s == 0)

        @pl.when(first)
        def _():
            dlg_ref[...] = jnp.zeros_like(dlg_ref)
            dhn_ref[...] = jnp.zeros_like(dhn_ref)

        dlg_ref[...] += dlg
        dhn_ref[...] += dhn

    rev = lambda b, s: b * n_seg + (n_seg - 1 - s)
    part = lambda p: pl.BlockSpec((ts, 128), lambda h, b, s: (rev(b, s), 4 * p + h))
    head = pl.BlockSpec((ts, 128), lambda h, b, s: (rev(b, s), h))
    dpart = jax.ShapeDtypeStruct((m, A_WIDTH), BF16)
    return pl.pallas_call(
        body, grid=(A_HEADS, n_batch, n_seg), name=name,
        in_specs=[part(0), part(1), part(2), part(3),
                  pl.BlockSpec((128, 128), lambda h, b, s: (rev(b, s) * A_HEADS + h, 0)),
                  pl.BlockSpec((3, 128), lambda h, b, s: (0, h)),
                  pl.BlockSpec((1, 128), lambda h, b, s: (0, h)),
                  head],
        out_specs=[head, head, head, head,
                   pl.BlockSpec((3, 128), lambda h, b, s: (0, h)),
                   pl.BlockSpec((1, 128), lambda h, b, s: (0, h))],
        out_shape=[dpart, dpart, dpart, dpart,
                   jax.ShapeDtypeStruct(logits.shape, F32),
                   jax.ShapeDtypeStruct(hnorm.shape, F32)],
        scratch_shapes=[pltpu.VMEM((128, 128), F32)],
        compiler_params=_params(("arbitrary", "arbitrary", "arbitrary")),
    )(z, z, z, z, sprev, logits, hnorm, do)


def _ffn_tiles(m, seq):
    tm = min(512, seq)
    return tm, seq // tm, m // tm


def ffn_mid_fwd(name, hid, cw, cb, layer, *, m, seq):
    tm, n_t, n_i = _ffn_tiles(m, seq)
    hb = tm // HALO

    def body(x_ref, xb_ref, cw_ref, cb_ref, o_ref):
        first = pl.program_id(1) % n_t == 0
        before = jnp.where(first, 0.0, xb_ref[...])
        ext = jnp.concatenate([before, x_ref[...]], axis=1)
        cg, cv = _ffn_conv(ext[0], ext[1], cw_ref[...], cb_ref[...])
        o_ref[...] = _ffn_gate(cg, cv).astype(o_ref.dtype)

    return pl.pallas_call(
        body, grid=(N_DEV // 2, n_i), name=name,
        in_specs=[pl.BlockSpec((2, None, tm, FF_BLK), lambda d, i: (0, d, i, 0)),
                  pl.BlockSpec((2, None, HALO, FF_BLK), lambda d, i: (0, d, jnp.maximum(i * hb - 1, 0), 0)),
                  pl.BlockSpec((2, None, None, FFN_CONV, FF_BLK), lambda d, i: (0, d, layer, 0, 0)),
                  pl.BlockSpec((None, 2, None, 1, FF_BLK), lambda d, i: (layer, 0, d, 0, 0))],
        out_specs=pl.BlockSpec((None, tm, FF_BLK), lambda d, i: (d, i, 0)),
        out_shape=jax.ShapeDtypeStruct((N_DEV // 2, m, FF_BLK), BF16),
        compiler_params=_params(("parallel", "parallel")),
    )(hid, hid, cw, cb)


def ffn_mid_bwd(name, hid, cw, cb, dact, layer, *, m, seq):
    tm, n_t, n_i = _ffn_tiles(m, seq)
    hb = tm // HALO
    last_blk = m // HALO - 1

    def body(x_ref, xb_ref, xa_ref, cw_ref, cb_ref, da_ref, daa_ref, dx_ref, dcw_ref, dcb_ref):
        i = pl.program_id(1)
        first = i % n_t == 0
        last = i % n_t == n_t - 1
        before = jnp.where(first, 0.0, xb_ref[...])
        ext = jnp.concatenate([before, x_ref[...], xa_ref[...]], axis=1)
        dact_ext = jnp.concatenate([da_ref[...].astype(F32), jnp.where(last, 0.0, daa_ref[...].astype(F32))], axis=0)
        (cg, cv), vjp_conv = jax.vjp(_ffn_conv, ext[0], ext[1], cw_ref[...], cb_ref[...])
        _, vjp_gate = jax.vjp(_ffn_gate, cg, cv)
        dcg, dcv = vjp_gate(dact_ext)
        dxg, dxv, _, _ = vjp_conv((dcg, dcv))
        dx_ref[0] = dxg[HALO:HALO + tm].astype(dx_ref.dtype)
        dx_ref[1] = dxv[HALO:HALO + tm].astype(dx_ref.dtype)
        own = lax.broadcasted_iota(jnp.int32, dcg.shape, 0) < tm
        _, _, dcw, dcb = vjp_conv((jnp.where(own, dcg, 0.0), jnp.where(own, dcv, 0.0)))

        @pl.when(i == 0)
        def _():
            dcw_ref[...] = jnp.zeros_like(dcw_ref)
            dcb_ref[...] = jnp.zeros_like(dcb_ref)

        dcw_ref[...] += dcw
        dcb_ref[...] += dcb

    return pl.pallas_call(
        body, grid=(N_DEV // 2, n_i), name=name,
        in_specs=[pl.BlockSpec((2, None, tm, FF_BLK), lambda d, i: (0, d, i, 0)),
                  pl.BlockSpec((2, None, HALO, FF_BLK), lambda d, i: (0, d, jnp.maximum(i * hb - 1, 0), 0)),
                  pl.BlockSpec((2, None, HALO, FF_BLK), lambda d, i: (0, d, jnp.minimum((i + 1) * hb, last_blk), 0)),
                  pl.BlockSpec((2, None, None, FFN_CONV, FF_BLK), lambda d, i: (0, d, layer, 0, 0)),
                  pl.BlockSpec((None, 2, None, 1, FF_BLK), lambda d, i: (layer, 0, d, 0, 0)),
                  pl.BlockSpec((None, tm, FF_BLK), lambda d, i: (d, i, 0)),
                  pl.BlockSpec((None, HALO, FF_BLK), lambda d, i: (d, jnp.minimum((i + 1) * hb, last_blk), 0))],
        out_specs=[pl.BlockSpec((2, None, tm, FF_BLK), lambda d, i: (0, d, i, 0)),
                   pl.BlockSpec((2, None, FFN_CONV, FF_BLK), lambda d, i: (0, d, 0, 0)),
                   pl.BlockSpec((2, None, 1, FF_BLK), lambda d, i: (0, d, 0, 0))],
        out_shape=[jax.ShapeDtypeStruct((2, N_DEV // 2, m, FF_BLK), BF16),
                   jax.ShapeDtypeStruct((2, N_DEV // 2, FFN_CONV, FF_BLK), F32),
                   jax.ShapeDtypeStruct((2, N_DEV // 2, 1, FF_BLK), F32)],
        compiler_params=_params(("arbitrary", "arbitrary")),
    )(hid, hid, hid, cw, cb, dact, dact)


ATT_BLK = 512
N_PAIR = C_HEADS // 2
TERM_W = C_HEADS * LANES


def term_placement():
    import numpy as np
    place = np.zeros((6, LANES, TERM_W), np.float32)
    ones_q = np.zeros((1, TERM_W), np.float32)
    ones_k = np.zeros((1, TERM_W), np.float32)
    for h in range(C_HEADS):
        for j in range(3):
            place[j, h, h * LANES + C_HEAD_DIM + j] = 1.0
            place[3 + j, h, h * LANES + C_HEAD_DIM + 3 + j] = 1.0
            ones_q[0, h * LANES + C_HEAD_DIM + 3 + j] = 1.0
            ones_k[0, h * LANES + C_HEAD_DIM + j] = 1.0
    return (jnp.asarray(place.reshape(6 * LANES, TERM_W), BF16), jnp.asarray(ones_q, F32), jnp.asarray(ones_k, F32))


def fn_fox_terms(c, place, ones_q, ones_k):
    parts = _split3(c)
    qt = ones_q
    kt = ones_k
    for j in range(3):
        qt = qt + _dg(parts[j], place[j * LANES:(j + 1) * LANES], "nn")
        kt = kt - _dg(parts[j], place[(3 + j) * LANES:(4 + j) * LANES], "nn")
    return qt.astype(BF16), kt.astype(BF16)


def _head_tile(z, terms, e):
    lane = lax.broadcasted_iota(jnp.int32, z.shape, 1)
    base = z if e == 0 else pltpu.roll(z, C_HEAD_DIM, 1)
    return jnp.where(lane < C_HEAD_DIM, base, terms.astype(z.dtype))


def _head_only(z, e):
    lane = lax.broadcasted_iota(jnp.int32, z.shape, 1)
    mine = (lane < C_HEAD_DIM) if e == 0 else (lane >= C_HEAD_DIM)
    return jnp.where(mine, z, jnp.zeros_like(z)).astype(BF16)


def _pair_tile(a0, a1):
    lane = lax.broadcasted_iota(jnp.int32, a0.shape, 1)
    return jnp.where(lane < C_HEAD_DIM, a0, pltpu.roll(a1, C_HEAD_DIM, 1))


def _lane_col(a, k):
    lane = lax.broadcasted_iota(jnp.int32, a.shape, 1)
    return jnp.sum(jnp.where(lane == k, a, 0.0), axis=1, keepdims=True)


def _causal(s):
    key = lax.broadcasted_iota(jnp.int32, s.shape, 0)
    qry = lax.broadcasted_iota(jnp.int32, s.shape, 1)
    return qry >= key


def fox_pair_fwd(name, z, qterm, kterm, *, n_batch, seq):
    m = n_batch * seq
    blk = min(ATT_BLK, seq)
    nq = seq // blk
    dh = C_HEAD_DIM

    def body(zq_ref, zk_ref, zv_ref, qt_ref, kt_ref, o_ref, lse_ref, ka_ref, vt_ref):
        qi = pl.program_id(2)

        @pl.when(qi == 0)
        def _():
            zk = zk_ref[...]
            for e in range(2):
                ka_ref[e] = _head_tile(zk, kt_ref[:, e * LANES:(e + 1) * LANES], e).astype(BF16)
            for cb in range(nq):
                vt_ref[cb] = zv_ref[cb * blk:(cb + 1) * blk, :].T.astype(BF16)

        zq = zq_ref[...] * dh ** -0.5
        qa = [_head_tile(zq, qt_ref[:, e * LANES:(e + 1) * LANES], e).astype(BF16) for e in range(2)]

        def block(j, carry, diagonal):
            rows = pl.ds(pl.multiple_of(j * blk, blk), blk)
            out = []
            for e in range(2):
                mx, l, acc = carry[e]
                s = _dg(ka_ref[e, rows, :], qa[e], "nt")
                if diagonal:
                    s = jnp.where(_causal(s), s, NEG)
                mx_new = jnp.maximum(mx, jnp.max(s, axis=0, keepdims=True))
                p = jnp.exp(s - mx_new)
                alpha = jnp.exp(mx - mx_new)
                l = alpha * l + jnp.sum(p, axis=0, keepdims=True)
                acc = alpha * acc + _dg(vt_ref[j, e * dh:(e + 1) * dh, :], p, "nn")
                out.append((mx_new, l, acc))
            return tuple(out)

        one = (jnp.full((1, blk), NEG, F32), jnp.zeros((1, blk), F32), jnp.zeros((dh, blk), F32))
        carry = lax.fori_loop(0, qi, lambda j, cr: block(j, cr, False), (one, one))
        res = block(qi, carry, True)
        ot = jnp.concatenate([res[e][2] / res[e][1] for e in range(2)], axis=0)
        o_ref[...] = ot.T.astype(o_ref.dtype)
        for e in range(2):
            lse_ref[e] = res[e][0] + jnp.log(res[e][1])

    col = lambda part: (lambda b, g, i: (b, part * N_PAIR + g))
    return pl.pallas_call(
        body, grid=(n_batch, N_PAIR, nq), name=name,
        in_specs=[pl.BlockSpec((blk, LANES), lambda b, g, i: (b * nq + i, g)),
                  pl.BlockSpec((seq, LANES), col(1)),
                  pl.BlockSpec((seq, LANES), col(2)),
                  pl.BlockSpec((blk, 2 * LANES), lambda b, g, i: (b * nq + i, g)),
                  pl.BlockSpec((seq, 2 * LANES), lambda b, g, i: (b, g))],
        out_specs=[pl.BlockSpec((blk, LANES), lambda b, g, i: (b * nq + i, g)),
                   pl.BlockSpec((None, None, None, 2, 1, blk), lambda b, g, i: (b, g, i, 0, 0, 0))],
        out_shape=[jax.ShapeDtypeStruct((m, D_MODEL), BF16), jax.ShapeDtypeStruct((n_batch, N_PAIR, nq, 2, 1, blk), F32)],
        scratch_shapes=[pltpu.VMEM((2, seq, LANES), BF16), pltpu.VMEM((nq, LANES, blk), BF16)],
        compiler_params=_params(("parallel", "parallel", "arbitrary")),
    )(z, z, z, qterm, kterm)


def fox_pair_bwd(name, z, qterm, kterm, o, do, lse, *, n_batch, seq):
    m = n_batch * seq
    blk = min(ATT_BLK, seq)
    nq = seq // blk
    dh = C_HEAD_DIM

    def body(zq_ref, zk_ref, zv_ref, qt_ref, kt_ref, o_ref, do_ref, lse_ref, dq_ref, dk_ref, dv_ref, dc_ref,
             qa_ref, doh_ref, del_ref, dqt_ref, dk_acc, dv_acc):
        g, j = pl.program_id(1), pl.program_id(2)
        lane = lax.broadcasted_iota(jnp.int32, (blk, LANES), 1)

        @pl.when(jnp.logical_and(g == 0, j == 0))
        def _():
            dc_ref[...] = jnp.zeros_like(dc_ref)

        @pl.when(j == 0)
        def _():
            zq = zq_ref[...] * dh ** -0.5
            dov = do_ref[...]
            for e in range(2):
                qa_ref[e] = _head_tile(zq, qt_ref[:, e * LANES:(e + 1) * LANES], e).astype(BF16)
                doh_ref[e] = _head_only(dov, e)
            for cb in range(nq):
                rows = slice(cb * blk, (cb + 1) * blk)
                prod_t = (do_ref[rows, :].astype(F32) * o_ref[rows, :].astype(F32)).T
                for e in range(2):
                    del_ref[cb, e] = jnp.sum(prod_t[e * dh:(e + 1) * dh], axis=0, keepdims=True)
            dqt_ref[...] = jnp.zeros_like(dqt_ref)

        zk, zv = zk_ref[...], zv_ref[...]
        ka32 = [_head_tile(zk, kt_ref[:, e * LANES:(e + 1) * LANES], e) for e in range(2)]
        ka = [t.astype(BF16) for t in ka32]
        kat = [t.T.astype(BF16) for t in ka32]
        vh = [_head_only(zv, e) for e in range(2)]
        dk_acc[...] = jnp.zeros_like(dk_acc)
        dv_acc[...] = jnp.zeros_like(dv_acc)

        def block(i, diagonal):
            rows = pl.ds(pl.multiple_of(i * blk, blk), blk)
            for e in range(2):
                qv, dov = qa_ref[e, rows, :], doh_ref[e, rows, :]
                p = jnp.exp(_dg(ka[e], qv, "nt") - lse_ref[i, e])
                if diagonal:
                    p = jnp.where(_causal(p), p, 0.0)
                dv_acc[...] += _dg(p, dov, "nn")
                ds = p * (_dg(vh[e], dov, "nt") - del_ref[i, e])
                dk_acc[e] += _dg(ds, qv, "nn")
                dqt_ref[i, e] += _dg(kat[e], ds, "nn")

        block(j, True)

        def rest(i, carry):
            block(i, False)
            return carry

        lax.fori_loop(j + 1, nq, rest, 0)
        dk0, dk1 = dk_acc[0], dk_acc[1]
        dk_ref[...] = _pair_tile(dk0, dk1).astype(dk_ref.dtype)
        dv_ref[...] = dv_acc[...].astype(dv_ref.dtype)
        rows_j = pl.ds(pl.multiple_of(j * blk, blk), blk)
        for e, dke in enumerate((dk0, dk1)):
            dc_ref[rows_j, :] -= jnp.where(lane == 2 * g + e, _lane_col(dke, dh + 3), 0.0)

        @pl.when(j == nq - 1)
        def _():
            for i in range(nq):
                nat = [dqt_ref[i, e].T for e in range(2)]
                rows = slice(i * blk, (i + 1) * blk)
                dq_ref[rows, :] = (_pair_tile(nat[0], nat[1]) * dh ** -0.5).astype(dq_ref.dtype)
                for e in range(2):
                    dc_ref[rows, :] += jnp.where(lane == 2 * g + e, _lane_col(nat[e], dh), 0.0)

    col = lambda part: (lambda b, g, j: (b, part * N_PAIR + g))
    colj = lambda part: (lambda b, g, j: (b * nq + j, part * N_PAIR + g))
    pair = jax.ShapeDtypeStruct((m, D_MODEL), BF16)
    return pl.pallas_call(
        body, grid=(n_batch, N_PAIR, nq), name=name,
        in_specs=[pl.BlockSpec((seq, LANES), col(0)),
                  pl.BlockSpec((blk, LANES), colj(1)),
                  pl.BlockSpec((blk, LANES), colj(2)),
                  pl.BlockSpec((seq, 2 * LANES), lambda b, g, j: (b, g)),
                  pl.BlockSpec((blk, 2 * LANES), lambda b, g, j: (b * nq + j, g)),
                  pl.BlockSpec((seq, LANES), col(0)),
                  pl.BlockSpec((seq, LANES), col(0)),
                  pl.BlockSpec((None, None, nq, 2, 1, blk), lambda b, g, j: (b, g, 0, 0, 0, 0))],
        out_specs=[pl.BlockSpec((seq, LANES), col(0)),
                   pl.BlockSpec((blk, LANES), colj(0)),
                   pl.BlockSpec((blk, LANES), colj(0)),
                   pl.BlockSpec((seq, LANES), lambda b, g, j: (b, 0))],
        out_shape=[pair, pair, pair, jax.ShapeDtypeStruct((m, LANES), F32)],
        scratch_shapes=[pltpu.VMEM((2, seq, LANES), BF16), pltpu.VMEM((2, seq, LANES), BF16),
                        pltpu.VMEM((nq, 2, 1, blk), F32), pltpu.VMEM((nq, 2, LANES, blk), F32),
                        pltpu.VMEM((2, blk, LANES), F32), pltpu.VMEM((blk, LANES), F32)],
        compiler_params=_params(("arbitrary", "arbitrary", "arbitrary")),
    )(z, z, z, qterm, kterm, o, do, lse)


def _split3(c):
    c1 = c.astype(BF16)
    r1 = c - c1.astype(F32)
    c2 = r1.astype(BF16)
    c3 = (r1 - c2.astype(F32)).astype(BF16)
    return c1, c2, c3


def fox_operands(q, k, c):
    bh, seq, dh = q.shape
    c1, c2, c3 = (t[..., None] for t in _split3(c))
    one = jnp.ones((bh, seq, 1), BF16)
    pad = jnp.zeros((bh, seq, LANES - dh - 6), BF16)
    qa = jnp.concatenate([(q * dh ** -0.5).astype(BF16), c1, c2, c3, one, one, one, pad], axis=-1)
    ka = jnp.concatenate([k.astype(BF16), one, one, one, -c1, -c2, -c3, pad], axis=-1)
    return qa, ka


def fox_fwd(name, qa, ka, vt):
    bh, seq, da = qa.shape
    blk = min(ATT_BLK, seq)
    nq = seq // blk
    dh = vt.shape[2]

    def body(q_ref, k_ref, v_ref, o_ref, lse_ref):
        qi = pl.program_id(1)
        qv = q_ref[0]

        def block(j, carry, diagonal):
            mx, l, acc = carry
            kj = k_ref[0, pl.ds(pl.multiple_of(j * blk, blk), blk), :]
            s = _dg(kj, qv, "nt")
            if diagonal:
                key = lax.broadcasted_iota(jnp.int32, (blk, blk), 0)
                qry = lax.broadcasted_iota(jnp.int32, (blk, blk), 1)
                s = jnp.where(qry >= key, s, NEG)
            mx_new = jnp.maximum(mx, jnp.max(s, axis=0, keepdims=True))
            p = jnp.exp(s - mx_new)
            alpha = jnp.exp(mx - mx_new)
            l = alpha * l + jnp.sum(p, axis=0, keepdims=True)
            acc = alpha * acc + _dg(v_ref[0, j], p, "nn")
            return mx_new, l, acc

        init = (jnp.full((1, blk), NEG, F32), jnp.zeros((1, blk), F32), jnp.zeros((dh, blk), F32))
        carry = lax.fori_loop(0, qi, lambda j, cr: block(j, cr, False), init)
        mx, l, acc = block(qi, carry, True)
        o_ref[0] = (acc / l).astype(o_ref.dtype)
        lse_ref[0, 0] = mx + jnp.log(l)

    return pl.pallas_call(
        body, grid=(bh, nq), name=name,
        in_specs=[pl.BlockSpec((1, blk, da), lambda b, i: (b, i, 0)),
                  pl.BlockSpec((1, seq, da), lambda b, i: (b, 0, 0)),
                  pl.BlockSpec((1, nq, dh, blk), lambda b, i: (b, 0, 0, 0))],
        out_specs=[pl.BlockSpec((1, dh, blk), lambda b, i: (b, 0, i)),
                   pl.BlockSpec((1, 1, 1, blk), lambda b, i: (b, i, 0, 0))],
        out_shape=[jax.ShapeDtypeStruct((bh, dh, seq), BF16), jax.ShapeDtypeStruct((bh, nq, 1, blk), F32)],
        compiler_params=_params(("parallel", "arbitrary")),
    )(qa, ka, vt)


def fox_bwd(name, qa, ka, kat, v, do, dot, ot, lse):
    bh, seq, da = qa.shape
    blk = min(ATT_BLK, seq)
    nq = seq // blk
    dh = v.shape[2]

    def body(q_ref, k_ref, kt_ref, v_ref, do_ref, dot_ref, ot_ref, lse_ref, dq_ref, dk_ref, dv_ref, del_ref):
        j = pl.program_id(1)

        @pl.when(j == 0)
        def _():
            dq_ref[...] = jnp.zeros_like(dq_ref)
            for i in range(nq):
                cols = slice(i * blk, (i + 1) * blk)
                del_ref[i] = jnp.sum(dot_ref[0, :, cols].astype(F32) * ot_ref[0, :, cols].astype(F32), axis=0, keepdims=True)

        kj, kjt, vj = k_ref[0], kt_ref[0, 0], v_ref[0]

        def block(i, carry, diagonal):
            dk, dv = carry
            rows = pl.ds(pl.multiple_of(i * blk, blk), blk)
            qv, dov = q_ref[0, rows, :], do_ref[0, rows, :]
            p = jnp.exp(_dg(kj, qv, "nt") - lse_ref[0, i])
            if diagonal:
                key = lax.broadcasted_iota(jnp.int32, (blk, blk), 0)
                qry = lax.broadcasted_iota(jnp.int32, (blk, blk), 1)
                p = jnp.where(qry >= key, p, 0.0)
            dv = dv + _dg(p, dov, "nn")
            ds = p * (_dg(vj, dov, "nt") - del_ref[i])
            dk = dk + _dg(ds, qv, "nn")
            dq_ref[0, i] += _dg(kjt, ds, "nn")
            return dk, dv

        init = (jnp.zeros((blk, da), F32), jnp.zeros((blk, dh), F32))
        carry = block(j, init, True)
        dk, dv = lax.fori_loop(j + 1, nq, lambda i, cr: block(i, cr, False), carry)
        dk_ref[0] = dk
        dv_ref[0] = dv

    return pl.pallas_call(
        body, grid=(bh, nq), name=name,
        in_specs=[pl.BlockSpec((1, seq, da), lambda b, j: (b, 0, 0)),
                  pl.BlockSpec((1, blk, da), lambda b, j: (b, j, 0)),
                  pl.BlockSpec((1, 1, da, blk), lambda b, j: (b, j, 0, 0)),
                  pl.BlockSpec((1, blk, dh), lambda b, j: (b, j, 0)),
                  pl.BlockSpec((1, seq, dh), lambda b, j: (b, 0, 0)),
                  pl.BlockSpec((1, dh, seq), lambda b, j: (b, 0, 0)),
                  pl.BlockSpec((1, dh, seq), lambda b, j: (b, 0, 0)),
                  pl.BlockSpec((1, nq, 1, blk), lambda b, j: (b, 0, 0, 0))],
        out_specs=[pl.BlockSpec((1, nq, da, blk), lambda b, j: (b, 0, 0, 0)),
                   pl.BlockSpec((1, blk, da), lambda b, j: (b, j, 0)),
                   pl.BlockSpec((1, blk, dh), lambda b, j: (b, j, 0))],
        out_shape=[jax.ShapeDtypeStruct((bh, nq, da, blk), F32), jax.ShapeDtypeStruct((bh, seq, da), F32),
                   jax.ShapeDtypeStruct((bh, seq, dh), F32)],
        scratch_shapes=[pltpu.VMEM((nq, 1, blk), F32)],
        compiler_params=_params(("parallel", "arbitrary")),
    )(qa, ka, kat, v, do, dot, ot, lse)


def _mesh_pos():
    return lax.axis_index("x"), lax.axis_index("y"), lax.axis_index("c")


def _flip(v, bit):
    return 1 - v if bit else v


def all_gather(name, blocks):
    n = len(blocks)

    def body(*refs):
        x_refs, out_refs = refs[:n], refs[n:2 * n]
        send_sems, recv_sems, local_sems = refs[2 * n:]
        x, y, c = _mesh_pos()
        me, sibling = (x, y, c), (x, y, 1 - c)
        chips = [(1 - x, y), (x, 1 - y), (1 - x, 1 - y)]

        def slot(a, px, py, pc):
            return out_refs[a].at[4 * px + 2 * py + pc]

        def copy(a, k, blk, to, src=None):
            return pltpu.make_async_remote_copy(
                src_ref=slot(a, *blk) if src is None else src, dst_ref=slot(a, *blk),
                send_sem=send_sems.at[a, k], recv_sem=recv_sems.at[a, k], device_id=to, device_id_type=MESH)

        mine = [pltpu.make_async_copy(x_refs[a], slot(a, *me), local_sems.at[a]) for a in range(n)]
        for cp in mine:
            cp.start()
        sends = []
        for a in range(n):
            sends.append(copy(a, 0, me, sibling, src=x_refs[a]))
            sends += [copy(a, 1 + j, me, (*chip, c), src=x_refs[a]) for j, chip in enumerate(chips)]
        for cp in sends:
            cp.start()
        for j, chip in enumerate(chips):
            for a in range(n):
                copy(a, 1 + j, (*chip, c), me).wait_recv()
                passed = copy(a, 4 + j, (*chip, c), sibling)
                passed.start()
                sends.append(passed)
        for a in range(n):
            copy(a, 0, sibling, me).wait_recv()
            for j, chip in enumerate(chips):
                copy(a, 4 + j, (*chip, 1 - c), me).wait_recv()
        for cp in sends:
            cp.wait_send()
        for cp in mine:
            cp.wait()

    hbm = pl.BlockSpec(memory_space=pl.ANY)
    return pl.pallas_call(
        body, name=name, out_shape=[jax.ShapeDtypeStruct((N_DEV,) + b.shape, b.dtype) for b in blocks],
        in_specs=[hbm] * n, out_specs=[hbm] * n,
        scratch_shapes=[pltpu.SemaphoreType.DMA((n, 7)), pltpu.SemaphoreType.DMA((n, 7)), pltpu.SemaphoreType.DMA((n,))],
    )(*blocks)


def _peers(x, y, c):
    return [(_flip(x, k & 4), _flip(y, k & 2), _flip(c, k & 1)) for k in range(1, N_DEV)]


def gather_start(name, blocks, lands):
    n = len(blocks)

    def body(*refs):
        x_refs, land_refs = refs[:n], refs[n:2 * n]
        send_sems, recv_sems = refs[2 * n], refs[2 * n + 1]
        token = refs[-1]
        x, y, c = _mesh_pos()
        me = 4 * x + 2 * y + c
        for k, peer in enumerate(_peers(x, y, c)):
            for a in range(n):
                pltpu.make_async_remote_copy(
                    src_ref=x_refs[a], dst_ref=land_refs[a].at[me], send_sem=send_sems.at[7 * a + k], recv_sem=recv_sems.at[7 * a + k],
                    device_id=peer, device_id_type=MESH).start()
        token[...] = jnp.zeros_like(token)

    hbm = pl.BlockSpec(memory_space=pltpu.HBM)
    sem = pl.BlockSpec(memory_space=pltpu.SEMAPHORE)
    out_shape = ([pltpu.SemaphoreType.DMA((7 * n,)), pltpu.SemaphoreType.DMA((7 * n,))]
                 + [pltpu.HBM(b.shape, b.dtype) for b in blocks] + [pltpu.HBM(l.shape, l.dtype) for l in lands]
                 + [jax.ShapeDtypeStruct((8, LANES), F32)])
    res = pl.pallas_call(
        body, name=name, out_shape=out_shape, in_specs=[hbm] * (2 * n),
        out_specs=[sem, sem] + [hbm] * (2 * n) + [pl.BlockSpec(memory_space=pltpu.VMEM)],
        input_output_aliases={a: 2 + a for a in range(2 * n)},
        compiler_params=pltpu.CompilerParams(has_side_effects=pltpu.SideEffectType.DATAFLOW_SIDE_EFFECTING),
    )(*[pltpu.with_memory_space_constraint(b, pltpu.HBM) for b in blocks],
      *[pltpu.with_memory_space_constraint(l, pltpu.HBM) for l in lands])
    return res[0], res[1], res[2:2 + n], res[2 + n:2 + 2 * n], res[-1]


def gather_wait(name, send_sems, recv_sems, blocks, lands, after):
    n = len(blocks)

    def body(*refs):
        x_refs, land_refs = refs[:n], refs[n:2 * n]
        s_sems, r_sems = refs[2 * n], refs[2 * n + 1]
        x, y, c = _mesh_pos()
        me = 4 * x + 2 * y + c
        for k, peer in enumerate(_peers(x, y, c)):
            for a in range(n):
                cp = pltpu.make_async_remote_copy(
                    src_ref=x_refs[a], dst_ref=land_refs[a].at[me], send_sem=s_sems.at[7 * a + k], recv_sem=r_sems.at[7 * a + k],
                    device_id=peer, device_id_type=MESH)
                cp.wait_send()
                cp.wait_recv()

    hbm = pl.BlockSpec(memory_space=pltpu.HBM)
    sem = pl.BlockSpec(memory_space=pltpu.SEMAPHORE)
    res = pl.pallas_call(
        body, name=name,
        out_shape=[pltpu.HBM(b.shape, b.dtype) for b in blocks] + [pltpu.HBM(l.shape, l.dtype) for l in lands],
        in_specs=[hbm] * (2 * n) + [sem, sem, pl.BlockSpec(memory_space=pl.ANY)], out_specs=[hbm] * (2 * n),
        input_output_aliases={a: a for a in range(2 * n)},
        compiler_params=pltpu.CompilerParams(has_side_effects=pltpu.SideEffectType.DATAFLOW_SIDE_EFFECTING),
    )(*blocks, *lands, send_sems, recv_sems, after)
    return res[n:]


def _split_exchange(name, sends, lands, sems, after):
    n = len(sends)
    starting = sems is None

    def body(*refs):
        s_refs, l_refs = refs[:n], refs[n:2 * n]
        send_sems, recv_sems = refs[2 * n], refs[2 * n + 1]
        x, y, c = _mesh_pos()
        me = 4 * x + 2 * y + c
        for k, (px, py, pc) in enumerate(_peers(x, y, c)):
            for a in range(n):
                cp = pltpu.make_async_remote_copy(
                    src_ref=s_refs[a].at[4 * px + 2 * py + pc], dst_ref=l_refs[a].at[me],
                    send_sem=send_sems.at[7 * a + k], recv_sem=recv_sems.at[7 * a + k],
                    device_id=(px, py, pc), device_id_type=MESH)
                if starting:
                    cp.start()
                else:
                    cp.wait_send()
                    cp.wait_recv()
        if starting:
            refs[-1][...] = jnp.zeros_like(refs[-1])

    hbm = pl.BlockSpec(memory_space=pltpu.HBM)
    sem = pl.BlockSpec(memory_space=pltpu.SEMAPHORE)
    thru = [pltpu.HBM(t.shape, t.dtype) for t in list(sends) + list(lands)]
    effect = pltpu.CompilerParams(has_side_effects=pltpu.SideEffectType.DATAFLOW_SIDE_EFFECTING)
    if starting:
        res = pl.pallas_call(
            body, name=name, in_specs=[hbm] * (2 * n),
            out_shape=[pltpu.SemaphoreType.DMA((7 * n,)), pltpu.SemaphoreType.DMA((7 * n,))] + thru + [jax.ShapeDtypeStruct((8, LANES), F32)],
            out_specs=[sem, sem] + [hbm] * (2 * n) + [pl.BlockSpec(memory_space=pltpu.VMEM)],
            input_output_aliases={a: 2 + a for a in range(2 * n)}, compiler_params=effect,
        )(*[pltpu.with_memory_space_constraint(t, pltpu.HBM) for t in list(sends) + list(lands)])
        return res[0], res[1], res[2:2 + n], res[2 + n:2 + 2 * n], res[-1]
    res = pl.pallas_call(
        body, name=name, out_shape=thru, in_specs=[hbm] * (2 * n) + [sem, sem, pl.BlockSpec(memory_space=pl.ANY)],
        out_specs=[hbm] * (2 * n), input_output_aliases={a: a for a in range(2 * n)}, compiler_params=effect,
    )(*sends, *lands, sems[0], sems[1], after)
    return res[n:]


def own_slot_only(send, me):
    mine = lax.dynamic_index_in_dim(send, me, 0, keepdims=False)
    return lax.dynamic_update_index_in_dim(lax.empty(send.shape, send.dtype), mine, me, 0)


def all_to_all(name, sends):
    n = len(sends)

    def body(*refs):
        s_refs, r_refs = refs[:n], refs[n:2 * n]
        send_sems, recv_sems, local_sems = refs[2 * n:]
        x, y, c = _mesh_pos()
        me = 4 * x + 2 * y + c
        mine = [pltpu.make_async_copy(s_refs[a].at[me], r_refs[a].at[me], local_sems.at[a]) for a in range(n)]
        for cp in mine:
            cp.start()
        copies = []
        for k in range(1, N_DEV):
            px, py, pc = _flip(x, k & 4), _flip(y, k & 2), _flip(c, k & 1)
            for a in range(n):
                copies.append(pltpu.make_async_remote_copy(
                    src_ref=s_refs[a].at[4 * px + 2 * py + pc], dst_ref=r_refs[a].at[me],
                    send_sem=send_sems.at[a, k - 1], recv_sem=recv_sems.at[a, k - 1],
                    device_id=(px, py, pc), device_id_type=MESH))
        for cp in copies:
            cp.start()
        for cp in copies:
            cp.wait_recv()
        for cp in copies:
            cp.wait_send()
        for cp in mine:
            cp.wait()

    hbm = pl.BlockSpec(memory_space=pl.ANY)
    return pl.pallas_call(
        body, name=name, out_shape=[jax.ShapeDtypeStruct(s.shape, s.dtype) for s in sends],
        in_specs=[hbm] * n, out_specs=[hbm] * n,
        scratch_shapes=[pltpu.SemaphoreType.DMA((n, 7)), pltpu.SemaphoreType.DMA((n, 7)), pltpu.SemaphoreType.DMA((n,))],
    )(*sends)


def pair_exchange(name, hs):
    n = len(hs)

    def body(*refs):
        h_refs, r_refs = refs[:n], refs[n:2 * n]
        send_sems, recv_sems = refs[2 * n:]
        x, y, c = _mesh_pos()
        copies = [pltpu.make_async_remote_copy(
            src_ref=h_refs[a].at[1 - c], dst_ref=r_refs[a], send_sem=send_sems.at[a], recv_sem=recv_sems.at[a],
            device_id=(x, y, 1 - c), device_id_type=MESH) for a in range(n)]
        for cp in copies:
            cp.start()
        for cp in copies:
            cp.wait_recv()
        for cp in copies:
            cp.wait_send()

    hbm = pl.BlockSpec(memory_space=pl.ANY)
    return pl.pallas_call(
        body, name=name, out_shape=[jax.ShapeDtypeStruct(h.shape[1:], h.dtype) for h in hs],
        in_specs=[hbm] * n, out_specs=[hbm] * n,
        scratch_shapes=[pltpu.SemaphoreType.DMA((n,)), pltpu.SemaphoreType.DMA((n,))],
    )(*hs)


def quad_exchange(name, ss):
    n = len(ss)

    def body(*refs):
        s_refs, r_refs = refs[:n], refs[n:2 * n]
        send_sems, recv_sems, local_sems = refs[2 * n:]
        x, y, c = _mesh_pos()
        me = 2 * x + y
        mine = [pltpu.make_async_copy(s_refs[a].at[me], r_refs[a].at[me], local_sems.at[a]) for a in range(n)]
        for cp in mine:
            cp.start()
        copies = []
        for k in range(1, 4):
            px, py = _flip(x, k & 2), _flip(y, k & 1)
            for a in range(n):
                copies.append(pltpu.make_async_remote_copy(
                    src_ref=s_refs[a].at[2 * px + py], dst_ref=r_refs[a].at[me],
                    send_sem=send_sems.at[a, k - 1], recv_sem=recv_sems.at[a, k - 1],
                    device_id=(px, py, c), device_id_type=MESH))
        for cp in copies:
            cp.start()
        for cp in copies:
            cp.wait_recv()
        for cp in copies:
            cp.wait_send()
        for cp in mine:
            cp.wait()

    hbm = pl.BlockSpec(memory_space=pl.ANY)
    return pl.pallas_call(
        body, name=name, out_shape=[jax.ShapeDtypeStruct(s.shape, s.dtype) for s in ss],
        in_specs=[hbm] * n, out_specs=[hbm] * n,
        scratch_shapes=[pltpu.SemaphoreType.DMA((n, 3)), pltpu.SemaphoreType.DMA((n, 3)), pltpu.SemaphoreType.DMA((n,))],
    )(*ss)


def _rows_cols(shape):
    r = 1
    for d in shape[:-1]:
        r *= d
    return r, shape[-1]


def _row_tile(r, cap, step):
    return next((t for t in range(cap, step - 1, -step) if r % t == 0), r)


def pair_add(name, h, recv, core):
    shape = recv.shape
    r, c = _rows_cols(shape[1:])
    tr = _row_tile(r, 256, 16)

    def body(core_ref, h_ref, r_ref, o_ref):
        o_ref[...] = (h_ref[...].astype(F32) + r_ref[...].astype(F32)).astype(o_ref.dtype)

    spec = pl.BlockSpec((None, tr, c), lambda q, i, core_ref: (q, i, 0))
    res = pl.pallas_call(
        body, name=name, out_shape=jax.ShapeDtypeStruct((4, r, c), h.dtype),
        grid_spec=pltpu.PrefetchScalarGridSpec(
            num_scalar_prefetch=1, grid=(4, r // tr),
            in_specs=[pl.BlockSpec((None, None, tr, c), lambda q, i, core_ref: (core_ref[0], q, i, 0)), spec],
            out_specs=spec),
        compiler_params=_params(("parallel", "parallel")),
    )(core, h.reshape(2, 4, r, c), recv.reshape(4, r, c))
    return res.reshape(shape)


def _sum_parts(p, n):
    t = [p[k].astype(F32) for k in range(n)]
    while len(t) > 1:
        t = [t[k] + t[k + 1] for k in range(0, len(t), 2)]
    return t[0]


def _adam(g, w, m, v):
    m = ADAM_B1 * m + (1.0 - ADAM_B1) * g
    v = ADAM_B2 * v + (1.0 - ADAM_B2) * (g * g)
    m_hat = m / (1.0 - ADAM_B1 ** ADAM_STEP)
    v_hat = v / (1.0 - ADAM_B2 ** ADAM_STEP)
    return -ADAM_LR * (m_hat / (jnp.sqrt(v_hat) + ADAM_EPS) + ADAM_WD * w), m, v


def adam_tiled(name, partials, w, m_, v_):
    shape = w.shape
    n_part = partials.shape[0]
    r, c = _rows_cols(shape)
    tr = _row_tile(r, 256, 16)

    def body(p_ref, w_ref, m_ref, v_ref, g_ref, d_ref, nm_ref, nv_ref):
        g = _sum_parts(p_ref, n_part)
        g_ref[...] = g
        d_ref[...], nm_ref[...], nv_ref[...] = _adam(g, w_ref[...], m_ref[...], v_ref[...])

    spec = pl.BlockSpec((tr, c), lambda i: (i, 0))
    res = pl.pallas_call(
        body, grid=(r // tr,), name=name,
        in_specs=[pl.BlockSpec((n_part, tr, c), lambda i: (0, i, 0)), spec, spec, spec],
        out_specs=[spec] * 4, out_shape=[jax.ShapeDtypeStruct((r, c), F32)] * 4,
        compiler_params=_params(("parallel",)),
    )(partials.reshape(n_part, r, c), w.reshape(r, c), m_.reshape(r, c), v_.reshape(r, c))
    return [t.reshape(shape) for t in res]


def adam_small(name, items, extra):
    n, ne = len(items), len(extra)

    def body(*refs):
        ins, outs = refs[:4 * n + ne], refs[4 * n + ne:]
        for a in range(n):
            p_ref, w_ref, m_ref, v_ref = ins[4 * a:4 * a + 4]
            g = _sum_parts(p_ref, N_DEV)
            outs[4 * a][...] = g
            outs[4 * a + 1][...], outs[4 * a + 2][...], outs[4 * a + 3][...] = _adam(g, w_ref[...], m_ref[...], v_ref[...])
        for e in range(ne):
            outs[4 * n + e][...] = _sum_parts(ins[4 * n + e], N_DEV)

    args, out_shape = [], []
    for p, w, m_, v_ in items:
        args += [p, w, m_, v_]
        out_shape += [jax.ShapeDtypeStruct(w.shape, F32)] * 4
    for e in extra:
        args.append(e)
        out_shape.append(jax.ShapeDtypeStruct(e.shape[1:], F32))
    vmem = pl.BlockSpec(memory_space=pltpu.VMEM)
    res = pl.pallas_call(body, name=name, in_specs=[vmem] * len(args), out_specs=[vmem] * len(out_shape), out_shape=out_shape)(*args)
    return [res[4 * a:4 * a + 4] for a in range(n)], res[4 * n:]


def _cols_from_gather(g):
    g = jnp.moveaxis(g, 0, -2)
    return g.reshape(g.shape[:-2] + (g.shape[-2] * g.shape[-1],))


def _cols_to_blocks(w):
    w = w.reshape(w.shape[:-1] + (N_DEV, w.shape[-1] // N_DEV))
    return jnp.moveaxis(w, -2, 0)


def _block_diag(w):
    z = jnp.zeros((B_BLOCK_DIM, B_BLOCK_DIM), w.dtype)
    rows = []
    for j in range(B_BLOCKS // 2):
        top = jnp.concatenate([w[2 * j], z], axis=1)
        bot = jnp.concatenate([z, w[2 * j + 1]], axis=1)
        rows.append(jnp.concatenate([top, bot], axis=0))
    return jnp.concatenate(rows, axis=0)


def _block_diag_grad(d):
    out = []
    for j in range(B_BLOCKS // 2):
        blk = d[128 * j:128 * (j + 1)]
        out.append(blk[:64, :64])
        out.append(blk[64:, 64:])
    return jnp.stack(out)


NAMES = ("norm_gains", "even_w_in", "hgrn_lb_logits", "hgrn_norm", "rg_conv_w", "rg_conv_b", "rg_wa", "rg_ba", "rg_wx", "rg_bx",
         "rg_lambda", "even_w_out", "odd_w_in", "fox_f_bias", "odd_w_out", "ffn_w_up", "ffn_conv_w", "ffn_conv_b", "ffn_w_down")
BIG = ("even_w_in", "even_w_out", "odd_w_in", "odd_w_out", "ffn_w_up", "ffn_w_down")
SMALL_SHARDED = ("norm_gains", "rg_conv_w", "ffn_conv_w")
REPLICATED = ("hgrn_lb_logits", "hgrn_norm", "rg_conv_b", "rg_wa", "rg_ba", "rg_wx", "rg_bx", "rg_lambda", "fox_f_bias", "ffn_conv_b")


def _ffn_forward(tag, layer, h, w_up_g, cw5, cb5, w_down_g, m, seq):
    tm = _div_tile(m, 1024)
    nm = m // tm
    hid = mm(f"{tag}_up", "nn",
             Blk(h, (tm, D_MODEL), lambda i, j, k: (i, 0)),
             Blk(w_up_g, (None, None, D_MODEL, FF_BLK), lambda i, j, k: (j, 0, 0, 0)),
             Blk((N_DEV, m, FF_BLK), (None, tm, FF_BLK), lambda i, j, k: (j, i, 0)), F32, (nm, N_DEV, 1))
    hid = hid.reshape(2, N_DEV // 2, m, FF_BLK)
    act = ffn_mid_fwd(f"{tag}_mid", hid, cw5, cb5, layer, m=m, seq=seq)
    f = mm(f"{tag}_down", "nn",
           Blk(act, (None, tm, FF_BLK), lambda i, j, k: (k, i, 0)),
           Blk(w_down_g, (2, None, FF_BLK // 2, D_MODEL), lambda i, j, k: (k, 0, 0, 0)),
           Blk((m, D_MODEL), (tm, D_MODEL), lambda i, j, k: (i, 0)), F32, (nm, 1, N_DEV // 2))
    return hid, act, f


def _ffn_backward(tag, layer, df, h, hid, act, w_up_g, cw5, cb5, w_down_g, m, seq):
    tm = _div_tile(m, 1024)
    nm = m // tm
    dact = mm(f"{tag}_dact", "nt",
              Blk(df, (tm, D_MODEL), lambda i, j, k: (i, 0)),
              Blk(w_down_g, (2, None, FF_BLK // 2, D_MODEL), lambda i, j, k: (j, 0, 0, 0)),
              Blk((N_DEV // 2, m, FF_BLK), (None, tm, FF_BLK), lambda i, j, k: (j, i, 0)), BF16, (nm, N_DEV // 2, 1))
    d_wdown = mm(f"{tag}_dwdown", "tn",
                 Blk(act, (None, tm, FF_BLK), lambda i, j, k: (i, k, 0)),
                 Blk(df, (tm, D_MODEL), lambda i, j, k: (k, 0)),
                 Blk(w_down_g.shape, (2, None, FF_BLK // 2, D_MODEL), lambda i, j, k: (i, 0, 0, 0)), BF16,
                 (N_DEV // 2, 1, nm))
    dhid, d_cw, d_cb = ffn_mid_bwd(f"{tag}_dmid", hid, cw5, cb5, dact, layer, m=m, seq=seq)
    dhid = dhid.reshape(N_DEV, m, FF_BLK)
    dh = mm(f"{tag}_dh", "nt",
            Blk(dhid, (None, tm, FF_BLK), lambda i, j, k: (k, i, 0)),
            Blk(w_up_g, (None, None, D_MODEL, FF_BLK), lambda i, j, k: (k, 0, 0, 0)),
            Blk((m, D_MODEL), (tm, D_MODEL), lambda i, j, k: (i, 0)), BF16, (nm, 1, N_DEV))
    d_wup = mm(f"{tag}_dwup", "tn",
               Blk(h, (tm, D_MODEL), lambda i, j, k: (k, 0)),
               Blk(dhid, (None, tm, FF_BLK), lambda i, j, k: (j, k, 0)),
               Blk(w_up_g.shape, (None, None, D_MODEL, FF_BLK), lambda i, j, k: (j, 0, 0, 0)), BF16,
               (1, N_DEV, nm))
    return dh, d_wup, d_cw, d_cb, d_wdown


def kernel(x, norm_gains, even_w_in, hgrn_lb_logits, hgrn_norm, rg_conv_w, rg_conv_b, rg_wa, rg_ba, rg_wx, rg_bx, rg_lambda, even_w_out, odd_w_in, fox_f_bias, odd_w_out, ffn_w_up, ffn_conv_w, ffn_conv_b, ffn_w_down, loss_target, m_norm_gains, m_even_w_in, m_hgrn_lb_logits, m_hgrn_norm, m_rg_conv_w, m_rg_conv_b, m_rg_wa, m_rg_ba, m_rg_wx, m_rg_bx, m_rg_lambda, m_even_w_out, m_odd_w_in, m_fox_f_bias, m_odd_w_out, m_ffn_w_up, m_ffn_conv_w, m_ffn_conv_b, m_ffn_w_down, v_norm_gains, v_even_w_in, v_hgrn_lb_logits, v_hgrn_norm, v_rg_conv_w, v_rg_conv_b, v_rg_wa, v_rg_ba, v_rg_wx, v_rg_bx, v_rg_lambda, v_even_w_out, v_odd_w_in, v_fox_f_bias, v_odd_w_out, v_ffn_w_up, v_ffn_conv_w, v_ffn_conv_b, v_ffn_w_down):
    local = dict(locals())
    w = {n: local[n] for n in NAMES}
    mom = {n: local["m_" + n] for n in NAMES}
    var = {n: local["v_" + n] for n in NAMES}
    n_batch, seq, _ = x.shape
    m = n_batch * seq
    tm = _div_tile(m, 512)
    tmm = _div_tile(m, 1024)
    nm = m // tmm

    now = [w["even_w_in"], w["even_w_out"]]
    gathered = all_gather("gather_weights", [t.astype(BF16) for t in now] + [w[n] for n in SMALL_SHARDED])
    g = dict(zip(("even_w_in", "even_w_out") + SMALL_SHARDED, gathered))
    w_in_e = g["even_w_in"]
    w_out_e = g["even_w_out"].reshape(D_MODEL, D_MODEL)
    gains = _cols_from_gather(g["norm_gains"])
    me = 4 * lax.axis_index("x") + 2 * lax.axis_index("y") + lax.axis_index("c")
    own_block_only = lambda t: lax.dynamic_update_index_in_dim(lax.empty((N_DEV,) + t.shape, t.dtype), t, me, 0)
    behind = (g["norm_gains"][0, 0, 0, 0] * 0.0).astype(BF16)
    ffn0 = [w["ffn_w_up"][0:1].astype(BF16) + behind, w["ffn_w_down"][0:1].astype(BF16) + behind]
    ffn0_sent = gather_start("gather_ffn0_start", ffn0, [own_block_only(t) for t in ffn0])
    behind = (ffn0_sent[4][0, 0] * 0.0).astype(BF16)
    mix1w = [w["odd_w_in"].astype(BF16) + behind, w["odd_w_out"].astype(BF16) + behind]
    mix1_sent = gather_start("gather_mix1_start", mix1w, [own_block_only(t) for t in mix1w])
    behind = (mix1_sent[4][0, 0] * 0.0).astype(BF16)
    ffn1 = [w["ffn_w_up"][1:2].astype(BF16) + behind, w["ffn_w_down"][1:2].astype(BF16) + behind]
    ffn1_sent = gather_start("gather_ffn1_start", ffn1, [own_block_only(t) for t in ffn1])
    started = ffn1_sent[4]
    rg_cw = _cols_from_gather(g["rg_conv_w"])[0]
    n_layer = ffn_conv_w.shape[0]
    cw5 = g["ffn_conv_w"].reshape(2, N_DEV // 2, n_layer, FFN_CONV, FF_BLK)
    cb5 = ffn_conv_b.reshape(n_layer, 2, N_DEV // 2, 1, FF_BLK)
    gain = lambda l, k: gains[l, k:k + 1, :]
    wa_bd, wx_bd = _block_diag(rg_wa[0]), _block_diag(rg_wx[0])
    fbias = jnp.pad(fox_f_bias, ((0, 0), (0, LANES - C_HEADS)))

    x0 = x.reshape(m, D_MODEL)
    tgt = loss_target.reshape(m, D_MODEL)

    (h0,) = tile_fwd("l0_prenorm", fn_prenorm_after, m=m, tm=tm, nj=1, rows=[Row(x0)], pars=[Par(gain(0, 0)), Par(started)],
                     outs=[Out(D_MODEL, BF16)])
    z0 = mm("l0_in", "nn",
            Blk(h0, (tmm, D_MODEL), lambda i, j, k: (i, 0)),
            Blk(w_in_e, (None, None, D_MODEL, 384), lambda i, j, k: (j, 0, 0, 0)),
            Blk((m, 3072), (tmm, 384), lambda i, j, k: (i, j)), F32, (nm, N_DEV, 1))
    oa, sprev = hgrn_fwd("l0_hgrn", z0, hgrn_lb_logits, hgrn_norm, n_batch=n_batch, seq=seq)
    rg_rows = lambda: [Row(z0, LANES, 16), Row(z0, LANES, 20)]
    rg_pars = lambda: [Par(rg_cw, "col", LANES), Par(rg_conv_b, "col", LANES), Par(wa_bd, "row", LANES), Par(rg_ba, "col", LANES),
                       Par(wx_bd, "row", LANES), Par(rg_bx, "col", LANES), Par(rg_lambda, "col", LANES)]
    (ob,) = tile_fwd("l0_rglru", fn_rglru, m=m, tm=seq, nj=B_WIDTH // LANES, rows=rg_rows(), pars=rg_pars(),
                     outs=[Out(B_WIDTH, BF16, LANES)])
    mixcat0 = jnp.concatenate([oa, ob], axis=-1)
    mix0 = mm2d("l0_out", "nn", mixcat0, w_out_e)
    x1, h1 = tile_fwd("l0_postnorm", fn_addnorm2, m=m, tm=tm, nj=1, rows=[Row(x0), Row(mix0)], pars=[Par(gain(0, 1)), Par(gain(0, 2))],
                      outs=[Out(D_MODEL, F32), Out(D_MODEL, BF16)])
    w_up_g0, w_down_g0 = gather_wait("gather_ffn0_wait", ffn0_sent[0], ffn0_sent[1], ffn0_sent[2], ffn0_sent[3], h1)
    hid0, act0, f0 = _ffn_forward("l0_ffn", 0, h1, w_up_g0, cw5, cb5, w_down_g0, m, seq)
    x2, h2 = tile_fwd("l0_ffnnorm", fn_addnorm2, m=m, tm=tm, nj=1, rows=[Row(x1), Row(f0)], pars=[Par(gain(0, 3)), Par(gain(1, 0))],
                      outs=[Out(D_MODEL, F32), Out(D_MODEL, BF16)])

    g_in_o, g_out_o = gather_wait("gather_mix1_wait", mix1_sent[0], mix1_sent[1], mix1_sent[2], mix1_sent[3], h2)
    w_in_o = jnp.pad(_cols_from_gather(g_in_o)[0], ((0, 0), (0, 3200 - 3088)))
    w_out_o = g_out_o.reshape(D_MODEL, D_MODEL)
    z1 = mm2d("l1_in", "nn", h2, w_in_o)
    (cgate,) = tile_fwd("l1_gate", fn_fox_gate, m=m, tm=seq, nj=1, rows=[Row(z1, LANES, 3072 // LANES)], pars=[Par(fbias)],
                        outs=[Out(LANES, F32)])
    place, ones_q, ones_k = term_placement()
    qterm, kterm = tile_fwd("l1_terms", fn_fox_terms, m=m, tm=tm, nj=1, rows=[Row(cgate)],
                            pars=[Par(place), Par(ones_q), Par(ones_k)], outs=[Out(TERM_W, BF16), Out(TERM_W, BF16)])
    oc, lse = fox_pair_fwd("l1_attn", z1, qterm, kterm, n_batch=n_batch, seq=seq)
    mix1 = mm2d("l1_out", "nn", oc, w_out_o)
    x3, h3 = tile_fwd("l1_postnorm", fn_addnorm2, m=m, tm=tm, nj=1, rows=[Row(x2), Row(mix1)], pars=[Par(gain(1, 1)), Par(gain(1, 2))],
                      outs=[Out(D_MODEL, F32), Out(D_MODEL, BF16)])
    w_up_g1, w_down_g1 = gather_wait("gather_ffn1_wait", ffn1_sent[0], ffn1_sent[1], ffn1_sent[2], ffn1_sent[3], h3)
    hid1, act1, f1 = _ffn_forward("l1_ffn", 1, h3, w_up_g1, cw5, cb5, w_down_g1, m, seq)
    dy, loss_part = tile_fwd("loss", fn_final, m=m, tm=tm, nj=1, rows=[Row(x3), Row(f1), Row(tgt)], pars=[Par(gain(1, 3))],
                             outs=[Out(D_MODEL, F32)], n_acc=1)

    df1, d_g13 = tile_bwd("l1_dffnnorm", fn_rms_only, m=m, tm=tm, nj=1, rows=[Row(f1)], pars=[Par(gain(1, 3))], cts=[Row(dy)],
                          drows=[Out(D_MODEL, BF16)])
    dh3, d_wup1, d_cw1, d_cb1, d_wdown1 = _ffn_backward("l1_ffn", 1, df1, h3, hid1, act1, w_up_g1, cw5, cb5, w_down_g1, m, seq)
    dx2, dmix1, d_g11, d_g12 = tile_bwd("l1_dpostnorm", fn_addnorm2, m=m, tm=tm, nj=1, rows=[Row(x2), Row(mix1)],
                                        pars=[Par(gain(1, 1)), Par(gain(1, 2))], cts=[Row(dy), Row(dh3)],
                                        drows=[Out(D_MODEL, F32), Out(D_MODEL, BF16)])
    doc = mm2d("l1_doc", "nt", dmix1, w_out_o, BF16)
    d_wout_o = mm2d("l1_dwout", "tn", oc, dmix1)
    dq, dk, dv, dc = fox_pair_bwd("l1_dattn", z1, qterm, kterm, oc, doc, lse, n_batch=n_batch, seq=seq)
    dzf, d_fbias = tile_bwd("l1_dgate", fn_fox_gate, m=m, tm=seq, nj=1, rows=[Row(z1, LANES, 3072 // LANES)], pars=[Par(fbias)],
                            cts=[Row(dc)], drows=[Out(LANES, BF16)])
    dz1 = jnp.concatenate([dq, dk, dv, dzf], axis=-1)
    dh2 = mm2d("l1_dh", "nt", dz1, w_in_o, BF16)
    d_win_o = mm2d("l1_dwin", "tn", h2, dz1)

    send1 = [_cols_to_blocks(d_win_o[None, :, :3088]).astype(BF16),
             d_wout_o.reshape(N_DEV, 1, D_MODEL // N_DEV, D_MODEL).astype(BF16), d_wup1, d_wdown1]
    sent1 = _split_exchange("exchange_l1_start", send1, [own_slot_only(t, me) for t in send1], None, None)

    dx1, df0, d_g03, d_g10 = tile_bwd("l0_dffnnorm", fn_addnorm2_after, m=m, tm=tm, nj=1, rows=[Row(x1), Row(f0)],
                                      pars=[Par(gain(0, 3)), Par(gain(1, 0)), Par(sent1[4])], cts=[Row(dx2), Row(dh2)],
                                      drows=[Out(D_MODEL, F32), Out(D_MODEL, BF16)])[:4]
    dh1, d_wup0, d_cw0, d_cb0, d_wdown0 = _ffn_backward("l0_ffn", 0, df0, h1, hid0, act0, w_up_g0, cw5, cb5, w_down_g0, m, seq)
    send0 = [d_wup0, d_wdown0]
    sent0 = _split_exchange("exchange_ffn0_start", send0, [own_slot_only(t, me) for t in send0], None, None)
    dx0a, dmix0, d_g01, d_g02 = tile_bwd("l0_dpostnorm", fn_addnorm2_after, m=m, tm=tm, nj=1, rows=[Row(x0), Row(mix0)],
                                         pars=[Par(gain(0, 1)), Par(gain(0, 2)), Par(sent0[4])], cts=[Row(dx1), Row(dh1)],
                                         drows=[Out(D_MODEL, F32), Out(D_MODEL, BF16)])[:4]
    dmixcat0 = mm2d("l0_dmixcat", "nt", dmix0, w_out_e, BF16)
    d_wout_e = mm2d("l0_dwout", "tn", mixcat0, dmix0)
    dzq, dzf0, dzv, dzg, d_lb, d_hnorm = hgrn_bwd("l0_dhgrn", z0, sprev, hgrn_lb_logits, hgrn_norm, dmixcat0, n_batch=n_batch, seq=seq)
    dzx, dzy, d_rcw, d_rcb, d_wa, d_ba, d_wx, d_bx, d_lam = tile_bwd(
        "l0_drglru", fn_rglru, m=m, tm=seq, nj=B_WIDTH // LANES, rows=rg_rows(), pars=rg_pars(),
        cts=[Row(dmixcat0, LANES, A_WIDTH // LANES)], drows=[Out(B_WIDTH, BF16, LANES), Out(B_WIDTH, BF16, LANES)])
    dz0 = jnp.concatenate([dzq, dzf0, dzv, dzg, dzx, dzy], axis=-1)
    d_win_e = mm("l0_dwin", "tn",
                 Blk(h0, (tmm, D_MODEL), lambda i, j, k: (k, 0)),
                 Blk(dz0, (tmm, 384), lambda i, j, k: (k, j)),
                 Blk(w_in_e.shape, (None, None, D_MODEL, 384), lambda i, j, k: (j, 0, 0, 0)), BF16, (1, N_DEV, nm))
    send_e = [d_win_e, d_wout_e.reshape(N_DEV, 1, D_MODEL // N_DEV, D_MODEL).astype(BF16)]
    sent_e = _split_exchange("exchange_even_start", send_e, [own_slot_only(t, me) for t in send_e], None, None)
    d_ffn_cb = jnp.stack([d_cb0, d_cb1]).reshape(n_layer, 2 * D_FF)
    rep = {"hgrn_lb_logits": d_lb, "hgrn_norm": d_hnorm, "rg_conv_b": d_rcb, "rg_wa": _block_diag_grad(d_wa)[None], "rg_ba": d_ba,
           "rg_wx": _block_diag_grad(d_wx)[None], "rg_bx": d_bx, "rg_lambda": d_lam, "fox_f_bias": d_fbias[:, :C_HEADS],
           "ffn_conv_b": d_ffn_cb}
    rep_blocks = [rep[n] for n in REPLICATED] + [loss_part]
    rep_sent = gather_start("gather_partials_start", rep_blocks, [own_block_only(t) for t in rep_blocks])
    dh0 = mm("l0_dh", "nt",
             Blk(dz0, (tmm, 384), lambda i, j, k: (i, k)),
             Blk(w_in_e, (None, None, D_MODEL, 384), lambda i, j, k: (k, 0, 0, 0)),
             Blk((m, D_MODEL), (tmm, D_MODEL), lambda i, j, k: (i, 0)), BF16, (nm, 1, N_DEV), after=sent_e[4] + rep_sent[4])
    dx0, d_g00 = tile_bwd("l0_dprenorm", fn_input_norm, m=m, tm=tm, nj=1, rows=[Row(x0)], pars=[Par(gain(0, 0))],
                          cts=[Row(dx0a), Row(dh0)], drows=[Out(D_MODEL, F32)])

    d_gains = jnp.stack([jnp.concatenate([d_g00, d_g01, d_g02, d_g03], axis=0), jnp.concatenate([d_g10, d_g11, d_g12, d_g13], axis=0)])
    d_ffn_cw = jnp.stack([d_cw0, d_cw1], axis=2).reshape(N_DEV, n_layer, FFN_CONV, FF_BLK)
    r_in_o, r_out_o, r_up1, r_down1 = _split_exchange("exchange_l1_wait", sent1[2], sent1[3], sent1[:2], dx0)
    r_up0, r_down0 = _split_exchange("exchange_ffn0_wait", sent0[2], sent0[3], sent0[:2], dx0)
    r_in_e, r_out_e = _split_exchange("exchange_even_wait", sent_e[2], sent_e[3], sent_e[:2], dx0)
    recv, res = {}, {}
    for n, r in (("even_w_in", r_in_e), ("even_w_out", r_out_e), ("odd_w_in", r_in_o), ("odd_w_out", r_out_o)):
        res[n] = adam_tiled("adam_" + n, r, w[n], mom[n], var[n])
    for n, parts_l in (("ffn_w_up", (r_up0, r_up1)), ("ffn_w_down", (r_down0, r_down1))):
        per_layer = [adam_tiled(f"adam_{n}_{l}", p, w[n][l:l + 1], mom[n][l:l + 1], var[n][l:l + 1]) for l, p in enumerate(parts_l)]
        res[n] = [jnp.concatenate([per_layer[0][k], per_layer[1][k]], axis=0) for k in range(4)]
    small_send = [_cols_to_blocks(d_gains), _cols_to_blocks(d_rcw[None]), d_ffn_cw]
    recv.update(zip(SMALL_SHARDED, all_to_all("exchange_small", small_send)))

    parts = gather_wait("gather_partials_wait", rep_sent[0], rep_sent[1], rep_sent[2], rep_sent[3], dx0)
    for n, p in zip(REPLICATED, parts):
        recv[n] = p
    small = SMALL_SHARDED + REPLICATED
    small_res, (loss_sum,) = adam_small("adam_small", [(recv[n], w[n], mom[n], var[n]) for n in small], [parts[-1]])
    res.update(dict(zip(small, small_res)))

    out = [loss_sum[0, 0], dx0.reshape(x.shape)]
    for k in range(4):
        out += [res[n][k] for n in NAMES]
    return tuple(out)
```

```python
import functools

import jax
import jax.numpy as jnp
from jax import lax
from jax.experimental import pallas as pl
from jax.experimental.pallas import tpu as pltpu

F32 = jnp.float32
BF16 = jnp.bfloat16

D_MODEL = 1024
A_HEADS = 4
A_WIDTH = 512
HGRN_CHUNK = 64
HGRN_SEG = 512
B_WIDTH = 512
B_BLOCKS = 8
B_BLOCK_DIM = 64
B_CONV = 4
RG_C = 8.0
C_HEADS = 16
C_HEAD_DIM = 64
D_FF = 2816
FFN_CONV = 3
EPS = 1e-6
LANES = 128
HALO = 16
N_DEV = 8
FF_BLK = 2 * D_FF // N_DEV
MESH = pl.DeviceIdType.MESH
NEG = -1e30
VMEM_LIMIT = 56 * 1024 * 1024

ADAM_LR = 0.001
ADAM_B1 = 0.9
ADAM_B2 = 0.999
ADAM_EPS = 1e-08
ADAM_WD = 0.01
ADAM_STEP = 10


def _dg(a, b, pat):
    nb = a.ndim - 2
    batch = (tuple(range(nb)), tuple(range(nb)))
    ca = a.ndim - 1 if pat[0] == "n" else a.ndim - 2
    cb = b.ndim - 2 if pat[1] == "n" else b.ndim - 1
    return lax.dot_general(a.astype(BF16), b.astype(BF16), (((ca,), (cb,)), batch), preferred_element_type=F32)


@functools.partial(jax.custom_vjp, nondiff_argnums=(2,))
def bdot(a, b, pat):
    return _dg(a, b, pat)


def _bdot_fwd(a, b, pat):
    return _dg(a, b, pat), (a, b)


def _bdot_bwd(pat, res, g):
    a, b = res
    if pat == "nn":
        return _dg(g, b, "nt"), _dg(a, g, "tn")
    if pat == "nt":
        return _dg(g, b, "nn"), _dg(g, a, "tn")
    return _dg(b, g, "nt"), _dg(a, g, "nn")


bdot.defvjp(_bdot_fwd, _bdot_bwd)


def _shift_raw(x, s, up, fill):
    if s == 0:
        return x
    n = x.shape[0]
    r = pltpu.roll(x, (n - s) if up else s, 0)
    idx = lax.broadcasted_iota(jnp.int32, x.shape, 0)
    mask = (idx >= n - s) if up else (idx < s)
    return jnp.where(mask, jnp.asarray(fill, x.dtype), r)


@functools.partial(jax.custom_vjp, nondiff_argnums=(1,))
def shift_down(x, s):
    return _shift_raw(x, s, False, 0.0)


def _shift_down_fwd(x, s):
    return _shift_raw(x, s, False, 0.0), None


def _shift_down_bwd(s, _, g):
    return (_shift_raw(g, s, True, 0.0),)


shift_down.defvjp(_shift_down_fwd, _shift_down_bwd)


def _scan_impl(a, u, up):
    n = a.shape[0]
    s = 1
    while s < n:
        u = a * _shift_raw(u, s, up, 0.0) + u
        if 2 * s < n:
            a = a * _shift_raw(a, s, up, 1.0)
        s *= 2
    return u


@jax.custom_vjp
def lin_scan(a, u):
    return _scan_impl(a, u, False)


def _lin_scan_fwd(a, u):
    h = _scan_impl(a, u, False)
    return h, (a, h)


def _lin_scan_bwd(res, g):
    a, h = res
    gh = _scan_impl(_shift_raw(a, 1, True, 0.0), g, True)
    return gh * _shift_raw(h, 1, False, 0.0), gh


lin_scan.defvjp(_lin_scan_fwd, _lin_scan_bwd)


def _cumsum_impl(x, up, period):
    n = x.shape[0]
    span = n if period is None else period
    idx = lax.broadcasted_iota(jnp.int32, x.shape, 0)
    pos = idx if period is None else idx % period
    s = 1
    while s < span:
        sh = _shift_raw(x, s, up, 0.0)
        if period is not None:
            keep = (pos < period - s) if up else (pos >= s)
            sh = jnp.where(keep, sh, 0.0)
        x = x + sh
        s *= 2
    return x


@functools.partial(jax.custom_vjp, nondiff_argnums=(1,))
def cumsum_rows(x, period):
    return _cumsum_impl(x, False, period)


def _cumsum_fwd(x, period):
    return _cumsum_impl(x, False, period), None


def _cumsum_bwd(period, _, g):
    return (_cumsum_impl(g, True, period),)


cumsum_rows.defvjp(_cumsum_fwd, _cumsum_bwd)


def _sigmoid(x):
    return jax.nn.sigmoid(x)


def _expm1(x):
    return jnp.tanh(0.5 * x) * (jnp.exp(x) + 1.0)


def _softplus(x):
    return jnp.maximum(x, 0.0) + jnp.log(1.0 + jnp.exp(-jnp.abs(x)))


def _rms(x, g):
    return x * lax.rsqrt(jnp.mean(x * x, axis=-1, keepdims=True) + EPS) * g


def fn_prenorm(x, g):
    return (_rms(x, g).astype(BF16),)


def fn_prenorm_after(x, g, _token):
    return fn_prenorm(x, g)


def fn_addnorm2(x, y, g_post, g_pre):
    x1 = x + _rms(y, g_post)
    return x1, _rms(x1, g_pre).astype(BF16)


def fn_addnorm2_after(x, y, g_post, g_pre, _token):
    return fn_addnorm2(x, y, g_post, g_pre)


def fn_input_norm(x, g):
    return x, _rms(x, g).astype(BF16)


def fn_final(x, y, tgt, g_post):
    out = x + _rms(y, g_post)
    err = out - tgt
    dy = err * (1.0 / D_MODEL)
    loss = 0.5 * jnp.sum(jnp.mean(err * err, axis=-1, keepdims=True), axis=0, keepdims=True)
    return dy, jnp.broadcast_to(loss, (1, LANES))


def fn_rms_only(y, g):
    return (_rms(y, g),)


def _causal_conv(x, w, b, taps):
    c = b
    for k in range(taps):
        c = c + w[k:k + 1, :] * shift_down(x, taps - 1 - k)
    return c


def fn_rglru(xb, yb, cw, cb, wa, ba, wx, bx, lam):
    xf = _causal_conv(xb, cw, cb, B_CONV)
    r = _sigmoid(bdot(xf, wa, "nn") + ba)
    i = _sigmoid(bdot(xf, wx, "nn") + bx)
    log_a = -RG_C * r * _softplus(-lam)
    a = jnp.exp(log_a)
    u = jnp.sqrt(-_expm1(2.0 * log_a)) * (i * xf)
    h = lin_scan(a, u)
    return ((h * jax.nn.gelu(yb)).astype(BF16),)


def fn_fox_gate(zf, bias):
    return (cumsum_rows(jax.nn.log_sigmoid(zf + bias), None),)


def fn_hgrn_seg(q, fl, v, g, st, logits, hn):
    rows = q.shape[0]
    nc = rows // HGRN_CHUNK
    l0, l1, l2 = logits[0:1, :], logits[1:2, :], logits[2:3, :]
    mx = jnp.maximum(jnp.maximum(l0, l1), l2)
    e0, e1, e2 = jnp.exp(l0 - mx), jnp.exp(l1 - mx), jnp.exp(l2 - mx)
    lb = e0 / (e0 + e1 + e2)
    forget = lb + (1.0 - lb) * _sigmoid(fl)
    qs = q * _sigmoid(q)
    kk = 1.0 - forget
    logf = jnp.log(forget)
    bcum = cumsum_rows(logf, HGRN_CHUNK)
    c3 = lambda t: t.reshape(nc, HGRN_CHUNK, 128)
    b_last = jnp.sum(c3(logf), axis=1, keepdims=True)
    bcum3 = c3(bcum)
    q_dec = c3(qs) * jnp.exp(bcum3)
    k_dec = c3(kk) * jnp.exp(-bcum3)
    k_upd = c3(kk) * jnp.exp(b_last - bcum3)
    v3 = c3(v)
    scores = bdot(q_dec, k_dec, "nt")
    ri = lax.broadcasted_iota(jnp.int32, scores.shape, 1)
    ci = lax.broadcasted_iota(jnp.int32, scores.shape, 2)
    scores = jnp.where(ri >= ci, scores, 0.0)
    o = bdot(scores, v3, "nn")
    upd_t = bdot(v3, k_upd, "tn")
    dec = jnp.exp(b_last)
    prev = []
    for n in range(nc):
        prev.append(st)
        st = st * dec[n] + upd_t[n]
    o = o + bdot(q_dec, jnp.stack(prev), "nt")
    o = o.reshape(rows, 128)
    o = o * lax.rsqrt(jnp.mean(o * o, axis=-1, keepdims=True) + EPS) * hn
    return (o * _sigmoid(g)).astype(BF16), st


def _ffn_conv(xg, xv, cw, cb):
    cg = _causal_conv(xg, cw[0], cb[0], FFN_CONV)[HALO:]
    cv = _causal_conv(xv, cw[1], cb[1], FFN_CONV)[HALO:]
    return cg, cv


def _ffn_gate(cg, cv):
    return jax.nn.gelu(cg) * cv


class Row:
    def __init__(self, arr, cb=None, off=0):
        self.arr, self.cb, self.off = arr, cb, off

    def spec(self, tm):
        if self.cb is None:
            return pl.BlockSpec((tm, self.arr.shape[1]), lambda j, i: (i, 0))
        off = self.off
        return pl.BlockSpec((tm, self.cb), lambda j, i: (i, j + off))


class Par:
    def __init__(self, arr, kind="full", bs=None):
        self.arr, self.kind, self.bs = arr, kind, bs

    def block(self):
        if self.kind == "full":
            return self.arr.shape
        if self.kind == "col":
            return (self.arr.shape[0], self.bs)
        return (self.bs, self.arr.shape[1])

    def spec(self):
        if self.kind == "full":
            return pl.BlockSpec(self.block(), lambda j, i: (0, 0))
        if self.kind == "col":
            return pl.BlockSpec(self.block(), lambda j, i: (0, j))
        return pl.BlockSpec(self.block(), lambda j, i: (j, 0))


class Out:
    def __init__(self, width, dtype, cb=None, off=0):
        self.width, self.dtype, self.cb, self.off = width, dtype, cb, off

    def spec(self, tm):
        if self.cb is None:
            return pl.BlockSpec((tm, self.width), lambda j, i: (i, 0))
        off = self.off
        return pl.BlockSpec((tm, self.cb), lambda j, i: (i, j + off))


def _params(sem):
    return pltpu.CompilerParams(dimension_semantics=sem, vmem_limit_bytes=VMEM_LIMIT)


def tile_fwd(name, fn, *, m, tm, nj, rows, pars, outs, n_acc=0):
    n_r, n_p, n_o = len(rows), len(pars), len(outs)

    def body(*refs):
        ins = [r[...] for r in refs[:n_r + n_p]]
        res = fn(*ins)
        o_refs = refs[n_r + n_p:]
        for k in range(n_o):
            o_refs[k][...] = res[k].astype(o_refs[k].dtype)
        first = jnp.logical_and(pl.program_id(0) == 0, pl.program_id(1) == 0)
        for k in range(n_acc):
            ref = o_refs[n_o + k]

            @pl.when(first)
            def _():
                ref[...] = jnp.zeros_like(ref)

            ref[...] += res[n_o + k]

    out_shape = [jax.ShapeDtypeStruct((m, o.width), o.dtype) for o in outs]
    out_specs = [o.spec(tm) for o in outs]
    for _ in range(n_acc):
        out_shape.append(jax.ShapeDtypeStruct((1, LANES), F32))
        out_specs.append(pl.BlockSpec((1, LANES), lambda j, i: (0, 0)))
    sem = ("arbitrary", "arbitrary") if n_acc else ("parallel", "parallel")
    return pl.pallas_call(
        body, grid=(nj, m // tm), name=name,
        in_specs=[r.spec(tm) for r in rows] + [p.spec() for p in pars],
        out_specs=out_specs, out_shape=out_shape, compiler_params=_params(sem),
    )(*[r.arr for r in rows], *[p.arr for p in pars])


def tile_bwd(name, fn, *, m, tm, nj, rows, pars, cts, drows):
    n_r, n_p, n_c = len(rows), len(pars), len(cts)
    want = [k for k in range(n_r) if drows[k] is not None]

    def body(*refs):
        ins = [r[...] for r in refs[:n_r + n_p]]
        ct = [r[...] for r in refs[n_r + n_p:n_r + n_p + n_c]]
        o_refs = refs[n_r + n_p + n_c:]
        res, vjp = jax.vjp(fn, *ins)
        grads = vjp(tuple(c.astype(r.dtype) for c, r in zip(ct, res)))
        for pos, k in enumerate(want):
            o_refs[pos][...] = grads[k].astype(o_refs[pos].dtype)
        for k in range(n_p):
            ref = o_refs[len(want) + k]
            first = pl.program_id(1) == 0
            if pars[k].kind == "full":
                first = jnp.logical_and(first, pl.program_id(0) == 0)

            @pl.when(first)
            def _():
                ref[...] = jnp.zeros_like(ref)

            ref[...] += grads[n_r + k].astype(F32)

    out_shape = [jax.ShapeDtypeStruct((m, drows[k].width), drows[k].dtype) for k in want]
    out_specs = [drows[k].spec(tm) for k in want]
    for p in pars:
        out_shape.append(jax.ShapeDtypeStruct(p.arr.shape, F32))
        out_specs.append(p.spec())
    return pl.pallas_call(
        body, grid=(nj, m // tm), name=name,
        in_specs=[r.spec(tm) for r in rows] + [p.spec() for p in pars] + [c.spec(tm) for c in cts],
        out_specs=out_specs, out_shape=out_shape, compiler_params=_params(("arbitrary", "arbitrary")),
    )(*[r.arr for r in rows], *[p.arr for p in pars], *[c.arr for c in cts])


class Blk:
    def __init__(self, arr, block, index):
        self.arr, self.block, self.index = arr, block, index

    def spec(self):
        return pl.BlockSpec(self.block, self.index)


def _flat2(v):
    return v if v.ndim == 2 else v.reshape(-1, v.shape[-1])


def mm(name, pat, a, b, o, out_dtype, grid, after=None, b_join=False, o_split=False):
    nk = grid[2]
    o_shape = o.arr

    def put(o_ref, r):
        if o_split:
            half = r.shape[1] // 2
            o_ref[0] = r[:, :half].astype(out_dtype)
            o_ref[1] = r[:, half:].astype(out_dtype)
        else:
            o_ref[...] = r.astype(out_dtype).reshape(o_ref.shape)

    def body(*refs):
        a_ref, b_ref = refs[0], refs[1]
        o_ref = refs[3] if after is not None else refs[2]
        bv = jnp.concatenate([b_ref[0], b_ref[1]], axis=1) if b_join else _flat2(b_ref[...])
        r = _dg(_flat2(a_ref[...]), bv, pat)
        if nk == 1:
            put(o_ref, r)
            return
        acc_ref = refs[-1]
        kk = pl.program_id(2)

        @pl.when(kk == 0)
        def _():
            acc_ref[...] = r

        @pl.when(kk > 0)
        def _():
            acc_ref[...] += r

        @pl.when(kk == nk - 1)
        def _():
            put(o_ref, acc_ref[...])

    ob = [d for d in o.block if d is not None]
    if o_split:
        acc_shape = (ob[1], 2 * ob[2])
    else:
        acc_shape = (ob[0], ob[1]) if len(ob) == 2 else (ob[0] * ob[1], ob[2])
    in_specs = [a.spec(), b.spec()]
    args = [a.arr, b.arr]
    if after is not None:
        in_specs.append(pl.BlockSpec(memory_space=pl.ANY))
        args.append(after)
    return pl.pallas_call(
        body, grid=grid, name=name, in_specs=in_specs, out_specs=o.spec(),
        out_shape=jax.ShapeDtypeStruct(o_shape, out_dtype),
        scratch_shapes=[pltpu.VMEM(acc_shape, F32)] if nk > 1 else [],
        compiler_params=_params(("parallel", "parallel", "arbitrary")),
    )(*args)


def _div_tile(n, cap):
    if n <= cap:
        return n
    best = 128
    for t in range(128, cap + 1, 128):
        if n % t == 0:
            best = t
    return best


def mm2d(name, pat, a, b, out_dtype=F32):
    if pat == "tn":
        k, m = a.shape
    else:
        m, k = a.shape
    n = b.shape[0] if pat == "nt" else b.shape[1]
    tm, tn, tk = _div_tile(m, 1024), _div_tile(n, 1024), _div_tile(k, 1024)
    a_blk = Blk(a, (tk, tm), lambda i, j, kk: (kk, i)) if pat == "tn" else Blk(a, (tm, tk), lambda i, j, kk: (i, kk))
    b_blk = Blk(b, (tn, tk), lambda i, j, kk: (j, kk)) if pat == "nt" else Blk(b, (tk, tn), lambda i, j, kk: (kk, j))
    o_blk = Blk((m, n), (tm, tn), lambda i, j, kk: (i, j))
    return mm(name, pat, a_blk, b_blk, o_blk, out_dtype, (m // tm, n // tn, k // tk))


def hgrn_fwd(name, z, logits, hnorm, *, n_batch, seq):
    m = n_batch * seq
    ts = min(HGRN_SEG, seq)
    n_seg = seq // ts

    def body(q_ref, f_ref, v_ref, g_ref, lg_ref, hn_ref, o_ref, sp_ref, st_ref):
        s = pl.program_id(2)

        @pl.when(s == 0)
        def _():
            st_ref[...] = jnp.zeros_like(st_ref)

        st = st_ref[...]
        sp_ref[...] = st
        o, st_new = fn_hgrn_seg(q_ref[...], f_ref[...], v_ref[...], g_ref[...], st, lg_ref[...], hn_ref[...])
        o_ref[...] = o
        st_ref[...] = st_new

    part = lambda p: pl.BlockSpec((ts, 128), lambda h, b, s: (b * n_seg + s, 4 * p + h))
    return pl.pallas_call(
        body, grid=(A_HEADS, n_batch, n_seg), name=name,
        in_specs=[part(0), part(1), part(2), part(3),
                  pl.BlockSpec((3, 128), lambda h, b, s: (0, h)),
                  pl.BlockSpec((1, 128), lambda h, b, s: (0, h))],
        out_specs=[pl.BlockSpec((ts, 128), lambda h, b, s: (b * n_seg + s, h)),
                   pl.BlockSpec((128, 128), lambda h, b, s: ((b * n_seg + s) * A_HEADS + h, 0))],
        out_shape=[jax.ShapeDtypeStruct((m, A_WIDTH), BF16),
                   jax.ShapeDtypeStruct((n_batch * n_seg * A_HEADS * 128, 128), F32)],
        scratch_shapes=[pltpu.VMEM((128, 128), F32)],
        compiler_params=_params(("arbitrary", "arbitrary", "arbitrary")),
    )(z, z, z, z, logits, hnorm)


def hgrn_bwd(name, z, sprev, logits, hnorm, do, *, n_batch, seq):
    m = n_batch * seq
    ts = min(HGRN_SEG, seq)
    n_seg = seq // ts

    def body(q_ref, f_ref, v_ref, g_ref, sp_ref, lg_ref, hn_ref, do_ref, dq_ref, df_ref, dv_ref, dg_ref, dlg_ref, dhn_ref, dst_ref):
        s = pl.program_id(2)

        @pl.when(s == 0)
        def _():
            dst_ref[...] = jnp.zeros_like(dst_ref)

        res, vjp = jax.vjp(fn_hgrn_seg, q_ref[...], f_ref[...], v_ref[...], g_ref[...], sp_ref[...], lg_ref[...], hn_ref[...])
        dq, df, dv, dg, dst, dlg, dhn = vjp((do_ref[...].astype(res[0].dtype), dst_ref[...]))
        dq_ref[...] = dq.astype(dq_ref.dtype)
        df_ref[...] = df.astype(df_ref.dtype)
        dv_ref[...] = dv.astype(dv_ref.dtype)
        dg_ref[...] = dg.astype(dg_ref.dtype)
        dst_ref[...] = dst
        first = jnp.logical_and(pl.program_id(1) == 0, s == 0)

        @pl.when(first)
        def _():
            dlg_ref[...] = jnp.zeros_like(dlg_ref)
            dhn_ref[...] = jnp.zeros_like(dhn_ref)

        dlg_ref[...] += dlg
        dhn_ref[...] += dhn

    rev = lambda b, s: b * n_seg + (n_seg - 1 - s)
    part = lambda p: pl.BlockSpec((ts, 128), lambda h, b, s: (rev(b, s), 4 * p + h))
    head = pl.BlockSpec((ts, 128), lambda h, b, s: (rev(b, s), h))
    dpart = jax.ShapeDtypeStruct((m, A_WIDTH), BF16)
    return pl.pallas_call(
        body, grid=(A_HEADS, n_batch, n_seg), name=name,
        in_specs=[part(0), part(1), part(2), part(3),
                  pl.BlockSpec((128, 128), lambda h, b, s: (rev(b, s) * A_HEADS + h, 0)),
                  pl.BlockSpec((3, 128), lambda h, b, s: (0, h)),
                  pl.BlockSpec((1, 128), lambda h, b, s: (0, h)),
                  head],
        out_specs=[head, head, head, head,
                   pl.BlockSpec((3, 128), lambda h, b, s: (0, h)),
                   pl.BlockSpec((1, 128), lambda h, b, s: (0, h))],
        out_shape=[dpart, dpart, dpart, dpart,
                   jax.ShapeDtypeStruct(logits.shape, F32),
                   jax.ShapeDtypeStruct(hnorm.shape, F32)],
        scratch_shapes=[pltpu.VMEM((128, 128), F32)],
        compiler_params=_params(("arbitrary", "arbitrary", "arbitrary")),
    )(z, z, z, z, sprev, logits, hnorm, do)


def _ffn_tiles(m, seq):
    tm = min(512, seq)
    return tm, seq // tm, m // tm


def ffn_mid_fwd(name, hid, cw, cb, layer, *, m, seq):
    tm, n_t, n_i = _ffn_tiles(m, seq)
    hb = tm // HALO

    def body(x_ref, xb_ref, cw_ref, cb_ref, o_ref):
        first = pl.program_id(1) % n_t == 0
        before = jnp.where(first, 0.0, xb_ref[...])
        ext = jnp.concatenate([before, x_ref[...]], axis=1)
        cg, cv = _ffn_conv(ext[0], ext[1], cw_ref[...], cb_ref[...])
        o_ref[...] = _ffn_gate(cg, cv).astype(o_ref.dtype)

    return pl.pallas_call(
        body, grid=(N_DEV // 2, n_i), name=name,
        in_specs=[pl.BlockSpec((2, None, tm, FF_BLK), lambda d, i: (0, d, i, 0)),
                  pl.BlockSpec((2, None, HALO, FF_BLK), lambda d, i: (0, d, jnp.maximum(i * hb - 1, 0), 0)),
                  pl.BlockSpec((2, None, None, FFN_CONV, FF_BLK), lambda d, i: (0, d, layer, 0, 0)),
                  pl.BlockSpec((None, 2, None, 1, FF_BLK), lambda d, i: (layer, 0, d, 0, 0))],
        out_specs=pl.BlockSpec((None, tm, FF_BLK), lambda d, i: (d, i, 0)),
        out_shape=jax.ShapeDtypeStruct((N_DEV // 2, m, FF_BLK), BF16),
        compiler_params=_params(("parallel", "parallel")),
    )(hid, hid, cw, cb)


def ffn_mid_bwd(name, hid, cw, cb, dact, layer, *, m, seq):
    tm, n_t, n_i = _ffn_tiles(m, seq)
    hb = tm // HALO
    last_blk = m // HALO - 1

    def body(x_ref, xb_ref, xa_ref, cw_ref, cb_ref, da_ref, daa_ref, dx_ref, dcw_ref, dcb_ref):
        i = pl.program_id(1)
        first = i % n_t == 0
        last = i % n_t == n_t - 1
        before = jnp.where(first, 0.0, xb_ref[...])
        ext = jnp.concatenate([before, x_ref[...], xa_ref[...]], axis=1)
        dact_ext = jnp.concatenate([da_ref[...].astype(F32), jnp.where(last, 0.0, daa_ref[...].astype(F32))], axis=0)
        (cg, cv), vjp_conv = jax.vjp(_ffn_conv, ext[0], ext[1], cw_ref[...], cb_ref[...])
        _, vjp_gate = jax.vjp(_ffn_gate, cg, cv)
        dcg, dcv = vjp_gate(dact_ext)
        dxg, dxv, _, _ = vjp_conv((dcg, dcv))
        dx_ref[0] = dxg[HALO:HALO + tm].astype(dx_ref.dtype)
        dx_ref[1] = dxv[HALO:HALO + tm].astype(dx_ref.dtype)
        own = lax.broadcasted_iota(jnp.int32, dcg.shape, 0) < tm
        _, _, dcw, dcb = vjp_conv((jnp.where(own, dcg, 0.0), jnp.where(own, dcv, 0.0)))

        @pl.when(i == 0)
        def _():
            dcw_ref[...] = jnp.zeros_like(dcw_ref)
            dcb_ref[...] = jnp.zeros_like(dcb_ref)

        dcw_ref[...] += dcw
        dcb_ref[...] += dcb

    return pl.pallas_call(
        body, grid=(N_DEV // 2, n_i), name=name,
        in_specs=[pl.BlockSpec((2, None, tm, FF_BLK), lambda d, i: (0, d, i, 0)),
                  pl.BlockSpec((2, None, HALO, FF_BLK), lambda d, i: (0, d, jnp.maximum(i * hb - 1, 0), 0)),
                  pl.BlockSpec((2, None, HALO, FF_BLK), lambda d, i: (0, d, jnp.minimum((i + 1) * hb, last_blk), 0)),
                  pl.BlockSpec((2, None, None, FFN_CONV, FF_BLK), lambda d, i: (0, d, layer, 0, 0)),
                  pl.BlockSpec((None, 2, None, 1, FF_BLK), lambda d, i: (layer, 0, d, 0, 0)),
                  pl.BlockSpec((None, tm, FF_BLK), lambda d, i: (d, i, 0)),
                  pl.BlockSpec((None, HALO, FF_BLK), lambda d, i: (d, jnp.minimum((i + 1) * hb, last_blk), 0))],
        out_specs=[pl.BlockSpec((2, None, tm, FF_BLK), lambda d, i: (0, d, i, 0)),
                   pl.BlockSpec((2, None, FFN_CONV, FF_BLK), lambda d, i: (0, d, 0, 0)),
                   pl.BlockSpec((2, None, 1, FF_BLK), lambda d, i: (0, d, 0, 0))],
        out_shape=[jax.ShapeDtypeStruct((2, N_DEV // 2, m, FF_BLK), BF16),
                   jax.ShapeDtypeStruct((2, N_DEV // 2, FFN_CONV, FF_BLK), F32),
                   jax.ShapeDtypeStruct((2, N_DEV // 2, 1, FF_BLK), F32)],
        compiler_params=_params(("arbitrary", "arbitrary")),
    )(hid, hid, hid, cw, cb, dact, dact)


ATT_BLK = 512
N_PAIR = C_HEADS // 2
TERM_W = C_HEADS * LANES


def term_placement():
    import numpy as np
    place = np.zeros((6, LANES, TERM_W), np.float32)
    ones_q = np.zeros((1, TERM_W), np.float32)
    ones_k = np.zeros((1, TERM_W), np.float32)
    for h in range(C_HEADS):
        for j in range(3):
            place[j, h, h * LANES + C_HEAD_DIM + j] = 1.0
            place[3 + j, h, h * LANES + C_HEAD_DIM + 3 + j] = 1.0
            ones_q[0, h * LANES + C_HEAD_DIM + 3 + j] = 1.0
            ones_k[0, h * LANES + C_HEAD_DIM + j] = 1.0
    return (jnp.asarray(place.reshape(6 * LANES, TERM_W), BF16), jnp.asarray(ones_q, F32), jnp.asarray(ones_k, F32))


def fn_fox_terms(c, place, ones_q, ones_k):
    parts = _split3(c)
    qt = ones_q
    kt = ones_k
    for j in range(3):
        qt = qt + _dg(parts[j], place[j * LANES:(j + 1) * LANES], "nn")
        kt = kt - _dg(parts[j], place[(3 + j) * LANES:(4 + j) * LANES], "nn")
    return qt.astype(BF16), kt.astype(BF16)


def _head_tile(z, terms, e):
    lane = lax.broadcasted_iota(jnp.int32, z.shape, 1)
    base = z if e == 0 else pltpu.roll(z, C_HEAD_DIM, 1)
    return jnp.where(lane < C_HEAD_DIM, base, terms.astype(z.dtype))


def _head_only(z, e):
    lane = lax.broadcasted_iota(jnp.int32, z.shape, 1)
    mine = (lane < C_HEAD_DIM) if e == 0 else (lane >= C_HEAD_DIM)
    return jnp.where(mine, z, jnp.zeros_like(z)).astype(BF16)


def _pair_tile(a0, a1):
    lane = lax.broadcasted_iota(jnp.int32, a0.shape, 1)
    return jnp.where(lane < C_HEAD_DIM, a0, pltpu.roll(a1, C_HEAD_DIM, 1))


def _lane_col(a, k):
    lane = lax.broadcasted_iota(jnp.int32, a.shape, 1)
    return jnp.sum(jnp.where(lane == k, a, 0.0), axis=1, keepdims=True)


def _causal(s):
    key = lax.broadcasted_iota(jnp.int32, s.shape, 0)
    qry = lax.broadcasted_iota(jnp.int32, s.shape, 1)
    return qry >= key


def fox_pair_fwd(name, z, qterm, kterm, *, n_batch, seq):
    m = n_batch * seq
    blk = min(ATT_BLK, seq)
    nq = seq // blk
    dh = C_HEAD_DIM

    def body(zq_ref, zk_ref, zv_ref, qt_ref, kt_ref, o_ref, lse_ref, ka_ref, vt_ref):
        qi = pl.program_id(2)

        @pl.when(qi == 0)
        def _():
            zk = zk_ref[...]
            for e in range(2):
                ka_ref[e] = _head_tile(zk, kt_ref[:, e * LANES:(e + 1) * LANES], e).astype(BF16)
            for cb in range(nq):
                vt_ref[cb] = zv_ref[cb * blk:(cb + 1) * blk, :].T.astype(BF16)

        zq = zq_ref[...] * dh ** -0.5
        qa = [_head_tile(zq, qt_ref[:, e * LANES:(e + 1) * LANES], e).astype(BF16) for e in range(2)]

        def block(j, carry, diagonal):
            rows = pl.ds(pl.multiple_of(j * blk, blk), blk)
            out = []
            for e in range(2):
                mx, l, acc = carry[e]
                s = _dg(ka_ref[e, rows, :], qa[e], "nt")
                if diagonal:
                    s = jnp.where(_causal(s), s, NEG)
                mx_new = jnp.maximum(mx, jnp.max(s, axis=0, keepdims=True))
                p = jnp.exp(s - mx_new)
                alpha = jnp.exp(mx - mx_new)
                l = alpha * l + jnp.sum(p, axis=0, keepdims=True)
                acc = alpha * acc + _dg(vt_ref[j, e * dh:(e + 1) * dh, :], p, "nn")
                out.append((mx_new, l, acc))
            return tuple(out)

        one = (jnp.full((1, blk), NEG, F32), jnp.zeros((1, blk), F32), jnp.zeros((dh, blk), F32))
        carry = lax.fori_loop(0, qi, lambda j, cr: block(j, cr, False), (one, one))
        res = block(qi, carry, True)
        ot = jnp.concatenate([res[e][2] / res[e][1] for e in range(2)], axis=0)
        o_ref[...] = ot.T.astype(o_ref.dtype)
        for e in range(2):
            lse_ref[e] = res[e][0] + jnp.log(res[e][1])

    col = lambda part: (lambda b, g, i: (b, part * N_PAIR + g))
    return pl.pallas_call(
        body, grid=(n_batch, N_PAIR, nq), name=name,
        in_specs=[pl.BlockSpec((blk, LANES), lambda b, g, i: (b * nq + i, g)),
                  pl.BlockSpec((seq, LANES), col(1)),
                  pl.BlockSpec((seq, LANES), col(2)),
                  pl.BlockSpec((blk, 2 * LANES), lambda b, g, i: (b * nq + i, g)),
                  pl.BlockSpec((seq, 2 * LANES), lambda b, g, i: (b, g))],
        out_specs=[pl.BlockSpec((blk, LANES), lambda b, g, i: (b * nq + i, g)),
                   pl.BlockSpec((None, None, None, 2, 1, blk), lambda b, g, i: (b, g, i, 0, 0, 0))],
        out_shape=[jax.ShapeDtypeStruct((m, D_MODEL), BF16), jax.ShapeDtypeStruct((n_batch, N_PAIR, nq, 2, 1, blk), F32)],
        scratch_shapes=[pltpu.VMEM((2, seq, LANES), BF16), pltpu.VMEM((nq, LANES, blk), BF16)],
        compiler_params=_params(("parallel", "parallel", "arbitrary")),
    )(z, z, z, qterm, kterm)


def fox_pair_bwd(name, z, qterm, kterm, o, do, lse, *, n_batch, seq):
    m = n_batch * seq
    blk = min(ATT_BLK, seq)
    nq = seq // blk
    dh = C_HEAD_DIM

    def body(zq_ref, zk_ref, zv_ref, qt_ref, kt_ref, o_ref, do_ref, lse_ref, dq_ref, dk_ref, dv_ref, dc_ref,
             qa_ref, doh_ref, del_ref, dqt_ref, dk_acc, dv_acc):
        g, j = pl.program_id(1), pl.program_id(2)
        lane = lax.broadcasted_iota(jnp.int32, (blk, LANES), 1)

        @pl.when(jnp.logical_and(g == 0, j == 0))
        def _():
            dc_ref[...] = jnp.zeros_like(dc_ref)

        @pl.when(j == 0)
        def _():
            zq = zq_ref[...] * dh ** -0.5
            dov = do_ref[...]
            for e in range(2):
                qa_ref[e] = _head_tile(zq, qt_ref[:, e * LANES:(e + 1) * LANES], e).astype(BF16)
                doh_ref[e] = _head_only(dov, e)
            for cb in range(nq):
                rows = slice(cb * blk, (cb + 1) * blk)
                prod_t = (do_ref[rows, :].astype(F32) * o_ref[rows, :].astype(F32)).T
                for e in range(2):
                    del_ref[cb, e] = jnp.sum(prod_t[e * dh:(e + 1) * dh], axis=0, keepdims=True)
            dqt_ref[...] = jnp.zeros_like(dqt_ref)

        zk, zv = zk_ref[...], zv_ref[...]
        ka32 = [_head_tile(zk, kt_ref[:, e * LANES:(e + 1) * LANES], e) for e in range(2)]
        ka = [t.astype(BF16) for t in ka32]
        kat = [t.T.astype(BF16) for t in ka32]
        vh = [_head_only(zv, e) for e in range(2)]
        dk_acc[...] = jnp.zeros_like(dk_acc)
        dv_acc[...] = jnp.zeros_like(dv_acc)

        def block(i, diagonal):
            rows = pl.ds(pl.multiple_of(i * blk, blk), blk)
            for e in range(2):
                qv, dov = qa_ref[e, rows, :], doh_ref[e, rows, :]
                p = jnp.exp(_dg(ka[e], qv, "nt") - lse_ref[i, e])
                if diagonal:
                    p = jnp.where(_causal(p), p, 0.0)
                dv_acc[...] += _dg(p, dov, "nn")
                ds = p * (_dg(vh[e], dov, "nt") - del_ref[i, e])
                dk_acc[e] += _dg(ds, qv, "nn")
                dqt_ref[i, e] += _dg(kat[e], ds, "nn")

        block(j, True)

        def rest(i, carry):
            block(i, False)
            return carry

        lax.fori_loop(j + 1, nq, rest, 0)
        dk0, dk1 = dk_acc[0], dk_acc[1]
        dk_ref[...] = _pair_tile(dk0, dk1).astype(dk_ref.dtype)
        dv_ref[...] = dv_acc[...].astype(dv_ref.dtype)
        rows_j = pl.ds(pl.multiple_of(j * blk, blk), blk)
        for e, dke in enumerate((dk0, dk1)):
            dc_ref[rows_j, :] -= jnp.where(lane == 2 * g + e, _lane_col(dke, dh + 3), 0.0)

        @pl.when(j == nq - 1)
        def _():
            for i in range(nq):
                nat = [dqt_ref[i, e].T for e in range(2)]
                rows = slice(i * blk, (i + 1) * blk)
                dq_ref[rows, :] = (_pair_tile(nat[0], nat[1]) * dh ** -0.5).astype(dq_ref.dtype)
                for e in range(2):
                    dc_ref[rows, :] += jnp.where(lane == 2 * g + e, _lane_col(nat[e], dh), 0.0)

    col = lambda part: (lambda b, g, j: (b, part * N_PAIR + g))
    colj = lambda part: (lambda b, g, j: (b * nq + j, part * N_PAIR + g))
    pair = jax.ShapeDtypeStruct((m, D_MODEL), BF16)
    return pl.pallas_call(
        body, grid=(n_batch, N_PAIR, nq), name=name,
        in_specs=[pl.BlockSpec((seq, LANES), col(0)),
                  pl.BlockSpec((blk, LANES), colj(1)),
                  pl.BlockSpec((blk, LANES), colj(2)),
                  pl.BlockSpec((seq, 2 * LANES), lambda b, g, j: (b, g)),
                  pl.BlockSpec((blk, 2 * LANES), lambda b, g, j: (b * nq + j, g)),
                  pl.BlockSpec((seq, LANES), col(0)),
                  pl.BlockSpec((seq, LANES), col(0)),
                  pl.BlockSpec((None, None, nq, 2, 1, blk), lambda b, g, j: (b, g, 0, 0, 0, 0))],
        out_specs=[pl.BlockSpec((seq, LANES), col(0)),
                   pl.BlockSpec((blk, LANES), colj(0)),
                   pl.BlockSpec((blk, LANES), colj(0)),
                   pl.BlockSpec((seq, LANES), lambda b, g, j: (b, 0))],
        out_shape=[pair, pair, pair, jax.ShapeDtypeStruct((m, LANES), F32)],
        scratch_shapes=[pltpu.VMEM((2, seq, LANES), BF16), pltpu.VMEM((2, seq, LANES), BF16),
                        pltpu.VMEM((nq, 2, 1, blk), F32), pltpu.VMEM((nq, 2, LANES, blk), F32),
                        pltpu.VMEM((2, blk, LANES), F32), pltpu.VMEM((blk, LANES), F32)],
        compiler_params=_params(("arbitrary", "arbitrary", "arbitrary")),
    )(z, z, z, qterm, kterm, o, do, lse)


def _split3(c):
    c1 = c.astype(BF16)
    r1 = c - c1.astype(F32)
    c2 = r1.astype(BF16)
    c3 = (r1 - c2.astype(F32)).astype(BF16)
    return c1, c2, c3


def _mesh_pos():
    return lax.axis_index("x"), lax.axis_index("y"), lax.axis_index("c")


def _flip(v, bit):
    return 1 - v if bit else v


def all_gather(name, blocks):
    n = len(blocks)

    def body(*refs):
        x_refs, out_refs = refs[:n], refs[n:2 * n]
        send_sems, recv_sems, local_sems = refs[2 * n:]
        x, y, c = _mesh_pos()
        me, sibling = (x, y, c), (x, y, 1 - c)
        chips = [(1 - x, y), (x, 1 - y), (1 - x, 1 - y)]

        def slot(a, px, py, pc):
            return out_refs[a].at[4 * px + 2 * py + pc]

        def copy(a, k, blk, to, src=None):
            return pltpu.make_async_remote_copy(
                src_ref=slot(a, *blk) if src is None else src, dst_ref=slot(a, *blk),
                send_sem=send_sems.at[a, k], recv_sem=recv_sems.at[a, k], device_id=to, device_id_type=MESH)

        mine = [pltpu.make_async_copy(x_refs[a], slot(a, *me), local_sems.at[a]) for a in range(n)]
        for cp in mine:
            cp.start()
        sends = []
        for a in range(n):
            sends.append(copy(a, 0, me, sibling, src=x_refs[a]))
            sends += [copy(a, 1 + j, me, (*chip, c), src=x_refs[a]) for j, chip in enumerate(chips)]
        for cp in sends:
            cp.start()
        for j, chip in enumerate(chips):
            for a in range(n):
                copy(a, 1 + j, (*chip, c), me).wait_recv()
                passed = copy(a, 4 + j, (*chip, c), sibling)
                passed.start()
                sends.append(passed)
        for a in range(n):
            copy(a, 0, sibling, me).wait_recv()
            for j, chip in enumerate(chips):
                copy(a, 4 + j, (*chip, 1 - c), me).wait_recv()
        for cp in sends:
            cp.wait_send()
        for cp in mine:
            cp.wait()

    hbm = pl.BlockSpec(memory_space=pl.ANY)
    return pl.pallas_call(
        body, name=name, out_shape=[jax.ShapeDtypeStruct((N_DEV,) + b.shape, b.dtype) for b in blocks],
        in_specs=[hbm] * n, out_specs=[hbm] * n,
        scratch_shapes=[pltpu.SemaphoreType.DMA((n, 7)), pltpu.SemaphoreType.DMA((n, 7)), pltpu.SemaphoreType.DMA((n,))],
    )(*blocks)


def _peers(x, y, c):
    return [(_flip(x, k & 4), _flip(y, k & 2), _flip(c, k & 1)) for k in range(1, N_DEV)]


def gather_start(name, blocks, lands):
    n = len(blocks)

    def body(*refs):
        x_refs, land_refs = refs[:n], refs[n:2 * n]
        send_sems, recv_sems = refs[2 * n], refs[2 * n + 1]
        token = refs[-1]
        x, y, c = _mesh_pos()
        me = 4 * x + 2 * y + c
        for k, peer in enumerate(_peers(x, y, c)):
            for a in range(n):
                pltpu.make_async_remote_copy(
                    src_ref=x_refs[a], dst_ref=land_refs[a].at[me], send_sem=send_sems.at[7 * a + k], recv_sem=recv_sems.at[7 * a + k],
                    device_id=peer, device_id_type=MESH).start()
        token[...] = jnp.zeros_like(token)

    hbm = pl.BlockSpec(memory_space=pltpu.HBM)
    sem = pl.BlockSpec(memory_space=pltpu.SEMAPHORE)
    out_shape = ([pltpu.SemaphoreType.DMA((7 * n,)), pltpu.SemaphoreType.DMA((7 * n,))]
                 + [pltpu.HBM(b.shape, b.dtype) for b in blocks] + [pltpu.HBM(l.shape, l.dtype) for l in lands]
                 + [jax.ShapeDtypeStruct((8, LANES), F32)])
    res = pl.pallas_call(
        body, name=name, out_shape=out_shape, in_specs=[hbm] * (2 * n),
        out_specs=[sem, sem] + [hbm] * (2 * n) + [pl.BlockSpec(memory_space=pltpu.VMEM)],
        input_output_aliases={a: 2 + a for a in range(2 * n)},
        compiler_params=pltpu.CompilerParams(has_side_effects=pltpu.SideEffectType.DATAFLOW_SIDE_EFFECTING),
    )(*[pltpu.with_memory_space_constraint(b, pltpu.HBM) for b in blocks],
      *[pltpu.with_memory_space_constraint(l, pltpu.HBM) for l in lands])
    return res[0], res[1], res[2:2 + n], res[2 + n:2 + 2 * n], res[-1]


def gather_wait(name, send_sems, recv_sems, blocks, lands, after):
    n = len(blocks)

    def body(*refs):
        x_refs, land_refs = refs[:n], refs[n:2 * n]
        s_sems, r_sems = refs[2 * n], refs[2 * n + 1]
        x, y, c = _mesh_pos()
        me = 4 * x + 2 * y + c
        for k, peer in enumerate(_peers(x, y, c)):
            for a in range(n):
                cp = pltpu.make_async_remote_copy(
                    src_ref=x_refs[a], dst_ref=land_refs[a].at[me], send_sem=s_sems.at[7 * a + k], recv_sem=r_sems.at[7 * a + k],
                    device_id=peer, device_id_type=MESH)
                cp.wait_send()
                cp.wait_recv()

    hbm = pl.BlockSpec(memory_space=pltpu.HBM)
    sem = pl.BlockSpec(memory_space=pltpu.SEMAPHORE)
    res = pl.pallas_call(
        body, name=name,
        out_shape=[pltpu.HBM(b.shape, b.dtype) for b in blocks] + [pltpu.HBM(l.shape, l.dtype) for l in lands],
        in_specs=[hbm] * (2 * n) + [sem, sem, pl.BlockSpec(memory_space=pl.ANY)], out_specs=[hbm] * (2 * n),
        input_output_aliases={a: a for a in range(2 * n)},
        compiler_params=pltpu.CompilerParams(has_side_effects=pltpu.SideEffectType.DATAFLOW_SIDE_EFFECTING),
    )(*blocks, *lands, send_sems, recv_sems, after)
    return res[n:]


def _split_exchange(name, sends, lands, sems, after):
    n = len(sends)
    starting = sems is None

    def body(*refs):
        s_refs, l_refs = refs[:n], refs[n:2 * n]
        send_sems, recv_sems = refs[2 * n], refs[2 * n + 1]
        x, y, c = _mesh_pos()
        me = 4 * x + 2 * y + c
        for k, (px, py, pc) in enumerate(_peers(x, y, c)):
            for a in range(n):
                cp = pltpu.make_async_remote_copy(
                    src_ref=s_refs[a].at[4 * px + 2 * py + pc], dst_ref=l_refs[a].at[me],
                    send_sem=send_sems.at[7 * a + k], recv_sem=recv_sems.at[7 * a + k],
                    device_id=(px, py, pc), device_id_type=MESH)
                if starting:
                    cp.start()
                else:
                    cp.wait_send()
                    cp.wait_recv()
        if starting:
            refs[-1][...] = jnp.zeros_like(refs[-1])

    hbm = pl.BlockSpec(memory_space=pltpu.HBM)
    sem = pl.BlockSpec(memory_space=pltpu.SEMAPHORE)
    thru = [pltpu.HBM(t.shape, t.dtype) for t in list(sends) + list(lands)]
    effect = pltpu.CompilerParams(has_side_effects=pltpu.SideEffectType.DATAFLOW_SIDE_EFFECTING)
    if starting:
        res = pl.pallas_call(
            body, name=name, in_specs=[hbm] * (2 * n),
            out_shape=[pltpu.SemaphoreType.DMA((7 * n,)), pltpu.SemaphoreType.DMA((7 * n,))] + thru + [jax.ShapeDtypeStruct((8, LANES), F32)],
            out_specs=[sem, sem] + [hbm] * (2 * n) + [pl.BlockSpec(memory_space=pltpu.VMEM)],
            input_output_aliases={a: 2 + a for a in range(2 * n)}, compiler_params=effect,
        )(*[pltpu.with_memory_space_constraint(t, pltpu.HBM) for t in list(sends) + list(lands)])
        return res[0], res[1], res[2:2 + n], res[2 + n:2 + 2 * n], res[-1]
    res = pl.pallas_call(
        body, name=name, out_shape=thru, in_specs=[hbm] * (2 * n) + [sem, sem, pl.BlockSpec(memory_space=pl.ANY)],
        out_specs=[hbm] * (2 * n), input_output_aliases={a: a for a in range(2 * n)}, compiler_params=effect,
    )(*sends, *lands, sems[0], sems[1], after)
    return res[n:]


def own_slot_only(send, me):
    mine = lax.dynamic_index_in_dim(send, me, 0, keepdims=False)
    return lax.dynamic_update_index_in_dim(lax.empty(send.shape, send.dtype), mine, me, 0)


def all_to_all(name, sends):
    n = len(sends)

    def body(*refs):
        s_refs, r_refs = refs[:n], refs[n:2 * n]
        send_sems, recv_sems, local_sems = refs[2 * n:]
        x, y, c = _mesh_pos()
        me = 4 * x + 2 * y + c
        mine = [pltpu.make_async_copy(s_refs[a].at[me], r_refs[a].at[me], local_sems.at[a]) for a in range(n)]
        for cp in mine:
            cp.start()
        copies = []
        for k in range(1, N_DEV):
            px, py, pc = _flip(x, k & 4), _flip(y, k & 2), _flip(c, k & 1)
            for a in range(n):
                copies.append(pltpu.make_async_remote_copy(
                    src_ref=s_refs[a].at[4 * px + 2 * py + pc], dst_ref=r_refs[a].at[me],
                    send_sem=send_sems.at[a, k - 1], recv_sem=recv_sems.at[a, k - 1],
                    device_id=(px, py, pc), device_id_type=MESH))
        for cp in copies:
            cp.start()
        for cp in copies:
            cp.wait_recv()
        for cp in copies:
            cp.wait_send()
        for cp in mine:
            cp.wait()

    hbm = pl.BlockSpec(memory_space=pl.ANY)
    return pl.pallas_call(
        body, name=name, out_shape=[jax.ShapeDtypeStruct(s.shape, s.dtype) for s in sends],
        in_specs=[hbm] * n, out_specs=[hbm] * n,
        scratch_shapes=[pltpu.SemaphoreType.DMA((n, 7)), pltpu.SemaphoreType.DMA((n, 7)), pltpu.SemaphoreType.DMA((n,))],
    )(*sends)


def _row_tile(r, cap, step):
    return next((t for t in range(cap, step - 1, -step) if r % t == 0), r)


def _sum_parts(p, n):
    t = [p[k].astype(F32) for k in range(n)]
    while len(t) > 1:
        t = [t[k] + t[k + 1] for k in range(0, len(t), 2)]
    return t[0]


def _adam(g, w, m, v):
    m = ADAM_B1 * m + (1.0 - ADAM_B1) * g
    v = ADAM_B2 * v + (1.0 - ADAM_B2) * (g * g)
    m_hat = m / (1.0 - ADAM_B1 ** ADAM_STEP)
    v_hat = v / (1.0 - ADAM_B2 ** ADAM_STEP)
    return -ADAM_LR * (m_hat / (jnp.sqrt(v_hat) + ADAM_EPS) + ADAM_WD * w), m, v


def adam_tiled(name, partials, w, m_, v_, layer=0, prev=None):
    _, r, c = w.shape
    n_part = partials.shape[0]
    tr = _row_tile(r, 256, 16)

    def body(*refs):
        p_ref, w_ref, m_ref, v_ref = refs[:4]
        g_ref, d_ref, nm_ref, nv_ref = refs[-4:]
        g = _sum_parts(p_ref, n_part)
        g_ref[...] = g
        d_ref[...], nm_ref[...], nv_ref[...] = _adam(g, w_ref[...], m_ref[...], v_ref[...])

    spec = pl.BlockSpec((None, tr, c), lambda i: (layer, i, 0))
    in_specs = [pl.BlockSpec((n_part, None, tr, c), lambda i: (0, 0, i, 0)), spec, spec, spec]
    args = [partials, w, m_, v_]
    aliases = {}
    if prev is not None:
        in_specs += [pl.BlockSpec(memory_space=pl.ANY)] * 4
        args += list(prev)
        aliases = {4 + k: k for k in range(4)}
    return pl.pallas_call(
        body, grid=(r // tr,), name=name, in_specs=in_specs,
        out_specs=[spec] * 4, out_shape=[jax.ShapeDtypeStruct(w.shape, F32)] * 4,
        input_output_aliases=aliases, compiler_params=_params(("parallel",)),
    )(*args)


def adam_small(name, items, extra):
    n, ne = len(items), len(extra)

    def body(*refs):
        ins, outs = refs[:4 * n + ne], refs[4 * n + ne:]
        for a in range(n):
            p_ref, w_ref, m_ref, v_ref = ins[4 * a:4 * a + 4]
            g = _sum_parts(p_ref, N_DEV)
            outs[4 * a][...] = g
            outs[4 * a + 1][...], outs[4 * a + 2][...], outs[4 * a + 3][...] = _adam(g, w_ref[...], m_ref[...], v_ref[...])
        for e in range(ne):
            outs[4 * n + e][...] = _sum_parts(ins[4 * n + e], N_DEV)

    args, out_shape = [], []
    for p, w, m_, v_ in items:
        args += [p, w, m_, v_]
        out_shape += [jax.ShapeDtypeStruct(w.shape, F32)] * 4
    for e in extra:
        args.append(e)
        out_shape.append(jax.ShapeDtypeStruct(e.shape[1:], F32))
    vmem = pl.BlockSpec(memory_space=pltpu.VMEM)
    res = pl.pallas_call(body, name=name, in_specs=[vmem] * len(args), out_specs=[vmem] * len(out_shape), out_shape=out_shape)(*args)
    return [res[4 * a:4 * a + 4] for a in range(n)], res[4 * n:]


def _cols_from_gather(g):
    g = jnp.moveaxis(g, 0, -2)
    return g.reshape(g.shape[:-2] + (g.shape[-2] * g.shape[-1],))


def _cols_to_blocks(w):
    w = w.reshape(w.shape[:-1] + (N_DEV, w.shape[-1] // N_DEV))
    return jnp.moveaxis(w, -2, 0)


def _block_diag(w):
    z = jnp.zeros((B_BLOCK_DIM, B_BLOCK_DIM), w.dtype)
    rows = []
    for j in range(B_BLOCKS // 2):
        top = jnp.concatenate([w[2 * j], z], axis=1)
        bot = jnp.concatenate([z, w[2 * j + 1]], axis=1)
        rows.append(jnp.concatenate([top, bot], axis=0))
    return jnp.concatenate(rows, axis=0)


def _block_diag_grad(d):
    out = []
    for j in range(B_BLOCKS // 2):
        blk = d[128 * j:128 * (j + 1)]
        out.append(blk[:64, :64])
        out.append(blk[64:, 64:])
    return jnp.stack(out)


NAMES = ("norm_gains", "even_w_in", "hgrn_lb_logits", "hgrn_norm", "rg_conv_w", "rg_conv_b", "rg_wa", "rg_ba", "rg_wx", "rg_bx",
         "rg_lambda", "even_w_out", "odd_w_in", "fox_f_bias", "odd_w_out", "ffn_w_up", "ffn_conv_w", "ffn_conv_b", "ffn_w_down")
SMALL_SHARDED = ("norm_gains", "rg_conv_w", "ffn_conv_w")
REPLICATED = ("hgrn_lb_logits", "hgrn_norm", "rg_conv_b", "rg_wa", "rg_ba", "rg_wx", "rg_bx", "rg_lambda", "fox_f_bias", "ffn_conv_b")


def _ffn_forward(tag, layer, h, w_up_g, cw5, cb5, w_down_g, m, seq):
    tm = _div_tile(m, 1024)
    nm = m // tm
    hid = mm(f"{tag}_up", "nn",
             Blk(h, (tm, D_MODEL), lambda i, j, k: (i, 0)),
             Blk(w_up_g, (None, None, D_MODEL, FF_BLK), lambda i, j, k: (j, 0, 0, 0)),
             Blk((N_DEV, m, FF_BLK), (None, tm, FF_BLK), lambda i, j, k: (j, i, 0)), F32, (nm, N_DEV, 1))
    hid = hid.reshape(2, N_DEV // 2, m, FF_BLK)
    act = ffn_mid_fwd(f"{tag}_mid", hid, cw5, cb5, layer, m=m, seq=seq)
    f = mm(f"{tag}_down", "nn",
           Blk(act, (None, tm, FF_BLK), lambda i, j, k: (k, i, 0)),
           Blk(w_down_g, (2, None, FF_BLK // 2, D_MODEL), lambda i, j, k: (k, 0, 0, 0)),
           Blk((m, D_MODEL), (tm, D_MODEL), lambda i, j, k: (i, 0)), F32, (nm, 1, N_DEV // 2))
    return hid, act, f


def _ffn_backward(tag, layer, df, h, hid, act, w_up_g, cw5, cb5, w_down_g, m, seq):
    tm = _div_tile(m, 1024)
    nm = m // tm
    dact = mm(f"{tag}_dact", "nt",
              Blk(df, (tm, D_MODEL), lambda i, j, k: (i, 0)),
              Blk(w_down_g, (2, None, FF_BLK // 2, D_MODEL), lambda i, j, k: (j, 0, 0, 0)),
              Blk((N_DEV // 2, m, FF_BLK), (None, tm, FF_BLK), lambda i, j, k: (j, i, 0)), BF16, (nm, N_DEV // 2, 1))
    d_wdown = mm(f"{tag}_dwdown", "tn",
                 Blk(act, (None, tm, FF_BLK), lambda i, j, k: (i, k, 0)),
                 Blk(df, (tm, D_MODEL), lambda i, j, k: (k, 0)),
                 Blk(w_down_g.shape, (2, None, FF_BLK // 2, D_MODEL), lambda i, j, k: (i, 0, 0, 0)), BF16,
                 (N_DEV // 2, 1, nm))
    dhid, d_cw, d_cb = ffn_mid_bwd(f"{tag}_dmid", hid, cw5, cb5, dact, layer, m=m, seq=seq)
    dhid = dhid.reshape(N_DEV, m, FF_BLK)
    dh = mm(f"{tag}_dh", "nt",
            Blk(dhid, (None, tm, FF_BLK), lambda i, j, k: (k, i, 0)),
            Blk(w_up_g, (None, None, D_MODEL, FF_BLK), lambda i, j, k: (k, 0, 0, 0)),
            Blk((m, D_MODEL), (tm, D_MODEL), lambda i, j, k: (i, 0)), BF16, (nm, 1, N_DEV))
    d_wup = mm(f"{tag}_dwup", "tn",
               Blk(h, (tm, D_MODEL), lambda i, j, k: (k, 0)),
               Blk(dhid, (None, tm, FF_BLK), lambda i, j, k: (j, k, 0)),
               Blk(w_up_g.shape, (None, None, D_MODEL, FF_BLK), lambda i, j, k: (j, 0, 0, 0)), BF16,
               (1, N_DEV, nm))
    return dh, d_wup, d_cw, d_cb, d_wdown


def kernel(x, norm_gains, even_w_in, hgrn_lb_logits, hgrn_norm, rg_conv_w, rg_conv_b, rg_wa, rg_ba, rg_wx, rg_bx, rg_lambda, even_w_out, odd_w_in, fox_f_bias, odd_w_out, ffn_w_up, ffn_conv_w, ffn_conv_b, ffn_w_down, loss_target, m_norm_gains, m_even_w_in, m_hgrn_lb_logits, m_hgrn_norm, m_rg_conv_w, m_rg_conv_b, m_rg_wa, m_rg_ba, m_rg_wx, m_rg_bx, m_rg_lambda, m_even_w_out, m_odd_w_in, m_fox_f_bias, m_odd_w_out, m_ffn_w_up, m_ffn_conv_w, m_ffn_conv_b, m_ffn_w_down, v_norm_gains, v_even_w_in, v_hgrn_lb_logits, v_hgrn_norm, v_rg_conv_w, v_rg_conv_b, v_rg_wa, v_rg_ba, v_rg_wx, v_rg_bx, v_rg_lambda, v_even_w_out, v_odd_w_in, v_fox_f_bias, v_odd_w_out, v_ffn_w_up, v_ffn_conv_w, v_ffn_conv_b, v_ffn_w_down):
    local = dict(locals())
    w = {n: local[n] for n in NAMES}
    mom = {n: local["m_" + n] for n in NAMES}
    var = {n: local["v_" + n] for n in NAMES}
    n_batch, seq, _ = x.shape
    m = n_batch * seq
    tm = _div_tile(m, 512)
    tmm = _div_tile(m, 1024)
    nm = m // tmm

    now = [w["even_w_in"], w["even_w_out"]]
    gathered = all_gather("gather_weights", [t.astype(BF16) for t in now] + [w[n] for n in SMALL_SHARDED])
    g = dict(zip(("even_w_in", "even_w_out") + SMALL_SHARDED, gathered))
    w_in_e = g["even_w_in"]
    w_out_e = g["even_w_out"].reshape(D_MODEL, D_MODEL)
    gains = _cols_from_gather(g["norm_gains"])
    me = 4 * lax.axis_index("x") + 2 * lax.axis_index("y") + lax.axis_index("c")
    own_block_only = lambda t: lax.dynamic_update_index_in_dim(lax.empty((N_DEV,) + t.shape, t.dtype), t, me, 0)
    behind = (g["norm_gains"][0, 0, 0, 0] * 0.0).astype(BF16)
    ffn0 = [w["ffn_w_up"][0:1].astype(BF16) + behind, w["ffn_w_down"][0:1].astype(BF16) + behind]
    ffn0_sent = gather_start("gather_ffn0_start", ffn0, [own_block_only(t) for t in ffn0])
    behind = (ffn0_sent[4][0, 0] * 0.0).astype(BF16)
    mix1w = [w["odd_w_in"].astype(BF16) + behind, w["odd_w_out"].astype(BF16) + behind]
    mix1_sent = gather_start("gather_mix1_start", mix1w, [own_block_only(t) for t in mix1w])
    behind = (mix1_sent[4][0, 0] * 0.0).astype(BF16)
    ffn1 = [w["ffn_w_up"][1:2].astype(BF16) + behind, w["ffn_w_down"][1:2].astype(BF16) + behind]
    ffn1_sent = gather_start("gather_ffn1_start", ffn1, [own_block_only(t) for t in ffn1])
    started = ffn1_sent[4]
    rg_cw = _cols_from_gather(g["rg_conv_w"])[0]
    n_layer = ffn_conv_w.shape[0]
    cw5 = g["ffn_conv_w"].reshape(2, N_DEV // 2, n_layer, FFN_CONV, FF_BLK)
    cb5 = ffn_conv_b.reshape(n_layer, 2, N_DEV // 2, 1, FF_BLK)
    gain = lambda l, k: gains[l, k:k + 1, :]
    wa_bd, wx_bd = _block_diag(rg_wa[0]), _block_diag(rg_wx[0])
    fbias = jnp.pad(fox_f_bias, ((0, 0), (0, LANES - C_HEADS)))

    x0 = x.reshape(m, D_MODEL)
    tgt = loss_target.reshape(m, D_MODEL)

    (h0,) = tile_fwd("l0_prenorm", fn_prenorm_after, m=m, tm=tm, nj=1, rows=[Row(x0)], pars=[Par(gain(0, 0)), Par(started)],
                     outs=[Out(D_MODEL, BF16)])
    z0 = mm("l0_in", "nn",
            Blk(h0, (tmm, D_MODEL), lambda i, j, k: (i, 0)),
            Blk(w_in_e, (2, None, D_MODEL, 384), lambda i, j, k: (j, 0, 0, 0)),
            Blk((m, 3072), (tmm, 768), lambda i, j, k: (i, j)), F32, (nm, N_DEV // 2, 1), b_join=True)
    oa, sprev = hgrn_fwd("l0_hgrn", z0, hgrn_lb_logits, hgrn_norm, n_batch=n_batch, seq=seq)
    rg_rows = lambda: [Row(z0, LANES, 16), Row(z0, LANES, 20)]
    rg_pars = lambda: [Par(rg_cw, "col", LANES), Par(rg_conv_b, "col", LANES), Par(wa_bd, "row", LANES), Par(rg_ba, "col", LANES),
                       Par(wx_bd, "row", LANES), Par(rg_bx, "col", LANES), Par(rg_lambda, "col", LANES)]
    (ob,) = tile_fwd("l0_rglru", fn_rglru, m=m, tm=seq, nj=B_WIDTH // LANES, rows=rg_rows(), pars=rg_pars(),
                     outs=[Out(B_WIDTH, BF16, LANES)])
    mixcat0 = jnp.concatenate([oa, ob], axis=-1)
    mix0 = mm2d("l0_out", "nn", mixcat0, w_out_e)
    x1, h1 = tile_fwd("l0_postnorm", fn_addnorm2, m=m, tm=tm, nj=1, rows=[Row(x0), Row(mix0)], pars=[Par(gain(0, 1)), Par(gain(0, 2))],
                      outs=[Out(D_MODEL, F32), Out(D_MODEL, BF16)])
    w_up_g0, w_down_g0 = gather_wait("gather_ffn0_wait", ffn0_sent[0], ffn0_sent[1], ffn0_sent[2], ffn0_sent[3], h1)
    hid0, act0, f0 = _ffn_forward("l0_ffn", 0, h1, w_up_g0, cw5, cb5, w_down_g0, m, seq)
    x2, h2 = tile_fwd("l0_ffnnorm", fn_addnorm2, m=m, tm=tm, nj=1, rows=[Row(x1), Row(f0)], pars=[Par(gain(0, 3)), Par(gain(1, 0))],
                      outs=[Out(D_MODEL, F32), Out(D_MODEL, BF16)])

    g_in_o, g_out_o = gather_wait("gather_mix1_wait", mix1_sent[0], mix1_sent[1], mix1_sent[2], mix1_sent[3], h2)
    w_in_o = jnp.pad(_cols_from_gather(g_in_o)[0], ((0, 0), (0, 3200 - 3088)))
    w_out_o = g_out_o.reshape(D_MODEL, D_MODEL)
    z1 = mm2d("l1_in", "nn", h2, w_in_o)
    (cgate,) = tile_fwd("l1_gate", fn_fox_gate, m=m, tm=seq, nj=1, rows=[Row(z1, LANES, 3072 // LANES)], pars=[Par(fbias)],
                        outs=[Out(LANES, F32)])
    place, ones_q, ones_k = term_placement()
    qterm, kterm = tile_fwd("l1_terms", fn_fox_terms, m=m, tm=tm, nj=1, rows=[Row(cgate)],
                            pars=[Par(place), Par(ones_q), Par(ones_k)], outs=[Out(TERM_W, BF16), Out(TERM_W, BF16)])
    oc, lse = fox_pair_fwd("l1_attn", z1, qterm, kterm, n_batch=n_batch, seq=seq)
    mix1 = mm2d("l1_out", "nn", oc, w_out_o)
    x3, h3 = tile_fwd("l1_postnorm", fn_addnorm2, m=m, tm=tm, nj=1, rows=[Row(x2), Row(mix1)], pars=[Par(gain(1, 1)), Par(gain(1, 2))],
                      outs=[Out(D_MODEL, F32), Out(D_MODEL, BF16)])
    w_up_g1, w_down_g1 = gather_wait("gather_ffn1_wait", ffn1_sent[0], ffn1_sent[1], ffn1_sent[2], ffn1_sent[3], h3)
    hid1, act1, f1 = _ffn_forward("l1_ffn", 1, h3, w_up_g1, cw5, cb5, w_down_g1, m, seq)
    dy, loss_part = tile_fwd("loss", fn_final, m=m, tm=tm, nj=1, rows=[Row(x3), Row(f1), Row(tgt)], pars=[Par(gain(1, 3))],
                             outs=[Out(D_MODEL, F32)], n_acc=1)

    df1, d_g13 = tile_bwd("l1_dffnnorm", fn_rms_only, m=m, tm=tm, nj=1, rows=[Row(f1)], pars=[Par(gain(1, 3))], cts=[Row(dy)],
                          drows=[Out(D_MODEL, BF16)])
    dh3, d_wup1, d_cw1, d_cb1, d_wdown1 = _ffn_backward("l1_ffn", 1, df1, h3, hid1, act1, w_up_g1, cw5, cb5, w_down_g1, m, seq)
    dx2, dmix1, d_g11, d_g12 = tile_bwd("l1_dpostnorm", fn_addnorm2, m=m, tm=tm, nj=1, rows=[Row(x2), Row(mix1)],
                                        pars=[Par(gain(1, 1)), Par(gain(1, 2))], cts=[Row(dy), Row(dh3)],
                                        drows=[Out(D_MODEL, F32), Out(D_MODEL, BF16)])
    doc = mm2d("l1_doc", "nt", dmix1, w_out_o, BF16)
    d_wout_o = mm2d("l1_dwout", "tn", oc, dmix1)
    dq, dk, dv, dc = fox_pair_bwd("l1_dattn", z1, qterm, kterm, oc, doc, lse, n_batch=n_batch, seq=seq)
    dzf, d_fbias = tile_bwd("l1_dgate", fn_fox_gate, m=m, tm=seq, nj=1, rows=[Row(z1, LANES, 3072 // LANES)], pars=[Par(fbias)],
                            cts=[Row(dc)], drows=[Out(LANES, BF16)])
    dz1 = jnp.concatenate([dq, dk, dv, dzf], axis=-1)
    dh2 = mm2d("l1_dh", "nt", dz1, w_in_o, BF16)
    d_win_o = mm2d("l1_dwin", "tn", h2, dz1)

    send1 = [_cols_to_blocks(d_win_o[None, :, :3088]).astype(BF16),
             d_wout_o.reshape(N_DEV, 1, D_MODEL // N_DEV, D_MODEL).astype(BF16), d_wup1, d_wdown1]
    sent1 = _split_exchange("exchange_l1_start", send1, [own_slot_only(t, me) for t in send1], None, None)

    dx1, df0, d_g03, d_g10 = tile_bwd("l0_dffnnorm", fn_addnorm2_after, m=m, tm=tm, nj=1, rows=[Row(x1), Row(f0)],
                                      pars=[Par(gain(0, 3)), Par(gain(1, 0)), Par(sent1[4])], cts=[Row(dx2), Row(dh2)],
                                      drows=[Out(D_MODEL, F32), Out(D_MODEL, BF16)])[:4]
    dh1, d_wup0, d_cw0, d_cb0, d_wdown0 = _ffn_backward("l0_ffn", 0, df0, h1, hid0, act0, w_up_g0, cw5, cb5, w_down_g0, m, seq)
    send0 = [d_wup0, d_wdown0]
    sent0 = _split_exchange("exchange_ffn0_start", send0, [own_slot_only(t, me) for t in send0], None, None)
    dx0a, dmix0, d_g01, d_g02 = tile_bwd("l0_dpostnorm", fn_addnorm2_after, m=m, tm=tm, nj=1, rows=[Row(x0), Row(mix0)],
                                         pars=[Par(gain(0, 1)), Par(gain(0, 2)), Par(sent0[4])], cts=[Row(dx1), Row(dh1)],
                                         drows=[Out(D_MODEL, F32), Out(D_MODEL, BF16)])[:4]
    dmixcat0 = mm2d("l0_dmixcat", "nt", dmix0, w_out_e, BF16)
    d_wout_e = mm2d("l0_dwout", "tn", mixcat0, dmix0)
    dzq, dzf0, dzv, dzg, d_lb, d_hnorm = hgrn_bwd("l0_dhgrn", z0, sprev, hgrn_lb_logits, hgrn_norm, dmixcat0, n_batch=n_batch, seq=seq)
    dzx, dzy, d_rcw, d_rcb, d_wa, d_ba, d_wx, d_bx, d_lam = tile_bwd(
        "l0_drglru", fn_rglru, m=m, tm=seq, nj=B_WIDTH // LANES, rows=rg_rows(), pars=rg_pars(),
        cts=[Row(dmixcat0, LANES, A_WIDTH // LANES)], drows=[Out(B_WIDTH, BF16, LANES), Out(B_WIDTH, BF16, LANES)])
    dz0 = jnp.concatenate([dzq, dzf0, dzv, dzg, dzx, dzy], axis=-1)
    d_win_e = mm("l0_dwin", "tn",
                 Blk(h0, (tmm, D_MODEL), lambda i, j, k: (k, 0)),
                 Blk(dz0, (tmm, 768), lambda i, j, k: (k, j)),
                 Blk(w_in_e.shape, (2, None, D_MODEL, 384), lambda i, j, k: (j, 0, 0, 0)), BF16, (1, N_DEV // 2, nm), o_split=True)
    send_e = [d_win_e, d_wout_e.reshape(N_DEV, 1, D_MODEL // N_DEV, D_MODEL).astype(BF16)]
    sent_e = _split_exchange("exchange_even_start", send_e, [own_slot_only(t, me) for t in send_e], None, None)
    d_ffn_cb = jnp.stack([d_cb0, d_cb1]).reshape(n_layer, 2 * D_FF)
    rep = {"hgrn_lb_logits": d_lb, "hgrn_norm": d_hnorm, "rg_conv_b": d_rcb, "rg_wa": _block_diag_grad(d_wa)[None], "rg_ba": d_ba,
           "rg_wx": _block_diag_grad(d_wx)[None], "rg_bx": d_bx, "rg_lambda": d_lam, "fox_f_bias": d_fbias[:, :C_HEADS],
           "ffn_conv_b": d_ffn_cb}
    rep_blocks = [rep[n] for n in REPLICATED] + [loss_part]
    rep_sent = gather_start("gather_partials_start", rep_blocks, [own_block_only(t) for t in rep_blocks])
    dh0 = mm("l0_dh", "nt",
             Blk(dz0, (tmm, 768), lambda i, j, k: (i, k)),
             Blk(w_in_e, (2, None, D_MODEL, 384), lambda i, j, k: (k, 0, 0, 0)),
             Blk((m, D_MODEL), (tmm, D_MODEL), lambda i, j, k: (i, 0)), BF16, (nm, 1, N_DEV // 2), after=sent_e[4] + rep_sent[4],
             b_join=True)
    dx0, d_g00 = tile_bwd("l0_dprenorm", fn_input_norm, m=m, tm=tm, nj=1, rows=[Row(x0)], pars=[Par(gain(0, 0))],
                          cts=[Row(dx0a), Row(dh0)], drows=[Out(D_MODEL, F32)])

    d_gains = jnp.stack([jnp.concatenate([d_g00, d_g01, d_g02, d_g03], axis=0), jnp.concatenate([d_g10, d_g11, d_g12, d_g13], axis=0)])
    d_ffn_cw = jnp.stack([d_cw0, d_cw1], axis=2).reshape(N_DEV, n_layer, FFN_CONV, FF_BLK)
    r_in_o, r_out_o, r_up1, r_down1 = _split_exchange("exchange_l1_wait", sent1[2], sent1[3], sent1[:2], dx0)
    r_up0, r_down0 = _split_exchange("exchange_ffn0_wait", sent0[2], sent0[3], sent0[:2], dx0)
    r_in_e, r_out_e = _split_exchange("exchange_even_wait", sent_e[2], sent_e[3], sent_e[:2], dx0)
    recv, res = {}, {}
    for n, r in (("even_w_in", r_in_e), ("even_w_out", r_out_e), ("odd_w_in", r_in_o), ("odd_w_out", r_out_o)):
        res[n] = adam_tiled("adam_" + n, r, w[n], mom[n], var[n])
    for n, parts_l in (("ffn_w_up", (r_up0, r_up1)), ("ffn_w_down", (r_down0, r_down1))):
        first_layer = adam_tiled(f"adam_{n}_0", parts_l[0], w[n], mom[n], var[n], layer=0)
        res[n] = adam_tiled(f"adam_{n}_1", parts_l[1], w[n], mom[n], var[n], layer=1, prev=first_layer)
    small_send = [_cols_to_blocks(d_gains), _cols_to_blocks(d_rcw[None]), d_ffn_cw]
    recv.update(zip(SMALL_SHARDED, all_to_all("exchange_small", small_send)))

    parts = gather_wait("gather_partials_wait", rep_sent[0], rep_sent[1], rep_sent[2], rep_sent[3], dx0)
    for n, p in zip(REPLICATED, parts):
        recv[n] = p
    small = SMALL_SHARDED + REPLICATED
    small_res, (loss_sum,) = adam_small("adam_small", [(recv[n], w[n], mom[n], var[n]) for n in small], [parts[-1]])
    res.update(dict(zip(small, small_res)))

    out = [loss_sum[0, 0], dx0.reshape(x.shape)]
    for k in range(4):
        out += [res[n][k] for n in NAMES]
    return tuple(out)
```

```python
import functools

import jax
import jax.numpy as jnp
from jax import lax
from jax.experimental import pallas as pl
from jax.experimental.pallas import tpu as pltpu

F32 = jnp.float32
BF16 = jnp.bfloat16

D_MODEL = 1024
A_HEADS = 4
A_WIDTH = 512
HGRN_CHUNK = 64
HGRN_SEG = 512
B_WIDTH = 512
B_BLOCKS = 8
B_BLOCK_DIM = 64
B_CONV = 4
RG_C = 8.0
C_HEADS = 16
C_HEAD_DIM = 64
D_FF = 2816
FFN_CONV = 3
EPS = 1e-6
LANES = 128
HALO = 16
N_DEV = 8
FF_BLK = 2 * D_FF // N_DEV
MESH = pl.DeviceIdType.MESH
NEG = -1e30
VMEM_LIMIT = 56 * 1024 * 1024

ADAM_LR = 0.001
ADAM_B1 = 0.9
ADAM_B2 = 0.999
ADAM_EPS = 1e-08
ADAM_WD = 0.01
ADAM_STEP = 10


def _dg(a, b, pat):
    nb = a.ndim - 2
    batch = (tuple(range(nb)), tuple(range(nb)))
    ca = a.ndim - 1 if pat[0] == "n" else a.ndim - 2
    cb = b.ndim - 2 if pat[1] == "n" else b.ndim - 1
    return lax.dot_general(a.astype(BF16), b.astype(BF16), (((ca,), (cb,)), batch), preferred_element_type=F32)


@functools.partial(jax.custom_vjp, nondiff_argnums=(2,))
def bdot(a, b, pat):
    return _dg(a, b, pat)


def _bdot_fwd(a, b, pat):
    return _dg(a, b, pat), (a, b)


def _bdot_bwd(pat, res, g):
    a, b = res
    if pat == "nn":
        return _dg(g, b, "nt"), _dg(a, g, "tn")
    if pat == "nt":
        return _dg(g, b, "nn"), _dg(g, a, "tn")
    return _dg(b, g, "nt"), _dg(a, g, "nn")


bdot.defvjp(_bdot_fwd, _bdot_bwd)


def _shift_raw(x, s, up, fill):
    if s == 0:
        return x
    n = x.shape[0]
    r = pltpu.roll(x, (n - s) if up else s, 0)
    idx = lax.broadcasted_iota(jnp.int32, x.shape, 0)
    mask = (idx >= n - s) if up else (idx < s)
    return jnp.where(mask, jnp.asarray(fill, x.dtype), r)


@functools.partial(jax.custom_vjp, nondiff_argnums=(1,))
def shift_down(x, s):
    return _shift_raw(x, s, False, 0.0)


def _shift_down_fwd(x, s):
    return _shift_raw(x, s, False, 0.0), None


def _shift_down_bwd(s, _, g):
    return (_shift_raw(g, s, True, 0.0),)


shift_down.defvjp(_shift_down_fwd, _shift_down_bwd)


def _scan_impl(a, u, up):
    n = a.shape[0]
    s = 1
    while s < n:
        u = a * _shift_raw(u, s, up, 0.0) + u
        if 2 * s < n:
            a = a * _shift_raw(a, s, up, 1.0)
        s *= 2
    return u


@jax.custom_vjp
def lin_scan(a, u):
    return _scan_impl(a, u, False)


def _lin_scan_fwd(a, u):
    h = _scan_impl(a, u, False)
    return h, (a, h)


def _lin_scan_bwd(res, g):
    a, h = res
    gh = _scan_impl(_shift_raw(a, 1, True, 0.0), g, True)
    return gh * _shift_raw(h, 1, False, 0.0), gh


lin_scan.defvjp(_lin_scan_fwd, _lin_scan_bwd)


def _cumsum_impl(x, up, period):
    n = x.shape[0]
    span = n if period is None else period
    idx = lax.broadcasted_iota(jnp.int32, x.shape, 0)
    pos = idx if period is None else idx % period
    s = 1
    while s < span:
        sh = _shift_raw(x, s, up, 0.0)
        if period is not None:
            keep = (pos < period - s) if up else (pos >= s)
            sh = jnp.where(keep, sh, 0.0)
        x = x + sh
        s *= 2
    return x


@functools.partial(jax.custom_vjp, nondiff_argnums=(1,))
def cumsum_rows(x, period):
    return _cumsum_impl(x, False, period)


def _cumsum_fwd(x, period):
    return _cumsum_impl(x, False, period), None


def _cumsum_bwd(period, _, g):
    return (_cumsum_impl(g, True, period),)


cumsum_rows.defvjp(_cumsum_fwd, _cumsum_bwd)


def _sigmoid(x):
    return jax.nn.sigmoid(x)


def _expm1(x):
    return jnp.tanh(0.5 * x) * (jnp.exp(x) + 1.0)


def _softplus(x):
    return jnp.maximum(x, 0.0) + jnp.log(1.0 + jnp.exp(-jnp.abs(x)))


def _rms(x, g):
    return x * lax.rsqrt(jnp.mean(x * x, axis=-1, keepdims=True) + EPS) * g


def fn_prenorm(x, g):
    return (_rms(x, g).astype(BF16),)


def fn_prenorm_after(x, g, _token):
    return fn_prenorm(x, g)


def fn_addnorm2(x, y, g_post, g_pre):
    x1 = x + _rms(y, g_post)
    return x1, _rms(x1, g_pre).astype(BF16)


def fn_addnorm2_after(x, y, g_post, g_pre, _token):
    return fn_addnorm2(x, y, g_post, g_pre)


def fn_input_norm(x, g):
    return x, _rms(x, g).astype(BF16)


def _causal_conv(x, w, b, taps):
    c = b
    for k in range(taps):
        c = c + w[k:k + 1, :] * shift_down(x, taps - 1 - k)
    return c


def fn_rglru(xb, yb, cw, cb, wa, ba, wx, bx, lam):
    xf = _causal_conv(xb, cw, cb, B_CONV)
    r = _sigmoid(bdot(xf, wa, "nn") + ba)
    i = _sigmoid(bdot(xf, wx, "nn") + bx)
    log_a = -RG_C * r * _softplus(-lam)
    a = jnp.exp(log_a)
    u = jnp.sqrt(-_expm1(2.0 * log_a)) * (i * xf)
    h = lin_scan(a, u)
    return ((h * jax.nn.gelu(yb)).astype(BF16),)


def fn_fox_gate(zf, bias):
    return (cumsum_rows(jax.nn.log_sigmoid(zf + bias), None),)


def fn_hgrn_seg(q, fl, v, g, st, logits, hn):
    rows = q.shape[0]
    nc = rows // HGRN_CHUNK
    l0, l1, l2 = logits[0:1, :], logits[1:2, :], logits[2:3, :]
    mx = jnp.maximum(jnp.maximum(l0, l1), l2)
    e0, e1, e2 = jnp.exp(l0 - mx), jnp.exp(l1 - mx), jnp.exp(l2 - mx)
    lb = e0 / (e0 + e1 + e2)
    forget = lb + (1.0 - lb) * _sigmoid(fl)
    qs = q * _sigmoid(q)
    kk = 1.0 - forget
    logf = jnp.log(forget)
    bcum = cumsum_rows(logf, HGRN_CHUNK)
    c3 = lambda t: t.reshape(nc, HGRN_CHUNK, 128)
    b_last = jnp.sum(c3(logf), axis=1, keepdims=True)
    bcum3 = c3(bcum)
    q_dec = c3(qs) * jnp.exp(bcum3)
    k_dec = c3(kk) * jnp.exp(-bcum3)
    k_upd = c3(kk) * jnp.exp(b_last - bcum3)
    v3 = c3(v)
    scores = bdot(q_dec, k_dec, "nt")
    ri = lax.broadcasted_iota(jnp.int32, scores.shape, 1)
    ci = lax.broadcasted_iota(jnp.int32, scores.shape, 2)
    scores = jnp.where(ri >= ci, scores, 0.0)
    o = bdot(scores, v3, "nn")
    upd_t = bdot(v3, k_upd, "tn")
    dec = jnp.exp(b_last)
    prev = []
    for n in range(nc):
        prev.append(st)
        st = st * dec[n] + upd_t[n]
    o = o + bdot(q_dec, jnp.stack(prev), "nt")
    o = o.reshape(rows, 128)
    o = o * lax.rsqrt(jnp.mean(o * o, axis=-1, keepdims=True) + EPS) * hn
    return (o * _sigmoid(g)).astype(BF16), st


def _ffn_conv(xg, xv, cw, cb):
    cg = _causal_conv(xg, cw[0], cb[0], FFN_CONV)[HALO:]
    cv = _causal_conv(xv, cw[1], cb[1], FFN_CONV)[HALO:]
    return cg, cv


def _ffn_gate(cg, cv):
    return jax.nn.gelu(cg) * cv


class Row:
    def __init__(self, arr, cb=None, off=0):
        self.arr, self.cb, self.off = arr, cb, off

    def spec(self, tm):
        if self.cb is None:
            return pl.BlockSpec((tm, self.arr.shape[1]), lambda j, i: (i, 0))
        off = self.off
        return pl.BlockSpec((tm, self.cb), lambda j, i: (i, j + off))


class Par:
    def __init__(self, arr, kind="full", bs=None):
        self.arr, self.kind, self.bs = arr, kind, bs

    def block(self):
        if self.kind == "full":
            return self.arr.shape
        if self.kind == "col":
            return (self.arr.shape[0], self.bs)
        return (self.bs, self.arr.shape[1])

    def spec(self):
        if self.kind == "full":
            return pl.BlockSpec(self.block(), lambda j, i: (0, 0))
        if self.kind == "col":
            return pl.BlockSpec(self.block(), lambda j, i: (0, j))
        return pl.BlockSpec(self.block(), lambda j, i: (j, 0))


class Out:
    def __init__(self, width, dtype, cb=None, off=0):
        self.width, self.dtype, self.cb, self.off = width, dtype, cb, off

    def spec(self, tm):
        if self.cb is None:
            return pl.BlockSpec((tm, self.width), lambda j, i: (i, 0))
        off = self.off
        return pl.BlockSpec((tm, self.cb), lambda j, i: (i, j + off))


def _params(sem):
    return pltpu.CompilerParams(dimension_semantics=sem, vmem_limit_bytes=VMEM_LIMIT)


def tile_fwd(name, fn, *, m, tm, nj, rows, pars, outs, n_acc=0):
    n_r, n_p, n_o = len(rows), len(pars), len(outs)

    def body(*refs):
        ins = [r[...] for r in refs[:n_r + n_p]]
        res = fn(*ins)
        o_refs = refs[n_r + n_p:]
        for k in range(n_o):
            o_refs[k][...] = res[k].astype(o_refs[k].dtype)
        first = jnp.logical_and(pl.program_id(0) == 0, pl.program_id(1) == 0)
        for k in range(n_acc):
            ref = o_refs[n_o + k]

            @pl.when(first)
            def _():
                ref[...] = jnp.zeros_like(ref)

            ref[...] += res[n_o + k]

    out_shape = [jax.ShapeDtypeStruct((m, o.width), o.dtype) for o in outs]
    out_specs = [o.spec(tm) for o in outs]
    for _ in range(n_acc):
        out_shape.append(jax.ShapeDtypeStruct((1, LANES), F32))
        out_specs.append(pl.BlockSpec((1, LANES), lambda j, i: (0, 0)))
    sem = ("arbitrary", "arbitrary") if n_acc else ("parallel", "parallel")
    return pl.pallas_call(
        body, grid=(nj, m // tm), name=name,
        in_specs=[r.spec(tm) for r in rows] + [p.spec() for p in pars],
        out_specs=out_specs, out_shape=out_shape, compiler_params=_params(sem),
    )(*[r.arr for r in rows], *[p.arr for p in pars])


def tile_bwd(name, fn, *, m, tm, nj, rows, pars, cts, drows):
    n_r, n_p, n_c = len(rows), len(pars), len(cts)
    want = [k for k in range(n_r) if drows[k] is not None]

    def body(*refs):
        ins = [r[...] for r in refs[:n_r + n_p]]
        ct = [r[...] for r in refs[n_r + n_p:n_r + n_p + n_c]]
        o_refs = refs[n_r + n_p + n_c:]
        res, vjp = jax.vjp(fn, *ins)
        grads = vjp(tuple(c.astype(r.dtype) for c, r in zip(ct, res)))
        for pos, k in enumerate(want):
            o_refs[pos][...] = grads[k].astype(o_refs[pos].dtype)
        for k in range(n_p):
            ref = o_refs[len(want) + k]
            first = pl.program_id(1) == 0
            if pars[k].kind == "full":
                first = jnp.logical_and(first, pl.program_id(0) == 0)

            @pl.when(first)
            def _():
                ref[...] = jnp.zeros_like(ref)

            ref[...] += grads[n_r + k].astype(F32)

    out_shape = [jax.ShapeDtypeStruct((m, drows[k].width), drows[k].dtype) for k in want]
    out_specs = [drows[k].spec(tm) for k in want]
    for p in pars:
        out_shape.append(jax.ShapeDtypeStruct(p.arr.shape, F32))
        out_specs.append(p.spec())
    return pl.pallas_call(
        body, grid=(nj, m // tm), name=name,
        in_specs=[r.spec(tm) for r in rows] + [p.spec() for p in pars] + [c.spec(tm) for c in cts],
        out_specs=out_specs, out_shape=out_shape, compiler_params=_params(("arbitrary", "arbitrary")),
    )(*[r.arr for r in rows], *[p.arr for p in pars], *[c.arr for c in cts])


def loss_head(name, x, y, tgt, g, *, m, tm):
    def body(x_ref, y_ref, t_ref, g_ref, dout_ref, dy_ref, loss_ref, dg_ref):
        normed, vjp = jax.vjp(_rms, y_ref[...], g_ref[...])
        err = x_ref[...] + normed - t_ref[...]
        dout = err * (1.0 / D_MODEL)
        dy, dg = vjp(dout)
        dout_ref[...] = dout
        dy_ref[...] = dy.astype(dy_ref.dtype)

        @pl.when(pl.program_id(0) == 0)
        def _():
            loss_ref[...] = jnp.zeros_like(loss_ref)
            dg_ref[...] = jnp.zeros_like(dg_ref)

        loss_ref[...] += 0.5 * jnp.sum(jnp.mean(err * err, axis=-1, keepdims=True), axis=0, keepdims=True)
        dg_ref[...] += dg

    row = pl.BlockSpec((tm, D_MODEL), lambda i: (i, 0))
    whole = lambda w: pl.BlockSpec((1, w), lambda i: (0, 0))
    return pl.pallas_call(
        body, grid=(m // tm,), name=name, in_specs=[row, row, row, whole(D_MODEL)],
        out_specs=[row, row, whole(LANES), whole(D_MODEL)],
        out_shape=[jax.ShapeDtypeStruct((m, D_MODEL), F32), jax.ShapeDtypeStruct((m, D_MODEL), BF16),
                   jax.ShapeDtypeStruct((1, LANES), F32), jax.ShapeDtypeStruct((1, D_MODEL), F32)],
        compiler_params=_params(("arbitrary",)),
    )(x, y, tgt, g)


class Blk:
    def __init__(self, arr, block, index):
        self.arr, self.block, self.index = arr, block, index

    def spec(self):
        return pl.BlockSpec(self.block, self.index)


def _flat2(v):
    return v if v.ndim == 2 else v.reshape(-1, v.shape[-1])


def mm(name, pat, a, b, o, out_dtype, grid, after=None, b_join=False, o_split=False):
    nk = grid[2]
    o_shape = o.arr

    def put(o_ref, r):
        if o_split:
            half = r.shape[1] // 2
            o_ref[0] = r[:, :half].astype(out_dtype)
            o_ref[1] = r[:, half:].astype(out_dtype)
        else:
            o_ref[...] = r.astype(out_dtype).reshape(o_ref.shape)

    def body(*refs):
        a_ref, b_ref = refs[0], refs[1]
        o_ref = refs[3] if after is not None else refs[2]
        bv = jnp.concatenate([b_ref[0], b_ref[1]], axis=1) if b_join else _flat2(b_ref[...])
        r = _dg(_flat2(a_ref[...]), bv, pat)
        if nk == 1:
            put(o_ref, r)
            return
        acc_ref = refs[-1]
        kk = pl.program_id(2)

        @pl.when(kk == 0)
        def _():
            acc_ref[...] = r

        @pl.when(kk > 0)
        def _():
            acc_ref[...] += r

        @pl.when(kk == nk - 1)
        def _():
            put(o_ref, acc_ref[...])

    ob = [d for d in o.block if d is not None]
    if o_split:
        acc_shape = (ob[1], 2 * ob[2])
    else:
        acc_shape = (ob[0], ob[1]) if len(ob) == 2 else (ob[0] * ob[1], ob[2])
    in_specs = [a.spec(), b.spec()]
    args = [a.arr, b.arr]
    if after is not None:
        in_specs.append(pl.BlockSpec(memory_space=pl.ANY))
        args.append(after)
    return pl.pallas_call(
        body, grid=grid, name=name, in_specs=in_specs, out_specs=o.spec(),
        out_shape=jax.ShapeDtypeStruct(o_shape, out_dtype),
        scratch_shapes=[pltpu.VMEM(acc_shape, F32)] if nk > 1 else [],
        compiler_params=_params(("parallel", "parallel", "arbitrary")),
    )(*args)


def _div_tile(n, cap):
    if n <= cap:
        return n
    best = 128
    for t in range(128, cap + 1, 128):
        if n % t == 0:
            best = t
    return best


def mm2d(name, pat, a, b, out_dtype=F32):
    if pat == "tn":
        k, m = a.shape
    else:
        m, k = a.shape
    n = b.shape[0] if pat == "nt" else b.shape[1]
    tm, tn, tk = _div_tile(m, 1024), _div_tile(n, 1024), _div_tile(k, 1024)
    a_blk = Blk(a, (tk, tm), lambda i, j, kk: (kk, i)) if pat == "tn" else Blk(a, (tm, tk), lambda i, j, kk: (i, kk))
    b_blk = Blk(b, (tn, tk), lambda i, j, kk: (j, kk)) if pat == "nt" else Blk(b, (tk, tn), lambda i, j, kk: (kk, j))
    o_blk = Blk((m, n), (tm, tn), lambda i, j, kk: (i, j))
    return mm(name, pat, a_blk, b_blk, o_blk, out_dtype, (m // tm, n // tn, k // tk))


def hgrn_fwd(name, z, logits, hnorm, *, n_batch, seq):
    m = n_batch * seq
    ts = min(HGRN_SEG, seq)
    n_seg = seq // ts

    def body(q_ref, f_ref, v_ref, g_ref, lg_ref, hn_ref, o_ref, sp_ref, st_ref):
        s = pl.program_id(2)

        @pl.when(s == 0)
        def _():
            st_ref[...] = jnp.zeros_like(st_ref)

        st = st_ref[...]
        sp_ref[...] = st
        o, st_new = fn_hgrn_seg(q_ref[...], f_ref[...], v_ref[...], g_ref[...], st, lg_ref[...], hn_ref[...])
        o_ref[...] = o
        st_ref[...] = st_new

    part = lambda p: pl.BlockSpec((ts, 128), lambda h, b, s: (b * n_seg + s, 4 * p + h))
    return pl.pallas_call(
        body, grid=(A_HEADS, n_batch, n_seg), name=name,
        in_specs=[part(0), part(1), part(2), part(3),
                  pl.BlockSpec((3, 128), lambda h, b, s: (0, h)),
                  pl.BlockSpec((1, 128), lambda h, b, s: (0, h))],
        out_specs=[pl.BlockSpec((ts, 128), lambda h, b, s: (b * n_seg + s, h)),
                   pl.BlockSpec((128, 128), lambda h, b, s: ((b * n_seg + s) * A_HEADS + h, 0))],
        out_shape=[jax.ShapeDtypeStruct((m, A_WIDTH), BF16),
                   jax.ShapeDtypeStruct((n_batch * n_seg * A_HEADS * 128, 128), F32)],
        scratch_shapes=[pltpu.VMEM((128, 128), F32)],
        compiler_params=_params(("arbitrary", "arbitrary", "arbitrary")),
    )(z, z, z, z, logits, hnorm)


def hgrn_bwd(name, z, sprev, logits, hnorm, do, *, n_batch, seq):
    m = n_batch * seq
    ts = min(HGRN_SEG, seq)
    n_seg = seq // ts

    def body(q_ref, f_ref, v_ref, g_ref, sp_ref, lg_ref, hn_ref, do_ref, dq_ref, df_ref, dv_ref, dg_ref, dlg_ref, dhn_ref, dst_ref):
        s = pl.program_id(2)

        @pl.when(s == 0)
        def _():
            dst_ref[...] = jnp.zeros_like(dst_ref)

        res, vjp = jax.vjp(fn_hgrn_seg, q_ref[...], f_ref[...], v_ref[...], g_ref[...], sp_ref[...], lg_ref[...], hn_ref[...])
        dq, df, dv, dg, dst, dlg, dhn = vjp((do_ref[...].astype(res[0].dtype), dst_ref[...]))
        dq_ref[...] = dq.astype(dq_ref.dtype)
        df_ref[...] = df.astype(df_ref.dtype)
        dv_ref[...] = dv.astype(dv_ref.dtype)
        dg_ref[...] = dg.astype(dg_ref.dtype)
        dst_ref[...] = dst
        first = jnp.logical_and(pl.program_id(1) == 0, s == 0)

        @pl.when(first)
        def _():
            dlg_ref[...] = jnp.zeros_like(dlg_ref)
            dhn_ref[...] = jnp.zeros_like(dhn_ref)

        dlg_ref[...] += dlg
        dhn_ref[...] += dhn

    rev = lambda b, s: b * n_seg + (n_seg - 1 - s)
    part = lambda p: pl.BlockSpec((ts, 128), lambda h, b, s: (rev(b, s), 4 * p + h))
    head = pl.BlockSpec((ts, 128), lambda h, b, s: (rev(b, s), h))
    dpart = jax.ShapeDtypeStruct((m, A_WIDTH), BF16)
    return pl.pallas_call(
        body, grid=(A_HEADS, n_batch, n_seg), name=name,
        in_specs=[part(0), part(1), part(2), part(3),
                  pl.BlockSpec((128, 128), lambda h, b, s: (rev(b, s) * A_HEADS + h, 0)),
                  pl.BlockSpec((3, 128), lambda h, b, s: (0, h)),
                  pl.BlockSpec((1, 128), lambda h, b, s: (0, h)),
                  head],
        out_specs=[head, head, head, head,
                   pl.BlockSpec((3, 128), lambda h, b, s: (0, h)),
                   pl.BlockSpec((1, 128), lambda h, b, s: (0, h))],
        out_shape=[dpart, dpart, dpart, dpart,
                   jax.ShapeDtypeStruct(logits.shape, F32),
                   jax.ShapeDtypeStruct(hnorm.shape, F32)],
        scratch_shapes=[pltpu.VMEM((128, 128), F32)],
        compiler_params=_params(("arbitrary", "arbitrary", "arbitrary")),
    )(z, z, z, z, sprev, logits, hnorm, do)


def _ffn_tiles(m, seq):
    tm = min(512, seq)
    return tm, seq // tm, m // tm


def ffn_mid_fwd(name, hid, cw, cb, layer, *, m, seq):
    tm, n_t, n_i = _ffn_tiles(m, seq)
    hb = tm // HALO

    def body(x_ref, xb_ref, cw_ref, cb_ref, o_ref):
        first = pl.program_id(1) % n_t == 0
        before = jnp.where(first, 0.0, xb_ref[...])
        ext = jnp.concatenate([before, x_ref[...]], axis=1)
        cg, cv = _ffn_conv(ext[0], ext[1], cw_ref[...], cb_ref[...])
        o_ref[...] = _ffn_gate(cg, cv).astype(o_ref.dtype)

    return pl.pallas_call(
        body, grid=(N_DEV // 2, n_i), name=name,
        in_specs=[pl.BlockSpec((2, None, tm, FF_BLK), lambda d, i: (0, d, i, 0)),
                  pl.BlockSpec((2, None, HALO, FF_BLK), lambda d, i: (0, d, jnp.maximum(i * hb - 1, 0), 0)),
                  pl.BlockSpec((2, None, None, FFN_CONV, FF_BLK), lambda d, i: (0, d, layer, 0, 0)),
                  pl.BlockSpec((None, 2, None, 1, FF_BLK), lambda d, i: (layer, 0, d, 0, 0))],
        out_specs=pl.BlockSpec((None, tm, FF_BLK), lambda d, i: (d, i, 0)),
        out_shape=jax.ShapeDtypeStruct((N_DEV // 2, m, FF_BLK), BF16),
        compiler_params=_params(("parallel", "parallel")),
    )(hid, hid, cw, cb)


def ffn_mid_bwd(name, hid, cw, cb, dact, layer, *, m, seq):
    tm, n_t, n_i = _ffn_tiles(m, seq)
    hb = tm // HALO
    last_blk = m // HALO - 1

    def body(x_ref, xb_ref, xa_ref, cw_ref, cb_ref, da_ref, daa_ref, dx_ref, dcw_ref, dcb_ref):
        i = pl.program_id(1)
        first = i % n_t == 0
        last = i % n_t == n_t - 1
        before = jnp.where(first, 0.0, xb_ref[...])
        ext = jnp.concatenate([before, x_ref[...], xa_ref[...]], axis=1)
        dact_ext = jnp.concatenate([da_ref[...].astype(F32), jnp.where(last, 0.0, daa_ref[...].astype(F32))], axis=0)
        (cg, cv), vjp_conv = jax.vjp(_ffn_conv, ext[0], ext[1], cw_ref[...], cb_ref[...])
        _, vjp_gate = jax.vjp(_ffn_gate, cg, cv)
        dcg, dcv = vjp_gate(dact_ext)
        dxg, dxv, _, _ = vjp_conv((dcg, dcv))
        dx_ref[0] = dxg[HALO:HALO + tm].astype(dx_ref.dtype)
        dx_ref[1] = dxv[HALO:HALO + tm].astype(dx_ref.dtype)
        own = lax.broadcasted_iota(jnp.int32, dcg.shape, 0) < tm
        _, _, dcw, dcb = vjp_conv((jnp.where(own, dcg, 0.0), jnp.where(own, dcv, 0.0)))

        @pl.when(i == 0)
        def _():
            dcw_ref[...] = jnp.zeros_like(dcw_ref)
            dcb_ref[...] = jnp.zeros_like(dcb_ref)

        dcw_ref[...] += dcw
        dcb_ref[...] += dcb

    return pl.pallas_call(
        body, grid=(N_DEV // 2, n_i), name=name,
        in_specs=[pl.BlockSpec((2, None, tm, FF_BLK), lambda d, i: (0, d, i, 0)),
                  pl.BlockSpec((2, None, HALO, FF_BLK), lambda d, i: (0, d, jnp.maximum(i * hb - 1, 0), 0)),
                  pl.BlockSpec((2, None, HALO, FF_BLK), lambda d, i: (0, d, jnp.minimum((i + 1) * hb, last_blk), 0)),
                  pl.BlockSpec((2, None, None, FFN_CONV, FF_BLK), lambda d, i: (0, d, layer, 0, 0)),
                  pl.BlockSpec((None, 2, None, 1, FF_BLK), lambda d, i: (layer, 0, d, 0, 0)),
                  pl.BlockSpec((None, tm, FF_BLK), lambda d, i: (d, i, 0)),
                  pl.BlockSpec((None, HALO, FF_BLK), lambda d, i: (d, jnp.minimum((i + 1) * hb, last_blk), 0))],
        out_specs=[pl.BlockSpec((2, None, tm, FF_BLK), lambda d, i: (0, d, i, 0)),
                   pl.BlockSpec((2, None, FFN_CONV, FF_BLK), lambda d, i: (0, d, 0, 0)),
                   pl.BlockSpec((2, None, 1, FF_BLK), lambda d, i: (0, d, 0, 0))],
        out_shape=[jax.ShapeDtypeStruct((2, N_DEV // 2, m, FF_BLK), BF16),
                   jax.ShapeDtypeStruct((2, N_DEV // 2, FFN_CONV, FF_BLK), F32),
                   jax.ShapeDtypeStruct((2, N_DEV // 2, 1, FF_BLK), F32)],
        compiler_params=_params(("arbitrary", "arbitrary")),
    )(hid, hid, hid, cw, cb, dact, dact)


ATT_BLK = 512
N_PAIR = C_HEADS // 2
TERM_W = C_HEADS * LANES


def term_placement():
    import numpy as np
    place = np.zeros((3, LANES, TERM_W), np.float32)
    ones_q = np.zeros((1, TERM_W), np.float32)
    ones_k = np.zeros((1, TERM_W), np.float32)
    for h in range(C_HEADS):
        for j in range(3):
            place[j, h, h * LANES + C_HEAD_DIM + j] = 1.0
            ones_q[0, h * LANES + C_HEAD_DIM + 3 + j] = 1.0
            ones_k[0, h * LANES + C_HEAD_DIM + j] = 1.0
    return (jnp.asarray(place.reshape(3 * LANES, TERM_W), BF16), jnp.asarray(ones_q, F32), jnp.asarray(ones_k, F32))


def fn_fox_terms(c, place, ones_q, ones_k):
    parts = _split3(c)
    placed = sum(_dg(parts[j], place[j * LANES:(j + 1) * LANES], "nn") for j in range(3))
    return (placed + ones_q).astype(BF16), (ones_k - pltpu.roll(placed, 3, 1)).astype(BF16)


def _head_tile(z, terms, e):
    lane = lax.broadcasted_iota(jnp.int32, z.shape, 1)
    base = z if e == 0 else pltpu.roll(z, C_HEAD_DIM, 1)
    return jnp.where(lane < C_HEAD_DIM, base, terms.astype(z.dtype))


def _head_only(z, e):
    lane = lax.broadcasted_iota(jnp.int32, z.shape, 1)
    mine = (lane < C_HEAD_DIM) if e == 0 else (lane >= C_HEAD_DIM)
    return jnp.where(mine, z, jnp.zeros_like(z)).astype(BF16)


def _pair_tile(a0, a1):
    lane = lax.broadcasted_iota(jnp.int32, a0.shape, 1)
    return jnp.where(lane < C_HEAD_DIM, a0, pltpu.roll(a1, C_HEAD_DIM, 1))


def _lane_col(a, k):
    lane = lax.broadcasted_iota(jnp.int32, a.shape, 1)
    return jnp.sum(jnp.where(lane == k, a, 0.0), axis=1, keepdims=True)


def _causal(s):
    key = lax.broadcasted_iota(jnp.int32, s.shape, 0)
    qry = lax.broadcasted_iota(jnp.int32, s.shape, 1)
    return qry >= key


def fox_pair_fwd(name, z, qterm, kterm, *, n_batch, seq):
    m = n_batch * seq
    blk = min(ATT_BLK, seq)
    nq = seq // blk
    dh = C_HEAD_DIM

    def body(zq_ref, zk_ref, zv_ref, qt_ref, kt_ref, o_ref, lse_ref, ka_ref, vt_ref):
        qi = pl.program_id(2)

        @pl.when(qi == 0)
        def _():
            zk = zk_ref[...]
            for e in range(2):
                ka_ref[e] = _head_tile(zk, kt_ref[:, e * LANES:(e + 1) * LANES], e).astype(BF16)
            for cb in range(nq):
                vt_ref[cb] = zv_ref[cb * blk:(cb + 1) * blk, :].T.astype(BF16)

        zq = zq_ref[...] * dh ** -0.5
        qa = [_head_tile(zq, qt_ref[:, e * LANES:(e + 1) * LANES], e).astype(BF16) for e in range(2)]

        def block(j, carry, diagonal):
            rows = pl.ds(pl.multiple_of(j * blk, blk), blk)
            out = []
            for e in range(2):
                mx, l, acc = carry[e]
                s = _dg(ka_ref[e, rows, :], qa[e], "nt")
                if diagonal:
                    s = jnp.where(_causal(s), s, NEG)
                mx_new = jnp.maximum(mx, jnp.max(s, axis=0, keepdims=True))
                p = jnp.exp(s - mx_new)
                alpha = jnp.exp(mx - mx_new)
                l = alpha * l + jnp.sum(p, axis=0, keepdims=True)
                acc = alpha * acc + _dg(vt_ref[j, e * dh:(e + 1) * dh, :], p, "nn")
                out.append((mx_new, l, acc))
            return tuple(out)

        one = (jnp.full((1, blk), NEG, F32), jnp.zeros((1, blk), F32), jnp.zeros((dh, blk), F32))
        carry = lax.fori_loop(0, qi, lambda j, cr: block(j, cr, False), (one, one))
        res = block(qi, carry, True)
        ot = jnp.concatenate([res[e][2] / res[e][1] for e in range(2)], axis=0)
        o_ref[...] = ot.T.astype(o_ref.dtype)
        for e in range(2):
            lse_ref[e] = res[e][0] + jnp.log(res[e][1])

    col = lambda part: (lambda b, g, i: (b, part * N_PAIR + g))
    return pl.pallas_call(
        body, grid=(n_batch, N_PAIR, nq), name=name,
        in_specs=[pl.BlockSpec((blk, LANES), lambda b, g, i: (b * nq + i, g)),
                  pl.BlockSpec((seq, LANES), col(1)),
                  pl.BlockSpec((seq, LANES), col(2)),
                  pl.BlockSpec((blk, 2 * LANES), lambda b, g, i: (b * nq + i, g)),
                  pl.BlockSpec((seq, 2 * LANES), lambda b, g, i: (b, g))],
        out_specs=[pl.BlockSpec((blk, LANES), lambda b, g, i: (b * nq + i, g)),
                   pl.BlockSpec((None, None, None, 2, 1, blk), lambda b, g, i: (b, g, i, 0, 0, 0))],
        out_shape=[jax.ShapeDtypeStruct((m, D_MODEL), BF16), jax.ShapeDtypeStruct((n_batch, N_PAIR, nq, 2, 1, blk), F32)],
        scratch_shapes=[pltpu.VMEM((2, seq, LANES), BF16), pltpu.VMEM((nq, LANES, blk), BF16)],
        compiler_params=_params(("parallel", "parallel", "arbitrary")),
    )(z, z, z, qterm, kterm)


def fox_pair_bwd(name, z, qterm, kterm, o, do, lse, *, n_batch, seq):
    m = n_batch * seq
    blk = min(ATT_BLK, seq)
    nq = seq // blk
    dh = C_HEAD_DIM

    def body(zq_ref, zk_ref, zv_ref, qt_ref, kt_ref, o_ref, do_ref, lse_ref, dq_ref, dk_ref, dv_ref, dc_ref,
             qa_ref, doh_ref, del_ref, dqt_ref, dk_acc, dv_acc):
        g, j = pl.program_id(1), pl.program_id(2)
        lane = lax.broadcasted_iota(jnp.int32, (blk, LANES), 1)

        @pl.when(jnp.logical_and(g == 0, j == 0))
        def _():
            dc_ref[...] = jnp.zeros_like(dc_ref)

        @pl.when(j == 0)
        def _():
            zq = zq_ref[...] * dh ** -0.5
            dov = do_ref[...]
            for e in range(2):
                qa_ref[e] = _head_tile(zq, qt_ref[:, e * LANES:(e + 1) * LANES], e).astype(BF16)
                doh_ref[e] = _head_only(dov, e)
            for cb in range(nq):
                rows = slice(cb * blk, (cb + 1) * blk)
                prod_t = (do_ref[rows, :].astype(F32) * o_ref[rows, :].astype(F32)).T
                for e in range(2):
                    del_ref[cb, e] = jnp.sum(prod_t[e * dh:(e + 1) * dh], axis=0, keepdims=True)
            dqt_ref[...] = jnp.zeros_like(dqt_ref)

        zk, zv = zk_ref[...], zv_ref[...]
        ka32 = [_head_tile(zk, kt_ref[:, e * LANES:(e + 1) * LANES], e) for e in range(2)]
        ka = [t.astype(BF16) for t in ka32]
        kat = [t.T.astype(BF16) for t in ka32]
        vh = [_head_only(zv, e) for e in range(2)]
        dk_acc[...] = jnp.zeros_like(dk_acc)
        dv_acc[...] = jnp.zeros_like(dv_acc)

        def block(i, diagonal):
            rows = pl.ds(pl.multiple_of(i * blk, blk), blk)
            for e in range(2):
                qv, dov = qa_ref[e, rows, :], doh_ref[e, rows, :]
                p = jnp.exp(_dg(ka[e], qv, "nt") - lse_ref[i, e])
                if diagonal:
                    p = jnp.where(_causal(p), p, 0.0)
                dv_acc[...] += _dg(p, dov, "nn")
                ds = p * (_dg(vh[e], dov, "nt") - del_ref[i, e])
                dk_acc[e] += _dg(ds, qv, "nn")
                dqt_ref[i, e] += _dg(kat[e], ds, "nn")

        block(j, True)

        def rest(i, carry):
            block(i, False)
            return carry

        lax.fori_loop(j + 1, nq, rest, 0)
        dk0, dk1 = dk_acc[0], dk_acc[1]
        dk_ref[...] = _pair_tile(dk0, dk1).astype(dk_ref.dtype)
        dv_ref[...] = dv_acc[...].astype(dv_ref.dtype)
        rows_j = pl.ds(pl.multiple_of(j * blk, blk), blk)
        for e, dke in enumerate((dk0, dk1)):
            dc_ref[rows_j, :] -= jnp.where(lane == 2 * g + e, _lane_col(dke, dh + 3), 0.0)

        @pl.when(j == nq - 1)
        def _():
            for i in range(nq):
                nat = [dqt_ref[i, e].T for e in range(2)]
                rows = slice(i * blk, (i + 1) * blk)
                dq_ref[rows, :] = (_pair_tile(nat[0], nat[1]) * dh ** -0.5).astype(dq_ref.dtype)
                for e in range(2):
                    dc_ref[rows, :] += jnp.where(lane == 2 * g + e, _lane_col(nat[e], dh), 0.0)

    col = lambda part: (lambda b, g, j: (b, part * N_PAIR + g))
    colj = lambda part: (lambda b, g, j: (b * nq + j, part * N_PAIR + g))
    pair = jax.ShapeDtypeStruct((m, D_MODEL), BF16)
    return pl.pallas_call(
        body, grid=(n_batch, N_PAIR, nq), name=name,
        in_specs=[pl.BlockSpec((seq, LANES), col(0)),
                  pl.BlockSpec((blk, LANES), colj(1)),
                  pl.BlockSpec((blk, LANES), colj(2)),
                  pl.BlockSpec((seq, 2 * LANES), lambda b, g, j: (b, g)),
                  pl.BlockSpec((blk, 2 * LANES), lambda b, g, j: (b * nq + j, g)),
                  pl.BlockSpec((seq, LANES), col(0)),
                  pl.BlockSpec((seq, LANES), col(0)),
                  pl.BlockSpec((None, None, nq, 2, 1, blk), lambda b, g, j: (b, g, 0, 0, 0, 0))],
        out_specs=[pl.BlockSpec((seq, LANES), col(0)),
                   pl.BlockSpec((blk, LANES), colj(0)),
                   pl.BlockSpec((blk, LANES), colj(0)),
                   pl.BlockSpec((seq, LANES), lambda b, g, j: (b, 0))],
        out_shape=[pair, pair, pair, jax.ShapeDtypeStruct((m, LANES), F32)],
        scratch_shapes=[pltpu.VMEM((2, seq, LANES), BF16), pltpu.VMEM((2, seq, LANES), BF16),
                        pltpu.VMEM((nq, 2, 1, blk), F32), pltpu.VMEM((nq, 2, LANES, blk), F32),
                        pltpu.VMEM((2, blk, LANES), F32), pltpu.VMEM((blk, LANES), F32)],
        compiler_params=_params(("arbitrary", "arbitrary", "arbitrary")),
    )(z, z, z, qterm, kterm, o, do, lse)


def _split3(c):
    c1 = c.astype(BF16)
    r1 = c - c1.astype(F32)
    c2 = r1.astype(BF16)
    c3 = (r1 - c2.astype(F32)).astype(BF16)
    return c1, c2, c3


def _mesh_pos():
    return lax.axis_index("x"), lax.axis_index("y"), lax.axis_index("c")


def _flip(v, bit):
    return 1 - v if bit else v


def all_gather(name, blocks):
    n = len(blocks)

    def body(*refs):
        x_refs, out_refs = refs[:n], refs[n:2 * n]
        send_sems, recv_sems, local_sems = refs[2 * n:]
        x, y, c = _mesh_pos()
        me, sibling = (x, y, c), (x, y, 1 - c)
        chips = [(1 - x, y), (x, 1 - y), (1 - x, 1 - y)]

        def slot(a, px, py, pc):
            return out_refs[a].at[4 * px + 2 * py + pc]

        def copy(a, k, blk, to, src=None):
            return pltpu.make_async_remote_copy(
                src_ref=slot(a, *blk) if src is None else src, dst_ref=slot(a, *blk),
                send_sem=send_sems.at[a, k], recv_sem=recv_sems.at[a, k], device_id=to, device_id_type=MESH)

        mine = [pltpu.make_async_copy(x_refs[a], slot(a, *me), local_sems.at[a]) for a in range(n)]
        for cp in mine:
            cp.start()
        sends = []
        for a in range(n):
            sends.append(copy(a, 0, me, sibling, src=x_refs[a]))
            sends += [copy(a, 1 + j, me, (*chip, c), src=x_refs[a]) for j, chip in enumerate(chips)]
        for cp in sends:
            cp.start()
        for j, chip in enumerate(chips):
            for a in range(n):
                copy(a, 1 + j, (*chip, c), me).wait_recv()
                passed = copy(a, 4 + j, (*chip, c), sibling)
                passed.start()
                sends.append(passed)
        for a in range(n):
            copy(a, 0, sibling, me).wait_recv()
            for j, chip in enumerate(chips):
                copy(a, 4 + j, (*chip, 1 - c), me).wait_recv()
        for cp in sends:
            cp.wait_send()
        for cp in mine:
            cp.wait()

    hbm = pl.BlockSpec(memory_space=pl.ANY)
    return pl.pallas_call(
        body, name=name, out_shape=[jax.ShapeDtypeStruct((N_DEV,) + b.shape, b.dtype) for b in blocks],
        in_specs=[hbm] * n, out_specs=[hbm] * n,
        scratch_shapes=[pltpu.SemaphoreType.DMA((n, 7)), pltpu.SemaphoreType.DMA((n, 7)), pltpu.SemaphoreType.DMA((n,))],
    )(*blocks)


def _peers(x, y, c):
    return [(_flip(x, k & 4), _flip(y, k & 2), _flip(c, k & 1)) for k in range(1, N_DEV)]


def gather_start(name, blocks, lands):
    n = len(blocks)

    def body(*refs):
        x_refs, land_refs = refs[:n], refs[n:2 * n]
        send_sems, recv_sems = refs[2 * n], refs[2 * n + 1]
        token = refs[-1]
        x, y, c = _mesh_pos()
        me = 4 * x + 2 * y + c
        for k, peer in enumerate(_peers(x, y, c)):
            for a in range(n):
                pltpu.make_async_remote_copy(
                    src_ref=x_refs[a], dst_ref=land_refs[a].at[me], send_sem=send_sems.at[7 * a + k], recv_sem=recv_sems.at[7 * a + k],
                    device_id=peer, device_id_type=MESH).start()
        token[...] = jnp.zeros_like(token)

    hbm = pl.BlockSpec(memory_space=pltpu.HBM)
    sem = pl.BlockSpec(memory_space=pltpu.SEMAPHORE)
    out_shape = ([pltpu.SemaphoreType.DMA((7 * n,)), pltpu.SemaphoreType.DMA((7 * n,))]
                 + [pltpu.HBM(b.shape, b.dtype) for b in blocks] + [pltpu.HBM(l.shape, l.dtype) for l in lands]
                 + [jax.ShapeDtypeStruct((8, LANES), F32)])
    res = pl.pallas_call(
        body, name=name, out_shape=out_shape, in_specs=[hbm] * (2 * n),
        out_specs=[sem, sem] + [hbm] * (2 * n) + [pl.BlockSpec(memory_space=pltpu.VMEM)],
        input_output_aliases={a: 2 + a for a in range(2 * n)},
        compiler_params=pltpu.CompilerParams(has_side_effects=pltpu.SideEffectType.DATAFLOW_SIDE_EFFECTING),
    )(*[pltpu.with_memory_space_constraint(b, pltpu.HBM) for b in blocks],
      *[pltpu.with_memory_space_constraint(l, pltpu.HBM) for l in lands])
    return res[0], res[1], res[2:2 + n], res[2 + n:2 + 2 * n], res[-1]


def gather_wait(name, send_sems, recv_sems, blocks, lands, after):
    n = len(blocks)

    def body(*refs):
        x_refs, land_refs = refs[:n], refs[n:2 * n]
        s_sems, r_sems = refs[2 * n], refs[2 * n + 1]
        x, y, c = _mesh_pos()
        me = 4 * x + 2 * y + c
        for k, peer in enumerate(_peers(x, y, c)):
            for a in range(n):
                cp = pltpu.make_async_remote_copy(
                    src_ref=x_refs[a], dst_ref=land_refs[a].at[me], send_sem=s_sems.at[7 * a + k], recv_sem=r_sems.at[7 * a + k],
                    device_id=peer, device_id_type=MESH)
                cp.wait_send()
                cp.wait_recv()

    hbm = pl.BlockSpec(memory_space=pltpu.HBM)
    sem = pl.BlockSpec(memory_space=pltpu.SEMAPHORE)
    res = pl.pallas_call(
        body, name=name,
        out_shape=[pltpu.HBM(b.shape, b.dtype) for b in blocks] + [pltpu.HBM(l.shape, l.dtype) for l in lands],
        in_specs=[hbm] * (2 * n) + [sem, sem, pl.BlockSpec(memory_space=pl.ANY)], out_specs=[hbm] * (2 * n),
        input_output_aliases={a: a for a in range(2 * n)},
        compiler_params=pltpu.CompilerParams(has_side_effects=pltpu.SideEffectType.DATAFLOW_SIDE_EFFECTING),
    )(*blocks, *lands, send_sems, recv_sems, after)
    return res[n:]


def _split_exchange(name, sends, lands, sems, after):
    n = len(sends)
    starting = sems is None

    def body(*refs):
        s_refs, l_refs = refs[:n], refs[n:2 * n]
        send_sems, recv_sems = refs[2 * n], refs[2 * n + 1]
        x, y, c = _mesh_pos()
        me = 4 * x + 2 * y + c
        for k, (px, py, pc) in enumerate(_peers(x, y, c)):
            for a in range(n):
                cp = pltpu.make_async_remote_copy(
                    src_ref=s_refs[a].at[4 * px + 2 * py + pc], dst_ref=l_refs[a].at[me],
                    send_sem=send_sems.at[7 * a + k], recv_sem=recv_sems.at[7 * a + k],
                    device_id=(px, py, pc), device_id_type=MESH)
                if starting:
                    cp.start()
                else:
                    cp.wait_send()
                    cp.wait_recv()
        if starting:
            refs[-1][...] = jnp.zeros_like(refs[-1])

    hbm = pl.BlockSpec(memory_space=pltpu.HBM)
    sem = pl.BlockSpec(memory_space=pltpu.SEMAPHORE)
    thru = [pltpu.HBM(t.shape, t.dtype) for t in list(sends) + list(lands)]
    effect = pltpu.CompilerParams(has_side_effects=pltpu.SideEffectType.DATAFLOW_SIDE_EFFECTING)
    if starting:
        res = pl.pallas_call(
            body, name=name, in_specs=[hbm] * (2 * n),
            out_shape=[pltpu.SemaphoreType.DMA((7 * n,)), pltpu.SemaphoreType.DMA((7 * n,))] + thru + [jax.ShapeDtypeStruct((8, LANES), F32)],
            out_specs=[sem, sem] + [hbm] * (2 * n) + [pl.BlockSpec(memory_space=pltpu.VMEM)],
            input_output_aliases={a: 2 + a for a in range(2 * n)}, compiler_params=effect,
        )(*[pltpu.with_memory_space_constraint(t, pltpu.HBM) for t in list(sends) + list(lands)])
        return res[0], res[1], res[2:2 + n], res[2 + n:2 + 2 * n], res[-1]
    res = pl.pallas_call(
        body, name=name, out_shape=thru, in_specs=[hbm] * (2 * n) + [sem, sem, pl.BlockSpec(memory_space=pl.ANY)],
        out_specs=[hbm] * (2 * n), input_output_aliases={a: a for a in range(2 * n)}, compiler_params=effect,
    )(*sends, *lands, sems[0], sems[1], after)
    return res[n:]


def own_slot_only(send, me):
    mine = lax.dynamic_index_in_dim(send, me, 0, keepdims=False)
    return lax.dynamic_update_index_in_dim(lax.empty(send.shape, send.dtype), mine, me, 0)


def all_to_all(name, sends):
    n = len(sends)

    def body(*refs):
        s_refs, r_refs = refs[:n], refs[n:2 * n]
        send_sems, recv_sems, local_sems = refs[2 * n:]
        x, y, c = _mesh_pos()
        me = 4 * x + 2 * y + c
        mine = [pltpu.make_async_copy(s_refs[a].at[me], r_refs[a].at[me], local_sems.at[a]) for a in range(n)]
        for cp in mine:
            cp.start()
        copies = []
        for k in range(1, N_DEV):
            px, py, pc = _flip(x, k & 4), _flip(y, k & 2), _flip(c, k & 1)
            for a in range(n):
                copies.append(pltpu.make_async_remote_copy(
                    src_ref=s_refs[a].at[4 * px + 2 * py + pc], dst_ref=r_refs[a].at[me],
                    send_sem=send_sems.at[a, k - 1], recv_sem=recv_sems.at[a, k - 1],
                    device_id=(px, py, pc), device_id_type=MESH))
        for cp in copies:
            cp.start()
        for cp in copies:
            cp.wait_recv()
        for cp in copies:
            cp.wait_send()
        for cp in mine:
            cp.wait()

    hbm = pl.BlockSpec(memory_space=pl.ANY)
    return pl.pallas_call(
        body, name=name, out_shape=[jax.ShapeDtypeStruct(s.shape, s.dtype) for s in sends],
        in_specs=[hbm] * n, out_specs=[hbm] * n,
        scratch_shapes=[pltpu.SemaphoreType.DMA((n, 7)), pltpu.SemaphoreType.DMA((n, 7)), pltpu.SemaphoreType.DMA((n,))],
    )(*sends)


def _row_tile(r, cap, step):
    return next((t for t in range(cap, step - 1, -step) if r % t == 0), r)


def _sum_parts(p, n):
    t = [p[k].astype(F32) for k in range(n)]
    while len(t) > 1:
        t = [t[k] + t[k + 1] for k in range(0, len(t), 2)]
    return t[0]


def _adam(g, w, m, v):
    m = ADAM_B1 * m + (1.0 - ADAM_B1) * g
    v = ADAM_B2 * v + (1.0 - ADAM_B2) * (g * g)
    m_hat = m / (1.0 - ADAM_B1 ** ADAM_STEP)
    v_hat = v / (1.0 - ADAM_B2 ** ADAM_STEP)
    return -ADAM_LR * (m_hat / (jnp.sqrt(v_hat) + ADAM_EPS) + ADAM_WD * w), m, v


def adam_tiled(name, partials, w, m_, v_, layer=0, prev=None):
    _, r, c = w.shape
    n_part = partials.shape[0]
    tr = _row_tile(r, 256, 16)

    def body(*refs):
        p_ref, w_ref, m_ref, v_ref = refs[:4]
        g_ref, d_ref, nm_ref, nv_ref = refs[-4:]
        g = _sum_parts(p_ref, n_part)
        g_ref[...] = g
        d_ref[...], nm_ref[...], nv_ref[...] = _adam(g, w_ref[...], m_ref[...], v_ref[...])

    spec = pl.BlockSpec((None, tr, c), lambda i: (layer, i, 0))
    in_specs = [pl.BlockSpec((n_part, None, tr, c), lambda i: (0, 0, i, 0)), spec, spec, spec]
    args = [partials, w, m_, v_]
    aliases = {}
    if prev is not None:
        in_specs += [pl.BlockSpec(memory_space=pl.ANY)] * 4
        args += list(prev)
        aliases = {4 + k: k for k in range(4)}
    return pl.pallas_call(
        body, grid=(r // tr,), name=name, in_specs=in_specs,
        out_specs=[spec] * 4, out_shape=[jax.ShapeDtypeStruct(w.shape, F32)] * 4,
        input_output_aliases=aliases, compiler_params=_params(("parallel",)),
    )(*args)


def adam_small(name, items, extra):
    n, ne = len(items), len(extra)

    def body(*refs):
        ins, outs = refs[:4 * n + ne], refs[4 * n + ne:]
        for a in range(n):
            p_ref, w_ref, m_ref, v_ref = ins[4 * a:4 * a + 4]
            g = _sum_parts(p_ref, N_DEV)
            outs[4 * a][...] = g
            outs[4 * a + 1][...], outs[4 * a + 2][...], outs[4 * a + 3][...] = _adam(g, w_ref[...], m_ref[...], v_ref[...])
        for e in range(ne):
            outs[4 * n + e][...] = _sum_parts(ins[4 * n + e], N_DEV)

    args, out_shape = [], []
    for p, w, m_, v_ in items:
        args += [p, w, m_, v_]
        out_shape += [jax.ShapeDtypeStruct(w.shape, F32)] * 4
    for e in extra:
        args.append(e)
        out_shape.append(jax.ShapeDtypeStruct(e.shape[1:], F32))
    vmem = pl.BlockSpec(memory_space=pltpu.VMEM)
    res = pl.pallas_call(body, name=name, in_specs=[vmem] * len(args), out_specs=[vmem] * len(out_shape), out_shape=out_shape)(*args)
    return [res[4 * a:4 * a + 4] for a in range(n)], res[4 * n:]


def _cols_from_gather(g):
    g = jnp.moveaxis(g, 0, -2)
    return g.reshape(g.shape[:-2] + (g.shape[-2] * g.shape[-1],))


def _cols_to_blocks(w):
    w = w.reshape(w.shape[:-1] + (N_DEV, w.shape[-1] // N_DEV))
    return jnp.moveaxis(w, -2, 0)


def _block_diag(w):
    z = jnp.zeros((B_BLOCK_DIM, B_BLOCK_DIM), w.dtype)
    rows = []
    for j in range(B_BLOCKS // 2):
        top = jnp.concatenate([w[2 * j], z], axis=1)
        bot = jnp.concatenate([z, w[2 * j + 1]], axis=1)
        rows.append(jnp.concatenate([top, bot], axis=0))
    return jnp.concatenate(rows, axis=0)


def _block_diag_grad(d):
    out = []
    for j in range(B_BLOCKS // 2):
        blk = d[128 * j:128 * (j + 1)]
        out.append(blk[:64, :64])
        out.append(blk[64:, 64:])
    return jnp.stack(out)


NAMES = ("norm_gains", "even_w_in", "hgrn_lb_logits", "hgrn_norm", "rg_conv_w", "rg_conv_b", "rg_wa", "rg_ba", "rg_wx", "rg_bx",
         "rg_lambda", "even_w_out", "odd_w_in", "fox_f_bias", "odd_w_out", "ffn_w_up", "ffn_conv_w", "ffn_conv_b", "ffn_w_down")
SMALL_SHARDED = ("norm_gains", "rg_conv_w", "ffn_conv_w")
REPLICATED = ("hgrn_lb_logits", "hgrn_norm", "rg_conv_b", "rg_wa", "rg_ba", "rg_wx", "rg_bx", "rg_lambda", "fox_f_bias", "ffn_conv_b")


def _ffn_forward(tag, layer, h, w_up_g, cw5, cb5, w_down_g, m, seq):
    tm = _div_tile(m, 1024)
    nm = m // tm
    hid = mm(f"{tag}_up", "nn",
             Blk(h, (tm, D_MODEL), lambda i, j, k: (i, 0)),
             Blk(w_up_g, (None, None, D_MODEL, FF_BLK), lambda i, j, k: (j, 0, 0, 0)),
             Blk((N_DEV, m, FF_BLK), (None, tm, FF_BLK), lambda i, j, k: (j, i, 0)), F32, (nm, N_DEV, 1))
    hid = hid.reshape(2, N_DEV // 2, m, FF_BLK)
    act = ffn_mid_fwd(f"{tag}_mid", hid, cw5, cb5, layer, m=m, seq=seq)
    f = mm(f"{tag}_down", "nn",
           Blk(act, (None, tm, FF_BLK), lambda i, j, k: (k, i, 0)),
           Blk(w_down_g, (2, None, FF_BLK // 2, D_MODEL), lambda i, j, k: (k, 0, 0, 0)),
           Blk((m, D_MODEL), (tm, D_MODEL), lambda i, j, k: (i, 0)), F32, (nm, 1, N_DEV // 2))
    return hid, act, f


def _ffn_backward(tag, layer, df, h, hid, act, w_up_g, cw5, cb5, w_down_g, m, seq):
    tm = _div_tile(m, 1024)
    nm = m // tm
    dact = mm(f"{tag}_dact", "nt",
              Blk(df, (tm, D_MODEL), lambda i, j, k: (i, 0)),
              Blk(w_down_g, (2, None, FF_BLK // 2, D_MODEL), lambda i, j, k: (j, 0, 0, 0)),
              Blk((N_DEV // 2, m, FF_BLK), (None, tm, FF_BLK), lambda i, j, k: (j, i, 0)), BF16, (nm, N_DEV // 2, 1))
    d_wdown = mm(f"{tag}_dwdown", "tn",
                 Blk(act, (None, tm, FF_BLK), lambda i, j, k: (i, k, 0)),
                 Blk(df, (tm, D_MODEL), lambda i, j, k: (k, 0)),
                 Blk(w_down_g.shape, (2, None, FF_BLK // 2, D_MODEL), lambda i, j, k: (i, 0, 0, 0)), BF16,
                 (N_DEV // 2, 1, nm))
    dhid, d_cw, d_cb = ffn_mid_bwd(f"{tag}_dmid", hid, cw5, cb5, dact, layer, m=m, seq=seq)
    dhid = dhid.reshape(N_DEV, m, FF_BLK)
    dh = mm(f"{tag}_dh", "nt",
            Blk(dhid, (None, tm, FF_BLK), lambda i, j, k: (k, i, 0)),
            Blk(w_up_g, (None, None, D_MODEL, FF_BLK), lambda i, j, k: (k, 0, 0, 0)),
            Blk((m, D_MODEL), (tm, D_MODEL), lambda i, j, k: (i, 0)), BF16, (nm, 1, N_DEV))
    d_wup = mm(f"{tag}_dwup", "tn",
               Blk(h, (tm, D_MODEL), lambda i, j, k: (k, 0)),
               Blk(dhid, (None, tm, FF_BLK), lambda i, j, k: (j, k, 0)),
               Blk(w_up_g.shape, (None, None, D_MODEL, FF_BLK), lambda i, j, k: (j, 0, 0, 0)), BF16,
               (1, N_DEV, nm))
    return dh, d_wup, d_cw, d_cb, d_wdown


def kernel(x, norm_gains, even_w_in, hgrn_lb_logits, hgrn_norm, rg_conv_w, rg_conv_b, rg_wa, rg_ba, rg_wx, rg_bx, rg_lambda, even_w_out, odd_w_in, fox_f_bias, odd_w_out, ffn_w_up, ffn_conv_w, ffn_conv_b, ffn_w_down, loss_target, m_norm_gains, m_even_w_in, m_hgrn_lb_logits, m_hgrn_norm, m_rg_conv_w, m_rg_conv_b, m_rg_wa, m_rg_ba, m_rg_wx, m_rg_bx, m_rg_lambda, m_even_w_out, m_odd_w_in, m_fox_f_bias, m_odd_w_out, m_ffn_w_up, m_ffn_conv_w, m_ffn_conv_b, m_ffn_w_down, v_norm_gains, v_even_w_in, v_hgrn_lb_logits, v_hgrn_norm, v_rg_conv_w, v_rg_conv_b, v_rg_wa, v_rg_ba, v_rg_wx, v_rg_bx, v_rg_lambda, v_even_w_out, v_odd_w_in, v_fox_f_bias, v_odd_w_out, v_ffn_w_up, v_ffn_conv_w, v_ffn_conv_b, v_ffn_w_down):
    local = dict(locals())
    w = {n: local[n] for n in NAMES}
    mom = {n: local["m_" + n] for n in NAMES}
    var = {n: local["v_" + n] for n in NAMES}
    n_batch, seq, _ = x.shape
    m = n_batch * seq
    tm = _div_tile(m, 512)
    tmm = _div_tile(m, 1024)
    nm = m // tmm

    gathered = all_gather("gather_weights", [w["even_w_in"].astype(BF16)] + [w[n] for n in SMALL_SHARDED])
    g = dict(zip(("even_w_in",) + SMALL_SHARDED, gathered))
    w_in_e = g["even_w_in"]
    gains = _cols_from_gather(g["norm_gains"])
    me = 4 * lax.axis_index("x") + 2 * lax.axis_index("y") + lax.axis_index("c")
    own_block_only = lambda t: lax.dynamic_update_index_in_dim(lax.empty((N_DEV,) + t.shape, t.dtype), t, me, 0)
    behind = (g["norm_gains"][0, 0, 0, 0] * 0.0).astype(BF16)
    out0 = [w["even_w_out"].astype(BF16) + behind]
    out0_sent = gather_start("gather_out0_start", out0, [own_block_only(t) for t in out0])
    behind = (out0_sent[4][0, 0] * 0.0).astype(BF16)
    ffn0 = [w["ffn_w_up"][0:1].astype(BF16) + behind, w["ffn_w_down"][0:1].astype(BF16) + behind]
    ffn0_sent = gather_start("gather_ffn0_start", ffn0, [own_block_only(t) for t in ffn0])
    behind = (ffn0_sent[4][0, 0] * 0.0).astype(BF16)
    mix1w = [w["odd_w_in"].astype(BF16) + behind, w["odd_w_out"].astype(BF16) + behind]
    mix1_sent = gather_start("gather_mix1_start", mix1w, [own_block_only(t) for t in mix1w])
    behind = (mix1_sent[4][0, 0] * 0.0).astype(BF16)
    ffn1 = [w["ffn_w_up"][1:2].astype(BF16) + behind, w["ffn_w_down"][1:2].astype(BF16) + behind]
    ffn1_sent = gather_start("gather_ffn1_start", ffn1, [own_block_only(t) for t in ffn1])
    started = ffn1_sent[4]
    rg_cw = _cols_from_gather(g["rg_conv_w"])[0]
    n_layer = ffn_conv_w.shape[0]
    cw5 = g["ffn_conv_w"].reshape(2, N_DEV // 2, n_layer, FFN_CONV, FF_BLK)
    cb5 = ffn_conv_b.reshape(n_layer, 2, N_DEV // 2, 1, FF_BLK)
    gain = lambda l, k: gains[l, k:k + 1, :]
    wa_bd, wx_bd = _block_diag(rg_wa[0]), _block_diag(rg_wx[0])
    fbias = jnp.pad(fox_f_bias, ((0, 0), (0, LANES - C_HEADS)))

    x0 = x.reshape(m, D_MODEL)
    tgt = loss_target.reshape(m, D_MODEL)

    (h0,) = tile_fwd("l0_prenorm", fn_prenorm_after, m=m, tm=tm, nj=1, rows=[Row(x0)], pars=[Par(gain(0, 0)), Par(started)],
                     outs=[Out(D_MODEL, BF16)])
    z0 = mm("l0_in", "nn",
            Blk(h0, (tmm, D_MODEL), lambda i, j, k: (i, 0)),
            Blk(w_in_e, (2, None, D_MODEL, 384), lambda i, j, k: (j, 0, 0, 0)),
            Blk((m, 3072), (tmm, 768), lambda i, j, k: (i, j)), F32, (nm, N_DEV // 2, 1), b_join=True)
    oa, sprev = hgrn_fwd("l0_hgrn", z0, hgrn_lb_logits, hgrn_norm, n_batch=n_batch, seq=seq)
    rg_rows = lambda: [Row(z0, LANES, 16), Row(z0, LANES, 20)]
    rg_pars = lambda: [Par(rg_cw, "col", LANES), Par(rg_conv_b, "col", LANES), Par(wa_bd, "row", LANES), Par(rg_ba, "col", LANES),
                       Par(wx_bd, "row", LANES), Par(rg_bx, "col", LANES), Par(rg_lambda, "col", LANES)]
    (ob,) = tile_fwd("l0_rglru", fn_rglru, m=m, tm=seq, nj=B_WIDTH // LANES, rows=rg_rows(), pars=rg_pars(),
                     outs=[Out(B_WIDTH, BF16, LANES)])
    mixcat0 = jnp.concatenate([oa, ob], axis=-1)
    (g_out_e,) = gather_wait("gather_out0_wait", out0_sent[0], out0_sent[1], out0_sent[2], out0_sent[3], mixcat0)
    w_out_e = g_out_e.reshape(D_MODEL, D_MODEL)
    mix0 = mm2d("l0_out", "nn", mixcat0, w_out_e)
    x1, h1 = tile_fwd("l0_postnorm", fn_addnorm2, m=m, tm=tm, nj=1, rows=[Row(x0), Row(mix0)], pars=[Par(gain(0, 1)), Par(gain(0, 2))],
                      outs=[Out(D_MODEL, F32), Out(D_MODEL, BF16)])
    w_up_g0, w_down_g0 = gather_wait("gather_ffn0_wait", ffn0_sent[0], ffn0_sent[1], ffn0_sent[2], ffn0_sent[3], h1)
    hid0, act0, f0 = _ffn_forward("l0_ffn", 0, h1, w_up_g0, cw5, cb5, w_down_g0, m, seq)
    x2, h2 = tile_fwd("l0_ffnnorm", fn_addnorm2, m=m, tm=tm, nj=1, rows=[Row(x1), Row(f0)], pars=[Par(gain(0, 3)), Par(gain(1, 0))],
                      outs=[Out(D_MODEL, F32), Out(D_MODEL, BF16)])

    g_in_o, g_out_o = gather_wait("gather_mix1_wait", mix1_sent[0], mix1_sent[1], mix1_sent[2], mix1_sent[3], h2)
    w_in_o = jnp.pad(_cols_from_gather(g_in_o)[0], ((0, 0), (0, 3200 - 3088)))
    w_out_o = g_out_o.reshape(D_MODEL, D_MODEL)
    z1 = mm2d("l1_in", "nn", h2, w_in_o)
    (cgate,) = tile_fwd("l1_gate", fn_fox_gate, m=m, tm=seq, nj=1, rows=[Row(z1, LANES, 3072 // LANES)], pars=[Par(fbias)],
                        outs=[Out(LANES, F32)])
    place, ones_q, ones_k = term_placement()
    qterm, kterm = tile_fwd("l1_terms", fn_fox_terms, m=m, tm=tm, nj=1, rows=[Row(cgate)],
                            pars=[Par(place), Par(ones_q), Par(ones_k)], outs=[Out(TERM_W, BF16), Out(TERM_W, BF16)])
    oc, lse = fox_pair_fwd("l1_attn", z1, qterm, kterm, n_batch=n_batch, seq=seq)
    mix1 = mm2d("l1_out", "nn", oc, w_out_o)
    x3, h3 = tile_fwd("l1_postnorm", fn_addnorm2, m=m, tm=tm, nj=1, rows=[Row(x2), Row(mix1)], pars=[Par(gain(1, 1)), Par(gain(1, 2))],
                      outs=[Out(D_MODEL, F32), Out(D_MODEL, BF16)])
    w_up_g1, w_down_g1 = gather_wait("gather_ffn1_wait", ffn1_sent[0], ffn1_sent[1], ffn1_sent[2], ffn1_sent[3], h3)
    hid1, act1, f1 = _ffn_forward("l1_ffn", 1, h3, w_up_g1, cw5, cb5, w_down_g1, m, seq)
    dy, df1, loss_part, d_g13 = loss_head("loss", x3, f1, tgt, gain(1, 3), m=m, tm=tm)
    dh3, d_wup1, d_cw1, d_cb1, d_wdown1 = _ffn_backward("l1_ffn", 1, df1, h3, hid1, act1, w_up_g1, cw5, cb5, w_down_g1, m, seq)
    dx2, dmix1, d_g11, d_g12 = tile_bwd("l1_dpostnorm", fn_addnorm2, m=m, tm=tm, nj=1, rows=[Row(x2), Row(mix1)],
                                        pars=[Par(gain(1, 1)), Par(gain(1, 2))], cts=[Row(dy), Row(dh3)],
                                        drows=[Out(D_MODEL, F32), Out(D_MODEL, BF16)])
    doc = mm2d("l1_doc", "nt", dmix1, w_out_o, BF16)
    d_wout_o = mm2d("l1_dwout", "tn", oc, dmix1)
    dq, dk, dv, dc = fox_pair_bwd("l1_dattn", z1, qterm, kterm, oc, doc, lse, n_batch=n_batch, seq=seq)
    dzf, d_fbias = tile_bwd("l1_dgate", fn_fox_gate, m=m, tm=seq, nj=1, rows=[Row(z1, LANES, 3072 // LANES)], pars=[Par(fbias)],
                            cts=[Row(dc)], drows=[Out(LANES, BF16)])
    dz1 = jnp.concatenate([dq, dk, dv, dzf], axis=-1)
    dh2 = mm2d("l1_dh", "nt", dz1, w_in_o, BF16)
    d_win_o = mm2d("l1_dwin", "tn", h2, dz1)

    send1 = [_cols_to_blocks(d_win_o[None, :, :3088]).astype(BF16),
             d_wout_o.reshape(N_DEV, 1, D_MODEL // N_DEV, D_MODEL).astype(BF16), d_wup1, d_wdown1]
    sent1 = _split_exchange("exchange_l1_start", send1, [own_slot_only(t, me) for t in send1], None, None)

    dx1, df0, d_g03, d_g10 = tile_bwd("l0_dffnnorm", fn_addnorm2_after, m=m, tm=tm, nj=1, rows=[Row(x1), Row(f0)],
                                      pars=[Par(gain(0, 3)), Par(gain(1, 0)), Par(sent1[4])], cts=[Row(dx2), Row(dh2)],
                                      drows=[Out(D_MODEL, F32), Out(D_MODEL, BF16)])[:4]
    dh1, d_wup0, d_cw0, d_cb0, d_wdown0 = _ffn_backward("l0_ffn", 0, df0, h1, hid0, act0, w_up_g0, cw5, cb5, w_down_g0, m, seq)
    send0 = [d_wup0, d_wdown0]
    sent0 = _split_exchange("exchange_ffn0_start", send0, [own_slot_only(t, me) for t in send0], None, None)
    dx0a, dmix0, d_g01, d_g02 = tile_bwd("l0_dpostnorm", fn_addnorm2_after, m=m, tm=tm, nj=1, rows=[Row(x0), Row(mix0)],
                                         pars=[Par(gain(0, 1)), Par(gain(0, 2)), Par(sent0[4])], cts=[Row(dx1), Row(dh1)],
                                         drows=[Out(D_MODEL, F32), Out(D_MODEL, BF16)])[:4]
    dmixcat0 = mm2d("l0_dmixcat", "nt", dmix0, w_out_e, BF16)
    d_wout_e = mm2d("l0_dwout", "tn", mixcat0, dmix0)
    dzq, dzf0, dzv, dzg, d_lb, d_hnorm = hgrn_bwd("l0_dhgrn", z0, sprev, hgrn_lb_logits, hgrn_norm, dmixcat0, n_batch=n_batch, seq=seq)
    dzx, dzy, d_rcw, d_rcb, d_wa, d_ba, d_wx, d_bx, d_lam = tile_bwd(
        "l0_drglru", fn_rglru, m=m, tm=seq, nj=B_WIDTH // LANES, rows=rg_rows(), pars=rg_pars(),
        cts=[Row(dmixcat0, LANES, A_WIDTH // LANES)], drows=[Out(B_WIDTH, BF16, LANES), Out(B_WIDTH, BF16, LANES)])
    dz0 = jnp.concatenate([dzq, dzf0, dzv, dzg, dzx, dzy], axis=-1)
    d_win_e = mm("l0_dwin", "tn",
                 Blk(h0, (tmm, D_MODEL), lambda i, j, k: (k, 0)),
                 Blk(dz0, (tmm, 768), lambda i, j, k: (k, j)),
                 Blk(w_in_e.shape, (2, None, D_MODEL, 384), lambda i, j, k: (j, 0, 0, 0)), BF16, (1, N_DEV // 2, nm), o_split=True)
    send_e = [d_win_e, d_wout_e.reshape(N_DEV, 1, D_MODEL // N_DEV, D_MODEL).astype(BF16)]
    sent_e = _split_exchange("exchange_even_start", send_e, [own_slot_only(t, me) for t in send_e], None, None)
    d_ffn_cb = jnp.stack([d_cb0, d_cb1]).reshape(n_layer, 2 * D_FF)
    rep = {"hgrn_lb_logits": d_lb, "hgrn_norm": d_hnorm, "rg_conv_b": d_rcb, "rg_wa": _block_diag_grad(d_wa)[None], "rg_ba": d_ba,
           "rg_wx": _block_diag_grad(d_wx)[None], "rg_bx": d_bx, "rg_lambda": d_lam, "fox_f_bias": d_fbias[:, :C_HEADS],
           "ffn_conv_b": d_ffn_cb}
    rep_blocks = [rep[n] for n in REPLICATED] + [loss_part]
    rep_sent = gather_start("gather_partials_start", rep_blocks, [own_block_only(t) for t in rep_blocks])
    dh0 = mm("l0_dh", "nt",
             Blk(dz0, (tmm, 768), lambda i, j, k: (i, k)),
             Blk(w_in_e, (2, None, D_MODEL, 384), lambda i, j, k: (k, 0, 0, 0)),
             Blk((m, D_MODEL), (tmm, D_MODEL), lambda i, j, k: (i, 0)), BF16, (nm, 1, N_DEV // 2), after=sent_e[4] + rep_sent[4],
             b_join=True)
    dx0, d_g00 = tile_bwd("l0_dprenorm", fn_input_norm, m=m, tm=tm, nj=1, rows=[Row(x0)], pars=[Par(gain(0, 0))],
                          cts=[Row(dx0a), Row(dh0)], drows=[Out(D_MODEL, F32)])

    d_gains = jnp.stack([jnp.concatenate([d_g00, d_g01, d_g02, d_g03], axis=0), jnp.concatenate([d_g10, d_g11, d_g12, d_g13], axis=0)])
    d_ffn_cw = jnp.stack([d_cw0, d_cw1], axis=2).reshape(N_DEV, n_layer, FFN_CONV, FF_BLK)
    r_in_o, r_out_o, r_up1, r_down1 = _split_exchange("exchange_l1_wait", sent1[2], sent1[3], sent1[:2], dx0)
    r_up0, r_down0 = _split_exchange("exchange_ffn0_wait", sent0[2], sent0[3], sent0[:2], dx0)
    r_in_e, r_out_e = _split_exchange("exchange_even_wait", sent_e[2], sent_e[3], sent_e[:2], dx0)
    recv, res = {}, {}
    for n, r in (("even_w_in", r_in_e), ("even_w_out", r_out_e), ("odd_w_in", r_in_o), ("odd_w_out", r_out_o)):
        res[n] = adam_tiled("adam_" + n, r, w[n], mom[n], var[n])
    for n, parts_l in (("ffn_w_up", (r_up0, r_up1)), ("ffn_w_down", (r_down0, r_down1))):
        first_layer = adam_tiled(f"adam_{n}_0", parts_l[0], w[n], mom[n], var[n], layer=0)
        res[n] = adam_tiled(f"adam_{n}_1", parts_l[1], w[n], mom[n], var[n], layer=1, prev=first_layer)
    small_send = [_cols_to_blocks(d_gains), _cols_to_blocks(d_rcw[None]), d_ffn_cw]
    recv.update(zip(SMALL_SHARDED, all_to_all("exchange_small", small_send)))

    parts = gather_wait("gather_partials_wait", rep_sent[0], rep_sent[1], rep_sent[2], rep_sent[3], dx0)
    for n, p in zip(REPLICATED, parts):
        recv[n] = p
    small = SMALL_SHARDED + REPLICATED
    small_res, (loss_sum,) = adam_small("adam_small", [(recv[n], w[n], mom[n], var[n]) for n in small], [parts[-1]])
    res.update(dict(zip(small, small_res)))

    out = [loss_sum[0, 0], dx0.reshape(x.shape)]
    for k in range(4):
        out += [res[n][k] for n in NAMES]
    return tuple(out)
```

```python
import functools

import jax
import jax.numpy as jnp
from jax import lax
from jax.experimental import pallas as pl
from jax.experimental.pallas import tpu as pltpu

F32 = jnp.float32
BF16 = jnp.bfloat16

D_MODEL = 1024
A_HEADS = 4
A_WIDTH = 512
HGRN_CHUNK = 64
HGRN_SEG = 512
B_WIDTH = 512
B_BLOCKS = 8
B_BLOCK_DIM = 64
B_CONV = 4
RG_C = 8.0
C_HEADS = 16
C_HEAD_DIM = 64
D_FF = 2816
FFN_CONV = 3
EPS = 1e-6
LANES = 128
HALO = 16
N_DEV = 8
FF_BLK = 2 * D_FF // N_DEV
MESH = pl.DeviceIdType.MESH
NEG = -1e30
VMEM_LIMIT = 56 * 1024 * 1024

ADAM_LR = 0.001
ADAM_B1 = 0.9
ADAM_B2 = 0.999
ADAM_EPS = 1e-08
ADAM_WD = 0.01
ADAM_STEP = 10


def _dg(a, b, pat):
    nb = a.ndim - 2
    batch = (tuple(range(nb)), tuple(range(nb)))
    ca = a.ndim - 1 if pat[0] == "n" else a.ndim - 2
    cb = b.ndim - 2 if pat[1] == "n" else b.ndim - 1
    return lax.dot_general(a.astype(BF16), b.astype(BF16), (((ca,), (cb,)), batch), preferred_element_type=F32)


@functools.partial(jax.custom_vjp, nondiff_argnums=(2,))
def bdot(a, b, pat):
    return _dg(a, b, pat)


def _bdot_fwd(a, b, pat):
    return _dg(a, b, pat), (a, b)


def _bdot_bwd(pat, res, g):
    a, b = res
    if pat == "nn":
        return _dg(g, b, "nt"), _dg(a, g, "tn")
    if pat == "nt":
        return _dg(g, b, "nn"), _dg(g, a, "tn")
    return _dg(b, g, "nt"), _dg(a, g, "nn")


bdot.defvjp(_bdot_fwd, _bdot_bwd)


def _shift_raw(x, s, up, fill):
    if s == 0:
        return x
    n = x.shape[0]
    r = pltpu.roll(x, (n - s) if up else s, 0)
    idx = lax.broadcasted_iota(jnp.int32, x.shape, 0)
    mask = (idx >= n - s) if up else (idx < s)
    return jnp.where(mask, jnp.asarray(fill, x.dtype), r)


@functools.partial(jax.custom_vjp, nondiff_argnums=(1,))
def shift_down(x, s):
    return _shift_raw(x, s, False, 0.0)


def _shift_down_fwd(x, s):
    return _shift_raw(x, s, False, 0.0), None


def _shift_down_bwd(s, _, g):
    return (_shift_raw(g, s, True, 0.0),)


shift_down.defvjp(_shift_down_fwd, _shift_down_bwd)


def _scan_impl(a, u, up):
    n = a.shape[0]
    s = 1
    while s < n:
        u = a * _shift_raw(u, s, up, 0.0) + u
        if 2 * s < n:
            a = a * _shift_raw(a, s, up, 1.0)
        s *= 2
    return u


@jax.custom_vjp
def lin_scan(a, u):
    return _scan_impl(a, u, False)


def _lin_scan_fwd(a, u):
    h = _scan_impl(a, u, False)
    return h, (a, h)


def _lin_scan_bwd(res, g):
    a, h = res
    gh = _scan_impl(_shift_raw(a, 1, True, 0.0), g, True)
    return gh * _shift_raw(h, 1, False, 0.0), gh


lin_scan.defvjp(_lin_scan_fwd, _lin_scan_bwd)


def _cumsum_impl(x, up, period):
    n = x.shape[0]
    span = n if period is None else period
    idx = lax.broadcasted_iota(jnp.int32, x.shape, 0)
    pos = idx if period is None else idx % period
    s = 1
    while s < span:
        sh = _shift_raw(x, s, up, 0.0)
        if period is not None:
            keep = (pos < period - s) if up else (pos >= s)
            sh = jnp.where(keep, sh, 0.0)
        x = x + sh
        s *= 2
    return x


@functools.partial(jax.custom_vjp, nondiff_argnums=(1,))
def cumsum_rows(x, period):
    return _cumsum_impl(x, False, period)


def _cumsum_fwd(x, period):
    return _cumsum_impl(x, False, period), None


def _cumsum_bwd(period, _, g):
    return (_cumsum_impl(g, True, period),)


cumsum_rows.defvjp(_cumsum_fwd, _cumsum_bwd)


def _sigmoid(x):
    return jax.nn.sigmoid(x)


def _expm1(x):
    return jnp.tanh(0.5 * x) * (jnp.exp(x) + 1.0)


def _softplus(x):
    return jnp.maximum(x, 0.0) + jnp.log(1.0 + jnp.exp(-jnp.abs(x)))


def _rms(x, g):
    return x * lax.rsqrt(jnp.mean(x * x, axis=-1, keepdims=True) + EPS) * g


def fn_prenorm(x, g):
    return (_rms(x, g).astype(BF16),)


def fn_prenorm_after(x, g, _token):
    return fn_prenorm(x, g)


def fn_addnorm2(x, y, g_post, g_pre):
    x1 = x + _rms(y, g_post)
    return x1, _rms(x1, g_pre).astype(BF16)


def fn_addnorm2_after(x, y, g_post, g_pre, _token):
    return fn_addnorm2(x, y, g_post, g_pre)


def fn_input_norm(x, g):
    return x, _rms(x, g).astype(BF16)


def _causal_conv(x, w, b, taps):
    c = b
    for k in range(taps):
        c = c + w[k:k + 1, :] * shift_down(x, taps - 1 - k)
    return c


def fn_rglru(xb, yb, cw, cb, wa, ba, wx, bx, lam):
    xf = _causal_conv(xb, cw, cb, B_CONV)
    r = _sigmoid(bdot(xf, wa, "nn") + ba)
    i = _sigmoid(bdot(xf, wx, "nn") + bx)
    log_a = -RG_C * r * _softplus(-lam)
    a = jnp.exp(log_a)
    u = jnp.sqrt(-_expm1(2.0 * log_a)) * (i * xf)
    h = lin_scan(a, u)
    return ((h * jax.nn.gelu(yb)).astype(BF16),)


def fn_fox_gate(zf, bias):
    return (cumsum_rows(jax.nn.log_sigmoid(zf + bias), None),)


def fn_hgrn_seg(q, fl, v, g, st, logits, hn):
    rows = q.shape[0]
    nc = rows // HGRN_CHUNK
    l0, l1, l2 = logits[0:1, :], logits[1:2, :], logits[2:3, :]
    mx = jnp.maximum(jnp.maximum(l0, l1), l2)
    e0, e1, e2 = jnp.exp(l0 - mx), jnp.exp(l1 - mx), jnp.exp(l2 - mx)
    lb = e0 / (e0 + e1 + e2)
    forget = lb + (1.0 - lb) * _sigmoid(fl)
    qs = q * _sigmoid(q)
    kk = 1.0 - forget
    logf = jnp.log(forget)
    bcum = cumsum_rows(logf, HGRN_CHUNK)
    c3 = lambda t: t.reshape(nc, HGRN_CHUNK, 128)
    b_last = jnp.sum(c3(logf), axis=1, keepdims=True)
    bcum3 = c3(bcum)
    q_dec = c3(qs) * jnp.exp(bcum3)
    k_dec = c3(kk) * jnp.exp(-bcum3)
    k_upd = c3(kk) * jnp.exp(b_last - bcum3)
    v3 = c3(v)
    scores = bdot(q_dec, k_dec, "nt")
    ri = lax.broadcasted_iota(jnp.int32, scores.shape, 1)
    ci = lax.broadcasted_iota(jnp.int32, scores.shape, 2)
    scores = jnp.where(ri >= ci, scores, 0.0)
    o = bdot(scores, v3, "nn")
    upd_t = bdot(v3, k_upd, "tn")
    dec = jnp.exp(b_last)
    prev = []
    for n in range(nc):
        prev.append(st)
        st = st * dec[n] + upd_t[n]
    o = o + bdot(q_dec, jnp.stack(prev), "nt")
    o = o.reshape(rows, 128)
    o = o * lax.rsqrt(jnp.mean(o * o, axis=-1, keepdims=True) + EPS) * hn
    return (o * _sigmoid(g)).astype(BF16), st


def _ffn_conv(xg, xv, cw, cb):
    cg = _causal_conv(xg, cw[0], cb[0], FFN_CONV)[HALO:]
    cv = _causal_conv(xv, cw[1], cb[1], FFN_CONV)[HALO:]
    return cg, cv


def _ffn_gate(cg, cv):
    return jax.nn.gelu(cg) * cv


class Row:
    def __init__(self, arr, cb=None, off=0):
        self.arr, self.cb, self.off = arr, cb, off

    def spec(self, tm):
        if self.cb is None:
            return pl.BlockSpec((tm, self.arr.shape[1]), lambda j, i: (i, 0))
        off = self.off
        return pl.BlockSpec((tm, self.cb), lambda j, i: (i, j + off))


class Par:
    def __init__(self, arr, kind="full", bs=None):
        self.arr, self.kind, self.bs = arr, kind, bs

    def block(self):
        if self.kind == "full":
            return self.arr.shape
        if self.kind == "col":
            return (self.arr.shape[0], self.bs)
        return (self.bs, self.arr.shape[1])

    def spec(self):
        if self.kind == "full":
            return pl.BlockSpec(self.block(), lambda j, i: (0, 0))
        if self.kind == "col":
            return pl.BlockSpec(self.block(), lambda j, i: (0, j))
        return pl.BlockSpec(self.block(), lambda j, i: (j, 0))


class Out:
    def __init__(self, width, dtype, cb=None, off=0):
        self.width, self.dtype, self.cb, self.off = width, dtype, cb, off

    def spec(self, tm):
        if self.cb is None:
            return pl.BlockSpec((tm, self.width), lambda j, i: (i, 0))
        off = self.off
        return pl.BlockSpec((tm, self.cb), lambda j, i: (i, j + off))


def _params(sem):
    return pltpu.CompilerParams(dimension_semantics=sem, vmem_limit_bytes=VMEM_LIMIT)


def tile_fwd(name, fn, *, m, tm, nj, rows, pars, outs, n_acc=0):
    n_r, n_p, n_o = len(rows), len(pars), len(outs)

    def body(*refs):
        ins = [r[...] for r in refs[:n_r + n_p]]
        res = fn(*ins)
        o_refs = refs[n_r + n_p:]
        for k in range(n_o):
            o_refs[k][...] = res[k].astype(o_refs[k].dtype)
        first = jnp.logical_and(pl.program_id(0) == 0, pl.program_id(1) == 0)
        for k in range(n_acc):
            ref = o_refs[n_o + k]

            @pl.when(first)
            def _():
                ref[...] = jnp.zeros_like(ref)

            ref[...] += res[n_o + k]

    out_shape = [jax.ShapeDtypeStruct((m, o.width), o.dtype) for o in outs]
    out_specs = [o.spec(tm) for o in outs]
    for _ in range(n_acc):
        out_shape.append(jax.ShapeDtypeStruct((1, LANES), F32))
        out_specs.append(pl.BlockSpec((1, LANES), lambda j, i: (0, 0)))
    sem = ("arbitrary", "arbitrary") if n_acc else ("parallel", "parallel")
    return pl.pallas_call(
        body, grid=(nj, m // tm), name=name,
        in_specs=[r.spec(tm) for r in rows] + [p.spec() for p in pars],
        out_specs=out_specs, out_shape=out_shape, compiler_params=_params(sem),
    )(*[r.arr for r in rows], *[p.arr for p in pars])


def tile_bwd(name, fn, *, m, tm, nj, rows, pars, cts, drows):
    n_r, n_p, n_c = len(rows), len(pars), len(cts)
    want = [k for k in range(n_r) if drows[k] is not None]

    def body(*refs):
        ins = [r[...] for r in refs[:n_r + n_p]]
        ct = [r[...] for r in refs[n_r + n_p:n_r + n_p + n_c]]
        o_refs = refs[n_r + n_p + n_c:]
        res, vjp = jax.vjp(fn, *ins)
        grads = vjp(tuple(c.astype(r.dtype) for c, r in zip(ct, res)))
        for pos, k in enumerate(want):
            o_refs[pos][...] = grads[k].astype(o_refs[pos].dtype)
        for k in range(n_p):
            ref = o_refs[len(want) + k]
            first = pl.program_id(1) == 0
            if pars[k].kind == "full":
                first = jnp.logical_and(first, pl.program_id(0) == 0)

            @pl.when(first)
            def _():
                ref[...] = jnp.zeros_like(ref)

            ref[...] += grads[n_r + k].astype(F32)

    out_shape = [jax.ShapeDtypeStruct((m, drows[k].width), drows[k].dtype) for k in want]
    out_specs = [drows[k].spec(tm) for k in want]
    for p in pars:
        out_shape.append(jax.ShapeDtypeStruct(p.arr.shape, F32))
        out_specs.append(p.spec())
    return pl.pallas_call(
        body, grid=(nj, m // tm), name=name,
        in_specs=[r.spec(tm) for r in rows] + [p.spec() for p in pars] + [c.spec(tm) for c in cts],
        out_specs=out_specs, out_shape=out_shape, compiler_params=_params(("arbitrary", "arbitrary")),
    )(*[r.arr for r in rows], *[p.arr for p in pars], *[c.arr for c in cts])


def loss_head(name, x, y, tgt, g, *, m, tm):
    def body(x_ref, y_ref, t_ref, g_ref, dout_ref, dy_ref, loss_ref, dg_ref):
        normed, vjp = jax.vjp(_rms, y_ref[...], g_ref[...])
        err = x_ref[...] + normed - t_ref[...]
        dout = err * (1.0 / D_MODEL)
        dy, dg = vjp(dout)
        dout_ref[...] = dout
        dy_ref[...] = dy.astype(dy_ref.dtype)

        @pl.when(pl.program_id(0) == 0)
        def _():
            loss_ref[...] = jnp.zeros_like(loss_ref)
            dg_ref[...] = jnp.zeros_like(dg_ref)

        loss_ref[...] += 0.5 * jnp.sum(jnp.mean(err * err, axis=-1, keepdims=True), axis=0, keepdims=True)
        dg_ref[...] += dg

    row = pl.BlockSpec((tm, D_MODEL), lambda i: (i, 0))
    whole = lambda w: pl.BlockSpec((1, w), lambda i: (0, 0))
    return pl.pallas_call(
        body, grid=(m // tm,), name=name, in_specs=[row, row, row, whole(D_MODEL)],
        out_specs=[row, row, whole(LANES), whole(D_MODEL)],
        out_shape=[jax.ShapeDtypeStruct((m, D_MODEL), F32), jax.ShapeDtypeStruct((m, D_MODEL), BF16),
                   jax.ShapeDtypeStruct((1, LANES), F32), jax.ShapeDtypeStruct((1, D_MODEL), F32)],
        compiler_params=_params(("arbitrary",)),
    )(x, y, tgt, g)


class Blk:
    def __init__(self, arr, block, index):
        self.arr, self.block, self.index = arr, block, index

    def spec(self):
        return pl.BlockSpec(self.block, self.index)


def _flat2(v):
    return v if v.ndim == 2 else v.reshape(-1, v.shape[-1])


def mm(name, pat, a, b, o, out_dtype, grid, after=None, b_join=False, o_split=False):
    nk = grid[2]
    o_shape = o.arr

    def put(o_ref, r):
        if o_split:
            half = r.shape[1] // 2
            o_ref[0] = r[:, :half].astype(out_dtype)
            o_ref[1] = r[:, half:].astype(out_dtype)
        else:
            o_ref[...] = r.astype(out_dtype).reshape(o_ref.shape)

    def body(*refs):
        a_ref, b_ref = refs[0], refs[1]
        o_ref = refs[3] if after is not None else refs[2]
        bv = jnp.concatenate([b_ref[0], b_ref[1]], axis=1) if b_join else _flat2(b_ref[...])
        r = _dg(_flat2(a_ref[...]), bv, pat)
        if nk == 1:
            put(o_ref, r)
            return
        acc_ref = refs[-1]
        kk = pl.program_id(2)

        @pl.when(kk == 0)
        def _():
            acc_ref[...] = r

        @pl.when(kk > 0)
        def _():
            acc_ref[...] += r

        @pl.when(kk == nk - 1)
        def _():
            put(o_ref, acc_ref[...])

    ob = [d for d in o.block if d is not None]
    if o_split:
        acc_shape = (ob[1], 2 * ob[2])
    else:
        acc_shape = (ob[0], ob[1]) if len(ob) == 2 else (ob[0] * ob[1], ob[2])
    in_specs = [a.spec(), b.spec()]
    args = [a.arr, b.arr]
    if after is not None:
        in_specs.append(pl.BlockSpec(memory_space=pl.ANY))
        args.append(after)
    return pl.pallas_call(
        body, grid=grid, name=name, in_specs=in_specs, out_specs=o.spec(),
        out_shape=jax.ShapeDtypeStruct(o_shape, out_dtype),
        scratch_shapes=[pltpu.VMEM(acc_shape, F32)] if nk > 1 else [],
        compiler_params=_params(("parallel", "parallel", "arbitrary")),
    )(*args)


def _div_tile(n, cap):
    if n <= cap:
        return n
    best = 128
    for t in range(128, cap + 1, 128):
        if n % t == 0:
            best = t
    return best


def mm2d(name, pat, a, b, out_dtype=F32):
    if pat == "tn":
        k, m = a.shape
    else:
        m, k = a.shape
    n = b.shape[0] if pat == "nt" else b.shape[1]
    tm, tn, tk = _div_tile(m, 1024), _div_tile(n, 1024), _div_tile(k, 1024)
    a_blk = Blk(a, (tk, tm), lambda i, j, kk: (kk, i)) if pat == "tn" else Blk(a, (tm, tk), lambda i, j, kk: (i, kk))
    b_blk = Blk(b, (tn, tk), lambda i, j, kk: (j, kk)) if pat == "nt" else Blk(b, (tk, tn), lambda i, j, kk: (kk, j))
    o_blk = Blk((m, n), (tm, tn), lambda i, j, kk: (i, j))
    return mm(name, pat, a_blk, b_blk, o_blk, out_dtype, (m // tm, n // tn, k // tk))


def hgrn_fwd(name, z, logits, hnorm, *, n_batch, seq):
    m = n_batch * seq
    ts = min(HGRN_SEG, seq)
    n_seg = seq // ts

    def body(q_ref, f_ref, v_ref, g_ref, lg_ref, hn_ref, o_ref, sp_ref, st_ref):
        s = pl.program_id(2)

        @pl.when(s == 0)
        def _():
            st_ref[...] = jnp.zeros_like(st_ref)

        st = st_ref[...]
        sp_ref[...] = st
        o, st_new = fn_hgrn_seg(q_ref[...], f_ref[...], v_ref[...], g_ref[...], st, lg_ref[...], hn_ref[...])
        o_ref[...] = o
        st_ref[...] = st_new

    part = lambda p: pl.BlockSpec((ts, 128), lambda h, b, s: (b * n_seg + s, 4 * p + h))
    return pl.pallas_call(
        body, grid=(A_HEADS, n_batch, n_seg), name=name,
        in_specs=[part(0), part(1), part(2), part(3),
                  pl.BlockSpec((3, 128), lambda h, b, s: (0, h)),
                  pl.BlockSpec((1, 128), lambda h, b, s: (0, h))],
        out_specs=[pl.BlockSpec((ts, 128), lambda h, b, s: (b * n_seg + s, h)),
                   pl.BlockSpec((128, 128), lambda h, b, s: ((b * n_seg + s) * A_HEADS + h, 0))],
        out_shape=[jax.ShapeDtypeStruct((m, A_WIDTH), BF16),
                   jax.ShapeDtypeStruct((n_batch * n_seg * A_HEADS * 128, 128), F32)],
        scratch_shapes=[pltpu.VMEM((128, 128), F32)],
        compiler_params=_params(("arbitrary", "arbitrary", "arbitrary")),
    )(z, z, z, z, logits, hnorm)


def hgrn_bwd(name, z, sprev, logits, hnorm, do, *, n_batch, seq):
    m = n_batch * seq
    ts = min(HGRN_SEG, seq)
    n_seg = seq // ts

    def body(q_ref, f_ref, v_ref, g_ref, sp_ref, lg_ref, hn_ref, do_ref, dq_ref, df_ref, dv_ref, dg_ref, dlg_ref, dhn_ref, dst_ref):
        s = pl.program_id(2)

        @pl.when(s == 0)
        def _():
            dst_ref[...] = jnp.zeros_like(dst_ref)

        res, vjp = jax.vjp(fn_hgrn_seg, q_ref[...], f_ref[...], v_ref[...], g_ref[...], sp_ref[...], lg_ref[...], hn_ref[...])
        dq, df, dv, dg, dst, dlg, dhn = vjp((do_ref[...].astype(res[0].dtype), dst_ref[...]))
        dq_ref[...] = dq.astype(dq_ref.dtype)
        df_ref[...] = df.astype(df_ref.dtype)
        dv_ref[...] = dv.astype(dv_ref.dtype)
        dg_ref[...] = dg.astype(dg_ref.dtype)
        dst_ref[...] = dst
        first = jnp.logical_and(pl.program_id(1) == 0, s == 0)

        @pl.when(first)
        def _():
            dlg_ref[...] = jnp.zeros_like(dlg_ref)
            dhn_ref[...] = jnp.zeros_like(dhn_ref)

        dlg_ref[...] += dlg
        dhn_ref[...] += dhn

    rev = lambda b, s: b * n_seg + (n_seg - 1 - s)
    part = lambda p: pl.BlockSpec((ts, 128), lambda h, b, s: (rev(b, s), 4 * p + h))
    head = pl.BlockSpec((ts, 128), lambda h, b, s: (rev(b, s), h))
    dpart = jax.ShapeDtypeStruct((m, A_WIDTH), BF16)
    return pl.pallas_call(
        body, grid=(A_HEADS, n_batch, n_seg), name=name,
        in_specs=[part(0), part(1), part(2), part(3),
                  pl.BlockSpec((128, 128), lambda h, b, s: (rev(b, s) * A_HEADS + h, 0)),
                  pl.BlockSpec((3, 128), lambda h, b, s: (0, h)),
                  pl.BlockSpec((1, 128), lambda h, b, s: (0, h)),
                  head],
        out_specs=[head, head, head, head,
                   pl.BlockSpec((3, 128), lambda h, b, s: (0, h)),
                   pl.BlockSpec((1, 128), lambda h, b, s: (0, h))],
        out_shape=[dpart, dpart, dpart, dpart,
                   jax.ShapeDtypeStruct(logits.shape, F32),
                   jax.ShapeDtypeStruct(hnorm.shape, F32)],
        scratch_shapes=[pltpu.VMEM((128, 128), F32)],
        compiler_params=_params(("arbitrary", "arbitrary", "arbitrary")),
    )(z, z, z, z, sprev, logits, hnorm, do)


FFN_ROWS = 128
FFN_LANES = 128


def _ffn_tiles(m, seq):
    tm = min(512, seq)
    return tm, seq // tm, m // tm


def ffn_mid_fwd(name, hid, cw, cb, layer, *, m, seq):
    tm, n_t, n_i = _ffn_tiles(m, seq)
    hb = tm // HALO

    def body(x_ref, xb_ref, cw_ref, cb_ref, o_ref, c_ref):
        first = pl.program_id(1) % n_t == 0
        before = jnp.where(first, 0.0, xb_ref[...])
        ext = jnp.concatenate([before, x_ref[...]], axis=1)
        cg, cv = _ffn_conv(ext[0], ext[1], cw_ref[...], cb_ref[...])
        o_ref[...] = _ffn_gate(cg, cv).astype(o_ref.dtype)
        c_ref[0] = cg.astype(c_ref.dtype)
        c_ref[1] = cv.astype(c_ref.dtype)

    return pl.pallas_call(
        body, grid=(N_DEV // 2, n_i), name=name,
        in_specs=[pl.BlockSpec((2, None, tm, FF_BLK), lambda d, i: (0, d, i, 0)),
                  pl.BlockSpec((2, None, HALO, FF_BLK), lambda d, i: (0, d, jnp.maximum(i * hb - 1, 0), 0)),
                  pl.BlockSpec((2, None, None, FFN_CONV, FF_BLK), lambda d, i: (0, d, layer, 0, 0)),
                  pl.BlockSpec((None, 2, None, 1, FF_BLK), lambda d, i: (layer, 0, d, 0, 0))],
        out_specs=[pl.BlockSpec((None, tm, FF_BLK), lambda d, i: (d, i, 0)),
                   pl.BlockSpec((2, None, tm, FF_BLK), lambda d, i: (0, d, i, 0))],
        out_shape=[jax.ShapeDtypeStruct((N_DEV // 2, m, FF_BLK), BF16),
                   jax.ShapeDtypeStruct((2, N_DEV // 2, m, FF_BLK), BF16)],
        compiler_params=_params(("parallel", "parallel")),
    )(hid, hid, cw, cb)


def ffn_mid_bwd(name, hid, conv, cw, dact, layer, *, m, seq):
    tm, n_t, n_i = _ffn_tiles(m, seq)
    hb = tm // HALO
    last_blk = m // HALO - 1

    rc = min(FFN_ROWS, tm)
    lane_chunks = [(l0, min(FFN_LANES, FF_BLK - l0)) for l0 in range(0, FF_BLK, FFN_LANES)]

    def body(x_ref, c_ref, ca_ref, cw_ref, da_ref, daa_ref, dx_ref, dcw_ref, dcb_ref, cext_ref, dext_ref):
        i = pl.program_id(1)
        last = i % n_t == n_t - 1
        cext_ref[:, :tm] = c_ref[...]
        cext_ref[:, tm:] = ca_ref[...]
        dext_ref[:tm] = da_ref[...]
        dext_ref[tm:] = jnp.where(last, jnp.zeros_like(daa_ref[...]), daa_ref[...])

        @pl.when(i == 0)
        def _():
            dcw_ref[...] = jnp.zeros_like(dcw_ref)
            dcb_ref[...] = jnp.zeros_like(dcb_ref)

        for l0, lw in lane_chunks:
            lanes = slice(l0, l0 + lw)

            def chunk(c, sums, lanes=lanes, lw=lw):
                r0 = pl.multiple_of(c * rc, rc)
                ext = pl.ds(r0, rc + HALO)
                cg, cv = cext_ref[0, ext, lanes].astype(F32), cext_ref[1, ext, lanes].astype(F32)
                _, vjp_gate = jax.vjp(_ffn_gate, cg, cv)
                dconv = vjp_gate(dext_ref[ext, lanes].astype(F32))
                out = []
                for half in range(2):
                    x = x_ref[half, pl.ds(r0, rc), lanes]
                    dx = None
                    for k in range(FFN_CONV):
                        s = FFN_CONV - 1 - k
                        dc_s = _shift_raw(dconv[half], s, True, 0.0)[:rc]
                        term = cw_ref[half, k:k + 1, lanes] * dc_s
                        dx = term if dx is None else dx + term
                        out.append(sums[len(out)] + jnp.sum(x * dc_s, axis=0, keepdims=True))
                    out.append(sums[len(out)] + jnp.sum(dconv[half][:rc], axis=0, keepdims=True))
                    dx_ref[half, pl.ds(r0, rc), lanes] = dx.astype(dx_ref.dtype)
                return tuple(out)

            zero = jnp.zeros((1, lw), F32)
            sums = lax.fori_loop(0, tm // rc, chunk, (zero,) * (2 * (FFN_CONV + 1)))
            for half in range(2):
                base = half * (FFN_CONV + 1)
                for k in range(FFN_CONV):
                    dcw_ref[half, k:k + 1, lanes] += sums[base + k]
                dcb_ref[half, :, lanes] += sums[base + FFN_CONV]

    return pl.pallas_call(
        body, grid=(N_DEV // 2, n_i), name=name,
        in_specs=[pl.BlockSpec((2, None, tm, FF_BLK), lambda d, i: (0, d, i, 0)),
                  pl.BlockSpec((2, None, tm, FF_BLK), lambda d, i: (0, d, i, 0)),
                  pl.BlockSpec((2, None, HALO, FF_BLK), lambda d, i: (0, d, jnp.minimum((i + 1) * hb, last_blk), 0)),
                  pl.BlockSpec((2, None, None, FFN_CONV, FF_BLK), lambda d, i: (0, d, layer, 0, 0)),
                  pl.BlockSpec((None, tm, FF_BLK), lambda d, i: (d, i, 0)),
                  pl.BlockSpec((None, HALO, FF_BLK), lambda d, i: (d, jnp.minimum((i + 1) * hb, last_blk), 0))],
        out_specs=[pl.BlockSpec((2, None, tm, FF_BLK), lambda d, i: (0, d, i, 0)),
                   pl.BlockSpec((2, None, FFN_CONV, FF_BLK), lambda d, i: (0, d, 0, 0)),
                   pl.BlockSpec((2, None, 1, FF_BLK), lambda d, i: (0, d, 0, 0))],
        out_shape=[jax.ShapeDtypeStruct((2, N_DEV // 2, m, FF_BLK), BF16),
                   jax.ShapeDtypeStruct((2, N_DEV // 2, FFN_CONV, FF_BLK), F32),
                   jax.ShapeDtypeStruct((2, N_DEV // 2, 1, FF_BLK), F32)],
        scratch_shapes=[pltpu.VMEM((2, tm + HALO, FF_BLK), BF16), pltpu.VMEM((tm + HALO, FF_BLK), BF16)],
        compiler_params=_params(("arbitrary", "arbitrary")),
    )(hid, conv, conv, cw, dact, dact)


ATT_BLK = 512
N_PAIR = C_HEADS // 2
TERM_W = C_HEADS * LANES


def term_placement():
    import numpy as np
    place = np.zeros((3, LANES, TERM_W), np.float32)
    ones_q = np.zeros((1, TERM_W), np.float32)
    ones_k = np.zeros((1, TERM_W), np.float32)
    for h in range(C_HEADS):
        for j in range(3):
            place[j, h, h * LANES + C_HEAD_DIM + j] = 1.0
            ones_q[0, h * LANES + C_HEAD_DIM + 3 + j] = 1.0
            ones_k[0, h * LANES + C_HEAD_DIM + j] = 1.0
    return (jnp.asarray(place.reshape(3 * LANES, TERM_W), BF16), jnp.asarray(ones_q, F32), jnp.asarray(ones_k, F32))


def fn_fox_terms(c, place, ones_q, ones_k):
    parts = _split3(c)
    placed = sum(_dg(parts[j], place[j * LANES:(j + 1) * LANES], "nn") for j in range(3))
    return (placed + ones_q).astype(BF16), (ones_k - pltpu.roll(placed, 3, 1)).astype(BF16)


def _head_tile(z, terms, e):
    lane = lax.broadcasted_iota(jnp.int32, z.shape, 1)
    base = z if e == 0 else pltpu.roll(z, C_HEAD_DIM, 1)
    return jnp.where(lane < C_HEAD_DIM, base, terms.astype(z.dtype))


def _head_only(z, e):
    lane = lax.broadcasted_iota(jnp.int32, z.shape, 1)
    mine = (lane < C_HEAD_DIM) if e == 0 else (lane >= C_HEAD_DIM)
    return jnp.where(mine, z, jnp.zeros_like(z)).astype(BF16)


def _pair_tile(a0, a1):
    lane = lax.broadcasted_iota(jnp.int32, a0.shape, 1)
    return jnp.where(lane < C_HEAD_DIM, a0, pltpu.roll(a1, C_HEAD_DIM, 1))


def _lane_col(a, k):
    lane = lax.broadcasted_iota(jnp.int32, a.shape, 1)
    return jnp.sum(jnp.where(lane == k, a, 0.0), axis=1, keepdims=True)


def _causal(s):
    key = lax.broadcasted_iota(jnp.int32, s.shape, 0)
    qry = lax.broadcasted_iota(jnp.int32, s.shape, 1)
    return qry >= key


def fox_pair_fwd(name, z, qterm, kterm, *, n_batch, seq):
    m = n_batch * seq
    blk = min(ATT_BLK, seq)
    nq = seq // blk
    dh = C_HEAD_DIM

    def body(zq_ref, zk_ref, zv_ref, qt_ref, kt_ref, o_ref, lse_ref, ka_ref, vt_ref):
        qi = pl.program_id(2)

        @pl.when(qi == 0)
        def _():
            zk = zk_ref[...]
            for e in range(2):
                ka_ref[e] = _head_tile(zk, kt_ref[:, e * LANES:(e + 1) * LANES], e).astype(BF16)
            for cb in range(nq):
                vt_ref[cb] = zv_ref[cb * blk:(cb + 1) * blk, :].T.astype(BF16)

        zq = zq_ref[...] * dh ** -0.5
        qa = [_head_tile(zq, qt_ref[:, e * LANES:(e + 1) * LANES], e).astype(BF16) for e in range(2)]

        def block(j, carry, diagonal):
            rows = pl.ds(pl.multiple_of(j * blk, blk), blk)
            out = []
            for e in range(2):
                mx, l, acc = carry[e]
                s = _dg(ka_ref[e, rows, :], qa[e], "nt")
                if diagonal:
                    s = jnp.where(_causal(s), s, NEG)
                mx_new = jnp.maximum(mx, jnp.max(s, axis=0, keepdims=True))
                p = jnp.exp(s - mx_new)
                alpha = jnp.exp(mx - mx_new)
                l = alpha * l + jnp.sum(p, axis=0, keepdims=True)
                acc = alpha * acc + _dg(vt_ref[j, e * dh:(e + 1) * dh, :], p, "nn")
                out.append((mx_new, l, acc))
            return tuple(out)

        one = (jnp.full((1, blk), NEG, F32), jnp.zeros((1, blk), F32), jnp.zeros((dh, blk), F32))
        carry = lax.fori_loop(0, qi, lambda j, cr: block(j, cr, False), (one, one))
        res = block(qi, carry, True)
        ot = jnp.concatenate([res[e][2] / res[e][1] for e in range(2)], axis=0)
        o_ref[...] = ot.T.astype(o_ref.dtype)
        for e in range(2):
            lse_ref[e] = res[e][0] + jnp.log(res[e][1])

    col = lambda part: (lambda b, g, i: (b, part * N_PAIR + g))
    return pl.pallas_call(
        body, grid=(n_batch, N_PAIR, nq), name=name,
        in_specs=[pl.BlockSpec((blk, LANES), lambda b, g, i: (b * nq + i, g)),
                  pl.BlockSpec((seq, LANES), col(1)),
                  pl.BlockSpec((seq, LANES), col(2)),
                  pl.BlockSpec((blk, 2 * LANES), lambda b, g, i: (b * nq + i, g)),
                  pl.BlockSpec((seq, 2 * LANES), lambda b, g, i: (b, g))],
        out_specs=[pl.BlockSpec((blk, LANES), lambda b, g, i: (b * nq + i, g)),
                   pl.BlockSpec((None, None, None, 2, 1, blk), lambda b, g, i: (b, g, i, 0, 0, 0))],
        out_shape=[jax.ShapeDtypeStruct((m, D_MODEL), BF16), jax.ShapeDtypeStruct((n_batch, N_PAIR, nq, 2, 1, blk), F32)],
        scratch_shapes=[pltpu.VMEM((2, seq, LANES), BF16), pltpu.VMEM((nq, LANES, blk), BF16)],
        compiler_params=_params(("parallel", "parallel", "arbitrary")),
    )(z, z, z, qterm, kterm)


def fox_pair_bwd(name, z, qterm, kterm, o, do, lse, *, n_batch, seq):
    m = n_batch * seq
    blk = min(ATT_BLK, seq)
    nq = seq // blk
    dh = C_HEAD_DIM

    def body(zq_ref, zk_ref, zv_ref, qt_ref, kt_ref, o_ref, do_ref, lse_ref, dq_ref, dk_ref, dv_ref, dc_ref,
             qa_ref, doh_ref, del_ref, dqt_ref, dk_acc, dv_acc):
        g, j = pl.program_id(1), pl.program_id(2)
        lane = lax.broadcasted_iota(jnp.int32, (blk, LANES), 1)

        @pl.when(jnp.logical_and(g == 0, j == 0))
        def _():
            dc_ref[...] = jnp.zeros_like(dc_ref)

        @pl.when(j == 0)
        def _():
            zq = zq_ref[...] * dh ** -0.5
            dov = do_ref[...]
            for e in range(2):
                qa_ref[e] = _head_tile(zq, qt_ref[:, e * LANES:(e + 1) * LANES], e).astype(BF16)
                doh_ref[e] = _head_only(dov, e)
            for cb in range(nq):
                rows = slice(cb * blk, (cb + 1) * blk)
                prod_t = (do_ref[rows, :].astype(F32) * o_ref[rows, :].astype(F32)).T
                for e in range(2):
                    del_ref[cb, e] = jnp.sum(prod_t[e * dh:(e + 1) * dh], axis=0, keepdims=True)
            dqt_ref[...] = jnp.zeros_like(dqt_ref)

        zk, zv = zk_ref[...], zv_ref[...]
        ka32 = [_head_tile(zk, kt_ref[:, e * LANES:(e + 1) * LANES], e) for e in range(2)]
        ka = [t.astype(BF16) for t in ka32]
        kat = [t.T.astype(BF16) for t in ka32]
        vh = [_head_only(zv, e) for e in range(2)]
        dk_acc[...] = jnp.zeros_like(dk_acc)
        dv_acc[...] = jnp.zeros_like(dv_acc)

        def block(i, diagonal):
            rows = pl.ds(pl.multiple_of(i * blk, blk), blk)
            for e in range(2):
                qv, dov = qa_ref[e, rows, :], doh_ref[e, rows, :]
                p = jnp.exp(_dg(ka[e], qv, "nt") - lse_ref[i, e])
                if diagonal:
                    p = jnp.where(_causal(p), p, 0.0)
                dv_acc[...] += _dg(p, dov, "nn")
                ds = p * (_dg(vh[e], dov, "nt") - del_ref[i, e])
                dk_acc[e] += _dg(ds, qv, "nn")
                dqt_ref[i, e] += _dg(kat[e], ds, "nn")

        block(j, True)

        def rest(i, carry):
            block(i, False)
            return carry

        lax.fori_loop(j + 1, nq, rest, 0)
        dk0, dk1 = dk_acc[0], dk_acc[1]
        dk_ref[...] = _pair_tile(dk0, dk1).astype(dk_ref.dtype)
        dv_ref[...] = dv_acc[...].astype(dv_ref.dtype)
        rows_j = pl.ds(pl.multiple_of(j * blk, blk), blk)
        for e, dke in enumerate((dk0, dk1)):
            dc_ref[rows_j, :] -= jnp.where(lane == 2 * g + e, _lane_col(dke, dh + 3), 0.0)

        @pl.when(j == nq - 1)
        def _():
            for i in range(nq):
                nat = [dqt_ref[i, e].T for e in range(2)]
                rows = slice(i * blk, (i + 1) * blk)
                dq_ref[rows, :] = (_pair_tile(nat[0], nat[1]) * dh ** -0.5).astype(dq_ref.dtype)
                for e in range(2):
                    dc_ref[rows, :] += jnp.where(lane == 2 * g + e, _lane_col(nat[e], dh), 0.0)

    col = lambda part: (lambda b, g, j: (b, part * N_PAIR + g))
    colj = lambda part: (lambda b, g, j: (b * nq + j, part * N_PAIR + g))
    pair = jax.ShapeDtypeStruct((m, D_MODEL), BF16)
    return pl.pallas_call(
        body, grid=(n_batch, N_PAIR, nq), name=name,
        in_specs=[pl.BlockSpec((seq, LANES), col(0)),
                  pl.BlockSpec((blk, LANES), colj(1)),
                  pl.BlockSpec((blk, LANES), colj(2)),
                  pl.BlockSpec((seq, 2 * LANES), lambda b, g, j: (b, g)),
                  pl.BlockSpec((blk, 2 * LANES), lambda b, g, j: (b * nq + j, g)),
                  pl.BlockSpec((seq, LANES), col(0)),
                  pl.BlockSpec((seq, LANES), col(0)),
                  pl.BlockSpec((None, None, nq, 2, 1, blk), lambda b, g, j: (b, g, 0, 0, 0, 0))],
        out_specs=[pl.BlockSpec((seq, LANES), col(0)),
                   pl.BlockSpec((blk, LANES), colj(0)),
                   pl.BlockSpec((blk, LANES), colj(0)),
                   pl.BlockSpec((seq, LANES), lambda b, g, j: (b, 0))],
        out_shape=[pair, pair, pair, jax.ShapeDtypeStruct((m, LANES), F32)],
        scratch_shapes=[pltpu.VMEM((2, seq, LANES), BF16), pltpu.VMEM((2, seq, LANES), BF16),
                        pltpu.VMEM((nq, 2, 1, blk), F32), pltpu.VMEM((nq, 2, LANES, blk), F32),
                        pltpu.VMEM((2, blk, LANES), F32), pltpu.VMEM((blk, LANES), F32)],
        compiler_params=_params(("arbitrary", "arbitrary", "arbitrary")),
    )(z, z, z, qterm, kterm, o, do, lse)


def _split3(c):
    c1 = c.astype(BF16)
    r1 = c - c1.astype(F32)
    c2 = r1.astype(BF16)
    c3 = (r1 - c2.astype(F32)).astype(BF16)
    return c1, c2, c3


def _mesh_pos():
    return lax.axis_index("x"), lax.axis_index("y"), lax.axis_index("c")


def _flip(v, bit):
    return 1 - v if bit else v


def all_gather(name, blocks):
    n = len(blocks)

    def body(*refs):
        x_refs, out_refs = refs[:n], refs[n:2 * n]
        send_sems, recv_sems, local_sems = refs[2 * n:]
        x, y, c = _mesh_pos()
        me, sibling = (x, y, c), (x, y, 1 - c)
        chips = [(1 - x, y), (x, 1 - y), (1 - x, 1 - y)]

        def slot(a, px, py, pc):
            return out_refs[a].at[4 * px + 2 * py + pc]

        def copy(a, k, blk, to, src=None):
            return pltpu.make_async_remote_copy(
                src_ref=slot(a, *blk) if src is None else src, dst_ref=slot(a, *blk),
                send_sem=send_sems.at[a, k], recv_sem=recv_sems.at[a, k], device_id=to, device_id_type=MESH)

        mine = [pltpu.make_async_copy(x_refs[a], slot(a, *me), local_sems.at[a]) for a in range(n)]
        for cp in mine:
            cp.start()
        sends = []
        for a in range(n):
            sends.append(copy(a, 0, me, sibling, src=x_refs[a]))
            sends += [copy(a, 1 + j, me, (*chip, c), src=x_refs[a]) for j, chip in enumerate(chips)]
        for cp in sends:
            cp.start()
        for j, chip in enumerate(chips):
            for a in range(n):
                copy(a, 1 + j, (*chip, c), me).wait_recv()
                passed = copy(a, 4 + j, (*chip, c), sibling)
                passed.start()
                sends.append(passed)
        for a in range(n):
            copy(a, 0, sibling, me).wait_recv()
            for j, chip in enumerate(chips):
                copy(a, 4 + j, (*chip, 1 - c), me).wait_recv()
        for cp in sends:
            cp.wait_send()
        for cp in mine:
            cp.wait()

    hbm = pl.BlockSpec(memory_space=pl.ANY)
    return pl.pallas_call(
        body, name=name, out_shape=[jax.ShapeDtypeStruct((N_DEV,) + b.shape, b.dtype) for b in blocks],
        in_specs=[hbm] * n, out_specs=[hbm] * n,
        scratch_shapes=[pltpu.SemaphoreType.DMA((n, 7)), pltpu.SemaphoreType.DMA((n, 7)), pltpu.SemaphoreType.DMA((n,))],
    )(*blocks)


def _peers(x, y, c):
    return [(_flip(x, k & 4), _flip(y, k & 2), _flip(c, k & 1)) for k in range(1, N_DEV)]


def gather_start(name, blocks, lands):
    n = len(blocks)

    def body(*refs):
        x_refs, land_refs = refs[:n], refs[n:2 * n]
        send_sems, recv_sems = refs[2 * n], refs[2 * n + 1]
        token = refs[-1]
        x, y, c = _mesh_pos()
        me = 4 * x + 2 * y + c
        for k, peer in enumerate(_peers(x, y, c)):
            for a in range(n):
                pltpu.make_async_remote_copy(
                    src_ref=x_refs[a], dst_ref=land_refs[a].at[me], send_sem=send_sems.at[7 * a + k], recv_sem=recv_sems.at[7 * a + k],
                    device_id=peer, device_id_type=MESH).start()
        token[...] = jnp.zeros_like(token)

    hbm = pl.BlockSpec(memory_space=pltpu.HBM)
    sem = pl.BlockSpec(memory_space=pltpu.SEMAPHORE)
    out_shape = ([pltpu.SemaphoreType.DMA((7 * n,)), pltpu.SemaphoreType.DMA((7 * n,))]
                 + [pltpu.HBM(b.shape, b.dtype) for b in blocks] + [pltpu.HBM(l.shape, l.dtype) for l in lands]
                 + [jax.ShapeDtypeStruct((8, LANES), F32)])
    res = pl.pallas_call(
        body, name=name, out_shape=out_shape, in_specs=[hbm] * (2 * n),
        out_specs=[sem, sem] + [hbm] * (2 * n) + [pl.BlockSpec(memory_space=pltpu.VMEM)],
        input_output_aliases={a: 2 + a for a in range(2 * n)},
        compiler_params=pltpu.CompilerParams(has_side_effects=pltpu.SideEffectType.DATAFLOW_SIDE_EFFECTING),
    )(*[pltpu.with_memory_space_constraint(b, pltpu.HBM) for b in blocks],
      *[pltpu.with_memory_space_constraint(l, pltpu.HBM) for l in lands])
    return res[0], res[1], res[2:2 + n], res[2 + n:2 + 2 * n], res[-1]


def gather_wait(name, send_sems, recv_sems, blocks, lands, after):
    n = len(blocks)

    def body(*refs):
        x_refs, land_refs = refs[:n], refs[n:2 * n]
        s_sems, r_sems = refs[2 * n], refs[2 * n + 1]
        x, y, c = _mesh_pos()
        me = 4 * x + 2 * y + c
        for k, peer in enumerate(_peers(x, y, c)):
            for a in range(n):
                cp = pltpu.make_async_remote_copy(
                    src_ref=x_refs[a], dst_ref=land_refs[a].at[me], send_sem=s_sems.at[7 * a + k], recv_sem=r_sems.at[7 * a + k],
                    device_id=peer, device_id_type=MESH)
                cp.wait_send()
                cp.wait_recv()

    hbm = pl.BlockSpec(memory_space=pltpu.HBM)
    sem = pl.BlockSpec(memory_space=pltpu.SEMAPHORE)
    res = pl.pallas_call(
        body, name=name,
        out_shape=[pltpu.HBM(b.shape, b.dtype) for b in blocks] + [pltpu.HBM(l.shape, l.dtype) for l in lands],
        in_specs=[hbm] * (2 * n) + [sem, sem, pl.BlockSpec(memory_space=pl.ANY)], out_specs=[hbm] * (2 * n),
        input_output_aliases={a: a for a in range(2 * n)},
        compiler_params=pltpu.CompilerParams(has_side_effects=pltpu.SideEffectType.DATAFLOW_SIDE_EFFECTING),
    )(*blocks, *lands, send_sems, recv_sems, after)
    return res[n:]


def _split_exchange(name, sends, lands, sems, after):
    n = len(sends)
    starting = sems is None

    def body(*refs):
        s_refs, l_refs = refs[:n], refs[n:2 * n]
        send_sems, recv_sems = refs[2 * n], refs[2 * n + 1]
        x, y, c = _mesh_pos()
        me = 4 * x + 2 * y + c
        for k, (px, py, pc) in enumerate(_peers(x, y, c)):
            for a in range(n):
                cp = pltpu.make_async_remote_copy(
                    src_ref=s_refs[a].at[4 * px + 2 * py + pc], dst_ref=l_refs[a].at[me],
                    send_sem=send_sems.at[7 * a + k], recv_sem=recv_sems.at[7 * a + k],
                    device_id=(px, py, pc), device_id_type=MESH)
                if starting:
                    cp.start()
                else:
                    cp.wait_send()
                    cp.wait_recv()
        if starting:
            refs[-1][...] = jnp.zeros_like(refs[-1])

    hbm = pl.BlockSpec(memory_space=pltpu.HBM)
    sem = pl.BlockSpec(memory_space=pltpu.SEMAPHORE)
    thru = [pltpu.HBM(t.shape, t.dtype) for t in list(sends) + list(lands)]
    effect = pltpu.CompilerParams(has_side_effects=pltpu.SideEffectType.DATAFLOW_SIDE_EFFECTING)
    if starting:
        res = pl.pallas_call(
            body, name=name, in_specs=[hbm] * (2 * n),
            out_shape=[pltpu.SemaphoreType.DMA((7 * n,)), pltpu.SemaphoreType.DMA((7 * n,))] + thru + [jax.ShapeDtypeStruct((8, LANES), F32)],
            out_specs=[sem, sem] + [hbm] * (2 * n) + [pl.BlockSpec(memory_space=pltpu.VMEM)],
            input_output_aliases={a: 2 + a for a in range(2 * n)}, compiler_params=effect,
        )(*[pltpu.with_memory_space_constraint(t, pltpu.HBM) for t in list(sends) + list(lands)])
        return res[0], res[1], res[2:2 + n], res[2 + n:2 + 2 * n], res[-1]
    res = pl.pallas_call(
        body, name=name, out_shape=thru, in_specs=[hbm] * (2 * n) + [sem, sem, pl.BlockSpec(memory_space=pl.ANY)],
        out_specs=[hbm] * (2 * n), input_output_aliases={a: a for a in range(2 * n)}, compiler_params=effect,
    )(*sends, *lands, sems[0], sems[1], after)
    return res[n:]


def own_slot_only(send, me):
    mine = lax.dynamic_index_in_dim(send, me, 0, keepdims=False)
    return lax.dynamic_update_index_in_dim(lax.empty(send.shape, send.dtype), mine, me, 0)


def all_to_all(name, sends):
    n = len(sends)

    def body(*refs):
        s_refs, r_refs = refs[:n], refs[n:2 * n]
        send_sems, recv_sems, local_sems = refs[2 * n:]
        x, y, c = _mesh_pos()
        me = 4 * x + 2 * y + c
        mine = [pltpu.make_async_copy(s_refs[a].at[me], r_refs[a].at[me], local_sems.at[a]) for a in range(n)]
        for cp in mine:
            cp.start()
        copies = []
        for k in range(1, N_DEV):
            px, py, pc = _flip(x, k & 4), _flip(y, k & 2), _flip(c, k & 1)
            for a in range(n):
                copies.append(pltpu.make_async_remote_copy(
                    src_ref=s_refs[a].at[4 * px + 2 * py + pc], dst_ref=r_refs[a].at[me],
                    send_sem=send_sems.at[a, k - 1], recv_sem=recv_sems.at[a, k - 1],
                    device_id=(px, py, pc), device_id_type=MESH))
        for cp in copies:
            cp.start()
        for cp in copies:
            cp.wait_recv()
        for cp in copies:
            cp.wait_send()
        for cp in mine:
            cp.wait()

    hbm = pl.BlockSpec(memory_space=pl.ANY)
    return pl.pallas_call(
        body, name=name, out_shape=[jax.ShapeDtypeStruct(s.shape, s.dtype) for s in sends],
        in_specs=[hbm] * n, out_specs=[hbm] * n,
        scratch_shapes=[pltpu.SemaphoreType.DMA((n, 7)), pltpu.SemaphoreType.DMA((n, 7)), pltpu.SemaphoreType.DMA((n,))],
    )(*sends)


def _row_tile(r, cap, step):
    return next((t for t in range(cap, step - 1, -step) if r % t == 0), r)


def _sum_parts(p, n):
    t = [p[k].astype(F32) for k in range(n)]
    while len(t) > 1:
        t = [t[k] + t[k + 1] for k in range(0, len(t), 2)]
    return t[0]


def _adam(g, w, m, v):
    m = ADAM_B1 * m + (1.0 - ADAM_B1) * g
    v = ADAM_B2 * v + (1.0 - ADAM_B2) * (g * g)
    m_hat = m / (1.0 - ADAM_B1 ** ADAM_STEP)
    v_hat = v / (1.0 - ADAM_B2 ** ADAM_STEP)
    return -ADAM_LR * (m_hat / (jnp.sqrt(v_hat) + ADAM_EPS) + ADAM_WD * w), m, v


def adam_tiled(name, partials, w, m_, v_, layer=0, prev=None):
    _, r, c = w.shape
    n_part = partials.shape[0]
    tr = _row_tile(r, 256, 16)

    def body(*refs):
        p_ref, w_ref, m_ref, v_ref = refs[:4]
        g_ref, d_ref, nm_ref, nv_ref = refs[-4:]
        g = _sum_parts(p_ref, n_part)
        g_ref[...] = g
        d_ref[...], nm_ref[...], nv_ref[...] = _adam(g, w_ref[...], m_ref[...], v_ref[...])

    spec = pl.BlockSpec((None, tr, c), lambda i: (layer, i, 0))
    in_specs = [pl.BlockSpec((n_part, None, tr, c), lambda i: (0, 0, i, 0)), spec, spec, spec]
    args = [partials, w, m_, v_]
    aliases = {}
    if prev is not None:
        in_specs += [pl.BlockSpec(memory_space=pl.ANY)] * 4
        args += list(prev)
        aliases = {4 + k: k for k in range(4)}
    return pl.pallas_call(
        body, grid=(r // tr,), name=name, in_specs=in_specs,
        out_specs=[spec] * 4, out_shape=[jax.ShapeDtypeStruct(w.shape, F32)] * 4,
        input_output_aliases=aliases, compiler_params=_params(("parallel",)),
    )(*args)


def adam_small(name, items, extra):
    n, ne = len(items), len(extra)

    def body(*refs):
        ins, outs = refs[:4 * n + ne], refs[4 * n + ne:]
        for a in range(n):
            p_ref, w_ref, m_ref, v_ref = ins[4 * a:4 * a + 4]
            g = _sum_parts(p_ref, N_DEV)
            outs[4 * a][...] = g
            outs[4 * a + 1][...], outs[4 * a + 2][...], outs[4 * a + 3][...] = _adam(g, w_ref[...], m_ref[...], v_ref[...])
        for e in range(ne):
            outs[4 * n + e][...] = _sum_parts(ins[4 * n + e], N_DEV)

    args, out_shape = [], []
    for p, w, m_, v_ in items:
        args += [p, w, m_, v_]
        out_shape += [jax.ShapeDtypeStruct(w.shape, F32)] * 4
    for e in extra:
        args.append(e)
        out_shape.append(jax.ShapeDtypeStruct(e.shape[1:], F32))
    vmem = pl.BlockSpec(memory_space=pltpu.VMEM)
    res = pl.pallas_call(body, name=name, in_specs=[vmem] * len(args), out_specs=[vmem] * len(out_shape), out_shape=out_shape)(*args)
    return [res[4 * a:4 * a + 4] for a in range(n)], res[4 * n:]


def _cols_from_gather(g):
    g = jnp.moveaxis(g, 0, -2)
    return g.reshape(g.shape[:-2] + (g.shape[-2] * g.shape[-1],))


def _cols_to_blocks(w):
    w = w.reshape(w.shape[:-1] + (N_DEV, w.shape[-1] // N_DEV))
    return jnp.moveaxis(w, -2, 0)


def _block_diag(w):
    z = jnp.zeros((B_BLOCK_DIM, B_BLOCK_DIM), w.dtype)
    rows = []
    for j in range(B_BLOCKS // 2):
        top = jnp.concatenate([w[2 * j], z], axis=1)
        bot = jnp.concatenate([z, w[2 * j + 1]], axis=1)
        rows.append(jnp.concatenate([top, bot], axis=0))
    return jnp.concatenate(rows, axis=0)


def _block_diag_grad(d):
    out = []
    for j in range(B_BLOCKS // 2):
        blk = d[128 * j:128 * (j + 1)]
        out.append(blk[:64, :64])
        out.append(blk[64:, 64:])
    return jnp.stack(out)


NAMES = ("norm_gains", "even_w_in", "hgrn_lb_logits", "hgrn_norm", "rg_conv_w", "rg_conv_b", "rg_wa", "rg_ba", "rg_wx", "rg_bx",
         "rg_lambda", "even_w_out", "odd_w_in", "fox_f_bias", "odd_w_out", "ffn_w_up", "ffn_conv_w", "ffn_conv_b", "ffn_w_down")
SMALL_SHARDED = ("norm_gains", "rg_conv_w", "ffn_conv_w")
REPLICATED = ("hgrn_lb_logits", "hgrn_norm", "rg_conv_b", "rg_wa", "rg_ba", "rg_wx", "rg_bx", "rg_lambda", "fox_f_bias", "ffn_conv_b")


def _ffn_forward(tag, layer, h, w_up_g, cw5, cb5, w_down_g, m, seq):
    tm = _div_tile(m, 1024)
    nm = m // tm
    hid = mm(f"{tag}_up", "nn",
             Blk(h, (tm, D_MODEL), lambda i, j, k: (i, 0)),
             Blk(w_up_g, (None, None, D_MODEL, FF_BLK), lambda i, j, k: (j, 0, 0, 0)),
             Blk((N_DEV, m, FF_BLK), (None, tm, FF_BLK), lambda i, j, k: (j, i, 0)), F32, (nm, N_DEV, 1))
    hid = hid.reshape(2, N_DEV // 2, m, FF_BLK)
    act, conv = ffn_mid_fwd(f"{tag}_mid", hid, cw5, cb5, layer, m=m, seq=seq)
    f = mm(f"{tag}_down", "nn",
           Blk(act, (None, tm, FF_BLK), lambda i, j, k: (k, i, 0)),
           Blk(w_down_g, (2, None, FF_BLK // 2, D_MODEL), lambda i, j, k: (k, 0, 0, 0)),
           Blk((m, D_MODEL), (tm, D_MODEL), lambda i, j, k: (i, 0)), F32, (nm, 1, N_DEV // 2))
    return (hid, conv), act, f


def _ffn_backward(tag, layer, df, h, hid, act, w_up_g, cw5, cb5, w_down_g, m, seq):
    tm = _div_tile(m, 1024)
    nm = m // tm
    dact = mm(f"{tag}_dact", "nt",
              Blk(df, (tm, D_MODEL), lambda i, j, k: (i, 0)),
              Blk(w_down_g, (2, None, FF_BLK // 2, D_MODEL), lambda i, j, k: (j, 0, 0, 0)),
              Blk((N_DEV // 2, m, FF_BLK), (None, tm, FF_BLK), lambda i, j, k: (j, i, 0)), BF16, (nm, N_DEV // 2, 1))
    d_wdown = mm(f"{tag}_dwdown", "tn",
                 Blk(act, (None, tm, FF_BLK), lambda i, j, k: (i, k, 0)),
                 Blk(df, (tm, D_MODEL), lambda i, j, k: (k, 0)),
                 Blk(w_down_g.shape, (2, None, FF_BLK // 2, D_MODEL), lambda i, j, k: (i, 0, 0, 0)), BF16,
                 (N_DEV // 2, 1, nm))
    dhid, d_cw, d_cb = ffn_mid_bwd(f"{tag}_dmid", hid[0], hid[1], cw5, dact, layer, m=m, seq=seq)
    dhid = dhid.reshape(N_DEV, m, FF_BLK)
    dh = mm(f"{tag}_dh", "nt",
            Blk(dhid, (None, tm, FF_BLK), lambda i, j, k: (k, i, 0)),
            Blk(w_up_g, (None, None, D_MODEL, FF_BLK), lambda i, j, k: (k, 0, 0, 0)),
            Blk((m, D_MODEL), (tm, D_MODEL), lambda i, j, k: (i, 0)), BF16, (nm, 1, N_DEV))
    d_wup = mm(f"{tag}_dwup", "tn",
               Blk(h, (tm, D_MODEL), lambda i, j, k: (k, 0)),
               Blk(dhid, (None, tm, FF_BLK), lambda i, j, k: (j, k, 0)),
               Blk(w_up_g.shape, (None, None, D_MODEL, FF_BLK), lambda i, j, k: (j, 0, 0, 0)), BF16,
               (1, N_DEV, nm))
    return dh, d_wup, d_cw, d_cb, d_wdown


def kernel(x, norm_gains, even_w_in, hgrn_lb_logits, hgrn_norm, rg_conv_w, rg_conv_b, rg_wa, rg_ba, rg_wx, rg_bx, rg_lambda, even_w_out, odd_w_in, fox_f_bias, odd_w_out, ffn_w_up, ffn_conv_w, ffn_conv_b, ffn_w_down, loss_target, m_norm_gains, m_even_w_in, m_hgrn_lb_logits, m_hgrn_norm, m_rg_conv_w, m_rg_conv_b, m_rg_wa, m_rg_ba, m_rg_wx, m_rg_bx, m_rg_lambda, m_even_w_out, m_odd_w_in, m_fox_f_bias, m_odd_w_out, m_ffn_w_up, m_ffn_conv_w, m_ffn_conv_b, m_ffn_w_down, v_norm_gains, v_even_w_in, v_hgrn_lb_logits, v_hgrn_norm, v_rg_conv_w, v_rg_conv_b, v_rg_wa, v_rg_ba, v_rg_wx, v_rg_bx, v_rg_lambda, v_even_w_out, v_odd_w_in, v_fox_f_bias, v_odd_w_out, v_ffn_w_up, v_ffn_conv_w, v_ffn_conv_b, v_ffn_w_down):
    local = dict(locals())
    w = {n: local[n] for n in NAMES}
    mom = {n: local["m_" + n] for n in NAMES}
    var = {n: local["v_" + n] for n in NAMES}
    n_batch, seq, _ = x.shape
    m = n_batch * seq
    tm = _div_tile(m, 512)
    tmm = _div_tile(m, 1024)
    nm = m // tmm

    gathered = all_gather("gather_weights", [w["even_w_in"].astype(BF16)] + [w[n] for n in SMALL_SHARDED])
    g = dict(zip(("even_w_in",) + SMALL_SHARDED, gathered))
    w_in_e = g["even_w_in"]
    gains = _cols_from_gather(g["norm_gains"])
    me = 4 * lax.axis_index("x") + 2 * lax.axis_index("y") + lax.axis_index("c")
    own_block_only = lambda t: lax.dynamic_update_index_in_dim(lax.empty((N_DEV,) + t.shape, t.dtype), t, me, 0)
    behind = (g["norm_gains"][0, 0, 0, 0] * 0.0).astype(BF16)
    out0 = [w["even_w_out"].astype(BF16) + behind]
    out0_sent = gather_start("gather_out0_start", out0, [own_block_only(t) for t in out0])
    behind = (out0_sent[4][0, 0] * 0.0).astype(BF16)
    ffn0 = [w["ffn_w_up"][0:1].astype(BF16) + behind, w["ffn_w_down"][0:1].astype(BF16) + behind]
    ffn0_sent = gather_start("gather_ffn0_start", ffn0, [own_block_only(t) for t in ffn0])
    behind = (ffn0_sent[4][0, 0] * 0.0).astype(BF16)
    mix1w = [w["odd_w_in"].astype(BF16) + behind, w["odd_w_out"].astype(BF16) + behind]
    mix1_sent = gather_start("gather_mix1_start", mix1w, [own_block_only(t) for t in mix1w])
    behind = (mix1_sent[4][0, 0] * 0.0).astype(BF16)
    ffn1 = [w["ffn_w_up"][1:2].astype(BF16) + behind, w["ffn_w_down"][1:2].astype(BF16) + behind]
    ffn1_sent = gather_start("gather_ffn1_start", ffn1, [own_block_only(t) for t in ffn1])
    started = ffn1_sent[4]
    rg_cw = _cols_from_gather(g["rg_conv_w"])[0]
    n_layer = ffn_conv_w.shape[0]
    cw5 = g["ffn_conv_w"].reshape(2, N_DEV // 2, n_layer, FFN_CONV, FF_BLK)
    cb5 = ffn_conv_b.reshape(n_layer, 2, N_DEV // 2, 1, FF_BLK)
    gain = lambda l, k: gains[l, k:k + 1, :]
    wa_bd, wx_bd = _block_diag(rg_wa[0]), _block_diag(rg_wx[0])
    fbias = jnp.pad(fox_f_bias, ((0, 0), (0, LANES - C_HEADS)))

    x0 = x.reshape(m, D_MODEL)
    tgt = loss_target.reshape(m, D_MODEL)

    (h0,) = tile_fwd("l0_prenorm", fn_prenorm_after, m=m, tm=tm, nj=1, rows=[Row(x0)], pars=[Par(gain(0, 0)), Par(started)],
                     outs=[Out(D_MODEL, BF16)])
    z0 = mm("l0_in", "nn",
            Blk(h0, (tmm, D_MODEL), lambda i, j, k: (i, 0)),
            Blk(w_in_e, (2, None, D_MODEL, 384), lambda i, j, k: (j, 0, 0, 0)),
            Blk((m, 3072), (tmm, 768), lambda i, j, k: (i, j)), F32, (nm, N_DEV // 2, 1), b_join=True)
    oa, sprev = hgrn_fwd("l0_hgrn", z0, hgrn_lb_logits, hgrn_norm, n_batch=n_batch, seq=seq)
    rg_rows = lambda: [Row(z0, LANES, 16), Row(z0, LANES, 20)]
    rg_pars = lambda: [Par(rg_cw, "col", LANES), Par(rg_conv_b, "col", LANES), Par(wa_bd, "row", LANES), Par(rg_ba, "col", LANES),
                       Par(wx_bd, "row", LANES), Par(rg_bx, "col", LANES), Par(rg_lambda, "col", LANES)]
    (ob,) = tile_fwd("l0_rglru", fn_rglru, m=m, tm=seq, nj=B_WIDTH // LANES, rows=rg_rows(), pars=rg_pars(),
                     outs=[Out(B_WIDTH, BF16, LANES)])
    mixcat0 = jnp.concatenate([oa, ob], axis=-1)
    (g_out_e,) = gather_wait("gather_out0_wait", out0_sent[0], out0_sent[1], out0_sent[2], out0_sent[3], mixcat0)
    w_out_e = g_out_e.reshape(D_MODEL, D_MODEL)
    mix0 = mm2d("l0_out", "nn", mixcat0, w_out_e)
    x1, h1 = tile_fwd("l0_postnorm", fn_addnorm2, m=m, tm=tm, nj=1, rows=[Row(x0), Row(mix0)], pars=[Par(gain(0, 1)), Par(gain(0, 2))],
                      outs=[Out(D_MODEL, F32), Out(D_MODEL, BF16)])
    w_up_g0, w_down_g0 = gather_wait("gather_ffn0_wait", ffn0_sent[0], ffn0_sent[1], ffn0_sent[2], ffn0_sent[3], h1)
    hid0, act0, f0 = _ffn_forward("l0_ffn", 0, h1, w_up_g0, cw5, cb5, w_down_g0, m, seq)
    x2, h2 = tile_fwd("l0_ffnnorm", fn_addnorm2, m=m, tm=tm, nj=1, rows=[Row(x1), Row(f0)], pars=[Par(gain(0, 3)), Par(gain(1, 0))],
                      outs=[Out(D_MODEL, F32), Out(D_MODEL, BF16)])

    g_in_o, g_out_o = gather_wait("gather_mix1_wait", mix1_sent[0], mix1_sent[1], mix1_sent[2], mix1_sent[3], h2)
    w_in_o = jnp.pad(_cols_from_gather(g_in_o)[0], ((0, 0), (0, 3200 - 3088)))
    w_out_o = g_out_o.reshape(D_MODEL, D_MODEL)
    z1 = mm2d("l1_in", "nn", h2, w_in_o)
    (cgate,) = tile_fwd("l1_gate", fn_fox_gate, m=m, tm=seq, nj=1, rows=[Row(z1, LANES, 3072 // LANES)], pars=[Par(fbias)],
                        outs=[Out(LANES, F32)])
    place, ones_q, ones_k = term_placement()
    qterm, kterm = tile_fwd("l1_terms", fn_fox_terms, m=m, tm=tm, nj=1, rows=[Row(cgate)],
                            pars=[Par(place), Par(ones_q), Par(ones_k)], outs=[Out(TERM_W, BF16), Out(TERM_W, BF16)])
    oc, lse = fox_pair_fwd("l1_attn", z1, qterm, kterm, n_batch=n_batch, seq=seq)
    mix1 = mm2d("l1_out", "nn", oc, w_out_o)
    x3, h3 = tile_fwd("l1_postnorm", fn_addnorm2, m=m, tm=tm, nj=1, rows=[Row(x2), Row(mix1)], pars=[Par(gain(1, 1)), Par(gain(1, 2))],
                      outs=[Out(D_MODEL, F32), Out(D_MODEL, BF16)])
    w_up_g1, w_down_g1 = gather_wait("gather_ffn1_wait", ffn1_sent[0], ffn1_sent[1], ffn1_sent[2], ffn1_sent[3], h3)
    hid1, act1, f1 = _ffn_forward("l1_ffn", 1, h3, w_up_g1, cw5, cb5, w_down_g1, m, seq)
    dy, df1, loss_part, d_g13 = loss_head("loss", x3, f1, tgt, gain(1, 3), m=m, tm=tm)
    dh3, d_wup1, d_cw1, d_cb1, d_wdown1 = _ffn_backward("l1_ffn", 1, df1, h3, hid1, act1, w_up_g1, cw5, cb5, w_down_g1, m, seq)
    dx2, dmix1, d_g11, d_g12 = tile_bwd("l1_dpostnorm", fn_addnorm2, m=m, tm=tm, nj=1, rows=[Row(x2), Row(mix1)],
                                        pars=[Par(gain(1, 1)), Par(gain(1, 2))], cts=[Row(dy), Row(dh3)],
                                        drows=[Out(D_MODEL, F32), Out(D_MODEL, BF16)])
    doc = mm2d("l1_doc", "nt", dmix1, w_out_o, BF16)
    d_wout_o = mm2d("l1_dwout", "tn", oc, dmix1)
    dq, dk, dv, dc = fox_pair_bwd("l1_dattn", z1, qterm, kterm, oc, doc, lse, n_batch=n_batch, seq=seq)
    dzf, d_fbias = tile_bwd("l1_dgate", fn_fox_gate, m=m, tm=seq, nj=1, rows=[Row(z1, LANES, 3072 // LANES)], pars=[Par(fbias)],
                            cts=[Row(dc)], drows=[Out(LANES, BF16)])
    dz1 = jnp.concatenate([dq, dk, dv, dzf], axis=-1)
    dh2 = mm2d("l1_dh", "nt", dz1, w_in_o, BF16)
    d_win_o = mm2d("l1_dwin", "tn", h2, dz1)

    send1 = [_cols_to_blocks(d_win_o[None, :, :3088]).astype(BF16),
             d_wout_o.reshape(N_DEV, 1, D_MODEL // N_DEV, D_MODEL).astype(BF16), d_wup1, d_wdown1]
    sent1 = _split_exchange("exchange_l1_start", send1, [own_slot_only(t, me) for t in send1], None, None)

    dx1, df0, d_g03, d_g10 = tile_bwd("l0_dffnnorm", fn_addnorm2_after, m=m, tm=tm, nj=1, rows=[Row(x1), Row(f0)],
                                      pars=[Par(gain(0, 3)), Par(gain(1, 0)), Par(sent1[4])], cts=[Row(dx2), Row(dh2)],
                                      drows=[Out(D_MODEL, F32), Out(D_MODEL, BF16)])[:4]
    dh1, d_wup0, d_cw0, d_cb0, d_wdown0 = _ffn_backward("l0_ffn", 0, df0, h1, hid0, act0, w_up_g0, cw5, cb5, w_down_g0, m, seq)
    send0 = [d_wup0, d_wdown0]
    sent0 = _split_exchange("exchange_ffn0_start", send0, [own_slot_only(t, me) for t in send0], None, None)
    dx0a, dmix0, d_g01, d_g02 = tile_bwd("l0_dpostnorm", fn_addnorm2_after, m=m, tm=tm, nj=1, rows=[Row(x0), Row(mix0)],
                                         pars=[Par(gain(0, 1)), Par(gain(0, 2)), Par(sent0[4])], cts=[Row(dx1), Row(dh1)],
                                         drows=[Out(D_MODEL, F32), Out(D_MODEL, BF16)])[:4]
    dmixcat0 = mm2d("l0_dmixcat", "nt", dmix0, w_out_e, BF16)
    d_wout_e = mm2d("l0_dwout", "tn", mixcat0, dmix0)
    dzq, dzf0, dzv, dzg, d_lb, d_hnorm = hgrn_bwd("l0_dhgrn", z0, sprev, hgrn_lb_logits, hgrn_norm, dmixcat0, n_batch=n_batch, seq=seq)
    dzx, dzy, d_rcw, d_rcb, d_wa, d_ba, d_wx, d_bx, d_lam = tile_bwd(
        "l0_drglru", fn_rglru, m=m, tm=seq, nj=B_WIDTH // LANES, rows=rg_rows(), pars=rg_pars(),
        cts=[Row(dmixcat0, LANES, A_WIDTH // LANES)], drows=[Out(B_WIDTH, BF16, LANES), Out(B_WIDTH, BF16, LANES)])
    dz0 = jnp.concatenate([dzq, dzf0, dzv, dzg, dzx, dzy], axis=-1)
    d_win_e = mm("l0_dwin", "tn",
                 Blk(h0, (tmm, D_MODEL), lambda i, j, k: (k, 0)),
                 Blk(dz0, (tmm, 768), lambda i, j, k: (k, j)),
                 Blk(w_in_e.shape, (2, None, D_MODEL, 384), lambda i, j, k: (j, 0, 0, 0)), BF16, (1, N_DEV // 2, nm), o_split=True)
    send_e = [d_win_e, d_wout_e.reshape(N_DEV, 1, D_MODEL // N_DEV, D_MODEL).astype(BF16)]
    sent_e = _split_exchange("exchange_even_start", send_e, [own_slot_only(t, me) for t in send_e], None, None)
    d_ffn_cb = jnp.stack([d_cb0, d_cb1]).reshape(n_layer, 2 * D_FF)
    rep = {"hgrn_lb_logits": d_lb, "hgrn_norm": d_hnorm, "rg_conv_b": d_rcb, "rg_wa": _block_diag_grad(d_wa)[None], "rg_ba": d_ba,
           "rg_wx": _block_diag_grad(d_wx)[None], "rg_bx": d_bx, "rg_lambda": d_lam, "fox_f_bias": d_fbias[:, :C_HEADS],
           "ffn_conv_b": d_ffn_cb}
    rep_blocks = [rep[n] for n in REPLICATED] + [loss_part]
    rep_sent = gather_start("gather_partials_start", rep_blocks, [own_block_only(t) for t in rep_blocks])
    dh0 = mm("l0_dh", "nt",
             Blk(dz0, (tmm, 768), lambda i, j, k: (i, k)),
             Blk(w_in_e, (2, None, D_MODEL, 384), lambda i, j, k: (k, 0, 0, 0)),
             Blk((m, D_MODEL), (tmm, D_MODEL), lambda i, j, k: (i, 0)), BF16, (nm, 1, N_DEV // 2), after=sent_e[4] + rep_sent[4],
             b_join=True)
    dx0, d_g00 = tile_bwd("l0_dprenorm", fn_input_norm, m=m, tm=tm, nj=1, rows=[Row(x0)], pars=[Par(gain(0, 0))],
                          cts=[Row(dx0a), Row(dh0)], drows=[Out(D_MODEL, F32)])

    d_gains = jnp.stack([jnp.concatenate([d_g00, d_g01, d_g02, d_g03], axis=0), jnp.concatenate([d_g10, d_g11, d_g12, d_g13], axis=0)])
    d_ffn_cw = jnp.stack([d_cw0, d_cw1], axis=2).reshape(N_DEV, n_layer, FFN_CONV, FF_BLK)
    r_in_o, r_out_o, r_up1, r_down1 = _split_exchange("exchange_l1_wait", sent1[2], sent1[3], sent1[:2], dx0)
    r_up0, r_down0 = _split_exchange("exchange_ffn0_wait", sent0[2], sent0[3], sent0[:2], dx0)
    r_in_e, r_out_e = _split_exchange("exchange_even_wait", sent_e[2], sent_e[3], sent_e[:2], dx0)
    recv, res = {}, {}
    for n, r in (("even_w_in", r_in_e), ("even_w_out", r_out_e), ("odd_w_in", r_in_o), ("odd_w_out", r_out_o)):
        res[n] = adam_tiled("adam_" + n, r, w[n], mom[n], var[n])
    for n, parts_l in (("ffn_w_up", (r_up0, r_up1)), ("ffn_w_down", (r_down0, r_down1))):
        first_layer = adam_tiled(f"adam_{n}_0", parts_l[0], w[n], mom[n], var[n], layer=0)
        res[n] = adam_tiled(f"adam_{n}_1", parts_l[1], w[n], mom[n], var[n], layer=1, prev=first_layer)
    small_send = [_cols_to_blocks(d_gains), _cols_to_blocks(d_rcw[None]), d_ffn_cw]
    recv.update(zip(SMALL_SHARDED, all_to_all("exchange_small", small_send)))

    parts = gather_wait("gather_partials_wait", rep_sent[0], rep_sent[1], rep_sent[2], rep_sent[3], dx0)
    for n, p in zip(REPLICATED, parts):
        recv[n] = p
    small = SMALL_SHARDED + REPLICATED
    small_res, (loss_sum,) = adam_small("adam_small", [(recv[n], w[n], mom[n], var[n]) for n in small], [parts[-1]])
    res.update(dict(zip(small, small_res)))

    out = [loss_sum[0, 0], dx0.reshape(x.shape)]
    for k in range(4):
        out += [res[n][k] for n in NAMES]
    return tuple(out)
```

```python
import functools

import jax
import jax.numpy as jnp
from jax import lax
from jax.experimental import pallas as pl
from jax.experimental.pallas import tpu as pltpu

F32 = jnp.float32
BF16 = jnp.bfloat16

D_MODEL = 1024
A_HEADS = 4
A_WIDTH = 512
HGRN_CHUNK = 64
HGRN_SEG = 512
B_WIDTH = 512
B_BLOCKS = 8
B_BLOCK_DIM = 64
B_CONV = 4
RG_C = 8.0
C_HEADS = 16
C_HEAD_DIM = 64
D_FF = 2816
FFN_CONV = 3
EPS = 1e-6
LANES = 128
HALO = 16
N_DEV = 8
FF_BLK = 2 * D_FF // N_DEV
MESH = pl.DeviceIdType.MESH
NEG = -1e30
VMEM_LIMIT = 56 * 1024 * 1024

ADAM_LR = 0.001
ADAM_B1 = 0.9
ADAM_B2 = 0.999
ADAM_EPS = 1e-08
ADAM_WD = 0.01
ADAM_STEP = 10


def _dg(a, b, pat):
    nb = a.ndim - 2
    batch = (tuple(range(nb)), tuple(range(nb)))
    ca = a.ndim - 1 if pat[0] == "n" else a.ndim - 2
    cb = b.ndim - 2 if pat[1] == "n" else b.ndim - 1
    return lax.dot_general(a.astype(BF16), b.astype(BF16), (((ca,), (cb,)), batch), preferred_element_type=F32)


@functools.partial(jax.custom_vjp, nondiff_argnums=(2,))
def bdot(a, b, pat):
    return _dg(a, b, pat)


def _bdot_fwd(a, b, pat):
    return _dg(a, b, pat), (a, b)


def _bdot_bwd(pat, res, g):
    a, b = res
    if pat == "nn":
        return _dg(g, b, "nt"), _dg(a, g, "tn")
    if pat == "nt":
        return _dg(g, b, "nn"), _dg(g, a, "tn")
    return _dg(b, g, "nt"), _dg(a, g, "nn")


bdot.defvjp(_bdot_fwd, _bdot_bwd)


def _shift_raw(x, s, up, fill):
    if s == 0:
        return x
    n = x.shape[0]
    r = pltpu.roll(x, (n - s) if up else s, 0)
    idx = lax.broadcasted_iota(jnp.int32, x.shape, 0)
    mask = (idx >= n - s) if up else (idx < s)
    return jnp.where(mask, jnp.asarray(fill, x.dtype), r)


@functools.partial(jax.custom_vjp, nondiff_argnums=(1,))
def shift_down(x, s):
    return _shift_raw(x, s, False, 0.0)


def _shift_down_fwd(x, s):
    return _shift_raw(x, s, False, 0.0), None


def _shift_down_bwd(s, _, g):
    return (_shift_raw(g, s, True, 0.0),)


shift_down.defvjp(_shift_down_fwd, _shift_down_bwd)


def _scan_impl(a, u, up):
    n = a.shape[0]
    s = 1
    while s < n:
        u = a * _shift_raw(u, s, up, 0.0) + u
        if 2 * s < n:
            a = a * _shift_raw(a, s, up, 1.0)
        s *= 2
    return u


@jax.custom_vjp
def lin_scan(a, u):
    return _scan_impl(a, u, False)


def _lin_scan_fwd(a, u):
    h = _scan_impl(a, u, False)
    return h, (a, h)


def _lin_scan_bwd(res, g):
    a, h = res
    gh = _scan_impl(_shift_raw(a, 1, True, 0.0), g, True)
    return gh * _shift_raw(h, 1, False, 0.0), gh


lin_scan.defvjp(_lin_scan_fwd, _lin_scan_bwd)


def _cumsum_impl(x, up, period):
    n = x.shape[0]
    span = n if period is None else period
    idx = lax.broadcasted_iota(jnp.int32, x.shape, 0)
    pos = idx if period is None else idx % period
    s = 1
    while s < span:
        sh = _shift_raw(x, s, up, 0.0)
        if period is not None:
            keep = (pos < period - s) if up else (pos >= s)
            sh = jnp.where(keep, sh, 0.0)
        x = x + sh
        s *= 2
    return x


@functools.partial(jax.custom_vjp, nondiff_argnums=(1,))
def cumsum_rows(x, period):
    return _cumsum_impl(x, False, period)


def _cumsum_fwd(x, period):
    return _cumsum_impl(x, False, period), None


def _cumsum_bwd(period, _, g):
    return (_cumsum_impl(g, True, period),)


cumsum_rows.defvjp(_cumsum_fwd, _cumsum_bwd)


def _sigmoid(x):
    return jax.nn.sigmoid(x)


def _expm1(x):
    return jnp.tanh(0.5 * x) * (jnp.exp(x) + 1.0)


def _softplus(x):
    return jnp.maximum(x, 0.0) + jnp.log(1.0 + jnp.exp(-jnp.abs(x)))


def _rms(x, g):
    return x * lax.rsqrt(jnp.mean(x * x, axis=-1, keepdims=True) + EPS) * g


def fn_prenorm(x, g):
    return (_rms(x, g).astype(BF16),)


def fn_prenorm_after(x, g, _token):
    return fn_prenorm(x, g)


def fn_addnorm2(x, y, g_post, g_pre):
    x1 = x + _rms(y, g_post)
    return x1, _rms(x1, g_pre).astype(BF16)


def fn_addnorm2_after(x, y, g_post, g_pre, _token):
    return fn_addnorm2(x, y, g_post, g_pre)


def fn_input_norm(x, g):
    return x, _rms(x, g).astype(BF16)


def _causal_conv(x, w, b, taps):
    c = b
    for k in range(taps):
        c = c + w[k:k + 1, :] * shift_down(x, taps - 1 - k)
    return c


def fn_rglru(xb, yb, cw, cb, wa, ba, wx, bx, lam):
    xf = _causal_conv(xb, cw, cb, B_CONV)
    r = _sigmoid(bdot(xf, wa, "nn") + ba)
    i = _sigmoid(bdot(xf, wx, "nn") + bx)
    log_a = -RG_C * r * _softplus(-lam)
    a = jnp.exp(log_a)
    u = jnp.sqrt(-_expm1(2.0 * log_a)) * (i * xf)
    h = lin_scan(a, u)
    return ((h * jax.nn.gelu(yb)).astype(BF16),)


def fn_fox_gate(zf, bias):
    return (cumsum_rows(jax.nn.log_sigmoid(zf + bias), None),)


def fn_hgrn_seg(q, fl, v, g, st, logits, hn):
    rows = q.shape[0]
    nc = rows // HGRN_CHUNK
    l0, l1, l2 = logits[0:1, :], logits[1:2, :], logits[2:3, :]
    mx = jnp.maximum(jnp.maximum(l0, l1), l2)
    e0, e1, e2 = jnp.exp(l0 - mx), jnp.exp(l1 - mx), jnp.exp(l2 - mx)
    lb = e0 / (e0 + e1 + e2)
    forget = lb + (1.0 - lb) * _sigmoid(fl)
    qs = q * _sigmoid(q)
    kk = 1.0 - forget
    logf = jnp.log(forget)
    bcum = cumsum_rows(logf, HGRN_CHUNK)
    c3 = lambda t: t.reshape(nc, HGRN_CHUNK, 128)
    b_last = jnp.sum(c3(logf), axis=1, keepdims=True)
    bcum3 = c3(bcum)
    q_dec = c3(qs) * jnp.exp(bcum3)
    k_dec = c3(kk) * jnp.exp(-bcum3)
    k_upd = c3(kk) * jnp.exp(b_last - bcum3)
    v3 = c3(v)
    scores = bdot(q_dec, k_dec, "nt")
    ri = lax.broadcasted_iota(jnp.int32, scores.shape, 1)
    ci = lax.broadcasted_iota(jnp.int32, scores.shape, 2)
    scores = jnp.where(ri >= ci, scores, 0.0)
    o = bdot(scores, v3, "nn")
    upd_t = bdot(v3, k_upd, "tn")
    dec = jnp.exp(b_last)
    prev = []
    for n in range(nc):
        prev.append(st)
        st = st * dec[n] + upd_t[n]
    o = o + bdot(q_dec, jnp.stack(prev), "nt")
    o = o.reshape(rows, 128)
    o = o * lax.rsqrt(jnp.mean(o * o, axis=-1, keepdims=True) + EPS) * hn
    return (o * _sigmoid(g)).astype(BF16), st


def _ffn_conv(xg, xv, cw, cb):
    cg = _causal_conv(xg, cw[0], cb[0], FFN_CONV)[HALO:]
    cv = _causal_conv(xv, cw[1], cb[1], FFN_CONV)[HALO:]
    return cg, cv


def _ffn_gate(cg, cv):
    return jax.nn.gelu(cg) * cv


class Row:
    def __init__(self, arr, cb=None, off=0):
        self.arr, self.cb, self.off = arr, cb, off

    def spec(self, tm):
        if self.cb is None:
            return pl.BlockSpec((tm, self.arr.shape[1]), lambda j, i: (i, 0))
        off = self.off
        return pl.BlockSpec((tm, self.cb), lambda j, i: (i, j + off))


class Par:
    def __init__(self, arr, kind="full", bs=None):
        self.arr, self.kind, self.bs = arr, kind, bs

    def block(self):
        if self.kind == "full":
            return self.arr.shape
        if self.kind == "col":
            return (self.arr.shape[0], self.bs)
        return (self.bs, self.arr.shape[1])

    def spec(self):
        if self.kind == "full":
            return pl.BlockSpec(self.block(), lambda j, i: (0, 0))
        if self.kind == "col":
            return pl.BlockSpec(self.block(), lambda j, i: (0, j))
        return pl.BlockSpec(self.block(), lambda j, i: (j, 0))


class Out:
    def __init__(self, width, dtype, cb=None, off=0):
        self.width, self.dtype, self.cb, self.off = width, dtype, cb, off

    def spec(self, tm):
        if self.cb is None:
            return pl.BlockSpec((tm, self.width), lambda j, i: (i, 0))
        off = self.off
        return pl.BlockSpec((tm, self.cb), lambda j, i: (i, j + off))


def _params(sem):
    return pltpu.CompilerParams(dimension_semantics=sem, vmem_limit_bytes=VMEM_LIMIT)


def tile_fwd(name, fn, *, m, tm, nj, rows, pars, outs, n_acc=0):
    n_r, n_p, n_o = len(rows), len(pars), len(outs)

    def body(*refs):
        ins = [r[...] for r in refs[:n_r + n_p]]
        res = fn(*ins)
        o_refs = refs[n_r + n_p:]
        for k in range(n_o):
            o_refs[k][...] = res[k].astype(o_refs[k].dtype)
        first = jnp.logical_and(pl.program_id(0) == 0, pl.program_id(1) == 0)
        for k in range(n_acc):
            ref = o_refs[n_o + k]

            @pl.when(first)
            def _():
                ref[...] = jnp.zeros_like(ref)

            ref[...] += res[n_o + k]

    out_shape = [jax.ShapeDtypeStruct((m, o.width), o.dtype) for o in outs]
    out_specs = [o.spec(tm) for o in outs]
    for _ in range(n_acc):
        out_shape.append(jax.ShapeDtypeStruct((1, LANES), F32))
        out_specs.append(pl.BlockSpec((1, LANES), lambda j, i: (0, 0)))
    sem = ("arbitrary", "arbitrary") if n_acc else ("parallel", "parallel")
    return pl.pallas_call(
        body, grid=(nj, m // tm), name=name,
        in_specs=[r.spec(tm) for r in rows] + [p.spec() for p in pars],
        out_specs=out_specs, out_shape=out_shape, compiler_params=_params(sem),
    )(*[r.arr for r in rows], *[p.arr for p in pars])


def tile_bwd(name, fn, *, m, tm, nj, rows, pars, cts, drows):
    n_r, n_p, n_c = len(rows), len(pars), len(cts)
    want = [k for k in range(n_r) if drows[k] is not None]

    def body(*refs):
        ins = [r[...] for r in refs[:n_r + n_p]]
        ct = [r[...] for r in refs[n_r + n_p:n_r + n_p + n_c]]
        o_refs = refs[n_r + n_p + n_c:]
        res, vjp = jax.vjp(fn, *ins)
        grads = vjp(tuple(c.astype(r.dtype) for c, r in zip(ct, res)))
        for pos, k in enumerate(want):
            o_refs[pos][...] = grads[k].astype(o_refs[pos].dtype)
        for k in range(n_p):
            ref = o_refs[len(want) + k]
            first = pl.program_id(1) == 0
            if pars[k].kind == "full":
                first = jnp.logical_and(first, pl.program_id(0) == 0)

            @pl.when(first)
            def _():
                ref[...] = jnp.zeros_like(ref)

            ref[...] += grads[n_r + k].astype(F32)

    out_shape = [jax.ShapeDtypeStruct((m, drows[k].width), drows[k].dtype) for k in want]
    out_specs = [drows[k].spec(tm) for k in want]
    for p in pars:
        out_shape.append(jax.ShapeDtypeStruct(p.arr.shape, F32))
        out_specs.append(p.spec())
    return pl.pallas_call(
        body, grid=(nj, m // tm), name=name,
        in_specs=[r.spec(tm) for r in rows] + [p.spec() for p in pars] + [c.spec(tm) for c in cts],
        out_specs=out_specs, out_shape=out_shape, compiler_params=_params(("arbitrary", "arbitrary")),
    )(*[r.arr for r in rows], *[p.arr for p in pars], *[c.arr for c in cts])


def loss_head(name, x, y, tgt, g, *, m, tm):
    def body(x_ref, y_ref, t_ref, g_ref, dout_ref, dy_ref, loss_ref, dg_ref):
        normed, vjp = jax.vjp(_rms, y_ref[...], g_ref[...])
        err = x_ref[...] + normed - t_ref[...]
        dout = err * (1.0 / D_MODEL)
        dy, dg = vjp(dout)
        dout_ref[...] = dout
        dy_ref[...] = dy.astype(dy_ref.dtype)

        @pl.when(pl.program_id(0) == 0)
        def _():
            loss_ref[...] = jnp.zeros_like(loss_ref)
            dg_ref[...] = jnp.zeros_like(dg_ref)

        loss_ref[...] += 0.5 * jnp.sum(jnp.mean(err * err, axis=-1, keepdims=True), axis=0, keepdims=True)
        dg_ref[...] += dg

    row = pl.BlockSpec((tm, D_MODEL), lambda i: (i, 0))
    whole = lambda w: pl.BlockSpec((1, w), lambda i: (0, 0))
    return pl.pallas_call(
        body, grid=(m // tm,), name=name, in_specs=[row, row, row, whole(D_MODEL)],
        out_specs=[row, row, whole(LANES), whole(D_MODEL)],
        out_shape=[jax.ShapeDtypeStruct((m, D_MODEL), F32), jax.ShapeDtypeStruct((m, D_MODEL), BF16),
                   jax.ShapeDtypeStruct((1, LANES), F32), jax.ShapeDtypeStruct((1, D_MODEL), F32)],
        compiler_params=_params(("arbitrary",)),
    )(x, y, tgt, g)


class Blk:
    def __init__(self, arr, block, index):
        self.arr, self.block, self.index = arr, block, index

    def spec(self):
        return pl.BlockSpec(self.block, self.index)


def _flat2(v):
    return v if v.ndim == 2 else v.reshape(-1, v.shape[-1])


def mm(name, pat, a, b, o, out_dtype, grid, after=None, b_join=False, o_split=False):
    nk = grid[2]
    o_shape = o.arr

    def put(o_ref, r):
        if o_split:
            half = r.shape[1] // 2
            o_ref[0] = r[:, :half].astype(out_dtype)
            o_ref[1] = r[:, half:].astype(out_dtype)
        else:
            o_ref[...] = r.astype(out_dtype).reshape(o_ref.shape)

    def body(*refs):
        a_ref, b_ref = refs[0], refs[1]
        o_ref = refs[3] if after is not None else refs[2]
        bv = jnp.concatenate([b_ref[0], b_ref[1]], axis=1) if b_join else _flat2(b_ref[...])
        r = _dg(_flat2(a_ref[...]), bv, pat)
        if nk == 1:
            put(o_ref, r)
            return
        acc_ref = refs[-1]
        kk = pl.program_id(2)

        @pl.when(kk == 0)
        def _():
            acc_ref[...] = r

        @pl.when(kk > 0)
        def _():
            acc_ref[...] += r

        @pl.when(kk == nk - 1)
        def _():
            put(o_ref, acc_ref[...])

    ob = [d for d in o.block if d is not None]
    if o_split:
        acc_shape = (ob[1], 2 * ob[2])
    else:
        acc_shape = (ob[0], ob[1]) if len(ob) == 2 else (ob[0] * ob[1], ob[2])
    in_specs = [a.spec(), b.spec()]
    args = [a.arr, b.arr]
    if after is not None:
        in_specs.append(pl.BlockSpec(memory_space=pl.ANY))
        args.append(after)
    return pl.pallas_call(
        body, grid=grid, name=name, in_specs=in_specs, out_specs=o.spec(),
        out_shape=jax.ShapeDtypeStruct(o_shape, out_dtype),
        scratch_shapes=[pltpu.VMEM(acc_shape, F32)] if nk > 1 else [],
        compiler_params=_params(("parallel", "parallel", "arbitrary")),
    )(*args)


def _div_tile(n, cap):
    if n <= cap:
        return n
    best = 128
    for t in range(128, cap + 1, 128):
        if n % t == 0:
            best = t
    return best


def mm2d(name, pat, a, b, out_dtype=F32):
    if pat == "tn":
        k, m = a.shape
    else:
        m, k = a.shape
    n = b.shape[0] if pat == "nt" else b.shape[1]
    tm, tn, tk = _div_tile(m, 1024), _div_tile(n, 1024), _div_tile(k, 1024)
    a_blk = Blk(a, (tk, tm), lambda i, j, kk: (kk, i)) if pat == "tn" else Blk(a, (tm, tk), lambda i, j, kk: (i, kk))
    b_blk = Blk(b, (tn, tk), lambda i, j, kk: (j, kk)) if pat == "nt" else Blk(b, (tk, tn), lambda i, j, kk: (kk, j))
    o_blk = Blk((m, n), (tm, tn), lambda i, j, kk: (i, j))
    return mm(name, pat, a_blk, b_blk, o_blk, out_dtype, (m // tm, n // tn, k // tk))


def hgrn_fwd(name, z, logits, hnorm, *, n_batch, seq):
    m = n_batch * seq
    ts = min(HGRN_SEG, seq)
    n_seg = seq // ts

    def body(q_ref, f_ref, v_ref, g_ref, lg_ref, hn_ref, o_ref, sp_ref, st_ref):
        s = pl.program_id(2)

        @pl.when(s == 0)
        def _():
            st_ref[...] = jnp.zeros_like(st_ref)

        st = st_ref[...]
        sp_ref[...] = st
        o, st_new = fn_hgrn_seg(q_ref[...], f_ref[...], v_ref[...], g_ref[...], st, lg_ref[...], hn_ref[...])
        o_ref[...] = o
        st_ref[...] = st_new

    part = lambda p: pl.BlockSpec((ts, 128), lambda h, b, s: (b * n_seg + s, 4 * p + h))
    return pl.pallas_call(
        body, grid=(A_HEADS, n_batch, n_seg), name=name,
        in_specs=[part(0), part(1), part(2), part(3),
                  pl.BlockSpec((3, 128), lambda h, b, s: (0, h)),
                  pl.BlockSpec((1, 128), lambda h, b, s: (0, h))],
        out_specs=[pl.BlockSpec((ts, 128), lambda h, b, s: (b * n_seg + s, h)),
                   pl.BlockSpec((128, 128), lambda h, b, s: ((b * n_seg + s) * A_HEADS + h, 0))],
        out_shape=[jax.ShapeDtypeStruct((m, A_WIDTH), BF16),
                   jax.ShapeDtypeStruct((n_batch * n_seg * A_HEADS * 128, 128), F32)],
        scratch_shapes=[pltpu.VMEM((128, 128), F32)],
        compiler_params=_params(("arbitrary", "arbitrary", "arbitrary")),
    )(z, z, z, z, logits, hnorm)


def hgrn_bwd(name, z, sprev, logits, hnorm, do, *, n_batch, seq):
    m = n_batch * seq
    ts = min(HGRN_SEG, seq)
    n_seg = seq // ts

    def body(q_ref, f_ref, v_ref, g_ref, sp_ref, lg_ref, hn_ref, do_ref, dq_ref, df_ref, dv_ref, dg_ref, dlg_ref, dhn_ref, dst_ref):
        s = pl.program_id(2)

        @pl.when(s == 0)
        def _():
            dst_ref[...] = jnp.zeros_like(dst_ref)

        res, vjp = jax.vjp(fn_hgrn_seg, q_ref[...], f_ref[...], v_ref[...], g_ref[...], sp_ref[...], lg_ref[...], hn_ref[...])
        dq, df, dv, dg, dst, dlg, dhn = vjp((do_ref[...].astype(res[0].dtype), dst_ref[...]))
        dq_ref[...] = dq.astype(dq_ref.dtype)
        df_ref[...] = df.astype(df_ref.dtype)
        dv_ref[...] = dv.astype(dv_ref.dtype)
        dg_ref[...] = dg.astype(dg_ref.dtype)
        dst_ref[...] = dst
        first = jnp.logical_and(pl.program_id(1) == 0, s == 0)

        @pl.when(first)
        def _():
            dlg_ref[...] = jnp.zeros_like(dlg_ref)
            dhn_ref[...] = jnp.zeros_like(dhn_ref)

        dlg_ref[...] += dlg
        dhn_ref[...] += dhn

    rev = lambda b, s: b * n_seg + (n_seg - 1 - s)
    part = lambda p: pl.BlockSpec((ts, 128), lambda h, b, s: (rev(b, s), 4 * p + h))
    head = pl.BlockSpec((ts, 128), lambda h, b, s: (rev(b, s), h))
    dpart = jax.ShapeDtypeStruct((m, A_WIDTH), BF16)
    return pl.pallas_call(
        body, grid=(A_HEADS, n_batch, n_seg), name=name,
        in_specs=[part(0), part(1), part(2), part(3),
                  pl.BlockSpec((128, 128), lambda h, b, s: (rev(b, s) * A_HEADS + h, 0)),
                  pl.BlockSpec((3, 128), lambda h, b, s: (0, h)),
                  pl.BlockSpec((1, 128), lambda h, b, s: (0, h)),
                  head],
        out_specs=[head, head, head, head,
                   pl.BlockSpec((3, 128), lambda h, b, s: (0, h)),
                   pl.BlockSpec((1, 128), lambda h, b, s: (0, h))],
        out_shape=[dpart, dpart, dpart, dpart,
                   jax.ShapeDtypeStruct(logits.shape, F32),
                   jax.ShapeDtypeStruct(hnorm.shape, F32)],
        scratch_shapes=[pltpu.VMEM((128, 128), F32)],
        compiler_params=_params(("arbitrary", "arbitrary", "arbitrary")),
    )(z, z, z, z, sprev, logits, hnorm, do)


FFN_ROWS = 128
FFN_LANES = 128


def _ffn_tiles(m, seq):
    tm = min(512, seq)
    return tm, seq // tm, m // tm


def ffn_mid_fwd(name, hid, cw, cb, layer, *, m, seq):
    tm, n_t, n_i = _ffn_tiles(m, seq)
    hb = tm // HALO

    def body(x_ref, xb_ref, cw_ref, cb_ref, o_ref, c_ref):
        first = pl.program_id(1) % n_t == 0
        before = jnp.where(first, 0.0, xb_ref[...])
        ext = jnp.concatenate([before, x_ref[...]], axis=1)
        cg, cv = _ffn_conv(ext[0], ext[1], cw_ref[...], cb_ref[...])
        o_ref[...] = _ffn_gate(cg, cv).astype(o_ref.dtype)
        c_ref[0] = cg.astype(c_ref.dtype)
        c_ref[1] = cv.astype(c_ref.dtype)

    return pl.pallas_call(
        body, grid=(N_DEV // 2, n_i), name=name,
        in_specs=[pl.BlockSpec((2, None, tm, FF_BLK), lambda d, i: (0, d, i, 0)),
                  pl.BlockSpec((2, None, HALO, FF_BLK), lambda d, i: (0, d, jnp.maximum(i * hb - 1, 0), 0)),
                  pl.BlockSpec((2, None, None, FFN_CONV, FF_BLK), lambda d, i: (0, d, layer, 0, 0)),
                  pl.BlockSpec((None, 2, None, 1, FF_BLK), lambda d, i: (layer, 0, d, 0, 0))],
        out_specs=[pl.BlockSpec((None, tm, FF_BLK), lambda d, i: (d, i, 0)),
                   pl.BlockSpec((2, None, tm, FF_BLK), lambda d, i: (0, d, i, 0))],
        out_shape=[jax.ShapeDtypeStruct((N_DEV // 2, m, FF_BLK), BF16),
                   jax.ShapeDtypeStruct((2, N_DEV // 2, m, FF_BLK), BF16)],
        compiler_params=_params(("parallel", "parallel")),
    )(hid, hid, cw, cb)


def ffn_mid_bwd(name, hid, conv, cw, dact, layer, *, m, seq):
    tm, n_t, n_i = _ffn_tiles(m, seq)
    hb = tm // HALO
    last_blk = m // HALO - 1

    rc = min(FFN_ROWS, tm)
    lane_chunks = [(l0, min(FFN_LANES, FF_BLK - l0)) for l0 in range(0, FF_BLK, FFN_LANES)]

    def body(x_ref, c_ref, ca_ref, cw_ref, da_ref, daa_ref, dx_ref, dcw_ref, dcb_ref, cext_ref, dext_ref):
        i = pl.program_id(1)
        last = i % n_t == n_t - 1
        cext_ref[:, :tm] = c_ref[...]
        cext_ref[:, tm:] = ca_ref[...]
        dext_ref[:tm] = da_ref[...]
        dext_ref[tm:] = jnp.where(last, jnp.zeros_like(daa_ref[...]), daa_ref[...])

        @pl.when(i == 0)
        def _():
            dcw_ref[...] = jnp.zeros_like(dcw_ref)
            dcb_ref[...] = jnp.zeros_like(dcb_ref)

        for l0, lw in lane_chunks:
            lanes = slice(l0, l0 + lw)

            def chunk(c, sums, lanes=lanes, lw=lw):
                r0 = pl.multiple_of(c * rc, rc)
                ext = pl.ds(r0, rc + HALO)
                cg, cv = cext_ref[0, ext, lanes].astype(F32), cext_ref[1, ext, lanes].astype(F32)
                _, vjp_gate = jax.vjp(_ffn_gate, cg, cv)
                dconv = vjp_gate(dext_ref[ext, lanes].astype(F32))
                out = []
                for half in range(2):
                    x = x_ref[half, pl.ds(r0, rc), lanes]
                    dx = None
                    for k in range(FFN_CONV):
                        s = FFN_CONV - 1 - k
                        dc_s = _shift_raw(dconv[half], s, True, 0.0)[:rc]
                        term = cw_ref[half, k:k + 1, lanes] * dc_s
                        dx = term if dx is None else dx + term
                        out.append(sums[len(out)] + jnp.sum(x * dc_s, axis=0, keepdims=True))
                    out.append(sums[len(out)] + jnp.sum(dconv[half][:rc], axis=0, keepdims=True))
                    dx_ref[half, pl.ds(r0, rc), lanes] = dx.astype(dx_ref.dtype)
                return tuple(out)

            zero = jnp.zeros((1, lw), F32)
            sums = lax.fori_loop(0, tm // rc, chunk, (zero,) * (2 * (FFN_CONV + 1)))
            for half in range(2):
                base = half * (FFN_CONV + 1)
                for k in range(FFN_CONV):
                    dcw_ref[half, k:k + 1, lanes] += sums[base + k]
                dcb_ref[half, :, lanes] += sums[base + FFN_CONV]

    return pl.pallas_call(
        body, grid=(N_DEV // 2, n_i), name=name,
        in_specs=[pl.BlockSpec((2, None, tm, FF_BLK), lambda d, i: (0, d, i, 0)),
                  pl.BlockSpec((2, None, tm, FF_BLK), lambda d, i: (0, d, i, 0)),
                  pl.BlockSpec((2, None, HALO, FF_BLK), lambda d, i: (0, d, jnp.minimum((i + 1) * hb, last_blk), 0)),
                  pl.BlockSpec((2, None, None, FFN_CONV, FF_BLK), lambda d, i: (0, d, layer, 0, 0)),
                  pl.BlockSpec((None, tm, FF_BLK), lambda d, i: (d, i, 0)),
                  pl.BlockSpec((None, HALO, FF_BLK), lambda d, i: (d, jnp.minimum((i + 1) * hb, last_blk), 0))],
        out_specs=[pl.BlockSpec((2, None, tm, FF_BLK), lambda d, i: (0, d, i, 0)),
                   pl.BlockSpec((2, None, FFN_CONV, FF_BLK), lambda d, i: (0, d, 0, 0)),
                   pl.BlockSpec((2, None, 1, FF_BLK), lambda d, i: (0, d, 0, 0))],
        out_shape=[jax.ShapeDtypeStruct((2, N_DEV // 2, m, FF_BLK), BF16),
                   jax.ShapeDtypeStruct((2, N_DEV // 2, FFN_CONV, FF_BLK), F32),
                   jax.ShapeDtypeStruct((2, N_DEV // 2, 1, FF_BLK), F32)],
        scratch_shapes=[pltpu.VMEM((2, tm + HALO, FF_BLK), BF16), pltpu.VMEM((tm + HALO, FF_BLK), BF16)],
        compiler_params=_params(("arbitrary", "arbitrary")),
    )(hid, conv, conv, cw, dact, dact)


ATT_BLK = 512
ATT_BLK_FWD = 1024
N_PAIR = C_HEADS // 2
TERM_W = C_HEADS * LANES


def term_placement():
    import numpy as np
    place = np.zeros((3, LANES, TERM_W), np.float32)
    ones_q = np.zeros((1, TERM_W), np.float32)
    ones_k = np.zeros((1, TERM_W), np.float32)
    for h in range(C_HEADS):
        for j in range(3):
            place[j, h, h * LANES + C_HEAD_DIM + j] = 1.0
            ones_q[0, h * LANES + C_HEAD_DIM + 3 + j] = 1.0
            ones_k[0, h * LANES + C_HEAD_DIM + j] = 1.0
    return (jnp.asarray(place.reshape(3 * LANES, TERM_W), BF16), jnp.asarray(ones_q, F32), jnp.asarray(ones_k, F32))


def fn_fox_terms(c, place, ones_q, ones_k):
    parts = _split3(c)
    placed = sum(_dg(parts[j], place[j * LANES:(j + 1) * LANES], "nn") for j in range(3))
    return (placed + ones_q).astype(BF16), (ones_k - pltpu.roll(placed, 3, 1)).astype(BF16)


def _head_tile(z, terms, e):
    lane = lax.broadcasted_iota(jnp.int32, z.shape, 1)
    base = z if e == 0 else pltpu.roll(z, C_HEAD_DIM, 1)
    return jnp.where(lane < C_HEAD_DIM, base, terms.astype(z.dtype))


def _head_only(z, e):
    lane = lax.broadcasted_iota(jnp.int32, z.shape, 1)
    mine = (lane < C_HEAD_DIM) if e == 0 else (lane >= C_HEAD_DIM)
    return jnp.where(mine, z, jnp.zeros_like(z)).astype(BF16)


def _pair_tile(a0, a1):
    lane = lax.broadcasted_iota(jnp.int32, a0.shape, 1)
    return jnp.where(lane < C_HEAD_DIM, a0, pltpu.roll(a1, C_HEAD_DIM, 1))


def _lane_col(a, k):
    lane = lax.broadcasted_iota(jnp.int32, a.shape, 1)
    return jnp.sum(jnp.where(lane == k, a, 0.0), axis=1, keepdims=True)


def _causal(s):
    key = lax.broadcasted_iota(jnp.int32, s.shape, 0)
    qry = lax.broadcasted_iota(jnp.int32, s.shape, 1)
    return qry >= key


def fox_pair_fwd(name, z, qterm, kterm, *, n_batch, seq):
    m = n_batch * seq
    blk = min(ATT_BLK_FWD, seq)
    nq = seq // blk
    dh = C_HEAD_DIM

    def body(zq_ref, zk_ref, zv_ref, qt_ref, kt_ref, o_ref, lse_ref, ka_ref, vt_ref):
        qi = pl.program_id(2)

        @pl.when(qi == 0)
        def _():
            zk = zk_ref[...]
            for e in range(2):
                ka_ref[e] = _head_tile(zk, kt_ref[:, e * LANES:(e + 1) * LANES], e).astype(BF16)
            for cb in range(nq):
                vt_ref[cb] = zv_ref[cb * blk:(cb + 1) * blk, :].T.astype(BF16)

        zq = zq_ref[...] * dh ** -0.5
        qa = [_head_tile(zq, qt_ref[:, e * LANES:(e + 1) * LANES], e).astype(BF16) for e in range(2)]

        def block(j, carry, diagonal):
            rows = pl.ds(pl.multiple_of(j * blk, blk), blk)
            out = []
            for e in range(2):
                mx, l, acc = carry[e]
                s = _dg(ka_ref[e, rows, :], qa[e], "nt")
                if diagonal:
                    s = jnp.where(_causal(s), s, NEG)
                mx_new = jnp.maximum(mx, jnp.max(s, axis=0, keepdims=True))
                p = jnp.exp(s - mx_new)
                alpha = jnp.exp(mx - mx_new)
                l = alpha * l + jnp.sum(p, axis=0, keepdims=True)
                acc = alpha * acc + _dg(vt_ref[j, e * dh:(e + 1) * dh, :], p, "nn")
                out.append((mx_new, l, acc))
            return tuple(out)

        one = (jnp.full((1, blk), NEG, F32), jnp.zeros((1, blk), F32), jnp.zeros((dh, blk), F32))
        carry = lax.fori_loop(0, qi, lambda j, cr: block(j, cr, False), (one, one))
        res = block(qi, carry, True)
        ot = jnp.concatenate([res[e][2] / res[e][1] for e in range(2)], axis=0)
        o_ref[...] = ot.T.astype(o_ref.dtype)
        for e in range(2):
            lse_ref[e] = res[e][0] + jnp.log(res[e][1])

    col = lambda part: (lambda b, g, i: (b, part * N_PAIR + g))
    return pl.pallas_call(
        body, grid=(n_batch, N_PAIR, nq), name=name,
        in_specs=[pl.BlockSpec((blk, LANES), lambda b, g, i: (b * nq + i, g)),
                  pl.BlockSpec((seq, LANES), col(1)),
                  pl.BlockSpec((seq, LANES), col(2)),
                  pl.BlockSpec((blk, 2 * LANES), lambda b, g, i: (b * nq + i, g)),
                  pl.BlockSpec((seq, 2 * LANES), lambda b, g, i: (b, g))],
        out_specs=[pl.BlockSpec((blk, LANES), lambda b, g, i: (b * nq + i, g)),
                   pl.BlockSpec((None, None, None, 2, 1, blk), lambda b, g, i: (b, g, i, 0, 0, 0))],
        out_shape=[jax.ShapeDtypeStruct((m, D_MODEL), BF16), jax.ShapeDtypeStruct((n_batch, N_PAIR, nq, 2, 1, blk), F32)],
        scratch_shapes=[pltpu.VMEM((2, seq, LANES), BF16), pltpu.VMEM((nq, LANES, blk), BF16)],
        compiler_params=_params(("parallel", "parallel", "arbitrary")),
    )(z, z, z, qterm, kterm)


def fox_pair_bwd(name, z, qterm, kterm, o, do, lse, *, n_batch, seq):
    m = n_batch * seq
    blk = min(ATT_BLK, seq)
    nq = seq // blk
    dh = C_HEAD_DIM

    def body(zq_ref, zk_ref, zv_ref, qt_ref, kt_ref, o_ref, do_ref, lse_ref, dq_ref, dk_ref, dv_ref, dc_ref,
             qa_ref, doh_ref, del_ref, dqt_ref, dk_acc, dv_acc):
        g, j = pl.program_id(1), pl.program_id(2)
        lane = lax.broadcasted_iota(jnp.int32, (blk, LANES), 1)

        @pl.when(jnp.logical_and(g == 0, j == 0))
        def _():
            dc_ref[...] = jnp.zeros_like(dc_ref)

        @pl.when(j == 0)
        def _():
            zq = zq_ref[...] * dh ** -0.5
            dov = do_ref[...]
            for e in range(2):
                qa_ref[e] = _head_tile(zq, qt_ref[:, e * LANES:(e + 1) * LANES], e).astype(BF16)
                doh_ref[e] = _head_only(dov, e)
            for cb in range(nq):
                rows = slice(cb * blk, (cb + 1) * blk)
                prod_t = (do_ref[rows, :].astype(F32) * o_ref[rows, :].astype(F32)).T
                for e in range(2):
                    del_ref[cb, e] = jnp.sum(prod_t[e * dh:(e + 1) * dh], axis=0, keepdims=True)
            dqt_ref[...] = jnp.zeros_like(dqt_ref)

        zk, zv = zk_ref[...], zv_ref[...]
        ka32 = [_head_tile(zk, kt_ref[:, e * LANES:(e + 1) * LANES], e) for e in range(2)]
        ka = [t.astype(BF16) for t in ka32]
        kat = [t.T.astype(BF16) for t in ka32]
        vh = [_head_only(zv, e) for e in range(2)]
        dk_acc[...] = jnp.zeros_like(dk_acc)
        dv_acc[...] = jnp.zeros_like(dv_acc)

        def block(i, diagonal):
            rows = pl.ds(pl.multiple_of(i * blk, blk), blk)
            for e in range(2):
                qv, dov = qa_ref[e, rows, :], doh_ref[e, rows, :]
                p = jnp.exp(_dg(ka[e], qv, "nt") - lse_ref[i, e])
                if diagonal:
                    p = jnp.where(_causal(p), p, 0.0)
                dv_acc[...] += _dg(p, dov, "nn")
                ds = p * (_dg(vh[e], dov, "nt") - del_ref[i, e])
                dk_acc[e] += _dg(ds, qv, "nn")
                dqt_ref[i, e] += _dg(kat[e], ds, "nn")

        block(j, True)

        def rest(i, carry):
            block(i, False)
            return carry

        lax.fori_loop(j + 1, nq, rest, 0)
        dk0, dk1 = dk_acc[0], dk_acc[1]
        dk_ref[...] = _pair_tile(dk0, dk1).astype(dk_ref.dtype)
        dv_ref[...] = dv_acc[...].astype(dv_ref.dtype)
        rows_j = pl.ds(pl.multiple_of(j * blk, blk), blk)
        for e, dke in enumerate((dk0, dk1)):
            dc_ref[rows_j, :] -= jnp.where(lane == 2 * g + e, _lane_col(dke, dh + 3), 0.0)

        @pl.when(j == nq - 1)
        def _():
            for i in range(nq):
                nat = [dqt_ref[i, e].T for e in range(2)]
                rows = slice(i * blk, (i + 1) * blk)
                dq_ref[rows, :] = (_pair_tile(nat[0], nat[1]) * dh ** -0.5).astype(dq_ref.dtype)
                for e in range(2):
                    dc_ref[rows, :] += jnp.where(lane == 2 * g + e, _lane_col(nat[e], dh), 0.0)

    col = lambda part: (lambda b, g, j: (b, part * N_PAIR + g))
    colj = lambda part: (lambda b, g, j: (b * nq + j, part * N_PAIR + g))
    pair = jax.ShapeDtypeStruct((m, D_MODEL), BF16)
    return pl.pallas_call(
        body, grid=(n_batch, N_PAIR, nq), name=name,
        in_specs=[pl.BlockSpec((seq, LANES), col(0)),
                  pl.BlockSpec((blk, LANES), colj(1)),
                  pl.BlockSpec((blk, LANES), colj(2)),
                  pl.BlockSpec((seq, 2 * LANES), lambda b, g, j: (b, g)),
                  pl.BlockSpec((blk, 2 * LANES), lambda b, g, j: (b * nq + j, g)),
                  pl.BlockSpec((seq, LANES), col(0)),
                  pl.BlockSpec((seq, LANES), col(0)),
                  pl.BlockSpec((None, None, nq, 2, 1, blk), lambda b, g, j: (b, g, 0, 0, 0, 0))],
        out_specs=[pl.BlockSpec((seq, LANES), col(0)),
                   pl.BlockSpec((blk, LANES), colj(0)),
                   pl.BlockSpec((blk, LANES), colj(0)),
                   pl.BlockSpec((seq, LANES), lambda b, g, j: (b, 0))],
        out_shape=[pair, pair, pair, jax.ShapeDtypeStruct((m, LANES), F32)],
        scratch_shapes=[pltpu.VMEM((2, seq, LANES), BF16), pltpu.VMEM((2, seq, LANES), BF16),
                        pltpu.VMEM((nq, 2, 1, blk), F32), pltpu.VMEM((nq, 2, LANES, blk), F32),
                        pltpu.VMEM((2, blk, LANES), F32), pltpu.VMEM((blk, LANES), F32)],
        compiler_params=_params(("arbitrary", "arbitrary", "arbitrary")),
    )(z, z, z, qterm, kterm, o, do, lse)


def _split3(c):
    c1 = c.astype(BF16)
    r1 = c - c1.astype(F32)
    c2 = r1.astype(BF16)
    c3 = (r1 - c2.astype(F32)).astype(BF16)
    return c1, c2, c3


def _mesh_pos():
    return lax.axis_index("x"), lax.axis_index("y"), lax.axis_index("c")


def _flip(v, bit):
    return 1 - v if bit else v


def all_gather(name, blocks):
    n = len(blocks)

    def body(*refs):
        x_refs, out_refs = refs[:n], refs[n:2 * n]
        send_sems, recv_sems, local_sems = refs[2 * n:]
        x, y, c = _mesh_pos()
        me, sibling = (x, y, c), (x, y, 1 - c)
        chips = [(1 - x, y), (x, 1 - y), (1 - x, 1 - y)]

        def slot(a, px, py, pc):
            return out_refs[a].at[4 * px + 2 * py + pc]

        def copy(a, k, blk, to, src=None):
            return pltpu.make_async_remote_copy(
                src_ref=slot(a, *blk) if src is None else src, dst_ref=slot(a, *blk),
                send_sem=send_sems.at[a, k], recv_sem=recv_sems.at[a, k], device_id=to, device_id_type=MESH)

        mine = [pltpu.make_async_copy(x_refs[a], slot(a, *me), local_sems.at[a]) for a in range(n)]
        for cp in mine:
            cp.start()
        sends = []
        for a in range(n):
            sends.append(copy(a, 0, me, sibling, src=x_refs[a]))
            sends += [copy(a, 1 + j, me, (*chip, c), src=x_refs[a]) for j, chip in enumerate(chips)]
        for cp in sends:
            cp.start()
        for j, chip in enumerate(chips):
            for a in range(n):
                copy(a, 1 + j, (*chip, c), me).wait_recv()
                passed = copy(a, 4 + j, (*chip, c), sibling)
                passed.start()
                sends.append(passed)
        for a in range(n):
            copy(a, 0, sibling, me).wait_recv()
            for j, chip in enumerate(chips):
                copy(a, 4 + j, (*chip, 1 - c), me).wait_recv()
        for cp in sends:
            cp.wait_send()
        for cp in mine:
            cp.wait()

    hbm = pl.BlockSpec(memory_space=pl.ANY)
    return pl.pallas_call(
        body, name=name, out_shape=[jax.ShapeDtypeStruct((N_DEV,) + b.shape, b.dtype) for b in blocks],
        in_specs=[hbm] * n, out_specs=[hbm] * n,
        scratch_shapes=[pltpu.SemaphoreType.DMA((n, 7)), pltpu.SemaphoreType.DMA((n, 7)), pltpu.SemaphoreType.DMA((n,))],
    )(*blocks)


def _peers(x, y, c):
    return [(_flip(x, k & 4), _flip(y, k & 2), _flip(c, k & 1)) for k in range(1, N_DEV)]


def gather_start(name, blocks, lands):
    n = len(blocks)

    def body(*refs):
        x_refs, land_refs = refs[:n], refs[n:2 * n]
        send_sems, recv_sems = refs[2 * n], refs[2 * n + 1]
        token = refs[-1]
        x, y, c = _mesh_pos()
        me = 4 * x + 2 * y + c
        for k, peer in enumerate(_peers(x, y, c)):
            for a in range(n):
                pltpu.make_async_remote_copy(
                    src_ref=x_refs[a], dst_ref=land_refs[a].at[me], send_sem=send_sems.at[7 * a + k], recv_sem=recv_sems.at[7 * a + k],
                    device_id=peer, device_id_type=MESH).start()
        token[...] = jnp.zeros_like(token)

    hbm = pl.BlockSpec(memory_space=pltpu.HBM)
    sem = pl.BlockSpec(memory_space=pltpu.SEMAPHORE)
    out_shape = ([pltpu.SemaphoreType.DMA((7 * n,)), pltpu.SemaphoreType.DMA((7 * n,))]
                 + [pltpu.HBM(b.shape, b.dtype) for b in blocks] + [pltpu.HBM(l.shape, l.dtype) for l in lands]
                 + [jax.ShapeDtypeStruct((8, LANES), F32)])
    res = pl.pallas_call(
        body, name=name, out_shape=out_shape, in_specs=[hbm] * (2 * n),
        out_specs=[sem, sem] + [hbm] * (2 * n) + [pl.BlockSpec(memory_space=pltpu.VMEM)],
        input_output_aliases={a: 2 + a for a in range(2 * n)},
        compiler_params=pltpu.CompilerParams(has_side_effects=pltpu.SideEffectType.DATAFLOW_SIDE_EFFECTING),
    )(*[pltpu.with_memory_space_constraint(b, pltpu.HBM) for b in blocks],
      *[pltpu.with_memory_space_constraint(l, pltpu.HBM) for l in lands])
    return res[0], res[1], res[2:2 + n], res[2 + n:2 + 2 * n], res[-1]


def gather_wait(name, send_sems, recv_sems, blocks, lands, after):
    n = len(blocks)

    def body(*refs):
        x_refs, land_refs = refs[:n], refs[n:2 * n]
        s_sems, r_sems = refs[2 * n], refs[2 * n + 1]
        x, y, c = _mesh_pos()
        me = 4 * x + 2 * y + c
        for k, peer in enumerate(_peers(x, y, c)):
            for a in range(n):
                cp = pltpu.make_async_remote_copy(
                    src_ref=x_refs[a], dst_ref=land_refs[a].at[me], send_sem=s_sems.at[7 * a + k], recv_sem=r_sems.at[7 * a + k],
                    device_id=peer, device_id_type=MESH)
                cp.wait_send()
                cp.wait_recv()

    hbm = pl.BlockSpec(memory_space=pltpu.HBM)
    sem = pl.BlockSpec(memory_space=pltpu.SEMAPHORE)
    res = pl.pallas_call(
        body, name=name,
        out_shape=[pltpu.HBM(b.shape, b.dtype) for b in blocks] + [pltpu.HBM(l.shape, l.dtype) for l in lands],
        in_specs=[hbm] * (2 * n) + [sem, sem, pl.BlockSpec(memory_space=pl.ANY)], out_specs=[hbm] * (2 * n),
        input_output_aliases={a: a for a in range(2 * n)},
        compiler_params=pltpu.CompilerParams(has_side_effects=pltpu.SideEffectType.DATAFLOW_SIDE_EFFECTING),
    )(*blocks, *lands, send_sems, recv_sems, after)
    return res[n:]


def _split_exchange(name, sends, lands, sems, after):
    n = len(sends)
    starting = sems is None

    def body(*refs):
        s_refs, l_refs = refs[:n], refs[n:2 * n]
        send_sems, recv_sems = refs[2 * n], refs[2 * n + 1]
        x, y, c = _mesh_pos()
        me = 4 * x + 2 * y + c
        for k, (px, py, pc) in enumerate(_peers(x, y, c)):
            for a in range(n):
                cp = pltpu.make_async_remote_copy(
                    src_ref=s_refs[a].at[4 * px + 2 * py + pc], dst_ref=l_refs[a].at[me],
                    send_sem=send_sems.at[7 * a + k], recv_sem=recv_sems.at[7 * a + k],
                    device_id=(px, py, pc), device_id_type=MESH)
                if starting:
                    cp.start()
                else:
                    cp.wait_send()
                    cp.wait_recv()
        if starting:
            refs[-1][...] = jnp.zeros_like(refs[-1])

    hbm = pl.BlockSpec(memory_space=pltpu.HBM)
    sem = pl.BlockSpec(memory_space=pltpu.SEMAPHORE)
    thru = [pltpu.HBM(t.shape, t.dtype) for t in list(sends) + list(lands)]
    effect = pltpu.CompilerParams(has_side_effects=pltpu.SideEffectType.DATAFLOW_SIDE_EFFECTING)
    if starting:
        res = pl.pallas_call(
            body, name=name, in_specs=[hbm] * (2 * n),
            out_shape=[pltpu.SemaphoreType.DMA((7 * n,)), pltpu.SemaphoreType.DMA((7 * n,))] + thru + [jax.ShapeDtypeStruct((8, LANES), F32)],
            out_specs=[sem, sem] + [hbm] * (2 * n) + [pl.BlockSpec(memory_space=pltpu.VMEM)],
            input_output_aliases={a: 2 + a for a in range(2 * n)}, compiler_params=effect,
        )(*[pltpu.with_memory_space_constraint(t, pltpu.HBM) for t in list(sends) + list(lands)])
        return res[0], res[1], res[2:2 + n], res[2 + n:2 + 2 * n], res[-1]
    res = pl.pallas_call(
        body, name=name, out_shape=thru, in_specs=[hbm] * (2 * n) + [sem, sem, pl.BlockSpec(memory_space=pl.ANY)],
        out_specs=[hbm] * (2 * n), input_output_aliases={a: a for a in range(2 * n)}, compiler_params=effect,
    )(*sends, *lands, sems[0], sems[1], after)
    return res[n:]


def own_slot_only(send, me):
    mine = lax.dynamic_index_in_dim(send, me, 0, keepdims=False)
    return lax.dynamic_update_index_in_dim(lax.empty(send.shape, send.dtype), mine, me, 0)


def all_to_all(name, sends):
    n = len(sends)

    def body(*refs):
        s_refs, r_refs = refs[:n], refs[n:2 * n]
        send_sems, recv_sems, local_sems = refs[2 * n:]
        x, y, c = _mesh_pos()
        me = 4 * x + 2 * y + c
        mine = [pltpu.make_async_copy(s_refs[a].at[me], r_refs[a].at[me], local_sems.at[a]) for a in range(n)]
        for cp in mine:
            cp.start()
        copies = []
        for k in range(1, N_DEV):
            px, py, pc = _flip(x, k & 4), _flip(y, k & 2), _flip(c, k & 1)
            for a in range(n):
                copies.append(pltpu.make_async_remote_copy(
                    src_ref=s_refs[a].at[4 * px + 2 * py + pc], dst_ref=r_refs[a].at[me],
                    send_sem=send_sems.at[a, k - 1], recv_sem=recv_sems.at[a, k - 1],
                    device_id=(px, py, pc), device_id_type=MESH))
        for cp in copies:
            cp.start()
        for cp in copies:
            cp.wait_recv()
        for cp in copies:
            cp.wait_send()
        for cp in mine:
            cp.wait()

    hbm = pl.BlockSpec(memory_space=pl.ANY)
    return pl.pallas_call(
        body, name=name, out_shape=[jax.ShapeDtypeStruct(s.shape, s.dtype) for s in sends],
        in_specs=[hbm] * n, out_specs=[hbm] * n,
        scratch_shapes=[pltpu.SemaphoreType.DMA((n, 7)), pltpu.SemaphoreType.DMA((n, 7)), pltpu.SemaphoreType.DMA((n,))],
    )(*sends)


def _row_tile(r, cap, step):
    return next((t for t in range(cap, step - 1, -step) if r % t == 0), r)


def _sum_parts(p, n):
    t = [p[k].astype(F32) for k in range(n)]
    while len(t) > 1:
        t = [t[k] + t[k + 1] for k in range(0, len(t), 2)]
    return t[0]


def _adam(g, w, m, v):
    m = ADAM_B1 * m + (1.0 - ADAM_B1) * g
    v = ADAM_B2 * v + (1.0 - ADAM_B2) * (g * g)
    m_hat = m / (1.0 - ADAM_B1 ** ADAM_STEP)
    v_hat = v / (1.0 - ADAM_B2 ** ADAM_STEP)
    return -ADAM_LR * (m_hat / (jnp.sqrt(v_hat) + ADAM_EPS) + ADAM_WD * w), m, v


def adam_tiled(name, partials, w, m_, v_, layer=0, prev=None):
    _, r, c = w.shape
    n_part = partials.shape[0]
    tr = _row_tile(r, 256, 16)

    def body(*refs):
        p_ref, w_ref, m_ref, v_ref = refs[:4]
        g_ref, d_ref, nm_ref, nv_ref = refs[-4:]
        g = _sum_parts(p_ref, n_part)
        g_ref[...] = g
        d_ref[...], nm_ref[...], nv_ref[...] = _adam(g, w_ref[...], m_ref[...], v_ref[...])

    spec = pl.BlockSpec((None, tr, c), lambda i: (layer, i, 0))
    in_specs = [pl.BlockSpec((n_part, None, tr, c), lambda i: (0, 0, i, 0)), spec, spec, spec]
    args = [partials, w, m_, v_]
    aliases = {}
    if prev is not None:
        in_specs += [pl.BlockSpec(memory_space=pl.ANY)] * 4
        args += list(prev)
        aliases = {4 + k: k for k in range(4)}
    return pl.pallas_call(
        body, grid=(r // tr,), name=name, in_specs=in_specs,
        out_specs=[spec] * 4, out_shape=[jax.ShapeDtypeStruct(w.shape, F32)] * 4,
        input_output_aliases=aliases, compiler_params=_params(("parallel",)),
    )(*args)


def adam_small(name, items, extra):
    n, ne = len(items), len(extra)

    def body(*refs):
        ins, outs = refs[:4 * n + ne], refs[4 * n + ne:]
        for a in range(n):
            p_ref, w_ref, m_ref, v_ref = ins[4 * a:4 * a + 4]
            g = _sum_parts(p_ref, N_DEV)
            outs[4 * a][...] = g
            outs[4 * a + 1][...], outs[4 * a + 2][...], outs[4 * a + 3][...] = _adam(g, w_ref[...], m_ref[...], v_ref[...])
        for e in range(ne):
            outs[4 * n + e][...] = _sum_parts(ins[4 * n + e], N_DEV)

    args, out_shape = [], []
    for p, w, m_, v_ in items:
        args += [p, w, m_, v_]
        out_shape += [jax.ShapeDtypeStruct(w.shape, F32)] * 4
    for e in extra:
        args.append(e)
        out_shape.append(jax.ShapeDtypeStruct(e.shape[1:], F32))
    vmem = pl.BlockSpec(memory_space=pltpu.VMEM)
    res = pl.pallas_call(body, name=name, in_specs=[vmem] * len(args), out_specs=[vmem] * len(out_shape), out_shape=out_shape)(*args)
    return [res[4 * a:4 * a + 4] for a in range(n)], res[4 * n:]


def _cols_from_gather(g):
    g = jnp.moveaxis(g, 0, -2)
    return g.reshape(g.shape[:-2] + (g.shape[-2] * g.shape[-1],))


def _cols_to_blocks(w):
    w = w.reshape(w.shape[:-1] + (N_DEV, w.shape[-1] // N_DEV))
    return jnp.moveaxis(w, -2, 0)


def _block_diag(w):
    z = jnp.zeros((B_BLOCK_DIM, B_BLOCK_DIM), w.dtype)
    rows = []
    for j in range(B_BLOCKS // 2):
        top = jnp.concatenate([w[2 * j], z], axis=1)
        bot = jnp.concatenate([z, w[2 * j + 1]], axis=1)
        rows.append(jnp.concatenate([top, bot], axis=0))
    return jnp.concatenate(rows, axis=0)


def _block_diag_grad(d):
    out = []
    for j in range(B_BLOCKS // 2):
        blk = d[128 * j:128 * (j + 1)]
        out.append(blk[:64, :64])
        out.append(blk[64:, 64:])
    return jnp.stack(out)


NAMES = ("norm_gains", "even_w_in", "hgrn_lb_logits", "hgrn_norm", "rg_conv_w", "rg_conv_b", "rg_wa", "rg_ba", "rg_wx", "rg_bx",
         "rg_lambda", "even_w_out", "odd_w_in", "fox_f_bias", "odd_w_out", "ffn_w_up", "ffn_conv_w", "ffn_conv_b", "ffn_w_down")
SMALL_SHARDED = ("norm_gains", "rg_conv_w", "ffn_conv_w")
REPLICATED = ("hgrn_lb_logits", "hgrn_norm", "rg_conv_b", "rg_wa", "rg_ba", "rg_wx", "rg_bx", "rg_lambda", "fox_f_bias", "ffn_conv_b")


def _ffn_forward(tag, layer, h, w_up_g, cw5, cb5, w_down_g, m, seq):
    tm = _div_tile(m, 1024)
    nm = m // tm
    hid = mm(f"{tag}_up", "nn",
             Blk(h, (tm, D_MODEL), lambda i, j, k: (i, 0)),
             Blk(w_up_g, (None, None, D_MODEL, FF_BLK), lambda i, j, k: (j, 0, 0, 0)),
             Blk((N_DEV, m, FF_BLK), (None, tm, FF_BLK), lambda i, j, k: (j, i, 0)), F32, (nm, N_DEV, 1))
    hid = hid.reshape(2, N_DEV // 2, m, FF_BLK)
    act, conv = ffn_mid_fwd(f"{tag}_mid", hid, cw5, cb5, layer, m=m, seq=seq)
    f = mm(f"{tag}_down", "nn",
           Blk(act, (None, tm, FF_BLK), lambda i, j, k: (k, i, 0)),
           Blk(w_down_g, (2, None, FF_BLK // 2, D_MODEL), lambda i, j, k: (k, 0, 0, 0)),
           Blk((m, D_MODEL), (tm, D_MODEL), lambda i, j, k: (i, 0)), F32, (nm, 1, N_DEV // 2))
    return (hid, conv), act, f


def _ffn_backward(tag, layer, df, h, hid, act, w_up_g, cw5, cb5, w_down_g, m, seq):
    tm = _div_tile(m, 1024)
    nm = m // tm
    dact = mm(f"{tag}_dact", "nt",
              Blk(df, (tm, D_MODEL), lambda i, j, k: (i, 0)),
              Blk(w_down_g, (2, None, FF_BLK // 2, D_MODEL), lambda i, j, k: (j, 0, 0, 0)),
              Blk((N_DEV // 2, m, FF_BLK), (None, tm, FF_BLK), lambda i, j, k: (j, i, 0)), BF16, (nm, N_DEV // 2, 1))
    d_wdown = mm(f"{tag}_dwdown", "tn",
                 Blk(act, (None, tm, FF_BLK), lambda i, j, k: (i, k, 0)),
                 Blk(df, (tm, D_MODEL), lambda i, j, k: (k, 0)),
                 Blk(w_down_g.shape, (2, None, FF_BLK // 2, D_MODEL), lambda i, j, k: (i, 0, 0, 0)), BF16,
                 (N_DEV // 2, 1, nm))
    dhid, d_cw, d_cb = ffn_mid_bwd(f"{tag}_dmid", hid[0], hid[1], cw5, dact, layer, m=m, seq=seq)
    dhid = dhid.reshape(N_DEV, m, FF_BLK)
    dh = mm(f"{tag}_dh", "nt",
            Blk(dhid, (None, tm, FF_BLK), lambda i, j, k: (k, i, 0)),
            Blk(w_up_g, (None, None, D_MODEL, FF_BLK), lambda i, j, k: (k, 0, 0, 0)),
            Blk((m, D_MODEL), (tm, D_MODEL), lambda i, j, k: (i, 0)), BF16, (nm, 1, N_DEV))
    d_wup = mm(f"{tag}_dwup", "tn",
               Blk(h, (tm, D_MODEL), lambda i, j, k: (k, 0)),
               Blk(dhid, (None, tm, FF_BLK), lambda i, j, k: (j, k, 0)),
               Blk(w_up_g.shape, (None, None, D_MODEL, FF_BLK), lambda i, j, k: (j, 0, 0, 0)), BF16,
               (1, N_DEV, nm))
    return dh, d_wup, d_cw, d_cb, d_wdown


def kernel(x, norm_gains, even_w_in, hgrn_lb_logits, hgrn_norm, rg_conv_w, rg_conv_b, rg_wa, rg_ba, rg_wx, rg_bx, rg_lambda, even_w_out, odd_w_in, fox_f_bias, odd_w_out, ffn_w_up, ffn_conv_w, ffn_conv_b, ffn_w_down, loss_target, m_norm_gains, m_even_w_in, m_hgrn_lb_logits, m_hgrn_norm, m_rg_conv_w, m_rg_conv_b, m_rg_wa, m_rg_ba, m_rg_wx, m_rg_bx, m_rg_lambda, m_even_w_out, m_odd_w_in, m_fox_f_bias, m_odd_w_out, m_ffn_w_up, m_ffn_conv_w, m_ffn_conv_b, m_ffn_w_down, v_norm_gains, v_even_w_in, v_hgrn_lb_logits, v_hgrn_norm, v_rg_conv_w, v_rg_conv_b, v_rg_wa, v_rg_ba, v_rg_wx, v_rg_bx, v_rg_lambda, v_even_w_out, v_odd_w_in, v_fox_f_bias, v_odd_w_out, v_ffn_w_up, v_ffn_conv_w, v_ffn_conv_b, v_ffn_w_down):
    local = dict(locals())
    w = {n: local[n] for n in NAMES}
    mom = {n: local["m_" + n] for n in NAMES}
    var = {n: local["v_" + n] for n in NAMES}
    n_batch, seq, _ = x.shape
    m = n_batch * seq
    tm = _div_tile(m, 512)
    tmm = _div_tile(m, 1024)
    nm = m // tmm

    gathered = all_gather("gather_weights", [w["even_w_in"].astype(BF16)] + [w[n] for n in SMALL_SHARDED])
    g = dict(zip(("even_w_in",) + SMALL_SHARDED, gathered))
    w_in_e = g["even_w_in"]
    gains = _cols_from_gather(g["norm_gains"])
    me = 4 * lax.axis_index("x") + 2 * lax.axis_index("y") + lax.axis_index("c")
    own_block_only = lambda t: lax.dynamic_update_index_in_dim(lax.empty((N_DEV,) + t.shape, t.dtype), t, me, 0)
    behind = (g["norm_gains"][0, 0, 0, 0] * 0.0).astype(BF16)
    out0 = [w["even_w_out"].astype(BF16) + behind]
    out0_sent = gather_start("gather_out0_start", out0, [own_block_only(t) for t in out0])
    behind = (out0_sent[4][0, 0] * 0.0).astype(BF16)
    ffn0 = [w["ffn_w_up"][0:1].astype(BF16) + behind, w["ffn_w_down"][0:1].astype(BF16) + behind]
    ffn0_sent = gather_start("gather_ffn0_start", ffn0, [own_block_only(t) for t in ffn0])
    behind = (ffn0_sent[4][0, 0] * 0.0).astype(BF16)
    mix1w = [w["odd_w_in"].astype(BF16) + behind, w["odd_w_out"].astype(BF16) + behind]
    mix1_sent = gather_start("gather_mix1_start", mix1w, [own_block_only(t) for t in mix1w])
    behind = (mix1_sent[4][0, 0] * 0.0).astype(BF16)
    ffn1 = [w["ffn_w_up"][1:2].astype(BF16) + behind, w["ffn_w_down"][1:2].astype(BF16) + behind]
    ffn1_sent = gather_start("gather_ffn1_start", ffn1, [own_block_only(t) for t in ffn1])
    started = ffn1_sent[4]
    rg_cw = _cols_from_gather(g["rg_conv_w"])[0]
    n_layer = ffn_conv_w.shape[0]
    cw5 = g["ffn_conv_w"].reshape(2, N_DEV // 2, n_layer, FFN_CONV, FF_BLK)
    cb5 = ffn_conv_b.reshape(n_layer, 2, N_DEV // 2, 1, FF_BLK)
    gain = lambda l, k: gains[l, k:k + 1, :]
    wa_bd, wx_bd = _block_diag(rg_wa[0]), _block_diag(rg_wx[0])
    fbias = jnp.pad(fox_f_bias, ((0, 0), (0, LANES - C_HEADS)))

    x0 = x.reshape(m, D_MODEL)
    tgt = loss_target.reshape(m, D_MODEL)

    (h0,) = tile_fwd("l0_prenorm", fn_prenorm_after, m=m, tm=tm, nj=1, rows=[Row(x0)], pars=[Par(gain(0, 0)), Par(started)],
                     outs=[Out(D_MODEL, BF16)])
    z0 = mm("l0_in", "nn",
            Blk(h0, (tmm, D_MODEL), lambda i, j, k: (i, 0)),
            Blk(w_in_e, (2, None, D_MODEL, 384), lambda i, j, k: (j, 0, 0, 0)),
            Blk((m, 3072), (tmm, 768), lambda i, j, k: (i, j)), F32, (nm, N_DEV // 2, 1), b_join=True)
    oa, sprev = hgrn_fwd("l0_hgrn", z0, hgrn_lb_logits, hgrn_norm, n_batch=n_batch, seq=seq)
    rg_rows = lambda: [Row(z0, LANES, 16), Row(z0, LANES, 20)]
    rg_pars = lambda: [Par(rg_cw, "col", LANES), Par(rg_conv_b, "col", LANES), Par(wa_bd, "row", LANES), Par(rg_ba, "col", LANES),
                       Par(wx_bd, "row", LANES), Par(rg_bx, "col", LANES), Par(rg_lambda, "col", LANES)]
    (ob,) = tile_fwd("l0_rglru", fn_rglru, m=m, tm=seq, nj=B_WIDTH // LANES, rows=rg_rows(), pars=rg_pars(),
                     outs=[Out(B_WIDTH, BF16, LANES)])
    mixcat0 = jnp.concatenate([oa, ob], axis=-1)
    (g_out_e,) = gather_wait("gather_out0_wait", out0_sent[0], out0_sent[1], out0_sent[2], out0_sent[3], mixcat0)
    w_out_e = g_out_e.reshape(D_MODEL, D_MODEL)
    mix0 = mm2d("l0_out", "nn", mixcat0, w_out_e)
    x1, h1 = tile_fwd("l0_postnorm", fn_addnorm2, m=m, tm=tm, nj=1, rows=[Row(x0), Row(mix0)], pars=[Par(gain(0, 1)), Par(gain(0, 2))],
                      outs=[Out(D_MODEL, F32), Out(D_MODEL, BF16)])
    w_up_g0, w_down_g0 = gather_wait("gather_ffn0_wait", ffn0_sent[0], ffn0_sent[1], ffn0_sent[2], ffn0_sent[3], h1)
    hid0, act0, f0 = _ffn_forward("l0_ffn", 0, h1, w_up_g0, cw5, cb5, w_down_g0, m, seq)
    x2, h2 = tile_fwd("l0_ffnnorm", fn_addnorm2, m=m, tm=tm, nj=1, rows=[Row(x1), Row(f0)], pars=[Par(gain(0, 3)), Par(gain(1, 0))],
                      outs=[Out(D_MODEL, F32), Out(D_MODEL, BF16)])

    g_in_o, g_out_o = gather_wait("gather_mix1_wait", mix1_sent[0], mix1_sent[1], mix1_sent[2], mix1_sent[3], h2)
    w_in_o = jnp.pad(_cols_from_gather(g_in_o)[0], ((0, 0), (0, 3200 - 3088)))
    w_out_o = g_out_o.reshape(D_MODEL, D_MODEL)
    z1 = mm2d("l1_in", "nn", h2, w_in_o)
    (cgate,) = tile_fwd("l1_gate", fn_fox_gate, m=m, tm=seq, nj=1, rows=[Row(z1, LANES, 3072 // LANES)], pars=[Par(fbias)],
                        outs=[Out(LANES, F32)])
    place, ones_q, ones_k = term_placement()
    qterm, kterm = tile_fwd("l1_terms", fn_fox_terms, m=m, tm=tm, nj=1, rows=[Row(cgate)],
                            pars=[Par(place), Par(ones_q), Par(ones_k)], outs=[Out(TERM_W, BF16), Out(TERM_W, BF16)])
    oc, lse = fox_pair_fwd("l1_attn", z1, qterm, kterm, n_batch=n_batch, seq=seq)
    blk_b = min(ATT_BLK, seq)
    lse = lse.reshape(n_batch, N_PAIR, -1, 2, lse.shape[-1] // blk_b, blk_b).swapaxes(3, 4).reshape(n_batch, N_PAIR, seq // blk_b, 2, 1, blk_b)
    mix1 = mm2d("l1_out", "nn", oc, w_out_o)
    x3, h3 = tile_fwd("l1_postnorm", fn_addnorm2, m=m, tm=tm, nj=1, rows=[Row(x2), Row(mix1)], pars=[Par(gain(1, 1)), Par(gain(1, 2))],
                      outs=[Out(D_MODEL, F32), Out(D_MODEL, BF16)])
    w_up_g1, w_down_g1 = gather_wait("gather_ffn1_wait", ffn1_sent[0], ffn1_sent[1], ffn1_sent[2], ffn1_sent[3], h3)
    hid1, act1, f1 = _ffn_forward("l1_ffn", 1, h3, w_up_g1, cw5, cb5, w_down_g1, m, seq)
    dy, df1, loss_part, d_g13 = loss_head("loss", x3, f1, tgt, gain(1, 3), m=m, tm=tm)
    dh3, d_wup1, d_cw1, d_cb1, d_wdown1 = _ffn_backward("l1_ffn", 1, df1, h3, hid1, act1, w_up_g1, cw5, cb5, w_down_g1, m, seq)
    dx2, dmix1, d_g11, d_g12 = tile_bwd("l1_dpostnorm", fn_addnorm2, m=m, tm=tm, nj=1, rows=[Row(x2), Row(mix1)],
                                        pars=[Par(gain(1, 1)), Par(gain(1, 2))], cts=[Row(dy), Row(dh3)],
                                        drows=[Out(D_MODEL, F32), Out(D_MODEL, BF16)])
    doc = mm2d("l1_doc", "nt", dmix1, w_out_o, BF16)
    d_wout_o = mm2d("l1_dwout", "tn", oc, dmix1)
    dq, dk, dv, dc = fox_pair_bwd("l1_dattn", z1, qterm, kterm, oc, doc, lse, n_batch=n_batch, seq=seq)
    dzf, d_fbias = tile_bwd("l1_dgate", fn_fox_gate, m=m, tm=seq, nj=1, rows=[Row(z1, LANES, 3072 // LANES)], pars=[Par(fbias)],
                            cts=[Row(dc)], drows=[Out(LANES, BF16)])
    dz1 = jnp.concatenate([dq, dk, dv, dzf], axis=-1)
    dh2 = mm2d("l1_dh", "nt", dz1, w_in_o, BF16)
    d_win_o = mm2d("l1_dwin", "tn", h2, dz1)

    send1 = [_cols_to_blocks(d_win_o[None, :, :3088]).astype(BF16),
             d_wout_o.reshape(N_DEV, 1, D_MODEL // N_DEV, D_MODEL).astype(BF16), d_wup1, d_wdown1]
    sent1 = _split_exchange("exchange_l1_start", send1, [own_slot_only(t, me) for t in send1], None, None)

    dx1, df0, d_g03, d_g10 = tile_bwd("l0_dffnnorm", fn_addnorm2_after, m=m, tm=tm, nj=1, rows=[Row(x1), Row(f0)],
                                      pars=[Par(gain(0, 3)), Par(gain(1, 0)), Par(sent1[4])], cts=[Row(dx2), Row(dh2)],
                                      drows=[Out(D_MODEL, F32), Out(D_MODEL, BF16)])[:4]
    dh1, d_wup0, d_cw0, d_cb0, d_wdown0 = _ffn_backward("l0_ffn", 0, df0, h1, hid0, act0, w_up_g0, cw5, cb5, w_down_g0, m, seq)
    send0 = [d_wup0, d_wdown0]
    sent0 = _split_exchange("exchange_ffn0_start", send0, [own_slot_only(t, me) for t in send0], None, None)
    dx0a, dmix0, d_g01, d_g02 = tile_bwd("l0_dpostnorm", fn_addnorm2_after, m=m, tm=tm, nj=1, rows=[Row(x0), Row(mix0)],
                                         pars=[Par(gain(0, 1)), Par(gain(0, 2)), Par(sent0[4])], cts=[Row(dx1), Row(dh1)],
                                         drows=[Out(D_MODEL, F32), Out(D_MODEL, BF16)])[:4]
    dmixcat0 = mm2d("l0_dmixcat", "nt", dmix0, w_out_e, BF16)
    d_wout_e = mm2d("l0_dwout", "tn", mixcat0, dmix0)
    dzq, dzf0, dzv, dzg, d_lb, d_hnorm = hgrn_bwd("l0_dhgrn", z0, sprev, hgrn_lb_logits, hgrn_norm, dmixcat0, n_batch=n_batch, seq=seq)
    dzx, dzy, d_rcw, d_rcb, d_wa, d_ba, d_wx, d_bx, d_lam = tile_bwd(
        "l0_drglru", fn_rglru, m=m, tm=seq, nj=B_WIDTH // LANES, rows=rg_rows(), pars=rg_pars(),
        cts=[Row(dmixcat0, LANES, A_WIDTH // LANES)], drows=[Out(B_WIDTH, BF16, LANES), Out(B_WIDTH, BF16, LANES)])
    dz0 = jnp.concatenate([dzq, dzf0, dzv, dzg, dzx, dzy], axis=-1)
    d_win_e = mm("l0_dwin", "tn",
                 Blk(h0, (tmm, D_MODEL), lambda i, j, k: (k, 0)),
                 Blk(dz0, (tmm, 768), lambda i, j, k: (k, j)),
                 Blk(w_in_e.shape, (2, None, D_MODEL, 384), lambda i, j, k: (j, 0, 0, 0)), BF16, (1, N_DEV // 2, nm), o_split=True)
    send_e = [d_win_e, d_wout_e.reshape(N_DEV, 1, D_MODEL // N_DEV, D_MODEL).astype(BF16)]
    sent_e = _split_exchange("exchange_even_start", send_e, [own_slot_only(t, me) for t in send_e], None, None)
    d_ffn_cb = jnp.stack([d_cb0, d_cb1]).reshape(n_layer, 2 * D_FF)
    rep = {"hgrn_lb_logits": d_lb, "hgrn_norm": d_hnorm, "rg_conv_b": d_rcb, "rg_wa": _block_diag_grad(d_wa)[None], "rg_ba": d_ba,
           "rg_wx": _block_diag_grad(d_wx)[None], "rg_bx": d_bx, "rg_lambda": d_lam, "fox_f_bias": d_fbias[:, :C_HEADS],
           "ffn_conv_b": d_ffn_cb}
    rep_blocks = [rep[n] for n in REPLICATED] + [loss_part]
    rep_sent = gather_start("gather_partials_start", rep_blocks, [own_block_only(t) for t in rep_blocks])
    dh0 = mm("l0_dh", "nt",
             Blk(dz0, (tmm, 768), lambda i, j, k: (i, k)),
             Blk(w_in_e, (2, None, D_MODEL, 384), lambda i, j, k: (k, 0, 0, 0)),
             Blk((m, D_MODEL), (tmm, D_MODEL), lambda i, j, k: (i, 0)), BF16, (nm, 1, N_DEV // 2), after=sent_e[4] + rep_sent[4],
             b_join=True)
    dx0, d_g00 = tile_bwd("l0_dprenorm", fn_input_norm, m=m, tm=tm, nj=1, rows=[Row(x0)], pars=[Par(gain(0, 0))],
                          cts=[Row(dx0a), Row(dh0)], drows=[Out(D_MODEL, F32)])

    d_gains = jnp.stack([jnp.concatenate([d_g00, d_g01, d_g02, d_g03], axis=0), jnp.concatenate([d_g10, d_g11, d_g12, d_g13], axis=0)])
    d_ffn_cw = jnp.stack([d_cw0, d_cw1], axis=2).reshape(N_DEV, n_layer, FFN_CONV, FF_BLK)
    r_in_o, r_out_o, r_up1, r_down1 = _split_exchange("exchange_l1_wait", sent1[2], sent1[3], sent1[:2], dx0)
    r_up0, r_down0 = _split_exchange("exchange_ffn0_wait", sent0[2], sent0[3], sent0[:2], dx0)
    r_in_e, r_out_e = _split_exchange("exchange_even_wait", sent_e[2], sent_e[3], sent_e[:2], dx0)
    recv, res = {}, {}
    for n, r in (("even_w_in", r_in_e), ("even_w_out", r_out_e), ("odd_w_in", r_in_o), ("odd_w_out", r_out_o)):
        res[n] = adam_tiled("adam_" + n, r, w[n], mom[n], var[n])
    for n, parts_l in (("ffn_w_up", (r_up0, r_up1)), ("ffn_w_down", (r_down0, r_down1))):
        first_layer = adam_tiled(f"adam_{n}_0", parts_l[0], w[n], mom[n], var[n], layer=0)
        res[n] = adam_tiled(f"adam_{n}_1", parts_l[1], w[n], mom[n], var[n], layer=1, prev=first_layer)
    small_send = [_cols_to_blocks(d_gains), _cols_to_blocks(d_rcw[None]), d_ffn_cw]
    recv.update(zip(SMALL_SHARDED, all_to_all("exchange_small", small_send)))

    parts = gather_wait("gather_partials_wait", rep_sent[0], rep_sent[1], rep_sent[2], rep_sent[3], dx0)
    for n, p in zip(REPLICATED, parts):
        recv[n] = p
    small = SMALL_SHARDED + REPLICATED
    small_res, (loss_sum,) = adam_small("adam_small", [(recv[n], w[n], mom[n], var[n]) for n in small], [parts[-1]])
    res.update(dict(zip(small, small_res)))

    out = [loss_sum[0, 0], dx0.reshape(x.shape)]
    for k in range(4):
        out += [res[n][k] for n in NAMES]
    return tuple(out)
```

```python
import functools

import jax
import jax.numpy as jnp
from jax import lax
from jax.experimental import pallas as pl
from jax.experimental.pallas import tpu as pltpu

F32 = jnp.float32
BF16 = jnp.bfloat16

D_MODEL = 1024
A_HEADS = 4
A_WIDTH = 512
HGRN_CHUNK = 64
HGRN_SEG = 512
B_WIDTH = 512
B_BLOCKS = 8
B_BLOCK_DIM = 64
B_CONV = 4
RG_C = 8.0
C_HEADS = 16
C_HEAD_DIM = 64
D_FF = 2816
FFN_CONV = 3
EPS = 1e-6
LANES = 128
HALO = 16
N_DEV = 8
FF_BLK = 2 * D_FF // N_DEV
MESH = pl.DeviceIdType.MESH
NEG = -1e30
VMEM_LIMIT = 56 * 1024 * 1024

ADAM_LR = 0.001
ADAM_B1 = 0.9
ADAM_B2 = 0.999
ADAM_EPS = 1e-08
ADAM_WD = 0.01
ADAM_STEP = 10


def _dg(a, b, pat):
    nb = a.ndim - 2
    batch = (tuple(range(nb)), tuple(range(nb)))
    ca = a.ndim - 1 if pat[0] == "n" else a.ndim - 2
    cb = b.ndim - 2 if pat[1] == "n" else b.ndim - 1
    return lax.dot_general(a.astype(BF16), b.astype(BF16), (((ca,), (cb,)), batch), preferred_element_type=F32)


@functools.partial(jax.custom_vjp, nondiff_argnums=(2,))
def bdot(a, b, pat):
    return _dg(a, b, pat)


def _bdot_fwd(a, b, pat):
    return _dg(a, b, pat), (a, b)


def _bdot_bwd(pat, res, g):
    a, b = res
    if pat == "nn":
        return _dg(g, b, "nt"), _dg(a, g, "tn")
    if pat == "nt":
        return _dg(g, b, "nn"), _dg(g, a, "tn")
    return _dg(b, g, "nt"), _dg(a, g, "nn")


bdot.defvjp(_bdot_fwd, _bdot_bwd)


def _shift_raw(x, s, up, fill):
    if s == 0:
        return x
    n = x.shape[0]
    r = pltpu.roll(x, (n - s) if up else s, 0)
    idx = lax.broadcasted_iota(jnp.int32, x.shape, 0)
    mask = (idx >= n - s) if up else (idx < s)
    return jnp.where(mask, jnp.asarray(fill, x.dtype), r)


@functools.partial(jax.custom_vjp, nondiff_argnums=(1,))
def shift_down(x, s):
    return _shift_raw(x, s, False, 0.0)


def _shift_down_fwd(x, s):
    return _shift_raw(x, s, False, 0.0), None


def _shift_down_bwd(s, _, g):
    return (_shift_raw(g, s, True, 0.0),)


shift_down.defvjp(_shift_down_fwd, _shift_down_bwd)


def _scan_impl(a, u, up):
    n = a.shape[0]
    s = 1
    while s < n:
        u = a * _shift_raw(u, s, up, 0.0) + u
        if 2 * s < n:
            a = a * _shift_raw(a, s, up, 1.0)
        s *= 2
    return u


@jax.custom_vjp
def lin_scan(a, u):
    return _scan_impl(a, u, False)


def _lin_scan_fwd(a, u):
    h = _scan_impl(a, u, False)
    return h, (a, h)


def _lin_scan_bwd(res, g):
    a, h = res
    gh = _scan_impl(_shift_raw(a, 1, True, 0.0), g, True)
    return gh * _shift_raw(h, 1, False, 0.0), gh


lin_scan.defvjp(_lin_scan_fwd, _lin_scan_bwd)


def _cumsum_impl(x, up, period):
    n = x.shape[0]
    span = n if period is None else period
    idx = lax.broadcasted_iota(jnp.int32, x.shape, 0)
    pos = idx if period is None else idx % period
    s = 1
    while s < span:
        sh = _shift_raw(x, s, up, 0.0)
        if period is not None:
            keep = (pos < period - s) if up else (pos >= s)
            sh = jnp.where(keep, sh, 0.0)
        x = x + sh
        s *= 2
    return x


@functools.partial(jax.custom_vjp, nondiff_argnums=(1,))
def cumsum_rows(x, period):
    return _cumsum_impl(x, False, period)


def _cumsum_fwd(x, period):
    return _cumsum_impl(x, False, period), None


def _cumsum_bwd(period, _, g):
    return (_cumsum_impl(g, True, period),)


cumsum_rows.defvjp(_cumsum_fwd, _cumsum_bwd)


def _sigmoid(x):
    return jax.nn.sigmoid(x)


def _expm1(x):
    return jnp.tanh(0.5 * x) * (jnp.exp(x) + 1.0)


def _softplus(x):
    return jnp.maximum(x, 0.0) + jnp.log(1.0 + jnp.exp(-jnp.abs(x)))


def _rms(x, g):
    return x * lax.rsqrt(jnp.mean(x * x, axis=-1, keepdims=True) + EPS) * g


def fn_prenorm(x, g):
    return (_rms(x, g).astype(BF16),)


def fn_prenorm_after(x, g, _token):
    return fn_prenorm(x, g)


def fn_addnorm2(x, y, g_post, g_pre):
    x1 = x + _rms(y, g_post)
    return x1, _rms(x1, g_pre).astype(BF16)


def fn_addnorm2_after(x, y, g_post, g_pre, _token):
    return fn_addnorm2(x, y, g_post, g_pre)


def fn_input_norm(x, g):
    return x, _rms(x, g).astype(BF16)


def _causal_conv(x, w, b, taps):
    c = b
    for k in range(taps):
        c = c + w[k:k + 1, :] * shift_down(x, taps - 1 - k)
    return c


def fn_rglru(xb, yb, cw, cb, wa, ba, wx, bx, lam):
    xf = _causal_conv(xb, cw, cb, B_CONV)
    r = _sigmoid(bdot(xf, wa, "nn") + ba)
    i = _sigmoid(bdot(xf, wx, "nn") + bx)
    log_a = -RG_C * r * _softplus(-lam)
    a = jnp.exp(log_a)
    u = jnp.sqrt(-_expm1(2.0 * log_a)) * (i * xf)
    h = lin_scan(a, u)
    return ((h * jax.nn.gelu(yb)).astype(BF16),)


def fn_fox_gate(zf, bias):
    return (cumsum_rows(jax.nn.log_sigmoid(zf + bias), None),)


def fn_hgrn_seg(q, fl, v, g, st, logits, hn):
    rows = q.shape[0]
    nc = rows // HGRN_CHUNK
    l0, l1, l2 = logits[0:1, :], logits[1:2, :], logits[2:3, :]
    mx = jnp.maximum(jnp.maximum(l0, l1), l2)
    e0, e1, e2 = jnp.exp(l0 - mx), jnp.exp(l1 - mx), jnp.exp(l2 - mx)
    lb = e0 / (e0 + e1 + e2)
    forget = lb + (1.0 - lb) * _sigmoid(fl)
    qs = q * _sigmoid(q)
    kk = 1.0 - forget
    logf = jnp.log(forget)
    bcum = cumsum_rows(logf, HGRN_CHUNK)
    c3 = lambda t: t.reshape(nc, HGRN_CHUNK, 128)
    b_last = jnp.sum(c3(logf), axis=1, keepdims=True)
    bcum3 = c3(bcum)
    q_dec = c3(qs) * jnp.exp(bcum3)
    k_dec = c3(kk) * jnp.exp(-bcum3)
    k_upd = c3(kk) * jnp.exp(b_last - bcum3)
    v3 = c3(v)
    scores = bdot(q_dec, k_dec, "nt")
    ri = lax.broadcasted_iota(jnp.int32, scores.shape, 1)
    ci = lax.broadcasted_iota(jnp.int32, scores.shape, 2)
    scores = jnp.where(ri >= ci, scores, 0.0)
    o = bdot(scores, v3, "nn")
    upd_t = bdot(v3, k_upd, "tn")
    dec = jnp.exp(b_last)
    prev = []
    for n in range(nc):
        prev.append(st)
        st = st * dec[n] + upd_t[n]
    o = o + bdot(q_dec, jnp.stack(prev), "nt")
    o = o.reshape(rows, 128)
    o = o * lax.rsqrt(jnp.mean(o * o, axis=-1, keepdims=True) + EPS) * hn
    return (o * _sigmoid(g)).astype(BF16), st


def _ffn_conv(xg, xv, cw, cb):
    cg = _causal_conv(xg, cw[0], cb[0], FFN_CONV)[HALO:]
    cv = _causal_conv(xv, cw[1], cb[1], FFN_CONV)[HALO:]
    return cg, cv


def _ffn_gate(cg, cv):
    return jax.nn.gelu(cg) * cv


class Row:
    def __init__(self, arr, cb=None, off=0):
        self.arr, self.cb, self.off = arr, cb, off

    def spec(self, tm):
        if self.cb is None:
            return pl.BlockSpec((tm, self.arr.shape[1]), lambda j, i: (i, 0))
        off = self.off
        return pl.BlockSpec((tm, self.cb), lambda j, i: (i, j + off))


class Par:
    def __init__(self, arr, kind="full", bs=None):
        self.arr, self.kind, self.bs = arr, kind, bs

    def block(self):
        if self.kind == "full":
            return self.arr.shape
        if self.kind == "col":
            return (self.arr.shape[0], self.bs)
        return (self.bs, self.arr.shape[1])

    def spec(self):
        if self.kind == "full":
            return pl.BlockSpec(self.block(), lambda j, i: (0, 0))
        if self.kind == "col":
            return pl.BlockSpec(self.block(), lambda j, i: (0, j))
        return pl.BlockSpec(self.block(), lambda j, i: (j, 0))


class Out:
    def __init__(self, width, dtype, cb=None, off=0):
        self.width, self.dtype, self.cb, self.off = width, dtype, cb, off

    def spec(self, tm):
        if self.cb is None:
            return pl.BlockSpec((tm, self.width), lambda j, i: (i, 0))
        off = self.off
        return pl.BlockSpec((tm, self.cb), lambda j, i: (i, j + off))


def _params(sem):
    return pltpu.CompilerParams(dimension_semantics=sem, vmem_limit_bytes=VMEM_LIMIT)


def tile_fwd(name, fn, *, m, tm, nj, rows, pars, outs, n_acc=0):
    n_r, n_p, n_o = len(rows), len(pars), len(outs)

    def body(*refs):
        ins = [r[...] for r in refs[:n_r + n_p]]
        res = fn(*ins)
        o_refs = refs[n_r + n_p:]
        for k in range(n_o):
            o_refs[k][...] = res[k].astype(o_refs[k].dtype)
        first = jnp.logical_and(pl.program_id(0) == 0, pl.program_id(1) == 0)
        for k in range(n_acc):
            ref = o_refs[n_o + k]

            @pl.when(first)
            def _():
                ref[...] = jnp.zeros_like(ref)

            ref[...] += res[n_o + k]

    out_shape = [jax.ShapeDtypeStruct((m, o.width), o.dtype) for o in outs]
    out_specs = [o.spec(tm) for o in outs]
    for _ in range(n_acc):
        out_shape.append(jax.ShapeDtypeStruct((1, LANES), F32))
        out_specs.append(pl.BlockSpec((1, LANES), lambda j, i: (0, 0)))
    sem = ("arbitrary", "arbitrary") if n_acc else ("parallel", "parallel")
    return pl.pallas_call(
        body, grid=(nj, m // tm), name=name,
        in_specs=[r.spec(tm) for r in rows] + [p.spec() for p in pars],
        out_specs=out_specs, out_shape=out_shape, compiler_params=_params(sem),
    )(*[r.arr for r in rows], *[p.arr for p in pars])


def tile_bwd(name, fn, *, m, tm, nj, rows, pars, cts, drows):
    n_r, n_p, n_c = len(rows), len(pars), len(cts)
    want = [k for k in range(n_r) if drows[k] is not None]

    def body(*refs):
        ins = [r[...] for r in refs[:n_r + n_p]]
        ct = [r[...] for r in refs[n_r + n_p:n_r + n_p + n_c]]
        o_refs = refs[n_r + n_p + n_c:]
        res, vjp = jax.vjp(fn, *ins)
        grads = vjp(tuple(c.astype(r.dtype) for c, r in zip(ct, res)))
        for pos, k in enumerate(want):
            o_refs[pos][...] = grads[k].astype(o_refs[pos].dtype)
        for k in range(n_p):
            ref = o_refs[len(want) + k]
            first = pl.program_id(1) == 0
            if pars[k].kind == "full":
                first = jnp.logical_and(first, pl.program_id(0) == 0)

            @pl.when(first)
            def _():
                ref[...] = jnp.zeros_like(ref)

            ref[...] += grads[n_r + k].astype(F32)

    out_shape = [jax.ShapeDtypeStruct((m, drows[k].width), drows[k].dtype) for k in want]
    out_specs = [drows[k].spec(tm) for k in want]
    for p in pars:
        out_shape.append(jax.ShapeDtypeStruct(p.arr.shape, F32))
        out_specs.append(p.spec())
    return pl.pallas_call(
        body, grid=(nj, m // tm), name=name,
        in_specs=[r.spec(tm) for r in rows] + [p.spec() for p in pars] + [c.spec(tm) for c in cts],
        out_specs=out_specs, out_shape=out_shape, compiler_params=_params(("arbitrary", "arbitrary")),
    )(*[r.arr for r in rows], *[p.arr for p in pars], *[c.arr for c in cts])


def loss_head(name, x, y, tgt, g, *, m, tm):
    def body(x_ref, y_ref, t_ref, g_ref, dout_ref, dy_ref, loss_ref, dg_ref):
        normed, vjp = jax.vjp(_rms, y_ref[...], g_ref[...])
        err = x_ref[...] + normed - t_ref[...]
        dout = err * (1.0 / D_MODEL)
        dy, dg = vjp(dout)
        dout_ref[...] = dout
        dy_ref[...] = dy.astype(dy_ref.dtype)

        @pl.when(pl.program_id(0) == 0)
        def _():
            loss_ref[...] = jnp.zeros_like(loss_ref)
            dg_ref[...] = jnp.zeros_like(dg_ref)

        loss_ref[...] += 0.5 * jnp.sum(jnp.mean(err * err, axis=-1, keepdims=True), axis=0, keepdims=True)
        dg_ref[...] += dg

    row = pl.BlockSpec((tm, D_MODEL), lambda i: (i, 0))
    whole = lambda w: pl.BlockSpec((1, w), lambda i: (0, 0))
    return pl.pallas_call(
        body, grid=(m // tm,), name=name, in_specs=[row, row, row, whole(D_MODEL)],
        out_specs=[row, row, whole(LANES), whole(D_MODEL)],
        out_shape=[jax.ShapeDtypeStruct((m, D_MODEL), F32), jax.ShapeDtypeStruct((m, D_MODEL), BF16),
                   jax.ShapeDtypeStruct((1, LANES), F32), jax.ShapeDtypeStruct((1, D_MODEL), F32)],
        compiler_params=_params(("arbitrary",)),
    )(x, y, tgt, g)


class Blk:
    def __init__(self, arr, block, index):
        self.arr, self.block, self.index = arr, block, index

    def spec(self):
        return pl.BlockSpec(self.block, self.index)


def _flat2(v):
    return v if v.ndim == 2 else v.reshape(-1, v.shape[-1])


def mm(name, pat, a, b, o, out_dtype, grid, after=None, b_join=False, o_split=False):
    nk = grid[2]
    o_shape = o.arr

    def put(o_ref, r):
        if o_split:
            half = r.shape[1] // 2
            o_ref[0] = r[:, :half].astype(out_dtype)
            o_ref[1] = r[:, half:].astype(out_dtype)
        else:
            o_ref[...] = r.astype(out_dtype).reshape(o_ref.shape)

    def body(*refs):
        a_ref, b_ref = refs[0], refs[1]
        o_ref = refs[3] if after is not None else refs[2]
        bv = jnp.concatenate([b_ref[0], b_ref[1]], axis=1) if b_join else _flat2(b_ref[...])
        r = _dg(_flat2(a_ref[...]), bv, pat)
        if nk == 1:
            put(o_ref, r)
            return
        acc_ref = refs[-1]
        kk = pl.program_id(2)

        @pl.when(kk == 0)
        def _():
            acc_ref[...] = r

        @pl.when(kk > 0)
        def _():
            acc_ref[...] += r

        @pl.when(kk == nk - 1)
        def _():
            put(o_ref, acc_ref[...])

    ob = [d for d in o.block if d is not None]
    if o_split:
        acc_shape = (ob[1], 2 * ob[2])
    else:
        acc_shape = (ob[0], ob[1]) if len(ob) == 2 else (ob[0] * ob[1], ob[2])
    in_specs = [a.spec(), b.spec()]
    args = [a.arr, b.arr]
    if after is not None:
        in_specs.append(pl.BlockSpec(memory_space=pl.ANY))
        args.append(after)
    return pl.pallas_call(
        body, grid=grid, name=name, in_specs=in_specs, out_specs=o.spec(),
        out_shape=jax.ShapeDtypeStruct(o_shape, out_dtype),
        scratch_shapes=[pltpu.VMEM(acc_shape, F32)] if nk > 1 else [],
        compiler_params=_params(("parallel", "parallel", "arbitrary")),
    )(*args)


def _div_tile(n, cap):
    if n <= cap:
        return n
    best = 128
    for t in range(128, cap + 1, 128):
        if n % t == 0:
            best = t
    return best


def mm2d(name, pat, a, b, out_dtype=F32):
    if pat == "tn":
        k, m = a.shape
    else:
        m, k = a.shape
    n = b.shape[0] if pat == "nt" else b.shape[1]
    tm, tn, tk = _div_tile(m, 1024), _div_tile(n, 1024), _div_tile(k, 1024)
    a_blk = Blk(a, (tk, tm), lambda i, j, kk: (kk, i)) if pat == "tn" else Blk(a, (tm, tk), lambda i, j, kk: (i, kk))
    b_blk = Blk(b, (tn, tk), lambda i, j, kk: (j, kk)) if pat == "nt" else Blk(b, (tk, tn), lambda i, j, kk: (kk, j))
    o_blk = Blk((m, n), (tm, tn), lambda i, j, kk: (i, j))
    return mm(name, pat, a_blk, b_blk, o_blk, out_dtype, (m // tm, n // tn, k // tk))


def hgrn_fwd(name, z, logits, hnorm, *, n_batch, seq):
    m = n_batch * seq
    ts = min(HGRN_SEG, seq)
    n_seg = seq // ts

    def body(q_ref, f_ref, v_ref, g_ref, lg_ref, hn_ref, o_ref, sp_ref, st_ref):
        s = pl.program_id(2)

        @pl.when(s == 0)
        def _():
            st_ref[...] = jnp.zeros_like(st_ref)

        st = st_ref[...]
        sp_ref[...] = st
        o, st_new = fn_hgrn_seg(q_ref[...], f_ref[...], v_ref[...], g_ref[...], st, lg_ref[...], hn_ref[...])
        o_ref[...] = o
        st_ref[...] = st_new

    part = lambda p: pl.BlockSpec((ts, 128), lambda h, b, s: (b * n_seg + s, 4 * p + h))
    return pl.pallas_call(
        body, grid=(A_HEADS, n_batch, n_seg), name=name,
        in_specs=[part(0), part(1), part(2), part(3),
                  pl.BlockSpec((3, 128), lambda h, b, s: (0, h)),
                  pl.BlockSpec((1, 128), lambda h, b, s: (0, h))],
        out_specs=[pl.BlockSpec((ts, 128), lambda h, b, s: (b * n_seg + s, h)),
                   pl.BlockSpec((128, 128), lambda h, b, s: ((b * n_seg + s) * A_HEADS + h, 0))],
        out_shape=[jax.ShapeDtypeStruct((m, A_WIDTH), BF16),
                   jax.ShapeDtypeStruct((n_batch * n_seg * A_HEADS * 128, 128), F32)],
        scratch_shapes=[pltpu.VMEM((128, 128), F32)],
        compiler_params=_params(("arbitrary", "arbitrary", "arbitrary")),
    )(z, z, z, z, logits, hnorm)


def hgrn_bwd(name, z, sprev, logits, hnorm, do, *, n_batch, seq):
    m = n_batch * seq
    ts = min(HGRN_SEG, seq)
    n_seg = seq // ts

    def body(q_ref, f_ref, v_ref, g_ref, sp_ref, lg_ref, hn_ref, do_ref, dq_ref, df_ref, dv_ref, dg_ref, dlg_ref, dhn_ref, dst_ref):
        s = pl.program_id(2)

        @pl.when(s == 0)
        def _():
            dst_ref[...] = jnp.zeros_like(dst_ref)

        res, vjp = jax.vjp(fn_hgrn_seg, q_ref[...], f_ref[...], v_ref[...], g_ref[...], sp_ref[...], lg_ref[...], hn_ref[...])
        dq, df, dv, dg, dst, dlg, dhn = vjp((do_ref[...].astype(res[0].dtype), dst_ref[...]))
        dq_ref[...] = dq.astype(dq_ref.dtype)
        df_ref[...] = df.astype(df_ref.dtype)
        dv_ref[...] = dv.astype(dv_ref.dtype)
        dg_ref[...] = dg.astype(dg_ref.dtype)
        dst_ref[...] = dst
        first = jnp.logical_and(pl.program_id(1) == 0, s == 0)

        @pl.when(first)
        def _():
            dlg_ref[...] = jnp.zeros_like(dlg_ref)
            dhn_ref[...] = jnp.zeros_like(dhn_ref)

        dlg_ref[...] += dlg
        dhn_ref[...] += dhn

    rev = lambda b, s: b * n_seg + (n_seg - 1 - s)
    part = lambda p: pl.BlockSpec((ts, 128), lambda h, b, s: (rev(b, s), 4 * p + h))
    head = pl.BlockSpec((ts, 128), lambda h, b, s: (rev(b, s), h))
    dpart = jax.ShapeDtypeStruct((m, A_WIDTH), BF16)
    return pl.pallas_call(
        body, grid=(A_HEADS, n_batch, n_seg), name=name,
        in_specs=[part(0), part(1), part(2), part(3),
                  pl.BlockSpec((128, 128), lambda h, b, s: (rev(b, s) * A_HEADS + h, 0)),
                  pl.BlockSpec((3, 128), lambda h, b, s: (0, h)),
                  pl.BlockSpec((1, 128), lambda h, b, s: (0, h)),
                  head],
        out_specs=[head, head, head, head,
                   pl.BlockSpec((3, 128), lambda h, b, s: (0, h)),
                   pl.BlockSpec((1, 128), lambda h, b, s: (0, h))],
        out_shape=[dpart, dpart, dpart, dpart,
                   jax.ShapeDtypeStruct(logits.shape, F32),
                   jax.ShapeDtypeStruct(hnorm.shape, F32)],
        scratch_shapes=[pltpu.VMEM((128, 128), F32)],
        compiler_params=_params(("arbitrary", "arbitrary", "arbitrary")),
    )(z, z, z, z, sprev, logits, hnorm, do)


FFN_ROWS = 128
FFN_LANES = 128


def _ffn_tiles(m, seq):
    tm = min(512, seq)
    return tm, seq // tm, m // tm


def ffn_mid_fwd(name, hid, cw, cb, layer, *, m, seq):
    tm, n_t, n_i = _ffn_tiles(m, seq)
    hb = tm // HALO

    def body(x_ref, xb_ref, cw_ref, cb_ref, o_ref, c_ref):
        first = pl.program_id(1) % n_t == 0
        before = jnp.where(first, 0.0, xb_ref[...])
        ext = jnp.concatenate([before, x_ref[...]], axis=1)
        cg, cv = _ffn_conv(ext[0], ext[1], cw_ref[...], cb_ref[...])
        o_ref[...] = _ffn_gate(cg, cv).astype(o_ref.dtype)
        c_ref[0] = cg.astype(c_ref.dtype)
        c_ref[1] = cv.astype(c_ref.dtype)

    return pl.pallas_call(
        body, grid=(N_DEV // 2, n_i), name=name,
        in_specs=[pl.BlockSpec((2, None, tm, FF_BLK), lambda d, i: (0, d, i, 0)),
                  pl.BlockSpec((2, None, HALO, FF_BLK), lambda d, i: (0, d, jnp.maximum(i * hb - 1, 0), 0)),
                  pl.BlockSpec((2, None, None, FFN_CONV, FF_BLK), lambda d, i: (0, d, layer, 0, 0)),
                  pl.BlockSpec((None, 2, None, 1, FF_BLK), lambda d, i: (layer, 0, d, 0, 0))],
        out_specs=[pl.BlockSpec((None, tm, FF_BLK), lambda d, i: (d, i, 0)),
                   pl.BlockSpec((2, None, tm, FF_BLK), lambda d, i: (0, d, i, 0))],
        out_shape=[jax.ShapeDtypeStruct((N_DEV // 2, m, FF_BLK), BF16),
                   jax.ShapeDtypeStruct((2, N_DEV // 2, m, FF_BLK), BF16)],
        compiler_params=_params(("parallel", "parallel")),
    )(hid, hid, cw, cb)


def ffn_mid_bwd(name, hid, conv, cw, dact, layer, *, m, seq):
    tm, n_t, n_i = _ffn_tiles(m, seq)
    hb = tm // HALO
    last_blk = m // HALO - 1

    rc = min(FFN_ROWS, tm)
    lane_chunks = [(l0, min(FFN_LANES, FF_BLK - l0)) for l0 in range(0, FF_BLK, FFN_LANES)]

    def body(x_ref, c_ref, ca_ref, cw_ref, da_ref, daa_ref, dx_ref, dcw_ref, dcb_ref, cext_ref, dext_ref):
        i = pl.program_id(1)
        last = i % n_t == n_t - 1
        cext_ref[:, :tm] = c_ref[...]
        cext_ref[:, tm:] = ca_ref[...]
        dext_ref[:tm] = da_ref[...]
        dext_ref[tm:] = jnp.where(last, jnp.zeros_like(daa_ref[...]), daa_ref[...])

        @pl.when(i == 0)
        def _():
            dcw_ref[...] = jnp.zeros_like(dcw_ref)
            dcb_ref[...] = jnp.zeros_like(dcb_ref)

        for l0, lw in lane_chunks:
            lanes = slice(l0, l0 + lw)

            def chunk(c, sums, lanes=lanes, lw=lw):
                r0 = pl.multiple_of(c * rc, rc)
                ext = pl.ds(r0, rc + HALO)
                cg, cv = cext_ref[0, ext, lanes].astype(F32), cext_ref[1, ext, lanes].astype(F32)
                _, vjp_gate = jax.vjp(_ffn_gate, cg, cv)
                dconv = vjp_gate(dext_ref[ext, lanes].astype(F32))
                out = []
                for half in range(2):
                    x = x_ref[half, pl.ds(r0, rc), lanes]
                    dx = None
                    for k in range(FFN_CONV):
                        s = FFN_CONV - 1 - k
                        dc_s = _shift_raw(dconv[half], s, True, 0.0)[:rc]
                        term = cw_ref[half, k:k + 1, lanes] * dc_s
                        dx = term if dx is None else dx + term
                        out.append(sums[len(out)] + jnp.sum(x * dc_s, axis=0, keepdims=True))
                    out.append(sums[len(out)] + jnp.sum(dconv[half][:rc], axis=0, keepdims=True))
                    dx_ref[half, pl.ds(r0, rc), lanes] = dx.astype(dx_ref.dtype)
                return tuple(out)

            zero = jnp.zeros((1, lw), F32)
            sums = lax.fori_loop(0, tm // rc, chunk, (zero,) * (2 * (FFN_CONV + 1)))
            for half in range(2):
                base = half * (FFN_CONV + 1)
                for k in range(FFN_CONV):
                    dcw_ref[half, k:k + 1, lanes] += sums[base + k]
                dcb_ref[half, :, lanes] += sums[base + FFN_CONV]

    return pl.pallas_call(
        body, grid=(N_DEV // 2, n_i), name=name,
        in_specs=[pl.BlockSpec((2, None, tm, FF_BLK), lambda d, i: (0, d, i, 0)),
                  pl.BlockSpec((2, None, tm, FF_BLK), lambda d, i: (0, d, i, 0)),
                  pl.BlockSpec((2, None, HALO, FF_BLK), lambda d, i: (0, d, jnp.minimum((i + 1) * hb, last_blk), 0)),
                  pl.BlockSpec((2, None, None, FFN_CONV, FF_BLK), lambda d, i: (0, d, layer, 0, 0)),
                  pl.BlockSpec((None, tm, FF_BLK), lambda d, i: (d, i, 0)),
                  pl.BlockSpec((None, HALO, FF_BLK), lambda d, i: (d, jnp.minimum((i + 1) * hb, last_blk), 0))],
        out_specs=[pl.BlockSpec((2, None, tm, FF_BLK), lambda d, i: (0, d, i, 0)),
                   pl.BlockSpec((2, None, FFN_CONV, FF_BLK), lambda d, i: (0, d, 0, 0)),
                   pl.BlockSpec((2, None, 1, FF_BLK), lambda d, i: (0, d, 0, 0))],
        out_shape=[jax.ShapeDtypeStruct((2, N_DEV // 2, m, FF_BLK), BF16),
                   jax.ShapeDtypeStruct((2, N_DEV // 2, FFN_CONV, FF_BLK), F32),
                   jax.ShapeDtypeStruct((2, N_DEV // 2, 1, FF_BLK), F32)],
        scratch_shapes=[pltpu.VMEM((2, tm + HALO, FF_BLK), BF16), pltpu.VMEM((tm + HALO, FF_BLK), BF16)],
        compiler_params=_params(("arbitrary", "arbitrary")),
    )(hid, conv, conv, cw, dact, dact)


ATT_BLK = 512
ATT_BLK_FWD = 1024
N_PAIR = C_HEADS // 2
TERM_W = C_HEADS * LANES


def term_placement():
    import numpy as np
    place = np.zeros((3, LANES, TERM_W), np.float32)
    ones_q = np.zeros((1, TERM_W), np.float32)
    ones_k = np.zeros((1, TERM_W), np.float32)
    for h in range(C_HEADS):
        for j in range(3):
            place[j, h, h * LANES + C_HEAD_DIM + j] = 1.0
            ones_q[0, h * LANES + C_HEAD_DIM + 3 + j] = 1.0
            ones_k[0, h * LANES + C_HEAD_DIM + j] = 1.0
    return (jnp.asarray(place.reshape(3 * LANES, TERM_W), BF16), jnp.asarray(ones_q, F32), jnp.asarray(ones_k, F32))


def fn_fox_terms(c, place, ones_q, ones_k):
    parts = _split3(c)
    placed = sum(_dg(parts[j], place[j * LANES:(j + 1) * LANES], "nn") for j in range(3))
    return (placed + ones_q).astype(BF16), (ones_k - pltpu.roll(placed, 3, 1)).astype(BF16)


def _head_tile(z, terms, e):
    lane = lax.broadcasted_iota(jnp.int32, z.shape, 1)
    base = z if e == 0 else pltpu.roll(z, C_HEAD_DIM, 1)
    return jnp.where(lane < C_HEAD_DIM, base, terms.astype(z.dtype))


def _head_only(z, e):
    lane = lax.broadcasted_iota(jnp.int32, z.shape, 1)
    mine = (lane < C_HEAD_DIM) if e == 0 else (lane >= C_HEAD_DIM)
    return jnp.where(mine, z, jnp.zeros_like(z)).astype(BF16)


def _pair_tile(a0, a1):
    lane = lax.broadcasted_iota(jnp.int32, a0.shape, 1)
    return jnp.where(lane < C_HEAD_DIM, a0, pltpu.roll(a1, C_HEAD_DIM, 1))


def _lane_col(a, k):
    lane = lax.broadcasted_iota(jnp.int32, a.shape, 1)
    return jnp.sum(jnp.where(lane == k, a, 0.0), axis=1, keepdims=True)


def _causal(s):
    key = lax.broadcasted_iota(jnp.int32, s.shape, 0)
    qry = lax.broadcasted_iota(jnp.int32, s.shape, 1)
    return qry >= key


def fox_pair_fwd(name, z, qterm, kterm, *, n_batch, seq):
    m = n_batch * seq
    blk = min(ATT_BLK_FWD, seq)
    nq = seq // blk
    dh = C_HEAD_DIM

    def body(zq_ref, zk_ref, zv_ref, qt_ref, kt_ref, o_ref, lse_ref, ka_ref, vt_ref):
        qi = pl.program_id(2)

        @pl.when(qi == 0)
        def _():
            zk = zk_ref[...]
            for e in range(2):
                ka_ref[e] = _head_tile(zk, kt_ref[:, e * LANES:(e + 1) * LANES], e).astype(BF16)
            for cb in range(nq):
                vt_ref[cb] = zv_ref[cb * blk:(cb + 1) * blk, :].T.astype(BF16)

        zq = zq_ref[...] * dh ** -0.5
        qa = [_head_tile(zq, qt_ref[:, e * LANES:(e + 1) * LANES], e).astype(BF16) for e in range(2)]

        def block(j, carry, diagonal):
            rows = pl.ds(pl.multiple_of(j * blk, blk), blk)
            out = []
            for e in range(2):
                mx, l, acc = carry[e]
                s = _dg(ka_ref[e, rows, :], qa[e], "nt")
                if diagonal:
                    s = jnp.where(_causal(s), s, NEG)
                mx_new = jnp.maximum(mx, jnp.max(s, axis=0, keepdims=True))
                p = jnp.exp(s - mx_new)
                alpha = jnp.exp(mx - mx_new)
                l = alpha * l + jnp.sum(p, axis=0, keepdims=True)
                acc = alpha * acc + _dg(vt_ref[j, e * dh:(e + 1) * dh, :], p, "nn")
                out.append((mx_new, l, acc))
            return tuple(out)

        one = (jnp.full((1, blk), NEG, F32), jnp.zeros((1, blk), F32), jnp.zeros((dh, blk), F32))
        carry = lax.fori_loop(0, qi, lambda j, cr: block(j, cr, False), (one, one))
        res = block(qi, carry, True)
        ot = jnp.concatenate([res[e][2] / res[e][1] for e in range(2)], axis=0)
        o_ref[...] = ot.T.astype(o_ref.dtype)
        for e in range(2):
            lse_ref[e] = res[e][0] + jnp.log(res[e][1])

    col = lambda part: (lambda b, g, i: (b, part * N_PAIR + g))
    return pl.pallas_call(
        body, grid=(n_batch, N_PAIR, nq), name=name,
        in_specs=[pl.BlockSpec((blk, LANES), lambda b, g, i: (b * nq + i, g)),
                  pl.BlockSpec((seq, LANES), col(1)),
                  pl.BlockSpec((seq, LANES), col(2)),
                  pl.BlockSpec((blk, 2 * LANES), lambda b, g, i: (b * nq + i, g)),
                  pl.BlockSpec((seq, 2 * LANES), lambda b, g, i: (b, g))],
        out_specs=[pl.BlockSpec((blk, LANES), lambda b, g, i: (b * nq + i, g)),
                   pl.BlockSpec((None, None, None, 2, 1, blk), lambda b, g, i: (b, g, i, 0, 0, 0))],
        out_shape=[jax.ShapeDtypeStruct((m, D_MODEL), BF16), jax.ShapeDtypeStruct((n_batch, N_PAIR, nq, 2, 1, blk), F32)],
        scratch_shapes=[pltpu.VMEM((2, seq, LANES), BF16), pltpu.VMEM((nq, LANES, blk), BF16)],
        compiler_params=_params(("parallel", "parallel", "arbitrary")),
    )(z, z, z, qterm, kterm)


def fox_pair_bwd(name, z, qterm, kterm, o, do, lse, *, n_batch, seq):
    m = n_batch * seq
    blk = min(ATT_BLK, seq)
    nq = seq // blk
    dh = C_HEAD_DIM

    def body(zq_ref, zk_ref, zv_ref, qt_ref, kt_ref, o_ref, do_ref, lse_ref, dq_ref, dk_ref, dv_ref, dc_ref,
             qa_ref, doh_ref, del_ref, dqt_ref, dk_acc, dv_acc):
        g, j = pl.program_id(1), pl.program_id(2)
        lane = lax.broadcasted_iota(jnp.int32, (blk, LANES), 1)

        @pl.when(jnp.logical_and(g == 0, j == 0))
        def _():
            dc_ref[...] = jnp.zeros_like(dc_ref)

        @pl.when(j == 0)
        def _():
            zq = zq_ref[...] * dh ** -0.5
            dov = do_ref[...]
            for e in range(2):
                qa_ref[e] = _head_tile(zq, qt_ref[:, e * LANES:(e + 1) * LANES], e).astype(BF16)
                doh_ref[e] = _head_only(dov, e)
            for cb in range(nq):
                rows = slice(cb * blk, (cb + 1) * blk)
                prod_t = (do_ref[rows, :].astype(F32) * o_ref[rows, :].astype(F32)).T
                for e in range(2):
                    del_ref[cb, e] = jnp.sum(prod_t[e * dh:(e + 1) * dh], axis=0, keepdims=True)
            dqt_ref[...] = jnp.zeros_like(dqt_ref)

        zk, zv = zk_ref[...], zv_ref[...]
        ka32 = [_head_tile(zk, kt_ref[:, e * LANES:(e + 1) * LANES], e) for e in range(2)]
        ka = [t.astype(BF16) for t in ka32]
        kat = [t.T.astype(BF16) for t in ka32]
        vh = [_head_only(zv, e) for e in range(2)]
        dk_acc[...] = jnp.zeros_like(dk_acc)
        dv_acc[...] = jnp.zeros_like(dv_acc)

        def block(i, diagonal):
            rows = pl.ds(pl.multiple_of(i * blk, blk), blk)
            for e in range(2):
                qv, dov = qa_ref[e, rows, :], doh_ref[e, rows, :]
                p = jnp.exp(_dg(ka[e], qv, "nt") - lse_ref[i, e])
                if diagonal:
                    p = jnp.where(_causal(p), p, 0.0)
                dv_acc[...] += _dg(p, dov, "nn")
                ds = p * (_dg(vh[e], dov, "nt") - del_ref[i, e])
                dk_acc[e] += _dg(ds, qv, "nn")
                dqt_ref[i, e] += _dg(kat[e], ds, "nn")

        block(j, True)

        def rest(i, carry):
            block(i, False)
            return carry

        lax.fori_loop(j + 1, nq, rest, 0)
        dk0, dk1 = dk_acc[0], dk_acc[1]
        dk_ref[...] = _pair_tile(dk0, dk1).astype(dk_ref.dtype)
        dv_ref[...] = dv_acc[...].astype(dv_ref.dtype)
        rows_j = pl.ds(pl.multiple_of(j * blk, blk), blk)
        for e, dke in enumerate((dk0, dk1)):
            dc_ref[rows_j, :] -= jnp.where(lane == 2 * g + e, _lane_col(dke, dh + 3), 0.0)

        @pl.when(j == nq - 1)
        def _():
            for i in range(nq):
                nat = [dqt_ref[i, e].T for e in range(2)]
                rows = slice(i * blk, (i + 1) * blk)
                dq_ref[rows, :] = (_pair_tile(nat[0], nat[1]) * dh ** -0.5).astype(dq_ref.dtype)
                for e in range(2):
                    dc_ref[rows, :] += jnp.where(lane == 2 * g + e, _lane_col(nat[e], dh), 0.0)

    col = lambda part: (lambda b, g, j: (b, part * N_PAIR + g))
    colj = lambda part: (lambda b, g, j: (b * nq + j, part * N_PAIR + g))
    pair = jax.ShapeDtypeStruct((m, D_MODEL), BF16)
    return pl.pallas_call(
        body, grid=(n_batch, N_PAIR, nq), name=name,
        in_specs=[pl.BlockSpec((seq, LANES), col(0)),
                  pl.BlockSpec((blk, LANES), colj(1)),
                  pl.BlockSpec((blk, LANES), colj(2)),
                  pl.BlockSpec((seq, 2 * LANES), lambda b, g, j: (b, g)),
                  pl.BlockSpec((blk, 2 * LANES), lambda b, g, j: (b * nq + j, g)),
                  pl.BlockSpec((seq, LANES), col(0)),
                  pl.BlockSpec((seq, LANES), col(0)),
                  pl.BlockSpec((None, None, nq, 2, 1, blk), lambda b, g, j: (b, g, 0, 0, 0, 0))],
        out_specs=[pl.BlockSpec((seq, LANES), col(0)),
                   pl.BlockSpec((blk, LANES), colj(0)),
                   pl.BlockSpec((blk, LANES), colj(0)),
                   pl.BlockSpec((seq, LANES), lambda b, g, j: (b, 0))],
        out_shape=[pair, pair, pair, jax.ShapeDtypeStruct((m, LANES), F32)],
        scratch_shapes=[pltpu.VMEM((2, seq, LANES), BF16), pltpu.VMEM((2, seq, LANES), BF16),
                        pltpu.VMEM((nq, 2, 1, blk), F32), pltpu.VMEM((nq, 2, LANES, blk), F32),
                        pltpu.VMEM((2, blk, LANES), F32), pltpu.VMEM((blk, LANES), F32)],
        compiler_params=_params(("arbitrary", "arbitrary", "arbitrary")),
    )(z, z, z, qterm, kterm, o, do, lse)


def _split3(c):
    c1 = c.astype(BF16)
    r1 = c - c1.astype(F32)
    c2 = r1.astype(BF16)
    c3 = (r1 - c2.astype(F32)).astype(BF16)
    return c1, c2, c3


def _mesh_pos():
    return lax.axis_index("x"), lax.axis_index("y"), lax.axis_index("c")


def _flip(v, bit):
    return 1 - v if bit else v


def all_gather(name, blocks):
    n = len(blocks)

    def body(*refs):
        x_refs, out_refs = refs[:n], refs[n:2 * n]
        send_sems, recv_sems, local_sems = refs[2 * n:]
        x, y, c = _mesh_pos()
        me, sibling = (x, y, c), (x, y, 1 - c)
        chips = [(1 - x, y), (x, 1 - y), (1 - x, 1 - y)]

        def slot(a, px, py, pc):
            return out_refs[a].at[4 * px + 2 * py + pc]

        def copy(a, k, blk, to, src=None):
            return pltpu.make_async_remote_copy(
                src_ref=slot(a, *blk) if src is None else src, dst_ref=slot(a, *blk),
                send_sem=send_sems.at[a, k], recv_sem=recv_sems.at[a, k], device_id=to, device_id_type=MESH)

        mine = [pltpu.make_async_copy(x_refs[a], slot(a, *me), local_sems.at[a]) for a in range(n)]
        for cp in mine:
            cp.start()
        sends = []
        for a in range(n):
            sends.append(copy(a, 0, me, sibling, src=x_refs[a]))
            sends += [copy(a, 1 + j, me, (*chip, c), src=x_refs[a]) for j, chip in enumerate(chips)]
        for cp in sends:
            cp.start()
        for j, chip in enumerate(chips):
            for a in range(n):
                copy(a, 1 + j, (*chip, c), me).wait_recv()
                passed = copy(a, 4 + j, (*chip, c), sibling)
                passed.start()
                sends.append(passed)
        for a in range(n):
            copy(a, 0, sibling, me).wait_recv()
            for j, chip in enumerate(chips):
                copy(a, 4 + j, (*chip, 1 - c), me).wait_recv()
        for cp in sends:
            cp.wait_send()
        for cp in mine:
            cp.wait()

    hbm = pl.BlockSpec(memory_space=pl.ANY)
    return pl.pallas_call(
        body, name=name, out_shape=[jax.ShapeDtypeStruct((N_DEV,) + b.shape, b.dtype) for b in blocks],
        in_specs=[hbm] * n, out_specs=[hbm] * n,
        scratch_shapes=[pltpu.SemaphoreType.DMA((n, 7)), pltpu.SemaphoreType.DMA((n, 7)), pltpu.SemaphoreType.DMA((n,))],
    )(*blocks)


def _peers(x, y, c):
    return [(_flip(x, k & 4), _flip(y, k & 2), _flip(c, k & 1)) for k in range(1, N_DEV)]


def gather_start(name, blocks, lands):
    n = len(blocks)

    def body(*refs):
        x_refs, land_refs = refs[:n], refs[n:2 * n]
        send_sems, recv_sems = refs[2 * n], refs[2 * n + 1]
        token = refs[-1]
        x, y, c = _mesh_pos()
        me = 4 * x + 2 * y + c
        for k, peer in enumerate(_peers(x, y, c)):
            for a in range(n):
                pltpu.make_async_remote_copy(
                    src_ref=x_refs[a], dst_ref=land_refs[a].at[me], send_sem=send_sems.at[7 * a + k], recv_sem=recv_sems.at[7 * a + k],
                    device_id=peer, device_id_type=MESH).start()
        token[...] = jnp.zeros_like(token)

    hbm = pl.BlockSpec(memory_space=pltpu.HBM)
    sem = pl.BlockSpec(memory_space=pltpu.SEMAPHORE)
    out_shape = ([pltpu.SemaphoreType.DMA((7 * n,)), pltpu.SemaphoreType.DMA((7 * n,))]
                 + [pltpu.HBM(b.shape, b.dtype) for b in blocks] + [pltpu.HBM(l.shape, l.dtype) for l in lands]
                 + [jax.ShapeDtypeStruct((8, LANES), F32)])
    res = pl.pallas_call(
        body, name=name, out_shape=out_shape, in_specs=[hbm] * (2 * n),
        out_specs=[sem, sem] + [hbm] * (2 * n) + [pl.BlockSpec(memory_space=pltpu.VMEM)],
        input_output_aliases={a: 2 + a for a in range(2 * n)},
        compiler_params=pltpu.CompilerParams(has_side_effects=pltpu.SideEffectType.DATAFLOW_SIDE_EFFECTING),
    )(*[pltpu.with_memory_space_constraint(b, pltpu.HBM) for b in blocks],
      *[pltpu.with_memory_space_constraint(l, pltpu.HBM) for l in lands])
    return res[0], res[1], res[2:2 + n], res[2 + n:2 + 2 * n], res[-1]


def gather_wait(name, send_sems, recv_sems, blocks, lands, after):
    n = len(blocks)

    def body(*refs):
        x_refs, land_refs = refs[:n], refs[n:2 * n]
        s_sems, r_sems = refs[2 * n], refs[2 * n + 1]
        x, y, c = _mesh_pos()
        me = 4 * x + 2 * y + c
        for k, peer in enumerate(_peers(x, y, c)):
            for a in range(n):
                cp = pltpu.make_async_remote_copy(
                    src_ref=x_refs[a], dst_ref=land_refs[a].at[me], send_sem=s_sems.at[7 * a + k], recv_sem=r_sems.at[7 * a + k],
                    device_id=peer, device_id_type=MESH)
                cp.wait_send()
                cp.wait_recv()

    hbm = pl.BlockSpec(memory_space=pltpu.HBM)
    sem = pl.BlockSpec(memory_space=pltpu.SEMAPHORE)
    res = pl.pallas_call(
        body, name=name,
        out_shape=[pltpu.HBM(b.shape, b.dtype) for b in blocks] + [pltpu.HBM(l.shape, l.dtype) for l in lands],
        in_specs=[hbm] * (2 * n) + [sem, sem, pl.BlockSpec(memory_space=pl.ANY)], out_specs=[hbm] * (2 * n),
        input_output_aliases={a: a for a in range(2 * n)},
        compiler_params=pltpu.CompilerParams(has_side_effects=pltpu.SideEffectType.DATAFLOW_SIDE_EFFECTING),
    )(*blocks, *lands, send_sems, recv_sems, after)
    return res[n:]


def _split_exchange(name, sends, lands, sems, after):
    n = len(sends)
    starting = sems is None

    def body(*refs):
        s_refs, l_refs = refs[:n], refs[n:2 * n]
        send_sems, recv_sems = refs[2 * n], refs[2 * n + 1]
        x, y, c = _mesh_pos()
        me = 4 * x + 2 * y + c
        for k, (px, py, pc) in enumerate(_peers(x, y, c)):
            for a in range(n):
                cp = pltpu.make_async_remote_copy(
                    src_ref=s_refs[a].at[4 * px + 2 * py + pc], dst_ref=l_refs[a].at[me],
                    send_sem=send_sems.at[7 * a + k], recv_sem=recv_sems.at[7 * a + k],
                    device_id=(px, py, pc), device_id_type=MESH)
                if starting:
                    cp.start()
                else:
                    cp.wait_send()
                    cp.wait_recv()
        if starting:
            refs[-1][...] = jnp.zeros_like(refs[-1])

    hbm = pl.BlockSpec(memory_space=pltpu.HBM)
    sem = pl.BlockSpec(memory_space=pltpu.SEMAPHORE)
    thru = [pltpu.HBM(t.shape, t.dtype) for t in list(sends) + list(lands)]
    effect = pltpu.CompilerParams(has_side_effects=pltpu.SideEffectType.DATAFLOW_SIDE_EFFECTING)
    if starting:
        res = pl.pallas_call(
            body, name=name, in_specs=[hbm] * (2 * n),
            out_shape=[pltpu.SemaphoreType.DMA((7 * n,)), pltpu.SemaphoreType.DMA((7 * n,))] + thru + [jax.ShapeDtypeStruct((8, LANES), F32)],
            out_specs=[sem, sem] + [hbm] * (2 * n) + [pl.BlockSpec(memory_space=pltpu.VMEM)],
            input_output_aliases={a: 2 + a for a in range(2 * n)}, compiler_params=effect,
        )(*[pltpu.with_memory_space_constraint(t, pltpu.HBM) for t in list(sends) + list(lands)])
        return res[0], res[1], res[2:2 + n], res[2 + n:2 + 2 * n], res[-1]
    res = pl.pallas_call(
        body, name=name, out_shape=thru, in_specs=[hbm] * (2 * n) + [sem, sem, pl.BlockSpec(memory_space=pl.ANY)],
        out_specs=[hbm] * (2 * n), input_output_aliases={a: a for a in range(2 * n)}, compiler_params=effect,
    )(*sends, *lands, sems[0], sems[1], after)
    return res[n:]


def own_slot_only(send, me):
    mine = lax.dynamic_index_in_dim(send, me, 0, keepdims=False)
    return lax.dynamic_update_index_in_dim(lax.empty(send.shape, send.dtype), mine, me, 0)


def all_to_all(name, sends):
    n = len(sends)

    def body(*refs):
        s_refs, r_refs = refs[:n], refs[n:2 * n]
        send_sems, recv_sems, local_sems = refs[2 * n:]
        x, y, c = _mesh_pos()
        me = 4 * x + 2 * y + c
        mine = [pltpu.make_async_copy(s_refs[a].at[me], r_refs[a].at[me], local_sems.at[a]) for a in range(n)]
        for cp in mine:
            cp.start()
        copies = []
        for k in range(1, N_DEV):
            px, py, pc = _flip(x, k & 4), _flip(y, k & 2), _flip(c, k & 1)
            for a in range(n):
                copies.append(pltpu.make_async_remote_copy(
                    src_ref=s_refs[a].at[4 * px + 2 * py + pc], dst_ref=r_refs[a].at[me],
                    send_sem=send_sems.at[a, k - 1], recv_sem=recv_sems.at[a, k - 1],
                    device_id=(px, py, pc), device_id_type=MESH))
        for cp in copies:
            cp.start()
        for cp in copies:
            cp.wait_recv()
        for cp in copies:
            cp.wait_send()
        for cp in mine:
            cp.wait()

    hbm = pl.BlockSpec(memory_space=pl.ANY)
    return pl.pallas_call(
        body, name=name, out_shape=[jax.ShapeDtypeStruct(s.shape, s.dtype) for s in sends],
        in_specs=[hbm] * n, out_specs=[hbm] * n,
        scratch_shapes=[pltpu.SemaphoreType.DMA((n, 7)), pltpu.SemaphoreType.DMA((n, 7)), pltpu.SemaphoreType.DMA((n,))],
    )(*sends)


def _row_tile(r, cap, step):
    return next((t for t in range(cap, step - 1, -step) if r % t == 0), r)


def _sum_parts(p, n):
    t = [p[k].astype(F32) for k in range(n)]
    while len(t) > 1:
        t = [t[k] + t[k + 1] for k in range(0, len(t), 2)]
    return t[0]


def _adam(g, w, m, v):
    m = ADAM_B1 * m + (1.0 - ADAM_B1) * g
    v = ADAM_B2 * v + (1.0 - ADAM_B2) * (g * g)
    m_hat = m / (1.0 - ADAM_B1 ** ADAM_STEP)
    v_hat = v / (1.0 - ADAM_B2 ** ADAM_STEP)
    return -ADAM_LR * (m_hat / (jnp.sqrt(v_hat) + ADAM_EPS) + ADAM_WD * w), m, v


def adam_tiled(name, partials, w, m_, v_, layer=0, prev=None):
    _, r, c = w.shape
    n_part = partials.shape[0]
    tr = _row_tile(r, 256, 16)

    def body(*refs):
        p_ref, w_ref, m_ref, v_ref = refs[:4]
        g_ref, d_ref, nm_ref, nv_ref = refs[-4:]
        g = _sum_parts(p_ref, n_part)
        g_ref[...] = g
        d_ref[...], nm_ref[...], nv_ref[...] = _adam(g, w_ref[...], m_ref[...], v_ref[...])

    spec = pl.BlockSpec((None, tr, c), lambda i: (layer, i, 0))
    in_specs = [pl.BlockSpec((n_part, None, tr, c), lambda i: (0, 0, i, 0)), spec, spec, spec]
    args = [partials, w, m_, v_]
    aliases = {}
    if prev is not None:
        in_specs += [pl.BlockSpec(memory_space=pl.ANY)] * 4
        args += list(prev)
        aliases = {4 + k: k for k in range(4)}
    return pl.pallas_call(
        body, grid=(r // tr,), name=name, in_specs=in_specs,
        out_specs=[spec] * 4, out_shape=[jax.ShapeDtypeStruct(w.shape, F32)] * 4,
        input_output_aliases=aliases, compiler_params=_params(("parallel",)),
    )(*args)


def adam_small(name, items, extra):
    n, ne = len(items), len(extra)

    def body(*refs):
        ins, outs = refs[:4 * n + ne], refs[4 * n + ne:]
        for a in range(n):
            p_ref, w_ref, m_ref, v_ref = ins[4 * a:4 * a + 4]
            g = _sum_parts(p_ref, N_DEV)
            outs[4 * a][...] = g
            outs[4 * a + 1][...], outs[4 * a + 2][...], outs[4 * a + 3][...] = _adam(g, w_ref[...], m_ref[...], v_ref[...])
        for e in range(ne):
            outs[4 * n + e][...] = _sum_parts(ins[4 * n + e], N_DEV)

    args, out_shape = [], []
    for p, w, m_, v_ in items:
        args += [p, w, m_, v_]
        out_shape += [jax.ShapeDtypeStruct(w.shape, F32)] * 4
    for e in extra:
        args.append(e)
        out_shape.append(jax.ShapeDtypeStruct(e.shape[1:], F32))
    vmem = pl.BlockSpec(memory_space=pltpu.VMEM)
    res = pl.pallas_call(body, name=name, in_specs=[vmem] * len(args), out_specs=[vmem] * len(out_shape), out_shape=out_shape)(*args)
    return [res[4 * a:4 * a + 4] for a in range(n)], res[4 * n:]


def _cols_from_gather(g):
    g = jnp.moveaxis(g, 0, -2)
    return g.reshape(g.shape[:-2] + (g.shape[-2] * g.shape[-1],))


def _cols_to_blocks(w):
    w = w.reshape(w.shape[:-1] + (N_DEV, w.shape[-1] // N_DEV))
    return jnp.moveaxis(w, -2, 0)


def _block_diag(w):
    z = jnp.zeros((B_BLOCK_DIM, B_BLOCK_DIM), w.dtype)
    rows = []
    for j in range(B_BLOCKS // 2):
        top = jnp.concatenate([w[2 * j], z], axis=1)
        bot = jnp.concatenate([z, w[2 * j + 1]], axis=1)
        rows.append(jnp.concatenate([top, bot], axis=0))
    return jnp.concatenate(rows, axis=0)


def _block_diag_grad(d):
    out = []
    for j in range(B_BLOCKS // 2):
        blk = d[128 * j:128 * (j + 1)]
        out.append(blk[:64, :64])
        out.append(blk[64:, 64:])
    return jnp.stack(out)


NAMES = ("norm_gains", "even_w_in", "hgrn_lb_logits", "hgrn_norm", "rg_conv_w", "rg_conv_b", "rg_wa", "rg_ba", "rg_wx", "rg_bx",
         "rg_lambda", "even_w_out", "odd_w_in", "fox_f_bias", "odd_w_out", "ffn_w_up", "ffn_conv_w", "ffn_conv_b", "ffn_w_down")
SMALL_SHARDED = ("norm_gains", "rg_conv_w", "ffn_conv_w")
REPLICATED = ("hgrn_lb_logits", "hgrn_norm", "rg_conv_b", "rg_wa", "rg_ba", "rg_wx", "rg_bx", "rg_lambda", "fox_f_bias", "ffn_conv_b")


def _ffn_forward(tag, layer, h, w_up_g, cw5, cb5, w_down_g, m, seq):
    tm = _div_tile(m, 1024)
    nm = m // tm
    hid = mm(f"{tag}_up", "nn",
             Blk(h, (tm, D_MODEL), lambda i, j, k: (i, 0)),
             Blk(w_up_g, (None, None, D_MODEL, FF_BLK), lambda i, j, k: (j, 0, 0, 0)),
             Blk((N_DEV, m, FF_BLK), (None, tm, FF_BLK), lambda i, j, k: (j, i, 0)), F32, (nm, N_DEV, 1))
    hid = hid.reshape(2, N_DEV // 2, m, FF_BLK)
    act, conv = ffn_mid_fwd(f"{tag}_mid", hid, cw5, cb5, layer, m=m, seq=seq)
    f = mm(f"{tag}_down", "nn",
           Blk(act, (None, tm, FF_BLK), lambda i, j, k: (k, i, 0)),
           Blk(w_down_g, (2, None, FF_BLK // 2, D_MODEL), lambda i, j, k: (k, 0, 0, 0)),
           Blk((m, D_MODEL), (tm, D_MODEL), lambda i, j, k: (i, 0)), F32, (nm, 1, N_DEV // 2))
    return (hid, conv), act, f


def _ffn_backward(tag, layer, df, h, hid, act, w_up_g, cw5, cb5, w_down_g, m, seq):
    tm = _div_tile(m, 1024)
    nm = m // tm
    dact = mm(f"{tag}_dact", "nt",
              Blk(df, (tm, D_MODEL), lambda i, j, k: (i, 0)),
              Blk(w_down_g, (2, None, FF_BLK // 2, D_MODEL), lambda i, j, k: (j, 0, 0, 0)),
              Blk((N_DEV // 2, m, FF_BLK), (None, tm, FF_BLK), lambda i, j, k: (j, i, 0)), BF16, (nm, N_DEV // 2, 1))
    d_wdown = mm(f"{tag}_dwdown", "tn",
                 Blk(act, (None, tm, FF_BLK), lambda i, j, k: (i, k, 0)),
                 Blk(df, (tm, D_MODEL), lambda i, j, k: (k, 0)),
                 Blk(w_down_g.shape, (2, None, FF_BLK // 2, D_MODEL), lambda i, j, k: (i, 0, 0, 0)), BF16,
                 (N_DEV // 2, 1, nm))
    dhid, d_cw, d_cb = ffn_mid_bwd(f"{tag}_dmid", hid[0], hid[1], cw5, dact, layer, m=m, seq=seq)
    dhid = dhid.reshape(N_DEV, m, FF_BLK)
    dh = mm(f"{tag}_dh", "nt",
            Blk(dhid, (None, tm, FF_BLK), lambda i, j, k: (k, i, 0)),
            Blk(w_up_g, (None, None, D_MODEL, FF_BLK), lambda i, j, k: (k, 0, 0, 0)),
            Blk((m, D_MODEL), (tm, D_MODEL), lambda i, j, k: (i, 0)), BF16, (nm, 1, N_DEV))
    d_wup = mm(f"{tag}_dwup", "tn",
               Blk(dhid, (None, tm, FF_BLK), lambda i, j, k: (i, k, 0)),
               Blk(h, (tm, D_MODEL), lambda i, j, k: (k, 0)),
               Blk((N_DEV, 1, FF_BLK, D_MODEL), (None, None, FF_BLK, D_MODEL), lambda i, j, k: (i, 0, 0, 0)), BF16,
               (N_DEV, 1, nm))
    return dh, d_wup, d_cw, d_cb, d_wdown


def kernel(x, norm_gains, even_w_in, hgrn_lb_logits, hgrn_norm, rg_conv_w, rg_conv_b, rg_wa, rg_ba, rg_wx, rg_bx, rg_lambda, even_w_out, odd_w_in, fox_f_bias, odd_w_out, ffn_w_up, ffn_conv_w, ffn_conv_b, ffn_w_down, loss_target, m_norm_gains, m_even_w_in, m_hgrn_lb_logits, m_hgrn_norm, m_rg_conv_w, m_rg_conv_b, m_rg_wa, m_rg_ba, m_rg_wx, m_rg_bx, m_rg_lambda, m_even_w_out, m_odd_w_in, m_fox_f_bias, m_odd_w_out, m_ffn_w_up, m_ffn_conv_w, m_ffn_conv_b, m_ffn_w_down, v_norm_gains, v_even_w_in, v_hgrn_lb_logits, v_hgrn_norm, v_rg_conv_w, v_rg_conv_b, v_rg_wa, v_rg_ba, v_rg_wx, v_rg_bx, v_rg_lambda, v_even_w_out, v_odd_w_in, v_fox_f_bias, v_odd_w_out, v_ffn_w_up, v_ffn_conv_w, v_ffn_conv_b, v_ffn_w_down):
    local = dict(locals())
    w = {n: local[n] for n in NAMES}
    mom = {n: local["m_" + n] for n in NAMES}
    var = {n: local["v_" + n] for n in NAMES}
    n_batch, seq, _ = x.shape
    m = n_batch * seq
    tm = _div_tile(m, 512)
    tmm = _div_tile(m, 1024)
    nm = m // tmm

    gathered = all_gather("gather_weights", [w["even_w_in"].astype(BF16)] + [w[n] for n in SMALL_SHARDED])
    g = dict(zip(("even_w_in",) + SMALL_SHARDED, gathered))
    w_in_e = g["even_w_in"]
    gains = _cols_from_gather(g["norm_gains"])
    me = 4 * lax.axis_index("x") + 2 * lax.axis_index("y") + lax.axis_index("c")
    own_block_only = lambda t: lax.dynamic_update_index_in_dim(lax.empty((N_DEV,) + t.shape, t.dtype), t, me, 0)
    behind = (g["norm_gains"][0, 0, 0, 0] * 0.0).astype(BF16)
    out0 = [w["even_w_out"].astype(BF16) + behind]
    out0_sent = gather_start("gather_out0_start", out0, [own_block_only(t) for t in out0])
    behind = (out0_sent[4][0, 0] * 0.0).astype(BF16)
    ffn0 = [w["ffn_w_up"][0:1].astype(BF16) + behind, w["ffn_w_down"][0:1].astype(BF16) + behind]
    ffn0_sent = gather_start("gather_ffn0_start", ffn0, [own_block_only(t) for t in ffn0])
    behind = (ffn0_sent[4][0, 0] * 0.0).astype(BF16)
    mix1w = [w["odd_w_in"].astype(BF16) + behind, w["odd_w_out"].astype(BF16) + behind]
    mix1_sent = gather_start("gather_mix1_start", mix1w, [own_block_only(t) for t in mix1w])
    behind = (mix1_sent[4][0, 0] * 0.0).astype(BF16)
    ffn1 = [w["ffn_w_up"][1:2].astype(BF16) + behind, w["ffn_w_down"][1:2].astype(BF16) + behind]
    ffn1_sent = gather_start("gather_ffn1_start", ffn1, [own_block_only(t) for t in ffn1])
    started = ffn1_sent[4]
    rg_cw = _cols_from_gather(g["rg_conv_w"])[0]
    n_layer = ffn_conv_w.shape[0]
    cw5 = g["ffn_conv_w"].reshape(2, N_DEV // 2, n_layer, FFN_CONV, FF_BLK)
    cb5 = ffn_conv_b.reshape(n_layer, 2, N_DEV // 2, 1, FF_BLK)
    gain = lambda l, k: gains[l, k:k + 1, :]
    wa_bd, wx_bd = _block_diag(rg_wa[0]), _block_diag(rg_wx[0])
    fbias = jnp.pad(fox_f_bias, ((0, 0), (0, LANES - C_HEADS)))

    x0 = x.reshape(m, D_MODEL)
    tgt = loss_target.reshape(m, D_MODEL)

    (h0,) = tile_fwd("l0_prenorm", fn_prenorm_after, m=m, tm=tm, nj=1, rows=[Row(x0)], pars=[Par(gain(0, 0)), Par(started)],
                     outs=[Out(D_MODEL, BF16)])
    z0 = mm("l0_in", "nn",
            Blk(h0, (tmm, D_MODEL), lambda i, j, k: (i, 0)),
            Blk(w_in_e, (2, None, D_MODEL, 384), lambda i, j, k: (j, 0, 0, 0)),
            Blk((m, 3072), (tmm, 768), lambda i, j, k: (i, j)), F32, (nm, N_DEV // 2, 1), b_join=True)
    oa, sprev = hgrn_fwd("l0_hgrn", z0, hgrn_lb_logits, hgrn_norm, n_batch=n_batch, seq=seq)
    rg_rows = lambda: [Row(z0, LANES, 16), Row(z0, LANES, 20)]
    rg_pars = lambda: [Par(rg_cw, "col", LANES), Par(rg_conv_b, "col", LANES), Par(wa_bd, "row", LANES), Par(rg_ba, "col", LANES),
                       Par(wx_bd, "row", LANES), Par(rg_bx, "col", LANES), Par(rg_lambda, "col", LANES)]
    (ob,) = tile_fwd("l0_rglru", fn_rglru, m=m, tm=seq, nj=B_WIDTH // LANES, rows=rg_rows(), pars=rg_pars(),
                     outs=[Out(B_WIDTH, BF16, LANES)])
    mixcat0 = jnp.concatenate([oa, ob], axis=-1)
    (g_out_e,) = gather_wait("gather_out0_wait", out0_sent[0], out0_sent[1], out0_sent[2], out0_sent[3], mixcat0)
    w_out_e = g_out_e.reshape(D_MODEL, D_MODEL)
    mix0 = mm2d("l0_out", "nn", mixcat0, w_out_e)
    x1, h1 = tile_fwd("l0_postnorm", fn_addnorm2, m=m, tm=tm, nj=1, rows=[Row(x0), Row(mix0)], pars=[Par(gain(0, 1)), Par(gain(0, 2))],
                      outs=[Out(D_MODEL, F32), Out(D_MODEL, BF16)])
    w_up_g0, w_down_g0 = gather_wait("gather_ffn0_wait", ffn0_sent[0], ffn0_sent[1], ffn0_sent[2], ffn0_sent[3], h1)
    hid0, act0, f0 = _ffn_forward("l0_ffn", 0, h1, w_up_g0, cw5, cb5, w_down_g0, m, seq)
    x2, h2 = tile_fwd("l0_ffnnorm", fn_addnorm2, m=m, tm=tm, nj=1, rows=[Row(x1), Row(f0)], pars=[Par(gain(0, 3)), Par(gain(1, 0))],
                      outs=[Out(D_MODEL, F32), Out(D_MODEL, BF16)])

    g_in_o, g_out_o = gather_wait("gather_mix1_wait", mix1_sent[0], mix1_sent[1], mix1_sent[2], mix1_sent[3], h2)
    w_in_o = jnp.pad(_cols_from_gather(g_in_o)[0], ((0, 0), (0, 3200 - 3088)))
    w_out_o = g_out_o.reshape(D_MODEL, D_MODEL)
    z1 = mm2d("l1_in", "nn", h2, w_in_o)
    (cgate,) = tile_fwd("l1_gate", fn_fox_gate, m=m, tm=seq, nj=1, rows=[Row(z1, LANES, 3072 // LANES)], pars=[Par(fbias)],
                        outs=[Out(LANES, F32)])
    place, ones_q, ones_k = term_placement()
    qterm, kterm = tile_fwd("l1_terms", fn_fox_terms, m=m, tm=tm, nj=1, rows=[Row(cgate)],
                            pars=[Par(place), Par(ones_q), Par(ones_k)], outs=[Out(TERM_W, BF16), Out(TERM_W, BF16)])
    oc, lse = fox_pair_fwd("l1_attn", z1, qterm, kterm, n_batch=n_batch, seq=seq)
    blk_b = min(ATT_BLK, seq)
    lse = lse.reshape(n_batch, N_PAIR, -1, 2, lse.shape[-1] // blk_b, blk_b).swapaxes(3, 4).reshape(n_batch, N_PAIR, seq // blk_b, 2, 1, blk_b)
    mix1 = mm2d("l1_out", "nn", oc, w_out_o)
    x3, h3 = tile_fwd("l1_postnorm", fn_addnorm2, m=m, tm=tm, nj=1, rows=[Row(x2), Row(mix1)], pars=[Par(gain(1, 1)), Par(gain(1, 2))],
                      outs=[Out(D_MODEL, F32), Out(D_MODEL, BF16)])
    w_up_g1, w_down_g1 = gather_wait("gather_ffn1_wait", ffn1_sent[0], ffn1_sent[1], ffn1_sent[2], ffn1_sent[3], h3)
    hid1, act1, f1 = _ffn_forward("l1_ffn", 1, h3, w_up_g1, cw5, cb5, w_down_g1, m, seq)
    dy, df1, loss_part, d_g13 = loss_head("loss", x3, f1, tgt, gain(1, 3), m=m, tm=tm)
    dh3, d_wup1, d_cw1, d_cb1, d_wdown1 = _ffn_backward("l1_ffn", 1, df1, h3, hid1, act1, w_up_g1, cw5, cb5, w_down_g1, m, seq)
    dx2, dmix1, d_g11, d_g12 = tile_bwd("l1_dpostnorm", fn_addnorm2, m=m, tm=tm, nj=1, rows=[Row(x2), Row(mix1)],
                                        pars=[Par(gain(1, 1)), Par(gain(1, 2))], cts=[Row(dy), Row(dh3)],
                                        drows=[Out(D_MODEL, F32), Out(D_MODEL, BF16)])
    doc = mm2d("l1_doc", "nt", dmix1, w_out_o, BF16)
    d_wout_o = mm2d("l1_dwout", "tn", oc, dmix1)
    dq, dk, dv, dc = fox_pair_bwd("l1_dattn", z1, qterm, kterm, oc, doc, lse, n_batch=n_batch, seq=seq)
    dzf, d_fbias = tile_bwd("l1_dgate", fn_fox_gate, m=m, tm=seq, nj=1, rows=[Row(z1, LANES, 3072 // LANES)], pars=[Par(fbias)],
                            cts=[Row(dc)], drows=[Out(LANES, BF16)])
    dz1 = jnp.concatenate([dq, dk, dv, dzf], axis=-1)
    dh2 = mm2d("l1_dh", "nt", dz1, w_in_o, BF16)
    d_win_o_t = mm2d("l1_dwin", "tn", dz1, h2, BF16)

    send1 = [d_win_o_t[:3088].reshape(N_DEV, 1, 3088 // N_DEV, D_MODEL),
             d_wout_o.reshape(N_DEV, 1, D_MODEL // N_DEV, D_MODEL).astype(BF16), d_wup1, d_wdown1]
    sent1 = _split_exchange("exchange_l1_start", send1, [own_slot_only(t, me) for t in send1], None, None)

    dx1, df0, d_g03, d_g10 = tile_bwd("l0_dffnnorm", fn_addnorm2_after, m=m, tm=tm, nj=1, rows=[Row(x1), Row(f0)],
                                      pars=[Par(gain(0, 3)), Par(gain(1, 0)), Par(sent1[4])], cts=[Row(dx2), Row(dh2)],
                                      drows=[Out(D_MODEL, F32), Out(D_MODEL, BF16)])[:4]
    dh1, d_wup0, d_cw0, d_cb0, d_wdown0 = _ffn_backward("l0_ffn", 0, df0, h1, hid0, act0, w_up_g0, cw5, cb5, w_down_g0, m, seq)
    send0 = [d_wup0, d_wdown0]
    sent0 = _split_exchange("exchange_ffn0_start", send0, [own_slot_only(t, me) for t in send0], None, None)
    dx0a, dmix0, d_g01, d_g02 = tile_bwd("l0_dpostnorm", fn_addnorm2_after, m=m, tm=tm, nj=1, rows=[Row(x0), Row(mix0)],
                                         pars=[Par(gain(0, 1)), Par(gain(0, 2)), Par(sent0[4])], cts=[Row(dx1), Row(dh1)],
                                         drows=[Out(D_MODEL, F32), Out(D_MODEL, BF16)])[:4]
    dmixcat0 = mm2d("l0_dmixcat", "nt", dmix0, w_out_e, BF16)
    d_wout_e = mm2d("l0_dwout", "tn", mixcat0, dmix0)
    dzq, dzf0, dzv, dzg, d_lb, d_hnorm = hgrn_bwd("l0_dhgrn", z0, sprev, hgrn_lb_logits, hgrn_norm, dmixcat0, n_batch=n_batch, seq=seq)
    dzx, dzy, d_rcw, d_rcb, d_wa, d_ba, d_wx, d_bx, d_lam = tile_bwd(
        "l0_drglru", fn_rglru, m=m, tm=seq, nj=B_WIDTH // LANES, rows=rg_rows(), pars=rg_pars(),
        cts=[Row(dmixcat0, LANES, A_WIDTH // LANES)], drows=[Out(B_WIDTH, BF16, LANES), Out(B_WIDTH, BF16, LANES)])
    dz0 = jnp.concatenate([dzq, dzf0, dzv, dzg, dzx, dzy], axis=-1)
    d_win_e = mm("l0_dwin", "tn",
                 Blk(h0, (tmm, D_MODEL), lambda i, j, k: (k, 0)),
                 Blk(dz0, (tmm, 768), lambda i, j, k: (k, j)),
                 Blk(w_in_e.shape, (2, None, D_MODEL, 384), lambda i, j, k: (j, 0, 0, 0)), BF16, (1, N_DEV // 2, nm), o_split=True)
    send_e = [d_win_e, d_wout_e.reshape(N_DEV, 1, D_MODEL // N_DEV, D_MODEL).astype(BF16)]
    sent_e = _split_exchange("exchange_even_start", send_e, [own_slot_only(t, me) for t in send_e], None, None)
    d_ffn_cb = jnp.stack([d_cb0, d_cb1]).reshape(n_layer, 2 * D_FF)
    rep = {"hgrn_lb_logits": d_lb, "hgrn_norm": d_hnorm, "rg_conv_b": d_rcb, "rg_wa": _block_diag_grad(d_wa)[None], "rg_ba": d_ba,
           "rg_wx": _block_diag_grad(d_wx)[None], "rg_bx": d_bx, "rg_lambda": d_lam, "fox_f_bias": d_fbias[:, :C_HEADS],
           "ffn_conv_b": d_ffn_cb}
    rep_blocks = [rep[n] for n in REPLICATED] + [loss_part]
    rep_sent = gather_start("gather_partials_start", rep_blocks, [own_block_only(t) for t in rep_blocks])
    dh0 = mm("l0_dh", "nt",
             Blk(dz0, (tmm, 768), lambda i, j, k: (i, k)),
             Blk(w_in_e, (2, None, D_MODEL, 384), lambda i, j, k: (k, 0, 0, 0)),
             Blk((m, D_MODEL), (tmm, D_MODEL), lambda i, j, k: (i, 0)), BF16, (nm, 1, N_DEV // 2), after=sent_e[4] + rep_sent[4],
             b_join=True)
    dx0, d_g00 = tile_bwd("l0_dprenorm", fn_input_norm, m=m, tm=tm, nj=1, rows=[Row(x0)], pars=[Par(gain(0, 0))],
                          cts=[Row(dx0a), Row(dh0)], drows=[Out(D_MODEL, F32)])

    d_gains = jnp.stack([jnp.concatenate([d_g00, d_g01, d_g02, d_g03], axis=0), jnp.concatenate([d_g10, d_g11, d_g12, d_g13], axis=0)])
    d_ffn_cw = jnp.stack([d_cw0, d_cw1], axis=2).reshape(N_DEV, n_layer, FFN_CONV, FF_BLK)
    r_in_o, r_out_o, r_up1, r_down1 = _split_exchange("exchange_l1_wait", sent1[2], sent1[3], sent1[:2], dx0)
    r_up0, r_down0 = _split_exchange("exchange_ffn0_wait", sent0[2], sent0[3], sent0[:2], dx0)
    r_in_e, r_out_e = _split_exchange("exchange_even_wait", sent_e[2], sent_e[3], sent_e[:2], dx0)
    recv, res = {}, {}
    flipped = ("odd_w_in", "ffn_w_up")
    view = lambda n, t: jnp.swapaxes(t, 1, 2) if n in flipped else t
    for n, r in (("even_w_in", r_in_e), ("even_w_out", r_out_e), ("odd_w_in", r_in_o), ("odd_w_out", r_out_o)):
        res[n] = [view(n, t) for t in adam_tiled("adam_" + n, r, view(n, w[n]), view(n, mom[n]), view(n, var[n]))]
    for n, parts_l in (("ffn_w_up", (r_up0, r_up1)), ("ffn_w_down", (r_down0, r_down1))):
        wmv = (view(n, w[n]), view(n, mom[n]), view(n, var[n]))
        first_layer = adam_tiled(f"adam_{n}_0", parts_l[0], *wmv, layer=0)
        res[n] = [view(n, t) for t in adam_tiled(f"adam_{n}_1", parts_l[1], *wmv, layer=1, prev=first_layer)]
    small_send = [_cols_to_blocks(d_gains), _cols_to_blocks(d_rcw[None]), d_ffn_cw]
    recv.update(zip(SMALL_SHARDED, all_to_all("exchange_small", small_send)))

    parts = gather_wait("gather_partials_wait", rep_sent[0], rep_sent[1], rep_sent[2], rep_sent[3], dx0)
    for n, p in zip(REPLICATED, parts):
        recv[n] = p
    small = SMALL_SHARDED + REPLICATED
    small_res, (loss_sum,) = adam_small("adam_small", [(recv[n], w[n], mom[n], var[n]) for n in small], [parts[-1]])
    res.update(dict(zip(small, small_res)))

    out = [loss_sum[0, 0], dx0.reshape(x.shape)]
    for k in range(4):
        out += [res[n][k] for n in NAMES]
    return tuple(out)
```

```python
import functools

import jax
import jax.numpy as jnp
from jax import lax
from jax.experimental import pallas as pl
from jax.experimental.pallas import tpu as pltpu

F32 = jnp.float32
BF16 = jnp.bfloat16

D_MODEL = 1024
A_HEADS = 4
A_WIDTH = 512
HGRN_CHUNK = 64
HGRN_SEG = 512
B_WIDTH = 512
B_BLOCKS = 8
B_BLOCK_DIM = 64
B_CONV = 4
RG_C = 8.0
C_HEADS = 16
C_HEAD_DIM = 64
D_FF = 2816
FFN_CONV = 3
EPS = 1e-6
LANES = 128
HALO = 16
N_DEV = 8
FF_BLK = 2 * D_FF // N_DEV
MESH = pl.DeviceIdType.MESH
NEG = -1e30
VMEM_LIMIT = 56 * 1024 * 1024

ADAM_LR = 0.001
ADAM_B1 = 0.9
ADAM_B2 = 0.999
ADAM_EPS = 1e-08
ADAM_WD = 0.01
ADAM_STEP = 10


def _dg(a, b, pat):
    nb = a.ndim - 2
    batch = (tuple(range(nb)), tuple(range(nb)))
    ca = a.ndim - 1 if pat[0] == "n" else a.ndim - 2
    cb = b.ndim - 2 if pat[1] == "n" else b.ndim - 1
    return lax.dot_general(a.astype(BF16), b.astype(BF16), (((ca,), (cb,)), batch), preferred_element_type=F32)


@functools.partial(jax.custom_vjp, nondiff_argnums=(2,))
def bdot(a, b, pat):
    return _dg(a, b, pat)


def _bdot_fwd(a, b, pat):
    return _dg(a, b, pat), (a, b)


def _bdot_bwd(pat, res, g):
    a, b = res
    if pat == "nn":
        return _dg(g, b, "nt"), _dg(a, g, "tn")
    if pat == "nt":
        return _dg(g, b, "nn"), _dg(g, a, "tn")
    return _dg(b, g, "nt"), _dg(a, g, "nn")


bdot.defvjp(_bdot_fwd, _bdot_bwd)


def _shift_raw(x, s, up, fill):
    if s == 0:
        return x
    n = x.shape[0]
    r = pltpu.roll(x, (n - s) if up else s, 0)
    idx = lax.broadcasted_iota(jnp.int32, x.shape, 0)
    mask = (idx >= n - s) if up else (idx < s)
    return jnp.where(mask, jnp.asarray(fill, x.dtype), r)


@functools.partial(jax.custom_vjp, nondiff_argnums=(1,))
def shift_down(x, s):
    return _shift_raw(x, s, False, 0.0)


def _shift_down_fwd(x, s):
    return _shift_raw(x, s, False, 0.0), None


def _shift_down_bwd(s, _, g):
    return (_shift_raw(g, s, True, 0.0),)


shift_down.defvjp(_shift_down_fwd, _shift_down_bwd)


def _scan_impl(a, u, up):
    n = a.shape[0]
    s = 1
    while s < n:
        u = a * _shift_raw(u, s, up, 0.0) + u
        if 2 * s < n:
            a = a * _shift_raw(a, s, up, 1.0)
        s *= 2
    return u


@jax.custom_vjp
def lin_scan(a, u):
    return _scan_impl(a, u, False)


def _lin_scan_fwd(a, u):
    h = _scan_impl(a, u, False)
    return h, (a, h)


def _lin_scan_bwd(res, g):
    a, h = res
    gh = _scan_impl(_shift_raw(a, 1, True, 0.0), g, True)
    return gh * _shift_raw(h, 1, False, 0.0), gh


lin_scan.defvjp(_lin_scan_fwd, _lin_scan_bwd)


def _cumsum_impl(x, up, period):
    n = x.shape[0]
    span = n if period is None else period
    idx = lax.broadcasted_iota(jnp.int32, x.shape, 0)
    pos = idx if period is None else idx % period
    s = 1
    while s < span:
        sh = _shift_raw(x, s, up, 0.0)
        if period is not None:
            keep = (pos < period - s) if up else (pos >= s)
            sh = jnp.where(keep, sh, 0.0)
        x = x + sh
        s *= 2
    return x


@functools.partial(jax.custom_vjp, nondiff_argnums=(1,))
def cumsum_rows(x, period):
    return _cumsum_impl(x, False, period)


def _cumsum_fwd(x, period):
    return _cumsum_impl(x, False, period), None


def _cumsum_bwd(period, _, g):
    return (_cumsum_impl(g, True, period),)


cumsum_rows.defvjp(_cumsum_fwd, _cumsum_bwd)


def _sigmoid(x):
    return jax.nn.sigmoid(x)


def _expm1(x):
    return jnp.tanh(0.5 * x) * (jnp.exp(x) + 1.0)


def _softplus(x):
    return jnp.maximum(x, 0.0) + jnp.log(1.0 + jnp.exp(-jnp.abs(x)))


def _rms(x, g):
    return x * lax.rsqrt(jnp.mean(x * x, axis=-1, keepdims=True) + EPS) * g


def fn_prenorm(x, g):
    return (_rms(x, g).astype(BF16),)


def fn_prenorm_after(x, g, _token):
    return fn_prenorm(x, g)


def fn_addnorm2(x, y, g_post, g_pre):
    x1 = x + _rms(y, g_post)
    return x1, _rms(x1, g_pre).astype(BF16)


def fn_addnorm2_after(x, y, g_post, g_pre, _token):
    return fn_addnorm2(x, y, g_post, g_pre)


def fn_input_norm(x, g):
    return x, _rms(x, g).astype(BF16)


def _causal_conv(x, w, b, taps):
    c = b
    for k in range(taps):
        c = c + w[k:k + 1, :] * shift_down(x, taps - 1 - k)
    return c


def fn_rglru(xb, yb, cw, cb, wa, ba, wx, bx, lam):
    xf = _causal_conv(xb, cw, cb, B_CONV)
    r = _sigmoid(bdot(xf, wa, "nn") + ba)
    i = _sigmoid(bdot(xf, wx, "nn") + bx)
    log_a = -RG_C * r * _softplus(-lam)
    a = jnp.exp(log_a)
    u = jnp.sqrt(-_expm1(2.0 * log_a)) * (i * xf)
    h = lin_scan(a, u)
    return ((h * jax.nn.gelu(yb)).astype(BF16),)


def fn_fox_gate(zf, bias):
    return (cumsum_rows(jax.nn.log_sigmoid(zf + bias), None),)


def fn_hgrn_seg(q, fl, v, g, st, logits, hn):
    rows = q.shape[0]
    nc = rows // HGRN_CHUNK
    l0, l1, l2 = logits[0:1, :], logits[1:2, :], logits[2:3, :]
    mx = jnp.maximum(jnp.maximum(l0, l1), l2)
    e0, e1, e2 = jnp.exp(l0 - mx), jnp.exp(l1 - mx), jnp.exp(l2 - mx)
    lb = e0 / (e0 + e1 + e2)
    forget = lb + (1.0 - lb) * _sigmoid(fl)
    qs = q * _sigmoid(q)
    kk = 1.0 - forget
    logf = jnp.log(forget)
    bcum = cumsum_rows(logf, HGRN_CHUNK)
    c3 = lambda t: t.reshape(nc, HGRN_CHUNK, 128)
    b_last = jnp.sum(c3(logf), axis=1, keepdims=True)
    bcum3 = c3(bcum)
    q_dec = c3(qs) * jnp.exp(bcum3)
    k_dec = c3(kk) * jnp.exp(-bcum3)
    k_upd = c3(kk) * jnp.exp(b_last - bcum3)
    v3 = c3(v)
    scores = bdot(q_dec, k_dec, "nt")
    ri = lax.broadcasted_iota(jnp.int32, scores.shape, 1)
    ci = lax.broadcasted_iota(jnp.int32, scores.shape, 2)
    scores = jnp.where(ri >= ci, scores, 0.0)
    o = bdot(scores, v3, "nn")
    upd_t = bdot(v3, k_upd, "tn")
    dec = jnp.exp(b_last)
    prev = []
    for n in range(nc):
        prev.append(st)
        st = st * dec[n] + upd_t[n]
    o = o + bdot(q_dec, jnp.stack(prev), "nt")
    o = o.reshape(rows, 128)
    o = o * lax.rsqrt(jnp.mean(o * o, axis=-1, keepdims=True) + EPS) * hn
    return (o * _sigmoid(g)).astype(BF16), st


def _ffn_conv(xg, xv, cw, cb):
    cg = _causal_conv(xg, cw[0], cb[0], FFN_CONV)[HALO:]
    cv = _causal_conv(xv, cw[1], cb[1], FFN_CONV)[HALO:]
    return cg, cv


def _ffn_gate(cg, cv):
    return jax.nn.gelu(cg) * cv


class Row:
    def __init__(self, arr, cb=None, off=0):
        self.arr, self.cb, self.off = arr, cb, off

    def spec(self, tm):
        if self.cb is None:
            return pl.BlockSpec((tm, self.arr.shape[1]), lambda j, i: (i, 0))
        off = self.off
        return pl.BlockSpec((tm, self.cb), lambda j, i: (i, j + off))


class Par:
    def __init__(self, arr, kind="full", bs=None):
        self.arr, self.kind, self.bs = arr, kind, bs

    def block(self):
        if self.kind == "full":
            return self.arr.shape
        if self.kind == "col":
            return (self.arr.shape[0], self.bs)
        return (self.bs, self.arr.shape[1])

    def spec(self):
        if self.kind == "full":
            return pl.BlockSpec(self.block(), lambda j, i: (0, 0))
        if self.kind == "col":
            return pl.BlockSpec(self.block(), lambda j, i: (0, j))
        return pl.BlockSpec(self.block(), lambda j, i: (j, 0))


class Out:
    def __init__(self, width, dtype, cb=None, off=0):
        self.width, self.dtype, self.cb, self.off = width, dtype, cb, off

    def spec(self, tm):
        if self.cb is None:
            return pl.BlockSpec((tm, self.width), lambda j, i: (i, 0))
        off = self.off
        return pl.BlockSpec((tm, self.cb), lambda j, i: (i, j + off))


def _params(sem):
    return pltpu.CompilerParams(dimension_semantics=sem, vmem_limit_bytes=VMEM_LIMIT)


def tile_fwd(name, fn, *, m, tm, nj, rows, pars, outs, n_acc=0):
    n_r, n_p, n_o = len(rows), len(pars), len(outs)

    def body(*refs):
        ins = [r[...] for r in refs[:n_r + n_p]]
        res = fn(*ins)
        o_refs = refs[n_r + n_p:]
        for k in range(n_o):
            o_refs[k][...] = res[k].astype(o_refs[k].dtype)
        first = jnp.logical_and(pl.program_id(0) == 0, pl.program_id(1) == 0)
        for k in range(n_acc):
            ref = o_refs[n_o + k]

            @pl.when(first)
            def _():
                ref[...] = jnp.zeros_like(ref)

            ref[...] += res[n_o + k]

    out_shape = [jax.ShapeDtypeStruct((m, o.width), o.dtype) for o in outs]
    out_specs = [o.spec(tm) for o in outs]
    for _ in range(n_acc):
        out_shape.append(jax.ShapeDtypeStruct((1, LANES), F32))
        out_specs.append(pl.BlockSpec((1, LANES), lambda j, i: (0, 0)))
    sem = ("arbitrary", "arbitrary") if n_acc else ("parallel", "parallel")
    return pl.pallas_call(
        body, grid=(nj, m // tm), name=name,
        in_specs=[r.spec(tm) for r in rows] + [p.spec() for p in pars],
        out_specs=out_specs, out_shape=out_shape, compiler_params=_params(sem),
    )(*[r.arr for r in rows], *[p.arr for p in pars])


def tile_bwd(name, fn, *, m, tm, nj, rows, pars, cts, drows):
    n_r, n_p, n_c = len(rows), len(pars), len(cts)
    want = [k for k in range(n_r) if drows[k] is not None]

    def body(*refs):
        ins = [r[...] for r in refs[:n_r + n_p]]
        ct = [r[...] for r in refs[n_r + n_p:n_r + n_p + n_c]]
        o_refs = refs[n_r + n_p + n_c:]
        res, vjp = jax.vjp(fn, *ins)
        grads = vjp(tuple(c.astype(r.dtype) for c, r in zip(ct, res)))
        for pos, k in enumerate(want):
            o_refs[pos][...] = grads[k].astype(o_refs[pos].dtype)
        for k in range(n_p):
            ref = o_refs[len(want) + k]
            first = pl.program_id(1) == 0
            if pars[k].kind == "full":
                first = jnp.logical_and(first, pl.program_id(0) == 0)

            @pl.when(first)
            def _():
                ref[...] = jnp.zeros_like(ref)

            ref[...] += grads[n_r + k].astype(F32)

    out_shape = [jax.ShapeDtypeStruct((m, drows[k].width), drows[k].dtype) for k in want]
    out_specs = [drows[k].spec(tm) for k in want]
    for p in pars:
        out_shape.append(jax.ShapeDtypeStruct(p.arr.shape, F32))
        out_specs.append(p.spec())
    return pl.pallas_call(
        body, grid=(nj, m // tm), name=name,
        in_specs=[r.spec(tm) for r in rows] + [p.spec() for p in pars] + [c.spec(tm) for c in cts],
        out_specs=out_specs, out_shape=out_shape, compiler_params=_params(("arbitrary", "arbitrary")),
    )(*[r.arr for r in rows], *[p.arr for p in pars], *[c.arr for c in cts])


def loss_head(name, x, y, tgt, g, *, m, tm):
    def body(x_ref, y_ref, t_ref, g_ref, dout_ref, dy_ref, loss_ref, dg_ref):
        normed, vjp = jax.vjp(_rms, y_ref[...], g_ref[...])
        err = x_ref[...] + normed - t_ref[...]
        dout = err * (1.0 / D_MODEL)
        dy, dg = vjp(dout)
        dout_ref[...] = dout
        dy_ref[...] = dy.astype(dy_ref.dtype)

        @pl.when(pl.program_id(0) == 0)
        def _():
            loss_ref[...] = jnp.zeros_like(loss_ref)
            dg_ref[...] = jnp.zeros_like(dg_ref)

        loss_ref[...] += 0.5 * jnp.sum(jnp.mean(err * err, axis=-1, keepdims=True), axis=0, keepdims=True)
        dg_ref[...] += dg

    row = pl.BlockSpec((tm, D_MODEL), lambda i: (i, 0))
    whole = lambda w: pl.BlockSpec((1, w), lambda i: (0, 0))
    return pl.pallas_call(
        body, grid=(m // tm,), name=name, in_specs=[row, row, row, whole(D_MODEL)],
        out_specs=[row, row, whole(LANES), whole(D_MODEL)],
        out_shape=[jax.ShapeDtypeStruct((m, D_MODEL), F32), jax.ShapeDtypeStruct((m, D_MODEL), BF16),
                   jax.ShapeDtypeStruct((1, LANES), F32), jax.ShapeDtypeStruct((1, D_MODEL), F32)],
        compiler_params=_params(("arbitrary",)),
    )(x, y, tgt, g)


class Blk:
    def __init__(self, arr, block, index):
        self.arr, self.block, self.index = arr, block, index

    def spec(self):
        return pl.BlockSpec(self.block, self.index)


def _flat2(v):
    return v if v.ndim == 2 else v.reshape(-1, v.shape[-1])


def mm(name, pat, a, b, o, out_dtype, grid, after=None, b_join=False, o_split=False):
    nk = grid[2]
    o_shape = o.arr

    def put(o_ref, r):
        if o_split:
            half = r.shape[1] // 2
            o_ref[0] = r[:, :half].astype(out_dtype)
            o_ref[1] = r[:, half:].astype(out_dtype)
        else:
            o_ref[...] = r.astype(out_dtype).reshape(o_ref.shape)

    def body(*refs):
        a_ref, b_ref = refs[0], refs[1]
        o_ref = refs[3] if after is not None else refs[2]
        bv = jnp.concatenate([b_ref[0], b_ref[1]], axis=1) if b_join else _flat2(b_ref[...])
        r = _dg(_flat2(a_ref[...]), bv, pat)
        if nk == 1:
            put(o_ref, r)
            return
        acc_ref = refs[-1]
        kk = pl.program_id(2)

        @pl.when(kk == 0)
        def _():
            acc_ref[...] = r

        @pl.when(kk > 0)
        def _():
            acc_ref[...] += r

        @pl.when(kk == nk - 1)
        def _():
            put(o_ref, acc_ref[...])

    ob = [d for d in o.block if d is not None]
    if o_split:
        acc_shape = (ob[1], 2 * ob[2])
    else:
        acc_shape = (ob[0], ob[1]) if len(ob) == 2 else (ob[0] * ob[1], ob[2])
    in_specs = [a.spec(), b.spec()]
    args = [a.arr, b.arr]
    if after is not None:
        in_specs.append(pl.BlockSpec(memory_space=pl.ANY))
        args.append(after)
    return pl.pallas_call(
        body, grid=grid, name=name, in_specs=in_specs, out_specs=o.spec(),
        out_shape=jax.ShapeDtypeStruct(o_shape, out_dtype),
        scratch_shapes=[pltpu.VMEM(acc_shape, F32)] if nk > 1 else [],
        compiler_params=_params(("parallel", "parallel", "arbitrary")),
    )(*args)


def _div_tile(n, cap):
    if n <= cap:
        return n
    best = 128
    for t in range(128, cap + 1, 128):
        if n % t == 0:
            best = t
    return best


def mm2d(name, pat, a, b, out_dtype=F32):
    if pat == "tn":
        k, m = a.shape
    else:
        m, k = a.shape
    n = b.shape[0] if pat == "nt" else b.shape[1]
    tm, tn, tk = _div_tile(m, 1024), _div_tile(n, 1024), _div_tile(k, 1024)
    a_blk = Blk(a, (tk, tm), lambda i, j, kk: (kk, i)) if pat == "tn" else Blk(a, (tm, tk), lambda i, j, kk: (i, kk))
    b_blk = Blk(b, (tn, tk), lambda i, j, kk: (j, kk)) if pat == "nt" else Blk(b, (tk, tn), lambda i, j, kk: (kk, j))
    o_blk = Blk((m, n), (tm, tn), lambda i, j, kk: (i, j))
    return mm(name, pat, a_blk, b_blk, o_blk, out_dtype, (m // tm, n // tn, k // tk))


def hgrn_fwd(name, z, logits, hnorm, *, n_batch, seq):
    m = n_batch * seq
    ts = min(HGRN_SEG, seq)
    n_seg = seq // ts

    def body(q_ref, f_ref, v_ref, g_ref, lg_ref, hn_ref, o_ref, sp_ref, st_ref):
        s = pl.program_id(2)

        @pl.when(s == 0)
        def _():
            st_ref[...] = jnp.zeros_like(st_ref)

        st = st_ref[...]
        sp_ref[...] = st
        o, st_new = fn_hgrn_seg(q_ref[...], f_ref[...], v_ref[...], g_ref[...], st, lg_ref[...], hn_ref[...])
        o_ref[...] = o
        st_ref[...] = st_new

    part = lambda p: pl.BlockSpec((ts, 128), lambda h, b, s: (b * n_seg + s, 4 * p + h))
    return pl.pallas_call(
        body, grid=(A_HEADS, n_batch, n_seg), name=name,
        in_specs=[part(0), part(1), part(2), part(3),
                  pl.BlockSpec((3, 128), lambda h, b, s: (0, h)),
                  pl.BlockSpec((1, 128), lambda h, b, s: (0, h))],
        out_specs=[pl.BlockSpec((ts, 128), lambda h, b, s: (b * n_seg + s, h)),
                   pl.BlockSpec((128, 128), lambda h, b, s: ((b * n_seg + s) * A_HEADS + h, 0))],
        out_shape=[jax.ShapeDtypeStruct((m, A_WIDTH), BF16),
                   jax.ShapeDtypeStruct((n_batch * n_seg * A_HEADS * 128, 128), F32)],
        scratch_shapes=[pltpu.VMEM((128, 128), F32)],
        compiler_params=_params(("arbitrary", "arbitrary", "arbitrary")),
    )(z, z, z, z, logits, hnorm)


def hgrn_bwd(name, z, sprev, logits, hnorm, do, *, n_batch, seq):
    m = n_batch * seq
    ts = min(HGRN_SEG, seq)
    n_seg = seq // ts

    def body(q_ref, f_ref, v_ref, g_ref, sp_ref, lg_ref, hn_ref, do_ref, dq_ref, df_ref, dv_ref, dg_ref, dlg_ref, dhn_ref, dst_ref):
        s = pl.program_id(2)

        @pl.when(s == 0)
        def _():
            dst_ref[...] = jnp.zeros_like(dst_ref)

        res, vjp = jax.vjp(fn_hgrn_seg, q_ref[...], f_ref[...], v_ref[...], g_ref[...], sp_ref[...], lg_ref[...], hn_ref[...])
        dq, df, dv, dg, dst, dlg, dhn = vjp((do_ref[...].astype(res[0].dtype), dst_ref[...]))
        dq_ref[...] = dq.astype(dq_ref.dtype)
        df_ref[...] = df.astype(df_ref.dtype)
        dv_ref[...] = dv.astype(dv_ref.dtype)
        dg_ref[...] = dg.astype(dg_ref.dtype)
        dst_ref[...] = dst
        first = jnp.logical_and(pl.program_id(1) == 0, s == 0)

        @pl.when(first)
        def _():
            dlg_ref[...] = jnp.zeros_like(dlg_ref)
            dhn_ref[...] = jnp.zeros_like(dhn_ref)

        dlg_ref[...] += dlg
        dhn_ref[...] += dhn

    rev = lambda b, s: b * n_seg + (n_seg - 1 - s)
    part = lambda p: pl.BlockSpec((ts, 128), lambda h, b, s: (rev(b, s), 4 * p + h))
    head = pl.BlockSpec((ts, 128), lambda h, b, s: (rev(b, s), h))
    dpart = jax.ShapeDtypeStruct((m, A_WIDTH), BF16)
    return pl.pallas_call(
        body, grid=(A_HEADS, n_batch, n_seg), name=name,
        in_specs=[part(0), part(1), part(2), part(3),
                  pl.BlockSpec((128, 128), lambda h, b, s: (rev(b, s) * A_HEADS + h, 0)),
                  pl.BlockSpec((3, 128), lambda h, b, s: (0, h)),
                  pl.BlockSpec((1, 128), lambda h, b, s: (0, h)),
                  head],
        out_specs=[head, head, head, head,
                   pl.BlockSpec((3, 128), lambda h, b, s: (0, h)),
                   pl.BlockSpec((1, 128), lambda h, b, s: (0, h))],
        out_shape=[dpart, dpart, dpart, dpart,
                   jax.ShapeDtypeStruct(logits.shape, F32),
                   jax.ShapeDtypeStruct(hnorm.shape, F32)],
        scratch_shapes=[pltpu.VMEM((128, 128), F32)],
        compiler_params=_params(("arbitrary", "arbitrary", "arbitrary")),
    )(z, z, z, z, sprev, logits, hnorm, do)


FFN_ROWS = 128
FFN_LANES = 128


def _ffn_tiles(m, seq):
    tm = min(512, seq)
    return tm, seq // tm, m // tm


def ffn_mid_fwd(name, hid, cw, cb, layer, *, m, seq):
    tm, n_t, n_i = _ffn_tiles(m, seq)
    hb = tm // HALO

    def body(x_ref, xb_ref, cw_ref, cb_ref, o_ref, c_ref):
        first = pl.program_id(1) % n_t == 0
        before = jnp.where(first, 0.0, xb_ref[...])
        ext = jnp.concatenate([before, x_ref[...]], axis=1)
        cg, cv = _ffn_conv(ext[0], ext[1], cw_ref[...], cb_ref[...])
        o_ref[...] = _ffn_gate(cg, cv).astype(o_ref.dtype)
        c_ref[0] = cg.astype(c_ref.dtype)
        c_ref[1] = cv.astype(c_ref.dtype)

    return pl.pallas_call(
        body, grid=(N_DEV // 2, n_i), name=name,
        in_specs=[pl.BlockSpec((2, None, tm, FF_BLK), lambda d, i: (0, d, i, 0)),
                  pl.BlockSpec((2, None, HALO, FF_BLK), lambda d, i: (0, d, jnp.maximum(i * hb - 1, 0), 0)),
                  pl.BlockSpec((2, None, None, FFN_CONV, FF_BLK), lambda d, i: (0, d, layer, 0, 0)),
                  pl.BlockSpec((None, 2, None, 1, FF_BLK), lambda d, i: (layer, 0, d, 0, 0))],
        out_specs=[pl.BlockSpec((None, tm, FF_BLK), lambda d, i: (d, i, 0)),
                   pl.BlockSpec((2, None, tm, FF_BLK), lambda d, i: (0, d, i, 0))],
        out_shape=[jax.ShapeDtypeStruct((N_DEV // 2, m, FF_BLK), BF16),
                   jax.ShapeDtypeStruct((2, N_DEV // 2, m, FF_BLK), BF16)],
        compiler_params=_params(("parallel", "parallel")),
    )(hid, hid, cw, cb)


def ffn_mid_bwd(name, hid, conv, cw, dact, layer, *, m, seq):
    tm, n_t, n_i = _ffn_tiles(m, seq)
    hb = tm // HALO
    last_blk = m // HALO - 1

    rc = min(FFN_ROWS, tm)
    lane_chunks = [(l0, min(FFN_LANES, FF_BLK - l0)) for l0 in range(0, FF_BLK, FFN_LANES)]

    def body(x_ref, c_ref, ca_ref, cw_ref, da_ref, daa_ref, dx_ref, dcw_ref, dcb_ref, cext_ref, dext_ref):
        i = pl.program_id(1)
        last = i % n_t == n_t - 1
        cext_ref[:, :tm] = c_ref[...]
        cext_ref[:, tm:] = ca_ref[...]
        dext_ref[:tm] = da_ref[...]
        dext_ref[tm:] = jnp.where(last, jnp.zeros_like(daa_ref[...]), daa_ref[...])

        @pl.when(i == 0)
        def _():
            dcw_ref[...] = jnp.zeros_like(dcw_ref)
            dcb_ref[...] = jnp.zeros_like(dcb_ref)

        for l0, lw in lane_chunks:
            lanes = slice(l0, l0 + lw)

            def chunk(c, sums, lanes=lanes, lw=lw):
                r0 = pl.multiple_of(c * rc, rc)
                ext = pl.ds(r0, rc + HALO)
                cg, cv = cext_ref[0, ext, lanes].astype(F32), cext_ref[1, ext, lanes].astype(F32)
                _, vjp_gate = jax.vjp(_ffn_gate, cg, cv)
                dconv = vjp_gate(dext_ref[ext, lanes].astype(F32))
                out = []
                for half in range(2):
                    x = x_ref[half, pl.ds(r0, rc), lanes]
                    dx = None
                    for k in range(FFN_CONV):
                        s = FFN_CONV - 1 - k
                        dc_s = _shift_raw(dconv[half], s, True, 0.0)[:rc]
                        term = cw_ref[half, k:k + 1, lanes] * dc_s
                        dx = term if dx is None else dx + term
                        out.append(sums[len(out)] + jnp.sum(x * dc_s, axis=0, keepdims=True))
                    out.append(sums[len(out)] + jnp.sum(dconv[half][:rc], axis=0, keepdims=True))
                    dx_ref[half, pl.ds(r0, rc), lanes] = dx.astype(dx_ref.dtype)
                return tuple(out)

            zero = jnp.zeros((1, lw), F32)
            sums = lax.fori_loop(0, tm // rc, chunk, (zero,) * (2 * (FFN_CONV + 1)))
            for half in range(2):
                base = half * (FFN_CONV + 1)
                for k in range(FFN_CONV):
                    dcw_ref[half, k:k + 1, lanes] += sums[base + k]
                dcb_ref[half, :, lanes] += sums[base + FFN_CONV]

    return pl.pallas_call(
        body, grid=(N_DEV // 2, n_i), name=name,
        in_specs=[pl.BlockSpec((2, None, tm, FF_BLK), lambda d, i: (0, d, i, 0)),
                  pl.BlockSpec((2, None, tm, FF_BLK), lambda d, i: (0, d, i, 0)),
                  pl.BlockSpec((2, None, HALO, FF_BLK), lambda d, i: (0, d, jnp.minimum((i + 1) * hb, last_blk), 0)),
                  pl.BlockSpec((2, None, None, FFN_CONV, FF_BLK), lambda d, i: (0, d, layer, 0, 0)),
                  pl.BlockSpec((None, tm, FF_BLK), lambda d, i: (d, i, 0)),
                  pl.BlockSpec((None, HALO, FF_BLK), lambda d, i: (d, jnp.minimum((i + 1) * hb, last_blk), 0))],
        out_specs=[pl.BlockSpec((2, None, tm, FF_BLK), lambda d, i: (0, d, i, 0)),
                   pl.BlockSpec((2, None, FFN_CONV, FF_BLK), lambda d, i: (0, d, 0, 0)),
                   pl.BlockSpec((2, None, 1, FF_BLK), lambda d, i: (0, d, 0, 0))],
        out_shape=[jax.ShapeDtypeStruct((2, N_DEV // 2, m, FF_BLK), BF16),
                   jax.ShapeDtypeStruct((2, N_DEV // 2, FFN_CONV, FF_BLK), F32),
                   jax.ShapeDtypeStruct((2, N_DEV // 2, 1, FF_BLK), F32)],
        scratch_shapes=[pltpu.VMEM((2, tm + HALO, FF_BLK), BF16), pltpu.VMEM((tm + HALO, FF_BLK), BF16)],
        compiler_params=_params(("arbitrary", "arbitrary")),
    )(hid, conv, conv, cw, dact, dact)


ATT_BLK = 512
ATT_BLK_FWD = 1024
N_PAIR = C_HEADS // 2
TERM_W = C_HEADS * LANES


def term_placement():
    import numpy as np
    place = np.zeros((3, LANES, TERM_W), np.float32)
    ones_q = np.zeros((1, TERM_W), np.float32)
    ones_k = np.zeros((1, TERM_W), np.float32)
    for h in range(C_HEADS):
        for j in range(3):
            place[j, h, h * LANES + C_HEAD_DIM + j] = 1.0
            ones_q[0, h * LANES + C_HEAD_DIM + 3 + j] = 1.0
            ones_k[0, h * LANES + C_HEAD_DIM + j] = 1.0
    return (jnp.asarray(place.reshape(3 * LANES, TERM_W), BF16), jnp.asarray(ones_q, F32), jnp.asarray(ones_k, F32))


def fn_fox_terms(c, place, ones_q, ones_k):
    parts = _split3(c)
    placed = sum(_dg(parts[j], place[j * LANES:(j + 1) * LANES], "nn") for j in range(3))
    return (placed + ones_q).astype(BF16), (ones_k - pltpu.roll(placed, 3, 1)).astype(BF16)


def _head_tile(z, terms, e):
    lane = lax.broadcasted_iota(jnp.int32, z.shape, 1)
    base = z if e == 0 else pltpu.roll(z, C_HEAD_DIM, 1)
    return jnp.where(lane < C_HEAD_DIM, base, terms.astype(z.dtype))


def _head_only(z, e):
    lane = lax.broadcasted_iota(jnp.int32, z.shape, 1)
    mine = (lane < C_HEAD_DIM) if e == 0 else (lane >= C_HEAD_DIM)
    return jnp.where(mine, z, jnp.zeros_like(z)).astype(BF16)


def _pair_tile(a0, a1):
    lane = lax.broadcasted_iota(jnp.int32, a0.shape, 1)
    return jnp.where(lane < C_HEAD_DIM, a0, pltpu.roll(a1, C_HEAD_DIM, 1))


def _lane_col(a, k):
    lane = lax.broadcasted_iota(jnp.int32, a.shape, 1)
    return jnp.sum(jnp.where(lane == k, a, 0.0), axis=1, keepdims=True)


def _causal(s):
    key = lax.broadcasted_iota(jnp.int32, s.shape, 0)
    qry = lax.broadcasted_iota(jnp.int32, s.shape, 1)
    return qry >= key


def fox_pair_fwd(name, z, qterm, kterm, *, n_batch, seq):
    m = n_batch * seq
    blk = min(ATT_BLK_FWD, seq)
    nq = seq // blk
    dh = C_HEAD_DIM

    def body(zq_ref, zk_ref, zv_ref, qt_ref, kt_ref, o_ref, lse_ref, ka_ref, vt_ref):
        qi = pl.program_id(2)

        @pl.when(qi == 0)
        def _():
            zk = zk_ref[...]
            for e in range(2):
                ka_ref[e] = _head_tile(zk, kt_ref[:, e * LANES:(e + 1) * LANES], e).astype(BF16)
            for cb in range(nq):
                vt_ref[cb] = zv_ref[cb * blk:(cb + 1) * blk, :].T.astype(BF16)

        zq = zq_ref[...] * dh ** -0.5
        qa = [_head_tile(zq, qt_ref[:, e * LANES:(e + 1) * LANES], e).astype(BF16) for e in range(2)]

        def block(j, carry, diagonal):
            rows = pl.ds(pl.multiple_of(j * blk, blk), blk)
            out = []
            for e in range(2):
                mx, l, acc = carry[e]
                s = _dg(ka_ref[e, rows, :], qa[e], "nt")
                if diagonal:
                    s = jnp.where(_causal(s), s, NEG)
                mx_new = jnp.maximum(mx, jnp.max(s, axis=0, keepdims=True))
                p = jnp.exp(s - mx_new)
                alpha = jnp.exp(mx - mx_new)
                l = alpha * l + jnp.sum(p, axis=0, keepdims=True)
                acc = alpha * acc + _dg(vt_ref[j, e * dh:(e + 1) * dh, :], p, "nn")
                out.append((mx_new, l, acc))
            return tuple(out)

        one = (jnp.full((1, blk), NEG, F32), jnp.zeros((1, blk), F32), jnp.zeros((dh, blk), F32))
        carry = lax.fori_loop(0, qi, lambda j, cr: block(j, cr, False), (one, one))
        res = block(qi, carry, True)
        ot = jnp.concatenate([res[e][2] / res[e][1] for e in range(2)], axis=0)
        o_ref[...] = ot.T.astype(o_ref.dtype)
        for e in range(2):
            lse_ref[e] = res[e][0] + jnp.log(res[e][1])

    col = lambda part: (lambda b, g, i: (b, part * N_PAIR + g))
    return pl.pallas_call(
        body, grid=(n_batch, N_PAIR, nq), name=name,
        in_specs=[pl.BlockSpec((blk, LANES), lambda b, g, i: (b * nq + i, g)),
                  pl.BlockSpec((seq, LANES), col(1)),
                  pl.BlockSpec((seq, LANES), col(2)),
                  pl.BlockSpec((blk, 2 * LANES), lambda b, g, i: (b * nq + i, g)),
                  pl.BlockSpec((seq, 2 * LANES), lambda b, g, i: (b, g))],
        out_specs=[pl.BlockSpec((blk, LANES), lambda b, g, i: (b * nq + i, g)),
                   pl.BlockSpec((None, None, None, 2, 1, blk), lambda b, g, i: (b, g, i, 0, 0, 0))],
        out_shape=[jax.ShapeDtypeStruct((m, D_MODEL), BF16), jax.ShapeDtypeStruct((n_batch, N_PAIR, nq, 2, 1, blk), F32)],
        scratch_shapes=[pltpu.VMEM((2, seq, LANES), BF16), pltpu.VMEM((nq, LANES, blk), BF16)],
        compiler_params=_params(("parallel", "parallel", "arbitrary")),
    )(z, z, z, qterm, kterm)


def fox_pair_bwd(name, z, qterm, kterm, o, do, lse, *, n_batch, seq):
    m = n_batch * seq
    blk = min(ATT_BLK, seq)
    nq = seq // blk
    dh = C_HEAD_DIM

    def body(zq_ref, zk_ref, zv_ref, qt_ref, kt_ref, o_ref, do_ref, lse_ref, dq_ref, dk_ref, dv_ref, dc_ref,
             qa_ref, doh_ref, del_ref, dqt_ref, dk_acc, dv_acc):
        g, j = pl.program_id(1), pl.program_id(2)
        lane = lax.broadcasted_iota(jnp.int32, (blk, LANES), 1)

        @pl.when(jnp.logical_and(g == 0, j == 0))
        def _():
            dc_ref[...] = jnp.zeros_like(dc_ref)

        @pl.when(j == 0)
        def _():
            zq = zq_ref[...] * dh ** -0.5
            dov = do_ref[...]
            for e in range(2):
                qa_ref[e] = _head_tile(zq, qt_ref[:, e * LANES:(e + 1) * LANES], e).astype(BF16)
                doh_ref[e] = _head_only(dov, e)
            for cb in range(nq):
                rows = slice(cb * blk, (cb + 1) * blk)
                prod_t = (do_ref[rows, :].astype(F32) * o_ref[rows, :].astype(F32)).T
                for e in range(2):
                    del_ref[cb, e] = jnp.sum(prod_t[e * dh:(e + 1) * dh], axis=0, keepdims=True)
            dqt_ref[...] = jnp.zeros_like(dqt_ref)

        zk, zv = zk_ref[...], zv_ref[...]
        ka32 = [_head_tile(zk, kt_ref[:, e * LANES:(e + 1) * LANES], e) for e in range(2)]
        ka = [t.astype(BF16) for t in ka32]
        kat = [t.T.astype(BF16) for t in ka32]
        vh = [_head_only(zv, e) for e in range(2)]
        dk_acc[...] = jnp.zeros_like(dk_acc)
        dv_acc[...] = jnp.zeros_like(dv_acc)

        def block(i, diagonal):
            rows = pl.ds(pl.multiple_of(i * blk, blk), blk)
            for e in range(2):
                qv, dov = qa_ref[e, rows, :], doh_ref[e, rows, :]
                p = jnp.exp(_dg(ka[e], qv, "nt") - lse_ref[i, e])
                if diagonal:
                    p = jnp.where(_causal(p), p, 0.0)
                dv_acc[...] += _dg(p, dov, "nn")
                ds = p * (_dg(vh[e], dov, "nt") - del_ref[i, e])
                dk_acc[e] += _dg(ds, qv, "nn")
                dqt_ref[i, e] += _dg(kat[e], ds, "nn")

        block(j, True)

        def rest(i, carry):
            block(i, False)
            return carry

        lax.fori_loop(j + 1, nq, rest, 0)
        dk0, dk1 = dk_acc[0], dk_acc[1]
        dk_ref[...] = _pair_tile(dk0, dk1).astype(dk_ref.dtype)
        dv_ref[...] = dv_acc[...].astype(dv_ref.dtype)
        rows_j = pl.ds(pl.multiple_of(j * blk, blk), blk)
        for e, dke in enumerate((dk0, dk1)):
            dc_ref[rows_j, :] -= jnp.where(lane == 2 * g + e, _lane_col(dke, dh + 3), 0.0)

        @pl.when(j == nq - 1)
        def _():
            for i in range(nq):
                nat = [dqt_ref[i, e].T for e in range(2)]
                rows = slice(i * blk, (i + 1) * blk)
                dq_ref[rows, :] = (_pair_tile(nat[0], nat[1]) * dh ** -0.5).astype(dq_ref.dtype)
                for e in range(2):
                    dc_ref[rows, :] += jnp.where(lane == 2 * g + e, _lane_col(nat[e], dh), 0.0)

    col = lambda part: (lambda b, g, j: (b, part * N_PAIR + g))
    colj = lambda part: (lambda b, g, j: (b * nq + j, part * N_PAIR + g))
    pair = jax.ShapeDtypeStruct((m, D_MODEL), BF16)
    return pl.pallas_call(
        body, grid=(n_batch, N_PAIR, nq), name=name,
        in_specs=[pl.BlockSpec((seq, LANES), col(0)),
                  pl.BlockSpec((blk, LANES), colj(1)),
                  pl.BlockSpec((blk, LANES), colj(2)),
                  pl.BlockSpec((seq, 2 * LANES), lambda b, g, j: (b, g)),
                  pl.BlockSpec((blk, 2 * LANES), lambda b, g, j: (b * nq + j, g)),
                  pl.BlockSpec((seq, LANES), col(0)),
                  pl.BlockSpec((seq, LANES), col(0)),
                  pl.BlockSpec((None, None, nq, 2, 1, blk), lambda b, g, j: (b, g, 0, 0, 0, 0))],
        out_specs=[pl.BlockSpec((seq, LANES), col(0)),
                   pl.BlockSpec((blk, LANES), colj(0)),
                   pl.BlockSpec((blk, LANES), colj(0)),
                   pl.BlockSpec((seq, LANES), lambda b, g, j: (b, 0))],
        out_shape=[pair, pair, pair, jax.ShapeDtypeStruct((m, LANES), F32)],
        scratch_shapes=[pltpu.VMEM((2, seq, LANES), BF16), pltpu.VMEM((2, seq, LANES), BF16),
                        pltpu.VMEM((nq, 2, 1, blk), F32), pltpu.VMEM((nq, 2, LANES, blk), F32),
                        pltpu.VMEM((2, blk, LANES), F32), pltpu.VMEM((blk, LANES), F32)],
        compiler_params=_params(("arbitrary", "arbitrary", "arbitrary")),
    )(z, z, z, qterm, kterm, o, do, lse)


def _split3(c):
    c1 = c.astype(BF16)
    r1 = c - c1.astype(F32)
    c2 = r1.astype(BF16)
    c3 = (r1 - c2.astype(F32)).astype(BF16)
    return c1, c2, c3


def _mesh_pos():
    return lax.axis_index("x"), lax.axis_index("y"), lax.axis_index("c")


def _flip(v, bit):
    return 1 - v if bit else v


def all_gather(name, blocks):
    n = len(blocks)

    def body(*refs):
        x_refs, out_refs = refs[:n], refs[n:2 * n]
        send_sems, recv_sems, local_sems = refs[2 * n:]
        x, y, c = _mesh_pos()
        me, sibling = (x, y, c), (x, y, 1 - c)
        chips = [(1 - x, y), (x, 1 - y), (1 - x, 1 - y)]

        def slot(a, px, py, pc):
            return out_refs[a].at[4 * px + 2 * py + pc]

        def copy(a, k, blk, to, src=None):
            return pltpu.make_async_remote_copy(
                src_ref=slot(a, *blk) if src is None else src, dst_ref=slot(a, *blk),
                send_sem=send_sems.at[a, k], recv_sem=recv_sems.at[a, k], device_id=to, device_id_type=MESH)

        mine = [pltpu.make_async_copy(x_refs[a], slot(a, *me), local_sems.at[a]) for a in range(n)]
        for cp in mine:
            cp.start()
        sends = []
        for a in range(n):
            sends.append(copy(a, 0, me, sibling, src=x_refs[a]))
            sends += [copy(a, 1 + j, me, (*chip, c), src=x_refs[a]) for j, chip in enumerate(chips)]
        for cp in sends:
            cp.start()
        for j, chip in enumerate(chips):
            for a in range(n):
                copy(a, 1 + j, (*chip, c), me).wait_recv()
                passed = copy(a, 4 + j, (*chip, c), sibling)
                passed.start()
                sends.append(passed)
        for a in range(n):
            copy(a, 0, sibling, me).wait_recv()
            for j, chip in enumerate(chips):
                copy(a, 4 + j, (*chip, 1 - c), me).wait_recv()
        for cp in sends:
            cp.wait_send()
        for cp in mine:
            cp.wait()

    hbm = pl.BlockSpec(memory_space=pl.ANY)
    return pl.pallas_call(
        body, name=name, out_shape=[jax.ShapeDtypeStruct((N_DEV,) + b.shape, b.dtype) for b in blocks],
        in_specs=[hbm] * n, out_specs=[hbm] * n,
        scratch_shapes=[pltpu.SemaphoreType.DMA((n, 7)), pltpu.SemaphoreType.DMA((n, 7)), pltpu.SemaphoreType.DMA((n,))],
    )(*blocks)


def _peers(x, y, c):
    return [(_flip(x, k & 4), _flip(y, k & 2), _flip(c, k & 1)) for k in range(1, N_DEV)]


def gather_start(name, blocks, lands):
    n = len(blocks)

    def body(*refs):
        x_refs, land_refs = refs[:n], refs[n:2 * n]
        send_sems, recv_sems = refs[2 * n], refs[2 * n + 1]
        token = refs[-1]
        x, y, c = _mesh_pos()
        me = 4 * x + 2 * y + c
        for k, peer in enumerate(_peers(x, y, c)):
            for a in range(n):
                pltpu.make_async_remote_copy(
                    src_ref=x_refs[a], dst_ref=land_refs[a].at[me], send_sem=send_sems.at[7 * a + k], recv_sem=recv_sems.at[7 * a + k],
                    device_id=peer, device_id_type=MESH).start()
        token[...] = jnp.zeros_like(token)

    hbm = pl.BlockSpec(memory_space=pltpu.HBM)
    sem = pl.BlockSpec(memory_space=pltpu.SEMAPHORE)
    out_shape = ([pltpu.SemaphoreType.DMA((7 * n,)), pltpu.SemaphoreType.DMA((7 * n,))]
                 + [pltpu.HBM(b.shape, b.dtype) for b in blocks] + [pltpu.HBM(l.shape, l.dtype) for l in lands]
                 + [jax.ShapeDtypeStruct((8, LANES), F32)])
    res = pl.pallas_call(
        body, name=name, out_shape=out_shape, in_specs=[hbm] * (2 * n),
        out_specs=[sem, sem] + [hbm] * (2 * n) + [pl.BlockSpec(memory_space=pltpu.VMEM)],
        input_output_aliases={a: 2 + a for a in range(2 * n)},
        compiler_params=pltpu.CompilerParams(has_side_effects=pltpu.SideEffectType.DATAFLOW_SIDE_EFFECTING),
    )(*[pltpu.with_memory_space_constraint(b, pltpu.HBM) for b in blocks],
      *[pltpu.with_memory_space_constraint(l, pltpu.HBM) for l in lands])
    return res[0], res[1], res[2:2 + n], res[2 + n:2 + 2 * n], res[-1]


def gather_wait(name, send_sems, recv_sems, blocks, lands, after):
    n = len(blocks)

    def body(*refs):
        x_refs, land_refs = refs[:n], refs[n:2 * n]
        s_sems, r_sems = refs[2 * n], refs[2 * n + 1]
        x, y, c = _mesh_pos()
        me = 4 * x + 2 * y + c
        for k, peer in enumerate(_peers(x, y, c)):
            for a in range(n):
                cp = pltpu.make_async_remote_copy(
                    src_ref=x_refs[a], dst_ref=land_refs[a].at[me], send_sem=s_sems.at[7 * a + k], recv_sem=r_sems.at[7 * a + k],
                    device_id=peer, device_id_type=MESH)
                cp.wait_send()
                cp.wait_recv()

    hbm = pl.BlockSpec(memory_space=pltpu.HBM)
    sem = pl.BlockSpec(memory_space=pltpu.SEMAPHORE)
    res = pl.pallas_call(
        body, name=name,
        out_shape=[pltpu.HBM(b.shape, b.dtype) for b in blocks] + [pltpu.HBM(l.shape, l.dtype) for l in lands],
        in_specs=[hbm] * (2 * n) + [sem, sem, pl.BlockSpec(memory_space=pl.ANY)], out_specs=[hbm] * (2 * n),
        input_output_aliases={a: a for a in range(2 * n)},
        compiler_params=pltpu.CompilerParams(has_side_effects=pltpu.SideEffectType.DATAFLOW_SIDE_EFFECTING),
    )(*blocks, *lands, send_sems, recv_sems, after)
    return res[n:]


def _split_exchange(name, sends, lands, sems, after):
    n = len(sends)
    starting = sems is None

    def body(*refs):
        s_refs, l_refs = refs[:n], refs[n:2 * n]
        send_sems, recv_sems = refs[2 * n], refs[2 * n + 1]
        x, y, c = _mesh_pos()
        me = 4 * x + 2 * y + c
        for k, (px, py, pc) in enumerate(_peers(x, y, c)):
            for a in range(n):
                cp = pltpu.make_async_remote_copy(
                    src_ref=s_refs[a].at[4 * px + 2 * py + pc], dst_ref=l_refs[a].at[me],
                    send_sem=send_sems.at[7 * a + k], recv_sem=recv_sems.at[7 * a + k],
                    device_id=(px, py, pc), device_id_type=MESH)
                if starting:
                    cp.start()
                else:
                    cp.wait_send()
                    cp.wait_recv()
        if starting:
            refs[-1][...] = jnp.zeros_like(refs[-1])

    hbm = pl.BlockSpec(memory_space=pltpu.HBM)
    sem = pl.BlockSpec(memory_space=pltpu.SEMAPHORE)
    thru = [pltpu.HBM(t.shape, t.dtype) for t in list(sends) + list(lands)]
    effect = pltpu.CompilerParams(has_side_effects=pltpu.SideEffectType.DATAFLOW_SIDE_EFFECTING)
    if starting:
        res = pl.pallas_call(
            body, name=name, in_specs=[hbm] * (2 * n),
            out_shape=[pltpu.SemaphoreType.DMA((7 * n,)), pltpu.SemaphoreType.DMA((7 * n,))] + thru + [jax.ShapeDtypeStruct((8, LANES), F32)],
            out_specs=[sem, sem] + [hbm] * (2 * n) + [pl.BlockSpec(memory_space=pltpu.VMEM)],
            input_output_aliases={a: 2 + a for a in range(2 * n)}, compiler_params=effect,
        )(*[pltpu.with_memory_space_constraint(t, pltpu.HBM) for t in list(sends) + list(lands)])
        return res[0], res[1], res[2:2 + n], res[2 + n:2 + 2 * n], res[-1]
    res = pl.pallas_call(
        body, name=name, out_shape=thru, in_specs=[hbm] * (2 * n) + [sem, sem, pl.BlockSpec(memory_space=pl.ANY)],
        out_specs=[hbm] * (2 * n), input_output_aliases={a: a for a in range(2 * n)}, compiler_params=effect,
    )(*sends, *lands, sems[0], sems[1], after)
    return res[n:]


def unwritten(name, like):
    def body(*refs):
        pass

    hbm = pl.BlockSpec(memory_space=pl.ANY)
    return pl.pallas_call(body, name=name, out_shape=[jax.ShapeDtypeStruct(t.shape, t.dtype) for t in like],
                          out_specs=[hbm] * len(like))()


def own_slot_only(send, land, me):
    mine = lax.dynamic_index_in_dim(send, me, 0, keepdims=False)
    return lax.dynamic_update_index_in_dim(land, mine, me, 0)


def all_to_all(name, sends):
    n = len(sends)

    def body(*refs):
        s_refs, r_refs = refs[:n], refs[n:2 * n]
        send_sems, recv_sems, local_sems = refs[2 * n:]
        x, y, c = _mesh_pos()
        me = 4 * x + 2 * y + c
        mine = [pltpu.make_async_copy(s_refs[a].at[me], r_refs[a].at[me], local_sems.at[a]) for a in range(n)]
        for cp in mine:
            cp.start()
        copies = []
        for k in range(1, N_DEV):
            px, py, pc = _flip(x, k & 4), _flip(y, k & 2), _flip(c, k & 1)
            for a in range(n):
                copies.append(pltpu.make_async_remote_copy(
                    src_ref=s_refs[a].at[4 * px + 2 * py + pc], dst_ref=r_refs[a].at[me],
                    send_sem=send_sems.at[a, k - 1], recv_sem=recv_sems.at[a, k - 1],
                    device_id=(px, py, pc), device_id_type=MESH))
        for cp in copies:
            cp.start()
        for cp in copies:
            cp.wait_recv()
        for cp in copies:
            cp.wait_send()
        for cp in mine:
            cp.wait()

    hbm = pl.BlockSpec(memory_space=pl.ANY)
    return pl.pallas_call(
        body, name=name, out_shape=[jax.ShapeDtypeStruct(s.shape, s.dtype) for s in sends],
        in_specs=[hbm] * n, out_specs=[hbm] * n,
        scratch_shapes=[pltpu.SemaphoreType.DMA((n, 7)), pltpu.SemaphoreType.DMA((n, 7)), pltpu.SemaphoreType.DMA((n,))],
    )(*sends)


def _row_tile(r, cap, step):
    return next((t for t in range(cap, step - 1, -step) if r % t == 0), r)


def _sum_parts(p, n):
    t = [p[k].astype(F32) for k in range(n)]
    while len(t) > 1:
        t = [t[k] + t[k + 1] for k in range(0, len(t), 2)]
    return t[0]


def _adam(g, w, m, v):
    m = ADAM_B1 * m + (1.0 - ADAM_B1) * g
    v = ADAM_B2 * v + (1.0 - ADAM_B2) * (g * g)
    m_hat = m / (1.0 - ADAM_B1 ** ADAM_STEP)
    v_hat = v / (1.0 - ADAM_B2 ** ADAM_STEP)
    return -ADAM_LR * (m_hat / (jnp.sqrt(v_hat) + ADAM_EPS) + ADAM_WD * w), m, v


def adam_tiled(name, partials, w, m_, v_, layer=0, prev=None):
    _, r, c = w.shape
    n_part = partials.shape[0]
    tr = _row_tile(r, 256, 16)

    def body(*refs):
        p_ref, w_ref, m_ref, v_ref = refs[:4]
        g_ref, d_ref, nm_ref, nv_ref = refs[-4:]
        g = _sum_parts(p_ref, n_part)
        g_ref[...] = g
        d_ref[...], nm_ref[...], nv_ref[...] = _adam(g, w_ref[...], m_ref[...], v_ref[...])

    spec = pl.BlockSpec((None, tr, c), lambda i: (layer, i, 0))
    in_specs = [pl.BlockSpec((n_part, None, tr, c), lambda i: (0, 0, i, 0)), spec, spec, spec]
    args = [partials, w, m_, v_]
    aliases = {}
    if prev is not None:
        in_specs += [pl.BlockSpec(memory_space=pl.ANY)] * 4
        args += list(prev)
        aliases = {4 + k: k for k in range(4)}
    return pl.pallas_call(
        body, grid=(r // tr,), name=name, in_specs=in_specs,
        out_specs=[spec] * 4, out_shape=[jax.ShapeDtypeStruct(w.shape, F32)] * 4,
        input_output_aliases=aliases, compiler_params=_params(("parallel",)),
    )(*args)


def adam_small(name, items, extra):
    n, ne = len(items), len(extra)

    def body(*refs):
        ins, outs = refs[:4 * n + ne], refs[4 * n + ne:]
        for a in range(n):
            p_ref, w_ref, m_ref, v_ref = ins[4 * a:4 * a + 4]
            g = _sum_parts(p_ref, N_DEV)
            outs[4 * a][...] = g
            outs[4 * a + 1][...], outs[4 * a + 2][...], outs[4 * a + 3][...] = _adam(g, w_ref[...], m_ref[...], v_ref[...])
        for e in range(ne):
            outs[4 * n + e][...] = _sum_parts(ins[4 * n + e], N_DEV)

    args, out_shape = [], []
    for p, w, m_, v_ in items:
        args += [p, w, m_, v_]
        out_shape += [jax.ShapeDtypeStruct(w.shape, F32)] * 4
    for e in extra:
        args.append(e)
        out_shape.append(jax.ShapeDtypeStruct(e.shape[1:], F32))
    vmem = pl.BlockSpec(memory_space=pltpu.VMEM)
    res = pl.pallas_call(body, name=name, in_specs=[vmem] * len(args), out_specs=[vmem] * len(out_shape), out_shape=out_shape)(*args)
    return [res[4 * a:4 * a + 4] for a in range(n)], res[4 * n:]


def _cols_from_gather(g):
    g = jnp.moveaxis(g, 0, -2)
    return g.reshape(g.shape[:-2] + (g.shape[-2] * g.shape[-1],))


def _cols_to_blocks(w):
    w = w.reshape(w.shape[:-1] + (N_DEV, w.shape[-1] // N_DEV))
    return jnp.moveaxis(w, -2, 0)


def _block_diag(w):
    z = jnp.zeros((B_BLOCK_DIM, B_BLOCK_DIM), w.dtype)
    rows = []
    for j in range(B_BLOCKS // 2):
        top = jnp.concatenate([w[2 * j], z], axis=1)
        bot = jnp.concatenate([z, w[2 * j + 1]], axis=1)
        rows.append(jnp.concatenate([top, bot], axis=0))
    return jnp.concatenate(rows, axis=0)


def _block_diag_grad(d):
    out = []
    for j in range(B_BLOCKS // 2):
        blk = d[128 * j:128 * (j + 1)]
        out.append(blk[:64, :64])
        out.append(blk[64:, 64:])
    return jnp.stack(out)


NAMES = ("norm_gains", "even_w_in", "hgrn_lb_logits", "hgrn_norm", "rg_conv_w", "rg_conv_b", "rg_wa", "rg_ba", "rg_wx", "rg_bx",
         "rg_lambda", "even_w_out", "odd_w_in", "fox_f_bias", "odd_w_out", "ffn_w_up", "ffn_conv_w", "ffn_conv_b", "ffn_w_down")
SMALL_SHARDED = ("norm_gains", "rg_conv_w", "ffn_conv_w")
REPLICATED = ("hgrn_lb_logits", "hgrn_norm", "rg_conv_b", "rg_wa", "rg_ba", "rg_wx", "rg_bx", "rg_lambda", "fox_f_bias", "ffn_conv_b")


def _ffn_forward(tag, layer, h, w_up_g, cw5, cb5, w_down_g, m, seq):
    tm = _div_tile(m, 1024)
    nm = m // tm
    hid = mm(f"{tag}_up", "nn",
             Blk(h, (tm, D_MODEL), lambda i, j, k: (i, 0)),
             Blk(w_up_g, (None, None, D_MODEL, FF_BLK), lambda i, j, k: (j, 0, 0, 0)),
             Blk((N_DEV, m, FF_BLK), (None, tm, FF_BLK), lambda i, j, k: (j, i, 0)), F32, (nm, N_DEV, 1))
    hid = hid.reshape(2, N_DEV // 2, m, FF_BLK)
    act, conv = ffn_mid_fwd(f"{tag}_mid", hid, cw5, cb5, layer, m=m, seq=seq)
    f = mm(f"{tag}_down", "nn",
           Blk(act, (None, tm, FF_BLK), lambda i, j, k: (k, i, 0)),
           Blk(w_down_g, (2, None, FF_BLK // 2, D_MODEL), lambda i, j, k: (k, 0, 0, 0)),
           Blk((m, D_MODEL), (tm, D_MODEL), lambda i, j, k: (i, 0)), F32, (nm, 1, N_DEV // 2))
    return (hid, conv), act, f


def _ffn_backward(tag, layer, df, h, hid, act, w_up_g, cw5, cb5, w_down_g, m, seq):
    tm = _div_tile(m, 1024)
    nm = m // tm
    dact = mm(f"{tag}_dact", "nt",
              Blk(df, (tm, D_MODEL), lambda i, j, k: (i, 0)),
              Blk(w_down_g, (2, None, FF_BLK // 2, D_MODEL), lambda i, j, k: (j, 0, 0, 0)),
              Blk((N_DEV // 2, m, FF_BLK), (None, tm, FF_BLK), lambda i, j, k: (j, i, 0)), BF16, (nm, N_DEV // 2, 1))
    d_wdown = mm(f"{tag}_dwdown", "tn",
                 Blk(act, (None, tm, FF_BLK), lambda i, j, k: (i, k, 0)),
                 Blk(df, (tm, D_MODEL), lambda i, j, k: (k, 0)),
                 Blk(w_down_g.shape, (2, None, FF_BLK // 2, D_MODEL), lambda i, j, k: (i, 0, 0, 0)), BF16,
                 (N_DEV // 2, 1, nm))
    dhid, d_cw, d_cb = ffn_mid_bwd(f"{tag}_dmid", hid[0], hid[1], cw5, dact, layer, m=m, seq=seq)
    dhid = dhid.reshape(N_DEV, m, FF_BLK)
    dh = mm(f"{tag}_dh", "nt",
            Blk(dhid, (None, tm, FF_BLK), lambda i, j, k: (k, i, 0)),
            Blk(w_up_g, (None, None, D_MODEL, FF_BLK), lambda i, j, k: (k, 0, 0, 0)),
            Blk((m, D_MODEL), (tm, D_MODEL), lambda i, j, k: (i, 0)), BF16, (nm, 1, N_DEV))
    d_wup = mm(f"{tag}_dwup", "tn",
               Blk(dhid, (None, tm, FF_BLK), lambda i, j, k: (i, k, 0)),
               Blk(h, (tm, D_MODEL), lambda i, j, k: (k, 0)),
               Blk((N_DEV, 1, FF_BLK, D_MODEL), (None, None, FF_BLK, D_MODEL), lambda i, j, k: (i, 0, 0, 0)), BF16,
               (N_DEV, 1, nm))
    return dh, d_wup, d_cw, d_cb, d_wdown


def kernel(x, norm_gains, even_w_in, hgrn_lb_logits, hgrn_norm, rg_conv_w, rg_conv_b, rg_wa, rg_ba, rg_wx, rg_bx, rg_lambda, even_w_out, odd_w_in, fox_f_bias, odd_w_out, ffn_w_up, ffn_conv_w, ffn_conv_b, ffn_w_down, loss_target, m_norm_gains, m_even_w_in, m_hgrn_lb_logits, m_hgrn_norm, m_rg_conv_w, m_rg_conv_b, m_rg_wa, m_rg_ba, m_rg_wx, m_rg_bx, m_rg_lambda, m_even_w_out, m_odd_w_in, m_fox_f_bias, m_odd_w_out, m_ffn_w_up, m_ffn_conv_w, m_ffn_conv_b, m_ffn_w_down, v_norm_gains, v_even_w_in, v_hgrn_lb_logits, v_hgrn_norm, v_rg_conv_w, v_rg_conv_b, v_rg_wa, v_rg_ba, v_rg_wx, v_rg_bx, v_rg_lambda, v_even_w_out, v_odd_w_in, v_fox_f_bias, v_odd_w_out, v_ffn_w_up, v_ffn_conv_w, v_ffn_conv_b, v_ffn_w_down):
    local = dict(locals())
    w = {n: local[n] for n in NAMES}
    mom = {n: local["m_" + n] for n in NAMES}
    var = {n: local["v_" + n] for n in NAMES}
    n_batch, seq, _ = x.shape
    m = n_batch * seq
    tm = _div_tile(m, 512)
    tmm = _div_tile(m, 1024)
    nm = m // tmm

    gathered = all_gather("gather_weights", [w["even_w_in"].astype(BF16)] + [w[n] for n in SMALL_SHARDED])
    g = dict(zip(("even_w_in",) + SMALL_SHARDED, gathered))
    w_in_e = g["even_w_in"]
    gains = _cols_from_gather(g["norm_gains"])
    me = 4 * lax.axis_index("x") + 2 * lax.axis_index("y") + lax.axis_index("c")
    def own_block_only(name, blocks):
        lands = unwritten(name, [jax.ShapeDtypeStruct((N_DEV,) + t.shape, t.dtype) for t in blocks])
        return [lax.dynamic_update_index_in_dim(ld, t, me, 0) for ld, t in zip(lands, blocks)]

    def own_slots_only(name, sends):
        return [own_slot_only(t, ld, me) for t, ld in zip(sends, unwritten(name, sends))]

    behind = (g["norm_gains"][0, 0, 0, 0] * 0.0).astype(BF16)
    out0 = [w["even_w_out"].astype(BF16) + behind]
    out0_sent = gather_start("gather_out0_start", out0, own_block_only("land_out0", out0))
    behind = (out0_sent[4][0, 0] * 0.0).astype(BF16)
    ffn0 = [w["ffn_w_up"][0:1].astype(BF16) + behind, w["ffn_w_down"][0:1].astype(BF16) + behind]
    ffn0_sent = gather_start("gather_ffn0_start", ffn0, own_block_only("land_ffn0", ffn0))
    behind = (ffn0_sent[4][0, 0] * 0.0).astype(BF16)
    mix1w = [jnp.swapaxes(w["odd_w_in"], 1, 2).astype(BF16) + behind, w["odd_w_out"].astype(BF16) + behind]
    mix1_sent = gather_start("gather_mix1_start", mix1w, own_block_only("land_mix1", mix1w))
    behind = (mix1_sent[4][0, 0] * 0.0).astype(BF16)
    ffn1 = [w["ffn_w_up"][1:2].astype(BF16) + behind, w["ffn_w_down"][1:2].astype(BF16) + behind]
    ffn1_sent = gather_start("gather_ffn1_start", ffn1, own_block_only("land_ffn1", ffn1))
    started = ffn1_sent[4]
    rg_cw = _cols_from_gather(g["rg_conv_w"])[0]
    n_layer = ffn_conv_w.shape[0]
    cw5 = g["ffn_conv_w"].reshape(2, N_DEV // 2, n_layer, FFN_CONV, FF_BLK)
    cb5 = ffn_conv_b.reshape(n_layer, 2, N_DEV // 2, 1, FF_BLK)
    gain = lambda l, k: gains[l, k:k + 1, :]
    wa_bd, wx_bd = _block_diag(rg_wa[0]), _block_diag(rg_wx[0])
    fbias = jnp.pad(fox_f_bias, ((0, 0), (0, LANES - C_HEADS)))

    x0 = x.reshape(m, D_MODEL)
    tgt = loss_target.reshape(m, D_MODEL)

    (h0,) = tile_fwd("l0_prenorm", fn_prenorm_after, m=m, tm=tm, nj=1, rows=[Row(x0)], pars=[Par(gain(0, 0)), Par(started)],
                     outs=[Out(D_MODEL, BF16)])
    z0 = mm("l0_in", "nn",
            Blk(h0, (tmm, D_MODEL), lambda i, j, k: (i, 0)),
            Blk(w_in_e, (2, None, D_MODEL, 384), lambda i, j, k: (j, 0, 0, 0)),
            Blk((m, 3072), (tmm, 768), lambda i, j, k: (i, j)), F32, (nm, N_DEV // 2, 1), b_join=True)
    oa, sprev = hgrn_fwd("l0_hgrn", z0, hgrn_lb_logits, hgrn_norm, n_batch=n_batch, seq=seq)
    rg_rows = lambda: [Row(z0, LANES, 16), Row(z0, LANES, 20)]
    rg_pars = lambda: [Par(rg_cw, "col", LANES), Par(rg_conv_b, "col", LANES), Par(wa_bd, "row", LANES), Par(rg_ba, "col", LANES),
                       Par(wx_bd, "row", LANES), Par(rg_bx, "col", LANES), Par(rg_lambda, "col", LANES)]
    (ob,) = tile_fwd("l0_rglru", fn_rglru, m=m, tm=seq, nj=B_WIDTH // LANES, rows=rg_rows(), pars=rg_pars(),
                     outs=[Out(B_WIDTH, BF16, LANES)])
    mixcat0 = jnp.concatenate([oa, ob], axis=-1)
    (g_out_e,) = gather_wait("gather_out0_wait", out0_sent[0], out0_sent[1], out0_sent[2], out0_sent[3], mixcat0)
    w_out_e = g_out_e.reshape(D_MODEL, D_MODEL)
    mix0 = mm2d("l0_out", "nn", mixcat0, w_out_e)
    x1, h1 = tile_fwd("l0_postnorm", fn_addnorm2, m=m, tm=tm, nj=1, rows=[Row(x0), Row(mix0)], pars=[Par(gain(0, 1)), Par(gain(0, 2))],
                      outs=[Out(D_MODEL, F32), Out(D_MODEL, BF16)])
    w_up_g0, w_down_g0 = gather_wait("gather_ffn0_wait", ffn0_sent[0], ffn0_sent[1], ffn0_sent[2], ffn0_sent[3], h1)
    hid0, act0, f0 = _ffn_forward("l0_ffn", 0, h1, w_up_g0, cw5, cb5, w_down_g0, m, seq)
    x2, h2 = tile_fwd("l0_ffnnorm", fn_addnorm2, m=m, tm=tm, nj=1, rows=[Row(x1), Row(f0)], pars=[Par(gain(0, 3)), Par(gain(1, 0))],
                      outs=[Out(D_MODEL, F32), Out(D_MODEL, BF16)])

    g_in_o, g_out_o = gather_wait("gather_mix1_wait", mix1_sent[0], mix1_sent[1], mix1_sent[2], mix1_sent[3], h2)
    w_in_o_t = jnp.pad(g_in_o.reshape(3088, D_MODEL), ((0, 3200 - 3088), (0, 0)))
    w_out_o = g_out_o.reshape(D_MODEL, D_MODEL)
    z1 = mm2d("l1_in", "nt", h2, w_in_o_t)
    (cgate,) = tile_fwd("l1_gate", fn_fox_gate, m=m, tm=seq, nj=1, rows=[Row(z1, LANES, 3072 // LANES)], pars=[Par(fbias)],
                        outs=[Out(LANES, F32)])
    place, ones_q, ones_k = term_placement()
    qterm, kterm = tile_fwd("l1_terms", fn_fox_terms, m=m, tm=tm, nj=1, rows=[Row(cgate)],
                            pars=[Par(place), Par(ones_q), Par(ones_k)], outs=[Out(TERM_W, BF16), Out(TERM_W, BF16)])
    oc, lse = fox_pair_fwd("l1_attn", z1, qterm, kterm, n_batch=n_batch, seq=seq)
    blk_b = min(ATT_BLK, seq)
    lse = lse.reshape(n_batch, N_PAIR, -1, 2, lse.shape[-1] // blk_b, blk_b).swapaxes(3, 4).reshape(n_batch, N_PAIR, seq // blk_b, 2, 1, blk_b)
    mix1 = mm2d("l1_out", "nn", oc, w_out_o)
    x3, h3 = tile_fwd("l1_postnorm", fn_addnorm2, m=m, tm=tm, nj=1, rows=[Row(x2), Row(mix1)], pars=[Par(gain(1, 1)), Par(gain(1, 2))],
                      outs=[Out(D_MODEL, F32), Out(D_MODEL, BF16)])
    w_up_g1, w_down_g1 = gather_wait("gather_ffn1_wait", ffn1_sent[0], ffn1_sent[1], ffn1_sent[2], ffn1_sent[3], h3)
    hid1, act1, f1 = _ffn_forward("l1_ffn", 1, h3, w_up_g1, cw5, cb5, w_down_g1, m, seq)
    dy, df1, loss_part, d_g13 = loss_head("loss", x3, f1, tgt, gain(1, 3), m=m, tm=tm)
    dh3, d_wup1, d_cw1, d_cb1, d_wdown1 = _ffn_backward("l1_ffn", 1, df1, h3, hid1, act1, w_up_g1, cw5, cb5, w_down_g1, m, seq)
    dx2, dmix1, d_g11, d_g12 = tile_bwd("l1_dpostnorm", fn_addnorm2, m=m, tm=tm, nj=1, rows=[Row(x2), Row(mix1)],
                                        pars=[Par(gain(1, 1)), Par(gain(1, 2))], cts=[Row(dy), Row(dh3)],
                                        drows=[Out(D_MODEL, F32), Out(D_MODEL, BF16)])
    doc = mm2d("l1_doc", "nt", dmix1, w_out_o, BF16)
    d_wout_o = mm2d("l1_dwout", "tn", oc, dmix1)
    dq, dk, dv, dc = fox_pair_bwd("l1_dattn", z1, qterm, kterm, oc, doc, lse, n_batch=n_batch, seq=seq)
    dzf, d_fbias = tile_bwd("l1_dgate", fn_fox_gate, m=m, tm=seq, nj=1, rows=[Row(z1, LANES, 3072 // LANES)], pars=[Par(fbias)],
                            cts=[Row(dc)], drows=[Out(LANES, BF16)])
    dz1 = jnp.concatenate([dq, dk, dv, dzf], axis=-1)
    dh2 = mm2d("l1_dh", "nn", dz1, w_in_o_t, BF16)
    d_win_o_t = mm2d("l1_dwin", "tn", dz1, h2, BF16)

    send1 = [d_win_o_t[:3088].reshape(N_DEV, 1, 3088 // N_DEV, D_MODEL),
             d_wout_o.reshape(N_DEV, 1, D_MODEL // N_DEV, D_MODEL).astype(BF16), d_wup1, d_wdown1]
    sent1 = _split_exchange("exchange_l1_start", send1, own_slots_only("land_l1", send1), None, None)

    dx1, df0, d_g03, d_g10 = tile_bwd("l0_dffnnorm", fn_addnorm2_after, m=m, tm=tm, nj=1, rows=[Row(x1), Row(f0)],
                                      pars=[Par(gain(0, 3)), Par(gain(1, 0)), Par(sent1[4])], cts=[Row(dx2), Row(dh2)],
                                      drows=[Out(D_MODEL, F32), Out(D_MODEL, BF16)])[:4]
    dh1, d_wup0, d_cw0, d_cb0, d_wdown0 = _ffn_backward("l0_ffn", 0, df0, h1, hid0, act0, w_up_g0, cw5, cb5, w_down_g0, m, seq)
    send0 = [d_wup0, d_wdown0]
    sent0 = _split_exchange("exchange_ffn0_start", send0, own_slots_only("land_dffn0", send0), None, None)
    dx0a, dmix0, d_g01, d_g02 = tile_bwd("l0_dpostnorm", fn_addnorm2_after, m=m, tm=tm, nj=1, rows=[Row(x0), Row(mix0)],
                                         pars=[Par(gain(0, 1)), Par(gain(0, 2)), Par(sent0[4])], cts=[Row(dx1), Row(dh1)],
                                         drows=[Out(D_MODEL, F32), Out(D_MODEL, BF16)])[:4]
    dmixcat0 = mm2d("l0_dmixcat", "nt", dmix0, w_out_e, BF16)
    d_wout_e = mm2d("l0_dwout", "tn", mixcat0, dmix0)
    dzq, dzf0, dzv, dzg, d_lb, d_hnorm = hgrn_bwd("l0_dhgrn", z0, sprev, hgrn_lb_logits, hgrn_norm, dmixcat0, n_batch=n_batch, seq=seq)
    dzx, dzy, d_rcw, d_rcb, d_wa, d_ba, d_wx, d_bx, d_lam = tile_bwd(
        "l0_drglru", fn_rglru, m=m, tm=seq, nj=B_WIDTH // LANES, rows=rg_rows(), pars=rg_pars(),
        cts=[Row(dmixcat0, LANES, A_WIDTH // LANES)], drows=[Out(B_WIDTH, BF16, LANES), Out(B_WIDTH, BF16, LANES)])
    dz0 = jnp.concatenate([dzq, dzf0, dzv, dzg, dzx, dzy], axis=-1)
    d_win_e = mm("l0_dwin", "tn",
                 Blk(h0, (tmm, D_MODEL), lambda i, j, k: (k, 0)),
                 Blk(dz0, (tmm, 768), lambda i, j, k: (k, j)),
                 Blk(w_in_e.shape, (2, None, D_MODEL, 384), lambda i, j, k: (j, 0, 0, 0)), BF16, (1, N_DEV // 2, nm), o_split=True)
    send_e = [d_win_e, d_wout_e.reshape(N_DEV, 1, D_MODEL // N_DEV, D_MODEL).astype(BF16)]
    sent_e = _split_exchange("exchange_even_start", send_e, own_slots_only("land_even", send_e), None, None)
    d_ffn_cb = jnp.stack([d_cb0, d_cb1]).reshape(n_layer, 2 * D_FF)
    rep = {"hgrn_lb_logits": d_lb, "hgrn_norm": d_hnorm, "rg_conv_b": d_rcb, "rg_wa": _block_diag_grad(d_wa)[None], "rg_ba": d_ba,
           "rg_wx": _block_diag_grad(d_wx)[None], "rg_bx": d_bx, "rg_lambda": d_lam, "fox_f_bias": d_fbias[:, :C_HEADS],
           "ffn_conv_b": d_ffn_cb}
    rep_blocks = [rep[n] for n in REPLICATED] + [loss_part]
    rep_sent = gather_start("gather_partials_start", rep_blocks, own_block_only("land_partials", rep_blocks))
    dh0 = mm("l0_dh", "nt",
             Blk(dz0, (tmm, 768), lambda i, j, k: (i, k)),
             Blk(w_in_e, (2, None, D_MODEL, 384), lambda i, j, k: (k, 0, 0, 0)),
             Blk((m, D_MODEL), (tmm, D_MODEL), lambda i, j, k: (i, 0)), BF16, (nm, 1, N_DEV // 2), after=sent_e[4] + rep_sent[4],
             b_join=True)
    dx0, d_g00 = tile_bwd("l0_dprenorm", fn_input_norm, m=m, tm=tm, nj=1, rows=[Row(x0)], pars=[Par(gain(0, 0))],
                          cts=[Row(dx0a), Row(dh0)], drows=[Out(D_MODEL, F32)])

    d_gains = jnp.stack([jnp.concatenate([d_g00, d_g01, d_g02, d_g03], axis=0), jnp.concatenate([d_g10, d_g11, d_g12, d_g13], axis=0)])
    d_ffn_cw = jnp.stack([d_cw0, d_cw1], axis=2).reshape(N_DEV, n_layer, FFN_CONV, FF_BLK)
    r_in_o, r_out_o, r_up1, r_down1 = _split_exchange("exchange_l1_wait", sent1[2], sent1[3], sent1[:2], dx0)
    r_up0, r_down0 = _split_exchange("exchange_ffn0_wait", sent0[2], sent0[3], sent0[:2], dx0)
    r_in_e, r_out_e = _split_exchange("exchange_even_wait", sent_e[2], sent_e[3], sent_e[:2], dx0)
    recv, res = {}, {}
    flipped = ("odd_w_in", "ffn_w_up")
    view = lambda n, t: jnp.swapaxes(t, 1, 2) if n in flipped else t
    for n, r in (("even_w_in", r_in_e), ("even_w_out", r_out_e), ("odd_w_in", r_in_o), ("odd_w_out", r_out_o)):
        res[n] = [view(n, t) for t in adam_tiled("adam_" + n, r, view(n, w[n]), view(n, mom[n]), view(n, var[n]))]
    for n, parts_l in (("ffn_w_up", (r_up0, r_up1)), ("ffn_w_down", (r_down0, r_down1))):
        wmv = (view(n, w[n]), view(n, mom[n]), view(n, var[n]))
        first_layer = adam_tiled(f"adam_{n}_0", parts_l[0], *wmv, layer=0)
        res[n] = [view(n, t) for t in adam_tiled(f"adam_{n}_1", parts_l[1], *wmv, layer=1, prev=first_layer)]
    small_send = [_cols_to_blocks(d_gains), _cols_to_blocks(d_rcw[None]), d_ffn_cw]
    recv.update(zip(SMALL_SHARDED, all_to_all("exchange_small", small_send)))

    parts = gather_wait("gather_partials_wait", rep_sent[0], rep_sent[1], rep_sent[2], rep_sent[3], dx0)
    for n, p in zip(REPLICATED, parts):
        recv[n] = p
    small = SMALL_SHARDED + REPLICATED
    small_res, (loss_sum,) = adam_small("adam_small", [(recv[n], w[n], mom[n], var[n]) for n in small], [parts[-1]])
    res.update(dict(zip(small, small_res)))

    out = [loss_sum[0, 0], dx0.reshape(x.shape)]
    for k in range(4):
        out += [res[n][k] for n in NAMES]
    return tuple(out)
```

```python
import functools

import jax
import jax.numpy as jnp
from jax import lax
from jax.experimental import pallas as pl
from jax.experimental.pallas import tpu as pltpu

F32 = jnp.float32
BF16 = jnp.bfloat16

D_MODEL = 1024
A_HEADS = 4
A_WIDTH = 512
HGRN_CHUNK = 64
HGRN_SEG = 512
B_WIDTH = 512
B_BLOCKS = 8
B_BLOCK_DIM = 64
B_CONV = 4
RG_C = 8.0
C_HEADS = 16
C_HEAD_DIM = 64
D_FF = 2816
FFN_CONV = 3
EPS = 1e-6
LANES = 128
HALO = 16
N_DEV = 8
FF_BLK = 2 * D_FF // N_DEV
MESH = pl.DeviceIdType.MESH
NEG = -1e30
VMEM_LIMIT = 56 * 1024 * 1024

ADAM_LR = 0.001
ADAM_B1 = 0.9
ADAM_B2 = 0.999
ADAM_EPS = 1e-08
ADAM_WD = 0.01
ADAM_STEP = 10


def _dg(a, b, pat):
    nb = a.ndim - 2
    batch = (tuple(range(nb)), tuple(range(nb)))
    ca = a.ndim - 1 if pat[0] == "n" else a.ndim - 2
    cb = b.ndim - 2 if pat[1] == "n" else b.ndim - 1
    return lax.dot_general(a.astype(BF16), b.astype(BF16), (((ca,), (cb,)), batch), preferred_element_type=F32)


@functools.partial(jax.custom_vjp, nondiff_argnums=(2,))
def bdot(a, b, pat):
    return _dg(a, b, pat)


def _bdot_fwd(a, b, pat):
    return _dg(a, b, pat), (a, b)


def _bdot_bwd(pat, res, g):
    a, b = res
    if pat == "nn":
        return _dg(g, b, "nt"), _dg(a, g, "tn")
    if pat == "nt":
        return _dg(g, b, "nn"), _dg(g, a, "tn")
    return _dg(b, g, "nt"), _dg(a, g, "nn")


bdot.defvjp(_bdot_fwd, _bdot_bwd)


def _shift_raw(x, s, up, fill):
    if s == 0:
        return x
    n = x.shape[0]
    r = pltpu.roll(x, (n - s) if up else s, 0)
    idx = lax.broadcasted_iota(jnp.int32, x.shape, 0)
    mask = (idx >= n - s) if up else (idx < s)
    return jnp.where(mask, jnp.asarray(fill, x.dtype), r)


@functools.partial(jax.custom_vjp, nondiff_argnums=(1,))
def shift_down(x, s):
    return _shift_raw(x, s, False, 0.0)


def _shift_down_fwd(x, s):
    return _shift_raw(x, s, False, 0.0), None


def _shift_down_bwd(s, _, g):
    return (_shift_raw(g, s, True, 0.0),)


shift_down.defvjp(_shift_down_fwd, _shift_down_bwd)


def _scan_impl(a, u, up):
    n = a.shape[0]
    s = 1
    while s < n:
        u = a * _shift_raw(u, s, up, 0.0) + u
        if 2 * s < n:
            a = a * _shift_raw(a, s, up, 1.0)
        s *= 2
    return u


@jax.custom_vjp
def lin_scan(a, u):
    return _scan_impl(a, u, False)


def _lin_scan_fwd(a, u):
    h = _scan_impl(a, u, False)
    return h, (a, h)


def _lin_scan_bwd(res, g):
    a, h = res
    gh = _scan_impl(_shift_raw(a, 1, True, 0.0), g, True)
    return gh * _shift_raw(h, 1, False, 0.0), gh


lin_scan.defvjp(_lin_scan_fwd, _lin_scan_bwd)


def _cumsum_impl(x, up, period):
    n = x.shape[0]
    span = n if period is None else period
    idx = lax.broadcasted_iota(jnp.int32, x.shape, 0)
    pos = idx if period is None else idx % period
    s = 1
    while s < span:
        sh = _shift_raw(x, s, up, 0.0)
        if period is not None:
            keep = (pos < period - s) if up else (pos >= s)
            sh = jnp.where(keep, sh, 0.0)
        x = x + sh
        s *= 2
    return x


@functools.partial(jax.custom_vjp, nondiff_argnums=(1,))
def cumsum_rows(x, period):
    return _cumsum_impl(x, False, period)


def _cumsum_fwd(x, period):
    return _cumsum_impl(x, False, period), None


def _cumsum_bwd(period, _, g):
    return (_cumsum_impl(g, True, period),)


cumsum_rows.defvjp(_cumsum_fwd, _cumsum_bwd)


def _sigmoid(x):
    return jax.nn.sigmoid(x)


def _expm1(x):
    return jnp.tanh(0.5 * x) * (jnp.exp(x) + 1.0)


def _softplus(x):
    return jnp.maximum(x, 0.0) + jnp.log(1.0 + jnp.exp(-jnp.abs(x)))


def _rms(x, g):
    return x * lax.rsqrt(jnp.mean(x * x, axis=-1, keepdims=True) + EPS) * g


def fn_prenorm(x, g):
    return (_rms(x, g).astype(BF16),)


def fn_prenorm_after(x, g, _token):
    return fn_prenorm(x, g)


def fn_addnorm2(x, y, g_post, g_pre):
    x1 = x + _rms(y, g_post)
    return x1, _rms(x1, g_pre).astype(BF16)


def fn_addnorm2_after(x, y, g_post, g_pre, _token):
    return fn_addnorm2(x, y, g_post, g_pre)


def fn_input_norm(x, g):
    return x, _rms(x, g).astype(BF16)


def _causal_conv(x, w, b, taps):
    c = b
    for k in range(taps):
        c = c + w[k:k + 1, :] * shift_down(x, taps - 1 - k)
    return c


def fn_rglru(xb, yb, cw, cb, wa, ba, wx, bx, lam):
    xf = _causal_conv(xb, cw, cb, B_CONV)
    r = _sigmoid(bdot(xf, wa, "nn") + ba)
    i = _sigmoid(bdot(xf, wx, "nn") + bx)
    log_a = -RG_C * r * _softplus(-lam)
    a = jnp.exp(log_a)
    u = jnp.sqrt(-_expm1(2.0 * log_a)) * (i * xf)
    h = lin_scan(a, u)
    return ((h * jax.nn.gelu(yb)).astype(BF16),)


def fn_fox_gate(zf, bias):
    return (cumsum_rows(jax.nn.log_sigmoid(zf + bias), None),)


def fn_hgrn_seg(q, fl, v, g, st, logits, hn):
    rows = q.shape[0]
    nc = rows // HGRN_CHUNK
    l0, l1, l2 = logits[0:1, :], logits[1:2, :], logits[2:3, :]
    mx = jnp.maximum(jnp.maximum(l0, l1), l2)
    e0, e1, e2 = jnp.exp(l0 - mx), jnp.exp(l1 - mx), jnp.exp(l2 - mx)
    lb = e0 / (e0 + e1 + e2)
    forget = lb + (1.0 - lb) * _sigmoid(fl)
    qs = q * _sigmoid(q)
    kk = 1.0 - forget
    logf = jnp.log(forget)
    bcum = cumsum_rows(logf, HGRN_CHUNK)
    c3 = lambda t: t.reshape(nc, HGRN_CHUNK, 128)
    b_last = jnp.sum(c3(logf), axis=1, keepdims=True)
    bcum3 = c3(bcum)
    q_dec = c3(qs) * jnp.exp(bcum3)
    k_dec = c3(kk) * jnp.exp(-bcum3)
    k_upd = c3(kk) * jnp.exp(b_last - bcum3)
    v3 = c3(v)
    scores = bdot(q_dec, k_dec, "nt")
    ri = lax.broadcasted_iota(jnp.int32, scores.shape, 1)
    ci = lax.broadcasted_iota(jnp.int32, scores.shape, 2)
    scores = jnp.where(ri >= ci, scores, 0.0)
    o = bdot(scores, v3, "nn")
    upd_t = bdot(v3, k_upd, "tn")
    dec = jnp.exp(b_last)
    prev = []
    for n in range(nc):
        prev.append(st)
        st = st * dec[n] + upd_t[n]
    o = o + bdot(q_dec, jnp.stack(prev), "nt")
    o = o.reshape(rows, 128)
    o = o * lax.rsqrt(jnp.mean(o * o, axis=-1, keepdims=True) + EPS) * hn
    return (o * _sigmoid(g)).astype(BF16), st


def _ffn_conv(xg, xv, cw, cb):
    cg = _causal_conv(xg, cw[0], cb[0], FFN_CONV)[HALO:]
    cv = _causal_conv(xv, cw[1], cb[1], FFN_CONV)[HALO:]
    return cg, cv


def _ffn_gate(cg, cv):
    return jax.nn.gelu(cg) * cv


class Row:
    def __init__(self, arr, cb=None, off=0):
        self.arr, self.cb, self.off = arr, cb, off

    def spec(self, tm):
        if self.cb is None:
            return pl.BlockSpec((tm, self.arr.shape[1]), lambda j, i: (i, 0))
        off = self.off
        return pl.BlockSpec((tm, self.cb), lambda j, i: (i, j + off))


class Par:
    def __init__(self, arr, kind="full", bs=None):
        self.arr, self.kind, self.bs = arr, kind, bs

    def block(self):
        if self.kind == "full":
            return self.arr.shape
        if self.kind == "col":
            return (self.arr.shape[0], self.bs)
        return (self.bs, self.arr.shape[1])

    def spec(self):
        if self.kind == "full":
            return pl.BlockSpec(self.block(), lambda j, i: (0, 0))
        if self.kind == "col":
            return pl.BlockSpec(self.block(), lambda j, i: (0, j))
        return pl.BlockSpec(self.block(), lambda j, i: (j, 0))


class Out:
    def __init__(self, width, dtype, cb=None, off=0):
        self.width, self.dtype, self.cb, self.off = width, dtype, cb, off

    def spec(self, tm):
        if self.cb is None:
            return pl.BlockSpec((tm, self.width), lambda j, i: (i, 0))
        off = self.off
        return pl.BlockSpec((tm, self.cb), lambda j, i: (i, j + off))


def _params(sem):
    return pltpu.CompilerParams(dimension_semantics=sem, vmem_limit_bytes=VMEM_LIMIT)


def tile_fwd(name, fn, *, m, tm, nj, rows, pars, outs, n_acc=0):
    n_r, n_p, n_o = len(rows), len(pars), len(outs)

    def body(*refs):
        ins = [r[...] for r in refs[:n_r + n_p]]
        res = fn(*ins)
        o_refs = refs[n_r + n_p:]
        for k in range(n_o):
            o_refs[k][...] = res[k].astype(o_refs[k].dtype)
        first = jnp.logical_and(pl.program_id(0) == 0, pl.program_id(1) == 0)
        for k in range(n_acc):
            ref = o_refs[n_o + k]

            @pl.when(first)
            def _():
                ref[...] = jnp.zeros_like(ref)

            ref[...] += res[n_o + k]

    out_shape = [jax.ShapeDtypeStruct((m, o.width), o.dtype) for o in outs]
    out_specs = [o.spec(tm) for o in outs]
    for _ in range(n_acc):
        out_shape.append(jax.ShapeDtypeStruct((1, LANES), F32))
        out_specs.append(pl.BlockSpec((1, LANES), lambda j, i: (0, 0)))
    sem = ("arbitrary", "arbitrary") if n_acc else ("parallel", "parallel")
    return pl.pallas_call(
        body, grid=(nj, m // tm), name=name,
        in_specs=[r.spec(tm) for r in rows] + [p.spec() for p in pars],
        out_specs=out_specs, out_shape=out_shape, compiler_params=_params(sem),
    )(*[r.arr for r in rows], *[p.arr for p in pars])


def tile_bwd(name, fn, *, m, tm, nj, rows, pars, cts, drows):
    n_r, n_p, n_c = len(rows), len(pars), len(cts)
    want = [k for k in range(n_r) if drows[k] is not None]

    def body(*refs):
        ins = [r[...] for r in refs[:n_r + n_p]]
        ct = [r[...] for r in refs[n_r + n_p:n_r + n_p + n_c]]
        o_refs = refs[n_r + n_p + n_c:]
        res, vjp = jax.vjp(fn, *ins)
        grads = vjp(tuple(c.astype(r.dtype) for c, r in zip(ct, res)))
        for pos, k in enumerate(want):
            o_refs[pos][...] = grads[k].astype(o_refs[pos].dtype)
        for k in range(n_p):
            ref = o_refs[len(want) + k]
            first = pl.program_id(1) == 0
            if pars[k].kind == "full":
                first = jnp.logical_and(first, pl.program_id(0) == 0)

            @pl.when(first)
            def _():
                ref[...] = jnp.zeros_like(ref)

            ref[...] += grads[n_r + k].astype(F32)

    out_shape = [jax.ShapeDtypeStruct((m, drows[k].width), drows[k].dtype) for k in want]
    out_specs = [drows[k].spec(tm) for k in want]
    for p in pars:
        out_shape.append(jax.ShapeDtypeStruct(p.arr.shape, F32))
        out_specs.append(p.spec())
    return pl.pallas_call(
        body, grid=(nj, m // tm), name=name,
        in_specs=[r.spec(tm) for r in rows] + [p.spec() for p in pars] + [c.spec(tm) for c in cts],
        out_specs=out_specs, out_shape=out_shape, compiler_params=_params(("arbitrary", "arbitrary")),
    )(*[r.arr for r in rows], *[p.arr for p in pars], *[c.arr for c in cts])


def loss_head(name, x, y, tgt, g, *, m, tm):
    def body(x_ref, y_ref, t_ref, g_ref, dout_ref, dy_ref, loss_ref, dg_ref):
        normed, vjp = jax.vjp(_rms, y_ref[...], g_ref[...])
        err = x_ref[...] + normed - t_ref[...]
        dout = err * (1.0 / D_MODEL)
        dy, dg = vjp(dout)
        dout_ref[...] = dout
        dy_ref[...] = dy.astype(dy_ref.dtype)

        @pl.when(pl.program_id(0) == 0)
        def _():
            loss_ref[...] = jnp.zeros_like(loss_ref)
            dg_ref[...] = jnp.zeros_like(dg_ref)

        loss_ref[...] += 0.5 * jnp.sum(jnp.mean(err * err, axis=-1, keepdims=True), axis=0, keepdims=True)
        dg_ref[...] += dg

    row = pl.BlockSpec((tm, D_MODEL), lambda i: (i, 0))
    whole = lambda w: pl.BlockSpec((1, w), lambda i: (0, 0))
    return pl.pallas_call(
        body, grid=(m // tm,), name=name, in_specs=[row, row, row, whole(D_MODEL)],
        out_specs=[row, row, whole(LANES), whole(D_MODEL)],
        out_shape=[jax.ShapeDtypeStruct((m, D_MODEL), F32), jax.ShapeDtypeStruct((m, D_MODEL), BF16),
                   jax.ShapeDtypeStruct((1, LANES), F32), jax.ShapeDtypeStruct((1, D_MODEL), F32)],
        compiler_params=_params(("arbitrary",)),
    )(x, y, tgt, g)


class Blk:
    def __init__(self, arr, block, index):
        self.arr, self.block, self.index = arr, block, index

    def spec(self):
        return pl.BlockSpec(self.block, self.index)


def _flat2(v):
    return v if v.ndim == 2 else v.reshape(-1, v.shape[-1])


def mm(name, pat, a, b, o, out_dtype, grid, after=None, b_join=False, o_split=False):
    nk = grid[2]
    o_shape = o.arr

    def put(o_ref, r):
        if o_split:
            half = r.shape[1] // 2
            o_ref[0] = r[:, :half].astype(out_dtype)
            o_ref[1] = r[:, half:].astype(out_dtype)
        else:
            o_ref[...] = r.astype(out_dtype).reshape(o_ref.shape)

    def body(*refs):
        a_ref, b_ref = refs[0], refs[1]
        o_ref = refs[3] if after is not None else refs[2]
        bv = jnp.concatenate([b_ref[0], b_ref[1]], axis=1) if b_join else _flat2(b_ref[...])
        r = _dg(_flat2(a_ref[...]), bv, pat)
        if nk == 1:
            put(o_ref, r)
            return
        acc_ref = refs[-1]
        kk = pl.program_id(2)

        @pl.when(kk == 0)
        def _():
            acc_ref[...] = r

        @pl.when(kk > 0)
        def _():
            acc_ref[...] += r

        @pl.when(kk == nk - 1)
        def _():
            put(o_ref, acc_ref[...])

    ob = [d for d in o.block if d is not None]
    if o_split:
        acc_shape = (ob[1], 2 * ob[2])
    else:
        acc_shape = (ob[0], ob[1]) if len(ob) == 2 else (ob[0] * ob[1], ob[2])
    in_specs = [a.spec(), b.spec()]
    args = [a.arr, b.arr]
    if after is not None:
        in_specs.append(pl.BlockSpec(memory_space=pl.ANY))
        args.append(after)
    return pl.pallas_call(
        body, grid=grid, name=name, in_specs=in_specs, out_specs=o.spec(),
        out_shape=jax.ShapeDtypeStruct(o_shape, out_dtype),
        scratch_shapes=[pltpu.VMEM(acc_shape, F32)] if nk > 1 else [],
        compiler_params=_params(("parallel", "parallel", "arbitrary")),
    )(*args)


def _div_tile(n, cap):
    if n <= cap:
        return n
    best = 128
    for t in range(128, cap + 1, 128):
        if n % t == 0:
            best = t
    return best


def mm2d(name, pat, a, b, out_dtype=F32):
    if pat == "tn":
        k, m = a.shape
    else:
        m, k = a.shape
    n = b.shape[0] if pat == "nt" else b.shape[1]
    tm, tn, tk = _div_tile(m, 1024), _div_tile(n, 1024), _div_tile(k, 1024)
    a_blk = Blk(a, (tk, tm), lambda i, j, kk: (kk, i)) if pat == "tn" else Blk(a, (tm, tk), lambda i, j, kk: (i, kk))
    b_blk = Blk(b, (tn, tk), lambda i, j, kk: (j, kk)) if pat == "nt" else Blk(b, (tk, tn), lambda i, j, kk: (kk, j))
    o_blk = Blk((m, n), (tm, tn), lambda i, j, kk: (i, j))
    return mm(name, pat, a_blk, b_blk, o_blk, out_dtype, (m // tm, n // tn, k // tk))


def hgrn_fwd(name, z, logits, hnorm, *, n_batch, seq):
    m = n_batch * seq
    ts = min(HGRN_SEG, seq)
    n_seg = seq // ts

    def body(q_ref, f_ref, v_ref, g_ref, lg_ref, hn_ref, o_ref, sp_ref, st_ref):
        s = pl.program_id(2)

        @pl.when(s == 0)
        def _():
            st_ref[...] = jnp.zeros_like(st_ref)

        st = st_ref[...]
        sp_ref[...] = st
        o, st_new = fn_hgrn_seg(q_ref[...], f_ref[...], v_ref[...], g_ref[...], st, lg_ref[...], hn_ref[...])
        o_ref[...] = o
        st_ref[...] = st_new

    part = lambda p: pl.BlockSpec((ts, 128), lambda h, b, s: (b * n_seg + s, 4 * p + h))
    return pl.pallas_call(
        body, grid=(A_HEADS, n_batch, n_seg), name=name,
        in_specs=[part(0), part(1), part(2), part(3),
                  pl.BlockSpec((3, 128), lambda h, b, s: (0, h)),
                  pl.BlockSpec((1, 128), lambda h, b, s: (0, h))],
        out_specs=[pl.BlockSpec((ts, 128), lambda h, b, s: (b * n_seg + s, h)),
                   pl.BlockSpec((128, 128), lambda h, b, s: ((b * n_seg + s) * A_HEADS + h, 0))],
        out_shape=[jax.ShapeDtypeStruct((m, A_WIDTH), BF16),
                   jax.ShapeDtypeStruct((n_batch * n_seg * A_HEADS * 128, 128), F32)],
        scratch_shapes=[pltpu.VMEM((128, 128), F32)],
        compiler_params=_params(("arbitrary", "arbitrary", "arbitrary")),
    )(z, z, z, z, logits, hnorm)


def hgrn_bwd(name, z, sprev, logits, hnorm, do, *, n_batch, seq):
    m = n_batch * seq
    ts = min(HGRN_SEG, seq)
    n_seg = seq // ts

    def body(q_ref, f_ref, v_ref, g_ref, sp_ref, lg_ref, hn_ref, do_ref, dq_ref, df_ref, dv_ref, dg_ref, dlg_ref, dhn_ref, dst_ref):
        s = pl.program_id(2)

        @pl.when(s == 0)
        def _():
            dst_ref[...] = jnp.zeros_like(dst_ref)

        res, vjp = jax.vjp(fn_hgrn_seg, q_ref[...], f_ref[...], v_ref[...], g_ref[...], sp_ref[...], lg_ref[...], hn_ref[...])
        dq, df, dv, dg, dst, dlg, dhn = vjp((do_ref[...].astype(res[0].dtype), dst_ref[...]))
        dq_ref[...] = dq.astype(dq_ref.dtype)
        df_ref[...] = df.astype(df_ref.dtype)
        dv_ref[...] = dv.astype(dv_ref.dtype)
        dg_ref[...] = dg.astype(dg_ref.dtype)
        dst_ref[...] = dst
        first = jnp.logical_and(pl.program_id(1) == 0, s == 0)

        @pl.when(first)
        def _():
            dlg_ref[...] = jnp.zeros_like(dlg_ref)
            dhn_ref[...] = jnp.zeros_like(dhn_ref)

        dlg_ref[...] += dlg
        dhn_ref[...] += dhn

    rev = lambda b, s: b * n_seg + (n_seg - 1 - s)
    part = lambda p: pl.BlockSpec((ts, 128), lambda h, b, s: (rev(b, s), 4 * p + h))
    head = pl.BlockSpec((ts, 128), lambda h, b, s: (rev(b, s), h))
    dpart = jax.ShapeDtypeStruct((m, A_WIDTH), BF16)
    return pl.pallas_call(
        body, grid=(A_HEADS, n_batch, n_seg), name=name,
        in_specs=[part(0), part(1), part(2), part(3),
                  pl.BlockSpec((128, 128), lambda h, b, s: (rev(b, s) * A_HEADS + h, 0)),
                  pl.BlockSpec((3, 128), lambda h, b, s: (0, h)),
                  pl.BlockSpec((1, 128), lambda h, b, s: (0, h)),
                  head],
        out_specs=[head, head, head, head,
                   pl.BlockSpec((3, 128), lambda h, b, s: (0, h)),
                   pl.BlockSpec((1, 128), lambda h, b, s: (0, h))],
        out_shape=[dpart, dpart, dpart, dpart,
                   jax.ShapeDtypeStruct(logits.shape, F32),
                   jax.ShapeDtypeStruct(hnorm.shape, F32)],
        scratch_shapes=[pltpu.VMEM((128, 128), F32)],
        compiler_params=_params(("arbitrary", "arbitrary", "arbitrary")),
    )(z, z, z, z, sprev, logits, hnorm, do)


FFN_ROWS = 128
FFN_LANES = 128


def _ffn_tiles(m, seq):
    tm = min(512, seq)
    return tm, seq // tm, m // tm


def ffn_mid_fwd(name, hid, cw, cb, layer, *, m, seq):
    tm, n_t, n_i = _ffn_tiles(m, seq)
    hb = tm // HALO

    def body(x_ref, xb_ref, cw_ref, cb_ref, o_ref, c_ref):
        first = pl.program_id(1) % n_t == 0
        before = jnp.where(first, 0.0, xb_ref[...])
        ext = jnp.concatenate([before, x_ref[...]], axis=1)
        cg, cv = _ffn_conv(ext[0], ext[1], cw_ref[...], cb_ref[...])
        o_ref[...] = _ffn_gate(cg, cv).astype(o_ref.dtype)
        c_ref[0] = cg.astype(c_ref.dtype)
        c_ref[1] = cv.astype(c_ref.dtype)

    return pl.pallas_call(
        body, grid=(N_DEV // 2, n_i), name=name,
        in_specs=[pl.BlockSpec((2, None, tm, FF_BLK), lambda d, i: (0, d, i, 0)),
                  pl.BlockSpec((2, None, HALO, FF_BLK), lambda d, i: (0, d, jnp.maximum(i * hb - 1, 0), 0)),
                  pl.BlockSpec((2, None, None, FFN_CONV, FF_BLK), lambda d, i: (0, d, layer, 0, 0)),
                  pl.BlockSpec((None, 2, None, 1, FF_BLK), lambda d, i: (layer, 0, d, 0, 0))],
        out_specs=[pl.BlockSpec((None, tm, FF_BLK), lambda d, i: (d, i, 0)),
                   pl.BlockSpec((2, None, tm, FF_BLK), lambda d, i: (0, d, i, 0))],
        out_shape=[jax.ShapeDtypeStruct((N_DEV // 2, m, FF_BLK), BF16),
                   jax.ShapeDtypeStruct((2, N_DEV // 2, m, FF_BLK), BF16)],
        compiler_params=_params(("parallel", "parallel")),
    )(hid, hid, cw, cb)


def ffn_mid_bwd(name, hid, conv, cw, dact, layer, *, m, seq):
    tm, n_t, n_i = _ffn_tiles(m, seq)
    hb = tm // HALO
    last_blk = m // HALO - 1

    rc = min(FFN_ROWS, tm)
    lane_chunks = [(l0, min(FFN_LANES, FF_BLK - l0)) for l0 in range(0, FF_BLK, FFN_LANES)]

    def body(x_ref, c_ref, ca_ref, cw_ref, da_ref, daa_ref, dx_ref, dcw_ref, dcb_ref, cext_ref, dext_ref):
        i = pl.program_id(1)
        last = i % n_t == n_t - 1
        cext_ref[:, :tm] = c_ref[...]
        cext_ref[:, tm:] = ca_ref[...]
        dext_ref[:tm] = da_ref[...]
        dext_ref[tm:] = jnp.where(last, jnp.zeros_like(daa_ref[...]), daa_ref[...])

        @pl.when(i == 0)
        def _():
            dcw_ref[...] = jnp.zeros_like(dcw_ref)
            dcb_ref[...] = jnp.zeros_like(dcb_ref)

        for l0, lw in lane_chunks:
            lanes = slice(l0, l0 + lw)

            def chunk(c, sums, lanes=lanes, lw=lw):
                r0 = pl.multiple_of(c * rc, rc)
                ext = pl.ds(r0, rc + HALO)
                cg, cv = cext_ref[0, ext, lanes].astype(F32), cext_ref[1, ext, lanes].astype(F32)
                _, vjp_gate = jax.vjp(_ffn_gate, cg, cv)
                dconv = vjp_gate(dext_ref[ext, lanes].astype(F32))
                out = []
                for half in range(2):
                    x = x_ref[half, pl.ds(r0, rc), lanes]
                    dx = None
                    for k in range(FFN_CONV):
                        s = FFN_CONV - 1 - k
                        dc_s = _shift_raw(dconv[half], s, True, 0.0)[:rc]
                        term = cw_ref[half, k:k + 1, lanes] * dc_s
                        dx = term if dx is None else dx + term
                        out.append(sums[len(out)] + jnp.sum(x * dc_s, axis=0, keepdims=True))
                    out.append(sums[len(out)] + jnp.sum(dconv[half][:rc], axis=0, keepdims=True))
                    dx_ref[half, pl.ds(r0, rc), lanes] = dx.astype(dx_ref.dtype)
                return tuple(out)

            zero = jnp.zeros((1, lw), F32)
            sums = lax.fori_loop(0, tm // rc, chunk, (zero,) * (2 * (FFN_CONV + 1)))
            for half in range(2):
                base = half * (FFN_CONV + 1)
                for k in range(FFN_CONV):
                    dcw_ref[half, k:k + 1, lanes] += sums[base + k]
                dcb_ref[half, :, lanes] += sums[base + FFN_CONV]

    return pl.pallas_call(
        body, grid=(N_DEV // 2, n_i), name=name,
        in_specs=[pl.BlockSpec((2, None, tm, FF_BLK), lambda d, i: (0, d, i, 0)),
                  pl.BlockSpec((2, None, tm, FF_BLK), lambda d, i: (0, d, i, 0)),
                  pl.BlockSpec((2, None, HALO, FF_BLK), lambda d, i: (0, d, jnp.minimum((i + 1) * hb, last_blk), 0)),
                  pl.BlockSpec((2, None, None, FFN_CONV, FF_BLK), lambda d, i: (0, d, layer, 0, 0)),
                  pl.BlockSpec((None, tm, FF_BLK), lambda d, i: (d, i, 0)),
                  pl.BlockSpec((None, HALO, FF_BLK), lambda d, i: (d, jnp.minimum((i + 1) * hb, last_blk), 0))],
        out_specs=[pl.BlockSpec((2, None, tm, FF_BLK), lambda d, i: (0, d, i, 0)),
                   pl.BlockSpec((2, None, FFN_CONV, FF_BLK), lambda d, i: (0, d, 0, 0)),
                   pl.BlockSpec((2, None, 1, FF_BLK), lambda d, i: (0, d, 0, 0))],
        out_shape=[jax.ShapeDtypeStruct((2, N_DEV // 2, m, FF_BLK), BF16),
                   jax.ShapeDtypeStruct((2, N_DEV // 2, FFN_CONV, FF_BLK), F32),
                   jax.ShapeDtypeStruct((2, N_DEV // 2, 1, FF_BLK), F32)],
        scratch_shapes=[pltpu.VMEM((2, tm + HALO, FF_BLK), BF16), pltpu.VMEM((tm + HALO, FF_BLK), BF16)],
        compiler_params=_params(("arbitrary", "arbitrary")),
    )(hid, conv, conv, cw, dact, dact)


ATT_BLK = 512
ATT_BLK_FWD = 1024
N_PAIR = C_HEADS // 2
TERM_W = C_HEADS * LANES


def term_placement():
    import numpy as np
    place = np.zeros((3, LANES, TERM_W), np.float32)
    ones_q = np.zeros((1, TERM_W), np.float32)
    ones_k = np.zeros((1, TERM_W), np.float32)
    for h in range(C_HEADS):
        for j in range(3):
            place[j, h, h * LANES + C_HEAD_DIM + j] = 1.0
            ones_q[0, h * LANES + C_HEAD_DIM + 3 + j] = 1.0
            ones_k[0, h * LANES + C_HEAD_DIM + j] = 1.0
    return (jnp.asarray(place.reshape(3 * LANES, TERM_W), BF16), jnp.asarray(ones_q, F32), jnp.asarray(ones_k, F32))


def fn_fox_terms(c, place, ones_q, ones_k):
    parts = _split3(c)
    placed = sum(_dg(parts[j], place[j * LANES:(j + 1) * LANES], "nn") for j in range(3))
    return (placed + ones_q).astype(BF16), (ones_k - pltpu.roll(placed, 3, 1)).astype(BF16)


def _head_tile(z, terms, e):
    lane = lax.broadcasted_iota(jnp.int32, z.shape, 1)
    base = z if e == 0 else pltpu.roll(z, C_HEAD_DIM, 1)
    return jnp.where(lane < C_HEAD_DIM, base, terms.astype(z.dtype))


def _head_only(z, e):
    lane = lax.broadcasted_iota(jnp.int32, z.shape, 1)
    mine = (lane < C_HEAD_DIM) if e == 0 else (lane >= C_HEAD_DIM)
    return jnp.where(mine, z, jnp.zeros_like(z)).astype(BF16)


def _pair_tile(a0, a1):
    lane = lax.broadcasted_iota(jnp.int32, a0.shape, 1)
    return jnp.where(lane < C_HEAD_DIM, a0, pltpu.roll(a1, C_HEAD_DIM, 1))


def _lane_col(a, k):
    lane = lax.broadcasted_iota(jnp.int32, a.shape, 1)
    return jnp.sum(jnp.where(lane == k, a, 0.0), axis=1, keepdims=True)


def _causal(s):
    key = lax.broadcasted_iota(jnp.int32, s.shape, 0)
    qry = lax.broadcasted_iota(jnp.int32, s.shape, 1)
    return qry >= key


def fox_pair_fwd(name, z, qterm, kterm, *, n_batch, seq):
    m = n_batch * seq
    blk = min(ATT_BLK_FWD, seq)
    nq = seq // blk
    dh = C_HEAD_DIM

    def body(zq_ref, zk_ref, zv_ref, qt_ref, kt_ref, o_ref, lse_ref, ka_ref, vt_ref):
        qi = pl.program_id(2)

        @pl.when(qi == 0)
        def _():
            zk = zk_ref[...]
            for e in range(2):
                ka_ref[e] = _head_tile(zk, kt_ref[:, e * LANES:(e + 1) * LANES], e).astype(BF16)
            for cb in range(nq):
                vt_ref[cb] = zv_ref[cb * blk:(cb + 1) * blk, :].T.astype(BF16)

        zq = zq_ref[...] * dh ** -0.5
        qa = [_head_tile(zq, qt_ref[:, e * LANES:(e + 1) * LANES], e).astype(BF16) for e in range(2)]

        def block(j, carry, diagonal):
            rows = pl.ds(pl.multiple_of(j * blk, blk), blk)
            out = []
            for e in range(2):
                mx, l, acc = carry[e]
                s = _dg(ka_ref[e, rows, :], qa[e], "nt")
                if diagonal:
                    s = jnp.where(_causal(s), s, NEG)
                mx_new = jnp.maximum(mx, jnp.max(s, axis=0, keepdims=True))
                p = jnp.exp(s - mx_new)
                alpha = jnp.exp(mx - mx_new)
                l = alpha * l + jnp.sum(p, axis=0, keepdims=True)
                acc = alpha * acc + _dg(vt_ref[j, e * dh:(e + 1) * dh, :], p, "nn")
                out.append((mx_new, l, acc))
            return tuple(out)

        one = (jnp.full((1, blk), NEG, F32), jnp.zeros((1, blk), F32), jnp.zeros((dh, blk), F32))
        carry = lax.fori_loop(0, qi, lambda j, cr: block(j, cr, False), (one, one))
        res = block(qi, carry, True)
        ot = jnp.concatenate([res[e][2] / res[e][1] for e in range(2)], axis=0)
        o_ref[...] = ot.T.astype(o_ref.dtype)
        for e in range(2):
            lse_ref[e] = res[e][0] + jnp.log(res[e][1])

    col = lambda part: (lambda b, g, i: (b, part * N_PAIR + g))
    return pl.pallas_call(
        body, grid=(n_batch, N_PAIR, nq), name=name,
        in_specs=[pl.BlockSpec((blk, LANES), lambda b, g, i: (b * nq + i, g)),
                  pl.BlockSpec((seq, LANES), col(1)),
                  pl.BlockSpec((seq, LANES), col(2)),
                  pl.BlockSpec((blk, 2 * LANES), lambda b, g, i: (b * nq + i, g)),
                  pl.BlockSpec((seq, 2 * LANES), lambda b, g, i: (b, g))],
        out_specs=[pl.BlockSpec((blk, LANES), lambda b, g, i: (b * nq + i, g)),
                   pl.BlockSpec((None, None, None, 2, 1, blk), lambda b, g, i: (b, g, i, 0, 0, 0))],
        out_shape=[jax.ShapeDtypeStruct((m, D_MODEL), BF16), jax.ShapeDtypeStruct((n_batch, N_PAIR, nq, 2, 1, blk), F32)],
        scratch_shapes=[pltpu.VMEM((2, seq, LANES), BF16), pltpu.VMEM((nq, LANES, blk), BF16)],
        compiler_params=_params(("parallel", "parallel", "arbitrary")),
    )(z, z, z, qterm, kterm)


def fox_pair_bwd(name, z, qterm, kterm, o, do, lse, *, n_batch, seq):
    m = n_batch * seq
    blk = min(ATT_BLK, seq)
    nq = seq // blk
    dh = C_HEAD_DIM

    def body(zq_ref, zk_ref, zv_ref, qt_ref, kt_ref, o_ref, do_ref, lse_ref, dq_ref, dk_ref, dv_ref, dc_ref,
             qa_ref, doh_ref, del_ref, dqt_ref, dk_acc, dv_acc):
        g, j = pl.program_id(1), pl.program_id(2)
        lane = lax.broadcasted_iota(jnp.int32, (blk, LANES), 1)

        @pl.when(jnp.logical_and(g == 0, j == 0))
        def _():
            dc_ref[...] = jnp.zeros_like(dc_ref)

        @pl.when(j == 0)
        def _():
            zq = zq_ref[...] * dh ** -0.5
            dov = do_ref[...]
            for e in range(2):
                qa_ref[e] = _head_tile(zq, qt_ref[:, e * LANES:(e + 1) * LANES], e).astype(BF16)
                doh_ref[e] = _head_only(dov, e)
            for cb in range(nq):
                rows = slice(cb * blk, (cb + 1) * blk)
                prod_t = (do_ref[rows, :].astype(F32) * o_ref[rows, :].astype(F32)).T
                for e in range(2):
                    del_ref[cb, e] = jnp.sum(prod_t[e * dh:(e + 1) * dh], axis=0, keepdims=True)
            dqt_ref[...] = jnp.zeros_like(dqt_ref)

        zk, zv = zk_ref[...], zv_ref[...]
        ka32 = [_head_tile(zk, kt_ref[:, e * LANES:(e + 1) * LANES], e) for e in range(2)]
        ka = [t.astype(BF16) for t in ka32]
        kat = [t.T.astype(BF16) for t in ka32]
        vh = [_head_only(zv, e) for e in range(2)]
        dk_acc[...] = jnp.zeros_like(dk_acc)
        dv_acc[...] = jnp.zeros_like(dv_acc)

        def block(i, diagonal):
            rows = pl.ds(pl.multiple_of(i * blk, blk), blk)
            for e in range(2):
                qv, dov = qa_ref[e, rows, :], doh_ref[e, rows, :]
                p = jnp.exp(_dg(ka[e], qv, "nt") - lse_ref[i, e])
                if diagonal:
                    p = jnp.where(_causal(p), p, 0.0)
                dv_acc[...] += _dg(p, dov, "nn")
                ds = p * (_dg(vh[e], dov, "nt") - del_ref[i, e])
                dk_acc[e] += _dg(ds, qv, "nn")
                dqt_ref[i, e] += _dg(kat[e], ds, "nn")

        block(j, True)

        def rest(i, carry):
            block(i, False)
            return carry

        lax.fori_loop(j + 1, nq, rest, 0)
        dk0, dk1 = dk_acc[0], dk_acc[1]
        dk_ref[...] = _pair_tile(dk0, dk1).astype(dk_ref.dtype)
        dv_ref[...] = dv_acc[...].astype(dv_ref.dtype)
        rows_j = pl.ds(pl.multiple_of(j * blk, blk), blk)
        for e, dke in enumerate((dk0, dk1)):
            dc_ref[rows_j, :] -= jnp.where(lane == 2 * g + e, _lane_col(dke, dh + 3), 0.0)

        @pl.when(j == nq - 1)
        def _():
            for i in range(nq):
                nat = [dqt_ref[i, e].T for e in range(2)]
                rows = slice(i * blk, (i + 1) * blk)
                dq_ref[rows, :] = (_pair_tile(nat[0], nat[1]) * dh ** -0.5).astype(dq_ref.dtype)
                for e in range(2):
                    dc_ref[rows, :] += jnp.where(lane == 2 * g + e, _lane_col(nat[e], dh), 0.0)

    col = lambda part: (lambda b, g, j: (b, part * N_PAIR + g))
    colj = lambda part: (lambda b, g, j: (b * nq + j, part * N_PAIR + g))
    pair = jax.ShapeDtypeStruct((m, D_MODEL), BF16)
    return pl.pallas_call(
        body, grid=(n_batch, N_PAIR, nq), name=name,
        in_specs=[pl.BlockSpec((seq, LANES), col(0)),
                  pl.BlockSpec((blk, LANES), colj(1)),
                  pl.BlockSpec((blk, LANES), colj(2)),
                  pl.BlockSpec((seq, 2 * LANES), lambda b, g, j: (b, g)),
                  pl.BlockSpec((blk, 2 * LANES), lambda b, g, j: (b * nq + j, g)),
                  pl.BlockSpec((seq, LANES), col(0)),
                  pl.BlockSpec((seq, LANES), col(0)),
                  pl.BlockSpec((None, None, nq, 2, 1, blk), lambda b, g, j: (b, g, 0, 0, 0, 0))],
        out_specs=[pl.BlockSpec((seq, LANES), col(0)),
                   pl.BlockSpec((blk, LANES), colj(0)),
                   pl.BlockSpec((blk, LANES), colj(0)),
                   pl.BlockSpec((seq, LANES), lambda b, g, j: (b, 0))],
        out_shape=[pair, pair, pair, jax.ShapeDtypeStruct((m, LANES), F32)],
        scratch_shapes=[pltpu.VMEM((2, seq, LANES), BF16), pltpu.VMEM((2, seq, LANES), BF16),
                        pltpu.VMEM((nq, 2, 1, blk), F32), pltpu.VMEM((nq, 2, LANES, blk), F32),
                        pltpu.VMEM((2, blk, LANES), F32), pltpu.VMEM((blk, LANES), F32)],
        compiler_params=_params(("arbitrary", "arbitrary", "arbitrary")),
    )(z, z, z, qterm, kterm, o, do, lse)


def _split3(c):
    c1 = c.astype(BF16)
    r1 = c - c1.astype(F32)
    c2 = r1.astype(BF16)
    c3 = (r1 - c2.astype(F32)).astype(BF16)
    return c1, c2, c3


def _mesh_pos():
    return lax.axis_index("x"), lax.axis_index("y"), lax.axis_index("c")


def _flip(v, bit):
    return 1 - v if bit else v


def all_gather(name, blocks):
    n = len(blocks)

    def body(*refs):
        x_refs, out_refs = refs[:n], refs[n:2 * n]
        send_sems, recv_sems, local_sems = refs[2 * n:]
        x, y, c = _mesh_pos()
        me, sibling = (x, y, c), (x, y, 1 - c)
        chips = [(1 - x, y), (x, 1 - y), (1 - x, 1 - y)]

        def slot(a, px, py, pc):
            return out_refs[a].at[4 * px + 2 * py + pc]

        def copy(a, k, blk, to, src=None):
            return pltpu.make_async_remote_copy(
                src_ref=slot(a, *blk) if src is None else src, dst_ref=slot(a, *blk),
                send_sem=send_sems.at[a, k], recv_sem=recv_sems.at[a, k], device_id=to, device_id_type=MESH)

        mine = [pltpu.make_async_copy(x_refs[a], slot(a, *me), local_sems.at[a]) for a in range(n)]
        for cp in mine:
            cp.start()
        sends = []
        for a in range(n):
            sends.append(copy(a, 0, me, sibling, src=x_refs[a]))
            sends += [copy(a, 1 + j, me, (*chip, c), src=x_refs[a]) for j, chip in enumerate(chips)]
        for cp in sends:
            cp.start()
        for j, chip in enumerate(chips):
            for a in range(n):
                copy(a, 1 + j, (*chip, c), me).wait_recv()
                passed = copy(a, 4 + j, (*chip, c), sibling)
                passed.start()
                sends.append(passed)
        for a in range(n):
            copy(a, 0, sibling, me).wait_recv()
            for j, chip in enumerate(chips):
                copy(a, 4 + j, (*chip, 1 - c), me).wait_recv()
        for cp in sends:
            cp.wait_send()
        for cp in mine:
            cp.wait()

    hbm = pl.BlockSpec(memory_space=pl.ANY)
    return pl.pallas_call(
        body, name=name, out_shape=[jax.ShapeDtypeStruct((N_DEV,) + b.shape, b.dtype) for b in blocks],
        in_specs=[hbm] * n, out_specs=[hbm] * n,
        scratch_shapes=[pltpu.SemaphoreType.DMA((n, 7)), pltpu.SemaphoreType.DMA((n, 7)), pltpu.SemaphoreType.DMA((n,))],
    )(*blocks)


def _peers(x, y, c):
    return [(_flip(x, k & 4), _flip(y, k & 2), _flip(c, k & 1)) for k in range(1, N_DEV)]


def gather_start(name, blocks, lands):
    n = len(blocks)

    def body(*refs):
        x_refs, land_refs = refs[:n], refs[n:2 * n]
        send_sems, recv_sems = refs[2 * n], refs[2 * n + 1]
        token = refs[-1]
        x, y, c = _mesh_pos()
        me = 4 * x + 2 * y + c
        for k, peer in enumerate(_peers(x, y, c)):
            for a in range(n):
                pltpu.make_async_remote_copy(
                    src_ref=x_refs[a], dst_ref=land_refs[a].at[me], send_sem=send_sems.at[7 * a + k], recv_sem=recv_sems.at[7 * a + k],
                    device_id=peer, device_id_type=MESH).start()
        token[...] = jnp.zeros_like(token)

    hbm = pl.BlockSpec(memory_space=pltpu.HBM)
    sem = pl.BlockSpec(memory_space=pltpu.SEMAPHORE)
    out_shape = ([pltpu.SemaphoreType.DMA((7 * n,)), pltpu.SemaphoreType.DMA((7 * n,))]
                 + [pltpu.HBM(b.shape, b.dtype) for b in blocks] + [pltpu.HBM(l.shape, l.dtype) for l in lands]
                 + [jax.ShapeDtypeStruct((8, LANES), F32)])
    res = pl.pallas_call(
        body, name=name, out_shape=out_shape, in_specs=[hbm] * (2 * n),
        out_specs=[sem, sem] + [hbm] * (2 * n) + [pl.BlockSpec(memory_space=pltpu.VMEM)],
        input_output_aliases={a: 2 + a for a in range(2 * n)},
        compiler_params=pltpu.CompilerParams(has_side_effects=pltpu.SideEffectType.DATAFLOW_SIDE_EFFECTING),
    )(*[pltpu.with_memory_space_constraint(b, pltpu.HBM) for b in blocks],
      *[pltpu.with_memory_space_constraint(l, pltpu.HBM) for l in lands])
    return res[0], res[1], res[2:2 + n], res[2 + n:2 + 2 * n], res[-1]


def gather_wait(name, send_sems, recv_sems, blocks, lands, after):
    n = len(blocks)

    def body(*refs):
        x_refs, land_refs = refs[:n], refs[n:2 * n]
        s_sems, r_sems = refs[2 * n], refs[2 * n + 1]
        x, y, c = _mesh_pos()
        me = 4 * x + 2 * y + c
        for k, peer in enumerate(_peers(x, y, c)):
            for a in range(n):
                cp = pltpu.make_async_remote_copy(
                    src_ref=x_refs[a], dst_ref=land_refs[a].at[me], send_sem=s_sems.at[7 * a + k], recv_sem=r_sems.at[7 * a + k],
                    device_id=peer, device_id_type=MESH)
                cp.wait_send()
                cp.wait_recv()

    hbm = pl.BlockSpec(memory_space=pltpu.HBM)
    sem = pl.BlockSpec(memory_space=pltpu.SEMAPHORE)
    res = pl.pallas_call(
        body, name=name,
        out_shape=[pltpu.HBM(b.shape, b.dtype) for b in blocks] + [pltpu.HBM(l.shape, l.dtype) for l in lands],
        in_specs=[hbm] * (2 * n) + [sem, sem, pl.BlockSpec(memory_space=pl.ANY)], out_specs=[hbm] * (2 * n),
        input_output_aliases={a: a for a in range(2 * n)},
        compiler_params=pltpu.CompilerParams(has_side_effects=pltpu.SideEffectType.DATAFLOW_SIDE_EFFECTING),
    )(*blocks, *lands, send_sems, recv_sems, after)
    return res[n:]


def _split_exchange(name, sends, lands, sems, after):
    n = len(sends)
    starting = sems is None

    def body(*refs):
        s_refs, l_refs = refs[:n], refs[n:2 * n]
        send_sems, recv_sems = refs[2 * n], refs[2 * n + 1]
        x, y, c = _mesh_pos()
        me = 4 * x + 2 * y + c
        for k, (px, py, pc) in enumerate(_peers(x, y, c)):
            for a in range(n):
                cp = pltpu.make_async_remote_copy(
                    src_ref=s_refs[a].at[4 * px + 2 * py + pc], dst_ref=l_refs[a].at[me],
                    send_sem=send_sems.at[7 * a + k], recv_sem=recv_sems.at[7 * a + k],
                    device_id=(px, py, pc), device_id_type=MESH)
                if starting:
                    cp.start()
                else:
                    cp.wait_send()
                    cp.wait_recv()
        if starting:
            refs[-1][...] = jnp.zeros_like(refs[-1])

    hbm = pl.BlockSpec(memory_space=pltpu.HBM)
    sem = pl.BlockSpec(memory_space=pltpu.SEMAPHORE)
    thru = [pltpu.HBM(t.shape, t.dtype) for t in list(sends) + list(lands)]
    effect = pltpu.CompilerParams(has_side_effects=pltpu.SideEffectType.DATAFLOW_SIDE_EFFECTING)
    if starting:
        res = pl.pallas_call(
            body, name=name, in_specs=[hbm] * (2 * n),
            out_shape=[pltpu.SemaphoreType.DMA((7 * n,)), pltpu.SemaphoreType.DMA((7 * n,))] + thru + [jax.ShapeDtypeStruct((8, LANES), F32)],
            out_specs=[sem, sem] + [hbm] * (2 * n) + [pl.BlockSpec(memory_space=pltpu.VMEM)],
            input_output_aliases={a: 2 + a for a in range(2 * n)}, compiler_params=effect,
        )(*[pltpu.with_memory_space_constraint(t, pltpu.HBM) for t in list(sends) + list(lands)])
        return res[0], res[1], res[2:2 + n], res[2 + n:2 + 2 * n], res[-1]
    res = pl.pallas_call(
        body, name=name, out_shape=thru, in_specs=[hbm] * (2 * n) + [sem, sem, pl.BlockSpec(memory_space=pl.ANY)],
        out_specs=[hbm] * (2 * n), input_output_aliases={a: a for a in range(2 * n)}, compiler_params=effect,
    )(*sends, *lands, sems[0], sems[1], after)
    return res[n:]


def unwritten(name, like):
    def body(*refs):
        pass

    hbm = pl.BlockSpec(memory_space=pl.ANY)
    return pl.pallas_call(body, name=name, out_shape=[jax.ShapeDtypeStruct(t.shape, t.dtype) for t in like],
                          out_specs=[hbm] * len(like))()


def own_slot_only(send, land, me):
    mine = lax.dynamic_index_in_dim(send, me, 0, keepdims=False)
    return lax.dynamic_update_index_in_dim(land, mine, me, 0)


def all_to_all(name, sends):
    n = len(sends)

    def body(*refs):
        s_refs, r_refs = refs[:n], refs[n:2 * n]
        send_sems, recv_sems, local_sems = refs[2 * n:]
        x, y, c = _mesh_pos()
        me = 4 * x + 2 * y + c
        mine = [pltpu.make_async_copy(s_refs[a].at[me], r_refs[a].at[me], local_sems.at[a]) for a in range(n)]
        for cp in mine:
            cp.start()
        copies = []
        for k in range(1, N_DEV):
            px, py, pc = _flip(x, k & 4), _flip(y, k & 2), _flip(c, k & 1)
            for a in range(n):
                copies.append(pltpu.make_async_remote_copy(
                    src_ref=s_refs[a].at[4 * px + 2 * py + pc], dst_ref=r_refs[a].at[me],
                    send_sem=send_sems.at[a, k - 1], recv_sem=recv_sems.at[a, k - 1],
                    device_id=(px, py, pc), device_id_type=MESH))
        for cp in copies:
            cp.start()
        for cp in copies:
            cp.wait_recv()
        for cp in copies:
            cp.wait_send()
        for cp in mine:
            cp.wait()

    hbm = pl.BlockSpec(memory_space=pl.ANY)
    return pl.pallas_call(
        body, name=name, out_shape=[jax.ShapeDtypeStruct(s.shape, s.dtype) for s in sends],
        in_specs=[hbm] * n, out_specs=[hbm] * n,
        scratch_shapes=[pltpu.SemaphoreType.DMA((n, 7)), pltpu.SemaphoreType.DMA((n, 7)), pltpu.SemaphoreType.DMA((n,))],
    )(*sends)


def _row_tile(r, cap, step):
    return next((t for t in range(cap, step - 1, -step) if r % t == 0), r)


def _sum_parts(p, n):
    t = [p[k].astype(F32) for k in range(n)]
    while len(t) > 1:
        t = [t[k] + t[k + 1] for k in range(0, len(t), 2)]
    return t[0]


def _adam(g, w, m, v):
    m = ADAM_B1 * m + (1.0 - ADAM_B1) * g
    v = ADAM_B2 * v + (1.0 - ADAM_B2) * (g * g)
    m_hat = m / (1.0 - ADAM_B1 ** ADAM_STEP)
    v_hat = v / (1.0 - ADAM_B2 ** ADAM_STEP)
    return -ADAM_LR * (m_hat / (jnp.sqrt(v_hat) + ADAM_EPS) + ADAM_WD * w), m, v


def adam_tiled(name, partials, w, m_, v_, layer=0, prev=None):
    _, r, c = w.shape
    n_part = partials.shape[0]
    tr = _row_tile(r, 256, 16)

    def body(*refs):
        p_ref, w_ref, m_ref, v_ref = refs[:4]
        g_ref, d_ref, nm_ref, nv_ref = refs[-4:]
        g = _sum_parts(p_ref, n_part)
        g_ref[...] = g
        d_ref[...], nm_ref[...], nv_ref[...] = _adam(g, w_ref[...], m_ref[...], v_ref[...])

    spec = pl.BlockSpec((None, tr, c), lambda i: (layer, i, 0))
    in_specs = [pl.BlockSpec((n_part, None, tr, c), lambda i: (0, 0, i, 0)), spec, spec, spec]
    args = [partials, w, m_, v_]
    aliases = {}
    if prev is not None:
        in_specs += [pl.BlockSpec(memory_space=pl.ANY)] * 4
        args += list(prev)
        aliases = {4 + k: k for k in range(4)}
    return pl.pallas_call(
        body, grid=(r // tr,), name=name, in_specs=in_specs,
        out_specs=[spec] * 4, out_shape=[jax.ShapeDtypeStruct(w.shape, F32)] * 4,
        input_output_aliases=aliases, compiler_params=_params(("parallel",)),
    )(*args)


def adam_small(name, items, extra):
    n, ne = len(items), len(extra)

    def body(*refs):
        ins, outs = refs[:4 * n + ne], refs[4 * n + ne:]
        for a in range(n):
            p_ref, w_ref, m_ref, v_ref = ins[4 * a:4 * a + 4]
            g = _sum_parts(p_ref, N_DEV)
            outs[4 * a][...] = g
            outs[4 * a + 1][...], outs[4 * a + 2][...], outs[4 * a + 3][...] = _adam(g, w_ref[...], m_ref[...], v_ref[...])
        for e in range(ne):
            outs[4 * n + e][...] = _sum_parts(ins[4 * n + e], N_DEV)

    args, out_shape = [], []
    for p, w, m_, v_ in items:
        args += [p, w, m_, v_]
        out_shape += [jax.ShapeDtypeStruct(w.shape, F32)] * 4
    for e in extra:
        args.append(e)
        out_shape.append(jax.ShapeDtypeStruct(e.shape[1:], F32))
    vmem = pl.BlockSpec(memory_space=pltpu.VMEM)
    res = pl.pallas_call(body, name=name, in_specs=[vmem] * len(args), out_specs=[vmem] * len(out_shape), out_shape=out_shape)(*args)
    return [res[4 * a:4 * a + 4] for a in range(n)], res[4 * n:]


def _cols_from_gather(g):
    g = jnp.moveaxis(g, 0, -2)
    return g.reshape(g.shape[:-2] + (g.shape[-2] * g.shape[-1],))


def _cols_to_blocks(w):
    w = w.reshape(w.shape[:-1] + (N_DEV, w.shape[-1] // N_DEV))
    return jnp.moveaxis(w, -2, 0)


def _block_diag(w):
    pairs = w.reshape(B_BLOCKS // 2, 2, B_BLOCK_DIM, 1, B_BLOCK_DIM)
    same = jnp.eye(2, dtype=bool).reshape(1, 2, 1, 2, 1)
    return jnp.where(same, pairs, 0.0).reshape(B_BLOCKS // 2 * LANES, LANES)


def _block_diag_grad(d):
    parts = d.reshape(B_BLOCKS // 2, 2, B_BLOCK_DIM, 2, B_BLOCK_DIM)
    same = jnp.eye(2, dtype=bool).reshape(1, 2, 1, 2, 1)
    return jnp.sum(jnp.where(same, parts, 0.0), axis=3).reshape(B_BLOCKS, B_BLOCK_DIM, B_BLOCK_DIM)


NAMES = ("norm_gains", "even_w_in", "hgrn_lb_logits", "hgrn_norm", "rg_conv_w", "rg_conv_b", "rg_wa", "rg_ba", "rg_wx", "rg_bx",
         "rg_lambda", "even_w_out", "odd_w_in", "fox_f_bias", "odd_w_out", "ffn_w_up", "ffn_conv_w", "ffn_conv_b", "ffn_w_down")
SMALL_SHARDED = ("norm_gains", "rg_conv_w", "ffn_conv_w")
REPLICATED = ("hgrn_lb_logits", "hgrn_norm", "rg_conv_b", "rg_wa", "rg_ba", "rg_wx", "rg_bx", "rg_lambda", "fox_f_bias", "ffn_conv_b")


def _ffn_forward(tag, layer, h, w_up_g, cw5, cb5, w_down_g, m, seq):
    tm = _div_tile(m, 1024)
    nm = m // tm
    hid = mm(f"{tag}_up", "nn",
             Blk(h, (tm, D_MODEL), lambda i, j, k: (i, 0)),
             Blk(w_up_g, (None, None, D_MODEL, FF_BLK), lambda i, j, k: (j, 0, 0, 0)),
             Blk((N_DEV, m, FF_BLK), (None, tm, FF_BLK), lambda i, j, k: (j, i, 0)), F32, (nm, N_DEV, 1))
    hid = hid.reshape(2, N_DEV // 2, m, FF_BLK)
    act, conv = ffn_mid_fwd(f"{tag}_mid", hid, cw5, cb5, layer, m=m, seq=seq)
    f = mm(f"{tag}_down", "nn",
           Blk(act, (None, tm, FF_BLK), lambda i, j, k: (k, i, 0)),
           Blk(w_down_g, (2, None, FF_BLK // 2, D_MODEL), lambda i, j, k: (k, 0, 0, 0)),
           Blk((m, D_MODEL), (tm, D_MODEL), lambda i, j, k: (i, 0)), F32, (nm, 1, N_DEV // 2))
    return (hid, conv), act, f


def _ffn_backward(tag, layer, df, h, hid, act, w_up_g, cw5, cb5, w_down_g, m, seq):
    tm = _div_tile(m, 1024)
    nm = m // tm
    dact = mm(f"{tag}_dact", "nt",
              Blk(df, (tm, D_MODEL), lambda i, j, k: (i, 0)),
              Blk(w_down_g, (2, None, FF_BLK // 2, D_MODEL), lambda i, j, k: (j, 0, 0, 0)),
              Blk((N_DEV // 2, m, FF_BLK), (None, tm, FF_BLK), lambda i, j, k: (j, i, 0)), BF16, (nm, N_DEV // 2, 1))
    d_wdown = mm(f"{tag}_dwdown", "tn",
                 Blk(act, (None, tm, FF_BLK), lambda i, j, k: (i, k, 0)),
                 Blk(df, (tm, D_MODEL), lambda i, j, k: (k, 0)),
                 Blk(w_down_g.shape, (2, None, FF_BLK // 2, D_MODEL), lambda i, j, k: (i, 0, 0, 0)), BF16,
                 (N_DEV // 2, 1, nm))
    dhid, d_cw, d_cb = ffn_mid_bwd(f"{tag}_dmid", hid[0], hid[1], cw5, dact, layer, m=m, seq=seq)
    dhid = dhid.reshape(N_DEV, m, FF_BLK)
    dh = mm(f"{tag}_dh", "nt",
            Blk(dhid, (None, tm, FF_BLK), lambda i, j, k: (k, i, 0)),
            Blk(w_up_g, (None, None, D_MODEL, FF_BLK), lambda i, j, k: (k, 0, 0, 0)),
            Blk((m, D_MODEL), (tm, D_MODEL), lambda i, j, k: (i, 0)), BF16, (nm, 1, N_DEV))
    d_wup = mm(f"{tag}_dwup", "tn",
               Blk(dhid, (None, tm, FF_BLK), lambda i, j, k: (i, k, 0)),
               Blk(h, (tm, D_MODEL), lambda i, j, k: (k, 0)),
               Blk((N_DEV, 1, FF_BLK, D_MODEL), (None, None, FF_BLK, D_MODEL), lambda i, j, k: (i, 0, 0, 0)), BF16,
               (N_DEV, 1, nm))
    return dh, d_wup, d_cw, d_cb, d_wdown


def kernel(x, norm_gains, even_w_in, hgrn_lb_logits, hgrn_norm, rg_conv_w, rg_conv_b, rg_wa, rg_ba, rg_wx, rg_bx, rg_lambda, even_w_out, odd_w_in, fox_f_bias, odd_w_out, ffn_w_up, ffn_conv_w, ffn_conv_b, ffn_w_down, loss_target, m_norm_gains, m_even_w_in, m_hgrn_lb_logits, m_hgrn_norm, m_rg_conv_w, m_rg_conv_b, m_rg_wa, m_rg_ba, m_rg_wx, m_rg_bx, m_rg_lambda, m_even_w_out, m_odd_w_in, m_fox_f_bias, m_odd_w_out, m_ffn_w_up, m_ffn_conv_w, m_ffn_conv_b, m_ffn_w_down, v_norm_gains, v_even_w_in, v_hgrn_lb_logits, v_hgrn_norm, v_rg_conv_w, v_rg_conv_b, v_rg_wa, v_rg_ba, v_rg_wx, v_rg_bx, v_rg_lambda, v_even_w_out, v_odd_w_in, v_fox_f_bias, v_odd_w_out, v_ffn_w_up, v_ffn_conv_w, v_ffn_conv_b, v_ffn_w_down):
    local = dict(locals())
    w = {n: local[n] for n in NAMES}
    mom = {n: local["m_" + n] for n in NAMES}
    var = {n: local["v_" + n] for n in NAMES}
    n_batch, seq, _ = x.shape
    m = n_batch * seq
    tm = _div_tile(m, 512)
    tmm = _div_tile(m, 1024)
    nm = m // tmm

    gathered = all_gather("gather_weights", [w["even_w_in"].astype(BF16)] + [w[n] for n in SMALL_SHARDED])
    g = dict(zip(("even_w_in",) + SMALL_SHARDED, gathered))
    w_in_e = g["even_w_in"]
    gains = _cols_from_gather(g["norm_gains"])
    me = 4 * lax.axis_index("x") + 2 * lax.axis_index("y") + lax.axis_index("c")
    def own_block_only(name, blocks):
        lands = unwritten(name, [jax.ShapeDtypeStruct((N_DEV,) + t.shape, t.dtype) for t in blocks])
        return [lax.dynamic_update_index_in_dim(ld, t, me, 0) for ld, t in zip(lands, blocks)]

    def own_slots_only(name, sends):
        return [own_slot_only(t, ld, me) for t, ld in zip(sends, unwritten(name, sends))]

    behind = (g["norm_gains"][0, 0, 0, 0] * 0.0).astype(BF16)
    out0 = [w["even_w_out"].astype(BF16) + behind]
    out0_sent = gather_start("gather_out0_start", out0, own_block_only("land_out0", out0))
    behind = (out0_sent[4][0, 0] * 0.0).astype(BF16)
    ffn0 = [w["ffn_w_up"][0:1].astype(BF16) + behind, w["ffn_w_down"][0:1].astype(BF16) + behind]
    ffn0_sent = gather_start("gather_ffn0_start", ffn0, own_block_only("land_ffn0", ffn0))
    behind = (ffn0_sent[4][0, 0] * 0.0).astype(BF16)
    mix1w = [jnp.swapaxes(w["odd_w_in"], 1, 2).astype(BF16) + behind, w["odd_w_out"].astype(BF16) + behind]
    mix1_sent = gather_start("gather_mix1_start", mix1w, own_block_only("land_mix1", mix1w))
    behind = (mix1_sent[4][0, 0] * 0.0).astype(BF16)
    ffn1 = [w["ffn_w_up"][1:2].astype(BF16) + behind, w["ffn_w_down"][1:2].astype(BF16) + behind]
    ffn1_sent = gather_start("gather_ffn1_start", ffn1, own_block_only("land_ffn1", ffn1))
    started = ffn1_sent[4]
    rg_cw = _cols_from_gather(g["rg_conv_w"])[0]
    n_layer = ffn_conv_w.shape[0]
    cw5 = g["ffn_conv_w"].reshape(2, N_DEV // 2, n_layer, FFN_CONV, FF_BLK)
    cb5 = ffn_conv_b.reshape(n_layer, 2, N_DEV // 2, 1, FF_BLK)
    gain = lambda l, k: gains[l, k:k + 1, :]
    wa_bd, wx_bd = _block_diag(rg_wa[0]), _block_diag(rg_wx[0])
    fbias = jnp.pad(fox_f_bias, ((0, 0), (0, LANES - C_HEADS)))

    x0 = x.reshape(m, D_MODEL)
    tgt = loss_target.reshape(m, D_MODEL)

    (h0,) = tile_fwd("l0_prenorm", fn_prenorm_after, m=m, tm=tm, nj=1, rows=[Row(x0)], pars=[Par(gain(0, 0)), Par(started)],
                     outs=[Out(D_MODEL, BF16)])
    z0 = mm("l0_in", "nn",
            Blk(h0, (tmm, D_MODEL), lambda i, j, k: (i, 0)),
            Blk(w_in_e, (2, None, D_MODEL, 384), lambda i, j, k: (j, 0, 0, 0)),
            Blk((m, 3072), (tmm, 768), lambda i, j, k: (i, j)), F32, (nm, N_DEV // 2, 1), b_join=True)
    oa, sprev = hgrn_fwd("l0_hgrn", z0, hgrn_lb_logits, hgrn_norm, n_batch=n_batch, seq=seq)
    rg_rows = lambda: [Row(z0, LANES, 16), Row(z0, LANES, 20)]
    rg_pars = lambda: [Par(rg_cw, "col", LANES), Par(rg_conv_b, "col", LANES), Par(wa_bd, "row", LANES), Par(rg_ba, "col", LANES),
                       Par(wx_bd, "row", LANES), Par(rg_bx, "col", LANES), Par(rg_lambda, "col", LANES)]
    (ob,) = tile_fwd("l0_rglru", fn_rglru, m=m, tm=seq, nj=B_WIDTH // LANES, rows=rg_rows(), pars=rg_pars(),
                     outs=[Out(B_WIDTH, BF16, LANES)])
    mixcat0 = jnp.concatenate([oa, ob], axis=-1)
    (g_out_e,) = gather_wait("gather_out0_wait", out0_sent[0], out0_sent[1], out0_sent[2], out0_sent[3], mixcat0)
    w_out_e = g_out_e.reshape(D_MODEL, D_MODEL)
    mix0 = mm2d("l0_out", "nn", mixcat0, w_out_e)
    x1, h1 = tile_fwd("l0_postnorm", fn_addnorm2, m=m, tm=tm, nj=1, rows=[Row(x0), Row(mix0)], pars=[Par(gain(0, 1)), Par(gain(0, 2))],
                      outs=[Out(D_MODEL, F32), Out(D_MODEL, BF16)])
    w_up_g0, w_down_g0 = gather_wait("gather_ffn0_wait", ffn0_sent[0], ffn0_sent[1], ffn0_sent[2], ffn0_sent[3], h1)
    hid0, act0, f0 = _ffn_forward("l0_ffn", 0, h1, w_up_g0, cw5, cb5, w_down_g0, m, seq)
    x2, h2 = tile_fwd("l0_ffnnorm", fn_addnorm2, m=m, tm=tm, nj=1, rows=[Row(x1), Row(f0)], pars=[Par(gain(0, 3)), Par(gain(1, 0))],
                      outs=[Out(D_MODEL, F32), Out(D_MODEL, BF16)])

    g_in_o, g_out_o = gather_wait("gather_mix1_wait", mix1_sent[0], mix1_sent[1], mix1_sent[2], mix1_sent[3], h2)
    w_in_o_t = jnp.pad(g_in_o.reshape(3088, D_MODEL), ((0, 3200 - 3088), (0, 0)))
    w_out_o = g_out_o.reshape(D_MODEL, D_MODEL)
    z1 = mm2d("l1_in", "nt", h2, w_in_o_t)
    (cgate,) = tile_fwd("l1_gate", fn_fox_gate, m=m, tm=seq, nj=1, rows=[Row(z1, LANES, 3072 // LANES)], pars=[Par(fbias)],
                        outs=[Out(LANES, F32)])
    place, ones_q, ones_k = term_placement()
    qterm, kterm = tile_fwd("l1_terms", fn_fox_terms, m=m, tm=tm, nj=1, rows=[Row(cgate)],
                            pars=[Par(place), Par(ones_q), Par(ones_k)], outs=[Out(TERM_W, BF16), Out(TERM_W, BF16)])
    oc, lse = fox_pair_fwd("l1_attn", z1, qterm, kterm, n_batch=n_batch, seq=seq)
    blk_b = min(ATT_BLK, seq)
    lse = lse.reshape(n_batch, N_PAIR, -1, 2, lse.shape[-1] // blk_b, blk_b).swapaxes(3, 4).reshape(n_batch, N_PAIR, seq // blk_b, 2, 1, blk_b)
    mix1 = mm2d("l1_out", "nn", oc, w_out_o)
    x3, h3 = tile_fwd("l1_postnorm", fn_addnorm2, m=m, tm=tm, nj=1, rows=[Row(x2), Row(mix1)], pars=[Par(gain(1, 1)), Par(gain(1, 2))],
                      outs=[Out(D_MODEL, F32), Out(D_MODEL, BF16)])
    w_up_g1, w_down_g1 = gather_wait("gather_ffn1_wait", ffn1_sent[0], ffn1_sent[1], ffn1_sent[2], ffn1_sent[3], h3)
    hid1, act1, f1 = _ffn_forward("l1_ffn", 1, h3, w_up_g1, cw5, cb5, w_down_g1, m, seq)
    dy, df1, loss_part, d_g13 = loss_head("loss", x3, f1, tgt, gain(1, 3), m=m, tm=tm)
    dh3, d_wup1, d_cw1, d_cb1, d_wdown1 = _ffn_backward("l1_ffn", 1, df1, h3, hid1, act1, w_up_g1, cw5, cb5, w_down_g1, m, seq)
    dx2, dmix1, d_g11, d_g12 = tile_bwd("l1_dpostnorm", fn_addnorm2, m=m, tm=tm, nj=1, rows=[Row(x2), Row(mix1)],
                                        pars=[Par(gain(1, 1)), Par(gain(1, 2))], cts=[Row(dy), Row(dh3)],
                                        drows=[Out(D_MODEL, F32), Out(D_MODEL, BF16)])
    doc = mm2d("l1_doc", "nt", dmix1, w_out_o, BF16)
    d_wout_o = mm2d("l1_dwout", "tn", oc, dmix1)
    dq, dk, dv, dc = fox_pair_bwd("l1_dattn", z1, qterm, kterm, oc, doc, lse, n_batch=n_batch, seq=seq)
    dzf, d_fbias = tile_bwd("l1_dgate", fn_fox_gate, m=m, tm=seq, nj=1, rows=[Row(z1, LANES, 3072 // LANES)], pars=[Par(fbias)],
                            cts=[Row(dc)], drows=[Out(LANES, BF16)])
    dz1 = jnp.concatenate([dq, dk, dv, dzf], axis=-1)
    dh2 = mm2d("l1_dh", "nn", dz1, w_in_o_t, BF16)
    d_win_o_t = mm2d("l1_dwin", "tn", dz1, h2, BF16)

    send1 = [d_win_o_t[:3088].reshape(N_DEV, 1, 3088 // N_DEV, D_MODEL),
             d_wout_o.reshape(N_DEV, 1, D_MODEL // N_DEV, D_MODEL).astype(BF16), d_wup1, d_wdown1]
    sent1 = _split_exchange("exchange_l1_start", send1, own_slots_only("land_l1", send1), None, None)

    dx1, df0, d_g03, d_g10 = tile_bwd("l0_dffnnorm", fn_addnorm2_after, m=m, tm=tm, nj=1, rows=[Row(x1), Row(f0)],
                                      pars=[Par(gain(0, 3)), Par(gain(1, 0)), Par(sent1[4])], cts=[Row(dx2), Row(dh2)],
                                      drows=[Out(D_MODEL, F32), Out(D_MODEL, BF16)])[:4]
    dh1, d_wup0, d_cw0, d_cb0, d_wdown0 = _ffn_backward("l0_ffn", 0, df0, h1, hid0, act0, w_up_g0, cw5, cb5, w_down_g0, m, seq)
    send0 = [d_wup0, d_wdown0]
    sent0 = _split_exchange("exchange_ffn0_start", send0, own_slots_only("land_dffn0", send0), None, None)
    dx0a, dmix0, d_g01, d_g02 = tile_bwd("l0_dpostnorm", fn_addnorm2_after, m=m, tm=tm, nj=1, rows=[Row(x0), Row(mix0)],
                                         pars=[Par(gain(0, 1)), Par(gain(0, 2)), Par(sent0[4])], cts=[Row(dx1), Row(dh1)],
                                         drows=[Out(D_MODEL, F32), Out(D_MODEL, BF16)])[:4]
    dmixcat0 = mm2d("l0_dmixcat", "nt", dmix0, w_out_e, BF16)
    d_wout_e = mm2d("l0_dwout", "tn", mixcat0, dmix0)
    dzq, dzf0, dzv, dzg, d_lb, d_hnorm = hgrn_bwd("l0_dhgrn", z0, sprev, hgrn_lb_logits, hgrn_norm, dmixcat0, n_batch=n_batch, seq=seq)
    dzx, dzy, d_rcw, d_rcb, d_wa, d_ba, d_wx, d_bx, d_lam = tile_bwd(
        "l0_drglru", fn_rglru, m=m, tm=seq, nj=B_WIDTH // LANES, rows=rg_rows(), pars=rg_pars(),
        cts=[Row(dmixcat0, LANES, A_WIDTH // LANES)], drows=[Out(B_WIDTH, BF16, LANES), Out(B_WIDTH, BF16, LANES)])
    dz0 = jnp.concatenate([dzq, dzf0, dzv, dzg, dzx, dzy], axis=-1)
    d_win_e = mm("l0_dwin", "tn",
                 Blk(h0, (tmm, D_MODEL), lambda i, j, k: (k, 0)),
                 Blk(dz0, (tmm, 768), lambda i, j, k: (k, j)),
                 Blk(w_in_e.shape, (2, None, D_MODEL, 384), lambda i, j, k: (j, 0, 0, 0)), BF16, (1, N_DEV // 2, nm), o_split=True)
    send_e = [d_win_e, d_wout_e.reshape(N_DEV, 1, D_MODEL // N_DEV, D_MODEL).astype(BF16)]
    sent_e = _split_exchange("exchange_even_start", send_e, own_slots_only("land_even", send_e), None, None)
    d_ffn_cb = jnp.stack([d_cb0, d_cb1]).reshape(n_layer, 2 * D_FF)
    rep = {"hgrn_lb_logits": d_lb, "hgrn_norm": d_hnorm, "rg_conv_b": d_rcb, "rg_wa": _block_diag_grad(d_wa)[None], "rg_ba": d_ba,
           "rg_wx": _block_diag_grad(d_wx)[None], "rg_bx": d_bx, "rg_lambda": d_lam, "fox_f_bias": d_fbias[:, :C_HEADS],
           "ffn_conv_b": d_ffn_cb}
    rep_blocks = [rep[n] for n in REPLICATED] + [loss_part]
    rep_sent = gather_start("gather_partials_start", rep_blocks, own_block_only("land_partials", rep_blocks))
    dh0 = mm("l0_dh", "nt",
             Blk(dz0, (tmm, 768), lambda i, j, k: (i, k)),
             Blk(w_in_e, (2, None, D_MODEL, 384), lambda i, j, k: (k, 0, 0, 0)),
             Blk((m, D_MODEL), (tmm, D_MODEL), lambda i, j, k: (i, 0)), BF16, (nm, 1, N_DEV // 2), after=sent_e[4] + rep_sent[4],
             b_join=True)
    dx0, d_g00 = tile_bwd("l0_dprenorm", fn_input_norm, m=m, tm=tm, nj=1, rows=[Row(x0)], pars=[Par(gain(0, 0))],
                          cts=[Row(dx0a), Row(dh0)], drows=[Out(D_MODEL, F32)])

    d_gains = jnp.stack([jnp.concatenate([d_g00, d_g01, d_g02, d_g03], axis=0), jnp.concatenate([d_g10, d_g11, d_g12, d_g13], axis=0)])
    d_ffn_cw = jnp.stack([d_cw0, d_cw1], axis=2).reshape(N_DEV, n_layer, FFN_CONV, FF_BLK)
    r_in_o, r_out_o, r_up1, r_down1 = _split_exchange("exchange_l1_wait", sent1[2], sent1[3], sent1[:2], dx0)
    r_up0, r_down0 = _split_exchange("exchange_ffn0_wait", sent0[2], sent0[3], sent0[:2], dx0)
    r_in_e, r_out_e = _split_exchange("exchange_even_wait", sent_e[2], sent_e[3], sent_e[:2], dx0)
    recv, res = {}, {}
    flipped = ("odd_w_in", "ffn_w_up")
    view = lambda n, t: jnp.swapaxes(t, 1, 2) if n in flipped else t
    for n, r in (("even_w_in", r_in_e), ("even_w_out", r_out_e), ("odd_w_in", r_in_o), ("odd_w_out", r_out_o)):
        res[n] = [view(n, t) for t in adam_tiled("adam_" + n, r, view(n, w[n]), view(n, mom[n]), view(n, var[n]))]
    for n, parts_l in (("ffn_w_up", (r_up0, r_up1)), ("ffn_w_down", (r_down0, r_down1))):
        wmv = (view(n, w[n]), view(n, mom[n]), view(n, var[n]))
        first_layer = adam_tiled(f"adam_{n}_0", parts_l[0], *wmv, layer=0)
        res[n] = [view(n, t) for t in adam_tiled(f"adam_{n}_1", parts_l[1], *wmv, layer=1, prev=first_layer)]
    small_send = [_cols_to_blocks(d_gains), _cols_to_blocks(d_rcw[None]), d_ffn_cw]
    recv.update(zip(SMALL_SHARDED, all_to_all("exchange_small", small_send)))

    parts = gather_wait("gather_partials_wait", rep_sent[0], rep_sent[1], rep_sent[2], rep_sent[3], dx0)
    for n, p in zip(REPLICATED, parts):
        recv[n] = p
    small = SMALL_SHARDED + REPLICATED
    small_res, (loss_sum,) = adam_small("adam_small", [(recv[n], w[n], mom[n], var[n]) for n in small], [parts[-1]])
    res.update(dict(zip(small, small_res)))

    out = [loss_sum[0, 0], dx0.reshape(x.shape)]
    for k in range(4):
        out += [res[n][k] for n in NAMES]
    return tuple(out)
```

```python
import functools

import jax
import jax.numpy as jnp
from jax import lax
from jax.experimental import pallas as pl
from jax.experimental.pallas import tpu as pltpu

F32 = jnp.float32
BF16 = jnp.bfloat16

D_MODEL = 1024
A_HEADS = 4
A_WIDTH = 512
HGRN_CHUNK = 64
HGRN_SEG = 2048
B_WIDTH = 512
B_BLOCKS = 8
B_BLOCK_DIM = 64
B_CONV = 4
RG_C = 8.0
C_HEADS = 16
C_HEAD_DIM = 64
D_FF = 2816
FFN_CONV = 3
EPS = 1e-6
LANES = 128
HALO = 16
N_DEV = 8
FF_BLK = 2 * D_FF // N_DEV
MESH = pl.DeviceIdType.MESH
NEG = -1e30
VMEM_LIMIT = 56 * 1024 * 1024

ADAM_LR = 0.001
ADAM_B1 = 0.9
ADAM_B2 = 0.999
ADAM_EPS = 1e-08
ADAM_WD = 0.01
ADAM_STEP = 10


def _dg(a, b, pat):
    nb = a.ndim - 2
    batch = (tuple(range(nb)), tuple(range(nb)))
    ca = a.ndim - 1 if pat[0] == "n" else a.ndim - 2
    cb = b.ndim - 2 if pat[1] == "n" else b.ndim - 1
    return lax.dot_general(a.astype(BF16), b.astype(BF16), (((ca,), (cb,)), batch), preferred_element_type=F32)


@functools.partial(jax.custom_vjp, nondiff_argnums=(2,))
def bdot(a, b, pat):
    return _dg(a, b, pat)


def _bdot_fwd(a, b, pat):
    return _dg(a, b, pat), (a, b)


def _bdot_bwd(pat, res, g):
    a, b = res
    if pat == "nn":
        return _dg(g, b, "nt"), _dg(a, g, "tn")
    if pat == "nt":
        return _dg(g, b, "nn"), _dg(g, a, "tn")
    return _dg(b, g, "nt"), _dg(a, g, "nn")


bdot.defvjp(_bdot_fwd, _bdot_bwd)


def _shift_raw(x, s, up, fill):
    if s == 0:
        return x
    n = x.shape[0]
    r = pltpu.roll(x, (n - s) if up else s, 0)
    idx = lax.broadcasted_iota(jnp.int32, x.shape, 0)
    mask = (idx >= n - s) if up else (idx < s)
    return jnp.where(mask, jnp.asarray(fill, x.dtype), r)


@functools.partial(jax.custom_vjp, nondiff_argnums=(1,))
def shift_down(x, s):
    return _shift_raw(x, s, False, 0.0)


def _shift_down_fwd(x, s):
    return _shift_raw(x, s, False, 0.0), None


def _shift_down_bwd(s, _, g):
    return (_shift_raw(g, s, True, 0.0),)


shift_down.defvjp(_shift_down_fwd, _shift_down_bwd)


def _scan_impl(a, u, up):
    n = a.shape[0]
    s = 1
    while s < n:
        u = a * _shift_raw(u, s, up, 0.0) + u
        if 2 * s < n:
            a = a * _shift_raw(a, s, up, 1.0)
        s *= 2
    return u


@jax.custom_vjp
def lin_scan(a, u):
    return _scan_impl(a, u, False)


def _lin_scan_fwd(a, u):
    h = _scan_impl(a, u, False)
    return h, (a, h)


def _lin_scan_bwd(res, g):
    a, h = res
    gh = _scan_impl(_shift_raw(a, 1, True, 0.0), g, True)
    return gh * _shift_raw(h, 1, False, 0.0), gh


lin_scan.defvjp(_lin_scan_fwd, _lin_scan_bwd)


def _cumsum_impl(x, up, period):
    n = x.shape[0]
    span = n if period is None else period
    idx = lax.broadcasted_iota(jnp.int32, x.shape, 0)
    pos = idx if period is None else idx % period
    s = 1
    while s < span:
        sh = _shift_raw(x, s, up, 0.0)
        if period is not None:
            keep = (pos < period - s) if up else (pos >= s)
            sh = jnp.where(keep, sh, 0.0)
        x = x + sh
        s *= 2
    return x


@functools.partial(jax.custom_vjp, nondiff_argnums=(1,))
def cumsum_rows(x, period):
    return _cumsum_impl(x, False, period)


def _cumsum_fwd(x, period):
    return _cumsum_impl(x, False, period), None


def _cumsum_bwd(period, _, g):
    return (_cumsum_impl(g, True, period),)


cumsum_rows.defvjp(_cumsum_fwd, _cumsum_bwd)


def _sigmoid(x):
    return jax.nn.sigmoid(x)


def _expm1(x):
    return jnp.tanh(0.5 * x) * (jnp.exp(x) + 1.0)


def _softplus(x):
    return jnp.maximum(x, 0.0) + jnp.log(1.0 + jnp.exp(-jnp.abs(x)))


def _rms(x, g):
    return x * lax.rsqrt(jnp.mean(x * x, axis=-1, keepdims=True) + EPS) * g


def fn_prenorm(x, g):
    return (_rms(x, g).astype(BF16),)


def fn_prenorm_after(x, g, _token):
    return fn_prenorm(x, g)


def fn_addnorm2(x, y, g_post, g_pre):
    x1 = x + _rms(y, g_post)
    return x1, _rms(x1, g_pre).astype(BF16)


def fn_addnorm2_after(x, y, g_post, g_pre, _token):
    return fn_addnorm2(x, y, g_post, g_pre)


def fn_input_norm(x, g):
    return x, _rms(x, g).astype(BF16)


def _causal_conv(x, w, b, taps):
    c = b
    for k in range(taps):
        c = c + w[k:k + 1, :] * shift_down(x, taps - 1 - k)
    return c


def fn_rglru(xb, yb, cw, cb, wa, ba, wx, bx, lam):
    xf = _causal_conv(xb, cw, cb, B_CONV)
    r = _sigmoid(bdot(xf, wa, "nn") + ba)
    i = _sigmoid(bdot(xf, wx, "nn") + bx)
    log_a = -RG_C * r * _softplus(-lam)
    a = jnp.exp(log_a)
    u = jnp.sqrt(-_expm1(2.0 * log_a)) * (i * xf)
    h = lin_scan(a, u)
    return ((h * jax.nn.gelu(yb)).astype(BF16),)


def fn_fox_gate(zf, bias):
    return (cumsum_rows(jax.nn.log_sigmoid(zf + bias), None),)


def fn_hgrn_seg(q, fl, v, g, st, logits, hn):
    rows = q.shape[0]
    nc = rows // HGRN_CHUNK
    l0, l1, l2 = logits[0:1, :], logits[1:2, :], logits[2:3, :]
    mx = jnp.maximum(jnp.maximum(l0, l1), l2)
    e0, e1, e2 = jnp.exp(l0 - mx), jnp.exp(l1 - mx), jnp.exp(l2 - mx)
    lb = e0 / (e0 + e1 + e2)
    forget = lb + (1.0 - lb) * _sigmoid(fl)
    qs = q * _sigmoid(q)
    kk = 1.0 - forget
    logf = jnp.log(forget)
    bcum = cumsum_rows(logf, HGRN_CHUNK)
    c3 = lambda t: t.reshape(nc, HGRN_CHUNK, 128)
    b_last = jnp.sum(c3(logf), axis=1, keepdims=True)
    bcum3 = c3(bcum)
    q_dec = c3(qs) * jnp.exp(bcum3)
    k_dec = c3(kk) * jnp.exp(-bcum3)
    k_upd = c3(kk) * jnp.exp(b_last - bcum3)
    v3 = c3(v)
    scores = bdot(q_dec, k_dec, "nt")
    ri = lax.broadcasted_iota(jnp.int32, scores.shape, 1)
    ci = lax.broadcasted_iota(jnp.int32, scores.shape, 2)
    scores = jnp.where(ri >= ci, scores, 0.0)
    o = bdot(scores, v3, "nn")
    upd_t = bdot(v3, k_upd, "tn")
    dec = jnp.exp(b_last)
    prev = []
    for n in range(nc):
        prev.append(st)
        st = st * dec[n] + upd_t[n]
    o = o + bdot(q_dec, jnp.stack(prev), "nt")
    o = o.reshape(rows, 128)
    o = o * lax.rsqrt(jnp.mean(o * o, axis=-1, keepdims=True) + EPS) * hn
    return (o * _sigmoid(g)).astype(BF16), st


def _ffn_conv(xg, xv, cw, cb):
    cg = _causal_conv(xg, cw[0], cb[0], FFN_CONV)[HALO:]
    cv = _causal_conv(xv, cw[1], cb[1], FFN_CONV)[HALO:]
    return cg, cv


def _ffn_gate(cg, cv):
    return jax.nn.gelu(cg) * cv


class Row:
    def __init__(self, arr, cb=None, off=0):
        self.arr, self.cb, self.off = arr, cb, off

    def spec(self, tm):
        if self.cb is None:
            return pl.BlockSpec((tm, self.arr.shape[1]), lambda j, i: (i, 0))
        off = self.off
        return pl.BlockSpec((tm, self.cb), lambda j, i: (i, j + off))


class Par:
    def __init__(self, arr, kind="full", bs=None):
        self.arr, self.kind, self.bs = arr, kind, bs

    def block(self):
        if self.kind == "full":
            return self.arr.shape
        if self.kind == "col":
            return (self.arr.shape[0], self.bs)
        return (self.bs, self.arr.shape[1])

    def spec(self):
        if self.kind == "full":
            return pl.BlockSpec(self.block(), lambda j, i: (0, 0))
        if self.kind == "col":
            return pl.BlockSpec(self.block(), lambda j, i: (0, j))
        return pl.BlockSpec(self.block(), lambda j, i: (j, 0))


class Out:
    def __init__(self, width, dtype, cb=None, off=0):
        self.width, self.dtype, self.cb, self.off = width, dtype, cb, off

    def spec(self, tm):
        if self.cb is None:
            return pl.BlockSpec((tm, self.width), lambda j, i: (i, 0))
        off = self.off
        return pl.BlockSpec((tm, self.cb), lambda j, i: (i, j + off))


def _params(sem):
    return pltpu.CompilerParams(dimension_semantics=sem, vmem_limit_bytes=VMEM_LIMIT)


def tile_fwd(name, fn, *, m, tm, nj, rows, pars, outs, n_acc=0):
    n_r, n_p, n_o = len(rows), len(pars), len(outs)

    def body(*refs):
        ins = [r[...] for r in refs[:n_r + n_p]]
        res = fn(*ins)
        o_refs = refs[n_r + n_p:]
        for k in range(n_o):
            o_refs[k][...] = res[k].astype(o_refs[k].dtype)
        first = jnp.logical_and(pl.program_id(0) == 0, pl.program_id(1) == 0)
        for k in range(n_acc):
            ref = o_refs[n_o + k]

            @pl.when(first)
            def _():
                ref[...] = jnp.zeros_like(ref)

            ref[...] += res[n_o + k]

    out_shape = [jax.ShapeDtypeStruct((m, o.width), o.dtype) for o in outs]
    out_specs = [o.spec(tm) for o in outs]
    for _ in range(n_acc):
        out_shape.append(jax.ShapeDtypeStruct((1, LANES), F32))
        out_specs.append(pl.BlockSpec((1, LANES), lambda j, i: (0, 0)))
    sem = ("arbitrary", "arbitrary") if n_acc else ("parallel", "parallel")
    return pl.pallas_call(
        body, grid=(nj, m // tm), name=name,
        in_specs=[r.spec(tm) for r in rows] + [p.spec() for p in pars],
        out_specs=out_specs, out_shape=out_shape, compiler_params=_params(sem),
    )(*[r.arr for r in rows], *[p.arr for p in pars])


def tile_bwd(name, fn, *, m, tm, nj, rows, pars, cts, drows):
    n_r, n_p, n_c = len(rows), len(pars), len(cts)
    want = [k for k in range(n_r) if drows[k] is not None]

    def body(*refs):
        ins = [r[...] for r in refs[:n_r + n_p]]
        ct = [r[...] for r in refs[n_r + n_p:n_r + n_p + n_c]]
        o_refs = refs[n_r + n_p + n_c:]
        res, vjp = jax.vjp(fn, *ins)
        grads = vjp(tuple(c.astype(r.dtype) for c, r in zip(ct, res)))
        for pos, k in enumerate(want):
            o_refs[pos][...] = grads[k].astype(o_refs[pos].dtype)
        for k in range(n_p):
            ref = o_refs[len(want) + k]
            first = pl.program_id(1) == 0
            if pars[k].kind == "full":
                first = jnp.logical_and(first, pl.program_id(0) == 0)

            @pl.when(first)
            def _():
                ref[...] = jnp.zeros_like(ref)

            ref[...] += grads[n_r + k].astype(F32)

    out_shape = [jax.ShapeDtypeStruct((m, drows[k].width), drows[k].dtype) for k in want]
    out_specs = [drows[k].spec(tm) for k in want]
    for p in pars:
        out_shape.append(jax.ShapeDtypeStruct(p.arr.shape, F32))
        out_specs.append(p.spec())
    return pl.pallas_call(
        body, grid=(nj, m // tm), name=name,
        in_specs=[r.spec(tm) for r in rows] + [p.spec() for p in pars] + [c.spec(tm) for c in cts],
        out_specs=out_specs, out_shape=out_shape, compiler_params=_params(("arbitrary", "arbitrary")),
    )(*[r.arr for r in rows], *[p.arr for p in pars], *[c.arr for c in cts])


def loss_head(name, x, y, tgt, g, *, m, tm):
    def body(x_ref, y_ref, t_ref, g_ref, dout_ref, dy_ref, loss_ref, dg_ref):
        normed, vjp = jax.vjp(_rms, y_ref[...], g_ref[...])
        err = x_ref[...] + normed - t_ref[...]
        dout = err * (1.0 / D_MODEL)
        dy, dg = vjp(dout)
        dout_ref[...] = dout
        dy_ref[...] = dy.astype(dy_ref.dtype)

        @pl.when(pl.program_id(0) == 0)
        def _():
            loss_ref[...] = jnp.zeros_like(loss_ref)
            dg_ref[...] = jnp.zeros_like(dg_ref)

        loss_ref[...] += 0.5 * jnp.sum(jnp.mean(err * err, axis=-1, keepdims=True), axis=0, keepdims=True)
        dg_ref[...] += dg

    row = pl.BlockSpec((tm, D_MODEL), lambda i: (i, 0))
    whole = lambda w: pl.BlockSpec((1, w), lambda i: (0, 0))
    return pl.pallas_call(
        body, grid=(m // tm,), name=name, in_specs=[row, row, row, whole(D_MODEL)],
        out_specs=[row, row, whole(LANES), whole(D_MODEL)],
        out_shape=[jax.ShapeDtypeStruct((m, D_MODEL), F32), jax.ShapeDtypeStruct((m, D_MODEL), BF16),
                   jax.ShapeDtypeStruct((1, LANES), F32), jax.ShapeDtypeStruct((1, D_MODEL), F32)],
        compiler_params=_params(("arbitrary",)),
    )(x, y, tgt, g)


class Blk:
    def __init__(self, arr, block, index):
        self.arr, self.block, self.index = arr, block, index

    def spec(self):
        return pl.BlockSpec(self.block, self.index)


def _flat2(v):
    return v if v.ndim == 2 else v.reshape(-1, v.shape[-1])


def mm(name, pat, a, b, o, out_dtype, grid, after=None, b_join=False, o_split=False):
    nk = grid[2]
    o_shape = o.arr

    def put(o_ref, r):
        if o_split:
            half = r.shape[1] // 2
            o_ref[0] = r[:, :half].astype(out_dtype)
            o_ref[1] = r[:, half:].astype(out_dtype)
        else:
            o_ref[...] = r.astype(out_dtype).reshape(o_ref.shape)

    def body(*refs):
        a_ref, b_ref = refs[0], refs[1]
        o_ref = refs[3] if after is not None else refs[2]
        bv = jnp.concatenate([b_ref[0], b_ref[1]], axis=1) if b_join else _flat2(b_ref[...])
        r = _dg(_flat2(a_ref[...]), bv, pat)
        if nk == 1:
            put(o_ref, r)
            return
        acc_ref = refs[-1]
        kk = pl.program_id(2)

        @pl.when(kk == 0)
        def _():
            acc_ref[...] = r

        @pl.when(kk > 0)
        def _():
            acc_ref[...] += r

        @pl.when(kk == nk - 1)
        def _():
            put(o_ref, acc_ref[...])

    ob = [d for d in o.block if d is not None]
    if o_split:
        acc_shape = (ob[1], 2 * ob[2])
    else:
        acc_shape = (ob[0], ob[1]) if len(ob) == 2 else (ob[0] * ob[1], ob[2])
    in_specs = [a.spec(), b.spec()]
    args = [a.arr, b.arr]
    if after is not None:
        in_specs.append(pl.BlockSpec(memory_space=pl.ANY))
        args.append(after)
    return pl.pallas_call(
        body, grid=grid, name=name, in_specs=in_specs, out_specs=o.spec(),
        out_shape=jax.ShapeDtypeStruct(o_shape, out_dtype),
        scratch_shapes=[pltpu.VMEM(acc_shape, F32)] if nk > 1 else [],
        compiler_params=_params(("parallel", "parallel", "arbitrary")),
    )(*args)


def _div_tile(n, cap):
    if n <= cap:
        return n
    best = 128
    for t in range(128, cap + 1, 128):
        if n % t == 0:
            best = t
    return best


def mm2d(name, pat, a, b, out_dtype=F32):
    if pat == "tn":
        k, m = a.shape
    else:
        m, k = a.shape
    n = b.shape[0] if pat == "nt" else b.shape[1]
    tm, tn, tk = _div_tile(m, 1024), _div_tile(n, 1024), _div_tile(k, 1024)
    a_blk = Blk(a, (tk, tm), lambda i, j, kk: (kk, i)) if pat == "tn" else Blk(a, (tm, tk), lambda i, j, kk: (i, kk))
    b_blk = Blk(b, (tn, tk), lambda i, j, kk: (j, kk)) if pat == "nt" else Blk(b, (tk, tn), lambda i, j, kk: (kk, j))
    o_blk = Blk((m, n), (tm, tn), lambda i, j, kk: (i, j))
    return mm(name, pat, a_blk, b_blk, o_blk, out_dtype, (m // tm, n // tn, k // tk))


def hgrn_fwd(name, z, logits, hnorm, *, n_batch, seq):
    m = n_batch * seq
    ts = min(HGRN_SEG, seq)
    n_seg = seq // ts

    def body(q_ref, f_ref, v_ref, g_ref, lg_ref, hn_ref, o_ref, sp_ref, st_ref):
        s = pl.program_id(2)

        @pl.when(s == 0)
        def _():
            st_ref[...] = jnp.zeros_like(st_ref)

        st = st_ref[...]
        sp_ref[...] = st
        o, st_new = fn_hgrn_seg(q_ref[...], f_ref[...], v_ref[...], g_ref[...], st, lg_ref[...], hn_ref[...])
        o_ref[...] = o
        st_ref[...] = st_new

    part = lambda p: pl.BlockSpec((ts, 128), lambda h, b, s: (b * n_seg + s, 4 * p + h))
    return pl.pallas_call(
        body, grid=(A_HEADS, n_batch, n_seg), name=name,
        in_specs=[part(0), part(1), part(2), part(3),
                  pl.BlockSpec((3, 128), lambda h, b, s: (0, h)),
                  pl.BlockSpec((1, 128), lambda h, b, s: (0, h))],
        out_specs=[pl.BlockSpec((ts, 128), lambda h, b, s: (b * n_seg + s, h)),
                   pl.BlockSpec((128, 128), lambda h, b, s: ((b * n_seg + s) * A_HEADS + h, 0))],
        out_shape=[jax.ShapeDtypeStruct((m, A_WIDTH), BF16),
                   jax.ShapeDtypeStruct((n_batch * n_seg * A_HEADS * 128, 128), F32)],
        scratch_shapes=[pltpu.VMEM((128, 128), F32)],
        compiler_params=_params(("arbitrary", "arbitrary", "arbitrary")),
    )(z, z, z, z, logits, hnorm)


def hgrn_bwd(name, z, sprev, logits, hnorm, do, *, n_batch, seq):
    m = n_batch * seq
    ts = min(HGRN_SEG, seq)
    n_seg = seq // ts

    def body(q_ref, f_ref, v_ref, g_ref, sp_ref, lg_ref, hn_ref, do_ref, dq_ref, df_ref, dv_ref, dg_ref, dlg_ref, dhn_ref, dst_ref):
        s = pl.program_id(2)

        @pl.when(s == 0)
        def _():
            dst_ref[...] = jnp.zeros_like(dst_ref)

        res, vjp = jax.vjp(fn_hgrn_seg, q_ref[...], f_ref[...], v_ref[...], g_ref[...], sp_ref[...], lg_ref[...], hn_ref[...])
        dq, df, dv, dg, dst, dlg, dhn = vjp((do_ref[...].astype(res[0].dtype), dst_ref[...]))
        dq_ref[...] = dq.astype(dq_ref.dtype)
        df_ref[...] = df.astype(df_ref.dtype)
        dv_ref[...] = dv.astype(dv_ref.dtype)
        dg_ref[...] = dg.astype(dg_ref.dtype)
        dst_ref[...] = dst
        first = jnp.logical_and(pl.program_id(1) == 0, s == 0)

        @pl.when(first)
        def _():
            dlg_ref[...] = jnp.zeros_like(dlg_ref)
            dhn_ref[...] = jnp.zeros_like(dhn_ref)

        dlg_ref[...] += dlg
        dhn_ref[...] += dhn

    rev = lambda b, s: b * n_seg + (n_seg - 1 - s)
    part = lambda p: pl.BlockSpec((ts, 128), lambda h, b, s: (rev(b, s), 4 * p + h))
    head = pl.BlockSpec((ts, 128), lambda h, b, s: (rev(b, s), h))
    dpart = jax.ShapeDtypeStruct((m, A_WIDTH), BF16)
    return pl.pallas_call(
        body, grid=(A_HEADS, n_batch, n_seg), name=name,
        in_specs=[part(0), part(1), part(2), part(3),
                  pl.BlockSpec((128, 128), lambda h, b, s: (rev(b, s) * A_HEADS + h, 0)),
                  pl.BlockSpec((3, 128), lambda h, b, s: (0, h)),
                  pl.BlockSpec((1, 128), lambda h, b, s: (0, h)),
                  head],
        out_specs=[head, head, head, head,
                   pl.BlockSpec((3, 128), lambda h, b, s: (0, h)),
                   pl.BlockSpec((1, 128), lambda h, b, s: (0, h))],
        out_shape=[dpart, dpart, dpart, dpart,
                   jax.ShapeDtypeStruct(logits.shape, F32),
                   jax.ShapeDtypeStruct(hnorm.shape, F32)],
        scratch_shapes=[pltpu.VMEM((128, 128), F32)],
        compiler_params=_params(("arbitrary", "arbitrary", "arbitrary")),
    )(z, z, z, z, sprev, logits, hnorm, do)


FFN_ROWS = 128
FFN_LANES = 128


def _ffn_tiles(m, seq):
    tm = min(512, seq)
    return tm, seq // tm, m // tm


def ffn_mid_fwd(name, hid, cw, cb, layer, *, m, seq):
    tm, n_t, n_i = _ffn_tiles(m, seq)
    hb = tm // HALO

    def body(x_ref, xb_ref, cw_ref, cb_ref, o_ref, c_ref):
        first = pl.program_id(1) % n_t == 0
        before = jnp.where(first, 0.0, xb_ref[...])
        ext = jnp.concatenate([before, x_ref[...]], axis=1)
        cg, cv = _ffn_conv(ext[0], ext[1], cw_ref[...], cb_ref[...])
        o_ref[...] = _ffn_gate(cg, cv).astype(o_ref.dtype)
        c_ref[0] = cg.astype(c_ref.dtype)
        c_ref[1] = cv.astype(c_ref.dtype)

    return pl.pallas_call(
        body, grid=(N_DEV // 2, n_i), name=name,
        in_specs=[pl.BlockSpec((2, None, tm, FF_BLK), lambda d, i: (0, d, i, 0)),
                  pl.BlockSpec((2, None, HALO, FF_BLK), lambda d, i: (0, d, jnp.maximum(i * hb - 1, 0), 0)),
                  pl.BlockSpec((2, None, None, FFN_CONV, FF_BLK), lambda d, i: (0, d, layer, 0, 0)),
                  pl.BlockSpec((None, 2, None, 1, FF_BLK), lambda d, i: (layer, 0, d, 0, 0))],
        out_specs=[pl.BlockSpec((None, tm, FF_BLK), lambda d, i: (d, i, 0)),
                   pl.BlockSpec((2, None, tm, FF_BLK), lambda d, i: (0, d, i, 0))],
        out_shape=[jax.ShapeDtypeStruct((N_DEV // 2, m, FF_BLK), BF16),
                   jax.ShapeDtypeStruct((2, N_DEV // 2, m, FF_BLK), BF16)],
        compiler_params=_params(("parallel", "parallel")),
    )(hid, hid, cw, cb)


def ffn_mid_bwd(name, hid, conv, cw, dact, layer, *, m, seq):
    tm, n_t, n_i = _ffn_tiles(m, seq)
    hb = tm // HALO
    last_blk = m // HALO - 1

    rc = min(FFN_ROWS, tm)
    lane_chunks = [(l0, min(FFN_LANES, FF_BLK - l0)) for l0 in range(0, FF_BLK, FFN_LANES)]

    def body(x_ref, c_ref, ca_ref, cw_ref, da_ref, daa_ref, dx_ref, dcw_ref, dcb_ref, cext_ref, dext_ref):
        i = pl.program_id(1)
        last = i % n_t == n_t - 1
        cext_ref[:, :tm] = c_ref[...]
        cext_ref[:, tm:] = ca_ref[...]
        dext_ref[:tm] = da_ref[...]
        dext_ref[tm:] = jnp.where(last, jnp.zeros_like(daa_ref[...]), daa_ref[...])

        @pl.when(i == 0)
        def _():
            dcw_ref[...] = jnp.zeros_like(dcw_ref)
            dcb_ref[...] = jnp.zeros_like(dcb_ref)

        for l0, lw in lane_chunks:
            lanes = slice(l0, l0 + lw)

            def chunk(c, sums, lanes=lanes, lw=lw):
                r0 = pl.multiple_of(c * rc, rc)
                ext = pl.ds(r0, rc + HALO)
                cg, cv = cext_ref[0, ext, lanes].astype(F32), cext_ref[1, ext, lanes].astype(F32)
                _, vjp_gate = jax.vjp(_ffn_gate, cg, cv)
                dconv = vjp_gate(dext_ref[ext, lanes].astype(F32))
                out = []
                for half in range(2):
                    x = x_ref[half, pl.ds(r0, rc), lanes]
                    dx = None
                    for k in range(FFN_CONV):
                        s = FFN_CONV - 1 - k
                        dc_s = _shift_raw(dconv[half], s, True, 0.0)[:rc]
                        term = cw_ref[half, k:k + 1, lanes] * dc_s
                        dx = term if dx is None else dx + term
                        out.append(sums[len(out)] + jnp.sum(x * dc_s, axis=0, keepdims=True))
                    out.append(sums[len(out)] + jnp.sum(dconv[half][:rc], axis=0, keepdims=True))
                    dx_ref[half, pl.ds(r0, rc), lanes] = dx.astype(dx_ref.dtype)
                return tuple(out)

            zero = jnp.zeros((1, lw), F32)
            sums = lax.fori_loop(0, tm // rc, chunk, (zero,) * (2 * (FFN_CONV + 1)))
            for half in range(2):
                base = half * (FFN_CONV + 1)
                for k in range(FFN_CONV):
                    dcw_ref[half, k:k + 1, lanes] += sums[base + k]
                dcb_ref[half, :, lanes] += sums[base + FFN_CONV]

    return pl.pallas_call(
        body, grid=(N_DEV // 2, n_i), name=name,
        in_specs=[pl.BlockSpec((2, None, tm, FF_BLK), lambda d, i: (0, d, i, 0)),
                  pl.BlockSpec((2, None, tm, FF_BLK), lambda d, i: (0, d, i, 0)),
                  pl.BlockSpec((2, None, HALO, FF_BLK), lambda d, i: (0, d, jnp.minimum((i + 1) * hb, last_blk), 0)),
                  pl.BlockSpec((2, None, None, FFN_CONV, FF_BLK), lambda d, i: (0, d, layer, 0, 0)),
                  pl.BlockSpec((None, tm, FF_BLK), lambda d, i: (d, i, 0)),
                  pl.BlockSpec((None, HALO, FF_BLK), lambda d, i: (d, jnp.minimum((i + 1) * hb, last_blk), 0))],
        out_specs=[pl.BlockSpec((2, None, tm, FF_BLK), lambda d, i: (0, d, i, 0)),
                   pl.BlockSpec((2, None, FFN_CONV, FF_BLK), lambda d, i: (0, d, 0, 0)),
                   pl.BlockSpec((2, None, 1, FF_BLK), lambda d, i: (0, d, 0, 0))],
        out_shape=[jax.ShapeDtypeStruct((2, N_DEV // 2, m, FF_BLK), BF16),
                   jax.ShapeDtypeStruct((2, N_DEV // 2, FFN_CONV, FF_BLK), F32),
                   jax.ShapeDtypeStruct((2, N_DEV // 2, 1, FF_BLK), F32)],
        scratch_shapes=[pltpu.VMEM((2, tm + HALO, FF_BLK), BF16), pltpu.VMEM((tm + HALO, FF_BLK), BF16)],
        compiler_params=_params(("arbitrary", "arbitrary")),
    )(hid, conv, conv, cw, dact, dact)


ATT_BLK = 512
ATT_BLK_FWD = 1024
N_PAIR = C_HEADS // 2
TERM_W = C_HEADS * LANES


def term_placement():
    import numpy as np
    place = np.zeros((3, LANES, TERM_W), np.float32)
    ones_q = np.zeros((1, TERM_W), np.float32)
    ones_k = np.zeros((1, TERM_W), np.float32)
    for h in range(C_HEADS):
        for j in range(3):
            place[j, h, h * LANES + C_HEAD_DIM + j] = 1.0
            ones_q[0, h * LANES + C_HEAD_DIM + 3 + j] = 1.0
            ones_k[0, h * LANES + C_HEAD_DIM + j] = 1.0
    return (jnp.asarray(place.reshape(3 * LANES, TERM_W), BF16), jnp.asarray(ones_q, F32), jnp.asarray(ones_k, F32))


def fn_fox_terms(c, place, ones_q, ones_k):
    parts = _split3(c)
    placed = sum(_dg(parts[j], place[j * LANES:(j + 1) * LANES], "nn") for j in range(3))
    return (placed + ones_q).astype(BF16), (ones_k - pltpu.roll(placed, 3, 1)).astype(BF16)


def _head_tile(z, terms, e):
    lane = lax.broadcasted_iota(jnp.int32, z.shape, 1)
    base = z if e == 0 else pltpu.roll(z, C_HEAD_DIM, 1)
    return jnp.where(lane < C_HEAD_DIM, base, terms.astype(z.dtype))


def _head_only(z, e):
    lane = lax.broadcasted_iota(jnp.int32, z.shape, 1)
    mine = (lane < C_HEAD_DIM) if e == 0 else (lane >= C_HEAD_DIM)
    return jnp.where(mine, z, jnp.zeros_like(z)).astype(BF16)


def _pair_tile(a0, a1):
    lane = lax.broadcasted_iota(jnp.int32, a0.shape, 1)
    return jnp.where(lane < C_HEAD_DIM, a0, pltpu.roll(a1, C_HEAD_DIM, 1))


def _lane_col(a, k):
    lane = lax.broadcasted_iota(jnp.int32, a.shape, 1)
    return jnp.sum(jnp.where(lane == k, a, 0.0), axis=1, keepdims=True)


def _causal(s):
    key = lax.broadcasted_iota(jnp.int32, s.shape, 0)
    qry = lax.broadcasted_iota(jnp.int32, s.shape, 1)
    return qry >= key


def fox_pair_fwd(name, z, qterm, kterm, *, n_batch, seq):
    m = n_batch * seq
    blk = min(ATT_BLK_FWD, seq)
    nq = seq // blk
    dh = C_HEAD_DIM

    def body(zq_ref, zk_ref, zv_ref, qt_ref, kt_ref, o_ref, lse_ref, ka_ref, vt_ref):
        qi = pl.program_id(2)

        @pl.when(qi == 0)
        def _():
            zk = zk_ref[...]
            for e in range(2):
                ka_ref[e] = _head_tile(zk, kt_ref[:, e * LANES:(e + 1) * LANES], e).astype(BF16)
            for cb in range(nq):
                vt_ref[cb] = zv_ref[cb * blk:(cb + 1) * blk, :].T.astype(BF16)

        zq = zq_ref[...] * dh ** -0.5
        qa = [_head_tile(zq, qt_ref[:, e * LANES:(e + 1) * LANES], e).astype(BF16) for e in range(2)]

        def block(j, carry, diagonal):
            rows = pl.ds(pl.multiple_of(j * blk, blk), blk)
            out = []
            for e in range(2):
                mx, l, acc = carry[e]
                s = _dg(ka_ref[e, rows, :], qa[e], "nt")
                if diagonal:
                    s = jnp.where(_causal(s), s, NEG)
                mx_new = jnp.maximum(mx, jnp.max(s, axis=0, keepdims=True))
                p = jnp.exp(s - mx_new)
                alpha = jnp.exp(mx - mx_new)
                l = alpha * l + jnp.sum(p, axis=0, keepdims=True)
                acc = alpha * acc + _dg(vt_ref[j, e * dh:(e + 1) * dh, :], p, "nn")
                out.append((mx_new, l, acc))
            return tuple(out)

        one = (jnp.full((1, blk), NEG, F32), jnp.zeros((1, blk), F32), jnp.zeros((dh, blk), F32))
        carry = lax.fori_loop(0, qi, lambda j, cr: block(j, cr, False), (one, one))
        res = block(qi, carry, True)
        ot = jnp.concatenate([res[e][2] / res[e][1] for e in range(2)], axis=0)
        o_ref[...] = ot.T.astype(o_ref.dtype)
        for e in range(2):
            lse_ref[e] = res[e][0] + jnp.log(res[e][1])

    col = lambda part: (lambda b, g, i: (b, part * N_PAIR + g))
    return pl.pallas_call(
        body, grid=(n_batch, N_PAIR, nq), name=name,
        in_specs=[pl.BlockSpec((blk, LANES), lambda b, g, i: (b * nq + i, g)),
                  pl.BlockSpec((seq, LANES), col(1)),
                  pl.BlockSpec((seq, LANES), col(2)),
                  pl.BlockSpec((blk, 2 * LANES), lambda b, g, i: (b * nq + i, g)),
                  pl.BlockSpec((seq, 2 * LANES), lambda b, g, i: (b, g))],
        out_specs=[pl.BlockSpec((blk, LANES), lambda b, g, i: (b * nq + i, g)),
                   pl.BlockSpec((None, None, None, 2, 1, blk), lambda b, g, i: (b, g, i, 0, 0, 0))],
        out_shape=[jax.ShapeDtypeStruct((m, D_MODEL), BF16), jax.ShapeDtypeStruct((n_batch, N_PAIR, nq, 2, 1, blk), F32)],
        scratch_shapes=[pltpu.VMEM((2, seq, LANES), BF16), pltpu.VMEM((nq, LANES, blk), BF16)],
        compiler_params=_params(("parallel", "parallel", "arbitrary")),
    )(z, z, z, qterm, kterm)


def fox_pair_bwd(name, z, qterm, kterm, o, do, lse, *, n_batch, seq):
    m = n_batch * seq
    blk = min(ATT_BLK, seq)
    nq = seq // blk
    dh = C_HEAD_DIM

    def body(zq_ref, zk_ref, zv_ref, qt_ref, kt_ref, o_ref, do_ref, lse_ref, dq_ref, dk_ref, dv_ref, dc_ref,
             qa_ref, doh_ref, del_ref, dqt_ref, dk_acc, dv_acc):
        g, j = pl.program_id(1), pl.program_id(2)
        lane = lax.broadcasted_iota(jnp.int32, (blk, LANES), 1)

        @pl.when(jnp.logical_and(g == 0, j == 0))
        def _():
            dc_ref[...] = jnp.zeros_like(dc_ref)

        @pl.when(j == 0)
        def _():
            zq = zq_ref[...] * dh ** -0.5
            dov = do_ref[...]
            for e in range(2):
                qa_ref[e] = _head_tile(zq, qt_ref[:, e * LANES:(e + 1) * LANES], e).astype(BF16)
                doh_ref[e] = _head_only(dov, e)
            for cb in range(nq):
                rows = slice(cb * blk, (cb + 1) * blk)
                prod_t = (do_ref[rows, :].astype(F32) * o_ref[rows, :].astype(F32)).T
                for e in range(2):
                    del_ref[cb, e] = jnp.sum(prod_t[e * dh:(e + 1) * dh], axis=0, keepdims=True)
            dqt_ref[...] = jnp.zeros_like(dqt_ref)

        zk, zv = zk_ref[...], zv_ref[...]
        ka32 = [_head_tile(zk, kt_ref[:, e * LANES:(e + 1) * LANES], e) for e in range(2)]
        ka = [t.astype(BF16) for t in ka32]
        kat = [t.T.astype(BF16) for t in ka32]
        vh = [_head_only(zv, e) for e in range(2)]
        dk_acc[...] = jnp.zeros_like(dk_acc)
        dv_acc[...] = jnp.zeros_like(dv_acc)

        def block(i, diagonal):
            rows = pl.ds(pl.multiple_of(i * blk, blk), blk)
            for e in range(2):
                qv, dov = qa_ref[e, rows, :], doh_ref[e, rows, :]
                p = jnp.exp(_dg(ka[e], qv, "nt") - lse_ref[i, e])
                if diagonal:
                    p = jnp.where(_causal(p), p, 0.0)
                dv_acc[...] += _dg(p, dov, "nn")
                ds = p * (_dg(vh[e], dov, "nt") - del_ref[i, e])
                dk_acc[e] += _dg(ds, qv, "nn")
                dqt_ref[i, e] += _dg(kat[e], ds, "nn")

        block(j, True)

        def rest(i, carry):
            block(i, False)
            return carry

        lax.fori_loop(j + 1, nq, rest, 0)
        dk0, dk1 = dk_acc[0], dk_acc[1]
        dk_ref[...] = _pair_tile(dk0, dk1).astype(dk_ref.dtype)
        dv_ref[...] = dv_acc[...].astype(dv_ref.dtype)
        rows_j = pl.ds(pl.multiple_of(j * blk, blk), blk)
        for e, dke in enumerate((dk0, dk1)):
            dc_ref[rows_j, :] -= jnp.where(lane == 2 * g + e, _lane_col(dke, dh + 3), 0.0)

        @pl.when(j == nq - 1)
        def _():
            for i in range(nq):
                nat = [dqt_ref[i, e].T for e in range(2)]
                rows = slice(i * blk, (i + 1) * blk)
                dq_ref[rows, :] = (_pair_tile(nat[0], nat[1]) * dh ** -0.5).astype(dq_ref.dtype)
                for e in range(2):
                    dc_ref[rows, :] += jnp.where(lane == 2 * g + e, _lane_col(nat[e], dh), 0.0)

    col = lambda part: (lambda b, g, j: (b, part * N_PAIR + g))
    colj = lambda part: (lambda b, g, j: (b * nq + j, part * N_PAIR + g))
    pair = jax.ShapeDtypeStruct((m, D_MODEL), BF16)
    return pl.pallas_call(
        body, grid=(n_batch, N_PAIR, nq), name=name,
        in_specs=[pl.BlockSpec((seq, LANES), col(0)),
                  pl.BlockSpec((blk, LANES), colj(1)),
                  pl.BlockSpec((blk, LANES), colj(2)),
                  pl.BlockSpec((seq, 2 * LANES), lambda b, g, j: (b, g)),
                  pl.BlockSpec((blk, 2 * LANES), lambda b, g, j: (b * nq + j, g)),
                  pl.BlockSpec((seq, LANES), col(0)),
                  pl.BlockSpec((seq, LANES), col(0)),
                  pl.BlockSpec((None, None, nq, 2, 1, blk), lambda b, g, j: (b, g, 0, 0, 0, 0))],
        out_specs=[pl.BlockSpec((seq, LANES), col(0)),
                   pl.BlockSpec((blk, LANES), colj(0)),
                   pl.BlockSpec((blk, LANES), colj(0)),
                   pl.BlockSpec((seq, LANES), lambda b, g, j: (b, 0))],
        out_shape=[pair, pair, pair, jax.ShapeDtypeStruct((m, LANES), F32)],
        scratch_shapes=[pltpu.VMEM((2, seq, LANES), BF16), pltpu.VMEM((2, seq, LANES), BF16),
                        pltpu.VMEM((nq, 2, 1, blk), F32), pltpu.VMEM((nq, 2, LANES, blk), F32),
                        pltpu.VMEM((2, blk, LANES), F32), pltpu.VMEM((blk, LANES), F32)],
        compiler_params=_params(("arbitrary", "arbitrary", "arbitrary")),
    )(z, z, z, qterm, kterm, o, do, lse)


def _split3(c):
    c1 = c.astype(BF16)
    r1 = c - c1.astype(F32)
    c2 = r1.astype(BF16)
    c3 = (r1 - c2.astype(F32)).astype(BF16)
    return c1, c2, c3


def _mesh_pos():
    return lax.axis_index("x"), lax.axis_index("y"), lax.axis_index("c")


def _flip(v, bit):
    return 1 - v if bit else v


def all_gather(name, blocks):
    n = len(blocks)

    def body(*refs):
        x_refs, out_refs = refs[:n], refs[n:2 * n]
        send_sems, recv_sems, local_sems = refs[2 * n:]
        x, y, c = _mesh_pos()
        me, sibling = (x, y, c), (x, y, 1 - c)
        chips = [(1 - x, y), (x, 1 - y), (1 - x, 1 - y)]

        def slot(a, px, py, pc):
            return out_refs[a].at[4 * px + 2 * py + pc]

        def copy(a, k, blk, to, src=None):
            return pltpu.make_async_remote_copy(
                src_ref=slot(a, *blk) if src is None else src, dst_ref=slot(a, *blk),
                send_sem=send_sems.at[a, k], recv_sem=recv_sems.at[a, k], device_id=to, device_id_type=MESH)

        mine = [pltpu.make_async_copy(x_refs[a], slot(a, *me), local_sems.at[a]) for a in range(n)]
        for cp in mine:
            cp.start()
        sends = []
        for a in range(n):
            sends.append(copy(a, 0, me, sibling, src=x_refs[a]))
            sends += [copy(a, 1 + j, me, (*chip, c), src=x_refs[a]) for j, chip in enumerate(chips)]
        for cp in sends:
            cp.start()
        for j, chip in enumerate(chips):
            for a in range(n):
                copy(a, 1 + j, (*chip, c), me).wait_recv()
                passed = copy(a, 4 + j, (*chip, c), sibling)
                passed.start()
                sends.append(passed)
        for a in range(n):
            copy(a, 0, sibling, me).wait_recv()
            for j, chip in enumerate(chips):
                copy(a, 4 + j, (*chip, 1 - c), me).wait_recv()
        for cp in sends:
            cp.wait_send()
        for cp in mine:
            cp.wait()

    hbm = pl.BlockSpec(memory_space=pl.ANY)
    return pl.pallas_call(
        body, name=name, out_shape=[jax.ShapeDtypeStruct((N_DEV,) + b.shape, b.dtype) for b in blocks],
        in_specs=[hbm] * n, out_specs=[hbm] * n,
        scratch_shapes=[pltpu.SemaphoreType.DMA((n, 7)), pltpu.SemaphoreType.DMA((n, 7)), pltpu.SemaphoreType.DMA((n,))],
    )(*blocks)


def _peers(x, y, c):
    return [(_flip(x, k & 4), _flip(y, k & 2), _flip(c, k & 1)) for k in range(1, N_DEV)]


def gather_start(name, blocks, lands):
    n = len(blocks)

    def body(*refs):
        x_refs, land_refs = refs[:n], refs[n:2 * n]
        send_sems, recv_sems = refs[2 * n], refs[2 * n + 1]
        token = refs[-1]
        x, y, c = _mesh_pos()
        me = 4 * x + 2 * y + c
        for k, peer in enumerate(_peers(x, y, c)):
            for a in range(n):
                pltpu.make_async_remote_copy(
                    src_ref=x_refs[a], dst_ref=land_refs[a].at[me], send_sem=send_sems.at[7 * a + k], recv_sem=recv_sems.at[7 * a + k],
                    device_id=peer, device_id_type=MESH).start()
        token[...] = jnp.zeros_like(token)

    hbm = pl.BlockSpec(memory_space=pltpu.HBM)
    sem = pl.BlockSpec(memory_space=pltpu.SEMAPHORE)
    out_shape = ([pltpu.SemaphoreType.DMA((7 * n,)), pltpu.SemaphoreType.DMA((7 * n,))]
                 + [pltpu.HBM(b.shape, b.dtype) for b in blocks] + [pltpu.HBM(l.shape, l.dtype) for l in lands]
                 + [jax.ShapeDtypeStruct((8, LANES), F32)])
    res = pl.pallas_call(
        body, name=name, out_shape=out_shape, in_specs=[hbm] * (2 * n),
        out_specs=[sem, sem] + [hbm] * (2 * n) + [pl.BlockSpec(memory_space=pltpu.VMEM)],
        input_output_aliases={a: 2 + a for a in range(2 * n)},
        compiler_params=pltpu.CompilerParams(has_side_effects=pltpu.SideEffectType.DATAFLOW_SIDE_EFFECTING),
    )(*[pltpu.with_memory_space_constraint(b, pltpu.HBM) for b in blocks],
      *[pltpu.with_memory_space_constraint(l, pltpu.HBM) for l in lands])
    return res[0], res[1], res[2:2 + n], res[2 + n:2 + 2 * n], res[-1]


def gather_wait(name, send_sems, recv_sems, blocks, lands, after):
    n = len(blocks)

    def body(*refs):
        x_refs, land_refs = refs[:n], refs[n:2 * n]
        s_sems, r_sems = refs[2 * n], refs[2 * n + 1]
        x, y, c = _mesh_pos()
        me = 4 * x + 2 * y + c
        for k, peer in enumerate(_peers(x, y, c)):
            for a in range(n):
                cp = pltpu.make_async_remote_copy(
                    src_ref=x_refs[a], dst_ref=land_refs[a].at[me], send_sem=s_sems.at[7 * a + k], recv_sem=r_sems.at[7 * a + k],
                    device_id=peer, device_id_type=MESH)
                cp.wait_send()
                cp.wait_recv()

    hbm = pl.BlockSpec(memory_space=pltpu.HBM)
    sem = pl.BlockSpec(memory_space=pltpu.SEMAPHORE)
    res = pl.pallas_call(
        body, name=name,
        out_shape=[pltpu.HBM(b.shape, b.dtype) for b in blocks] + [pltpu.HBM(l.shape, l.dtype) for l in lands],
        in_specs=[hbm] * (2 * n) + [sem, sem, pl.BlockSpec(memory_space=pl.ANY)], out_specs=[hbm] * (2 * n),
        input_output_aliases={a: a for a in range(2 * n)},
        compiler_params=pltpu.CompilerParams(has_side_effects=pltpu.SideEffectType.DATAFLOW_SIDE_EFFECTING),
    )(*blocks, *lands, send_sems, recv_sems, after)
    return res[n:]


def _split_exchange(name, sends, lands, sems, after):
    n = len(sends)
    starting = sems is None

    def body(*refs):
        s_refs, l_refs = refs[:n], refs[n:2 * n]
        send_sems, recv_sems = refs[2 * n], refs[2 * n + 1]
        x, y, c = _mesh_pos()
        me = 4 * x + 2 * y + c
        for k, (px, py, pc) in enumerate(_peers(x, y, c)):
            for a in range(n):
                cp = pltpu.make_async_remote_copy(
                    src_ref=s_refs[a].at[4 * px + 2 * py + pc], dst_ref=l_refs[a].at[me],
                    send_sem=send_sems.at[7 * a + k], recv_sem=recv_sems.at[7 * a + k],
                    device_id=(px, py, pc), device_id_type=MESH)
                if starting:
                    cp.start()
                else:
                    cp.wait_send()
                    cp.wait_recv()
        if starting:
            refs[-1][...] = jnp.zeros_like(refs[-1])

    hbm = pl.BlockSpec(memory_space=pltpu.HBM)
    sem = pl.BlockSpec(memory_space=pltpu.SEMAPHORE)
    thru = [pltpu.HBM(t.shape, t.dtype) for t in list(sends) + list(lands)]
    effect = pltpu.CompilerParams(has_side_effects=pltpu.SideEffectType.DATAFLOW_SIDE_EFFECTING)
    if starting:
        res = pl.pallas_call(
            body, name=name, in_specs=[hbm] * (2 * n),
            out_shape=[pltpu.SemaphoreType.DMA((7 * n,)), pltpu.SemaphoreType.DMA((7 * n,))] + thru + [jax.ShapeDtypeStruct((8, LANES), F32)],
            out_specs=[sem, sem] + [hbm] * (2 * n) + [pl.BlockSpec(memory_space=pltpu.VMEM)],
            input_output_aliases={a: 2 + a for a in range(2 * n)}, compiler_params=effect,
        )(*[pltpu.with_memory_space_constraint(t, pltpu.HBM) for t in list(sends) + list(lands)])
        return res[0], res[1], res[2:2 + n], res[2 + n:2 + 2 * n], res[-1]
    res = pl.pallas_call(
        body, name=name, out_shape=thru, in_specs=[hbm] * (2 * n) + [sem, sem, pl.BlockSpec(memory_space=pl.ANY)],
        out_specs=[hbm] * (2 * n), input_output_aliases={a: a for a in range(2 * n)}, compiler_params=effect,
    )(*sends, *lands, sems[0], sems[1], after)
    return res[n:]


def unwritten(name, like):
    def body(*refs):
        pass

    hbm = pl.BlockSpec(memory_space=pl.ANY)
    return pl.pallas_call(body, name=name, out_shape=[jax.ShapeDtypeStruct(t.shape, t.dtype) for t in like],
                          out_specs=[hbm] * len(like))()


def own_slot_only(send, land, me):
    mine = lax.dynamic_index_in_dim(send, me, 0, keepdims=False)
    return lax.dynamic_update_index_in_dim(land, mine, me, 0)


def all_to_all(name, sends):
    n = len(sends)

    def body(*refs):
        s_refs, r_refs = refs[:n], refs[n:2 * n]
        send_sems, recv_sems, local_sems = refs[2 * n:]
        x, y, c = _mesh_pos()
        me = 4 * x + 2 * y + c
        mine = [pltpu.make_async_copy(s_refs[a].at[me], r_refs[a].at[me], local_sems.at[a]) for a in range(n)]
        for cp in mine:
            cp.start()
        copies = []
        for k in range(1, N_DEV):
            px, py, pc = _flip(x, k & 4), _flip(y, k & 2), _flip(c, k & 1)
            for a in range(n):
                copies.append(pltpu.make_async_remote_copy(
                    src_ref=s_refs[a].at[4 * px + 2 * py + pc], dst_ref=r_refs[a].at[me],
                    send_sem=send_sems.at[a, k - 1], recv_sem=recv_sems.at[a, k - 1],
                    device_id=(px, py, pc), device_id_type=MESH))
        for cp in copies:
            cp.start()
        for cp in copies:
            cp.wait_recv()
        for cp in copies:
            cp.wait_send()
        for cp in mine:
            cp.wait()

    hbm = pl.BlockSpec(memory_space=pl.ANY)
    return pl.pallas_call(
        body, name=name, out_shape=[jax.ShapeDtypeStruct(s.shape, s.dtype) for s in sends],
        in_specs=[hbm] * n, out_specs=[hbm] * n,
        scratch_shapes=[pltpu.SemaphoreType.DMA((n, 7)), pltpu.SemaphoreType.DMA((n, 7)), pltpu.SemaphoreType.DMA((n,))],
    )(*sends)


def _row_tile(r, cap, step):
    return next((t for t in range(cap, step - 1, -step) if r % t == 0), r)


def _sum_parts(p, n):
    t = [p[k].astype(F32) for k in range(n)]
    while len(t) > 1:
        t = [t[k] + t[k + 1] for k in range(0, len(t), 2)]
    return t[0]


def _adam(g, w, m, v):
    m = ADAM_B1 * m + (1.0 - ADAM_B1) * g
    v = ADAM_B2 * v + (1.0 - ADAM_B2) * (g * g)
    m_hat = m / (1.0 - ADAM_B1 ** ADAM_STEP)
    v_hat = v / (1.0 - ADAM_B2 ** ADAM_STEP)
    return -ADAM_LR * (m_hat / (jnp.sqrt(v_hat) + ADAM_EPS) + ADAM_WD * w), m, v


def adam_tiled(name, partials, w, m_, v_, layer=0, prev=None):
    _, r, c = w.shape
    n_part = partials.shape[0]
    tr = _row_tile(r, 256, 16)

    def body(*refs):
        p_ref, w_ref, m_ref, v_ref = refs[:4]
        g_ref, d_ref, nm_ref, nv_ref = refs[-4:]
        g = _sum_parts(p_ref, n_part)
        g_ref[...] = g
        d_ref[...], nm_ref[...], nv_ref[...] = _adam(g, w_ref[...], m_ref[...], v_ref[...])

    spec = pl.BlockSpec((None, tr, c), lambda i: (layer, i, 0))
    in_specs = [pl.BlockSpec((n_part, None, tr, c), lambda i: (0, 0, i, 0)), spec, spec, spec]
    args = [partials, w, m_, v_]
    aliases = {}
    if prev is not None:
        in_specs += [pl.BlockSpec(memory_space=pl.ANY)] * 4
        args += list(prev)
        aliases = {4 + k: k for k in range(4)}
    return pl.pallas_call(
        body, grid=(r // tr,), name=name, in_specs=in_specs,
        out_specs=[spec] * 4, out_shape=[jax.ShapeDtypeStruct(w.shape, F32)] * 4,
        input_output_aliases=aliases, compiler_params=_params(("parallel",)),
    )(*args)


def adam_small(name, items, extra):
    n, ne = len(items), len(extra)

    def body(*refs):
        ins, outs = refs[:4 * n + ne], refs[4 * n + ne:]
        for a in range(n):
            p_ref, w_ref, m_ref, v_ref = ins[4 * a:4 * a + 4]
            g = _sum_parts(p_ref, N_DEV)
            outs[4 * a][...] = g
            outs[4 * a + 1][...], outs[4 * a + 2][...], outs[4 * a + 3][...] = _adam(g, w_ref[...], m_ref[...], v_ref[...])
        for e in range(ne):
            outs[4 * n + e][...] = _sum_parts(ins[4 * n + e], N_DEV)

    args, out_shape = [], []
    for p, w, m_, v_ in items:
        args += [p, w, m_, v_]
        out_shape += [jax.ShapeDtypeStruct(w.shape, F32)] * 4
    for e in extra:
        args.append(e)
        out_shape.append(jax.ShapeDtypeStruct(e.shape[1:], F32))
    vmem = pl.BlockSpec(memory_space=pltpu.VMEM)
    res = pl.pallas_call(body, name=name, in_specs=[vmem] * len(args), out_specs=[vmem] * len(out_shape), out_shape=out_shape)(*args)
    return [res[4 * a:4 * a + 4] for a in range(n)], res[4 * n:]


def _cols_from_gather(g):
    g = jnp.moveaxis(g, 0, -2)
    return g.reshape(g.shape[:-2] + (g.shape[-2] * g.shape[-1],))


def _cols_to_blocks(w):
    w = w.reshape(w.shape[:-1] + (N_DEV, w.shape[-1] // N_DEV))
    return jnp.moveaxis(w, -2, 0)


def _block_diag(w):
    pairs = w.reshape(B_BLOCKS // 2, 2, B_BLOCK_DIM, 1, B_BLOCK_DIM)
    same = jnp.eye(2, dtype=bool).reshape(1, 2, 1, 2, 1)
    return jnp.where(same, pairs, 0.0).reshape(B_BLOCKS // 2 * LANES, LANES)


def _block_diag_grad(d):
    parts = d.reshape(B_BLOCKS // 2, 2, B_BLOCK_DIM, 2, B_BLOCK_DIM)
    same = jnp.eye(2, dtype=bool).reshape(1, 2, 1, 2, 1)
    return jnp.sum(jnp.where(same, parts, 0.0), axis=3).reshape(B_BLOCKS, B_BLOCK_DIM, B_BLOCK_DIM)


NAMES = ("norm_gains", "even_w_in", "hgrn_lb_logits", "hgrn_norm", "rg_conv_w", "rg_conv_b", "rg_wa", "rg_ba", "rg_wx", "rg_bx",
         "rg_lambda", "even_w_out", "odd_w_in", "fox_f_bias", "odd_w_out", "ffn_w_up", "ffn_conv_w", "ffn_conv_b", "ffn_w_down")
SMALL_SHARDED = ("norm_gains", "rg_conv_w", "ffn_conv_w")
REPLICATED = ("hgrn_lb_logits", "hgrn_norm", "rg_conv_b", "rg_wa", "rg_ba", "rg_wx", "rg_bx", "rg_lambda", "fox_f_bias", "ffn_conv_b")


def _ffn_forward(tag, layer, h, w_up_g, cw5, cb5, w_down_g, m, seq):
    tm = _div_tile(m, 1024)
    nm = m // tm
    hid = mm(f"{tag}_up", "nn",
             Blk(h, (tm, D_MODEL), lambda i, j, k: (i, 0)),
             Blk(w_up_g, (None, None, D_MODEL, FF_BLK), lambda i, j, k: (j, 0, 0, 0)),
             Blk((N_DEV, m, FF_BLK), (None, tm, FF_BLK), lambda i, j, k: (j, i, 0)), F32, (nm, N_DEV, 1))
    hid = hid.reshape(2, N_DEV // 2, m, FF_BLK)
    act, conv = ffn_mid_fwd(f"{tag}_mid", hid, cw5, cb5, layer, m=m, seq=seq)
    f = mm(f"{tag}_down", "nn",
           Blk(act, (None, tm, FF_BLK), lambda i, j, k: (k, i, 0)),
           Blk(w_down_g, (2, None, FF_BLK // 2, D_MODEL), lambda i, j, k: (k, 0, 0, 0)),
           Blk((m, D_MODEL), (tm, D_MODEL), lambda i, j, k: (i, 0)), F32, (nm, 1, N_DEV // 2))
    return (hid, conv), act, f


def _ffn_backward(tag, layer, df, h, hid, act, w_up_g, cw5, cb5, w_down_g, m, seq):
    tm = _div_tile(m, 1024)
    nm = m // tm
    dact = mm(f"{tag}_dact", "nt",
              Blk(df, (tm, D_MODEL), lambda i, j, k: (i, 0)),
              Blk(w_down_g, (2, None, FF_BLK // 2, D_MODEL), lambda i, j, k: (j, 0, 0, 0)),
              Blk((N_DEV // 2, m, FF_BLK), (None, tm, FF_BLK), lambda i, j, k: (j, i, 0)), BF16, (nm, N_DEV // 2, 1))
    d_wdown = mm(f"{tag}_dwdown", "tn",
                 Blk(act, (None, tm, FF_BLK), lambda i, j, k: (i, k, 0)),
                 Blk(df, (tm, D_MODEL), lambda i, j, k: (k, 0)),
                 Blk(w_down_g.shape, (2, None, FF_BLK // 2, D_MODEL), lambda i, j, k: (i, 0, 0, 0)), BF16,
                 (N_DEV // 2, 1, nm))
    dhid, d_cw, d_cb = ffn_mid_bwd(f"{tag}_dmid", hid[0], hid[1], cw5, dact, layer, m=m, seq=seq)
    dhid = dhid.reshape(N_DEV, m, FF_BLK)
    dh = mm(f"{tag}_dh", "nt",
            Blk(dhid, (None, tm, FF_BLK), lambda i, j, k: (k, i, 0)),
            Blk(w_up_g, (None, None, D_MODEL, FF_BLK), lambda i, j, k: (k, 0, 0, 0)),
            Blk((m, D_MODEL), (tm, D_MODEL), lambda i, j, k: (i, 0)), BF16, (nm, 1, N_DEV))
    d_wup = mm(f"{tag}_dwup", "tn",
               Blk(dhid, (None, tm, FF_BLK), lambda i, j, k: (i, k, 0)),
               Blk(h, (tm, D_MODEL), lambda i, j, k: (k, 0)),
               Blk((N_DEV, 1, FF_BLK, D_MODEL), (None, None, FF_BLK, D_MODEL), lambda i, j, k: (i, 0, 0, 0)), BF16,
               (N_DEV, 1, nm))
    return dh, d_wup, d_cw, d_cb, d_wdown


def kernel(x, norm_gains, even_w_in, hgrn_lb_logits, hgrn_norm, rg_conv_w, rg_conv_b, rg_wa, rg_ba, rg_wx, rg_bx, rg_lambda, even_w_out, odd_w_in, fox_f_bias, odd_w_out, ffn_w_up, ffn_conv_w, ffn_conv_b, ffn_w_down, loss_target, m_norm_gains, m_even_w_in, m_hgrn_lb_logits, m_hgrn_norm, m_rg_conv_w, m_rg_conv_b, m_rg_wa, m_rg_ba, m_rg_wx, m_rg_bx, m_rg_lambda, m_even_w_out, m_odd_w_in, m_fox_f_bias, m_odd_w_out, m_ffn_w_up, m_ffn_conv_w, m_ffn_conv_b, m_ffn_w_down, v_norm_gains, v_even_w_in, v_hgrn_lb_logits, v_hgrn_norm, v_rg_conv_w, v_rg_conv_b, v_rg_wa, v_rg_ba, v_rg_wx, v_rg_bx, v_rg_lambda, v_even_w_out, v_odd_w_in, v_fox_f_bias, v_odd_w_out, v_ffn_w_up, v_ffn_conv_w, v_ffn_conv_b, v_ffn_w_down):
    local = dict(locals())
    w = {n: local[n] for n in NAMES}
    mom = {n: local["m_" + n] for n in NAMES}
    var = {n: local["v_" + n] for n in NAMES}
    n_batch, seq, _ = x.shape
    m = n_batch * seq
    tm = _div_tile(m, 512)
    tmm = _div_tile(m, 1024)
    nm = m // tmm

    gathered = all_gather("gather_weights", [w["even_w_in"].astype(BF16)] + [w[n] for n in SMALL_SHARDED])
    g = dict(zip(("even_w_in",) + SMALL_SHARDED, gathered))
    w_in_e = g["even_w_in"]
    gains = _cols_from_gather(g["norm_gains"])
    me = 4 * lax.axis_index("x") + 2 * lax.axis_index("y") + lax.axis_index("c")
    def own_block_only(name, blocks):
        lands = unwritten(name, [jax.ShapeDtypeStruct((N_DEV,) + t.shape, t.dtype) for t in blocks])
        return [lax.dynamic_update_index_in_dim(ld, t, me, 0) for ld, t in zip(lands, blocks)]

    def own_slots_only(name, sends):
        return [own_slot_only(t, ld, me) for t, ld in zip(sends, unwritten(name, sends))]

    behind = (g["norm_gains"][0, 0, 0, 0] * 0.0).astype(BF16)
    out0 = [w["even_w_out"].astype(BF16) + behind]
    out0_sent = gather_start("gather_out0_start", out0, own_block_only("land_out0", out0))
    behind = (out0_sent[4][0, 0] * 0.0).astype(BF16)
    ffn0 = [w["ffn_w_up"][0:1].astype(BF16) + behind, w["ffn_w_down"][0:1].astype(BF16) + behind]
    ffn0_sent = gather_start("gather_ffn0_start", ffn0, own_block_only("land_ffn0", ffn0))
    behind = (ffn0_sent[4][0, 0] * 0.0).astype(BF16)
    mix1w = [jnp.swapaxes(w["odd_w_in"], 1, 2).astype(BF16) + behind, w["odd_w_out"].astype(BF16) + behind]
    mix1_sent = gather_start("gather_mix1_start", mix1w, own_block_only("land_mix1", mix1w))
    behind = (mix1_sent[4][0, 0] * 0.0).astype(BF16)
    ffn1 = [w["ffn_w_up"][1:2].astype(BF16) + behind, w["ffn_w_down"][1:2].astype(BF16) + behind]
    ffn1_sent = gather_start("gather_ffn1_start", ffn1, own_block_only("land_ffn1", ffn1))
    started = ffn1_sent[4]
    rg_cw = _cols_from_gather(g["rg_conv_w"])[0]
    n_layer = ffn_conv_w.shape[0]
    cw5 = g["ffn_conv_w"].reshape(2, N_DEV // 2, n_layer, FFN_CONV, FF_BLK)
    cb5 = ffn_conv_b.reshape(n_layer, 2, N_DEV // 2, 1, FF_BLK)
    gain = lambda l, k: gains[l, k:k + 1, :]
    wa_bd, wx_bd = _block_diag(rg_wa[0]), _block_diag(rg_wx[0])
    fbias = jnp.pad(fox_f_bias, ((0, 0), (0, LANES - C_HEADS)))

    x0 = x.reshape(m, D_MODEL)
    tgt = loss_target.reshape(m, D_MODEL)

    (h0,) = tile_fwd("l0_prenorm", fn_prenorm_after, m=m, tm=tm, nj=1, rows=[Row(x0)], pars=[Par(gain(0, 0)), Par(started)],
                     outs=[Out(D_MODEL, BF16)])
    z0 = mm("l0_in", "nn",
            Blk(h0, (tmm, D_MODEL), lambda i, j, k: (i, 0)),
            Blk(w_in_e, (2, None, D_MODEL, 384), lambda i, j, k: (j, 0, 0, 0)),
            Blk((m, 3072), (tmm, 768), lambda i, j, k: (i, j)), F32, (nm, N_DEV // 2, 1), b_join=True)
    oa, sprev = hgrn_fwd("l0_hgrn", z0, hgrn_lb_logits, hgrn_norm, n_batch=n_batch, seq=seq)
    rg_rows = lambda: [Row(z0, LANES, 16), Row(z0, LANES, 20)]
    rg_pars = lambda: [Par(rg_cw, "col", LANES), Par(rg_conv_b, "col", LANES), Par(wa_bd, "row", LANES), Par(rg_ba, "col", LANES),
                       Par(wx_bd, "row", LANES), Par(rg_bx, "col", LANES), Par(rg_lambda, "col", LANES)]
    (ob,) = tile_fwd("l0_rglru", fn_rglru, m=m, tm=seq, nj=B_WIDTH // LANES, rows=rg_rows(), pars=rg_pars(),
                     outs=[Out(B_WIDTH, BF16, LANES)])
    mixcat0 = jnp.concatenate([oa, ob], axis=-1)
    (g_out_e,) = gather_wait("gather_out0_wait", out0_sent[0], out0_sent[1], out0_sent[2], out0_sent[3], mixcat0)
    w_out_e = g_out_e.reshape(D_MODEL, D_MODEL)
    mix0 = mm2d("l0_out", "nn", mixcat0, w_out_e)
    x1, h1 = tile_fwd("l0_postnorm", fn_addnorm2, m=m, tm=tm, nj=1, rows=[Row(x0), Row(mix0)], pars=[Par(gain(0, 1)), Par(gain(0, 2))],
                      outs=[Out(D_MODEL, F32), Out(D_MODEL, BF16)])
    w_up_g0, w_down_g0 = gather_wait("gather_ffn0_wait", ffn0_sent[0], ffn0_sent[1], ffn0_sent[2], ffn0_sent[3], h1)
    hid0, act0, f0 = _ffn_forward("l0_ffn", 0, h1, w_up_g0, cw5, cb5, w_down_g0, m, seq)
    x2, h2 = tile_fwd("l0_ffnnorm", fn_addnorm2, m=m, tm=tm, nj=1, rows=[Row(x1), Row(f0)], pars=[Par(gain(0, 3)), Par(gain(1, 0))],
                      outs=[Out(D_MODEL, F32), Out(D_MODEL, BF16)])

    g_in_o, g_out_o = gather_wait("gather_mix1_wait", mix1_sent[0], mix1_sent[1], mix1_sent[2], mix1_sent[3], h2)
    w_in_o_t = jnp.pad(g_in_o.reshape(3088, D_MODEL), ((0, 3200 - 3088), (0, 0)))
    w_out_o = g_out_o.reshape(D_MODEL, D_MODEL)
    z1 = mm2d("l1_in", "nt", h2, w_in_o_t)
    (cgate,) = tile_fwd("l1_gate", fn_fox_gate, m=m, tm=seq, nj=1, rows=[Row(z1, LANES, 3072 // LANES)], pars=[Par(fbias)],
                        outs=[Out(LANES, F32)])
    place, ones_q, ones_k = term_placement()
    qterm, kterm = tile_fwd("l1_terms", fn_fox_terms, m=m, tm=tm, nj=1, rows=[Row(cgate)],
                            pars=[Par(place), Par(ones_q), Par(ones_k)], outs=[Out(TERM_W, BF16), Out(TERM_W, BF16)])
    oc, lse = fox_pair_fwd("l1_attn", z1, qterm, kterm, n_batch=n_batch, seq=seq)
    blk_b = min(ATT_BLK, seq)
    lse = lse.reshape(n_batch, N_PAIR, -1, 2, lse.shape[-1] // blk_b, blk_b).swapaxes(3, 4).reshape(n_batch, N_PAIR, seq // blk_b, 2, 1, blk_b)
    mix1 = mm2d("l1_out", "nn", oc, w_out_o)
    x3, h3 = tile_fwd("l1_postnorm", fn_addnorm2, m=m, tm=tm, nj=1, rows=[Row(x2), Row(mix1)], pars=[Par(gain(1, 1)), Par(gain(1, 2))],
                      outs=[Out(D_MODEL, F32), Out(D_MODEL, BF16)])
    w_up_g1, w_down_g1 = gather_wait("gather_ffn1_wait", ffn1_sent[0], ffn1_sent[1], ffn1_sent[2], ffn1_sent[3], h3)
    hid1, act1, f1 = _ffn_forward("l1_ffn", 1, h3, w_up_g1, cw5, cb5, w_down_g1, m, seq)
    dy, df1, loss_part, d_g13 = loss_head("loss", x3, f1, tgt, gain(1, 3), m=m, tm=tm)
    dh3, d_wup1, d_cw1, d_cb1, d_wdown1 = _ffn_backward("l1_ffn", 1, df1, h3, hid1, act1, w_up_g1, cw5, cb5, w_down_g1, m, seq)
    dx2, dmix1, d_g11, d_g12 = tile_bwd("l1_dpostnorm", fn_addnorm2, m=m, tm=tm, nj=1, rows=[Row(x2), Row(mix1)],
                                        pars=[Par(gain(1, 1)), Par(gain(1, 2))], cts=[Row(dy), Row(dh3)],
                                        drows=[Out(D_MODEL, F32), Out(D_MODEL, BF16)])
    doc = mm2d("l1_doc", "nt", dmix1, w_out_o, BF16)
    d_wout_o = mm2d("l1_dwout", "tn", oc, dmix1)
    dq, dk, dv, dc = fox_pair_bwd("l1_dattn", z1, qterm, kterm, oc, doc, lse, n_batch=n_batch, seq=seq)
    dzf, d_fbias = tile_bwd("l1_dgate", fn_fox_gate, m=m, tm=seq, nj=1, rows=[Row(z1, LANES, 3072 // LANES)], pars=[Par(fbias)],
                            cts=[Row(dc)], drows=[Out(LANES, BF16)])
    dz1 = jnp.concatenate([dq, dk, dv, dzf], axis=-1)
    dh2 = mm2d("l1_dh", "nn", dz1, w_in_o_t, BF16)
    d_win_o_t = mm2d("l1_dwin", "tn", dz1, h2, BF16)

    send1 = [d_win_o_t[:3088].reshape(N_DEV, 1, 3088 // N_DEV, D_MODEL),
             d_wout_o.reshape(N_DEV, 1, D_MODEL // N_DEV, D_MODEL).astype(BF16), d_wup1, d_wdown1]
    sent1 = _split_exchange("exchange_l1_start", send1, own_slots_only("land_l1", send1), None, None)

    dx1, df0, d_g03, d_g10 = tile_bwd("l0_dffnnorm", fn_addnorm2_after, m=m, tm=tm, nj=1, rows=[Row(x1), Row(f0)],
                                      pars=[Par(gain(0, 3)), Par(gain(1, 0)), Par(sent1[4])], cts=[Row(dx2), Row(dh2)],
                                      drows=[Out(D_MODEL, F32), Out(D_MODEL, BF16)])[:4]
    dh1, d_wup0, d_cw0, d_cb0, d_wdown0 = _ffn_backward("l0_ffn", 0, df0, h1, hid0, act0, w_up_g0, cw5, cb5, w_down_g0, m, seq)
    send0 = [d_wup0, d_wdown0]
    sent0 = _split_exchange("exchange_ffn0_start", send0, own_slots_only("land_dffn0", send0), None, None)
    dx0a, dmix0, d_g01, d_g02 = tile_bwd("l0_dpostnorm", fn_addnorm2_after, m=m, tm=tm, nj=1, rows=[Row(x0), Row(mix0)],
                                         pars=[Par(gain(0, 1)), Par(gain(0, 2)), Par(sent0[4])], cts=[Row(dx1), Row(dh1)],
                                         drows=[Out(D_MODEL, F32), Out(D_MODEL, BF16)])[:4]
    dmixcat0 = mm2d("l0_dmixcat", "nt", dmix0, w_out_e, BF16)
    d_wout_e = mm2d("l0_dwout", "tn", mixcat0, dmix0)
    dzq, dzf0, dzv, dzg, d_lb, d_hnorm = hgrn_bwd("l0_dhgrn", z0, sprev, hgrn_lb_logits, hgrn_norm, dmixcat0, n_batch=n_batch, seq=seq)
    dzx, dzy, d_rcw, d_rcb, d_wa, d_ba, d_wx, d_bx, d_lam = tile_bwd(
        "l0_drglru", fn_rglru, m=m, tm=seq, nj=B_WIDTH // LANES, rows=rg_rows(), pars=rg_pars(),
        cts=[Row(dmixcat0, LANES, A_WIDTH // LANES)], drows=[Out(B_WIDTH, BF16, LANES), Out(B_WIDTH, BF16, LANES)])
    dz0 = jnp.concatenate([dzq, dzf0, dzv, dzg, dzx, dzy], axis=-1)
    d_win_e = mm("l0_dwin", "tn",
                 Blk(h0, (tmm, D_MODEL), lambda i, j, k: (k, 0)),
                 Blk(dz0, (tmm, 768), lambda i, j, k: (k, j)),
                 Blk(w_in_e.shape, (2, None, D_MODEL, 384), lambda i, j, k: (j, 0, 0, 0)), BF16, (1, N_DEV // 2, nm), o_split=True)
    send_e = [d_win_e, d_wout_e.reshape(N_DEV, 1, D_MODEL // N_DEV, D_MODEL).astype(BF16)]
    sent_e = _split_exchange("exchange_even_start", send_e, own_slots_only("land_even", send_e), None, None)
    d_ffn_cb = jnp.stack([d_cb0, d_cb1]).reshape(n_layer, 2 * D_FF)
    rep = {"hgrn_lb_logits": d_lb, "hgrn_norm": d_hnorm, "rg_conv_b": d_rcb, "rg_wa": _block_diag_grad(d_wa)[None], "rg_ba": d_ba,
           "rg_wx": _block_diag_grad(d_wx)[None], "rg_bx": d_bx, "rg_lambda": d_lam, "fox_f_bias": d_fbias[:, :C_HEADS],
           "ffn_conv_b": d_ffn_cb}
    rep_blocks = [rep[n] for n in REPLICATED] + [loss_part]
    rep_sent = gather_start("gather_partials_start", rep_blocks, own_block_only("land_partials", rep_blocks))
    dh0 = mm("l0_dh", "nt",
             Blk(dz0, (tmm, 768), lambda i, j, k: (i, k)),
             Blk(w_in_e, (2, None, D_MODEL, 384), lambda i, j, k: (k, 0, 0, 0)),
             Blk((m, D_MODEL), (tmm, D_MODEL), lambda i, j, k: (i, 0)), BF16, (nm, 1, N_DEV // 2), after=sent_e[4] + rep_sent[4],
             b_join=True)
    dx0, d_g00 = tile_bwd("l0_dprenorm", fn_input_norm, m=m, tm=tm, nj=1, rows=[Row(x0)], pars=[Par(gain(0, 0))],
                          cts=[Row(dx0a), Row(dh0)], drows=[Out(D_MODEL, F32)])

    d_gains = jnp.stack([jnp.concatenate([d_g00, d_g01, d_g02, d_g03], axis=0), jnp.concatenate([d_g10, d_g11, d_g12, d_g13], axis=0)])
    d_ffn_cw = jnp.stack([d_cw0, d_cw1], axis=2).reshape(N_DEV, n_layer, FFN_CONV, FF_BLK)
    r_in_o, r_out_o, r_up1, r_down1 = _split_exchange("exchange_l1_wait", sent1[2], sent1[3], sent1[:2], dx0)
    r_up0, r_down0 = _split_exchange("exchange_ffn0_wait", sent0[2], sent0[3], sent0[:2], dx0)
    r_in_e, r_out_e = _split_exchange("exchange_even_wait", sent_e[2], sent_e[3], sent_e[:2], dx0)
    recv, res = {}, {}
    flipped = ("odd_w_in", "ffn_w_up")
    view = lambda n, t: jnp.swapaxes(t, 1, 2) if n in flipped else t
    for n, r in (("even_w_in", r_in_e), ("even_w_out", r_out_e), ("odd_w_in", r_in_o), ("odd_w_out", r_out_o)):
        res[n] = [view(n, t) for t in adam_tiled("adam_" + n, r, view(n, w[n]), view(n, mom[n]), view(n, var[n]))]
    for n, parts_l in (("ffn_w_up", (r_up0, r_up1)), ("ffn_w_down", (r_down0, r_down1))):
        wmv = (view(n, w[n]), view(n, mom[n]), view(n, var[n]))
        first_layer = adam_tiled(f"adam_{n}_0", parts_l[0], *wmv, layer=0)
        res[n] = [view(n, t) for t in adam_tiled(f"adam_{n}_1", parts_l[1], *wmv, layer=1, prev=first_layer)]
    small_send = [_cols_to_blocks(d_gains), _cols_to_blocks(d_rcw[None]), d_ffn_cw]
    recv.update(zip(SMALL_SHARDED, all_to_all("exchange_small", small_send)))

    parts = gather_wait("gather_partials_wait", rep_sent[0], rep_sent[1], rep_sent[2], rep_sent[3], dx0)
    for n, p in zip(REPLICATED, parts):
        recv[n] = p
    small = SMALL_SHARDED + REPLICATED
    small_res, (loss_sum,) = adam_small("adam_small", [(recv[n], w[n], mom[n], var[n]) for n in small], [parts[-1]])
    res.update(dict(zip(small, small_res)))

    out = [loss_sum[0, 0], dx0.reshape(x.shape)]
    for k in range(4):
        out += [res[n][k] for n in NAMES]
    return tuple(out)
```

```python
import functools

import jax
import jax.numpy as jnp
from jax import lax
from jax.experimental import pallas as pl
from jax.experimental.pallas import tpu as pltpu

F32 = jnp.float32
BF16 = jnp.bfloat16

D_MODEL = 1024
A_HEADS = 4
A_WIDTH = 512
HGRN_CHUNK = 64
HGRN_SEG = 2048
B_WIDTH = 512
B_BLOCKS = 8
B_BLOCK_DIM = 64
B_CONV = 4
RG_C = 8.0
C_HEADS = 16
C_HEAD_DIM = 64
D_FF = 2816
FFN_CONV = 3
EPS = 1e-6
LANES = 128
HALO = 16
N_DEV = 8
FF_BLK = 2 * D_FF // N_DEV
MESH = pl.DeviceIdType.MESH
NEG = -1e30
VMEM_LIMIT = 56 * 1024 * 1024

ADAM_LR = 0.001
ADAM_B1 = 0.9
ADAM_B2 = 0.999
ADAM_EPS = 1e-08
ADAM_WD = 0.01
ADAM_STEP = 10


def _dg(a, b, pat):
    nb = a.ndim - 2
    batch = (tuple(range(nb)), tuple(range(nb)))
    ca = a.ndim - 1 if pat[0] == "n" else a.ndim - 2
    cb = b.ndim - 2 if pat[1] == "n" else b.ndim - 1
    return lax.dot_general(a.astype(BF16), b.astype(BF16), (((ca,), (cb,)), batch), preferred_element_type=F32)


@functools.partial(jax.custom_vjp, nondiff_argnums=(2,))
def bdot(a, b, pat):
    return _dg(a, b, pat)


def _bdot_fwd(a, b, pat):
    return _dg(a, b, pat), (a, b)


def _bdot_bwd(pat, res, g):
    a, b = res
    if pat == "nn":
        return _dg(g, b, "nt"), _dg(a, g, "tn")
    if pat == "nt":
        return _dg(g, b, "nn"), _dg(g, a, "tn")
    return _dg(b, g, "nt"), _dg(a, g, "nn")


bdot.defvjp(_bdot_fwd, _bdot_bwd)


def _shift_raw(x, s, up, fill):
    if s == 0:
        return x
    n = x.shape[0]
    r = pltpu.roll(x, (n - s) if up else s, 0)
    idx = lax.broadcasted_iota(jnp.int32, x.shape, 0)
    mask = (idx >= n - s) if up else (idx < s)
    return jnp.where(mask, jnp.asarray(fill, x.dtype), r)


@functools.partial(jax.custom_vjp, nondiff_argnums=(1,))
def shift_down(x, s):
    return _shift_raw(x, s, False, 0.0)


def _shift_down_fwd(x, s):
    return _shift_raw(x, s, False, 0.0), None


def _shift_down_bwd(s, _, g):
    return (_shift_raw(g, s, True, 0.0),)


shift_down.defvjp(_shift_down_fwd, _shift_down_bwd)


def _scan_impl(a, u, up):
    n = a.shape[0]
    s = 1
    while s < n:
        u = a * _shift_raw(u, s, up, 0.0) + u
        if 2 * s < n:
            a = a * _shift_raw(a, s, up, 1.0)
        s *= 2
    return u


@jax.custom_vjp
def lin_scan(a, u):
    return _scan_impl(a, u, False)


def _lin_scan_fwd(a, u):
    h = _scan_impl(a, u, False)
    return h, (a, h)


def _lin_scan_bwd(res, g):
    a, h = res
    gh = _scan_impl(_shift_raw(a, 1, True, 0.0), g, True)
    return gh * _shift_raw(h, 1, False, 0.0), gh


lin_scan.defvjp(_lin_scan_fwd, _lin_scan_bwd)


def _cumsum_impl(x, up, period):
    n = x.shape[0]
    span = n if period is None else period
    idx = lax.broadcasted_iota(jnp.int32, x.shape, 0)
    pos = idx if period is None else idx % period
    s = 1
    while s < span:
        sh = _shift_raw(x, s, up, 0.0)
        if period is not None:
            keep = (pos < period - s) if up else (pos >= s)
            sh = jnp.where(keep, sh, 0.0)
        x = x + sh
        s *= 2
    return x


@functools.partial(jax.custom_vjp, nondiff_argnums=(1,))
def cumsum_rows(x, period):
    return _cumsum_impl(x, False, period)


def _cumsum_fwd(x, period):
    return _cumsum_impl(x, False, period), None


def _cumsum_bwd(period, _, g):
    return (_cumsum_impl(g, True, period),)


cumsum_rows.defvjp(_cumsum_fwd, _cumsum_bwd)


def _sigmoid(x):
    return jax.nn.sigmoid(x)


def _expm1(x):
    return jnp.tanh(0.5 * x) * (jnp.exp(x) + 1.0)


def _softplus(x):
    return jnp.maximum(x, 0.0) + jnp.log(1.0 + jnp.exp(-jnp.abs(x)))


def _rms(x, g):
    return x * lax.rsqrt(jnp.mean(x * x, axis=-1, keepdims=True) + EPS) * g


def fn_prenorm(x, g):
    return (_rms(x, g).astype(BF16),)


def fn_prenorm_after(x, g, _token):
    return fn_prenorm(x, g)


def fn_addnorm2(x, y, g_post, g_pre):
    x1 = x + _rms(y, g_post)
    return x1, _rms(x1, g_pre).astype(BF16)


def fn_addnorm2_after(x, y, g_post, g_pre, _token):
    return fn_addnorm2(x, y, g_post, g_pre)


def fn_input_norm(x, g):
    return x, _rms(x, g).astype(BF16)


def _causal_conv(x, w, b, taps):
    c = b
    for k in range(taps):
        c = c + w[k:k + 1, :] * shift_down(x, taps - 1 - k)
    return c


def fn_rglru(xb, yb, cw, cb, wa, ba, wx, bx, lam):
    xf = _causal_conv(xb, cw, cb, B_CONV)
    r = _sigmoid(bdot(xf, wa, "nn") + ba)
    i = _sigmoid(bdot(xf, wx, "nn") + bx)
    log_a = -RG_C * r * _softplus(-lam)
    a = jnp.exp(log_a)
    u = jnp.sqrt(-_expm1(2.0 * log_a)) * (i * xf)
    h = lin_scan(a, u)
    return ((h * jax.nn.gelu(yb)).astype(BF16),)


def fn_fox_gate(zf, bias):
    return (cumsum_rows(jax.nn.log_sigmoid(zf + bias), None),)


def fn_hgrn_seg(q, fl, v, g, st, logits, hn):
    rows = q.shape[0]
    nc = rows // HGRN_CHUNK
    l0, l1, l2 = logits[0:1, :], logits[1:2, :], logits[2:3, :]
    mx = jnp.maximum(jnp.maximum(l0, l1), l2)
    e0, e1, e2 = jnp.exp(l0 - mx), jnp.exp(l1 - mx), jnp.exp(l2 - mx)
    lb = e0 / (e0 + e1 + e2)
    forget = lb + (1.0 - lb) * _sigmoid(fl)
    qs = q * _sigmoid(q)
    kk = 1.0 - forget
    logf = jnp.log(forget)
    bcum = cumsum_rows(logf, HGRN_CHUNK)
    c3 = lambda t: t.reshape(nc, HGRN_CHUNK, 128)
    b_last = jnp.sum(c3(logf), axis=1, keepdims=True)
    bcum3 = c3(bcum)
    q_dec = c3(qs) * jnp.exp(bcum3)
    k_dec = c3(kk) * jnp.exp(-bcum3)
    k_upd = c3(kk) * jnp.exp(b_last - bcum3)
    v3 = c3(v)
    scores = bdot(q_dec, k_dec, "nt")
    ri = lax.broadcasted_iota(jnp.int32, scores.shape, 1)
    ci = lax.broadcasted_iota(jnp.int32, scores.shape, 2)
    scores = jnp.where(ri >= ci, scores, 0.0)
    o = bdot(scores, v3, "nn")
    upd_t = bdot(v3, k_upd, "tn")
    dec = jnp.exp(b_last)
    prev = []
    for n in range(nc):
        prev.append(st)
        st = st * dec[n] + upd_t[n]
    o = o + bdot(q_dec, jnp.stack(prev), "nt")
    o = o.reshape(rows, 128)
    o = o * lax.rsqrt(jnp.mean(o * o, axis=-1, keepdims=True) + EPS) * hn
    return (o * _sigmoid(g)).astype(BF16), st


def _ffn_conv(xg, xv, cw, cb):
    cg = _causal_conv(xg, cw[0], cb[0], FFN_CONV)[HALO:]
    cv = _causal_conv(xv, cw[1], cb[1], FFN_CONV)[HALO:]
    return cg, cv


def _ffn_gate(cg, cv):
    return jax.nn.gelu(cg) * cv


class Row:
    def __init__(self, arr, cb=None, off=0):
        self.arr, self.cb, self.off = arr, cb, off

    def spec(self, tm):
        if self.cb is None:
            return pl.BlockSpec((tm, self.arr.shape[1]), lambda j, i: (i, 0))
        off = self.off
        return pl.BlockSpec((tm, self.cb), lambda j, i: (i, j + off))


class Par:
    def __init__(self, arr, kind="full", bs=None):
        self.arr, self.kind, self.bs = arr, kind, bs

    def block(self):
        if self.kind == "full":
            return self.arr.shape
        if self.kind == "col":
            return (self.arr.shape[0], self.bs)
        return (self.bs, self.arr.shape[1])

    def spec(self):
        if self.kind == "full":
            return pl.BlockSpec(self.block(), lambda j, i: (0, 0))
        if self.kind == "col":
            return pl.BlockSpec(self.block(), lambda j, i: (0, j))
        return pl.BlockSpec(self.block(), lambda j, i: (j, 0))


class Out:
    def __init__(self, width, dtype, cb=None, off=0):
        self.width, self.dtype, self.cb, self.off = width, dtype, cb, off

    def spec(self, tm):
        if self.cb is None:
            return pl.BlockSpec((tm, self.width), lambda j, i: (i, 0))
        off = self.off
        return pl.BlockSpec((tm, self.cb), lambda j, i: (i, j + off))


def _params(sem):
    return pltpu.CompilerParams(dimension_semantics=sem, vmem_limit_bytes=VMEM_LIMIT)


def tile_fwd(name, fn, *, m, tm, nj, rows, pars, outs, n_acc=0):
    n_r, n_p, n_o = len(rows), len(pars), len(outs)

    def body(*refs):
        ins = [r[...] for r in refs[:n_r + n_p]]
        res = fn(*ins)
        o_refs = refs[n_r + n_p:]
        for k in range(n_o):
            o_refs[k][...] = res[k].astype(o_refs[k].dtype)
        first = jnp.logical_and(pl.program_id(0) == 0, pl.program_id(1) == 0)
        for k in range(n_acc):
            ref = o_refs[n_o + k]

            @pl.when(first)
            def _():
                ref[...] = jnp.zeros_like(ref)

            ref[...] += res[n_o + k]

    out_shape = [jax.ShapeDtypeStruct((m, o.width), o.dtype) for o in outs]
    out_specs = [o.spec(tm) for o in outs]
    for _ in range(n_acc):
        out_shape.append(jax.ShapeDtypeStruct((1, LANES), F32))
        out_specs.append(pl.BlockSpec((1, LANES), lambda j, i: (0, 0)))
    sem = ("arbitrary", "arbitrary") if n_acc else ("parallel", "parallel")
    return pl.pallas_call(
        body, grid=(nj, m // tm), name=name,
        in_specs=[r.spec(tm) for r in rows] + [p.spec() for p in pars],
        out_specs=out_specs, out_shape=out_shape, compiler_params=_params(sem),
    )(*[r.arr for r in rows], *[p.arr for p in pars])


def tile_bwd(name, fn, *, m, tm, nj, rows, pars, cts, drows):
    n_r, n_p, n_c = len(rows), len(pars), len(cts)
    want = [k for k in range(n_r) if drows[k] is not None]

    def body(*refs):
        ins = [r[...] for r in refs[:n_r + n_p]]
        ct = [r[...] for r in refs[n_r + n_p:n_r + n_p + n_c]]
        o_refs = refs[n_r + n_p + n_c:]
        res, vjp = jax.vjp(fn, *ins)
        grads = vjp(tuple(c.astype(r.dtype) for c, r in zip(ct, res)))
        for pos, k in enumerate(want):
            o_refs[pos][...] = grads[k].astype(o_refs[pos].dtype)
        for k in range(n_p):
            ref = o_refs[len(want) + k]
            first = pl.program_id(1) == 0
            if pars[k].kind == "full":
                first = jnp.logical_and(first, pl.program_id(0) == 0)

            @pl.when(first)
            def _():
                ref[...] = jnp.zeros_like(ref)

            ref[...] += grads[n_r + k].astype(F32)

    out_shape = [jax.ShapeDtypeStruct((m, drows[k].width), drows[k].dtype) for k in want]
    out_specs = [drows[k].spec(tm) for k in want]
    for p in pars:
        out_shape.append(jax.ShapeDtypeStruct(p.arr.shape, F32))
        out_specs.append(p.spec())
    return pl.pallas_call(
        body, grid=(nj, m // tm), name=name,
        in_specs=[r.spec(tm) for r in rows] + [p.spec() for p in pars] + [c.spec(tm) for c in cts],
        out_specs=out_specs, out_shape=out_shape, compiler_params=_params(("arbitrary", "arbitrary")),
    )(*[r.arr for r in rows], *[p.arr for p in pars], *[c.arr for c in cts])


def loss_head(name, x, y, tgt, g, *, m, tm):
    def body(x_ref, y_ref, t_ref, g_ref, dout_ref, dy_ref, loss_ref, dg_ref):
        normed, vjp = jax.vjp(_rms, y_ref[...], g_ref[...])
        err = x_ref[...] + normed - t_ref[...]
        dout = err * (1.0 / D_MODEL)
        dy, dg = vjp(dout)
        dout_ref[...] = dout
        dy_ref[...] = dy.astype(dy_ref.dtype)

        @pl.when(pl.program_id(0) == 0)
        def _():
            loss_ref[...] = jnp.zeros_like(loss_ref)
            dg_ref[...] = jnp.zeros_like(dg_ref)

        loss_ref[...] += 0.5 * jnp.sum(jnp.mean(err * err, axis=-1, keepdims=True), axis=0, keepdims=True)
        dg_ref[...] += dg

    row = pl.BlockSpec((tm, D_MODEL), lambda i: (i, 0))
    whole = lambda w: pl.BlockSpec((1, w), lambda i: (0, 0))
    return pl.pallas_call(
        body, grid=(m // tm,), name=name, in_specs=[row, row, row, whole(D_MODEL)],
        out_specs=[row, row, whole(LANES), whole(D_MODEL)],
        out_shape=[jax.ShapeDtypeStruct((m, D_MODEL), F32), jax.ShapeDtypeStruct((m, D_MODEL), BF16),
                   jax.ShapeDtypeStruct((1, LANES), F32), jax.ShapeDtypeStruct((1, D_MODEL), F32)],
        compiler_params=_params(("arbitrary",)),
    )(x, y, tgt, g)


class Blk:
    def __init__(self, arr, block, index):
        self.arr, self.block, self.index = arr, block, index

    def spec(self):
        return pl.BlockSpec(self.block, self.index)


def _flat2(v):
    return v if v.ndim == 2 else v.reshape(-1, v.shape[-1])


def mm(name, pat, a, b, o, out_dtype, grid, after=None, b_join=False, o_split=False):
    nk = grid[2]
    o_shape = o.arr

    def put(o_ref, r):
        if o_split:
            half = r.shape[1] // 2
            o_ref[0] = r[:, :half].astype(out_dtype)
            o_ref[1] = r[:, half:].astype(out_dtype)
        else:
            o_ref[...] = r.astype(out_dtype).reshape(o_ref.shape)

    def body(*refs):
        a_ref, b_ref = refs[0], refs[1]
        o_ref = refs[3] if after is not None else refs[2]
        bv = jnp.concatenate([b_ref[0], b_ref[1]], axis=1) if b_join else _flat2(b_ref[...])
        r = _dg(_flat2(a_ref[...]), bv, pat)
        if nk == 1:
            put(o_ref, r)
            return
        acc_ref = refs[-1]
        kk = pl.program_id(2)

        @pl.when(kk == 0)
        def _():
            acc_ref[...] = r

        @pl.when(kk > 0)
        def _():
            acc_ref[...] += r

        @pl.when(kk == nk - 1)
        def _():
            put(o_ref, acc_ref[...])

    ob = [d for d in o.block if d is not None]
    if o_split:
        acc_shape = (ob[1], 2 * ob[2])
    else:
        acc_shape = (ob[0], ob[1]) if len(ob) == 2 else (ob[0] * ob[1], ob[2])
    in_specs = [a.spec(), b.spec()]
    args = [a.arr, b.arr]
    if after is not None:
        in_specs.append(pl.BlockSpec(memory_space=pl.ANY))
        args.append(after)
    return pl.pallas_call(
        body, grid=grid, name=name, in_specs=in_specs, out_specs=o.spec(),
        out_shape=jax.ShapeDtypeStruct(o_shape, out_dtype),
        scratch_shapes=[pltpu.VMEM(acc_shape, F32)] if nk > 1 else [],
        compiler_params=_params(("parallel", "parallel", "arbitrary")),
    )(*args)


def _div_tile(n, cap):
    if n <= cap:
        return n
    best = 128
    for t in range(128, cap + 1, 128):
        if n % t == 0:
            best = t
    return best


def mm2d(name, pat, a, b, out_dtype=F32):
    if pat == "tn":
        k, m = a.shape
    else:
        m, k = a.shape
    n = b.shape[0] if pat == "nt" else b.shape[1]
    tm, tn, tk = _div_tile(m, 1024), _div_tile(n, 1024), _div_tile(k, 1024)
    a_blk = Blk(a, (tk, tm), lambda i, j, kk: (kk, i)) if pat == "tn" else Blk(a, (tm, tk), lambda i, j, kk: (i, kk))
    b_blk = Blk(b, (tn, tk), lambda i, j, kk: (j, kk)) if pat == "nt" else Blk(b, (tk, tn), lambda i, j, kk: (kk, j))
    o_blk = Blk((m, n), (tm, tn), lambda i, j, kk: (i, j))
    return mm(name, pat, a_blk, b_blk, o_blk, out_dtype, (m // tm, n // tn, k // tk))


def hgrn_fwd(name, z, logits, hnorm, *, n_batch, seq):
    m = n_batch * seq
    ts = min(HGRN_SEG, seq)
    n_seg = seq // ts

    def body(q_ref, f_ref, v_ref, g_ref, lg_ref, hn_ref, o_ref, sp_ref, st_ref):
        s = pl.program_id(2)

        @pl.when(s == 0)
        def _():
            st_ref[...] = jnp.zeros_like(st_ref)

        st = st_ref[...]
        sp_ref[...] = st
        o, st_new = fn_hgrn_seg(q_ref[...], f_ref[...], v_ref[...], g_ref[...], st, lg_ref[...], hn_ref[...])
        o_ref[...] = o
        st_ref[...] = st_new

    part = lambda p: pl.BlockSpec((ts, 128), lambda h, b, s: (b * n_seg + s, 4 * p + h))
    return pl.pallas_call(
        body, grid=(A_HEADS, n_batch, n_seg), name=name,
        in_specs=[part(0), part(1), part(2), part(3),
                  pl.BlockSpec((3, 128), lambda h, b, s: (0, h)),
                  pl.BlockSpec((1, 128), lambda h, b, s: (0, h))],
        out_specs=[pl.BlockSpec((ts, 128), lambda h, b, s: (b * n_seg + s, h)),
                   pl.BlockSpec((128, 128), lambda h, b, s: ((b * n_seg + s) * A_HEADS + h, 0))],
        out_shape=[jax.ShapeDtypeStruct((m, A_WIDTH), BF16),
                   jax.ShapeDtypeStruct((n_batch * n_seg * A_HEADS * 128, 128), F32)],
        scratch_shapes=[pltpu.VMEM((128, 128), F32)],
        compiler_params=_params(("arbitrary", "arbitrary", "arbitrary")),
    )(z, z, z, z, logits, hnorm)


def hgrn_bwd(name, z, sprev, logits, hnorm, do, *, n_batch, seq):
    m = n_batch * seq
    ts = min(HGRN_SEG, seq)
    n_seg = seq // ts

    def body(q_ref, f_ref, v_ref, g_ref, sp_ref, lg_ref, hn_ref, do_ref, dq_ref, df_ref, dv_ref, dg_ref, dlg_ref, dhn_ref, dst_ref):
        s = pl.program_id(2)

        @pl.when(s == 0)
        def _():
            dst_ref[...] = jnp.zeros_like(dst_ref)

        res, vjp = jax.vjp(fn_hgrn_seg, q_ref[...], f_ref[...], v_ref[...], g_ref[...], sp_ref[...], lg_ref[...], hn_ref[...])
        dq, df, dv, dg, dst, dlg, dhn = vjp((do_ref[...].astype(res[0].dtype), dst_ref[...]))
        dq_ref[...] = dq.astype(dq_ref.dtype)
        df_ref[...] = df.astype(df_ref.dtype)
        dv_ref[...] = dv.astype(dv_ref.dtype)
        dg_ref[...] = dg.astype(dg_ref.dtype)
        dst_ref[...] = dst
        first = jnp.logical_and(pl.program_id(1) == 0, s == 0)

        @pl.when(first)
        def _():
            dlg_ref[...] = jnp.zeros_like(dlg_ref)
            dhn_ref[...] = jnp.zeros_like(dhn_ref)

        dlg_ref[...] += dlg
        dhn_ref[...] += dhn

    rev = lambda b, s: b * n_seg + (n_seg - 1 - s)
    part = lambda p: pl.BlockSpec((ts, 128), lambda h, b, s: (rev(b, s), 4 * p + h))
    head = pl.BlockSpec((ts, 128), lambda h, b, s: (rev(b, s), h))
    dpart = jax.ShapeDtypeStruct((m, A_WIDTH), BF16)
    return pl.pallas_call(
        body, grid=(A_HEADS, n_batch, n_seg), name=name,
        in_specs=[part(0), part(1), part(2), part(3),
                  pl.BlockSpec((128, 128), lambda h, b, s: (rev(b, s) * A_HEADS + h, 0)),
                  pl.BlockSpec((3, 128), lambda h, b, s: (0, h)),
                  pl.BlockSpec((1, 128), lambda h, b, s: (0, h)),
                  head],
        out_specs=[head, head, head, head,
                   pl.BlockSpec((3, 128), lambda h, b, s: (0, h)),
                   pl.BlockSpec((1, 128), lambda h, b, s: (0, h))],
        out_shape=[dpart, dpart, dpart, dpart,
                   jax.ShapeDtypeStruct(logits.shape, F32),
                   jax.ShapeDtypeStruct(hnorm.shape, F32)],
        scratch_shapes=[pltpu.VMEM((128, 128), F32)],
        compiler_params=_params(("arbitrary", "arbitrary", "arbitrary")),
    )(z, z, z, z, sprev, logits, hnorm, do)


FFN_ROWS = 512
FFN_LANES = 128


def _ffn_tiles(m, seq):
    tm = min(512, seq)
    return tm, seq // tm, m // tm


def ffn_mid_fwd(name, hid, cw, cb, layer, *, m, seq):
    tm, n_t, n_i = _ffn_tiles(m, seq)
    hb = tm // HALO

    def body(x_ref, xb_ref, cw_ref, cb_ref, o_ref, c_ref):
        first = pl.program_id(1) % n_t == 0
        for l0 in range(0, FF_BLK, FFN_LANES):
            lanes = slice(l0, min(l0 + FFN_LANES, FF_BLK))
            before = jnp.where(first, 0.0, xb_ref[:, :, lanes])
            ext = jnp.concatenate([before, x_ref[:, :, lanes]], axis=1)
            cg, cv = _ffn_conv(ext[0], ext[1], cw_ref[:, :, lanes], cb_ref[:, :, lanes])
            o_ref[:, lanes] = _ffn_gate(cg, cv).astype(o_ref.dtype)
            c_ref[0, :, lanes] = cg.astype(c_ref.dtype)
            c_ref[1, :, lanes] = cv.astype(c_ref.dtype)

    return pl.pallas_call(
        body, grid=(N_DEV // 2, n_i), name=name,
        in_specs=[pl.BlockSpec((2, None, tm, FF_BLK), lambda d, i: (0, d, i, 0)),
                  pl.BlockSpec((2, None, HALO, FF_BLK), lambda d, i: (0, d, jnp.maximum(i * hb - 1, 0), 0)),
                  pl.BlockSpec((2, None, None, FFN_CONV, FF_BLK), lambda d, i: (0, d, layer, 0, 0)),
                  pl.BlockSpec((None, 2, None, 1, FF_BLK), lambda d, i: (layer, 0, d, 0, 0))],
        out_specs=[pl.BlockSpec((None, tm, FF_BLK), lambda d, i: (d, i, 0)),
                   pl.BlockSpec((2, None, tm, FF_BLK), lambda d, i: (0, d, i, 0))],
        out_shape=[jax.ShapeDtypeStruct((N_DEV // 2, m, FF_BLK), BF16),
                   jax.ShapeDtypeStruct((2, N_DEV // 2, m, FF_BLK), BF16)],
        compiler_params=_params(("parallel", "parallel")),
    )(hid, hid, cw, cb)


def ffn_mid_bwd(name, hid, conv, cw, dact, layer, *, m, seq):
    tm, n_t, n_i = _ffn_tiles(m, seq)
    hb = tm // HALO
    last_blk = m // HALO - 1

    rc = min(FFN_ROWS, tm)
    lane_chunks = [(l0, min(FFN_LANES, FF_BLK - l0)) for l0 in range(0, FF_BLK, FFN_LANES)]

    def body(x_ref, c_ref, ca_ref, cw_ref, da_ref, daa_ref, dx_ref, dcw_ref, dcb_ref, cext_ref, dext_ref):
        i = pl.program_id(1)
        last = i % n_t == n_t - 1
        cext_ref[:, :tm] = c_ref[...]
        cext_ref[:, tm:] = ca_ref[...]
        dext_ref[:tm] = da_ref[...]
        dext_ref[tm:] = jnp.where(last, jnp.zeros_like(daa_ref[...]), daa_ref[...])

        @pl.when(i == 0)
        def _():
            dcw_ref[...] = jnp.zeros_like(dcw_ref)
            dcb_ref[...] = jnp.zeros_like(dcb_ref)

        for l0, lw in lane_chunks:
            lanes = slice(l0, l0 + lw)

            def chunk(c, sums, lanes=lanes, lw=lw):
                r0 = pl.multiple_of(c * rc, rc)
                ext = pl.ds(r0, rc + HALO)
                cg, cv = cext_ref[0, ext, lanes].astype(F32), cext_ref[1, ext, lanes].astype(F32)
                _, vjp_gate = jax.vjp(_ffn_gate, cg, cv)
                dconv = vjp_gate(dext_ref[ext, lanes].astype(F32))
                out = []
                for half in range(2):
                    x = x_ref[half, pl.ds(r0, rc), lanes]
                    dx = None
                    for k in range(FFN_CONV):
                        s = FFN_CONV - 1 - k
                        dc_s = _shift_raw(dconv[half], s, True, 0.0)[:rc]
                        term = cw_ref[half, k:k + 1, lanes] * dc_s
                        dx = term if dx is None else dx + term
                        out.append(sums[len(out)] + jnp.sum(x * dc_s, axis=0, keepdims=True))
                    out.append(sums[len(out)] + jnp.sum(dconv[half][:rc], axis=0, keepdims=True))
                    dx_ref[half, pl.ds(r0, rc), lanes] = dx.astype(dx_ref.dtype)
                return tuple(out)

            zero = jnp.zeros((1, lw), F32)
            sums = lax.fori_loop(0, tm // rc, chunk, (zero,) * (2 * (FFN_CONV + 1)))
            for half in range(2):
                base = half * (FFN_CONV + 1)
                for k in range(FFN_CONV):
                    dcw_ref[half, k:k + 1, lanes] += sums[base + k]
                dcb_ref[half, :, lanes] += sums[base + FFN_CONV]

    return pl.pallas_call(
        body, grid=(N_DEV // 2, n_i), name=name,
        in_specs=[pl.BlockSpec((2, None, tm, FF_BLK), lambda d, i: (0, d, i, 0)),
                  pl.BlockSpec((2, None, tm, FF_BLK), lambda d, i: (0, d, i, 0)),
                  pl.BlockSpec((2, None, HALO, FF_BLK), lambda d, i: (0, d, jnp.minimum((i + 1) * hb, last_blk), 0)),
                  pl.BlockSpec((2, None, None, FFN_CONV, FF_BLK), lambda d, i: (0, d, layer, 0, 0)),
                  pl.BlockSpec((None, tm, FF_BLK), lambda d, i: (d, i, 0)),
                  pl.BlockSpec((None, HALO, FF_BLK), lambda d, i: (d, jnp.minimum((i + 1) * hb, last_blk), 0))],
        out_specs=[pl.BlockSpec((2, None, tm, FF_BLK), lambda d, i: (0, d, i, 0)),
                   pl.BlockSpec((2, None, FFN_CONV, FF_BLK), lambda d, i: (0, d, 0, 0)),
                   pl.BlockSpec((2, None, 1, FF_BLK), lambda d, i: (0, d, 0, 0))],
        out_shape=[jax.ShapeDtypeStruct((2, N_DEV // 2, m, FF_BLK), BF16),
                   jax.ShapeDtypeStruct((2, N_DEV // 2, FFN_CONV, FF_BLK), F32),
                   jax.ShapeDtypeStruct((2, N_DEV // 2, 1, FF_BLK), F32)],
        scratch_shapes=[pltpu.VMEM((2, tm + HALO, FF_BLK), BF16), pltpu.VMEM((tm + HALO, FF_BLK), BF16)],
        compiler_params=_params(("arbitrary", "arbitrary")),
    )(hid, conv, conv, cw, dact, dact)


ATT_BLK = 512
ATT_BLK_FWD = 1024
N_PAIR = C_HEADS // 2
TERM_W = C_HEADS * LANES


def term_placement():
    import numpy as np
    place = np.zeros((3, LANES, TERM_W), np.float32)
    ones_q = np.zeros((1, TERM_W), np.float32)
    ones_k = np.zeros((1, TERM_W), np.float32)
    for h in range(C_HEADS):
        for j in range(3):
            place[j, h, h * LANES + C_HEAD_DIM + j] = 1.0
            ones_q[0, h * LANES + C_HEAD_DIM + 3 + j] = 1.0
            ones_k[0, h * LANES + C_HEAD_DIM + j] = 1.0
    return (jnp.asarray(place.reshape(3 * LANES, TERM_W), BF16), jnp.asarray(ones_q, F32), jnp.asarray(ones_k, F32))


def fn_fox_terms(c, place, ones_q, ones_k):
    parts = _split3(c)
    placed = sum(_dg(parts[j], place[j * LANES:(j + 1) * LANES], "nn") for j in range(3))
    return (placed + ones_q).astype(BF16), (ones_k - pltpu.roll(placed, 3, 1)).astype(BF16)


def _head_tile(z, terms, e):
    lane = lax.broadcasted_iota(jnp.int32, z.shape, 1)
    base = z if e == 0 else pltpu.roll(z, C_HEAD_DIM, 1)
    return jnp.where(lane < C_HEAD_DIM, base, terms.astype(z.dtype))


def _head_only(z, e):
    lane = lax.broadcasted_iota(jnp.int32, z.shape, 1)
    mine = (lane < C_HEAD_DIM) if e == 0 else (lane >= C_HEAD_DIM)
    return jnp.where(mine, z, jnp.zeros_like(z)).astype(BF16)


def _pair_tile(a0, a1):
    lane = lax.broadcasted_iota(jnp.int32, a0.shape, 1)
    return jnp.where(lane < C_HEAD_DIM, a0, pltpu.roll(a1, C_HEAD_DIM, 1))


def _lane_col(a, k):
    lane = lax.broadcasted_iota(jnp.int32, a.shape, 1)
    return jnp.sum(jnp.where(lane == k, a, 0.0), axis=1, keepdims=True)


def _causal(s):
    key = lax.broadcasted_iota(jnp.int32, s.shape, 0)
    qry = lax.broadcasted_iota(jnp.int32, s.shape, 1)
    return qry >= key


def fox_pair_fwd(name, z, qterm, kterm, *, n_batch, seq):
    m = n_batch * seq
    blk = min(ATT_BLK_FWD, seq)
    nq = seq // blk
    dh = C_HEAD_DIM

    def body(zq_ref, zk_ref, zv_ref, qt_ref, kt_ref, o_ref, lse_ref, ka_ref, vt_ref):
        qi = pl.program_id(2)

        @pl.when(qi == 0)
        def _():
            zk = zk_ref[...]
            for e in range(2):
                ka_ref[e] = _head_tile(zk, kt_ref[:, e * LANES:(e + 1) * LANES], e).astype(BF16)
            for cb in range(nq):
                vt_ref[cb] = zv_ref[cb * blk:(cb + 1) * blk, :].T.astype(BF16)

        zq = zq_ref[...] * dh ** -0.5
        qa = [_head_tile(zq, qt_ref[:, e * LANES:(e + 1) * LANES], e).astype(BF16) for e in range(2)]

        def block(j, carry, diagonal):
            rows = pl.ds(pl.multiple_of(j * blk, blk), blk)
            out = []
            for e in range(2):
                mx, l, acc = carry[e]
                s = _dg(ka_ref[e, rows, :], qa[e], "nt")
                if diagonal:
                    s = jnp.where(_causal(s), s, NEG)
                mx_new = jnp.maximum(mx, jnp.max(s, axis=0, keepdims=True))
                p = jnp.exp(s - mx_new)
                alpha = jnp.exp(mx - mx_new)
                l = alpha * l + jnp.sum(p, axis=0, keepdims=True)
                acc = alpha * acc + _dg(vt_ref[j, e * dh:(e + 1) * dh, :], p, "nn")
                out.append((mx_new, l, acc))
            return tuple(out)

        one = (jnp.full((1, blk), NEG, F32), jnp.zeros((1, blk), F32), jnp.zeros((dh, blk), F32))
        carry = lax.fori_loop(0, qi, lambda j, cr: block(j, cr, False), (one, one))
        res = block(qi, carry, True)
        ot = jnp.concatenate([res[e][2] / res[e][1] for e in range(2)], axis=0)
        o_ref[...] = ot.T.astype(o_ref.dtype)
        for e in range(2):
            lse_ref[e] = res[e][0] + jnp.log(res[e][1])

    col = lambda part: (lambda b, g, i: (b, part * N_PAIR + g))
    return pl.pallas_call(
        body, grid=(n_batch, N_PAIR, nq), name=name,
        in_specs=[pl.BlockSpec((blk, LANES), lambda b, g, i: (b * nq + i, g)),
                  pl.BlockSpec((seq, LANES), col(1)),
                  pl.BlockSpec((seq, LANES), col(2)),
                  pl.BlockSpec((blk, 2 * LANES), lambda b, g, i: (b * nq + i, g)),
                  pl.BlockSpec((seq, 2 * LANES), lambda b, g, i: (b, g))],
        out_specs=[pl.BlockSpec((blk, LANES), lambda b, g, i: (b * nq + i, g)),
                   pl.BlockSpec((None, None, None, 2, 1, blk), lambda b, g, i: (b, g, i, 0, 0, 0))],
        out_shape=[jax.ShapeDtypeStruct((m, D_MODEL), BF16), jax.ShapeDtypeStruct((n_batch, N_PAIR, nq, 2, 1, blk), F32)],
        scratch_shapes=[pltpu.VMEM((2, seq, LANES), BF16), pltpu.VMEM((nq, LANES, blk), BF16)],
        compiler_params=_params(("parallel", "parallel", "arbitrary")),
    )(z, z, z, qterm, kterm)


def fox_pair_bwd(name, z, qterm, kterm, o, do, lse, *, n_batch, seq):
    m = n_batch * seq
    blk = min(ATT_BLK, seq)
    nq = seq // blk
    dh = C_HEAD_DIM

    def body(zq_ref, zk_ref, zv_ref, qt_ref, kt_ref, o_ref, do_ref, lse_ref, dq_ref, dk_ref, dv_ref, dc_ref,
             qa_ref, doh_ref, del_ref, dqt_ref, dk_acc, dv_acc):
        g, j = pl.program_id(1), pl.program_id(2)
        lane = lax.broadcasted_iota(jnp.int32, (blk, LANES), 1)

        @pl.when(jnp.logical_and(g == 0, j == 0))
        def _():
            dc_ref[...] = jnp.zeros_like(dc_ref)

        @pl.when(j == 0)
        def _():
            zq = zq_ref[...] * dh ** -0.5
            dov = do_ref[...]
            for e in range(2):
                qa_ref[e] = _head_tile(zq, qt_ref[:, e * LANES:(e + 1) * LANES], e).astype(BF16)
                doh_ref[e] = _head_only(dov, e)
            for cb in range(nq):
                rows = slice(cb * blk, (cb + 1) * blk)
                prod_t = (do_ref[rows, :].astype(F32) * o_ref[rows, :].astype(F32)).T
                for e in range(2):
                    del_ref[cb, e] = jnp.sum(prod_t[e * dh:(e + 1) * dh], axis=0, keepdims=True)
            dqt_ref[...] = jnp.zeros_like(dqt_ref)

        zk, zv = zk_ref[...], zv_ref[...]
        ka32 = [_head_tile(zk, kt_ref[:, e * LANES:(e + 1) * LANES], e) for e in range(2)]
        ka = [t.astype(BF16) for t in ka32]
        kat = [t.T.astype(BF16) for t in ka32]
        vh = [_head_only(zv, e) for e in range(2)]
        dk_acc[...] = jnp.zeros_like(dk_acc)
        dv_acc[...] = jnp.zeros_like(dv_acc)

        def block(i, diagonal):
            rows = pl.ds(pl.multiple_of(i * blk, blk), blk)
            for e in range(2):
                qv, dov = qa_ref[e, rows, :], doh_ref[e, rows, :]
                p = jnp.exp(_dg(ka[e], qv, "nt") - lse_ref[i, e])
                if diagonal:
                    p = jnp.where(_causal(p), p, 0.0)
                dv_acc[...] += _dg(p, dov, "nn")
                ds = p * (_dg(vh[e], dov, "nt") - del_ref[i, e])
                dk_acc[e] += _dg(ds, qv, "nn")
                dqt_ref[i, e] += _dg(kat[e], ds, "nn")

        block(j, True)

        def rest(i, carry):
            block(i, False)
            return carry

        lax.fori_loop(j + 1, nq, rest, 0)
        dk0, dk1 = dk_acc[0], dk_acc[1]
        dk_ref[...] = _pair_tile(dk0, dk1).astype(dk_ref.dtype)
        dv_ref[...] = dv_acc[...].astype(dv_ref.dtype)
        rows_j = pl.ds(pl.multiple_of(j * blk, blk), blk)
        for e, dke in enumerate((dk0, dk1)):
            dc_ref[rows_j, :] -= jnp.where(lane == 2 * g + e, _lane_col(dke, dh + 3), 0.0)

        @pl.when(j == nq - 1)
        def _():
            for i in range(nq):
                nat = [dqt_ref[i, e].T for e in range(2)]
                rows = slice(i * blk, (i + 1) * blk)
                dq_ref[rows, :] = (_pair_tile(nat[0], nat[1]) * dh ** -0.5).astype(dq_ref.dtype)
                for e in range(2):
                    dc_ref[rows, :] += jnp.where(lane == 2 * g + e, _lane_col(nat[e], dh), 0.0)

    col = lambda part: (lambda b, g, j: (b, part * N_PAIR + g))
    colj = lambda part: (lambda b, g, j: (b * nq + j, part * N_PAIR + g))
    pair = jax.ShapeDtypeStruct((m, D_MODEL), BF16)
    return pl.pallas_call(
        body, grid=(n_batch, N_PAIR, nq), name=name,
        in_specs=[pl.BlockSpec((seq, LANES), col(0)),
                  pl.BlockSpec((blk, LANES), colj(1)),
                  pl.BlockSpec((blk, LANES), colj(2)),
                  pl.BlockSpec((seq, 2 * LANES), lambda b, g, j: (b, g)),
                  pl.BlockSpec((blk, 2 * LANES), lambda b, g, j: (b * nq + j, g)),
                  pl.BlockSpec((seq, LANES), col(0)),
                  pl.BlockSpec((seq, LANES), col(0)),
                  pl.BlockSpec((None, None, nq, 2, 1, blk), lambda b, g, j: (b, g, 0, 0, 0, 0))],
        out_specs=[pl.BlockSpec((seq, LANES), col(0)),
                   pl.BlockSpec((blk, LANES), colj(0)),
                   pl.BlockSpec((blk, LANES), colj(0)),
                   pl.BlockSpec((seq, LANES), lambda b, g, j: (b, 0))],
        out_shape=[pair, pair, pair, jax.ShapeDtypeStruct((m, LANES), F32)],
        scratch_shapes=[pltpu.VMEM((2, seq, LANES), BF16), pltpu.VMEM((2, seq, LANES), BF16),
                        pltpu.VMEM((nq, 2, 1, blk), F32), pltpu.VMEM((nq, 2, LANES, blk), F32),
                        pltpu.VMEM((2, blk, LANES), F32), pltpu.VMEM((blk, LANES), F32)],
        compiler_params=_params(("arbitrary", "arbitrary", "arbitrary")),
    )(z, z, z, qterm, kterm, o, do, lse)


def _split3(c):
    c1 = c.astype(BF16)
    r1 = c - c1.astype(F32)
    c2 = r1.astype(BF16)
    c3 = (r1 - c2.astype(F32)).astype(BF16)
    return c1, c2, c3


def _mesh_pos():
    return lax.axis_index("x"), lax.axis_index("y"), lax.axis_index("c")


def _flip(v, bit):
    return 1 - v if bit else v


def all_gather(name, blocks):
    n = len(blocks)

    def body(*refs):
        x_refs, out_refs = refs[:n], refs[n:2 * n]
        send_sems, recv_sems, local_sems = refs[2 * n:]
        x, y, c = _mesh_pos()
        me, sibling = (x, y, c), (x, y, 1 - c)
        chips = [(1 - x, y), (x, 1 - y), (1 - x, 1 - y)]

        def slot(a, px, py, pc):
            return out_refs[a].at[4 * px + 2 * py + pc]

        def copy(a, k, blk, to, src=None):
            return pltpu.make_async_remote_copy(
                src_ref=slot(a, *blk) if src is None else src, dst_ref=slot(a, *blk),
                send_sem=send_sems.at[a, k], recv_sem=recv_sems.at[a, k], device_id=to, device_id_type=MESH)

        mine = [pltpu.make_async_copy(x_refs[a], slot(a, *me), local_sems.at[a]) for a in range(n)]
        for cp in mine:
            cp.start()
        sends = []
        for a in range(n):
            sends.append(copy(a, 0, me, sibling, src=x_refs[a]))
            sends += [copy(a, 1 + j, me, (*chip, c), src=x_refs[a]) for j, chip in enumerate(chips)]
        for cp in sends:
            cp.start()
        for j, chip in enumerate(chips):
            for a in range(n):
                copy(a, 1 + j, (*chip, c), me).wait_recv()
                passed = copy(a, 4 + j, (*chip, c), sibling)
                passed.start()
                sends.append(passed)
        for a in range(n):
            copy(a, 0, sibling, me).wait_recv()
            for j, chip in enumerate(chips):
                copy(a, 4 + j, (*chip, 1 - c), me).wait_recv()
        for cp in sends:
            cp.wait_send()
        for cp in mine:
            cp.wait()

    hbm = pl.BlockSpec(memory_space=pl.ANY)
    return pl.pallas_call(
        body, name=name, out_shape=[jax.ShapeDtypeStruct((N_DEV,) + b.shape, b.dtype) for b in blocks],
        in_specs=[hbm] * n, out_specs=[hbm] * n,
        scratch_shapes=[pltpu.SemaphoreType.DMA((n, 7)), pltpu.SemaphoreType.DMA((n, 7)), pltpu.SemaphoreType.DMA((n,))],
    )(*blocks)


def _peers(x, y, c):
    return [(_flip(x, k & 4), _flip(y, k & 2), _flip(c, k & 1)) for k in range(1, N_DEV)]


def gather_start(name, blocks, lands):
    n = len(blocks)

    def body(*refs):
        x_refs, land_refs = refs[:n], refs[n:2 * n]
        send_sems, recv_sems = refs[2 * n], refs[2 * n + 1]
        token = refs[-1]
        x, y, c = _mesh_pos()
        me = 4 * x + 2 * y + c
        for k, peer in enumerate(_peers(x, y, c)):
            for a in range(n):
                pltpu.make_async_remote_copy(
                    src_ref=x_refs[a], dst_ref=land_refs[a].at[me], send_sem=send_sems.at[7 * a + k], recv_sem=recv_sems.at[7 * a + k],
                    device_id=peer, device_id_type=MESH).start()
        token[...] = jnp.zeros_like(token)

    hbm = pl.BlockSpec(memory_space=pltpu.HBM)
    sem = pl.BlockSpec(memory_space=pltpu.SEMAPHORE)
    out_shape = ([pltpu.SemaphoreType.DMA((7 * n,)), pltpu.SemaphoreType.DMA((7 * n,))]
                 + [pltpu.HBM(b.shape, b.dtype) for b in blocks] + [pltpu.HBM(l.shape, l.dtype) for l in lands]
                 + [jax.ShapeDtypeStruct((8, LANES), F32)])
    res = pl.pallas_call(
        body, name=name, out_shape=out_shape, in_specs=[hbm] * (2 * n),
        out_specs=[sem, sem] + [hbm] * (2 * n) + [pl.BlockSpec(memory_space=pltpu.VMEM)],
        input_output_aliases={a: 2 + a for a in range(2 * n)},
        compiler_params=pltpu.CompilerParams(has_side_effects=pltpu.SideEffectType.DATAFLOW_SIDE_EFFECTING),
    )(*[pltpu.with_memory_space_constraint(b, pltpu.HBM) for b in blocks],
      *[pltpu.with_memory_space_constraint(l, pltpu.HBM) for l in lands])
    return res[0], res[1], res[2:2 + n], res[2 + n:2 + 2 * n], res[-1]


def gather_wait(name, send_sems, recv_sems, blocks, lands, after):
    n = len(blocks)

    def body(*refs):
        x_refs, land_refs = refs[:n], refs[n:2 * n]
        s_sems, r_sems = refs[2 * n], refs[2 * n + 1]
        x, y, c = _mesh_pos()
        me = 4 * x + 2 * y + c
        for k, peer in enumerate(_peers(x, y, c)):
            for a in range(n):
                cp = pltpu.make_async_remote_copy(
                    src_ref=x_refs[a], dst_ref=land_refs[a].at[me], send_sem=s_sems.at[7 * a + k], recv_sem=r_sems.at[7 * a + k],
                    device_id=peer, device_id_type=MESH)
                cp.wait_send()
                cp.wait_recv()

    hbm = pl.BlockSpec(memory_space=pltpu.HBM)
    sem = pl.BlockSpec(memory_space=pltpu.SEMAPHORE)
    res = pl.pallas_call(
        body, name=name,
        out_shape=[pltpu.HBM(b.shape, b.dtype) for b in blocks] + [pltpu.HBM(l.shape, l.dtype) for l in lands],
        in_specs=[hbm] * (2 * n) + [sem, sem, pl.BlockSpec(memory_space=pl.ANY)], out_specs=[hbm] * (2 * n),
        input_output_aliases={a: a for a in range(2 * n)},
        compiler_params=pltpu.CompilerParams(has_side_effects=pltpu.SideEffectType.DATAFLOW_SIDE_EFFECTING),
    )(*blocks, *lands, send_sems, recv_sems, after)
    return res[n:]


def _split_exchange(name, sends, lands, sems, after):
    n = len(sends)
    starting = sems is None

    def body(*refs):
        s_refs, l_refs = refs[:n], refs[n:2 * n]
        send_sems, recv_sems = refs[2 * n], refs[2 * n + 1]
        x, y, c = _mesh_pos()
        me = 4 * x + 2 * y + c
        for k, (px, py, pc) in enumerate(_peers(x, y, c)):
            for a in range(n):
                cp = pltpu.make_async_remote_copy(
                    src_ref=s_refs[a].at[4 * px + 2 * py + pc], dst_ref=l_refs[a].at[me],
                    send_sem=send_sems.at[7 * a + k], recv_sem=recv_sems.at[7 * a + k],
                    device_id=(px, py, pc), device_id_type=MESH)
                if starting:
                    cp.start()
                else:
                    cp.wait_send()
                    cp.wait_recv()
        if starting:
            refs[-1][...] = jnp.zeros_like(refs[-1])

    hbm = pl.BlockSpec(memory_space=pltpu.HBM)
    sem = pl.BlockSpec(memory_space=pltpu.SEMAPHORE)
    thru = [pltpu.HBM(t.shape, t.dtype) for t in list(sends) + list(lands)]
    effect = pltpu.CompilerParams(has_side_effects=pltpu.SideEffectType.DATAFLOW_SIDE_EFFECTING)
    if starting:
        res = pl.pallas_call(
            body, name=name, in_specs=[hbm] * (2 * n),
            out_shape=[pltpu.SemaphoreType.DMA((7 * n,)), pltpu.SemaphoreType.DMA((7 * n,))] + thru + [jax.ShapeDtypeStruct((8, LANES), F32)],
            out_specs=[sem, sem] + [hbm] * (2 * n) + [pl.BlockSpec(memory_space=pltpu.VMEM)],
            input_output_aliases={a: 2 + a for a in range(2 * n)}, compiler_params=effect,
        )(*[pltpu.with_memory_space_constraint(t, pltpu.HBM) for t in list(sends) + list(lands)])
        return res[0], res[1], res[2:2 + n], res[2 + n:2 + 2 * n], res[-1]
    res = pl.pallas_call(
        body, name=name, out_shape=thru, in_specs=[hbm] * (2 * n) + [sem, sem, pl.BlockSpec(memory_space=pl.ANY)],
        out_specs=[hbm] * (2 * n), input_output_aliases={a: a for a in range(2 * n)}, compiler_params=effect,
    )(*sends, *lands, sems[0], sems[1], after)
    return res[n:]


def unwritten(name, like):
    def body(*refs):
        pass

    hbm = pl.BlockSpec(memory_space=pl.ANY)
    return pl.pallas_call(body, name=name, out_shape=[jax.ShapeDtypeStruct(t.shape, t.dtype) for t in like],
                          out_specs=[hbm] * len(like))()


def own_slot_only(send, land, me):
    mine = lax.dynamic_index_in_dim(send, me, 0, keepdims=False)
    return lax.dynamic_update_index_in_dim(land, mine, me, 0)


def all_to_all(name, sends):
    n = len(sends)

    def body(*refs):
        s_refs, r_refs = refs[:n], refs[n:2 * n]
        send_sems, recv_sems, local_sems = refs[2 * n:]
        x, y, c = _mesh_pos()
        me = 4 * x + 2 * y + c
        mine = [pltpu.make_async_copy(s_refs[a].at[me], r_refs[a].at[me], local_sems.at[a]) for a in range(n)]
        for cp in mine:
            cp.start()
        copies = []
        for k in range(1, N_DEV):
            px, py, pc = _flip(x, k & 4), _flip(y, k & 2), _flip(c, k & 1)
            for a in range(n):
                copies.append(pltpu.make_async_remote_copy(
                    src_ref=s_refs[a].at[4 * px + 2 * py + pc], dst_ref=r_refs[a].at[me],
                    send_sem=send_sems.at[a, k - 1], recv_sem=recv_sems.at[a, k - 1],
                    device_id=(px, py, pc), device_id_type=MESH))
        for cp in copies:
            cp.start()
        for cp in copies:
            cp.wait_recv()
        for cp in copies:
            cp.wait_send()
        for cp in mine:
            cp.wait()

    hbm = pl.BlockSpec(memory_space=pl.ANY)
    return pl.pallas_call(
        body, name=name, out_shape=[jax.ShapeDtypeStruct(s.shape, s.dtype) for s in sends],
        in_specs=[hbm] * n, out_specs=[hbm] * n,
        scratch_shapes=[pltpu.SemaphoreType.DMA((n, 7)), pltpu.SemaphoreType.DMA((n, 7)), pltpu.SemaphoreType.DMA((n,))],
    )(*sends)


def _row_tile(r, cap, step):
    return next((t for t in range(cap, step - 1, -step) if r % t == 0), r)


def _sum_parts(p, n):
    t = [p[k].astype(F32) for k in range(n)]
    while len(t) > 1:
        t = [t[k] + t[k + 1] for k in range(0, len(t), 2)]
    return t[0]


def _adam(g, w, m, v):
    m = ADAM_B1 * m + (1.0 - ADAM_B1) * g
    v = ADAM_B2 * v + (1.0 - ADAM_B2) * (g * g)
    m_hat = m / (1.0 - ADAM_B1 ** ADAM_STEP)
    v_hat = v / (1.0 - ADAM_B2 ** ADAM_STEP)
    return -ADAM_LR * (m_hat / (jnp.sqrt(v_hat) + ADAM_EPS) + ADAM_WD * w), m, v


def adam_tiled(name, partials, w, m_, v_, layer=0, prev=None):
    _, r, c = w.shape
    n_part = partials.shape[0]
    tr = _row_tile(r, 256, 16)

    def body(*refs):
        p_ref, w_ref, m_ref, v_ref = refs[:4]
        g_ref, d_ref, nm_ref, nv_ref = refs[-4:]
        g = _sum_parts(p_ref, n_part)
        g_ref[...] = g
        d_ref[...], nm_ref[...], nv_ref[...] = _adam(g, w_ref[...], m_ref[...], v_ref[...])

    spec = pl.BlockSpec((None, tr, c), lambda i: (layer, i, 0))
    in_specs = [pl.BlockSpec((n_part, None, tr, c), lambda i: (0, 0, i, 0)), spec, spec, spec]
    args = [partials, w, m_, v_]
    aliases = {}
    if prev is not None:
        in_specs += [pl.BlockSpec(memory_space=pl.ANY)] * 4
        args += list(prev)
        aliases = {4 + k: k for k in range(4)}
    return pl.pallas_call(
        body, grid=(r // tr,), name=name, in_specs=in_specs,
        out_specs=[spec] * 4, out_shape=[jax.ShapeDtypeStruct(w.shape, F32)] * 4,
        input_output_aliases=aliases, compiler_params=_params(("parallel",)),
    )(*args)


def adam_small(name, items, extra):
    n, ne = len(items), len(extra)

    def body(*refs):
        ins, outs = refs[:4 * n + ne], refs[4 * n + ne:]
        for a in range(n):
            p_ref, w_ref, m_ref, v_ref = ins[4 * a:4 * a + 4]
            g = _sum_parts(p_ref, N_DEV)
            outs[4 * a][...] = g
            outs[4 * a + 1][...], outs[4 * a + 2][...], outs[4 * a + 3][...] = _adam(g, w_ref[...], m_ref[...], v_ref[...])
        for e in range(ne):
            outs[4 * n + e][...] = _sum_parts(ins[4 * n + e], N_DEV)

    args, out_shape = [], []
    for p, w, m_, v_ in items:
        args += [p, w, m_, v_]
        out_shape += [jax.ShapeDtypeStruct(w.shape, F32)] * 4
    for e in extra:
        args.append(e)
        out_shape.append(jax.ShapeDtypeStruct(e.shape[1:], F32))
    vmem = pl.BlockSpec(memory_space=pltpu.VMEM)
    res = pl.pallas_call(body, name=name, in_specs=[vmem] * len(args), out_specs=[vmem] * len(out_shape), out_shape=out_shape)(*args)
    return [res[4 * a:4 * a + 4] for a in range(n)], res[4 * n:]


def _cols_from_gather(g):
    g = jnp.moveaxis(g, 0, -2)
    return g.reshape(g.shape[:-2] + (g.shape[-2] * g.shape[-1],))


def _cols_to_blocks(w):
    w = w.reshape(w.shape[:-1] + (N_DEV, w.shape[-1] // N_DEV))
    return jnp.moveaxis(w, -2, 0)


def _block_diag(w):
    pairs = w.reshape(B_BLOCKS // 2, 2, B_BLOCK_DIM, 1, B_BLOCK_DIM)
    same = jnp.eye(2, dtype=bool).reshape(1, 2, 1, 2, 1)
    return jnp.where(same, pairs, 0.0).reshape(B_BLOCKS // 2 * LANES, LANES)


def _block_diag_grad(d):
    parts = d.reshape(B_BLOCKS // 2, 2, B_BLOCK_DIM, 2, B_BLOCK_DIM)
    same = jnp.eye(2, dtype=bool).reshape(1, 2, 1, 2, 1)
    return jnp.sum(jnp.where(same, parts, 0.0), axis=3).reshape(B_BLOCKS, B_BLOCK_DIM, B_BLOCK_DIM)


NAMES = ("norm_gains", "even_w_in", "hgrn_lb_logits", "hgrn_norm", "rg_conv_w", "rg_conv_b", "rg_wa", "rg_ba", "rg_wx", "rg_bx",
         "rg_lambda", "even_w_out", "odd_w_in", "fox_f_bias", "odd_w_out", "ffn_w_up", "ffn_conv_w", "ffn_conv_b", "ffn_w_down")
SMALL_SHARDED = ("norm_gains", "rg_conv_w", "ffn_conv_w")
REPLICATED = ("hgrn_lb_logits", "hgrn_norm", "rg_conv_b", "rg_wa", "rg_ba", "rg_wx", "rg_bx", "rg_lambda", "fox_f_bias", "ffn_conv_b")


def _ffn_forward(tag, layer, h, w_up_g, cw5, cb5, w_down_g, m, seq):
    tm = _div_tile(m, 1024)
    nm = m // tm
    hid = mm(f"{tag}_up", "nn",
             Blk(h, (tm, D_MODEL), lambda i, j, k: (i, 0)),
             Blk(w_up_g, (None, None, D_MODEL, FF_BLK), lambda i, j, k: (j, 0, 0, 0)),
             Blk((N_DEV, m, FF_BLK), (None, tm, FF_BLK), lambda i, j, k: (j, i, 0)), F32, (nm, N_DEV, 1))
    hid = hid.reshape(2, N_DEV // 2, m, FF_BLK)
    act, conv = ffn_mid_fwd(f"{tag}_mid", hid, cw5, cb5, layer, m=m, seq=seq)
    f = mm(f"{tag}_down", "nn",
           Blk(act, (None, tm, FF_BLK), lambda i, j, k: (k, i, 0)),
           Blk(w_down_g, (2, None, FF_BLK // 2, D_MODEL), lambda i, j, k: (k, 0, 0, 0)),
           Blk((m, D_MODEL), (tm, D_MODEL), lambda i, j, k: (i, 0)), F32, (nm, 1, N_DEV // 2))
    return (hid, conv), act, f


def _ffn_backward(tag, layer, df, h, hid, act, w_up_g, cw5, cb5, w_down_g, m, seq):
    tm = _div_tile(m, 1024)
    nm = m // tm
    dact = mm(f"{tag}_dact", "nt",
              Blk(df, (tm, D_MODEL), lambda i, j, k: (i, 0)),
              Blk(w_down_g, (2, None, FF_BLK // 2, D_MODEL), lambda i, j, k: (j, 0, 0, 0)),
              Blk((N_DEV // 2, m, FF_BLK), (None, tm, FF_BLK), lambda i, j, k: (j, i, 0)), BF16, (nm, N_DEV // 2, 1))
    d_wdown = mm(f"{tag}_dwdown", "tn",
                 Blk(act, (None, tm, FF_BLK), lambda i, j, k: (i, k, 0)),
                 Blk(df, (tm, D_MODEL), lambda i, j, k: (k, 0)),
                 Blk(w_down_g.shape, (2, None, FF_BLK // 2, D_MODEL), lambda i, j, k: (i, 0, 0, 0)), BF16,
                 (N_DEV // 2, 1, nm))
    dhid, d_cw, d_cb = ffn_mid_bwd(f"{tag}_dmid", hid[0], hid[1], cw5, dact, layer, m=m, seq=seq)
    dhid = dhid.reshape(N_DEV, m, FF_BLK)
    dh = mm(f"{tag}_dh", "nt",
            Blk(dhid, (None, tm, FF_BLK), lambda i, j, k: (k, i, 0)),
            Blk(w_up_g, (None, None, D_MODEL, FF_BLK), lambda i, j, k: (k, 0, 0, 0)),
            Blk((m, D_MODEL), (tm, D_MODEL), lambda i, j, k: (i, 0)), BF16, (nm, 1, N_DEV))
    d_wup = mm(f"{tag}_dwup", "tn",
               Blk(dhid, (None, tm, FF_BLK), lambda i, j, k: (i, k, 0)),
               Blk(h, (tm, D_MODEL), lambda i, j, k: (k, 0)),
               Blk((N_DEV, 1, FF_BLK, D_MODEL), (None, None, FF_BLK, D_MODEL), lambda i, j, k: (i, 0, 0, 0)), BF16,
               (N_DEV, 1, nm))
    return dh, d_wup, d_cw, d_cb, d_wdown


def kernel(x, norm_gains, even_w_in, hgrn_lb_logits, hgrn_norm, rg_conv_w, rg_conv_b, rg_wa, rg_ba, rg_wx, rg_bx, rg_lambda, even_w_out, odd_w_in, fox_f_bias, odd_w_out, ffn_w_up, ffn_conv_w, ffn_conv_b, ffn_w_down, loss_target, m_norm_gains, m_even_w_in, m_hgrn_lb_logits, m_hgrn_norm, m_rg_conv_w, m_rg_conv_b, m_rg_wa, m_rg_ba, m_rg_wx, m_rg_bx, m_rg_lambda, m_even_w_out, m_odd_w_in, m_fox_f_bias, m_odd_w_out, m_ffn_w_up, m_ffn_conv_w, m_ffn_conv_b, m_ffn_w_down, v_norm_gains, v_even_w_in, v_hgrn_lb_logits, v_hgrn_norm, v_rg_conv_w, v_rg_conv_b, v_rg_wa, v_rg_ba, v_rg_wx, v_rg_bx, v_rg_lambda, v_even_w_out, v_odd_w_in, v_fox_f_bias, v_odd_w_out, v_ffn_w_up, v_ffn_conv_w, v_ffn_conv_b, v_ffn_w_down):
    local = dict(locals())
    w = {n: local[n] for n in NAMES}
    mom = {n: local["m_" + n] for n in NAMES}
    var = {n: local["v_" + n] for n in NAMES}
    n_batch, seq, _ = x.shape
    m = n_batch * seq
    tm = _div_tile(m, 512)
    tmm = _div_tile(m, 1024)
    nm = m // tmm

    gathered = all_gather("gather_weights", [w["even_w_in"].astype(BF16)] + [w[n] for n in SMALL_SHARDED])
    g = dict(zip(("even_w_in",) + SMALL_SHARDED, gathered))
    w_in_e = g["even_w_in"]
    gains = _cols_from_gather(g["norm_gains"])
    me = 4 * lax.axis_index("x") + 2 * lax.axis_index("y") + lax.axis_index("c")
    def own_block_only(name, blocks):
        lands = unwritten(name, [jax.ShapeDtypeStruct((N_DEV,) + t.shape, t.dtype) for t in blocks])
        return [lax.dynamic_update_index_in_dim(ld, t, me, 0) for ld, t in zip(lands, blocks)]

    def own_slots_only(name, sends):
        return [own_slot_only(t, ld, me) for t, ld in zip(sends, unwritten(name, sends))]

    behind = (g["norm_gains"][0, 0, 0, 0] * 0.0).astype(BF16)
    out0 = [w["even_w_out"].astype(BF16) + behind]
    out0_sent = gather_start("gather_out0_start", out0, own_block_only("land_out0", out0))
    behind = (out0_sent[4][0, 0] * 0.0).astype(BF16)
    ffn0 = [w["ffn_w_up"][0:1].astype(BF16) + behind, w["ffn_w_down"][0:1].astype(BF16) + behind]
    ffn0_sent = gather_start("gather_ffn0_start", ffn0, own_block_only("land_ffn0", ffn0))
    behind = (ffn0_sent[4][0, 0] * 0.0).astype(BF16)
    mix1w = [jnp.swapaxes(w["odd_w_in"], 1, 2).astype(BF16) + behind, w["odd_w_out"].astype(BF16) + behind]
    mix1_sent = gather_start("gather_mix1_start", mix1w, own_block_only("land_mix1", mix1w))
    behind = (mix1_sent[4][0, 0] * 0.0).astype(BF16)
    ffn1 = [w["ffn_w_up"][1:2].astype(BF16) + behind, w["ffn_w_down"][1:2].astype(BF16) + behind]
    ffn1_sent = gather_start("gather_ffn1_start", ffn1, own_block_only("land_ffn1", ffn1))
    started = ffn1_sent[4]
    rg_cw = _cols_from_gather(g["rg_conv_w"])[0]
    n_layer = ffn_conv_w.shape[0]
    cw5 = g["ffn_conv_w"].reshape(2, N_DEV // 2, n_layer, FFN_CONV, FF_BLK)
    cb5 = ffn_conv_b.reshape(n_layer, 2, N_DEV // 2, 1, FF_BLK)
    gain = lambda l, k: gains[l, k:k + 1, :]
    wa_bd, wx_bd = _block_diag(rg_wa[0]), _block_diag(rg_wx[0])
    fbias = jnp.pad(fox_f_bias, ((0, 0), (0, LANES - C_HEADS)))

    x0 = x.reshape(m, D_MODEL)
    tgt = loss_target.reshape(m, D_MODEL)

    (h0,) = tile_fwd("l0_prenorm", fn_prenorm_after, m=m, tm=tm, nj=1, rows=[Row(x0)], pars=[Par(gain(0, 0)), Par(started)],
                     outs=[Out(D_MODEL, BF16)])
    z0 = mm("l0_in", "nn",
            Blk(h0, (tmm, D_MODEL), lambda i, j, k: (i, 0)),
            Blk(w_in_e, (2, None, D_MODEL, 384), lambda i, j, k: (j, 0, 0, 0)),
            Blk((m, 3072), (tmm, 768), lambda i, j, k: (i, j)), F32, (nm, N_DEV // 2, 1), b_join=True)
    oa, sprev = hgrn_fwd("l0_hgrn", z0, hgrn_lb_logits, hgrn_norm, n_batch=n_batch, seq=seq)
    rg_rows = lambda: [Row(z0, LANES, 16), Row(z0, LANES, 20)]
    rg_pars = lambda: [Par(rg_cw, "col", LANES), Par(rg_conv_b, "col", LANES), Par(wa_bd, "row", LANES), Par(rg_ba, "col", LANES),
                       Par(wx_bd, "row", LANES), Par(rg_bx, "col", LANES), Par(rg_lambda, "col", LANES)]
    (ob,) = tile_fwd("l0_rglru", fn_rglru, m=m, tm=seq, nj=B_WIDTH // LANES, rows=rg_rows(), pars=rg_pars(),
                     outs=[Out(B_WIDTH, BF16, LANES)])
    mixcat0 = jnp.concatenate([oa, ob], axis=-1)
    (g_out_e,) = gather_wait("gather_out0_wait", out0_sent[0], out0_sent[1], out0_sent[2], out0_sent[3], mixcat0)
    w_out_e = g_out_e.reshape(D_MODEL, D_MODEL)
    mix0 = mm2d("l0_out", "nn", mixcat0, w_out_e)
    x1, h1 = tile_fwd("l0_postnorm", fn_addnorm2, m=m, tm=tm, nj=1, rows=[Row(x0), Row(mix0)], pars=[Par(gain(0, 1)), Par(gain(0, 2))],
                      outs=[Out(D_MODEL, F32), Out(D_MODEL, BF16)])
    w_up_g0, w_down_g0 = gather_wait("gather_ffn0_wait", ffn0_sent[0], ffn0_sent[1], ffn0_sent[2], ffn0_sent[3], h1)
    hid0, act0, f0 = _ffn_forward("l0_ffn", 0, h1, w_up_g0, cw5, cb5, w_down_g0, m, seq)
    x2, h2 = tile_fwd("l0_ffnnorm", fn_addnorm2, m=m, tm=tm, nj=1, rows=[Row(x1), Row(f0)], pars=[Par(gain(0, 3)), Par(gain(1, 0))],
                      outs=[Out(D_MODEL, F32), Out(D_MODEL, BF16)])

    g_in_o, g_out_o = gather_wait("gather_mix1_wait", mix1_sent[0], mix1_sent[1], mix1_sent[2], mix1_sent[3], h2)
    w_in_o_t = jnp.pad(g_in_o.reshape(3088, D_MODEL), ((0, 3200 - 3088), (0, 0)))
    w_out_o = g_out_o.reshape(D_MODEL, D_MODEL)
    z1 = mm2d("l1_in", "nt", h2, w_in_o_t)
    (cgate,) = tile_fwd("l1_gate", fn_fox_gate, m=m, tm=seq, nj=1, rows=[Row(z1, LANES, 3072 // LANES)], pars=[Par(fbias)],
                        outs=[Out(LANES, F32)])
    place, ones_q, ones_k = term_placement()
    qterm, kterm = tile_fwd("l1_terms", fn_fox_terms, m=m, tm=tm, nj=1, rows=[Row(cgate)],
                            pars=[Par(place), Par(ones_q), Par(ones_k)], outs=[Out(TERM_W, BF16), Out(TERM_W, BF16)])
    oc, lse = fox_pair_fwd("l1_attn", z1, qterm, kterm, n_batch=n_batch, seq=seq)
    blk_b = min(ATT_BLK, seq)
    lse = lse.reshape(n_batch, N_PAIR, -1, 2, lse.shape[-1] // blk_b, blk_b).swapaxes(3, 4).reshape(n_batch, N_PAIR, seq // blk_b, 2, 1, blk_b)
    mix1 = mm2d("l1_out", "nn", oc, w_out_o)
    x3, h3 = tile_fwd("l1_postnorm", fn_addnorm2, m=m, tm=tm, nj=1, rows=[Row(x2), Row(mix1)], pars=[Par(gain(1, 1)), Par(gain(1, 2))],
                      outs=[Out(D_MODEL, F32), Out(D_MODEL, BF16)])
    w_up_g1, w_down_g1 = gather_wait("gather_ffn1_wait", ffn1_sent[0], ffn1_sent[1], ffn1_sent[2], ffn1_sent[3], h3)
    hid1, act1, f1 = _ffn_forward("l1_ffn", 1, h3, w_up_g1, cw5, cb5, w_down_g1, m, seq)
    dy, df1, loss_part, d_g13 = loss_head("loss", x3, f1, tgt, gain(1, 3), m=m, tm=tm)
    dh3, d_wup1, d_cw1, d_cb1, d_wdown1 = _ffn_backward("l1_ffn", 1, df1, h3, hid1, act1, w_up_g1, cw5, cb5, w_down_g1, m, seq)
    dx2, dmix1, d_g11, d_g12 = tile_bwd("l1_dpostnorm", fn_addnorm2, m=m, tm=tm, nj=1, rows=[Row(x2), Row(mix1)],
                                        pars=[Par(gain(1, 1)), Par(gain(1, 2))], cts=[Row(dy), Row(dh3)],
                                        drows=[Out(D_MODEL, F32), Out(D_MODEL, BF16)])
    doc = mm2d("l1_doc", "nt", dmix1, w_out_o, BF16)
    d_wout_o = mm2d("l1_dwout", "tn", oc, dmix1)
    dq, dk, dv, dc = fox_pair_bwd("l1_dattn", z1, qterm, kterm, oc, doc, lse, n_batch=n_batch, seq=seq)
    dzf, d_fbias = tile_bwd("l1_dgate", fn_fox_gate, m=m, tm=seq, nj=1, rows=[Row(z1, LANES, 3072 // LANES)], pars=[Par(fbias)],
                            cts=[Row(dc)], drows=[Out(LANES, BF16)])
    dz1 = jnp.concatenate([dq, dk, dv, dzf], axis=-1)
    dh2 = mm2d("l1_dh", "nn", dz1, w_in_o_t, BF16)
    d_win_o_t = mm2d("l1_dwin", "tn", dz1, h2, BF16)

    send1 = [d_win_o_t[:3088].reshape(N_DEV, 1, 3088 // N_DEV, D_MODEL),
             d_wout_o.reshape(N_DEV, 1, D_MODEL // N_DEV, D_MODEL).astype(BF16), d_wup1, d_wdown1]
    sent1 = _split_exchange("exchange_l1_start", send1, own_slots_only("land_l1", send1), None, None)

    dx1, df0, d_g03, d_g10 = tile_bwd("l0_dffnnorm", fn_addnorm2_after, m=m, tm=tm, nj=1, rows=[Row(x1), Row(f0)],
                                      pars=[Par(gain(0, 3)), Par(gain(1, 0)), Par(sent1[4])], cts=[Row(dx2), Row(dh2)],
                                      drows=[Out(D_MODEL, F32), Out(D_MODEL, BF16)])[:4]
    dh1, d_wup0, d_cw0, d_cb0, d_wdown0 = _ffn_backward("l0_ffn", 0, df0, h1, hid0, act0, w_up_g0, cw5, cb5, w_down_g0, m, seq)
    send0 = [d_wup0, d_wdown0]
    sent0 = _split_exchange("exchange_ffn0_start", send0, own_slots_only("land_dffn0", send0), None, None)
    dx0a, dmix0, d_g01, d_g02 = tile_bwd("l0_dpostnorm", fn_addnorm2_after, m=m, tm=tm, nj=1, rows=[Row(x0), Row(mix0)],
                                         pars=[Par(gain(0, 1)), Par(gain(0, 2)), Par(sent0[4])], cts=[Row(dx1), Row(dh1)],
                                         drows=[Out(D_MODEL, F32), Out(D_MODEL, BF16)])[:4]
    dmixcat0 = mm2d("l0_dmixcat", "nt", dmix0, w_out_e, BF16)
    d_wout_e = mm2d("l0_dwout", "tn", mixcat0, dmix0)
    dzq, dzf0, dzv, dzg, d_lb, d_hnorm = hgrn_bwd("l0_dhgrn", z0, sprev, hgrn_lb_logits, hgrn_norm, dmixcat0, n_batch=n_batch, seq=seq)
    dzx, dzy, d_rcw, d_rcb, d_wa, d_ba, d_wx, d_bx, d_lam = tile_bwd(
        "l0_drglru", fn_rglru, m=m, tm=seq, nj=B_WIDTH // LANES, rows=rg_rows(), pars=rg_pars(),
        cts=[Row(dmixcat0, LANES, A_WIDTH // LANES)], drows=[Out(B_WIDTH, BF16, LANES), Out(B_WIDTH, BF16, LANES)])
    dz0 = jnp.concatenate([dzq, dzf0, dzv, dzg, dzx, dzy], axis=-1)
    d_win_e = mm("l0_dwin", "tn",
                 Blk(h0, (tmm, D_MODEL), lambda i, j, k: (k, 0)),
                 Blk(dz0, (tmm, 768), lambda i, j, k: (k, j)),
                 Blk(w_in_e.shape, (2, None, D_MODEL, 384), lambda i, j, k: (j, 0, 0, 0)), BF16, (1, N_DEV // 2, nm), o_split=True)
    send_e = [d_win_e, d_wout_e.reshape(N_DEV, 1, D_MODEL // N_DEV, D_MODEL).astype(BF16)]
    sent_e = _split_exchange("exchange_even_start", send_e, own_slots_only("land_even", send_e), None, None)
    d_ffn_cb = jnp.stack([d_cb0, d_cb1]).reshape(n_layer, 2 * D_FF)
    rep = {"hgrn_lb_logits": d_lb, "hgrn_norm": d_hnorm, "rg_conv_b": d_rcb, "rg_wa": _block_diag_grad(d_wa)[None], "rg_ba": d_ba,
           "rg_wx": _block_diag_grad(d_wx)[None], "rg_bx": d_bx, "rg_lambda": d_lam, "fox_f_bias": d_fbias[:, :C_HEADS],
           "ffn_conv_b": d_ffn_cb}
    rep_blocks = [rep[n] for n in REPLICATED] + [loss_part]
    rep_sent = gather_start("gather_partials_start", rep_blocks, own_block_only("land_partials", rep_blocks))
    dh0 = mm("l0_dh", "nt",
             Blk(dz0, (tmm, 768), lambda i, j, k: (i, k)),
             Blk(w_in_e, (2, None, D_MODEL, 384), lambda i, j, k: (k, 0, 0, 0)),
             Blk((m, D_MODEL), (tmm, D_MODEL), lambda i, j, k: (i, 0)), BF16, (nm, 1, N_DEV // 2), after=sent_e[4] + rep_sent[4],
             b_join=True)
    dx0, d_g00 = tile_bwd("l0_dprenorm", fn_input_norm, m=m, tm=tm, nj=1, rows=[Row(x0)], pars=[Par(gain(0, 0))],
                          cts=[Row(dx0a), Row(dh0)], drows=[Out(D_MODEL, F32)])

    d_gains = jnp.stack([jnp.concatenate([d_g00, d_g01, d_g02, d_g03], axis=0), jnp.concatenate([d_g10, d_g11, d_g12, d_g13], axis=0)])
    d_ffn_cw = jnp.stack([d_cw0, d_cw1], axis=2).reshape(N_DEV, n_layer, FFN_CONV, FF_BLK)
    r_in_o, r_out_o, r_up1, r_down1 = _split_exchange("exchange_l1_wait", sent1[2], sent1[3], sent1[:2], dx0)
    r_up0, r_down0 = _split_exchange("exchange_ffn0_wait", sent0[2], sent0[3], sent0[:2], dx0)
    r_in_e, r_out_e = _split_exchange("exchange_even_wait", sent_e[2], sent_e[3], sent_e[:2], dx0)
    recv, res = {}, {}
    flipped = ("odd_w_in", "ffn_w_up")
    view = lambda n, t: jnp.swapaxes(t, 1, 2) if n in flipped else t
    for n, r in (("even_w_in", r_in_e), ("even_w_out", r_out_e), ("odd_w_in", r_in_o), ("odd_w_out", r_out_o)):
        res[n] = [view(n, t) for t in adam_tiled("adam_" + n, r, view(n, w[n]), view(n, mom[n]), view(n, var[n]))]
    for n, parts_l in (("ffn_w_up", (r_up0, r_up1)), ("ffn_w_down", (r_down0, r_down1))):
        wmv = (view(n, w[n]), view(n, mom[n]), view(n, var[n]))
        first_layer = adam_tiled(f"adam_{n}_0", parts_l[0], *wmv, layer=0)
        res[n] = [view(n, t) for t in adam_tiled(f"adam_{n}_1", parts_l[1], *wmv, layer=1, prev=first_layer)]
    small_send = [_cols_to_blocks(d_gains), _cols_to_blocks(d_rcw[None]), d_ffn_cw]
    recv.update(zip(SMALL_SHARDED, all_to_all("exchange_small", small_send)))

    parts = gather_wait("gather_partials_wait", rep_sent[0], rep_sent[1], rep_sent[2], rep_sent[3], dx0)
    for n, p in zip(REPLICATED, parts):
        recv[n] = p
    small = SMALL_SHARDED + REPLICATED
    small_res, (loss_sum,) = adam_small("adam_small", [(recv[n], w[n], mom[n], var[n]) for n in small], [parts[-1]])
    res.update(dict(zip(small, small_res)))

    out = [loss_sum[0, 0], dx0.reshape(x.shape)]
    for k in range(4):
        out += [res[n][k] for n in NAMES]
    return tuple(out)
```

```python
import functools

import jax
import jax.numpy as jnp
from jax import lax
from jax.experimental import pallas as pl
from jax.experimental.pallas import tpu as pltpu

F32 = jnp.float32
BF16 = jnp.bfloat16

D_MODEL = 1024
A_HEADS = 4
A_WIDTH = 512
HGRN_CHUNK = 64
HGRN_SEG = 2048
B_WIDTH = 512
B_BLOCKS = 8
B_BLOCK_DIM = 64
B_CONV = 4
RG_C = 8.0
C_HEADS = 16
C_HEAD_DIM = 64
D_FF = 2816
FFN_CONV = 3
EPS = 1e-6
LANES = 128
HALO = 16
N_DEV = 8
FF_BLK = 2 * D_FF // N_DEV
MESH = pl.DeviceIdType.MESH
NEG = -1e30
VMEM_LIMIT = 56 * 1024 * 1024

ADAM_LR = 0.001
ADAM_B1 = 0.9
ADAM_B2 = 0.999
ADAM_EPS = 1e-08
ADAM_WD = 0.01
ADAM_STEP = 10


def _dg(a, b, pat):
    nb = a.ndim - 2
    batch = (tuple(range(nb)), tuple(range(nb)))
    ca = a.ndim - 1 if pat[0] == "n" else a.ndim - 2
    cb = b.ndim - 2 if pat[1] == "n" else b.ndim - 1
    return lax.dot_general(a.astype(BF16), b.astype(BF16), (((ca,), (cb,)), batch), preferred_element_type=F32)


@functools.partial(jax.custom_vjp, nondiff_argnums=(2,))
def bdot(a, b, pat):
    return _dg(a, b, pat)


def _bdot_fwd(a, b, pat):
    return _dg(a, b, pat), (a, b)


def _bdot_bwd(pat, res, g):
    a, b = res
    if pat == "nn":
        return _dg(g, b, "nt"), _dg(a, g, "tn")
    if pat == "nt":
        return _dg(g, b, "nn"), _dg(g, a, "tn")
    return _dg(b, g, "nt"), _dg(a, g, "nn")


bdot.defvjp(_bdot_fwd, _bdot_bwd)


def _shift_raw(x, s, up, fill):
    if s == 0:
        return x
    n = x.shape[0]
    r = pltpu.roll(x, (n - s) if up else s, 0)
    idx = lax.broadcasted_iota(jnp.int32, x.shape, 0)
    mask = (idx >= n - s) if up else (idx < s)
    return jnp.where(mask, jnp.asarray(fill, x.dtype), r)


@functools.partial(jax.custom_vjp, nondiff_argnums=(1,))
def shift_down(x, s):
    return _shift_raw(x, s, False, 0.0)


def _shift_down_fwd(x, s):
    return _shift_raw(x, s, False, 0.0), None


def _shift_down_bwd(s, _, g):
    return (_shift_raw(g, s, True, 0.0),)


shift_down.defvjp(_shift_down_fwd, _shift_down_bwd)


def _scan_impl(a, u, up):
    n = a.shape[0]
    s = 1
    while s < n:
        u = a * _shift_raw(u, s, up, 0.0) + u
        if 2 * s < n:
            a = a * _shift_raw(a, s, up, 1.0)
        s *= 2
    return u


@jax.custom_vjp
def lin_scan(a, u):
    return _scan_impl(a, u, False)


def _lin_scan_fwd(a, u):
    h = _scan_impl(a, u, False)
    return h, (a, h)


def _lin_scan_bwd(res, g):
    a, h = res
    gh = _scan_impl(_shift_raw(a, 1, True, 0.0), g, True)
    return gh * _shift_raw(h, 1, False, 0.0), gh


lin_scan.defvjp(_lin_scan_fwd, _lin_scan_bwd)


def _cumsum_impl(x, up, period):
    n = x.shape[0]
    span = n if period is None else period
    idx = lax.broadcasted_iota(jnp.int32, x.shape, 0)
    pos = idx if period is None else idx % period
    s = 1
    while s < span:
        sh = _shift_raw(x, s, up, 0.0)
        if period is not None:
            keep = (pos < period - s) if up else (pos >= s)
            sh = jnp.where(keep, sh, 0.0)
        x = x + sh
        s *= 2
    return x


@functools.partial(jax.custom_vjp, nondiff_argnums=(1,))
def cumsum_rows(x, period):
    return _cumsum_impl(x, False, period)


def _cumsum_fwd(x, period):
    return _cumsum_impl(x, False, period), None


def _cumsum_bwd(period, _, g):
    return (_cumsum_impl(g, True, period),)


cumsum_rows.defvjp(_cumsum_fwd, _cumsum_bwd)


def _sigmoid(x):
    return jax.nn.sigmoid(x)


def _expm1(x):
    return jnp.tanh(0.5 * x) * (jnp.exp(x) + 1.0)


def _softplus(x):
    return jnp.maximum(x, 0.0) + jnp.log(1.0 + jnp.exp(-jnp.abs(x)))


def _rms(x, g):
    return x * lax.rsqrt(jnp.mean(x * x, axis=-1, keepdims=True) + EPS) * g


def fn_prenorm(x, g):
    return (_rms(x, g).astype(BF16),)


def fn_prenorm_after(x, g, _token):
    return fn_prenorm(x, g)


def fn_addnorm2(x, y, g_post, g_pre):
    x1 = x + _rms(y, g_post)
    return x1, _rms(x1, g_pre).astype(BF16)


def fn_addnorm2_after(x, y, g_post, g_pre, _token):
    return fn_addnorm2(x, y, g_post, g_pre)


def fn_input_norm(x, g):
    return x, _rms(x, g).astype(BF16)


def _causal_conv(x, w, b, taps):
    c = b
    for k in range(taps):
        c = c + w[k:k + 1, :] * shift_down(x, taps - 1 - k)
    return c


def fn_rglru(xb, yb, cw, cb, wa, ba, wx, bx, lam):
    xf = _causal_conv(xb, cw, cb, B_CONV)
    r = _sigmoid(bdot(xf, wa, "nn") + ba)
    i = _sigmoid(bdot(xf, wx, "nn") + bx)
    log_a = -RG_C * r * _softplus(-lam)
    a = jnp.exp(log_a)
    u = jnp.sqrt(-_expm1(2.0 * log_a)) * (i * xf)
    h = lin_scan(a, u)
    return ((h * jax.nn.gelu(yb)).astype(BF16),)


def fn_fox_gate(zf, bias):
    return (cumsum_rows(jax.nn.log_sigmoid(zf + bias), None),)


def fn_hgrn_seg(q, fl, v, g, st, logits, hn):
    rows = q.shape[0]
    nc = rows // HGRN_CHUNK
    l0, l1, l2 = logits[0:1, :], logits[1:2, :], logits[2:3, :]
    mx = jnp.maximum(jnp.maximum(l0, l1), l2)
    e0, e1, e2 = jnp.exp(l0 - mx), jnp.exp(l1 - mx), jnp.exp(l2 - mx)
    lb = e0 / (e0 + e1 + e2)
    forget = lb + (1.0 - lb) * _sigmoid(fl)
    qs = q * _sigmoid(q)
    kk = 1.0 - forget
    logf = jnp.log(forget)
    bcum = cumsum_rows(logf, HGRN_CHUNK)
    c3 = lambda t: t.reshape(nc, HGRN_CHUNK, 128)
    b_last = jnp.sum(c3(logf), axis=1, keepdims=True)
    bcum3 = c3(bcum)
    q_dec = c3(qs) * jnp.exp(bcum3)
    k_dec = c3(kk) * jnp.exp(-bcum3)
    k_upd = c3(kk) * jnp.exp(b_last - bcum3)
    v3 = c3(v)
    scores = bdot(q_dec, k_dec, "nt")
    ri = lax.broadcasted_iota(jnp.int32, scores.shape, 1)
    ci = lax.broadcasted_iota(jnp.int32, scores.shape, 2)
    scores = jnp.where(ri >= ci, scores, 0.0)
    o = bdot(scores, v3, "nn")
    upd_t = bdot(v3, k_upd, "tn")
    dec = jnp.exp(b_last)
    prev = []
    for n in range(nc):
        prev.append(st)
        st = st * dec[n] + upd_t[n]
    o = o + bdot(q_dec, jnp.stack(prev), "nt")
    o = o.reshape(rows, 128)
    o = o * lax.rsqrt(jnp.mean(o * o, axis=-1, keepdims=True) + EPS) * hn
    return (o * _sigmoid(g)).astype(BF16), st


def _ffn_conv(xg, xv, cw, cb):
    cg = _causal_conv(xg, cw[0], cb[0], FFN_CONV)[HALO:]
    cv = _causal_conv(xv, cw[1], cb[1], FFN_CONV)[HALO:]
    return cg, cv


def _ffn_gate(cg, cv):
    return jax.nn.gelu(cg) * cv


class Row:
    def __init__(self, arr, cb=None, off=0):
        self.arr, self.cb, self.off = arr, cb, off

    def spec(self, tm):
        if self.cb is None:
            return pl.BlockSpec((tm, self.arr.shape[1]), lambda j, i: (i, 0))
        off = self.off
        return pl.BlockSpec((tm, self.cb), lambda j, i: (i, j + off))


class Par:
    def __init__(self, arr, kind="full", bs=None):
        self.arr, self.kind, self.bs = arr, kind, bs

    def block(self):
        if self.kind == "full":
            return self.arr.shape
        if self.kind == "col":
            return (self.arr.shape[0], self.bs)
        return (self.bs, self.arr.shape[1])

    def spec(self):
        if self.kind == "full":
            return pl.BlockSpec(self.block(), lambda j, i: (0, 0))
        if self.kind == "col":
            return pl.BlockSpec(self.block(), lambda j, i: (0, j))
        return pl.BlockSpec(self.block(), lambda j, i: (j, 0))


class Out:
    def __init__(self, width, dtype, cb=None, off=0):
        self.width, self.dtype, self.cb, self.off = width, dtype, cb, off

    def spec(self, tm):
        if self.cb is None:
            return pl.BlockSpec((tm, self.width), lambda j, i: (i, 0))
        off = self.off
        return pl.BlockSpec((tm, self.cb), lambda j, i: (i, j + off))


def _params(sem):
    return pltpu.CompilerParams(dimension_semantics=sem, vmem_limit_bytes=VMEM_LIMIT)


def tile_fwd(name, fn, *, m, tm, nj, rows, pars, outs, n_acc=0):
    n_r, n_p, n_o = len(rows), len(pars), len(outs)

    def body(*refs):
        ins = [r[...] for r in refs[:n_r + n_p]]
        res = fn(*ins)
        o_refs = refs[n_r + n_p:]
        for k in range(n_o):
            o_refs[k][...] = res[k].astype(o_refs[k].dtype)
        first = jnp.logical_and(pl.program_id(0) == 0, pl.program_id(1) == 0)
        for k in range(n_acc):
            ref = o_refs[n_o + k]

            @pl.when(first)
            def _():
                ref[...] = jnp.zeros_like(ref)

            ref[...] += res[n_o + k]

    out_shape = [jax.ShapeDtypeStruct((m, o.width), o.dtype) for o in outs]
    out_specs = [o.spec(tm) for o in outs]
    for _ in range(n_acc):
        out_shape.append(jax.ShapeDtypeStruct((1, LANES), F32))
        out_specs.append(pl.BlockSpec((1, LANES), lambda j, i: (0, 0)))
    sem = ("arbitrary", "arbitrary") if n_acc else ("parallel", "parallel")
    return pl.pallas_call(
        body, grid=(nj, m // tm), name=name,
        in_specs=[r.spec(tm) for r in rows] + [p.spec() for p in pars],
        out_specs=out_specs, out_shape=out_shape, compiler_params=_params(sem),
    )(*[r.arr for r in rows], *[p.arr for p in pars])


def tile_bwd(name, fn, *, m, tm, nj, rows, pars, cts, drows):
    n_r, n_p, n_c = len(rows), len(pars), len(cts)
    want = [k for k in range(n_r) if drows[k] is not None]

    def body(*refs):
        ins = [r[...] for r in refs[:n_r + n_p]]
        ct = [r[...] for r in refs[n_r + n_p:n_r + n_p + n_c]]
        o_refs = refs[n_r + n_p + n_c:]
        res, vjp = jax.vjp(fn, *ins)
        grads = vjp(tuple(c.astype(r.dtype) for c, r in zip(ct, res)))
        for pos, k in enumerate(want):
            o_refs[pos][...] = grads[k].astype(o_refs[pos].dtype)
        for k in range(n_p):
            ref = o_refs[len(want) + k]
            first = pl.program_id(1) == 0
            if pars[k].kind == "full":
                first = jnp.logical_and(first, pl.program_id(0) == 0)

            @pl.when(first)
            def _():
                ref[...] = jnp.zeros_like(ref)

            ref[...] += grads[n_r + k].astype(F32)

    out_shape = [jax.ShapeDtypeStruct((m, drows[k].width), drows[k].dtype) for k in want]
    out_specs = [drows[k].spec(tm) for k in want]
    for p in pars:
        out_shape.append(jax.ShapeDtypeStruct(p.arr.shape, F32))
        out_specs.append(p.spec())
    return pl.pallas_call(
        body, grid=(nj, m // tm), name=name,
        in_specs=[r.spec(tm) for r in rows] + [p.spec() for p in pars] + [c.spec(tm) for c in cts],
        out_specs=out_specs, out_shape=out_shape, compiler_params=_params(("arbitrary", "arbitrary")),
    )(*[r.arr for r in rows], *[p.arr for p in pars], *[c.arr for c in cts])


def loss_head(name, x, y, tgt, g, *, m, tm):
    def body(x_ref, y_ref, t_ref, g_ref, dout_ref, dy_ref, loss_ref, dg_ref):
        normed, vjp = jax.vjp(_rms, y_ref[...], g_ref[...])
        err = x_ref[...] + normed - t_ref[...]
        dout = err * (1.0 / D_MODEL)
        dy, dg = vjp(dout)
        dout_ref[...] = dout
        dy_ref[...] = dy.astype(dy_ref.dtype)

        @pl.when(pl.program_id(0) == 0)
        def _():
            loss_ref[...] = jnp.zeros_like(loss_ref)
            dg_ref[...] = jnp.zeros_like(dg_ref)

        loss_ref[...] += 0.5 * jnp.sum(jnp.mean(err * err, axis=-1, keepdims=True), axis=0, keepdims=True)
        dg_ref[...] += dg

    row = pl.BlockSpec((tm, D_MODEL), lambda i: (i, 0))
    whole = lambda w: pl.BlockSpec((1, w), lambda i: (0, 0))
    return pl.pallas_call(
        body, grid=(m // tm,), name=name, in_specs=[row, row, row, whole(D_MODEL)],
        out_specs=[row, row, whole(LANES), whole(D_MODEL)],
        out_shape=[jax.ShapeDtypeStruct((m, D_MODEL), F32), jax.ShapeDtypeStruct((m, D_MODEL), BF16),
                   jax.ShapeDtypeStruct((1, LANES), F32), jax.ShapeDtypeStruct((1, D_MODEL), F32)],
        compiler_params=_params(("arbitrary",)),
    )(x, y, tgt, g)


class Blk:
    def __init__(self, arr, block, index):
        self.arr, self.block, self.index = arr, block, index

    def spec(self):
        return pl.BlockSpec(self.block, self.index)


def _flat2(v):
    return v if v.ndim == 2 else v.reshape(-1, v.shape[-1])


def mm(name, pat, a, b, o, out_dtype, grid, after=None, b_join=False, o_split=False):
    nk = grid[2]
    o_shape = o.arr

    def put(o_ref, r):
        if o_split:
            half = r.shape[1] // 2
            o_ref[0] = r[:, :half].astype(out_dtype)
            o_ref[1] = r[:, half:].astype(out_dtype)
        else:
            o_ref[...] = r.astype(out_dtype).reshape(o_ref.shape)

    def body(*refs):
        a_ref, b_ref = refs[0], refs[1]
        o_ref = refs[3] if after is not None else refs[2]
        bv = jnp.concatenate([b_ref[0], b_ref[1]], axis=1) if b_join else _flat2(b_ref[...])
        r = _dg(_flat2(a_ref[...]), bv, pat)
        if nk == 1:
            put(o_ref, r)
            return
        acc_ref = refs[-1]
        kk = pl.program_id(2)

        @pl.when(kk == 0)
        def _():
            acc_ref[...] = r

        @pl.when(kk > 0)
        def _():
            acc_ref[...] += r

        @pl.when(kk == nk - 1)
        def _():
            put(o_ref, acc_ref[...])

    ob = [d for d in o.block if d is not None]
    if o_split:
        acc_shape = (ob[1], 2 * ob[2])
    else:
        acc_shape = (ob[0], ob[1]) if len(ob) == 2 else (ob[0] * ob[1], ob[2])
    in_specs = [a.spec(), b.spec()]
    args = [a.arr, b.arr]
    if after is not None:
        in_specs.append(pl.BlockSpec(memory_space=pl.ANY))
        args.append(after)
    return pl.pallas_call(
        body, grid=grid, name=name, in_specs=in_specs, out_specs=o.spec(),
        out_shape=jax.ShapeDtypeStruct(o_shape, out_dtype),
        scratch_shapes=[pltpu.VMEM(acc_shape, F32)] if nk > 1 else [],
        compiler_params=_params(("parallel", "parallel", "arbitrary")),
    )(*args)


def _div_tile(n, cap):
    if n <= cap:
        return n
    best = 128
    for t in range(128, cap + 1, 128):
        if n % t == 0:
            best = t
    return best


def mm2d(name, pat, a, b, out_dtype=F32):
    if pat == "tn":
        k, m = a.shape
    else:
        m, k = a.shape
    n = b.shape[0] if pat == "nt" else b.shape[1]
    tm, tn, tk = _div_tile(m, 1024), _div_tile(n, 1024), _div_tile(k, 1024)
    a_blk = Blk(a, (tk, tm), lambda i, j, kk: (kk, i)) if pat == "tn" else Blk(a, (tm, tk), lambda i, j, kk: (i, kk))
    b_blk = Blk(b, (tn, tk), lambda i, j, kk: (j, kk)) if pat == "nt" else Blk(b, (tk, tn), lambda i, j, kk: (kk, j))
    o_blk = Blk((m, n), (tm, tn), lambda i, j, kk: (i, j))
    return mm(name, pat, a_blk, b_blk, o_blk, out_dtype, (m // tm, n // tn, k // tk))


def hgrn_fwd(name, z, logits, hnorm, *, n_batch, seq):
    m = n_batch * seq
    ts = min(HGRN_SEG, seq)
    n_seg = seq // ts

    def body(q_ref, f_ref, v_ref, g_ref, lg_ref, hn_ref, o_ref, sp_ref, st_ref):
        s = pl.program_id(2)

        @pl.when(s == 0)
        def _():
            st_ref[...] = jnp.zeros_like(st_ref)

        st = st_ref[...]
        sp_ref[...] = st
        o, st_new = fn_hgrn_seg(q_ref[...], f_ref[...], v_ref[...], g_ref[...], st, lg_ref[...], hn_ref[...])
        o_ref[...] = o
        st_ref[...] = st_new

    part = lambda p: pl.BlockSpec((ts, 128), lambda h, b, s: (b * n_seg + s, 4 * p + h))
    return pl.pallas_call(
        body, grid=(A_HEADS, n_batch, n_seg), name=name,
        in_specs=[part(0), part(1), part(2), part(3),
                  pl.BlockSpec((3, 128), lambda h, b, s: (0, h)),
                  pl.BlockSpec((1, 128), lambda h, b, s: (0, h))],
        out_specs=[pl.BlockSpec((ts, 128), lambda h, b, s: (b * n_seg + s, h)),
                   pl.BlockSpec((128, 128), lambda h, b, s: ((b * n_seg + s) * A_HEADS + h, 0))],
        out_shape=[jax.ShapeDtypeStruct((m, A_WIDTH), BF16),
                   jax.ShapeDtypeStruct((n_batch * n_seg * A_HEADS * 128, 128), F32)],
        scratch_shapes=[pltpu.VMEM((128, 128), F32)],
        compiler_params=_params(("arbitrary", "arbitrary", "arbitrary")),
    )(z, z, z, z, logits, hnorm)


def hgrn_bwd(name, z, sprev, logits, hnorm, do, *, n_batch, seq):
    m = n_batch * seq
    ts = min(HGRN_SEG, seq)
    n_seg = seq // ts

    def body(q_ref, f_ref, v_ref, g_ref, sp_ref, lg_ref, hn_ref, do_ref, dq_ref, df_ref, dv_ref, dg_ref, dlg_ref, dhn_ref, dst_ref):
        s = pl.program_id(2)

        @pl.when(s == 0)
        def _():
            dst_ref[...] = jnp.zeros_like(dst_ref)

        res, vjp = jax.vjp(fn_hgrn_seg, q_ref[...], f_ref[...], v_ref[...], g_ref[...], sp_ref[...], lg_ref[...], hn_ref[...])
        dq, df, dv, dg, dst, dlg, dhn = vjp((do_ref[...].astype(res[0].dtype), dst_ref[...]))
        dq_ref[...] = dq.astype(dq_ref.dtype)
        df_ref[...] = df.astype(df_ref.dtype)
        dv_ref[...] = dv.astype(dv_ref.dtype)
        dg_ref[...] = dg.astype(dg_ref.dtype)
        dst_ref[...] = dst
        first = jnp.logical_and(pl.program_id(1) == 0, s == 0)

        @pl.when(first)
        def _():
            dlg_ref[...] = jnp.zeros_like(dlg_ref)
            dhn_ref[...] = jnp.zeros_like(dhn_ref)

        dlg_ref[...] += dlg
        dhn_ref[...] += dhn

    rev = lambda b, s: b * n_seg + (n_seg - 1 - s)
    part = lambda p: pl.BlockSpec((ts, 128), lambda h, b, s: (rev(b, s), 4 * p + h))
    head = pl.BlockSpec((ts, 128), lambda h, b, s: (rev(b, s), h))
    dpart = jax.ShapeDtypeStruct((m, A_WIDTH), BF16)
    return pl.pallas_call(
        body, grid=(A_HEADS, n_batch, n_seg), name=name,
        in_specs=[part(0), part(1), part(2), part(3),
                  pl.BlockSpec((128, 128), lambda h, b, s: (rev(b, s) * A_HEADS + h, 0)),
                  pl.BlockSpec((3, 128), lambda h, b, s: (0, h)),
                  pl.BlockSpec((1, 128), lambda h, b, s: (0, h)),
                  head],
        out_specs=[head, head, head, head,
                   pl.BlockSpec((3, 128), lambda h, b, s: (0, h)),
                   pl.BlockSpec((1, 128), lambda h, b, s: (0, h))],
        out_shape=[dpart, dpart, dpart, dpart,
                   jax.ShapeDtypeStruct(logits.shape, F32),
                   jax.ShapeDtypeStruct(hnorm.shape, F32)],
        scratch_shapes=[pltpu.VMEM((128, 128), F32)],
        compiler_params=_params(("arbitrary", "arbitrary", "arbitrary")),
    )(z, z, z, z, sprev, logits, hnorm, do)


FFN_ROWS = 1024
FFN_LANES = 128


def _ffn_tiles(m, seq):
    tm = min(FFN_ROWS, seq)
    return tm, seq // tm, m // tm


def ffn_mid_fwd(name, hid, cw, cb, layer, *, m, seq):
    tm, n_t, n_i = _ffn_tiles(m, seq)
    hb = tm // HALO

    def body(x_ref, xb_ref, cw_ref, cb_ref, o_ref, c_ref):
        first = pl.program_id(1) % n_t == 0
        for l0 in range(0, FF_BLK, FFN_LANES):
            lanes = slice(l0, min(l0 + FFN_LANES, FF_BLK))
            before = jnp.where(first, 0.0, xb_ref[:, :, lanes])
            ext = jnp.concatenate([before, x_ref[:, :, lanes]], axis=1)
            cg, cv = _ffn_conv(ext[0], ext[1], cw_ref[:, :, lanes], cb_ref[:, :, lanes])
            o_ref[:, lanes] = _ffn_gate(cg, cv).astype(o_ref.dtype)
            c_ref[0, :, lanes] = cg.astype(c_ref.dtype)
            c_ref[1, :, lanes] = cv.astype(c_ref.dtype)

    return pl.pallas_call(
        body, grid=(N_DEV // 2, n_i), name=name,
        in_specs=[pl.BlockSpec((2, None, tm, FF_BLK), lambda d, i: (0, d, i, 0)),
                  pl.BlockSpec((2, None, HALO, FF_BLK), lambda d, i: (0, d, jnp.maximum(i * hb - 1, 0), 0)),
                  pl.BlockSpec((2, None, None, FFN_CONV, FF_BLK), lambda d, i: (0, d, layer, 0, 0)),
                  pl.BlockSpec((None, 2, None, 1, FF_BLK), lambda d, i: (layer, 0, d, 0, 0))],
        out_specs=[pl.BlockSpec((None, tm, FF_BLK), lambda d, i: (d, i, 0)),
                   pl.BlockSpec((2, None, tm, FF_BLK), lambda d, i: (0, d, i, 0))],
        out_shape=[jax.ShapeDtypeStruct((N_DEV // 2, m, FF_BLK), BF16),
                   jax.ShapeDtypeStruct((2, N_DEV // 2, m, FF_BLK), BF16)],
        compiler_params=_params(("parallel", "parallel")),
    )(hid, hid, cw, cb)


def ffn_mid_bwd(name, hid, conv, cw, dact, layer, *, m, seq):
    tm, n_t, n_i = _ffn_tiles(m, seq)
    hb = tm // HALO
    last_blk = m // HALO - 1

    rc = min(FFN_ROWS, tm)
    lane_chunks = [(l0, min(FFN_LANES, FF_BLK - l0)) for l0 in range(0, FF_BLK, FFN_LANES)]

    def body(x_ref, c_ref, ca_ref, cw_ref, da_ref, daa_ref, dx_ref, dcw_ref, dcb_ref, cext_ref, dext_ref):
        i = pl.program_id(1)
        last = i % n_t == n_t - 1
        cext_ref[:, :tm] = c_ref[...]
        cext_ref[:, tm:] = ca_ref[...]
        dext_ref[:tm] = da_ref[...]
        dext_ref[tm:] = jnp.where(last, jnp.zeros_like(daa_ref[...]), daa_ref[...])

        @pl.when(i == 0)
        def _():
            dcw_ref[...] = jnp.zeros_like(dcw_ref)
            dcb_ref[...] = jnp.zeros_like(dcb_ref)

        for l0, lw in lane_chunks:
            lanes = slice(l0, l0 + lw)

            def chunk(c, sums, lanes=lanes, lw=lw):
                r0 = pl.multiple_of(c * rc, rc)
                ext = pl.ds(r0, rc + HALO)
                cg, cv = cext_ref[0, ext, lanes].astype(F32), cext_ref[1, ext, lanes].astype(F32)
                _, vjp_gate = jax.vjp(_ffn_gate, cg, cv)
                dconv = vjp_gate(dext_ref[ext, lanes].astype(F32))
                out = []
                for half in range(2):
                    x = x_ref[half, pl.ds(r0, rc), lanes]
                    dx = None
                    for k in range(FFN_CONV):
                        s = FFN_CONV - 1 - k
                        dc_s = _shift_raw(dconv[half], s, True, 0.0)[:rc]
                        term = cw_ref[half, k:k + 1, lanes] * dc_s
                        dx = term if dx is None else dx + term
                        out.append(sums[len(out)] + jnp.sum(x * dc_s, axis=0, keepdims=True))
                    out.append(sums[len(out)] + jnp.sum(dconv[half][:rc], axis=0, keepdims=True))
                    dx_ref[half, pl.ds(r0, rc), lanes] = dx.astype(dx_ref.dtype)
                return tuple(out)

            zero = jnp.zeros((1, lw), F32)
            sums = lax.fori_loop(0, tm // rc, chunk, (zero,) * (2 * (FFN_CONV + 1)))
            for half in range(2):
                base = half * (FFN_CONV + 1)
                for k in range(FFN_CONV):
                    dcw_ref[half, k:k + 1, lanes] += sums[base + k]
                dcb_ref[half, :, lanes] += sums[base + FFN_CONV]

    return pl.pallas_call(
        body, grid=(N_DEV // 2, n_i), name=name,
        in_specs=[pl.BlockSpec((2, None, tm, FF_BLK), lambda d, i: (0, d, i, 0)),
                  pl.BlockSpec((2, None, tm, FF_BLK), lambda d, i: (0, d, i, 0)),
                  pl.BlockSpec((2, None, HALO, FF_BLK), lambda d, i: (0, d, jnp.minimum((i + 1) * hb, last_blk), 0)),
                  pl.BlockSpec((2, None, None, FFN_CONV, FF_BLK), lambda d, i: (0, d, layer, 0, 0)),
                  pl.BlockSpec((None, tm, FF_BLK), lambda d, i: (d, i, 0)),
                  pl.BlockSpec((None, HALO, FF_BLK), lambda d, i: (d, jnp.minimum((i + 1) * hb, last_blk), 0))],
        out_specs=[pl.BlockSpec((2, None, tm, FF_BLK), lambda d, i: (0, d, i, 0)),
                   pl.BlockSpec((2, None, FFN_CONV, FF_BLK), lambda d, i: (0, d, 0, 0)),
                   pl.BlockSpec((2, None, 1, FF_BLK), lambda d, i: (0, d, 0, 0))],
        out_shape=[jax.ShapeDtypeStruct((2, N_DEV // 2, m, FF_BLK), BF16),
                   jax.ShapeDtypeStruct((2, N_DEV // 2, FFN_CONV, FF_BLK), F32),
                   jax.ShapeDtypeStruct((2, N_DEV // 2, 1, FF_BLK), F32)],
        scratch_shapes=[pltpu.VMEM((2, tm + HALO, FF_BLK), BF16), pltpu.VMEM((tm + HALO, FF_BLK), BF16)],
        compiler_params=_params(("arbitrary", "arbitrary")),
    )(hid, conv, conv, cw, dact, dact)


ATT_BLK = 512
ATT_BLK_FWD = 1024
N_PAIR = C_HEADS // 2
TERM_W = C_HEADS * LANES


def term_placement():
    import numpy as np
    place = np.zeros((3, LANES, TERM_W), np.float32)
    ones_q = np.zeros((1, TERM_W), np.float32)
    ones_k = np.zeros((1, TERM_W), np.float32)
    for h in range(C_HEADS):
        for j in range(3):
            place[j, h, h * LANES + C_HEAD_DIM + j] = 1.0
            ones_q[0, h * LANES + C_HEAD_DIM + 3 + j] = 1.0
            ones_k[0, h * LANES + C_HEAD_DIM + j] = 1.0
    return (jnp.asarray(place.reshape(3 * LANES, TERM_W), BF16), jnp.asarray(ones_q, F32), jnp.asarray(ones_k, F32))


def fn_fox_terms(c, place, ones_q, ones_k):
    parts = _split3(c)
    placed = sum(_dg(parts[j], place[j * LANES:(j + 1) * LANES], "nn") for j in range(3))
    return (placed + ones_q).astype(BF16), (ones_k - pltpu.roll(placed, 3, 1)).astype(BF16)


def _head_tile(z, terms, e):
    lane = lax.broadcasted_iota(jnp.int32, z.shape, 1)
    base = z if e == 0 else pltpu.roll(z, C_HEAD_DIM, 1)
    return jnp.where(lane < C_HEAD_DIM, base, terms.astype(z.dtype))


def _head_only(z, e):
    lane = lax.broadcasted_iota(jnp.int32, z.shape, 1)
    mine = (lane < C_HEAD_DIM) if e == 0 else (lane >= C_HEAD_DIM)
    return jnp.where(mine, z, jnp.zeros_like(z)).astype(BF16)


def _pair_tile(a0, a1):
    lane = lax.broadcasted_iota(jnp.int32, a0.shape, 1)
    return jnp.where(lane < C_HEAD_DIM, a0, pltpu.roll(a1, C_HEAD_DIM, 1))


def _lane_col(a, k):
    lane = lax.broadcasted_iota(jnp.int32, a.shape, 1)
    return jnp.sum(jnp.where(lane == k, a, 0.0), axis=1, keepdims=True)


def _causal(s):
    key = lax.broadcasted_iota(jnp.int32, s.shape, 0)
    qry = lax.broadcasted_iota(jnp.int32, s.shape, 1)
    return qry >= key


def fox_pair_fwd(name, z, qterm, kterm, *, n_batch, seq):
    m = n_batch * seq
    blk = min(ATT_BLK_FWD, seq)
    nq = seq // blk
    dh = C_HEAD_DIM

    def body(zq_ref, zk_ref, zv_ref, qt_ref, kt_ref, o_ref, lse_ref, ka_ref, vt_ref):
        qi = pl.program_id(2)

        @pl.when(qi == 0)
        def _():
            zk = zk_ref[...]
            for e in range(2):
                ka_ref[e] = _head_tile(zk, kt_ref[:, e * LANES:(e + 1) * LANES], e).astype(BF16)
            for cb in range(nq):
                vt_ref[cb] = zv_ref[cb * blk:(cb + 1) * blk, :].T.astype(BF16)

        zq = zq_ref[...] * dh ** -0.5
        qa = [_head_tile(zq, qt_ref[:, e * LANES:(e + 1) * LANES], e).astype(BF16) for e in range(2)]

        def block(j, carry, diagonal):
            rows = pl.ds(pl.multiple_of(j * blk, blk), blk)
            out = []
            for e in range(2):
                mx, l, acc = carry[e]
                s = _dg(ka_ref[e, rows, :], qa[e], "nt")
                if diagonal:
                    s = jnp.where(_causal(s), s, NEG)
                mx_new = jnp.maximum(mx, jnp.max(s, axis=0, keepdims=True))
                p = jnp.exp(s - mx_new)
                alpha = jnp.exp(mx - mx_new)
                l = alpha * l + jnp.sum(p, axis=0, keepdims=True)
                acc = alpha * acc + _dg(vt_ref[j, e * dh:(e + 1) * dh, :], p, "nn")
                out.append((mx_new, l, acc))
            return tuple(out)

        one = (jnp.full((1, blk), NEG, F32), jnp.zeros((1, blk), F32), jnp.zeros((dh, blk), F32))
        carry = lax.fori_loop(0, qi, lambda j, cr: block(j, cr, False), (one, one))
        res = block(qi, carry, True)
        ot = jnp.concatenate([res[e][2] / res[e][1] for e in range(2)], axis=0)
        o_ref[...] = ot.T.astype(o_ref.dtype)
        for e in range(2):
            lse_ref[e] = res[e][0] + jnp.log(res[e][1])

    col = lambda part: (lambda b, g, i: (b, part * N_PAIR + g))
    return pl.pallas_call(
        body, grid=(n_batch, N_PAIR, nq), name=name,
        in_specs=[pl.BlockSpec((blk, LANES), lambda b, g, i: (b * nq + i, g)),
                  pl.BlockSpec((seq, LANES), col(1)),
                  pl.BlockSpec((seq, LANES), col(2)),
                  pl.BlockSpec((blk, 2 * LANES), lambda b, g, i: (b * nq + i, g)),
                  pl.BlockSpec((seq, 2 * LANES), lambda b, g, i: (b, g))],
        out_specs=[pl.BlockSpec((blk, LANES), lambda b, g, i: (b * nq + i, g)),
                   pl.BlockSpec((None, None, None, 2, 1, blk), lambda b, g, i: (b, g, i, 0, 0, 0))],
        out_shape=[jax.ShapeDtypeStruct((m, D_MODEL), BF16), jax.ShapeDtypeStruct((n_batch, N_PAIR, nq, 2, 1, blk), F32)],
        scratch_shapes=[pltpu.VMEM((2, seq, LANES), BF16), pltpu.VMEM((nq, LANES, blk), BF16)],
        compiler_params=_params(("parallel", "parallel", "arbitrary")),
    )(z, z, z, qterm, kterm)


def fox_pair_bwd(name, z, qterm, kterm, o, do, lse, *, n_batch, seq):
    m = n_batch * seq
    blk = min(ATT_BLK, seq)
    nq = seq // blk
    dh = C_HEAD_DIM

    def body(zq_ref, zk_ref, zv_ref, qt_ref, kt_ref, o_ref, do_ref, lse_ref, dq_ref, dk_ref, dv_ref, dc_ref,
             qa_ref, doh_ref, del_ref, dqt_ref, dk_acc, dv_acc):
        g, j = pl.program_id(1), pl.program_id(2)
        lane = lax.broadcasted_iota(jnp.int32, (blk, LANES), 1)

        @pl.when(jnp.logical_and(g == 0, j == 0))
        def _():
            dc_ref[...] = jnp.zeros_like(dc_ref)

        @pl.when(j == 0)
        def _():
            zq = zq_ref[...] * dh ** -0.5
            dov = do_ref[...]
            for e in range(2):
                qa_ref[e] = _head_tile(zq, qt_ref[:, e * LANES:(e + 1) * LANES], e).astype(BF16)
                doh_ref[e] = _head_only(dov, e)
            for cb in range(nq):
                rows = slice(cb * blk, (cb + 1) * blk)
                prod_t = (do_ref[rows, :].astype(F32) * o_ref[rows, :].astype(F32)).T
                for e in range(2):
                    del_ref[cb, e] = jnp.sum(prod_t[e * dh:(e + 1) * dh], axis=0, keepdims=True)
            dqt_ref[...] = jnp.zeros_like(dqt_ref)

        zk, zv = zk_ref[...], zv_ref[...]
        ka32 = [_head_tile(zk, kt_ref[:, e * LANES:(e + 1) * LANES], e) for e in range(2)]
        ka = [t.astype(BF16) for t in ka32]
        kat = [t.T.astype(BF16) for t in ka32]
        vh = [_head_only(zv, e) for e in range(2)]
        dk_acc[...] = jnp.zeros_like(dk_acc)
        dv_acc[...] = jnp.zeros_like(dv_acc)

        def block(i, diagonal):
            rows = pl.ds(pl.multiple_of(i * blk, blk), blk)
            for e in range(2):
                qv, dov = qa_ref[e, rows, :], doh_ref[e, rows, :]
                p = jnp.exp(_dg(ka[e], qv, "nt") - lse_ref[i, e])
                if diagonal:
                    p = jnp.where(_causal(p), p, 0.0)
                dv_acc[...] += _dg(p, dov, "nn")
                ds = p * (_dg(vh[e], dov, "nt") - del_ref[i, e])
                dk_acc[e] += _dg(ds, qv, "nn")
                dqt_ref[i, e] += _dg(kat[e], ds, "nn")

        block(j, True)

        def rest(i, carry):
            block(i, False)
            return carry

        lax.fori_loop(j + 1, nq, rest, 0)
        dk0, dk1 = dk_acc[0], dk_acc[1]
        dk_ref[...] = _pair_tile(dk0, dk1).astype(dk_ref.dtype)
        dv_ref[...] = dv_acc[...].astype(dv_ref.dtype)
        rows_j = pl.ds(pl.multiple_of(j * blk, blk), blk)
        for e, dke in enumerate((dk0, dk1)):
            dc_ref[rows_j, :] -= jnp.where(lane == 2 * g + e, _lane_col(dke, dh + 3), 0.0)

        @pl.when(j == nq - 1)
        def _():
            for i in range(nq):
                nat = [dqt_ref[i, e].T for e in range(2)]
                rows = slice(i * blk, (i + 1) * blk)
                dq_ref[rows, :] = (_pair_tile(nat[0], nat[1]) * dh ** -0.5).astype(dq_ref.dtype)
                for e in range(2):
                    dc_ref[rows, :] += jnp.where(lane == 2 * g + e, _lane_col(nat[e], dh), 0.0)

    col = lambda part: (lambda b, g, j: (b, part * N_PAIR + g))
    colj = lambda part: (lambda b, g, j: (b * nq + j, part * N_PAIR + g))
    pair = jax.ShapeDtypeStruct((m, D_MODEL), BF16)
    return pl.pallas_call(
        body, grid=(n_batch, N_PAIR, nq), name=name,
        in_specs=[pl.BlockSpec((seq, LANES), col(0)),
                  pl.BlockSpec((blk, LANES), colj(1)),
                  pl.BlockSpec((blk, LANES), colj(2)),
                  pl.BlockSpec((seq, 2 * LANES), lambda b, g, j: (b, g)),
                  pl.BlockSpec((blk, 2 * LANES), lambda b, g, j: (b * nq + j, g)),
                  pl.BlockSpec((seq, LANES), col(0)),
                  pl.BlockSpec((seq, LANES), col(0)),
                  pl.BlockSpec((None, None, nq, 2, 1, blk), lambda b, g, j: (b, g, 0, 0, 0, 0))],
        out_specs=[pl.BlockSpec((seq, LANES), col(0)),
                   pl.BlockSpec((blk, LANES), colj(0)),
                   pl.BlockSpec((blk, LANES), colj(0)),
                   pl.BlockSpec((seq, LANES), lambda b, g, j: (b, 0))],
        out_shape=[pair, pair, pair, jax.ShapeDtypeStruct((m, LANES), F32)],
        scratch_shapes=[pltpu.VMEM((2, seq, LANES), BF16), pltpu.VMEM((2, seq, LANES), BF16),
                        pltpu.VMEM((nq, 2, 1, blk), F32), pltpu.VMEM((nq, 2, LANES, blk), F32),
                        pltpu.VMEM((2, blk, LANES), F32), pltpu.VMEM((blk, LANES), F32)],
        compiler_params=_params(("arbitrary", "arbitrary", "arbitrary")),
    )(z, z, z, qterm, kterm, o, do, lse)


def _split3(c):
    c1 = c.astype(BF16)
    r1 = c - c1.astype(F32)
    c2 = r1.astype(BF16)
    c3 = (r1 - c2.astype(F32)).astype(BF16)
    return c1, c2, c3


def _mesh_pos():
    return lax.axis_index("x"), lax.axis_index("y"), lax.axis_index("c")


def _flip(v, bit):
    return 1 - v if bit else v


def all_gather(name, blocks):
    n = len(blocks)

    def body(*refs):
        x_refs, out_refs = refs[:n], refs[n:2 * n]
        send_sems, recv_sems, local_sems = refs[2 * n:]
        x, y, c = _mesh_pos()
        me, sibling = (x, y, c), (x, y, 1 - c)
        chips = [(1 - x, y), (x, 1 - y), (1 - x, 1 - y)]

        def slot(a, px, py, pc):
            return out_refs[a].at[4 * px + 2 * py + pc]

        def copy(a, k, blk, to, src=None):
            return pltpu.make_async_remote_copy(
                src_ref=slot(a, *blk) if src is None else src, dst_ref=slot(a, *blk),
                send_sem=send_sems.at[a, k], recv_sem=recv_sems.at[a, k], device_id=to, device_id_type=MESH)

        mine = [pltpu.make_async_copy(x_refs[a], slot(a, *me), local_sems.at[a]) for a in range(n)]
        for cp in mine:
            cp.start()
        sends = []
        for a in range(n):
            sends.append(copy(a, 0, me, sibling, src=x_refs[a]))
            sends += [copy(a, 1 + j, me, (*chip, c), src=x_refs[a]) for j, chip in enumerate(chips)]
        for cp in sends:
            cp.start()
        for j, chip in enumerate(chips):
            for a in range(n):
                copy(a, 1 + j, (*chip, c), me).wait_recv()
                passed = copy(a, 4 + j, (*chip, c), sibling)
                passed.start()
                sends.append(passed)
        for a in range(n):
            copy(a, 0, sibling, me).wait_recv()
            for j, chip in enumerate(chips):
                copy(a, 4 + j, (*chip, 1 - c), me).wait_recv()
        for cp in sends:
            cp.wait_send()
        for cp in mine:
            cp.wait()

    hbm = pl.BlockSpec(memory_space=pl.ANY)
    return pl.pallas_call(
        body, name=name, out_shape=[jax.ShapeDtypeStruct((N_DEV,) + b.shape, b.dtype) for b in blocks],
        in_specs=[hbm] * n, out_specs=[hbm] * n,
        scratch_shapes=[pltpu.SemaphoreType.DMA((n, 7)), pltpu.SemaphoreType.DMA((n, 7)), pltpu.SemaphoreType.DMA((n,))],
    )(*blocks)


def _peers(x, y, c):
    return [(_flip(x, k & 4), _flip(y, k & 2), _flip(c, k & 1)) for k in range(1, N_DEV)]


def gather_start(name, blocks, lands):
    n = len(blocks)

    def body(*refs):
        x_refs, land_refs = refs[:n], refs[n:2 * n]
        send_sems, recv_sems = refs[2 * n], refs[2 * n + 1]
        token = refs[-1]
        x, y, c = _mesh_pos()
        me = 4 * x + 2 * y + c
        for k, peer in enumerate(_peers(x, y, c)):
            for a in range(n):
                pltpu.make_async_remote_copy(
                    src_ref=x_refs[a], dst_ref=land_refs[a].at[me], send_sem=send_sems.at[7 * a + k], recv_sem=recv_sems.at[7 * a + k],
                    device_id=peer, device_id_type=MESH).start()
        token[...] = jnp.zeros_like(token)

    hbm = pl.BlockSpec(memory_space=pltpu.HBM)
    sem = pl.BlockSpec(memory_space=pltpu.SEMAPHORE)
    out_shape = ([pltpu.SemaphoreType.DMA((7 * n,)), pltpu.SemaphoreType.DMA((7 * n,))]
                 + [pltpu.HBM(b.shape, b.dtype) for b in blocks] + [pltpu.HBM(l.shape, l.dtype) for l in lands]
                 + [jax.ShapeDtypeStruct((8, LANES), F32)])
    res = pl.pallas_call(
        body, name=name, out_shape=out_shape, in_specs=[hbm] * (2 * n),
        out_specs=[sem, sem] + [hbm] * (2 * n) + [pl.BlockSpec(memory_space=pltpu.VMEM)],
        input_output_aliases={a: 2 + a for a in range(2 * n)},
        compiler_params=pltpu.CompilerParams(has_side_effects=pltpu.SideEffectType.DATAFLOW_SIDE_EFFECTING),
    )(*[pltpu.with_memory_space_constraint(b, pltpu.HBM) for b in blocks],
      *[pltpu.with_memory_space_constraint(l, pltpu.HBM) for l in lands])
    return res[0], res[1], res[2:2 + n], res[2 + n:2 + 2 * n], res[-1]


def gather_wait(name, send_sems, recv_sems, blocks, lands, after):
    n = len(blocks)

    def body(*refs):
        x_refs, land_refs = refs[:n], refs[n:2 * n]
        s_sems, r_sems = refs[2 * n], refs[2 * n + 1]
        x, y, c = _mesh_pos()
        me = 4 * x + 2 * y + c
        for k, peer in enumerate(_peers(x, y, c)):
            for a in range(n):
                cp = pltpu.make_async_remote_copy(
                    src_ref=x_refs[a], dst_ref=land_refs[a].at[me], send_sem=s_sems.at[7 * a + k], recv_sem=r_sems.at[7 * a + k],
                    device_id=peer, device_id_type=MESH)
                cp.wait_send()
                cp.wait_recv()

    hbm = pl.BlockSpec(memory_space=pltpu.HBM)
    sem = pl.BlockSpec(memory_space=pltpu.SEMAPHORE)
    res = pl.pallas_call(
        body, name=name,
        out_shape=[pltpu.HBM(b.shape, b.dtype) for b in blocks] + [pltpu.HBM(l.shape, l.dtype) for l in lands],
        in_specs=[hbm] * (2 * n) + [sem, sem, pl.BlockSpec(memory_space=pl.ANY)], out_specs=[hbm] * (2 * n),
        input_output_aliases={a: a for a in range(2 * n)},
        compiler_params=pltpu.CompilerParams(has_side_effects=pltpu.SideEffectType.DATAFLOW_SIDE_EFFECTING),
    )(*blocks, *lands, send_sems, recv_sems, after)
    return res[n:]


def _split_exchange(name, sends, lands, sems, after):
    n = len(sends)
    starting = sems is None

    def body(*refs):
        s_refs, l_refs = refs[:n], refs[n:2 * n]
        send_sems, recv_sems = refs[2 * n], refs[2 * n + 1]
        x, y, c = _mesh_pos()
        me = 4 * x + 2 * y + c
        for k, (px, py, pc) in enumerate(_peers(x, y, c)):
            for a in range(n):
                cp = pltpu.make_async_remote_copy(
                    src_ref=s_refs[a].at[4 * px + 2 * py + pc], dst_ref=l_refs[a].at[me],
                    send_sem=send_sems.at[7 * a + k], recv_sem=recv_sems.at[7 * a + k],
                    device_id=(px, py, pc), device_id_type=MESH)
                if starting:
                    cp.start()
                else:
                    cp.wait_send()
                    cp.wait_recv()
        if starting:
            refs[-1][...] = jnp.zeros_like(refs[-1])

    hbm = pl.BlockSpec(memory_space=pltpu.HBM)
    sem = pl.BlockSpec(memory_space=pltpu.SEMAPHORE)
    thru = [pltpu.HBM(t.shape, t.dtype) for t in list(sends) + list(lands)]
    effect = pltpu.CompilerParams(has_side_effects=pltpu.SideEffectType.DATAFLOW_SIDE_EFFECTING)
    if starting:
        res = pl.pallas_call(
            body, name=name, in_specs=[hbm] * (2 * n),
            out_shape=[pltpu.SemaphoreType.DMA((7 * n,)), pltpu.SemaphoreType.DMA((7 * n,))] + thru + [jax.ShapeDtypeStruct((8, LANES), F32)],
            out_specs=[sem, sem] + [hbm] * (2 * n) + [pl.BlockSpec(memory_space=pltpu.VMEM)],
            input_output_aliases={a: 2 + a for a in range(2 * n)}, compiler_params=effect,
        )(*[pltpu.with_memory_space_constraint(t, pltpu.HBM) for t in list(sends) + list(lands)])
        return res[0], res[1], res[2:2 + n], res[2 + n:2 + 2 * n], res[-1]
    res = pl.pallas_call(
        body, name=name, out_shape=thru, in_specs=[hbm] * (2 * n) + [sem, sem, pl.BlockSpec(memory_space=pl.ANY)],
        out_specs=[hbm] * (2 * n), input_output_aliases={a: a for a in range(2 * n)}, compiler_params=effect,
    )(*sends, *lands, sems[0], sems[1], after)
    return res[n:]


def unwritten(name, like):
    def body(*refs):
        pass

    hbm = pl.BlockSpec(memory_space=pl.ANY)
    return pl.pallas_call(body, name=name, out_shape=[jax.ShapeDtypeStruct(t.shape, t.dtype) for t in like],
                          out_specs=[hbm] * len(like))()


def own_slot_only(send, land, me):
    mine = lax.dynamic_index_in_dim(send, me, 0, keepdims=False)
    return lax.dynamic_update_index_in_dim(land, mine, me, 0)


def all_to_all(name, sends):
    n = len(sends)

    def body(*refs):
        s_refs, r_refs = refs[:n], refs[n:2 * n]
        send_sems, recv_sems, local_sems = refs[2 * n:]
        x, y, c = _mesh_pos()
        me = 4 * x + 2 * y + c
        mine = [pltpu.make_async_copy(s_refs[a].at[me], r_refs[a].at[me], local_sems.at[a]) for a in range(n)]
        for cp in mine:
            cp.start()
        copies = []
        for k in range(1, N_DEV):
            px, py, pc = _flip(x, k & 4), _flip(y, k & 2), _flip(c, k & 1)
            for a in range(n):
                copies.append(pltpu.make_async_remote_copy(
                    src_ref=s_refs[a].at[4 * px + 2 * py + pc], dst_ref=r_refs[a].at[me],
                    send_sem=send_sems.at[a, k - 1], recv_sem=recv_sems.at[a, k - 1],
                    device_id=(px, py, pc), device_id_type=MESH))
        for cp in copies:
            cp.start()
        for cp in copies:
            cp.wait_recv()
        for cp in copies:
            cp.wait_send()
        for cp in mine:
            cp.wait()

    hbm = pl.BlockSpec(memory_space=pl.ANY)
    return pl.pallas_call(
        body, name=name, out_shape=[jax.ShapeDtypeStruct(s.shape, s.dtype) for s in sends],
        in_specs=[hbm] * n, out_specs=[hbm] * n,
        scratch_shapes=[pltpu.SemaphoreType.DMA((n, 7)), pltpu.SemaphoreType.DMA((n, 7)), pltpu.SemaphoreType.DMA((n,))],
    )(*sends)


def _row_tile(r, cap, step):
    return next((t for t in range(cap, step - 1, -step) if r % t == 0), r)


def _sum_parts(p, n):
    t = [p[k].astype(F32) for k in range(n)]
    while len(t) > 1:
        t = [t[k] + t[k + 1] for k in range(0, len(t), 2)]
    return t[0]


def _adam(g, w, m, v):
    m = ADAM_B1 * m + (1.0 - ADAM_B1) * g
    v = ADAM_B2 * v + (1.0 - ADAM_B2) * (g * g)
    m_hat = m / (1.0 - ADAM_B1 ** ADAM_STEP)
    v_hat = v / (1.0 - ADAM_B2 ** ADAM_STEP)
    return -ADAM_LR * (m_hat / (jnp.sqrt(v_hat) + ADAM_EPS) + ADAM_WD * w), m, v


def adam_tiled(name, partials, w, m_, v_, layer=0, prev=None):
    _, r, c = w.shape
    n_part = partials.shape[0]
    tr = _row_tile(r, 256, 16)

    def body(*refs):
        p_ref, w_ref, m_ref, v_ref = refs[:4]
        g_ref, d_ref, nm_ref, nv_ref = refs[-4:]
        g = _sum_parts(p_ref, n_part)
        g_ref[...] = g
        d_ref[...], nm_ref[...], nv_ref[...] = _adam(g, w_ref[...], m_ref[...], v_ref[...])

    spec = pl.BlockSpec((None, tr, c), lambda i: (layer, i, 0))
    in_specs = [pl.BlockSpec((n_part, None, tr, c), lambda i: (0, 0, i, 0)), spec, spec, spec]
    args = [partials, w, m_, v_]
    aliases = {}
    if prev is not None:
        in_specs += [pl.BlockSpec(memory_space=pl.ANY)] * 4
        args += list(prev)
        aliases = {4 + k: k for k in range(4)}
    return pl.pallas_call(
        body, grid=(r // tr,), name=name, in_specs=in_specs,
        out_specs=[spec] * 4, out_shape=[jax.ShapeDtypeStruct(w.shape, F32)] * 4,
        input_output_aliases=aliases, compiler_params=_params(("parallel",)),
    )(*args)


def adam_small(name, items, extra):
    n, ne = len(items), len(extra)

    def body(*refs):
        ins, outs = refs[:4 * n + ne], refs[4 * n + ne:]
        for a in range(n):
            p_ref, w_ref, m_ref, v_ref = ins[4 * a:4 * a + 4]
            g = _sum_parts(p_ref, N_DEV)
            outs[4 * a][...] = g
            outs[4 * a + 1][...], outs[4 * a + 2][...], outs[4 * a + 3][...] = _adam(g, w_ref[...], m_ref[...], v_ref[...])
        for e in range(ne):
            outs[4 * n + e][...] = _sum_parts(ins[4 * n + e], N_DEV)

    args, out_shape = [], []
    for p, w, m_, v_ in items:
        args += [p, w, m_, v_]
        out_shape += [jax.ShapeDtypeStruct(w.shape, F32)] * 4
    for e in extra:
        args.append(e)
        out_shape.append(jax.ShapeDtypeStruct(e.shape[1:], F32))
    vmem = pl.BlockSpec(memory_space=pltpu.VMEM)
    res = pl.pallas_call(body, name=name, in_specs=[vmem] * len(args), out_specs=[vmem] * len(out_shape), out_shape=out_shape)(*args)
    return [res[4 * a:4 * a + 4] for a in range(n)], res[4 * n:]


def _cols_from_gather(g):
    g = jnp.moveaxis(g, 0, -2)
    return g.reshape(g.shape[:-2] + (g.shape[-2] * g.shape[-1],))


def _cols_to_blocks(w):
    w = w.reshape(w.shape[:-1] + (N_DEV, w.shape[-1] // N_DEV))
    return jnp.moveaxis(w, -2, 0)


def _block_diag(w):
    pairs = w.reshape(B_BLOCKS // 2, 2, B_BLOCK_DIM, 1, B_BLOCK_DIM)
    same = jnp.eye(2, dtype=bool).reshape(1, 2, 1, 2, 1)
    return jnp.where(same, pairs, 0.0).reshape(B_BLOCKS // 2 * LANES, LANES)


def _block_diag_grad(d):
    parts = d.reshape(B_BLOCKS // 2, 2, B_BLOCK_DIM, 2, B_BLOCK_DIM)
    same = jnp.eye(2, dtype=bool).reshape(1, 2, 1, 2, 1)
    return jnp.sum(jnp.where(same, parts, 0.0), axis=3).reshape(B_BLOCKS, B_BLOCK_DIM, B_BLOCK_DIM)


NAMES = ("norm_gains", "even_w_in", "hgrn_lb_logits", "hgrn_norm", "rg_conv_w", "rg_conv_b", "rg_wa", "rg_ba", "rg_wx", "rg_bx",
         "rg_lambda", "even_w_out", "odd_w_in", "fox_f_bias", "odd_w_out", "ffn_w_up", "ffn_conv_w", "ffn_conv_b", "ffn_w_down")
SMALL_SHARDED = ("norm_gains", "rg_conv_w", "ffn_conv_w")
REPLICATED = ("hgrn_lb_logits", "hgrn_norm", "rg_conv_b", "rg_wa", "rg_ba", "rg_wx", "rg_bx", "rg_lambda", "fox_f_bias", "ffn_conv_b")


def _ffn_forward(tag, layer, h, w_up_g, cw5, cb5, w_down_g, m, seq):
    tm = _div_tile(m, 1024)
    nm = m // tm
    hid = mm(f"{tag}_up", "nn",
             Blk(h, (tm, D_MODEL), lambda i, j, k: (i, 0)),
             Blk(w_up_g, (None, None, D_MODEL, FF_BLK), lambda i, j, k: (j, 0, 0, 0)),
             Blk((N_DEV, m, FF_BLK), (None, tm, FF_BLK), lambda i, j, k: (j, i, 0)), F32, (nm, N_DEV, 1))
    hid = hid.reshape(2, N_DEV // 2, m, FF_BLK)
    act, conv = ffn_mid_fwd(f"{tag}_mid", hid, cw5, cb5, layer, m=m, seq=seq)
    f = mm(f"{tag}_down", "nn",
           Blk(act, (None, tm, FF_BLK), lambda i, j, k: (k, i, 0)),
           Blk(w_down_g, (2, None, FF_BLK // 2, D_MODEL), lambda i, j, k: (k, 0, 0, 0)),
           Blk((m, D_MODEL), (tm, D_MODEL), lambda i, j, k: (i, 0)), F32, (nm, 1, N_DEV // 2))
    return (hid, conv), act, f


def _ffn_backward(tag, layer, df, h, hid, act, w_up_g, cw5, cb5, w_down_g, m, seq):
    tm = _div_tile(m, 1024)
    nm = m // tm
    dact = mm(f"{tag}_dact", "nt",
              Blk(df, (tm, D_MODEL), lambda i, j, k: (i, 0)),
              Blk(w_down_g, (2, None, FF_BLK // 2, D_MODEL), lambda i, j, k: (j, 0, 0, 0)),
              Blk((N_DEV // 2, m, FF_BLK), (None, tm, FF_BLK), lambda i, j, k: (j, i, 0)), BF16, (nm, N_DEV // 2, 1))
    d_wdown = mm(f"{tag}_dwdown", "tn",
                 Blk(act, (None, tm, FF_BLK), lambda i, j, k: (i, k, 0)),
                 Blk(df, (tm, D_MODEL), lambda i, j, k: (k, 0)),
                 Blk(w_down_g.shape, (2, None, FF_BLK // 2, D_MODEL), lambda i, j, k: (i, 0, 0, 0)), BF16,
                 (N_DEV // 2, 1, nm))
    dhid, d_cw, d_cb = ffn_mid_bwd(f"{tag}_dmid", hid[0], hid[1], cw5, dact, layer, m=m, seq=seq)
    dhid = dhid.reshape(N_DEV, m, FF_BLK)
    dh = mm(f"{tag}_dh", "nt",
            Blk(dhid, (None, tm, FF_BLK), lambda i, j, k: (k, i, 0)),
            Blk(w_up_g, (None, None, D_MODEL, FF_BLK), lambda i, j, k: (k, 0, 0, 0)),
            Blk((m, D_MODEL), (tm, D_MODEL), lambda i, j, k: (i, 0)), BF16, (nm, 1, N_DEV))
    d_wup = mm(f"{tag}_dwup", "tn",
               Blk(dhid, (None, tm, FF_BLK), lambda i, j, k: (i, k, 0)),
               Blk(h, (tm, D_MODEL), lambda i, j, k: (k, 0)),
               Blk((N_DEV, 1, FF_BLK, D_MODEL), (None, None, FF_BLK, D_MODEL), lambda i, j, k: (i, 0, 0, 0)), BF16,
               (N_DEV, 1, nm))
    return dh, d_wup, d_cw, d_cb, d_wdown


def kernel(x, norm_gains, even_w_in, hgrn_lb_logits, hgrn_norm, rg_conv_w, rg_conv_b, rg_wa, rg_ba, rg_wx, rg_bx, rg_lambda, even_w_out, odd_w_in, fox_f_bias, odd_w_out, ffn_w_up, ffn_conv_w, ffn_conv_b, ffn_w_down, loss_target, m_norm_gains, m_even_w_in, m_hgrn_lb_logits, m_hgrn_norm, m_rg_conv_w, m_rg_conv_b, m_rg_wa, m_rg_ba, m_rg_wx, m_rg_bx, m_rg_lambda, m_even_w_out, m_odd_w_in, m_fox_f_bias, m_odd_w_out, m_ffn_w_up, m_ffn_conv_w, m_ffn_conv_b, m_ffn_w_down, v_norm_gains, v_even_w_in, v_hgrn_lb_logits, v_hgrn_norm, v_rg_conv_w, v_rg_conv_b, v_rg_wa, v_rg_ba, v_rg_wx, v_rg_bx, v_rg_lambda, v_even_w_out, v_odd_w_in, v_fox_f_bias, v_odd_w_out, v_ffn_w_up, v_ffn_conv_w, v_ffn_conv_b, v_ffn_w_down):
    local = dict(locals())
    w = {n: local[n] for n in NAMES}
    mom = {n: local["m_" + n] for n in NAMES}
    var = {n: local["v_" + n] for n in NAMES}
    n_batch, seq, _ = x.shape
    m = n_batch * seq
    tm = _div_tile(m, 512)
    tmm = _div_tile(m, 1024)
    nm = m // tmm

    gathered = all_gather("gather_weights", [w["even_w_in"].astype(BF16)] + [w[n] for n in SMALL_SHARDED])
    g = dict(zip(("even_w_in",) + SMALL_SHARDED, gathered))
    w_in_e = g["even_w_in"]
    gains = _cols_from_gather(g["norm_gains"])
    me = 4 * lax.axis_index("x") + 2 * lax.axis_index("y") + lax.axis_index("c")
    def own_block_only(name, blocks):
        lands = unwritten(name, [jax.ShapeDtypeStruct((N_DEV,) + t.shape, t.dtype) for t in blocks])
        return [lax.dynamic_update_index_in_dim(ld, t, me, 0) for ld, t in zip(lands, blocks)]

    def own_slots_only(name, sends):
        return [own_slot_only(t, ld, me) for t, ld in zip(sends, unwritten(name, sends))]

    behind = (g["norm_gains"][0, 0, 0, 0] * 0.0).astype(BF16)
    out0 = [w["even_w_out"].astype(BF16) + behind]
    out0_sent = gather_start("gather_out0_start", out0, own_block_only("land_out0", out0))
    behind = (out0_sent[4][0, 0] * 0.0).astype(BF16)
    ffn0 = [w["ffn_w_up"][0:1].astype(BF16) + behind, w["ffn_w_down"][0:1].astype(BF16) + behind]
    ffn0_sent = gather_start("gather_ffn0_start", ffn0, own_block_only("land_ffn0", ffn0))
    behind = (ffn0_sent[4][0, 0] * 0.0).astype(BF16)
    mix1w = [jnp.swapaxes(w["odd_w_in"], 1, 2).astype(BF16) + behind, w["odd_w_out"].astype(BF16) + behind]
    mix1_sent = gather_start("gather_mix1_start", mix1w, own_block_only("land_mix1", mix1w))
    behind = (mix1_sent[4][0, 0] * 0.0).astype(BF16)
    ffn1 = [w["ffn_w_up"][1:2].astype(BF16) + behind, w["ffn_w_down"][1:2].astype(BF16) + behind]
    ffn1_sent = gather_start("gather_ffn1_start", ffn1, own_block_only("land_ffn1", ffn1))
    started = ffn1_sent[4]
    rg_cw = _cols_from_gather(g["rg_conv_w"])[0]
    n_layer = ffn_conv_w.shape[0]
    cw5 = g["ffn_conv_w"].reshape(2, N_DEV // 2, n_layer, FFN_CONV, FF_BLK)
    cb5 = ffn_conv_b.reshape(n_layer, 2, N_DEV // 2, 1, FF_BLK)
    gain = lambda l, k: gains[l, k:k + 1, :]
    wa_bd, wx_bd = _block_diag(rg_wa[0]), _block_diag(rg_wx[0])
    fbias = jnp.pad(fox_f_bias, ((0, 0), (0, LANES - C_HEADS)))

    x0 = x.reshape(m, D_MODEL)
    tgt = loss_target.reshape(m, D_MODEL)

    (h0,) = tile_fwd("l0_prenorm", fn_prenorm_after, m=m, tm=tm, nj=1, rows=[Row(x0)], pars=[Par(gain(0, 0)), Par(started)],
                     outs=[Out(D_MODEL, BF16)])
    z0 = mm("l0_in", "nn",
            Blk(h0, (tmm, D_MODEL), lambda i, j, k: (i, 0)),
            Blk(w_in_e, (2, None, D_MODEL, 384), lambda i, j, k: (j, 0, 0, 0)),
            Blk((m, 3072), (tmm, 768), lambda i, j, k: (i, j)), F32, (nm, N_DEV // 2, 1), b_join=True)
    oa, sprev = hgrn_fwd("l0_hgrn", z0, hgrn_lb_logits, hgrn_norm, n_batch=n_batch, seq=seq)
    rg_rows = lambda: [Row(z0, LANES, 16), Row(z0, LANES, 20)]
    rg_pars = lambda: [Par(rg_cw, "col", LANES), Par(rg_conv_b, "col", LANES), Par(wa_bd, "row", LANES), Par(rg_ba, "col", LANES),
                       Par(wx_bd, "row", LANES), Par(rg_bx, "col", LANES), Par(rg_lambda, "col", LANES)]
    (ob,) = tile_fwd("l0_rglru", fn_rglru, m=m, tm=seq, nj=B_WIDTH // LANES, rows=rg_rows(), pars=rg_pars(),
                     outs=[Out(B_WIDTH, BF16, LANES)])
    mixcat0 = jnp.concatenate([oa, ob], axis=-1)
    (g_out_e,) = gather_wait("gather_out0_wait", out0_sent[0], out0_sent[1], out0_sent[2], out0_sent[3], mixcat0)
    w_out_e = g_out_e.reshape(D_MODEL, D_MODEL)
    mix0 = mm2d("l0_out", "nn", mixcat0, w_out_e)
    x1, h1 = tile_fwd("l0_postnorm", fn_addnorm2, m=m, tm=tm, nj=1, rows=[Row(x0), Row(mix0)], pars=[Par(gain(0, 1)), Par(gain(0, 2))],
                      outs=[Out(D_MODEL, F32), Out(D_MODEL, BF16)])
    w_up_g0, w_down_g0 = gather_wait("gather_ffn0_wait", ffn0_sent[0], ffn0_sent[1], ffn0_sent[2], ffn0_sent[3], h1)
    hid0, act0, f0 = _ffn_forward("l0_ffn", 0, h1, w_up_g0, cw5, cb5, w_down_g0, m, seq)
    x2, h2 = tile_fwd("l0_ffnnorm", fn_addnorm2, m=m, tm=tm, nj=1, rows=[Row(x1), Row(f0)], pars=[Par(gain(0, 3)), Par(gain(1, 0))],
                      outs=[Out(D_MODEL, F32), Out(D_MODEL, BF16)])

    g_in_o, g_out_o = gather_wait("gather_mix1_wait", mix1_sent[0], mix1_sent[1], mix1_sent[2], mix1_sent[3], h2)
    w_in_o_t = jnp.pad(g_in_o.reshape(3088, D_MODEL), ((0, 3200 - 3088), (0, 0)))
    w_out_o = g_out_o.reshape(D_MODEL, D_MODEL)
    z1 = mm2d("l1_in", "nt", h2, w_in_o_t)
    (cgate,) = tile_fwd("l1_gate", fn_fox_gate, m=m, tm=seq, nj=1, rows=[Row(z1, LANES, 3072 // LANES)], pars=[Par(fbias)],
                        outs=[Out(LANES, F32)])
    place, ones_q, ones_k = term_placement()
    qterm, kterm = tile_fwd("l1_terms", fn_fox_terms, m=m, tm=tm, nj=1, rows=[Row(cgate)],
                            pars=[Par(place), Par(ones_q), Par(ones_k)], outs=[Out(TERM_W, BF16), Out(TERM_W, BF16)])
    oc, lse = fox_pair_fwd("l1_attn", z1, qterm, kterm, n_batch=n_batch, seq=seq)
    blk_b = min(ATT_BLK, seq)
    lse = lse.reshape(n_batch, N_PAIR, -1, 2, lse.shape[-1] // blk_b, blk_b).swapaxes(3, 4).reshape(n_batch, N_PAIR, seq // blk_b, 2, 1, blk_b)
    mix1 = mm2d("l1_out", "nn", oc, w_out_o)
    x3, h3 = tile_fwd("l1_postnorm", fn_addnorm2, m=m, tm=tm, nj=1, rows=[Row(x2), Row(mix1)], pars=[Par(gain(1, 1)), Par(gain(1, 2))],
                      outs=[Out(D_MODEL, F32), Out(D_MODEL, BF16)])
    w_up_g1, w_down_g1 = gather_wait("gather_ffn1_wait", ffn1_sent[0], ffn1_sent[1], ffn1_sent[2], ffn1_sent[3], h3)
    hid1, act1, f1 = _ffn_forward("l1_ffn", 1, h3, w_up_g1, cw5, cb5, w_down_g1, m, seq)
    dy, df1, loss_part, d_g13 = loss_head("loss", x3, f1, tgt, gain(1, 3), m=m, tm=tm)
    dh3, d_wup1, d_cw1, d_cb1, d_wdown1 = _ffn_backward("l1_ffn", 1, df1, h3, hid1, act1, w_up_g1, cw5, cb5, w_down_g1, m, seq)
    dx2, dmix1, d_g11, d_g12 = tile_bwd("l1_dpostnorm", fn_addnorm2, m=m, tm=tm, nj=1, rows=[Row(x2), Row(mix1)],
                                        pars=[Par(gain(1, 1)), Par(gain(1, 2))], cts=[Row(dy), Row(dh3)],
                                        drows=[Out(D_MODEL, F32), Out(D_MODEL, BF16)])
    doc = mm2d("l1_doc", "nt", dmix1, w_out_o, BF16)
    d_wout_o = mm2d("l1_dwout", "tn", oc, dmix1)
    dq, dk, dv, dc = fox_pair_bwd("l1_dattn", z1, qterm, kterm, oc, doc, lse, n_batch=n_batch, seq=seq)
    dzf, d_fbias = tile_bwd("l1_dgate", fn_fox_gate, m=m, tm=seq, nj=1, rows=[Row(z1, LANES, 3072 // LANES)], pars=[Par(fbias)],
                            cts=[Row(dc)], drows=[Out(LANES, BF16)])
    dz1 = jnp.concatenate([dq, dk, dv, dzf], axis=-1)
    dh2 = mm2d("l1_dh", "nn", dz1, w_in_o_t, BF16)
    d_win_o_t = mm2d("l1_dwin", "tn", dz1, h2, BF16)

    send1 = [d_win_o_t[:3088].reshape(N_DEV, 1, 3088 // N_DEV, D_MODEL),
             d_wout_o.reshape(N_DEV, 1, D_MODEL // N_DEV, D_MODEL).astype(BF16), d_wup1, d_wdown1]
    sent1 = _split_exchange("exchange_l1_start", send1, own_slots_only("land_l1", send1), None, None)

    dx1, df0, d_g03, d_g10 = tile_bwd("l0_dffnnorm", fn_addnorm2_after, m=m, tm=tm, nj=1, rows=[Row(x1), Row(f0)],
                                      pars=[Par(gain(0, 3)), Par(gain(1, 0)), Par(sent1[4])], cts=[Row(dx2), Row(dh2)],
                                      drows=[Out(D_MODEL, F32), Out(D_MODEL, BF16)])[:4]
    dh1, d_wup0, d_cw0, d_cb0, d_wdown0 = _ffn_backward("l0_ffn", 0, df0, h1, hid0, act0, w_up_g0, cw5, cb5, w_down_g0, m, seq)
    send0 = [d_wup0, d_wdown0]
    sent0 = _split_exchange("exchange_ffn0_start", send0, own_slots_only("land_dffn0", send0), None, None)
    dx0a, dmix0, d_g01, d_g02 = tile_bwd("l0_dpostnorm", fn_addnorm2_after, m=m, tm=tm, nj=1, rows=[Row(x0), Row(mix0)],
                                         pars=[Par(gain(0, 1)), Par(gain(0, 2)), Par(sent0[4])], cts=[Row(dx1), Row(dh1)],
                                         drows=[Out(D_MODEL, F32), Out(D_MODEL, BF16)])[:4]
    dmixcat0 = mm2d("l0_dmixcat", "nt", dmix0, w_out_e, BF16)
    d_wout_e = mm2d("l0_dwout", "tn", mixcat0, dmix0)
    dzq, dzf0, dzv, dzg, d_lb, d_hnorm = hgrn_bwd("l0_dhgrn", z0, sprev, hgrn_lb_logits, hgrn_norm, dmixcat0, n_batch=n_batch, seq=seq)
    dzx, dzy, d_rcw, d_rcb, d_wa, d_ba, d_wx, d_bx, d_lam = tile_bwd(
        "l0_drglru", fn_rglru, m=m, tm=seq, nj=B_WIDTH // LANES, rows=rg_rows(), pars=rg_pars(),
        cts=[Row(dmixcat0, LANES, A_WIDTH // LANES)], drows=[Out(B_WIDTH, BF16, LANES), Out(B_WIDTH, BF16, LANES)])
    dz0 = jnp.concatenate([dzq, dzf0, dzv, dzg, dzx, dzy], axis=-1)
    d_win_e = mm("l0_dwin", "tn",
                 Blk(h0, (tmm, D_MODEL), lambda i, j, k: (k, 0)),
                 Blk(dz0, (tmm, 768), lambda i, j, k: (k, j)),
                 Blk(w_in_e.shape, (2, None, D_MODEL, 384), lambda i, j, k: (j, 0, 0, 0)), BF16, (1, N_DEV // 2, nm), o_split=True)
    send_e = [d_win_e, d_wout_e.reshape(N_DEV, 1, D_MODEL // N_DEV, D_MODEL).astype(BF16)]
    sent_e = _split_exchange("exchange_even_start", send_e, own_slots_only("land_even", send_e), None, None)
    d_ffn_cb = jnp.stack([d_cb0, d_cb1]).reshape(n_layer, 2 * D_FF)
    rep = {"hgrn_lb_logits": d_lb, "hgrn_norm": d_hnorm, "rg_conv_b": d_rcb, "rg_wa": _block_diag_grad(d_wa)[None], "rg_ba": d_ba,
           "rg_wx": _block_diag_grad(d_wx)[None], "rg_bx": d_bx, "rg_lambda": d_lam, "fox_f_bias": d_fbias[:, :C_HEADS],
           "ffn_conv_b": d_ffn_cb}
    rep_blocks = [rep[n] for n in REPLICATED] + [loss_part]
    rep_sent = gather_start("gather_partials_start", rep_blocks, own_block_only("land_partials", rep_blocks))
    dh0 = mm("l0_dh", "nt",
             Blk(dz0, (tmm, 768), lambda i, j, k: (i, k)),
             Blk(w_in_e, (2, None, D_MODEL, 384), lambda i, j, k: (k, 0, 0, 0)),
             Blk((m, D_MODEL), (tmm, D_MODEL), lambda i, j, k: (i, 0)), BF16, (nm, 1, N_DEV // 2), after=sent_e[4] + rep_sent[4],
             b_join=True)
    dx0, d_g00 = tile_bwd("l0_dprenorm", fn_input_norm, m=m, tm=tm, nj=1, rows=[Row(x0)], pars=[Par(gain(0, 0))],
                          cts=[Row(dx0a), Row(dh0)], drows=[Out(D_MODEL, F32)])

    d_gains = jnp.stack([jnp.concatenate([d_g00, d_g01, d_g02, d_g03], axis=0), jnp.concatenate([d_g10, d_g11, d_g12, d_g13], axis=0)])
    d_ffn_cw = jnp.stack([d_cw0, d_cw1], axis=2).reshape(N_DEV, n_layer, FFN_CONV, FF_BLK)
    r_in_o, r_out_o, r_up1, r_down1 = _split_exchange("exchange_l1_wait", sent1[2], sent1[3], sent1[:2], dx0)
    r_up0, r_down0 = _split_exchange("exchange_ffn0_wait", sent0[2], sent0[3], sent0[:2], dx0)
    r_in_e, r_out_e = _split_exchange("exchange_even_wait", sent_e[2], sent_e[3], sent_e[:2], dx0)
    recv, res = {}, {}
    flipped = ("odd_w_in", "ffn_w_up")
    view = lambda n, t: jnp.swapaxes(t, 1, 2) if n in flipped else t
    for n, r in (("even_w_in", r_in_e), ("even_w_out", r_out_e), ("odd_w_in", r_in_o), ("odd_w_out", r_out_o)):
        res[n] = [view(n, t) for t in adam_tiled("adam_" + n, r, view(n, w[n]), view(n, mom[n]), view(n, var[n]))]
    for n, parts_l in (("ffn_w_up", (r_up0, r_up1)), ("ffn_w_down", (r_down0, r_down1))):
        wmv = (view(n, w[n]), view(n, mom[n]), view(n, var[n]))
        first_layer = adam_tiled(f"adam_{n}_0", parts_l[0], *wmv, layer=0)
        res[n] = [view(n, t) for t in adam_tiled(f"adam_{n}_1", parts_l[1], *wmv, layer=1, prev=first_layer)]
    small_send = [_cols_to_blocks(d_gains), _cols_to_blocks(d_rcw[None]), d_ffn_cw]
    recv.update(zip(SMALL_SHARDED, all_to_all("exchange_small", small_send)))

    parts = gather_wait("gather_partials_wait", rep_sent[0], rep_sent[1], rep_sent[2], rep_sent[3], dx0)
    for n, p in zip(REPLICATED, parts):
        recv[n] = p
    small = SMALL_SHARDED + REPLICATED
    small_res, (loss_sum,) = adam_small("adam_small", [(recv[n], w[n], mom[n], var[n]) for n in small], [parts[-1]])
    res.update(dict(zip(small, small_res)))

    out = [loss_sum[0, 0], dx0.reshape(x.shape)]
    for k in range(4):
        out += [res[n][k] for n in NAMES]
    return tuple(out)
```

```python
import functools

import jax
import jax.numpy as jnp
from jax import lax
from jax.experimental import pallas as pl
from jax.experimental.pallas import tpu as pltpu

F32 = jnp.float32
BF16 = jnp.bfloat16

D_MODEL = 1024
A_HEADS = 4
A_WIDTH = 512
HGRN_CHUNK = 64
HGRN_SEG = 2048
B_WIDTH = 512
B_BLOCKS = 8
B_BLOCK_DIM = 64
B_CONV = 4
RG_C = 8.0
C_HEADS = 16
C_HEAD_DIM = 64
D_FF = 2816
FFN_CONV = 3
EPS = 1e-6
LANES = 128
HALO = 16
N_DEV = 8
FF_BLK = 2 * D_FF // N_DEV
MESH = pl.DeviceIdType.MESH
NEG = -1e30
VMEM_LIMIT = 56 * 1024 * 1024
MM_ROWS = 2048

ADAM_LR = 0.001
ADAM_B1 = 0.9
ADAM_B2 = 0.999
ADAM_EPS = 1e-08
ADAM_WD = 0.01
ADAM_STEP = 10


def _dg(a, b, pat):
    nb = a.ndim - 2
    batch = (tuple(range(nb)), tuple(range(nb)))
    ca = a.ndim - 1 if pat[0] == "n" else a.ndim - 2
    cb = b.ndim - 2 if pat[1] == "n" else b.ndim - 1
    return lax.dot_general(a.astype(BF16), b.astype(BF16), (((ca,), (cb,)), batch), preferred_element_type=F32)


@functools.partial(jax.custom_vjp, nondiff_argnums=(2,))
def bdot(a, b, pat):
    return _dg(a, b, pat)


def _bdot_fwd(a, b, pat):
    return _dg(a, b, pat), (a, b)


def _bdot_bwd(pat, res, g):
    a, b = res
    if pat == "nn":
        return _dg(g, b, "nt"), _dg(a, g, "tn")
    if pat == "nt":
        return _dg(g, b, "nn"), _dg(g, a, "tn")
    return _dg(b, g, "nt"), _dg(a, g, "nn")


bdot.defvjp(_bdot_fwd, _bdot_bwd)


def _shift_raw(x, s, up, fill):
    if s == 0:
        return x
    n = x.shape[0]
    r = pltpu.roll(x, (n - s) if up else s, 0)
    idx = lax.broadcasted_iota(jnp.int32, x.shape, 0)
    mask = (idx >= n - s) if up else (idx < s)
    return jnp.where(mask, jnp.asarray(fill, x.dtype), r)


@functools.partial(jax.custom_vjp, nondiff_argnums=(1,))
def shift_down(x, s):
    return _shift_raw(x, s, False, 0.0)


def _shift_down_fwd(x, s):
    return _shift_raw(x, s, False, 0.0), None


def _shift_down_bwd(s, _, g):
    return (_shift_raw(g, s, True, 0.0),)


shift_down.defvjp(_shift_down_fwd, _shift_down_bwd)


def _scan_impl(a, u, up):
    n = a.shape[0]
    s = 1
    while s < n:
        u = a * _shift_raw(u, s, up, 0.0) + u
        if 2 * s < n:
            a = a * _shift_raw(a, s, up, 1.0)
        s *= 2
    return u


@jax.custom_vjp
def lin_scan(a, u):
    return _scan_impl(a, u, False)


def _lin_scan_fwd(a, u):
    h = _scan_impl(a, u, False)
    return h, (a, h)


def _lin_scan_bwd(res, g):
    a, h = res
    gh = _scan_impl(_shift_raw(a, 1, True, 0.0), g, True)
    return gh * _shift_raw(h, 1, False, 0.0), gh


lin_scan.defvjp(_lin_scan_fwd, _lin_scan_bwd)


def _cumsum_impl(x, up, period):
    n = x.shape[0]
    span = n if period is None else period
    idx = lax.broadcasted_iota(jnp.int32, x.shape, 0)
    pos = idx if period is None else idx % period
    s = 1
    while s < span:
        sh = _shift_raw(x, s, up, 0.0)
        if period is not None:
            keep = (pos < period - s) if up else (pos >= s)
            sh = jnp.where(keep, sh, 0.0)
        x = x + sh
        s *= 2
    return x


@functools.partial(jax.custom_vjp, nondiff_argnums=(1,))
def cumsum_rows(x, period):
    return _cumsum_impl(x, False, period)


def _cumsum_fwd(x, period):
    return _cumsum_impl(x, False, period), None


def _cumsum_bwd(period, _, g):
    return (_cumsum_impl(g, True, period),)


cumsum_rows.defvjp(_cumsum_fwd, _cumsum_bwd)


def _sigmoid(x):
    return jax.nn.sigmoid(x)


def _expm1(x):
    return jnp.tanh(0.5 * x) * (jnp.exp(x) + 1.0)


def _softplus(x):
    return jnp.maximum(x, 0.0) + jnp.log(1.0 + jnp.exp(-jnp.abs(x)))


def _rms(x, g):
    return x * lax.rsqrt(jnp.mean(x * x, axis=-1, keepdims=True) + EPS) * g


def fn_prenorm(x, g):
    return (_rms(x, g).astype(BF16),)


def fn_prenorm_after(x, g, _token):
    return fn_prenorm(x, g)


def fn_addnorm2(x, y, g_post, g_pre):
    x1 = x + _rms(y, g_post)
    return x1, _rms(x1, g_pre).astype(BF16)


def fn_addnorm2_after(x, y, g_post, g_pre, _token):
    return fn_addnorm2(x, y, g_post, g_pre)


def fn_input_norm(x, g):
    return x, _rms(x, g).astype(BF16)


def _causal_conv(x, w, b, taps):
    c = b
    for k in range(taps):
        c = c + w[k:k + 1, :] * shift_down(x, taps - 1 - k)
    return c


def fn_rglru(xb, yb, cw, cb, wa, ba, wx, bx, lam):
    xf = _causal_conv(xb, cw, cb, B_CONV)
    r = _sigmoid(bdot(xf, wa, "nn") + ba)
    i = _sigmoid(bdot(xf, wx, "nn") + bx)
    log_a = -RG_C * r * _softplus(-lam)
    a = jnp.exp(log_a)
    u = jnp.sqrt(-_expm1(2.0 * log_a)) * (i * xf)
    h = lin_scan(a, u)
    return ((h * jax.nn.gelu(yb)).astype(BF16),)


def fn_fox_gate(zf, bias):
    return (cumsum_rows(jax.nn.log_sigmoid(zf + bias), None),)


def fn_hgrn_seg(q, fl, v, g, st, logits, hn):
    rows = q.shape[0]
    nc = rows // HGRN_CHUNK
    l0, l1, l2 = logits[0:1, :], logits[1:2, :], logits[2:3, :]
    mx = jnp.maximum(jnp.maximum(l0, l1), l2)
    e0, e1, e2 = jnp.exp(l0 - mx), jnp.exp(l1 - mx), jnp.exp(l2 - mx)
    lb = e0 / (e0 + e1 + e2)
    forget = lb + (1.0 - lb) * _sigmoid(fl)
    qs = q * _sigmoid(q)
    kk = 1.0 - forget
    logf = jnp.log(forget)
    bcum = cumsum_rows(logf, HGRN_CHUNK)
    c3 = lambda t: t.reshape(nc, HGRN_CHUNK, 128)
    b_last = jnp.sum(c3(logf), axis=1, keepdims=True)
    bcum3 = c3(bcum)
    q_dec = c3(qs) * jnp.exp(bcum3)
    k_dec = c3(kk) * jnp.exp(-bcum3)
    k_upd = c3(kk) * jnp.exp(b_last - bcum3)
    v3 = c3(v)
    scores = bdot(q_dec, k_dec, "nt")
    ri = lax.broadcasted_iota(jnp.int32, scores.shape, 1)
    ci = lax.broadcasted_iota(jnp.int32, scores.shape, 2)
    scores = jnp.where(ri >= ci, scores, 0.0)
    o = bdot(scores, v3, "nn")
    upd_t = bdot(v3, k_upd, "tn")
    dec = jnp.exp(b_last)
    prev = []
    for n in range(nc):
        prev.append(st)
        st = st * dec[n] + upd_t[n]
    o = o + bdot(q_dec, jnp.stack(prev), "nt")
    o = o.reshape(rows, 128)
    o = o * lax.rsqrt(jnp.mean(o * o, axis=-1, keepdims=True) + EPS) * hn
    return (o * _sigmoid(g)).astype(BF16), st


def _ffn_conv(xg, xv, cw, cb):
    cg = _causal_conv(xg, cw[0], cb[0], FFN_CONV)[HALO:]
    cv = _causal_conv(xv, cw[1], cb[1], FFN_CONV)[HALO:]
    return cg, cv


def _ffn_gate(cg, cv):
    return jax.nn.gelu(cg) * cv


class Row:
    def __init__(self, arr, cb=None, off=0):
        self.arr, self.cb, self.off = arr, cb, off

    def spec(self, tm):
        if self.cb is None:
            return pl.BlockSpec((tm, self.arr.shape[1]), lambda j, i: (i, 0))
        off = self.off
        return pl.BlockSpec((tm, self.cb), lambda j, i: (i, j + off))


class Par:
    def __init__(self, arr, kind="full", bs=None):
        self.arr, self.kind, self.bs = arr, kind, bs

    def block(self):
        if self.kind == "full":
            return self.arr.shape
        if self.kind == "col":
            return (self.arr.shape[0], self.bs)
        return (self.bs, self.arr.shape[1])

    def spec(self):
        if self.kind == "full":
            return pl.BlockSpec(self.block(), lambda j, i: (0, 0))
        if self.kind == "col":
            return pl.BlockSpec(self.block(), lambda j, i: (0, j))
        return pl.BlockSpec(self.block(), lambda j, i: (j, 0))


class Out:
    def __init__(self, width, dtype, cb=None, off=0):
        self.width, self.dtype, self.cb, self.off = width, dtype, cb, off

    def spec(self, tm):
        if self.cb is None:
            return pl.BlockSpec((tm, self.width), lambda j, i: (i, 0))
        off = self.off
        return pl.BlockSpec((tm, self.cb), lambda j, i: (i, j + off))


def _params(sem):
    return pltpu.CompilerParams(dimension_semantics=sem, vmem_limit_bytes=VMEM_LIMIT)


def tile_fwd(name, fn, *, m, tm, nj, rows, pars, outs, n_acc=0):
    n_r, n_p, n_o = len(rows), len(pars), len(outs)

    def body(*refs):
        ins = [r[...] for r in refs[:n_r + n_p]]
        res = fn(*ins)
        o_refs = refs[n_r + n_p:]
        for k in range(n_o):
            o_refs[k][...] = res[k].astype(o_refs[k].dtype)
        first = jnp.logical_and(pl.program_id(0) == 0, pl.program_id(1) == 0)
        for k in range(n_acc):
            ref = o_refs[n_o + k]

            @pl.when(first)
            def _():
                ref[...] = jnp.zeros_like(ref)

            ref[...] += res[n_o + k]

    out_shape = [jax.ShapeDtypeStruct((m, o.width), o.dtype) for o in outs]
    out_specs = [o.spec(tm) for o in outs]
    for _ in range(n_acc):
        out_shape.append(jax.ShapeDtypeStruct((1, LANES), F32))
        out_specs.append(pl.BlockSpec((1, LANES), lambda j, i: (0, 0)))
    sem = ("arbitrary", "arbitrary") if n_acc else ("parallel", "parallel")
    return pl.pallas_call(
        body, grid=(nj, m // tm), name=name,
        in_specs=[r.spec(tm) for r in rows] + [p.spec() for p in pars],
        out_specs=out_specs, out_shape=out_shape, compiler_params=_params(sem),
    )(*[r.arr for r in rows], *[p.arr for p in pars])


def tile_bwd(name, fn, *, m, tm, nj, rows, pars, cts, drows):
    n_r, n_p, n_c = len(rows), len(pars), len(cts)
    want = [k for k in range(n_r) if drows[k] is not None]

    def body(*refs):
        ins = [r[...] for r in refs[:n_r + n_p]]
        ct = [r[...] for r in refs[n_r + n_p:n_r + n_p + n_c]]
        o_refs = refs[n_r + n_p + n_c:]
        res, vjp = jax.vjp(fn, *ins)
        grads = vjp(tuple(c.astype(r.dtype) for c, r in zip(ct, res)))
        for pos, k in enumerate(want):
            o_refs[pos][...] = grads[k].astype(o_refs[pos].dtype)
        for k in range(n_p):
            ref = o_refs[len(want) + k]
            first = pl.program_id(1) == 0
            if pars[k].kind == "full":
                first = jnp.logical_and(first, pl.program_id(0) == 0)

            @pl.when(first)
            def _():
                ref[...] = jnp.zeros_like(ref)

            ref[...] += grads[n_r + k].astype(F32)

    out_shape = [jax.ShapeDtypeStruct((m, drows[k].width), drows[k].dtype) for k in want]
    out_specs = [drows[k].spec(tm) for k in want]
    for p in pars:
        out_shape.append(jax.ShapeDtypeStruct(p.arr.shape, F32))
        out_specs.append(p.spec())
    return pl.pallas_call(
        body, grid=(nj, m // tm), name=name,
        in_specs=[r.spec(tm) for r in rows] + [p.spec() for p in pars] + [c.spec(tm) for c in cts],
        out_specs=out_specs, out_shape=out_shape, compiler_params=_params(("arbitrary", "arbitrary")),
    )(*[r.arr for r in rows], *[p.arr for p in pars], *[c.arr for c in cts])


def loss_head(name, x, y, tgt, g, *, m, tm):
    def body(x_ref, y_ref, t_ref, g_ref, dout_ref, dy_ref, loss_ref, dg_ref):
        normed, vjp = jax.vjp(_rms, y_ref[...], g_ref[...])
        err = x_ref[...] + normed - t_ref[...]
        dout = err * (1.0 / D_MODEL)
        dy, dg = vjp(dout)
        dout_ref[...] = dout
        dy_ref[...] = dy.astype(dy_ref.dtype)

        @pl.when(pl.program_id(0) == 0)
        def _():
            loss_ref[...] = jnp.zeros_like(loss_ref)
            dg_ref[...] = jnp.zeros_like(dg_ref)

        loss_ref[...] += 0.5 * jnp.sum(jnp.mean(err * err, axis=-1, keepdims=True), axis=0, keepdims=True)
        dg_ref[...] += dg

    row = pl.BlockSpec((tm, D_MODEL), lambda i: (i, 0))
    whole = lambda w: pl.BlockSpec((1, w), lambda i: (0, 0))
    return pl.pallas_call(
        body, grid=(m // tm,), name=name, in_specs=[row, row, row, whole(D_MODEL)],
        out_specs=[row, row, whole(LANES), whole(D_MODEL)],
        out_shape=[jax.ShapeDtypeStruct((m, D_MODEL), F32), jax.ShapeDtypeStruct((m, D_MODEL), BF16),
                   jax.ShapeDtypeStruct((1, LANES), F32), jax.ShapeDtypeStruct((1, D_MODEL), F32)],
        compiler_params=_params(("arbitrary",)),
    )(x, y, tgt, g)


class Blk:
    def __init__(self, arr, block, index):
        self.arr, self.block, self.index = arr, block, index

    def spec(self):
        return pl.BlockSpec(self.block, self.index)


def _flat2(v):
    return v if v.ndim == 2 else v.reshape(-1, v.shape[-1])


def mm(name, pat, a, b, o, out_dtype, grid, after=None, b_join=False, o_split=False):
    nk = grid[2]
    o_shape = o.arr

    def put(o_ref, r):
        if o_split:
            half = r.shape[1] // 2
            o_ref[0] = r[:, :half].astype(out_dtype)
            o_ref[1] = r[:, half:].astype(out_dtype)
        else:
            o_ref[...] = r.astype(out_dtype).reshape(o_ref.shape)

    def body(*refs):
        a_ref, b_ref = refs[0], refs[1]
        o_ref = refs[3] if after is not None else refs[2]
        bv = jnp.concatenate([b_ref[0], b_ref[1]], axis=1) if b_join else _flat2(b_ref[...])
        r = _dg(_flat2(a_ref[...]), bv, pat)
        if nk == 1:
            put(o_ref, r)
            return
        acc_ref = refs[-1]
        kk = pl.program_id(2)

        @pl.when(kk == 0)
        def _():
            acc_ref[...] = r

        @pl.when(kk > 0)
        def _():
            acc_ref[...] += r

        @pl.when(kk == nk - 1)
        def _():
            put(o_ref, acc_ref[...])

    ob = [d for d in o.block if d is not None]
    if o_split:
        acc_shape = (ob[1], 2 * ob[2])
    else:
        acc_shape = (ob[0], ob[1]) if len(ob) == 2 else (ob[0] * ob[1], ob[2])
    in_specs = [a.spec(), b.spec()]
    args = [a.arr, b.arr]
    if after is not None:
        in_specs.append(pl.BlockSpec(memory_space=pl.ANY))
        args.append(after)
    return pl.pallas_call(
        body, grid=grid, name=name, in_specs=in_specs, out_specs=o.spec(),
        out_shape=jax.ShapeDtypeStruct(o_shape, out_dtype),
        scratch_shapes=[pltpu.VMEM(acc_shape, F32)] if nk > 1 else [],
        compiler_params=_params(("parallel", "parallel", "arbitrary")),
    )(*args)


def _div_tile(n, cap):
    if n <= cap:
        return n
    best = 128
    for t in range(128, cap + 1, 128):
        if n % t == 0:
            best = t
    return best


def mm2d(name, pat, a, b, out_dtype=F32):
    if pat == "tn":
        k, m = a.shape
    else:
        m, k = a.shape
    n = b.shape[0] if pat == "nt" else b.shape[1]
    tm, tn, tk = _div_tile(m, 1024), _div_tile(n, 1024), _div_tile(k, 1024)
    a_blk = Blk(a, (tk, tm), lambda i, j, kk: (kk, i)) if pat == "tn" else Blk(a, (tm, tk), lambda i, j, kk: (i, kk))
    b_blk = Blk(b, (tn, tk), lambda i, j, kk: (j, kk)) if pat == "nt" else Blk(b, (tk, tn), lambda i, j, kk: (kk, j))
    o_blk = Blk((m, n), (tm, tn), lambda i, j, kk: (i, j))
    return mm(name, pat, a_blk, b_blk, o_blk, out_dtype, (m // tm, n // tn, k // tk))


def hgrn_fwd(name, z, logits, hnorm, *, n_batch, seq):
    m = n_batch * seq
    ts = min(HGRN_SEG, seq)
    n_seg = seq // ts

    def body(q_ref, f_ref, v_ref, g_ref, lg_ref, hn_ref, o_ref, sp_ref, st_ref):
        s = pl.program_id(2)

        @pl.when(s == 0)
        def _():
            st_ref[...] = jnp.zeros_like(st_ref)

        st = st_ref[...]
        sp_ref[...] = st
        o, st_new = fn_hgrn_seg(q_ref[...], f_ref[...], v_ref[...], g_ref[...], st, lg_ref[...], hn_ref[...])
        o_ref[...] = o
        st_ref[...] = st_new

    part = lambda p: pl.BlockSpec((ts, 128), lambda h, b, s: (b * n_seg + s, 4 * p + h))
    return pl.pallas_call(
        body, grid=(A_HEADS, n_batch, n_seg), name=name,
        in_specs=[part(0), part(1), part(2), part(3),
                  pl.BlockSpec((3, 128), lambda h, b, s: (0, h)),
                  pl.BlockSpec((1, 128), lambda h, b, s: (0, h))],
        out_specs=[pl.BlockSpec((ts, 128), lambda h, b, s: (b * n_seg + s, h)),
                   pl.BlockSpec((128, 128), lambda h, b, s: ((b * n_seg + s) * A_HEADS + h, 0))],
        out_shape=[jax.ShapeDtypeStruct((m, A_WIDTH), BF16),
                   jax.ShapeDtypeStruct((n_batch * n_seg * A_HEADS * 128, 128), F32)],
        scratch_shapes=[pltpu.VMEM((128, 128), F32)],
        compiler_params=_params(("arbitrary", "arbitrary", "arbitrary")),
    )(z, z, z, z, logits, hnorm)


def hgrn_bwd(name, z, sprev, logits, hnorm, do, *, n_batch, seq):
    m = n_batch * seq
    ts = min(HGRN_SEG, seq)
    n_seg = seq // ts

    def body(q_ref, f_ref, v_ref, g_ref, sp_ref, lg_ref, hn_ref, do_ref, dq_ref, df_ref, dv_ref, dg_ref, dlg_ref, dhn_ref, dst_ref):
        s = pl.program_id(2)

        @pl.when(s == 0)
        def _():
            dst_ref[...] = jnp.zeros_like(dst_ref)

        res, vjp = jax.vjp(fn_hgrn_seg, q_ref[...], f_ref[...], v_ref[...], g_ref[...], sp_ref[...], lg_ref[...], hn_ref[...])
        dq, df, dv, dg, dst, dlg, dhn = vjp((do_ref[...].astype(res[0].dtype), dst_ref[...]))
        dq_ref[...] = dq.astype(dq_ref.dtype)
        df_ref[...] = df.astype(df_ref.dtype)
        dv_ref[...] = dv.astype(dv_ref.dtype)
        dg_ref[...] = dg.astype(dg_ref.dtype)
        dst_ref[...] = dst
        first = jnp.logical_and(pl.program_id(1) == 0, s == 0)

        @pl.when(first)
        def _():
            dlg_ref[...] = jnp.zeros_like(dlg_ref)
            dhn_ref[...] = jnp.zeros_like(dhn_ref)

        dlg_ref[...] += dlg
        dhn_ref[...] += dhn

    rev = lambda b, s: b * n_seg + (n_seg - 1 - s)
    part = lambda p: pl.BlockSpec((ts, 128), lambda h, b, s: (rev(b, s), 4 * p + h))
    head = pl.BlockSpec((ts, 128), lambda h, b, s: (rev(b, s), h))
    dpart = jax.ShapeDtypeStruct((m, A_WIDTH), BF16)
    return pl.pallas_call(
        body, grid=(A_HEADS, n_batch, n_seg), name=name,
        in_specs=[part(0), part(1), part(2), part(3),
                  pl.BlockSpec((128, 128), lambda h, b, s: (rev(b, s) * A_HEADS + h, 0)),
                  pl.BlockSpec((3, 128), lambda h, b, s: (0, h)),
                  pl.BlockSpec((1, 128), lambda h, b, s: (0, h)),
                  head],
        out_specs=[head, head, head, head,
                   pl.BlockSpec((3, 128), lambda h, b, s: (0, h)),
                   pl.BlockSpec((1, 128), lambda h, b, s: (0, h))],
        out_shape=[dpart, dpart, dpart, dpart,
                   jax.ShapeDtypeStruct(logits.shape, F32),
                   jax.ShapeDtypeStruct(hnorm.shape, F32)],
        scratch_shapes=[pltpu.VMEM((128, 128), F32)],
        compiler_params=_params(("arbitrary", "arbitrary", "arbitrary")),
    )(z, z, z, z, sprev, logits, hnorm, do)


FFN_ROWS = 1024
FFN_LANES = 128


def _ffn_tiles(m, seq):
    tm = min(FFN_ROWS, seq)
    return tm, seq // tm, m // tm


def ffn_mid_fwd(name, hid, cw, cb, layer, *, m, seq):
    tm, n_t, n_i = _ffn_tiles(m, seq)
    hb = tm // HALO

    def body(x_ref, xb_ref, cw_ref, cb_ref, o_ref, c_ref):
        first = pl.program_id(1) % n_t == 0
        for l0 in range(0, FF_BLK, FFN_LANES):
            lanes = slice(l0, min(l0 + FFN_LANES, FF_BLK))
            before = jnp.where(first, 0.0, xb_ref[:, :, lanes])
            ext = jnp.concatenate([before, x_ref[:, :, lanes]], axis=1)
            cg, cv = _ffn_conv(ext[0], ext[1], cw_ref[:, :, lanes], cb_ref[:, :, lanes])
            o_ref[:, lanes] = _ffn_gate(cg, cv).astype(o_ref.dtype)
            c_ref[0, :, lanes] = cg.astype(c_ref.dtype)
            c_ref[1, :, lanes] = cv.astype(c_ref.dtype)

    return pl.pallas_call(
        body, grid=(N_DEV // 2, n_i), name=name,
        in_specs=[pl.BlockSpec((2, None, tm, FF_BLK), lambda d, i: (0, d, i, 0)),
                  pl.BlockSpec((2, None, HALO, FF_BLK), lambda d, i: (0, d, jnp.maximum(i * hb - 1, 0), 0)),
                  pl.BlockSpec((2, None, None, FFN_CONV, FF_BLK), lambda d, i: (0, d, layer, 0, 0)),
                  pl.BlockSpec((None, 2, None, 1, FF_BLK), lambda d, i: (layer, 0, d, 0, 0))],
        out_specs=[pl.BlockSpec((None, tm, FF_BLK), lambda d, i: (d, i, 0)),
                   pl.BlockSpec((2, None, tm, FF_BLK), lambda d, i: (0, d, i, 0))],
        out_shape=[jax.ShapeDtypeStruct((N_DEV // 2, m, FF_BLK), BF16),
                   jax.ShapeDtypeStruct((2, N_DEV // 2, m, FF_BLK), BF16)],
        compiler_params=_params(("parallel", "parallel")),
    )(hid, hid, cw, cb)


def ffn_mid_bwd(name, hid, conv, cw, dact, layer, *, m, seq):
    tm, n_t, n_i = _ffn_tiles(m, seq)
    hb = tm // HALO
    last_blk = m // HALO - 1

    rc = min(FFN_ROWS, tm)
    lane_chunks = [(l0, min(FFN_LANES, FF_BLK - l0)) for l0 in range(0, FF_BLK, FFN_LANES)]

    def body(x_ref, c_ref, ca_ref, cw_ref, da_ref, daa_ref, dx_ref, dcw_ref, dcb_ref, cext_ref, dext_ref):
        i = pl.program_id(1)
        last = i % n_t == n_t - 1
        cext_ref[:, :tm] = c_ref[...]
        cext_ref[:, tm:] = ca_ref[...]
        dext_ref[:tm] = da_ref[...]
        dext_ref[tm:] = jnp.where(last, jnp.zeros_like(daa_ref[...]), daa_ref[...])

        @pl.when(i == 0)
        def _():
            dcw_ref[...] = jnp.zeros_like(dcw_ref)
            dcb_ref[...] = jnp.zeros_like(dcb_ref)

        for l0, lw in lane_chunks:
            lanes = slice(l0, l0 + lw)

            def chunk(c, sums, lanes=lanes, lw=lw):
                r0 = pl.multiple_of(c * rc, rc)
                ext = pl.ds(r0, rc + HALO)
                cg, cv = cext_ref[0, ext, lanes].astype(F32), cext_ref[1, ext, lanes].astype(F32)
                _, vjp_gate = jax.vjp(_ffn_gate, cg, cv)
                dconv = vjp_gate(dext_ref[ext, lanes].astype(F32))
                out = []
                for half in range(2):
                    x = x_ref[half, pl.ds(r0, rc), lanes]
                    dx = None
                    for k in range(FFN_CONV):
                        s = FFN_CONV - 1 - k
                        dc_s = _shift_raw(dconv[half], s, True, 0.0)[:rc]
                        term = cw_ref[half, k:k + 1, lanes] * dc_s
                        dx = term if dx is None else dx + term
                        out.append(sums[len(out)] + jnp.sum(x * dc_s, axis=0, keepdims=True))
                    out.append(sums[len(out)] + jnp.sum(dconv[half][:rc], axis=0, keepdims=True))
                    dx_ref[half, pl.ds(r0, rc), lanes] = dx.astype(dx_ref.dtype)
                return tuple(out)

            zero = jnp.zeros((1, lw), F32)
            sums = lax.fori_loop(0, tm // rc, chunk, (zero,) * (2 * (FFN_CONV + 1)))
            for half in range(2):
                base = half * (FFN_CONV + 1)
                for k in range(FFN_CONV):
                    dcw_ref[half, k:k + 1, lanes] += sums[base + k]
                dcb_ref[half, :, lanes] += sums[base + FFN_CONV]

    return pl.pallas_call(
        body, grid=(N_DEV // 2, n_i), name=name,
        in_specs=[pl.BlockSpec((2, None, tm, FF_BLK), lambda d, i: (0, d, i, 0)),
                  pl.BlockSpec((2, None, tm, FF_BLK), lambda d, i: (0, d, i, 0)),
                  pl.BlockSpec((2, None, HALO, FF_BLK), lambda d, i: (0, d, jnp.minimum((i + 1) * hb, last_blk), 0)),
                  pl.BlockSpec((2, None, None, FFN_CONV, FF_BLK), lambda d, i: (0, d, layer, 0, 0)),
                  pl.BlockSpec((None, tm, FF_BLK), lambda d, i: (d, i, 0)),
                  pl.BlockSpec((None, HALO, FF_BLK), lambda d, i: (d, jnp.minimum((i + 1) * hb, last_blk), 0))],
        out_specs=[pl.BlockSpec((2, None, tm, FF_BLK), lambda d, i: (0, d, i, 0)),
                   pl.BlockSpec((2, None, FFN_CONV, FF_BLK), lambda d, i: (0, d, 0, 0)),
                   pl.BlockSpec((2, None, 1, FF_BLK), lambda d, i: (0, d, 0, 0))],
        out_shape=[jax.ShapeDtypeStruct((2, N_DEV // 2, m, FF_BLK), BF16),
                   jax.ShapeDtypeStruct((2, N_DEV // 2, FFN_CONV, FF_BLK), F32),
                   jax.ShapeDtypeStruct((2, N_DEV // 2, 1, FF_BLK), F32)],
        scratch_shapes=[pltpu.VMEM((2, tm + HALO, FF_BLK), BF16), pltpu.VMEM((tm + HALO, FF_BLK), BF16)],
        compiler_params=_params(("arbitrary", "arbitrary")),
    )(hid, conv, conv, cw, dact, dact)


ATT_BLK = 512
ATT_BLK_FWD = 1024
N_PAIR = C_HEADS // 2
TERM_W = C_HEADS * LANES


def term_placement():
    import numpy as np
    place = np.zeros((3, LANES, TERM_W), np.float32)
    ones_q = np.zeros((1, TERM_W), np.float32)
    ones_k = np.zeros((1, TERM_W), np.float32)
    for h in range(C_HEADS):
        for j in range(3):
            place[j, h, h * LANES + C_HEAD_DIM + j] = 1.0
            ones_q[0, h * LANES + C_HEAD_DIM + 3 + j] = 1.0
            ones_k[0, h * LANES + C_HEAD_DIM + j] = 1.0
    return (jnp.asarray(place.reshape(3 * LANES, TERM_W), BF16), jnp.asarray(ones_q, F32), jnp.asarray(ones_k, F32))


def fn_fox_terms(c, place, ones_q, ones_k):
    parts = _split3(c)
    placed = sum(_dg(parts[j], place[j * LANES:(j + 1) * LANES], "nn") for j in range(3))
    return (placed + ones_q).astype(BF16), (ones_k - pltpu.roll(placed, 3, 1)).astype(BF16)


def _head_tile(z, terms, e):
    lane = lax.broadcasted_iota(jnp.int32, z.shape, 1)
    base = z if e == 0 else pltpu.roll(z, C_HEAD_DIM, 1)
    return jnp.where(lane < C_HEAD_DIM, base, terms.astype(z.dtype))


def _head_only(z, e):
    lane = lax.broadcasted_iota(jnp.int32, z.shape, 1)
    mine = (lane < C_HEAD_DIM) if e == 0 else (lane >= C_HEAD_DIM)
    return jnp.where(mine, z, jnp.zeros_like(z)).astype(BF16)


def _pair_tile(a0, a1):
    lane = lax.broadcasted_iota(jnp.int32, a0.shape, 1)
    return jnp.where(lane < C_HEAD_DIM, a0, pltpu.roll(a1, C_HEAD_DIM, 1))


def _lane_col(a, k):
    lane = lax.broadcasted_iota(jnp.int32, a.shape, 1)
    return jnp.sum(jnp.where(lane == k, a, 0.0), axis=1, keepdims=True)


def _causal(s):
    key = lax.broadcasted_iota(jnp.int32, s.shape, 0)
    qry = lax.broadcasted_iota(jnp.int32, s.shape, 1)
    return qry >= key


def fox_pair_fwd(name, z, qterm, kterm, *, n_batch, seq):
    m = n_batch * seq
    blk = min(ATT_BLK_FWD, seq)
    nq = seq // blk
    dh = C_HEAD_DIM

    def body(zq_ref, zk_ref, zv_ref, qt_ref, kt_ref, o_ref, lse_ref, ka_ref, vt_ref):
        qi = pl.program_id(2)

        @pl.when(qi == 0)
        def _():
            zk = zk_ref[...]
            for e in range(2):
                ka_ref[e] = _head_tile(zk, kt_ref[:, e * LANES:(e + 1) * LANES], e).astype(BF16)
            for cb in range(nq):
                vt_ref[cb] = zv_ref[cb * blk:(cb + 1) * blk, :].T.astype(BF16)

        zq = zq_ref[...] * dh ** -0.5
        qa = [_head_tile(zq, qt_ref[:, e * LANES:(e + 1) * LANES], e).astype(BF16) for e in range(2)]

        def block(j, carry, diagonal):
            rows = pl.ds(pl.multiple_of(j * blk, blk), blk)
            out = []
            for e in range(2):
                mx, l, acc = carry[e]
                s = _dg(ka_ref[e, rows, :], qa[e], "nt")
                if diagonal:
                    s = jnp.where(_causal(s), s, NEG)
                mx_new = jnp.maximum(mx, jnp.max(s, axis=0, keepdims=True))
                p = jnp.exp(s - mx_new)
                alpha = jnp.exp(mx - mx_new)
                l = alpha * l + jnp.sum(p, axis=0, keepdims=True)
                acc = alpha * acc + _dg(vt_ref[j, e * dh:(e + 1) * dh, :], p, "nn")
                out.append((mx_new, l, acc))
            return tuple(out)

        one = (jnp.full((1, blk), NEG, F32), jnp.zeros((1, blk), F32), jnp.zeros((dh, blk), F32))
        carry = lax.fori_loop(0, qi, lambda j, cr: block(j, cr, False), (one, one))
        res = block(qi, carry, True)
        ot = jnp.concatenate([res[e][2] / res[e][1] for e in range(2)], axis=0)
        o_ref[...] = ot.T.astype(o_ref.dtype)
        for e in range(2):
            lse_ref[e] = res[e][0] + jnp.log(res[e][1])

    col = lambda part: (lambda b, g, i: (b, part * N_PAIR + g))
    return pl.pallas_call(
        body, grid=(n_batch, N_PAIR, nq), name=name,
        in_specs=[pl.BlockSpec((blk, LANES), lambda b, g, i: (b * nq + i, g)),
                  pl.BlockSpec((seq, LANES), col(1)),
                  pl.BlockSpec((seq, LANES), col(2)),
                  pl.BlockSpec((blk, 2 * LANES), lambda b, g, i: (b * nq + i, g)),
                  pl.BlockSpec((seq, 2 * LANES), lambda b, g, i: (b, g))],
        out_specs=[pl.BlockSpec((blk, LANES), lambda b, g, i: (b * nq + i, g)),
                   pl.BlockSpec((None, None, None, 2, 1, blk), lambda b, g, i: (b, g, i, 0, 0, 0))],
        out_shape=[jax.ShapeDtypeStruct((m, D_MODEL), BF16), jax.ShapeDtypeStruct((n_batch, N_PAIR, nq, 2, 1, blk), F32)],
        scratch_shapes=[pltpu.VMEM((2, seq, LANES), BF16), pltpu.VMEM((nq, LANES, blk), BF16)],
        compiler_params=_params(("parallel", "parallel", "arbitrary")),
    )(z, z, z, qterm, kterm)


def fox_pair_bwd(name, z, qterm, kterm, o, do, lse, *, n_batch, seq):
    m = n_batch * seq
    blk = min(ATT_BLK, seq)
    nq = seq // blk
    dh = C_HEAD_DIM

    def body(zq_ref, zk_ref, zv_ref, qt_ref, kt_ref, o_ref, do_ref, lse_ref, dq_ref, dk_ref, dv_ref, dc_ref,
             qa_ref, doh_ref, del_ref, dqt_ref, dk_acc, dv_acc):
        g, j = pl.program_id(1), pl.program_id(2)
        lane = lax.broadcasted_iota(jnp.int32, (blk, LANES), 1)

        @pl.when(jnp.logical_and(g == 0, j == 0))
        def _():
            dc_ref[...] = jnp.zeros_like(dc_ref)

        @pl.when(j == 0)
        def _():
            zq = zq_ref[...] * dh ** -0.5
            dov = do_ref[...]
            for e in range(2):
                qa_ref[e] = _head_tile(zq, qt_ref[:, e * LANES:(e + 1) * LANES], e).astype(BF16)
                doh_ref[e] = _head_only(dov, e)
            for cb in range(nq):
                rows = slice(cb * blk, (cb + 1) * blk)
                prod_t = (do_ref[rows, :].astype(F32) * o_ref[rows, :].astype(F32)).T
                for e in range(2):
                    del_ref[cb, e] = jnp.sum(prod_t[e * dh:(e + 1) * dh], axis=0, keepdims=True)
            dqt_ref[...] = jnp.zeros_like(dqt_ref)

        zk, zv = zk_ref[...], zv_ref[...]
        ka32 = [_head_tile(zk, kt_ref[:, e * LANES:(e + 1) * LANES], e) for e in range(2)]
        ka = [t.astype(BF16) for t in ka32]
        kat = [t.T.astype(BF16) for t in ka32]
        vh = [_head_only(zv, e) for e in range(2)]
        dk_acc[...] = jnp.zeros_like(dk_acc)
        dv_acc[...] = jnp.zeros_like(dv_acc)

        def block(i, diagonal):
            rows = pl.ds(pl.multiple_of(i * blk, blk), blk)
            for e in range(2):
                qv, dov = qa_ref[e, rows, :], doh_ref[e, rows, :]
                p = jnp.exp(_dg(ka[e], qv, "nt") - lse_ref[i, e])
                if diagonal:
                    p = jnp.where(_causal(p), p, 0.0)
                dv_acc[...] += _dg(p, dov, "nn")
                ds = p * (_dg(vh[e], dov, "nt") - del_ref[i, e])
                dk_acc[e] += _dg(ds, qv, "nn")
                dqt_ref[i, e] += _dg(kat[e], ds, "nn")

        block(j, True)

        def rest(i, carry):
            block(i, False)
            return carry

        lax.fori_loop(j + 1, nq, rest, 0)
        dk0, dk1 = dk_acc[0], dk_acc[1]
        dk_ref[...] = _pair_tile(dk0, dk1).astype(dk_ref.dtype)
        dv_ref[...] = dv_acc[...].astype(dv_ref.dtype)
        rows_j = pl.ds(pl.multiple_of(j * blk, blk), blk)
        for e, dke in enumerate((dk0, dk1)):
            dc_ref[rows_j, :] -= jnp.where(lane == 2 * g + e, _lane_col(dke, dh + 3), 0.0)

        @pl.when(j == nq - 1)
        def _():
            for i in range(nq):
                nat = [dqt_ref[i, e].T for e in range(2)]
                rows = slice(i * blk, (i + 1) * blk)
                dq_ref[rows, :] = (_pair_tile(nat[0], nat[1]) * dh ** -0.5).astype(dq_ref.dtype)
                for e in range(2):
                    dc_ref[rows, :] += jnp.where(lane == 2 * g + e, _lane_col(nat[e], dh), 0.0)

    col = lambda part: (lambda b, g, j: (b, part * N_PAIR + g))
    colj = lambda part: (lambda b, g, j: (b * nq + j, part * N_PAIR + g))
    pair = jax.ShapeDtypeStruct((m, D_MODEL), BF16)
    return pl.pallas_call(
        body, grid=(n_batch, N_PAIR, nq), name=name,
        in_specs=[pl.BlockSpec((seq, LANES), col(0)),
                  pl.BlockSpec((blk, LANES), colj(1)),
                  pl.BlockSpec((blk, LANES), colj(2)),
                  pl.BlockSpec((seq, 2 * LANES), lambda b, g, j: (b, g)),
                  pl.BlockSpec((blk, 2 * LANES), lambda b, g, j: (b * nq + j, g)),
                  pl.BlockSpec((seq, LANES), col(0)),
                  pl.BlockSpec((seq, LANES), col(0)),
                  pl.BlockSpec((None, None, nq, 2, 1, blk), lambda b, g, j: (b, g, 0, 0, 0, 0))],
        out_specs=[pl.BlockSpec((seq, LANES), col(0)),
                   pl.BlockSpec((blk, LANES), colj(0)),
                   pl.BlockSpec((blk, LANES), colj(0)),
                   pl.BlockSpec((seq, LANES), lambda b, g, j: (b, 0))],
        out_shape=[pair, pair, pair, jax.ShapeDtypeStruct((m, LANES), F32)],
        scratch_shapes=[pltpu.VMEM((2, seq, LANES), BF16), pltpu.VMEM((2, seq, LANES), BF16),
                        pltpu.VMEM((nq, 2, 1, blk), F32), pltpu.VMEM((nq, 2, LANES, blk), F32),
                        pltpu.VMEM((2, blk, LANES), F32), pltpu.VMEM((blk, LANES), F32)],
        compiler_params=_params(("arbitrary", "arbitrary", "arbitrary")),
    )(z, z, z, qterm, kterm, o, do, lse)


def _split3(c):
    c1 = c.astype(BF16)
    r1 = c - c1.astype(F32)
    c2 = r1.astype(BF16)
    c3 = (r1 - c2.astype(F32)).astype(BF16)
    return c1, c2, c3


def _mesh_pos():
    return lax.axis_index("x"), lax.axis_index("y"), lax.axis_index("c")


def _flip(v, bit):
    return 1 - v if bit else v


def all_gather(name, blocks):
    n = len(blocks)

    def body(*refs):
        x_refs, out_refs = refs[:n], refs[n:2 * n]
        send_sems, recv_sems, local_sems = refs[2 * n:]
        x, y, c = _mesh_pos()
        me, sibling = (x, y, c), (x, y, 1 - c)
        chips = [(1 - x, y), (x, 1 - y), (1 - x, 1 - y)]

        def slot(a, px, py, pc):
            return out_refs[a].at[4 * px + 2 * py + pc]

        def copy(a, k, blk, to, src=None):
            return pltpu.make_async_remote_copy(
                src_ref=slot(a, *blk) if src is None else src, dst_ref=slot(a, *blk),
                send_sem=send_sems.at[a, k], recv_sem=recv_sems.at[a, k], device_id=to, device_id_type=MESH)

        mine = [pltpu.make_async_copy(x_refs[a], slot(a, *me), local_sems.at[a]) for a in range(n)]
        for cp in mine:
            cp.start()
        sends = []
        for a in range(n):
            sends.append(copy(a, 0, me, sibling, src=x_refs[a]))
            sends += [copy(a, 1 + j, me, (*chip, c), src=x_refs[a]) for j, chip in enumerate(chips)]
        for cp in sends:
            cp.start()
        for j, chip in enumerate(chips):
            for a in range(n):
                copy(a, 1 + j, (*chip, c), me).wait_recv()
                passed = copy(a, 4 + j, (*chip, c), sibling)
                passed.start()
                sends.append(passed)
        for a in range(n):
            copy(a, 0, sibling, me).wait_recv()
            for j, chip in enumerate(chips):
                copy(a, 4 + j, (*chip, 1 - c), me).wait_recv()
        for cp in sends:
            cp.wait_send()
        for cp in mine:
            cp.wait()

    hbm = pl.BlockSpec(memory_space=pl.ANY)
    return pl.pallas_call(
        body, name=name, out_shape=[jax.ShapeDtypeStruct((N_DEV,) + b.shape, b.dtype) for b in blocks],
        in_specs=[hbm] * n, out_specs=[hbm] * n,
        scratch_shapes=[pltpu.SemaphoreType.DMA((n, 7)), pltpu.SemaphoreType.DMA((n, 7)), pltpu.SemaphoreType.DMA((n,))],
    )(*blocks)


def _peers(x, y, c):
    return [(_flip(x, k & 4), _flip(y, k & 2), _flip(c, k & 1)) for k in range(1, N_DEV)]


def gather_start(name, blocks, lands):
    n = len(blocks)

    def body(*refs):
        x_refs, land_refs = refs[:n], refs[n:2 * n]
        send_sems, recv_sems = refs[2 * n], refs[2 * n + 1]
        token = refs[-1]
        x, y, c = _mesh_pos()
        me = 4 * x + 2 * y + c
        for k, peer in enumerate(_peers(x, y, c)):
            for a in range(n):
                pltpu.make_async_remote_copy(
                    src_ref=x_refs[a], dst_ref=land_refs[a].at[me], send_sem=send_sems.at[7 * a + k], recv_sem=recv_sems.at[7 * a + k],
                    device_id=peer, device_id_type=MESH).start()
        token[...] = jnp.zeros_like(token)

    hbm = pl.BlockSpec(memory_space=pltpu.HBM)
    sem = pl.BlockSpec(memory_space=pltpu.SEMAPHORE)
    out_shape = ([pltpu.SemaphoreType.DMA((7 * n,)), pltpu.SemaphoreType.DMA((7 * n,))]
                 + [pltpu.HBM(b.shape, b.dtype) for b in blocks] + [pltpu.HBM(l.shape, l.dtype) for l in lands]
                 + [jax.ShapeDtypeStruct((8, LANES), F32)])
    res = pl.pallas_call(
        body, name=name, out_shape=out_shape, in_specs=[hbm] * (2 * n),
        out_specs=[sem, sem] + [hbm] * (2 * n) + [pl.BlockSpec(memory_space=pltpu.VMEM)],
        input_output_aliases={a: 2 + a for a in range(2 * n)},
        compiler_params=pltpu.CompilerParams(has_side_effects=pltpu.SideEffectType.DATAFLOW_SIDE_EFFECTING),
    )(*[pltpu.with_memory_space_constraint(b, pltpu.HBM) for b in blocks],
      *[pltpu.with_memory_space_constraint(l, pltpu.HBM) for l in lands])
    return res[0], res[1], res[2:2 + n], res[2 + n:2 + 2 * n], res[-1]


def gather_wait(name, send_sems, recv_sems, blocks, lands, after):
    n = len(blocks)

    def body(*refs):
        x_refs, land_refs = refs[:n], refs[n:2 * n]
        s_sems, r_sems = refs[2 * n], refs[2 * n + 1]
        x, y, c = _mesh_pos()
        me = 4 * x + 2 * y + c
        for k, peer in enumerate(_peers(x, y, c)):
            for a in range(n):
                cp = pltpu.make_async_remote_copy(
                    src_ref=x_refs[a], dst_ref=land_refs[a].at[me], send_sem=s_sems.at[7 * a + k], recv_sem=r_sems.at[7 * a + k],
                    device_id=peer, device_id_type=MESH)
                cp.wait_send()
                cp.wait_recv()

    hbm = pl.BlockSpec(memory_space=pltpu.HBM)
    sem = pl.BlockSpec(memory_space=pltpu.SEMAPHORE)
    res = pl.pallas_call(
        body, name=name,
        out_shape=[pltpu.HBM(b.shape, b.dtype) for b in blocks] + [pltpu.HBM(l.shape, l.dtype) for l in lands],
        in_specs=[hbm] * (2 * n) + [sem, sem, pl.BlockSpec(memory_space=pl.ANY)], out_specs=[hbm] * (2 * n),
        input_output_aliases={a: a for a in range(2 * n)},
        compiler_params=pltpu.CompilerParams(has_side_effects=pltpu.SideEffectType.DATAFLOW_SIDE_EFFECTING),
    )(*blocks, *lands, send_sems, recv_sems, after)
    return res[n:]


def _split_exchange(name, sends, lands, sems, after):
    n = len(sends)
    starting = sems is None

    def body(*refs):
        s_refs, l_refs = refs[:n], refs[n:2 * n]
        send_sems, recv_sems = refs[2 * n], refs[2 * n + 1]
        x, y, c = _mesh_pos()
        me = 4 * x + 2 * y + c
        for k, (px, py, pc) in enumerate(_peers(x, y, c)):
            for a in range(n):
                cp = pltpu.make_async_remote_copy(
                    src_ref=s_refs[a].at[4 * px + 2 * py + pc], dst_ref=l_refs[a].at[me],
                    send_sem=send_sems.at[7 * a + k], recv_sem=recv_sems.at[7 * a + k],
                    device_id=(px, py, pc), device_id_type=MESH)
                if starting:
                    cp.start()
                else:
                    cp.wait_send()
                    cp.wait_recv()
        if starting:
            refs[-1][...] = jnp.zeros_like(refs[-1])

    hbm = pl.BlockSpec(memory_space=pltpu.HBM)
    sem = pl.BlockSpec(memory_space=pltpu.SEMAPHORE)
    thru = [pltpu.HBM(t.shape, t.dtype) for t in list(sends) + list(lands)]
    effect = pltpu.CompilerParams(has_side_effects=pltpu.SideEffectType.DATAFLOW_SIDE_EFFECTING)
    if starting:
        res = pl.pallas_call(
            body, name=name, in_specs=[hbm] * (2 * n),
            out_shape=[pltpu.SemaphoreType.DMA((7 * n,)), pltpu.SemaphoreType.DMA((7 * n,))] + thru + [jax.ShapeDtypeStruct((8, LANES), F32)],
            out_specs=[sem, sem] + [hbm] * (2 * n) + [pl.BlockSpec(memory_space=pltpu.VMEM)],
            input_output_aliases={a: 2 + a for a in range(2 * n)}, compiler_params=effect,
        )(*[pltpu.with_memory_space_constraint(t, pltpu.HBM) for t in list(sends) + list(lands)])
        return res[0], res[1], res[2:2 + n], res[2 + n:2 + 2 * n], res[-1]
    res = pl.pallas_call(
        body, name=name, out_shape=thru, in_specs=[hbm] * (2 * n) + [sem, sem, pl.BlockSpec(memory_space=pl.ANY)],
        out_specs=[hbm] * (2 * n), input_output_aliases={a: a for a in range(2 * n)}, compiler_params=effect,
    )(*sends, *lands, sems[0], sems[1], after)
    return res[n:]


def unwritten(name, like):
    def body(*refs):
        pass

    hbm = pl.BlockSpec(memory_space=pl.ANY)
    return pl.pallas_call(body, name=name, out_shape=[jax.ShapeDtypeStruct(t.shape, t.dtype) for t in like],
                          out_specs=[hbm] * len(like))()


def own_slot_only(send, land, me):
    mine = lax.dynamic_index_in_dim(send, me, 0, keepdims=False)
    return lax.dynamic_update_index_in_dim(land, mine, me, 0)


def all_to_all(name, sends):
    n = len(sends)

    def body(*refs):
        s_refs, r_refs = refs[:n], refs[n:2 * n]
        send_sems, recv_sems, local_sems = refs[2 * n:]
        x, y, c = _mesh_pos()
        me = 4 * x + 2 * y + c
        mine = [pltpu.make_async_copy(s_refs[a].at[me], r_refs[a].at[me], local_sems.at[a]) for a in range(n)]
        for cp in mine:
            cp.start()
        copies = []
        for k in range(1, N_DEV):
            px, py, pc = _flip(x, k & 4), _flip(y, k & 2), _flip(c, k & 1)
            for a in range(n):
                copies.append(pltpu.make_async_remote_copy(
                    src_ref=s_refs[a].at[4 * px + 2 * py + pc], dst_ref=r_refs[a].at[me],
                    send_sem=send_sems.at[a, k - 1], recv_sem=recv_sems.at[a, k - 1],
                    device_id=(px, py, pc), device_id_type=MESH))
        for cp in copies:
            cp.start()
        for cp in copies:
            cp.wait_recv()
        for cp in copies:
            cp.wait_send()
        for cp in mine:
            cp.wait()

    hbm = pl.BlockSpec(memory_space=pl.ANY)
    return pl.pallas_call(
        body, name=name, out_shape=[jax.ShapeDtypeStruct(s.shape, s.dtype) for s in sends],
        in_specs=[hbm] * n, out_specs=[hbm] * n,
        scratch_shapes=[pltpu.SemaphoreType.DMA((n, 7)), pltpu.SemaphoreType.DMA((n, 7)), pltpu.SemaphoreType.DMA((n,))],
    )(*sends)


def _row_tile(r, cap, step):
    return next((t for t in range(cap, step - 1, -step) if r % t == 0), r)


def _sum_parts(p, n):
    t = [p[k].astype(F32) for k in range(n)]
    while len(t) > 1:
        t = [t[k] + t[k + 1] for k in range(0, len(t), 2)]
    return t[0]


def _adam(g, w, m, v):
    m = ADAM_B1 * m + (1.0 - ADAM_B1) * g
    v = ADAM_B2 * v + (1.0 - ADAM_B2) * (g * g)
    m_hat = m / (1.0 - ADAM_B1 ** ADAM_STEP)
    v_hat = v / (1.0 - ADAM_B2 ** ADAM_STEP)
    return -ADAM_LR * (m_hat / (jnp.sqrt(v_hat) + ADAM_EPS) + ADAM_WD * w), m, v


def adam_tiled(name, partials, w, m_, v_, layer=0, prev=None):
    _, r, c = w.shape
    n_part = partials.shape[0]
    tr = _row_tile(r, 256, 16)

    def body(*refs):
        p_ref, w_ref, m_ref, v_ref = refs[:4]
        g_ref, d_ref, nm_ref, nv_ref = refs[-4:]
        g = _sum_parts(p_ref, n_part)
        g_ref[...] = g
        d_ref[...], nm_ref[...], nv_ref[...] = _adam(g, w_ref[...], m_ref[...], v_ref[...])

    spec = pl.BlockSpec((None, tr, c), lambda i: (layer, i, 0))
    in_specs = [pl.BlockSpec((n_part, None, tr, c), lambda i: (0, 0, i, 0)), spec, spec, spec]
    args = [partials, w, m_, v_]
    aliases = {}
    if prev is not None:
        in_specs += [pl.BlockSpec(memory_space=pl.ANY)] * 4
        args += list(prev)
        aliases = {4 + k: k for k in range(4)}
    return pl.pallas_call(
        body, grid=(r // tr,), name=name, in_specs=in_specs,
        out_specs=[spec] * 4, out_shape=[jax.ShapeDtypeStruct(w.shape, F32)] * 4,
        input_output_aliases=aliases, compiler_params=_params(("parallel",)),
    )(*args)


def adam_small(name, items, extra):
    n, ne = len(items), len(extra)

    def body(*refs):
        ins, outs = refs[:4 * n + ne], refs[4 * n + ne:]
        for a in range(n):
            p_ref, w_ref, m_ref, v_ref = ins[4 * a:4 * a + 4]
            g = _sum_parts(p_ref, N_DEV)
            outs[4 * a][...] = g
            outs[4 * a + 1][...], outs[4 * a + 2][...], outs[4 * a + 3][...] = _adam(g, w_ref[...], m_ref[...], v_ref[...])
        for e in range(ne):
            outs[4 * n + e][...] = _sum_parts(ins[4 * n + e], N_DEV)

    args, out_shape = [], []
    for p, w, m_, v_ in items:
        args += [p, w, m_, v_]
        out_shape += [jax.ShapeDtypeStruct(w.shape, F32)] * 4
    for e in extra:
        args.append(e)
        out_shape.append(jax.ShapeDtypeStruct(e.shape[1:], F32))
    vmem = pl.BlockSpec(memory_space=pltpu.VMEM)
    res = pl.pallas_call(body, name=name, in_specs=[vmem] * len(args), out_specs=[vmem] * len(out_shape), out_shape=out_shape)(*args)
    return [res[4 * a:4 * a + 4] for a in range(n)], res[4 * n:]


def _cols_from_gather(g):
    g = jnp.moveaxis(g, 0, -2)
    return g.reshape(g.shape[:-2] + (g.shape[-2] * g.shape[-1],))


def _cols_to_blocks(w):
    w = w.reshape(w.shape[:-1] + (N_DEV, w.shape[-1] // N_DEV))
    return jnp.moveaxis(w, -2, 0)


def _block_diag(w):
    pairs = w.reshape(B_BLOCKS // 2, 2, B_BLOCK_DIM, 1, B_BLOCK_DIM)
    same = jnp.eye(2, dtype=bool).reshape(1, 2, 1, 2, 1)
    return jnp.where(same, pairs, 0.0).reshape(B_BLOCKS // 2 * LANES, LANES)


def _block_diag_grad(d):
    parts = d.reshape(B_BLOCKS // 2, 2, B_BLOCK_DIM, 2, B_BLOCK_DIM)
    same = jnp.eye(2, dtype=bool).reshape(1, 2, 1, 2, 1)
    return jnp.sum(jnp.where(same, parts, 0.0), axis=3).reshape(B_BLOCKS, B_BLOCK_DIM, B_BLOCK_DIM)


NAMES = ("norm_gains", "even_w_in", "hgrn_lb_logits", "hgrn_norm", "rg_conv_w", "rg_conv_b", "rg_wa", "rg_ba", "rg_wx", "rg_bx",
         "rg_lambda", "even_w_out", "odd_w_in", "fox_f_bias", "odd_w_out", "ffn_w_up", "ffn_conv_w", "ffn_conv_b", "ffn_w_down")
SMALL_SHARDED = ("norm_gains", "rg_conv_w", "ffn_conv_w")
REPLICATED = ("hgrn_lb_logits", "hgrn_norm", "rg_conv_b", "rg_wa", "rg_ba", "rg_wx", "rg_bx", "rg_lambda", "fox_f_bias", "ffn_conv_b")


def _ffn_forward(tag, layer, h, w_up_g, cw5, cb5, w_down_g, m, seq):
    tm = _div_tile(m, MM_ROWS)
    nm = m // tm
    hid = mm(f"{tag}_up", "nn",
             Blk(h, (tm, D_MODEL), lambda i, j, k: (i, 0)),
             Blk(w_up_g, (None, None, D_MODEL, FF_BLK), lambda i, j, k: (j, 0, 0, 0)),
             Blk((N_DEV, m, FF_BLK), (None, tm, FF_BLK), lambda i, j, k: (j, i, 0)), F32, (nm, N_DEV, 1))
    hid = hid.reshape(2, N_DEV // 2, m, FF_BLK)
    act, conv = ffn_mid_fwd(f"{tag}_mid", hid, cw5, cb5, layer, m=m, seq=seq)
    f = mm(f"{tag}_down", "nn",
           Blk(act, (None, tm, FF_BLK), lambda i, j, k: (k, i, 0)),
           Blk(w_down_g, (2, None, FF_BLK // 2, D_MODEL), lambda i, j, k: (k, 0, 0, 0)),
           Blk((m, D_MODEL), (tm, D_MODEL), lambda i, j, k: (i, 0)), F32, (nm, 1, N_DEV // 2))
    return (hid, conv), act, f


def _ffn_backward(tag, layer, df, h, hid, act, w_up_g, cw5, cb5, w_down_g, m, seq):
    tm = _div_tile(m, MM_ROWS)
    nm = m // tm
    dact = mm(f"{tag}_dact", "nt",
              Blk(df, (tm, D_MODEL), lambda i, j, k: (i, 0)),
              Blk(w_down_g, (2, None, FF_BLK // 2, D_MODEL), lambda i, j, k: (j, 0, 0, 0)),
              Blk((N_DEV // 2, m, FF_BLK), (None, tm, FF_BLK), lambda i, j, k: (j, i, 0)), BF16, (nm, N_DEV // 2, 1))
    d_wdown = mm(f"{tag}_dwdown", "tn",
                 Blk(act, (None, tm, FF_BLK), lambda i, j, k: (i, k, 0)),
                 Blk(df, (tm, D_MODEL), lambda i, j, k: (k, 0)),
                 Blk(w_down_g.shape, (2, None, FF_BLK // 2, D_MODEL), lambda i, j, k: (i, 0, 0, 0)), BF16,
                 (N_DEV // 2, 1, nm))
    dhid, d_cw, d_cb = ffn_mid_bwd(f"{tag}_dmid", hid[0], hid[1], cw5, dact, layer, m=m, seq=seq)
    dhid = dhid.reshape(N_DEV, m, FF_BLK)
    dh = mm(f"{tag}_dh", "nt",
            Blk(dhid, (None, tm, FF_BLK), lambda i, j, k: (k, i, 0)),
            Blk(w_up_g, (None, None, D_MODEL, FF_BLK), lambda i, j, k: (k, 0, 0, 0)),
            Blk((m, D_MODEL), (tm, D_MODEL), lambda i, j, k: (i, 0)), BF16, (nm, 1, N_DEV))
    d_wup = mm(f"{tag}_dwup", "tn",
               Blk(dhid, (None, tm, FF_BLK), lambda i, j, k: (i, k, 0)),
               Blk(h, (tm, D_MODEL), lambda i, j, k: (k, 0)),
               Blk((N_DEV, 1, FF_BLK, D_MODEL), (None, None, FF_BLK, D_MODEL), lambda i, j, k: (i, 0, 0, 0)), BF16,
               (N_DEV, 1, nm))
    return dh, d_wup, d_cw, d_cb, d_wdown


def kernel(x, norm_gains, even_w_in, hgrn_lb_logits, hgrn_norm, rg_conv_w, rg_conv_b, rg_wa, rg_ba, rg_wx, rg_bx, rg_lambda, even_w_out, odd_w_in, fox_f_bias, odd_w_out, ffn_w_up, ffn_conv_w, ffn_conv_b, ffn_w_down, loss_target, m_norm_gains, m_even_w_in, m_hgrn_lb_logits, m_hgrn_norm, m_rg_conv_w, m_rg_conv_b, m_rg_wa, m_rg_ba, m_rg_wx, m_rg_bx, m_rg_lambda, m_even_w_out, m_odd_w_in, m_fox_f_bias, m_odd_w_out, m_ffn_w_up, m_ffn_conv_w, m_ffn_conv_b, m_ffn_w_down, v_norm_gains, v_even_w_in, v_hgrn_lb_logits, v_hgrn_norm, v_rg_conv_w, v_rg_conv_b, v_rg_wa, v_rg_ba, v_rg_wx, v_rg_bx, v_rg_lambda, v_even_w_out, v_odd_w_in, v_fox_f_bias, v_odd_w_out, v_ffn_w_up, v_ffn_conv_w, v_ffn_conv_b, v_ffn_w_down):
    local = dict(locals())
    w = {n: local[n] for n in NAMES}
    mom = {n: local["m_" + n] for n in NAMES}
    var = {n: local["v_" + n] for n in NAMES}
    n_batch, seq, _ = x.shape
    m = n_batch * seq
    tm = _div_tile(m, 512)
    tmm = _div_tile(m, MM_ROWS)
    nm = m // tmm

    gathered = all_gather("gather_weights", [w["even_w_in"].astype(BF16)] + [w[n] for n in SMALL_SHARDED])
    g = dict(zip(("even_w_in",) + SMALL_SHARDED, gathered))
    w_in_e = g["even_w_in"]
    gains = _cols_from_gather(g["norm_gains"])
    me = 4 * lax.axis_index("x") + 2 * lax.axis_index("y") + lax.axis_index("c")
    def own_block_only(name, blocks):
        lands = unwritten(name, [jax.ShapeDtypeStruct((N_DEV,) + t.shape, t.dtype) for t in blocks])
        return [lax.dynamic_update_index_in_dim(ld, t, me, 0) for ld, t in zip(lands, blocks)]

    def own_slots_only(name, sends):
        return [own_slot_only(t, ld, me) for t, ld in zip(sends, unwritten(name, sends))]

    behind = (g["norm_gains"][0, 0, 0, 0] * 0.0).astype(BF16)
    out0 = [w["even_w_out"].astype(BF16) + behind]
    out0_sent = gather_start("gather_out0_start", out0, own_block_only("land_out0", out0))
    behind = (out0_sent[4][0, 0] * 0.0).astype(BF16)
    ffn0 = [w["ffn_w_up"][0:1].astype(BF16) + behind, w["ffn_w_down"][0:1].astype(BF16) + behind]
    ffn0_sent = gather_start("gather_ffn0_start", ffn0, own_block_only("land_ffn0", ffn0))
    behind = (ffn0_sent[4][0, 0] * 0.0).astype(BF16)
    mix1w = [jnp.swapaxes(w["odd_w_in"], 1, 2).astype(BF16) + behind, w["odd_w_out"].astype(BF16) + behind]
    mix1_sent = gather_start("gather_mix1_start", mix1w, own_block_only("land_mix1", mix1w))
    behind = (mix1_sent[4][0, 0] * 0.0).astype(BF16)
    ffn1 = [w["ffn_w_up"][1:2].astype(BF16) + behind, w["ffn_w_down"][1:2].astype(BF16) + behind]
    ffn1_sent = gather_start("gather_ffn1_start", ffn1, own_block_only("land_ffn1", ffn1))
    started = ffn1_sent[4]
    rg_cw = _cols_from_gather(g["rg_conv_w"])[0]
    n_layer = ffn_conv_w.shape[0]
    cw5 = g["ffn_conv_w"].reshape(2, N_DEV // 2, n_layer, FFN_CONV, FF_BLK)
    cb5 = ffn_conv_b.reshape(n_layer, 2, N_DEV // 2, 1, FF_BLK)
    gain = lambda l, k: gains[l, k:k + 1, :]
    wa_bd, wx_bd = _block_diag(rg_wa[0]), _block_diag(rg_wx[0])
    fbias = jnp.pad(fox_f_bias, ((0, 0), (0, LANES - C_HEADS)))

    x0 = x.reshape(m, D_MODEL)
    tgt = loss_target.reshape(m, D_MODEL)

    (h0,) = tile_fwd("l0_prenorm", fn_prenorm_after, m=m, tm=tm, nj=1, rows=[Row(x0)], pars=[Par(gain(0, 0)), Par(started)],
                     outs=[Out(D_MODEL, BF16)])
    z0 = mm("l0_in", "nn",
            Blk(h0, (tmm, D_MODEL), lambda i, j, k: (i, 0)),
            Blk(w_in_e, (2, None, D_MODEL, 384), lambda i, j, k: (j, 0, 0, 0)),
            Blk((m, 3072), (tmm, 768), lambda i, j, k: (i, j)), F32, (nm, N_DEV // 2, 1), b_join=True)
    oa, sprev = hgrn_fwd("l0_hgrn", z0, hgrn_lb_logits, hgrn_norm, n_batch=n_batch, seq=seq)
    rg_rows = lambda: [Row(z0, LANES, 16), Row(z0, LANES, 20)]
    rg_pars = lambda: [Par(rg_cw, "col", LANES), Par(rg_conv_b, "col", LANES), Par(wa_bd, "row", LANES), Par(rg_ba, "col", LANES),
                       Par(wx_bd, "row", LANES), Par(rg_bx, "col", LANES), Par(rg_lambda, "col", LANES)]
    (ob,) = tile_fwd("l0_rglru", fn_rglru, m=m, tm=seq, nj=B_WIDTH // LANES, rows=rg_rows(), pars=rg_pars(),
                     outs=[Out(B_WIDTH, BF16, LANES)])
    mixcat0 = jnp.concatenate([oa, ob], axis=-1)
    (g_out_e,) = gather_wait("gather_out0_wait", out0_sent[0], out0_sent[1], out0_sent[2], out0_sent[3], mixcat0)
    w_out_e = g_out_e.reshape(D_MODEL, D_MODEL)
    mix0 = mm2d("l0_out", "nn", mixcat0, w_out_e)
    x1, h1 = tile_fwd("l0_postnorm", fn_addnorm2, m=m, tm=tm, nj=1, rows=[Row(x0), Row(mix0)], pars=[Par(gain(0, 1)), Par(gain(0, 2))],
                      outs=[Out(D_MODEL, F32), Out(D_MODEL, BF16)])
    w_up_g0, w_down_g0 = gather_wait("gather_ffn0_wait", ffn0_sent[0], ffn0_sent[1], ffn0_sent[2], ffn0_sent[3], h1)
    hid0, act0, f0 = _ffn_forward("l0_ffn", 0, h1, w_up_g0, cw5, cb5, w_down_g0, m, seq)
    x2, h2 = tile_fwd("l0_ffnnorm", fn_addnorm2, m=m, tm=tm, nj=1, rows=[Row(x1), Row(f0)], pars=[Par(gain(0, 3)), Par(gain(1, 0))],
                      outs=[Out(D_MODEL, F32), Out(D_MODEL, BF16)])

    g_in_o, g_out_o = gather_wait("gather_mix1_wait", mix1_sent[0], mix1_sent[1], mix1_sent[2], mix1_sent[3], h2)
    w_in_o_t = jnp.pad(g_in_o.reshape(3088, D_MODEL), ((0, 3200 - 3088), (0, 0)))
    w_out_o = g_out_o.reshape(D_MODEL, D_MODEL)
    z1 = mm2d("l1_in", "nt", h2, w_in_o_t)
    (cgate,) = tile_fwd("l1_gate", fn_fox_gate, m=m, tm=seq, nj=1, rows=[Row(z1, LANES, 3072 // LANES)], pars=[Par(fbias)],
                        outs=[Out(LANES, F32)])
    place, ones_q, ones_k = term_placement()
    qterm, kterm = tile_fwd("l1_terms", fn_fox_terms, m=m, tm=tm, nj=1, rows=[Row(cgate)],
                            pars=[Par(place), Par(ones_q), Par(ones_k)], outs=[Out(TERM_W, BF16), Out(TERM_W, BF16)])
    oc, lse = fox_pair_fwd("l1_attn", z1, qterm, kterm, n_batch=n_batch, seq=seq)
    blk_b = min(ATT_BLK, seq)
    lse = lse.reshape(n_batch, N_PAIR, -1, 2, lse.shape[-1] // blk_b, blk_b).swapaxes(3, 4).reshape(n_batch, N_PAIR, seq // blk_b, 2, 1, blk_b)
    mix1 = mm2d("l1_out", "nn", oc, w_out_o)
    x3, h3 = tile_fwd("l1_postnorm", fn_addnorm2, m=m, tm=tm, nj=1, rows=[Row(x2), Row(mix1)], pars=[Par(gain(1, 1)), Par(gain(1, 2))],
                      outs=[Out(D_MODEL, F32), Out(D_MODEL, BF16)])
    w_up_g1, w_down_g1 = gather_wait("gather_ffn1_wait", ffn1_sent[0], ffn1_sent[1], ffn1_sent[2], ffn1_sent[3], h3)
    hid1, act1, f1 = _ffn_forward("l1_ffn", 1, h3, w_up_g1, cw5, cb5, w_down_g1, m, seq)
    dy, df1, loss_part, d_g13 = loss_head("loss", x3, f1, tgt, gain(1, 3), m=m, tm=tm)
    dh3, d_wup1, d_cw1, d_cb1, d_wdown1 = _ffn_backward("l1_ffn", 1, df1, h3, hid1, act1, w_up_g1, cw5, cb5, w_down_g1, m, seq)
    dx2, dmix1, d_g11, d_g12 = tile_bwd("l1_dpostnorm", fn_addnorm2, m=m, tm=tm, nj=1, rows=[Row(x2), Row(mix1)],
                                        pars=[Par(gain(1, 1)), Par(gain(1, 2))], cts=[Row(dy), Row(dh3)],
                                        drows=[Out(D_MODEL, F32), Out(D_MODEL, BF16)])
    doc = mm2d("l1_doc", "nt", dmix1, w_out_o, BF16)
    d_wout_o = mm2d("l1_dwout", "tn", oc, dmix1)
    dq, dk, dv, dc = fox_pair_bwd("l1_dattn", z1, qterm, kterm, oc, doc, lse, n_batch=n_batch, seq=seq)
    dzf, d_fbias = tile_bwd("l1_dgate", fn_fox_gate, m=m, tm=seq, nj=1, rows=[Row(z1, LANES, 3072 // LANES)], pars=[Par(fbias)],
                            cts=[Row(dc)], drows=[Out(LANES, BF16)])
    dz1 = jnp.concatenate([dq, dk, dv, dzf], axis=-1)
    dh2 = mm2d("l1_dh", "nn", dz1, w_in_o_t, BF16)
    d_win_o_t = mm2d("l1_dwin", "tn", dz1, h2, BF16)

    send1 = [d_win_o_t[:3088].reshape(N_DEV, 1, 3088 // N_DEV, D_MODEL),
             d_wout_o.reshape(N_DEV, 1, D_MODEL // N_DEV, D_MODEL).astype(BF16), d_wup1, d_wdown1]
    sent1 = _split_exchange("exchange_l1_start", send1, own_slots_only("land_l1", send1), None, None)

    dx1, df0, d_g03, d_g10 = tile_bwd("l0_dffnnorm", fn_addnorm2_after, m=m, tm=tm, nj=1, rows=[Row(x1), Row(f0)],
                                      pars=[Par(gain(0, 3)), Par(gain(1, 0)), Par(sent1[4])], cts=[Row(dx2), Row(dh2)],
                                      drows=[Out(D_MODEL, F32), Out(D_MODEL, BF16)])[:4]
    dh1, d_wup0, d_cw0, d_cb0, d_wdown0 = _ffn_backward("l0_ffn", 0, df0, h1, hid0, act0, w_up_g0, cw5, cb5, w_down_g0, m, seq)
    send0 = [d_wup0, d_wdown0]
    sent0 = _split_exchange("exchange_ffn0_start", send0, own_slots_only("land_dffn0", send0), None, None)
    dx0a, dmix0, d_g01, d_g02 = tile_bwd("l0_dpostnorm", fn_addnorm2_after, m=m, tm=tm, nj=1, rows=[Row(x0), Row(mix0)],
                                         pars=[Par(gain(0, 1)), Par(gain(0, 2)), Par(sent0[4])], cts=[Row(dx1), Row(dh1)],
                                         drows=[Out(D_MODEL, F32), Out(D_MODEL, BF16)])[:4]
    dmixcat0 = mm2d("l0_dmixcat", "nt", dmix0, w_out_e, BF16)
    d_wout_e = mm2d("l0_dwout", "tn", mixcat0, dmix0)
    dzq, dzf0, dzv, dzg, d_lb, d_hnorm = hgrn_bwd("l0_dhgrn", z0, sprev, hgrn_lb_logits, hgrn_norm, dmixcat0, n_batch=n_batch, seq=seq)
    dzx, dzy, d_rcw, d_rcb, d_wa, d_ba, d_wx, d_bx, d_lam = tile_bwd(
        "l0_drglru", fn_rglru, m=m, tm=seq, nj=B_WIDTH // LANES, rows=rg_rows(), pars=rg_pars(),
        cts=[Row(dmixcat0, LANES, A_WIDTH // LANES)], drows=[Out(B_WIDTH, BF16, LANES), Out(B_WIDTH, BF16, LANES)])
    dz0 = jnp.concatenate([dzq, dzf0, dzv, dzg, dzx, dzy], axis=-1)
    d_win_e = mm("l0_dwin", "tn",
                 Blk(h0, (tmm, D_MODEL), lambda i, j, k: (k, 0)),
                 Blk(dz0, (tmm, 768), lambda i, j, k: (k, j)),
                 Blk(w_in_e.shape, (2, None, D_MODEL, 384), lambda i, j, k: (j, 0, 0, 0)), BF16, (1, N_DEV // 2, nm), o_split=True)
    send_e = [d_win_e, d_wout_e.reshape(N_DEV, 1, D_MODEL // N_DEV, D_MODEL).astype(BF16)]
    sent_e = _split_exchange("exchange_even_start", send_e, own_slots_only("land_even", send_e), None, None)
    d_ffn_cb = jnp.stack([d_cb0, d_cb1]).reshape(n_layer, 2 * D_FF)
    rep = {"hgrn_lb_logits": d_lb, "hgrn_norm": d_hnorm, "rg_conv_b": d_rcb, "rg_wa": _block_diag_grad(d_wa)[None], "rg_ba": d_ba,
           "rg_wx": _block_diag_grad(d_wx)[None], "rg_bx": d_bx, "rg_lambda": d_lam, "fox_f_bias": d_fbias[:, :C_HEADS],
           "ffn_conv_b": d_ffn_cb}
    rep_blocks = [rep[n] for n in REPLICATED] + [loss_part]
    rep_sent = gather_start("gather_partials_start", rep_blocks, own_block_only("land_partials", rep_blocks))
    dh0 = mm("l0_dh", "nt",
             Blk(dz0, (tmm, 768), lambda i, j, k: (i, k)),
             Blk(w_in_e, (2, None, D_MODEL, 384), lambda i, j, k: (k, 0, 0, 0)),
             Blk((m, D_MODEL), (tmm, D_MODEL), lambda i, j, k: (i, 0)), BF16, (nm, 1, N_DEV // 2), after=sent_e[4] + rep_sent[4],
             b_join=True)
    dx0, d_g00 = tile_bwd("l0_dprenorm", fn_input_norm, m=m, tm=tm, nj=1, rows=[Row(x0)], pars=[Par(gain(0, 0))],
                          cts=[Row(dx0a), Row(dh0)], drows=[Out(D_MODEL, F32)])

    d_gains = jnp.stack([jnp.concatenate([d_g00, d_g01, d_g02, d_g03], axis=0), jnp.concatenate([d_g10, d_g11, d_g12, d_g13], axis=0)])
    d_ffn_cw = jnp.stack([d_cw0, d_cw1], axis=2).reshape(N_DEV, n_layer, FFN_CONV, FF_BLK)
    r_in_o, r_out_o, r_up1, r_down1 = _split_exchange("exchange_l1_wait", sent1[2], sent1[3], sent1[:2], dx0)
    r_up0, r_down0 = _split_exchange("exchange_ffn0_wait", sent0[2], sent0[3], sent0[:2], dx0)
    r_in_e, r_out_e = _split_exchange("exchange_even_wait", sent_e[2], sent_e[3], sent_e[:2], dx0)
    recv, res = {}, {}
    flipped = ("odd_w_in", "ffn_w_up")
    view = lambda n, t: jnp.swapaxes(t, 1, 2) if n in flipped else t
    for n, r in (("even_w_in", r_in_e), ("even_w_out", r_out_e), ("odd_w_in", r_in_o), ("odd_w_out", r_out_o)):
        res[n] = [view(n, t) for t in adam_tiled("adam_" + n, r, view(n, w[n]), view(n, mom[n]), view(n, var[n]))]
    for n, parts_l in (("ffn_w_up", (r_up0, r_up1)), ("ffn_w_down", (r_down0, r_down1))):
        wmv = (view(n, w[n]), view(n, mom[n]), view(n, var[n]))
        first_layer = adam_tiled(f"adam_{n}_0", parts_l[0], *wmv, layer=0)
        res[n] = [view(n, t) for t in adam_tiled(f"adam_{n}_1", parts_l[1], *wmv, layer=1, prev=first_layer)]
    small_send = [_cols_to_blocks(d_gains), _cols_to_blocks(d_rcw[None]), d_ffn_cw]
    recv.update(zip(SMALL_SHARDED, all_to_all("exchange_small", small_send)))

    parts = gather_wait("gather_partials_wait", rep_sent[0], rep_sent[1], rep_sent[2], rep_sent[3], dx0)
    for n, p in zip(REPLICATED, parts):
        recv[n] = p
    small = SMALL_SHARDED + REPLICATED
    small_res, (loss_sum,) = adam_small("adam_small", [(recv[n], w[n], mom[n], var[n]) for n in small], [parts[-1]])
    res.update(dict(zip(small, small_res)))

    out = [loss_sum[0, 0], dx0.reshape(x.shape)]
    for k in range(4):
        out += [res[n][k] for n in NAMES]
    return tuple(out)
```

```python
import functools

import jax
import jax.numpy as jnp
from jax import lax
from jax.experimental import pallas as pl
from jax.experimental.pallas import tpu as pltpu

F32 = jnp.float32
BF16 = jnp.bfloat16

D_MODEL = 1024
A_HEADS = 4
A_WIDTH = 512
HGRN_CHUNK = 64
HGRN_SEG = 2048
B_WIDTH = 512
B_BLOCKS = 8
B_BLOCK_DIM = 64
B_CONV = 4
RG_C = 8.0
C_HEADS = 16
C_HEAD_DIM = 64
D_FF = 2816
FFN_CONV = 3
EPS = 1e-6
LANES = 128
HALO = 16
N_DEV = 8
FF_BLK = 2 * D_FF // N_DEV
MESH = pl.DeviceIdType.MESH
NEG = -1e30
VMEM_LIMIT = 56 * 1024 * 1024
MM_ROWS = 2048

ADAM_LR = 0.001
ADAM_B1 = 0.9
ADAM_B2 = 0.999
ADAM_EPS = 1e-08
ADAM_WD = 0.01
ADAM_STEP = 10


def _dg(a, b, pat):
    nb = a.ndim - 2
    batch = (tuple(range(nb)), tuple(range(nb)))
    ca = a.ndim - 1 if pat[0] == "n" else a.ndim - 2
    cb = b.ndim - 2 if pat[1] == "n" else b.ndim - 1
    return lax.dot_general(a.astype(BF16), b.astype(BF16), (((ca,), (cb,)), batch), preferred_element_type=F32)


@functools.partial(jax.custom_vjp, nondiff_argnums=(2,))
def bdot(a, b, pat):
    return _dg(a, b, pat)


def _bdot_fwd(a, b, pat):
    return _dg(a, b, pat), (a, b)


def _bdot_bwd(pat, res, g):
    a, b = res
    if pat == "nn":
        return _dg(g, b, "nt"), _dg(a, g, "tn")
    if pat == "nt":
        return _dg(g, b, "nn"), _dg(g, a, "tn")
    return _dg(b, g, "nt"), _dg(a, g, "nn")


bdot.defvjp(_bdot_fwd, _bdot_bwd)


def _shift_raw(x, s, up, fill):
    if s == 0:
        return x
    n = x.shape[0]
    r = pltpu.roll(x, (n - s) if up else s, 0)
    idx = lax.broadcasted_iota(jnp.int32, x.shape, 0)
    mask = (idx >= n - s) if up else (idx < s)
    return jnp.where(mask, jnp.asarray(fill, x.dtype), r)


@functools.partial(jax.custom_vjp, nondiff_argnums=(1,))
def shift_down(x, s):
    return _shift_raw(x, s, False, 0.0)


def _shift_down_fwd(x, s):
    return _shift_raw(x, s, False, 0.0), None


def _shift_down_bwd(s, _, g):
    return (_shift_raw(g, s, True, 0.0),)


shift_down.defvjp(_shift_down_fwd, _shift_down_bwd)


def _scan_impl(a, u, up):
    n = a.shape[0]
    s = 1
    while s < n:
        u = a * _shift_raw(u, s, up, 0.0) + u
        if 2 * s < n:
            a = a * _shift_raw(a, s, up, 1.0)
        s *= 2
    return u


@jax.custom_vjp
def lin_scan(a, u):
    return _scan_impl(a, u, False)


def _lin_scan_fwd(a, u):
    h = _scan_impl(a, u, False)
    return h, (a, h)


def _lin_scan_bwd(res, g):
    a, h = res
    gh = _scan_impl(_shift_raw(a, 1, True, 0.0), g, True)
    return gh * _shift_raw(h, 1, False, 0.0), gh


lin_scan.defvjp(_lin_scan_fwd, _lin_scan_bwd)


def _cumsum_impl(x, up, period):
    n = x.shape[0]
    span = n if period is None else period
    idx = lax.broadcasted_iota(jnp.int32, x.shape, 0)
    pos = idx if period is None else idx % period
    s = 1
    while s < span:
        sh = _shift_raw(x, s, up, 0.0)
        if period is not None:
            keep = (pos < period - s) if up else (pos >= s)
            sh = jnp.where(keep, sh, 0.0)
        x = x + sh
        s *= 2
    return x


@functools.partial(jax.custom_vjp, nondiff_argnums=(1,))
def cumsum_rows(x, period):
    return _cumsum_impl(x, False, period)


def _cumsum_fwd(x, period):
    return _cumsum_impl(x, False, period), None


def _cumsum_bwd(period, _, g):
    return (_cumsum_impl(g, True, period),)


cumsum_rows.defvjp(_cumsum_fwd, _cumsum_bwd)


def _sigmoid(x):
    return jax.nn.sigmoid(x)


def _expm1(x):
    return jnp.tanh(0.5 * x) * (jnp.exp(x) + 1.0)


def _softplus(x):
    return jnp.maximum(x, 0.0) + jnp.log(1.0 + jnp.exp(-jnp.abs(x)))


def _rms(x, g):
    return x * lax.rsqrt(jnp.mean(x * x, axis=-1, keepdims=True) + EPS) * g


def fn_prenorm(x, g):
    return (_rms(x, g).astype(BF16),)


def fn_prenorm_after(x, g, _token):
    return fn_prenorm(x, g)


def fn_addnorm2(x, y, g_post, g_pre):
    x1 = x + _rms(y, g_post)
    return x1, _rms(x1, g_pre).astype(BF16)


def fn_addnorm2_after(x, y, g_post, g_pre, _token):
    return fn_addnorm2(x, y, g_post, g_pre)


def fn_input_norm(x, g):
    return x, _rms(x, g).astype(BF16)


def _causal_conv(x, w, b, taps):
    c = b
    for k in range(taps):
        c = c + w[k:k + 1, :] * shift_down(x, taps - 1 - k)
    return c


def fn_rglru(xb, yb, cw, cb, wa, ba, wx, bx, lam):
    xf = _causal_conv(xb, cw, cb, B_CONV)
    r = _sigmoid(bdot(xf, wa, "nn") + ba)
    i = _sigmoid(bdot(xf, wx, "nn") + bx)
    log_a = -RG_C * r * _softplus(-lam)
    a = jnp.exp(log_a)
    u = jnp.sqrt(-_expm1(2.0 * log_a)) * (i * xf)
    h = lin_scan(a, u)
    return ((h * jax.nn.gelu(yb)).astype(BF16),)


def fn_fox_gate(zf, bias):
    return (cumsum_rows(jax.nn.log_sigmoid(zf + bias), None),)


def fn_hgrn_seg(q, fl, v, g, st, logits, hn):
    rows = q.shape[0]
    nc = rows // HGRN_CHUNK
    l0, l1, l2 = logits[0:1, :], logits[1:2, :], logits[2:3, :]
    mx = jnp.maximum(jnp.maximum(l0, l1), l2)
    e0, e1, e2 = jnp.exp(l0 - mx), jnp.exp(l1 - mx), jnp.exp(l2 - mx)
    lb = e0 / (e0 + e1 + e2)
    forget = lb + (1.0 - lb) * _sigmoid(fl)
    qs = q * _sigmoid(q)
    kk = 1.0 - forget
    logf = jnp.log(forget)
    bcum = cumsum_rows(logf, HGRN_CHUNK)
    c3 = lambda t: t.reshape(nc, HGRN_CHUNK, 128)
    b_last = jnp.sum(c3(logf), axis=1, keepdims=True)
    bcum3 = c3(bcum)
    q_dec = c3(qs) * jnp.exp(bcum3)
    k_dec = c3(kk) * jnp.exp(-bcum3)
    k_upd = c3(kk) * jnp.exp(b_last - bcum3)
    v3 = c3(v)
    scores = bdot(q_dec, k_dec, "nt")
    ri = lax.broadcasted_iota(jnp.int32, scores.shape, 1)
    ci = lax.broadcasted_iota(jnp.int32, scores.shape, 2)
    scores = jnp.where(ri >= ci, scores, 0.0)
    o = bdot(scores, v3, "nn")
    upd_t = bdot(v3, k_upd, "tn")
    dec = jnp.exp(b_last)
    prev = []
    for n in range(nc):
        prev.append(st)
        st = st * dec[n] + upd_t[n]
    o = o + bdot(q_dec, jnp.stack(prev), "nt")
    o = o.reshape(rows, 128)
    o = o * lax.rsqrt(jnp.mean(o * o, axis=-1, keepdims=True) + EPS) * hn
    return (o * _sigmoid(g)).astype(BF16), st


def _ffn_conv(xg, xv, cw, cb):
    cg = _causal_conv(xg, cw[0], cb[0], FFN_CONV)[HALO:]
    cv = _causal_conv(xv, cw[1], cb[1], FFN_CONV)[HALO:]
    return cg, cv


def _ffn_gate(cg, cv):
    return jax.nn.gelu(cg) * cv


class Row:
    def __init__(self, arr, cb=None, off=0):
        self.arr, self.cb, self.off = arr, cb, off

    def spec(self, tm):
        if self.cb is None:
            return pl.BlockSpec((tm, self.arr.shape[1]), lambda j, i: (i, 0))
        off = self.off
        return pl.BlockSpec((tm, self.cb), lambda j, i: (i, j + off))


class Par:
    def __init__(self, arr, kind="full", bs=None):
        self.arr, self.kind, self.bs = arr, kind, bs

    def block(self):
        if self.kind == "full":
            return self.arr.shape
        if self.kind == "col":
            return (self.arr.shape[0], self.bs)
        return (self.bs, self.arr.shape[1])

    def spec(self):
        if self.kind == "full":
            return pl.BlockSpec(self.block(), lambda j, i: (0, 0))
        if self.kind == "col":
            return pl.BlockSpec(self.block(), lambda j, i: (0, j))
        return pl.BlockSpec(self.block(), lambda j, i: (j, 0))


class Out:
    def __init__(self, width, dtype, cb=None, off=0):
        self.width, self.dtype, self.cb, self.off = width, dtype, cb, off

    def spec(self, tm):
        if self.cb is None:
            return pl.BlockSpec((tm, self.width), lambda j, i: (i, 0))
        off = self.off
        return pl.BlockSpec((tm, self.cb), lambda j, i: (i, j + off))


def _params(sem):
    return pltpu.CompilerParams(dimension_semantics=sem, vmem_limit_bytes=VMEM_LIMIT)


def tile_fwd(name, fn, *, m, tm, nj, rows, pars, outs, n_acc=0):
    n_r, n_p, n_o = len(rows), len(pars), len(outs)

    def body(*refs):
        ins = [r[...] for r in refs[:n_r + n_p]]
        res = fn(*ins)
        o_refs = refs[n_r + n_p:]
        for k in range(n_o):
            o_refs[k][...] = res[k].astype(o_refs[k].dtype)
        first = jnp.logical_and(pl.program_id(0) == 0, pl.program_id(1) == 0)
        for k in range(n_acc):
            ref = o_refs[n_o + k]

            @pl.when(first)
            def _():
                ref[...] = jnp.zeros_like(ref)

            ref[...] += res[n_o + k]

    out_shape = [jax.ShapeDtypeStruct((m, o.width), o.dtype) for o in outs]
    out_specs = [o.spec(tm) for o in outs]
    for _ in range(n_acc):
        out_shape.append(jax.ShapeDtypeStruct((1, LANES), F32))
        out_specs.append(pl.BlockSpec((1, LANES), lambda j, i: (0, 0)))
    sem = ("arbitrary", "arbitrary") if n_acc else ("parallel", "parallel")
    return pl.pallas_call(
        body, grid=(nj, m // tm), name=name,
        in_specs=[r.spec(tm) for r in rows] + [p.spec() for p in pars],
        out_specs=out_specs, out_shape=out_shape, compiler_params=_params(sem),
    )(*[r.arr for r in rows], *[p.arr for p in pars])


def tile_bwd(name, fn, *, m, tm, nj, rows, pars, cts, drows):
    n_r, n_p, n_c = len(rows), len(pars), len(cts)
    want = [k for k in range(n_r) if drows[k] is not None]

    def body(*refs):
        ins = [r[...] for r in refs[:n_r + n_p]]
        ct = [r[...] for r in refs[n_r + n_p:n_r + n_p + n_c]]
        o_refs = refs[n_r + n_p + n_c:]
        res, vjp = jax.vjp(fn, *ins)
        grads = vjp(tuple(c.astype(r.dtype) for c, r in zip(ct, res)))
        for pos, k in enumerate(want):
            o_refs[pos][...] = grads[k].astype(o_refs[pos].dtype)
        for k in range(n_p):
            ref = o_refs[len(want) + k]
            first = pl.program_id(1) == 0
            if pars[k].kind == "full":
                first = jnp.logical_and(first, pl.program_id(0) == 0)

            @pl.when(first)
            def _():
                ref[...] = jnp.zeros_like(ref)

            ref[...] += grads[n_r + k].astype(F32)

    out_shape = [jax.ShapeDtypeStruct((m, drows[k].width), drows[k].dtype) for k in want]
    out_specs = [drows[k].spec(tm) for k in want]
    for p in pars:
        out_shape.append(jax.ShapeDtypeStruct(p.arr.shape, F32))
        out_specs.append(p.spec())
    return pl.pallas_call(
        body, grid=(nj, m // tm), name=name,
        in_specs=[r.spec(tm) for r in rows] + [p.spec() for p in pars] + [c.spec(tm) for c in cts],
        out_specs=out_specs, out_shape=out_shape, compiler_params=_params(("arbitrary", "arbitrary")),
    )(*[r.arr for r in rows], *[p.arr for p in pars], *[c.arr for c in cts])


def loss_head(name, x, y, tgt, g, *, m, tm):
    def body(x_ref, y_ref, t_ref, g_ref, dout_ref, dy_ref, loss_ref, dg_ref):
        normed, vjp = jax.vjp(_rms, y_ref[...], g_ref[...])
        err = x_ref[...] + normed - t_ref[...]
        dout = err * (1.0 / D_MODEL)
        dy, dg = vjp(dout)
        dout_ref[...] = dout
        dy_ref[...] = dy.astype(dy_ref.dtype)

        @pl.when(pl.program_id(0) == 0)
        def _():
            loss_ref[...] = jnp.zeros_like(loss_ref)
            dg_ref[...] = jnp.zeros_like(dg_ref)

        loss_ref[...] += 0.5 * jnp.sum(jnp.mean(err * err, axis=-1, keepdims=True), axis=0, keepdims=True)
        dg_ref[...] += dg

    row = pl.BlockSpec((tm, D_MODEL), lambda i: (i, 0))
    whole = lambda w: pl.BlockSpec((1, w), lambda i: (0, 0))
    return pl.pallas_call(
        body, grid=(m // tm,), name=name, in_specs=[row, row, row, whole(D_MODEL)],
        out_specs=[row, row, whole(LANES), whole(D_MODEL)],
        out_shape=[jax.ShapeDtypeStruct((m, D_MODEL), F32), jax.ShapeDtypeStruct((m, D_MODEL), BF16),
                   jax.ShapeDtypeStruct((1, LANES), F32), jax.ShapeDtypeStruct((1, D_MODEL), F32)],
        compiler_params=_params(("arbitrary",)),
    )(x, y, tgt, g)


class Blk:
    def __init__(self, arr, block, index):
        self.arr, self.block, self.index = arr, block, index

    def spec(self):
        return pl.BlockSpec(self.block, self.index)


def _flat2(v):
    return v if v.ndim == 2 else v.reshape(-1, v.shape[-1])


def mm(name, pat, a, b, o, out_dtype, grid, after=None, b_join=False, o_split=False):
    nk = grid[2]
    o_shape = o.arr

    def put(o_ref, r):
        if o_split:
            half = r.shape[1] // 2
            o_ref[0] = r[:, :half].astype(out_dtype)
            o_ref[1] = r[:, half:].astype(out_dtype)
        else:
            o_ref[...] = r.astype(out_dtype).reshape(o_ref.shape)

    def body(*refs):
        a_ref, b_ref = refs[0], refs[1]
        o_ref = refs[3] if after is not None else refs[2]
        bv = jnp.concatenate([b_ref[0], b_ref[1]], axis=1) if b_join else _flat2(b_ref[...])
        r = _dg(_flat2(a_ref[...]), bv, pat)
        if nk == 1:
            put(o_ref, r)
            return
        acc_ref = refs[-1]
        kk = pl.program_id(2)

        @pl.when(kk == 0)
        def _():
            acc_ref[...] = r

        @pl.when(kk > 0)
        def _():
            acc_ref[...] += r

        @pl.when(kk == nk - 1)
        def _():
            put(o_ref, acc_ref[...])

    ob = [d for d in o.block if d is not None]
    if o_split:
        acc_shape = (ob[1], 2 * ob[2])
    else:
        acc_shape = (ob[0], ob[1]) if len(ob) == 2 else (ob[0] * ob[1], ob[2])
    in_specs = [a.spec(), b.spec()]
    args = [a.arr, b.arr]
    if after is not None:
        in_specs.append(pl.BlockSpec(memory_space=pl.ANY))
        args.append(after)
    return pl.pallas_call(
        body, grid=grid, name=name, in_specs=in_specs, out_specs=o.spec(),
        out_shape=jax.ShapeDtypeStruct(o_shape, out_dtype),
        scratch_shapes=[pltpu.VMEM(acc_shape, F32)] if nk > 1 else [],
        compiler_params=_params(("parallel", "parallel", "arbitrary")),
    )(*args)


def _div_tile(n, cap):
    if n <= cap:
        return n
    best = 128
    for t in range(128, cap + 1, 128):
        if n % t == 0:
            best = t
    return best


def mm2d(name, pat, a, b, out_dtype=F32):
    if pat == "tn":
        k, m = a.shape
    else:
        m, k = a.shape
    n = b.shape[0] if pat == "nt" else b.shape[1]
    tm, tn, tk = _div_tile(m, MM_ROWS), _div_tile(n, 1024), _div_tile(k, MM_ROWS)
    a_blk = Blk(a, (tk, tm), lambda i, j, kk: (kk, i)) if pat == "tn" else Blk(a, (tm, tk), lambda i, j, kk: (i, kk))
    b_blk = Blk(b, (tn, tk), lambda i, j, kk: (j, kk)) if pat == "nt" else Blk(b, (tk, tn), lambda i, j, kk: (kk, j))
    o_blk = Blk((m, n), (tm, tn), lambda i, j, kk: (i, j))
    return mm(name, pat, a_blk, b_blk, o_blk, out_dtype, (m // tm, n // tn, k // tk))


def hgrn_fwd(name, z, logits, hnorm, *, n_batch, seq):
    m = n_batch * seq
    ts = min(HGRN_SEG, seq)
    n_seg = seq // ts

    def body(q_ref, f_ref, v_ref, g_ref, lg_ref, hn_ref, o_ref, sp_ref, st_ref):
        s = pl.program_id(2)

        @pl.when(s == 0)
        def _():
            st_ref[...] = jnp.zeros_like(st_ref)

        st = st_ref[...]
        sp_ref[...] = st
        o, st_new = fn_hgrn_seg(q_ref[...], f_ref[...], v_ref[...], g_ref[...], st, lg_ref[...], hn_ref[...])
        o_ref[...] = o
        st_ref[...] = st_new

    part = lambda p: pl.BlockSpec((ts, 128), lambda h, b, s: (b * n_seg + s, 4 * p + h))
    return pl.pallas_call(
        body, grid=(A_HEADS, n_batch, n_seg), name=name,
        in_specs=[part(0), part(1), part(2), part(3),
                  pl.BlockSpec((3, 128), lambda h, b, s: (0, h)),
                  pl.BlockSpec((1, 128), lambda h, b, s: (0, h))],
        out_specs=[pl.BlockSpec((ts, 128), lambda h, b, s: (b * n_seg + s, h)),
                   pl.BlockSpec((128, 128), lambda h, b, s: ((b * n_seg + s) * A_HEADS + h, 0))],
        out_shape=[jax.ShapeDtypeStruct((m, A_WIDTH), BF16),
                   jax.ShapeDtypeStruct((n_batch * n_seg * A_HEADS * 128, 128), F32)],
        scratch_shapes=[pltpu.VMEM((128, 128), F32)],
        compiler_params=_params(("arbitrary", "arbitrary", "arbitrary")),
    )(z, z, z, z, logits, hnorm)


def hgrn_bwd(name, z, sprev, logits, hnorm, do, *, n_batch, seq):
    m = n_batch * seq
    ts = min(HGRN_SEG, seq)
    n_seg = seq // ts

    def body(q_ref, f_ref, v_ref, g_ref, sp_ref, lg_ref, hn_ref, do_ref, dq_ref, df_ref, dv_ref, dg_ref, dlg_ref, dhn_ref, dst_ref):
        s = pl.program_id(2)

        @pl.when(s == 0)
        def _():
            dst_ref[...] = jnp.zeros_like(dst_ref)

        res, vjp = jax.vjp(fn_hgrn_seg, q_ref[...], f_ref[...], v_ref[...], g_ref[...], sp_ref[...], lg_ref[...], hn_ref[...])
        dq, df, dv, dg, dst, dlg, dhn = vjp((do_ref[...].astype(res[0].dtype), dst_ref[...]))
        dq_ref[...] = dq.astype(dq_ref.dtype)
        df_ref[...] = df.astype(df_ref.dtype)
        dv_ref[...] = dv.astype(dv_ref.dtype)
        dg_ref[...] = dg.astype(dg_ref.dtype)
        dst_ref[...] = dst
        first = jnp.logical_and(pl.program_id(1) == 0, s == 0)

        @pl.when(first)
        def _():
            dlg_ref[...] = jnp.zeros_like(dlg_ref)
            dhn_ref[...] = jnp.zeros_like(dhn_ref)

        dlg_ref[...] += dlg
        dhn_ref[...] += dhn

    rev = lambda b, s: b * n_seg + (n_seg - 1 - s)
    part = lambda p: pl.BlockSpec((ts, 128), lambda h, b, s: (rev(b, s), 4 * p + h))
    head = pl.BlockSpec((ts, 128), lambda h, b, s: (rev(b, s), h))
    dpart = jax.ShapeDtypeStruct((m, A_WIDTH), BF16)
    return pl.pallas_call(
        body, grid=(A_HEADS, n_batch, n_seg), name=name,
        in_specs=[part(0), part(1), part(2), part(3),
                  pl.BlockSpec((128, 128), lambda h, b, s: (rev(b, s) * A_HEADS + h, 0)),
                  pl.BlockSpec((3, 128), lambda h, b, s: (0, h)),
                  pl.BlockSpec((1, 128), lambda h, b, s: (0, h)),
                  head],
        out_specs=[head, head, head, head,
                   pl.BlockSpec((3, 128), lambda h, b, s: (0, h)),
                   pl.BlockSpec((1, 128), lambda h, b, s: (0, h))],
        out_shape=[dpart, dpart, dpart, dpart,
                   jax.ShapeDtypeStruct(logits.shape, F32),
                   jax.ShapeDtypeStruct(hnorm.shape, F32)],
        scratch_shapes=[pltpu.VMEM((128, 128), F32)],
        compiler_params=_params(("arbitrary", "arbitrary", "arbitrary")),
    )(z, z, z, z, sprev, logits, hnorm, do)


FFN_ROWS = 1024
FFN_LANES = 128


def _ffn_tiles(m, seq):
    tm = min(FFN_ROWS, seq)
    return tm, seq // tm, m // tm


def ffn_mid_fwd(name, hid, cw, cb, layer, *, m, seq):
    tm, n_t, n_i = _ffn_tiles(m, seq)
    hb = tm // HALO

    def body(x_ref, xb_ref, cw_ref, cb_ref, o_ref, c_ref):
        first = pl.program_id(1) % n_t == 0
        for l0 in range(0, FF_BLK, FFN_LANES):
            lanes = slice(l0, min(l0 + FFN_LANES, FF_BLK))
            before = jnp.where(first, 0.0, xb_ref[:, :, lanes])
            ext = jnp.concatenate([before, x_ref[:, :, lanes]], axis=1)
            cg, cv = _ffn_conv(ext[0], ext[1], cw_ref[:, :, lanes], cb_ref[:, :, lanes])
            o_ref[:, lanes] = _ffn_gate(cg, cv).astype(o_ref.dtype)
            c_ref[0, :, lanes] = cg.astype(c_ref.dtype)
            c_ref[1, :, lanes] = cv.astype(c_ref.dtype)

    return pl.pallas_call(
        body, grid=(N_DEV // 2, n_i), name=name,
        in_specs=[pl.BlockSpec((2, None, tm, FF_BLK), lambda d, i: (0, d, i, 0)),
                  pl.BlockSpec((2, None, HALO, FF_BLK), lambda d, i: (0, d, jnp.maximum(i * hb - 1, 0), 0)),
                  pl.BlockSpec((2, None, None, FFN_CONV, FF_BLK), lambda d, i: (0, d, layer, 0, 0)),
                  pl.BlockSpec((None, 2, None, 1, FF_BLK), lambda d, i: (layer, 0, d, 0, 0))],
        out_specs=[pl.BlockSpec((None, tm, FF_BLK), lambda d, i: (d, i, 0)),
                   pl.BlockSpec((2, None, tm, FF_BLK), lambda d, i: (0, d, i, 0))],
        out_shape=[jax.ShapeDtypeStruct((N_DEV // 2, m, FF_BLK), BF16),
                   jax.ShapeDtypeStruct((2, N_DEV // 2, m, FF_BLK), BF16)],
        compiler_params=_params(("parallel", "parallel")),
    )(hid, hid, cw, cb)


def ffn_mid_bwd(name, hid, conv, cw, dact, layer, *, m, seq):
    tm, n_t, n_i = _ffn_tiles(m, seq)
    hb = tm // HALO
    last_blk = m // HALO - 1

    rc = min(FFN_ROWS, tm)
    lane_chunks = [(l0, min(FFN_LANES, FF_BLK - l0)) for l0 in range(0, FF_BLK, FFN_LANES)]

    def body(x_ref, c_ref, ca_ref, cw_ref, da_ref, daa_ref, dx_ref, dcw_ref, dcb_ref, cext_ref, dext_ref):
        i = pl.program_id(1)
        last = i % n_t == n_t - 1
        cext_ref[:, :tm] = c_ref[...]
        cext_ref[:, tm:] = ca_ref[...]
        dext_ref[:tm] = da_ref[...]
        dext_ref[tm:] = jnp.where(last, jnp.zeros_like(daa_ref[...]), daa_ref[...])

        @pl.when(i == 0)
        def _():
            dcw_ref[...] = jnp.zeros_like(dcw_ref)
            dcb_ref[...] = jnp.zeros_like(dcb_ref)

        for l0, lw in lane_chunks:
            lanes = slice(l0, l0 + lw)

            def chunk(c, sums, lanes=lanes, lw=lw):
                r0 = pl.multiple_of(c * rc, rc)
                ext = pl.ds(r0, rc + HALO)
                cg, cv = cext_ref[0, ext, lanes].astype(F32), cext_ref[1, ext, lanes].astype(F32)
                _, vjp_gate = jax.vjp(_ffn_gate, cg, cv)
                dconv = vjp_gate(dext_ref[ext, lanes].astype(F32))
                out = []
                for half in range(2):
                    x = x_ref[half, pl.ds(r0, rc), lanes]
                    dx = None
                    for k in range(FFN_CONV):
                        s = FFN_CONV - 1 - k
                        dc_s = _shift_raw(dconv[half], s, True, 0.0)[:rc]
                        term = cw_ref[half, k:k + 1, lanes] * dc_s
                        dx = term if dx is None else dx + term
                        out.append(sums[len(out)] + jnp.sum(x * dc_s, axis=0, keepdims=True))
                    out.append(sums[len(out)] + jnp.sum(dconv[half][:rc], axis=0, keepdims=True))
                    dx_ref[half, pl.ds(r0, rc), lanes] = dx.astype(dx_ref.dtype)
                return tuple(out)

            zero = jnp.zeros((1, lw), F32)
            sums = lax.fori_loop(0, tm // rc, chunk, (zero,) * (2 * (FFN_CONV + 1)))
            for half in range(2):
                base = half * (FFN_CONV + 1)
                for k in range(FFN_CONV):
                    dcw_ref[half, k:k + 1, lanes] += sums[base + k]
                dcb_ref[half, :, lanes] += sums[base + FFN_CONV]

    return pl.pallas_call(
        body, grid=(N_DEV // 2, n_i), name=name,
        in_specs=[pl.BlockSpec((2, None, tm, FF_BLK), lambda d, i: (0, d, i, 0)),
                  pl.BlockSpec((2, None, tm, FF_BLK), lambda d, i: (0, d, i, 0)),
                  pl.BlockSpec((2, None, HALO, FF_BLK), lambda d, i: (0, d, jnp.minimum((i + 1) * hb, last_blk), 0)),
                  pl.BlockSpec((2, None, None, FFN_CONV, FF_BLK), lambda d, i: (0, d, layer, 0, 0)),
                  pl.BlockSpec((None, tm, FF_BLK), lambda d, i: (d, i, 0)),
                  pl.BlockSpec((None, HALO, FF_BLK), lambda d, i: (d, jnp.minimum((i + 1) * hb, last_blk), 0))],
        out_specs=[pl.BlockSpec((2, None, tm, FF_BLK), lambda d, i: (0, d, i, 0)),
                   pl.BlockSpec((2, None, FFN_CONV, FF_BLK), lambda d, i: (0, d, 0, 0)),
                   pl.BlockSpec((2, None, 1, FF_BLK), lambda d, i: (0, d, 0, 0))],
        out_shape=[jax.ShapeDtypeStruct((2, N_DEV // 2, m, FF_BLK), BF16),
                   jax.ShapeDtypeStruct((2, N_DEV // 2, FFN_CONV, FF_BLK), F32),
                   jax.ShapeDtypeStruct((2, N_DEV // 2, 1, FF_BLK), F32)],
        scratch_shapes=[pltpu.VMEM((2, tm + HALO, FF_BLK), BF16), pltpu.VMEM((tm + HALO, FF_BLK), BF16)],
        compiler_params=_params(("arbitrary", "arbitrary")),
    )(hid, conv, conv, cw, dact, dact)


ATT_BLK = 512
ATT_BLK_FWD = 1024
N_PAIR = C_HEADS // 2
TERM_W = C_HEADS * LANES


def term_placement():
    import numpy as np
    place = np.zeros((3, LANES, TERM_W), np.float32)
    ones_q = np.zeros((1, TERM_W), np.float32)
    ones_k = np.zeros((1, TERM_W), np.float32)
    for h in range(C_HEADS):
        for j in range(3):
            place[j, h, h * LANES + C_HEAD_DIM + j] = 1.0
            ones_q[0, h * LANES + C_HEAD_DIM + 3 + j] = 1.0
            ones_k[0, h * LANES + C_HEAD_DIM + j] = 1.0
    return (jnp.asarray(place.reshape(3 * LANES, TERM_W), BF16), jnp.asarray(ones_q, F32), jnp.asarray(ones_k, F32))


def fn_fox_terms(c, place, ones_q, ones_k):
    parts = _split3(c)
    placed = sum(_dg(parts[j], place[j * LANES:(j + 1) * LANES], "nn") for j in range(3))
    return (placed + ones_q).astype(BF16), (ones_k - pltpu.roll(placed, 3, 1)).astype(BF16)


def _head_tile(z, terms, e):
    lane = lax.broadcasted_iota(jnp.int32, z.shape, 1)
    base = z if e == 0 else pltpu.roll(z, C_HEAD_DIM, 1)
    return jnp.where(lane < C_HEAD_DIM, base, terms.astype(z.dtype))


def _head_only(z, e):
    lane = lax.broadcasted_iota(jnp.int32, z.shape, 1)
    mine = (lane < C_HEAD_DIM) if e == 0 else (lane >= C_HEAD_DIM)
    return jnp.where(mine, z, jnp.zeros_like(z)).astype(BF16)


def _pair_tile(a0, a1):
    lane = lax.broadcasted_iota(jnp.int32, a0.shape, 1)
    return jnp.where(lane < C_HEAD_DIM, a0, pltpu.roll(a1, C_HEAD_DIM, 1))


def _lane_col(a, k):
    lane = lax.broadcasted_iota(jnp.int32, a.shape, 1)
    return jnp.sum(jnp.where(lane == k, a, 0.0), axis=1, keepdims=True)


def _causal(s):
    key = lax.broadcasted_iota(jnp.int32, s.shape, 0)
    qry = lax.broadcasted_iota(jnp.int32, s.shape, 1)
    return qry >= key


def fox_pair_fwd(name, z, qterm, kterm, *, n_batch, seq):
    m = n_batch * seq
    blk = min(ATT_BLK_FWD, seq)
    nq = seq // blk
    dh = C_HEAD_DIM

    def body(zq_ref, zk_ref, zv_ref, qt_ref, kt_ref, o_ref, lse_ref, ka_ref, vt_ref):
        qi = pl.program_id(2)

        @pl.when(qi == 0)
        def _():
            zk = zk_ref[...]
            for e in range(2):
                ka_ref[e] = _head_tile(zk, kt_ref[:, e * LANES:(e + 1) * LANES], e).astype(BF16)
            for cb in range(nq):
                vt_ref[cb] = zv_ref[cb * blk:(cb + 1) * blk, :].T.astype(BF16)

        zq = zq_ref[...] * dh ** -0.5
        qa = [_head_tile(zq, qt_ref[:, e * LANES:(e + 1) * LANES], e).astype(BF16) for e in range(2)]

        def block(j, carry, diagonal):
            rows = pl.ds(pl.multiple_of(j * blk, blk), blk)
            out = []
            for e in range(2):
                mx, l, acc = carry[e]
                s = _dg(ka_ref[e, rows, :], qa[e], "nt")
                if diagonal:
                    s = jnp.where(_causal(s), s, NEG)
                mx_new = jnp.maximum(mx, jnp.max(s, axis=0, keepdims=True))
                p = jnp.exp(s - mx_new)
                alpha = jnp.exp(mx - mx_new)
                l = alpha * l + jnp.sum(p, axis=0, keepdims=True)
                acc = alpha * acc + _dg(vt_ref[j, e * dh:(e + 1) * dh, :], p, "nn")
                out.append((mx_new, l, acc))
            return tuple(out)

        one = (jnp.full((1, blk), NEG, F32), jnp.zeros((1, blk), F32), jnp.zeros((dh, blk), F32))
        carry = lax.fori_loop(0, qi, lambda j, cr: block(j, cr, False), (one, one))
        res = block(qi, carry, True)
        ot = jnp.concatenate([res[e][2] / res[e][1] for e in range(2)], axis=0)
        o_ref[...] = ot.T.astype(o_ref.dtype)
        for e in range(2):
            lse_ref[e] = res[e][0] + jnp.log(res[e][1])

    col = lambda part: (lambda b, g, i: (b, part * N_PAIR + g))
    return pl.pallas_call(
        body, grid=(n_batch, N_PAIR, nq), name=name,
        in_specs=[pl.BlockSpec((blk, LANES), lambda b, g, i: (b * nq + i, g)),
                  pl.BlockSpec((seq, LANES), col(1)),
                  pl.BlockSpec((seq, LANES), col(2)),
                  pl.BlockSpec((blk, 2 * LANES), lambda b, g, i: (b * nq + i, g)),
                  pl.BlockSpec((seq, 2 * LANES), lambda b, g, i: (b, g))],
        out_specs=[pl.BlockSpec((blk, LANES), lambda b, g, i: (b * nq + i, g)),
                   pl.BlockSpec((None, None, None, 2, 1, blk), lambda b, g, i: (b, g, i, 0, 0, 0))],
        out_shape=[jax.ShapeDtypeStruct((m, D_MODEL), BF16), jax.ShapeDtypeStruct((n_batch, N_PAIR, nq, 2, 1, blk), F32)],
        scratch_shapes=[pltpu.VMEM((2, seq, LANES), BF16), pltpu.VMEM((nq, LANES, blk), BF16)],
        compiler_params=_params(("parallel", "parallel", "arbitrary")),
    )(z, z, z, qterm, kterm)


def fox_pair_bwd(name, z, qterm, kterm, o, do, lse, *, n_batch, seq):
    m = n_batch * seq
    blk = min(ATT_BLK, seq)
    nq = seq // blk
    dh = C_HEAD_DIM

    def body(zq_ref, zk_ref, zv_ref, qt_ref, kt_ref, o_ref, do_ref, lse_ref, dq_ref, dk_ref, dv_ref, dc_ref,
             qa_ref, doh_ref, del_ref, dqt_ref, dk_acc, dv_acc):
        g, j = pl.program_id(1), pl.program_id(2)
        lane = lax.broadcasted_iota(jnp.int32, (blk, LANES), 1)

        @pl.when(jnp.logical_and(g == 0, j == 0))
        def _():
            dc_ref[...] = jnp.zeros_like(dc_ref)

        @pl.when(j == 0)
        def _():
            zq = zq_ref[...] * dh ** -0.5
            dov = do_ref[...]
            for e in range(2):
                qa_ref[e] = _head_tile(zq, qt_ref[:, e * LANES:(e + 1) * LANES], e).astype(BF16)
                doh_ref[e] = _head_only(dov, e)
            for cb in range(nq):
                rows = slice(cb * blk, (cb + 1) * blk)
                prod_t = (do_ref[rows, :].astype(F32) * o_ref[rows, :].astype(F32)).T
                for e in range(2):
                    del_ref[cb, e] = jnp.sum(prod_t[e * dh:(e + 1) * dh], axis=0, keepdims=True)
            dqt_ref[...] = jnp.zeros_like(dqt_ref)

        zk, zv = zk_ref[...], zv_ref[...]
        ka32 = [_head_tile(zk, kt_ref[:, e * LANES:(e + 1) * LANES], e) for e in range(2)]
        ka = [t.astype(BF16) for t in ka32]
        kat = [t.T.astype(BF16) for t in ka32]
        vh = [_head_only(zv, e) for e in range(2)]
        dk_acc[...] = jnp.zeros_like(dk_acc)
        dv_acc[...] = jnp.zeros_like(dv_acc)

        def block(i, diagonal):
            rows = pl.ds(pl.multiple_of(i * blk, blk), blk)
            for e in range(2):
                qv, dov = qa_ref[e, rows, :], doh_ref[e, rows, :]
                p = jnp.exp(_dg(ka[e], qv, "nt") - lse_ref[i, e])
                if diagonal:
                    p = jnp.where(_causal(p), p, 0.0)
                dv_acc[...] += _dg(p, dov, "nn")
                ds = p * (_dg(vh[e], dov, "nt") - del_ref[i, e])
                dk_acc[e] += _dg(ds, qv, "nn")
                dqt_ref[i, e] += _dg(kat[e], ds, "nn")

        block(j, True)

        def rest(i, carry):
            block(i, False)
            return carry

        lax.fori_loop(j + 1, nq, rest, 0)
        dk0, dk1 = dk_acc[0], dk_acc[1]
        dk_ref[...] = _pair_tile(dk0, dk1).astype(dk_ref.dtype)
        dv_ref[...] = dv_acc[...].astype(dv_ref.dtype)
        rows_j = pl.ds(pl.multiple_of(j * blk, blk), blk)
        for e, dke in enumerate((dk0, dk1)):
            dc_ref[rows_j, :] -= jnp.where(lane == 2 * g + e, _lane_col(dke, dh + 3), 0.0)

        @pl.when(j == nq - 1)
        def _():
            for i in range(nq):
                nat = [dqt_ref[i, e].T for e in range(2)]
                rows = slice(i * blk, (i + 1) * blk)
                dq_ref[rows, :] = (_pair_tile(nat[0], nat[1]) * dh ** -0.5).astype(dq_ref.dtype)
                for e in range(2):
                    dc_ref[rows, :] += jnp.where(lane == 2 * g + e, _lane_col(nat[e], dh), 0.0)

    col = lambda part: (lambda b, g, j: (b, part * N_PAIR + g))
    colj = lambda part: (lambda b, g, j: (b * nq + j, part * N_PAIR + g))
    pair = jax.ShapeDtypeStruct((m, D_MODEL), BF16)
    return pl.pallas_call(
        body, grid=(n_batch, N_PAIR, nq), name=name,
        in_specs=[pl.BlockSpec((seq, LANES), col(0)),
                  pl.BlockSpec((blk, LANES), colj(1)),
                  pl.BlockSpec((blk, LANES), colj(2)),
                  pl.BlockSpec((seq, 2 * LANES), lambda b, g, j: (b, g)),
                  pl.BlockSpec((blk, 2 * LANES), lambda b, g, j: (b * nq + j, g)),
                  pl.BlockSpec((seq, LANES), col(0)),
                  pl.BlockSpec((seq, LANES), col(0)),
                  pl.BlockSpec((None, None, nq, 2, 1, blk), lambda b, g, j: (b, g, 0, 0, 0, 0))],
        out_specs=[pl.BlockSpec((seq, LANES), col(0)),
                   pl.BlockSpec((blk, LANES), colj(0)),
                   pl.BlockSpec((blk, LANES), colj(0)),
                   pl.BlockSpec((seq, LANES), lambda b, g, j: (b, 0))],
        out_shape=[pair, pair, pair, jax.ShapeDtypeStruct((m, LANES), F32)],
        scratch_shapes=[pltpu.VMEM((2, seq, LANES), BF16), pltpu.VMEM((2, seq, LANES), BF16),
                        pltpu.VMEM((nq, 2, 1, blk), F32), pltpu.VMEM((nq, 2, LANES, blk), F32),
                        pltpu.VMEM((2, blk, LANES), F32), pltpu.VMEM((blk, LANES), F32)],
        compiler_params=_params(("arbitrary", "arbitrary", "arbitrary")),
    )(z, z, z, qterm, kterm, o, do, lse)


def _split3(c):
    c1 = c.astype(BF16)
    r1 = c - c1.astype(F32)
    c2 = r1.astype(BF16)
    c3 = (r1 - c2.astype(F32)).astype(BF16)
    return c1, c2, c3


def _mesh_pos():
    return lax.axis_index("x"), lax.axis_index("y"), lax.axis_index("c")


def _flip(v, bit):
    return 1 - v if bit else v


def all_gather(name, blocks):
    n = len(blocks)

    def body(*refs):
        x_refs, out_refs = refs[:n], refs[n:2 * n]
        send_sems, recv_sems, local_sems = refs[2 * n:]
        x, y, c = _mesh_pos()
        me, sibling = (x, y, c), (x, y, 1 - c)
        chips = [(1 - x, y), (x, 1 - y), (1 - x, 1 - y)]

        def slot(a, px, py, pc):
            return out_refs[a].at[4 * px + 2 * py + pc]

        def copy(a, k, blk, to, src=None):
            return pltpu.make_async_remote_copy(
                src_ref=slot(a, *blk) if src is None else src, dst_ref=slot(a, *blk),
                send_sem=send_sems.at[a, k], recv_sem=recv_sems.at[a, k], device_id=to, device_id_type=MESH)

        mine = [pltpu.make_async_copy(x_refs[a], slot(a, *me), local_sems.at[a]) for a in range(n)]
        for cp in mine:
            cp.start()
        sends = []
        for a in range(n):
            sends.append(copy(a, 0, me, sibling, src=x_refs[a]))
            sends += [copy(a, 1 + j, me, (*chip, c), src=x_refs[a]) for j, chip in enumerate(chips)]
        for cp in sends:
            cp.start()
        for j, chip in enumerate(chips):
            for a in range(n):
                copy(a, 1 + j, (*chip, c), me).wait_recv()
                passed = copy(a, 4 + j, (*chip, c), sibling)
                passed.start()
                sends.append(passed)
        for a in range(n):
            copy(a, 0, sibling, me).wait_recv()
            for j, chip in enumerate(chips):
                copy(a, 4 + j, (*chip, 1 - c), me).wait_recv()
        for cp in sends:
            cp.wait_send()
        for cp in mine:
            cp.wait()

    hbm = pl.BlockSpec(memory_space=pl.ANY)
    return pl.pallas_call(
        body, name=name, out_shape=[jax.ShapeDtypeStruct((N_DEV,) + b.shape, b.dtype) for b in blocks],
        in_specs=[hbm] * n, out_specs=[hbm] * n,
        scratch_shapes=[pltpu.SemaphoreType.DMA((n, 7)), pltpu.SemaphoreType.DMA((n, 7)), pltpu.SemaphoreType.DMA((n,))],
    )(*blocks)


def _peers(x, y, c):
    return [(_flip(x, k & 4), _flip(y, k & 2), _flip(c, k & 1)) for k in range(1, N_DEV)]


def gather_start(name, blocks, lands):
    n = len(blocks)

    def body(*refs):
        x_refs, land_refs = refs[:n], refs[n:2 * n]
        send_sems, recv_sems = refs[2 * n], refs[2 * n + 1]
        token = refs[-1]
        x, y, c = _mesh_pos()
        me = 4 * x + 2 * y + c
        for k, peer in enumerate(_peers(x, y, c)):
            for a in range(n):
                pltpu.make_async_remote_copy(
                    src_ref=x_refs[a], dst_ref=land_refs[a].at[me], send_sem=send_sems.at[7 * a + k], recv_sem=recv_sems.at[7 * a + k],
                    device_id=peer, device_id_type=MESH).start()
        token[...] = jnp.zeros_like(token)

    hbm = pl.BlockSpec(memory_space=pltpu.HBM)
    sem = pl.BlockSpec(memory_space=pltpu.SEMAPHORE)
    out_shape = ([pltpu.SemaphoreType.DMA((7 * n,)), pltpu.SemaphoreType.DMA((7 * n,))]
                 + [pltpu.HBM(b.shape, b.dtype) for b in blocks] + [pltpu.HBM(l.shape, l.dtype) for l in lands]
                 + [jax.ShapeDtypeStruct((8, LANES), F32)])
    res = pl.pallas_call(
        body, name=name, out_shape=out_shape, in_specs=[hbm] * (2 * n),
        out_specs=[sem, sem] + [hbm] * (2 * n) + [pl.BlockSpec(memory_space=pltpu.VMEM)],
        input_output_aliases={a: 2 + a for a in range(2 * n)},
        compiler_params=pltpu.CompilerParams(has_side_effects=pltpu.SideEffectType.DATAFLOW_SIDE_EFFECTING),
    )(*[pltpu.with_memory_space_constraint(b, pltpu.HBM) for b in blocks],
      *[pltpu.with_memory_space_constraint(l, pltpu.HBM) for l in lands])
    return res[0], res[1], res[2:2 + n], res[2 + n:2 + 2 * n], res[-1]


def gather_wait(name, send_sems, recv_sems, blocks, lands, after):
    n = len(blocks)

    def body(*refs):
        x_refs, land_refs = refs[:n], refs[n:2 * n]
        s_sems, r_sems = refs[2 * n], refs[2 * n + 1]
        x, y, c = _mesh_pos()
        me = 4 * x + 2 * y + c
        for k, peer in enumerate(_peers(x, y, c)):
            for a in range(n):
                cp = pltpu.make_async_remote_copy(
                    src_ref=x_refs[a], dst_ref=land_refs[a].at[me], send_sem=s_sems.at[7 * a + k], recv_sem=r_sems.at[7 * a + k],
                    device_id=peer, device_id_type=MESH)
                cp.wait_send()
                cp.wait_recv()

    hbm = pl.BlockSpec(memory_space=pltpu.HBM)
    sem = pl.BlockSpec(memory_space=pltpu.SEMAPHORE)
    res = pl.pallas_call(
        body, name=name,
        out_shape=[pltpu.HBM(b.shape, b.dtype) for b in blocks] + [pltpu.HBM(l.shape, l.dtype) for l in lands],
        in_specs=[hbm] * (2 * n) + [sem, sem, pl.BlockSpec(memory_space=pl.ANY)], out_specs=[hbm] * (2 * n),
        input_output_aliases={a: a for a in range(2 * n)},
        compiler_params=pltpu.CompilerParams(has_side_effects=pltpu.SideEffectType.DATAFLOW_SIDE_EFFECTING),
    )(*blocks, *lands, send_sems, recv_sems, after)
    return res[n:]


def _split_exchange(name, sends, lands, sems, after):
    n = len(sends)
    starting = sems is None

    def body(*refs):
        s_refs, l_refs = refs[:n], refs[n:2 * n]
        send_sems, recv_sems = refs[2 * n], refs[2 * n + 1]
        x, y, c = _mesh_pos()
        me = 4 * x + 2 * y + c
        for k, (px, py, pc) in enumerate(_peers(x, y, c)):
            for a in range(n):
                cp = pltpu.make_async_remote_copy(
                    src_ref=s_refs[a].at[4 * px + 2 * py + pc], dst_ref=l_refs[a].at[me],
                    send_sem=send_sems.at[7 * a + k], recv_sem=recv_sems.at[7 * a + k],
                    device_id=(px, py, pc), device_id_type=MESH)
                if starting:
                    cp.start()
                else:
                    cp.wait_send()
                    cp.wait_recv()
        if starting:
            refs[-1][...] = jnp.zeros_like(refs[-1])

    hbm = pl.BlockSpec(memory_space=pltpu.HBM)
    sem = pl.BlockSpec(memory_space=pltpu.SEMAPHORE)
    thru = [pltpu.HBM(t.shape, t.dtype) for t in list(sends) + list(lands)]
    effect = pltpu.CompilerParams(has_side_effects=pltpu.SideEffectType.DATAFLOW_SIDE_EFFECTING)
    if starting:
        res = pl.pallas_call(
            body, name=name, in_specs=[hbm] * (2 * n),
            out_shape=[pltpu.SemaphoreType.DMA((7 * n,)), pltpu.SemaphoreType.DMA((7 * n,))] + thru + [jax.ShapeDtypeStruct((8, LANES), F32)],
            out_specs=[sem, sem] + [hbm] * (2 * n) + [pl.BlockSpec(memory_space=pltpu.VMEM)],
            input_output_aliases={a: 2 + a for a in range(2 * n)}, compiler_params=effect,
        )(*[pltpu.with_memory_space_constraint(t, pltpu.HBM) for t in list(sends) + list(lands)])
        return res[0], res[1], res[2:2 + n], res[2 + n:2 + 2 * n], res[-1]
    res = pl.pallas_call(
        body, name=name, out_shape=thru, in_specs=[hbm] * (2 * n) + [sem, sem, pl.BlockSpec(memory_space=pl.ANY)],
        out_specs=[hbm] * (2 * n), input_output_aliases={a: a for a in range(2 * n)}, compiler_params=effect,
    )(*sends, *lands, sems[0], sems[1], after)
    return res[n:]


def unwritten(name, like):
    def body(*refs):
        pass

    hbm = pl.BlockSpec(memory_space=pl.ANY)
    return pl.pallas_call(body, name=name, out_shape=[jax.ShapeDtypeStruct(t.shape, t.dtype) for t in like],
                          out_specs=[hbm] * len(like))()


def own_slot_only(send, land, me):
    mine = lax.dynamic_index_in_dim(send, me, 0, keepdims=False)
    return lax.dynamic_update_index_in_dim(land, mine, me, 0)


def all_to_all(name, sends):
    n = len(sends)

    def body(*refs):
        s_refs, r_refs = refs[:n], refs[n:2 * n]
        send_sems, recv_sems, local_sems = refs[2 * n:]
        x, y, c = _mesh_pos()
        me = 4 * x + 2 * y + c
        mine = [pltpu.make_async_copy(s_refs[a].at[me], r_refs[a].at[me], local_sems.at[a]) for a in range(n)]
        for cp in mine:
            cp.start()
        copies = []
        for k in range(1, N_DEV):
            px, py, pc = _flip(x, k & 4), _flip(y, k & 2), _flip(c, k & 1)
            for a in range(n):
                copies.append(pltpu.make_async_remote_copy(
                    src_ref=s_refs[a].at[4 * px + 2 * py + pc], dst_ref=r_refs[a].at[me],
                    send_sem=send_sems.at[a, k - 1], recv_sem=recv_sems.at[a, k - 1],
                    device_id=(px, py, pc), device_id_type=MESH))
        for cp in copies:
            cp.start()
        for cp in copies:
            cp.wait_recv()
        for cp in copies:
            cp.wait_send()
        for cp in mine:
            cp.wait()

    hbm = pl.BlockSpec(memory_space=pl.ANY)
    return pl.pallas_call(
        body, name=name, out_shape=[jax.ShapeDtypeStruct(s.shape, s.dtype) for s in sends],
        in_specs=[hbm] * n, out_specs=[hbm] * n,
        scratch_shapes=[pltpu.SemaphoreType.DMA((n, 7)), pltpu.SemaphoreType.DMA((n, 7)), pltpu.SemaphoreType.DMA((n,))],
    )(*sends)


def _row_tile(r, cap, step):
    return next((t for t in range(cap, step - 1, -step) if r % t == 0), r)


def _sum_parts(p, n):
    t = [p[k].astype(F32) for k in range(n)]
    while len(t) > 1:
        t = [t[k] + t[k + 1] for k in range(0, len(t), 2)]
    return t[0]


def _adam(g, w, m, v):
    m = ADAM_B1 * m + (1.0 - ADAM_B1) * g
    v = ADAM_B2 * v + (1.0 - ADAM_B2) * (g * g)
    m_hat = m / (1.0 - ADAM_B1 ** ADAM_STEP)
    v_hat = v / (1.0 - ADAM_B2 ** ADAM_STEP)
    return -ADAM_LR * (m_hat / (jnp.sqrt(v_hat) + ADAM_EPS) + ADAM_WD * w), m, v


def adam_tiled(name, partials, w, m_, v_, layer=0, prev=None):
    _, r, c = w.shape
    n_part = partials.shape[0]
    tr = _row_tile(r, 256, 16)

    def body(*refs):
        p_ref, w_ref, m_ref, v_ref = refs[:4]
        g_ref, d_ref, nm_ref, nv_ref = refs[-4:]
        g = _sum_parts(p_ref, n_part)
        g_ref[...] = g
        d_ref[...], nm_ref[...], nv_ref[...] = _adam(g, w_ref[...], m_ref[...], v_ref[...])

    spec = pl.BlockSpec((None, tr, c), lambda i: (layer, i, 0))
    in_specs = [pl.BlockSpec((n_part, None, tr, c), lambda i: (0, 0, i, 0)), spec, spec, spec]
    args = [partials, w, m_, v_]
    aliases = {}
    if prev is not None:
        in_specs += [pl.BlockSpec(memory_space=pl.ANY)] * 4
        args += list(prev)
        aliases = {4 + k: k for k in range(4)}
    return pl.pallas_call(
        body, grid=(r // tr,), name=name, in_specs=in_specs,
        out_specs=[spec] * 4, out_shape=[jax.ShapeDtypeStruct(w.shape, F32)] * 4,
        input_output_aliases=aliases, compiler_params=_params(("parallel",)),
    )(*args)


def adam_small(name, items, extra):
    n, ne = len(items), len(extra)

    def body(*refs):
        ins, outs = refs[:4 * n + ne], refs[4 * n + ne:]
        for a in range(n):
            p_ref, w_ref, m_ref, v_ref = ins[4 * a:4 * a + 4]
            g = _sum_parts(p_ref, N_DEV)
            outs[4 * a][...] = g
            outs[4 * a + 1][...], outs[4 * a + 2][...], outs[4 * a + 3][...] = _adam(g, w_ref[...], m_ref[...], v_ref[...])
        for e in range(ne):
            outs[4 * n + e][...] = _sum_parts(ins[4 * n + e], N_DEV)

    args, out_shape = [], []
    for p, w, m_, v_ in items:
        args += [p, w, m_, v_]
        out_shape += [jax.ShapeDtypeStruct(w.shape, F32)] * 4
    for e in extra:
        args.append(e)
        out_shape.append(jax.ShapeDtypeStruct(e.shape[1:], F32))
    vmem = pl.BlockSpec(memory_space=pltpu.VMEM)
    res = pl.pallas_call(body, name=name, in_specs=[vmem] * len(args), out_specs=[vmem] * len(out_shape), out_shape=out_shape)(*args)
    return [res[4 * a:4 * a + 4] for a in range(n)], res[4 * n:]


def _cols_from_gather(g):
    g = jnp.moveaxis(g, 0, -2)
    return g.reshape(g.shape[:-2] + (g.shape[-2] * g.shape[-1],))


def _cols_to_blocks(w):
    w = w.reshape(w.shape[:-1] + (N_DEV, w.shape[-1] // N_DEV))
    return jnp.moveaxis(w, -2, 0)


def _block_diag(w):
    pairs = w.reshape(B_BLOCKS // 2, 2, B_BLOCK_DIM, 1, B_BLOCK_DIM)
    same = jnp.eye(2, dtype=bool).reshape(1, 2, 1, 2, 1)
    return jnp.where(same, pairs, 0.0).reshape(B_BLOCKS // 2 * LANES, LANES)


def _block_diag_grad(d):
    parts = d.reshape(B_BLOCKS // 2, 2, B_BLOCK_DIM, 2, B_BLOCK_DIM)
    same = jnp.eye(2, dtype=bool).reshape(1, 2, 1, 2, 1)
    return jnp.sum(jnp.where(same, parts, 0.0), axis=3).reshape(B_BLOCKS, B_BLOCK_DIM, B_BLOCK_DIM)


NAMES = ("norm_gains", "even_w_in", "hgrn_lb_logits", "hgrn_norm", "rg_conv_w", "rg_conv_b", "rg_wa", "rg_ba", "rg_wx", "rg_bx",
         "rg_lambda", "even_w_out", "odd_w_in", "fox_f_bias", "odd_w_out", "ffn_w_up", "ffn_conv_w", "ffn_conv_b", "ffn_w_down")
SMALL_SHARDED = ("norm_gains", "rg_conv_w", "ffn_conv_w")
REPLICATED = ("hgrn_lb_logits", "hgrn_norm", "rg_conv_b", "rg_wa", "rg_ba", "rg_wx", "rg_bx", "rg_lambda", "fox_f_bias", "ffn_conv_b")


def _ffn_forward(tag, layer, h, w_up_g, cw5, cb5, w_down_g, m, seq):
    tm = _div_tile(m, MM_ROWS)
    nm = m // tm
    hid = mm(f"{tag}_up", "nn",
             Blk(h, (tm, D_MODEL), lambda i, j, k: (i, 0)),
             Blk(w_up_g, (None, None, D_MODEL, FF_BLK), lambda i, j, k: (j, 0, 0, 0)),
             Blk((N_DEV, m, FF_BLK), (None, tm, FF_BLK), lambda i, j, k: (j, i, 0)), F32, (nm, N_DEV, 1))
    hid = hid.reshape(2, N_DEV // 2, m, FF_BLK)
    act, conv = ffn_mid_fwd(f"{tag}_mid", hid, cw5, cb5, layer, m=m, seq=seq)
    f = mm(f"{tag}_down", "nn",
           Blk(act, (None, tm, FF_BLK), lambda i, j, k: (k, i, 0)),
           Blk(w_down_g, (2, None, FF_BLK // 2, D_MODEL), lambda i, j, k: (k, 0, 0, 0)),
           Blk((m, D_MODEL), (tm, D_MODEL), lambda i, j, k: (i, 0)), F32, (nm, 1, N_DEV // 2))
    return (hid, conv), act, f


def _ffn_backward(tag, layer, df, h, hid, act, w_up_g, cw5, cb5, w_down_g, m, seq):
    tm = _div_tile(m, MM_ROWS)
    nm = m // tm
    dact = mm(f"{tag}_dact", "nt",
              Blk(df, (tm, D_MODEL), lambda i, j, k: (i, 0)),
              Blk(w_down_g, (2, None, FF_BLK // 2, D_MODEL), lambda i, j, k: (j, 0, 0, 0)),
              Blk((N_DEV // 2, m, FF_BLK), (None, tm, FF_BLK), lambda i, j, k: (j, i, 0)), BF16, (nm, N_DEV // 2, 1))
    d_wdown = mm(f"{tag}_dwdown", "tn",
                 Blk(act, (None, tm, FF_BLK), lambda i, j, k: (i, k, 0)),
                 Blk(df, (tm, D_MODEL), lambda i, j, k: (k, 0)),
                 Blk(w_down_g.shape, (2, None, FF_BLK // 2, D_MODEL), lambda i, j, k: (i, 0, 0, 0)), BF16,
                 (N_DEV // 2, 1, nm))
    dhid, d_cw, d_cb = ffn_mid_bwd(f"{tag}_dmid", hid[0], hid[1], cw5, dact, layer, m=m, seq=seq)
    dhid = dhid.reshape(N_DEV, m, FF_BLK)
    dh = mm(f"{tag}_dh", "nt",
            Blk(dhid, (None, tm, FF_BLK), lambda i, j, k: (k, i, 0)),
            Blk(w_up_g, (None, None, D_MODEL, FF_BLK), lambda i, j, k: (k, 0, 0, 0)),
            Blk((m, D_MODEL), (tm, D_MODEL), lambda i, j, k: (i, 0)), BF16, (nm, 1, N_DEV))
    d_wup = mm(f"{tag}_dwup", "tn",
               Blk(dhid, (None, tm, FF_BLK), lambda i, j, k: (i, k, 0)),
               Blk(h, (tm, D_MODEL), lambda i, j, k: (k, 0)),
               Blk((N_DEV, 1, FF_BLK, D_MODEL), (None, None, FF_BLK, D_MODEL), lambda i, j, k: (i, 0, 0, 0)), BF16,
               (N_DEV, 1, nm))
    return dh, d_wup, d_cw, d_cb, d_wdown


def kernel(x, norm_gains, even_w_in, hgrn_lb_logits, hgrn_norm, rg_conv_w, rg_conv_b, rg_wa, rg_ba, rg_wx, rg_bx, rg_lambda, even_w_out, odd_w_in, fox_f_bias, odd_w_out, ffn_w_up, ffn_conv_w, ffn_conv_b, ffn_w_down, loss_target, m_norm_gains, m_even_w_in, m_hgrn_lb_logits, m_hgrn_norm, m_rg_conv_w, m_rg_conv_b, m_rg_wa, m_rg_ba, m_rg_wx, m_rg_bx, m_rg_lambda, m_even_w_out, m_odd_w_in, m_fox_f_bias, m_odd_w_out, m_ffn_w_up, m_ffn_conv_w, m_ffn_conv_b, m_ffn_w_down, v_norm_gains, v_even_w_in, v_hgrn_lb_logits, v_hgrn_norm, v_rg_conv_w, v_rg_conv_b, v_rg_wa, v_rg_ba, v_rg_wx, v_rg_bx, v_rg_lambda, v_even_w_out, v_odd_w_in, v_fox_f_bias, v_odd_w_out, v_ffn_w_up, v_ffn_conv_w, v_ffn_conv_b, v_ffn_w_down):
    local = dict(locals())
    w = {n: local[n] for n in NAMES}
    mom = {n: local["m_" + n] for n in NAMES}
    var = {n: local["v_" + n] for n in NAMES}
    n_batch, seq, _ = x.shape
    m = n_batch * seq
    tm = _div_tile(m, 512)
    tmm = _div_tile(m, MM_ROWS)
    nm = m // tmm

    gathered = all_gather("gather_weights", [w["even_w_in"].astype(BF16)] + [w[n] for n in SMALL_SHARDED])
    g = dict(zip(("even_w_in",) + SMALL_SHARDED, gathered))
    w_in_e = g["even_w_in"]
    gains = _cols_from_gather(g["norm_gains"])
    me = 4 * lax.axis_index("x") + 2 * lax.axis_index("y") + lax.axis_index("c")
    def own_block_only(name, blocks):
        lands = unwritten(name, [jax.ShapeDtypeStruct((N_DEV,) + t.shape, t.dtype) for t in blocks])
        return [lax.dynamic_update_index_in_dim(ld, t, me, 0) for ld, t in zip(lands, blocks)]

    def own_slots_only(name, sends):
        return [own_slot_only(t, ld, me) for t, ld in zip(sends, unwritten(name, sends))]

    behind = (g["norm_gains"][0, 0, 0, 0] * 0.0).astype(BF16)
    out0 = [w["even_w_out"].astype(BF16) + behind]
    out0_sent = gather_start("gather_out0_start", out0, own_block_only("land_out0", out0))
    behind = (out0_sent[4][0, 0] * 0.0).astype(BF16)
    ffn0 = [w["ffn_w_up"][0:1].astype(BF16) + behind, w["ffn_w_down"][0:1].astype(BF16) + behind]
    ffn0_sent = gather_start("gather_ffn0_start", ffn0, own_block_only("land_ffn0", ffn0))
    behind = (ffn0_sent[4][0, 0] * 0.0).astype(BF16)
    mix1w = [jnp.swapaxes(w["odd_w_in"], 1, 2).astype(BF16) + behind, w["odd_w_out"].astype(BF16) + behind]
    mix1_sent = gather_start("gather_mix1_start", mix1w, own_block_only("land_mix1", mix1w))
    behind = (mix1_sent[4][0, 0] * 0.0).astype(BF16)
    ffn1 = [w["ffn_w_up"][1:2].astype(BF16) + behind, w["ffn_w_down"][1:2].astype(BF16) + behind]
    ffn1_sent = gather_start("gather_ffn1_start", ffn1, own_block_only("land_ffn1", ffn1))
    started = ffn1_sent[4]
    rg_cw = _cols_from_gather(g["rg_conv_w"])[0]
    n_layer = ffn_conv_w.shape[0]
    cw5 = g["ffn_conv_w"].reshape(2, N_DEV // 2, n_layer, FFN_CONV, FF_BLK)
    cb5 = ffn_conv_b.reshape(n_layer, 2, N_DEV // 2, 1, FF_BLK)
    gain = lambda l, k: gains[l, k:k + 1, :]
    wa_bd, wx_bd = _block_diag(rg_wa[0]), _block_diag(rg_wx[0])
    fbias = jnp.pad(fox_f_bias, ((0, 0), (0, LANES - C_HEADS)))

    x0 = x.reshape(m, D_MODEL)
    tgt = loss_target.reshape(m, D_MODEL)

    (h0,) = tile_fwd("l0_prenorm", fn_prenorm_after, m=m, tm=tm, nj=1, rows=[Row(x0)], pars=[Par(gain(0, 0)), Par(started)],
                     outs=[Out(D_MODEL, BF16)])
    z0 = mm("l0_in", "nn",
            Blk(h0, (tmm, D_MODEL), lambda i, j, k: (i, 0)),
            Blk(w_in_e, (2, None, D_MODEL, 384), lambda i, j, k: (j, 0, 0, 0)),
            Blk((m, 3072), (tmm, 768), lambda i, j, k: (i, j)), F32, (nm, N_DEV // 2, 1), b_join=True)
    oa, sprev = hgrn_fwd("l0_hgrn", z0, hgrn_lb_logits, hgrn_norm, n_batch=n_batch, seq=seq)
    rg_rows = lambda: [Row(z0, LANES, 16), Row(z0, LANES, 20)]
    rg_pars = lambda: [Par(rg_cw, "col", LANES), Par(rg_conv_b, "col", LANES), Par(wa_bd, "row", LANES), Par(rg_ba, "col", LANES),
                       Par(wx_bd, "row", LANES), Par(rg_bx, "col", LANES), Par(rg_lambda, "col", LANES)]
    (ob,) = tile_fwd("l0_rglru", fn_rglru, m=m, tm=seq, nj=B_WIDTH // LANES, rows=rg_rows(), pars=rg_pars(),
                     outs=[Out(B_WIDTH, BF16, LANES)])
    mixcat0 = jnp.concatenate([oa, ob], axis=-1)
    (g_out_e,) = gather_wait("gather_out0_wait", out0_sent[0], out0_sent[1], out0_sent[2], out0_sent[3], mixcat0)
    w_out_e = g_out_e.reshape(D_MODEL, D_MODEL)
    mix0 = mm2d("l0_out", "nn", mixcat0, w_out_e)
    x1, h1 = tile_fwd("l0_postnorm", fn_addnorm2, m=m, tm=tm, nj=1, rows=[Row(x0), Row(mix0)], pars=[Par(gain(0, 1)), Par(gain(0, 2))],
                      outs=[Out(D_MODEL, F32), Out(D_MODEL, BF16)])
    w_up_g0, w_down_g0 = gather_wait("gather_ffn0_wait", ffn0_sent[0], ffn0_sent[1], ffn0_sent[2], ffn0_sent[3], h1)
    hid0, act0, f0 = _ffn_forward("l0_ffn", 0, h1, w_up_g0, cw5, cb5, w_down_g0, m, seq)
    x2, h2 = tile_fwd("l0_ffnnorm", fn_addnorm2, m=m, tm=tm, nj=1, rows=[Row(x1), Row(f0)], pars=[Par(gain(0, 3)), Par(gain(1, 0))],
                      outs=[Out(D_MODEL, F32), Out(D_MODEL, BF16)])

    g_in_o, g_out_o = gather_wait("gather_mix1_wait", mix1_sent[0], mix1_sent[1], mix1_sent[2], mix1_sent[3], h2)
    w_in_o_t = jnp.pad(g_in_o.reshape(3088, D_MODEL), ((0, 3200 - 3088), (0, 0)))
    w_out_o = g_out_o.reshape(D_MODEL, D_MODEL)
    z1 = mm2d("l1_in", "nt", h2, w_in_o_t)
    (cgate,) = tile_fwd("l1_gate", fn_fox_gate, m=m, tm=seq, nj=1, rows=[Row(z1, LANES, 3072 // LANES)], pars=[Par(fbias)],
                        outs=[Out(LANES, F32)])
    place, ones_q, ones_k = term_placement()
    qterm, kterm = tile_fwd("l1_terms", fn_fox_terms, m=m, tm=tm, nj=1, rows=[Row(cgate)],
                            pars=[Par(place), Par(ones_q), Par(ones_k)], outs=[Out(TERM_W, BF16), Out(TERM_W, BF16)])
    oc, lse = fox_pair_fwd("l1_attn", z1, qterm, kterm, n_batch=n_batch, seq=seq)
    blk_b = min(ATT_BLK, seq)
    lse = lse.reshape(n_batch, N_PAIR, -1, 2, lse.shape[-1] // blk_b, blk_b).swapaxes(3, 4).reshape(n_batch, N_PAIR, seq // blk_b, 2, 1, blk_b)
    mix1 = mm2d("l1_out", "nn", oc, w_out_o)
    x3, h3 = tile_fwd("l1_postnorm", fn_addnorm2, m=m, tm=tm, nj=1, rows=[Row(x2), Row(mix1)], pars=[Par(gain(1, 1)), Par(gain(1, 2))],
                      outs=[Out(D_MODEL, F32), Out(D_MODEL, BF16)])
    w_up_g1, w_down_g1 = gather_wait("gather_ffn1_wait", ffn1_sent[0], ffn1_sent[1], ffn1_sent[2], ffn1_sent[3], h3)
    hid1, act1, f1 = _ffn_forward("l1_ffn", 1, h3, w_up_g1, cw5, cb5, w_down_g1, m, seq)
    dy, df1, loss_part, d_g13 = loss_head("loss", x3, f1, tgt, gain(1, 3), m=m, tm=tm)
    dh3, d_wup1, d_cw1, d_cb1, d_wdown1 = _ffn_backward("l1_ffn", 1, df1, h3, hid1, act1, w_up_g1, cw5, cb5, w_down_g1, m, seq)
    dx2, dmix1, d_g11, d_g12 = tile_bwd("l1_dpostnorm", fn_addnorm2, m=m, tm=tm, nj=1, rows=[Row(x2), Row(mix1)],
                                        pars=[Par(gain(1, 1)), Par(gain(1, 2))], cts=[Row(dy), Row(dh3)],
                                        drows=[Out(D_MODEL, F32), Out(D_MODEL, BF16)])
    doc = mm2d("l1_doc", "nt", dmix1, w_out_o, BF16)
    d_wout_o = mm2d("l1_dwout", "tn", oc, dmix1)
    dq, dk, dv, dc = fox_pair_bwd("l1_dattn", z1, qterm, kterm, oc, doc, lse, n_batch=n_batch, seq=seq)
    dzf, d_fbias = tile_bwd("l1_dgate", fn_fox_gate, m=m, tm=seq, nj=1, rows=[Row(z1, LANES, 3072 // LANES)], pars=[Par(fbias)],
                            cts=[Row(dc)], drows=[Out(LANES, BF16)])
    dz1 = jnp.concatenate([dq, dk, dv, dzf], axis=-1)
    dh2 = mm2d("l1_dh", "nn", dz1, w_in_o_t, BF16)
    d_win_o_t = mm2d("l1_dwin", "tn", dz1, h2, BF16)

    send1 = [d_win_o_t[:3088].reshape(N_DEV, 1, 3088 // N_DEV, D_MODEL),
             d_wout_o.reshape(N_DEV, 1, D_MODEL // N_DEV, D_MODEL).astype(BF16), d_wup1, d_wdown1]
    sent1 = _split_exchange("exchange_l1_start", send1, own_slots_only("land_l1", send1), None, None)

    dx1, df0, d_g03, d_g10 = tile_bwd("l0_dffnnorm", fn_addnorm2_after, m=m, tm=tm, nj=1, rows=[Row(x1), Row(f0)],
                                      pars=[Par(gain(0, 3)), Par(gain(1, 0)), Par(sent1[4])], cts=[Row(dx2), Row(dh2)],
                                      drows=[Out(D_MODEL, F32), Out(D_MODEL, BF16)])[:4]
    dh1, d_wup0, d_cw0, d_cb0, d_wdown0 = _ffn_backward("l0_ffn", 0, df0, h1, hid0, act0, w_up_g0, cw5, cb5, w_down_g0, m, seq)
    send0 = [d_wup0, d_wdown0]
    sent0 = _split_exchange("exchange_ffn0_start", send0, own_slots_only("land_dffn0", send0), None, None)
    dx0a, dmix0, d_g01, d_g02 = tile_bwd("l0_dpostnorm", fn_addnorm2_after, m=m, tm=tm, nj=1, rows=[Row(x0), Row(mix0)],
                                         pars=[Par(gain(0, 1)), Par(gain(0, 2)), Par(sent0[4])], cts=[Row(dx1), Row(dh1)],
                                         drows=[Out(D_MODEL, F32), Out(D_MODEL, BF16)])[:4]
    dmixcat0 = mm2d("l0_dmixcat", "nt", dmix0, w_out_e, BF16)
    d_wout_e = mm2d("l0_dwout", "tn", mixcat0, dmix0)
    dzq, dzf0, dzv, dzg, d_lb, d_hnorm = hgrn_bwd("l0_dhgrn", z0, sprev, hgrn_lb_logits, hgrn_norm, dmixcat0, n_batch=n_batch, seq=seq)
    dzx, dzy, d_rcw, d_rcb, d_wa, d_ba, d_wx, d_bx, d_lam = tile_bwd(
        "l0_drglru", fn_rglru, m=m, tm=seq, nj=B_WIDTH // LANES, rows=rg_rows(), pars=rg_pars(),
        cts=[Row(dmixcat0, LANES, A_WIDTH // LANES)], drows=[Out(B_WIDTH, BF16, LANES), Out(B_WIDTH, BF16, LANES)])
    dz0 = jnp.concatenate([dzq, dzf0, dzv, dzg, dzx, dzy], axis=-1)
    d_win_e = mm("l0_dwin", "tn",
                 Blk(h0, (tmm, D_MODEL), lambda i, j, k: (k, 0)),
                 Blk(dz0, (tmm, 768), lambda i, j, k: (k, j)),
                 Blk(w_in_e.shape, (2, None, D_MODEL, 384), lambda i, j, k: (j, 0, 0, 0)), BF16, (1, N_DEV // 2, nm), o_split=True)
    send_e = [d_win_e, d_wout_e.reshape(N_DEV, 1, D_MODEL // N_DEV, D_MODEL).astype(BF16)]
    sent_e = _split_exchange("exchange_even_start", send_e, own_slots_only("land_even", send_e), None, None)
    d_ffn_cb = jnp.stack([d_cb0, d_cb1]).reshape(n_layer, 2 * D_FF)
    rep = {"hgrn_lb_logits": d_lb, "hgrn_norm": d_hnorm, "rg_conv_b": d_rcb, "rg_wa": _block_diag_grad(d_wa)[None], "rg_ba": d_ba,
           "rg_wx": _block_diag_grad(d_wx)[None], "rg_bx": d_bx, "rg_lambda": d_lam, "fox_f_bias": d_fbias[:, :C_HEADS],
           "ffn_conv_b": d_ffn_cb}
    rep_blocks = [rep[n] for n in REPLICATED] + [loss_part]
    rep_sent = gather_start("gather_partials_start", rep_blocks, own_block_only("land_partials", rep_blocks))
    dh0 = mm("l0_dh", "nt",
             Blk(dz0, (tmm, 768), lambda i, j, k: (i, k)),
             Blk(w_in_e, (2, None, D_MODEL, 384), lambda i, j, k: (k, 0, 0, 0)),
             Blk((m, D_MODEL), (tmm, D_MODEL), lambda i, j, k: (i, 0)), BF16, (nm, 1, N_DEV // 2), after=sent_e[4] + rep_sent[4],
             b_join=True)
    dx0, d_g00 = tile_bwd("l0_dprenorm", fn_input_norm, m=m, tm=tm, nj=1, rows=[Row(x0)], pars=[Par(gain(0, 0))],
                          cts=[Row(dx0a), Row(dh0)], drows=[Out(D_MODEL, F32)])

    d_gains = jnp.stack([jnp.concatenate([d_g00, d_g01, d_g02, d_g03], axis=0), jnp.concatenate([d_g10, d_g11, d_g12, d_g13], axis=0)])
    d_ffn_cw = jnp.stack([d_cw0, d_cw1], axis=2).reshape(N_DEV, n_layer, FFN_CONV, FF_BLK)
    r_in_o, r_out_o, r_up1, r_down1 = _split_exchange("exchange_l1_wait", sent1[2], sent1[3], sent1[:2], dx0)
    r_up0, r_down0 = _split_exchange("exchange_ffn0_wait", sent0[2], sent0[3], sent0[:2], dx0)
    r_in_e, r_out_e = _split_exchange("exchange_even_wait", sent_e[2], sent_e[3], sent_e[:2], dx0)
    recv, res = {}, {}
    flipped = ("odd_w_in", "ffn_w_up")
    view = lambda n, t: jnp.swapaxes(t, 1, 2) if n in flipped else t
    for n, r in (("even_w_in", r_in_e), ("even_w_out", r_out_e), ("odd_w_in", r_in_o), ("odd_w_out", r_out_o)):
        res[n] = [view(n, t) for t in adam_tiled("adam_" + n, r, view(n, w[n]), view(n, mom[n]), view(n, var[n]))]
    for n, parts_l in (("ffn_w_up", (r_up0, r_up1)), ("ffn_w_down", (r_down0, r_down1))):
        wmv = (view(n, w[n]), view(n, mom[n]), view(n, var[n]))
        first_layer = adam_tiled(f"adam_{n}_0", parts_l[0], *wmv, layer=0)
        res[n] = [view(n, t) for t in adam_tiled(f"adam_{n}_1", parts_l[1], *wmv, layer=1, prev=first_layer)]
    small_send = [_cols_to_blocks(d_gains), _cols_to_blocks(d_rcw[None]), d_ffn_cw]
    recv.update(zip(SMALL_SHARDED, all_to_all("exchange_small", small_send)))

    parts = gather_wait("gather_partials_wait", rep_sent[0], rep_sent[1], rep_sent[2], rep_sent[3], dx0)
    for n, p in zip(REPLICATED, parts):
        recv[n] = p
    small = SMALL_SHARDED + REPLICATED
    small_res, (loss_sum,) = adam_small("adam_small", [(recv[n], w[n], mom[n], var[n]) for n in small], [parts[-1]])
    res.update(dict(zip(small, small_res)))

    out = [loss_sum[0, 0], dx0.reshape(x.shape)]
    for k in range(4):
        out += [res[n][k] for n in NAMES]
    return tuple(out)
```

```python
import functools

import jax
import jax.numpy as jnp
from jax import lax
from jax.experimental import pallas as pl
from jax.experimental.pallas import tpu as pltpu

F32 = jnp.float32
BF16 = jnp.bfloat16

D_MODEL = 1024
A_HEADS = 4
A_WIDTH = 512
HGRN_CHUNK = 64
HGRN_SEG = 2048
B_WIDTH = 512
B_BLOCKS = 8
B_BLOCK_DIM = 64
B_CONV = 4
RG_C = 8.0
C_HEADS = 16
C_HEAD_DIM = 64
D_FF = 2816
FFN_CONV = 3
EPS = 1e-6
LANES = 128
HALO = 16
N_DEV = 8
FF_BLK = 2 * D_FF // N_DEV
MESH = pl.DeviceIdType.MESH
NEG = -1e30
VMEM_LIMIT = 56 * 1024 * 1024
MM_ROWS = 2048

ADAM_LR = 0.001
ADAM_B1 = 0.9
ADAM_B2 = 0.999
ADAM_EPS = 1e-08
ADAM_WD = 0.01
ADAM_STEP = 10


def _dg(a, b, pat):
    nb = a.ndim - 2
    batch = (tuple(range(nb)), tuple(range(nb)))
    ca = a.ndim - 1 if pat[0] == "n" else a.ndim - 2
    cb = b.ndim - 2 if pat[1] == "n" else b.ndim - 1
    return lax.dot_general(a.astype(BF16), b.astype(BF16), (((ca,), (cb,)), batch), preferred_element_type=F32)


@functools.partial(jax.custom_vjp, nondiff_argnums=(2,))
def bdot(a, b, pat):
    return _dg(a, b, pat)


def _bdot_fwd(a, b, pat):
    return _dg(a, b, pat), (a, b)


def _bdot_bwd(pat, res, g):
    a, b = res
    if pat == "nn":
        return _dg(g, b, "nt"), _dg(a, g, "tn")
    if pat == "nt":
        return _dg(g, b, "nn"), _dg(g, a, "tn")
    return _dg(b, g, "nt"), _dg(a, g, "nn")


bdot.defvjp(_bdot_fwd, _bdot_bwd)


def _shift_raw(x, s, up, fill):
    if s == 0:
        return x
    n = x.shape[0]
    r = pltpu.roll(x, (n - s) if up else s, 0)
    idx = lax.broadcasted_iota(jnp.int32, x.shape, 0)
    mask = (idx >= n - s) if up else (idx < s)
    return jnp.where(mask, jnp.asarray(fill, x.dtype), r)


@functools.partial(jax.custom_vjp, nondiff_argnums=(1,))
def shift_down(x, s):
    return _shift_raw(x, s, False, 0.0)


def _shift_down_fwd(x, s):
    return _shift_raw(x, s, False, 0.0), None


def _shift_down_bwd(s, _, g):
    return (_shift_raw(g, s, True, 0.0),)


shift_down.defvjp(_shift_down_fwd, _shift_down_bwd)


def _scan_impl(a, u, up):
    n = a.shape[0]
    s = 1
    while s < n:
        u = a * _shift_raw(u, s, up, 0.0) + u
        if 2 * s < n:
            a = a * _shift_raw(a, s, up, 1.0)
        s *= 2
    return u


@jax.custom_vjp
def lin_scan(a, u):
    return _scan_impl(a, u, False)


def _lin_scan_fwd(a, u):
    h = _scan_impl(a, u, False)
    return h, (a, h)


def _lin_scan_bwd(res, g):
    a, h = res
    gh = _scan_impl(_shift_raw(a, 1, True, 0.0), g, True)
    return gh * _shift_raw(h, 1, False, 0.0), gh


lin_scan.defvjp(_lin_scan_fwd, _lin_scan_bwd)


def _cumsum_impl(x, up, period):
    n = x.shape[0]
    span = n if period is None else period
    idx = lax.broadcasted_iota(jnp.int32, x.shape, 0)
    pos = idx if period is None else idx % period
    s = 1
    while s < span:
        sh = _shift_raw(x, s, up, 0.0)
        if period is not None:
            keep = (pos < period - s) if up else (pos >= s)
            sh = jnp.where(keep, sh, 0.0)
        x = x + sh
        s *= 2
    return x


@functools.partial(jax.custom_vjp, nondiff_argnums=(1,))
def cumsum_rows(x, period):
    return _cumsum_impl(x, False, period)


def _cumsum_fwd(x, period):
    return _cumsum_impl(x, False, period), None


def _cumsum_bwd(period, _, g):
    return (_cumsum_impl(g, True, period),)


cumsum_rows.defvjp(_cumsum_fwd, _cumsum_bwd)


def _sigmoid(x):
    return jax.nn.sigmoid(x)


def _expm1(x):
    return jnp.tanh(0.5 * x) * (jnp.exp(x) + 1.0)


def _softplus(x):
    return jnp.maximum(x, 0.0) + jnp.log(1.0 + jnp.exp(-jnp.abs(x)))


def _rms(x, g):
    return x * lax.rsqrt(jnp.mean(x * x, axis=-1, keepdims=True) + EPS) * g


def fn_prenorm(x, g):
    return (_rms(x, g).astype(BF16),)


def fn_prenorm_after(x, g, _token):
    return fn_prenorm(x, g)


def fn_addnorm2(x, y, g_post, g_pre):
    x1 = x + _rms(y, g_post)
    return x1, _rms(x1, g_pre).astype(BF16)


def fn_addnorm2_after(x, y, g_post, g_pre, _token):
    return fn_addnorm2(x, y, g_post, g_pre)


def fn_input_norm(x, g):
    return x, _rms(x, g).astype(BF16)


def _causal_conv(x, w, b, taps):
    c = b
    for k in range(taps):
        c = c + w[k:k + 1, :] * shift_down(x, taps - 1 - k)
    return c


def fn_rglru(xb, yb, cw, cb, wa, ba, wx, bx, lam):
    xf = _causal_conv(xb, cw, cb, B_CONV)
    r = _sigmoid(bdot(xf, wa, "nn") + ba)
    i = _sigmoid(bdot(xf, wx, "nn") + bx)
    log_a = -RG_C * r * _softplus(-lam)
    a = jnp.exp(log_a)
    u = jnp.sqrt(-_expm1(2.0 * log_a)) * (i * xf)
    h = lin_scan(a, u)
    return ((h * jax.nn.gelu(yb)).astype(BF16),)


def fn_fox_gate(zf, bias):
    return (cumsum_rows(jax.nn.log_sigmoid(zf + bias), None),)


def fn_hgrn_seg(q, fl, v, g, st, logits, hn):
    rows = q.shape[0]
    nc = rows // HGRN_CHUNK
    l0, l1, l2 = logits[0:1, :], logits[1:2, :], logits[2:3, :]
    mx = jnp.maximum(jnp.maximum(l0, l1), l2)
    e0, e1, e2 = jnp.exp(l0 - mx), jnp.exp(l1 - mx), jnp.exp(l2 - mx)
    lb = e0 / (e0 + e1 + e2)
    forget = lb + (1.0 - lb) * _sigmoid(fl)
    qs = q * _sigmoid(q)
    kk = 1.0 - forget
    logf = jnp.log(forget)
    bcum = cumsum_rows(logf, HGRN_CHUNK)
    c3 = lambda t: t.reshape(nc, HGRN_CHUNK, 128)
    b_last = jnp.sum(c3(logf), axis=1, keepdims=True)
    bcum3 = c3(bcum)
    q_dec = c3(qs) * jnp.exp(bcum3)
    k_dec = c3(kk) * jnp.exp(-bcum3)
    k_upd = c3(kk) * jnp.exp(b_last - bcum3)
    v3 = c3(v)
    scores = bdot(q_dec, k_dec, "nt")
    ri = lax.broadcasted_iota(jnp.int32, scores.shape, 1)
    ci = lax.broadcasted_iota(jnp.int32, scores.shape, 2)
    scores = jnp.where(ri >= ci, scores, 0.0)
    o = bdot(scores, v3, "nn")
    upd_t = bdot(v3, k_upd, "tn")
    dec = jnp.exp(b_last)
    prev = []
    for n in range(nc):
        prev.append(st)
        st = st * dec[n] + upd_t[n]
    o = o + bdot(q_dec, jnp.stack(prev), "nt")
    o = o.reshape(rows, 128)
    o = o * lax.rsqrt(jnp.mean(o * o, axis=-1, keepdims=True) + EPS) * hn
    return (o * _sigmoid(g)).astype(BF16), st


def _ffn_conv(xg, xv, cw, cb):
    cg = _causal_conv(xg, cw[0], cb[0], FFN_CONV)[HALO:]
    cv = _causal_conv(xv, cw[1], cb[1], FFN_CONV)[HALO:]
    return cg, cv


def _ffn_gate(cg, cv):
    return jax.nn.gelu(cg) * cv


class Row:
    def __init__(self, arr, cb=None, off=0):
        self.arr, self.cb, self.off = arr, cb, off

    def spec(self, tm):
        if self.cb is None:
            return pl.BlockSpec((tm, self.arr.shape[1]), lambda j, i: (i, 0))
        off = self.off
        return pl.BlockSpec((tm, self.cb), lambda j, i: (i, j + off))


class Par:
    def __init__(self, arr, kind="full", bs=None):
        self.arr, self.kind, self.bs = arr, kind, bs

    def block(self):
        if self.kind == "full":
            return self.arr.shape
        if self.kind == "col":
            return (self.arr.shape[0], self.bs)
        return (self.bs, self.arr.shape[1])

    def spec(self):
        if self.kind == "full":
            return pl.BlockSpec(self.block(), lambda j, i: (0, 0))
        if self.kind == "col":
            return pl.BlockSpec(self.block(), lambda j, i: (0, j))
        return pl.BlockSpec(self.block(), lambda j, i: (j, 0))


class Out:
    def __init__(self, width, dtype, cb=None, off=0):
        self.width, self.dtype, self.cb, self.off = width, dtype, cb, off

    def spec(self, tm):
        if self.cb is None:
            return pl.BlockSpec((tm, self.width), lambda j, i: (i, 0))
        off = self.off
        return pl.BlockSpec((tm, self.cb), lambda j, i: (i, j + off))


def _params(sem):
    return pltpu.CompilerParams(dimension_semantics=sem, vmem_limit_bytes=VMEM_LIMIT)


def tile_fwd(name, fn, *, m, tm, nj, rows, pars, outs, n_acc=0, into=None):
    n_r, n_p, n_o = len(rows), len(pars), len(outs)
    n_in = n_r + n_p + (into is not None)

    def body(*refs):
        ins = [r[...] for r in refs[:n_r + n_p]]
        res = fn(*ins)
        o_refs = refs[n_in:]
        for k in range(n_o):
            o_refs[k][...] = res[k].astype(o_refs[k].dtype)
        first = jnp.logical_and(pl.program_id(0) == 0, pl.program_id(1) == 0)
        for k in range(n_acc):
            ref = o_refs[n_o + k]

            @pl.when(first)
            def _():
                ref[...] = jnp.zeros_like(ref)

            ref[...] += res[n_o + k]

    out_shape = [jax.ShapeDtypeStruct((m, o.width), o.dtype) for o in outs]
    out_specs = [o.spec(tm) for o in outs]
    for _ in range(n_acc):
        out_shape.append(jax.ShapeDtypeStruct((1, LANES), F32))
        out_specs.append(pl.BlockSpec((1, LANES), lambda j, i: (0, 0)))
    sem = ("arbitrary", "arbitrary") if n_acc else ("parallel", "parallel")
    extra = [] if into is None else [into]
    return pl.pallas_call(
        body, grid=(nj, m // tm), name=name,
        in_specs=[r.spec(tm) for r in rows] + [p.spec() for p in pars] + [pl.BlockSpec(memory_space=pl.ANY)] * len(extra),
        out_specs=out_specs, out_shape=out_shape, compiler_params=_params(sem),
        input_output_aliases={} if into is None else {n_in - 1: 0},
    )(*[r.arr for r in rows], *[p.arr for p in pars], *extra)


def tile_bwd(name, fn, *, m, tm, nj, rows, pars, cts, drows):
    n_r, n_p, n_c = len(rows), len(pars), len(cts)
    want = [k for k in range(n_r) if drows[k] is not None]

    def body(*refs):
        ins = [r[...] for r in refs[:n_r + n_p]]
        ct = [r[...] for r in refs[n_r + n_p:n_r + n_p + n_c]]
        o_refs = refs[n_r + n_p + n_c:]
        res, vjp = jax.vjp(fn, *ins)
        grads = vjp(tuple(c.astype(r.dtype) for c, r in zip(ct, res)))
        for pos, k in enumerate(want):
            o_refs[pos][...] = grads[k].astype(o_refs[pos].dtype)
        for k in range(n_p):
            ref = o_refs[len(want) + k]
            first = pl.program_id(1) == 0
            if pars[k].kind == "full":
                first = jnp.logical_and(first, pl.program_id(0) == 0)

            @pl.when(first)
            def _():
                ref[...] = jnp.zeros_like(ref)

            ref[...] += grads[n_r + k].astype(F32)

    out_shape = [jax.ShapeDtypeStruct((m, drows[k].width), drows[k].dtype) for k in want]
    out_specs = [drows[k].spec(tm) for k in want]
    for p in pars:
        out_shape.append(jax.ShapeDtypeStruct(p.arr.shape, F32))
        out_specs.append(p.spec())
    return pl.pallas_call(
        body, grid=(nj, m // tm), name=name,
        in_specs=[r.spec(tm) for r in rows] + [p.spec() for p in pars] + [c.spec(tm) for c in cts],
        out_specs=out_specs, out_shape=out_shape, compiler_params=_params(("arbitrary", "arbitrary")),
    )(*[r.arr for r in rows], *[p.arr for p in pars], *[c.arr for c in cts])


def loss_head(name, x, y, tgt, g, *, m, tm):
    def body(x_ref, y_ref, t_ref, g_ref, dout_ref, dy_ref, loss_ref, dg_ref):
        normed, vjp = jax.vjp(_rms, y_ref[...], g_ref[...])
        err = x_ref[...] + normed - t_ref[...]
        dout = err * (1.0 / D_MODEL)
        dy, dg = vjp(dout)
        dout_ref[...] = dout
        dy_ref[...] = dy.astype(dy_ref.dtype)

        @pl.when(pl.program_id(0) == 0)
        def _():
            loss_ref[...] = jnp.zeros_like(loss_ref)
            dg_ref[...] = jnp.zeros_like(dg_ref)

        loss_ref[...] += 0.5 * jnp.sum(jnp.mean(err * err, axis=-1, keepdims=True), axis=0, keepdims=True)
        dg_ref[...] += dg

    row = pl.BlockSpec((tm, D_MODEL), lambda i: (i, 0))
    whole = lambda w: pl.BlockSpec((1, w), lambda i: (0, 0))
    return pl.pallas_call(
        body, grid=(m // tm,), name=name, in_specs=[row, row, row, whole(D_MODEL)],
        out_specs=[row, row, whole(LANES), whole(D_MODEL)],
        out_shape=[jax.ShapeDtypeStruct((m, D_MODEL), F32), jax.ShapeDtypeStruct((m, D_MODEL), BF16),
                   jax.ShapeDtypeStruct((1, LANES), F32), jax.ShapeDtypeStruct((1, D_MODEL), F32)],
        compiler_params=_params(("arbitrary",)),
    )(x, y, tgt, g)


class Blk:
    def __init__(self, arr, block, index):
        self.arr, self.block, self.index = arr, block, index

    def spec(self):
        return pl.BlockSpec(self.block, self.index)


def _flat2(v):
    return v if v.ndim == 2 else v.reshape(-1, v.shape[-1])


def mm(name, pat, a, b, o, out_dtype, grid, after=None, b_join=False, o_split=False):
    nk = grid[2]
    o_shape = o.arr

    def put(o_ref, r):
        if o_split:
            half = r.shape[1] // 2
            o_ref[0] = r[:, :half].astype(out_dtype)
            o_ref[1] = r[:, half:].astype(out_dtype)
        else:
            o_ref[...] = r.astype(out_dtype).reshape(o_ref.shape)

    def body(*refs):
        a_ref, b_ref = refs[0], refs[1]
        o_ref = refs[3] if after is not None else refs[2]
        bv = jnp.concatenate([b_ref[0], b_ref[1]], axis=1) if b_join else _flat2(b_ref[...])
        r = _dg(_flat2(a_ref[...]), bv, pat)
        if nk == 1:
            put(o_ref, r)
            return
        acc_ref = refs[-1]
        kk = pl.program_id(2)

        @pl.when(kk == 0)
        def _():
            acc_ref[...] = r

        @pl.when(kk > 0)
        def _():
            acc_ref[...] += r

        @pl.when(kk == nk - 1)
        def _():
            put(o_ref, acc_ref[...])

    ob = [d for d in o.block if d is not None]
    if o_split:
        acc_shape = (ob[1], 2 * ob[2])
    else:
        acc_shape = (ob[0], ob[1]) if len(ob) == 2 else (ob[0] * ob[1], ob[2])
    in_specs = [a.spec(), b.spec()]
    args = [a.arr, b.arr]
    if after is not None:
        in_specs.append(pl.BlockSpec(memory_space=pl.ANY))
        args.append(after)
    return pl.pallas_call(
        body, grid=grid, name=name, in_specs=in_specs, out_specs=o.spec(),
        out_shape=jax.ShapeDtypeStruct(o_shape, out_dtype),
        scratch_shapes=[pltpu.VMEM(acc_shape, F32)] if nk > 1 else [],
        compiler_params=_params(("parallel", "parallel", "arbitrary")),
    )(*args)


def _div_tile(n, cap):
    if n <= cap:
        return n
    best = 128
    for t in range(128, cap + 1, 128):
        if n % t == 0:
            best = t
    return best


def mm2d(name, pat, a, b, out_dtype=F32):
    if pat == "tn":
        k, m = a.shape
    else:
        m, k = a.shape
    n = b.shape[0] if pat == "nt" else b.shape[1]
    tm, tn, tk = _div_tile(m, MM_ROWS), _div_tile(n, 1024), _div_tile(k, MM_ROWS)
    a_blk = Blk(a, (tk, tm), lambda i, j, kk: (kk, i)) if pat == "tn" else Blk(a, (tm, tk), lambda i, j, kk: (i, kk))
    b_blk = Blk(b, (tn, tk), lambda i, j, kk: (j, kk)) if pat == "nt" else Blk(b, (tk, tn), lambda i, j, kk: (kk, j))
    o_blk = Blk((m, n), (tm, tn), lambda i, j, kk: (i, j))
    return mm(name, pat, a_blk, b_blk, o_blk, out_dtype, (m // tm, n // tn, k // tk))


def hgrn_fwd(name, z, logits, hnorm, *, n_batch, seq):
    m = n_batch * seq
    ts = min(HGRN_SEG, seq)
    n_seg = seq // ts

    def body(q_ref, f_ref, v_ref, g_ref, lg_ref, hn_ref, o_ref, sp_ref, st_ref):
        s = pl.program_id(2)

        @pl.when(s == 0)
        def _():
            st_ref[...] = jnp.zeros_like(st_ref)

        st = st_ref[...]
        sp_ref[...] = st
        o, st_new = fn_hgrn_seg(q_ref[...], f_ref[...], v_ref[...], g_ref[...], st, lg_ref[...], hn_ref[...])
        o_ref[...] = o
        st_ref[...] = st_new

    part = lambda p: pl.BlockSpec((ts, 128), lambda h, b, s: (b * n_seg + s, 4 * p + h))
    return pl.pallas_call(
        body, grid=(A_HEADS, n_batch, n_seg), name=name,
        in_specs=[part(0), part(1), part(2), part(3),
                  pl.BlockSpec((3, 128), lambda h, b, s: (0, h)),
                  pl.BlockSpec((1, 128), lambda h, b, s: (0, h))],
        out_specs=[pl.BlockSpec((ts, 128), lambda h, b, s: (b * n_seg + s, h)),
                   pl.BlockSpec((128, 128), lambda h, b, s: ((b * n_seg + s) * A_HEADS + h, 0))],
        out_shape=[jax.ShapeDtypeStruct((m, D_MODEL), BF16),
                   jax.ShapeDtypeStruct((n_batch * n_seg * A_HEADS * 128, 128), F32)],
        scratch_shapes=[pltpu.VMEM((128, 128), F32)],
        compiler_params=_params(("arbitrary", "arbitrary", "arbitrary")),
    )(z, z, z, z, logits, hnorm)


def hgrn_bwd(name, z, sprev, logits, hnorm, do, *, n_batch, seq):
    m = n_batch * seq
    ts = min(HGRN_SEG, seq)
    n_seg = seq // ts

    def body(q_ref, f_ref, v_ref, g_ref, sp_ref, lg_ref, hn_ref, do_ref, dq_ref, df_ref, dv_ref, dg_ref, dlg_ref, dhn_ref, dst_ref):
        s = pl.program_id(2)

        @pl.when(s == 0)
        def _():
            dst_ref[...] = jnp.zeros_like(dst_ref)

        res, vjp = jax.vjp(fn_hgrn_seg, q_ref[...], f_ref[...], v_ref[...], g_ref[...], sp_ref[...], lg_ref[...], hn_ref[...])
        dq, df, dv, dg, dst, dlg, dhn = vjp((do_ref[...].astype(res[0].dtype), dst_ref[...]))
        dq_ref[...] = dq.astype(dq_ref.dtype)
        df_ref[...] = df.astype(df_ref.dtype)
        dv_ref[...] = dv.astype(dv_ref.dtype)
        dg_ref[...] = dg.astype(dg_ref.dtype)
        dst_ref[...] = dst
        first = jnp.logical_and(pl.program_id(1) == 0, s == 0)

        @pl.when(first)
        def _():
            dlg_ref[...] = jnp.zeros_like(dlg_ref)
            dhn_ref[...] = jnp.zeros_like(dhn_ref)

        dlg_ref[...] += dlg
        dhn_ref[...] += dhn

    rev = lambda b, s: b * n_seg + (n_seg - 1 - s)
    part = lambda p: pl.BlockSpec((ts, 128), lambda h, b, s: (rev(b, s), 4 * p + h))
    head = pl.BlockSpec((ts, 128), lambda h, b, s: (rev(b, s), h))
    dpart = jax.ShapeDtypeStruct((m, A_WIDTH), BF16)
    return pl.pallas_call(
        body, grid=(A_HEADS, n_batch, n_seg), name=name,
        in_specs=[part(0), part(1), part(2), part(3),
                  pl.BlockSpec((128, 128), lambda h, b, s: (rev(b, s) * A_HEADS + h, 0)),
                  pl.BlockSpec((3, 128), lambda h, b, s: (0, h)),
                  pl.BlockSpec((1, 128), lambda h, b, s: (0, h)),
                  head],
        out_specs=[head, head, head, head,
                   pl.BlockSpec((3, 128), lambda h, b, s: (0, h)),
                   pl.BlockSpec((1, 128), lambda h, b, s: (0, h))],
        out_shape=[dpart, dpart, dpart, dpart,
                   jax.ShapeDtypeStruct(logits.shape, F32),
                   jax.ShapeDtypeStruct(hnorm.shape, F32)],
        scratch_shapes=[pltpu.VMEM((128, 128), F32)],
        compiler_params=_params(("arbitrary", "arbitrary", "arbitrary")),
    )(z, z, z, z, sprev, logits, hnorm, do)


FFN_ROWS = 1024
FFN_LANES = 128


def _ffn_tiles(m, seq):
    tm = min(FFN_ROWS, seq)
    return tm, seq // tm, m // tm


def ffn_mid_fwd(name, hid, cw, cb, layer, *, m, seq):
    tm, n_t, n_i = _ffn_tiles(m, seq)
    hb = tm // HALO

    def body(x_ref, xb_ref, cw_ref, cb_ref, o_ref, c_ref):
        first = pl.program_id(1) % n_t == 0
        for l0 in range(0, FF_BLK, FFN_LANES):
            lanes = slice(l0, min(l0 + FFN_LANES, FF_BLK))
            before = jnp.where(first, 0.0, xb_ref[:, :, lanes])
            ext = jnp.concatenate([before, x_ref[:, :, lanes]], axis=1)
            cg, cv = _ffn_conv(ext[0], ext[1], cw_ref[:, :, lanes], cb_ref[:, :, lanes])
            o_ref[:, lanes] = _ffn_gate(cg, cv).astype(o_ref.dtype)
            c_ref[0, :, lanes] = cg.astype(c_ref.dtype)
            c_ref[1, :, lanes] = cv.astype(c_ref.dtype)

    return pl.pallas_call(
        body, grid=(N_DEV // 2, n_i), name=name,
        in_specs=[pl.BlockSpec((2, None, tm, FF_BLK), lambda d, i: (0, d, i, 0)),
                  pl.BlockSpec((2, None, HALO, FF_BLK), lambda d, i: (0, d, jnp.maximum(i * hb - 1, 0), 0)),
                  pl.BlockSpec((2, None, None, FFN_CONV, FF_BLK), lambda d, i: (0, d, layer, 0, 0)),
                  pl.BlockSpec((None, 2, None, 1, FF_BLK), lambda d, i: (layer, 0, d, 0, 0))],
        out_specs=[pl.BlockSpec((None, tm, FF_BLK), lambda d, i: (d, i, 0)),
                   pl.BlockSpec((2, None, tm, FF_BLK), lambda d, i: (0, d, i, 0))],
        out_shape=[jax.ShapeDtypeStruct((N_DEV // 2, m, FF_BLK), BF16),
                   jax.ShapeDtypeStruct((2, N_DEV // 2, m, FF_BLK), BF16)],
        compiler_params=_params(("parallel", "parallel")),
    )(hid, hid, cw, cb)


def ffn_mid_bwd(name, hid, conv, cw, dact, layer, *, m, seq):
    tm, n_t, n_i = _ffn_tiles(m, seq)
    hb = tm // HALO
    last_blk = m // HALO - 1

    rc = min(FFN_ROWS, tm)
    lane_chunks = [(l0, min(FFN_LANES, FF_BLK - l0)) for l0 in range(0, FF_BLK, FFN_LANES)]

    def body(x_ref, c_ref, ca_ref, cw_ref, da_ref, daa_ref, dx_ref, dcw_ref, dcb_ref, cext_ref, dext_ref):
        i = pl.program_id(1)
        last = i % n_t == n_t - 1
        cext_ref[:, :tm] = c_ref[...]
        cext_ref[:, tm:] = ca_ref[...]
        dext_ref[:tm] = da_ref[...]
        dext_ref[tm:] = jnp.where(last, jnp.zeros_like(daa_ref[...]), daa_ref[...])

        @pl.when(i == 0)
        def _():
            dcw_ref[...] = jnp.zeros_like(dcw_ref)
            dcb_ref[...] = jnp.zeros_like(dcb_ref)

        for l0, lw in lane_chunks:
            lanes = slice(l0, l0 + lw)

            def chunk(c, sums, lanes=lanes, lw=lw):
                r0 = pl.multiple_of(c * rc, rc)
                ext = pl.ds(r0, rc + HALO)
                cg, cv = cext_ref[0, ext, lanes].astype(F32), cext_ref[1, ext, lanes].astype(F32)
                _, vjp_gate = jax.vjp(_ffn_gate, cg, cv)
                dconv = vjp_gate(dext_ref[ext, lanes].astype(F32))
                out = []
                for half in range(2):
                    x = x_ref[half, pl.ds(r0, rc), lanes]
                    dx = None
                    for k in range(FFN_CONV):
                        s = FFN_CONV - 1 - k
                        dc_s = _shift_raw(dconv[half], s, True, 0.0)[:rc]
                        term = cw_ref[half, k:k + 1, lanes] * dc_s
                        dx = term if dx is None else dx + term
                        out.append(sums[len(out)] + jnp.sum(x * dc_s, axis=0, keepdims=True))
                    out.append(sums[len(out)] + jnp.sum(dconv[half][:rc], axis=0, keepdims=True))
                    dx_ref[half, pl.ds(r0, rc), lanes] = dx.astype(dx_ref.dtype)
                return tuple(out)

            zero = jnp.zeros((1, lw), F32)
            sums = lax.fori_loop(0, tm // rc, chunk, (zero,) * (2 * (FFN_CONV + 1)))
            for half in range(2):
                base = half * (FFN_CONV + 1)
                for k in range(FFN_CONV):
                    dcw_ref[half, k:k + 1, lanes] += sums[base + k]
                dcb_ref[half, :, lanes] += sums[base + FFN_CONV]

    return pl.pallas_call(
        body, grid=(N_DEV // 2, n_i), name=name,
        in_specs=[pl.BlockSpec((2, None, tm, FF_BLK), lambda d, i: (0, d, i, 0)),
                  pl.BlockSpec((2, None, tm, FF_BLK), lambda d, i: (0, d, i, 0)),
                  pl.BlockSpec((2, None, HALO, FF_BLK), lambda d, i: (0, d, jnp.minimum((i + 1) * hb, last_blk), 0)),
                  pl.BlockSpec((2, None, None, FFN_CONV, FF_BLK), lambda d, i: (0, d, layer, 0, 0)),
                  pl.BlockSpec((None, tm, FF_BLK), lambda d, i: (d, i, 0)),
                  pl.BlockSpec((None, HALO, FF_BLK), lambda d, i: (d, jnp.minimum((i + 1) * hb, last_blk), 0))],
        out_specs=[pl.BlockSpec((2, None, tm, FF_BLK), lambda d, i: (0, d, i, 0)),
                   pl.BlockSpec((2, None, FFN_CONV, FF_BLK), lambda d, i: (0, d, 0, 0)),
                   pl.BlockSpec((2, None, 1, FF_BLK), lambda d, i: (0, d, 0, 0))],
        out_shape=[jax.ShapeDtypeStruct((2, N_DEV // 2, m, FF_BLK), BF16),
                   jax.ShapeDtypeStruct((2, N_DEV // 2, FFN_CONV, FF_BLK), F32),
                   jax.ShapeDtypeStruct((2, N_DEV // 2, 1, FF_BLK), F32)],
        scratch_shapes=[pltpu.VMEM((2, tm + HALO, FF_BLK), BF16), pltpu.VMEM((tm + HALO, FF_BLK), BF16)],
        compiler_params=_params(("arbitrary", "arbitrary")),
    )(hid, conv, conv, cw, dact, dact)


ATT_BLK = 512
ATT_BLK_FWD = 1024
N_PAIR = C_HEADS // 2
TERM_W = C_HEADS * LANES


def term_placement():
    import numpy as np
    place = np.zeros((3, LANES, TERM_W), np.float32)
    ones_q = np.zeros((1, TERM_W), np.float32)
    ones_k = np.zeros((1, TERM_W), np.float32)
    for h in range(C_HEADS):
        for j in range(3):
            place[j, h, h * LANES + C_HEAD_DIM + j] = 1.0
            ones_q[0, h * LANES + C_HEAD_DIM + 3 + j] = 1.0
            ones_k[0, h * LANES + C_HEAD_DIM + j] = 1.0
    return (jnp.asarray(place.reshape(3 * LANES, TERM_W), BF16), jnp.asarray(ones_q, F32), jnp.asarray(ones_k, F32))


def fn_fox_terms(c, place, ones_q, ones_k):
    parts = _split3(c)
    placed = sum(_dg(parts[j], place[j * LANES:(j + 1) * LANES], "nn") for j in range(3))
    return (placed + ones_q).astype(BF16), (ones_k - pltpu.roll(placed, 3, 1)).astype(BF16)


def _head_tile(z, terms, e):
    lane = lax.broadcasted_iota(jnp.int32, z.shape, 1)
    base = z if e == 0 else pltpu.roll(z, C_HEAD_DIM, 1)
    return jnp.where(lane < C_HEAD_DIM, base, terms.astype(z.dtype))


def _head_only(z, e):
    lane = lax.broadcasted_iota(jnp.int32, z.shape, 1)
    mine = (lane < C_HEAD_DIM) if e == 0 else (lane >= C_HEAD_DIM)
    return jnp.where(mine, z, jnp.zeros_like(z)).astype(BF16)


def _pair_tile(a0, a1):
    lane = lax.broadcasted_iota(jnp.int32, a0.shape, 1)
    return jnp.where(lane < C_HEAD_DIM, a0, pltpu.roll(a1, C_HEAD_DIM, 1))


def _lane_col(a, k):
    lane = lax.broadcasted_iota(jnp.int32, a.shape, 1)
    return jnp.sum(jnp.where(lane == k, a, 0.0), axis=1, keepdims=True)


def _causal(s):
    key = lax.broadcasted_iota(jnp.int32, s.shape, 0)
    qry = lax.broadcasted_iota(jnp.int32, s.shape, 1)
    return qry >= key


def fox_pair_fwd(name, z, qterm, kterm, *, n_batch, seq):
    m = n_batch * seq
    blk = min(ATT_BLK_FWD, seq)
    nq = seq // blk
    dh = C_HEAD_DIM

    def body(zq_ref, zk_ref, zv_ref, qt_ref, kt_ref, o_ref, lse_ref, ka_ref, vt_ref):
        qi = pl.program_id(2)

        @pl.when(qi == 0)
        def _():
            zk = zk_ref[...]
            for e in range(2):
                ka_ref[e] = _head_tile(zk, kt_ref[:, e * LANES:(e + 1) * LANES], e).astype(BF16)
            for cb in range(nq):
                vt_ref[cb] = zv_ref[cb * blk:(cb + 1) * blk, :].T.astype(BF16)

        zq = zq_ref[...] * dh ** -0.5
        qa = [_head_tile(zq, qt_ref[:, e * LANES:(e + 1) * LANES], e).astype(BF16) for e in range(2)]

        def block(j, carry, diagonal):
            rows = pl.ds(pl.multiple_of(j * blk, blk), blk)
            out = []
            for e in range(2):
                mx, l, acc = carry[e]
                s = _dg(ka_ref[e, rows, :], qa[e], "nt")
                if diagonal:
                    s = jnp.where(_causal(s), s, NEG)
                mx_new = jnp.maximum(mx, jnp.max(s, axis=0, keepdims=True))
                p = jnp.exp(s - mx_new)
                alpha = jnp.exp(mx - mx_new)
                l = alpha * l + jnp.sum(p, axis=0, keepdims=True)
                acc = alpha * acc + _dg(vt_ref[j, e * dh:(e + 1) * dh, :], p, "nn")
                out.append((mx_new, l, acc))
            return tuple(out)

        one = (jnp.full((1, blk), NEG, F32), jnp.zeros((1, blk), F32), jnp.zeros((dh, blk), F32))
        carry = lax.fori_loop(0, qi, lambda j, cr: block(j, cr, False), (one, one))
        res = block(qi, carry, True)
        ot = jnp.concatenate([res[e][2] / res[e][1] for e in range(2)], axis=0)
        o_ref[...] = ot.T.astype(o_ref.dtype)
        for e in range(2):
            lse_ref[e] = res[e][0] + jnp.log(res[e][1])

    col = lambda part: (lambda b, g, i: (b, part * N_PAIR + g))
    return pl.pallas_call(
        body, grid=(n_batch, N_PAIR, nq), name=name,
        in_specs=[pl.BlockSpec((blk, LANES), lambda b, g, i: (b * nq + i, g)),
                  pl.BlockSpec((seq, LANES), col(1)),
                  pl.BlockSpec((seq, LANES), col(2)),
                  pl.BlockSpec((blk, 2 * LANES), lambda b, g, i: (b * nq + i, g)),
                  pl.BlockSpec((seq, 2 * LANES), lambda b, g, i: (b, g))],
        out_specs=[pl.BlockSpec((blk, LANES), lambda b, g, i: (b * nq + i, g)),
                   pl.BlockSpec((None, None, None, 2, 1, blk), lambda b, g, i: (b, g, i, 0, 0, 0))],
        out_shape=[jax.ShapeDtypeStruct((m, D_MODEL), BF16), jax.ShapeDtypeStruct((n_batch, N_PAIR, nq, 2, 1, blk), F32)],
        scratch_shapes=[pltpu.VMEM((2, seq, LANES), BF16), pltpu.VMEM((nq, LANES, blk), BF16)],
        compiler_params=_params(("parallel", "parallel", "arbitrary")),
    )(z, z, z, qterm, kterm)


def fox_pair_bwd(name, z, qterm, kterm, o, do, lse, *, n_batch, seq):
    m = n_batch * seq
    blk = min(ATT_BLK, seq)
    nq = seq // blk
    dh = C_HEAD_DIM

    def body(zq_ref, zk_ref, zv_ref, qt_ref, kt_ref, o_ref, do_ref, lse_ref, dq_ref, dk_ref, dv_ref, dc_ref,
             qa_ref, doh_ref, del_ref, dqt_ref, dk_acc, dv_acc):
        g, j = pl.program_id(1), pl.program_id(2)
        lane = lax.broadcasted_iota(jnp.int32, (blk, LANES), 1)

        @pl.when(jnp.logical_and(g == 0, j == 0))
        def _():
            dc_ref[...] = jnp.zeros_like(dc_ref)

        @pl.when(j == 0)
        def _():
            zq = zq_ref[...] * dh ** -0.5
            dov = do_ref[...]
            for e in range(2):
                qa_ref[e] = _head_tile(zq, qt_ref[:, e * LANES:(e + 1) * LANES], e).astype(BF16)
                doh_ref[e] = _head_only(dov, e)
            for cb in range(nq):
                rows = slice(cb * blk, (cb + 1) * blk)
                prod_t = (do_ref[rows, :].astype(F32) * o_ref[rows, :].astype(F32)).T
                for e in range(2):
                    del_ref[cb, e] = jnp.sum(prod_t[e * dh:(e + 1) * dh], axis=0, keepdims=True)
            dqt_ref[...] = jnp.zeros_like(dqt_ref)

        zk, zv = zk_ref[...], zv_ref[...]
        ka32 = [_head_tile(zk, kt_ref[:, e * LANES:(e + 1) * LANES], e) for e in range(2)]
        ka = [t.astype(BF16) for t in ka32]
        kat = [t.T.astype(BF16) for t in ka32]
        vh = [_head_only(zv, e) for e in range(2)]
        dk_acc[...] = jnp.zeros_like(dk_acc)
        dv_acc[...] = jnp.zeros_like(dv_acc)

        def block(i, diagonal):
            rows = pl.ds(pl.multiple_of(i * blk, blk), blk)
            for e in range(2):
                qv, dov = qa_ref[e, rows, :], doh_ref[e, rows, :]
                p = jnp.exp(_dg(ka[e], qv, "nt") - lse_ref[i, e])
                if diagonal:
                    p = jnp.where(_causal(p), p, 0.0)
                dv_acc[...] += _dg(p, dov, "nn")
                ds = p * (_dg(vh[e], dov, "nt") - del_ref[i, e])
                dk_acc[e] += _dg(ds, qv, "nn")
                dqt_ref[i, e] += _dg(kat[e], ds, "nn")

        block(j, True)

        def rest(i, carry):
            block(i, False)
            return carry

        lax.fori_loop(j + 1, nq, rest, 0)
        dk0, dk1 = dk_acc[0], dk_acc[1]
        dk_ref[...] = _pair_tile(dk0, dk1).astype(dk_ref.dtype)
        dv_ref[...] = dv_acc[...].astype(dv_ref.dtype)
        rows_j = pl.ds(pl.multiple_of(j * blk, blk), blk)
        for e, dke in enumerate((dk0, dk1)):
            dc_ref[rows_j, :] -= jnp.where(lane == 2 * g + e, _lane_col(dke, dh + 3), 0.0)

        @pl.when(j == nq - 1)
        def _():
            for i in range(nq):
                nat = [dqt_ref[i, e].T for e in range(2)]
                rows = slice(i * blk, (i + 1) * blk)
                dq_ref[rows, :] = (_pair_tile(nat[0], nat[1]) * dh ** -0.5).astype(dq_ref.dtype)
                for e in range(2):
                    dc_ref[rows, :] += jnp.where(lane == 2 * g + e, _lane_col(nat[e], dh), 0.0)

    col = lambda part: (lambda b, g, j: (b, part * N_PAIR + g))
    colj = lambda part: (lambda b, g, j: (b * nq + j, part * N_PAIR + g))
    pair = jax.ShapeDtypeStruct((m, D_MODEL), BF16)
    return pl.pallas_call(
        body, grid=(n_batch, N_PAIR, nq), name=name,
        in_specs=[pl.BlockSpec((seq, LANES), col(0)),
                  pl.BlockSpec((blk, LANES), colj(1)),
                  pl.BlockSpec((blk, LANES), colj(2)),
                  pl.BlockSpec((seq, 2 * LANES), lambda b, g, j: (b, g)),
                  pl.BlockSpec((blk, 2 * LANES), lambda b, g, j: (b * nq + j, g)),
                  pl.BlockSpec((seq, LANES), col(0)),
                  pl.BlockSpec((seq, LANES), col(0)),
                  pl.BlockSpec((None, None, nq, 2, 1, blk), lambda b, g, j: (b, g, 0, 0, 0, 0))],
        out_specs=[pl.BlockSpec((seq, LANES), col(0)),
                   pl.BlockSpec((blk, LANES), colj(0)),
                   pl.BlockSpec((blk, LANES), colj(0)),
                   pl.BlockSpec((seq, LANES), lambda b, g, j: (b, 0))],
        out_shape=[pair, pair, pair, jax.ShapeDtypeStruct((m, LANES), F32)],
        scratch_shapes=[pltpu.VMEM((2, seq, LANES), BF16), pltpu.VMEM((2, seq, LANES), BF16),
                        pltpu.VMEM((nq, 2, 1, blk), F32), pltpu.VMEM((nq, 2, LANES, blk), F32),
                        pltpu.VMEM((2, blk, LANES), F32), pltpu.VMEM((blk, LANES), F32)],
        compiler_params=_params(("arbitrary", "arbitrary", "arbitrary")),
    )(z, z, z, qterm, kterm, o, do, lse)


def _split3(c):
    c1 = c.astype(BF16)
    r1 = c - c1.astype(F32)
    c2 = r1.astype(BF16)
    c3 = (r1 - c2.astype(F32)).astype(BF16)
    return c1, c2, c3


def _mesh_pos():
    return lax.axis_index("x"), lax.axis_index("y"), lax.axis_index("c")


def _flip(v, bit):
    return 1 - v if bit else v


def all_gather(name, blocks):
    n = len(blocks)

    def body(*refs):
        x_refs, out_refs = refs[:n], refs[n:2 * n]
        send_sems, recv_sems, local_sems = refs[2 * n:]
        x, y, c = _mesh_pos()
        me, sibling = (x, y, c), (x, y, 1 - c)
        chips = [(1 - x, y), (x, 1 - y), (1 - x, 1 - y)]

        def slot(a, px, py, pc):
            return out_refs[a].at[4 * px + 2 * py + pc]

        def copy(a, k, blk, to, src=None):
            return pltpu.make_async_remote_copy(
                src_ref=slot(a, *blk) if src is None else src, dst_ref=slot(a, *blk),
                send_sem=send_sems.at[a, k], recv_sem=recv_sems.at[a, k], device_id=to, device_id_type=MESH)

        mine = [pltpu.make_async_copy(x_refs[a], slot(a, *me), local_sems.at[a]) for a in range(n)]
        for cp in mine:
            cp.start()
        sends = []
        for a in range(n):
            sends.append(copy(a, 0, me, sibling, src=x_refs[a]))
            sends += [copy(a, 1 + j, me, (*chip, c), src=x_refs[a]) for j, chip in enumerate(chips)]
        for cp in sends:
            cp.start()
        for j, chip in enumerate(chips):
            for a in range(n):
                copy(a, 1 + j, (*chip, c), me).wait_recv()
                passed = copy(a, 4 + j, (*chip, c), sibling)
                passed.start()
                sends.append(passed)
        for a in range(n):
            copy(a, 0, sibling, me).wait_recv()
            for j, chip in enumerate(chips):
                copy(a, 4 + j, (*chip, 1 - c), me).wait_recv()
        for cp in sends:
            cp.wait_send()
        for cp in mine:
            cp.wait()

    hbm = pl.BlockSpec(memory_space=pl.ANY)
    return pl.pallas_call(
        body, name=name, out_shape=[jax.ShapeDtypeStruct((N_DEV,) + b.shape, b.dtype) for b in blocks],
        in_specs=[hbm] * n, out_specs=[hbm] * n,
        scratch_shapes=[pltpu.SemaphoreType.DMA((n, 7)), pltpu.SemaphoreType.DMA((n, 7)), pltpu.SemaphoreType.DMA((n,))],
    )(*blocks)


def _peers(x, y, c):
    return [(_flip(x, k & 4), _flip(y, k & 2), _flip(c, k & 1)) for k in range(1, N_DEV)]


def gather_start(name, blocks, lands):
    n = len(blocks)

    def body(*refs):
        x_refs, land_refs = refs[:n], refs[n:2 * n]
        send_sems, recv_sems = refs[2 * n], refs[2 * n + 1]
        token = refs[-1]
        x, y, c = _mesh_pos()
        me = 4 * x + 2 * y + c
        for k, peer in enumerate(_peers(x, y, c)):
            for a in range(n):
                pltpu.make_async_remote_copy(
                    src_ref=x_refs[a], dst_ref=land_refs[a].at[me], send_sem=send_sems.at[7 * a + k], recv_sem=recv_sems.at[7 * a + k],
                    device_id=peer, device_id_type=MESH).start()
        token[...] = jnp.zeros_like(token)

    hbm = pl.BlockSpec(memory_space=pltpu.HBM)
    sem = pl.BlockSpec(memory_space=pltpu.SEMAPHORE)
    out_shape = ([pltpu.SemaphoreType.DMA((7 * n,)), pltpu.SemaphoreType.DMA((7 * n,))]
                 + [pltpu.HBM(b.shape, b.dtype) for b in blocks] + [pltpu.HBM(l.shape, l.dtype) for l in lands]
                 + [jax.ShapeDtypeStruct((8, LANES), F32)])
    res = pl.pallas_call(
        body, name=name, out_shape=out_shape, in_specs=[hbm] * (2 * n),
        out_specs=[sem, sem] + [hbm] * (2 * n) + [pl.BlockSpec(memory_space=pltpu.VMEM)],
        input_output_aliases={a: 2 + a for a in range(2 * n)},
        compiler_params=pltpu.CompilerParams(has_side_effects=pltpu.SideEffectType.DATAFLOW_SIDE_EFFECTING),
    )(*[pltpu.with_memory_space_constraint(b, pltpu.HBM) for b in blocks],
      *[pltpu.with_memory_space_constraint(l, pltpu.HBM) for l in lands])
    return res[0], res[1], res[2:2 + n], res[2 + n:2 + 2 * n], res[-1]


def gather_wait(name, send_sems, recv_sems, blocks, lands, after):
    n = len(blocks)

    def body(*refs):
        x_refs, land_refs = refs[:n], refs[n:2 * n]
        s_sems, r_sems = refs[2 * n], refs[2 * n + 1]
        x, y, c = _mesh_pos()
        me = 4 * x + 2 * y + c
        for k, peer in enumerate(_peers(x, y, c)):
            for a in range(n):
                cp = pltpu.make_async_remote_copy(
                    src_ref=x_refs[a], dst_ref=land_refs[a].at[me], send_sem=s_sems.at[7 * a + k], recv_sem=r_sems.at[7 * a + k],
                    device_id=peer, device_id_type=MESH)
                cp.wait_send()
                cp.wait_recv()

    hbm = pl.BlockSpec(memory_space=pltpu.HBM)
    sem = pl.BlockSpec(memory_space=pltpu.SEMAPHORE)
    res = pl.pallas_call(
        body, name=name,
        out_shape=[pltpu.HBM(b.shape, b.dtype) for b in blocks] + [pltpu.HBM(l.shape, l.dtype) for l in lands],
        in_specs=[hbm] * (2 * n) + [sem, sem, pl.BlockSpec(memory_space=pl.ANY)], out_specs=[hbm] * (2 * n),
        input_output_aliases={a: a for a in range(2 * n)},
        compiler_params=pltpu.CompilerParams(has_side_effects=pltpu.SideEffectType.DATAFLOW_SIDE_EFFECTING),
    )(*blocks, *lands, send_sems, recv_sems, after)
    return res[n:]


def _split_exchange(name, sends, lands, sems, after):
    n = len(sends)
    starting = sems is None

    def body(*refs):
        s_refs, l_refs = refs[:n], refs[n:2 * n]
        send_sems, recv_sems = refs[2 * n], refs[2 * n + 1]
        x, y, c = _mesh_pos()
        me = 4 * x + 2 * y + c
        for k, (px, py, pc) in enumerate(_peers(x, y, c)):
            for a in range(n):
                cp = pltpu.make_async_remote_copy(
                    src_ref=s_refs[a].at[4 * px + 2 * py + pc], dst_ref=l_refs[a].at[me],
                    send_sem=send_sems.at[7 * a + k], recv_sem=recv_sems.at[7 * a + k],
                    device_id=(px, py, pc), device_id_type=MESH)
                if starting:
                    cp.start()
                else:
                    cp.wait_send()
                    cp.wait_recv()
        if starting:
            refs[-1][...] = jnp.zeros_like(refs[-1])

    hbm = pl.BlockSpec(memory_space=pltpu.HBM)
    sem = pl.BlockSpec(memory_space=pltpu.SEMAPHORE)
    thru = [pltpu.HBM(t.shape, t.dtype) for t in list(sends) + list(lands)]
    effect = pltpu.CompilerParams(has_side_effects=pltpu.SideEffectType.DATAFLOW_SIDE_EFFECTING)
    if starting:
        res = pl.pallas_call(
            body, name=name, in_specs=[hbm] * (2 * n),
            out_shape=[pltpu.SemaphoreType.DMA((7 * n,)), pltpu.SemaphoreType.DMA((7 * n,))] + thru + [jax.ShapeDtypeStruct((8, LANES), F32)],
            out_specs=[sem, sem] + [hbm] * (2 * n) + [pl.BlockSpec(memory_space=pltpu.VMEM)],
            input_output_aliases={a: 2 + a for a in range(2 * n)}, compiler_params=effect,
        )(*[pltpu.with_memory_space_constraint(t, pltpu.HBM) for t in list(sends) + list(lands)])
        return res[0], res[1], res[2:2 + n], res[2 + n:2 + 2 * n], res[-1]
    res = pl.pallas_call(
        body, name=name, out_shape=thru, in_specs=[hbm] * (2 * n) + [sem, sem, pl.BlockSpec(memory_space=pl.ANY)],
        out_specs=[hbm] * (2 * n), input_output_aliases={a: a for a in range(2 * n)}, compiler_params=effect,
    )(*sends, *lands, sems[0], sems[1], after)
    return res[n:]


def unwritten(name, like):
    def body(*refs):
        pass

    hbm = pl.BlockSpec(memory_space=pl.ANY)
    return pl.pallas_call(body, name=name, out_shape=[jax.ShapeDtypeStruct(t.shape, t.dtype) for t in like],
                          out_specs=[hbm] * len(like))()


def own_slot_only(send, land, me):
    mine = lax.dynamic_index_in_dim(send, me, 0, keepdims=False)
    return lax.dynamic_update_index_in_dim(land, mine, me, 0)


def all_to_all(name, sends):
    n = len(sends)

    def body(*refs):
        s_refs, r_refs = refs[:n], refs[n:2 * n]
        send_sems, recv_sems, local_sems = refs[2 * n:]
        x, y, c = _mesh_pos()
        me = 4 * x + 2 * y + c
        mine = [pltpu.make_async_copy(s_refs[a].at[me], r_refs[a].at[me], local_sems.at[a]) for a in range(n)]
        for cp in mine:
            cp.start()
        copies = []
        for k in range(1, N_DEV):
            px, py, pc = _flip(x, k & 4), _flip(y, k & 2), _flip(c, k & 1)
            for a in range(n):
                copies.append(pltpu.make_async_remote_copy(
                    src_ref=s_refs[a].at[4 * px + 2 * py + pc], dst_ref=r_refs[a].at[me],
                    send_sem=send_sems.at[a, k - 1], recv_sem=recv_sems.at[a, k - 1],
                    device_id=(px, py, pc), device_id_type=MESH))
        for cp in copies:
            cp.start()
        for cp in copies:
            cp.wait_recv()
        for cp in copies:
            cp.wait_send()
        for cp in mine:
            cp.wait()

    hbm = pl.BlockSpec(memory_space=pl.ANY)
    return pl.pallas_call(
        body, name=name, out_shape=[jax.ShapeDtypeStruct(s.shape, s.dtype) for s in sends],
        in_specs=[hbm] * n, out_specs=[hbm] * n,
        scratch_shapes=[pltpu.SemaphoreType.DMA((n, 7)), pltpu.SemaphoreType.DMA((n, 7)), pltpu.SemaphoreType.DMA((n,))],
    )(*sends)


def _row_tile(r, cap, step):
    return next((t for t in range(cap, step - 1, -step) if r % t == 0), r)


def _sum_parts(p, n):
    t = [p[k].astype(F32) for k in range(n)]
    while len(t) > 1:
        t = [t[k] + t[k + 1] for k in range(0, len(t), 2)]
    return t[0]


def _adam(g, w, m, v):
    m = ADAM_B1 * m + (1.0 - ADAM_B1) * g
    v = ADAM_B2 * v + (1.0 - ADAM_B2) * (g * g)
    m_hat = m / (1.0 - ADAM_B1 ** ADAM_STEP)
    v_hat = v / (1.0 - ADAM_B2 ** ADAM_STEP)
    return -ADAM_LR * (m_hat / (jnp.sqrt(v_hat) + ADAM_EPS) + ADAM_WD * w), m, v


def adam_tiled(name, partials, w, m_, v_, layer=0, prev=None):
    _, r, c = w.shape
    n_part = partials.shape[0]
    tr = _row_tile(r, 256, 16)

    def body(*refs):
        p_ref, w_ref, m_ref, v_ref = refs[:4]
        g_ref, d_ref, nm_ref, nv_ref = refs[-4:]
        g = _sum_parts(p_ref, n_part)
        g_ref[...] = g
        d_ref[...], nm_ref[...], nv_ref[...] = _adam(g, w_ref[...], m_ref[...], v_ref[...])

    spec = pl.BlockSpec((None, tr, c), lambda i: (layer, i, 0))
    in_specs = [pl.BlockSpec((n_part, None, tr, c), lambda i: (0, 0, i, 0)), spec, spec, spec]
    args = [partials, w, m_, v_]
    aliases = {}
    if prev is not None:
        in_specs += [pl.BlockSpec(memory_space=pl.ANY)] * 4
        args += list(prev)
        aliases = {4 + k: k for k in range(4)}
    return pl.pallas_call(
        body, grid=(r // tr,), name=name, in_specs=in_specs,
        out_specs=[spec] * 4, out_shape=[jax.ShapeDtypeStruct(w.shape, F32)] * 4,
        input_output_aliases=aliases, compiler_params=_params(("parallel",)),
    )(*args)


def adam_small(name, items, extra):
    n, ne = len(items), len(extra)

    def body(*refs):
        ins, outs = refs[:4 * n + ne], refs[4 * n + ne:]
        for a in range(n):
            p_ref, w_ref, m_ref, v_ref = ins[4 * a:4 * a + 4]
            g = _sum_parts(p_ref, N_DEV)
            outs[4 * a][...] = g
            outs[4 * a + 1][...], outs[4 * a + 2][...], outs[4 * a + 3][...] = _adam(g, w_ref[...], m_ref[...], v_ref[...])
        for e in range(ne):
            outs[4 * n + e][...] = _sum_parts(ins[4 * n + e], N_DEV)

    args, out_shape = [], []
    for p, w, m_, v_ in items:
        args += [p, w, m_, v_]
        out_shape += [jax.ShapeDtypeStruct(w.shape, F32)] * 4
    for e in extra:
        args.append(e)
        out_shape.append(jax.ShapeDtypeStruct(e.shape[1:], F32))
    vmem = pl.BlockSpec(memory_space=pltpu.VMEM)
    res = pl.pallas_call(body, name=name, in_specs=[vmem] * len(args), out_specs=[vmem] * len(out_shape), out_shape=out_shape)(*args)
    return [res[4 * a:4 * a + 4] for a in range(n)], res[4 * n:]


def _cols_from_gather(g):
    g = jnp.moveaxis(g, 0, -2)
    return g.reshape(g.shape[:-2] + (g.shape[-2] * g.shape[-1],))


def _cols_to_blocks(w):
    w = w.reshape(w.shape[:-1] + (N_DEV, w.shape[-1] // N_DEV))
    return jnp.moveaxis(w, -2, 0)


def _block_diag(w):
    pairs = w.reshape(B_BLOCKS // 2, 2, B_BLOCK_DIM, 1, B_BLOCK_DIM)
    same = jnp.eye(2, dtype=bool).reshape(1, 2, 1, 2, 1)
    return jnp.where(same, pairs, 0.0).reshape(B_BLOCKS // 2 * LANES, LANES)


def _block_diag_grad(d):
    parts = d.reshape(B_BLOCKS // 2, 2, B_BLOCK_DIM, 2, B_BLOCK_DIM)
    same = jnp.eye(2, dtype=bool).reshape(1, 2, 1, 2, 1)
    return jnp.sum(jnp.where(same, parts, 0.0), axis=3).reshape(B_BLOCKS, B_BLOCK_DIM, B_BLOCK_DIM)


NAMES = ("norm_gains", "even_w_in", "hgrn_lb_logits", "hgrn_norm", "rg_conv_w", "rg_conv_b", "rg_wa", "rg_ba", "rg_wx", "rg_bx",
         "rg_lambda", "even_w_out", "odd_w_in", "fox_f_bias", "odd_w_out", "ffn_w_up", "ffn_conv_w", "ffn_conv_b", "ffn_w_down")
SMALL_SHARDED = ("norm_gains", "rg_conv_w", "ffn_conv_w")
REPLICATED = ("hgrn_lb_logits", "hgrn_norm", "rg_conv_b", "rg_wa", "rg_ba", "rg_wx", "rg_bx", "rg_lambda", "fox_f_bias", "ffn_conv_b")


def _ffn_forward(tag, layer, h, w_up_g, cw5, cb5, w_down_g, m, seq):
    tm = _div_tile(m, MM_ROWS)
    nm = m // tm
    hid = mm(f"{tag}_up", "nn",
             Blk(h, (tm, D_MODEL), lambda i, j, k: (i, 0)),
             Blk(w_up_g, (None, None, D_MODEL, FF_BLK), lambda i, j, k: (j, 0, 0, 0)),
             Blk((N_DEV, m, FF_BLK), (None, tm, FF_BLK), lambda i, j, k: (j, i, 0)), F32, (nm, N_DEV, 1))
    hid = hid.reshape(2, N_DEV // 2, m, FF_BLK)
    act, conv = ffn_mid_fwd(f"{tag}_mid", hid, cw5, cb5, layer, m=m, seq=seq)
    f = mm(f"{tag}_down", "nn",
           Blk(act, (None, tm, FF_BLK), lambda i, j, k: (k, i, 0)),
           Blk(w_down_g, (2, None, FF_BLK // 2, D_MODEL), lambda i, j, k: (k, 0, 0, 0)),
           Blk((m, D_MODEL), (tm, D_MODEL), lambda i, j, k: (i, 0)), F32, (nm, 1, N_DEV // 2))
    return (hid, conv), act, f


def _ffn_backward(tag, layer, df, h, hid, act, w_up_g, cw5, cb5, w_down_g, m, seq):
    tm = _div_tile(m, MM_ROWS)
    nm = m // tm
    dact = mm(f"{tag}_dact", "nt",
              Blk(df, (tm, D_MODEL), lambda i, j, k: (i, 0)),
              Blk(w_down_g, (2, None, FF_BLK // 2, D_MODEL), lambda i, j, k: (j, 0, 0, 0)),
              Blk((N_DEV // 2, m, FF_BLK), (None, tm, FF_BLK), lambda i, j, k: (j, i, 0)), BF16, (nm, N_DEV // 2, 1))
    d_wdown = mm(f"{tag}_dwdown", "tn",
                 Blk(act, (None, tm, FF_BLK), lambda i, j, k: (i, k, 0)),
                 Blk(df, (tm, D_MODEL), lambda i, j, k: (k, 0)),
                 Blk(w_down_g.shape, (2, None, FF_BLK // 2, D_MODEL), lambda i, j, k: (i, 0, 0, 0)), BF16,
                 (N_DEV // 2, 1, nm))
    dhid, d_cw, d_cb = ffn_mid_bwd(f"{tag}_dmid", hid[0], hid[1], cw5, dact, layer, m=m, seq=seq)
    dhid = dhid.reshape(N_DEV, m, FF_BLK)
    dh = mm(f"{tag}_dh", "nt",
            Blk(dhid, (None, tm, FF_BLK), lambda i, j, k: (k, i, 0)),
            Blk(w_up_g, (None, None, D_MODEL, FF_BLK), lambda i, j, k: (k, 0, 0, 0)),
            Blk((m, D_MODEL), (tm, D_MODEL), lambda i, j, k: (i, 0)), BF16, (nm, 1, N_DEV))
    d_wup = mm(f"{tag}_dwup", "tn",
               Blk(dhid, (None, tm, FF_BLK), lambda i, j, k: (i, k, 0)),
               Blk(h, (tm, D_MODEL), lambda i, j, k: (k, 0)),
               Blk((N_DEV, 1, FF_BLK, D_MODEL), (None, None, FF_BLK, D_MODEL), lambda i, j, k: (i, 0, 0, 0)), BF16,
               (N_DEV, 1, nm))
    return dh, d_wup, d_cw, d_cb, d_wdown


def kernel(x, norm_gains, even_w_in, hgrn_lb_logits, hgrn_norm, rg_conv_w, rg_conv_b, rg_wa, rg_ba, rg_wx, rg_bx, rg_lambda, even_w_out, odd_w_in, fox_f_bias, odd_w_out, ffn_w_up, ffn_conv_w, ffn_conv_b, ffn_w_down, loss_target, m_norm_gains, m_even_w_in, m_hgrn_lb_logits, m_hgrn_norm, m_rg_conv_w, m_rg_conv_b, m_rg_wa, m_rg_ba, m_rg_wx, m_rg_bx, m_rg_lambda, m_even_w_out, m_odd_w_in, m_fox_f_bias, m_odd_w_out, m_ffn_w_up, m_ffn_conv_w, m_ffn_conv_b, m_ffn_w_down, v_norm_gains, v_even_w_in, v_hgrn_lb_logits, v_hgrn_norm, v_rg_conv_w, v_rg_conv_b, v_rg_wa, v_rg_ba, v_rg_wx, v_rg_bx, v_rg_lambda, v_even_w_out, v_odd_w_in, v_fox_f_bias, v_odd_w_out, v_ffn_w_up, v_ffn_conv_w, v_ffn_conv_b, v_ffn_w_down):
    local = dict(locals())
    w = {n: local[n] for n in NAMES}
    mom = {n: local["m_" + n] for n in NAMES}
    var = {n: local["v_" + n] for n in NAMES}
    n_batch, seq, _ = x.shape
    m = n_batch * seq
    tm = _div_tile(m, 512)
    tmm = _div_tile(m, MM_ROWS)
    nm = m // tmm

    gathered = all_gather("gather_weights", [w["even_w_in"].astype(BF16)] + [w[n] for n in SMALL_SHARDED])
    g = dict(zip(("even_w_in",) + SMALL_SHARDED, gathered))
    w_in_e = g["even_w_in"]
    gains = _cols_from_gather(g["norm_gains"])
    me = 4 * lax.axis_index("x") + 2 * lax.axis_index("y") + lax.axis_index("c")
    def own_block_only(name, blocks):
        lands = unwritten(name, [jax.ShapeDtypeStruct((N_DEV,) + t.shape, t.dtype) for t in blocks])
        return [lax.dynamic_update_index_in_dim(ld, t, me, 0) for ld, t in zip(lands, blocks)]

    def own_slots_only(name, sends):
        return [own_slot_only(t, ld, me) for t, ld in zip(sends, unwritten(name, sends))]

    behind = (g["norm_gains"][0, 0, 0, 0] * 0.0).astype(BF16)
    out0 = [w["even_w_out"].astype(BF16) + behind]
    out0_sent = gather_start("gather_out0_start", out0, own_block_only("land_out0", out0))
    behind = (out0_sent[4][0, 0] * 0.0).astype(BF16)
    ffn0 = [w["ffn_w_up"][0:1].astype(BF16) + behind, w["ffn_w_down"][0:1].astype(BF16) + behind]
    ffn0_sent = gather_start("gather_ffn0_start", ffn0, own_block_only("land_ffn0", ffn0))
    behind = (ffn0_sent[4][0, 0] * 0.0).astype(BF16)
    mix1w = [jnp.swapaxes(w["odd_w_in"], 1, 2).astype(BF16) + behind, w["odd_w_out"].astype(BF16) + behind]
    mix1_sent = gather_start("gather_mix1_start", mix1w, own_block_only("land_mix1", mix1w))
    behind = (mix1_sent[4][0, 0] * 0.0).astype(BF16)
    ffn1 = [w["ffn_w_up"][1:2].astype(BF16) + behind, w["ffn_w_down"][1:2].astype(BF16) + behind]
    ffn1_sent = gather_start("gather_ffn1_start", ffn1, own_block_only("land_ffn1", ffn1))
    started = ffn1_sent[4]
    rg_cw = _cols_from_gather(g["rg_conv_w"])[0]
    n_layer = ffn_conv_w.shape[0]
    cw5 = g["ffn_conv_w"].reshape(2, N_DEV // 2, n_layer, FFN_CONV, FF_BLK)
    cb5 = ffn_conv_b.reshape(n_layer, 2, N_DEV // 2, 1, FF_BLK)
    gain = lambda l, k: gains[l, k:k + 1, :]
    wa_bd, wx_bd = _block_diag(rg_wa[0]), _block_diag(rg_wx[0])
    fbias = jnp.pad(fox_f_bias, ((0, 0), (0, LANES - C_HEADS)))

    x0 = x.reshape(m, D_MODEL)
    tgt = loss_target.reshape(m, D_MODEL)

    (h0,) = tile_fwd("l0_prenorm", fn_prenorm_after, m=m, tm=tm, nj=1, rows=[Row(x0)], pars=[Par(gain(0, 0)), Par(started)],
                     outs=[Out(D_MODEL, BF16)])
    z0 = mm("l0_in", "nn",
            Blk(h0, (tmm, D_MODEL), lambda i, j, k: (i, 0)),
            Blk(w_in_e, (2, None, D_MODEL, 384), lambda i, j, k: (j, 0, 0, 0)),
            Blk((m, 3072), (tmm, 768), lambda i, j, k: (i, j)), F32, (nm, N_DEV // 2, 1), b_join=True)
    oa, sprev = hgrn_fwd("l0_hgrn", z0, hgrn_lb_logits, hgrn_norm, n_batch=n_batch, seq=seq)
    rg_rows = lambda: [Row(z0, LANES, 16), Row(z0, LANES, 20)]
    rg_pars = lambda: [Par(rg_cw, "col", LANES), Par(rg_conv_b, "col", LANES), Par(wa_bd, "row", LANES), Par(rg_ba, "col", LANES),
                       Par(wx_bd, "row", LANES), Par(rg_bx, "col", LANES), Par(rg_lambda, "col", LANES)]
    (mixcat0,) = tile_fwd("l0_rglru", fn_rglru, m=m, tm=seq, nj=B_WIDTH // LANES, rows=rg_rows(), pars=rg_pars(),
                          outs=[Out(D_MODEL, BF16, LANES, A_WIDTH // LANES)], into=oa)
    (g_out_e,) = gather_wait("gather_out0_wait", out0_sent[0], out0_sent[1], out0_sent[2], out0_sent[3], mixcat0)
    w_out_e = g_out_e.reshape(D_MODEL, D_MODEL)
    mix0 = mm2d("l0_out", "nn", mixcat0, w_out_e)
    x1, h1 = tile_fwd("l0_postnorm", fn_addnorm2, m=m, tm=tm, nj=1, rows=[Row(x0), Row(mix0)], pars=[Par(gain(0, 1)), Par(gain(0, 2))],
                      outs=[Out(D_MODEL, F32), Out(D_MODEL, BF16)])
    w_up_g0, w_down_g0 = gather_wait("gather_ffn0_wait", ffn0_sent[0], ffn0_sent[1], ffn0_sent[2], ffn0_sent[3], h1)
    hid0, act0, f0 = _ffn_forward("l0_ffn", 0, h1, w_up_g0, cw5, cb5, w_down_g0, m, seq)
    x2, h2 = tile_fwd("l0_ffnnorm", fn_addnorm2, m=m, tm=tm, nj=1, rows=[Row(x1), Row(f0)], pars=[Par(gain(0, 3)), Par(gain(1, 0))],
                      outs=[Out(D_MODEL, F32), Out(D_MODEL, BF16)])

    g_in_o, g_out_o = gather_wait("gather_mix1_wait", mix1_sent[0], mix1_sent[1], mix1_sent[2], mix1_sent[3], h2)
    w_in_o_t = jnp.pad(g_in_o.reshape(3088, D_MODEL), ((0, 3200 - 3088), (0, 0)))
    w_out_o = g_out_o.reshape(D_MODEL, D_MODEL)
    z1 = mm2d("l1_in", "nt", h2, w_in_o_t)
    (cgate,) = tile_fwd("l1_gate", fn_fox_gate, m=m, tm=seq, nj=1, rows=[Row(z1, LANES, 3072 // LANES)], pars=[Par(fbias)],
                        outs=[Out(LANES, F32)])
    place, ones_q, ones_k = term_placement()
    qterm, kterm = tile_fwd("l1_terms", fn_fox_terms, m=m, tm=tm, nj=1, rows=[Row(cgate)],
                            pars=[Par(place), Par(ones_q), Par(ones_k)], outs=[Out(TERM_W, BF16), Out(TERM_W, BF16)])
    oc, lse = fox_pair_fwd("l1_attn", z1, qterm, kterm, n_batch=n_batch, seq=seq)
    blk_b = min(ATT_BLK, seq)
    lse = lse.reshape(n_batch, N_PAIR, -1, 2, lse.shape[-1] // blk_b, blk_b).swapaxes(3, 4).reshape(n_batch, N_PAIR, seq // blk_b, 2, 1, blk_b)
    mix1 = mm2d("l1_out", "nn", oc, w_out_o)
    x3, h3 = tile_fwd("l1_postnorm", fn_addnorm2, m=m, tm=tm, nj=1, rows=[Row(x2), Row(mix1)], pars=[Par(gain(1, 1)), Par(gain(1, 2))],
                      outs=[Out(D_MODEL, F32), Out(D_MODEL, BF16)])
    w_up_g1, w_down_g1 = gather_wait("gather_ffn1_wait", ffn1_sent[0], ffn1_sent[1], ffn1_sent[2], ffn1_sent[3], h3)
    hid1, act1, f1 = _ffn_forward("l1_ffn", 1, h3, w_up_g1, cw5, cb5, w_down_g1, m, seq)
    dy, df1, loss_part, d_g13 = loss_head("loss", x3, f1, tgt, gain(1, 3), m=m, tm=tm)
    dh3, d_wup1, d_cw1, d_cb1, d_wdown1 = _ffn_backward("l1_ffn", 1, df1, h3, hid1, act1, w_up_g1, cw5, cb5, w_down_g1, m, seq)
    dx2, dmix1, d_g11, d_g12 = tile_bwd("l1_dpostnorm", fn_addnorm2, m=m, tm=tm, nj=1, rows=[Row(x2), Row(mix1)],
                                        pars=[Par(gain(1, 1)), Par(gain(1, 2))], cts=[Row(dy), Row(dh3)],
                                        drows=[Out(D_MODEL, F32), Out(D_MODEL, BF16)])
    doc = mm2d("l1_doc", "nt", dmix1, w_out_o, BF16)
    d_wout_o = mm2d("l1_dwout", "tn", oc, dmix1)
    dq, dk, dv, dc = fox_pair_bwd("l1_dattn", z1, qterm, kterm, oc, doc, lse, n_batch=n_batch, seq=seq)
    dzf, d_fbias = tile_bwd("l1_dgate", fn_fox_gate, m=m, tm=seq, nj=1, rows=[Row(z1, LANES, 3072 // LANES)], pars=[Par(fbias)],
                            cts=[Row(dc)], drows=[Out(LANES, BF16)])
    dz1 = jnp.concatenate([dq, dk, dv, dzf], axis=-1)
    dh2 = mm2d("l1_dh", "nn", dz1, w_in_o_t, BF16)
    d_win_o_t = mm2d("l1_dwin", "tn", dz1, h2, BF16)

    send1 = [d_win_o_t[:3088].reshape(N_DEV, 1, 3088 // N_DEV, D_MODEL),
             d_wout_o.reshape(N_DEV, 1, D_MODEL // N_DEV, D_MODEL).astype(BF16), d_wup1, d_wdown1]
    sent1 = _split_exchange("exchange_l1_start", send1, own_slots_only("land_l1", send1), None, None)

    dx1, df0, d_g03, d_g10 = tile_bwd("l0_dffnnorm", fn_addnorm2_after, m=m, tm=tm, nj=1, rows=[Row(x1), Row(f0)],
                                      pars=[Par(gain(0, 3)), Par(gain(1, 0)), Par(sent1[4])], cts=[Row(dx2), Row(dh2)],
                                      drows=[Out(D_MODEL, F32), Out(D_MODEL, BF16)])[:4]
    dh1, d_wup0, d_cw0, d_cb0, d_wdown0 = _ffn_backward("l0_ffn", 0, df0, h1, hid0, act0, w_up_g0, cw5, cb5, w_down_g0, m, seq)
    send0 = [d_wup0, d_wdown0]
    sent0 = _split_exchange("exchange_ffn0_start", send0, own_slots_only("land_dffn0", send0), None, None)
    dx0a, dmix0, d_g01, d_g02 = tile_bwd("l0_dpostnorm", fn_addnorm2_after, m=m, tm=tm, nj=1, rows=[Row(x0), Row(mix0)],
                                         pars=[Par(gain(0, 1)), Par(gain(0, 2)), Par(sent0[4])], cts=[Row(dx1), Row(dh1)],
                                         drows=[Out(D_MODEL, F32), Out(D_MODEL, BF16)])[:4]
    dmixcat0 = mm2d("l0_dmixcat", "nt", dmix0, w_out_e, BF16)
    d_wout_e = mm2d("l0_dwout", "tn", mixcat0, dmix0)
    dzq, dzf0, dzv, dzg, d_lb, d_hnorm = hgrn_bwd("l0_dhgrn", z0, sprev, hgrn_lb_logits, hgrn_norm, dmixcat0, n_batch=n_batch, seq=seq)
    dzx, dzy, d_rcw, d_rcb, d_wa, d_ba, d_wx, d_bx, d_lam = tile_bwd(
        "l0_drglru", fn_rglru, m=m, tm=seq, nj=B_WIDTH // LANES, rows=rg_rows(), pars=rg_pars(),
        cts=[Row(dmixcat0, LANES, A_WIDTH // LANES)], drows=[Out(B_WIDTH, BF16, LANES), Out(B_WIDTH, BF16, LANES)])
    dz0 = jnp.concatenate([dzq, dzf0, dzv, dzg, dzx, dzy], axis=-1)
    d_win_e = mm("l0_dwin", "tn",
                 Blk(h0, (tmm, D_MODEL), lambda i, j, k: (k, 0)),
                 Blk(dz0, (tmm, 768), lambda i, j, k: (k, j)),
                 Blk(w_in_e.shape, (2, None, D_MODEL, 384), lambda i, j, k: (j, 0, 0, 0)), BF16, (1, N_DEV // 2, nm), o_split=True)
    send_e = [d_win_e, d_wout_e.reshape(N_DEV, 1, D_MODEL // N_DEV, D_MODEL).astype(BF16)]
    sent_e = _split_exchange("exchange_even_start", send_e, own_slots_only("land_even", send_e), None, None)
    d_ffn_cb = jnp.stack([d_cb0, d_cb1]).reshape(n_layer, 2 * D_FF)
    rep = {"hgrn_lb_logits": d_lb, "hgrn_norm": d_hnorm, "rg_conv_b": d_rcb, "rg_wa": _block_diag_grad(d_wa)[None], "rg_ba": d_ba,
           "rg_wx": _block_diag_grad(d_wx)[None], "rg_bx": d_bx, "rg_lambda": d_lam, "fox_f_bias": d_fbias[:, :C_HEADS],
           "ffn_conv_b": d_ffn_cb}
    rep_blocks = [rep[n] for n in REPLICATED] + [loss_part]
    rep_sent = gather_start("gather_partials_start", rep_blocks, own_block_only("land_partials", rep_blocks))
    dh0 = mm("l0_dh", "nt",
             Blk(dz0, (tmm, 768), lambda i, j, k: (i, k)),
             Blk(w_in_e, (2, None, D_MODEL, 384), lambda i, j, k: (k, 0, 0, 0)),
             Blk((m, D_MODEL), (tmm, D_MODEL), lambda i, j, k: (i, 0)), BF16, (nm, 1, N_DEV // 2), after=sent_e[4] + rep_sent[4],
             b_join=True)
    dx0, d_g00 = tile_bwd("l0_dprenorm", fn_input_norm, m=m, tm=tm, nj=1, rows=[Row(x0)], pars=[Par(gain(0, 0))],
                          cts=[Row(dx0a), Row(dh0)], drows=[Out(D_MODEL, F32)])

    d_gains = jnp.stack([jnp.concatenate([d_g00, d_g01, d_g02, d_g03], axis=0), jnp.concatenate([d_g10, d_g11, d_g12, d_g13], axis=0)])
    d_ffn_cw = jnp.stack([d_cw0, d_cw1], axis=2).reshape(N_DEV, n_layer, FFN_CONV, FF_BLK)
    r_in_o, r_out_o, r_up1, r_down1 = _split_exchange("exchange_l1_wait", sent1[2], sent1[3], sent1[:2], dx0)
    r_up0, r_down0 = _split_exchange("exchange_ffn0_wait", sent0[2], sent0[3], sent0[:2], dx0)
    r_in_e, r_out_e = _split_exchange("exchange_even_wait", sent_e[2], sent_e[3], sent_e[:2], dx0)
    recv, res = {}, {}
    flipped = ("odd_w_in", "ffn_w_up")
    view = lambda n, t: jnp.swapaxes(t, 1, 2) if n in flipped else t
    for n, r in (("even_w_in", r_in_e), ("even_w_out", r_out_e), ("odd_w_in", r_in_o), ("odd_w_out", r_out_o)):
        res[n] = [view(n, t) for t in adam_tiled("adam_" + n, r, view(n, w[n]), view(n, mom[n]), view(n, var[n]))]
    for n, parts_l in (("ffn_w_up", (r_up0, r_up1)), ("ffn_w_down", (r_down0, r_down1))):
        wmv = (view(n, w[n]), view(n, mom[n]), view(n, var[n]))
        first_layer = adam_tiled(f"adam_{n}_0", parts_l[0], *wmv, layer=0)
        res[n] = [view(n, t) for t in adam_tiled(f"adam_{n}_1", parts_l[1], *wmv, layer=1, prev=first_layer)]
    small_send = [_cols_to_blocks(d_gains), _cols_to_blocks(d_rcw[None]), d_ffn_cw]
    recv.update(zip(SMALL_SHARDED, all_to_all("exchange_small", small_send)))

    parts = gather_wait("gather_partials_wait", rep_sent[0], rep_sent[1], rep_sent[2], rep_sent[3], dx0)
    for n, p in zip(REPLICATED, parts):
        recv[n] = p
    small = SMALL_SHARDED + REPLICATED
    small_res, (loss_sum,) = adam_small("adam_small", [(recv[n], w[n], mom[n], var[n]) for n in small], [parts[-1]])
    res.update(dict(zip(small, small_res)))

    out = [loss_sum[0, 0], dx0.reshape(x.shape)]
    for k in range(4):
        out += [res[n][k] for n in NAMES]
    return tuple(out)
```
